```python
import math
import jax, jax.numpy as jnp
from jax import lax
import numpy as np

D_MODEL = 1024
BATCH = 8
SEQ = 4096
DEPTH = 4

N_MIXERS = 3
N_META = 16
EPS = 1e-6
S5_WIDTH = D_MODEL
S5_GROUP = 16
S5_GROUPS = S5_WIDTH // S5_GROUP
S5_STATE = 64
DT_MIN = 1e-3
DT_MAX = 1e-1
CONV_E = 2 * D_MODEL
CONV_K = 3
POOL_E = 2 * D_MODEL
POOL_WINDOWS = (2, 4, 8, 16)
POOL_GROUP = POOL_E // len(POOL_WINDOWS)

kernel_name = "hybrid_s5_shortconv_pool_interleaved"


def rmsnorm(h, g):
    hf = h.astype(jnp.float32)
    y = hf * lax.rsqrt(jnp.mean(hf * hf, axis=-1, keepdims=True) + EPS)
    return (y * g.astype(jnp.float32)).astype(h.dtype)


def s5_branch(n, w_in, lam_re, lam_im, log_dt, b_re, b_im, c_re, c_im, d_skip, w_glu, b_glu, w_out):
    f32 = jnp.float32
    bsz, L, _ = n.shape
    u, z = jnp.split(n @ w_in, 2, axis=-1)
    uf = u.astype(f32).reshape(bsz, L, S5_GROUPS, S5_GROUP)
    lr = lam_re.astype(f32)
    li = lam_im.astype(f32)
    dt = jnp.exp(log_dt.astype(f32))[:, None]
    mag = jnp.exp(lr * dt)
    ar = mag * jnp.cos(li * dt)
    ai = mag * jnp.sin(li * dt)
    den = lr * lr + li * li
    kr = ((ar - 1.0) * lr + ai * li) / den
    ki = (ai * lr - (ar - 1.0) * li) / den
    br = b_re.astype(f32)
    bi = b_im.astype(f32)
    bbr = kr[..., None] * br - ki[..., None] * bi
    bbi = kr[..., None] * bi + ki[..., None] * br
    xr = jnp.einsum("blgi,gpi->blgp", uf, bbr)
    xi = jnp.einsum("blgi,gpi->blgp", uf, bbi)
    a_r = jnp.broadcast_to(ar[None, None], (1, L, S5_GROUPS, S5_STATE))
    a_i = jnp.broadcast_to(ai[None, None], (1, L, S5_GROUPS, S5_STATE))

    def combine(e1, e2):
        a1r, a1i, b1r, b1i = e1
        a2r, a2i, b2r, b2i = e2
        return (a2r * a1r - a2i * a1i,
                a2r * a1i + a2i * a1r,
                a2r * b1r - a2i * b1i + b2r,
                a2r * b1i + a2i * b1r + b2i)

    _, _, sr, si = lax.associative_scan(combine, (a_r, a_i, xr, xi), axis=1)
    y = (jnp.einsum("blgp,gip->blgi", sr, c_re.astype(f32))
         - jnp.einsum("blgp,gip->blgi", si, c_im.astype(f32))
         + d_skip.astype(f32).reshape(S5_GROUPS, S5_GROUP) * uf)
    y = jax.nn.gelu(y.reshape(bsz, L, S5_WIDTH))
    y = y * jax.nn.sigmoid(y @ w_glu.astype(f32) + b_glu.astype(f32))
    y = y.astype(n.dtype) * jax.nn.silu(z)
    return y @ w_out


def shortconv_branch(n, w_in, conv_w, conv_b, w_out):
    bg, cg, v, z = jnp.split(n @ w_in, 4, axis=-1)
    hc = cg * v
    conv = lax.conv_general_dilated(
        hc, conv_w[:, None, :], window_strides=(1,), padding=[(CONV_K - 1, 0)],
        dimension_numbers=("NWC", "WIO", "NWC"), feature_group_count=CONV_E) + conv_b
    y = bg * conv
    return (y * jax.nn.silu(z)) @ w_out


def pool_branch(n, w_in, w_grp, b_grp, scale, w_out):
    f32 = jnp.float32
    bsz, L, _ = n.shape
    u, z = jnp.split(n @ w_in, 2, axis=-1)
    ug = u.astype(f32).reshape(bsz, L, len(POOL_WINDOWS), POOL_GROUP)
    cs = jnp.cumsum(ug, axis=1)
    t = jnp.arange(1, L + 1, dtype=f32)[:, None]
    outs = []
    for k, w in enumerate(POOL_WINDOWS):
        c = cs[:, :, k]
        lag = jnp.concatenate([jnp.zeros_like(c[:, :w]), c[:, :L - w]], axis=1)
        mixed = (c - lag) / jnp.minimum(t, float(w)) - ug[:, :, k]
        outs.append(mixed @ w_grp[k].astype(f32) + b_grp[k].astype(f32))
    y = jnp.concatenate(outs, axis=-1) * scale.astype(f32)
    y = y.astype(n.dtype) * jax.nn.silu(z)
    return y @ w_out


def _normal(key, shape, std):
    return jax.random.normal(key, shape, jnp.float32) * std


def _s5_params(key, p):
    ks = jax.random.split(key, 12)
    n_idx = jnp.arange(S5_STATE, dtype=jnp.float32)
    return {
        p + "w_in": _normal(ks[0], (D_MODEL, 2 * S5_WIDTH), D_MODEL ** -0.5),
        p + "lam_re": -0.5 + _normal(ks[1], (S5_GROUPS, S5_STATE), 0.01),
        p + "lam_im": math.pi * n_idx[None, :] + _normal(ks[2], (S5_GROUPS, S5_STATE), 0.01),
        p + "log_dt": jax.random.uniform(ks[3], (S5_GROUPS,), jnp.float32,
                                         math.log(DT_MIN), math.log(DT_MAX)),
        p + "b_re": _normal(ks[4], (S5_GROUPS, S5_STATE, S5_GROUP), (2 * S5_GROUP) ** -0.5),
        p + "b_im": _normal(ks[5], (S5_GROUPS, S5_STATE, S5_GROUP), (2 * S5_GROUP) ** -0.5),
        p + "c_re": _normal(ks[6], (S5_GROUPS, S5_GROUP, S5_STATE), (2 * S5_STATE) ** -0.5),
        p + "c_im": _normal(ks[7], (S5_GROUPS, S5_GROUP, S5_STATE), (2 * S5_STATE) ** -0.5),
        p + "d_skip": _normal(ks[8], (S5_WIDTH,), 1.0),
        p + "w_glu": _normal(ks[9], (S5_WIDTH, S5_WIDTH), S5_WIDTH ** -0.5),
        p + "b_glu": _normal(ks[10], (S5_WIDTH,), 0.01),
        p + "w_out": _normal(ks[11], (S5_WIDTH, D_MODEL), S5_WIDTH ** -0.5),
    }


def _conv_params(key, p):
    ks = jax.random.split(key, 4)
    return {
        p + "w_in": _normal(ks[0], (D_MODEL, 4 * CONV_E), D_MODEL ** -0.5),
        p + "conv_w": _normal(ks[1], (CONV_K, CONV_E), CONV_K ** -0.5),
        p + "conv_b": _normal(ks[2], (CONV_E,), 0.01),
        p + "w_out": _normal(ks[3], (CONV_E, D_MODEL), CONV_E ** -0.5),
    }


def _pool_params(key, p):
    ks = jax.random.split(key, 5)
    ng = len(POOL_WINDOWS)
    return {
        p + "w_in": _normal(ks[0], (D_MODEL, 2 * POOL_E), D_MODEL ** -0.5),
        p + "w_grp": _normal(ks[1], (ng, POOL_GROUP, POOL_GROUP), POOL_GROUP ** -0.5),
        p + "b_grp": _normal(ks[2], (ng, POOL_GROUP), 0.01),
        p + "scale": 1.0 + _normal(ks[3], (POOL_E,), 0.02),
        p + "w_out": _normal(ks[4], (POOL_E, D_MODEL), POOL_E ** -0.5),
    }


def _fwd_setup_inputs(seed: int = 0) -> dict:
    key = jax.random.key(seed)
    ks = jax.random.split(key, 3 + 2 * DEPTH)
    out = {
        "x": _normal(ks[0], (BATCH, SEQ, D_MODEL), 1.0),
        "meta_tokens": _normal(ks[1], (N_META, D_MODEL), 1.0),
    }
    builders = (_s5_params, _conv_params, _pool_params)
    for i in range(DEPTH):
        out["norm%d_g" % i] = 1.0 + _normal(ks[3 + 2 * i], (D_MODEL,), 0.02)
        out.update(builders[i % N_MIXERS](ks[4 + 2 * i], "l%d_" % i))
    out["final_g"] = 1.0 + _normal(ks[2], (D_MODEL,), 0.02)
    return out


def _fwd_reference(x, meta_tokens,
              norm0_g, l0_w_in, l0_lam_re, l0_lam_im, l0_log_dt, l0_b_re, l0_b_im, l0_c_re, l0_c_im,
              l0_d_skip, l0_w_glu, l0_b_glu, l0_w_out,
              norm1_g, l1_w_in, l1_conv_w, l1_conv_b, l1_w_out,
              norm2_g, l2_w_in, l2_w_grp, l2_b_grp, l2_scale, l2_w_out,
              norm3_g, l3_w_in, l3_lam_re, l3_lam_im, l3_log_dt, l3_b_re, l3_b_im, l3_c_re, l3_c_im,
              l3_d_skip, l3_w_glu, l3_b_glu, l3_w_out,
              final_g):
    bsz = x.shape[0]
    meta = jnp.broadcast_to(meta_tokens[None].astype(x.dtype), (bsz, N_META, D_MODEL))
    h = jnp.concatenate([meta, x], axis=1)
    layers = [
        (norm0_g, (l0_w_in, l0_lam_re, l0_lam_im, l0_log_dt, l0_b_re, l0_b_im, l0_c_re, l0_c_im,
                   l0_d_skip, l0_w_glu, l0_b_glu, l0_w_out)),
        (norm1_g, (l1_w_in, l1_conv_w, l1_conv_b, l1_w_out)),
        (norm2_g, (l2_w_in, l2_w_grp, l2_b_grp, l2_scale, l2_w_out)),
        (norm3_g, (l3_w_in, l3_lam_re, l3_lam_im, l3_log_dt, l3_b_re, l3_b_im, l3_c_re, l3_c_im,
                   l3_d_skip, l3_w_glu, l3_b_glu, l3_w_out)),
    ]
    mixers = (s5_branch, shortconv_branch, pool_branch)
    for i in range(DEPTH):
        g, params = layers[i]
        h = h + mixers[i % N_MIXERS](rmsnorm(h, g), *params)
    return rmsnorm(h[:, N_META:], final_g)


import jax as _jax
import jax.numpy as _jnp

TWIN_FORMAT = 'train_step'
FWD_PARAMS = ['x', 'meta_tokens', 'norm0_g', 'l0_w_in', 'l0_lam_re', 'l0_lam_im', 'l0_log_dt', 'l0_b_re', 'l0_b_im', 'l0_c_re', 'l0_c_im', 'l0_d_skip', 'l0_w_glu', 'l0_b_glu', 'l0_w_out', 'norm1_g', 'l1_w_in', 'l1_conv_w', 'l1_conv_b', 'l1_w_out', 'norm2_g', 'l2_w_in', 'l2_w_grp', 'l2_b_grp', 'l2_scale', 'l2_w_out', 'norm3_g', 'l3_w_in', 'l3_lam_re', 'l3_lam_im', 'l3_log_dt', 'l3_b_re', 'l3_b_im', 'l3_c_re', 'l3_c_im', 'l3_d_skip', 'l3_w_glu', 'l3_b_glu', 'l3_w_out', 'final_g']
TWIN_WEIGHTS = ['meta_tokens', 'norm0_g', 'l0_w_in', 'l0_lam_re', 'l0_lam_im', 'l0_log_dt', 'l0_b_re', 'l0_b_im', 'l0_c_re', 'l0_c_im', 'l0_d_skip', 'l0_w_glu', 'l0_b_glu', 'l0_w_out', 'norm1_g', 'l1_w_in', 'l1_conv_w', 'l1_conv_b', 'l1_w_out', 'norm2_g', 'l2_w_in', 'l2_w_grp', 'l2_b_grp', 'l2_scale', 'l2_w_out', 'norm3_g', 'l3_w_in', 'l3_lam_re', 'l3_lam_im', 'l3_log_dt', 'l3_b_re', 'l3_b_im', 'l3_c_re', 'l3_c_im', 'l3_d_skip', 'l3_w_glu', 'l3_b_glu', 'l3_w_out', 'final_g']
TWIN_DIFF_INPUT = 'x'
TWIN_INPUTS = ['x', 'meta_tokens', 'norm0_g', 'l0_w_in', 'l0_lam_re', 'l0_lam_im', 'l0_log_dt', 'l0_b_re', 'l0_b_im', 'l0_c_re', 'l0_c_im', 'l0_d_skip', 'l0_w_glu', 'l0_b_glu', 'l0_w_out', 'norm1_g', 'l1_w_in', 'l1_conv_w', 'l1_conv_b', 'l1_w_out', 'norm2_g', 'l2_w_in', 'l2_w_grp', 'l2_b_grp', 'l2_scale', 'l2_w_out', 'norm3_g', 'l3_w_in', 'l3_lam_re', 'l3_lam_im', 'l3_log_dt', 'l3_b_re', 'l3_b_im', 'l3_c_re', 'l3_c_im', 'l3_d_skip', 'l3_w_glu', 'l3_b_glu', 'l3_w_out', 'final_g', 'loss_target', 'm_meta_tokens', 'm_norm0_g', 'm_l0_w_in', 'm_l0_lam_re', 'm_l0_lam_im', 'm_l0_log_dt', 'm_l0_b_re', 'm_l0_b_im', 'm_l0_c_re', 'm_l0_c_im', 'm_l0_d_skip', 'm_l0_w_glu', 'm_l0_b_glu', 'm_l0_w_out', 'm_norm1_g', 'm_l1_w_in', 'm_l1_conv_w', 'm_l1_conv_b', 'm_l1_w_out', 'm_norm2_g', 'm_l2_w_in', 'm_l2_w_grp', 'm_l2_b_grp', 'm_l2_scale', 'm_l2_w_out', 'm_norm3_g', 'm_l3_w_in', 'm_l3_lam_re', 'm_l3_lam_im', 'm_l3_log_dt', 'm_l3_b_re', 'm_l3_b_im', 'm_l3_c_re', 'm_l3_c_im', 'm_l3_d_skip', 'm_l3_w_glu', 'm_l3_b_glu', 'm_l3_w_out', 'm_final_g', 'v_meta_tokens', 'v_norm0_g', 'v_l0_w_in', 'v_l0_lam_re', 'v_l0_lam_im', 'v_l0_log_dt', 'v_l0_b_re', 'v_l0_b_im', 'v_l0_c_re', 'v_l0_c_im', 'v_l0_d_skip', 'v_l0_w_glu', 'v_l0_b_glu', 'v_l0_w_out', 'v_norm1_g', 'v_l1_w_in', 'v_l1_conv_w', 'v_l1_conv_b', 'v_l1_w_out', 'v_norm2_g', 'v_l2_w_in', 'v_l2_w_grp', 'v_l2_b_grp', 'v_l2_scale', 'v_l2_w_out', 'v_norm3_g', 'v_l3_w_in', 'v_l3_lam_re', 'v_l3_lam_im', 'v_l3_log_dt', 'v_l3_b_re', 'v_l3_b_im', 'v_l3_c_re', 'v_l3_c_im', 'v_l3_d_skip', 'v_l3_w_glu', 'v_l3_b_glu', 'v_l3_w_out', 'v_final_g']
TWIN_OUTPUTS = ['loss', 'grad_x', 'grad_meta_tokens', 'grad_norm0_g', 'grad_l0_w_in', 'grad_l0_lam_re', 'grad_l0_lam_im', 'grad_l0_log_dt', 'grad_l0_b_re', 'grad_l0_b_im', 'grad_l0_c_re', 'grad_l0_c_im', 'grad_l0_d_skip', 'grad_l0_w_glu', 'grad_l0_b_glu', 'grad_l0_w_out', 'grad_norm1_g', 'grad_l1_w_in', 'grad_l1_conv_w', 'grad_l1_conv_b', 'grad_l1_w_out', 'grad_norm2_g', 'grad_l2_w_in', 'grad_l2_w_grp', 'grad_l2_b_grp', 'grad_l2_scale', 'grad_l2_w_out', 'grad_norm3_g', 'grad_l3_w_in', 'grad_l3_lam_re', 'grad_l3_lam_im', 'grad_l3_log_dt', 'grad_l3_b_re', 'grad_l3_b_im', 'grad_l3_c_re', 'grad_l3_c_im', 'grad_l3_d_skip', 'grad_l3_w_glu', 'grad_l3_b_glu', 'grad_l3_w_out', 'grad_final_g', 'delta_meta_tokens', 'delta_norm0_g', 'delta_l0_w_in', 'delta_l0_lam_re', 'delta_l0_lam_im', 'delta_l0_log_dt', 'delta_l0_b_re', 'delta_l0_b_im', 'delta_l0_c_re', 'delta_l0_c_im', 'delta_l0_d_skip', 'delta_l0_w_glu', 'delta_l0_b_glu', 'delta_l0_w_out', 'delta_norm1_g', 'delta_l1_w_in', 'delta_l1_conv_w', 'delta_l1_conv_b', 'delta_l1_w_out', 'delta_norm2_g', 'delta_l2_w_in', 'delta_l2_w_grp', 'delta_l2_b_grp', 'delta_l2_scale', 'delta_l2_w_out', 'delta_norm3_g', 'delta_l3_w_in', 'delta_l3_lam_re', 'delta_l3_lam_im', 'delta_l3_log_dt', 'delta_l3_b_re', 'delta_l3_b_im', 'delta_l3_c_re', 'delta_l3_c_im', 'delta_l3_d_skip', 'delta_l3_w_glu', 'delta_l3_b_glu', 'delta_l3_w_out', 'delta_final_g', 'new_m_meta_tokens', 'new_m_norm0_g', 'new_m_l0_w_in', 'new_m_l0_lam_re', 'new_m_l0_lam_im', 'new_m_l0_log_dt', 'new_m_l0_b_re', 'new_m_l0_b_im', 'new_m_l0_c_re', 'new_m_l0_c_im', 'new_m_l0_d_skip', 'new_m_l0_w_glu', 'new_m_l0_b_glu', 'new_m_l0_w_out', 'new_m_norm1_g', 'new_m_l1_w_in', 'new_m_l1_conv_w', 'new_m_l1_conv_b', 'new_m_l1_w_out', 'new_m_norm2_g', 'new_m_l2_w_in', 'new_m_l2_w_grp', 'new_m_l2_b_grp', 'new_m_l2_scale', 'new_m_l2_w_out', 'new_m_norm3_g', 'new_m_l3_w_in', 'new_m_l3_lam_re', 'new_m_l3_lam_im', 'new_m_l3_log_dt', 'new_m_l3_b_re', 'new_m_l3_b_im', 'new_m_l3_c_re', 'new_m_l3_c_im', 'new_m_l3_d_skip', 'new_m_l3_w_glu', 'new_m_l3_b_glu', 'new_m_l3_w_out', 'new_m_final_g', 'new_v_meta_tokens', 'new_v_norm0_g', 'new_v_l0_w_in', 'new_v_l0_lam_re', 'new_v_l0_lam_im', 'new_v_l0_log_dt', 'new_v_l0_b_re', 'new_v_l0_b_im', 'new_v_l0_c_re', 'new_v_l0_c_im', 'new_v_l0_d_skip', 'new_v_l0_w_glu', 'new_v_l0_b_glu', 'new_v_l0_w_out', 'new_v_norm1_g', 'new_v_l1_w_in', 'new_v_l1_conv_w', 'new_v_l1_conv_b', 'new_v_l1_w_out', 'new_v_norm2_g', 'new_v_l2_w_in', 'new_v_l2_w_grp', 'new_v_l2_b_grp', 'new_v_l2_scale', 'new_v_l2_w_out', 'new_v_norm3_g', 'new_v_l3_w_in', 'new_v_l3_lam_re', 'new_v_l3_lam_im', 'new_v_l3_log_dt', 'new_v_l3_b_re', 'new_v_l3_b_im', 'new_v_l3_c_re', 'new_v_l3_c_im', 'new_v_l3_d_skip', 'new_v_l3_w_glu', 'new_v_l3_b_glu', 'new_v_l3_w_out', 'new_v_final_g']
TWIN_LEAF_KINDS = {'loss': 'loss', 'grad_x': 'grad_x', 'grad_meta_tokens': 'grad_w', 'grad_norm0_g': 'grad_w', 'grad_l0_w_in': 'grad_w', 'grad_l0_lam_re': 'grad_w', 'grad_l0_lam_im': 'grad_w', 'grad_l0_log_dt': 'grad_w', 'grad_l0_b_re': 'grad_w', 'grad_l0_b_im': 'grad_w', 'grad_l0_c_re': 'grad_w', 'grad_l0_c_im': 'grad_w', 'grad_l0_d_skip': 'grad_w', 'grad_l0_w_glu': 'grad_w', 'grad_l0_b_glu': 'grad_w', 'grad_l0_w_out': 'grad_w', 'grad_norm1_g': 'grad_w', 'grad_l1_w_in': 'grad_w', 'grad_l1_conv_w': 'grad_w', 'grad_l1_conv_b': 'grad_w', 'grad_l1_w_out': 'grad_w', 'grad_norm2_g': 'grad_w', 'grad_l2_w_in': 'grad_w', 'grad_l2_w_grp': 'grad_w', 'grad_l2_b_grp': 'grad_w', 'grad_l2_scale': 'grad_w', 'grad_l2_w_out': 'grad_w', 'grad_norm3_g': 'grad_w', 'grad_l3_w_in': 'grad_w', 'grad_l3_lam_re': 'grad_w', 'grad_l3_lam_im': 'grad_w', 'grad_l3_log_dt': 'grad_w', 'grad_l3_b_re': 'grad_w', 'grad_l3_b_im': 'grad_w', 'grad_l3_c_re': 'grad_w', 'grad_l3_c_im': 'grad_w', 'grad_l3_d_skip': 'grad_w', 'grad_l3_w_glu': 'grad_w', 'grad_l3_b_glu': 'grad_w', 'grad_l3_w_out': 'grad_w', 'grad_final_g': 'grad_w', 'delta_meta_tokens': 'delta_w', 'delta_norm0_g': 'delta_w', 'delta_l0_w_in': 'delta_w', 'delta_l0_lam_re': 'delta_w', 'delta_l0_lam_im': 'delta_w', 'delta_l0_log_dt': 'delta_w', 'delta_l0_b_re': 'delta_w', 'delta_l0_b_im': 'delta_w', 'delta_l0_c_re': 'delta_w', 'delta_l0_c_im': 'delta_w', 'delta_l0_d_skip': 'delta_w', 'delta_l0_w_glu': 'delta_w', 'delta_l0_b_glu': 'delta_w', 'delta_l0_w_out': 'delta_w', 'delta_norm1_g': 'delta_w', 'delta_l1_w_in': 'delta_w', 'delta_l1_conv_w': 'delta_w', 'delta_l1_conv_b': 'delta_w', 'delta_l1_w_out': 'delta_w', 'delta_norm2_g': 'delta_w', 'delta_l2_w_in': 'delta_w', 'delta_l2_w_grp': 'delta_w', 'delta_l2_b_grp': 'delta_w', 'delta_l2_scale': 'delta_w', 'delta_l2_w_out': 'delta_w', 'delta_norm3_g': 'delta_w', 'delta_l3_w_in': 'delta_w', 'delta_l3_lam_re': 'delta_w', 'delta_l3_lam_im': 'delta_w', 'delta_l3_log_dt': 'delta_w', 'delta_l3_b_re': 'delta_w', 'delta_l3_b_im': 'delta_w', 'delta_l3_c_re': 'delta_w', 'delta_l3_c_im': 'delta_w', 'delta_l3_d_skip': 'delta_w', 'delta_l3_w_glu': 'delta_w', 'delta_l3_b_glu': 'delta_w', 'delta_l3_w_out': 'delta_w', 'delta_final_g': 'delta_w', 'new_m_meta_tokens': 'new_m', 'new_m_norm0_g': 'new_m', 'new_m_l0_w_in': 'new_m', 'new_m_l0_lam_re': 'new_m', 'new_m_l0_lam_im': 'new_m', 'new_m_l0_log_dt': 'new_m', 'new_m_l0_b_re': 'new_m', 'new_m_l0_b_im': 'new_m', 'new_m_l0_c_re': 'new_m', 'new_m_l0_c_im': 'new_m', 'new_m_l0_d_skip': 'new_m', 'new_m_l0_w_glu': 'new_m', 'new_m_l0_b_glu': 'new_m', 'new_m_l0_w_out': 'new_m', 'new_m_norm1_g': 'new_m', 'new_m_l1_w_in': 'new_m', 'new_m_l1_conv_w': 'new_m', 'new_m_l1_conv_b': 'new_m', 'new_m_l1_w_out': 'new_m', 'new_m_norm2_g': 'new_m', 'new_m_l2_w_in': 'new_m', 'new_m_l2_w_grp': 'new_m', 'new_m_l2_b_grp': 'new_m', 'new_m_l2_scale': 'new_m', 'new_m_l2_w_out': 'new_m', 'new_m_norm3_g': 'new_m', 'new_m_l3_w_in': 'new_m', 'new_m_l3_lam_re': 'new_m', 'new_m_l3_lam_im': 'new_m', 'new_m_l3_log_dt': 'new_m', 'new_m_l3_b_re': 'new_m', 'new_m_l3_b_im': 'new_m', 'new_m_l3_c_re': 'new_m', 'new_m_l3_c_im': 'new_m', 'new_m_l3_d_skip': 'new_m', 'new_m_l3_w_glu': 'new_m', 'new_m_l3_b_glu': 'new_m', 'new_m_l3_w_out': 'new_m', 'new_m_final_g': 'new_m', 'new_v_meta_tokens': 'new_v', 'new_v_norm0_g': 'new_v', 'new_v_l0_w_in': 'new_v', 'new_v_l0_lam_re': 'new_v', 'new_v_l0_lam_im': 'new_v', 'new_v_l0_log_dt': 'new_v', 'new_v_l0_b_re': 'new_v', 'new_v_l0_b_im': 'new_v', 'new_v_l0_c_re': 'new_v', 'new_v_l0_c_im': 'new_v', 'new_v_l0_d_skip': 'new_v', 'new_v_l0_w_glu': 'new_v', 'new_v_l0_b_glu': 'new_v', 'new_v_l0_w_out': 'new_v', 'new_v_norm1_g': 'new_v', 'new_v_l1_w_in': 'new_v', 'new_v_l1_conv_w': 'new_v', 'new_v_l1_conv_b': 'new_v', 'new_v_l1_w_out': 'new_v', 'new_v_norm2_g': 'new_v', 'new_v_l2_w_in': 'new_v', 'new_v_l2_w_grp': 'new_v', 'new_v_l2_b_grp': 'new_v', 'new_v_l2_scale': 'new_v', 'new_v_l2_w_out': 'new_v', 'new_v_norm3_g': 'new_v', 'new_v_l3_w_in': 'new_v', 'new_v_l3_lam_re': 'new_v', 'new_v_l3_lam_im': 'new_v', 'new_v_l3_log_dt': 'new_v', 'new_v_l3_b_re': 'new_v', 'new_v_l3_b_im': 'new_v', 'new_v_l3_c_re': 'new_v', 'new_v_l3_c_im': 'new_v', 'new_v_l3_d_skip': 'new_v', 'new_v_l3_w_glu': 'new_v', 'new_v_l3_b_glu': 'new_v', 'new_v_l3_w_out': 'new_v', 'new_v_final_g': 'new_v'}


def _forward(args):
    return _fwd_reference(*[args[k] for k in FWD_PARAMS])


def _output_shape():
    out = _jax.eval_shape(lambda: _forward(_fwd_setup_inputs(0)))
    return out.shape, out.dtype

N_MICROBATCH = 1
ADAM_LR = 0.001
ADAM_B1 = 0.9
ADAM_B2 = 0.999
ADAM_EPS = 1e-08
ADAM_WD = 0.01
ADAM_STEP = 10
PER_EXAMPLE_BATCH_AXIS = {'x': 0, 'loss_target': 0}
SHARED_INPUTS = []
_WEIGHT_DTYPES = {'meta_tokens': _jnp.float32, 'norm0_g': _jnp.float32, 'l0_w_in': _jnp.float32, 'l0_lam_re': _jnp.float32, 'l0_lam_im': _jnp.float32, 'l0_log_dt': _jnp.float32, 'l0_b_re': _jnp.float32, 'l0_b_im': _jnp.float32, 'l0_c_re': _jnp.float32, 'l0_c_im': _jnp.float32, 'l0_d_skip': _jnp.float32, 'l0_w_glu': _jnp.float32, 'l0_b_glu': _jnp.float32, 'l0_w_out': _jnp.float32, 'norm1_g': _jnp.float32, 'l1_w_in': _jnp.float32, 'l1_conv_w': _jnp.float32, 'l1_conv_b': _jnp.float32, 'l1_w_out': _jnp.float32, 'norm2_g': _jnp.float32, 'l2_w_in': _jnp.float32, 'l2_w_grp': _jnp.float32, 'l2_b_grp': _jnp.float32, 'l2_scale': _jnp.float32, 'l2_w_out': _jnp.float32, 'norm3_g': _jnp.float32, 'l3_w_in': _jnp.float32, 'l3_lam_re': _jnp.float32, 'l3_lam_im': _jnp.float32, 'l3_log_dt': _jnp.float32, 'l3_b_re': _jnp.float32, 'l3_b_im': _jnp.float32, 'l3_c_re': _jnp.float32, 'l3_c_im': _jnp.float32, 'l3_d_skip': _jnp.float32, 'l3_w_glu': _jnp.float32, 'l3_b_glu': _jnp.float32, 'l3_w_out': _jnp.float32, 'final_g': _jnp.float32}
MOMENT_SCALE = {'meta_tokens': 2.347543e-03, 'norm0_g': 7.955675e-02, 'l0_w_in': 5.660083e-02, 'l0_lam_re': 3.068966e-03, 'l0_lam_im': 2.768014e-03, 'l0_log_dt': 2.362331e+00, 'l0_b_re': 1.778346e-03, 'l0_b_im': 1.767152e-03, 'l0_c_re': 3.550124e-03, 'l0_c_im': 3.577773e-03, 'l0_d_skip': 6.035771e-02, 'l0_w_glu': 1.719597e-02, 'l0_b_glu': 3.078605e-02, 'l0_w_out': 5.400041e-02, 'norm1_g': 1.999278e-01, 'l1_w_in': 7.065557e-02, 'l1_conv_w': 7.327814e-02, 'l1_conv_b': 7.119716e-02, 'l1_w_out': 9.846800e-02, 'norm2_g': 1.092607e-01, 'l2_w_in': 5.207243e-02, 'l2_w_grp': 5.149129e-02, 'l2_b_grp': 5.799075e-02, 'l2_scale': 5.182771e-02, 'l2_w_out': 7.245167e-02, 'norm3_g': 4.222641e-02, 'l3_w_in': 2.968085e-02, 'l3_lam_re': 1.591094e-03, 'l3_lam_im': 1.634554e-03, 'l3_log_dt': 9.715484e-01, 'l3_b_re': 9.716849e-04, 'l3_b_im': 9.690803e-04, 'l3_c_re': 2.003158e-03, 'l3_c_im': 1.908270e-03, 'l3_d_skip': 3.157341e-02, 'l3_w_glu': 8.379649e-03, 'l3_b_glu': 1.284899e-02, 'l3_w_out': 2.852457e-02, 'final_g': 3.200758e+01}


def _to_microbatches(a, axis):
    t = _jnp.moveaxis(a, axis, 0)
    t = t.reshape((N_MICROBATCH, t.shape[0] // N_MICROBATCH) + t.shape[1:])
    return _jnp.moveaxis(t, 1, axis + 1)


def setup_inputs(seed: int = 0) -> dict:
    inp = _fwd_setup_inputs(seed)
    key = _jax.random.fold_in(_jax.random.key(seed), 7919)
    shape, _ = _output_shape()
    out = dict(inp)
    out["loss_target"] = _jax.random.normal(_jax.random.fold_in(key, 0), shape, _jnp.float32)
    for i, name in enumerate(TWIN_WEIGHTS):
        w = inp[name].astype(_jnp.float32)
        if MOMENT_SCALE is None:
            s = _jnp.sqrt(_jnp.mean(_jnp.square(w)) + 1e-30)
        else:
            s = MOMENT_SCALE[name]
        km, kv = _jax.random.split(_jax.random.fold_in(key, i + 1))
        out[name] = w
        out["m_" + name] = s * _jax.random.normal(km, w.shape, _jnp.float32)
        out["v_" + name] = (s * s) * _jax.random.uniform(kv, w.shape, _jnp.float32, 0.5, 1.5)
    if N_MICROBATCH > 1:
        for name, axis in PER_EXAMPLE_BATCH_AXIS.items():
            out[name] = _to_microbatches(out[name], axis)
    return {'x': out['x'], 'meta_tokens': out['meta_tokens'], 'norm0_g': out['norm0_g'], 'l0_w_in': out['l0_w_in'], 'l0_lam_re': out['l0_lam_re'], 'l0_lam_im': out['l0_lam_im'], 'l0_log_dt': out['l0_log_dt'], 'l0_b_re': out['l0_b_re'], 'l0_b_im': out['l0_b_im'], 'l0_c_re': out['l0_c_re'], 'l0_c_im': out['l0_c_im'], 'l0_d_skip': out['l0_d_skip'], 'l0_w_glu': out['l0_w_glu'], 'l0_b_glu': out['l0_b_glu'], 'l0_w_out': out['l0_w_out'], 'norm1_g': out['norm1_g'], 'l1_w_in': out['l1_w_in'], 'l1_conv_w': out['l1_conv_w'], 'l1_conv_b': out['l1_conv_b'], 'l1_w_out': out['l1_w_out'], 'norm2_g': out['norm2_g'], 'l2_w_in': out['l2_w_in'], 'l2_w_grp': out['l2_w_grp'], 'l2_b_grp': out['l2_b_grp'], 'l2_scale': out['l2_scale'], 'l2_w_out': out['l2_w_out'], 'norm3_g': out['norm3_g'], 'l3_w_in': out['l3_w_in'], 'l3_lam_re': out['l3_lam_re'], 'l3_lam_im': out['l3_lam_im'], 'l3_log_dt': out['l3_log_dt'], 'l3_b_re': out['l3_b_re'], 'l3_b_im': out['l3_b_im'], 'l3_c_re': out['l3_c_re'], 'l3_c_im': out['l3_c_im'], 'l3_d_skip': out['l3_d_skip'], 'l3_w_glu': out['l3_w_glu'], 'l3_b_glu': out['l3_b_glu'], 'l3_w_out': out['l3_w_out'], 'final_g': out['final_g'], 'loss_target': out['loss_target'], 'm_meta_tokens': out['m_meta_tokens'], 'm_norm0_g': out['m_norm0_g'], 'm_l0_w_in': out['m_l0_w_in'], 'm_l0_lam_re': out['m_l0_lam_re'], 'm_l0_lam_im': out['m_l0_lam_im'], 'm_l0_log_dt': out['m_l0_log_dt'], 'm_l0_b_re': out['m_l0_b_re'], 'm_l0_b_im': out['m_l0_b_im'], 'm_l0_c_re': out['m_l0_c_re'], 'm_l0_c_im': out['m_l0_c_im'], 'm_l0_d_skip': out['m_l0_d_skip'], 'm_l0_w_glu': out['m_l0_w_glu'], 'm_l0_b_glu': out['m_l0_b_glu'], 'm_l0_w_out': out['m_l0_w_out'], 'm_norm1_g': out['m_norm1_g'], 'm_l1_w_in': out['m_l1_w_in'], 'm_l1_conv_w': out['m_l1_conv_w'], 'm_l1_conv_b': out['m_l1_conv_b'], 'm_l1_w_out': out['m_l1_w_out'], 'm_norm2_g': out['m_norm2_g'], 'm_l2_w_in': out['m_l2_w_in'], 'm_l2_w_grp': out['m_l2_w_grp'], 'm_l2_b_grp': out['m_l2_b_grp'], 'm_l2_scale': out['m_l2_scale'], 'm_l2_w_out': out['m_l2_w_out'], 'm_norm3_g': out['m_norm3_g'], 'm_l3_w_in': out['m_l3_w_in'], 'm_l3_lam_re': out['m_l3_lam_re'], 'm_l3_lam_im': out['m_l3_lam_im'], 'm_l3_log_dt': out['m_l3_log_dt'], 'm_l3_b_re': out['m_l3_b_re'], 'm_l3_b_im': out['m_l3_b_im'], 'm_l3_c_re': out['m_l3_c_re'], 'm_l3_c_im': out['m_l3_c_im'], 'm_l3_d_skip': out['m_l3_d_skip'], 'm_l3_w_glu': out['m_l3_w_glu'], 'm_l3_b_glu': out['m_l3_b_glu'], 'm_l3_w_out': out['m_l3_w_out'], 'm_final_g': out['m_final_g'], 'v_meta_tokens': out['v_meta_tokens'], 'v_norm0_g': out['v_norm0_g'], 'v_l0_w_in': out['v_l0_w_in'], 'v_l0_lam_re': out['v_l0_lam_re'], 'v_l0_lam_im': out['v_l0_lam_im'], 'v_l0_log_dt': out['v_l0_log_dt'], 'v_l0_b_re': out['v_l0_b_re'], 'v_l0_b_im': out['v_l0_b_im'], 'v_l0_c_re': out['v_l0_c_re'], 'v_l0_c_im': out['v_l0_c_im'], 'v_l0_d_skip': out['v_l0_d_skip'], 'v_l0_w_glu': out['v_l0_w_glu'], 'v_l0_b_glu': out['v_l0_b_glu'], 'v_l0_w_out': out['v_l0_w_out'], 'v_norm1_g': out['v_norm1_g'], 'v_l1_w_in': out['v_l1_w_in'], 'v_l1_conv_w': out['v_l1_conv_w'], 'v_l1_conv_b': out['v_l1_conv_b'], 'v_l1_w_out': out['v_l1_w_out'], 'v_norm2_g': out['v_norm2_g'], 'v_l2_w_in': out['v_l2_w_in'], 'v_l2_w_grp': out['v_l2_w_grp'], 'v_l2_b_grp': out['v_l2_b_grp'], 'v_l2_scale': out['v_l2_scale'], 'v_l2_w_out': out['v_l2_w_out'], 'v_norm3_g': out['v_norm3_g'], 'v_l3_w_in': out['v_l3_w_in'], 'v_l3_lam_re': out['v_l3_lam_re'], 'v_l3_lam_im': out['v_l3_lam_im'], 'v_l3_log_dt': out['v_l3_log_dt'], 'v_l3_b_re': out['v_l3_b_re'], 'v_l3_b_im': out['v_l3_b_im'], 'v_l3_c_re': out['v_l3_c_re'], 'v_l3_c_im': out['v_l3_c_im'], 'v_l3_d_skip': out['v_l3_d_skip'], 'v_l3_w_glu': out['v_l3_w_glu'], 'v_l3_b_glu': out['v_l3_b_glu'], 'v_l3_w_out': out['v_l3_w_out'], 'v_final_g': out['v_final_g']}


def _loss(weights, diff, rest, loss_target):
    with _jax.named_scope("forward"):
        args = {**rest, TWIN_DIFF_INPUT: diff, **{k: w.astype(_WEIGHT_DTYPES[k]) for k, w in weights.items()}}
        y = _forward(args)
    with _jax.named_scope("loss_head"):
        err = _jnp.square(y.astype(_jnp.float32) - loss_target)
        return 0.5 * _jnp.sum(_jnp.mean(err, axis=-1)) if err.ndim else 0.5 * err


def _adamw(w, g, m, v):
    m = ADAM_B1 * m + (1.0 - ADAM_B1) * g
    v = ADAM_B2 * v + (1.0 - ADAM_B2) * _jnp.square(g)
    m_hat = m / (1.0 - ADAM_B1 ** ADAM_STEP)
    v_hat = v / (1.0 - ADAM_B2 ** ADAM_STEP)
    delta = -ADAM_LR * (m_hat / (_jnp.sqrt(v_hat) + ADAM_EPS) + ADAM_WD * w)
    return delta, m, v


def reference(x, meta_tokens, norm0_g, l0_w_in, l0_lam_re, l0_lam_im, l0_log_dt, l0_b_re, l0_b_im, l0_c_re, l0_c_im, l0_d_skip, l0_w_glu, l0_b_glu, l0_w_out, norm1_g, l1_w_in, l1_conv_w, l1_conv_b, l1_w_out, norm2_g, l2_w_in, l2_w_grp, l2_b_grp, l2_scale, l2_w_out, norm3_g, l3_w_in, l3_lam_re, l3_lam_im, l3_log_dt, l3_b_re, l3_b_im, l3_c_re, l3_c_im, l3_d_skip, l3_w_glu, l3_b_glu, l3_w_out, final_g, loss_target, m_meta_tokens, m_norm0_g, m_l0_w_in, m_l0_lam_re, m_l0_lam_im, m_l0_log_dt, m_l0_b_re, m_l0_b_im, m_l0_c_re, m_l0_c_im, m_l0_d_skip, m_l0_w_glu, m_l0_b_glu, m_l0_w_out, m_norm1_g, m_l1_w_in, m_l1_conv_w, m_l1_conv_b, m_l1_w_out, m_norm2_g, m_l2_w_in, m_l2_w_grp, m_l2_b_grp, m_l2_scale, m_l2_w_out, m_norm3_g, m_l3_w_in, m_l3_lam_re, m_l3_lam_im, m_l3_log_dt, m_l3_b_re, m_l3_b_im, m_l3_c_re, m_l3_c_im, m_l3_d_skip, m_l3_w_glu, m_l3_b_glu, m_l3_w_out, m_final_g, v_meta_tokens, v_norm0_g, v_l0_w_in, v_l0_lam_re, v_l0_lam_im, v_l0_log_dt, v_l0_b_re, v_l0_b_im, v_l0_c_re, v_l0_c_im, v_l0_d_skip, v_l0_w_glu, v_l0_b_glu, v_l0_w_out, v_norm1_g, v_l1_w_in, v_l1_conv_w, v_l1_conv_b, v_l1_w_out, v_norm2_g, v_l2_w_in, v_l2_w_grp, v_l2_b_grp, v_l2_scale, v_l2_w_out, v_norm3_g, v_l3_w_in, v_l3_lam_re, v_l3_lam_im, v_l3_log_dt, v_l3_b_re, v_l3_b_im, v_l3_c_re, v_l3_c_im, v_l3_d_skip, v_l3_w_glu, v_l3_b_glu, v_l3_w_out, v_final_g):
    given = dict(x=x, meta_tokens=meta_tokens, norm0_g=norm0_g, l0_w_in=l0_w_in, l0_lam_re=l0_lam_re, l0_lam_im=l0_lam_im, l0_log_dt=l0_log_dt, l0_b_re=l0_b_re, l0_b_im=l0_b_im, l0_c_re=l0_c_re, l0_c_im=l0_c_im, l0_d_skip=l0_d_skip, l0_w_glu=l0_w_glu, l0_b_glu=l0_b_glu, l0_w_out=l0_w_out, norm1_g=norm1_g, l1_w_in=l1_w_in, l1_conv_w=l1_conv_w, l1_conv_b=l1_conv_b, l1_w_out=l1_w_out, norm2_g=norm2_g, l2_w_in=l2_w_in, l2_w_grp=l2_w_grp, l2_b_grp=l2_b_grp, l2_scale=l2_scale, l2_w_out=l2_w_out, norm3_g=norm3_g, l3_w_in=l3_w_in, l3_lam_re=l3_lam_re, l3_lam_im=l3_lam_im, l3_log_dt=l3_log_dt, l3_b_re=l3_b_re, l3_b_im=l3_b_im, l3_c_re=l3_c_re, l3_c_im=l3_c_im, l3_d_skip=l3_d_skip, l3_w_glu=l3_w_glu, l3_b_glu=l3_b_glu, l3_w_out=l3_w_out, final_g=final_g, loss_target=loss_target, m_meta_tokens=m_meta_tokens, m_norm0_g=m_norm0_g, m_l0_w_in=m_l0_w_in, m_l0_lam_re=m_l0_lam_re, m_l0_lam_im=m_l0_lam_im, m_l0_log_dt=m_l0_log_dt, m_l0_b_re=m_l0_b_re, m_l0_b_im=m_l0_b_im, m_l0_c_re=m_l0_c_re, m_l0_c_im=m_l0_c_im, m_l0_d_skip=m_l0_d_skip, m_l0_w_glu=m_l0_w_glu, m_l0_b_glu=m_l0_b_glu, m_l0_w_out=m_l0_w_out, m_norm1_g=m_norm1_g, m_l1_w_in=m_l1_w_in, m_l1_conv_w=m_l1_conv_w, m_l1_conv_b=m_l1_conv_b, m_l1_w_out=m_l1_w_out, m_norm2_g=m_norm2_g, m_l2_w_in=m_l2_w_in, m_l2_w_grp=m_l2_w_grp, m_l2_b_grp=m_l2_b_grp, m_l2_scale=m_l2_scale, m_l2_w_out=m_l2_w_out, m_norm3_g=m_norm3_g, m_l3_w_in=m_l3_w_in, m_l3_lam_re=m_l3_lam_re, m_l3_lam_im=m_l3_lam_im, m_l3_log_dt=m_l3_log_dt, m_l3_b_re=m_l3_b_re, m_l3_b_im=m_l3_b_im, m_l3_c_re=m_l3_c_re, m_l3_c_im=m_l3_c_im, m_l3_d_skip=m_l3_d_skip, m_l3_w_glu=m_l3_w_glu, m_l3_b_glu=m_l3_b_glu, m_l3_w_out=m_l3_w_out, m_final_g=m_final_g, v_meta_tokens=v_meta_tokens, v_norm0_g=v_norm0_g, v_l0_w_in=v_l0_w_in, v_l0_lam_re=v_l0_lam_re, v_l0_lam_im=v_l0_lam_im, v_l0_log_dt=v_l0_log_dt, v_l0_b_re=v_l0_b_re, v_l0_b_im=v_l0_b_im, v_l0_c_re=v_l0_c_re, v_l0_c_im=v_l0_c_im, v_l0_d_skip=v_l0_d_skip, v_l0_w_glu=v_l0_w_glu, v_l0_b_glu=v_l0_b_glu, v_l0_w_out=v_l0_w_out, v_norm1_g=v_norm1_g, v_l1_w_in=v_l1_w_in, v_l1_conv_w=v_l1_conv_w, v_l1_conv_b=v_l1_conv_b, v_l1_w_out=v_l1_w_out, v_norm2_g=v_norm2_g, v_l2_w_in=v_l2_w_in, v_l2_w_grp=v_l2_w_grp, v_l2_b_grp=v_l2_b_grp, v_l2_scale=v_l2_scale, v_l2_w_out=v_l2_w_out, v_norm3_g=v_norm3_g, v_l3_w_in=v_l3_w_in, v_l3_lam_re=v_l3_lam_re, v_l3_lam_im=v_l3_lam_im, v_l3_log_dt=v_l3_log_dt, v_l3_b_re=v_l3_b_re, v_l3_b_im=v_l3_b_im, v_l3_c_re=v_l3_c_re, v_l3_c_im=v_l3_c_im, v_l3_d_skip=v_l3_d_skip, v_l3_w_glu=v_l3_w_glu, v_l3_b_glu=v_l3_b_glu, v_l3_w_out=v_l3_w_out, v_final_g=v_final_g)
    weights = {n: given[n] for n in TWIN_WEIGHTS}
    shared = {n: given[n] for n in SHARED_INPUTS}
    per_example = {n: given[n] for n in ['x']}
    grad_fn = _jax.value_and_grad(_loss, argnums=(0, 1))

    def one_microbatch(ex, loss_target):
        ex = dict(ex)
        diff = ex.pop(TWIN_DIFF_INPUT)
        return grad_fn(weights, diff, {**shared, **ex}, loss_target)

    if N_MICROBATCH == 1:
        loss, (grad_w, grad_x) = one_microbatch(per_example, given["loss_target"])
    else:
        def body(carry, xs):
            loss_sum, grad_sum = carry
            l_k, (gw_k, gx_k) = one_microbatch(xs[0], xs[1])
            with _jax.named_scope("update"):
                return (loss_sum + l_k, _jax.tree.map(_jnp.add, grad_sum, gw_k)), gx_k

        init = (_jnp.zeros((), _jnp.float32), _jax.tree.map(_jnp.zeros_like, weights))
        (loss, grad_w), grad_x = _jax.lax.scan(body, init, (per_example, given["loss_target"]))
    with _jax.named_scope("update"):
        delta_w, new_m, new_v = {}, {}, {}
        for n in TWIN_WEIGHTS:
            delta_w[n], new_m[n], new_v[n] = _adamw(weights[n], grad_w[n], given["m_" + n], given["v_" + n])
    return (loss, grad_x, *[grad_w[n] for n in TWIN_WEIGHTS], *[delta_w[n] for n in TWIN_WEIGHTS],
            *[new_m[n] for n in TWIN_WEIGHTS], *[new_v[n] for n in TWIN_WEIGHTS])
```

```python
import functools
import math

import jax
import jax.numpy as jnp
from jax import lax
from jax.experimental import pallas as pl
from jax.experimental.pallas import tpu as pltpu

F32 = jnp.float32
BF16 = jnp.bfloat16
EPS = 1e-6
N_DEV = 8
TOKEN_TILE = 256
SCAN_TILE = 64
S5_GROUP = 16
S5_STATE = 64
POOL_WINDOWS = (2, 4, 8, 16)
POOL_HALO = 16
CONV_K = 3
CONV_HALO = 8
ADAM_LR = 0.001
ADAM_B1 = 0.9
ADAM_B2 = 0.999
ADAM_EPS = 1e-08
ADAM_WD = 0.01
ADAM_STEP = 10
GELU_C = math.sqrt(2.0 / math.pi)
GELU_A = 0.044715
UPDATE_TILE_ELEMS = 1 << 17
PACK_ROWS = 512
VMEM_LIMIT = 56 << 20

ANY = pl.BlockSpec(memory_space=pl.ANY)


def _params(vmem=VMEM_LIMIT, ndim=1):
    return pltpu.CompilerParams(vmem_limit_bytes=vmem, dimension_semantics=("arbitrary",) * ndim)


def _dot(a, b):
    return jnp.dot(a.astype(BF16), b.astype(BF16), preferred_element_type=F32)


def _dot_nt(a, b):
    return lax.dot_general(a.astype(BF16), b.astype(BF16), (((1,), (1,)), ((), ())), preferred_element_type=F32)


def _dot_tn(a, b):
    return lax.dot_general(a.astype(BF16), b.astype(BF16), (((0,), (0,)), ((), ())), preferred_element_type=F32)


def _rms_fwd(h, g):
    r = lax.rsqrt(jnp.mean(h * h, axis=-1, keepdims=True) + EPS)
    hh = h * r
    return hh * g, hh, r


def _rms_bwd(dn, hh, r, g):
    dhh = dn * g
    return r * (dhh - hh * jnp.mean(dhh * hh, axis=-1, keepdims=True))


def _sigmoid(x):
    return 1.0 / (1.0 + jnp.exp(-x))


def _silu_and_grad(z):
    s = _sigmoid(z)
    return z * s, s * (1.0 + z * (1.0 - s))


def _gelu(y):
    t = jnp.tanh(GELU_C * (y + GELU_A * y * y * y))
    return 0.5 * y * (1.0 + t), t


def _gelu_grad(y, t):
    return 0.5 * (1.0 + t) + 0.5 * y * (1.0 - t * t) * GELU_C * (1.0 + 3.0 * GELU_A * y * y)


def _rows(shape):
    return lax.broadcasted_iota(jnp.int32, shape, 0)


def _shift_down(x, k, halo):
    y = pltpu.roll(x, k, 0)
    rows = _rows(x.shape)
    for j in range(k):
        y = jnp.where(rows == j, halo[halo.shape[0] - k + j:halo.shape[0] - k + j + 1, :], y)
    return y


def _shift_up(x, k, halo):
    n = x.shape[0]
    y = pltpu.roll(x, n - k, 0)
    rows = _rows(x.shape)
    for j in range(k):
        y = jnp.where(rows == n - k + j, halo[j:j + 1, :], y)
    return y


def _window_sums_back(ext):
    out = []
    s = ext
    for k in (1, 2, 4, 8):
        s = s + pltpu.roll(s, k, 0)
        out.append(s)
    return out


def _window_sums_fwd(ext):
    n = ext.shape[0]
    out = []
    s = ext
    for k in (1, 2, 4, 8):
        s = s + pltpu.roll(s, n - k, 0)
        out.append(s)
    return out


def _pool_inv_count(tile, tt, first_pos, w, width):
    pos = _rows((tt, width)) + (tile * tt - first_pos + 1)
    return 1.0 / jnp.clip(pos, 1, w).astype(F32)


def _s5_disc_math(lr, li, ldt, br, bi):
    dt = jnp.exp(ldt)
    mag = jnp.exp(lr * dt)
    ar = mag * jnp.cos(li * dt)
    ai = mag * jnp.sin(li * dt)
    den = lr * lr + li * li
    kr = ((ar - 1.0) * lr + ai * li) / den
    ki = (ai * lr - (ar - 1.0) * li) / den
    bbr = kr[None] * br - ki[None] * bi
    bbi = kr[None] * bi + ki[None] * br
    return ar, ai, bbr, bbi


def s5_disc_fwd(lr, li, ldt, br_t, bi_t, name):
    def body(lr_ref, li_ref, ldt_ref, br_ref, bi_ref, ar_ref, ai_ref, bbr_ref, bbi_ref):
        ar, ai, bbr, bbi = _s5_disc_math(lr_ref[...], li_ref[...], ldt_ref[...], br_ref[...], bi_ref[...])
        ar_ref[...] = ar
        ai_ref[...] = ai
        bbr_ref[...] = bbr
        bbi_ref[...] = bbi

    sd = jax.ShapeDtypeStruct
    return pl.pallas_call(
        body, name=name,
        out_shape=(sd(lr.shape, F32), sd(lr.shape, F32), sd(br_t.shape, F32), sd(br_t.shape, F32)),
    )(lr, li, ldt, br_t, bi_t)


def s5_disc_bwd(lr, li, ldt, br_t, bi_t, dar, dai, dbbr, dbbi, name):
    def body(lr_ref, li_ref, ldt_ref, br_ref, bi_ref, dar_ref, dai_ref, dbbr_ref, dbbi_ref,
             dlr_ref, dli_ref, dldt_ref, dbr_ref, dbi_ref):
        _, vjp = jax.vjp(_s5_disc_math, lr_ref[...], li_ref[...], ldt_ref[...], br_ref[...], bi_ref[...])
        dlr, dli, dldt, dbr, dbi = vjp((dar_ref[...], dai_ref[...], dbbr_ref[...], dbbi_ref[...]))
        dlr_ref[...] = dlr
        dli_ref[...] = dli
        dldt_ref[...] = dldt
        dbr_ref[...] = dbr
        dbi_ref[...] = dbi

    sd = jax.ShapeDtypeStruct
    return pl.pallas_call(
        body, name=name,
        out_shape=(sd(lr.shape, F32), sd(lr.shape, F32), sd(ldt.shape, F32), sd(br_t.shape, F32), sd(br_t.shape, F32)),
    )(lr, li, ldt, br_t, bi_t, dar, dai, dbbr, dbbi)


def s5_fwd1(h, g, w_in, bdre, bdim, name):
    lp, d = h.shape
    tt = TOKEN_TILE
    cw, sw = bdre.shape[1], bdre.shape[2]

    def body(h_ref, g_ref, w_hbm, bdre_hbm, bdim_hbm, u_ref, z_ref, xr_ref, xi_ref, w, bre, bim, n_sc):
        i, c = pl.program_id(0), pl.program_id(1)

        @pl.when((i == 0) & (c == 0))
        def _():
            pltpu.sync_copy(w_hbm, w)
            pltpu.sync_copy(bdre_hbm, bre)
            pltpu.sync_copy(bdim_hbm, bim)

        @pl.when(c == 0)
        def _():
            n_sc[...] = _rms_fwd(h_ref[...], g_ref[...])[0].astype(BF16)

        n = n_sc[...]
        u = jnp.dot(n, w[c], preferred_element_type=F32)
        u_ref[...] = u
        z_ref[...] = jnp.dot(n, w[c + 4], preferred_element_type=F32)
        ub = u.astype(BF16)
        xr_ref[...] = jnp.dot(ub, bre[c], preferred_element_type=F32)
        xi_ref[...] = jnp.dot(ub, bim[c], preferred_element_type=F32)

    sd = jax.ShapeDtypeStruct
    return pl.pallas_call(
        body, name=name, grid=(lp // tt, 4),
        in_specs=[pl.BlockSpec((tt, d), lambda i, c: (i, 0)), pl.BlockSpec((1, d), lambda i, c: (0, 0)), ANY, ANY, ANY],
        out_specs=[pl.BlockSpec((tt, cw), lambda i, c: (i, c)), pl.BlockSpec((tt, cw), lambda i, c: (i, c)),
                   pl.BlockSpec((tt, sw), lambda i, c: (i, c)), pl.BlockSpec((tt, sw), lambda i, c: (i, c))],
        out_shape=(sd((lp, d), F32), sd((lp, d), F32), sd((lp, 4 * sw), F32), sd((lp, 4 * sw), F32)),
        scratch_shapes=[pltpu.VMEM(w_in.shape, BF16), pltpu.VMEM(bdre.shape, BF16), pltpu.VMEM(bdim.shape, BF16),
                        pltpu.VMEM((tt, d), BF16)],
        compiler_params=_params(ndim=2),
    )(h, g, w_in, bdre, bdim)


def s5_scan_fwd(xr, xi, ar, ai, name):
    r = ar.shape[0]
    ts = SCAN_TILE
    lp = xr.shape[0] // r

    def body(xr_ref, xi_ref, ar_ref, ai_ref, sr_ref, si_ref, st_r, st_i):
        @pl.when(pl.program_id(0) == 0)
        def _():
            st_r[...] = jnp.zeros_like(st_r)
            st_i[...] = jnp.zeros_like(st_i)

        a_r, a_i = ar_ref[...], ai_ref[...]

        def step(t, carry):
            s_r, s_i = carry
            rows = pl.ds(pl.multiple_of(t * r, r), r)
            n_r = a_r * s_r - a_i * s_i + xr_ref[rows, :]
            n_i = a_r * s_i + a_i * s_r + xi_ref[rows, :]
            sr_ref[rows, :] = n_r
            si_ref[rows, :] = n_i
            return n_r, n_i

        s_r, s_i = lax.fori_loop(0, ts, step, (st_r[...], st_i[...]), unroll=4)
        st_r[...] = s_r
        st_i[...] = s_i

    blk = pl.BlockSpec((ts * r, 128), lambda i: (i, 0))
    par = pl.BlockSpec((r, 128), lambda i: (0, 0))
    sd = jax.ShapeDtypeStruct
    return pl.pallas_call(
        body, name=name, grid=(lp // ts,),
        in_specs=[blk, blk, par, par], out_specs=[blk, blk],
        out_shape=(sd(xr.shape, F32), sd(xr.shape, F32)),
        scratch_shapes=[pltpu.VMEM((r, 128), F32), pltpu.VMEM((r, 128), F32)],
        compiler_params=_params(),
    )(xr, xi, ar, ai)


def _s5_mix_fwd(sr_ref, si_ref, u_ref, d_ref, cre, cim, c):
    y = _dot(sr_ref[...], cre[c]) + _dot(si_ref[...], cim[c]) + d_ref[c] * u_ref[...]
    gy, t = _gelu(y)
    return y, gy, t


def s5_fwd3(sr, si, u, z, h, cdre, cdim, w_glu, w_out, d_skip, b_glu, name):
    lp, d = h.shape
    tt = TOKEN_TILE
    sw, cw = cdre.shape[1], cdre.shape[2]

    def body(sr_ref, si_ref, u_ref, z_ref, h_ref, d_ref, bg_ref, cre_hbm, cim_hbm, wg_hbm, wo_hbm,
             o_ref, cre, cim, wg, wo, gy_sc, q_sc):
        i, c = pl.program_id(0), pl.program_id(1)

        @pl.when((i == 0) & (c == 0))
        def _():
            pltpu.sync_copy(cre_hbm, cre)
            pltpu.sync_copy(cim_hbm, cim)
            pltpu.sync_copy(wg_hbm, wg)
            pltpu.sync_copy(wo_hbm, wo)

        _, gy, _ = _s5_mix_fwd(sr_ref, si_ref, u_ref, d_ref, cre, cim, c)
        gy_sc[c] = gy
        part = _dot(gy, wg[c])

        @pl.when(c == 0)
        def _():
            q_sc[...] = part

        @pl.when(c > 0)
        def _():
            q_sc[...] += part

        @pl.when(c == 3)
        def _():
            sig = _sigmoid(q_sc[...] + bg_ref[...])
            zz = z_ref[...]
            sz = zz * _sigmoid(zz)
            o = h_ref[...]
            for k in range(4):
                cols = slice(k * cw, (k + 1) * cw)
                o = o + _dot(gy_sc[k] * sig[:, cols] * sz[:, cols], wo[k])
            o_ref[...] = o

    row = lambda i, c: (i, 0)
    chunk = lambda i, c: (i, c)
    return pl.pallas_call(
        body, name=name, grid=(lp // tt, 4),
        in_specs=[pl.BlockSpec((tt, sw), chunk), pl.BlockSpec((tt, sw), chunk), pl.BlockSpec((tt, cw), chunk),
                  pl.BlockSpec((tt, d), row), pl.BlockSpec((tt, d), row),
                  pl.BlockSpec((4, 1, cw), lambda i, c: (0, 0, 0)), pl.BlockSpec((1, d), lambda i, c: (0, 0)),
                  ANY, ANY, ANY, ANY],
        out_specs=pl.BlockSpec((tt, d), row),
        out_shape=jax.ShapeDtypeStruct((lp, d), F32),
        scratch_shapes=[pltpu.VMEM(cdre.shape, BF16), pltpu.VMEM(cdim.shape, BF16), pltpu.VMEM(w_glu.shape, BF16),
                        pltpu.VMEM(w_out.shape, BF16), pltpu.VMEM((4, tt, cw), F32), pltpu.VMEM((tt, d), F32)],
        compiler_params=_params(ndim=2),
    )(sr, si, u, z, h, d_skip, b_glu, cdre, cdim, w_glu, w_out)


def s5_bwd3a(dh, sr, si, u, z, cdre, cdim, w_glu, w_out, d_skip, b_glu, name):
    lp, d = dh.shape
    tt = TOKEN_TILE
    nt = lp // tt
    sw, cw = cdre.shape[1], cdre.shape[2]

    def body(dh_ref, sr_ref, si_ref, u_ref, z_ref, d_ref, bg_ref, cre_hbm, cim_hbm, wg_hbm, wo_hbm,
             dy_ref, dp_ref, dwo_hbm, dwg_hbm, dbg_hbm,
             cre, cim, wg, wo, y_sc, t_sc, gy_sc, q_sc, dq_sc, dgy_sc, dwo, dwg, dbg):
        i, c = pl.program_id(0), pl.program_id(1)

        @pl.when((i == 0) & (c == 0))
        def _():
            pltpu.sync_copy(cre_hbm, cre)
            pltpu.sync_copy(cim_hbm, cim)
            pltpu.sync_copy(wg_hbm, wg)
            pltpu.sync_copy(wo_hbm, wo)
            dwo[...] = jnp.zeros_like(dwo)
            dwg[...] = jnp.zeros_like(dwg)
            dbg[...] = jnp.zeros_like(dbg)

        y, gy, t = _s5_mix_fwd(sr_ref, si_ref, u_ref, d_ref, cre, cim, c)
        y_sc[c] = y
        t_sc[c] = t
        gy_sc[c] = gy
        part = _dot(gy, wg[c])

        @pl.when(c == 0)
        def _():
            q_sc[...] = part

        @pl.when(c > 0)
        def _():
            q_sc[...] += part

        @pl.when(c == 3)
        def _():
            sig = _sigmoid(q_sc[...] + bg_ref[...])
            sz, dsz = _silu_and_grad(z_ref[...])
            dhv = dh_ref[...]
            for k in range(4):
                cols = slice(k * cw, (k + 1) * cw)
                gy_k, sig_k, sz_k = gy_sc[k], sig[:, cols], sz[:, cols]
                y2 = gy_k * sig_k
                dy3 = _dot_nt(dhv, wo[k])
                dwo[k] += _dot_tn(y2 * sz_k, dhv)
                dy2 = dy3 * sz_k
                dp_ref[0, :, cols] = (dy3 * y2 * dsz[:, cols]).astype(BF16)
                dq_sc[:, cols] = dy2 * gy_k * sig_k * (1.0 - sig_k)
                dgy_sc[k] = dy2 * sig_k
            dq = dq_sc[...]
            dbg[...] += jnp.sum(dq, axis=0, keepdims=True)
            for k in range(4):
                cols = slice(k * cw, (k + 1) * cw)
                dwg[k] += _dot_tn(gy_sc[k], dq)
                dgy = dgy_sc[k] + _dot_nt(dq, wg[k])
                dy_ref[:, cols] = dgy * _gelu_grad(y_sc[k], t_sc[k])

        @pl.when((i == nt - 1) & (c == 3))
        def _():
            pltpu.sync_copy(dwo, dwo_hbm)
            pltpu.sync_copy(dwg, dwg_hbm)
            pltpu.sync_copy(dbg, dbg_hbm)

    row = lambda i, c: (i, 0)
    chunk = lambda i, c: (i, c)
    sd = jax.ShapeDtypeStruct
    acc = pltpu.VMEM((4, tt, cw), F32)
    return pl.pallas_call(
        body, name=name, grid=(nt, 4),
        in_specs=[pl.BlockSpec((tt, d), row), pl.BlockSpec((tt, sw), chunk), pl.BlockSpec((tt, sw), chunk),
                  pl.BlockSpec((tt, cw), chunk), pl.BlockSpec((tt, d), row),
                  pl.BlockSpec((4, 1, cw), lambda i, c: (0, 0, 0)), pl.BlockSpec((1, d), lambda i, c: (0, 0)),
                  ANY, ANY, ANY, ANY],
        out_specs=[pl.BlockSpec((tt, d), row), pl.BlockSpec((1, tt, d), lambda i, c: (1, i, 0)), ANY, ANY, ANY],
        out_shape=(sd((lp, d), F32), sd((2, lp, d), BF16), sd(w_out.shape, F32), sd(w_glu.shape, F32), sd((1, d), F32)),
        scratch_shapes=[pltpu.VMEM(cdre.shape, BF16), pltpu.VMEM(cdim.shape, BF16), pltpu.VMEM(w_glu.shape, BF16),
                        pltpu.VMEM(w_out.shape, BF16), acc, acc, acc, pltpu.VMEM((tt, d), F32), pltpu.VMEM((tt, d), F32), acc,
                        pltpu.VMEM(w_out.shape, F32), pltpu.VMEM(w_glu.shape, F32), pltpu.VMEM((1, d), F32)],
        compiler_params=_params(ndim=2),
    )(dh, sr, si, u, z, d_skip, b_glu, cdre, cdim, w_glu, w_out)


def s5_bwd3b(dy, sr, si, u, cdre, cdim, d_skip, name):
    lp, d = dy.shape
    tt = TOKEN_TILE
    nt = lp // tt
    sw, cw = cdre.shape[1], cdre.shape[2]

    def body(dy_ref, sr_ref, si_ref, u_ref, d_ref, cre_hbm, cim_hbm,
             dsr_ref, dsi_ref, dus_ref, dcre_hbm, dcim_hbm, dd_hbm, cre, cim, dcre, dcim, dd):
        i, c = pl.program_id(0), pl.program_id(1)

        @pl.when((i == 0) & (c == 0))
        def _():
            pltpu.sync_copy(cre_hbm, cre)
            pltpu.sync_copy(cim_hbm, cim)
            dcre[...] = jnp.zeros_like(dcre)
            dcim[...] = jnp.zeros_like(dcim)
            dd[...] = jnp.zeros_like(dd)

        dyv = dy_ref[...]
        dd[c] += jnp.sum(dyv * u_ref[...], axis=0, keepdims=True)
        dus_ref[...] = dyv * d_ref[c]
        dsr_ref[...] = _dot_nt(dyv, cre[c])
        dsi_ref[...] = _dot_nt(dyv, cim[c])
        dcre[c] += _dot_tn(sr_ref[...], dyv)
        dcim[c] += _dot_tn(si_ref[...], dyv)

        @pl.when((i == nt - 1) & (c == 3))
        def _():
            pltpu.sync_copy(dcre, dcre_hbm)
            pltpu.sync_copy(dcim, dcim_hbm)
            pltpu.sync_copy(dd, dd_hbm)

    chunk = lambda i, c: (i, c)
    sd = jax.ShapeDtypeStruct
    return pl.pallas_call(
        body, name=name, grid=(nt, 4),
        in_specs=[pl.BlockSpec((tt, cw), chunk), pl.BlockSpec((tt, sw), chunk), pl.BlockSpec((tt, sw), chunk),
                  pl.BlockSpec((tt, cw), chunk), pl.BlockSpec((4, 1, cw), lambda i, c: (0, 0, 0)), ANY, ANY],
        out_specs=[pl.BlockSpec((tt, sw), chunk), pl.BlockSpec((tt, sw), chunk), pl.BlockSpec((tt, cw), chunk), ANY, ANY, ANY],
        out_shape=(sd((lp, 4 * sw), F32), sd((lp, 4 * sw), F32), sd((lp, d), F32),
                   sd(cdre.shape, F32), sd(cdim.shape, F32), sd((4, 1, cw), F32)),
        scratch_shapes=[pltpu.VMEM(cdre.shape, BF16), pltpu.VMEM(cdim.shape, BF16),
                        pltpu.VMEM(cdre.shape, F32), pltpu.VMEM(cdim.shape, F32), pltpu.VMEM((4, 1, cw), F32)],
        compiler_params=_params(ndim=2),
    )(dy, sr, si, u, d_skip, cdre, cdim)


def s5_scan_bwd(gr, gi, sr, si, ar, ai, name):
    r = ar.shape[0]
    ts = SCAN_TILE
    lp = gr.shape[0] // r
    nt = lp // ts

    def body(gr_ref, gi_ref, sr_ref, si_ref, pr_ref, pi_ref, ar_ref, ai_ref,
             lr_ref, li_ref, dar_ref, dai_ref, st_r, st_i, acc_r, acc_i):
        i = pl.program_id(0)

        @pl.when(i == 0)
        def _():
            for ref in (st_r, st_i, acc_r, acc_i):
                ref[...] = jnp.zeros_like(ref)

        a_r, a_i = ar_ref[...], ai_ref[...]

        def adjoint(t, l_r, l_i):
            rows = pl.ds(pl.multiple_of(t * r, r), r)
            n_r = gr_ref[rows, :] + a_r * l_r + a_i * l_i
            n_i = gi_ref[rows, :] + a_r * l_i - a_i * l_r
            lr_ref[rows, :] = n_r
            li_ref[rows, :] = n_i
            return n_r, n_i

        def step(k, carry):
            l_r, l_i, d_r, d_i = carry
            t = ts - 1 - k
            l_r, l_i = adjoint(t, l_r, l_i)
            prev = pl.ds(pl.multiple_of((t - 1) * r, r), r)
            p_r, p_i = sr_ref[prev, :], si_ref[prev, :]
            return l_r, l_i, d_r + l_r * p_r + l_i * p_i, d_i + l_i * p_r - l_r * p_i

        l_r, l_i, d_r, d_i = lax.fori_loop(0, ts - 1, step, (st_r[...], st_i[...], acc_r[...], acc_i[...]), unroll=4)
        l_r, l_i = adjoint(0, l_r, l_i)
        first = (i < nt - 1).astype(F32)
        p_r, p_i = pr_ref[...] * first, pi_ref[...] * first
        d_r = d_r + l_r * p_r + l_i * p_i
        d_i = d_i + l_i * p_r - l_r * p_i
        st_r[...] = l_r
        st_i[...] = l_i
        acc_r[...] = d_r
        acc_i[...] = d_i
        dar_ref[...] = d_r
        dai_ref[...] = d_i

    blk = pl.BlockSpec((ts * r, 128), lambda i: (nt - 1 - i, 0))
    prev = pl.BlockSpec((r, 128), lambda i: (jnp.maximum((nt - 1 - i) * ts - 1, 0), 0))
    par = pl.BlockSpec((r, 128), lambda i: (0, 0))
    sd = jax.ShapeDtypeStruct
    return pl.pallas_call(
        body, name=name, grid=(nt,),
        in_specs=[blk, blk, blk, blk, prev, prev, par, par], out_specs=[blk, blk, par, par],
        out_shape=(sd(gr.shape, F32), sd(gr.shape, F32), sd((r, 128), F32), sd((r, 128), F32)),
        scratch_shapes=[pltpu.VMEM((r, 128), F32)] * 4,
        compiler_params=_params(),
    )(gr, gi, sr, si, sr, si, ar, ai)


def s5_bwd1(lam_r, lam_i, dus, u, dp, h, dh, g, w_in, bdre, bdim, name):
    lp, d = h.shape
    tt = TOKEN_TILE
    nt = lp // tt
    cw, sw = bdre.shape[1], bdre.shape[2]

    def body(lr_ref, li_ref, dus_ref, u_ref, dpz_ref, h_ref, dh_ref, g_ref, w_hbm, bre_hbm, bim_hbm,
             dpu_ref, dho_ref, n_ref, dbre_hbm, dbim_hbm, dg_hbm, w, bre, bim, dn_sc, dbre, dbim, dg):
        i, c = pl.program_id(0), pl.program_id(1)

        @pl.when((i == 0) & (c == 0))
        def _():
            pltpu.sync_copy(w_hbm, w)
            pltpu.sync_copy(bre_hbm, bre)
            pltpu.sync_copy(bim_hbm, bim)
            dbre[...] = jnp.zeros_like(dbre)
            dbim[...] = jnp.zeros_like(dbim)
            dg[...] = jnp.zeros_like(dg)

        l_r, l_i, uv = lr_ref[...], li_ref[...], u_ref[...]
        du = dus_ref[...] + _dot_nt(l_r, bre[c]) + _dot_nt(l_i, bim[c])
        dbre[c] += _dot_tn(uv, l_r)
        dbim[c] += _dot_tn(uv, l_i)
        dpu_ref[0] = du.astype(BF16)
        part = _dot_nt(du, w[c])

        @pl.when(c == 0)
        def _():
            dn_sc[...] = part

        @pl.when(c > 0)
        def _():
            dn_sc[...] += part

        @pl.when(c == 3)
        def _():
            dz = dpz_ref[0]
            dn = dn_sc[...]
            for k in range(4):
                dn = dn + _dot_nt(dz[:, k * cw:(k + 1) * cw], w[4 + k])
            gv = g_ref[...]
            n, hh, rr = _rms_fwd(h_ref[...], gv)
            n_ref[...] = n.astype(BF16)
            dg[...] += jnp.sum(dn * hh, axis=0, keepdims=True)
            dho_ref[...] = dh_ref[...] + _rms_bwd(dn, hh, rr, gv)

        @pl.when((i == nt - 1) & (c == 3))
        def _():
            pltpu.sync_copy(dbre, dbre_hbm)
            pltpu.sync_copy(dbim, dbim_hbm)
            pltpu.sync_copy(dg, dg_hbm)

    row = lambda i, c: (i, 0)
    chunk = lambda i, c: (i, c)
    sd = jax.ShapeDtypeStruct
    return pl.pallas_call(
        body, name=name, grid=(nt, 4),
        in_specs=[pl.BlockSpec((tt, sw), chunk), pl.BlockSpec((tt, sw), chunk), pl.BlockSpec((tt, cw), chunk),
                  pl.BlockSpec((tt, cw), chunk), pl.BlockSpec((1, tt, d), lambda i, c: (1, i, 0)),
                  pl.BlockSpec((tt, d), row), pl.BlockSpec((tt, d), row), pl.BlockSpec((1, d), lambda i, c: (0, 0)),
                  ANY, ANY, ANY],
        out_specs=[pl.BlockSpec((1, tt, cw), lambda i, c: (0, i, c)), pl.BlockSpec((tt, d), row), pl.BlockSpec((tt, d), row),
                   ANY, ANY, ANY],
        out_shape=(sd(dp.shape, BF16), sd((lp, d), F32), sd((lp, d), BF16),
                   sd(bdre.shape, F32), sd(bdim.shape, F32), sd((1, d), F32)),
        input_output_aliases={4: 0},
        scratch_shapes=[pltpu.VMEM(w_in.shape, BF16), pltpu.VMEM(bdre.shape, BF16), pltpu.VMEM(bdim.shape, BF16),
                        pltpu.VMEM((tt, d), F32), pltpu.VMEM(bdre.shape, F32), pltpu.VMEM(bdim.shape, F32), pltpu.VMEM((1, d), F32)],
        compiler_params=_params(ndim=2),
    )(lam_r, lam_i, dus, u, dp, h, dh, g, w_in, bdre, bdim)


def grad_w_in(n, dp, blk, name):
    lp, d = n.shape
    npart, _, width = dp.shape
    tt = TOKEN_TILE
    per = width // blk

    def body(n_ref, dp_ref, o_ref):
        part = _dot_tn(n_ref[...], dp_ref[0])

        @pl.when(pl.program_id(1) == 0)
        def _():
            o_ref[0] = part

        @pl.when(pl.program_id(1) > 0)
        def _():
            o_ref[0] += part

    return pl.pallas_call(
        body, name=name, grid=(npart * per, lp // tt),
        in_specs=[pl.BlockSpec((tt, d), lambda j, i: (i, 0)), pl.BlockSpec((1, tt, blk), lambda j, i: (j // per, i, j % per))],
        out_specs=pl.BlockSpec((1, d, blk), lambda j, i: (j, 0, 0)),
        out_shape=jax.ShapeDtypeStruct((npart * per, d, blk), F32),
        compiler_params=_params(ndim=2),
    )(n, dp)


def _conv_fwd_chunk(n, w, cw_ref, cb_ref, halo, c, nch):
    bg = jnp.dot(n, w[c], preferred_element_type=F32)
    cg = jnp.dot(n, w[nch + c], preferred_element_type=F32)
    v = jnp.dot(n, w[2 * nch + c], preferred_element_type=F32)
    z = jnp.dot(n, w[3 * nch + c], preferred_element_type=F32)
    hc = cg * v
    taps = cw_ref[c]
    conv = taps[2:3, :] * hc + taps[1:2, :] * _shift_down(hc, 1, halo) + taps[0:1, :] * _shift_down(hc, 2, halo) + cb_ref[c]
    return bg, cg, v, z, hc, conv


def conv_fwd(h, g, w_in, conv_w, conv_b, w_out, name):
    lp, d = h.shape
    tt = TOKEN_TILE
    nt = lp // tt
    nch, ce = w_out.shape[0], w_out.shape[1]

    def body(h_ref, g_ref, cw_ref, cb_ref, w_hbm, wo_hbm, o_ref, halo_ref, w, wo, halo):
        i = pl.program_id(0)

        @pl.when(i == 0)
        def _():
            pltpu.sync_copy(w_hbm, w)
            pltpu.sync_copy(wo_hbm, wo)
            halo[...] = jnp.zeros_like(halo)

        hv = h_ref[...]
        n = _rms_fwd(hv, g_ref[...])[0].astype(BF16)
        o = hv
        for c in range(nch):
            bg, _, _, z, hc, conv = _conv_fwd_chunk(n, w, cw_ref, cb_ref, halo[c], c, nch)
            o = o + _dot(bg * conv * (z * _sigmoid(z)), wo[c])
            halo[c] = hc[tt - CONV_HALO:, :]
            halo_ref[0, c] = hc[tt - CONV_HALO:, :]
        o_ref[...] = o

    sd = jax.ShapeDtypeStruct
    return pl.pallas_call(
        body, name=name, grid=(nt,),
        in_specs=[pl.BlockSpec((tt, d), lambda i: (i, 0)), pl.BlockSpec((1, d), lambda i: (0, 0)),
                  pl.BlockSpec(conv_w.shape, lambda i: (0, 0, 0)), pl.BlockSpec(conv_b.shape, lambda i: (0, 0, 0)), ANY, ANY],
        out_specs=[pl.BlockSpec((tt, d), lambda i: (i, 0)), pl.BlockSpec((1, nch, CONV_HALO, ce), lambda i: (i, 0, 0, 0))],
        out_shape=(sd((lp, d), F32), sd((nt, nch, CONV_HALO, ce), F32)),
        scratch_shapes=[pltpu.VMEM(w_in.shape, BF16), pltpu.VMEM(w_out.shape, BF16), pltpu.VMEM((nch, CONV_HALO, ce), F32)],
        compiler_params=_params(),
    )(h, g, conv_w, conv_b, w_in, w_out)


def conv_bwd(h, dh, halos, g, w_in, conv_w, conv_b, w_out, name):
    lp, d = h.shape
    tt = TOKEN_TILE
    nt = lp // tt
    nch, ce = w_out.shape[0], w_out.shape[1]

    def body(h_ref, dh_ref, halo_ref, g_ref, cw_ref, cb_ref, w_hbm, wo_hbm,
             dho_ref, n_ref, dp_ref, dwo_hbm, dcw_hbm, dcb_hbm, dg_hbm, w, wo, nxt, dwo, dcw, dcb, dg):
        i = pl.program_id(0)

        @pl.when(i == 0)
        def _():
            pltpu.sync_copy(w_hbm, w)
            pltpu.sync_copy(wo_hbm, wo)
            for ref in (nxt, dwo, dcw, dcb, dg):
                ref[...] = jnp.zeros_like(ref)

        gv = g_ref[...]
        nf, hh, rr = _rms_fwd(h_ref[...], gv)
        n = nf.astype(BF16)
        n_ref[...] = n
        dhv = dh_ref[...]
        has_prev = (i < nt - 1).astype(F32)
        dn = jnp.zeros((tt, d), F32)
        for c in range(nch):
            halo = halo_ref[0, c] * has_prev
            bg, cg, v, z, hc, conv = _conv_fwd_chunk(n, w, cw_ref, cb_ref, halo, c, nch)
            sz, dsz = _silu_and_grad(z)
            y1 = bg * conv
            dy2 = _dot_nt(dhv, wo[c])
            dwo[c] += _dot_tn(y1 * sz, dhv)
            dy1 = dy2 * sz
            dz = dy2 * y1 * dsz
            dbg = dy1 * conv
            dconv = dy1 * bg
            dcb[c] += jnp.sum(dconv, axis=0, keepdims=True)
            up1 = _shift_up(dconv, 1, nxt[c])
            up2 = _shift_up(dconv, 2, nxt[c])
            nxt[c] = dconv[:CONV_HALO, :]
            taps = cw_ref[c]
            dhc = taps[2:3, :] * dconv + taps[1:2, :] * up1 + taps[0:1, :] * up2
            dcw[c, 0:1, :] += jnp.sum(hc * up2, axis=0, keepdims=True)
            dcw[c, 1:2, :] += jnp.sum(hc * up1, axis=0, keepdims=True)
            dcw[c, 2:3, :] += jnp.sum(hc * dconv, axis=0, keepdims=True)
            dcg = dhc * v
            dv = dhc * cg
            cols = slice(c * ce, (c + 1) * ce)
            for p, val in enumerate((dbg, dcg, dv, dz)):
                dp_ref[p, :, cols] = val.astype(BF16)
                dn = dn + _dot_nt(val, w[p * nch + c])
        dg[...] += jnp.sum(dn * hh, axis=0, keepdims=True)
        dho_ref[...] = dhv + _rms_bwd(dn, hh, rr, gv)

        @pl.when(i == nt - 1)
        def _():
            pltpu.sync_copy(dwo, dwo_hbm)
            pltpu.sync_copy(dcw, dcw_hbm)
            pltpu.sync_copy(dcb, dcb_hbm)
            pltpu.sync_copy(dg, dg_hbm)

    rev = lambda i: (nt - 1 - i, 0)
    sd = jax.ShapeDtypeStruct
    return pl.pallas_call(
        body, name=name, grid=(nt,),
        in_specs=[pl.BlockSpec((tt, d), rev), pl.BlockSpec((tt, d), rev),
                  pl.BlockSpec((1, nch, CONV_HALO, ce), lambda i: (jnp.maximum(nt - 2 - i, 0), 0, 0, 0)),
                  pl.BlockSpec((1, d), lambda i: (0, 0)),
                  pl.BlockSpec(conv_w.shape, lambda i: (0, 0, 0)), pl.BlockSpec(conv_b.shape, lambda i: (0, 0, 0)), ANY, ANY],
        out_specs=[pl.BlockSpec((tt, d), rev), pl.BlockSpec((tt, d), rev),
                   pl.BlockSpec((4, tt, nch * ce), lambda i: (0, nt - 1 - i, 0)), ANY, ANY, ANY, ANY],
        out_shape=(sd((lp, d), F32), sd((lp, d), BF16), sd((4, lp, nch * ce), BF16),
                   sd(w_out.shape, F32), sd((nch, 8, ce), F32), sd((nch, 1, ce), F32), sd((1, d), F32)),
        scratch_shapes=[pltpu.VMEM(w_in.shape, BF16), pltpu.VMEM(w_out.shape, BF16), pltpu.VMEM((nch, CONV_HALO, ce), F32),
                        pltpu.VMEM(w_out.shape, F32), pltpu.VMEM((nch, 8, ce), F32), pltpu.VMEM((nch, 1, ce), F32),
                        pltpu.VMEM((1, d), F32)],
        compiler_params=_params(),
    )(h, dh, halos, g, conv_w, conv_b, w_in, w_out)


def _pool_fwd_group(n, w, wg, bg_ref, sc_ref, halo, k, tile, tt, first_pos):
    u = jnp.dot(n, w[k], preferred_element_type=F32)
    z = jnp.dot(n, w[4 + k], preferred_element_type=F32)
    ext = jnp.concatenate([halo, u], axis=0)
    win = _window_sums_back(ext)[k][POOL_HALO:, :]
    mixed = win * _pool_inv_count(tile, tt, first_pos, POOL_WINDOWS[k], u.shape[1]) - u
    outs = _dot(mixed, wg[k]) + bg_ref[k]
    return u, z, mixed, outs, outs * sc_ref[k]


def pool_fwd(h, g, w_in, w_grp, b_grp, scale, w_out, first_pos, name):
    lp, d = h.shape
    tt = TOKEN_TILE
    nt = lp // tt
    gw = w_grp.shape[1]

    def body(h_ref, g_ref, bg_ref, sc_ref, w_hbm, wg_hbm, wo_hbm, o_ref, halo_ref, w, wg, wo, halo):
        i = pl.program_id(0)

        @pl.when(i == 0)
        def _():
            pltpu.sync_copy(w_hbm, w)
            pltpu.sync_copy(wg_hbm, wg)
            pltpu.sync_copy(wo_hbm, wo)
            halo[...] = jnp.zeros_like(halo)

        hv = h_ref[...]
        n = _rms_fwd(hv, g_ref[...])[0].astype(BF16)
        o = hv
        for k in range(4):
            u, z, _, _, yp = _pool_fwd_group(n, w, wg, bg_ref, sc_ref, halo[k], k, i, tt, first_pos)
            o = o + _dot(yp * (z * _sigmoid(z)), wo[k])
            halo[k] = u[tt - POOL_HALO:, :]
            halo_ref[0, k] = u[tt - POOL_HALO:, :]
        o_ref[...] = o

    sd = jax.ShapeDtypeStruct
    small = pl.BlockSpec((4, 1, gw), lambda i: (0, 0, 0))
    return pl.pallas_call(
        body, name=name, grid=(nt,),
        in_specs=[pl.BlockSpec((tt, d), lambda i: (i, 0)), pl.BlockSpec((1, d), lambda i: (0, 0)), small, small, ANY, ANY, ANY],
        out_specs=[pl.BlockSpec((tt, d), lambda i: (i, 0)), pl.BlockSpec((1, 4, POOL_HALO, gw), lambda i: (i, 0, 0, 0))],
        out_shape=(sd((lp, d), F32), sd((nt, 4, POOL_HALO, gw), F32)),
        scratch_shapes=[pltpu.VMEM(w_in.shape, BF16), pltpu.VMEM(w_grp.shape, BF16), pltpu.VMEM(w_out.shape, BF16),
                        pltpu.VMEM((4, POOL_HALO, gw), F32)],
        compiler_params=_params(),
    )(h, g, b_grp, scale, w_in, w_grp, w_out)


def pool_bwd(h, dh, halos, g, w_in, w_grp, b_grp, scale, w_out, first_pos, name):
    lp, d = h.shape
    tt = TOKEN_TILE
    nt = lp // tt
    gw = w_grp.shape[1]

    def body(h_ref, dh_ref, halo_ref, g_ref, bg_ref, sc_ref, w_hbm, wg_hbm, wo_hbm,
             dho_ref, n_ref, dp_ref, dwo_hbm, dwg_hbm, dbg_hbm, dsc_hbm, dg_hbm,
             w, wg, wo, nxt, dwo, dwg, dbg, dsc, dg):
        i = pl.program_id(0)
        tile = nt - 1 - i

        @pl.when(i == 0)
        def _():
            pltpu.sync_copy(w_hbm, w)
            pltpu.sync_copy(wg_hbm, wg)
            pltpu.sync_copy(wo_hbm, wo)
            for ref in (nxt, dwo, dwg, dbg, dsc, dg):
                ref[...] = jnp.zeros_like(ref)

        gv = g_ref[...]
        nf, hh, rr = _rms_fwd(h_ref[...], gv)
        n = nf.astype(BF16)
        n_ref[...] = n
        dhv = dh_ref[...]
        has_prev = (i < nt - 1).astype(F32)
        dn = jnp.zeros((tt, d), F32)
        for k in range(4):
            u, z, mixed, outs, yp = _pool_fwd_group(n, w, wg, bg_ref, sc_ref, halo_ref[0, k] * has_prev, k, tile, tt, first_pos)
            sz, dsz = _silu_and_grad(z)
            dy = _dot_nt(dhv, wo[k])
            dwo[k] += _dot_tn(yp * sz, dhv)
            dyp = dy * sz
            dz = dy * yp * dsz
            dsc[k] += jnp.sum(dyp * outs, axis=0, keepdims=True)
            douts = dyp * sc_ref[k]
            dbg[k] += jnp.sum(douts, axis=0, keepdims=True)
            dwg[k] += _dot_tn(mixed, douts)
            dmixed = _dot_nt(douts, wg[k])
            dm = dmixed * _pool_inv_count(tile, tt, first_pos, POOL_WINDOWS[k], gw)
            ext = jnp.concatenate([dm, nxt[k]], axis=0)
            du = _window_sums_fwd(ext)[k][:tt, :] - dmixed
            nxt[k] = dm[:POOL_HALO, :]
            cols = slice(k * gw, (k + 1) * gw)
            dp_ref[0, :, cols] = du.astype(BF16)
            dp_ref[1, :, cols] = dz.astype(BF16)
            dn = dn + _dot_nt(du, w[k]) + _dot_nt(dz, w[4 + k])
        dg[...] += jnp.sum(dn * hh, axis=0, keepdims=True)
        dho_ref[...] = dhv + _rms_bwd(dn, hh, rr, gv)

        @pl.when(i == nt - 1)
        def _():
            pltpu.sync_copy(dwo, dwo_hbm)
            pltpu.sync_copy(dwg, dwg_hbm)
            pltpu.sync_copy(dbg, dbg_hbm)
            pltpu.sync_copy(dsc, dsc_hbm)
            pltpu.sync_copy(dg, dg_hbm)

    rev = lambda i: (nt - 1 - i, 0)
    sd = jax.ShapeDtypeStruct
    small = pl.BlockSpec((4, 1, gw), lambda i: (0, 0, 0))
    return pl.pallas_call(
        body, name=name, grid=(nt,),
        in_specs=[pl.BlockSpec((tt, d), rev), pl.BlockSpec((tt, d), rev),
                  pl.BlockSpec((1, 4, POOL_HALO, gw), lambda i: (jnp.maximum(nt - 2 - i, 0), 0, 0, 0)),
                  pl.BlockSpec((1, d), lambda i: (0, 0)), small, small, ANY, ANY, ANY],
        out_specs=[pl.BlockSpec((tt, d), rev), pl.BlockSpec((tt, d), rev),
                   pl.BlockSpec((2, tt, 4 * gw), lambda i: (0, nt - 1 - i, 0)), ANY, ANY, ANY, ANY, ANY],
        out_shape=(sd((lp, d), F32), sd((lp, d), BF16), sd((2, lp, 4 * gw), BF16),
                   sd(w_out.shape, F32), sd(w_grp.shape, F32), sd((4, 1, gw), F32), sd((4, 1, gw), F32), sd((1, d), F32)),
        scratch_shapes=[pltpu.VMEM(w_in.shape, BF16), pltpu.VMEM(w_grp.shape, BF16), pltpu.VMEM(w_out.shape, BF16),
                        pltpu.VMEM((4, POOL_HALO, gw), F32), pltpu.VMEM(w_out.shape, F32), pltpu.VMEM(w_grp.shape, F32),
                        pltpu.VMEM((4, 1, gw), F32), pltpu.VMEM((4, 1, gw), F32), pltpu.VMEM((1, d), F32)],
        compiler_params=_params(),
    )(h, dh, halos, g, b_grp, scale, w_in, w_grp, w_out)


def loss_head(h, target, g, pad_tiles, name):
    lp, d = h.shape
    tt = TOKEN_TILE
    nt = lp // tt

    def body(h_ref, t_ref, g_ref, dh_ref, dg_ref, loss_ref, acc):
        i = pl.program_id(0)

        @pl.when(i == 0)
        def _():
            acc[...] = jnp.zeros_like(acc)
            dg_ref[...] = jnp.zeros_like(dg_ref)

        @pl.when(i < pad_tiles)
        def _():
            dh_ref[...] = jnp.zeros_like(dh_ref)

        @pl.when(i >= pad_tiles)
        def _():
            gv = g_ref[...]
            n, hh, rr = _rms_fwd(h_ref[...], gv)
            err = n - t_ref[...]
            acc[...] += 0.5 * jnp.sum(jnp.mean(err * err, axis=-1, keepdims=True), axis=0, keepdims=True)
            dn = err * (1.0 / d)
            dg_ref[...] += jnp.sum(dn * hh, axis=0, keepdims=True)
            dh_ref[...] = _rms_bwd(dn, hh, rr, gv)

        loss_ref[...] = jnp.broadcast_to(acc[...], loss_ref.shape)

    sd = jax.ShapeDtypeStruct
    return pl.pallas_call(
        body, name=name, grid=(nt,),
        in_specs=[pl.BlockSpec((tt, d), lambda i: (i, 0)), pl.BlockSpec((tt, d), lambda i: (jnp.maximum(i - pad_tiles, 0), 0)),
                  pl.BlockSpec((1, d), lambda i: (0, 0))],
        out_specs=[pl.BlockSpec((tt, d), lambda i: (i, 0)), pl.BlockSpec((1, d), lambda i: (0, 0)),
                   pl.BlockSpec((8, 128), lambda i: (0, 0))],
        out_shape=(sd((lp, d), F32), sd((1, d), F32), sd((8, 128), F32)),
        scratch_shapes=[pltpu.VMEM((1, 1), F32)],
        compiler_params=_params(),
    )(h, target, g)


def exchange(arrs, gather, name):
    n = len(arrs)

    def body(*refs):
        ins, outs = refs[:n], refs[n:2 * n]
        send_sems, recv_sems, own_sems = refs[2 * n:]
        x, y, c = lax.axis_index("x"), lax.axis_index("y"), lax.axis_index("c")
        me = 4 * x + 2 * y + c
        own = []
        for a in range(n):
            cp = pltpu.make_async_copy(ins[a] if gather else ins[a].at[me], outs[a].at[me], own_sems.at[a])
            cp.start()
            own.append(cp)
        sent = []
        for k in range(1, N_DEV):
            px = 1 - x if k & 4 else x
            py = 1 - y if k & 2 else y
            pc = 1 - c if k & 1 else c
            peer = 4 * px + 2 * py + pc
            for a in range(n):
                cp = pltpu.make_async_remote_copy(
                    src_ref=ins[a] if gather else ins[a].at[peer], dst_ref=outs[a].at[me],
                    send_sem=send_sems.at[a, k - 1], recv_sem=recv_sems.at[a, k - 1],
                    device_id=(px, py, pc), device_id_type=pl.DeviceIdType.MESH)
                cp.start()
                sent.append((cp, a, k, peer, (px, py, pc)))
        for cp, a, k, peer, pid in sent:
            cp.wait_send()
            pltpu.make_async_remote_copy(
                src_ref=ins[a] if gather else ins[a].at[peer], dst_ref=outs[a].at[peer],
                send_sem=send_sems.at[a, k - 1], recv_sem=recv_sems.at[a, k - 1],
                device_id=pid, device_id_type=pl.DeviceIdType.MESH).wait_recv()
        for cp in own:
            cp.wait()

    hbm = pl.BlockSpec(memory_space=pltpu.HBM)
    out_shape = tuple(jax.ShapeDtypeStruct(((N_DEV,) + a.shape) if gather else a.shape, a.dtype) for a in arrs)
    return pl.pallas_call(
        body, name=name, in_specs=[hbm] * n, out_specs=[hbm] * n, out_shape=out_shape,
        scratch_shapes=[pltpu.SemaphoreType.DMA((n, N_DEV - 1)), pltpu.SemaphoreType.DMA((n, N_DEV - 1)),
                        pltpu.SemaphoreType.DMA((n,))],
    )(*[pltpu.with_memory_space_constraint(a, pltpu.HBM) for a in arrs])


def _adamw(w, g, m, v):
    m = ADAM_B1 * m + (1.0 - ADAM_B1) * g
    v = ADAM_B2 * v + (1.0 - ADAM_B2) * (g * g)
    m_hat = m / (1.0 - ADAM_B1 ** ADAM_STEP)
    v_hat = v / (1.0 - ADAM_B2 ** ADAM_STEP)
    return -ADAM_LR * (m_hat / (jnp.sqrt(v_hat) + ADAM_EPS) + ADAM_WD * w), m, v


def sum_adamw(parts, w, m, v, name):
    rows, cols = w.shape
    tr = rows
    if rows * cols > UPDATE_TILE_ELEMS:
        tr = max(t for t in range(8, UPDATE_TILE_ELEMS // cols + 1, 8) if rows % t == 0)

    def body(p_ref, w_ref, m_ref, v_ref, g_ref, d_ref, nm_ref, nv_ref):
        g = p_ref[0]
        for j in range(1, N_DEV):
            g = g + p_ref[j]
        delta, nm, nv = _adamw(w_ref[...], g, m_ref[...], v_ref[...])
        g_ref[...] = g
        d_ref[...] = delta
        nm_ref[...] = nm
        nv_ref[...] = nv

    blk = pl.BlockSpec((tr, cols), lambda i: (i, 0))
    sd = jax.ShapeDtypeStruct((rows, cols), F32)
    return pl.pallas_call(
        body, name=name, grid=(rows // tr,),
        in_specs=[pl.BlockSpec((N_DEV, tr, cols), lambda i: (0, i, 0)), blk, blk, blk],
        out_specs=[blk] * 4, out_shape=(sd,) * 4,
        compiler_params=_params(),
    )(parts, w, m, v)


S5_NAMES = ("w_in", "lam_re", "lam_im", "log_dt", "b_re", "b_im", "c_re", "c_im", "d_skip", "w_glu", "b_glu", "w_out")
CONV_NAMES = ("w_in", "conv_w", "conv_b", "w_out")
POOL_NAMES = ("w_in", "w_grp", "b_grp", "scale", "w_out")
LAYER_KINDS = ("s5", "conv", "pool", "s5")
LAYER_NAMES = {"s5": S5_NAMES, "conv": CONV_NAMES, "pool": POOL_NAMES}
SHARDED = {"s5": ("w_in", "w_glu", "w_out"), "conv": ("w_in", "conv_w", "w_out"), "pool": ("w_in", "w_grp", "b_grp", "w_out")}
GATHER_F32 = ("conv_w", "b_grp")


def weight_names():
    names = ["meta_tokens"]
    for i, kind in enumerate(LAYER_KINDS):
        names.append("norm%d_g" % i)
        names += ["l%d_%s" % (i, n) for n in LAYER_NAMES[kind]]
    names.append("final_g")
    return names


def sharded_names():
    return ["meta_tokens"] + ["l%d_%s" % (i, n) for i, kind in enumerate(LAYER_KINDS) for n in SHARDED[kind]]


def _block_diag_in(bb_t, gc):
    i, g, p = bb_t.shape
    t = bb_t.reshape(i, 4, gc, p)
    return jnp.einsum("icjp,jk->cjikp", t, jnp.eye(gc, dtype=F32)).reshape(4, gc * i, gc * p)


def _block_diag_in_grad(dbd, gc):
    i, p = dbd.shape[1] // gc, dbd.shape[2] // gc
    return jnp.einsum("cjijp->icjp", dbd.reshape(4, gc, i, gc, p)).reshape(i, 4 * gc, p)


def _block_diag_out(cc, gc):
    g, i, p = cc.shape
    return jnp.einsum("cjip,jk->cjpki", cc.reshape(4, gc, i, p), jnp.eye(gc, dtype=F32)).reshape(4, gc * p, gc * i)


def _block_diag_out_grad(dcd, gc):
    p, i = dcd.shape[1] // gc, dcd.shape[2] // gc
    return jnp.einsum("cjpji->cjip", dcd.reshape(4, gc, p, gc, i)).reshape(4 * gc, i, p)


def _to_owner_blocks(a, axis):
    shape = a.shape[:axis] + (N_DEV, a.shape[axis] // N_DEV) + a.shape[axis + 1:]
    return jnp.moveaxis(a.reshape(shape), axis, 0)


def _from_owner_blocks(a, axis):
    a = jnp.moveaxis(a, 0, axis)
    return a.reshape(a.shape[:axis] + (a.shape[axis] * a.shape[axis + 1],) + a.shape[axis + 2:])


def _step(x, target, weights, moments_m, moments_v):
    seq, d = x.shape[1], x.shape[2]
    n_meta = weights["meta_tokens"].shape[0]
    tt = TOKEN_TILE
    pad_tiles = -(-n_meta // tt)
    p0 = pad_tiles * tt
    lp = p0 + seq
    first_pos = p0 - n_meta
    gc = d // 4 // S5_GROUP
    cw = d // 4

    big_names = [n for n in sharded_names() if n != "meta_tokens" and n.split("_", 1)[1] not in GATHER_F32]
    small_names = [n for n in sharded_names() if n not in big_names]
    gathered = dict(zip(big_names, exchange([weights[n].astype(BF16) for n in big_names], True, "gather_weights")))
    gathered.update(zip(small_names, exchange([weights[n] for n in small_names], True, "gather_small")))

    meta = _from_owner_blocks(gathered["meta_tokens"], 1)
    h = jnp.concatenate([jnp.zeros((first_pos, d), F32), meta, x[0]], axis=0)

    def vec(name):
        return weights[name].reshape(1, -1)

    full = {}
    for i, kind in enumerate(LAYER_KINDS):
        p = "l%d_" % i
        w_in = gathered[p + "w_in"]
        if kind == "s5":
            lr, li = weights[p + "lam_re"], weights[p + "lam_im"]
            ldt = weights[p + "log_dt"].reshape(-1, 1)
            br_t = jnp.transpose(weights[p + "b_re"], (2, 0, 1))
            bi_t = jnp.transpose(weights[p + "b_im"], (2, 0, 1))
            ar, ai, bbr, bbi = s5_disc_fwd(lr, li, ldt, br_t, bi_t, p + "disc_fwd")
            full[i] = dict(
                w_in=w_in, disc=(lr, li, ldt, br_t, bi_t),
                ar=ar.reshape(-1, 128), ai=ai.reshape(-1, 128),
                bdre=_block_diag_in(bbr, gc).astype(BF16), bdim=_block_diag_in(bbi, gc).astype(BF16),
                cdre=_block_diag_out(weights[p + "c_re"], gc).astype(BF16),
                cdim=_block_diag_out(-weights[p + "c_im"], gc).astype(BF16),
                w_glu=gathered[p + "w_glu"].reshape(4, cw, d), w_out=gathered[p + "w_out"].reshape(4, cw, d),
                d_skip=weights[p + "d_skip"].reshape(4, 1, cw), b_glu=vec(p + "b_glu"))
        elif kind == "conv":
            ce = w_in.shape[2]
            nch = 2
            conv_w = _from_owner_blocks(gathered[p + "conv_w"], 1)
            full[i] = dict(
                w_in=w_in, conv_w=jnp.transpose(conv_w.reshape(CONV_K, nch, ce), (1, 0, 2)),
                conv_b=weights[p + "conv_b"].reshape(nch, 1, ce), w_out=gathered[p + "w_out"].reshape(nch, ce, d))
        else:
            gw = w_in.shape[2]
            full[i] = dict(
                w_in=w_in, w_grp=_from_owner_blocks(gathered[p + "w_grp"], 1),
                b_grp=_from_owner_blocks(gathered[p + "b_grp"], 1).reshape(4, 1, gw),
                scale=weights[p + "scale"].reshape(4, 1, gw), w_out=gathered[p + "w_out"].reshape(4, gw, d))

    saved = {}
    for i, kind in enumerate(LAYER_KINDS):
        p, f, g = "l%d_" % i, full[i], vec("norm%d_g" % i)
        if kind == "s5":
            u, z, xr, xi = s5_fwd1(h, g, f["w_in"], f["bdre"], f["bdim"], p + "fwd_in")
            sr, si = s5_scan_fwd(xr.reshape(-1, 128), xi.reshape(-1, 128), f["ar"], f["ai"], p + "scan_fwd")
            sr, si = sr.reshape(lp, -1), si.reshape(lp, -1)
            saved[i] = (h, u, z, sr, si)
            h = s5_fwd3(sr, si, u, z, h, f["cdre"], f["cdim"], f["w_glu"], f["w_out"], f["d_skip"], f["b_glu"], p + "fwd_out")
        elif kind == "conv":
            h_new, halos = conv_fwd(h, g, f["w_in"], f["conv_w"], f["conv_b"], f["w_out"], p + "fwd")
            saved[i] = (h, halos)
            h = h_new
        else:
            h_new, halos = pool_fwd(h, g, f["w_in"], f["w_grp"], f["b_grp"], f["scale"], f["w_out"], first_pos, p + "fwd")
            saved[i] = (h, halos)
            h = h_new

    dh, dg_final, loss_tile = loss_head(h, target[0], vec("final_g"), pad_tiles, "loss_head")
    loss = lax.psum(loss_tile[0, 0], ("x", "y", "c"))

    grads = {"final_g": dg_final}
    for i in reversed(range(len(LAYER_KINDS))):
        kind = LAYER_KINDS[i]
        p, f, g = "l%d_" % i, full[i], vec("norm%d_g" % i)
        if kind == "s5":
            h_in, u, z, sr, si = saved[i]
            dy, dp, dwo, dwg, dbg = s5_bwd3a(dh, sr, si, u, z, f["cdre"], f["cdim"], f["w_glu"], f["w_out"],
                                             f["d_skip"], f["b_glu"], p + "bwd_out")
            dsr, dsi, dus, dcre, dcim, dd = s5_bwd3b(dy, sr, si, u, f["cdre"], f["cdim"], f["d_skip"], p + "bwd_read")
            lam_r, lam_i, dar, dai = s5_scan_bwd(dsr.reshape(-1, 128), dsi.reshape(-1, 128), sr.reshape(-1, 128),
                                                 si.reshape(-1, 128), f["ar"], f["ai"], p + "scan_bwd")
            dp, dh, n, dbre, dbim, dg = s5_bwd1(lam_r.reshape(lp, -1), lam_i.reshape(lp, -1), dus, u, dp, h_in, dh, g,
                                                f["w_in"], f["bdre"], f["bdim"], p + "bwd_in")
            dw_in = grad_w_in(n, dp, f["w_in"].shape[2], p + "grad_w_in")
            lr, li, ldt, br_t, bi_t = f["disc"]
            dlr, dli, dldt, dbr_t, dbi_t = s5_disc_bwd(
                lr, li, ldt, br_t, bi_t, dar.reshape(lr.shape), dai.reshape(lr.shape),
                _block_diag_in_grad(dbre, gc), _block_diag_in_grad(dbim, gc), p + "disc_bwd")
            grads.update({
                p + "w_in": dw_in, p + "lam_re": dlr, p + "lam_im": dli, p + "log_dt": dldt,
                p + "b_re": jnp.transpose(dbr_t, (1, 2, 0)), p + "b_im": jnp.transpose(dbi_t, (1, 2, 0)),
                p + "c_re": _block_diag_out_grad(dcre, gc), p + "c_im": -_block_diag_out_grad(dcim, gc),
                p + "d_skip": dd, p + "w_glu": dwg.reshape(N_DEV, -1, d), p + "b_glu": dbg,
                p + "w_out": dwo.reshape(N_DEV, -1, d)})
        elif kind == "conv":
            h_in, halos = saved[i]
            dh, n, dp, dwo, dcw, dcb, dg = conv_bwd(h_in, dh, halos, g, f["w_in"], f["conv_w"], f["conv_b"], f["w_out"], p + "bwd")
            dw_in = grad_w_in(n, dp, f["w_in"].shape[2], p + "grad_w_in")
            dconv_w = jnp.transpose(dcw[:, :CONV_K, :], (1, 0, 2)).reshape(CONV_K, -1)
            grads.update({p + "w_in": dw_in, p + "conv_w": _to_owner_blocks(dconv_w, 1), p + "conv_b": dcb,
                          p + "w_out": dwo.reshape(N_DEV, -1, d)})
        else:
            h_in, halos = saved[i]
            dh, n, dp, dwo, dwgrp, dbgrp, dsc, dg = pool_bwd(h_in, dh, halos, g, f["w_in"], f["w_grp"], f["b_grp"], f["scale"],
                                                             f["w_out"], first_pos, p + "bwd")
            dw_in = grad_w_in(n, dp, f["w_in"].shape[2], p + "grad_w_in")
            grads.update({p + "w_in": dw_in, p + "w_grp": _to_owner_blocks(dwgrp, 1),
                          p + "b_grp": _to_owner_blocks(dbgrp.reshape(4, -1), 1), p + "scale": dsc,
                          p + "w_out": dwo.reshape(N_DEV, -1, d)})
        grads["norm%d_g" % i] = dg
    grad_x = dh[p0:][None]
    grads["meta_tokens"] = _to_owner_blocks(dh[first_pos:p0], 1)

    names = weight_names()
    sh_names = sharded_names()
    rep_names = [n for n in names if n not in sh_names]

    def as2d(a):
        return a.reshape(-1, a.shape[-1])

    sent = [grads[n].reshape(N_DEV, -1, grads[n].shape[-1]) for n in sh_names]
    received = exchange(sent, False, "scatter_grads")
    out = {}
    for n, parts in zip(sh_names, received):
        res = sum_adamw(parts, as2d(weights[n]), as2d(moments_m[n]), as2d(moments_v[n]), "update_" + n)
        out[n] = [r.reshape(weights[n].shape) for r in res]

    def pack(tree):
        flat = [jnp.pad(tree[n].reshape(-1), (0, -tree[n].size % 1024)) for n in rep_names]
        flat = jnp.concatenate(flat)
        return jnp.pad(flat, (0, -flat.size % (PACK_ROWS * 128))).reshape(-1, 128)

    g_parts = exchange([pack(grads)], True, "gather_small_grads")[0]
    packed = sum_adamw(g_parts, pack(weights), pack(moments_m), pack(moments_v), "update_replicated")
    offset = 0
    for n in rep_names:
        size = weights[n].size
        out[n] = [r.reshape(-1)[offset:offset + size].reshape(weights[n].shape) for r in packed]
        offset += size + (-size % 1024)

    return (loss, grad_x) + tuple(out[n][k] for k in range(4) for n in names)


def kernel(x, *rest):
    names = weight_names()
    nw = len(names)
    weights = dict(zip(names, rest[:nw]))
    target = rest[nw]
    moments_m = dict(zip(names, rest[nw + 1:2 * nw + 1]))
    moments_v = dict(zip(names, rest[2 * nw + 1:3 * nw + 1]))
    return _step(x, target, weights, moments_m, moments_v)
```

```python
import functools
import math

import jax
import jax.numpy as jnp
from jax import lax
from jax.experimental import pallas as pl
from jax.experimental.pallas import tpu as pltpu

F32 = jnp.float32
BF16 = jnp.bfloat16
EPS = 1e-6
N_DEV = 8
TOKEN_TILE = 256
SCAN_CHUNKS = 2
S5_GROUP = 16
S5_STATE = 64
POOL_WINDOWS = (2, 4, 8, 16)
POOL_HALO = 16
CONV_K = 3
CONV_HALO = 8
ADAM_LR = 0.001
ADAM_B1 = 0.9
ADAM_B2 = 0.999
ADAM_EPS = 1e-08
ADAM_WD = 0.01
ADAM_STEP = 10
GELU_C = math.sqrt(2.0 / math.pi)
GELU_A = 0.044715
UPDATE_TILE_ELEMS = 1 << 17
PACK_ROWS = 512
VMEM_LIMIT = 56 << 20

ANY = pl.BlockSpec(memory_space=pl.ANY)


def _params(vmem=VMEM_LIMIT, ndim=1):
    return pltpu.CompilerParams(vmem_limit_bytes=vmem, dimension_semantics=("arbitrary",) * ndim)


def _dot(a, b):
    return jnp.dot(a.astype(BF16), b.astype(BF16), preferred_element_type=F32)


def _dot_nt(a, b):
    return lax.dot_general(a.astype(BF16), b.astype(BF16), (((1,), (1,)), ((), ())), preferred_element_type=F32)


def _dot_tn(a, b):
    return lax.dot_general(a.astype(BF16), b.astype(BF16), (((0,), (0,)), ((), ())), preferred_element_type=F32)


def _rms_fwd(h, g):
    r = lax.rsqrt(jnp.mean(h * h, axis=-1, keepdims=True) + EPS)
    hh = h * r
    return hh * g, hh, r


def _rms_bwd(dn, hh, r, g):
    dhh = dn * g
    return r * (dhh - hh * jnp.mean(dhh * hh, axis=-1, keepdims=True))


def _sigmoid(x):
    return 1.0 / (1.0 + jnp.exp(-x))


def _silu_and_grad(z):
    s = _sigmoid(z)
    return z * s, s * (1.0 + z * (1.0 - s))


def _gelu(y):
    t = jnp.tanh(GELU_C * (y + GELU_A * y * y * y))
    return 0.5 * y * (1.0 + t), t


def _gelu_grad(y, t):
    return 0.5 * (1.0 + t) + 0.5 * y * (1.0 - t * t) * GELU_C * (1.0 + 3.0 * GELU_A * y * y)


def _rows(shape):
    return lax.broadcasted_iota(jnp.int32, shape, 0)


def _shift_down(x, k, halo):
    y = pltpu.roll(x, k, 0)
    rows = _rows(x.shape)
    for j in range(k):
        y = jnp.where(rows == j, halo[halo.shape[0] - k + j:halo.shape[0] - k + j + 1, :], y)
    return y


def _shift_up(x, k, halo):
    n = x.shape[0]
    y = pltpu.roll(x, n - k, 0)
    rows = _rows(x.shape)
    for j in range(k):
        y = jnp.where(rows == n - k + j, halo[j:j + 1, :], y)
    return y


def _window_sums_back(ext):
    out = []
    s = ext
    for k in (1, 2, 4, 8):
        s = s + pltpu.roll(s, k, 0)
        out.append(s)
    return out


def _window_sums_fwd(ext):
    n = ext.shape[0]
    out = []
    s = ext
    for k in (1, 2, 4, 8):
        s = s + pltpu.roll(s, n - k, 0)
        out.append(s)
    return out


def _pool_inv_count(tile, tt, first_pos, w, width):
    pos = _rows((tt, width)) + (tile * tt - first_pos + 1)
    return 1.0 / jnp.clip(pos, 1, w).astype(F32)


def _slab_spec(lp, tt, sw, index_map):
    nj = sw // 128
    return pl.BlockSpec((tt * nj, 128), index_map), (lp * 4 * nj, 128)


def _slab_load(ref):
    nj = ref.shape[0] // TOKEN_TILE
    return jnp.concatenate([ref[pl.ds(j, TOKEN_TILE, stride=nj), :] for j in range(nj)], axis=1)


def _slab_store(ref, val):
    nj = ref.shape[0] // TOKEN_TILE
    for j in range(nj):
        ref[pl.ds(j, TOKEN_TILE, stride=nj), :] = val[:, j * 128:(j + 1) * 128]


def _s5_disc_math(lr, li, ldt, br, bi):
    dt = jnp.exp(ldt)
    mag = jnp.exp(lr * dt)
    ar = mag * jnp.cos(li * dt)
    ai = mag * jnp.sin(li * dt)
    den = lr * lr + li * li
    kr = ((ar - 1.0) * lr + ai * li) / den
    ki = (ai * lr - (ar - 1.0) * li) / den
    bbr = kr[None] * br - ki[None] * bi
    bbi = kr[None] * bi + ki[None] * br
    return ar, ai, bbr, bbi


def s5_disc_fwd(lr, li, ldt, br_t, bi_t, name):
    def body(lr_ref, li_ref, ldt_ref, br_ref, bi_ref, ar_ref, ai_ref, bbr_ref, bbi_ref):
        ar, ai, bbr, bbi = _s5_disc_math(lr_ref[...], li_ref[...], ldt_ref[...], br_ref[...], bi_ref[...])
        ar_ref[...] = ar
        ai_ref[...] = ai
        bbr_ref[...] = bbr
        bbi_ref[...] = bbi

    sd = jax.ShapeDtypeStruct
    return pl.pallas_call(
        body, name=name,
        out_shape=(sd(lr.shape, F32), sd(lr.shape, F32), sd(br_t.shape, F32), sd(br_t.shape, F32)),
    )(lr, li, ldt, br_t, bi_t)


def s5_disc_bwd(lr, li, ldt, br_t, bi_t, dar, dai, dbbr, dbbi, name):
    def body(lr_ref, li_ref, ldt_ref, br_ref, bi_ref, dar_ref, dai_ref, dbbr_ref, dbbi_ref,
             dlr_ref, dli_ref, dldt_ref, dbr_ref, dbi_ref):
        _, vjp = jax.vjp(_s5_disc_math, lr_ref[...], li_ref[...], ldt_ref[...], br_ref[...], bi_ref[...])
        dlr, dli, dldt, dbr, dbi = vjp((dar_ref[...], dai_ref[...], dbbr_ref[...], dbbi_ref[...]))
        dlr_ref[...] = dlr
        dli_ref[...] = dli
        dldt_ref[...] = dldt
        dbr_ref[...] = dbr
        dbi_ref[...] = dbi

    sd = jax.ShapeDtypeStruct
    return pl.pallas_call(
        body, name=name,
        out_shape=(sd(lr.shape, F32), sd(lr.shape, F32), sd(ldt.shape, F32), sd(br_t.shape, F32), sd(br_t.shape, F32)),
    )(lr, li, ldt, br_t, bi_t, dar, dai, dbbr, dbbi)


def s5_fwd1(h, g, w_in, bdre, bdim, name):
    lp, d = h.shape
    tt = TOKEN_TILE
    cw, sw = bdre.shape[1], bdre.shape[2]

    def body(h_ref, g_ref, w_hbm, bdre_hbm, bdim_hbm, u_ref, z_ref, xr_ref, xi_ref, w, bre, bim, n_sc):
        i, c = pl.program_id(0), pl.program_id(1)

        @pl.when((i == 0) & (c == 0))
        def _():
            pltpu.sync_copy(w_hbm, w)
            pltpu.sync_copy(bdre_hbm, bre)
            pltpu.sync_copy(bdim_hbm, bim)

        @pl.when(c == 0)
        def _():
            n_sc[...] = _rms_fwd(h_ref[...], g_ref[...])[0].astype(BF16)

        n = n_sc[...]
        u = jnp.dot(n, w[c], preferred_element_type=F32)
        u_ref[...] = u
        z_ref[...] = jnp.dot(n, w[c + 4], preferred_element_type=F32)
        ub = u.astype(BF16)
        _slab_store(xr_ref, jnp.dot(ub, bre[c], preferred_element_type=F32))
        _slab_store(xi_ref, jnp.dot(ub, bim[c], preferred_element_type=F32))

    sd = jax.ShapeDtypeStruct
    slab, slab_shape = _slab_spec(lp, tt, sw, lambda i, c: (i * 4 + c, 0))
    return pl.pallas_call(
        body, name=name, grid=(lp // tt, 4),
        in_specs=[pl.BlockSpec((tt, d), lambda i, c: (i, 0)), pl.BlockSpec((1, d), lambda i, c: (0, 0)), ANY, ANY, ANY],
        out_specs=[pl.BlockSpec((tt, cw), lambda i, c: (i, c)), pl.BlockSpec((tt, cw), lambda i, c: (i, c)), slab, slab],
        out_shape=(sd((lp, d), F32), sd((lp, d), F32), sd(slab_shape, F32), sd(slab_shape, F32)),
        scratch_shapes=[pltpu.VMEM(w_in.shape, BF16), pltpu.VMEM(bdre.shape, BF16), pltpu.VMEM(bdim.shape, BF16),
                        pltpu.VMEM((tt, d), BF16)],
        compiler_params=_params(ndim=2),
    )(h, g, w_in, bdre, bdim)


def s5_scan_fwd(xr, xi, ar, ai, name):
    nj = ar.shape[1]
    tt = TOKEN_TILE
    cpb = SCAN_CHUNKS
    nt = xr.shape[0] // (4 * tt * nj)

    def body(xr_ref, xi_ref, ar_ref, ai_ref, sr_ref, si_ref, st_r, st_i):
        i, cg = pl.program_id(0), pl.program_id(1)

        @pl.when(i == 0)
        def _():
            for q in range(cpb):
                st_r[cg * cpb + q] = jnp.zeros((nj, 128), F32)
                st_i[cg * cpb + q] = jnp.zeros((nj, 128), F32)

        a_r = [ar_ref[cg * cpb + q] for q in range(cpb)]
        a_i = [ai_ref[cg * cpb + q] for q in range(cpb)]

        def step(t, carry):
            out = []
            for q in range(cpb):
                s_r, s_i = carry[q]
                rows = pl.ds(pl.multiple_of((q * tt + t) * nj, nj), nj)
                n_r = a_r[q] * s_r - a_i[q] * s_i + xr_ref[rows, :]
                n_i = a_r[q] * s_i + a_i[q] * s_r + xi_ref[rows, :]
                sr_ref[rows, :] = n_r
                si_ref[rows, :] = n_i
                out.append((n_r, n_i))
            return tuple(out)

        init = tuple((st_r[cg * cpb + q], st_i[cg * cpb + q]) for q in range(cpb))
        final = lax.fori_loop(0, tt, step, init, unroll=8)
        for q in range(cpb):
            st_r[cg * cpb + q] = final[q][0]
            st_i[cg * cpb + q] = final[q][1]

    blk = pl.BlockSpec((cpb * tt * nj, 128), lambda i, cg: (i * (4 // cpb) + cg, 0))
    par = pl.BlockSpec((4, nj, 128), lambda i, cg: (0, 0, 0))
    sd = jax.ShapeDtypeStruct
    return pl.pallas_call(
        body, name=name, grid=(nt, 4 // cpb),
        in_specs=[blk, blk, par, par], out_specs=[blk, blk],
        out_shape=(sd(xr.shape, F32), sd(xr.shape, F32)),
        scratch_shapes=[pltpu.VMEM((4, nj, 128), F32), pltpu.VMEM((4, nj, 128), F32)],
        compiler_params=_params(ndim=2),
    )(xr, xi, ar, ai)


def _s5_mix_fwd(sr_ref, si_ref, u_ref, d_ref, cre, cim, c):
    y = _dot(_slab_load(sr_ref), cre[c]) + _dot(_slab_load(si_ref), cim[c]) + d_ref[c] * u_ref[...]
    gy, t = _gelu(y)
    return y, gy, t


def s5_fwd3(sr, si, u, z, h, cdre, cdim, w_glu, w_out, d_skip, b_glu, name):
    lp, d = h.shape
    tt = TOKEN_TILE
    sw, cw = cdre.shape[1], cdre.shape[2]

    def body(sr_ref, si_ref, u_ref, z_ref, h_ref, d_ref, bg_ref, cre_hbm, cim_hbm, wg_hbm, wo_hbm,
             o_ref, cre, cim, wg, wo, gy_sc, q_sc):
        i, c = pl.program_id(0), pl.program_id(1)

        @pl.when((i == 0) & (c == 0))
        def _():
            pltpu.sync_copy(cre_hbm, cre)
            pltpu.sync_copy(cim_hbm, cim)
            pltpu.sync_copy(wg_hbm, wg)
            pltpu.sync_copy(wo_hbm, wo)

        _, gy, _ = _s5_mix_fwd(sr_ref, si_ref, u_ref, d_ref, cre, cim, c)
        gy_sc[c] = gy
        part = _dot(gy, wg[c])

        @pl.when(c == 0)
        def _():
            q_sc[...] = part

        @pl.when(c > 0)
        def _():
            q_sc[...] += part

        @pl.when(c == 3)
        def _():
            sig = _sigmoid(q_sc[...] + bg_ref[...])
            zz = z_ref[...]
            sz = zz * _sigmoid(zz)
            o = h_ref[...]
            for k in range(4):
                cols = slice(k * cw, (k + 1) * cw)
                o = o + _dot(gy_sc[k] * sig[:, cols] * sz[:, cols], wo[k])
            o_ref[...] = o

    row = lambda i, c: (i, 0)
    chunk = lambda i, c: (i, c)
    slab, _ = _slab_spec(lp, tt, sw, lambda i, c: (i * 4 + c, 0))
    return pl.pallas_call(
        body, name=name, grid=(lp // tt, 4),
        in_specs=[slab, slab, pl.BlockSpec((tt, cw), chunk),
                  pl.BlockSpec((tt, d), row), pl.BlockSpec((tt, d), row),
                  pl.BlockSpec((4, 1, cw), lambda i, c: (0, 0, 0)), pl.BlockSpec((1, d), lambda i, c: (0, 0)),
                  ANY, ANY, ANY, ANY],
        out_specs=pl.BlockSpec((tt, d), row),
        out_shape=jax.ShapeDtypeStruct((lp, d), F32),
        scratch_shapes=[pltpu.VMEM(cdre.shape, BF16), pltpu.VMEM(cdim.shape, BF16), pltpu.VMEM(w_glu.shape, BF16),
                        pltpu.VMEM(w_out.shape, BF16), pltpu.VMEM((4, tt, cw), F32), pltpu.VMEM((tt, d), F32)],
        compiler_params=_params(ndim=2),
    )(sr, si, u, z, h, d_skip, b_glu, cdre, cdim, w_glu, w_out)


def s5_bwd3a(dh, sr, si, u, z, cdre, cdim, w_glu, w_out, d_skip, b_glu, name):
    lp, d = dh.shape
    tt = TOKEN_TILE
    nt = lp // tt
    sw, cw = cdre.shape[1], cdre.shape[2]

    def body(dh_ref, sr_ref, si_ref, u_ref, z_ref, d_ref, bg_ref, cre_hbm, cim_hbm, wg_hbm, wo_hbm,
             dy_ref, dp_ref, dwo_hbm, dwg_hbm, dbg_hbm,
             cre, cim, wg, wo, y_sc, t_sc, gy_sc, q_sc, dq_sc, dgy_sc, dwo, dwg, dbg):
        i, c = pl.program_id(0), pl.program_id(1)

        @pl.when((i == 0) & (c == 0))
        def _():
            pltpu.sync_copy(cre_hbm, cre)
            pltpu.sync_copy(cim_hbm, cim)
            pltpu.sync_copy(wg_hbm, wg)
            pltpu.sync_copy(wo_hbm, wo)
            dwo[...] = jnp.zeros_like(dwo)
            dwg[...] = jnp.zeros_like(dwg)
            dbg[...] = jnp.zeros_like(dbg)

        y, gy, t = _s5_mix_fwd(sr_ref, si_ref, u_ref, d_ref, cre, cim, c)
        y_sc[c] = y
        t_sc[c] = t
        gy_sc[c] = gy
        part = _dot(gy, wg[c])

        @pl.when(c == 0)
        def _():
            q_sc[...] = part

        @pl.when(c > 0)
        def _():
            q_sc[...] += part

        @pl.when(c == 3)
        def _():
            sig = _sigmoid(q_sc[...] + bg_ref[...])
            sz, dsz = _silu_and_grad(z_ref[...])
            dhv = dh_ref[...]
            for k in range(4):
                cols = slice(k * cw, (k + 1) * cw)
                gy_k, sig_k, sz_k = gy_sc[k], sig[:, cols], sz[:, cols]
                y2 = gy_k * sig_k
                dy3 = _dot_nt(dhv, wo[k])
                dwo[k] += _dot_tn(y2 * sz_k, dhv)
                dy2 = dy3 * sz_k
                dp_ref[0, :, cols] = (dy3 * y2 * dsz[:, cols]).astype(BF16)
                dq_sc[:, cols] = dy2 * gy_k * sig_k * (1.0 - sig_k)
                dgy_sc[k] = dy2 * sig_k
            dq = dq_sc[...]
            dbg[...] += jnp.sum(dq, axis=0, keepdims=True)
            for k in range(4):
                cols = slice(k * cw, (k + 1) * cw)
                dwg[k] += _dot_tn(gy_sc[k], dq)
                dgy = dgy_sc[k] + _dot_nt(dq, wg[k])
                dy_ref[:, cols] = dgy * _gelu_grad(y_sc[k], t_sc[k])

        @pl.when((i == nt - 1) & (c == 3))
        def _():
            pltpu.sync_copy(dwo, dwo_hbm)
            pltpu.sync_copy(dwg, dwg_hbm)
            pltpu.sync_copy(dbg, dbg_hbm)

    row = lambda i, c: (i, 0)
    chunk = lambda i, c: (i, c)
    sd = jax.ShapeDtypeStruct
    acc = pltpu.VMEM((4, tt, cw), F32)
    slab, _ = _slab_spec(lp, tt, sw, lambda i, c: (i * 4 + c, 0))
    return pl.pallas_call(
        body, name=name, grid=(nt, 4),
        in_specs=[pl.BlockSpec((tt, d), row), slab, slab,
                  pl.BlockSpec((tt, cw), chunk), pl.BlockSpec((tt, d), row),
                  pl.BlockSpec((4, 1, cw), lambda i, c: (0, 0, 0)), pl.BlockSpec((1, d), lambda i, c: (0, 0)),
                  ANY, ANY, ANY, ANY],
        out_specs=[pl.BlockSpec((tt, d), row), pl.BlockSpec((1, tt, d), lambda i, c: (1, i, 0)), ANY, ANY, ANY],
        out_shape=(sd((lp, d), F32), sd((2, lp, d), BF16), sd(w_out.shape, F32), sd(w_glu.shape, F32), sd((1, d), F32)),
        scratch_shapes=[pltpu.VMEM(cdre.shape, BF16), pltpu.VMEM(cdim.shape, BF16), pltpu.VMEM(w_glu.shape, BF16),
                        pltpu.VMEM(w_out.shape, BF16), acc, acc, acc, pltpu.VMEM((tt, d), F32), pltpu.VMEM((tt, d), F32), acc,
                        pltpu.VMEM(w_out.shape, F32), pltpu.VMEM(w_glu.shape, F32), pltpu.VMEM((1, d), F32)],
        compiler_params=_params(ndim=2),
    )(dh, sr, si, u, z, d_skip, b_glu, cdre, cdim, w_glu, w_out)


def s5_bwd3b(dy, sr, si, u, cdre, cdim, d_skip, name):
    lp, d = dy.shape
    tt = TOKEN_TILE
    nt = lp // tt
    sw, cw = cdre.shape[1], cdre.shape[2]

    def body(dy_ref, sr_ref, si_ref, u_ref, d_ref, cre_hbm, cim_hbm,
             dsr_ref, dsi_ref, dus_ref, dcre_hbm, dcim_hbm, dd_hbm, cre, cim, dcre, dcim, dd):
        i, c = pl.program_id(0), pl.program_id(1)

        @pl.when((i == 0) & (c == 0))
        def _():
            pltpu.sync_copy(cre_hbm, cre)
            pltpu.sync_copy(cim_hbm, cim)
            dcre[...] = jnp.zeros_like(dcre)
            dcim[...] = jnp.zeros_like(dcim)
            dd[...] = jnp.zeros_like(dd)

        dyv = dy_ref[...]
        dd[c] += jnp.sum(dyv * u_ref[...], axis=0, keepdims=True)
        dus_ref[...] = dyv * d_ref[c]
        _slab_store(dsr_ref, _dot_nt(dyv, cre[c]))
        _slab_store(dsi_ref, _dot_nt(dyv, cim[c]))
        dcre[c] += _dot_tn(_slab_load(sr_ref), dyv)
        dcim[c] += _dot_tn(_slab_load(si_ref), dyv)

        @pl.when((i == nt - 1) & (c == 3))
        def _():
            pltpu.sync_copy(dcre, dcre_hbm)
            pltpu.sync_copy(dcim, dcim_hbm)
            pltpu.sync_copy(dd, dd_hbm)

    chunk = lambda i, c: (i, c)
    sd = jax.ShapeDtypeStruct
    slab, slab_shape = _slab_spec(lp, tt, sw, lambda i, c: (i * 4 + c, 0))
    return pl.pallas_call(
        body, name=name, grid=(nt, 4),
        in_specs=[pl.BlockSpec((tt, cw), chunk), slab, slab,
                  pl.BlockSpec((tt, cw), chunk), pl.BlockSpec((4, 1, cw), lambda i, c: (0, 0, 0)), ANY, ANY],
        out_specs=[slab, slab, pl.BlockSpec((tt, cw), chunk), ANY, ANY, ANY],
        out_shape=(sd(slab_shape, F32), sd(slab_shape, F32), sd((lp, d), F32),
                   sd(cdre.shape, F32), sd(cdim.shape, F32), sd((4, 1, cw), F32)),
        scratch_shapes=[pltpu.VMEM(cdre.shape, BF16), pltpu.VMEM(cdim.shape, BF16),
                        pltpu.VMEM(cdre.shape, F32), pltpu.VMEM(cdim.shape, F32), pltpu.VMEM((4, 1, cw), F32)],
        compiler_params=_params(ndim=2),
    )(dy, sr, si, u, d_skip, cdre, cdim)


def s5_scan_bwd(gr, gi, sr, si, ar, ai, name):
    nj = ar.shape[1]
    tt = TOKEN_TILE
    cpb = SCAN_CHUNKS
    nt = gr.shape[0] // (4 * tt * nj)

    def body(gr_ref, gi_ref, sr_ref, si_ref, ar_ref, ai_ref, lr_ref, li_ref, dar_ref, dai_ref, st_r, st_i, acc_r, acc_i):
        i, cg = pl.program_id(0), pl.program_id(1)

        @pl.when((i == 0) & (cg == 0))
        def _():
            for ref in (st_r, st_i, acc_r, acc_i):
                ref[...] = jnp.zeros_like(ref)

        a_r = [ar_ref[cg * cpb + q] for q in range(cpb)]
        a_i = [ai_ref[cg * cpb + q] for q in range(cpb)]

        def slab(q, t):
            return pl.ds(pl.multiple_of((q * tt + t) * nj, nj), nj)

        def adjoint(q, t, l_r, l_i):
            rows = slab(q, t)
            n_r = gr_ref[rows, :] + a_r[q] * l_r + a_i[q] * l_i
            n_i = gi_ref[rows, :] + a_r[q] * l_i - a_i[q] * l_r
            lr_ref[rows, :] = n_r
            li_ref[rows, :] = n_i
            return n_r, n_i

        def pair(q, t, l_r, l_i, d_r, d_i):
            rows = slab(q, t)
            p_r, p_i = sr_ref[rows, :], si_ref[rows, :]
            return d_r + l_r * p_r + l_i * p_i, d_i + l_i * p_r - l_r * p_i

        def step(k, carry):
            t = tt - 1 - k
            out = []
            for q in range(cpb):
                l_r, l_i, d_r, d_i = carry[q]
                l_r, l_i = adjoint(q, t, l_r, l_i)
                d_r, d_i = pair(q, t - 1, l_r, l_i, d_r, d_i)
                out.append((l_r, l_i, d_r, d_i))
            return tuple(out)

        init = []
        for q in range(cpb):
            ch = cg * cpb + q
            l_r, l_i = st_r[ch], st_i[ch]
            d_r, d_i = pair(q, tt - 1, l_r, l_i, acc_r[ch], acc_i[ch])
            init.append((l_r, l_i, d_r, d_i))
        final = lax.fori_loop(0, tt - 1, step, tuple(init), unroll=8)
        for q in range(cpb):
            ch = cg * cpb + q
            l_r, l_i, d_r, d_i = final[q]
            l_r, l_i = adjoint(q, 0, l_r, l_i)
            st_r[ch] = l_r
            st_i[ch] = l_i
            acc_r[ch] = d_r
            acc_i[ch] = d_i
            dar_ref[ch] = d_r
            dai_ref[ch] = d_i

    blk = pl.BlockSpec((cpb * tt * nj, 128), lambda i, cg: ((nt - 1 - i) * (4 // cpb) + cg, 0))
    par = pl.BlockSpec((4, nj, 128), lambda i, cg: (0, 0, 0))
    sd = jax.ShapeDtypeStruct
    return pl.pallas_call(
        body, name=name, grid=(nt, 4 // cpb),
        in_specs=[blk, blk, blk, blk, par, par], out_specs=[blk, blk, par, par],
        out_shape=(sd(gr.shape, F32), sd(gr.shape, F32), sd((4, nj, 128), F32), sd((4, nj, 128), F32)),
        scratch_shapes=[pltpu.VMEM((4, nj, 128), F32)] * 4,
        compiler_params=_params(ndim=2),
    )(gr, gi, sr, si, ar, ai)


def s5_bwd1(lam_r, lam_i, dus, u, dp, h, dh, g, w_in, bdre, bdim, name):
    lp, d = h.shape
    tt = TOKEN_TILE
    nt = lp // tt
    cw, sw = bdre.shape[1], bdre.shape[2]

    def body(lr_ref, li_ref, dus_ref, u_ref, dpz_ref, h_ref, dh_ref, g_ref, w_hbm, bre_hbm, bim_hbm,
             dpu_ref, dho_ref, n_ref, dbre_hbm, dbim_hbm, dg_hbm, w, bre, bim, dn_sc, dbre, dbim, dg):
        i, c = pl.program_id(0), pl.program_id(1)

        @pl.when((i == 0) & (c == 0))
        def _():
            pltpu.sync_copy(w_hbm, w)
            pltpu.sync_copy(bre_hbm, bre)
            pltpu.sync_copy(bim_hbm, bim)
            dbre[...] = jnp.zeros_like(dbre)
            dbim[...] = jnp.zeros_like(dbim)
            dg[...] = jnp.zeros_like(dg)

        l_r, l_i, uv = _slab_load(lr_ref), _slab_load(li_ref), u_ref[...]
        du = dus_ref[...] + _dot_nt(l_r, bre[c]) + _dot_nt(l_i, bim[c])
        dbre[c] += _dot_tn(uv, l_r)
        dbim[c] += _dot_tn(uv, l_i)
        dpu_ref[0] = du.astype(BF16)
        part = _dot_nt(du, w[c])

        @pl.when(c == 0)
        def _():
            dn_sc[...] = part

        @pl.when(c > 0)
        def _():
            dn_sc[...] += part

        @pl.when(c == 3)
        def _():
            dz = dpz_ref[0]
            dn = dn_sc[...]
            for k in range(4):
                dn = dn + _dot_nt(dz[:, k * cw:(k + 1) * cw], w[4 + k])
            gv = g_ref[...]
            n, hh, rr = _rms_fwd(h_ref[...], gv)
            n_ref[...] = n.astype(BF16)
            dg[...] += jnp.sum(dn * hh, axis=0, keepdims=True)
            dho_ref[...] = dh_ref[...] + _rms_bwd(dn, hh, rr, gv)

        @pl.when((i == nt - 1) & (c == 3))
        def _():
            pltpu.sync_copy(dbre, dbre_hbm)
            pltpu.sync_copy(dbim, dbim_hbm)
            pltpu.sync_copy(dg, dg_hbm)

    row = lambda i, c: (i, 0)
    chunk = lambda i, c: (i, c)
    sd = jax.ShapeDtypeStruct
    slab, _ = _slab_spec(lp, tt, sw, lambda i, c: (i * 4 + c, 0))
    return pl.pallas_call(
        body, name=name, grid=(nt, 4),
        in_specs=[slab, slab, pl.BlockSpec((tt, cw), chunk),
                  pl.BlockSpec((tt, cw), chunk), pl.BlockSpec((1, tt, d), lambda i, c: (1, i, 0)),
                  pl.BlockSpec((tt, d), row), pl.BlockSpec((tt, d), row), pl.BlockSpec((1, d), lambda i, c: (0, 0)),
                  ANY, ANY, ANY],
        out_specs=[pl.BlockSpec((1, tt, cw), lambda i, c: (0, i, c)), pl.BlockSpec((tt, d), row), pl.BlockSpec((tt, d), row),
                   ANY, ANY, ANY],
        out_shape=(sd(dp.shape, BF16), sd((lp, d), F32), sd((lp, d), BF16),
                   sd(bdre.shape, F32), sd(bdim.shape, F32), sd((1, d), F32)),
        input_output_aliases={4: 0},
        scratch_shapes=[pltpu.VMEM(w_in.shape, BF16), pltpu.VMEM(bdre.shape, BF16), pltpu.VMEM(bdim.shape, BF16),
                        pltpu.VMEM((tt, d), F32), pltpu.VMEM(bdre.shape, F32), pltpu.VMEM(bdim.shape, F32), pltpu.VMEM((1, d), F32)],
        compiler_params=_params(ndim=2),
    )(lam_r, lam_i, dus, u, dp, h, dh, g, w_in, bdre, bdim)


def grad_w_in(n, dp, blk, name):
    lp, d = n.shape
    npart, _, width = dp.shape
    tt = TOKEN_TILE
    per = width // blk

    def body(n_ref, dp_ref, o_ref):
        part = _dot_tn(n_ref[...], dp_ref[0])

        @pl.when(pl.program_id(1) == 0)
        def _():
            o_ref[0] = part

        @pl.when(pl.program_id(1) > 0)
        def _():
            o_ref[0] += part

    return pl.pallas_call(
        body, name=name, grid=(npart * per, lp // tt),
        in_specs=[pl.BlockSpec((tt, d), lambda j, i: (i, 0)), pl.BlockSpec((1, tt, blk), lambda j, i: (j // per, i, j % per))],
        out_specs=pl.BlockSpec((1, d, blk), lambda j, i: (j, 0, 0)),
        out_shape=jax.ShapeDtypeStruct((npart * per, d, blk), F32),
        compiler_params=_params(ndim=2),
    )(n, dp)


def _conv_fwd_chunk(n, w, cw_ref, cb_ref, halo, c, nch):
    bg = jnp.dot(n, w[c], preferred_element_type=F32)
    cg = jnp.dot(n, w[nch + c], preferred_element_type=F32)
    v = jnp.dot(n, w[2 * nch + c], preferred_element_type=F32)
    z = jnp.dot(n, w[3 * nch + c], preferred_element_type=F32)
    hc = cg * v
    taps = cw_ref[c]
    conv = taps[2:3, :] * hc + taps[1:2, :] * _shift_down(hc, 1, halo) + taps[0:1, :] * _shift_down(hc, 2, halo) + cb_ref[c]
    return bg, cg, v, z, hc, conv


def conv_fwd(h, g, w_in, conv_w, conv_b, w_out, name):
    lp, d = h.shape
    tt = TOKEN_TILE
    nt = lp // tt
    nch, ce = w_out.shape[0], w_out.shape[1]

    def body(h_ref, g_ref, cw_ref, cb_ref, w_hbm, wo_hbm, o_ref, halo_ref, w, wo, halo):
        i = pl.program_id(0)

        @pl.when(i == 0)
        def _():
            pltpu.sync_copy(w_hbm, w)
            pltpu.sync_copy(wo_hbm, wo)
            halo[...] = jnp.zeros_like(halo)

        hv = h_ref[...]
        n = _rms_fwd(hv, g_ref[...])[0].astype(BF16)
        o = hv
        for c in range(nch):
            bg, _, _, z, hc, conv = _conv_fwd_chunk(n, w, cw_ref, cb_ref, halo[c], c, nch)
            o = o + _dot(bg * conv * (z * _sigmoid(z)), wo[c])
            halo[c] = hc[tt - CONV_HALO:, :]
            halo_ref[0, c] = hc[tt - CONV_HALO:, :]
        o_ref[...] = o

    sd = jax.ShapeDtypeStruct
    return pl.pallas_call(
        body, name=name, grid=(nt,),
        in_specs=[pl.BlockSpec((tt, d), lambda i: (i, 0)), pl.BlockSpec((1, d), lambda i: (0, 0)),
                  pl.BlockSpec(conv_w.shape, lambda i: (0, 0, 0)), pl.BlockSpec(conv_b.shape, lambda i: (0, 0, 0)), ANY, ANY],
        out_specs=[pl.BlockSpec((tt, d), lambda i: (i, 0)), pl.BlockSpec((1, nch, CONV_HALO, ce), lambda i: (i, 0, 0, 0))],
        out_shape=(sd((lp, d), F32), sd((nt, nch, CONV_HALO, ce), F32)),
        scratch_shapes=[pltpu.VMEM(w_in.shape, BF16), pltpu.VMEM(w_out.shape, BF16), pltpu.VMEM((nch, CONV_HALO, ce), F32)],
        compiler_params=_params(),
    )(h, g, conv_w, conv_b, w_in, w_out)


def conv_bwd(h, dh, halos, g, w_in, conv_w, conv_b, w_out, name):
    lp, d = h.shape
    tt = TOKEN_TILE
    nt = lp // tt
    nch, ce = w_out.shape[0], w_out.shape[1]

    def body(h_ref, dh_ref, halo_ref, g_ref, cw_ref, cb_ref, w_hbm, wo_hbm,
             dho_ref, n_ref, dp_ref, dwo_hbm, dcw_hbm, dcb_hbm, dg_hbm, w, wo, nxt, dwo, dcw, dcb, dg):
        i = pl.program_id(0)

        @pl.when(i == 0)
        def _():
            pltpu.sync_copy(w_hbm, w)
            pltpu.sync_copy(wo_hbm, wo)
            for ref in (nxt, dwo, dcw, dcb, dg):
                ref[...] = jnp.zeros_like(ref)

        gv = g_ref[...]
        nf, hh, rr = _rms_fwd(h_ref[...], gv)
        n = nf.astype(BF16)
        n_ref[...] = n
        dhv = dh_ref[...]
        has_prev = (i < nt - 1).astype(F32)
        dn = jnp.zeros((tt, d), F32)
        for c in range(nch):
            halo = halo_ref[0, c] * has_prev
            bg, cg, v, z, hc, conv = _conv_fwd_chunk(n, w, cw_ref, cb_ref, halo, c, nch)
            sz, dsz = _silu_and_grad(z)
            y1 = bg * conv
            dy2 = _dot_nt(dhv, wo[c])
            dwo[c] += _dot_tn(y1 * sz, dhv)
            dy1 = dy2 * sz
            dz = dy2 * y1 * dsz
            dbg = dy1 * conv
            dconv = dy1 * bg
            dcb[c] += jnp.sum(dconv, axis=0, keepdims=True)
            up1 = _shift_up(dconv, 1, nxt[c])
            up2 = _shift_up(dconv, 2, nxt[c])
            nxt[c] = dconv[:CONV_HALO, :]
            taps = cw_ref[c]
            dhc = taps[2:3, :] * dconv + taps[1:2, :] * up1 + taps[0:1, :] * up2
            dcw[c, 0:1, :] += jnp.sum(hc * up2, axis=0, keepdims=True)
            dcw[c, 1:2, :] += jnp.sum(hc * up1, axis=0, keepdims=True)
            dcw[c, 2:3, :] += jnp.sum(hc * dconv, axis=0, keepdims=True)
            dcg = dhc * v
            dv = dhc * cg
            cols = slice(c * ce, (c + 1) * ce)
            for p, val in enumerate((dbg, dcg, dv, dz)):
                dp_ref[p, :, cols] = val.astype(BF16)
                dn = dn + _dot_nt(val, w[p * nch + c])
        dg[...] += jnp.sum(dn * hh, axis=0, keepdims=True)
        dho_ref[...] = dhv + _rms_bwd(dn, hh, rr, gv)

        @pl.when(i == nt - 1)
        def _():
            pltpu.sync_copy(dwo, dwo_hbm)
            pltpu.sync_copy(dcw, dcw_hbm)
            pltpu.sync_copy(dcb, dcb_hbm)
            pltpu.sync_copy(dg, dg_hbm)

    rev = lambda i: (nt - 1 - i, 0)
    sd = jax.ShapeDtypeStruct
    return pl.pallas_call(
        body, name=name, grid=(nt,),
        in_specs=[pl.BlockSpec((tt, d), rev), pl.BlockSpec((tt, d), rev),
                  pl.BlockSpec((1, nch, CONV_HALO, ce), lambda i: (jnp.maximum(nt - 2 - i, 0), 0, 0, 0)),
                  pl.BlockSpec((1, d), lambda i: (0, 0)),
                  pl.BlockSpec(conv_w.shape, lambda i: (0, 0, 0)), pl.BlockSpec(conv_b.shape, lambda i: (0, 0, 0)), ANY, ANY],
        out_specs=[pl.BlockSpec((tt, d), rev), pl.BlockSpec((tt, d), rev),
                   pl.BlockSpec((4, tt, nch * ce), lambda i: (0, nt - 1 - i, 0)), ANY, ANY, ANY, ANY],
        out_shape=(sd((lp, d), F32), sd((lp, d), BF16), sd((4, lp, nch * ce), BF16),
                   sd(w_out.shape, F32), sd((nch, 8, ce), F32), sd((nch, 1, ce), F32), sd((1, d), F32)),
        scratch_shapes=[pltpu.VMEM(w_in.shape, BF16), pltpu.VMEM(w_out.shape, BF16), pltpu.VMEM((nch, CONV_HALO, ce), F32),
                        pltpu.VMEM(w_out.shape, F32), pltpu.VMEM((nch, 8, ce), F32), pltpu.VMEM((nch, 1, ce), F32),
                        pltpu.VMEM((1, d), F32)],
        compiler_params=_params(),
    )(h, dh, halos, g, conv_w, conv_b, w_in, w_out)


def _pool_fwd_group(n, w, wg, bg_ref, sc_ref, halo, k, tile, tt, first_pos):
    u = jnp.dot(n, w[k], preferred_element_type=F32)
    z = jnp.dot(n, w[4 + k], preferred_element_type=F32)
    ext = jnp.concatenate([halo, u], axis=0)
    win = _window_sums_back(ext)[k][POOL_HALO:, :]
    mixed = win * _pool_inv_count(tile, tt, first_pos, POOL_WINDOWS[k], u.shape[1]) - u
    outs = _dot(mixed, wg[k]) + bg_ref[k]
    return u, z, mixed, outs, outs * sc_ref[k]


def pool_fwd(h, g, w_in, w_grp, b_grp, scale, w_out, first_pos, name):
    lp, d = h.shape
    tt = TOKEN_TILE
    nt = lp // tt
    gw = w_grp.shape[1]

    def body(h_ref, g_ref, bg_ref, sc_ref, w_hbm, wg_hbm, wo_hbm, o_ref, halo_ref, w, wg, wo, halo):
        i = pl.program_id(0)

        @pl.when(i == 0)
        def _():
            pltpu.sync_copy(w_hbm, w)
            pltpu.sync_copy(wg_hbm, wg)
            pltpu.sync_copy(wo_hbm, wo)
            halo[...] = jnp.zeros_like(halo)

        hv = h_ref[...]
        n = _rms_fwd(hv, g_ref[...])[0].astype(BF16)
        o = hv
        for k in range(4):
            u, z, _, _, yp = _pool_fwd_group(n, w, wg, bg_ref, sc_ref, halo[k], k, i, tt, first_pos)
            o = o + _dot(yp * (z * _sigmoid(z)), wo[k])
            halo[k] = u[tt - POOL_HALO:, :]
            halo_ref[0, k] = u[tt - POOL_HALO:, :]
        o_ref[...] = o

    sd = jax.ShapeDtypeStruct
    small = pl.BlockSpec((4, 1, gw), lambda i: (0, 0, 0))
    return pl.pallas_call(
        body, name=name, grid=(nt,),
        in_specs=[pl.BlockSpec((tt, d), lambda i: (i, 0)), pl.BlockSpec((1, d), lambda i: (0, 0)), small, small, ANY, ANY, ANY],
        out_specs=[pl.BlockSpec((tt, d), lambda i: (i, 0)), pl.BlockSpec((1, 4, POOL_HALO, gw), lambda i: (i, 0, 0, 0))],
        out_shape=(sd((lp, d), F32), sd((nt, 4, POOL_HALO, gw), F32)),
        scratch_shapes=[pltpu.VMEM(w_in.shape, BF16), pltpu.VMEM(w_grp.shape, BF16), pltpu.VMEM(w_out.shape, BF16),
                        pltpu.VMEM((4, POOL_HALO, gw), F32)],
        compiler_params=_params(),
    )(h, g, b_grp, scale, w_in, w_grp, w_out)


def pool_bwd(h, dh, halos, g, w_in, w_grp, b_grp, scale, w_out, first_pos, name):
    lp, d = h.shape
    tt = TOKEN_TILE
    nt = lp // tt
    gw = w_grp.shape[1]

    def body(h_ref, dh_ref, halo_ref, g_ref, bg_ref, sc_ref, w_hbm, wg_hbm, wo_hbm,
             dho_ref, n_ref, dp_ref, dwo_hbm, dwg_hbm, dbg_hbm, dsc_hbm, dg_hbm,
             w, wg, wo, nxt, dwo, dwg, dbg, dsc, dg):
        i = pl.program_id(0)
        tile = nt - 1 - i

        @pl.when(i == 0)
        def _():
            pltpu.sync_copy(w_hbm, w)
            pltpu.sync_copy(wg_hbm, wg)
            pltpu.sync_copy(wo_hbm, wo)
            for ref in (nxt, dwo, dwg, dbg, dsc, dg):
                ref[...] = jnp.zeros_like(ref)

        gv = g_ref[...]
        nf, hh, rr = _rms_fwd(h_ref[...], gv)
        n = nf.astype(BF16)
        n_ref[...] = n
        dhv = dh_ref[...]
        has_prev = (i < nt - 1).astype(F32)
        dn = jnp.zeros((tt, d), F32)
        for k in range(4):
            u, z, mixed, outs, yp = _pool_fwd_group(n, w, wg, bg_ref, sc_ref, halo_ref[0, k] * has_prev, k, tile, tt, first_pos)
            sz, dsz = _silu_and_grad(z)
            dy = _dot_nt(dhv, wo[k])
            dwo[k] += _dot_tn(yp * sz, dhv)
            dyp = dy * sz
            dz = dy * yp * dsz
            dsc[k] += jnp.sum(dyp * outs, axis=0, keepdims=True)
            douts = dyp * sc_ref[k]
            dbg[k] += jnp.sum(douts, axis=0, keepdims=True)
            dwg[k] += _dot_tn(mixed, douts)
            dmixed = _dot_nt(douts, wg[k])
            dm = dmixed * _pool_inv_count(tile, tt, first_pos, POOL_WINDOWS[k], gw)
            ext = jnp.concatenate([dm, nxt[k]], axis=0)
            du = _window_sums_fwd(ext)[k][:tt, :] - dmixed
            nxt[k] = dm[:POOL_HALO, :]
            cols = slice(k * gw, (k + 1) * gw)
            dp_ref[0, :, cols] = du.astype(BF16)
            dp_ref[1, :, cols] = dz.astype(BF16)
            dn = dn + _dot_nt(du, w[k]) + _dot_nt(dz, w[4 + k])
        dg[...] += jnp.sum(dn * hh, axis=0, keepdims=True)
        dho_ref[...] = dhv + _rms_bwd(dn, hh, rr, gv)

        @pl.when(i == nt - 1)
        def _():
            pltpu.sync_copy(dwo, dwo_hbm)
            pltpu.sync_copy(dwg, dwg_hbm)
            pltpu.sync_copy(dbg, dbg_hbm)
            pltpu.sync_copy(dsc, dsc_hbm)
            pltpu.sync_copy(dg, dg_hbm)

    rev = lambda i: (nt - 1 - i, 0)
    sd = jax.ShapeDtypeStruct
    small = pl.BlockSpec((4, 1, gw), lambda i: (0, 0, 0))
    return pl.pallas_call(
        body, name=name, grid=(nt,),
        in_specs=[pl.BlockSpec((tt, d), rev), pl.BlockSpec((tt, d), rev),
                  pl.BlockSpec((1, 4, POOL_HALO, gw), lambda i: (jnp.maximum(nt - 2 - i, 0), 0, 0, 0)),
                  pl.BlockSpec((1, d), lambda i: (0, 0)), small, small, ANY, ANY, ANY],
        out_specs=[pl.BlockSpec((tt, d), rev), pl.BlockSpec((tt, d), rev),
                   pl.BlockSpec((2, tt, 4 * gw), lambda i: (0, nt - 1 - i, 0)), ANY, ANY, ANY, ANY, ANY],
        out_shape=(sd((lp, d), F32), sd((lp, d), BF16), sd((2, lp, 4 * gw), BF16),
                   sd(w_out.shape, F32), sd(w_grp.shape, F32), sd((4, 1, gw), F32), sd((4, 1, gw), F32), sd((1, d), F32)),
        scratch_shapes=[pltpu.VMEM(w_in.shape, BF16), pltpu.VMEM(w_grp.shape, BF16), pltpu.VMEM(w_out.shape, BF16),
                        pltpu.VMEM((4, POOL_HALO, gw), F32), pltpu.VMEM(w_out.shape, F32), pltpu.VMEM(w_grp.shape, F32),
                        pltpu.VMEM((4, 1, gw), F32), pltpu.VMEM((4, 1, gw), F32), pltpu.VMEM((1, d), F32)],
        compiler_params=_params(),
    )(h, dh, halos, g, b_grp, scale, w_in, w_grp, w_out)


def loss_head(h, target, g, pad_tiles, name):
    lp, d = h.shape
    tt = TOKEN_TILE
    nt = lp // tt

    def body(h_ref, t_ref, g_ref, dh_ref, dg_ref, loss_ref, acc):
        i = pl.program_id(0)

        @pl.when(i == 0)
        def _():
            acc[...] = jnp.zeros_like(acc)
            dg_ref[...] = jnp.zeros_like(dg_ref)

        @pl.when(i < pad_tiles)
        def _():
            dh_ref[...] = jnp.zeros_like(dh_ref)

        @pl.when(i >= pad_tiles)
        def _():
            gv = g_ref[...]
            n, hh, rr = _rms_fwd(h_ref[...], gv)
            err = n - t_ref[...]
            acc[...] += 0.5 * jnp.sum(jnp.mean(err * err, axis=-1, keepdims=True), axis=0, keepdims=True)
            dn = err * (1.0 / d)
            dg_ref[...] += jnp.sum(dn * hh, axis=0, keepdims=True)
            dh_ref[...] = _rms_bwd(dn, hh, rr, gv)

        loss_ref[...] = jnp.broadcast_to(acc[...], loss_ref.shape)

    sd = jax.ShapeDtypeStruct
    return pl.pallas_call(
        body, name=name, grid=(nt,),
        in_specs=[pl.BlockSpec((tt, d), lambda i: (i, 0)), pl.BlockSpec((tt, d), lambda i: (jnp.maximum(i - pad_tiles, 0), 0)),
                  pl.BlockSpec((1, d), lambda i: (0, 0))],
        out_specs=[pl.BlockSpec((tt, d), lambda i: (i, 0)), pl.BlockSpec((1, d), lambda i: (0, 0)),
                   pl.BlockSpec((8, 128), lambda i: (0, 0))],
        out_shape=(sd((lp, d), F32), sd((1, d), F32), sd((8, 128), F32)),
        scratch_shapes=[pltpu.VMEM((1, 1), F32)],
        compiler_params=_params(),
    )(h, target, g)


def exchange(arrs, gather, name):
    n = len(arrs)

    def body(*refs):
        ins, outs = refs[:n], refs[n:2 * n]
        send_sems, recv_sems, own_sems = refs[2 * n:]
        x, y, c = lax.axis_index("x"), lax.axis_index("y"), lax.axis_index("c")
        me = 4 * x + 2 * y + c
        own = []
        for a in range(n):
            cp = pltpu.make_async_copy(ins[a] if gather else ins[a].at[me], outs[a].at[me], own_sems.at[a])
            cp.start()
            own.append(cp)
        sent = []
        for k in range(1, N_DEV):
            px = 1 - x if k & 4 else x
            py = 1 - y if k & 2 else y
            pc = 1 - c if k & 1 else c
            peer = 4 * px + 2 * py + pc
            for a in range(n):
                cp = pltpu.make_async_remote_copy(
                    src_ref=ins[a] if gather else ins[a].at[peer], dst_ref=outs[a].at[me],
                    send_sem=send_sems.at[a, k - 1], recv_sem=recv_sems.at[a, k - 1],
                    device_id=(px, py, pc), device_id_type=pl.DeviceIdType.MESH)
                cp.start()
                sent.append((cp, a, k, peer, (px, py, pc)))
        for cp, a, k, peer, pid in sent:
            cp.wait_send()
            pltpu.make_async_remote_copy(
                src_ref=ins[a] if gather else ins[a].at[peer], dst_ref=outs[a].at[peer],
                send_sem=send_sems.at[a, k - 1], recv_sem=recv_sems.at[a, k - 1],
                device_id=pid, device_id_type=pl.DeviceIdType.MESH).wait_recv()
        for cp in own:
            cp.wait()

    hbm = pl.BlockSpec(memory_space=pltpu.HBM)
    out_shape = tuple(jax.ShapeDtypeStruct(((N_DEV,) + a.shape) if gather else a.shape, a.dtype) for a in arrs)
    return pl.pallas_call(
        body, name=name, in_specs=[hbm] * n, out_specs=[hbm] * n, out_shape=out_shape,
        scratch_shapes=[pltpu.SemaphoreType.DMA((n, N_DEV - 1)), pltpu.SemaphoreType.DMA((n, N_DEV - 1)),
                        pltpu.SemaphoreType.DMA((n,))],
    )(*[pltpu.with_memory_space_constraint(a, pltpu.HBM) for a in arrs])


def _adamw(w, g, m, v):
    m = ADAM_B1 * m + (1.0 - ADAM_B1) * g
    v = ADAM_B2 * v + (1.0 - ADAM_B2) * (g * g)
    m_hat = m / (1.0 - ADAM_B1 ** ADAM_STEP)
    v_hat = v / (1.0 - ADAM_B2 ** ADAM_STEP)
    return -ADAM_LR * (m_hat / (jnp.sqrt(v_hat) + ADAM_EPS) + ADAM_WD * w), m, v


def _update_tile_rows(rows, cols):
    if rows * cols <= UPDATE_TILE_ELEMS:
        return rows
    return max(t for t in range(8, UPDATE_TILE_ELEMS // cols + 1, 8) if rows % t == 0)


def _sum_in_order(p_ref):
    g = p_ref[0]
    for j in range(1, p_ref.shape[0]):
        g = g + p_ref[j]
    return g


def sum_parts(parts, name):
    nparts, rows, cols = parts.shape
    tr = _update_tile_rows(rows, cols)

    def body(p_ref, g_ref):
        g_ref[...] = _sum_in_order(p_ref)

    return pl.pallas_call(
        body, name=name, grid=(rows // tr,),
        in_specs=[pl.BlockSpec((nparts, tr, cols), lambda i: (0, i, 0))],
        out_specs=pl.BlockSpec((tr, cols), lambda i: (i, 0)), out_shape=jax.ShapeDtypeStruct((rows, cols), F32),
        compiler_params=_params(),
    )(parts)


def sum_adamw(parts, w, m, v, name):
    rows, cols = w.shape
    nparts = parts.shape[0]
    tr = _update_tile_rows(rows, cols)

    def body(p_ref, w_ref, m_ref, v_ref, g_ref, d_ref, nm_ref, nv_ref):
        g = _sum_in_order(p_ref)
        delta, nm, nv = _adamw(w_ref[...], g, m_ref[...], v_ref[...])
        g_ref[...] = g
        d_ref[...] = delta
        nm_ref[...] = nm
        nv_ref[...] = nv

    blk = pl.BlockSpec((tr, cols), lambda i: (i, 0))
    sd = jax.ShapeDtypeStruct((rows, cols), F32)
    return pl.pallas_call(
        body, name=name, grid=(rows // tr,),
        in_specs=[pl.BlockSpec((nparts, tr, cols), lambda i: (0, i, 0)), blk, blk, blk],
        out_specs=[blk] * 4, out_shape=(sd,) * 4,
        compiler_params=_params(),
    )(parts, w, m, v)


S5_NAMES = ("w_in", "lam_re", "lam_im", "log_dt", "b_re", "b_im", "c_re", "c_im", "d_skip", "w_glu", "b_glu", "w_out")
CONV_NAMES = ("w_in", "conv_w", "conv_b", "w_out")
POOL_NAMES = ("w_in", "w_grp", "b_grp", "scale", "w_out")
LAYER_KINDS = ("s5", "conv", "pool", "s5")
LAYER_NAMES = {"s5": S5_NAMES, "conv": CONV_NAMES, "pool": POOL_NAMES}
SHARDED = {"s5": ("w_in", "w_glu", "w_out"), "conv": ("w_in", "conv_w", "w_out"), "pool": ("w_in", "w_grp", "b_grp", "w_out")}
GATHER_F32 = ("conv_w", "b_grp")


def weight_names():
    names = ["meta_tokens"]
    for i, kind in enumerate(LAYER_KINDS):
        names.append("norm%d_g" % i)
        names += ["l%d_%s" % (i, n) for n in LAYER_NAMES[kind]]
    names.append("final_g")
    return names


def sharded_names():
    return ["meta_tokens"] + ["l%d_%s" % (i, n) for i, kind in enumerate(LAYER_KINDS) for n in SHARDED[kind]]


def _block_diag_in(bb_t, gc):
    i, g, p = bb_t.shape
    t = bb_t.reshape(i, 4, gc, p)
    return jnp.einsum("icjp,jk->cjikp", t, jnp.eye(gc, dtype=F32)).reshape(4, gc * i, gc * p)


def _block_diag_in_grad(dbd, gc):
    i, p = dbd.shape[1] // gc, dbd.shape[2] // gc
    return jnp.einsum("cjijp->icjp", dbd.reshape(4, gc, i, gc, p)).reshape(i, 4 * gc, p)


def _block_diag_out(cc, gc):
    g, i, p = cc.shape
    return jnp.einsum("cjip,jk->cjpki", cc.reshape(4, gc, i, p), jnp.eye(gc, dtype=F32)).reshape(4, gc * p, gc * i)


def _block_diag_out_grad(dcd, gc):
    p, i = dcd.shape[1] // gc, dcd.shape[2] // gc
    return jnp.einsum("cjpji->cjip", dcd.reshape(4, gc, p, gc, i)).reshape(4 * gc, i, p)


def _to_owner_blocks(a, axis):
    shape = a.shape[:axis] + (N_DEV, a.shape[axis] // N_DEV) + a.shape[axis + 1:]
    return jnp.moveaxis(a.reshape(shape), axis, 0)


def _from_owner_blocks(a, axis):
    a = jnp.moveaxis(a, 0, axis)
    return a.reshape(a.shape[:axis] + (a.shape[axis] * a.shape[axis + 1],) + a.shape[axis + 2:])


def _step(x, target, weights, moments_m, moments_v):
    seq, d = x.shape[1], x.shape[2]
    n_meta = weights["meta_tokens"].shape[0]
    tt = TOKEN_TILE
    pad_tiles = -(-n_meta // tt)
    p0 = pad_tiles * tt
    lp = p0 + seq
    first_pos = p0 - n_meta
    gc = d // 4 // S5_GROUP
    cw = d // 4

    big_names = [n for n in sharded_names() if n != "meta_tokens" and n.split("_", 1)[1] not in GATHER_F32]
    small_names = [n for n in sharded_names() if n not in big_names]
    gathered = dict(zip(big_names, exchange([weights[n].astype(BF16) for n in big_names], True, "gather_weights")))
    gathered.update(zip(small_names, exchange([weights[n] for n in small_names], True, "gather_small")))

    meta = _from_owner_blocks(gathered["meta_tokens"], 1)
    h = jnp.concatenate([jnp.zeros((first_pos, d), F32), meta, x[0]], axis=0)

    def vec(name):
        return weights[name].reshape(1, -1)

    full = {}
    for i, kind in enumerate(LAYER_KINDS):
        p = "l%d_" % i
        w_in = gathered[p + "w_in"]
        if kind == "s5":
            lr, li = weights[p + "lam_re"], weights[p + "lam_im"]
            ldt = weights[p + "log_dt"].reshape(-1, 1)
            br_t = jnp.transpose(weights[p + "b_re"], (2, 0, 1))
            bi_t = jnp.transpose(weights[p + "b_im"], (2, 0, 1))
            ar, ai, bbr, bbi = s5_disc_fwd(lr, li, ldt, br_t, bi_t, p + "disc_fwd")
            full[i] = dict(
                w_in=w_in, disc=(lr, li, ldt, br_t, bi_t),
                ar=ar.reshape(4, -1, 128), ai=ai.reshape(4, -1, 128),
                bdre=_block_diag_in(bbr, gc).astype(BF16), bdim=_block_diag_in(bbi, gc).astype(BF16),
                cdre=_block_diag_out(weights[p + "c_re"], gc).astype(BF16),
                cdim=_block_diag_out(-weights[p + "c_im"], gc).astype(BF16),
                w_glu=gathered[p + "w_glu"].reshape(4, cw, d), w_out=gathered[p + "w_out"].reshape(4, cw, d),
                d_skip=weights[p + "d_skip"].reshape(4, 1, cw), b_glu=vec(p + "b_glu"))
        elif kind == "conv":
            ce = w_in.shape[2]
            nch = 2
            conv_w = _from_owner_blocks(gathered[p + "conv_w"], 1)
            full[i] = dict(
                w_in=w_in, conv_w=jnp.transpose(conv_w.reshape(CONV_K, nch, ce), (1, 0, 2)),
                conv_b=weights[p + "conv_b"].reshape(nch, 1, ce), w_out=gathered[p + "w_out"].reshape(nch, ce, d))
        else:
            gw = w_in.shape[2]
            full[i] = dict(
                w_in=w_in, w_grp=_from_owner_blocks(gathered[p + "w_grp"], 1),
                b_grp=_from_owner_blocks(gathered[p + "b_grp"], 1).reshape(4, 1, gw),
                scale=weights[p + "scale"].reshape(4, 1, gw), w_out=gathered[p + "w_out"].reshape(4, gw, d))

    saved = {}
    for i, kind in enumerate(LAYER_KINDS):
        p, f, g = "l%d_" % i, full[i], vec("norm%d_g" % i)
        if kind == "s5":
            u, z, xr, xi = s5_fwd1(h, g, f["w_in"], f["bdre"], f["bdim"], p + "fwd_in")
            sr, si = s5_scan_fwd(xr, xi, f["ar"], f["ai"], p + "scan_fwd")
            saved[i] = (h, u, z, sr, si)
            h = s5_fwd3(sr, si, u, z, h, f["cdre"], f["cdim"], f["w_glu"], f["w_out"], f["d_skip"], f["b_glu"], p + "fwd_out")
        elif kind == "conv":
            h_new, halos = conv_fwd(h, g, f["w_in"], f["conv_w"], f["conv_b"], f["w_out"], p + "fwd")
            saved[i] = (h, halos)
            h = h_new
        else:
            h_new, halos = pool_fwd(h, g, f["w_in"], f["w_grp"], f["b_grp"], f["scale"], f["w_out"], first_pos, p + "fwd")
            saved[i] = (h, halos)
            h = h_new

    dh, dg_final, loss_tile = loss_head(h, target[0], vec("final_g"), pad_tiles, "loss_head")
    loss = lax.psum(loss_tile[0, 0], ("x", "y", "c"))

    grads = {"final_g": dg_final}
    for i in reversed(range(len(LAYER_KINDS))):
        kind = LAYER_KINDS[i]
        p, f, g = "l%d_" % i, full[i], vec("norm%d_g" % i)
        if kind == "s5":
            h_in, u, z, sr, si = saved[i]
            dy, dp, dwo, dwg, dbg = s5_bwd3a(dh, sr, si, u, z, f["cdre"], f["cdim"], f["w_glu"], f["w_out"],
                                             f["d_skip"], f["b_glu"], p + "bwd_out")
            dsr, dsi, dus, dcre, dcim, dd = s5_bwd3b(dy, sr, si, u, f["cdre"], f["cdim"], f["d_skip"], p + "bwd_read")
            lam_r, lam_i, dar, dai = s5_scan_bwd(dsr, dsi, sr, si, f["ar"], f["ai"], p + "scan_bwd")
            dp, dh, n, dbre, dbim, dg = s5_bwd1(lam_r, lam_i, dus, u, dp, h_in, dh, g, f["w_in"], f["bdre"], f["bdim"], p + "bwd_in")
            dw_in = grad_w_in(n, dp, f["w_in"].shape[2], p + "grad_w_in")
            lr, li, ldt, br_t, bi_t = f["disc"]
            dlr, dli, dldt, dbr_t, dbi_t = s5_disc_bwd(
                lr, li, ldt, br_t, bi_t, dar.reshape(lr.shape), dai.reshape(lr.shape),
                _block_diag_in_grad(dbre, gc), _block_diag_in_grad(dbim, gc), p + "disc_bwd")
            grads.update({
                p + "w_in": dw_in, p + "lam_re": dlr, p + "lam_im": dli, p + "log_dt": dldt,
                p + "b_re": jnp.transpose(dbr_t, (1, 2, 0)), p + "b_im": jnp.transpose(dbi_t, (1, 2, 0)),
                p + "c_re": _block_diag_out_grad(dcre, gc), p + "c_im": -_block_diag_out_grad(dcim, gc),
                p + "d_skip": dd, p + "w_glu": dwg.reshape(N_DEV, -1, d), p + "b_glu": dbg,
                p + "w_out": dwo.reshape(N_DEV, -1, d)})
        elif kind == "conv":
            h_in, halos = saved[i]
            dh, n, dp, dwo, dcw, dcb, dg = conv_bwd(h_in, dh, halos, g, f["w_in"], f["conv_w"], f["conv_b"], f["w_out"], p + "bwd")
            dw_in = grad_w_in(n, dp, f["w_in"].shape[2], p + "grad_w_in")
            dconv_w = jnp.transpose(dcw[:, :CONV_K, :], (1, 0, 2)).reshape(CONV_K, -1)
            grads.update({p + "w_in": dw_in, p + "conv_w": _to_owner_blocks(dconv_w, 1), p + "conv_b": dcb,
                          p + "w_out": dwo.reshape(N_DEV, -1, d)})
        else:
            h_in, halos = saved[i]
            dh, n, dp, dwo, dwgrp, dbgrp, dsc, dg = pool_bwd(h_in, dh, halos, g, f["w_in"], f["w_grp"], f["b_grp"], f["scale"],
                                                             f["w_out"], first_pos, p + "bwd")
            dw_in = grad_w_in(n, dp, f["w_in"].shape[2], p + "grad_w_in")
            grads.update({p + "w_in": dw_in, p + "w_grp": _to_owner_blocks(dwgrp, 1),
                          p + "b_grp": _to_owner_blocks(dbgrp.reshape(4, -1), 1), p + "scale": dsc,
                          p + "w_out": dwo.reshape(N_DEV, -1, d)})
        grads["norm%d_g" % i] = dg
    grad_x = dh[p0:][None]
    grads["meta_tokens"] = _to_owner_blocks(dh[first_pos:p0], 1)

    names = weight_names()
    sh_names = sharded_names()
    rep_names = [n for n in names if n not in sh_names]

    def as2d(a):
        return a.reshape(-1, a.shape[-1])

    def pack(tree):
        flat = [jnp.pad(tree[n].reshape(-1), (0, -tree[n].size % 1024)) for n in rep_names]
        flat = jnp.concatenate(flat)
        return jnp.pad(flat, (0, -flat.size % (PACK_ROWS * 128))).reshape(-1, 128)

    sent = [grads[n].reshape(N_DEV, -1, grads[n].shape[-1]) for n in sh_names]
    received = exchange(sent + [pack(grads).reshape(N_DEV, -1, 128)], False, "scatter_grads")
    out = {}
    for n, parts in zip(sh_names, received):
        res = sum_adamw(parts, as2d(weights[n]), as2d(moments_m[n]), as2d(moments_v[n]), "update_" + n)
        out[n] = [r.reshape(weights[n].shape) for r in res]

    g_full = exchange([sum_parts(received[-1], "sum_replicated")], True, "gather_small_grads")[0].reshape(1, -1, 128)
    packed = sum_adamw(g_full, pack(weights), pack(moments_m), pack(moments_v), "update_replicated")
    offset = 0
    for n in rep_names:
        size = weights[n].size
        out[n] = [r.reshape(-1)[offset:offset + size].reshape(weights[n].shape) for r in packed]
        offset += size + (-size % 1024)

    return (loss, grad_x) + tuple(out[n][k] for k in range(4) for n in names)


def kernel(x, *rest):
    names = weight_names()
    nw = len(names)
    weights = dict(zip(names, rest[:nw]))
    target = rest[nw]
    moments_m = dict(zip(names, rest[nw + 1:2 * nw + 1]))
    moments_v = dict(zip(names, rest[2 * nw + 1:3 * nw + 1]))
    return _step(x, target, weights, moments_m, moments_v)
```

```python
import functools
import math

import jax
import jax.numpy as jnp
from jax import lax
from jax.experimental import pallas as pl
from jax.experimental.pallas import tpu as pltpu

F32 = jnp.float32
BF16 = jnp.bfloat16
EPS = 1e-6
N_DEV = 8
TOKEN_TILE = 256
SCAN_CHUNKS = 2
S5_GROUP = 16
S5_STATE = 64
POOL_WINDOWS = (2, 4, 8, 16)
POOL_HALO = 16
CONV_K = 3
CONV_HALO = 8
ADAM_LR = 0.001
ADAM_B1 = 0.9
ADAM_B2 = 0.999
ADAM_EPS = 1e-08
ADAM_WD = 0.01
ADAM_STEP = 10
GELU_C = math.sqrt(2.0 / math.pi)
GELU_A = 0.044715
UPDATE_TILE_ELEMS = 1 << 17
PACK_ROWS = 512
VMEM_LIMIT = 56 << 20

ANY = pl.BlockSpec(memory_space=pl.ANY)


def _params(vmem=VMEM_LIMIT, ndim=1):
    return pltpu.CompilerParams(vmem_limit_bytes=vmem, dimension_semantics=("arbitrary",) * ndim)


def _dot(a, b):
    return jnp.dot(a.astype(BF16), b.astype(BF16), preferred_element_type=F32)


def _dot_nt(a, b):
    return lax.dot_general(a.astype(BF16), b.astype(BF16), (((1,), (1,)), ((), ())), preferred_element_type=F32)


def _dot_tn(a, b):
    return lax.dot_general(a.astype(BF16), b.astype(BF16), (((0,), (0,)), ((), ())), preferred_element_type=F32)


def _rms_fwd(h, g):
    r = lax.rsqrt(jnp.mean(h * h, axis=-1, keepdims=True) + EPS)
    hh = h * r
    return hh * g, hh, r


def _rms_bwd(dn, hh, r, g):
    dhh = dn * g
    return r * (dhh - hh * jnp.mean(dhh * hh, axis=-1, keepdims=True))


def _sigmoid(x):
    return 1.0 / (1.0 + jnp.exp(-x))


def _silu_and_grad(z):
    s = _sigmoid(z)
    return z * s, s * (1.0 + z * (1.0 - s))


def _gelu(y):
    t = jnp.tanh(GELU_C * (y + GELU_A * y * y * y))
    return 0.5 * y * (1.0 + t), t


def _gelu_grad(y, t):
    return 0.5 * (1.0 + t) + 0.5 * y * (1.0 - t * t) * GELU_C * (1.0 + 3.0 * GELU_A * y * y)


def _rows(shape):
    return lax.broadcasted_iota(jnp.int32, shape, 0)


def _shift_down(x, k, halo):
    y = pltpu.roll(x, k, 0)
    rows = _rows(x.shape)
    for j in range(k):
        y = jnp.where(rows == j, halo[halo.shape[0] - k + j:halo.shape[0] - k + j + 1, :], y)
    return y


def _shift_up(x, k, halo):
    n = x.shape[0]
    y = pltpu.roll(x, n - k, 0)
    rows = _rows(x.shape)
    for j in range(k):
        y = jnp.where(rows == n - k + j, halo[j:j + 1, :], y)
    return y


def _window_sums_back(ext):
    out = []
    s = ext
    for k in (1, 2, 4, 8):
        s = s + pltpu.roll(s, k, 0)
        out.append(s)
    return out


def _window_sums_fwd(ext):
    n = ext.shape[0]
    out = []
    s = ext
    for k in (1, 2, 4, 8):
        s = s + pltpu.roll(s, n - k, 0)
        out.append(s)
    return out


def _pool_inv_count(tile, tt, first_pos, w, width):
    pos = _rows((tt, width)) + (tile * tt - first_pos + 1)
    return 1.0 / jnp.clip(pos, 1, w).astype(F32)


def _slab_spec(lp, tt, sw, index_map):
    nj = sw // 128
    return pl.BlockSpec((tt * nj, 128), index_map), (lp * 4 * nj, 128)


def _slab_load(ref):
    nj = ref.shape[0] // TOKEN_TILE
    return jnp.concatenate([ref[pl.ds(j, TOKEN_TILE, stride=nj), :] for j in range(nj)], axis=1)


def _slab_store(ref, val):
    nj = ref.shape[0] // TOKEN_TILE
    for j in range(nj):
        ref[pl.ds(j, TOKEN_TILE, stride=nj), :] = val[:, j * 128:(j + 1) * 128]


def _s5_disc_math(lr, li, ldt, br, bi):
    dt = jnp.exp(ldt)
    mag = jnp.exp(lr * dt)
    ar = mag * jnp.cos(li * dt)
    ai = mag * jnp.sin(li * dt)
    den = lr * lr + li * li
    kr = ((ar - 1.0) * lr + ai * li) / den
    ki = (ai * lr - (ar - 1.0) * li) / den
    bbr = kr[None] * br - ki[None] * bi
    bbi = kr[None] * bi + ki[None] * br
    return ar, ai, bbr, bbi


def s5_disc_fwd(lr, li, ldt, br_t, bi_t, name):
    def body(lr_ref, li_ref, ldt_ref, br_ref, bi_ref, ar_ref, ai_ref, bbr_ref, bbi_ref):
        ar, ai, bbr, bbi = _s5_disc_math(lr_ref[...], li_ref[...], ldt_ref[...], br_ref[...], bi_ref[...])
        ar_ref[...] = ar
        ai_ref[...] = ai
        bbr_ref[...] = bbr
        bbi_ref[...] = bbi

    sd = jax.ShapeDtypeStruct
    return pl.pallas_call(
        body, name=name,
        out_shape=(sd(lr.shape, F32), sd(lr.shape, F32), sd(br_t.shape, F32), sd(br_t.shape, F32)),
    )(lr, li, ldt, br_t, bi_t)


def s5_disc_bwd(lr, li, ldt, br_t, bi_t, dar, dai, dbbr, dbbi, name):
    def body(lr_ref, li_ref, ldt_ref, br_ref, bi_ref, dar_ref, dai_ref, dbbr_ref, dbbi_ref,
             dlr_ref, dli_ref, dldt_ref, dbr_ref, dbi_ref):
        _, vjp = jax.vjp(_s5_disc_math, lr_ref[...], li_ref[...], ldt_ref[...], br_ref[...], bi_ref[...])
        dlr, dli, dldt, dbr, dbi = vjp((dar_ref[...], dai_ref[...], dbbr_ref[...], dbbi_ref[...]))
        dlr_ref[...] = dlr
        dli_ref[...] = dli
        dldt_ref[...] = dldt
        dbr_ref[...] = dbr
        dbi_ref[...] = dbi

    sd = jax.ShapeDtypeStruct
    return pl.pallas_call(
        body, name=name,
        out_shape=(sd(lr.shape, F32), sd(lr.shape, F32), sd(ldt.shape, F32), sd(br_t.shape, F32), sd(br_t.shape, F32)),
    )(lr, li, ldt, br_t, bi_t, dar, dai, dbbr, dbbi)


def s5_fwd1(h, g, w_in, bdre, bdim, name):
    lp, d = h.shape
    tt = TOKEN_TILE
    cw, sw = bdre.shape[1], bdre.shape[2]

    def body(h_ref, g_ref, w_hbm, bdre_hbm, bdim_hbm, u_ref, z_ref, xr_ref, xi_ref, w, bre, bim, n_sc):
        i, c = pl.program_id(0), pl.program_id(1)

        @pl.when((i == 0) & (c == 0))
        def _():
            pltpu.sync_copy(w_hbm, w)
            pltpu.sync_copy(bdre_hbm, bre)
            pltpu.sync_copy(bdim_hbm, bim)

        @pl.when(c == 0)
        def _():
            n_sc[...] = _rms_fwd(h_ref[...], g_ref[...])[0].astype(BF16)

        n = n_sc[...]
        u = jnp.dot(n, w[c], preferred_element_type=F32)
        u_ref[...] = u
        z_ref[...] = jnp.dot(n, w[c + 4], preferred_element_type=F32)
        ub = u.astype(BF16)
        _slab_store(xr_ref, jnp.dot(ub, bre[c], preferred_element_type=F32))
        _slab_store(xi_ref, jnp.dot(ub, bim[c], preferred_element_type=F32))

    sd = jax.ShapeDtypeStruct
    slab, slab_shape = _slab_spec(lp, tt, sw, lambda i, c: (i * 4 + c, 0))
    return pl.pallas_call(
        body, name=name, grid=(lp // tt, 4),
        in_specs=[pl.BlockSpec((tt, d), lambda i, c: (i, 0)), pl.BlockSpec((1, d), lambda i, c: (0, 0)), ANY, ANY, ANY],
        out_specs=[pl.BlockSpec((tt, cw), lambda i, c: (i, c)), pl.BlockSpec((tt, cw), lambda i, c: (i, c)), slab, slab],
        out_shape=(sd((lp, d), F32), sd((lp, d), F32), sd(slab_shape, F32), sd(slab_shape, F32)),
        scratch_shapes=[pltpu.VMEM(w_in.shape, BF16), pltpu.VMEM(bdre.shape, BF16), pltpu.VMEM(bdim.shape, BF16),
                        pltpu.VMEM((tt, d), BF16)],
        compiler_params=_params(ndim=2),
    )(h, g, w_in, bdre, bdim)


def s5_scan_fwd(xr, xi, ar, ai, name):
    nj = ar.shape[1]
    tt = TOKEN_TILE
    cpb = SCAN_CHUNKS
    nt = xr.shape[0] // (4 * tt * nj)

    def body(xr_ref, xi_ref, ar_ref, ai_ref, sr_ref, si_ref, st_r, st_i):
        i, cg = pl.program_id(0), pl.program_id(1)

        @pl.when(i == 0)
        def _():
            for q in range(cpb):
                st_r[cg * cpb + q] = jnp.zeros((nj, 128), F32)
                st_i[cg * cpb + q] = jnp.zeros((nj, 128), F32)

        a_r = [ar_ref[cg * cpb + q] for q in range(cpb)]
        a_i = [ai_ref[cg * cpb + q] for q in range(cpb)]

        def step(t, carry):
            out = []
            for q in range(cpb):
                s_r, s_i = carry[q]
                rows = pl.ds(pl.multiple_of((q * tt + t) * nj, nj), nj)
                n_r = a_r[q] * s_r - a_i[q] * s_i + xr_ref[rows, :]
                n_i = a_r[q] * s_i + a_i[q] * s_r + xi_ref[rows, :]
                sr_ref[rows, :] = n_r
                si_ref[rows, :] = n_i
                out.append((n_r, n_i))
            return tuple(out)

        init = tuple((st_r[cg * cpb + q], st_i[cg * cpb + q]) for q in range(cpb))
        final = lax.fori_loop(0, tt, step, init, unroll=8)
        for q in range(cpb):
            st_r[cg * cpb + q] = final[q][0]
            st_i[cg * cpb + q] = final[q][1]

    blk = pl.BlockSpec((cpb * tt * nj, 128), lambda i, cg: (i * (4 // cpb) + cg, 0))
    par = pl.BlockSpec((4, nj, 128), lambda i, cg: (0, 0, 0))
    sd = jax.ShapeDtypeStruct
    return pl.pallas_call(
        body, name=name, grid=(nt, 4 // cpb),
        in_specs=[blk, blk, par, par], out_specs=[blk, blk],
        out_shape=(sd(xr.shape, F32), sd(xr.shape, F32)),
        scratch_shapes=[pltpu.VMEM((4, nj, 128), F32), pltpu.VMEM((4, nj, 128), F32)],
        compiler_params=_params(ndim=2),
    )(xr, xi, ar, ai)


def _s5_mix_fwd(sr_ref, si_ref, u_ref, d_ref, cre, cim, c):
    y = _dot(_slab_load(sr_ref), cre[c]) + _dot(_slab_load(si_ref), cim[c]) + d_ref[c] * u_ref[...]
    gy, t = _gelu(y)
    return y, gy, t


def s5_fwd3(sr, si, u, z, h, cdre, cdim, w_glu, w_out, d_skip, b_glu, name):
    lp, d = h.shape
    tt = TOKEN_TILE
    sw, cw = cdre.shape[1], cdre.shape[2]

    def body(sr_ref, si_ref, u_ref, z_ref, h_ref, d_ref, bg_ref, cre_hbm, cim_hbm, wg_hbm, wo_hbm,
             o_ref, cre, cim, wg, wo, gy_sc, q_sc):
        i, c = pl.program_id(0), pl.program_id(1)

        @pl.when((i == 0) & (c == 0))
        def _():
            pltpu.sync_copy(cre_hbm, cre)
            pltpu.sync_copy(cim_hbm, cim)
            pltpu.sync_copy(wg_hbm, wg)
            pltpu.sync_copy(wo_hbm, wo)

        _, gy, _ = _s5_mix_fwd(sr_ref, si_ref, u_ref, d_ref, cre, cim, c)
        gy_sc[c] = gy
        part = _dot(gy, wg[c])

        @pl.when(c == 0)
        def _():
            q_sc[...] = part

        @pl.when(c > 0)
        def _():
            q_sc[...] += part

        @pl.when(c == 3)
        def _():
            sig = _sigmoid(q_sc[...] + bg_ref[...])
            zz = z_ref[...]
            sz = zz * _sigmoid(zz)
            o = h_ref[...]
            for k in range(4):
                cols = slice(k * cw, (k + 1) * cw)
                o = o + _dot(gy_sc[k] * sig[:, cols] * sz[:, cols], wo[k])
            o_ref[...] = o

    row = lambda i, c: (i, 0)
    chunk = lambda i, c: (i, c)
    slab, _ = _slab_spec(lp, tt, sw, lambda i, c: (i * 4 + c, 0))
    return pl.pallas_call(
        body, name=name, grid=(lp // tt, 4),
        in_specs=[slab, slab, pl.BlockSpec((tt, cw), chunk),
                  pl.BlockSpec((tt, d), row), pl.BlockSpec((tt, d), row),
                  pl.BlockSpec((4, 1, cw), lambda i, c: (0, 0, 0)), pl.BlockSpec((1, d), lambda i, c: (0, 0)),
                  ANY, ANY, ANY, ANY],
        out_specs=pl.BlockSpec((tt, d), row),
        out_shape=jax.ShapeDtypeStruct((lp, d), F32),
        scratch_shapes=[pltpu.VMEM(cdre.shape, BF16), pltpu.VMEM(cdim.shape, BF16), pltpu.VMEM(w_glu.shape, BF16),
                        pltpu.VMEM(w_out.shape, BF16), pltpu.VMEM((4, tt, cw), F32), pltpu.VMEM((tt, d), F32)],
        compiler_params=_params(ndim=2),
    )(sr, si, u, z, h, d_skip, b_glu, cdre, cdim, w_glu, w_out)


def s5_bwd3a(dh, sr, si, u, z, cdre, cdim, w_glu, w_out, d_skip, b_glu, name):
    lp, d = dh.shape
    tt = TOKEN_TILE
    nt = lp // tt
    sw, cw = cdre.shape[1], cdre.shape[2]

    def body(dh_ref, sr_ref, si_ref, u_ref, z_ref, d_ref, bg_ref, cre_hbm, cim_hbm, wg_hbm, wo_hbm,
             dy_ref, dp_ref, dwo_hbm, dwg_hbm, dbg_hbm,
             cre, cim, wg, wo, y_sc, t_sc, gy_sc, q_sc, dq_sc, dgy_sc, dwo, dwg, dbg):
        i, c = pl.program_id(0), pl.program_id(1)

        @pl.when((i == 0) & (c == 0))
        def _():
            pltpu.sync_copy(cre_hbm, cre)
            pltpu.sync_copy(cim_hbm, cim)
            pltpu.sync_copy(wg_hbm, wg)
            pltpu.sync_copy(wo_hbm, wo)
            dwo[...] = jnp.zeros_like(dwo)
            dwg[...] = jnp.zeros_like(dwg)
            dbg[...] = jnp.zeros_like(dbg)

        y, gy, t = _s5_mix_fwd(sr_ref, si_ref, u_ref, d_ref, cre, cim, c)
        y_sc[c] = y
        t_sc[c] = t
        gy_sc[c] = gy
        part = _dot(gy, wg[c])

        @pl.when(c == 0)
        def _():
            q_sc[...] = part

        @pl.when(c > 0)
        def _():
            q_sc[...] += part

        @pl.when(c == 3)
        def _():
            sig = _sigmoid(q_sc[...] + bg_ref[...])
            sz, dsz = _silu_and_grad(z_ref[...])
            dhv = dh_ref[...]
            for k in range(4):
                cols = slice(k * cw, (k + 1) * cw)
                gy_k, sig_k, sz_k = gy_sc[k], sig[:, cols], sz[:, cols]
                y2 = gy_k * sig_k
                dy3 = _dot_nt(dhv, wo[k])
                dwo[k] += _dot_tn(y2 * sz_k, dhv)
                dy2 = dy3 * sz_k
                dp_ref[0, :, cols] = (dy3 * y2 * dsz[:, cols]).astype(BF16)
                dq_sc[:, cols] = dy2 * gy_k * sig_k * (1.0 - sig_k)
                dgy_sc[k] = dy2 * sig_k
            dq = dq_sc[...]
            dbg[...] += jnp.sum(dq, axis=0, keepdims=True)
            for k in range(4):
                cols = slice(k * cw, (k + 1) * cw)
                dwg[k] += _dot_tn(gy_sc[k], dq)
                dgy = dgy_sc[k] + _dot_nt(dq, wg[k])
                dy_ref[:, cols] = dgy * _gelu_grad(y_sc[k], t_sc[k])

        @pl.when((i == nt - 1) & (c == 3))
        def _():
            pltpu.sync_copy(dwo, dwo_hbm)
            pltpu.sync_copy(dwg, dwg_hbm)
            pltpu.sync_copy(dbg, dbg_hbm)

    row = lambda i, c: (i, 0)
    chunk = lambda i, c: (i, c)
    sd = jax.ShapeDtypeStruct
    acc = pltpu.VMEM((4, tt, cw), F32)
    slab, _ = _slab_spec(lp, tt, sw, lambda i, c: (i * 4 + c, 0))
    return pl.pallas_call(
        body, name=name, grid=(nt, 4),
        in_specs=[pl.BlockSpec((tt, d), row), slab, slab,
                  pl.BlockSpec((tt, cw), chunk), pl.BlockSpec((tt, d), row),
                  pl.BlockSpec((4, 1, cw), lambda i, c: (0, 0, 0)), pl.BlockSpec((1, d), lambda i, c: (0, 0)),
                  ANY, ANY, ANY, ANY],
        out_specs=[pl.BlockSpec((tt, d), row), pl.BlockSpec((1, tt, d), lambda i, c: (1, i, 0)), ANY, ANY, ANY],
        out_shape=(sd((lp, d), F32), sd((2, lp, d), BF16), sd(w_out.shape, F32), sd(w_glu.shape, F32), sd((1, d), F32)),
        scratch_shapes=[pltpu.VMEM(cdre.shape, BF16), pltpu.VMEM(cdim.shape, BF16), pltpu.VMEM(w_glu.shape, BF16),
                        pltpu.VMEM(w_out.shape, BF16), acc, acc, acc, pltpu.VMEM((tt, d), F32), pltpu.VMEM((tt, d), F32), acc,
                        pltpu.VMEM(w_out.shape, F32), pltpu.VMEM(w_glu.shape, F32), pltpu.VMEM((1, d), F32)],
        compiler_params=_params(ndim=2),
    )(dh, sr, si, u, z, d_skip, b_glu, cdre, cdim, w_glu, w_out)


def s5_bwd3b(dy, sr, si, u, cdre, cdim, d_skip, name):
    lp, d = dy.shape
    tt = TOKEN_TILE
    nt = lp // tt
    sw, cw = cdre.shape[1], cdre.shape[2]

    def body(dy_ref, sr_ref, si_ref, u_ref, d_ref, cre_hbm, cim_hbm,
             dsr_ref, dsi_ref, dus_ref, dcre_hbm, dcim_hbm, dd_hbm, cre, cim, dcre, dcim, dd):
        i, c = pl.program_id(0), pl.program_id(1)

        @pl.when((i == 0) & (c == 0))
        def _():
            pltpu.sync_copy(cre_hbm, cre)
            pltpu.sync_copy(cim_hbm, cim)
            dcre[...] = jnp.zeros_like(dcre)
            dcim[...] = jnp.zeros_like(dcim)
            dd[...] = jnp.zeros_like(dd)

        dyv = dy_ref[...]
        dd[c] += jnp.sum(dyv * u_ref[...], axis=0, keepdims=True)
        dus_ref[...] = dyv * d_ref[c]
        _slab_store(dsr_ref, _dot_nt(dyv, cre[c]))
        _slab_store(dsi_ref, _dot_nt(dyv, cim[c]))
        dcre[c] += _dot_tn(_slab_load(sr_ref), dyv)
        dcim[c] += _dot_tn(_slab_load(si_ref), dyv)

        @pl.when((i == nt - 1) & (c == 3))
        def _():
            pltpu.sync_copy(dcre, dcre_hbm)
            pltpu.sync_copy(dcim, dcim_hbm)
            pltpu.sync_copy(dd, dd_hbm)

    chunk = lambda i, c: (i, c)
    sd = jax.ShapeDtypeStruct
    slab, slab_shape = _slab_spec(lp, tt, sw, lambda i, c: (i * 4 + c, 0))
    return pl.pallas_call(
        body, name=name, grid=(nt, 4),
        in_specs=[pl.BlockSpec((tt, cw), chunk), slab, slab,
                  pl.BlockSpec((tt, cw), chunk), pl.BlockSpec((4, 1, cw), lambda i, c: (0, 0, 0)), ANY, ANY],
        out_specs=[slab, slab, pl.BlockSpec((tt, cw), chunk), ANY, ANY, ANY],
        out_shape=(sd(slab_shape, F32), sd(slab_shape, F32), sd((lp, d), F32),
                   sd(cdre.shape, F32), sd(cdim.shape, F32), sd((4, 1, cw), F32)),
        scratch_shapes=[pltpu.VMEM(cdre.shape, BF16), pltpu.VMEM(cdim.shape, BF16),
                        pltpu.VMEM(cdre.shape, F32), pltpu.VMEM(cdim.shape, F32), pltpu.VMEM((4, 1, cw), F32)],
        compiler_params=_params(ndim=2),
    )(dy, sr, si, u, d_skip, cdre, cdim)


def s5_scan_bwd(gr, gi, sr, si, ar, ai, name):
    nj = ar.shape[1]
    tt = TOKEN_TILE
    cpb = SCAN_CHUNKS
    nt = gr.shape[0] // (4 * tt * nj)

    def body(gr_ref, gi_ref, sr_ref, si_ref, ar_ref, ai_ref, lr_ref, li_ref, dar_ref, dai_ref, st_r, st_i, acc_r, acc_i):
        i, cg = pl.program_id(0), pl.program_id(1)

        @pl.when((i == 0) & (cg == 0))
        def _():
            for ref in (st_r, st_i, acc_r, acc_i):
                ref[...] = jnp.zeros_like(ref)

        a_r = [ar_ref[cg * cpb + q] for q in range(cpb)]
        a_i = [ai_ref[cg * cpb + q] for q in range(cpb)]

        def slab(q, t):
            return pl.ds(pl.multiple_of((q * tt + t) * nj, nj), nj)

        def adjoint(q, t, l_r, l_i):
            rows = slab(q, t)
            n_r = gr_ref[rows, :] + a_r[q] * l_r + a_i[q] * l_i
            n_i = gi_ref[rows, :] + a_r[q] * l_i - a_i[q] * l_r
            lr_ref[rows, :] = n_r
            li_ref[rows, :] = n_i
            return n_r, n_i

        def pair(q, t, l_r, l_i, d_r, d_i):
            rows = slab(q, t)
            p_r, p_i = sr_ref[rows, :], si_ref[rows, :]
            return d_r + l_r * p_r + l_i * p_i, d_i + l_i * p_r - l_r * p_i

        def step(k, carry):
            t = tt - 1 - k
            out = []
            for q in range(cpb):
                l_r, l_i, d_r, d_i = carry[q]
                l_r, l_i = adjoint(q, t, l_r, l_i)
                d_r, d_i = pair(q, t - 1, l_r, l_i, d_r, d_i)
                out.append((l_r, l_i, d_r, d_i))
            return tuple(out)

        init = []
        for q in range(cpb):
            ch = cg * cpb + q
            l_r, l_i = st_r[ch], st_i[ch]
            d_r, d_i = pair(q, tt - 1, l_r, l_i, acc_r[ch], acc_i[ch])
            init.append((l_r, l_i, d_r, d_i))
        final = lax.fori_loop(0, tt - 1, step, tuple(init), unroll=8)
        for q in range(cpb):
            ch = cg * cpb + q
            l_r, l_i, d_r, d_i = final[q]
            l_r, l_i = adjoint(q, 0, l_r, l_i)
            st_r[ch] = l_r
            st_i[ch] = l_i
            acc_r[ch] = d_r
            acc_i[ch] = d_i
            dar_ref[ch] = d_r
            dai_ref[ch] = d_i

    blk = pl.BlockSpec((cpb * tt * nj, 128), lambda i, cg: ((nt - 1 - i) * (4 // cpb) + cg, 0))
    par = pl.BlockSpec((4, nj, 128), lambda i, cg: (0, 0, 0))
    sd = jax.ShapeDtypeStruct
    return pl.pallas_call(
        body, name=name, grid=(nt, 4 // cpb),
        in_specs=[blk, blk, blk, blk, par, par], out_specs=[blk, blk, par, par],
        out_shape=(sd(gr.shape, F32), sd(gr.shape, F32), sd((4, nj, 128), F32), sd((4, nj, 128), F32)),
        scratch_shapes=[pltpu.VMEM((4, nj, 128), F32)] * 4,
        compiler_params=_params(ndim=2),
    )(gr, gi, sr, si, ar, ai)


def s5_bwd1(lam_r, lam_i, dus, u, dp, h, dh, g, w_in, bdre, bdim, name):
    lp, d = h.shape
    tt = TOKEN_TILE
    nt = lp // tt
    cw, sw = bdre.shape[1], bdre.shape[2]

    def body(lr_ref, li_ref, dus_ref, u_ref, dpz_ref, h_ref, dh_ref, g_ref, w_hbm, bre_hbm, bim_hbm,
             dpu_ref, dho_ref, n_ref, dbre_hbm, dbim_hbm, dg_hbm, w, bre, bim, dn_sc, dbre, dbim, dg):
        i, c = pl.program_id(0), pl.program_id(1)

        @pl.when((i == 0) & (c == 0))
        def _():
            pltpu.sync_copy(w_hbm, w)
            pltpu.sync_copy(bre_hbm, bre)
            pltpu.sync_copy(bim_hbm, bim)
            dbre[...] = jnp.zeros_like(dbre)
            dbim[...] = jnp.zeros_like(dbim)
            dg[...] = jnp.zeros_like(dg)

        l_r, l_i, uv = _slab_load(lr_ref), _slab_load(li_ref), u_ref[...]
        du = dus_ref[...] + _dot_nt(l_r, bre[c]) + _dot_nt(l_i, bim[c])
        dbre[c] += _dot_tn(uv, l_r)
        dbim[c] += _dot_tn(uv, l_i)
        dpu_ref[0] = du.astype(BF16)
        part = _dot_nt(du, w[c])

        @pl.when(c == 0)
        def _():
            dn_sc[...] = part

        @pl.when(c > 0)
        def _():
            dn_sc[...] += part

        @pl.when(c == 3)
        def _():
            dz = dpz_ref[0]
            dn = dn_sc[...]
            for k in range(4):
                dn = dn + _dot_nt(dz[:, k * cw:(k + 1) * cw], w[4 + k])
            gv = g_ref[...]
            n, hh, rr = _rms_fwd(h_ref[...], gv)
            n_ref[...] = n.astype(BF16)
            dg[...] += jnp.sum(dn * hh, axis=0, keepdims=True)
            dho_ref[...] = dh_ref[...] + _rms_bwd(dn, hh, rr, gv)

        @pl.when((i == nt - 1) & (c == 3))
        def _():
            pltpu.sync_copy(dbre, dbre_hbm)
            pltpu.sync_copy(dbim, dbim_hbm)
            pltpu.sync_copy(dg, dg_hbm)

    row = lambda i, c: (i, 0)
    chunk = lambda i, c: (i, c)
    sd = jax.ShapeDtypeStruct
    slab, _ = _slab_spec(lp, tt, sw, lambda i, c: (i * 4 + c, 0))
    return pl.pallas_call(
        body, name=name, grid=(nt, 4),
        in_specs=[slab, slab, pl.BlockSpec((tt, cw), chunk),
                  pl.BlockSpec((tt, cw), chunk), pl.BlockSpec((1, tt, d), lambda i, c: (1, i, 0)),
                  pl.BlockSpec((tt, d), row), pl.BlockSpec((tt, d), row), pl.BlockSpec((1, d), lambda i, c: (0, 0)),
                  ANY, ANY, ANY],
        out_specs=[pl.BlockSpec((1, tt, cw), lambda i, c: (0, i, c)), pl.BlockSpec((tt, d), row), pl.BlockSpec((tt, d), row),
                   ANY, ANY, ANY],
        out_shape=(sd(dp.shape, BF16), sd((lp, d), F32), sd((lp, d), BF16),
                   sd(bdre.shape, F32), sd(bdim.shape, F32), sd((1, d), F32)),
        input_output_aliases={4: 0},
        scratch_shapes=[pltpu.VMEM(w_in.shape, BF16), pltpu.VMEM(bdre.shape, BF16), pltpu.VMEM(bdim.shape, BF16),
                        pltpu.VMEM((tt, d), F32), pltpu.VMEM(bdre.shape, F32), pltpu.VMEM(bdim.shape, F32), pltpu.VMEM((1, d), F32)],
        compiler_params=_params(ndim=2),
    )(lam_r, lam_i, dus, u, dp, h, dh, g, w_in, bdre, bdim)


def grad_w_in(n, dp, blk, name):
    lp, d = n.shape
    npart, _, width = dp.shape
    tt = TOKEN_TILE
    per = width // blk

    def body(n_ref, dp_ref, o_ref):
        part = _dot_tn(n_ref[...], dp_ref[0])

        @pl.when(pl.program_id(1) == 0)
        def _():
            o_ref[0] = part

        @pl.when(pl.program_id(1) > 0)
        def _():
            o_ref[0] += part

    return pl.pallas_call(
        body, name=name, grid=(npart * per, lp // tt),
        in_specs=[pl.BlockSpec((tt, d), lambda j, i: (i, 0)), pl.BlockSpec((1, tt, blk), lambda j, i: (j // per, i, j % per))],
        out_specs=pl.BlockSpec((1, d, blk), lambda j, i: (j, 0, 0)),
        out_shape=jax.ShapeDtypeStruct((npart * per, d, blk), F32),
        compiler_params=_params(ndim=2),
    )(n, dp)


def _conv_fwd_chunk(n, w, cw_ref, cb_ref, halo, c, nch):
    bg = jnp.dot(n, w[c], preferred_element_type=F32)
    cg = jnp.dot(n, w[nch + c], preferred_element_type=F32)
    v = jnp.dot(n, w[2 * nch + c], preferred_element_type=F32)
    z = jnp.dot(n, w[3 * nch + c], preferred_element_type=F32)
    hc = cg * v
    taps = cw_ref[c]
    conv = taps[2:3, :] * hc + taps[1:2, :] * _shift_down(hc, 1, halo) + taps[0:1, :] * _shift_down(hc, 2, halo) + cb_ref[c]
    return bg, cg, v, z, hc, conv


def conv_fwd(h, g, w_in, conv_w, conv_b, w_out, name):
    lp, d = h.shape
    tt = TOKEN_TILE
    nt = lp // tt
    nch, ce = w_out.shape[0], w_out.shape[1]

    def body(h_ref, g_ref, cw_ref, cb_ref, w_hbm, wo_hbm, o_ref, halo_ref, w, wo, halo):
        i = pl.program_id(0)

        @pl.when(i == 0)
        def _():
            pltpu.sync_copy(w_hbm, w)
            pltpu.sync_copy(wo_hbm, wo)
            halo[...] = jnp.zeros_like(halo)

        hv = h_ref[...]
        n = _rms_fwd(hv, g_ref[...])[0].astype(BF16)
        o = hv
        for c in range(nch):
            bg, _, _, z, hc, conv = _conv_fwd_chunk(n, w, cw_ref, cb_ref, halo[c], c, nch)
            o = o + _dot(bg * conv * (z * _sigmoid(z)), wo[c])
            halo[c] = hc[tt - CONV_HALO:, :]
            halo_ref[0, c] = hc[tt - CONV_HALO:, :]
        o_ref[...] = o

    sd = jax.ShapeDtypeStruct
    return pl.pallas_call(
        body, name=name, grid=(nt,),
        in_specs=[pl.BlockSpec((tt, d), lambda i: (i, 0)), pl.BlockSpec((1, d), lambda i: (0, 0)),
                  pl.BlockSpec(conv_w.shape, lambda i: (0, 0, 0)), pl.BlockSpec(conv_b.shape, lambda i: (0, 0, 0)), ANY, ANY],
        out_specs=[pl.BlockSpec((tt, d), lambda i: (i, 0)), pl.BlockSpec((1, nch, CONV_HALO, ce), lambda i: (i, 0, 0, 0))],
        out_shape=(sd((lp, d), F32), sd((nt, nch, CONV_HALO, ce), F32)),
        scratch_shapes=[pltpu.VMEM(w_in.shape, BF16), pltpu.VMEM(w_out.shape, BF16), pltpu.VMEM((nch, CONV_HALO, ce), F32)],
        compiler_params=_params(),
    )(h, g, conv_w, conv_b, w_in, w_out)


def conv_bwd(h, dh, halos, g, w_in, conv_w, conv_b, w_out, name):
    lp, d = h.shape
    tt = TOKEN_TILE
    nt = lp // tt
    nch, ce = w_out.shape[0], w_out.shape[1]

    def body(h_ref, dh_ref, halo_ref, g_ref, cw_ref, cb_ref, w_hbm, wo_hbm,
             dho_ref, n_ref, dp_ref, dwo_hbm, dcw_hbm, dcb_hbm, dg_hbm, w, wo, nxt, dwo, dcw, dcb, dg):
        i = pl.program_id(0)

        @pl.when(i == 0)
        def _():
            pltpu.sync_copy(w_hbm, w)
            pltpu.sync_copy(wo_hbm, wo)
            for ref in (nxt, dwo, dcw, dcb, dg):
                ref[...] = jnp.zeros_like(ref)

        gv = g_ref[...]
        nf, hh, rr = _rms_fwd(h_ref[...], gv)
        n = nf.astype(BF16)
        n_ref[...] = n
        dhv = dh_ref[...]
        has_prev = (i < nt - 1).astype(F32)
        dn = jnp.zeros((tt, d), F32)
        for c in range(nch):
            halo = halo_ref[0, c] * has_prev
            bg, cg, v, z, hc, conv = _conv_fwd_chunk(n, w, cw_ref, cb_ref, halo, c, nch)
            sz, dsz = _silu_and_grad(z)
            y1 = bg * conv
            dy2 = _dot_nt(dhv, wo[c])
            dwo[c] += _dot_tn(y1 * sz, dhv)
            dy1 = dy2 * sz
            dz = dy2 * y1 * dsz
            dbg = dy1 * conv
            dconv = dy1 * bg
            dcb[c] += jnp.sum(dconv, axis=0, keepdims=True)
            up1 = _shift_up(dconv, 1, nxt[c])
            up2 = _shift_up(dconv, 2, nxt[c])
            nxt[c] = dconv[:CONV_HALO, :]
            taps = cw_ref[c]
            dhc = taps[2:3, :] * dconv + taps[1:2, :] * up1 + taps[0:1, :] * up2
            dcw[c, 0:1, :] += jnp.sum(hc * up2, axis=0, keepdims=True)
            dcw[c, 1:2, :] += jnp.sum(hc * up1, axis=0, keepdims=True)
            dcw[c, 2:3, :] += jnp.sum(hc * dconv, axis=0, keepdims=True)
            dcg = dhc * v
            dv = dhc * cg
            cols = slice(c * ce, (c + 1) * ce)
            for p, val in enumerate((dbg, dcg, dv, dz)):
                dp_ref[p, :, cols] = val.astype(BF16)
                dn = dn + _dot_nt(val, w[p * nch + c])
        dg[...] += jnp.sum(dn * hh, axis=0, keepdims=True)
        dho_ref[...] = dhv + _rms_bwd(dn, hh, rr, gv)

        @pl.when(i == nt - 1)
        def _():
            pltpu.sync_copy(dwo, dwo_hbm)
            pltpu.sync_copy(dcw, dcw_hbm)
            pltpu.sync_copy(dcb, dcb_hbm)
            pltpu.sync_copy(dg, dg_hbm)

    rev = lambda i: (nt - 1 - i, 0)
    sd = jax.ShapeDtypeStruct
    return pl.pallas_call(
        body, name=name, grid=(nt,),
        in_specs=[pl.BlockSpec((tt, d), rev), pl.BlockSpec((tt, d), rev),
                  pl.BlockSpec((1, nch, CONV_HALO, ce), lambda i: (jnp.maximum(nt - 2 - i, 0), 0, 0, 0)),
                  pl.BlockSpec((1, d), lambda i: (0, 0)),
                  pl.BlockSpec(conv_w.shape, lambda i: (0, 0, 0)), pl.BlockSpec(conv_b.shape, lambda i: (0, 0, 0)), ANY, ANY],
        out_specs=[pl.BlockSpec((tt, d), rev), pl.BlockSpec((tt, d), rev),
                   pl.BlockSpec((4, tt, nch * ce), lambda i: (0, nt - 1 - i, 0)), ANY, ANY, ANY, ANY],
        out_shape=(sd((lp, d), F32), sd((lp, d), BF16), sd((4, lp, nch * ce), BF16),
                   sd(w_out.shape, F32), sd((nch, 8, ce), F32), sd((nch, 1, ce), F32), sd((1, d), F32)),
        scratch_shapes=[pltpu.VMEM(w_in.shape, BF16), pltpu.VMEM(w_out.shape, BF16), pltpu.VMEM((nch, CONV_HALO, ce), F32),
                        pltpu.VMEM(w_out.shape, F32), pltpu.VMEM((nch, 8, ce), F32), pltpu.VMEM((nch, 1, ce), F32),
                        pltpu.VMEM((1, d), F32)],
        compiler_params=_params(),
    )(h, dh, halos, g, conv_w, conv_b, w_in, w_out)


def _pool_fwd_group(n, w, wg, bg_ref, sc_ref, halo, k, tile, tt, first_pos):
    u = jnp.dot(n, w[k], preferred_element_type=F32)
    z = jnp.dot(n, w[4 + k], preferred_element_type=F32)
    ext = jnp.concatenate([halo, u], axis=0)
    win = _window_sums_back(ext)[k][POOL_HALO:, :]
    mixed = win * _pool_inv_count(tile, tt, first_pos, POOL_WINDOWS[k], u.shape[1]) - u
    outs = _dot(mixed, wg[k]) + bg_ref[k]
    return u, z, mixed, outs, outs * sc_ref[k]


def pool_fwd(h, g, w_in, w_grp, b_grp, scale, w_out, first_pos, name):
    lp, d = h.shape
    tt = TOKEN_TILE
    nt = lp // tt
    gw = w_grp.shape[1]

    def body(h_ref, g_ref, bg_ref, sc_ref, w_hbm, wg_hbm, wo_hbm, o_ref, halo_ref, w, wg, wo, halo):
        i = pl.program_id(0)

        @pl.when(i == 0)
        def _():
            pltpu.sync_copy(w_hbm, w)
            pltpu.sync_copy(wg_hbm, wg)
            pltpu.sync_copy(wo_hbm, wo)
            halo[...] = jnp.zeros_like(halo)

        hv = h_ref[...]
        n = _rms_fwd(hv, g_ref[...])[0].astype(BF16)
        o = hv
        for k in range(4):
            u, z, _, _, yp = _pool_fwd_group(n, w, wg, bg_ref, sc_ref, halo[k], k, i, tt, first_pos)
            o = o + _dot(yp * (z * _sigmoid(z)), wo[k])
            halo[k] = u[tt - POOL_HALO:, :]
            halo_ref[0, k] = u[tt - POOL_HALO:, :]
        o_ref[...] = o

    sd = jax.ShapeDtypeStruct
    small = pl.BlockSpec((4, 1, gw), lambda i: (0, 0, 0))
    return pl.pallas_call(
        body, name=name, grid=(nt,),
        in_specs=[pl.BlockSpec((tt, d), lambda i: (i, 0)), pl.BlockSpec((1, d), lambda i: (0, 0)), small, small, ANY, ANY, ANY],
        out_specs=[pl.BlockSpec((tt, d), lambda i: (i, 0)), pl.BlockSpec((1, 4, POOL_HALO, gw), lambda i: (i, 0, 0, 0))],
        out_shape=(sd((lp, d), F32), sd((nt, 4, POOL_HALO, gw), F32)),
        scratch_shapes=[pltpu.VMEM(w_in.shape, BF16), pltpu.VMEM(w_grp.shape, BF16), pltpu.VMEM(w_out.shape, BF16),
                        pltpu.VMEM((4, POOL_HALO, gw), F32)],
        compiler_params=_params(),
    )(h, g, b_grp, scale, w_in, w_grp, w_out)


def pool_bwd(h, dh, halos, g, w_in, w_grp, b_grp, scale, w_out, first_pos, name):
    lp, d = h.shape
    tt = TOKEN_TILE
    nt = lp // tt
    gw = w_grp.shape[1]

    def body(h_ref, dh_ref, halo_ref, g_ref, bg_ref, sc_ref, w_hbm, wg_hbm, wo_hbm,
             dho_ref, n_ref, dp_ref, dwo_hbm, dwg_hbm, dbg_hbm, dsc_hbm, dg_hbm,
             w, wg, wo, nxt, dwo, dwg, dbg, dsc, dg):
        i = pl.program_id(0)
        tile = nt - 1 - i

        @pl.when(i == 0)
        def _():
            pltpu.sync_copy(w_hbm, w)
            pltpu.sync_copy(wg_hbm, wg)
            pltpu.sync_copy(wo_hbm, wo)
            for ref in (nxt, dwo, dwg, dbg, dsc, dg):
                ref[...] = jnp.zeros_like(ref)

        gv = g_ref[...]
        nf, hh, rr = _rms_fwd(h_ref[...], gv)
        n = nf.astype(BF16)
        n_ref[...] = n
        dhv = dh_ref[...]
        has_prev = (i < nt - 1).astype(F32)
        dn = jnp.zeros((tt, d), F32)
        for k in range(4):
            u, z, mixed, outs, yp = _pool_fwd_group(n, w, wg, bg_ref, sc_ref, halo_ref[0, k] * has_prev, k, tile, tt, first_pos)
            sz, dsz = _silu_and_grad(z)
            dy = _dot_nt(dhv, wo[k])
            dwo[k] += _dot_tn(yp * sz, dhv)
            dyp = dy * sz
            dz = dy * yp * dsz
            dsc[k] += jnp.sum(dyp * outs, axis=0, keepdims=True)
            douts = dyp * sc_ref[k]
            dbg[k] += jnp.sum(douts, axis=0, keepdims=True)
            dwg[k] += _dot_tn(mixed, douts)
            dmixed = _dot_nt(douts, wg[k])
            dm = dmixed * _pool_inv_count(tile, tt, first_pos, POOL_WINDOWS[k], gw)
            ext = jnp.concatenate([dm, nxt[k]], axis=0)
            du = _window_sums_fwd(ext)[k][:tt, :] - dmixed
            nxt[k] = dm[:POOL_HALO, :]
            cols = slice(k * gw, (k + 1) * gw)
            dp_ref[0, :, cols] = du.astype(BF16)
            dp_ref[1, :, cols] = dz.astype(BF16)
            dn = dn + _dot_nt(du, w[k]) + _dot_nt(dz, w[4 + k])
        dg[...] += jnp.sum(dn * hh, axis=0, keepdims=True)
        dho_ref[...] = dhv + _rms_bwd(dn, hh, rr, gv)

        @pl.when(i == nt - 1)
        def _():
            pltpu.sync_copy(dwo, dwo_hbm)
            pltpu.sync_copy(dwg, dwg_hbm)
            pltpu.sync_copy(dbg, dbg_hbm)
            pltpu.sync_copy(dsc, dsc_hbm)
            pltpu.sync_copy(dg, dg_hbm)

    rev = lambda i: (nt - 1 - i, 0)
    sd = jax.ShapeDtypeStruct
    small = pl.BlockSpec((4, 1, gw), lambda i: (0, 0, 0))
    return pl.pallas_call(
        body, name=name, grid=(nt,),
        in_specs=[pl.BlockSpec((tt, d), rev), pl.BlockSpec((tt, d), rev),
                  pl.BlockSpec((1, 4, POOL_HALO, gw), lambda i: (jnp.maximum(nt - 2 - i, 0), 0, 0, 0)),
                  pl.BlockSpec((1, d), lambda i: (0, 0)), small, small, ANY, ANY, ANY],
        out_specs=[pl.BlockSpec((tt, d), rev), pl.BlockSpec((tt, d), rev),
                   pl.BlockSpec((2, tt, 4 * gw), lambda i: (0, nt - 1 - i, 0)), ANY, ANY, ANY, ANY, ANY],
        out_shape=(sd((lp, d), F32), sd((lp, d), BF16), sd((2, lp, 4 * gw), BF16),
                   sd(w_out.shape, F32), sd(w_grp.shape, F32), sd((4, 1, gw), F32), sd((4, 1, gw), F32), sd((1, d), F32)),
        scratch_shapes=[pltpu.VMEM(w_in.shape, BF16), pltpu.VMEM(w_grp.shape, BF16), pltpu.VMEM(w_out.shape, BF16),
                        pltpu.VMEM((4, POOL_HALO, gw), F32), pltpu.VMEM(w_out.shape, F32), pltpu.VMEM(w_grp.shape, F32),
                        pltpu.VMEM((4, 1, gw), F32), pltpu.VMEM((4, 1, gw), F32), pltpu.VMEM((1, d), F32)],
        compiler_params=_params(),
    )(h, dh, halos, g, b_grp, scale, w_in, w_grp, w_out)


def loss_head(h, target, g, pad_tiles, name):
    lp, d = h.shape
    tt = TOKEN_TILE
    nt = lp // tt

    def body(h_ref, t_ref, g_ref, dh_ref, dg_ref, loss_ref, acc):
        i = pl.program_id(0)

        @pl.when(i == 0)
        def _():
            acc[...] = jnp.zeros_like(acc)
            dg_ref[...] = jnp.zeros_like(dg_ref)

        @pl.when(i < pad_tiles)
        def _():
            dh_ref[...] = jnp.zeros_like(dh_ref)

        @pl.when(i >= pad_tiles)
        def _():
            gv = g_ref[...]
            n, hh, rr = _rms_fwd(h_ref[...], gv)
            err = n - t_ref[...]
            acc[...] += 0.5 * jnp.sum(jnp.mean(err * err, axis=-1, keepdims=True), axis=0, keepdims=True)
            dn = err * (1.0 / d)
            dg_ref[...] += jnp.sum(dn * hh, axis=0, keepdims=True)
            dh_ref[...] = _rms_bwd(dn, hh, rr, gv)

        loss_ref[...] = jnp.broadcast_to(acc[...], loss_ref.shape)

    sd = jax.ShapeDtypeStruct
    return pl.pallas_call(
        body, name=name, grid=(nt,),
        in_specs=[pl.BlockSpec((tt, d), lambda i: (i, 0)), pl.BlockSpec((tt, d), lambda i: (jnp.maximum(i - pad_tiles, 0), 0)),
                  pl.BlockSpec((1, d), lambda i: (0, 0))],
        out_specs=[pl.BlockSpec((tt, d), lambda i: (i, 0)), pl.BlockSpec((1, d), lambda i: (0, 0)),
                   pl.BlockSpec((8, 128), lambda i: (0, 0))],
        out_shape=(sd((lp, d), F32), sd((1, d), F32), sd((8, 128), F32)),
        scratch_shapes=[pltpu.VMEM((1, 1), F32)],
        compiler_params=_params(),
    )(h, target, g)


def exchange(arrs, gather, name):
    n = len(arrs)

    def body(*refs):
        ins, outs = refs[:n], refs[n:2 * n]
        send_sems, recv_sems, own_sems = refs[2 * n:]
        x, y, c = lax.axis_index("x"), lax.axis_index("y"), lax.axis_index("c")
        me = 4 * x + 2 * y + c
        own = []
        for a in range(n):
            cp = pltpu.make_async_copy(ins[a] if gather else ins[a].at[me], outs[a].at[me], own_sems.at[a])
            cp.start()
            own.append(cp)
        sent = []
        for k in range(1, N_DEV):
            px = 1 - x if k & 4 else x
            py = 1 - y if k & 2 else y
            pc = 1 - c if k & 1 else c
            peer = 4 * px + 2 * py + pc
            for a in range(n):
                cp = pltpu.make_async_remote_copy(
                    src_ref=ins[a] if gather else ins[a].at[peer], dst_ref=outs[a].at[me],
                    send_sem=send_sems.at[a, k - 1], recv_sem=recv_sems.at[a, k - 1],
                    device_id=(px, py, pc), device_id_type=pl.DeviceIdType.MESH)
                cp.start()
                sent.append((cp, a, k, peer, (px, py, pc)))
        for cp, a, k, peer, pid in sent:
            cp.wait_send()
            pltpu.make_async_remote_copy(
                src_ref=ins[a] if gather else ins[a].at[peer], dst_ref=outs[a].at[peer],
                send_sem=send_sems.at[a, k - 1], recv_sem=recv_sems.at[a, k - 1],
                device_id=pid, device_id_type=pl.DeviceIdType.MESH).wait_recv()
        for cp in own:
            cp.wait()

    hbm = pl.BlockSpec(memory_space=pltpu.HBM)
    out_shape = tuple(jax.ShapeDtypeStruct(((N_DEV,) + a.shape) if gather else a.shape, a.dtype) for a in arrs)
    return pl.pallas_call(
        body, name=name, in_specs=[hbm] * n, out_specs=[hbm] * n, out_shape=out_shape,
        scratch_shapes=[pltpu.SemaphoreType.DMA((n, N_DEV - 1)), pltpu.SemaphoreType.DMA((n, N_DEV - 1)),
                        pltpu.SemaphoreType.DMA((n,))],
    )(*[pltpu.with_memory_space_constraint(a, pltpu.HBM) for a in arrs])


def _peers(x, y, c):
    out = []
    for k in range(1, N_DEV):
        px = 1 - x if k & 4 else x
        py = 1 - y if k & 2 else y
        pc = 1 - c if k & 1 else c
        out.append((k, (px, py, pc), 4 * px + 2 * py + pc))
    return out


def exchange_start(arrs, gather, name):
    n = len(arrs)
    lands = [lax.empty(((N_DEV,) + a.shape) if gather else a.shape, a.dtype) for a in arrs]

    def body(*refs):
        ins, land = refs[:n], refs[n:2 * n]
        send_sems, recv_sems, token = refs[2 * n], refs[2 * n + 1], refs[4 * n + 2]
        x, y, c = lax.axis_index("x"), lax.axis_index("y"), lax.axis_index("c")
        me = 4 * x + 2 * y + c
        for k, pid, peer in _peers(x, y, c):
            for a in range(n):
                pltpu.make_async_remote_copy(
                    src_ref=ins[a] if gather else ins[a].at[peer], dst_ref=land[a].at[me],
                    send_sem=send_sems.at[a * (N_DEV - 1) + k - 1], recv_sem=recv_sems.at[a * (N_DEV - 1) + k - 1],
                    device_id=pid, device_id_type=pl.DeviceIdType.MESH).start()
        token[...] = jnp.zeros_like(token)

    hbm = pl.BlockSpec(memory_space=pltpu.HBM)
    sem = pl.BlockSpec(memory_space=pltpu.SEMAPHORE)
    sems = pltpu.SemaphoreType.DMA((n * (N_DEV - 1),))
    res = pl.pallas_call(
        body, name=name, in_specs=[hbm] * (2 * n),
        out_specs=[sem, sem] + [hbm] * (2 * n) + [pl.BlockSpec(memory_space=pltpu.VMEM)],
        out_shape=[sems, sems] + [pltpu.HBM(a.shape, a.dtype) for a in arrs] + [pltpu.HBM(l.shape, l.dtype) for l in lands]
        + [jax.ShapeDtypeStruct((8, 128), F32)],
        input_output_aliases={a: 2 + a for a in range(2 * n)},
        compiler_params=pltpu.CompilerParams(has_side_effects=pltpu.SideEffectType.DATAFLOW_SIDE_EFFECTING),
    )(*[pltpu.with_memory_space_constraint(a, pltpu.HBM) for a in list(arrs) + lands])
    return res[0], res[1], res[2:2 + n], res[2 + n:2 + 2 * n], res[-1]


def exchange_wait(started, gather, after, name):
    send_sems, recv_sems, srcs, lands, _ = started
    n = len(srcs)

    def body(*refs):
        ins, land = refs[:n], refs[n:2 * n]
        send_sems, recv_sems = refs[2 * n], refs[2 * n + 1]
        got = refs[3 * n + 3:]
        x, y, c = lax.axis_index("x"), lax.axis_index("y"), lax.axis_index("c")
        me = 4 * x + 2 * y + c
        for k, pid, peer in _peers(x, y, c):
            for a in range(n):
                cp = pltpu.make_async_remote_copy(
                    src_ref=ins[a] if gather else ins[a].at[peer], dst_ref=land[a].at[peer],
                    send_sem=send_sems.at[a * (N_DEV - 1) + k - 1], recv_sem=recv_sems.at[a * (N_DEV - 1) + k - 1],
                    device_id=pid, device_id_type=pl.DeviceIdType.MESH)
                cp.wait_send()
                cp.wait_recv()
        for a in range(n):
            pltpu.sync_copy(ins[a] if gather else ins[a].at[me], got[a].at[me])

    hbm = pl.BlockSpec(memory_space=pltpu.HBM)
    sem = pl.BlockSpec(memory_space=pltpu.SEMAPHORE)
    res = pl.pallas_call(
        body, name=name, in_specs=[hbm] * (2 * n) + [sem, sem, ANY],
        out_specs=[hbm] * (2 * n),
        out_shape=[pltpu.HBM(a.shape, a.dtype) for a in list(srcs) + list(lands)],
        input_output_aliases={a: a for a in range(2 * n)},
        compiler_params=pltpu.CompilerParams(has_side_effects=pltpu.SideEffectType.DATAFLOW_SIDE_EFFECTING),
    )(*srcs, *lands, send_sems, recv_sems, after)
    return res[n:]


def _adamw(w, g, m, v):
    m = ADAM_B1 * m + (1.0 - ADAM_B1) * g
    v = ADAM_B2 * v + (1.0 - ADAM_B2) * (g * g)
    m_hat = m / (1.0 - ADAM_B1 ** ADAM_STEP)
    v_hat = v / (1.0 - ADAM_B2 ** ADAM_STEP)
    return -ADAM_LR * (m_hat / (jnp.sqrt(v_hat) + ADAM_EPS) + ADAM_WD * w), m, v


def _update_tile_rows(rows, cols):
    if rows * cols <= UPDATE_TILE_ELEMS:
        return rows
    return max(t for t in range(8, UPDATE_TILE_ELEMS // cols + 1, 8) if rows % t == 0)


def _sum_in_order(p_ref):
    g = p_ref[0]
    for j in range(1, p_ref.shape[0]):
        g = g + p_ref[j]
    return g


def sum_parts(parts, name):
    nparts, rows, cols = parts.shape
    tr = _update_tile_rows(rows, cols)

    def body(p_ref, g_ref):
        g_ref[...] = _sum_in_order(p_ref)

    return pl.pallas_call(
        body, name=name, grid=(rows // tr,),
        in_specs=[pl.BlockSpec((nparts, tr, cols), lambda i: (0, i, 0))],
        out_specs=pl.BlockSpec((tr, cols), lambda i: (i, 0)), out_shape=jax.ShapeDtypeStruct((rows, cols), F32),
        compiler_params=_params(),
    )(parts)


def sum_adamw(parts, w, m, v, name):
    rows, cols = w.shape
    nparts = parts.shape[0]
    tr = _update_tile_rows(rows, cols)

    def body(p_ref, w_ref, m_ref, v_ref, g_ref, d_ref, nm_ref, nv_ref):
        g = _sum_in_order(p_ref)
        delta, nm, nv = _adamw(w_ref[...], g, m_ref[...], v_ref[...])
        g_ref[...] = g
        d_ref[...] = delta
        nm_ref[...] = nm
        nv_ref[...] = nv

    blk = pl.BlockSpec((tr, cols), lambda i: (i, 0))
    sd = jax.ShapeDtypeStruct((rows, cols), F32)
    return pl.pallas_call(
        body, name=name, grid=(rows // tr,),
        in_specs=[pl.BlockSpec((nparts, tr, cols), lambda i: (0, i, 0)), blk, blk, blk],
        out_specs=[blk] * 4, out_shape=(sd,) * 4,
        compiler_params=_params(),
    )(parts, w, m, v)


S5_NAMES = ("w_in", "lam_re", "lam_im", "log_dt", "b_re", "b_im", "c_re", "c_im", "d_skip", "w_glu", "b_glu", "w_out")
CONV_NAMES = ("w_in", "conv_w", "conv_b", "w_out")
POOL_NAMES = ("w_in", "w_grp", "b_grp", "scale", "w_out")
LAYER_KINDS = ("s5", "conv", "pool", "s5")
LAYER_NAMES = {"s5": S5_NAMES, "conv": CONV_NAMES, "pool": POOL_NAMES}
SHARDED = {"s5": ("w_in", "w_glu", "w_out"), "conv": ("w_in", "conv_w", "w_out"), "pool": ("w_in", "w_grp", "b_grp", "w_out")}
GATHER_F32 = ("conv_w", "b_grp")


def weight_names():
    names = ["meta_tokens"]
    for i, kind in enumerate(LAYER_KINDS):
        names.append("norm%d_g" % i)
        names += ["l%d_%s" % (i, n) for n in LAYER_NAMES[kind]]
    names.append("final_g")
    return names


def sharded_names():
    return ["meta_tokens"] + ["l%d_%s" % (i, n) for i, kind in enumerate(LAYER_KINDS) for n in SHARDED[kind]]


def _block_diag_in(bb_t, gc):
    i, g, p = bb_t.shape
    t = bb_t.reshape(i, 4, gc, p)
    return jnp.einsum("icjp,jk->cjikp", t, jnp.eye(gc, dtype=F32)).reshape(4, gc * i, gc * p)


def _block_diag_in_grad(dbd, gc):
    i, p = dbd.shape[1] // gc, dbd.shape[2] // gc
    return jnp.einsum("cjijp->icjp", dbd.reshape(4, gc, i, gc, p)).reshape(i, 4 * gc, p)


def _block_diag_out(cc, gc):
    g, i, p = cc.shape
    return jnp.einsum("cjip,jk->cjpki", cc.reshape(4, gc, i, p), jnp.eye(gc, dtype=F32)).reshape(4, gc * p, gc * i)


def _block_diag_out_grad(dcd, gc):
    p, i = dcd.shape[1] // gc, dcd.shape[2] // gc
    return jnp.einsum("cjpji->cjip", dcd.reshape(4, gc, p, gc, i)).reshape(4 * gc, i, p)


def _to_owner_blocks(a, axis):
    shape = a.shape[:axis] + (N_DEV, a.shape[axis] // N_DEV) + a.shape[axis + 1:]
    return jnp.moveaxis(a.reshape(shape), axis, 0)


def _from_owner_blocks(a, axis):
    a = jnp.moveaxis(a, 0, axis)
    return a.reshape(a.shape[:axis] + (a.shape[axis] * a.shape[axis + 1],) + a.shape[axis + 2:])


def _step(x, target, weights, moments_m, moments_v):
    seq, d = x.shape[1], x.shape[2]
    n_meta = weights["meta_tokens"].shape[0]
    tt = TOKEN_TILE
    pad_tiles = -(-n_meta // tt)
    p0 = pad_tiles * tt
    lp = p0 + seq
    first_pos = p0 - n_meta
    gc = d // 4 // S5_GROUP
    cw = d // 4

    big_names = [n for n in sharded_names() if n != "meta_tokens" and n.split("_", 1)[1] not in GATHER_F32]
    small_names = [n for n in sharded_names() if n not in big_names]
    layer_big = [[n for n in big_names if n.startswith("l%d_" % i)] for i in range(len(LAYER_KINDS))]
    gather_started = [exchange_start([weights[n].astype(BF16) for n in names], True, "gather_start_l%d" % i)
                      for i, names in enumerate(layer_big)]
    started = sum(st[4][0, 0] for st in gather_started)
    gathered = dict(zip(small_names, exchange([weights[n] + started for n in small_names], True, "gather_small")))

    meta = _from_owner_blocks(gathered["meta_tokens"], 1)
    h = jnp.concatenate([jnp.zeros((first_pos, d), F32), meta, x[0]], axis=0)

    def vec(name):
        return weights[name].reshape(1, -1)

    full = {}

    def layer_weights(i, kind, after):
        p = "l%d_" % i
        gathered.update(zip(layer_big[i], exchange_wait(gather_started[i], True, after, "gather_wait_l%d" % i)))
        w_in = gathered[p + "w_in"]
        if kind == "s5":
            lr, li = weights[p + "lam_re"], weights[p + "lam_im"]
            ldt = weights[p + "log_dt"].reshape(-1, 1)
            br_t = jnp.transpose(weights[p + "b_re"], (2, 0, 1))
            bi_t = jnp.transpose(weights[p + "b_im"], (2, 0, 1))
            ar, ai, bbr, bbi = s5_disc_fwd(lr, li, ldt, br_t, bi_t, p + "disc_fwd")
            full[i] = dict(
                w_in=w_in, disc=(lr, li, ldt, br_t, bi_t),
                ar=ar.reshape(4, -1, 128), ai=ai.reshape(4, -1, 128),
                bdre=_block_diag_in(bbr, gc).astype(BF16), bdim=_block_diag_in(bbi, gc).astype(BF16),
                cdre=_block_diag_out(weights[p + "c_re"], gc).astype(BF16),
                cdim=_block_diag_out(-weights[p + "c_im"], gc).astype(BF16),
                w_glu=gathered[p + "w_glu"].reshape(4, cw, d), w_out=gathered[p + "w_out"].reshape(4, cw, d),
                d_skip=weights[p + "d_skip"].reshape(4, 1, cw), b_glu=vec(p + "b_glu"))
        elif kind == "conv":
            ce = w_in.shape[2]
            nch = 2
            conv_w = _from_owner_blocks(gathered[p + "conv_w"], 1)
            full[i] = dict(
                w_in=w_in, conv_w=jnp.transpose(conv_w.reshape(CONV_K, nch, ce), (1, 0, 2)),
                conv_b=weights[p + "conv_b"].reshape(nch, 1, ce), w_out=gathered[p + "w_out"].reshape(nch, ce, d))
        else:
            gw = w_in.shape[2]
            full[i] = dict(
                w_in=w_in, w_grp=_from_owner_blocks(gathered[p + "w_grp"], 1),
                b_grp=_from_owner_blocks(gathered[p + "b_grp"], 1).reshape(4, 1, gw),
                scale=weights[p + "scale"].reshape(4, 1, gw), w_out=gathered[p + "w_out"].reshape(4, gw, d))
        return full[i]

    saved = {}
    for i, kind in enumerate(LAYER_KINDS):
        p, f, g = "l%d_" % i, layer_weights(i, kind, h), vec("norm%d_g" % i)
        if kind == "s5":
            u, z, xr, xi = s5_fwd1(h, g, f["w_in"], f["bdre"], f["bdim"], p + "fwd_in")
            sr, si = s5_scan_fwd(xr, xi, f["ar"], f["ai"], p + "scan_fwd")
            saved[i] = (h, u, z, sr, si)
            h = s5_fwd3(sr, si, u, z, h, f["cdre"], f["cdim"], f["w_glu"], f["w_out"], f["d_skip"], f["b_glu"], p + "fwd_out")
        elif kind == "conv":
            h_new, halos = conv_fwd(h, g, f["w_in"], f["conv_w"], f["conv_b"], f["w_out"], p + "fwd")
            saved[i] = (h, halos)
            h = h_new
        else:
            h_new, halos = pool_fwd(h, g, f["w_in"], f["w_grp"], f["b_grp"], f["scale"], f["w_out"], first_pos, p + "fwd")
            saved[i] = (h, halos)
            h = h_new

    dh, dg_final, loss_tile = loss_head(h, target[0], vec("final_g"), pad_tiles, "loss_head")
    loss = lax.psum(loss_tile[0, 0], ("x", "y", "c"))

    grads = {"final_g": dg_final}
    def owner_blocks(a):
        return a.reshape(N_DEV, -1, a.shape[-1])

    layer_sharded, scatter_started = {}, {}
    ordered = jnp.zeros((), F32)
    for i in reversed(range(len(LAYER_KINDS))):
        kind = LAYER_KINDS[i]
        p, f, g = "l%d_" % i, full[i], vec("norm%d_g" % i) + ordered
        if kind == "s5":
            h_in, u, z, sr, si = saved[i]
            dy, dp, dwo, dwg, dbg = s5_bwd3a(dh, sr, si, u, z, f["cdre"], f["cdim"], f["w_glu"], f["w_out"],
                                             f["d_skip"], f["b_glu"], p + "bwd_out")
            dsr, dsi, dus, dcre, dcim, dd = s5_bwd3b(dy, sr, si, u, f["cdre"], f["cdim"], f["d_skip"], p + "bwd_read")
            lam_r, lam_i, dar, dai = s5_scan_bwd(dsr, dsi, sr, si, f["ar"], f["ai"], p + "scan_bwd")
            dp, dh, n, dbre, dbim, dg = s5_bwd1(lam_r, lam_i, dus, u, dp, h_in, dh, g, f["w_in"], f["bdre"], f["bdim"], p + "bwd_in")
            dw_in = grad_w_in(n, dp, f["w_in"].shape[2], p + "grad_w_in")
            lr, li, ldt, br_t, bi_t = f["disc"]
            dlr, dli, dldt, dbr_t, dbi_t = s5_disc_bwd(
                lr, li, ldt, br_t, bi_t, dar.reshape(lr.shape), dai.reshape(lr.shape),
                _block_diag_in_grad(dbre, gc), _block_diag_in_grad(dbim, gc), p + "disc_bwd")
            grads.update({
                p + "w_in": dw_in, p + "lam_re": dlr, p + "lam_im": dli, p + "log_dt": dldt,
                p + "b_re": jnp.transpose(dbr_t, (1, 2, 0)), p + "b_im": jnp.transpose(dbi_t, (1, 2, 0)),
                p + "c_re": _block_diag_out_grad(dcre, gc), p + "c_im": -_block_diag_out_grad(dcim, gc),
                p + "d_skip": dd, p + "w_glu": dwg.reshape(N_DEV, -1, d), p + "b_glu": dbg,
                p + "w_out": dwo.reshape(N_DEV, -1, d)})
        elif kind == "conv":
            h_in, halos = saved[i]
            dh, n, dp, dwo, dcw, dcb, dg = conv_bwd(h_in, dh, halos, g, f["w_in"], f["conv_w"], f["conv_b"], f["w_out"], p + "bwd")
            dw_in = grad_w_in(n, dp, f["w_in"].shape[2], p + "grad_w_in")
            dconv_w = jnp.transpose(dcw[:, :CONV_K, :], (1, 0, 2)).reshape(CONV_K, -1)
            grads.update({p + "w_in": dw_in, p + "conv_w": _to_owner_blocks(dconv_w, 1), p + "conv_b": dcb,
                          p + "w_out": dwo.reshape(N_DEV, -1, d)})
        else:
            h_in, halos = saved[i]
            dh, n, dp, dwo, dwgrp, dbgrp, dsc, dg = pool_bwd(h_in, dh, halos, g, f["w_in"], f["w_grp"], f["b_grp"], f["scale"],
                                                             f["w_out"], first_pos, p + "bwd")
            dw_in = grad_w_in(n, dp, f["w_in"].shape[2], p + "grad_w_in")
            grads.update({p + "w_in": dw_in, p + "w_grp": _to_owner_blocks(dwgrp, 1),
                          p + "b_grp": _to_owner_blocks(dbgrp.reshape(4, -1), 1), p + "scale": dsc,
                          p + "w_out": dwo.reshape(N_DEV, -1, d)})
        grads["norm%d_g" % i] = dg
        layer_sharded[i] = ["l%d_%s" % (i, n) for n in SHARDED[kind]]
        scatter_started[i] = exchange_start([owner_blocks(grads[n]) for n in layer_sharded[i]], False, "scatter_start_l%d" % i)
        ordered = scatter_started[i][4][0, 0]
    grad_x = dh[p0:][None]
    grads["meta_tokens"] = _to_owner_blocks(dh[first_pos:p0], 1)

    names = weight_names()
    sh_names = sharded_names()
    rep_names = [n for n in names if n not in sh_names]

    def as2d(a):
        return a.reshape(-1, a.shape[-1])

    def pack(tree):
        flat = [jnp.pad(tree[n].reshape(-1), (0, -tree[n].size % 1024)) for n in rep_names]
        flat = jnp.concatenate(flat)
        return jnp.pad(flat, (0, -flat.size % (PACK_ROWS * 128))).reshape(-1, 128)

    received = dict(zip(["meta_tokens", "replicated"], exchange(
        [owner_blocks(grads["meta_tokens"]), pack(grads).reshape(N_DEV, -1, 128)], False, "scatter_small")))
    for i in reversed(range(len(LAYER_KINDS))):
        received.update(zip(layer_sharded[i], exchange_wait(scatter_started[i], False, dh, "scatter_wait_l%d" % i)))
    out = {}
    for n in sh_names:
        res = sum_adamw(received[n], as2d(weights[n]), as2d(moments_m[n]), as2d(moments_v[n]), "update_" + n)
        out[n] = [r.reshape(weights[n].shape) for r in res]

    g_full = exchange([sum_parts(received["replicated"], "sum_replicated")], True, "gather_small_grads")[0].reshape(1, -1, 128)
    packed = sum_adamw(g_full, pack(weights), pack(moments_m), pack(moments_v), "update_replicated")
    offset = 0
    for n in rep_names:
        size = weights[n].size
        out[n] = [r.reshape(-1)[offset:offset + size].reshape(weights[n].shape) for r in packed]
        offset += size + (-size % 1024)

    return (loss, grad_x) + tuple(out[n][k] for k in range(4) for n in names)


def kernel(x, *rest):
    names = weight_names()
    nw = len(names)
    weights = dict(zip(names, rest[:nw]))
    target = rest[nw]
    moments_m = dict(zip(names, rest[nw + 1:2 * nw + 1]))
    moments_v = dict(zip(names, rest[2 * nw + 1:3 * nw + 1]))
    return _step(x, target, weights, moments_m, moments_v)
```

```python
import functools
import math

import jax
import jax.numpy as jnp
from jax import lax
from jax.experimental import pallas as pl
from jax.experimental.pallas import tpu as pltpu

F32 = jnp.float32
BF16 = jnp.bfloat16
EPS = 1e-6
N_DEV = 8
TOKEN_TILE = 256
SCAN_CHUNKS = 2
S5_GROUP = 16
S5_STATE = 64
POOL_WINDOWS = (2, 4, 8, 16)
POOL_HALO = 16
CONV_K = 3
CONV_HALO = 8
ADAM_LR = 0.001
ADAM_B1 = 0.9
ADAM_B2 = 0.999
ADAM_EPS = 1e-08
ADAM_WD = 0.01
ADAM_STEP = 10
GELU_C = math.sqrt(2.0 / math.pi)
GELU_A = 0.044715
UPDATE_TILE_ELEMS = 1 << 17
PACK_ROWS = 512
VMEM_LIMIT = 56 << 20

ANY = pl.BlockSpec(memory_space=pl.ANY)


def _params(vmem=VMEM_LIMIT, ndim=1):
    return pltpu.CompilerParams(vmem_limit_bytes=vmem, dimension_semantics=("arbitrary",) * ndim)


def _dot(a, b):
    return jnp.dot(a.astype(BF16), b.astype(BF16), preferred_element_type=F32)


def _dot_nt(a, b):
    return lax.dot_general(a.astype(BF16), b.astype(BF16), (((1,), (1,)), ((), ())), preferred_element_type=F32)


def _dot_tn(a, b):
    return lax.dot_general(a.astype(BF16), b.astype(BF16), (((0,), (0,)), ((), ())), preferred_element_type=F32)


def _rms_fwd(h, g):
    r = lax.rsqrt(jnp.mean(h * h, axis=-1, keepdims=True) + EPS)
    hh = h * r
    return hh * g, hh, r


def _rms_bwd(dn, hh, r, g):
    dhh = dn * g
    return r * (dhh - hh * jnp.mean(dhh * hh, axis=-1, keepdims=True))


def _sigmoid(x):
    return 1.0 / (1.0 + jnp.exp(-x))


def _silu_and_grad(z):
    s = _sigmoid(z)
    return z * s, s * (1.0 + z * (1.0 - s))


def _gelu(y):
    t = jnp.tanh(GELU_C * (y + GELU_A * y * y * y))
    return 0.5 * y * (1.0 + t), t


def _gelu_grad(y, t):
    return 0.5 * (1.0 + t) + 0.5 * y * (1.0 - t * t) * GELU_C * (1.0 + 3.0 * GELU_A * y * y)


def _rows(shape):
    return lax.broadcasted_iota(jnp.int32, shape, 0)


def _shift_down(x, k, halo):
    y = pltpu.roll(x, k, 0)
    rows = _rows(x.shape)
    for j in range(k):
        y = jnp.where(rows == j, halo[halo.shape[0] - k + j:halo.shape[0] - k + j + 1, :], y)
    return y


def _shift_up(x, k, halo):
    n = x.shape[0]
    y = pltpu.roll(x, n - k, 0)
    rows = _rows(x.shape)
    for j in range(k):
        y = jnp.where(rows == n - k + j, halo[j:j + 1, :], y)
    return y


def _window_sums_back(ext):
    out = []
    s = ext
    for k in (1, 2, 4, 8):
        s = s + pltpu.roll(s, k, 0)
        out.append(s)
    return out


def _window_sums_fwd(ext):
    n = ext.shape[0]
    out = []
    s = ext
    for k in (1, 2, 4, 8):
        s = s + pltpu.roll(s, n - k, 0)
        out.append(s)
    return out


def _pool_inv_count(tile, tt, first_pos, w, width):
    pos = _rows((tt, width)) + (tile * tt - first_pos + 1)
    return 1.0 / jnp.clip(pos, 1, w).astype(F32)


def _slab_spec(lp, tt, sw, index_map):
    nj = sw // 128
    return pl.BlockSpec((tt * nj, 128), index_map), (lp * 4 * nj, 128)


def _slab_load(ref):
    nj = ref.shape[0] // TOKEN_TILE
    return jnp.concatenate([ref[pl.ds(j, TOKEN_TILE, stride=nj), :] for j in range(nj)], axis=1)


def _slab_store(ref, val):
    nj = ref.shape[0] // TOKEN_TILE
    for j in range(nj):
        ref[pl.ds(j, TOKEN_TILE, stride=nj), :] = val[:, j * 128:(j + 1) * 128]


def _s5_disc_math(lr, li, ldt, br, bi):
    dt = jnp.exp(ldt)
    mag = jnp.exp(lr * dt)
    ar = mag * jnp.cos(li * dt)
    ai = mag * jnp.sin(li * dt)
    den = lr * lr + li * li
    kr = ((ar - 1.0) * lr + ai * li) / den
    ki = (ai * lr - (ar - 1.0) * li) / den
    bbr = kr[None] * br - ki[None] * bi
    bbi = kr[None] * bi + ki[None] * br
    return ar, ai, bbr, bbi


def s5_disc_fwd(lr, li, ldt, br_t, bi_t, name):
    def body(lr_ref, li_ref, ldt_ref, br_ref, bi_ref, ar_ref, ai_ref, bbr_ref, bbi_ref):
        ar, ai, bbr, bbi = _s5_disc_math(lr_ref[...], li_ref[...], ldt_ref[...], br_ref[...], bi_ref[...])
        ar_ref[...] = ar
        ai_ref[...] = ai
        bbr_ref[...] = bbr
        bbi_ref[...] = bbi

    sd = jax.ShapeDtypeStruct
    return pl.pallas_call(
        body, name=name,
        out_shape=(sd(lr.shape, F32), sd(lr.shape, F32), sd(br_t.shape, F32), sd(br_t.shape, F32)),
    )(lr, li, ldt, br_t, bi_t)


def s5_disc_bwd(lr, li, ldt, br_t, bi_t, dar, dai, dbbr, dbbi, name):
    def body(lr_ref, li_ref, ldt_ref, br_ref, bi_ref, dar_ref, dai_ref, dbbr_ref, dbbi_ref,
             dlr_ref, dli_ref, dldt_ref, dbr_ref, dbi_ref):
        _, vjp = jax.vjp(_s5_disc_math, lr_ref[...], li_ref[...], ldt_ref[...], br_ref[...], bi_ref[...])
        dlr, dli, dldt, dbr, dbi = vjp((dar_ref[...], dai_ref[...], dbbr_ref[...], dbbi_ref[...]))
        dlr_ref[...] = dlr
        dli_ref[...] = dli
        dldt_ref[...] = dldt
        dbr_ref[...] = dbr
        dbi_ref[...] = dbi

    sd = jax.ShapeDtypeStruct
    return pl.pallas_call(
        body, name=name,
        out_shape=(sd(lr.shape, F32), sd(lr.shape, F32), sd(ldt.shape, F32), sd(br_t.shape, F32), sd(br_t.shape, F32)),
    )(lr, li, ldt, br_t, bi_t, dar, dai, dbbr, dbbi)


def s5_fwd1(h, g, w_in, bdre, bdim, name):
    lp, d = h.shape
    tt = TOKEN_TILE
    cw, sw = bdre.shape[1], bdre.shape[2]

    def body(h_ref, g_ref, w_hbm, bdre_hbm, bdim_hbm, u_ref, z_ref, xr_ref, xi_ref, w, bre, bim, n_sc):
        i, c = pl.program_id(0), pl.program_id(1)

        @pl.when((i == 0) & (c == 0))
        def _():
            pltpu.sync_copy(w_hbm, w)
            pltpu.sync_copy(bdre_hbm, bre)
            pltpu.sync_copy(bdim_hbm, bim)

        @pl.when(c == 0)
        def _():
            n_sc[...] = _rms_fwd(h_ref[...], g_ref[...])[0].astype(BF16)

        n = n_sc[...]
        u = jnp.dot(n, w[c], preferred_element_type=F32)
        u_ref[...] = u
        z_ref[...] = jnp.dot(n, w[c + 4], preferred_element_type=F32)
        ub = u.astype(BF16)
        _slab_store(xr_ref, jnp.dot(ub, bre[c], preferred_element_type=F32))
        _slab_store(xi_ref, jnp.dot(ub, bim[c], preferred_element_type=F32))

    sd = jax.ShapeDtypeStruct
    slab, slab_shape = _slab_spec(lp, tt, sw, lambda i, c: (i * 4 + c, 0))
    return pl.pallas_call(
        body, name=name, grid=(lp // tt, 4),
        in_specs=[pl.BlockSpec((tt, d), lambda i, c: (i, 0)), pl.BlockSpec((1, d), lambda i, c: (0, 0)), ANY, ANY, ANY],
        out_specs=[pl.BlockSpec((tt, cw), lambda i, c: (i, c)), pl.BlockSpec((tt, cw), lambda i, c: (i, c)), slab, slab],
        out_shape=(sd((lp, d), F32), sd((lp, d), F32), sd(slab_shape, F32), sd(slab_shape, F32)),
        scratch_shapes=[pltpu.VMEM(w_in.shape, BF16), pltpu.VMEM(bdre.shape, BF16), pltpu.VMEM(bdim.shape, BF16),
                        pltpu.VMEM((tt, d), BF16)],
        compiler_params=_params(ndim=2),
    )(h, g, w_in, bdre, bdim)


def s5_scan_fwd(xr, xi, ar, ai, name):
    nj = ar.shape[1]
    tt = TOKEN_TILE
    cpb = SCAN_CHUNKS
    nt = xr.shape[0] // (4 * tt * nj)

    def body(xr_ref, xi_ref, ar_ref, ai_ref, sr_ref, si_ref, st_r, st_i):
        i, cg = pl.program_id(0), pl.program_id(1)

        @pl.when(i == 0)
        def _():
            for q in range(cpb):
                st_r[cg * cpb + q] = jnp.zeros((nj, 128), F32)
                st_i[cg * cpb + q] = jnp.zeros((nj, 128), F32)

        a_r = [ar_ref[cg * cpb + q] for q in range(cpb)]
        a_i = [ai_ref[cg * cpb + q] for q in range(cpb)]

        def step(t, carry):
            out = []
            for q in range(cpb):
                s_r, s_i = carry[q]
                rows = pl.ds(pl.multiple_of((q * tt + t) * nj, nj), nj)
                n_r = a_r[q] * s_r - a_i[q] * s_i + xr_ref[rows, :]
                n_i = a_r[q] * s_i + a_i[q] * s_r + xi_ref[rows, :]
                sr_ref[rows, :] = n_r
                si_ref[rows, :] = n_i
                out.append((n_r, n_i))
            return tuple(out)

        init = tuple((st_r[cg * cpb + q], st_i[cg * cpb + q]) for q in range(cpb))
        final = lax.fori_loop(0, tt, step, init, unroll=8)
        for q in range(cpb):
            st_r[cg * cpb + q] = final[q][0]
            st_i[cg * cpb + q] = final[q][1]

    blk = pl.BlockSpec((cpb * tt * nj, 128), lambda i, cg: (i * (4 // cpb) + cg, 0))
    par = pl.BlockSpec((4, nj, 128), lambda i, cg: (0, 0, 0))
    sd = jax.ShapeDtypeStruct
    return pl.pallas_call(
        body, name=name, grid=(nt, 4 // cpb),
        in_specs=[blk, blk, par, par], out_specs=[blk, blk],
        out_shape=(sd(xr.shape, F32), sd(xr.shape, F32)),
        scratch_shapes=[pltpu.VMEM((4, nj, 128), F32), pltpu.VMEM((4, nj, 128), F32)],
        compiler_params=_params(ndim=2),
    )(xr, xi, ar, ai)


def _s5_mix_fwd(sr_ref, si_ref, u_ref, d_ref, cre, cim, c):
    y = _dot(_slab_load(sr_ref), cre[c]) + _dot(_slab_load(si_ref), cim[c]) + d_ref[c] * u_ref[...]
    gy, t = _gelu(y)
    return y, gy, t


def s5_fwd3(sr, si, u, z, h, cdre, cdim, w_glu, w_out, d_skip, b_glu, name):
    lp, d = h.shape
    tt = TOKEN_TILE
    sw, cw = cdre.shape[1], cdre.shape[2]

    def body(sr_ref, si_ref, u_ref, z_ref, h_ref, d_ref, bg_ref, cre_hbm, cim_hbm, wg_hbm, wo_hbm,
             o_ref, cre, cim, wg, wo, gy_sc, q_sc):
        i, c = pl.program_id(0), pl.program_id(1)

        @pl.when((i == 0) & (c == 0))
        def _():
            pltpu.sync_copy(cre_hbm, cre)
            pltpu.sync_copy(cim_hbm, cim)
            pltpu.sync_copy(wg_hbm, wg)
            pltpu.sync_copy(wo_hbm, wo)

        _, gy, _ = _s5_mix_fwd(sr_ref, si_ref, u_ref, d_ref, cre, cim, c)
        gy_sc[c] = gy
        part = _dot(gy, wg[c])

        @pl.when(c == 0)
        def _():
            q_sc[...] = part

        @pl.when(c > 0)
        def _():
            q_sc[...] += part

        @pl.when(c == 3)
        def _():
            sig = _sigmoid(q_sc[...] + bg_ref[...])
            zz = z_ref[...]
            sz = zz * _sigmoid(zz)
            o = h_ref[...]
            for k in range(4):
                cols = slice(k * cw, (k + 1) * cw)
                o = o + _dot(gy_sc[k] * sig[:, cols] * sz[:, cols], wo[k])
            o_ref[...] = o

    row = lambda i, c: (i, 0)
    chunk = lambda i, c: (i, c)
    slab, _ = _slab_spec(lp, tt, sw, lambda i, c: (i * 4 + c, 0))
    return pl.pallas_call(
        body, name=name, grid=(lp // tt, 4),
        in_specs=[slab, slab, pl.BlockSpec((tt, cw), chunk),
                  pl.BlockSpec((tt, d), row), pl.BlockSpec((tt, d), row),
                  pl.BlockSpec((4, 1, cw), lambda i, c: (0, 0, 0)), pl.BlockSpec((1, d), lambda i, c: (0, 0)),
                  ANY, ANY, ANY, ANY],
        out_specs=pl.BlockSpec((tt, d), row),
        out_shape=jax.ShapeDtypeStruct((lp, d), F32),
        scratch_shapes=[pltpu.VMEM(cdre.shape, BF16), pltpu.VMEM(cdim.shape, BF16), pltpu.VMEM(w_glu.shape, BF16),
                        pltpu.VMEM(w_out.shape, BF16), pltpu.VMEM((4, tt, cw), F32), pltpu.VMEM((tt, d), F32)],
        compiler_params=_params(ndim=2),
    )(sr, si, u, z, h, d_skip, b_glu, cdre, cdim, w_glu, w_out)


def s5_bwd3a(dh, sr, si, u, z, cdre, cdim, w_glu, w_out, d_skip, b_glu, name):
    lp, d = dh.shape
    tt = TOKEN_TILE
    nt = lp // tt
    sw, cw = cdre.shape[1], cdre.shape[2]

    def body(dh_ref, sr_ref, si_ref, u_ref, z_ref, d_ref, bg_ref, cre_hbm, cim_hbm, wg_hbm, wo_hbm,
             dy_ref, dp_ref, dwo_hbm, dwg_hbm, dbg_hbm,
             cre, cim, wg, wo, y_sc, t_sc, gy_sc, q_sc, dq_sc, dgy_sc, dwo, dwg, dbg):
        i, c = pl.program_id(0), pl.program_id(1)

        @pl.when((i == 0) & (c == 0))
        def _():
            pltpu.sync_copy(cre_hbm, cre)
            pltpu.sync_copy(cim_hbm, cim)
            pltpu.sync_copy(wg_hbm, wg)
            pltpu.sync_copy(wo_hbm, wo)
            dwo[...] = jnp.zeros_like(dwo)
            dwg[...] = jnp.zeros_like(dwg)
            dbg[...] = jnp.zeros_like(dbg)

        y, gy, t = _s5_mix_fwd(sr_ref, si_ref, u_ref, d_ref, cre, cim, c)
        y_sc[c] = y
        t_sc[c] = t
        gy_sc[c] = gy
        part = _dot(gy, wg[c])

        @pl.when(c == 0)
        def _():
            q_sc[...] = part

        @pl.when(c > 0)
        def _():
            q_sc[...] += part

        @pl.when(c == 3)
        def _():
            sig = _sigmoid(q_sc[...] + bg_ref[...])
            sz, dsz = _silu_and_grad(z_ref[...])
            dhv = dh_ref[...]
            for k in range(4):
                cols = slice(k * cw, (k + 1) * cw)
                gy_k, sig_k, sz_k = gy_sc[k], sig[:, cols], sz[:, cols]
                y2 = gy_k * sig_k
                dy3 = _dot_nt(dhv, wo[k])
                dwo[k] += _dot_tn(y2 * sz_k, dhv)
                dy2 = dy3 * sz_k
                dp_ref[0, :, cols] = (dy3 * y2 * dsz[:, cols]).astype(BF16)
                dq_sc[:, cols] = dy2 * gy_k * sig_k * (1.0 - sig_k)
                dgy_sc[k] = dy2 * sig_k
            dq = dq_sc[...]
            dbg[...] += jnp.sum(dq, axis=0, keepdims=True)
            for k in range(4):
                cols = slice(k * cw, (k + 1) * cw)
                dwg[k] += _dot_tn(gy_sc[k], dq)
                dgy = dgy_sc[k] + _dot_nt(dq, wg[k])
                dy_ref[:, cols] = dgy * _gelu_grad(y_sc[k], t_sc[k])

        @pl.when((i == nt - 1) & (c == 3))
        def _():
            pltpu.sync_copy(dwo, dwo_hbm)
            pltpu.sync_copy(dwg, dwg_hbm)
            pltpu.sync_copy(dbg, dbg_hbm)

    row = lambda i, c: (i, 0)
    chunk = lambda i, c: (i, c)
    sd = jax.ShapeDtypeStruct
    acc = pltpu.VMEM((4, tt, cw), F32)
    slab, _ = _slab_spec(lp, tt, sw, lambda i, c: (i * 4 + c, 0))
    return pl.pallas_call(
        body, name=name, grid=(nt, 4),
        in_specs=[pl.BlockSpec((tt, d), row), slab, slab,
                  pl.BlockSpec((tt, cw), chunk), pl.BlockSpec((tt, d), row),
                  pl.BlockSpec((4, 1, cw), lambda i, c: (0, 0, 0)), pl.BlockSpec((1, d), lambda i, c: (0, 0)),
                  ANY, ANY, ANY, ANY],
        out_specs=[pl.BlockSpec((tt, d), row), pl.BlockSpec((1, tt, d), lambda i, c: (1, i, 0)), ANY, ANY, ANY],
        out_shape=(sd((lp, d), F32), sd((2, lp, d), BF16), sd(w_out.shape, F32), sd(w_glu.shape, F32), sd((1, d), F32)),
        scratch_shapes=[pltpu.VMEM(cdre.shape, BF16), pltpu.VMEM(cdim.shape, BF16), pltpu.VMEM(w_glu.shape, BF16),
                        pltpu.VMEM(w_out.shape, BF16), acc, acc, acc, pltpu.VMEM((tt, d), F32), pltpu.VMEM((tt, d), F32), acc,
                        pltpu.VMEM(w_out.shape, F32), pltpu.VMEM(w_glu.shape, F32), pltpu.VMEM((1, d), F32)],
        compiler_params=_params(ndim=2),
    )(dh, sr, si, u, z, d_skip, b_glu, cdre, cdim, w_glu, w_out)


def s5_bwd3b(dy, sr, si, u, cdre, cdim, d_skip, name):
    lp, d = dy.shape
    tt = TOKEN_TILE
    nt = lp // tt
    sw, cw = cdre.shape[1], cdre.shape[2]

    def body(dy_ref, sr_ref, si_ref, u_ref, d_ref, cre_hbm, cim_hbm,
             dsr_ref, dsi_ref, dus_ref, dcre_hbm, dcim_hbm, dd_hbm, cre, cim, dcre, dcim, dd):
        i, c = pl.program_id(0), pl.program_id(1)

        @pl.when((i == 0) & (c == 0))
        def _():
            pltpu.sync_copy(cre_hbm, cre)
            pltpu.sync_copy(cim_hbm, cim)
            dcre[...] = jnp.zeros_like(dcre)
            dcim[...] = jnp.zeros_like(dcim)
            dd[...] = jnp.zeros_like(dd)

        dyv = dy_ref[...]
        dd[c] += jnp.sum(dyv * u_ref[...], axis=0, keepdims=True)
        dus_ref[...] = dyv * d_ref[c]
        _slab_store(dsr_ref, _dot_nt(dyv, cre[c]))
        _slab_store(dsi_ref, _dot_nt(dyv, cim[c]))
        dcre[c] += _dot_tn(_slab_load(sr_ref), dyv)
        dcim[c] += _dot_tn(_slab_load(si_ref), dyv)

        @pl.when((i == nt - 1) & (c == 3))
        def _():
            pltpu.sync_copy(dcre, dcre_hbm)
            pltpu.sync_copy(dcim, dcim_hbm)
            pltpu.sync_copy(dd, dd_hbm)

    chunk = lambda i, c: (i, c)
    sd = jax.ShapeDtypeStruct
    slab, slab_shape = _slab_spec(lp, tt, sw, lambda i, c: (i * 4 + c, 0))
    return pl.pallas_call(
        body, name=name, grid=(nt, 4),
        in_specs=[pl.BlockSpec((tt, cw), chunk), slab, slab,
                  pl.BlockSpec((tt, cw), chunk), pl.BlockSpec((4, 1, cw), lambda i, c: (0, 0, 0)), ANY, ANY],
        out_specs=[slab, slab, pl.BlockSpec((tt, cw), chunk), ANY, ANY, ANY],
        out_shape=(sd(slab_shape, F32), sd(slab_shape, F32), sd((lp, d), F32),
                   sd(cdre.shape, F32), sd(cdim.shape, F32), sd((4, 1, cw), F32)),
        scratch_shapes=[pltpu.VMEM(cdre.shape, BF16), pltpu.VMEM(cdim.shape, BF16),
                        pltpu.VMEM(cdre.shape, F32), pltpu.VMEM(cdim.shape, F32), pltpu.VMEM((4, 1, cw), F32)],
        compiler_params=_params(ndim=2),
    )(dy, sr, si, u, d_skip, cdre, cdim)


def s5_scan_bwd(gr, gi, sr, si, ar, ai, name):
    nj = ar.shape[1]
    tt = TOKEN_TILE
    cpb = SCAN_CHUNKS
    nt = gr.shape[0] // (4 * tt * nj)

    def body(gr_ref, gi_ref, sr_ref, si_ref, ar_ref, ai_ref, lr_ref, li_ref, dar_ref, dai_ref, st_r, st_i, acc_r, acc_i):
        i, cg = pl.program_id(0), pl.program_id(1)

        @pl.when((i == 0) & (cg == 0))
        def _():
            for ref in (st_r, st_i, acc_r, acc_i):
                ref[...] = jnp.zeros_like(ref)

        a_r = [ar_ref[cg * cpb + q] for q in range(cpb)]
        a_i = [ai_ref[cg * cpb + q] for q in range(cpb)]

        def slab(q, t):
            return pl.ds(pl.multiple_of((q * tt + t) * nj, nj), nj)

        def adjoint(q, t, l_r, l_i):
            rows = slab(q, t)
            n_r = gr_ref[rows, :] + a_r[q] * l_r + a_i[q] * l_i
            n_i = gi_ref[rows, :] + a_r[q] * l_i - a_i[q] * l_r
            lr_ref[rows, :] = n_r
            li_ref[rows, :] = n_i
            return n_r, n_i

        def pair(q, t, l_r, l_i, d_r, d_i):
            rows = slab(q, t)
            p_r, p_i = sr_ref[rows, :], si_ref[rows, :]
            return d_r + l_r * p_r + l_i * p_i, d_i + l_i * p_r - l_r * p_i

        def step(k, carry):
            t = tt - 1 - k
            out = []
            for q in range(cpb):
                l_r, l_i, d_r, d_i = carry[q]
                l_r, l_i = adjoint(q, t, l_r, l_i)
                d_r, d_i = pair(q, t - 1, l_r, l_i, d_r, d_i)
                out.append((l_r, l_i, d_r, d_i))
            return tuple(out)

        init = []
        for q in range(cpb):
            ch = cg * cpb + q
            l_r, l_i = st_r[ch], st_i[ch]
            d_r, d_i = pair(q, tt - 1, l_r, l_i, acc_r[ch], acc_i[ch])
            init.append((l_r, l_i, d_r, d_i))
        final = lax.fori_loop(0, tt - 1, step, tuple(init), unroll=8)
        for q in range(cpb):
            ch = cg * cpb + q
            l_r, l_i, d_r, d_i = final[q]
            l_r, l_i = adjoint(q, 0, l_r, l_i)
            st_r[ch] = l_r
            st_i[ch] = l_i
            acc_r[ch] = d_r
            acc_i[ch] = d_i
            dar_ref[ch] = d_r
            dai_ref[ch] = d_i

    blk = pl.BlockSpec((cpb * tt * nj, 128), lambda i, cg: ((nt - 1 - i) * (4 // cpb) + cg, 0))
    par = pl.BlockSpec((4, nj, 128), lambda i, cg: (0, 0, 0))
    sd = jax.ShapeDtypeStruct
    return pl.pallas_call(
        body, name=name, grid=(nt, 4 // cpb),
        in_specs=[blk, blk, blk, blk, par, par], out_specs=[blk, blk, par, par],
        out_shape=(sd(gr.shape, F32), sd(gr.shape, F32), sd((4, nj, 128), F32), sd((4, nj, 128), F32)),
        scratch_shapes=[pltpu.VMEM((4, nj, 128), F32)] * 4,
        compiler_params=_params(ndim=2),
    )(gr, gi, sr, si, ar, ai)


def s5_bwd1(lam_r, lam_i, dus, u, dp, h, dh, g, w_in, bdre, bdim, name):
    lp, d = h.shape
    tt = TOKEN_TILE
    nt = lp // tt
    cw, sw = bdre.shape[1], bdre.shape[2]

    def body(lr_ref, li_ref, dus_ref, u_ref, dpz_ref, h_ref, dh_ref, g_ref, w_hbm, bre_hbm, bim_hbm,
             dpu_ref, dho_ref, n_ref, dbre_hbm, dbim_hbm, dg_hbm, w, bre, bim, dn_sc, dbre, dbim, dg):
        i, c = pl.program_id(0), pl.program_id(1)

        @pl.when((i == 0) & (c == 0))
        def _():
            pltpu.sync_copy(w_hbm, w)
            pltpu.sync_copy(bre_hbm, bre)
            pltpu.sync_copy(bim_hbm, bim)
            dbre[...] = jnp.zeros_like(dbre)
            dbim[...] = jnp.zeros_like(dbim)
            dg[...] = jnp.zeros_like(dg)

        l_r, l_i, uv = _slab_load(lr_ref), _slab_load(li_ref), u_ref[...]
        du = dus_ref[...] + _dot_nt(l_r, bre[c]) + _dot_nt(l_i, bim[c])
        dbre[c] += _dot_tn(uv, l_r)
        dbim[c] += _dot_tn(uv, l_i)
        dpu_ref[0] = du.astype(BF16)
        part = _dot_nt(du, w[c])

        @pl.when(c == 0)
        def _():
            dn_sc[...] = part

        @pl.when(c > 0)
        def _():
            dn_sc[...] += part

        @pl.when(c == 3)
        def _():
            dz = dpz_ref[0]
            dn = dn_sc[...]
            for k in range(4):
                dn = dn + _dot_nt(dz[:, k * cw:(k + 1) * cw], w[4 + k])
            gv = g_ref[...]
            n, hh, rr = _rms_fwd(h_ref[...], gv)
            n_ref[...] = n.astype(BF16)
            dg[...] += jnp.sum(dn * hh, axis=0, keepdims=True)
            dho_ref[...] = dh_ref[...] + _rms_bwd(dn, hh, rr, gv)

        @pl.when((i == nt - 1) & (c == 3))
        def _():
            pltpu.sync_copy(dbre, dbre_hbm)
            pltpu.sync_copy(dbim, dbim_hbm)
            pltpu.sync_copy(dg, dg_hbm)

    row = lambda i, c: (i, 0)
    chunk = lambda i, c: (i, c)
    sd = jax.ShapeDtypeStruct
    slab, _ = _slab_spec(lp, tt, sw, lambda i, c: (i * 4 + c, 0))
    return pl.pallas_call(
        body, name=name, grid=(nt, 4),
        in_specs=[slab, slab, pl.BlockSpec((tt, cw), chunk),
                  pl.BlockSpec((tt, cw), chunk), pl.BlockSpec((1, tt, d), lambda i, c: (1, i, 0)),
                  pl.BlockSpec((tt, d), row), pl.BlockSpec((tt, d), row), pl.BlockSpec((1, d), lambda i, c: (0, 0)),
                  ANY, ANY, ANY],
        out_specs=[pl.BlockSpec((1, tt, cw), lambda i, c: (0, i, c)), pl.BlockSpec((tt, d), row), pl.BlockSpec((tt, d), row),
                   ANY, ANY, ANY],
        out_shape=(sd(dp.shape, BF16), sd((lp, d), F32), sd((lp, d), BF16),
                   sd(bdre.shape, F32), sd(bdim.shape, F32), sd((1, d), F32)),
        input_output_aliases={4: 0},
        scratch_shapes=[pltpu.VMEM(w_in.shape, BF16), pltpu.VMEM(bdre.shape, BF16), pltpu.VMEM(bdim.shape, BF16),
                        pltpu.VMEM((tt, d), F32), pltpu.VMEM(bdre.shape, F32), pltpu.VMEM(bdim.shape, F32), pltpu.VMEM((1, d), F32)],
        compiler_params=_params(ndim=2),
    )(lam_r, lam_i, dus, u, dp, h, dh, g, w_in, bdre, bdim)


def grad_w_in(n, dp, blk, name):
    lp, d = n.shape
    npart, _, width = dp.shape
    tt = TOKEN_TILE
    per = width // blk

    def body(n_ref, dp_ref, o_ref):
        part = _dot_tn(n_ref[...], dp_ref[0])

        @pl.when(pl.program_id(1) == 0)
        def _():
            o_ref[0] = part

        @pl.when(pl.program_id(1) > 0)
        def _():
            o_ref[0] += part

    return pl.pallas_call(
        body, name=name, grid=(npart * per, lp // tt),
        in_specs=[pl.BlockSpec((tt, d), lambda j, i: (i, 0)), pl.BlockSpec((1, tt, blk), lambda j, i: (j // per, i, j % per))],
        out_specs=pl.BlockSpec((1, d, blk), lambda j, i: (j, 0, 0)),
        out_shape=jax.ShapeDtypeStruct((npart * per, d, blk), F32),
        compiler_params=_params(ndim=2),
    )(n, dp)


def _conv_fwd_chunk(n, w, cw_ref, cb_ref, halo, c, nch):
    bg = jnp.dot(n, w[c], preferred_element_type=F32)
    cg = jnp.dot(n, w[nch + c], preferred_element_type=F32)
    v = jnp.dot(n, w[2 * nch + c], preferred_element_type=F32)
    z = jnp.dot(n, w[3 * nch + c], preferred_element_type=F32)
    hc = cg * v
    taps = cw_ref[c]
    conv = taps[2:3, :] * hc + taps[1:2, :] * _shift_down(hc, 1, halo) + taps[0:1, :] * _shift_down(hc, 2, halo) + cb_ref[c]
    return bg, cg, v, z, hc, conv


def conv_fwd(h, g, w_in, conv_w, conv_b, w_out, name):
    lp, d = h.shape
    tt = TOKEN_TILE
    nt = lp // tt
    nch, ce = w_out.shape[0], w_out.shape[1]

    def body(h_ref, g_ref, cw_ref, cb_ref, w_hbm, wo_hbm, o_ref, halo_ref, w, wo, halo):
        i = pl.program_id(0)

        @pl.when(i == 0)
        def _():
            pltpu.sync_copy(w_hbm, w)
            pltpu.sync_copy(wo_hbm, wo)
            halo[...] = jnp.zeros_like(halo)

        hv = h_ref[...]
        n = _rms_fwd(hv, g_ref[...])[0].astype(BF16)
        o = hv
        for c in range(nch):
            bg, _, _, z, hc, conv = _conv_fwd_chunk(n, w, cw_ref, cb_ref, halo[c], c, nch)
            o = o + _dot(bg * conv * (z * _sigmoid(z)), wo[c])
            halo[c] = hc[tt - CONV_HALO:, :]
            halo_ref[0, c] = hc[tt - CONV_HALO:, :]
        o_ref[...] = o

    sd = jax.ShapeDtypeStruct
    return pl.pallas_call(
        body, name=name, grid=(nt,),
        in_specs=[pl.BlockSpec((tt, d), lambda i: (i, 0)), pl.BlockSpec((1, d), lambda i: (0, 0)),
                  pl.BlockSpec(conv_w.shape, lambda i: (0, 0, 0)), pl.BlockSpec(conv_b.shape, lambda i: (0, 0, 0)), ANY, ANY],
        out_specs=[pl.BlockSpec((tt, d), lambda i: (i, 0)), pl.BlockSpec((1, nch, CONV_HALO, ce), lambda i: (i, 0, 0, 0))],
        out_shape=(sd((lp, d), F32), sd((nt, nch, CONV_HALO, ce), F32)),
        scratch_shapes=[pltpu.VMEM(w_in.shape, BF16), pltpu.VMEM(w_out.shape, BF16), pltpu.VMEM((nch, CONV_HALO, ce), F32)],
        compiler_params=_params(),
    )(h, g, conv_w, conv_b, w_in, w_out)


def conv_bwd(h, dh, halos, g, w_in, conv_w, conv_b, w_out, name):
    lp, d = h.shape
    tt = TOKEN_TILE
    nt = lp // tt
    nch, ce = w_out.shape[0], w_out.shape[1]

    def body(h_ref, dh_ref, halo_ref, g_ref, cw_ref, cb_ref, w_hbm, wo_hbm,
             dho_ref, n_ref, dp_ref, dwo_hbm, dcw_hbm, dcb_hbm, dg_hbm, w, wo, nxt, dwo, dcw, dcb, dg):
        i = pl.program_id(0)

        @pl.when(i == 0)
        def _():
            pltpu.sync_copy(w_hbm, w)
            pltpu.sync_copy(wo_hbm, wo)
            for ref in (nxt, dwo, dcw, dcb, dg):
                ref[...] = jnp.zeros_like(ref)

        gv = g_ref[...]
        nf, hh, rr = _rms_fwd(h_ref[...], gv)
        n = nf.astype(BF16)
        n_ref[...] = n
        dhv = dh_ref[...]
        has_prev = (i < nt - 1).astype(F32)
        dn = jnp.zeros((tt, d), F32)
        for c in range(nch):
            halo = halo_ref[0, c] * has_prev
            bg, cg, v, z, hc, conv = _conv_fwd_chunk(n, w, cw_ref, cb_ref, halo, c, nch)
            sz, dsz = _silu_and_grad(z)
            y1 = bg * conv
            dy2 = _dot_nt(dhv, wo[c])
            dwo[c] += _dot_tn(y1 * sz, dhv)
            dy1 = dy2 * sz
            dz = dy2 * y1 * dsz
            dbg = dy1 * conv
            dconv = dy1 * bg
            dcb[c] += jnp.sum(dconv, axis=0, keepdims=True)
            up1 = _shift_up(dconv, 1, nxt[c])
            up2 = _shift_up(dconv, 2, nxt[c])
            nxt[c] = dconv[:CONV_HALO, :]
            taps = cw_ref[c]
            dhc = taps[2:3, :] * dconv + taps[1:2, :] * up1 + taps[0:1, :] * up2
            dcw[c, 0:1, :] += jnp.sum(hc * up2, axis=0, keepdims=True)
            dcw[c, 1:2, :] += jnp.sum(hc * up1, axis=0, keepdims=True)
            dcw[c, 2:3, :] += jnp.sum(hc * dconv, axis=0, keepdims=True)
            dcg = dhc * v
            dv = dhc * cg
            cols = slice(c * ce, (c + 1) * ce)
            for p, val in enumerate((dbg, dcg, dv, dz)):
                dp_ref[p, :, cols] = val.astype(BF16)
                dn = dn + _dot_nt(val, w[p * nch + c])
        dg[...] += jnp.sum(dn * hh, axis=0, keepdims=True)
        dho_ref[...] = dhv + _rms_bwd(dn, hh, rr, gv)

        @pl.when(i == nt - 1)
        def _():
            pltpu.sync_copy(dwo, dwo_hbm)
            pltpu.sync_copy(dcw, dcw_hbm)
            pltpu.sync_copy(dcb, dcb_hbm)
            pltpu.sync_copy(dg, dg_hbm)

    rev = lambda i: (nt - 1 - i, 0)
    sd = jax.ShapeDtypeStruct
    return pl.pallas_call(
        body, name=name, grid=(nt,),
        in_specs=[pl.BlockSpec((tt, d), rev), pl.BlockSpec((tt, d), rev),
                  pl.BlockSpec((1, nch, CONV_HALO, ce), lambda i: (jnp.maximum(nt - 2 - i, 0), 0, 0, 0)),
                  pl.BlockSpec((1, d), lambda i: (0, 0)),
                  pl.BlockSpec(conv_w.shape, lambda i: (0, 0, 0)), pl.BlockSpec(conv_b.shape, lambda i: (0, 0, 0)), ANY, ANY],
        out_specs=[pl.BlockSpec((tt, d), rev), pl.BlockSpec((tt, d), rev),
                   pl.BlockSpec((4, tt, nch * ce), lambda i: (0, nt - 1 - i, 0)), ANY, ANY, ANY, ANY],
        out_shape=(sd((lp, d), F32), sd((lp, d), BF16), sd((4, lp, nch * ce), BF16),
                   sd(w_out.shape, F32), sd((nch, 8, ce), F32), sd((nch, 1, ce), F32), sd((1, d), F32)),
        scratch_shapes=[pltpu.VMEM(w_in.shape, BF16), pltpu.VMEM(w_out.shape, BF16), pltpu.VMEM((nch, CONV_HALO, ce), F32),
                        pltpu.VMEM(w_out.shape, F32), pltpu.VMEM((nch, 8, ce), F32), pltpu.VMEM((nch, 1, ce), F32),
                        pltpu.VMEM((1, d), F32)],
        compiler_params=_params(),
    )(h, dh, halos, g, conv_w, conv_b, w_in, w_out)


def _pool_fwd_group(n, w, wg, bg_ref, sc_ref, halo, k, tile, tt, first_pos):
    u = jnp.dot(n, w[k], preferred_element_type=F32)
    z = jnp.dot(n, w[4 + k], preferred_element_type=F32)
    ext = jnp.concatenate([halo, u], axis=0)
    win = _window_sums_back(ext)[k][POOL_HALO:, :]
    mixed = win * _pool_inv_count(tile, tt, first_pos, POOL_WINDOWS[k], u.shape[1]) - u
    outs = _dot(mixed, wg[k]) + bg_ref[k]
    return u, z, mixed, outs, outs * sc_ref[k]


def pool_fwd(h, g, w_in, w_grp, b_grp, scale, w_out, first_pos, name):
    lp, d = h.shape
    tt = TOKEN_TILE
    nt = lp // tt
    gw = w_grp.shape[1]

    def body(h_ref, g_ref, bg_ref, sc_ref, w_hbm, wg_hbm, wo_hbm, o_ref, halo_ref, w, wg, wo, halo):
        i = pl.program_id(0)

        @pl.when(i == 0)
        def _():
            pltpu.sync_copy(w_hbm, w)
            pltpu.sync_copy(wg_hbm, wg)
            pltpu.sync_copy(wo_hbm, wo)
            halo[...] = jnp.zeros_like(halo)

        hv = h_ref[...]
        n = _rms_fwd(hv, g_ref[...])[0].astype(BF16)
        o = hv
        for k in range(4):
            u, z, _, _, yp = _pool_fwd_group(n, w, wg, bg_ref, sc_ref, halo[k], k, i, tt, first_pos)
            o = o + _dot(yp * (z * _sigmoid(z)), wo[k])
            halo[k] = u[tt - POOL_HALO:, :]
            halo_ref[0, k] = u[tt - POOL_HALO:, :]
        o_ref[...] = o

    sd = jax.ShapeDtypeStruct
    small = pl.BlockSpec((4, 1, gw), lambda i: (0, 0, 0))
    return pl.pallas_call(
        body, name=name, grid=(nt,),
        in_specs=[pl.BlockSpec((tt, d), lambda i: (i, 0)), pl.BlockSpec((1, d), lambda i: (0, 0)), small, small, ANY, ANY, ANY],
        out_specs=[pl.BlockSpec((tt, d), lambda i: (i, 0)), pl.BlockSpec((1, 4, POOL_HALO, gw), lambda i: (i, 0, 0, 0))],
        out_shape=(sd((lp, d), F32), sd((nt, 4, POOL_HALO, gw), F32)),
        scratch_shapes=[pltpu.VMEM(w_in.shape, BF16), pltpu.VMEM(w_grp.shape, BF16), pltpu.VMEM(w_out.shape, BF16),
                        pltpu.VMEM((4, POOL_HALO, gw), F32)],
        compiler_params=_params(),
    )(h, g, b_grp, scale, w_in, w_grp, w_out)


def pool_bwd(h, dh, halos, g, w_in, w_grp, b_grp, scale, w_out, first_pos, name):
    lp, d = h.shape
    tt = TOKEN_TILE
    nt = lp // tt
    gw = w_grp.shape[1]

    def body(h_ref, dh_ref, halo_ref, g_ref, bg_ref, sc_ref, w_hbm, wg_hbm, wo_hbm,
             dho_ref, n_ref, dp_ref, dwo_hbm, dwg_hbm, dbg_hbm, dsc_hbm, dg_hbm,
             w, wg, wo, nxt, dwo, dwg, dbg, dsc, dg):
        i = pl.program_id(0)
        tile = nt - 1 - i

        @pl.when(i == 0)
        def _():
            pltpu.sync_copy(w_hbm, w)
            pltpu.sync_copy(wg_hbm, wg)
            pltpu.sync_copy(wo_hbm, wo)
            for ref in (nxt, dwo, dwg, dbg, dsc, dg):
                ref[...] = jnp.zeros_like(ref)

        gv = g_ref[...]
        nf, hh, rr = _rms_fwd(h_ref[...], gv)
        n = nf.astype(BF16)
        n_ref[...] = n
        dhv = dh_ref[...]
        has_prev = (i < nt - 1).astype(F32)
        dn = jnp.zeros((tt, d), F32)
        for k in range(4):
            u, z, mixed, outs, yp = _pool_fwd_group(n, w, wg, bg_ref, sc_ref, halo_ref[0, k] * has_prev, k, tile, tt, first_pos)
            sz, dsz = _silu_and_grad(z)
            dy = _dot_nt(dhv, wo[k])
            dwo[k] += _dot_tn(yp * sz, dhv)
            dyp = dy * sz
            dz = dy * yp * dsz
            dsc[k] += jnp.sum(dyp * outs, axis=0, keepdims=True)
            douts = dyp * sc_ref[k]
            dbg[k] += jnp.sum(douts, axis=0, keepdims=True)
            dwg[k] += _dot_tn(mixed, douts)
            dmixed = _dot_nt(douts, wg[k])
            dm = dmixed * _pool_inv_count(tile, tt, first_pos, POOL_WINDOWS[k], gw)
            ext = jnp.concatenate([dm, nxt[k]], axis=0)
            du = _window_sums_fwd(ext)[k][:tt, :] - dmixed
            nxt[k] = dm[:POOL_HALO, :]
            cols = slice(k * gw, (k + 1) * gw)
            dp_ref[0, :, cols] = du.astype(BF16)
            dp_ref[1, :, cols] = dz.astype(BF16)
            dn = dn + _dot_nt(du, w[k]) + _dot_nt(dz, w[4 + k])
        dg[...] += jnp.sum(dn * hh, axis=0, keepdims=True)
        dho_ref[...] = dhv + _rms_bwd(dn, hh, rr, gv)

        @pl.when(i == nt - 1)
        def _():
            pltpu.sync_copy(dwo, dwo_hbm)
            pltpu.sync_copy(dwg, dwg_hbm)
            pltpu.sync_copy(dbg, dbg_hbm)
            pltpu.sync_copy(dsc, dsc_hbm)
            pltpu.sync_copy(dg, dg_hbm)

    rev = lambda i: (nt - 1 - i, 0)
    sd = jax.ShapeDtypeStruct
    small = pl.BlockSpec((4, 1, gw), lambda i: (0, 0, 0))
    return pl.pallas_call(
        body, name=name, grid=(nt,),
        in_specs=[pl.BlockSpec((tt, d), rev), pl.BlockSpec((tt, d), rev),
                  pl.BlockSpec((1, 4, POOL_HALO, gw), lambda i: (jnp.maximum(nt - 2 - i, 0), 0, 0, 0)),
                  pl.BlockSpec((1, d), lambda i: (0, 0)), small, small, ANY, ANY, ANY],
        out_specs=[pl.BlockSpec((tt, d), rev), pl.BlockSpec((tt, d), rev),
                   pl.BlockSpec((2, tt, 4 * gw), lambda i: (0, nt - 1 - i, 0)), ANY, ANY, ANY, ANY, ANY],
        out_shape=(sd((lp, d), F32), sd((lp, d), BF16), sd((2, lp, 4 * gw), BF16),
                   sd(w_out.shape, F32), sd(w_grp.shape, F32), sd((4, 1, gw), F32), sd((4, 1, gw), F32), sd((1, d), F32)),
        scratch_shapes=[pltpu.VMEM(w_in.shape, BF16), pltpu.VMEM(w_grp.shape, BF16), pltpu.VMEM(w_out.shape, BF16),
                        pltpu.VMEM((4, POOL_HALO, gw), F32), pltpu.VMEM(w_out.shape, F32), pltpu.VMEM(w_grp.shape, F32),
                        pltpu.VMEM((4, 1, gw), F32), pltpu.VMEM((4, 1, gw), F32), pltpu.VMEM((1, d), F32)],
        compiler_params=_params(),
    )(h, dh, halos, g, b_grp, scale, w_in, w_grp, w_out)


def loss_head(h, target, g, pad_tiles, name):
    lp, d = h.shape
    tt = TOKEN_TILE
    nt = lp // tt

    def body(h_ref, t_ref, g_ref, dh_ref, dg_ref, loss_ref, acc):
        i = pl.program_id(0)

        @pl.when(i == 0)
        def _():
            acc[...] = jnp.zeros_like(acc)
            dg_ref[...] = jnp.zeros_like(dg_ref)

        @pl.when(i < pad_tiles)
        def _():
            dh_ref[...] = jnp.zeros_like(dh_ref)

        @pl.when(i >= pad_tiles)
        def _():
            gv = g_ref[...]
            n, hh, rr = _rms_fwd(h_ref[...], gv)
            err = n - t_ref[...]
            acc[...] += 0.5 * jnp.sum(jnp.mean(err * err, axis=-1, keepdims=True), axis=0, keepdims=True)
            dn = err * (1.0 / d)
            dg_ref[...] += jnp.sum(dn * hh, axis=0, keepdims=True)
            dh_ref[...] = _rms_bwd(dn, hh, rr, gv)

        loss_ref[...] = jnp.broadcast_to(acc[...], loss_ref.shape)

    sd = jax.ShapeDtypeStruct
    return pl.pallas_call(
        body, name=name, grid=(nt,),
        in_specs=[pl.BlockSpec((tt, d), lambda i: (i, 0)), pl.BlockSpec((tt, d), lambda i: (jnp.maximum(i - pad_tiles, 0), 0)),
                  pl.BlockSpec((1, d), lambda i: (0, 0))],
        out_specs=[pl.BlockSpec((tt, d), lambda i: (i, 0)), pl.BlockSpec((1, d), lambda i: (0, 0)),
                   pl.BlockSpec((8, 128), lambda i: (0, 0))],
        out_shape=(sd((lp, d), F32), sd((1, d), F32), sd((8, 128), F32)),
        scratch_shapes=[pltpu.VMEM((1, 1), F32)],
        compiler_params=_params(),
    )(h, target, g)


def exchange(arrs, gather, name):
    n = len(arrs)

    def body(*refs):
        ins, outs = refs[:n], refs[n:2 * n]
        send_sems, recv_sems, own_sems = refs[2 * n:]
        x, y, c = lax.axis_index("x"), lax.axis_index("y"), lax.axis_index("c")
        me = 4 * x + 2 * y + c
        own = []
        for a in range(n):
            cp = pltpu.make_async_copy(ins[a] if gather else ins[a].at[me], outs[a].at[me], own_sems.at[a])
            cp.start()
            own.append(cp)
        sent = []
        for k in range(1, N_DEV):
            px = 1 - x if k & 4 else x
            py = 1 - y if k & 2 else y
            pc = 1 - c if k & 1 else c
            peer = 4 * px + 2 * py + pc
            for a in range(n):
                cp = pltpu.make_async_remote_copy(
                    src_ref=ins[a] if gather else ins[a].at[peer], dst_ref=outs[a].at[me],
                    send_sem=send_sems.at[a, k - 1], recv_sem=recv_sems.at[a, k - 1],
                    device_id=(px, py, pc), device_id_type=pl.DeviceIdType.MESH)
                cp.start()
                sent.append((cp, a, k, peer, (px, py, pc)))
        for cp, a, k, peer, pid in sent:
            cp.wait_send()
            pltpu.make_async_remote_copy(
                src_ref=ins[a] if gather else ins[a].at[peer], dst_ref=outs[a].at[peer],
                send_sem=send_sems.at[a, k - 1], recv_sem=recv_sems.at[a, k - 1],
                device_id=pid, device_id_type=pl.DeviceIdType.MESH).wait_recv()
        for cp in own:
            cp.wait()

    hbm = pl.BlockSpec(memory_space=pltpu.HBM)
    out_shape = tuple(jax.ShapeDtypeStruct(((N_DEV,) + a.shape) if gather else a.shape, a.dtype) for a in arrs)
    return pl.pallas_call(
        body, name=name, in_specs=[hbm] * n, out_specs=[hbm] * n, out_shape=out_shape,
        scratch_shapes=[pltpu.SemaphoreType.DMA((n, N_DEV - 1)), pltpu.SemaphoreType.DMA((n, N_DEV - 1)),
                        pltpu.SemaphoreType.DMA((n,))],
    )(*[pltpu.with_memory_space_constraint(a, pltpu.HBM) for a in arrs])


def _peers(x, y, c):
    out = []
    for k in range(1, N_DEV):
        px = 1 - x if k & 4 else x
        py = 1 - y if k & 2 else y
        pc = 1 - c if k & 1 else c
        out.append((k, (px, py, pc), 4 * px + 2 * py + pc))
    return out


def exchange_start(arrs, gather, name):
    n = len(arrs)
    lands = [lax.empty(((N_DEV,) + a.shape) if gather else a.shape, a.dtype) for a in arrs]

    def body(*refs):
        ins, land = refs[:n], refs[n:2 * n]
        send_sems, recv_sems, token = refs[2 * n], refs[2 * n + 1], refs[4 * n + 2]
        x, y, c = lax.axis_index("x"), lax.axis_index("y"), lax.axis_index("c")
        me = 4 * x + 2 * y + c
        for k, pid, peer in _peers(x, y, c):
            for a in range(n):
                pltpu.make_async_remote_copy(
                    src_ref=ins[a] if gather else ins[a].at[peer], dst_ref=land[a].at[me],
                    send_sem=send_sems.at[a * (N_DEV - 1) + k - 1], recv_sem=recv_sems.at[a * (N_DEV - 1) + k - 1],
                    device_id=pid, device_id_type=pl.DeviceIdType.MESH).start()
        token[...] = jnp.zeros_like(token)

    hbm = pl.BlockSpec(memory_space=pltpu.HBM)
    sem = pl.BlockSpec(memory_space=pltpu.SEMAPHORE)
    sems = pltpu.SemaphoreType.DMA((n * (N_DEV - 1),))
    res = pl.pallas_call(
        body, name=name, in_specs=[hbm] * (2 * n),
        out_specs=[sem, sem] + [hbm] * (2 * n) + [pl.BlockSpec(memory_space=pltpu.VMEM)],
        out_shape=[sems, sems] + [pltpu.HBM(a.shape, a.dtype) for a in arrs] + [pltpu.HBM(l.shape, l.dtype) for l in lands]
        + [jax.ShapeDtypeStruct((8, 128), F32)],
        input_output_aliases={a: 2 + a for a in range(2 * n)},
        compiler_params=pltpu.CompilerParams(has_side_effects=pltpu.SideEffectType.DATAFLOW_SIDE_EFFECTING),
    )(*[pltpu.with_memory_space_constraint(a, pltpu.HBM) for a in list(arrs) + lands])
    return res[0], res[1], res[2:2 + n], res[2 + n:2 + 2 * n], res[-1]


def exchange_wait(started, gather, after, name):
    send_sems, recv_sems, srcs, lands, _ = started
    n = len(srcs)

    def body(*refs):
        ins, land = refs[:n], refs[n:2 * n]
        send_sems, recv_sems = refs[2 * n], refs[2 * n + 1]
        got = refs[3 * n + 3:]
        x, y, c = lax.axis_index("x"), lax.axis_index("y"), lax.axis_index("c")
        me = 4 * x + 2 * y + c
        for k, pid, peer in _peers(x, y, c):
            for a in range(n):
                cp = pltpu.make_async_remote_copy(
                    src_ref=ins[a] if gather else ins[a].at[peer], dst_ref=land[a].at[peer],
                    send_sem=send_sems.at[a * (N_DEV - 1) + k - 1], recv_sem=recv_sems.at[a * (N_DEV - 1) + k - 1],
                    device_id=pid, device_id_type=pl.DeviceIdType.MESH)
                cp.wait_send()
                cp.wait_recv()
        for a in range(n):
            pltpu.sync_copy(ins[a] if gather else ins[a].at[me], got[a].at[me])

    hbm = pl.BlockSpec(memory_space=pltpu.HBM)
    sem = pl.BlockSpec(memory_space=pltpu.SEMAPHORE)
    res = pl.pallas_call(
        body, name=name, in_specs=[hbm] * (2 * n) + [sem, sem, ANY],
        out_specs=[hbm] * (2 * n),
        out_shape=[pltpu.HBM(a.shape, a.dtype) for a in list(srcs) + list(lands)],
        input_output_aliases={a: a for a in range(2 * n)},
        compiler_params=pltpu.CompilerParams(has_side_effects=pltpu.SideEffectType.DATAFLOW_SIDE_EFFECTING),
    )(*srcs, *lands, send_sems, recv_sems, after)
    return res[n:]


def _adamw(w, g, m, v):
    m = ADAM_B1 * m + (1.0 - ADAM_B1) * g
    v = ADAM_B2 * v + (1.0 - ADAM_B2) * (g * g)
    m_hat = m / (1.0 - ADAM_B1 ** ADAM_STEP)
    v_hat = v / (1.0 - ADAM_B2 ** ADAM_STEP)
    return -ADAM_LR * (m_hat / (jnp.sqrt(v_hat) + ADAM_EPS) + ADAM_WD * w), m, v


def _update_tile_rows(rows, cols):
    if rows * cols <= UPDATE_TILE_ELEMS:
        return rows
    return max(t for t in range(8, UPDATE_TILE_ELEMS // cols + 1, 8) if rows % t == 0)


def _sum_in_order(p_ref):
    g = p_ref[0]
    for j in range(1, p_ref.shape[0]):
        g = g + p_ref[j]
    return g


def sum_parts(parts, name):
    nparts, rows, cols = parts.shape
    tr = _update_tile_rows(rows, cols)

    def body(p_ref, g_ref):
        g_ref[...] = _sum_in_order(p_ref)

    return pl.pallas_call(
        body, name=name, grid=(rows // tr,),
        in_specs=[pl.BlockSpec((nparts, tr, cols), lambda i: (0, i, 0))],
        out_specs=pl.BlockSpec((tr, cols), lambda i: (i, 0)), out_shape=jax.ShapeDtypeStruct((rows, cols), F32),
        compiler_params=_params(),
    )(parts)


def sum_adamw(parts, w, m, v, name):
    rows, cols = w.shape
    nparts = parts.shape[0]
    tr = _update_tile_rows(rows, cols)

    def body(p_ref, w_ref, m_ref, v_ref, g_ref, d_ref, nm_ref, nv_ref):
        g = _sum_in_order(p_ref)
        delta, nm, nv = _adamw(w_ref[...], g, m_ref[...], v_ref[...])
        g_ref[...] = g
        d_ref[...] = delta
        nm_ref[...] = nm
        nv_ref[...] = nv

    blk = pl.BlockSpec((tr, cols), lambda i: (i, 0))
    sd = jax.ShapeDtypeStruct((rows, cols), F32)
    return pl.pallas_call(
        body, name=name, grid=(rows // tr,),
        in_specs=[pl.BlockSpec((nparts, tr, cols), lambda i: (0, i, 0)), blk, blk, blk],
        out_specs=[blk] * 4, out_shape=(sd,) * 4,
        compiler_params=_params(),
    )(parts, w, m, v)


S5_NAMES = ("w_in", "lam_re", "lam_im", "log_dt", "b_re", "b_im", "c_re", "c_im", "d_skip", "w_glu", "b_glu", "w_out")
CONV_NAMES = ("w_in", "conv_w", "conv_b", "w_out")
POOL_NAMES = ("w_in", "w_grp", "b_grp", "scale", "w_out")
LAYER_KINDS = ("s5", "conv", "pool", "s5")
LAYER_NAMES = {"s5": S5_NAMES, "conv": CONV_NAMES, "pool": POOL_NAMES}
SHARDED = {"s5": ("w_in", "w_glu", "w_out"), "conv": ("w_in", "conv_w", "w_out"), "pool": ("w_in", "w_grp", "b_grp", "w_out")}
GATHER_F32 = ("conv_w", "b_grp")


def weight_names():
    names = ["meta_tokens"]
    for i, kind in enumerate(LAYER_KINDS):
        names.append("norm%d_g" % i)
        names += ["l%d_%s" % (i, n) for n in LAYER_NAMES[kind]]
    names.append("final_g")
    return names


def sharded_names():
    return ["meta_tokens"] + ["l%d_%s" % (i, n) for i, kind in enumerate(LAYER_KINDS) for n in SHARDED[kind]]


def _block_diag_in(bb_t, gc):
    i, g, p = bb_t.shape
    t = bb_t.reshape(i, 4, gc, p)
    return jnp.einsum("icjp,jk->cjikp", t, jnp.eye(gc, dtype=F32)).reshape(4, gc * i, gc * p)


def _block_diag_in_grad(dbd, gc):
    i, p = dbd.shape[1] // gc, dbd.shape[2] // gc
    return jnp.einsum("cjijp->icjp", dbd.reshape(4, gc, i, gc, p)).reshape(i, 4 * gc, p)


def _block_diag_out(cc, gc):
    g, i, p = cc.shape
    return jnp.einsum("cjip,jk->cjpki", cc.reshape(4, gc, i, p), jnp.eye(gc, dtype=F32)).reshape(4, gc * p, gc * i)


def _block_diag_out_grad(dcd, gc):
    p, i = dcd.shape[1] // gc, dcd.shape[2] // gc
    return jnp.einsum("cjpji->cjip", dcd.reshape(4, gc, p, gc, i)).reshape(4 * gc, i, p)


def _to_owner_blocks(a, axis):
    shape = a.shape[:axis] + (N_DEV, a.shape[axis] // N_DEV) + a.shape[axis + 1:]
    return jnp.moveaxis(a.reshape(shape), axis, 0)


def _from_owner_blocks(a, axis):
    a = jnp.moveaxis(a, 0, axis)
    return a.reshape(a.shape[:axis] + (a.shape[axis] * a.shape[axis + 1],) + a.shape[axis + 2:])


def _step(x, target, weights, moments_m, moments_v):
    seq, d = x.shape[1], x.shape[2]
    n_meta = weights["meta_tokens"].shape[0]
    tt = TOKEN_TILE
    pad_tiles = -(-n_meta // tt)
    p0 = pad_tiles * tt
    lp = p0 + seq
    first_pos = p0 - n_meta
    gc = d // 4 // S5_GROUP
    cw = d // 4

    big_names = [n for n in sharded_names() if n != "meta_tokens" and n.split("_", 1)[1] not in GATHER_F32]
    small_names = [n for n in sharded_names() if n not in big_names]
    layer_big = [[n for n in big_names if n.startswith("l%d_" % i)] for i in range(len(LAYER_KINDS))]
    layer_big[0] = small_names + layer_big[0]
    gather_started = [exchange_start([weights[n] if n in small_names else weights[n].astype(BF16) for n in names], True,
                                     "gather_start_l%d" % i) for i, names in enumerate(layer_big)]
    started = sum(st[4] for st in gather_started)
    gathered = dict(zip(layer_big[0], exchange_wait(gather_started[0], True, started, "gather_wait_l0")))

    meta = _from_owner_blocks(gathered["meta_tokens"], 1)
    h = jnp.concatenate([jnp.zeros((first_pos, d), F32), meta, x[0]], axis=0)

    def vec(name):
        return weights[name].reshape(1, -1)

    full = {}

    def layer_weights(i, kind, after):
        p = "l%d_" % i
        if i > 0:
            gathered.update(zip(layer_big[i], exchange_wait(gather_started[i], True, after, "gather_wait_l%d" % i)))
        w_in = gathered[p + "w_in"]
        if kind == "s5":
            lr, li = weights[p + "lam_re"], weights[p + "lam_im"]
            ldt = weights[p + "log_dt"].reshape(-1, 1)
            br_t = jnp.transpose(weights[p + "b_re"], (2, 0, 1))
            bi_t = jnp.transpose(weights[p + "b_im"], (2, 0, 1))
            ar, ai, bbr, bbi = s5_disc_fwd(lr, li, ldt, br_t, bi_t, p + "disc_fwd")
            full[i] = dict(
                w_in=w_in, disc=(lr, li, ldt, br_t, bi_t),
                ar=ar.reshape(4, -1, 128), ai=ai.reshape(4, -1, 128),
                bdre=_block_diag_in(bbr, gc).astype(BF16), bdim=_block_diag_in(bbi, gc).astype(BF16),
                cdre=_block_diag_out(weights[p + "c_re"], gc).astype(BF16),
                cdim=_block_diag_out(-weights[p + "c_im"], gc).astype(BF16),
                w_glu=gathered[p + "w_glu"].reshape(4, cw, d), w_out=gathered[p + "w_out"].reshape(4, cw, d),
                d_skip=weights[p + "d_skip"].reshape(4, 1, cw), b_glu=vec(p + "b_glu"))
        elif kind == "conv":
            ce = w_in.shape[2]
            nch = 2
            conv_w = _from_owner_blocks(gathered[p + "conv_w"], 1)
            full[i] = dict(
                w_in=w_in, conv_w=jnp.transpose(conv_w.reshape(CONV_K, nch, ce), (1, 0, 2)),
                conv_b=weights[p + "conv_b"].reshape(nch, 1, ce), w_out=gathered[p + "w_out"].reshape(nch, ce, d))
        else:
            gw = w_in.shape[2]
            full[i] = dict(
                w_in=w_in, w_grp=_from_owner_blocks(gathered[p + "w_grp"], 1),
                b_grp=_from_owner_blocks(gathered[p + "b_grp"], 1).reshape(4, 1, gw),
                scale=weights[p + "scale"].reshape(4, 1, gw), w_out=gathered[p + "w_out"].reshape(4, gw, d))
        return full[i]

    saved = {}
    for i, kind in enumerate(LAYER_KINDS):
        p, f, g = "l%d_" % i, layer_weights(i, kind, h), vec("norm%d_g" % i)
        if kind == "s5":
            u, z, xr, xi = s5_fwd1(h, g, f["w_in"], f["bdre"], f["bdim"], p + "fwd_in")
            sr, si = s5_scan_fwd(xr, xi, f["ar"], f["ai"], p + "scan_fwd")
            saved[i] = (h, u, z, sr, si)
            h = s5_fwd3(sr, si, u, z, h, f["cdre"], f["cdim"], f["w_glu"], f["w_out"], f["d_skip"], f["b_glu"], p + "fwd_out")
        elif kind == "conv":
            h_new, halos = conv_fwd(h, g, f["w_in"], f["conv_w"], f["conv_b"], f["w_out"], p + "fwd")
            saved[i] = (h, halos)
            h = h_new
        else:
            h_new, halos = pool_fwd(h, g, f["w_in"], f["w_grp"], f["b_grp"], f["scale"], f["w_out"], first_pos, p + "fwd")
            saved[i] = (h, halos)
            h = h_new

    dh, dg_final, loss_tile = loss_head(h, target[0], vec("final_g"), pad_tiles, "loss_head")
    loss = lax.psum(loss_tile[0, 0], ("x", "y", "c"))

    grads = {"final_g": dg_final}
    names = weight_names()
    sh_names = sharded_names()
    rep_names = [n for n in names if n not in sh_names]

    def owner_blocks(a):
        return a.reshape(N_DEV, -1, a.shape[-1])

    def as2d(a):
        return a.reshape(-1, a.shape[-1])

    def pack(tree):
        flat = [jnp.pad(tree[n].reshape(-1), (0, -tree[n].size % 1024)) for n in rep_names]
        flat = jnp.concatenate(flat)
        return jnp.pad(flat, (0, -flat.size % (PACK_ROWS * 128))).reshape(-1, 128)

    layer_sharded, scatter_started = {}, {}
    ordered = jnp.zeros((), F32)
    for i in reversed(range(len(LAYER_KINDS))):
        kind = LAYER_KINDS[i]
        p, f, g = "l%d_" % i, full[i], vec("norm%d_g" % i) + ordered
        if kind == "s5":
            h_in, u, z, sr, si = saved[i]
            dy, dp, dwo, dwg, dbg = s5_bwd3a(dh, sr, si, u, z, f["cdre"], f["cdim"], f["w_glu"], f["w_out"],
                                             f["d_skip"], f["b_glu"] + ordered, p + "bwd_out")
            dsr, dsi, dus, dcre, dcim, dd = s5_bwd3b(dy, sr, si, u, f["cdre"], f["cdim"], f["d_skip"], p + "bwd_read")
            lam_r, lam_i, dar, dai = s5_scan_bwd(dsr, dsi, sr, si, f["ar"], f["ai"], p + "scan_bwd")
            dp, dh, n, dbre, dbim, dg = s5_bwd1(lam_r, lam_i, dus, u, dp, h_in, dh, g, f["w_in"], f["bdre"], f["bdim"], p + "bwd_in")
            dw_in = grad_w_in(n, dp, f["w_in"].shape[2], p + "grad_w_in")
            lr, li, ldt, br_t, bi_t = f["disc"]
            dlr, dli, dldt, dbr_t, dbi_t = s5_disc_bwd(
                lr, li, ldt, br_t, bi_t, dar.reshape(lr.shape), dai.reshape(lr.shape),
                _block_diag_in_grad(dbre, gc), _block_diag_in_grad(dbim, gc), p + "disc_bwd")
            grads.update({
                p + "w_in": dw_in, p + "lam_re": dlr, p + "lam_im": dli, p + "log_dt": dldt,
                p + "b_re": jnp.transpose(dbr_t, (1, 2, 0)), p + "b_im": jnp.transpose(dbi_t, (1, 2, 0)),
                p + "c_re": _block_diag_out_grad(dcre, gc), p + "c_im": -_block_diag_out_grad(dcim, gc),
                p + "d_skip": dd, p + "w_glu": dwg.reshape(N_DEV, -1, d), p + "b_glu": dbg,
                p + "w_out": dwo.reshape(N_DEV, -1, d)})
        elif kind == "conv":
            h_in, halos = saved[i]
            dh, n, dp, dwo, dcw, dcb, dg = conv_bwd(h_in, dh, halos, g, f["w_in"], f["conv_w"], f["conv_b"], f["w_out"], p + "bwd")
            dw_in = grad_w_in(n, dp, f["w_in"].shape[2], p + "grad_w_in")
            dconv_w = jnp.transpose(dcw[:, :CONV_K, :], (1, 0, 2)).reshape(CONV_K, -1)
            grads.update({p + "w_in": dw_in, p + "conv_w": _to_owner_blocks(dconv_w, 1), p + "conv_b": dcb,
                          p + "w_out": dwo.reshape(N_DEV, -1, d)})
        else:
            h_in, halos = saved[i]
            dh, n, dp, dwo, dwgrp, dbgrp, dsc, dg = pool_bwd(h_in, dh, halos, g, f["w_in"], f["w_grp"], f["b_grp"], f["scale"],
                                                             f["w_out"], first_pos, p + "bwd")
            dw_in = grad_w_in(n, dp, f["w_in"].shape[2], p + "grad_w_in")
            grads.update({p + "w_in": dw_in, p + "w_grp": _to_owner_blocks(dwgrp, 1),
                          p + "b_grp": _to_owner_blocks(dbgrp.reshape(4, -1), 1), p + "scale": dsc,
                          p + "w_out": dwo.reshape(N_DEV, -1, d)})
        grads["norm%d_g" % i] = dg
        layer_sharded[i] = ["l%d_%s" % (i, n) for n in SHARDED[kind]]
        blocks = [owner_blocks(grads[n]) for n in layer_sharded[i]]
        if i == 0:
            grads["meta_tokens"] = _to_owner_blocks(dh[first_pos:p0], 1)
            layer_sharded[i] = layer_sharded[i] + ["meta_tokens", "replicated"]
            blocks += [owner_blocks(grads["meta_tokens"]), pack(grads).reshape(N_DEV, -1, 128)]
        scatter_started[i] = exchange_start(blocks, False, "scatter_start_l%d" % i)
        ordered = scatter_started[i][4][0, 0]
    grad_x = dh[p0:][None]

    out = {}
    received = {}
    for i in reversed(range(len(LAYER_KINDS))):
        after = scatter_started[0][4] if i == len(LAYER_KINDS) - 1 else out[layer_sharded[i + 1][0]][0]
        received.update(zip(layer_sharded[i], exchange_wait(scatter_started[i], False, after, "scatter_wait_l%d" % i)))
        for n in layer_sharded[i]:
            if n != "replicated":
                res = sum_adamw(received[n], as2d(weights[n]), as2d(moments_m[n]), as2d(moments_v[n]), "update_" + n)
                out[n] = [r.reshape(weights[n].shape) for r in res]

    g_full = exchange([sum_parts(received["replicated"], "sum_replicated")], True, "gather_small_grads")[0].reshape(1, -1, 128)
    packed = sum_adamw(g_full, pack(weights), pack(moments_m), pack(moments_v), "update_replicated")
    offset = 0
    for n in rep_names:
        size = weights[n].size
        out[n] = [r.reshape(-1)[offset:offset + size].reshape(weights[n].shape) for r in packed]
        offset += size + (-size % 1024)

    return (loss, grad_x) + tuple(out[n][k] for k in range(4) for n in names)


def kernel(x, *rest):
    names = weight_names()
    nw = len(names)
    weights = dict(zip(names, rest[:nw]))
    target = rest[nw]
    moments_m = dict(zip(names, rest[nw + 1:2 * nw + 1]))
    moments_v = dict(zip(names, rest[2 * nw + 1:3 * nw + 1]))
    return _step(x, target, weights, moments_m, moments_v)
```

```python
import functools
import math

import jax
import jax.numpy as jnp
from jax import lax
from jax.experimental import pallas as pl
from jax.experimental.pallas import tpu as pltpu

F32 = jnp.float32
BF16 = jnp.bfloat16
EPS = 1e-6
N_DEV = 8
TOKEN_TILE = 256
SCAN_CHUNKS = 2
S5_GROUP = 16
S5_STATE = 64
POOL_WINDOWS = (2, 4, 8, 16)
POOL_HALO = 16
CONV_K = 3
CONV_HALO = 8
ADAM_LR = 0.001
ADAM_B1 = 0.9
ADAM_B2 = 0.999
ADAM_EPS = 1e-08
ADAM_WD = 0.01
ADAM_STEP = 10
GELU_C = math.sqrt(2.0 / math.pi)
GELU_A = 0.044715
UPDATE_TILE_ELEMS = 1 << 17
PACK_ROWS = 512
VMEM_LIMIT = 56 << 20

ANY = pl.BlockSpec(memory_space=pl.ANY)


def _params(vmem=VMEM_LIMIT, ndim=1):
    return pltpu.CompilerParams(vmem_limit_bytes=vmem, dimension_semantics=("arbitrary",) * ndim)


def _dot(a, b):
    return jnp.dot(a.astype(BF16), b.astype(BF16), preferred_element_type=F32)


def _dot_nt(a, b):
    return lax.dot_general(a.astype(BF16), b.astype(BF16), (((1,), (1,)), ((), ())), preferred_element_type=F32)


def _dot_tn(a, b):
    return lax.dot_general(a.astype(BF16), b.astype(BF16), (((0,), (0,)), ((), ())), preferred_element_type=F32)


def _rms_fwd(h, g):
    r = lax.rsqrt(jnp.mean(h * h, axis=-1, keepdims=True) + EPS)
    hh = h * r
    return hh * g, hh, r


def _rms_bwd(dn, hh, r, g):
    dhh = dn * g
    return r * (dhh - hh * jnp.mean(dhh * hh, axis=-1, keepdims=True))


def _sigmoid(x):
    return 1.0 / (1.0 + jnp.exp(-x))


def _silu_and_grad(z):
    s = _sigmoid(z)
    return z * s, s * (1.0 + z * (1.0 - s))


def _gelu(y):
    t = jnp.tanh(GELU_C * (y + GELU_A * y * y * y))
    return 0.5 * y * (1.0 + t), t


def _gelu_grad(y, t):
    return 0.5 * (1.0 + t) + 0.5 * y * (1.0 - t * t) * GELU_C * (1.0 + 3.0 * GELU_A * y * y)


def _rows(shape):
    return lax.broadcasted_iota(jnp.int32, shape, 0)


def _shift_down(x, k, halo):
    y = pltpu.roll(x, k, 0)
    rows = _rows(x.shape)
    for j in range(k):
        y = jnp.where(rows == j, halo[halo.shape[0] - k + j:halo.shape[0] - k + j + 1, :], y)
    return y


def _shift_up(x, k, halo):
    n = x.shape[0]
    y = pltpu.roll(x, n - k, 0)
    rows = _rows(x.shape)
    for j in range(k):
        y = jnp.where(rows == n - k + j, halo[j:j + 1, :], y)
    return y


def _window_sums_back(ext):
    out = []
    s = ext
    for k in (1, 2, 4, 8):
        s = s + pltpu.roll(s, k, 0)
        out.append(s)
    return out


def _window_sums_fwd(ext):
    n = ext.shape[0]
    out = []
    s = ext
    for k in (1, 2, 4, 8):
        s = s + pltpu.roll(s, n - k, 0)
        out.append(s)
    return out


def _pool_inv_count(tile, tt, first_pos, w, width):
    pos = _rows((tt, width)) + (tile * tt - first_pos + 1)
    return 1.0 / jnp.clip(pos, 1, w).astype(F32)


def _slab_spec(lp, tt, sw, index_map):
    nj = sw // 128
    return pl.BlockSpec((tt * nj, 128), index_map), (lp * 4 * nj, 128)


def _slab_load(ref):
    nj = ref.shape[0] // TOKEN_TILE
    return jnp.concatenate([ref[pl.ds(j, TOKEN_TILE, stride=nj), :] for j in range(nj)], axis=1)


def _slab_store(ref, val):
    nj = ref.shape[0] // TOKEN_TILE
    for j in range(nj):
        ref[pl.ds(j, TOKEN_TILE, stride=nj), :] = val[:, j * 128:(j + 1) * 128]


def _s5_disc_math(lr, li, ldt, br, bi):
    dt = jnp.exp(ldt)
    mag = jnp.exp(lr * dt)
    ar = mag * jnp.cos(li * dt)
    ai = mag * jnp.sin(li * dt)
    den = lr * lr + li * li
    kr = ((ar - 1.0) * lr + ai * li) / den
    ki = (ai * lr - (ar - 1.0) * li) / den
    bbr = kr[None] * br - ki[None] * bi
    bbi = kr[None] * bi + ki[None] * br
    return ar, ai, bbr, bbi


def s5_disc_fwd(lr, li, ldt, br_t, bi_t, name):
    def body(lr_ref, li_ref, ldt_ref, br_ref, bi_ref, ar_ref, ai_ref, bbr_ref, bbi_ref):
        ar, ai, bbr, bbi = _s5_disc_math(lr_ref[...], li_ref[...], ldt_ref[...], br_ref[...], bi_ref[...])
        ar_ref[...] = ar
        ai_ref[...] = ai
        bbr_ref[...] = bbr
        bbi_ref[...] = bbi

    sd = jax.ShapeDtypeStruct
    return pl.pallas_call(
        body, name=name,
        out_shape=(sd(lr.shape, F32), sd(lr.shape, F32), sd(br_t.shape, F32), sd(br_t.shape, F32)),
    )(lr, li, ldt, br_t, bi_t)


def s5_disc_bwd(lr, li, ldt, br_t, bi_t, dar, dai, dbbr, dbbi, name):
    def body(lr_ref, li_ref, ldt_ref, br_ref, bi_ref, dar_ref, dai_ref, dbbr_ref, dbbi_ref,
             dlr_ref, dli_ref, dldt_ref, dbr_ref, dbi_ref):
        _, vjp = jax.vjp(_s5_disc_math, lr_ref[...], li_ref[...], ldt_ref[...], br_ref[...], bi_ref[...])
        dlr, dli, dldt, dbr, dbi = vjp((dar_ref[...], dai_ref[...], dbbr_ref[...], dbbi_ref[...]))
        dlr_ref[...] = dlr
        dli_ref[...] = dli
        dldt_ref[...] = dldt
        dbr_ref[...] = dbr
        dbi_ref[...] = dbi

    sd = jax.ShapeDtypeStruct
    return pl.pallas_call(
        body, name=name,
        out_shape=(sd(lr.shape, F32), sd(lr.shape, F32), sd(ldt.shape, F32), sd(br_t.shape, F32), sd(br_t.shape, F32)),
    )(lr, li, ldt, br_t, bi_t, dar, dai, dbbr, dbbi)


def s5_fwd1(h, g, w_in, bdre, bdim, name):
    lp, d = h.shape
    tt = TOKEN_TILE
    cw, sw = bdre.shape[1], bdre.shape[2]

    def body(h_ref, g_ref, w_hbm, bdre_hbm, bdim_hbm, u_ref, z_ref, xr_ref, xi_ref, w, bre, bim, n_sc):
        i, c = pl.program_id(0), pl.program_id(1)

        @pl.when((i == 0) & (c == 0))
        def _():
            pltpu.sync_copy(w_hbm, w)
            pltpu.sync_copy(bdre_hbm, bre)
            pltpu.sync_copy(bdim_hbm, bim)

        @pl.when(c == 0)
        def _():
            n_sc[...] = _rms_fwd(h_ref[...], g_ref[...])[0].astype(BF16)

        n = n_sc[...]
        u = jnp.dot(n, w[c], preferred_element_type=F32)
        u_ref[...] = u
        z_ref[...] = jnp.dot(n, w[c + 4], preferred_element_type=F32)
        ub = u.astype(BF16)
        _slab_store(xr_ref, jnp.dot(ub, bre[c], preferred_element_type=F32))
        _slab_store(xi_ref, jnp.dot(ub, bim[c], preferred_element_type=F32))

    sd = jax.ShapeDtypeStruct
    slab, slab_shape = _slab_spec(lp, tt, sw, lambda i, c: (i * 4 + c, 0))
    return pl.pallas_call(
        body, name=name, grid=(lp // tt, 4),
        in_specs=[pl.BlockSpec((tt, d), lambda i, c: (i, 0)), pl.BlockSpec((1, d), lambda i, c: (0, 0)), ANY, ANY, ANY],
        out_specs=[pl.BlockSpec((tt, cw), lambda i, c: (i, c)), pl.BlockSpec((tt, cw), lambda i, c: (i, c)), slab, slab],
        out_shape=(sd((lp, d), F32), sd((lp, d), F32), sd(slab_shape, F32), sd(slab_shape, F32)),
        scratch_shapes=[pltpu.VMEM(w_in.shape, BF16), pltpu.VMEM(bdre.shape, BF16), pltpu.VMEM(bdim.shape, BF16),
                        pltpu.VMEM((tt, d), BF16)],
        compiler_params=_params(ndim=2),
    )(h, g, w_in, bdre, bdim)


def s5_scan_fwd(xr, xi, ar, ai, name):
    nj = ar.shape[1]
    tt = TOKEN_TILE
    cpb = SCAN_CHUNKS
    nt = xr.shape[0] // (4 * tt * nj)

    def body(xr_ref, xi_ref, ar_ref, ai_ref, sr_ref, si_ref, st_r, st_i):
        i, cg = pl.program_id(0), pl.program_id(1)

        @pl.when(i == 0)
        def _():
            for q in range(cpb):
                st_r[cg * cpb + q] = jnp.zeros((nj, 128), F32)
                st_i[cg * cpb + q] = jnp.zeros((nj, 128), F32)

        a_r = [ar_ref[cg * cpb + q] for q in range(cpb)]
        a_i = [ai_ref[cg * cpb + q] for q in range(cpb)]

        def step(t, carry):
            out = []
            for q in range(cpb):
                s_r, s_i = carry[q]
                rows = pl.ds(pl.multiple_of((q * tt + t) * nj, nj), nj)
                n_r = a_r[q] * s_r - a_i[q] * s_i + xr_ref[rows, :]
                n_i = a_r[q] * s_i + a_i[q] * s_r + xi_ref[rows, :]
                sr_ref[rows, :] = n_r
                si_ref[rows, :] = n_i
                out.append((n_r, n_i))
            return tuple(out)

        init = tuple((st_r[cg * cpb + q], st_i[cg * cpb + q]) for q in range(cpb))
        final = lax.fori_loop(0, tt, step, init, unroll=8)
        for q in range(cpb):
            st_r[cg * cpb + q] = final[q][0]
            st_i[cg * cpb + q] = final[q][1]

    blk = pl.BlockSpec((cpb * tt * nj, 128), lambda i, cg: (i * (4 // cpb) + cg, 0))
    par = pl.BlockSpec((4, nj, 128), lambda i, cg: (0, 0, 0))
    sd = jax.ShapeDtypeStruct
    return pl.pallas_call(
        body, name=name, grid=(nt, 4 // cpb),
        in_specs=[blk, blk, par, par], out_specs=[blk, blk],
        out_shape=(sd(xr.shape, F32), sd(xr.shape, F32)),
        scratch_shapes=[pltpu.VMEM((4, nj, 128), F32), pltpu.VMEM((4, nj, 128), F32)],
        compiler_params=_params(ndim=2),
    )(xr, xi, ar, ai)


def _s5_mix_fwd(sr_ref, si_ref, u_ref, d_ref, cre, cim, c):
    y = _dot(_slab_load(sr_ref), cre[c]) + _dot(_slab_load(si_ref), cim[c]) + d_ref[c] * u_ref[...]
    gy, t = _gelu(y)
    return y, gy, t


def s5_fwd3(sr, si, u, z, h, cdre, cdim, w_glu, w_out, d_skip, b_glu, name):
    lp, d = h.shape
    tt = TOKEN_TILE
    sw, cw = cdre.shape[1], cdre.shape[2]

    def body(sr_ref, si_ref, u_ref, z_ref, h_ref, d_ref, bg_ref, cre_hbm, cim_hbm, wg_hbm, wo_hbm,
             o_ref, cre, cim, wg, wo, gy_sc, q_sc):
        i, c = pl.program_id(0), pl.program_id(1)

        @pl.when((i == 0) & (c == 0))
        def _():
            pltpu.sync_copy(cre_hbm, cre)
            pltpu.sync_copy(cim_hbm, cim)
            pltpu.sync_copy(wg_hbm, wg)
            pltpu.sync_copy(wo_hbm, wo)

        _, gy, _ = _s5_mix_fwd(sr_ref, si_ref, u_ref, d_ref, cre, cim, c)
        gy_sc[c] = gy
        part = _dot(gy, wg[c])

        @pl.when(c == 0)
        def _():
            q_sc[...] = part

        @pl.when(c > 0)
        def _():
            q_sc[...] += part

        @pl.when(c == 3)
        def _():
            sig = _sigmoid(q_sc[...] + bg_ref[...])
            zz = z_ref[...]
            sz = zz * _sigmoid(zz)
            o = h_ref[...]
            for k in range(4):
                cols = slice(k * cw, (k + 1) * cw)
                o = o + _dot(gy_sc[k] * sig[:, cols] * sz[:, cols], wo[k])
            o_ref[...] = o

    row = lambda i, c: (i, 0)
    chunk = lambda i, c: (i, c)
    slab, _ = _slab_spec(lp, tt, sw, lambda i, c: (i * 4 + c, 0))
    return pl.pallas_call(
        body, name=name, grid=(lp // tt, 4),
        in_specs=[slab, slab, pl.BlockSpec((tt, cw), chunk),
                  pl.BlockSpec((tt, d), row), pl.BlockSpec((tt, d), row),
                  pl.BlockSpec((4, 1, cw), lambda i, c: (0, 0, 0)), pl.BlockSpec((1, d), lambda i, c: (0, 0)),
                  ANY, ANY, ANY, ANY],
        out_specs=pl.BlockSpec((tt, d), row),
        out_shape=jax.ShapeDtypeStruct((lp, d), F32),
        scratch_shapes=[pltpu.VMEM(cdre.shape, BF16), pltpu.VMEM(cdim.shape, BF16), pltpu.VMEM(w_glu.shape, BF16),
                        pltpu.VMEM(w_out.shape, BF16), pltpu.VMEM((4, tt, cw), F32), pltpu.VMEM((tt, d), F32)],
        compiler_params=_params(ndim=2),
    )(sr, si, u, z, h, d_skip, b_glu, cdre, cdim, w_glu, w_out)


def s5_bwd3a(dh, sr, si, u, z, cdre, cdim, w_glu, w_out, d_skip, b_glu, name):
    lp, d = dh.shape
    tt = TOKEN_TILE
    nt = lp // tt
    sw, cw = cdre.shape[1], cdre.shape[2]

    def body(dh_ref, sr_ref, si_ref, u_ref, z_ref, d_ref, bg_ref, cre_hbm, cim_hbm, wg_hbm, wo_hbm,
             dy_ref, dp_ref, dwo_hbm, dwg_hbm, dbg_hbm,
             cre, cim, wg, wo, y_sc, t_sc, gy_sc, q_sc, dq_sc, dgy_sc, dwo, dwg, dbg):
        i, c = pl.program_id(0), pl.program_id(1)

        @pl.when((i == 0) & (c == 0))
        def _():
            pltpu.sync_copy(cre_hbm, cre)
            pltpu.sync_copy(cim_hbm, cim)
            pltpu.sync_copy(wg_hbm, wg)
            pltpu.sync_copy(wo_hbm, wo)
            dwo[...] = jnp.zeros_like(dwo)
            dwg[...] = jnp.zeros_like(dwg)
            dbg[...] = jnp.zeros_like(dbg)

        y, gy, t = _s5_mix_fwd(sr_ref, si_ref, u_ref, d_ref, cre, cim, c)
        y_sc[c] = y
        t_sc[c] = t
        gy_sc[c] = gy
        part = _dot(gy, wg[c])

        @pl.when(c == 0)
        def _():
            q_sc[...] = part

        @pl.when(c > 0)
        def _():
            q_sc[...] += part

        @pl.when(c == 3)
        def _():
            sig = _sigmoid(q_sc[...] + bg_ref[...])
            sz, dsz = _silu_and_grad(z_ref[...])
            dhv = dh_ref[...]
            for k in range(4):
                cols = slice(k * cw, (k + 1) * cw)
                gy_k, sig_k, sz_k = gy_sc[k], sig[:, cols], sz[:, cols]
                y2 = gy_k * sig_k
                dy3 = _dot_nt(dhv, wo[k])
                dwo[k] += _dot_tn(y2 * sz_k, dhv)
                dy2 = dy3 * sz_k
                dp_ref[0, :, cols] = (dy3 * y2 * dsz[:, cols]).astype(BF16)
                dq_sc[:, cols] = dy2 * gy_k * sig_k * (1.0 - sig_k)
                dgy_sc[k] = dy2 * sig_k
            dq = dq_sc[...]
            dbg[...] += jnp.sum(dq, axis=0, keepdims=True)
            for k in range(4):
                cols = slice(k * cw, (k + 1) * cw)
                dwg[k] += _dot_tn(gy_sc[k], dq)
                dgy = dgy_sc[k] + _dot_nt(dq, wg[k])
                dy_ref[:, cols] = dgy * _gelu_grad(y_sc[k], t_sc[k])

        @pl.when((i == nt - 1) & (c == 3))
        def _():
            pltpu.sync_copy(dwo, dwo_hbm)
            pltpu.sync_copy(dwg, dwg_hbm)
            pltpu.sync_copy(dbg, dbg_hbm)

    row = lambda i, c: (i, 0)
    chunk = lambda i, c: (i, c)
    sd = jax.ShapeDtypeStruct
    acc = pltpu.VMEM((4, tt, cw), F32)
    slab, _ = _slab_spec(lp, tt, sw, lambda i, c: (i * 4 + c, 0))
    return pl.pallas_call(
        body, name=name, grid=(nt, 4),
        in_specs=[pl.BlockSpec((tt, d), row), slab, slab,
                  pl.BlockSpec((tt, cw), chunk), pl.BlockSpec((tt, d), row),
                  pl.BlockSpec((4, 1, cw), lambda i, c: (0, 0, 0)), pl.BlockSpec((1, d), lambda i, c: (0, 0)),
                  ANY, ANY, ANY, ANY],
        out_specs=[pl.BlockSpec((tt, d), row), pl.BlockSpec((1, tt, d), lambda i, c: (1, i, 0)), ANY, ANY, ANY],
        out_shape=(sd((lp, d), F32), sd((2, lp, d), BF16), sd(w_out.shape, F32), sd(w_glu.shape, F32), sd((1, d), F32)),
        scratch_shapes=[pltpu.VMEM(cdre.shape, BF16), pltpu.VMEM(cdim.shape, BF16), pltpu.VMEM(w_glu.shape, BF16),
                        pltpu.VMEM(w_out.shape, BF16), acc, acc, acc, pltpu.VMEM((tt, d), F32), pltpu.VMEM((tt, d), F32), acc,
                        pltpu.VMEM(w_out.shape, F32), pltpu.VMEM(w_glu.shape, F32), pltpu.VMEM((1, d), F32)],
        compiler_params=_params(ndim=2),
    )(dh, sr, si, u, z, d_skip, b_glu, cdre, cdim, w_glu, w_out)


def s5_bwd3b(dy, sr, si, u, cdre, cdim, d_skip, name):
    lp, d = dy.shape
    tt = TOKEN_TILE
    nt = lp // tt
    sw, cw = cdre.shape[1], cdre.shape[2]

    def body(dy_ref, sr_ref, si_ref, u_ref, d_ref, cre_hbm, cim_hbm,
             dsr_ref, dsi_ref, dus_ref, dcre_hbm, dcim_hbm, dd_hbm, cre, cim, dcre, dcim, dd):
        i, c = pl.program_id(0), pl.program_id(1)

        @pl.when((i == 0) & (c == 0))
        def _():
            pltpu.sync_copy(cre_hbm, cre)
            pltpu.sync_copy(cim_hbm, cim)
            dcre[...] = jnp.zeros_like(dcre)
            dcim[...] = jnp.zeros_like(dcim)
            dd[...] = jnp.zeros_like(dd)

        dyv = dy_ref[...]
        dd[c] += jnp.sum(dyv * u_ref[...], axis=0, keepdims=True)
        dus_ref[...] = dyv * d_ref[c]
        _slab_store(dsr_ref, _dot_nt(dyv, cre[c]))
        _slab_store(dsi_ref, _dot_nt(dyv, cim[c]))
        dcre[c] += _dot_tn(_slab_load(sr_ref), dyv)
        dcim[c] += _dot_tn(_slab_load(si_ref), dyv)

        @pl.when((i == nt - 1) & (c == 3))
        def _():
            pltpu.sync_copy(dcre, dcre_hbm)
            pltpu.sync_copy(dcim, dcim_hbm)
            pltpu.sync_copy(dd, dd_hbm)

    chunk = lambda i, c: (i, c)
    sd = jax.ShapeDtypeStruct
    slab, slab_shape = _slab_spec(lp, tt, sw, lambda i, c: (i * 4 + c, 0))
    return pl.pallas_call(
        body, name=name, grid=(nt, 4),
        in_specs=[pl.BlockSpec((tt, cw), chunk), slab, slab,
                  pl.BlockSpec((tt, cw), chunk), pl.BlockSpec((4, 1, cw), lambda i, c: (0, 0, 0)), ANY, ANY],
        out_specs=[slab, slab, pl.BlockSpec((tt, cw), chunk), ANY, ANY, ANY],
        out_shape=(sd(slab_shape, F32), sd(slab_shape, F32), sd((lp, d), F32),
                   sd(cdre.shape, F32), sd(cdim.shape, F32), sd((4, 1, cw), F32)),
        scratch_shapes=[pltpu.VMEM(cdre.shape, BF16), pltpu.VMEM(cdim.shape, BF16),
                        pltpu.VMEM(cdre.shape, F32), pltpu.VMEM(cdim.shape, F32), pltpu.VMEM((4, 1, cw), F32)],
        compiler_params=_params(ndim=2),
    )(dy, sr, si, u, d_skip, cdre, cdim)


def s5_scan_bwd(gr, gi, sr, si, ar, ai, name):
    nj = ar.shape[1]
    tt = TOKEN_TILE
    cpb = SCAN_CHUNKS
    nt = gr.shape[0] // (4 * tt * nj)

    def body(gr_ref, gi_ref, sr_ref, si_ref, ar_ref, ai_ref, lr_ref, li_ref, dar_ref, dai_ref, st_r, st_i, acc_r, acc_i):
        i, cg = pl.program_id(0), pl.program_id(1)

        @pl.when((i == 0) & (cg == 0))
        def _():
            for ref in (st_r, st_i, acc_r, acc_i):
                ref[...] = jnp.zeros_like(ref)

        a_r = [ar_ref[cg * cpb + q] for q in range(cpb)]
        a_i = [ai_ref[cg * cpb + q] for q in range(cpb)]

        def slab(q, t):
            return pl.ds(pl.multiple_of((q * tt + t) * nj, nj), nj)

        def adjoint(q, t, l_r, l_i):
            rows = slab(q, t)
            n_r = gr_ref[rows, :] + a_r[q] * l_r + a_i[q] * l_i
            n_i = gi_ref[rows, :] + a_r[q] * l_i - a_i[q] * l_r
            lr_ref[rows, :] = n_r
            li_ref[rows, :] = n_i
            return n_r, n_i

        def pair(q, t, l_r, l_i, d_r, d_i):
            rows = slab(q, t)
            p_r, p_i = sr_ref[rows, :], si_ref[rows, :]
            return d_r + l_r * p_r + l_i * p_i, d_i + l_i * p_r - l_r * p_i

        def step(k, carry):
            t = tt - 1 - k
            out = []
            for q in range(cpb):
                l_r, l_i, d_r, d_i = carry[q]
                l_r, l_i = adjoint(q, t, l_r, l_i)
                d_r, d_i = pair(q, t - 1, l_r, l_i, d_r, d_i)
                out.append((l_r, l_i, d_r, d_i))
            return tuple(out)

        init = []
        for q in range(cpb):
            ch = cg * cpb + q
            l_r, l_i = st_r[ch], st_i[ch]
            d_r, d_i = pair(q, tt - 1, l_r, l_i, acc_r[ch], acc_i[ch])
            init.append((l_r, l_i, d_r, d_i))
        final = lax.fori_loop(0, tt - 1, step, tuple(init), unroll=8)
        for q in range(cpb):
            ch = cg * cpb + q
            l_r, l_i, d_r, d_i = final[q]
            l_r, l_i = adjoint(q, 0, l_r, l_i)
            st_r[ch] = l_r
            st_i[ch] = l_i
            acc_r[ch] = d_r
            acc_i[ch] = d_i
            dar_ref[ch] = d_r
            dai_ref[ch] = d_i

    blk = pl.BlockSpec((cpb * tt * nj, 128), lambda i, cg: ((nt - 1 - i) * (4 // cpb) + cg, 0))
    par = pl.BlockSpec((4, nj, 128), lambda i, cg: (0, 0, 0))
    sd = jax.ShapeDtypeStruct
    return pl.pallas_call(
        body, name=name, grid=(nt, 4 // cpb),
        in_specs=[blk, blk, blk, blk, par, par], out_specs=[blk, blk, par, par],
        out_shape=(sd(gr.shape, F32), sd(gr.shape, F32), sd((4, nj, 128), F32), sd((4, nj, 128), F32)),
        scratch_shapes=[pltpu.VMEM((4, nj, 128), F32)] * 4,
        compiler_params=_params(ndim=2),
    )(gr, gi, sr, si, ar, ai)


def s5_bwd1(lam_r, lam_i, dus, u, dp, h, dh, g, w_in, bdre, bdim, name):
    lp, d = h.shape
    tt = TOKEN_TILE
    nt = lp // tt
    cw, sw = bdre.shape[1], bdre.shape[2]

    def body(lr_ref, li_ref, dus_ref, u_ref, dpz_ref, h_ref, dh_ref, g_ref, w_hbm, bre_hbm, bim_hbm,
             dpu_ref, dho_ref, n_ref, dbre_hbm, dbim_hbm, dg_hbm, w, bre, bim, dn_sc, dbre, dbim, dg):
        i, c = pl.program_id(0), pl.program_id(1)

        @pl.when((i == 0) & (c == 0))
        def _():
            pltpu.sync_copy(w_hbm, w)
            pltpu.sync_copy(bre_hbm, bre)
            pltpu.sync_copy(bim_hbm, bim)
            dbre[...] = jnp.zeros_like(dbre)
            dbim[...] = jnp.zeros_like(dbim)
            dg[...] = jnp.zeros_like(dg)

        l_r, l_i, uv = _slab_load(lr_ref), _slab_load(li_ref), u_ref[...]
        du = dus_ref[...] + _dot_nt(l_r, bre[c]) + _dot_nt(l_i, bim[c])
        dbre[c] += _dot_tn(uv, l_r)
        dbim[c] += _dot_tn(uv, l_i)
        dpu_ref[0] = du.astype(BF16)
        part = _dot_nt(du, w[c])

        @pl.when(c == 0)
        def _():
            dn_sc[...] = part

        @pl.when(c > 0)
        def _():
            dn_sc[...] += part

        @pl.when(c == 3)
        def _():
            dz = dpz_ref[0]
            dn = dn_sc[...]
            for k in range(4):
                dn = dn + _dot_nt(dz[:, k * cw:(k + 1) * cw], w[4 + k])
            gv = g_ref[...]
            n, hh, rr = _rms_fwd(h_ref[...], gv)
            n_ref[...] = n.T.astype(BF16)
            dg[...] += jnp.sum(dn * hh, axis=0, keepdims=True)
            dho_ref[...] = dh_ref[...] + _rms_bwd(dn, hh, rr, gv)

        @pl.when((i == nt - 1) & (c == 3))
        def _():
            pltpu.sync_copy(dbre, dbre_hbm)
            pltpu.sync_copy(dbim, dbim_hbm)
            pltpu.sync_copy(dg, dg_hbm)

    row = lambda i, c: (i, 0)
    chunk = lambda i, c: (i, c)
    sd = jax.ShapeDtypeStruct
    slab, _ = _slab_spec(lp, tt, sw, lambda i, c: (i * 4 + c, 0))
    return pl.pallas_call(
        body, name=name, grid=(nt, 4),
        in_specs=[slab, slab, pl.BlockSpec((tt, cw), chunk),
                  pl.BlockSpec((tt, cw), chunk), pl.BlockSpec((1, tt, d), lambda i, c: (1, i, 0)),
                  pl.BlockSpec((tt, d), row), pl.BlockSpec((tt, d), row), pl.BlockSpec((1, d), lambda i, c: (0, 0)),
                  ANY, ANY, ANY],
        out_specs=[pl.BlockSpec((1, tt, cw), lambda i, c: (0, i, c)), pl.BlockSpec((tt, d), row),
                   pl.BlockSpec((d, tt), lambda i, c: (0, i)), ANY, ANY, ANY],
        out_shape=(sd(dp.shape, BF16), sd((lp, d), F32), sd((d, lp), BF16),
                   sd(bdre.shape, F32), sd(bdim.shape, F32), sd((1, d), F32)),
        input_output_aliases={4: 0},
        scratch_shapes=[pltpu.VMEM(w_in.shape, BF16), pltpu.VMEM(bdre.shape, BF16), pltpu.VMEM(bdim.shape, BF16),
                        pltpu.VMEM((tt, d), F32), pltpu.VMEM(bdre.shape, F32), pltpu.VMEM(bdim.shape, F32), pltpu.VMEM((1, d), F32)],
        compiler_params=_params(ndim=2),
    )(lam_r, lam_i, dus, u, dp, h, dh, g, w_in, bdre, bdim)


def grad_w_in(n_t, dp, blk, name):
    d, lp = n_t.shape
    npart, _, width = dp.shape
    tt = TOKEN_TILE
    per = width // blk

    def body(n_ref, dp_ref, o_ref):
        part = jnp.dot(n_ref[...], dp_ref[0], preferred_element_type=F32)

        @pl.when(pl.program_id(1) == 0)
        def _():
            o_ref[0] = part

        @pl.when(pl.program_id(1) > 0)
        def _():
            o_ref[0] += part

    return pl.pallas_call(
        body, name=name, grid=(npart * per, lp // tt),
        in_specs=[pl.BlockSpec((d, tt), lambda j, i: (0, i)), pl.BlockSpec((1, tt, blk), lambda j, i: (j // per, i, j % per))],
        out_specs=pl.BlockSpec((1, d, blk), lambda j, i: (j, 0, 0)),
        out_shape=jax.ShapeDtypeStruct((npart * per, d, blk), F32),
        compiler_params=_params(ndim=2),
    )(n_t, dp)


def _conv_fwd_chunk(n, w, cw_ref, cb_ref, halo, c, nch):
    bg = jnp.dot(n, w[c], preferred_element_type=F32)
    cg = jnp.dot(n, w[nch + c], preferred_element_type=F32)
    v = jnp.dot(n, w[2 * nch + c], preferred_element_type=F32)
    z = jnp.dot(n, w[3 * nch + c], preferred_element_type=F32)
    hc = cg * v
    taps = cw_ref[c]
    conv = taps[2:3, :] * hc + taps[1:2, :] * _shift_down(hc, 1, halo) + taps[0:1, :] * _shift_down(hc, 2, halo) + cb_ref[c]
    return bg, cg, v, z, hc, conv


def conv_fwd(h, g, w_in, conv_w, conv_b, w_out, name):
    lp, d = h.shape
    tt = TOKEN_TILE
    nt = lp // tt
    nch, ce = w_out.shape[0], w_out.shape[1]

    def body(h_ref, g_ref, cw_ref, cb_ref, w_hbm, wo_hbm, o_ref, halo_ref, w, wo, halo):
        i = pl.program_id(0)

        @pl.when(i == 0)
        def _():
            pltpu.sync_copy(w_hbm, w)
            pltpu.sync_copy(wo_hbm, wo)
            halo[...] = jnp.zeros_like(halo)

        hv = h_ref[...]
        n = _rms_fwd(hv, g_ref[...])[0].astype(BF16)
        o = hv
        for c in range(nch):
            bg, _, _, z, hc, conv = _conv_fwd_chunk(n, w, cw_ref, cb_ref, halo[c], c, nch)
            o = o + _dot(bg * conv * (z * _sigmoid(z)), wo[c])
            halo[c] = hc[tt - CONV_HALO:, :]
            halo_ref[0, c] = hc[tt - CONV_HALO:, :]
        o_ref[...] = o

    sd = jax.ShapeDtypeStruct
    return pl.pallas_call(
        body, name=name, grid=(nt,),
        in_specs=[pl.BlockSpec((tt, d), lambda i: (i, 0)), pl.BlockSpec((1, d), lambda i: (0, 0)),
                  pl.BlockSpec(conv_w.shape, lambda i: (0, 0, 0)), pl.BlockSpec(conv_b.shape, lambda i: (0, 0, 0)), ANY, ANY],
        out_specs=[pl.BlockSpec((tt, d), lambda i: (i, 0)), pl.BlockSpec((1, nch, CONV_HALO, ce), lambda i: (i, 0, 0, 0))],
        out_shape=(sd((lp, d), F32), sd((nt, nch, CONV_HALO, ce), F32)),
        scratch_shapes=[pltpu.VMEM(w_in.shape, BF16), pltpu.VMEM(w_out.shape, BF16), pltpu.VMEM((nch, CONV_HALO, ce), F32)],
        compiler_params=_params(),
    )(h, g, conv_w, conv_b, w_in, w_out)


def conv_bwd(h, dh, halos, g, w_in, conv_w, conv_b, w_out, name):
    lp, d = h.shape
    tt = TOKEN_TILE
    nt = lp // tt
    nch, ce = w_out.shape[0], w_out.shape[1]

    def body(h_ref, dh_ref, halo_ref, g_ref, cw_ref, cb_ref, w_hbm, wo_hbm,
             dho_ref, n_ref, dp_ref, dwo_hbm, dcw_hbm, dcb_hbm, dg_hbm, w, wo, nxt, dwo, dcw, dcb, dg):
        i = pl.program_id(0)

        @pl.when(i == 0)
        def _():
            pltpu.sync_copy(w_hbm, w)
            pltpu.sync_copy(wo_hbm, wo)
            for ref in (nxt, dwo, dcw, dcb, dg):
                ref[...] = jnp.zeros_like(ref)

        gv = g_ref[...]
        nf, hh, rr = _rms_fwd(h_ref[...], gv)
        n = nf.astype(BF16)
        n_ref[...] = nf.T.astype(BF16)
        dhv = dh_ref[...]
        has_prev = (i < nt - 1).astype(F32)
        dn = jnp.zeros((tt, d), F32)
        for c in range(nch):
            halo = halo_ref[0, c] * has_prev
            bg, cg, v, z, hc, conv = _conv_fwd_chunk(n, w, cw_ref, cb_ref, halo, c, nch)
            sz, dsz = _silu_and_grad(z)
            y1 = bg * conv
            dy2 = _dot_nt(dhv, wo[c])
            dwo[c] += _dot_tn(y1 * sz, dhv)
            dy1 = dy2 * sz
            dz = dy2 * y1 * dsz
            dbg = dy1 * conv
            dconv = dy1 * bg
            dcb[c] += jnp.sum(dconv, axis=0, keepdims=True)
            up1 = _shift_up(dconv, 1, nxt[c])
            up2 = _shift_up(dconv, 2, nxt[c])
            nxt[c] = dconv[:CONV_HALO, :]
            taps = cw_ref[c]
            dhc = taps[2:3, :] * dconv + taps[1:2, :] * up1 + taps[0:1, :] * up2
            dcw[c, 0:1, :] += jnp.sum(hc * up2, axis=0, keepdims=True)
            dcw[c, 1:2, :] += jnp.sum(hc * up1, axis=0, keepdims=True)
            dcw[c, 2:3, :] += jnp.sum(hc * dconv, axis=0, keepdims=True)
            dcg = dhc * v
            dv = dhc * cg
            cols = slice(c * ce, (c + 1) * ce)
            for p, val in enumerate((dbg, dcg, dv, dz)):
                dp_ref[p, :, cols] = val.astype(BF16)
                dn = dn + _dot_nt(val, w[p * nch + c])
        dg[...] += jnp.sum(dn * hh, axis=0, keepdims=True)
        dho_ref[...] = dhv + _rms_bwd(dn, hh, rr, gv)

        @pl.when(i == nt - 1)
        def _():
            pltpu.sync_copy(dwo, dwo_hbm)
            pltpu.sync_copy(dcw, dcw_hbm)
            pltpu.sync_copy(dcb, dcb_hbm)
            pltpu.sync_copy(dg, dg_hbm)

    rev = lambda i: (nt - 1 - i, 0)
    sd = jax.ShapeDtypeStruct
    return pl.pallas_call(
        body, name=name, grid=(nt,),
        in_specs=[pl.BlockSpec((tt, d), rev), pl.BlockSpec((tt, d), rev),
                  pl.BlockSpec((1, nch, CONV_HALO, ce), lambda i: (jnp.maximum(nt - 2 - i, 0), 0, 0, 0)),
                  pl.BlockSpec((1, d), lambda i: (0, 0)),
                  pl.BlockSpec(conv_w.shape, lambda i: (0, 0, 0)), pl.BlockSpec(conv_b.shape, lambda i: (0, 0, 0)), ANY, ANY],
        out_specs=[pl.BlockSpec((tt, d), rev), pl.BlockSpec((d, tt), lambda i: (0, nt - 1 - i)),
                   pl.BlockSpec((4, tt, nch * ce), lambda i: (0, nt - 1 - i, 0)), ANY, ANY, ANY, ANY],
        out_shape=(sd((lp, d), F32), sd((d, lp), BF16), sd((4, lp, nch * ce), BF16),
                   sd(w_out.shape, F32), sd((nch, 8, ce), F32), sd((nch, 1, ce), F32), sd((1, d), F32)),
        scratch_shapes=[pltpu.VMEM(w_in.shape, BF16), pltpu.VMEM(w_out.shape, BF16), pltpu.VMEM((nch, CONV_HALO, ce), F32),
                        pltpu.VMEM(w_out.shape, F32), pltpu.VMEM((nch, 8, ce), F32), pltpu.VMEM((nch, 1, ce), F32),
                        pltpu.VMEM((1, d), F32)],
        compiler_params=_params(),
    )(h, dh, halos, g, conv_w, conv_b, w_in, w_out)


def _pool_fwd_group(n, w, wg, bg_ref, sc_ref, halo, k, tile, tt, first_pos):
    u = jnp.dot(n, w[k], preferred_element_type=F32)
    z = jnp.dot(n, w[4 + k], preferred_element_type=F32)
    ext = jnp.concatenate([halo, u], axis=0)
    win = _window_sums_back(ext)[k][POOL_HALO:, :]
    mixed = win * _pool_inv_count(tile, tt, first_pos, POOL_WINDOWS[k], u.shape[1]) - u
    outs = _dot(mixed, wg[k]) + bg_ref[k]
    return u, z, mixed, outs, outs * sc_ref[k]


def pool_fwd(h, g, w_in, w_grp, b_grp, scale, w_out, first_pos, name):
    lp, d = h.shape
    tt = TOKEN_TILE
    nt = lp // tt
    gw = w_grp.shape[1]

    def body(h_ref, g_ref, bg_ref, sc_ref, w_hbm, wg_hbm, wo_hbm, o_ref, halo_ref, w, wg, wo, halo):
        i = pl.program_id(0)

        @pl.when(i == 0)
        def _():
            pltpu.sync_copy(w_hbm, w)
            pltpu.sync_copy(wg_hbm, wg)
            pltpu.sync_copy(wo_hbm, wo)
            halo[...] = jnp.zeros_like(halo)

        hv = h_ref[...]
        n = _rms_fwd(hv, g_ref[...])[0].astype(BF16)
        o = hv
        for k in range(4):
            u, z, _, _, yp = _pool_fwd_group(n, w, wg, bg_ref, sc_ref, halo[k], k, i, tt, first_pos)
            o = o + _dot(yp * (z * _sigmoid(z)), wo[k])
            halo[k] = u[tt - POOL_HALO:, :]
            halo_ref[0, k] = u[tt - POOL_HALO:, :]
        o_ref[...] = o

    sd = jax.ShapeDtypeStruct
    small = pl.BlockSpec((4, 1, gw), lambda i: (0, 0, 0))
    return pl.pallas_call(
        body, name=name, grid=(nt,),
        in_specs=[pl.BlockSpec((tt, d), lambda i: (i, 0)), pl.BlockSpec((1, d), lambda i: (0, 0)), small, small, ANY, ANY, ANY],
        out_specs=[pl.BlockSpec((tt, d), lambda i: (i, 0)), pl.BlockSpec((1, 4, POOL_HALO, gw), lambda i: (i, 0, 0, 0))],
        out_shape=(sd((lp, d), F32), sd((nt, 4, POOL_HALO, gw), F32)),
        scratch_shapes=[pltpu.VMEM(w_in.shape, BF16), pltpu.VMEM(w_grp.shape, BF16), pltpu.VMEM(w_out.shape, BF16),
                        pltpu.VMEM((4, POOL_HALO, gw), F32)],
        compiler_params=_params(),
    )(h, g, b_grp, scale, w_in, w_grp, w_out)


def pool_bwd(h, dh, halos, g, w_in, w_grp, b_grp, scale, w_out, first_pos, name):
    lp, d = h.shape
    tt = TOKEN_TILE
    nt = lp // tt
    gw = w_grp.shape[1]

    def body(h_ref, dh_ref, halo_ref, g_ref, bg_ref, sc_ref, w_hbm, wg_hbm, wo_hbm,
             dho_ref, n_ref, dp_ref, dwo_hbm, dwg_hbm, dbg_hbm, dsc_hbm, dg_hbm,
             w, wg, wo, nxt, dwo, dwg, dbg, dsc, dg):
        i = pl.program_id(0)
        tile = nt - 1 - i

        @pl.when(i == 0)
        def _():
            pltpu.sync_copy(w_hbm, w)
            pltpu.sync_copy(wg_hbm, wg)
            pltpu.sync_copy(wo_hbm, wo)
            for ref in (nxt, dwo, dwg, dbg, dsc, dg):
                ref[...] = jnp.zeros_like(ref)

        gv = g_ref[...]
        nf, hh, rr = _rms_fwd(h_ref[...], gv)
        n = nf.astype(BF16)
        n_ref[...] = nf.T.astype(BF16)
        dhv = dh_ref[...]
        has_prev = (i < nt - 1).astype(F32)
        dn = jnp.zeros((tt, d), F32)
        for k in range(4):
            u, z, mixed, outs, yp = _pool_fwd_group(n, w, wg, bg_ref, sc_ref, halo_ref[0, k] * has_prev, k, tile, tt, first_pos)
            sz, dsz = _silu_and_grad(z)
            dy = _dot_nt(dhv, wo[k])
            dwo[k] += _dot_tn(yp * sz, dhv)
            dyp = dy * sz
            dz = dy * yp * dsz
            dsc[k] += jnp.sum(dyp * outs, axis=0, keepdims=True)
            douts = dyp * sc_ref[k]
            dbg[k] += jnp.sum(douts, axis=0, keepdims=True)
            dwg[k] += _dot_tn(mixed, douts)
            dmixed = _dot_nt(douts, wg[k])
            dm = dmixed * _pool_inv_count(tile, tt, first_pos, POOL_WINDOWS[k], gw)
            ext = jnp.concatenate([dm, nxt[k]], axis=0)
            du = _window_sums_fwd(ext)[k][:tt, :] - dmixed
            nxt[k] = dm[:POOL_HALO, :]
            cols = slice(k * gw, (k + 1) * gw)
            dp_ref[0, :, cols] = du.astype(BF16)
            dp_ref[1, :, cols] = dz.astype(BF16)
            dn = dn + _dot_nt(du, w[k]) + _dot_nt(dz, w[4 + k])
        dg[...] += jnp.sum(dn * hh, axis=0, keepdims=True)
        dho_ref[...] = dhv + _rms_bwd(dn, hh, rr, gv)

        @pl.when(i == nt - 1)
        def _():
            pltpu.sync_copy(dwo, dwo_hbm)
            pltpu.sync_copy(dwg, dwg_hbm)
            pltpu.sync_copy(dbg, dbg_hbm)
            pltpu.sync_copy(dsc, dsc_hbm)
            pltpu.sync_copy(dg, dg_hbm)

    rev = lambda i: (nt - 1 - i, 0)
    sd = jax.ShapeDtypeStruct
    small = pl.BlockSpec((4, 1, gw), lambda i: (0, 0, 0))
    return pl.pallas_call(
        body, name=name, grid=(nt,),
        in_specs=[pl.BlockSpec((tt, d), rev), pl.BlockSpec((tt, d), rev),
                  pl.BlockSpec((1, 4, POOL_HALO, gw), lambda i: (jnp.maximum(nt - 2 - i, 0), 0, 0, 0)),
                  pl.BlockSpec((1, d), lambda i: (0, 0)), small, small, ANY, ANY, ANY],
        out_specs=[pl.BlockSpec((tt, d), rev), pl.BlockSpec((d, tt), lambda i: (0, nt - 1 - i)),
                   pl.BlockSpec((2, tt, 4 * gw), lambda i: (0, nt - 1 - i, 0)), ANY, ANY, ANY, ANY, ANY],
        out_shape=(sd((lp, d), F32), sd((d, lp), BF16), sd((2, lp, 4 * gw), BF16),
                   sd(w_out.shape, F32), sd(w_grp.shape, F32), sd((4, 1, gw), F32), sd((4, 1, gw), F32), sd((1, d), F32)),
        scratch_shapes=[pltpu.VMEM(w_in.shape, BF16), pltpu.VMEM(w_grp.shape, BF16), pltpu.VMEM(w_out.shape, BF16),
                        pltpu.VMEM((4, POOL_HALO, gw), F32), pltpu.VMEM(w_out.shape, F32), pltpu.VMEM(w_grp.shape, F32),
                        pltpu.VMEM((4, 1, gw), F32), pltpu.VMEM((4, 1, gw), F32), pltpu.VMEM((1, d), F32)],
        compiler_params=_params(),
    )(h, dh, halos, g, b_grp, scale, w_in, w_grp, w_out)


def loss_head(h, target, g, pad_tiles, name):
    lp, d = h.shape
    tt = TOKEN_TILE
    nt = lp // tt

    def body(h_ref, t_ref, g_ref, dh_ref, dg_ref, loss_ref, acc):
        i = pl.program_id(0)

        @pl.when(i == 0)
        def _():
            acc[...] = jnp.zeros_like(acc)
            dg_ref[...] = jnp.zeros_like(dg_ref)

        @pl.when(i < pad_tiles)
        def _():
            dh_ref[...] = jnp.zeros_like(dh_ref)

        @pl.when(i >= pad_tiles)
        def _():
            gv = g_ref[...]
            n, hh, rr = _rms_fwd(h_ref[...], gv)
            err = n - t_ref[...]
            acc[...] += 0.5 * jnp.sum(jnp.mean(err * err, axis=-1, keepdims=True), axis=0, keepdims=True)
            dn = err * (1.0 / d)
            dg_ref[...] += jnp.sum(dn * hh, axis=0, keepdims=True)
            dh_ref[...] = _rms_bwd(dn, hh, rr, gv)

        loss_ref[...] = jnp.broadcast_to(acc[...], loss_ref.shape)

    sd = jax.ShapeDtypeStruct
    return pl.pallas_call(
        body, name=name, grid=(nt,),
        in_specs=[pl.BlockSpec((tt, d), lambda i: (i, 0)), pl.BlockSpec((tt, d), lambda i: (jnp.maximum(i - pad_tiles, 0), 0)),
                  pl.BlockSpec((1, d), lambda i: (0, 0))],
        out_specs=[pl.BlockSpec((tt, d), lambda i: (i, 0)), pl.BlockSpec((1, d), lambda i: (0, 0)),
                   pl.BlockSpec((8, 128), lambda i: (0, 0))],
        out_shape=(sd((lp, d), F32), sd((1, d), F32), sd((8, 128), F32)),
        scratch_shapes=[pltpu.VMEM((1, 1), F32)],
        compiler_params=_params(),
    )(h, target, g)


def exchange(arrs, gather, name):
    n = len(arrs)

    def body(*refs):
        ins, outs = refs[:n], refs[n:2 * n]
        send_sems, recv_sems, own_sems = refs[2 * n:]
        x, y, c = lax.axis_index("x"), lax.axis_index("y"), lax.axis_index("c")
        me = 4 * x + 2 * y + c
        own = []
        for a in range(n):
            cp = pltpu.make_async_copy(ins[a] if gather else ins[a].at[me], outs[a].at[me], own_sems.at[a])
            cp.start()
            own.append(cp)
        sent = []
        for k in range(1, N_DEV):
            px = 1 - x if k & 4 else x
            py = 1 - y if k & 2 else y
            pc = 1 - c if k & 1 else c
            peer = 4 * px + 2 * py + pc
            for a in range(n):
                cp = pltpu.make_async_remote_copy(
                    src_ref=ins[a] if gather else ins[a].at[peer], dst_ref=outs[a].at[me],
                    send_sem=send_sems.at[a, k - 1], recv_sem=recv_sems.at[a, k - 1],
                    device_id=(px, py, pc), device_id_type=pl.DeviceIdType.MESH)
                cp.start()
                sent.append((cp, a, k, peer, (px, py, pc)))
        for cp, a, k, peer, pid in sent:
            cp.wait_send()
            pltpu.make_async_remote_copy(
                src_ref=ins[a] if gather else ins[a].at[peer], dst_ref=outs[a].at[peer],
                send_sem=send_sems.at[a, k - 1], recv_sem=recv_sems.at[a, k - 1],
                device_id=pid, device_id_type=pl.DeviceIdType.MESH).wait_recv()
        for cp in own:
            cp.wait()

    hbm = pl.BlockSpec(memory_space=pltpu.HBM)
    out_shape = tuple(jax.ShapeDtypeStruct(((N_DEV,) + a.shape) if gather else a.shape, a.dtype) for a in arrs)
    return pl.pallas_call(
        body, name=name, in_specs=[hbm] * n, out_specs=[hbm] * n, out_shape=out_shape,
        scratch_shapes=[pltpu.SemaphoreType.DMA((n, N_DEV - 1)), pltpu.SemaphoreType.DMA((n, N_DEV - 1)),
                        pltpu.SemaphoreType.DMA((n,))],
    )(*[pltpu.with_memory_space_constraint(a, pltpu.HBM) for a in arrs])


def _peers(x, y, c):
    out = []
    for k in range(1, N_DEV):
        px = 1 - x if k & 4 else x
        py = 1 - y if k & 2 else y
        pc = 1 - c if k & 1 else c
        out.append((k, (px, py, pc), 4 * px + 2 * py + pc))
    return out


def exchange_start(arrs, gather, name):
    n = len(arrs)
    me = 4 * lax.axis_index("x") + 2 * lax.axis_index("y") + lax.axis_index("c")
    lands = []
    for a in arrs:
        own = a[None] if gather else lax.dynamic_index_in_dim(a, me, 0, keepdims=True)
        lands.append(lax.dynamic_update_index_in_dim(lax.empty(((N_DEV,) + a.shape) if gather else a.shape, a.dtype), own, me, 0))

    def body(*refs):
        ins, land = refs[:n], refs[n:2 * n]
        send_sems, recv_sems, token = refs[2 * n], refs[2 * n + 1], refs[4 * n + 2]
        x, y, c = lax.axis_index("x"), lax.axis_index("y"), lax.axis_index("c")
        me = 4 * x + 2 * y + c
        for k, pid, peer in _peers(x, y, c):
            for a in range(n):
                pltpu.make_async_remote_copy(
                    src_ref=ins[a] if gather else ins[a].at[peer], dst_ref=land[a].at[me],
                    send_sem=send_sems.at[a * (N_DEV - 1) + k - 1], recv_sem=recv_sems.at[a * (N_DEV - 1) + k - 1],
                    device_id=pid, device_id_type=pl.DeviceIdType.MESH).start()
        token[...] = jnp.zeros_like(token)

    hbm = pl.BlockSpec(memory_space=pltpu.HBM)
    sem = pl.BlockSpec(memory_space=pltpu.SEMAPHORE)
    sems = pltpu.SemaphoreType.DMA((n * (N_DEV - 1),))
    res = pl.pallas_call(
        body, name=name, in_specs=[hbm] * (2 * n),
        out_specs=[sem, sem] + [hbm] * (2 * n) + [pl.BlockSpec(memory_space=pltpu.VMEM)],
        out_shape=[sems, sems] + [pltpu.HBM(a.shape, a.dtype) for a in arrs] + [pltpu.HBM(l.shape, l.dtype) for l in lands]
        + [jax.ShapeDtypeStruct((8, 128), F32)],
        input_output_aliases={a: 2 + a for a in range(2 * n)},
        compiler_params=pltpu.CompilerParams(has_side_effects=pltpu.SideEffectType.DATAFLOW_SIDE_EFFECTING),
    )(*[pltpu.with_memory_space_constraint(a, pltpu.HBM) for a in list(arrs) + lands])
    return res[0], res[1], res[2:2 + n], res[2 + n:2 + 2 * n], res[-1]


def exchange_wait(started, gather, after, name):
    send_sems, recv_sems, srcs, lands, _ = started
    n = len(srcs)

    def body(*refs):
        ins, land = refs[:n], refs[n:2 * n]
        send_sems, recv_sems = refs[2 * n], refs[2 * n + 1]
        x, y, c = lax.axis_index("x"), lax.axis_index("y"), lax.axis_index("c")
        for k, pid, peer in _peers(x, y, c):
            for a in range(n):
                cp = pltpu.make_async_remote_copy(
                    src_ref=ins[a] if gather else ins[a].at[peer], dst_ref=land[a].at[peer],
                    send_sem=send_sems.at[a * (N_DEV - 1) + k - 1], recv_sem=recv_sems.at[a * (N_DEV - 1) + k - 1],
                    device_id=pid, device_id_type=pl.DeviceIdType.MESH)
                cp.wait_send()
                cp.wait_recv()

    hbm = pl.BlockSpec(memory_space=pltpu.HBM)
    sem = pl.BlockSpec(memory_space=pltpu.SEMAPHORE)
    res = pl.pallas_call(
        body, name=name, in_specs=[hbm] * (2 * n) + [sem, sem, ANY],
        out_specs=[hbm] * (2 * n),
        out_shape=[pltpu.HBM(a.shape, a.dtype) for a in list(srcs) + list(lands)],
        input_output_aliases={a: a for a in range(2 * n)},
        compiler_params=pltpu.CompilerParams(has_side_effects=pltpu.SideEffectType.DATAFLOW_SIDE_EFFECTING),
    )(*srcs, *lands, send_sems, recv_sems, after)
    return res[n:]


def _adamw(w, g, m, v):
    m = ADAM_B1 * m + (1.0 - ADAM_B1) * g
    v = ADAM_B2 * v + (1.0 - ADAM_B2) * (g * g)
    m_hat = m / (1.0 - ADAM_B1 ** ADAM_STEP)
    v_hat = v / (1.0 - ADAM_B2 ** ADAM_STEP)
    return -ADAM_LR * (m_hat / (jnp.sqrt(v_hat) + ADAM_EPS) + ADAM_WD * w), m, v


def _update_tile_rows(rows, cols):
    if rows * cols <= UPDATE_TILE_ELEMS:
        return rows
    return max(t for t in range(8, UPDATE_TILE_ELEMS // cols + 1, 8) if rows % t == 0)


def _sum_in_order(p_ref):
    g = p_ref[0]
    for j in range(1, p_ref.shape[0]):
        g = g + p_ref[j]
    return g


def sum_parts(parts, name):
    nparts, rows, cols = parts.shape
    tr = _update_tile_rows(rows, cols)

    def body(p_ref, g_ref):
        g_ref[...] = _sum_in_order(p_ref)

    return pl.pallas_call(
        body, name=name, grid=(rows // tr,),
        in_specs=[pl.BlockSpec((nparts, tr, cols), lambda i: (0, i, 0))],
        out_specs=pl.BlockSpec((tr, cols), lambda i: (i, 0)), out_shape=jax.ShapeDtypeStruct((rows, cols), F32),
        compiler_params=_params(),
    )(parts)


def sum_adamw(parts, w, m, v, name):
    rows, cols = w.shape
    nparts = parts.shape[0]
    tr = _update_tile_rows(rows, cols)

    def body(p_ref, w_ref, m_ref, v_ref, g_ref, d_ref, nm_ref, nv_ref):
        g = _sum_in_order(p_ref)
        delta, nm, nv = _adamw(w_ref[...], g, m_ref[...], v_ref[...])
        g_ref[...] = g
        d_ref[...] = delta
        nm_ref[...] = nm
        nv_ref[...] = nv

    blk = pl.BlockSpec((tr, cols), lambda i: (i, 0))
    sd = jax.ShapeDtypeStruct((rows, cols), F32)
    return pl.pallas_call(
        body, name=name, grid=(rows // tr,),
        in_specs=[pl.BlockSpec((nparts, tr, cols), lambda i: (0, i, 0)), blk, blk, blk],
        out_specs=[blk] * 4, out_shape=(sd,) * 4,
        compiler_params=_params(),
    )(parts, w, m, v)


S5_NAMES = ("w_in", "lam_re", "lam_im", "log_dt", "b_re", "b_im", "c_re", "c_im", "d_skip", "w_glu", "b_glu", "w_out")
CONV_NAMES = ("w_in", "conv_w", "conv_b", "w_out")
POOL_NAMES = ("w_in", "w_grp", "b_grp", "scale", "w_out")
LAYER_KINDS = ("s5", "conv", "pool", "s5")
LAYER_NAMES = {"s5": S5_NAMES, "conv": CONV_NAMES, "pool": POOL_NAMES}
SHARDED = {"s5": ("w_in", "w_glu", "w_out"), "conv": ("w_in", "conv_w", "w_out"), "pool": ("w_in", "w_grp", "b_grp", "w_out")}
GATHER_F32 = ("conv_w", "b_grp")


def weight_names():
    names = ["meta_tokens"]
    for i, kind in enumerate(LAYER_KINDS):
        names.append("norm%d_g" % i)
        names += ["l%d_%s" % (i, n) for n in LAYER_NAMES[kind]]
    names.append("final_g")
    return names


def sharded_names():
    return ["meta_tokens"] + ["l%d_%s" % (i, n) for i, kind in enumerate(LAYER_KINDS) for n in SHARDED[kind]]


def _block_diag_in(bb_t, gc):
    i, g, p = bb_t.shape
    t = bb_t.reshape(i, 4, gc, p)
    return jnp.einsum("icjp,jk->cjikp", t, jnp.eye(gc, dtype=F32)).reshape(4, gc * i, gc * p)


def _block_diag_in_grad(dbd, gc):
    i, p = dbd.shape[1] // gc, dbd.shape[2] // gc
    return jnp.einsum("cjijp->icjp", dbd.reshape(4, gc, i, gc, p)).reshape(i, 4 * gc, p)


def _block_diag_out(cc, gc):
    g, i, p = cc.shape
    return jnp.einsum("cjip,jk->cjpki", cc.reshape(4, gc, i, p), jnp.eye(gc, dtype=F32)).reshape(4, gc * p, gc * i)


def _block_diag_out_grad(dcd, gc):
    p, i = dcd.shape[1] // gc, dcd.shape[2] // gc
    return jnp.einsum("cjpji->cjip", dcd.reshape(4, gc, p, gc, i)).reshape(4 * gc, i, p)


def _to_owner_blocks(a, axis):
    shape = a.shape[:axis] + (N_DEV, a.shape[axis] // N_DEV) + a.shape[axis + 1:]
    return jnp.moveaxis(a.reshape(shape), axis, 0)


def _from_owner_blocks(a, axis):
    a = jnp.moveaxis(a, 0, axis)
    return a.reshape(a.shape[:axis] + (a.shape[axis] * a.shape[axis + 1],) + a.shape[axis + 2:])


def _step(x, target, weights, moments_m, moments_v):
    seq, d = x.shape[1], x.shape[2]
    n_meta = weights["meta_tokens"].shape[0]
    tt = TOKEN_TILE
    pad_tiles = -(-n_meta // tt)
    p0 = pad_tiles * tt
    lp = p0 + seq
    first_pos = p0 - n_meta
    gc = d // 4 // S5_GROUP
    cw = d // 4

    big_names = [n for n in sharded_names() if n != "meta_tokens" and n.split("_", 1)[1] not in GATHER_F32]
    small_names = [n for n in sharded_names() if n not in big_names]
    layer_big = [[n for n in big_names if n.startswith("l%d_" % i)] for i in range(len(LAYER_KINDS))]
    layer_big[0] = small_names + layer_big[0]
    gather_started = [exchange_start([weights[n] if n in small_names else weights[n].astype(BF16) for n in names], True,
                                     "gather_start_l%d" % i) for i, names in enumerate(layer_big)]
    started = sum(st[4] for st in gather_started)
    gathered = dict(zip(layer_big[0], exchange_wait(gather_started[0], True, started, "gather_wait_l0")))

    meta = _from_owner_blocks(gathered["meta_tokens"], 1)
    h = jnp.concatenate([jnp.zeros((first_pos, d), F32), meta, x[0]], axis=0)

    def vec(name):
        return weights[name].reshape(1, -1)

    full = {}

    def layer_weights(i, kind, after):
        p = "l%d_" % i
        if i > 0:
            gathered.update(zip(layer_big[i], exchange_wait(gather_started[i], True, after, "gather_wait_l%d" % i)))
        w_in = gathered[p + "w_in"]
        if kind == "s5":
            lr, li = weights[p + "lam_re"], weights[p + "lam_im"]
            ldt = weights[p + "log_dt"].reshape(-1, 1)
            br_t = jnp.transpose(weights[p + "b_re"], (2, 0, 1))
            bi_t = jnp.transpose(weights[p + "b_im"], (2, 0, 1))
            ar, ai, bbr, bbi = s5_disc_fwd(lr, li, ldt, br_t, bi_t, p + "disc_fwd")
            full[i] = dict(
                w_in=w_in, disc=(lr, li, ldt, br_t, bi_t),
                ar=ar.reshape(4, -1, 128), ai=ai.reshape(4, -1, 128),
                bdre=_block_diag_in(bbr, gc).astype(BF16), bdim=_block_diag_in(bbi, gc).astype(BF16),
                cdre=_block_diag_out(weights[p + "c_re"], gc).astype(BF16),
                cdim=_block_diag_out(-weights[p + "c_im"], gc).astype(BF16),
                w_glu=gathered[p + "w_glu"].reshape(4, cw, d), w_out=gathered[p + "w_out"].reshape(4, cw, d),
                d_skip=weights[p + "d_skip"].reshape(4, 1, cw), b_glu=vec(p + "b_glu"))
        elif kind == "conv":
            ce = w_in.shape[2]
            nch = 2
            conv_w = _from_owner_blocks(gathered[p + "conv_w"], 1)
            full[i] = dict(
                w_in=w_in, conv_w=jnp.transpose(conv_w.reshape(CONV_K, nch, ce), (1, 0, 2)),
                conv_b=weights[p + "conv_b"].reshape(nch, 1, ce), w_out=gathered[p + "w_out"].reshape(nch, ce, d))
        else:
            gw = w_in.shape[2]
            full[i] = dict(
                w_in=w_in, w_grp=_from_owner_blocks(gathered[p + "w_grp"], 1),
                b_grp=_from_owner_blocks(gathered[p + "b_grp"], 1).reshape(4, 1, gw),
                scale=weights[p + "scale"].reshape(4, 1, gw), w_out=gathered[p + "w_out"].reshape(4, gw, d))
        return full[i]

    saved = {}
    for i, kind in enumerate(LAYER_KINDS):
        p, f, g = "l%d_" % i, layer_weights(i, kind, h), vec("norm%d_g" % i)
        if kind == "s5":
            u, z, xr, xi = s5_fwd1(h, g, f["w_in"], f["bdre"], f["bdim"], p + "fwd_in")
            sr, si = s5_scan_fwd(xr, xi, f["ar"], f["ai"], p + "scan_fwd")
            saved[i] = (h, u, z, sr, si)
            h = s5_fwd3(sr, si, u, z, h, f["cdre"], f["cdim"], f["w_glu"], f["w_out"], f["d_skip"], f["b_glu"], p + "fwd_out")
        elif kind == "conv":
            h_new, halos = conv_fwd(h, g, f["w_in"], f["conv_w"], f["conv_b"], f["w_out"], p + "fwd")
            saved[i] = (h, halos)
            h = h_new
        else:
            h_new, halos = pool_fwd(h, g, f["w_in"], f["w_grp"], f["b_grp"], f["scale"], f["w_out"], first_pos, p + "fwd")
            saved[i] = (h, halos)
            h = h_new

    dh, dg_final, loss_tile = loss_head(h, target[0], vec("final_g"), pad_tiles, "loss_head")
    loss = lax.psum(loss_tile[0, 0], ("x", "y", "c"))

    grads = {"final_g": dg_final}
    names = weight_names()
    sh_names = sharded_names()
    rep_names = [n for n in names if n not in sh_names]

    def owner_blocks(a):
        return a.reshape(N_DEV, -1, a.shape[-1])

    def as2d(a):
        return a.reshape(-1, a.shape[-1])

    def pack(tree):
        flat = [jnp.pad(tree[n].reshape(-1), (0, -tree[n].size % 1024)) for n in rep_names]
        flat = jnp.concatenate(flat)
        return jnp.pad(flat, (0, -flat.size % (PACK_ROWS * 128))).reshape(-1, 128)

    layer_sharded, scatter_started = {}, {}
    ordered = jnp.zeros((), F32)
    for i in reversed(range(len(LAYER_KINDS))):
        kind = LAYER_KINDS[i]
        p, f, g = "l%d_" % i, full[i], vec("norm%d_g" % i) + ordered
        if kind == "s5":
            h_in, u, z, sr, si = saved[i]
            dy, dp, dwo, dwg, dbg = s5_bwd3a(dh, sr, si, u, z, f["cdre"], f["cdim"], f["w_glu"], f["w_out"],
                                             f["d_skip"], f["b_glu"] + ordered, p + "bwd_out")
            dsr, dsi, dus, dcre, dcim, dd = s5_bwd3b(dy, sr, si, u, f["cdre"], f["cdim"], f["d_skip"], p + "bwd_read")
            lam_r, lam_i, dar, dai = s5_scan_bwd(dsr, dsi, sr, si, f["ar"], f["ai"], p + "scan_bwd")
            dp, dh, n, dbre, dbim, dg = s5_bwd1(lam_r, lam_i, dus, u, dp, h_in, dh, g, f["w_in"], f["bdre"], f["bdim"], p + "bwd_in")
            dw_in = grad_w_in(n, dp, f["w_in"].shape[2], p + "grad_w_in")
            lr, li, ldt, br_t, bi_t = f["disc"]
            dlr, dli, dldt, dbr_t, dbi_t = s5_disc_bwd(
                lr, li, ldt, br_t, bi_t, dar.reshape(lr.shape), dai.reshape(lr.shape),
                _block_diag_in_grad(dbre, gc), _block_diag_in_grad(dbim, gc), p + "disc_bwd")
            grads.update({
                p + "w_in": dw_in, p + "lam_re": dlr, p + "lam_im": dli, p + "log_dt": dldt,
                p + "b_re": jnp.transpose(dbr_t, (1, 2, 0)), p + "b_im": jnp.transpose(dbi_t, (1, 2, 0)),
                p + "c_re": _block_diag_out_grad(dcre, gc), p + "c_im": -_block_diag_out_grad(dcim, gc),
                p + "d_skip": dd, p + "w_glu": dwg.reshape(N_DEV, -1, d), p + "b_glu": dbg,
                p + "w_out": dwo.reshape(N_DEV, -1, d)})
        elif kind == "conv":
            h_in, halos = saved[i]
            dh, n, dp, dwo, dcw, dcb, dg = conv_bwd(h_in, dh, halos, g, f["w_in"], f["conv_w"], f["conv_b"], f["w_out"], p + "bwd")
            dw_in = grad_w_in(n, dp, f["w_in"].shape[2], p + "grad_w_in")
            dconv_w = jnp.transpose(dcw[:, :CONV_K, :], (1, 0, 2)).reshape(CONV_K, -1)
            grads.update({p + "w_in": dw_in, p + "conv_w": _to_owner_blocks(dconv_w, 1), p + "conv_b": dcb,
                          p + "w_out": dwo.reshape(N_DEV, -1, d)})
        else:
            h_in, halos = saved[i]
            dh, n, dp, dwo, dwgrp, dbgrp, dsc, dg = pool_bwd(h_in, dh, halos, g, f["w_in"], f["w_grp"], f["b_grp"], f["scale"],
                                                             f["w_out"], first_pos, p + "bwd")
            dw_in = grad_w_in(n, dp, f["w_in"].shape[2], p + "grad_w_in")
            grads.update({p + "w_in": dw_in, p + "w_grp": _to_owner_blocks(dwgrp, 1),
                          p + "b_grp": _to_owner_blocks(dbgrp.reshape(4, -1), 1), p + "scale": dsc,
                          p + "w_out": dwo.reshape(N_DEV, -1, d)})
        grads["norm%d_g" % i] = dg
        layer_sharded[i] = ["l%d_%s" % (i, n) for n in SHARDED[kind]]
        blocks = [owner_blocks(grads[n]) for n in layer_sharded[i]]
        if i == 0:
            grads["meta_tokens"] = _to_owner_blocks(dh[first_pos:p0], 1)
            layer_sharded[i] = layer_sharded[i] + ["meta_tokens", "replicated"]
            blocks += [owner_blocks(grads["meta_tokens"]), pack(grads).reshape(N_DEV, -1, 128)]
        scatter_started[i] = exchange_start(blocks, False, "scatter_start_l%d" % i)
        ordered = scatter_started[i][4][0, 0]
    grad_x = dh[p0:][None]

    out = {}
    received = {}
    for i in reversed(range(len(LAYER_KINDS))):
        after = scatter_started[0][4] if i == len(LAYER_KINDS) - 1 else out[layer_sharded[i + 1][0]][0]
        received.update(zip(layer_sharded[i], exchange_wait(scatter_started[i], False, after, "scatter_wait_l%d" % i)))
        for n in layer_sharded[i]:
            if n != "replicated":
                res = sum_adamw(received[n], as2d(weights[n]), as2d(moments_m[n]), as2d(moments_v[n]), "update_" + n)
                out[n] = [r.reshape(weights[n].shape) for r in res]

    g_full = exchange([sum_parts(received["replicated"], "sum_replicated")], True, "gather_small_grads")[0].reshape(1, -1, 128)
    packed = sum_adamw(g_full, pack(weights), pack(moments_m), pack(moments_v), "update_replicated")
    offset = 0
    for n in rep_names:
        size = weights[n].size
        out[n] = [r.reshape(-1)[offset:offset + size].reshape(weights[n].shape) for r in packed]
        offset += size + (-size % 1024)

    return (loss, grad_x) + tuple(out[n][k] for k in range(4) for n in names)


def kernel(x, *rest):
    names = weight_names()
    nw = len(names)
    weights = dict(zip(names, rest[:nw]))
    target = rest[nw]
    moments_m = dict(zip(names, rest[nw + 1:2 * nw + 1]))
    moments_v = dict(zip(names, rest[2 * nw + 1:3 * nw + 1]))
    return _step(x, target, weights, moments_m, moments_v)
```

```python
import functools
import math

import jax
import jax.numpy as jnp
from jax import lax
from jax.experimental import pallas as pl
from jax.experimental.pallas import tpu as pltpu

F32 = jnp.float32
BF16 = jnp.bfloat16
EPS = 1e-6
N_DEV = 8
TOKEN_TILE = 256
SCAN_CHUNKS = 2
S5_GROUP = 16
S5_STATE = 64
POOL_WINDOWS = (2, 4, 8, 16)
POOL_HALO = 16
CONV_K = 3
CONV_HALO = 8
ADAM_LR = 0.001
ADAM_B1 = 0.9
ADAM_B2 = 0.999
ADAM_EPS = 1e-08
ADAM_WD = 0.01
ADAM_STEP = 10
GELU_C = math.sqrt(2.0 / math.pi)
GELU_A = 0.044715
UPDATE_TILE_ELEMS = 1 << 17
PACK_ROWS = 512
VMEM_LIMIT = 56 << 20

ANY = pl.BlockSpec(memory_space=pl.ANY)


def _params(vmem=VMEM_LIMIT, ndim=1):
    return pltpu.CompilerParams(vmem_limit_bytes=vmem, dimension_semantics=("arbitrary",) * ndim)


def _dot(a, b):
    return jnp.dot(a.astype(BF16), b.astype(BF16), preferred_element_type=F32)


def _dot_nt(a, b):
    return lax.dot_general(a.astype(BF16), b.astype(BF16), (((1,), (1,)), ((), ())), preferred_element_type=F32)


def _dot_tn(a, b):
    return lax.dot_general(a.astype(BF16), b.astype(BF16), (((0,), (0,)), ((), ())), preferred_element_type=F32)


def _rms_fwd(h, g):
    r = lax.rsqrt(jnp.mean(h * h, axis=-1, keepdims=True) + EPS)
    hh = h * r
    return hh * g, hh, r


def _rms_bwd(dn, hh, r, g):
    dhh = dn * g
    return r * (dhh - hh * jnp.mean(dhh * hh, axis=-1, keepdims=True))


def _sigmoid(x):
    return 1.0 / (1.0 + jnp.exp(-x))


def _silu_and_grad(z):
    s = _sigmoid(z)
    return z * s, s * (1.0 + z * (1.0 - s))


def _gelu(y):
    t = jnp.tanh(GELU_C * (y + GELU_A * y * y * y))
    return 0.5 * y * (1.0 + t), t


def _gelu_grad(y, t):
    return 0.5 * (1.0 + t) + 0.5 * y * (1.0 - t * t) * GELU_C * (1.0 + 3.0 * GELU_A * y * y)


def _rows(shape):
    return lax.broadcasted_iota(jnp.int32, shape, 0)


def _shift_down(x, k, halo):
    y = pltpu.roll(x, k, 0)
    rows = _rows(x.shape)
    for j in range(k):
        y = jnp.where(rows == j, halo[halo.shape[0] - k + j:halo.shape[0] - k + j + 1, :], y)
    return y


def _shift_up(x, k, halo):
    n = x.shape[0]
    y = pltpu.roll(x, n - k, 0)
    rows = _rows(x.shape)
    for j in range(k):
        y = jnp.where(rows == n - k + j, halo[j:j + 1, :], y)
    return y


def _window_sums_back(ext):
    out = []
    s = ext
    for k in (1, 2, 4, 8):
        s = s + pltpu.roll(s, k, 0)
        out.append(s)
    return out


def _window_sums_fwd(ext):
    n = ext.shape[0]
    out = []
    s = ext
    for k in (1, 2, 4, 8):
        s = s + pltpu.roll(s, n - k, 0)
        out.append(s)
    return out


def _pool_inv_count(tile, tt, first_pos, w, width):
    pos = _rows((tt, width)) + (tile * tt - first_pos + 1)
    return 1.0 / jnp.clip(pos, 1, w).astype(F32)


def _slab_spec(lp, tt, sw, index_map):
    nj = sw // 128
    return pl.BlockSpec((tt * nj, 128), index_map), (lp * 4 * nj, 128)


def _pack_pair(re, im):
    def rounded(v):
        b = lax.bitcast_convert_type(v, jnp.int32)
        return b + 0x7FFF + (lax.shift_right_logical(b, 16) & 1)
    return lax.bitcast_convert_type((rounded(re) & -65536) | lax.shift_right_logical(rounded(im), 16), F32)


def _unpack_pair(w):
    b = lax.bitcast_convert_type(w, jnp.int32)
    return lax.bitcast_convert_type(b & -65536, F32), lax.bitcast_convert_type(lax.shift_left(b, 16), F32)


def _slab_load(ref):
    nj = ref.shape[0] // TOKEN_TILE
    return _unpack_pair(jnp.concatenate([ref[pl.ds(j, TOKEN_TILE, stride=nj), :] for j in range(nj)], axis=1))


def _slab_store(ref, re, im):
    nj = ref.shape[0] // TOKEN_TILE
    val = _pack_pair(re, im)
    for j in range(nj):
        ref[pl.ds(j, TOKEN_TILE, stride=nj), :] = val[:, j * 128:(j + 1) * 128]


def _s5_disc_math(lr, li, ldt, br, bi):
    dt = jnp.exp(ldt)
    mag = jnp.exp(lr * dt)
    ar = mag * jnp.cos(li * dt)
    ai = mag * jnp.sin(li * dt)
    den = lr * lr + li * li
    kr = ((ar - 1.0) * lr + ai * li) / den
    ki = (ai * lr - (ar - 1.0) * li) / den
    bbr = kr[None] * br - ki[None] * bi
    bbi = kr[None] * bi + ki[None] * br
    return ar, ai, bbr, bbi


def s5_disc_fwd(lr, li, ldt, br_t, bi_t, name):
    def body(lr_ref, li_ref, ldt_ref, br_ref, bi_ref, ar_ref, ai_ref, bbr_ref, bbi_ref):
        ar, ai, bbr, bbi = _s5_disc_math(lr_ref[...], li_ref[...], ldt_ref[...], br_ref[...], bi_ref[...])
        ar_ref[...] = ar
        ai_ref[...] = ai
        bbr_ref[...] = bbr
        bbi_ref[...] = bbi

    sd = jax.ShapeDtypeStruct
    return pl.pallas_call(
        body, name=name,
        out_shape=(sd(lr.shape, F32), sd(lr.shape, F32), sd(br_t.shape, F32), sd(br_t.shape, F32)),
    )(lr, li, ldt, br_t, bi_t)


def s5_disc_bwd(lr, li, ldt, br_t, bi_t, dar, dai, dbbr, dbbi, name):
    def body(lr_ref, li_ref, ldt_ref, br_ref, bi_ref, dar_ref, dai_ref, dbbr_ref, dbbi_ref,
             dlr_ref, dli_ref, dldt_ref, dbr_ref, dbi_ref):
        _, vjp = jax.vjp(_s5_disc_math, lr_ref[...], li_ref[...], ldt_ref[...], br_ref[...], bi_ref[...])
        dlr, dli, dldt, dbr, dbi = vjp((dar_ref[...], dai_ref[...], dbbr_ref[...], dbbi_ref[...]))
        dlr_ref[...] = dlr
        dli_ref[...] = dli
        dldt_ref[...] = dldt
        dbr_ref[...] = dbr
        dbi_ref[...] = dbi

    sd = jax.ShapeDtypeStruct
    return pl.pallas_call(
        body, name=name,
        out_shape=(sd(lr.shape, F32), sd(lr.shape, F32), sd(ldt.shape, F32), sd(br_t.shape, F32), sd(br_t.shape, F32)),
    )(lr, li, ldt, br_t, bi_t, dar, dai, dbbr, dbbi)


def s5_fwd1(h, g, w_in, bdre, bdim, name):
    lp, d = h.shape
    tt = TOKEN_TILE
    cw, sw = bdre.shape[1], bdre.shape[2]

    def body(h_ref, g_ref, w_hbm, bdre_hbm, bdim_hbm, u_ref, z_ref, x_ref, w, bre, bim, n_sc):
        i, c = pl.program_id(0), pl.program_id(1)

        @pl.when((i == 0) & (c == 0))
        def _():
            pltpu.sync_copy(w_hbm, w)
            pltpu.sync_copy(bdre_hbm, bre)
            pltpu.sync_copy(bdim_hbm, bim)

        @pl.when(c == 0)
        def _():
            n_sc[...] = _rms_fwd(h_ref[...], g_ref[...])[0].astype(BF16)

        n = n_sc[...]
        u = jnp.dot(n, w[c], preferred_element_type=F32)
        u_ref[...] = u
        z_ref[...] = jnp.dot(n, w[c + 4], preferred_element_type=F32)
        ub = u.astype(BF16)
        _slab_store(x_ref, jnp.dot(ub, bre[c], preferred_element_type=F32), jnp.dot(ub, bim[c], preferred_element_type=F32))

    sd = jax.ShapeDtypeStruct
    slab, slab_shape = _slab_spec(lp, tt, sw, lambda i, c: (i * 4 + c, 0))
    return pl.pallas_call(
        body, name=name, grid=(lp // tt, 4),
        in_specs=[pl.BlockSpec((tt, d), lambda i, c: (i, 0)), pl.BlockSpec((1, d), lambda i, c: (0, 0)), ANY, ANY, ANY],
        out_specs=[pl.BlockSpec((tt, cw), lambda i, c: (i, c)), pl.BlockSpec((tt, cw), lambda i, c: (i, c)), slab],
        out_shape=(sd((lp, d), F32), sd((lp, d), F32), sd(slab_shape, F32)),
        scratch_shapes=[pltpu.VMEM(w_in.shape, BF16), pltpu.VMEM(bdre.shape, BF16), pltpu.VMEM(bdim.shape, BF16),
                        pltpu.VMEM((tt, d), BF16)],
        compiler_params=_params(ndim=2),
    )(h, g, w_in, bdre, bdim)


def s5_scan_fwd(x, ar, ai, name):
    nj = ar.shape[1]
    tt = TOKEN_TILE
    cpb = SCAN_CHUNKS
    nt = x.shape[0] // (4 * tt * nj)

    def body(x_ref, ar_ref, ai_ref, s_ref, st_r, st_i):
        i, cg = pl.program_id(0), pl.program_id(1)

        @pl.when(i == 0)
        def _():
            for q in range(cpb):
                st_r[cg * cpb + q] = jnp.zeros((nj, 128), F32)
                st_i[cg * cpb + q] = jnp.zeros((nj, 128), F32)

        a_r = [ar_ref[cg * cpb + q] for q in range(cpb)]
        a_i = [ai_ref[cg * cpb + q] for q in range(cpb)]

        def step(t, carry):
            out = []
            for q in range(cpb):
                s_r, s_i = carry[q]
                rows = pl.ds(pl.multiple_of((q * tt + t) * nj, nj), nj)
                x_r, x_i = _unpack_pair(x_ref[rows, :])
                n_r = a_r[q] * s_r - a_i[q] * s_i + x_r
                n_i = a_r[q] * s_i + a_i[q] * s_r + x_i
                s_ref[rows, :] = _pack_pair(n_r, n_i)
                out.append((n_r, n_i))
            return tuple(out)

        init = tuple((st_r[cg * cpb + q], st_i[cg * cpb + q]) for q in range(cpb))
        final = lax.fori_loop(0, tt, step, init, unroll=8)
        for q in range(cpb):
            st_r[cg * cpb + q] = final[q][0]
            st_i[cg * cpb + q] = final[q][1]

    blk = pl.BlockSpec((cpb * tt * nj, 128), lambda i, cg: (i * (4 // cpb) + cg, 0))
    par = pl.BlockSpec((4, nj, 128), lambda i, cg: (0, 0, 0))
    sd = jax.ShapeDtypeStruct
    return pl.pallas_call(
        body, name=name, grid=(nt, 4 // cpb),
        in_specs=[blk, par, par], out_specs=blk,
        out_shape=sd(x.shape, F32),
        scratch_shapes=[pltpu.VMEM((4, nj, 128), F32), pltpu.VMEM((4, nj, 128), F32)],
        compiler_params=_params(ndim=2),
    )(x, ar, ai)


def _s5_mix_fwd(s_ref, u_ref, d_ref, cre, cim, c):
    s_r, s_i = _slab_load(s_ref)
    y = _dot(s_r, cre[c]) + _dot(s_i, cim[c]) + d_ref[c] * u_ref[...]
    gy, t = _gelu(y)
    return y, gy, t


def s5_fwd3(s, u, z, h, cdre, cdim, w_glu, w_out, d_skip, b_glu, name):
    lp, d = h.shape
    tt = TOKEN_TILE
    sw, cw = cdre.shape[1], cdre.shape[2]

    def body(s_ref, u_ref, z_ref, h_ref, d_ref, bg_ref, cre_hbm, cim_hbm, wg_hbm, wo_hbm,
             o_ref, cre, cim, wg, wo, gy_sc, q_sc):
        i, c = pl.program_id(0), pl.program_id(1)

        @pl.when((i == 0) & (c == 0))
        def _():
            pltpu.sync_copy(cre_hbm, cre)
            pltpu.sync_copy(cim_hbm, cim)
            pltpu.sync_copy(wg_hbm, wg)
            pltpu.sync_copy(wo_hbm, wo)

        _, gy, _ = _s5_mix_fwd(s_ref, u_ref, d_ref, cre, cim, c)
        gy_sc[c] = gy
        part = _dot(gy, wg[c])

        @pl.when(c == 0)
        def _():
            q_sc[...] = part

        @pl.when(c > 0)
        def _():
            q_sc[...] += part

        @pl.when(c == 3)
        def _():
            sig = _sigmoid(q_sc[...] + bg_ref[...])
            zz = z_ref[...]
            sz = zz * _sigmoid(zz)
            o = h_ref[...]
            for k in range(4):
                cols = slice(k * cw, (k + 1) * cw)
                o = o + _dot(gy_sc[k] * sig[:, cols] * sz[:, cols], wo[k])
            o_ref[...] = o

    row = lambda i, c: (i, 0)
    chunk = lambda i, c: (i, c)
    slab, _ = _slab_spec(lp, tt, sw, lambda i, c: (i * 4 + c, 0))
    return pl.pallas_call(
        body, name=name, grid=(lp // tt, 4),
        in_specs=[slab, pl.BlockSpec((tt, cw), chunk),
                  pl.BlockSpec((tt, d), row), pl.BlockSpec((tt, d), row),
                  pl.BlockSpec((4, 1, cw), lambda i, c: (0, 0, 0)), pl.BlockSpec((1, d), lambda i, c: (0, 0)),
                  ANY, ANY, ANY, ANY],
        out_specs=pl.BlockSpec((tt, d), row),
        out_shape=jax.ShapeDtypeStruct((lp, d), F32),
        scratch_shapes=[pltpu.VMEM(cdre.shape, BF16), pltpu.VMEM(cdim.shape, BF16), pltpu.VMEM(w_glu.shape, BF16),
                        pltpu.VMEM(w_out.shape, BF16), pltpu.VMEM((4, tt, cw), F32), pltpu.VMEM((tt, d), F32)],
        compiler_params=_params(ndim=2),
    )(s, u, z, h, d_skip, b_glu, cdre, cdim, w_glu, w_out)


def s5_bwd3a(dh, s, u, z, cdre, cdim, w_glu, w_out, d_skip, b_glu, name):
    lp, d = dh.shape
    tt = TOKEN_TILE
    nt = lp // tt
    sw, cw = cdre.shape[1], cdre.shape[2]

    def body(dh_ref, s_ref, u_ref, z_ref, d_ref, bg_ref, cre_hbm, cim_hbm, wg_hbm, wo_hbm,
             dy_ref, dp_ref, dwo_hbm, dwg_hbm, dbg_hbm,
             cre, cim, wg, wo, y_sc, t_sc, gy_sc, q_sc, dq_sc, dgy_sc, dwo, dwg, dbg):
        i, c = pl.program_id(0), pl.program_id(1)

        @pl.when((i == 0) & (c == 0))
        def _():
            pltpu.sync_copy(cre_hbm, cre)
            pltpu.sync_copy(cim_hbm, cim)
            pltpu.sync_copy(wg_hbm, wg)
            pltpu.sync_copy(wo_hbm, wo)
            dwo[...] = jnp.zeros_like(dwo)
            dwg[...] = jnp.zeros_like(dwg)
            dbg[...] = jnp.zeros_like(dbg)

        y, gy, t = _s5_mix_fwd(s_ref, u_ref, d_ref, cre, cim, c)
        y_sc[c] = y
        t_sc[c] = t
        gy_sc[c] = gy
        part = _dot(gy, wg[c])

        @pl.when(c == 0)
        def _():
            q_sc[...] = part

        @pl.when(c > 0)
        def _():
            q_sc[...] += part

        @pl.when(c == 3)
        def _():
            sig = _sigmoid(q_sc[...] + bg_ref[...])
            sz, dsz = _silu_and_grad(z_ref[...])
            dhv = dh_ref[...]
            for k in range(4):
                cols = slice(k * cw, (k + 1) * cw)
                gy_k, sig_k, sz_k = gy_sc[k], sig[:, cols], sz[:, cols]
                y2 = gy_k * sig_k
                dy3 = _dot_nt(dhv, wo[k])
                dwo[k] += _dot_tn(y2 * sz_k, dhv)
                dy2 = dy3 * sz_k
                dp_ref[0, :, cols] = (dy3 * y2 * dsz[:, cols]).astype(BF16)
                dq_sc[:, cols] = dy2 * gy_k * sig_k * (1.0 - sig_k)
                dgy_sc[k] = dy2 * sig_k
            dq = dq_sc[...]
            dbg[...] += jnp.sum(dq, axis=0, keepdims=True)
            for k in range(4):
                cols = slice(k * cw, (k + 1) * cw)
                dwg[k] += _dot_tn(gy_sc[k], dq)
                dgy = dgy_sc[k] + _dot_nt(dq, wg[k])
                dy_ref[:, cols] = dgy * _gelu_grad(y_sc[k], t_sc[k])

        @pl.when((i == nt - 1) & (c == 3))
        def _():
            pltpu.sync_copy(dwo, dwo_hbm)
            pltpu.sync_copy(dwg, dwg_hbm)
            pltpu.sync_copy(dbg, dbg_hbm)

    row = lambda i, c: (i, 0)
    chunk = lambda i, c: (i, c)
    sd = jax.ShapeDtypeStruct
    acc = pltpu.VMEM((4, tt, cw), F32)
    slab, _ = _slab_spec(lp, tt, sw, lambda i, c: (i * 4 + c, 0))
    return pl.pallas_call(
        body, name=name, grid=(nt, 4),
        in_specs=[pl.BlockSpec((tt, d), row), slab,
                  pl.BlockSpec((tt, cw), chunk), pl.BlockSpec((tt, d), row),
                  pl.BlockSpec((4, 1, cw), lambda i, c: (0, 0, 0)), pl.BlockSpec((1, d), lambda i, c: (0, 0)),
                  ANY, ANY, ANY, ANY],
        out_specs=[pl.BlockSpec((tt, d), row), pl.BlockSpec((1, tt, d), lambda i, c: (1, i, 0)), ANY, ANY, ANY],
        out_shape=(sd((lp, d), F32), sd((2, lp, d), BF16), sd(w_out.shape, F32), sd(w_glu.shape, F32), sd((1, d), F32)),
        scratch_shapes=[pltpu.VMEM(cdre.shape, BF16), pltpu.VMEM(cdim.shape, BF16), pltpu.VMEM(w_glu.shape, BF16),
                        pltpu.VMEM(w_out.shape, BF16), acc, acc, acc, pltpu.VMEM((tt, d), F32), pltpu.VMEM((tt, d), F32), acc,
                        pltpu.VMEM(w_out.shape, F32), pltpu.VMEM(w_glu.shape, F32), pltpu.VMEM((1, d), F32)],
        compiler_params=_params(ndim=2),
    )(dh, s, u, z, d_skip, b_glu, cdre, cdim, w_glu, w_out)


def s5_bwd3b(dy, s, u, cdre, cdim, d_skip, name):
    lp, d = dy.shape
    tt = TOKEN_TILE
    nt = lp // tt
    sw, cw = cdre.shape[1], cdre.shape[2]

    def body(dy_ref, s_ref, u_ref, d_ref, cre_hbm, cim_hbm,
             ds_ref, dus_ref, dcre_hbm, dcim_hbm, dd_hbm, cre, cim, dcre, dcim, dd):
        i, c = pl.program_id(0), pl.program_id(1)

        @pl.when((i == 0) & (c == 0))
        def _():
            pltpu.sync_copy(cre_hbm, cre)
            pltpu.sync_copy(cim_hbm, cim)
            dcre[...] = jnp.zeros_like(dcre)
            dcim[...] = jnp.zeros_like(dcim)
            dd[...] = jnp.zeros_like(dd)

        dyv = dy_ref[...]
        dd[c] += jnp.sum(dyv * u_ref[...], axis=0, keepdims=True)
        dus_ref[...] = dyv * d_ref[c]
        _slab_store(ds_ref, _dot_nt(dyv, cre[c]), _dot_nt(dyv, cim[c]))
        s_r, s_i = _slab_load(s_ref)
        dcre[c] += _dot_tn(s_r, dyv)
        dcim[c] += _dot_tn(s_i, dyv)

        @pl.when((i == nt - 1) & (c == 3))
        def _():
            pltpu.sync_copy(dcre, dcre_hbm)
            pltpu.sync_copy(dcim, dcim_hbm)
            pltpu.sync_copy(dd, dd_hbm)

    chunk = lambda i, c: (i, c)
    sd = jax.ShapeDtypeStruct
    slab, slab_shape = _slab_spec(lp, tt, sw, lambda i, c: (i * 4 + c, 0))
    return pl.pallas_call(
        body, name=name, grid=(nt, 4),
        in_specs=[pl.BlockSpec((tt, cw), chunk), slab,
                  pl.BlockSpec((tt, cw), chunk), pl.BlockSpec((4, 1, cw), lambda i, c: (0, 0, 0)), ANY, ANY],
        out_specs=[slab, pl.BlockSpec((tt, cw), chunk), ANY, ANY, ANY],
        out_shape=(sd(slab_shape, F32), sd((lp, d), F32),
                   sd(cdre.shape, F32), sd(cdim.shape, F32), sd((4, 1, cw), F32)),
        scratch_shapes=[pltpu.VMEM(cdre.shape, BF16), pltpu.VMEM(cdim.shape, BF16),
                        pltpu.VMEM(cdre.shape, F32), pltpu.VMEM(cdim.shape, F32), pltpu.VMEM((4, 1, cw), F32)],
        compiler_params=_params(ndim=2),
    )(dy, s, u, d_skip, cdre, cdim)


def s5_scan_bwd(g, s, ar, ai, name):
    nj = ar.shape[1]
    tt = TOKEN_TILE
    cpb = SCAN_CHUNKS
    nt = g.shape[0] // (4 * tt * nj)

    def body(g_ref, s_ref, ar_ref, ai_ref, lam_ref, dar_ref, dai_ref, st_r, st_i, acc_r, acc_i):
        i, cg = pl.program_id(0), pl.program_id(1)

        @pl.when((i == 0) & (cg == 0))
        def _():
            for ref in (st_r, st_i, acc_r, acc_i):
                ref[...] = jnp.zeros_like(ref)

        a_r = [ar_ref[cg * cpb + q] for q in range(cpb)]
        a_i = [ai_ref[cg * cpb + q] for q in range(cpb)]

        def slab(q, t):
            return pl.ds(pl.multiple_of((q * tt + t) * nj, nj), nj)

        def adjoint(q, t, l_r, l_i):
            rows = slab(q, t)
            g_r, g_i = _unpack_pair(g_ref[rows, :])
            n_r = g_r + a_r[q] * l_r + a_i[q] * l_i
            n_i = g_i + a_r[q] * l_i - a_i[q] * l_r
            lam_ref[rows, :] = _pack_pair(n_r, n_i)
            return n_r, n_i

        def pair(q, t, l_r, l_i, d_r, d_i):
            p_r, p_i = _unpack_pair(s_ref[slab(q, t), :])
            return d_r + l_r * p_r + l_i * p_i, d_i + l_i * p_r - l_r * p_i

        def step(k, carry):
            t = tt - 1 - k
            out = []
            for q in range(cpb):
                l_r, l_i, d_r, d_i = carry[q]
                l_r, l_i = adjoint(q, t, l_r, l_i)
                d_r, d_i = pair(q, t - 1, l_r, l_i, d_r, d_i)
                out.append((l_r, l_i, d_r, d_i))
            return tuple(out)

        init = []
        for q in range(cpb):
            ch = cg * cpb + q
            l_r, l_i = st_r[ch], st_i[ch]
            d_r, d_i = pair(q, tt - 1, l_r, l_i, acc_r[ch], acc_i[ch])
            init.append((l_r, l_i, d_r, d_i))
        final = lax.fori_loop(0, tt - 1, step, tuple(init), unroll=8)
        for q in range(cpb):
            ch = cg * cpb + q
            l_r, l_i, d_r, d_i = final[q]
            l_r, l_i = adjoint(q, 0, l_r, l_i)
            st_r[ch] = l_r
            st_i[ch] = l_i
            acc_r[ch] = d_r
            acc_i[ch] = d_i
            dar_ref[ch] = d_r
            dai_ref[ch] = d_i

    blk = pl.BlockSpec((cpb * tt * nj, 128), lambda i, cg: ((nt - 1 - i) * (4 // cpb) + cg, 0))
    par = pl.BlockSpec((4, nj, 128), lambda i, cg: (0, 0, 0))
    sd = jax.ShapeDtypeStruct
    return pl.pallas_call(
        body, name=name, grid=(nt, 4 // cpb),
        in_specs=[blk, blk, par, par], out_specs=[blk, par, par],
        out_shape=(sd(g.shape, F32), sd((4, nj, 128), F32), sd((4, nj, 128), F32)),
        scratch_shapes=[pltpu.VMEM((4, nj, 128), F32)] * 4,
        compiler_params=_params(ndim=2),
    )(g, s, ar, ai)


def s5_bwd1(lam, dus, u, dp, h, dh, g, w_in, bdre, bdim, name):
    lp, d = h.shape
    tt = TOKEN_TILE
    nt = lp // tt
    cw, sw = bdre.shape[1], bdre.shape[2]

    def body(lam_ref, dus_ref, u_ref, dpz_ref, h_ref, dh_ref, g_ref, w_hbm, bre_hbm, bim_hbm,
             dpu_ref, dho_ref, n_ref, dbre_hbm, dbim_hbm, dg_hbm, w, bre, bim, dn_sc, dbre, dbim, dg):
        i, c = pl.program_id(0), pl.program_id(1)

        @pl.when((i == 0) & (c == 0))
        def _():
            pltpu.sync_copy(w_hbm, w)
            pltpu.sync_copy(bre_hbm, bre)
            pltpu.sync_copy(bim_hbm, bim)
            dbre[...] = jnp.zeros_like(dbre)
            dbim[...] = jnp.zeros_like(dbim)
            dg[...] = jnp.zeros_like(dg)

        (l_r, l_i), uv = _slab_load(lam_ref), u_ref[...]
        du = dus_ref[...] + _dot_nt(l_r, bre[c]) + _dot_nt(l_i, bim[c])
        dbre[c] += _dot_tn(uv, l_r)
        dbim[c] += _dot_tn(uv, l_i)
        dpu_ref[0] = du.astype(BF16)
        part = _dot_nt(du, w[c])

        @pl.when(c == 0)
        def _():
            dn_sc[...] = part

        @pl.when(c > 0)
        def _():
            dn_sc[...] += part

        @pl.when(c == 3)
        def _():
            dz = dpz_ref[0]
            dn = dn_sc[...]
            for k in range(4):
                dn = dn + _dot_nt(dz[:, k * cw:(k + 1) * cw], w[4 + k])
            gv = g_ref[...]
            n, hh, rr = _rms_fwd(h_ref[...], gv)
            n_ref[...] = n.T.astype(BF16)
            dg[...] += jnp.sum(dn * hh, axis=0, keepdims=True)
            dho_ref[...] = dh_ref[...] + _rms_bwd(dn, hh, rr, gv)

        @pl.when((i == nt - 1) & (c == 3))
        def _():
            pltpu.sync_copy(dbre, dbre_hbm)
            pltpu.sync_copy(dbim, dbim_hbm)
            pltpu.sync_copy(dg, dg_hbm)

    row = lambda i, c: (i, 0)
    chunk = lambda i, c: (i, c)
    sd = jax.ShapeDtypeStruct
    slab, _ = _slab_spec(lp, tt, sw, lambda i, c: (i * 4 + c, 0))
    return pl.pallas_call(
        body, name=name, grid=(nt, 4),
        in_specs=[slab, pl.BlockSpec((tt, cw), chunk),
                  pl.BlockSpec((tt, cw), chunk), pl.BlockSpec((1, tt, d), lambda i, c: (1, i, 0)),
                  pl.BlockSpec((tt, d), row), pl.BlockSpec((tt, d), row), pl.BlockSpec((1, d), lambda i, c: (0, 0)),
                  ANY, ANY, ANY],
        out_specs=[pl.BlockSpec((1, tt, cw), lambda i, c: (0, i, c)), pl.BlockSpec((tt, d), row),
                   pl.BlockSpec((d, tt), lambda i, c: (0, i)), ANY, ANY, ANY],
        out_shape=(sd(dp.shape, BF16), sd((lp, d), F32), sd((d, lp), BF16),
                   sd(bdre.shape, F32), sd(bdim.shape, F32), sd((1, d), F32)),
        input_output_aliases={3: 0},
        scratch_shapes=[pltpu.VMEM(w_in.shape, BF16), pltpu.VMEM(bdre.shape, BF16), pltpu.VMEM(bdim.shape, BF16),
                        pltpu.VMEM((tt, d), F32), pltpu.VMEM(bdre.shape, F32), pltpu.VMEM(bdim.shape, F32), pltpu.VMEM((1, d), F32)],
        compiler_params=_params(ndim=2),
    )(lam, dus, u, dp, h, dh, g, w_in, bdre, bdim)


def grad_w_in(n_t, dp, blk, name):
    d, lp = n_t.shape
    npart, _, width = dp.shape
    tt = TOKEN_TILE
    per = width // blk

    def body(n_ref, dp_ref, o_ref):
        part = jnp.dot(n_ref[...], dp_ref[0], preferred_element_type=F32)

        @pl.when(pl.program_id(1) == 0)
        def _():
            o_ref[0] = part

        @pl.when(pl.program_id(1) > 0)
        def _():
            o_ref[0] += part

    return pl.pallas_call(
        body, name=name, grid=(npart * per, lp // tt),
        in_specs=[pl.BlockSpec((d, tt), lambda j, i: (0, i)), pl.BlockSpec((1, tt, blk), lambda j, i: (j // per, i, j % per))],
        out_specs=pl.BlockSpec((1, d, blk), lambda j, i: (j, 0, 0)),
        out_shape=jax.ShapeDtypeStruct((npart * per, d, blk), F32),
        compiler_params=_params(ndim=2),
    )(n_t, dp)


def _conv_fwd_chunk(n, w, cw_ref, cb_ref, halo, c, nch):
    bg = jnp.dot(n, w[c], preferred_element_type=F32)
    cg = jnp.dot(n, w[nch + c], preferred_element_type=F32)
    v = jnp.dot(n, w[2 * nch + c], preferred_element_type=F32)
    z = jnp.dot(n, w[3 * nch + c], preferred_element_type=F32)
    hc = cg * v
    taps = cw_ref[c]
    conv = taps[2:3, :] * hc + taps[1:2, :] * _shift_down(hc, 1, halo) + taps[0:1, :] * _shift_down(hc, 2, halo) + cb_ref[c]
    return bg, cg, v, z, hc, conv


def conv_fwd(h, g, w_in, conv_w, conv_b, w_out, name):
    lp, d = h.shape
    tt = TOKEN_TILE
    nt = lp // tt
    nch, ce = w_out.shape[0], w_out.shape[1]

    def body(h_ref, g_ref, cw_ref, cb_ref, w_hbm, wo_hbm, o_ref, halo_ref, w, wo, halo):
        i = pl.program_id(0)

        @pl.when(i == 0)
        def _():
            pltpu.sync_copy(w_hbm, w)
            pltpu.sync_copy(wo_hbm, wo)
            halo[...] = jnp.zeros_like(halo)

        hv = h_ref[...]
        n = _rms_fwd(hv, g_ref[...])[0].astype(BF16)
        o = hv
        for c in range(nch):
            bg, _, _, z, hc, conv = _conv_fwd_chunk(n, w, cw_ref, cb_ref, halo[c], c, nch)
            o = o + _dot(bg * conv * (z * _sigmoid(z)), wo[c])
            halo[c] = hc[tt - CONV_HALO:, :]
            halo_ref[0, c] = hc[tt - CONV_HALO:, :]
        o_ref[...] = o

    sd = jax.ShapeDtypeStruct
    return pl.pallas_call(
        body, name=name, grid=(nt,),
        in_specs=[pl.BlockSpec((tt, d), lambda i: (i, 0)), pl.BlockSpec((1, d), lambda i: (0, 0)),
                  pl.BlockSpec(conv_w.shape, lambda i: (0, 0, 0)), pl.BlockSpec(conv_b.shape, lambda i: (0, 0, 0)), ANY, ANY],
        out_specs=[pl.BlockSpec((tt, d), lambda i: (i, 0)), pl.BlockSpec((1, nch, CONV_HALO, ce), lambda i: (i, 0, 0, 0))],
        out_shape=(sd((lp, d), F32), sd((nt, nch, CONV_HALO, ce), F32)),
        scratch_shapes=[pltpu.VMEM(w_in.shape, BF16), pltpu.VMEM(w_out.shape, BF16), pltpu.VMEM((nch, CONV_HALO, ce), F32)],
        compiler_params=_params(),
    )(h, g, conv_w, conv_b, w_in, w_out)


def conv_bwd(h, dh, halos, g, w_in, conv_w, conv_b, w_out, name):
    lp, d = h.shape
    tt = TOKEN_TILE
    nt = lp // tt
    nch, ce = w_out.shape[0], w_out.shape[1]

    def body(h_ref, dh_ref, halo_ref, g_ref, cw_ref, cb_ref, w_hbm, wo_hbm,
             dho_ref, n_ref, dp_ref, dwo_hbm, dcw_hbm, dcb_hbm, dg_hbm, w, wo, nxt, dwo, dcw, dcb, dg):
        i = pl.program_id(0)

        @pl.when(i == 0)
        def _():
            pltpu.sync_copy(w_hbm, w)
            pltpu.sync_copy(wo_hbm, wo)
            for ref in (nxt, dwo, dcw, dcb, dg):
                ref[...] = jnp.zeros_like(ref)

        gv = g_ref[...]
        nf, hh, rr = _rms_fwd(h_ref[...], gv)
        n = nf.astype(BF16)
        n_ref[...] = nf.T.astype(BF16)
        dhv = dh_ref[...]
        has_prev = (i < nt - 1).astype(F32)
        dn = jnp.zeros((tt, d), F32)
        for c in range(nch):
            halo = halo_ref[0, c] * has_prev
            bg, cg, v, z, hc, conv = _conv_fwd_chunk(n, w, cw_ref, cb_ref, halo, c, nch)
            sz, dsz = _silu_and_grad(z)
            y1 = bg * conv
            dy2 = _dot_nt(dhv, wo[c])
            dwo[c] += _dot_tn(y1 * sz, dhv)
            dy1 = dy2 * sz
            dz = dy2 * y1 * dsz
            dbg = dy1 * conv
            dconv = dy1 * bg
            dcb[c] += jnp.sum(dconv, axis=0, keepdims=True)
            up1 = _shift_up(dconv, 1, nxt[c])
            up2 = _shift_up(dconv, 2, nxt[c])
            nxt[c] = dconv[:CONV_HALO, :]
            taps = cw_ref[c]
            dhc = taps[2:3, :] * dconv + taps[1:2, :] * up1 + taps[0:1, :] * up2
            dcw[c, 0:1, :] += jnp.sum(hc * up2, axis=0, keepdims=True)
            dcw[c, 1:2, :] += jnp.sum(hc * up1, axis=0, keepdims=True)
            dcw[c, 2:3, :] += jnp.sum(hc * dconv, axis=0, keepdims=True)
            dcg = dhc * v
            dv = dhc * cg
            cols = slice(c * ce, (c + 1) * ce)
            for p, val in enumerate((dbg, dcg, dv, dz)):
                dp_ref[p, :, cols] = val.astype(BF16)
                dn = dn + _dot_nt(val, w[p * nch + c])
        dg[...] += jnp.sum(dn * hh, axis=0, keepdims=True)
        dho_ref[...] = dhv + _rms_bwd(dn, hh, rr, gv)

        @pl.when(i == nt - 1)
        def _():
            pltpu.sync_copy(dwo, dwo_hbm)
            pltpu.sync_copy(dcw, dcw_hbm)
            pltpu.sync_copy(dcb, dcb_hbm)
            pltpu.sync_copy(dg, dg_hbm)

    rev = lambda i: (nt - 1 - i, 0)
    sd = jax.ShapeDtypeStruct
    return pl.pallas_call(
        body, name=name, grid=(nt,),
        in_specs=[pl.BlockSpec((tt, d), rev), pl.BlockSpec((tt, d), rev),
                  pl.BlockSpec((1, nch, CONV_HALO, ce), lambda i: (jnp.maximum(nt - 2 - i, 0), 0, 0, 0)),
                  pl.BlockSpec((1, d), lambda i: (0, 0)),
                  pl.BlockSpec(conv_w.shape, lambda i: (0, 0, 0)), pl.BlockSpec(conv_b.shape, lambda i: (0, 0, 0)), ANY, ANY],
        out_specs=[pl.BlockSpec((tt, d), rev), pl.BlockSpec((d, tt), lambda i: (0, nt - 1 - i)),
                   pl.BlockSpec((4, tt, nch * ce), lambda i: (0, nt - 1 - i, 0)), ANY, ANY, ANY, ANY],
        out_shape=(sd((lp, d), F32), sd((d, lp), BF16), sd((4, lp, nch * ce), BF16),
                   sd(w_out.shape, F32), sd((nch, 8, ce), F32), sd((nch, 1, ce), F32), sd((1, d), F32)),
        scratch_shapes=[pltpu.VMEM(w_in.shape, BF16), pltpu.VMEM(w_out.shape, BF16), pltpu.VMEM((nch, CONV_HALO, ce), F32),
                        pltpu.VMEM(w_out.shape, F32), pltpu.VMEM((nch, 8, ce), F32), pltpu.VMEM((nch, 1, ce), F32),
                        pltpu.VMEM((1, d), F32)],
        compiler_params=_params(),
    )(h, dh, halos, g, conv_w, conv_b, w_in, w_out)


def _pool_fwd_group(n, w, wg, bg_ref, sc_ref, halo, k, tile, tt, first_pos):
    u = jnp.dot(n, w[k], preferred_element_type=F32)
    z = jnp.dot(n, w[4 + k], preferred_element_type=F32)
    ext = jnp.concatenate([halo, u], axis=0)
    win = _window_sums_back(ext)[k][POOL_HALO:, :]
    mixed = win * _pool_inv_count(tile, tt, first_pos, POOL_WINDOWS[k], u.shape[1]) - u
    outs = _dot(mixed, wg[k]) + bg_ref[k]
    return u, z, mixed, outs, outs * sc_ref[k]


def pool_fwd(h, g, w_in, w_grp, b_grp, scale, w_out, first_pos, name):
    lp, d = h.shape
    tt = TOKEN_TILE
    nt = lp // tt
    gw = w_grp.shape[1]

    def body(h_ref, g_ref, bg_ref, sc_ref, w_hbm, wg_hbm, wo_hbm, o_ref, halo_ref, w, wg, wo, halo):
        i = pl.program_id(0)

        @pl.when(i == 0)
        def _():
            pltpu.sync_copy(w_hbm, w)
            pltpu.sync_copy(wg_hbm, wg)
            pltpu.sync_copy(wo_hbm, wo)
            halo[...] = jnp.zeros_like(halo)

        hv = h_ref[...]
        n = _rms_fwd(hv, g_ref[...])[0].astype(BF16)
        o = hv
        for k in range(4):
            u, z, _, _, yp = _pool_fwd_group(n, w, wg, bg_ref, sc_ref, halo[k], k, i, tt, first_pos)
            o = o + _dot(yp * (z * _sigmoid(z)), wo[k])
            halo[k] = u[tt - POOL_HALO:, :]
            halo_ref[0, k] = u[tt - POOL_HALO:, :]
        o_ref[...] = o

    sd = jax.ShapeDtypeStruct
    small = pl.BlockSpec((4, 1, gw), lambda i: (0, 0, 0))
    return pl.pallas_call(
        body, name=name, grid=(nt,),
        in_specs=[pl.BlockSpec((tt, d), lambda i: (i, 0)), pl.BlockSpec((1, d), lambda i: (0, 0)), small, small, ANY, ANY, ANY],
        out_specs=[pl.BlockSpec((tt, d), lambda i: (i, 0)), pl.BlockSpec((1, 4, POOL_HALO, gw), lambda i: (i, 0, 0, 0))],
        out_shape=(sd((lp, d), F32), sd((nt, 4, POOL_HALO, gw), F32)),
        scratch_shapes=[pltpu.VMEM(w_in.shape, BF16), pltpu.VMEM(w_grp.shape, BF16), pltpu.VMEM(w_out.shape, BF16),
                        pltpu.VMEM((4, POOL_HALO, gw), F32)],
        compiler_params=_params(),
    )(h, g, b_grp, scale, w_in, w_grp, w_out)


def pool_bwd(h, dh, halos, g, w_in, w_grp, b_grp, scale, w_out, first_pos, name):
    lp, d = h.shape
    tt = TOKEN_TILE
    nt = lp // tt
    gw = w_grp.shape[1]

    def body(h_ref, dh_ref, halo_ref, g_ref, bg_ref, sc_ref, w_hbm, wg_hbm, wo_hbm,
             dho_ref, n_ref, dp_ref, dwo_hbm, dwg_hbm, dbg_hbm, dsc_hbm, dg_hbm,
             w, wg, wo, nxt, dwo, dwg, dbg, dsc, dg):
        i = pl.program_id(0)
        tile = nt - 1 - i

        @pl.when(i == 0)
        def _():
            pltpu.sync_copy(w_hbm, w)
            pltpu.sync_copy(wg_hbm, wg)
            pltpu.sync_copy(wo_hbm, wo)
            for ref in (nxt, dwo, dwg, dbg, dsc, dg):
                ref[...] = jnp.zeros_like(ref)

        gv = g_ref[...]
        nf, hh, rr = _rms_fwd(h_ref[...], gv)
        n = nf.astype(BF16)
        n_ref[...] = nf.T.astype(BF16)
        dhv = dh_ref[...]
        has_prev = (i < nt - 1).astype(F32)
        dn = jnp.zeros((tt, d), F32)
        for k in range(4):
            u, z, mixed, outs, yp = _pool_fwd_group(n, w, wg, bg_ref, sc_ref, halo_ref[0, k] * has_prev, k, tile, tt, first_pos)
            sz, dsz = _silu_and_grad(z)
            dy = _dot_nt(dhv, wo[k])
            dwo[k] += _dot_tn(yp * sz, dhv)
            dyp = dy * sz
            dz = dy * yp * dsz
            dsc[k] += jnp.sum(dyp * outs, axis=0, keepdims=True)
            douts = dyp * sc_ref[k]
            dbg[k] += jnp.sum(douts, axis=0, keepdims=True)
            dwg[k] += _dot_tn(mixed, douts)
            dmixed = _dot_nt(douts, wg[k])
            dm = dmixed * _pool_inv_count(tile, tt, first_pos, POOL_WINDOWS[k], gw)
            ext = jnp.concatenate([dm, nxt[k]], axis=0)
            du = _window_sums_fwd(ext)[k][:tt, :] - dmixed
            nxt[k] = dm[:POOL_HALO, :]
            cols = slice(k * gw, (k + 1) * gw)
            dp_ref[0, :, cols] = du.astype(BF16)
            dp_ref[1, :, cols] = dz.astype(BF16)
            dn = dn + _dot_nt(du, w[k]) + _dot_nt(dz, w[4 + k])
        dg[...] += jnp.sum(dn * hh, axis=0, keepdims=True)
        dho_ref[...] = dhv + _rms_bwd(dn, hh, rr, gv)

        @pl.when(i == nt - 1)
        def _():
            pltpu.sync_copy(dwo, dwo_hbm)
            pltpu.sync_copy(dwg, dwg_hbm)
            pltpu.sync_copy(dbg, dbg_hbm)
            pltpu.sync_copy(dsc, dsc_hbm)
            pltpu.sync_copy(dg, dg_hbm)

    rev = lambda i: (nt - 1 - i, 0)
    sd = jax.ShapeDtypeStruct
    small = pl.BlockSpec((4, 1, gw), lambda i: (0, 0, 0))
    return pl.pallas_call(
        body, name=name, grid=(nt,),
        in_specs=[pl.BlockSpec((tt, d), rev), pl.BlockSpec((tt, d), rev),
                  pl.BlockSpec((1, 4, POOL_HALO, gw), lambda i: (jnp.maximum(nt - 2 - i, 0), 0, 0, 0)),
                  pl.BlockSpec((1, d), lambda i: (0, 0)), small, small, ANY, ANY, ANY],
        out_specs=[pl.BlockSpec((tt, d), rev), pl.BlockSpec((d, tt), lambda i: (0, nt - 1 - i)),
                   pl.BlockSpec((2, tt, 4 * gw), lambda i: (0, nt - 1 - i, 0)), ANY, ANY, ANY, ANY, ANY],
        out_shape=(sd((lp, d), F32), sd((d, lp), BF16), sd((2, lp, 4 * gw), BF16),
                   sd(w_out.shape, F32), sd(w_grp.shape, F32), sd((4, 1, gw), F32), sd((4, 1, gw), F32), sd((1, d), F32)),
        scratch_shapes=[pltpu.VMEM(w_in.shape, BF16), pltpu.VMEM(w_grp.shape, BF16), pltpu.VMEM(w_out.shape, BF16),
                        pltpu.VMEM((4, POOL_HALO, gw), F32), pltpu.VMEM(w_out.shape, F32), pltpu.VMEM(w_grp.shape, F32),
                        pltpu.VMEM((4, 1, gw), F32), pltpu.VMEM((4, 1, gw), F32), pltpu.VMEM((1, d), F32)],
        compiler_params=_params(),
    )(h, dh, halos, g, b_grp, scale, w_in, w_grp, w_out)


def loss_head(h, target, g, pad_tiles, name):
    lp, d = h.shape
    tt = TOKEN_TILE
    nt = lp // tt

    def body(h_ref, t_ref, g_ref, dh_ref, dg_ref, loss_ref, acc):
        i = pl.program_id(0)

        @pl.when(i == 0)
        def _():
            acc[...] = jnp.zeros_like(acc)
            dg_ref[...] = jnp.zeros_like(dg_ref)

        @pl.when(i < pad_tiles)
        def _():
            dh_ref[...] = jnp.zeros_like(dh_ref)

        @pl.when(i >= pad_tiles)
        def _():
            gv = g_ref[...]
            n, hh, rr = _rms_fwd(h_ref[...], gv)
            err = n - t_ref[...]
            acc[...] += 0.5 * jnp.sum(jnp.mean(err * err, axis=-1, keepdims=True), axis=0, keepdims=True)
            dn = err * (1.0 / d)
            dg_ref[...] += jnp.sum(dn * hh, axis=0, keepdims=True)
            dh_ref[...] = _rms_bwd(dn, hh, rr, gv)

        loss_ref[...] = jnp.broadcast_to(acc[...], loss_ref.shape)

    sd = jax.ShapeDtypeStruct
    return pl.pallas_call(
        body, name=name, grid=(nt,),
        in_specs=[pl.BlockSpec((tt, d), lambda i: (i, 0)), pl.BlockSpec((tt, d), lambda i: (jnp.maximum(i - pad_tiles, 0), 0)),
                  pl.BlockSpec((1, d), lambda i: (0, 0))],
        out_specs=[pl.BlockSpec((tt, d), lambda i: (i, 0)), pl.BlockSpec((1, d), lambda i: (0, 0)),
                   pl.BlockSpec((8, 128), lambda i: (0, 0))],
        out_shape=(sd((lp, d), F32), sd((1, d), F32), sd((8, 128), F32)),
        scratch_shapes=[pltpu.VMEM((1, 1), F32)],
        compiler_params=_params(),
    )(h, target, g)


def exchange(arrs, gather, name):
    n = len(arrs)

    def body(*refs):
        ins, outs = refs[:n], refs[n:2 * n]
        send_sems, recv_sems, own_sems = refs[2 * n:]
        x, y, c = lax.axis_index("x"), lax.axis_index("y"), lax.axis_index("c")
        me = 4 * x + 2 * y + c
        own = []
        for a in range(n):
            cp = pltpu.make_async_copy(ins[a] if gather else ins[a].at[me], outs[a].at[me], own_sems.at[a])
            cp.start()
            own.append(cp)
        sent = []
        for k in range(1, N_DEV):
            px = 1 - x if k & 4 else x
            py = 1 - y if k & 2 else y
            pc = 1 - c if k & 1 else c
            peer = 4 * px + 2 * py + pc
            for a in range(n):
                cp = pltpu.make_async_remote_copy(
                    src_ref=ins[a] if gather else ins[a].at[peer], dst_ref=outs[a].at[me],
                    send_sem=send_sems.at[a, k - 1], recv_sem=recv_sems.at[a, k - 1],
                    device_id=(px, py, pc), device_id_type=pl.DeviceIdType.MESH)
                cp.start()
                sent.append((cp, a, k, peer, (px, py, pc)))
        for cp, a, k, peer, pid in sent:
            cp.wait_send()
            pltpu.make_async_remote_copy(
                src_ref=ins[a] if gather else ins[a].at[peer], dst_ref=outs[a].at[peer],
                send_sem=send_sems.at[a, k - 1], recv_sem=recv_sems.at[a, k - 1],
                device_id=pid, device_id_type=pl.DeviceIdType.MESH).wait_recv()
        for cp in own:
            cp.wait()

    hbm = pl.BlockSpec(memory_space=pltpu.HBM)
    out_shape = tuple(jax.ShapeDtypeStruct(((N_DEV,) + a.shape) if gather else a.shape, a.dtype) for a in arrs)
    return pl.pallas_call(
        body, name=name, in_specs=[hbm] * n, out_specs=[hbm] * n, out_shape=out_shape,
        scratch_shapes=[pltpu.SemaphoreType.DMA((n, N_DEV - 1)), pltpu.SemaphoreType.DMA((n, N_DEV - 1)),
                        pltpu.SemaphoreType.DMA((n,))],
    )(*[pltpu.with_memory_space_constraint(a, pltpu.HBM) for a in arrs])


def _peers(x, y, c):
    out = []
    for k in range(1, N_DEV):
        px = 1 - x if k & 4 else x
        py = 1 - y if k & 2 else y
        pc = 1 - c if k & 1 else c
        out.append((k, (px, py, pc), 4 * px + 2 * py + pc))
    return out


def exchange_start(arrs, gather, after, name):
    n = len(arrs)
    me = 4 * lax.axis_index("x") + 2 * lax.axis_index("y") + lax.axis_index("c")
    lands = []
    for a in arrs:
        own = a[None] if gather else lax.dynamic_index_in_dim(a, me, 0, keepdims=True)
        lands.append(lax.dynamic_update_index_in_dim(lax.empty(((N_DEV,) + a.shape) if gather else a.shape, a.dtype), own, me, 0))

    def body(*refs):
        ins, land = refs[:n], refs[n:2 * n]
        send_sems, recv_sems, token = refs[2 * n + 1], refs[2 * n + 2], refs[4 * n + 3]
        x, y, c = lax.axis_index("x"), lax.axis_index("y"), lax.axis_index("c")
        me = 4 * x + 2 * y + c
        for k, pid, peer in _peers(x, y, c):
            for a in range(n):
                pltpu.make_async_remote_copy(
                    src_ref=ins[a] if gather else ins[a].at[peer], dst_ref=land[a].at[me],
                    send_sem=send_sems.at[a * (N_DEV - 1) + k - 1], recv_sem=recv_sems.at[a * (N_DEV - 1) + k - 1],
                    device_id=pid, device_id_type=pl.DeviceIdType.MESH).start()
        token[...] = jnp.zeros_like(token)

    hbm = pl.BlockSpec(memory_space=pltpu.HBM)
    sem = pl.BlockSpec(memory_space=pltpu.SEMAPHORE)
    sems = pltpu.SemaphoreType.DMA((n * (N_DEV - 1),))
    res = pl.pallas_call(
        body, name=name, in_specs=[hbm] * (2 * n) + [ANY],
        out_specs=[sem, sem] + [hbm] * (2 * n) + [pl.BlockSpec(memory_space=pltpu.VMEM)],
        out_shape=[sems, sems] + [pltpu.HBM(a.shape, a.dtype) for a in arrs] + [pltpu.HBM(l.shape, l.dtype) for l in lands]
        + [jax.ShapeDtypeStruct((8, 128), F32)],
        input_output_aliases={a: 2 + a for a in range(2 * n)},
        compiler_params=pltpu.CompilerParams(has_side_effects=pltpu.SideEffectType.DATAFLOW_SIDE_EFFECTING),
    )(*[pltpu.with_memory_space_constraint(a, pltpu.HBM) for a in list(arrs) + lands], after)
    return res[0], res[1], res[2:2 + n], res[2 + n:2 + 2 * n], res[-1]


def exchange_wait(started, gather, after, name):
    send_sems, recv_sems, srcs, lands, _ = started
    n = len(srcs)

    def body(*refs):
        ins, land = refs[:n], refs[n:2 * n]
        send_sems, recv_sems = refs[2 * n], refs[2 * n + 1]
        x, y, c = lax.axis_index("x"), lax.axis_index("y"), lax.axis_index("c")
        for k, pid, peer in _peers(x, y, c):
            for a in range(n):
                cp = pltpu.make_async_remote_copy(
                    src_ref=ins[a] if gather else ins[a].at[peer], dst_ref=land[a].at[peer],
                    send_sem=send_sems.at[a * (N_DEV - 1) + k - 1], recv_sem=recv_sems.at[a * (N_DEV - 1) + k - 1],
                    device_id=pid, device_id_type=pl.DeviceIdType.MESH)
                cp.wait_send()
                cp.wait_recv()

    hbm = pl.BlockSpec(memory_space=pltpu.HBM)
    sem = pl.BlockSpec(memory_space=pltpu.SEMAPHORE)
    res = pl.pallas_call(
        body, name=name, in_specs=[hbm] * (2 * n) + [sem, sem, ANY],
        out_specs=[hbm] * (2 * n),
        out_shape=[pltpu.HBM(a.shape, a.dtype) for a in list(srcs) + list(lands)],
        input_output_aliases={a: a for a in range(2 * n)},
        compiler_params=pltpu.CompilerParams(has_side_effects=pltpu.SideEffectType.DATAFLOW_SIDE_EFFECTING),
    )(*srcs, *lands, send_sems, recv_sems, after)
    return res[n:]


def _adamw(w, g, m, v):
    m = ADAM_B1 * m + (1.0 - ADAM_B1) * g
    v = ADAM_B2 * v + (1.0 - ADAM_B2) * (g * g)
    m_hat = m / (1.0 - ADAM_B1 ** ADAM_STEP)
    v_hat = v / (1.0 - ADAM_B2 ** ADAM_STEP)
    return -ADAM_LR * (m_hat / (jnp.sqrt(v_hat) + ADAM_EPS) + ADAM_WD * w), m, v


def _update_tile_rows(rows, cols):
    if rows * cols <= UPDATE_TILE_ELEMS:
        return rows
    return max(t for t in range(8, UPDATE_TILE_ELEMS // cols + 1, 8) if rows % t == 0)


def _sum_in_order(p_ref):
    g = p_ref[0]
    for j in range(1, p_ref.shape[0]):
        g = g + p_ref[j]
    return g


def sum_parts(parts, name):
    nparts, rows, cols = parts.shape
    tr = _update_tile_rows(rows, cols)

    def body(p_ref, g_ref):
        g_ref[...] = _sum_in_order(p_ref)

    return pl.pallas_call(
        body, name=name, grid=(rows // tr,),
        in_specs=[pl.BlockSpec((nparts, tr, cols), lambda i: (0, i, 0))],
        out_specs=pl.BlockSpec((tr, cols), lambda i: (i, 0)), out_shape=jax.ShapeDtypeStruct((rows, cols), F32),
        compiler_params=_params(),
    )(parts)


def sum_adamw(parts, w, m, v, name):
    rows, cols = w.shape
    nparts = parts.shape[0]
    tr = _update_tile_rows(rows, cols)

    def body(p_ref, w_ref, m_ref, v_ref, g_ref, d_ref, nm_ref, nv_ref):
        g = _sum_in_order(p_ref)
        delta, nm, nv = _adamw(w_ref[...], g, m_ref[...], v_ref[...])
        g_ref[...] = g
        d_ref[...] = delta
        nm_ref[...] = nm
        nv_ref[...] = nv

    blk = pl.BlockSpec((tr, cols), lambda i: (i, 0))
    sd = jax.ShapeDtypeStruct((rows, cols), F32)
    return pl.pallas_call(
        body, name=name, grid=(rows // tr,),
        in_specs=[pl.BlockSpec((nparts, tr, cols), lambda i: (0, i, 0)), blk, blk, blk],
        out_specs=[blk] * 4, out_shape=(sd,) * 4,
        compiler_params=_params(),
    )(parts, w, m, v)


S5_NAMES = ("w_in", "lam_re", "lam_im", "log_dt", "b_re", "b_im", "c_re", "c_im", "d_skip", "w_glu", "b_glu", "w_out")
CONV_NAMES = ("w_in", "conv_w", "conv_b", "w_out")
POOL_NAMES = ("w_in", "w_grp", "b_grp", "scale", "w_out")
LAYER_KINDS = ("s5", "conv", "pool", "s5")
LAYER_NAMES = {"s5": S5_NAMES, "conv": CONV_NAMES, "pool": POOL_NAMES}
SHARDED = {"s5": ("w_in", "w_glu", "w_out"), "conv": ("w_in", "conv_w", "w_out"), "pool": ("w_in", "w_grp", "b_grp", "w_out")}
GATHER_F32 = ("conv_w", "b_grp")


def weight_names():
    names = ["meta_tokens"]
    for i, kind in enumerate(LAYER_KINDS):
        names.append("norm%d_g" % i)
        names += ["l%d_%s" % (i, n) for n in LAYER_NAMES[kind]]
    names.append("final_g")
    return names


def sharded_names():
    return ["meta_tokens"] + ["l%d_%s" % (i, n) for i, kind in enumerate(LAYER_KINDS) for n in SHARDED[kind]]


def _block_diag_in(bb_t, gc):
    i, g, p = bb_t.shape
    t = bb_t.reshape(i, 4, gc, p)
    return jnp.einsum("icjp,jk->cjikp", t, jnp.eye(gc, dtype=F32)).reshape(4, gc * i, gc * p)


def _block_diag_in_grad(dbd, gc):
    i, p = dbd.shape[1] // gc, dbd.shape[2] // gc
    return jnp.einsum("cjijp->icjp", dbd.reshape(4, gc, i, gc, p)).reshape(i, 4 * gc, p)


def _block_diag_out(cc, gc):
    g, i, p = cc.shape
    return jnp.einsum("cjip,jk->cjpki", cc.reshape(4, gc, i, p), jnp.eye(gc, dtype=F32)).reshape(4, gc * p, gc * i)


def _block_diag_out_grad(dcd, gc):
    p, i = dcd.shape[1] // gc, dcd.shape[2] // gc
    return jnp.einsum("cjpji->cjip", dcd.reshape(4, gc, p, gc, i)).reshape(4 * gc, i, p)


def _to_owner_blocks(a, axis):
    shape = a.shape[:axis] + (N_DEV, a.shape[axis] // N_DEV) + a.shape[axis + 1:]
    return jnp.moveaxis(a.reshape(shape), axis, 0)


def _from_owner_blocks(a, axis):
    a = jnp.moveaxis(a, 0, axis)
    return a.reshape(a.shape[:axis] + (a.shape[axis] * a.shape[axis + 1],) + a.shape[axis + 2:])


def _step(x, target, weights, moments_m, moments_v):
    seq, d = x.shape[1], x.shape[2]
    n_meta = weights["meta_tokens"].shape[0]
    tt = TOKEN_TILE
    pad_tiles = -(-n_meta // tt)
    p0 = pad_tiles * tt
    lp = p0 + seq
    first_pos = p0 - n_meta
    gc = d // 4 // S5_GROUP
    cw = d // 4

    big_names = [n for n in sharded_names() if n != "meta_tokens" and n.split("_", 1)[1] not in GATHER_F32]
    small_names = [n for n in sharded_names() if n not in big_names]
    layer_big = [[n for n in big_names if n.startswith("l%d_" % i)] for i in range(len(LAYER_KINDS))]
    layer_big[0] = small_names + layer_big[0]
    gather_started = []
    after = jnp.zeros((8, 128), F32)
    for i, names in enumerate(layer_big):
        gather_started.append(exchange_start([weights[n] if n in small_names else weights[n].astype(BF16) for n in names], True,
                                             after, "gather_start_l%d" % i))
        after = gather_started[-1][4]

    def vec(name):
        return weights[name].reshape(1, -1)

    s5_prep = {}
    for i, kind in enumerate(LAYER_KINDS):
        if kind == "s5":
            p = "l%d_" % i
            lr, li = weights[p + "lam_re"], weights[p + "lam_im"] + after[0, 0]
            ldt = weights[p + "log_dt"].reshape(-1, 1)
            br_t = jnp.transpose(weights[p + "b_re"], (2, 0, 1))
            bi_t = jnp.transpose(weights[p + "b_im"], (2, 0, 1))
            ar, ai, bbr, bbi = s5_disc_fwd(lr, li, ldt, br_t, bi_t, p + "disc_fwd")
            s5_prep[i] = dict(
                disc=(lr, li, ldt, br_t, bi_t), ar=ar.reshape(4, -1, 128), ai=ai.reshape(4, -1, 128),
                bdre=_block_diag_in(bbr, gc).astype(BF16), bdim=_block_diag_in(bbi, gc).astype(BF16),
                cdre=_block_diag_out(weights[p + "c_re"], gc).astype(BF16),
                cdim=_block_diag_out(-weights[p + "c_im"], gc).astype(BF16),
                d_skip=weights[p + "d_skip"].reshape(4, 1, cw), b_glu=vec(p + "b_glu"))

    gathered = dict(zip(layer_big[0], exchange_wait(gather_started[0], True, s5_prep[max(s5_prep)]["cdim"], "gather_wait_l0")))
    meta = _from_owner_blocks(gathered["meta_tokens"], 1)
    h = jnp.concatenate([jnp.zeros((first_pos, d), F32), meta, x[0]], axis=0)

    full = {}

    def layer_weights(i, kind, after):
        p = "l%d_" % i
        if i > 0:
            gathered.update(zip(layer_big[i], exchange_wait(gather_started[i], True, after, "gather_wait_l%d" % i)))
        w_in = gathered[p + "w_in"]
        if kind == "s5":
            full[i] = dict(s5_prep[i], w_in=w_in, w_glu=gathered[p + "w_glu"].reshape(4, cw, d),
                           w_out=gathered[p + "w_out"].reshape(4, cw, d))
        elif kind == "conv":
            ce = w_in.shape[2]
            nch = 2
            conv_w = _from_owner_blocks(gathered[p + "conv_w"], 1)
            full[i] = dict(
                w_in=w_in, conv_w=jnp.transpose(conv_w.reshape(CONV_K, nch, ce), (1, 0, 2)),
                conv_b=weights[p + "conv_b"].reshape(nch, 1, ce), w_out=gathered[p + "w_out"].reshape(nch, ce, d))
        else:
            gw = w_in.shape[2]
            full[i] = dict(
                w_in=w_in, w_grp=_from_owner_blocks(gathered[p + "w_grp"], 1),
                b_grp=_from_owner_blocks(gathered[p + "b_grp"], 1).reshape(4, 1, gw),
                scale=weights[p + "scale"].reshape(4, 1, gw), w_out=gathered[p + "w_out"].reshape(4, gw, d))
        return full[i]

    saved = {}
    for i, kind in enumerate(LAYER_KINDS):
        p, f, g = "l%d_" % i, layer_weights(i, kind, h), vec("norm%d_g" % i)
        if kind == "s5":
            u, z, xs = s5_fwd1(h, g, f["w_in"], f["bdre"], f["bdim"], p + "fwd_in")
            s = s5_scan_fwd(xs, f["ar"], f["ai"], p + "scan_fwd")
            saved[i] = (h, u, z, s)
            h = s5_fwd3(s, u, z, h, f["cdre"], f["cdim"], f["w_glu"], f["w_out"], f["d_skip"], f["b_glu"], p + "fwd_out")
        elif kind == "conv":
            h_new, halos = conv_fwd(h, g, f["w_in"], f["conv_w"], f["conv_b"], f["w_out"], p + "fwd")
            saved[i] = (h, halos)
            h = h_new
        else:
            h_new, halos = pool_fwd(h, g, f["w_in"], f["w_grp"], f["b_grp"], f["scale"], f["w_out"], first_pos, p + "fwd")
            saved[i] = (h, halos)
            h = h_new

    dh, dg_final, loss_tile = loss_head(h, target[0], vec("final_g"), pad_tiles, "loss_head")
    loss = lax.psum(loss_tile[0, 0], ("x", "y", "c"))

    grads = {"final_g": dg_final}
    names = weight_names()
    sh_names = sharded_names()
    rep_names = [n for n in names if n not in sh_names]

    def owner_blocks(a):
        return a.reshape(N_DEV, -1, a.shape[-1])

    def as2d(a):
        return a.reshape(-1, a.shape[-1])

    def pack(tree):
        flat = [jnp.pad(tree[n].reshape(-1), (0, -tree[n].size % 1024)) for n in rep_names]
        flat = jnp.concatenate(flat)
        return jnp.pad(flat, (0, -flat.size % (PACK_ROWS * 128))).reshape(-1, 128)

    layer_sharded, scatter_started = {}, {}
    ordered = jnp.zeros((), F32)
    for i in reversed(range(len(LAYER_KINDS))):
        kind = LAYER_KINDS[i]
        p, f, g = "l%d_" % i, full[i], vec("norm%d_g" % i) + ordered
        if kind == "s5":
            h_in, u, z, s = saved[i]
            dy, dp, dwo, dwg, dbg = s5_bwd3a(dh, s, u, z, f["cdre"], f["cdim"], f["w_glu"], f["w_out"],
                                             f["d_skip"], f["b_glu"] + ordered, p + "bwd_out")
            ds, dus, dcre, dcim, dd = s5_bwd3b(dy, s, u, f["cdre"], f["cdim"], f["d_skip"], p + "bwd_read")
            lam, dar, dai = s5_scan_bwd(ds, s, f["ar"], f["ai"], p + "scan_bwd")
            dp, dh, n, dbre, dbim, dg = s5_bwd1(lam, dus, u, dp, h_in, dh, g, f["w_in"], f["bdre"], f["bdim"], p + "bwd_in")
            dw_in = grad_w_in(n, dp, f["w_in"].shape[2], p + "grad_w_in")
            grads.update({p + "w_in": dw_in, p + "w_glu": dwg.reshape(N_DEV, -1, d), p + "w_out": dwo.reshape(N_DEV, -1, d),
                          p + "d_skip": dd, p + "b_glu": dbg})

            def replicated_grads(p=p, f=f, dar=dar, dai=dai, dbre=dbre, dbim=dbim, dcre=dcre, dcim=dcim, token=None):
                lr, li, ldt, br_t, bi_t = f["disc"]
                dlr, dli, dldt, dbr_t, dbi_t = s5_disc_bwd(
                    lr, li, ldt, br_t, bi_t, dar.reshape(lr.shape) + token, dai.reshape(lr.shape),
                    _block_diag_in_grad(dbre, gc), _block_diag_in_grad(dbim, gc), p + "disc_bwd")
                grads.update({
                    p + "lam_re": dlr, p + "lam_im": dli, p + "log_dt": dldt,
                    p + "b_re": jnp.transpose(dbr_t, (1, 2, 0)), p + "b_im": jnp.transpose(dbi_t, (1, 2, 0)),
                    p + "c_re": _block_diag_out_grad(dcre, gc), p + "c_im": -_block_diag_out_grad(dcim, gc)})
        elif kind == "conv":
            replicated_grads = None
            h_in, halos = saved[i]
            dh, n, dp, dwo, dcw, dcb, dg = conv_bwd(h_in, dh, halos, g, f["w_in"], f["conv_w"], f["conv_b"], f["w_out"], p + "bwd")
            dw_in = grad_w_in(n, dp, f["w_in"].shape[2], p + "grad_w_in")
            dconv_w = jnp.transpose(dcw[:, :CONV_K, :], (1, 0, 2)).reshape(CONV_K, -1)
            grads.update({p + "w_in": dw_in, p + "conv_w": _to_owner_blocks(dconv_w, 1), p + "conv_b": dcb,
                          p + "w_out": dwo.reshape(N_DEV, -1, d)})
        else:
            replicated_grads = None
            h_in, halos = saved[i]
            dh, n, dp, dwo, dwgrp, dbgrp, dsc, dg = pool_bwd(h_in, dh, halos, g, f["w_in"], f["w_grp"], f["b_grp"], f["scale"],
                                                             f["w_out"], first_pos, p + "bwd")
            dw_in = grad_w_in(n, dp, f["w_in"].shape[2], p + "grad_w_in")
            grads.update({p + "w_in": dw_in, p + "w_grp": _to_owner_blocks(dwgrp, 1),
                          p + "b_grp": _to_owner_blocks(dbgrp.reshape(4, -1), 1), p + "scale": dsc,
                          p + "w_out": dwo.reshape(N_DEV, -1, d)})
        grads["norm%d_g" % i] = dg
        layer_sharded[i] = ["l%d_%s" % (i, n) for n in SHARDED[kind]]
        scatter_started[i] = exchange_start([owner_blocks(grads[n]) for n in layer_sharded[i]], False, dh,
                                            "scatter_start_l%d" % i)
        ordered = scatter_started[i][4][0, 0]
        if replicated_grads is not None:
            replicated_grads(token=ordered)
    grad_x = dh[p0:][None]
    grads["meta_tokens"] = _to_owner_blocks(dh[first_pos:p0], 1)
    last = len(LAYER_KINDS)
    layer_sharded[last] = ["meta_tokens", "replicated"]
    scatter_started[last] = exchange_start([owner_blocks(grads["meta_tokens"]), pack(grads).reshape(N_DEV, -1, 128)], False,
                                           scatter_started[0][4], "scatter_start_replicated")

    out = {}
    received = {}
    after = scatter_started[last][4]
    for i in list(reversed(range(last))) + [last]:
        received.update(zip(layer_sharded[i], exchange_wait(scatter_started[i], False, after, "scatter_wait_%d" % i)))
        for n in layer_sharded[i]:
            if n != "replicated":
                res = sum_adamw(received[n], as2d(weights[n]), as2d(moments_m[n]), as2d(moments_v[n]), "update_" + n)
                out[n] = [r.reshape(weights[n].shape) for r in res]
                after = out[n][0]

    g_full = exchange([sum_parts(received["replicated"], "sum_replicated")], True, "gather_small_grads")[0].reshape(1, -1, 128)
    packed = sum_adamw(g_full, pack(weights), pack(moments_m), pack(moments_v), "update_replicated")
    offset = 0
    for n in rep_names:
        size = weights[n].size
        out[n] = [r.reshape(-1)[offset:offset + size].reshape(weights[n].shape) for r in packed]
        offset += size + (-size % 1024)

    return (loss, grad_x) + tuple(out[n][k] for k in range(4) for n in names)


def kernel(x, *rest):
    names = weight_names()
    nw = len(names)
    weights = dict(zip(names, rest[:nw]))
    target = rest[nw]
    moments_m = dict(zip(names, rest[nw + 1:2 * nw + 1]))
    moments_v = dict(zip(names, rest[2 * nw + 1:3 * nw + 1]))
    return _step(x, target, weights, moments_m, moments_v)
```

```python
import functools
import math

import jax
import jax.numpy as jnp
from jax import lax
from jax.experimental import pallas as pl
from jax.experimental.pallas import tpu as pltpu

F32 = jnp.float32
BF16 = jnp.bfloat16
EPS = 1e-6
N_DEV = 8
TOKEN_TILE = 256
SCAN_CHUNKS = 4
S5_GROUP = 16
S5_STATE = 64
POOL_WINDOWS = (2, 4, 8, 16)
POOL_HALO = 16
CONV_K = 3
CONV_HALO = 8
ADAM_LR = 0.001
ADAM_B1 = 0.9
ADAM_B2 = 0.999
ADAM_EPS = 1e-08
ADAM_WD = 0.01
ADAM_STEP = 10
GELU_C = math.sqrt(2.0 / math.pi)
GELU_A = 0.044715
UPDATE_TILE_ELEMS = 1 << 17
PACK_ROWS = 512
VMEM_LIMIT = 56 << 20

ANY = pl.BlockSpec(memory_space=pl.ANY)


def _params(vmem=VMEM_LIMIT, ndim=1):
    return pltpu.CompilerParams(vmem_limit_bytes=vmem, dimension_semantics=("arbitrary",) * ndim)


def _dot(a, b):
    return jnp.dot(a.astype(BF16), b.astype(BF16), preferred_element_type=F32)


def _dot_nt(a, b):
    return lax.dot_general(a.astype(BF16), b.astype(BF16), (((1,), (1,)), ((), ())), preferred_element_type=F32)


def _dot_tn(a, b):
    return lax.dot_general(a.astype(BF16), b.astype(BF16), (((0,), (0,)), ((), ())), preferred_element_type=F32)


def _rms_fwd(h, g):
    r = lax.rsqrt(jnp.mean(h * h, axis=-1, keepdims=True) + EPS)
    hh = h * r
    return hh * g, hh, r


def _rms_bwd(dn, hh, r, g):
    dhh = dn * g
    return r * (dhh - hh * jnp.mean(dhh * hh, axis=-1, keepdims=True))


def _sigmoid(x):
    return 1.0 / (1.0 + jnp.exp(-x))


def _silu_and_grad(z):
    s = _sigmoid(z)
    return z * s, s * (1.0 + z * (1.0 - s))


def _gelu(y):
    t = jnp.tanh(GELU_C * (y + GELU_A * y * y * y))
    return 0.5 * y * (1.0 + t), t


def _gelu_grad(y, t):
    return 0.5 * (1.0 + t) + 0.5 * y * (1.0 - t * t) * GELU_C * (1.0 + 3.0 * GELU_A * y * y)


def _rows(shape):
    return lax.broadcasted_iota(jnp.int32, shape, 0)


def _shift_down(x, k, halo):
    y = pltpu.roll(x, k, 0)
    rows = _rows(x.shape)
    for j in range(k):
        y = jnp.where(rows == j, halo[halo.shape[0] - k + j:halo.shape[0] - k + j + 1, :], y)
    return y


def _shift_up(x, k, halo):
    n = x.shape[0]
    y = pltpu.roll(x, n - k, 0)
    rows = _rows(x.shape)
    for j in range(k):
        y = jnp.where(rows == n - k + j, halo[j:j + 1, :], y)
    return y


def _window_sums_back(ext):
    out = []
    s = ext
    for k in (1, 2, 4, 8):
        s = s + pltpu.roll(s, k, 0)
        out.append(s)
    return out


def _window_sums_fwd(ext):
    n = ext.shape[0]
    out = []
    s = ext
    for k in (1, 2, 4, 8):
        s = s + pltpu.roll(s, n - k, 0)
        out.append(s)
    return out


def _pool_inv_count(tile, tt, first_pos, w, width):
    pos = _rows((tt, width)) + (tile * tt - first_pos + 1)
    return 1.0 / jnp.clip(pos, 1, w).astype(F32)


def _slab_spec(lp, tt, sw, index_map):
    nj = sw // 128
    return pl.BlockSpec((tt * nj, 128), index_map), (lp * 4 * nj, 128)


def _pack_pair(re, im):
    def rounded(v):
        return lax.bitcast_convert_type(v, jnp.int32) + 0x8000
    return lax.bitcast_convert_type((rounded(re) & -65536) | lax.shift_right_logical(rounded(im), 16), F32)


def _unpack_pair(w):
    b = lax.bitcast_convert_type(w, jnp.int32)
    return lax.bitcast_convert_type(b & -65536, F32), lax.bitcast_convert_type(lax.shift_left(b, 16), F32)


def _slab_load(ref):
    nj = ref.shape[0] // TOKEN_TILE
    return _unpack_pair(jnp.concatenate([ref[pl.ds(j, TOKEN_TILE, stride=nj), :] for j in range(nj)], axis=1))


def _slab_store(ref, re, im):
    nj = ref.shape[0] // TOKEN_TILE
    val = _pack_pair(re, im)
    for j in range(nj):
        ref[pl.ds(j, TOKEN_TILE, stride=nj), :] = val[:, j * 128:(j + 1) * 128]


def _s5_disc_math(lr, li, ldt, br, bi):
    dt = jnp.exp(ldt)
    mag = jnp.exp(lr * dt)
    ar = mag * jnp.cos(li * dt)
    ai = mag * jnp.sin(li * dt)
    den = lr * lr + li * li
    kr = ((ar - 1.0) * lr + ai * li) / den
    ki = (ai * lr - (ar - 1.0) * li) / den
    bbr = kr[None] * br - ki[None] * bi
    bbi = kr[None] * bi + ki[None] * br
    return ar, ai, bbr, bbi


def s5_disc_fwd(lr, li, ldt, br_t, bi_t, name):
    def body(lr_ref, li_ref, ldt_ref, br_ref, bi_ref, ar_ref, ai_ref, bbr_ref, bbi_ref):
        ar, ai, bbr, bbi = _s5_disc_math(lr_ref[...], li_ref[...], ldt_ref[...], br_ref[...], bi_ref[...])
        ar_ref[...] = ar
        ai_ref[...] = ai
        bbr_ref[...] = bbr
        bbi_ref[...] = bbi

    sd = jax.ShapeDtypeStruct
    return pl.pallas_call(
        body, name=name,
        out_shape=(sd(lr.shape, F32), sd(lr.shape, F32), sd(br_t.shape, F32), sd(br_t.shape, F32)),
    )(lr, li, ldt, br_t, bi_t)


def s5_disc_bwd(lr, li, ldt, br_t, bi_t, dar, dai, dbbr, dbbi, name):
    def body(lr_ref, li_ref, ldt_ref, br_ref, bi_ref, dar_ref, dai_ref, dbbr_ref, dbbi_ref,
             dlr_ref, dli_ref, dldt_ref, dbr_ref, dbi_ref):
        _, vjp = jax.vjp(_s5_disc_math, lr_ref[...], li_ref[...], ldt_ref[...], br_ref[...], bi_ref[...])
        dlr, dli, dldt, dbr, dbi = vjp((dar_ref[...], dai_ref[...], dbbr_ref[...], dbbi_ref[...]))
        dlr_ref[...] = dlr
        dli_ref[...] = dli
        dldt_ref[...] = dldt
        dbr_ref[...] = dbr
        dbi_ref[...] = dbi

    sd = jax.ShapeDtypeStruct
    return pl.pallas_call(
        body, name=name,
        out_shape=(sd(lr.shape, F32), sd(lr.shape, F32), sd(ldt.shape, F32), sd(br_t.shape, F32), sd(br_t.shape, F32)),
    )(lr, li, ldt, br_t, bi_t, dar, dai, dbbr, dbbi)


def s5_fwd1(h, g, w_in, bdre, bdim, name):
    lp, d = h.shape
    tt = TOKEN_TILE
    cw, sw = bdre.shape[1], bdre.shape[2]

    def body(h_ref, g_ref, w_hbm, bdre_hbm, bdim_hbm, u_ref, z_ref, x_ref, w, bre, bim, n_sc):
        i, c = pl.program_id(0), pl.program_id(1)

        @pl.when((i == 0) & (c == 0))
        def _():
            pltpu.sync_copy(w_hbm, w)
            pltpu.sync_copy(bdre_hbm, bre)
            pltpu.sync_copy(bdim_hbm, bim)

        @pl.when(c == 0)
        def _():
            n_sc[...] = _rms_fwd(h_ref[...], g_ref[...])[0].astype(BF16)

        n = n_sc[...]
        u = jnp.dot(n, w[c], preferred_element_type=F32)
        u_ref[...] = u
        z_ref[...] = jnp.dot(n, w[c + 4], preferred_element_type=F32)
        ub = u.astype(BF16)
        _slab_store(x_ref, jnp.dot(ub, bre[c], preferred_element_type=F32), jnp.dot(ub, bim[c], preferred_element_type=F32))

    sd = jax.ShapeDtypeStruct
    slab, slab_shape = _slab_spec(lp, tt, sw, lambda i, c: (i * 4 + c, 0))
    return pl.pallas_call(
        body, name=name, grid=(lp // tt, 4),
        in_specs=[pl.BlockSpec((tt, d), lambda i, c: (i, 0)), pl.BlockSpec((1, d), lambda i, c: (0, 0)), ANY, ANY, ANY],
        out_specs=[pl.BlockSpec((tt, cw), lambda i, c: (i, c)), pl.BlockSpec((tt, cw), lambda i, c: (i, c)), slab],
        out_shape=(sd((lp, d), F32), sd((lp, d), F32), sd(slab_shape, F32)),
        scratch_shapes=[pltpu.VMEM(w_in.shape, BF16), pltpu.VMEM(bdre.shape, BF16), pltpu.VMEM(bdim.shape, BF16),
                        pltpu.VMEM((tt, d), BF16)],
        compiler_params=_params(ndim=2),
    )(h, g, w_in, bdre, bdim)


def s5_scan_fwd(x, ar, ai, name):
    nj = ar.shape[1]
    tt = TOKEN_TILE
    cpb = SCAN_CHUNKS
    nt = x.shape[0] // (4 * tt * nj)

    def body(x_ref, ar_ref, ai_ref, s_ref, st_r, st_i):
        i, cg = pl.program_id(0), pl.program_id(1)

        @pl.when(i == 0)
        def _():
            for q in range(cpb):
                st_r[cg * cpb + q] = jnp.zeros((nj, 128), F32)
                st_i[cg * cpb + q] = jnp.zeros((nj, 128), F32)

        a_r = [ar_ref[cg * cpb + q] for q in range(cpb)]
        a_i = [ai_ref[cg * cpb + q] for q in range(cpb)]

        def step(t, carry):
            out = []
            for q in range(cpb):
                s_r, s_i = carry[q]
                rows = pl.ds(pl.multiple_of((q * tt + t) * nj, nj), nj)
                x_r, x_i = _unpack_pair(x_ref[rows, :])
                n_r = a_r[q] * s_r - a_i[q] * s_i + x_r
                n_i = a_r[q] * s_i + a_i[q] * s_r + x_i
                s_ref[rows, :] = _pack_pair(n_r, n_i)
                out.append((n_r, n_i))
            return tuple(out)

        init = tuple((st_r[cg * cpb + q], st_i[cg * cpb + q]) for q in range(cpb))
        final = lax.fori_loop(0, tt, step, init, unroll=8)
        for q in range(cpb):
            st_r[cg * cpb + q] = final[q][0]
            st_i[cg * cpb + q] = final[q][1]

    blk = pl.BlockSpec((cpb * tt * nj, 128), lambda i, cg: (i * (4 // cpb) + cg, 0))
    par = pl.BlockSpec((4, nj, 128), lambda i, cg: (0, 0, 0))
    sd = jax.ShapeDtypeStruct
    return pl.pallas_call(
        body, name=name, grid=(nt, 4 // cpb),
        in_specs=[blk, par, par], out_specs=blk,
        out_shape=sd(x.shape, F32),
        scratch_shapes=[pltpu.VMEM((4, nj, 128), F32), pltpu.VMEM((4, nj, 128), F32)],
        compiler_params=_params(ndim=2),
    )(x, ar, ai)


def _s5_mix_fwd(s_ref, u_ref, d_ref, cre, cim, c):
    s_r, s_i = _slab_load(s_ref)
    y = _dot(s_r, cre[c]) + _dot(s_i, cim[c]) + d_ref[c] * u_ref[...]
    gy, t = _gelu(y)
    return y, gy, t


def s5_fwd3(s, u, z, h, cdre, cdim, w_glu, w_out, d_skip, b_glu, name):
    lp, d = h.shape
    tt = TOKEN_TILE
    sw, cw = cdre.shape[1], cdre.shape[2]

    def body(s_ref, u_ref, z_ref, h_ref, d_ref, bg_ref, cre_hbm, cim_hbm, wg_hbm, wo_hbm,
             o_ref, cre, cim, wg, wo, gy_sc, q_sc):
        i, c = pl.program_id(0), pl.program_id(1)

        @pl.when((i == 0) & (c == 0))
        def _():
            pltpu.sync_copy(cre_hbm, cre)
            pltpu.sync_copy(cim_hbm, cim)
            pltpu.sync_copy(wg_hbm, wg)
            pltpu.sync_copy(wo_hbm, wo)

        _, gy, _ = _s5_mix_fwd(s_ref, u_ref, d_ref, cre, cim, c)
        gy_sc[c] = gy
        part = _dot(gy, wg[c])

        @pl.when(c == 0)
        def _():
            q_sc[...] = part

        @pl.when(c > 0)
        def _():
            q_sc[...] += part

        @pl.when(c == 3)
        def _():
            sig = _sigmoid(q_sc[...] + bg_ref[...])
            zz = z_ref[...]
            sz = zz * _sigmoid(zz)
            o = h_ref[...]
            for k in range(4):
                cols = slice(k * cw, (k + 1) * cw)
                o = o + _dot(gy_sc[k] * sig[:, cols] * sz[:, cols], wo[k])
            o_ref[...] = o

    row = lambda i, c: (i, 0)
    chunk = lambda i, c: (i, c)
    slab, _ = _slab_spec(lp, tt, sw, lambda i, c: (i * 4 + c, 0))
    return pl.pallas_call(
        body, name=name, grid=(lp // tt, 4),
        in_specs=[slab, pl.BlockSpec((tt, cw), chunk),
                  pl.BlockSpec((tt, d), row), pl.BlockSpec((tt, d), row),
                  pl.BlockSpec((4, 1, cw), lambda i, c: (0, 0, 0)), pl.BlockSpec((1, d), lambda i, c: (0, 0)),
                  ANY, ANY, ANY, ANY],
        out_specs=pl.BlockSpec((tt, d), row),
        out_shape=jax.ShapeDtypeStruct((lp, d), F32),
        scratch_shapes=[pltpu.VMEM(cdre.shape, BF16), pltpu.VMEM(cdim.shape, BF16), pltpu.VMEM(w_glu.shape, BF16),
                        pltpu.VMEM(w_out.shape, BF16), pltpu.VMEM((4, tt, cw), F32), pltpu.VMEM((tt, d), F32)],
        compiler_params=_params(ndim=2),
    )(s, u, z, h, d_skip, b_glu, cdre, cdim, w_glu, w_out)


def s5_bwd3a(dh, s, u, z, cdre, cdim, w_glu, w_out, d_skip, b_glu, name):
    lp, d = dh.shape
    tt = TOKEN_TILE
    nt = lp // tt
    sw, cw = cdre.shape[1], cdre.shape[2]

    def body(dh_ref, s_ref, u_ref, z_ref, d_ref, bg_ref, cre_hbm, cim_hbm, wg_hbm, wo_hbm,
             dy_ref, dp_ref, dwo_hbm, dwg_hbm, dbg_hbm,
             cre, cim, wg, wo, y_sc, t_sc, gy_sc, q_sc, dq_sc, dgy_sc, dwo, dwg, dbg):
        i, c = pl.program_id(0), pl.program_id(1)

        @pl.when((i == 0) & (c == 0))
        def _():
            pltpu.sync_copy(cre_hbm, cre)
            pltpu.sync_copy(cim_hbm, cim)
            pltpu.sync_copy(wg_hbm, wg)
            pltpu.sync_copy(wo_hbm, wo)
            dwo[...] = jnp.zeros_like(dwo)
            dwg[...] = jnp.zeros_like(dwg)
            dbg[...] = jnp.zeros_like(dbg)

        y, gy, t = _s5_mix_fwd(s_ref, u_ref, d_ref, cre, cim, c)
        y_sc[c] = y
        t_sc[c] = t
        gy_sc[c] = gy
        part = _dot(gy, wg[c])

        @pl.when(c == 0)
        def _():
            q_sc[...] = part

        @pl.when(c > 0)
        def _():
            q_sc[...] += part

        @pl.when(c == 3)
        def _():
            sig = _sigmoid(q_sc[...] + bg_ref[...])
            sz, dsz = _silu_and_grad(z_ref[...])
            dhv = dh_ref[...]
            for k in range(4):
                cols = slice(k * cw, (k + 1) * cw)
                gy_k, sig_k, sz_k = gy_sc[k], sig[:, cols], sz[:, cols]
                y2 = gy_k * sig_k
                dy3 = _dot_nt(dhv, wo[k])
                dwo[k] += _dot_tn(y2 * sz_k, dhv)
                dy2 = dy3 * sz_k
                dp_ref[0, :, cols] = (dy3 * y2 * dsz[:, cols]).astype(BF16)
                dq_sc[:, cols] = dy2 * gy_k * sig_k * (1.0 - sig_k)
                dgy_sc[k] = dy2 * sig_k
            dq = dq_sc[...]
            dbg[...] += jnp.sum(dq, axis=0, keepdims=True)
            for k in range(4):
                cols = slice(k * cw, (k + 1) * cw)
                dwg[k] += _dot_tn(gy_sc[k], dq)
                dgy = dgy_sc[k] + _dot_nt(dq, wg[k])
                dy_ref[:, cols] = dgy * _gelu_grad(y_sc[k], t_sc[k])

        @pl.when((i == nt - 1) & (c == 3))
        def _():
            pltpu.sync_copy(dwo, dwo_hbm)
            pltpu.sync_copy(dwg, dwg_hbm)
            pltpu.sync_copy(dbg, dbg_hbm)

    row = lambda i, c: (i, 0)
    chunk = lambda i, c: (i, c)
    sd = jax.ShapeDtypeStruct
    acc = pltpu.VMEM((4, tt, cw), F32)
    slab, _ = _slab_spec(lp, tt, sw, lambda i, c: (i * 4 + c, 0))
    return pl.pallas_call(
        body, name=name, grid=(nt, 4),
        in_specs=[pl.BlockSpec((tt, d), row), slab,
                  pl.BlockSpec((tt, cw), chunk), pl.BlockSpec((tt, d), row),
                  pl.BlockSpec((4, 1, cw), lambda i, c: (0, 0, 0)), pl.BlockSpec((1, d), lambda i, c: (0, 0)),
                  ANY, ANY, ANY, ANY],
        out_specs=[pl.BlockSpec((tt, d), row), pl.BlockSpec((1, tt, d), lambda i, c: (1, i, 0)), ANY, ANY, ANY],
        out_shape=(sd((lp, d), F32), sd((2, lp, d), BF16), sd(w_out.shape, F32), sd(w_glu.shape, F32), sd((1, d), F32)),
        scratch_shapes=[pltpu.VMEM(cdre.shape, BF16), pltpu.VMEM(cdim.shape, BF16), pltpu.VMEM(w_glu.shape, BF16),
                        pltpu.VMEM(w_out.shape, BF16), acc, acc, acc, pltpu.VMEM((tt, d), F32), pltpu.VMEM((tt, d), F32), acc,
                        pltpu.VMEM(w_out.shape, F32), pltpu.VMEM(w_glu.shape, F32), pltpu.VMEM((1, d), F32)],
        compiler_params=_params(ndim=2),
    )(dh, s, u, z, d_skip, b_glu, cdre, cdim, w_glu, w_out)


def s5_bwd3b(dy, s, u, cdre, cdim, d_skip, name):
    lp, d = dy.shape
    tt = TOKEN_TILE
    nt = lp // tt
    sw, cw = cdre.shape[1], cdre.shape[2]

    def body(dy_ref, s_ref, u_ref, d_ref, cre_hbm, cim_hbm,
             ds_ref, dus_ref, dcre_hbm, dcim_hbm, dd_hbm, cre, cim, dcre, dcim, dd):
        i, c = pl.program_id(0), pl.program_id(1)

        @pl.when((i == 0) & (c == 0))
        def _():
            pltpu.sync_copy(cre_hbm, cre)
            pltpu.sync_copy(cim_hbm, cim)
            dcre[...] = jnp.zeros_like(dcre)
            dcim[...] = jnp.zeros_like(dcim)
            dd[...] = jnp.zeros_like(dd)

        dyv = dy_ref[...]
        dd[c] += jnp.sum(dyv * u_ref[...], axis=0, keepdims=True)
        dus_ref[...] = dyv * d_ref[c]
        _slab_store(ds_ref, _dot_nt(dyv, cre[c]), _dot_nt(dyv, cim[c]))
        s_r, s_i = _slab_load(s_ref)
        dcre[c] += _dot_tn(s_r, dyv)
        dcim[c] += _dot_tn(s_i, dyv)

        @pl.when((i == nt - 1) & (c == 3))
        def _():
            pltpu.sync_copy(dcre, dcre_hbm)
            pltpu.sync_copy(dcim, dcim_hbm)
            pltpu.sync_copy(dd, dd_hbm)

    chunk = lambda i, c: (i, c)
    sd = jax.ShapeDtypeStruct
    slab, slab_shape = _slab_spec(lp, tt, sw, lambda i, c: (i * 4 + c, 0))
    return pl.pallas_call(
        body, name=name, grid=(nt, 4),
        in_specs=[pl.BlockSpec((tt, cw), chunk), slab,
                  pl.BlockSpec((tt, cw), chunk), pl.BlockSpec((4, 1, cw), lambda i, c: (0, 0, 0)), ANY, ANY],
        out_specs=[slab, pl.BlockSpec((tt, cw), chunk), ANY, ANY, ANY],
        out_shape=(sd(slab_shape, F32), sd((lp, d), F32),
                   sd(cdre.shape, F32), sd(cdim.shape, F32), sd((4, 1, cw), F32)),
        scratch_shapes=[pltpu.VMEM(cdre.shape, BF16), pltpu.VMEM(cdim.shape, BF16),
                        pltpu.VMEM(cdre.shape, F32), pltpu.VMEM(cdim.shape, F32), pltpu.VMEM((4, 1, cw), F32)],
        compiler_params=_params(ndim=2),
    )(dy, s, u, d_skip, cdre, cdim)


def s5_scan_bwd(g, s, ar, ai, name):
    nj = ar.shape[1]
    tt = TOKEN_TILE
    cpb = SCAN_CHUNKS
    nt = g.shape[0] // (4 * tt * nj)

    def body(g_ref, s_ref, ar_ref, ai_ref, lam_ref, dar_ref, dai_ref, st_r, st_i, acc_r, acc_i):
        i, cg = pl.program_id(0), pl.program_id(1)

        @pl.when((i == 0) & (cg == 0))
        def _():
            for ref in (st_r, st_i, acc_r, acc_i):
                ref[...] = jnp.zeros_like(ref)

        a_r = [ar_ref[cg * cpb + q] for q in range(cpb)]
        a_i = [ai_ref[cg * cpb + q] for q in range(cpb)]

        def slab(q, t):
            return pl.ds(pl.multiple_of((q * tt + t) * nj, nj), nj)

        def adjoint(q, t, l_r, l_i):
            rows = slab(q, t)
            g_r, g_i = _unpack_pair(g_ref[rows, :])
            n_r = g_r + a_r[q] * l_r + a_i[q] * l_i
            n_i = g_i + a_r[q] * l_i - a_i[q] * l_r
            lam_ref[rows, :] = _pack_pair(n_r, n_i)
            return n_r, n_i

        def pair(q, t, l_r, l_i, d_r, d_i):
            p_r, p_i = _unpack_pair(s_ref[slab(q, t), :])
            return d_r + l_r * p_r + l_i * p_i, d_i + l_i * p_r - l_r * p_i

        def step(k, carry):
            t = tt - 1 - k
            out = []
            for q in range(cpb):
                l_r, l_i, d_r, d_i = carry[q]
                l_r, l_i = adjoint(q, t, l_r, l_i)
                d_r, d_i = pair(q, t - 1, l_r, l_i, d_r, d_i)
                out.append((l_r, l_i, d_r, d_i))
            return tuple(out)

        init = []
        for q in range(cpb):
            ch = cg * cpb + q
            l_r, l_i = st_r[ch], st_i[ch]
            d_r, d_i = pair(q, tt - 1, l_r, l_i, acc_r[ch], acc_i[ch])
            init.append((l_r, l_i, d_r, d_i))
        final = lax.fori_loop(0, tt - 1, step, tuple(init), unroll=8)
        for q in range(cpb):
            ch = cg * cpb + q
            l_r, l_i, d_r, d_i = final[q]
            l_r, l_i = adjoint(q, 0, l_r, l_i)
            st_r[ch] = l_r
            st_i[ch] = l_i
            acc_r[ch] = d_r
            acc_i[ch] = d_i
            dar_ref[ch] = d_r
            dai_ref[ch] = d_i

    blk = pl.BlockSpec((cpb * tt * nj, 128), lambda i, cg: ((nt - 1 - i) * (4 // cpb) + cg, 0))
    par = pl.BlockSpec((4, nj, 128), lambda i, cg: (0, 0, 0))
    sd = jax.ShapeDtypeStruct
    return pl.pallas_call(
        body, name=name, grid=(nt, 4 // cpb),
        in_specs=[blk, blk, par, par], out_specs=[blk, par, par],
        out_shape=(sd(g.shape, F32), sd((4, nj, 128), F32), sd((4, nj, 128), F32)),
        scratch_shapes=[pltpu.VMEM((4, nj, 128), F32)] * 4,
        compiler_params=_params(ndim=2),
    )(g, s, ar, ai)


def s5_bwd1(lam, dus, u, dp, h, dh, g, w_in, bdre, bdim, name):
    lp, d = h.shape
    tt = TOKEN_TILE
    nt = lp // tt
    cw, sw = bdre.shape[1], bdre.shape[2]

    def body(lam_ref, dus_ref, u_ref, dpz_ref, h_ref, dh_ref, g_ref, w_hbm, bre_hbm, bim_hbm,
             dpu_ref, dho_ref, n_ref, dbre_hbm, dbim_hbm, dg_hbm, w, bre, bim, dn_sc, dbre, dbim, dg):
        i, c = pl.program_id(0), pl.program_id(1)

        @pl.when((i == 0) & (c == 0))
        def _():
            pltpu.sync_copy(w_hbm, w)
            pltpu.sync_copy(bre_hbm, bre)
            pltpu.sync_copy(bim_hbm, bim)
            dbre[...] = jnp.zeros_like(dbre)
            dbim[...] = jnp.zeros_like(dbim)
            dg[...] = jnp.zeros_like(dg)

        (l_r, l_i), uv = _slab_load(lam_ref), u_ref[...]
        du = dus_ref[...] + _dot_nt(l_r, bre[c]) + _dot_nt(l_i, bim[c])
        dbre[c] += _dot_tn(uv, l_r)
        dbim[c] += _dot_tn(uv, l_i)
        dpu_ref[0] = du.astype(BF16)
        part = _dot_nt(du, w[c])

        @pl.when(c == 0)
        def _():
            dn_sc[...] = part

        @pl.when(c > 0)
        def _():
            dn_sc[...] += part

        @pl.when(c == 3)
        def _():
            dz = dpz_ref[0]
            dn = dn_sc[...]
            for k in range(4):
                dn = dn + _dot_nt(dz[:, k * cw:(k + 1) * cw], w[4 + k])
            gv = g_ref[...]
            n, hh, rr = _rms_fwd(h_ref[...], gv)
            n_ref[...] = n.T.astype(BF16)
            dg[...] += jnp.sum(dn * hh, axis=0, keepdims=True)
            dho_ref[...] = dh_ref[...] + _rms_bwd(dn, hh, rr, gv)

        @pl.when((i == nt - 1) & (c == 3))
        def _():
            pltpu.sync_copy(dbre, dbre_hbm)
            pltpu.sync_copy(dbim, dbim_hbm)
            pltpu.sync_copy(dg, dg_hbm)

    row = lambda i, c: (i, 0)
    chunk = lambda i, c: (i, c)
    sd = jax.ShapeDtypeStruct
    slab, _ = _slab_spec(lp, tt, sw, lambda i, c: (i * 4 + c, 0))
    return pl.pallas_call(
        body, name=name, grid=(nt, 4),
        in_specs=[slab, pl.BlockSpec((tt, cw), chunk),
                  pl.BlockSpec((tt, cw), chunk), pl.BlockSpec((1, tt, d), lambda i, c: (1, i, 0)),
                  pl.BlockSpec((tt, d), row), pl.BlockSpec((tt, d), row), pl.BlockSpec((1, d), lambda i, c: (0, 0)),
                  ANY, ANY, ANY],
        out_specs=[pl.BlockSpec((1, tt, cw), lambda i, c: (0, i, c)), pl.BlockSpec((tt, d), row),
                   pl.BlockSpec((d, tt), lambda i, c: (0, i)), ANY, ANY, ANY],
        out_shape=(sd(dp.shape, BF16), sd((lp, d), F32), sd((d, lp), BF16),
                   sd(bdre.shape, F32), sd(bdim.shape, F32), sd((1, d), F32)),
        input_output_aliases={3: 0},
        scratch_shapes=[pltpu.VMEM(w_in.shape, BF16), pltpu.VMEM(bdre.shape, BF16), pltpu.VMEM(bdim.shape, BF16),
                        pltpu.VMEM((tt, d), F32), pltpu.VMEM(bdre.shape, F32), pltpu.VMEM(bdim.shape, F32), pltpu.VMEM((1, d), F32)],
        compiler_params=_params(ndim=2),
    )(lam, dus, u, dp, h, dh, g, w_in, bdre, bdim)


def grad_w_in(n_t, dp, blk, name):
    d, lp = n_t.shape
    npart, _, width = dp.shape
    per = width // blk

    def body(n_ref, dp_ref, o_ref):
        o_ref[0] = jnp.dot(n_ref[...], dp_ref[0], preferred_element_type=F32)

    return pl.pallas_call(
        body, name=name, grid=(npart * per,),
        in_specs=[pl.BlockSpec((d, lp), lambda j: (0, 0), pipeline_mode=pl.Buffered(1)),
                  pl.BlockSpec((1, lp, blk), lambda j: (j // per, 0, j % per))],
        out_specs=pl.BlockSpec((1, d, blk), lambda j: (j, 0, 0)),
        out_shape=jax.ShapeDtypeStruct((npart * per, d, blk), F32),
        compiler_params=_params(),
    )(n_t, dp)


def _conv_fwd_chunk(n, w, cw_ref, cb_ref, halo, c, nch):
    bg = jnp.dot(n, w[c], preferred_element_type=F32)
    cg = jnp.dot(n, w[nch + c], preferred_element_type=F32)
    v = jnp.dot(n, w[2 * nch + c], preferred_element_type=F32)
    z = jnp.dot(n, w[3 * nch + c], preferred_element_type=F32)
    hc = cg * v
    taps = cw_ref[c]
    conv = taps[2:3, :] * hc + taps[1:2, :] * _shift_down(hc, 1, halo) + taps[0:1, :] * _shift_down(hc, 2, halo) + cb_ref[c]
    return bg, cg, v, z, hc, conv


def conv_fwd(h, g, w_in, conv_w, conv_b, w_out, name):
    lp, d = h.shape
    tt = TOKEN_TILE
    nt = lp // tt
    nch, ce = w_out.shape[0], w_out.shape[1]

    def body(h_ref, g_ref, cw_ref, cb_ref, w_hbm, wo_hbm, o_ref, halo_ref, w, wo, halo):
        i = pl.program_id(0)

        @pl.when(i == 0)
        def _():
            pltpu.sync_copy(w_hbm, w)
            pltpu.sync_copy(wo_hbm, wo)
            halo[...] = jnp.zeros_like(halo)

        hv = h_ref[...]
        n = _rms_fwd(hv, g_ref[...])[0].astype(BF16)
        o = hv
        for c in range(nch):
            bg, _, _, z, hc, conv = _conv_fwd_chunk(n, w, cw_ref, cb_ref, halo[c], c, nch)
            o = o + _dot(bg * conv * (z * _sigmoid(z)), wo[c])
            halo[c] = hc[tt - CONV_HALO:, :]
            halo_ref[0, c] = hc[tt - CONV_HALO:, :]
        o_ref[...] = o

    sd = jax.ShapeDtypeStruct
    return pl.pallas_call(
        body, name=name, grid=(nt,),
        in_specs=[pl.BlockSpec((tt, d), lambda i: (i, 0)), pl.BlockSpec((1, d), lambda i: (0, 0)),
                  pl.BlockSpec(conv_w.shape, lambda i: (0, 0, 0)), pl.BlockSpec(conv_b.shape, lambda i: (0, 0, 0)), ANY, ANY],
        out_specs=[pl.BlockSpec((tt, d), lambda i: (i, 0)), pl.BlockSpec((1, nch, CONV_HALO, ce), lambda i: (i, 0, 0, 0))],
        out_shape=(sd((lp, d), F32), sd((nt, nch, CONV_HALO, ce), F32)),
        scratch_shapes=[pltpu.VMEM(w_in.shape, BF16), pltpu.VMEM(w_out.shape, BF16), pltpu.VMEM((nch, CONV_HALO, ce), F32)],
        compiler_params=_params(),
    )(h, g, conv_w, conv_b, w_in, w_out)


def conv_bwd(h, dh, halos, g, w_in, conv_w, conv_b, w_out, name):
    lp, d = h.shape
    tt = TOKEN_TILE
    nt = lp // tt
    nch, ce = w_out.shape[0], w_out.shape[1]

    def body(h_ref, dh_ref, halo_ref, g_ref, cw_ref, cb_ref, w_hbm, wo_hbm,
             dho_ref, n_ref, dp_ref, dwo_hbm, dcw_hbm, dcb_hbm, dg_hbm, w, wo, nxt, dwo, dcw, dcb, dg):
        i = pl.program_id(0)

        @pl.when(i == 0)
        def _():
            pltpu.sync_copy(w_hbm, w)
            pltpu.sync_copy(wo_hbm, wo)
            for ref in (nxt, dwo, dcw, dcb, dg):
                ref[...] = jnp.zeros_like(ref)

        gv = g_ref[...]
        nf, hh, rr = _rms_fwd(h_ref[...], gv)
        n = nf.astype(BF16)
        n_ref[...] = nf.T.astype(BF16)
        dhv = dh_ref[...]
        has_prev = (i < nt - 1).astype(F32)
        dn = jnp.zeros((tt, d), F32)
        for c in range(nch):
            halo = halo_ref[0, c] * has_prev
            bg, cg, v, z, hc, conv = _conv_fwd_chunk(n, w, cw_ref, cb_ref, halo, c, nch)
            sz, dsz = _silu_and_grad(z)
            y1 = bg * conv
            dy2 = _dot_nt(dhv, wo[c])
            dwo[c] += _dot_tn(y1 * sz, dhv)
            dy1 = dy2 * sz
            dz = dy2 * y1 * dsz
            dbg = dy1 * conv
            dconv = dy1 * bg
            dcb[c] += jnp.sum(dconv, axis=0, keepdims=True)
            up1 = _shift_up(dconv, 1, nxt[c])
            up2 = _shift_up(dconv, 2, nxt[c])
            nxt[c] = dconv[:CONV_HALO, :]
            taps = cw_ref[c]
            dhc = taps[2:3, :] * dconv + taps[1:2, :] * up1 + taps[0:1, :] * up2
            dcw[c, 0:1, :] += jnp.sum(hc * up2, axis=0, keepdims=True)
            dcw[c, 1:2, :] += jnp.sum(hc * up1, axis=0, keepdims=True)
            dcw[c, 2:3, :] += jnp.sum(hc * dconv, axis=0, keepdims=True)
            dcg = dhc * v
            dv = dhc * cg
            cols = slice(c * ce, (c + 1) * ce)
            for p, val in enumerate((dbg, dcg, dv, dz)):
                dp_ref[p, :, cols] = val.astype(BF16)
                dn = dn + _dot_nt(val, w[p * nch + c])
        dg[...] += jnp.sum(dn * hh, axis=0, keepdims=True)
        dho_ref[...] = dhv + _rms_bwd(dn, hh, rr, gv)

        @pl.when(i == nt - 1)
        def _():
            pltpu.sync_copy(dwo, dwo_hbm)
            pltpu.sync_copy(dcw, dcw_hbm)
            pltpu.sync_copy(dcb, dcb_hbm)
            pltpu.sync_copy(dg, dg_hbm)

    rev = lambda i: (nt - 1 - i, 0)
    sd = jax.ShapeDtypeStruct
    return pl.pallas_call(
        body, name=name, grid=(nt,),
        in_specs=[pl.BlockSpec((tt, d), rev), pl.BlockSpec((tt, d), rev),
                  pl.BlockSpec((1, nch, CONV_HALO, ce), lambda i: (jnp.maximum(nt - 2 - i, 0), 0, 0, 0)),
                  pl.BlockSpec((1, d), lambda i: (0, 0)),
                  pl.BlockSpec(conv_w.shape, lambda i: (0, 0, 0)), pl.BlockSpec(conv_b.shape, lambda i: (0, 0, 0)), ANY, ANY],
        out_specs=[pl.BlockSpec((tt, d), rev), pl.BlockSpec((d, tt), lambda i: (0, nt - 1 - i)),
                   pl.BlockSpec((4, tt, nch * ce), lambda i: (0, nt - 1 - i, 0)), ANY, ANY, ANY, ANY],
        out_shape=(sd((lp, d), F32), sd((d, lp), BF16), sd((4, lp, nch * ce), BF16),
                   sd(w_out.shape, F32), sd((nch, 8, ce), F32), sd((nch, 1, ce), F32), sd((1, d), F32)),
        scratch_shapes=[pltpu.VMEM(w_in.shape, BF16), pltpu.VMEM(w_out.shape, BF16), pltpu.VMEM((nch, CONV_HALO, ce), F32),
                        pltpu.VMEM(w_out.shape, F32), pltpu.VMEM((nch, 8, ce), F32), pltpu.VMEM((nch, 1, ce), F32),
                        pltpu.VMEM((1, d), F32)],
        compiler_params=_params(),
    )(h, dh, halos, g, conv_w, conv_b, w_in, w_out)


def _pool_fwd_group(n, w, wg, bg_ref, sc_ref, halo, k, tile, tt, first_pos):
    u = jnp.dot(n, w[k], preferred_element_type=F32)
    z = jnp.dot(n, w[4 + k], preferred_element_type=F32)
    ext = jnp.concatenate([halo, u], axis=0)
    win = _window_sums_back(ext)[k][POOL_HALO:, :]
    mixed = win * _pool_inv_count(tile, tt, first_pos, POOL_WINDOWS[k], u.shape[1]) - u
    outs = _dot(mixed, wg[k]) + bg_ref[k]
    return u, z, mixed, outs, outs * sc_ref[k]


def pool_fwd(h, g, w_in, w_grp, b_grp, scale, w_out, first_pos, name):
    lp, d = h.shape
    tt = TOKEN_TILE
    nt = lp // tt
    gw = w_grp.shape[1]

    def body(h_ref, g_ref, bg_ref, sc_ref, w_hbm, wg_hbm, wo_hbm, o_ref, halo_ref, w, wg, wo, halo):
        i = pl.program_id(0)

        @pl.when(i == 0)
        def _():
            pltpu.sync_copy(w_hbm, w)
            pltpu.sync_copy(wg_hbm, wg)
            pltpu.sync_copy(wo_hbm, wo)
            halo[...] = jnp.zeros_like(halo)

        hv = h_ref[...]
        n = _rms_fwd(hv, g_ref[...])[0].astype(BF16)
        o = hv
        for k in range(4):
            u, z, _, _, yp = _pool_fwd_group(n, w, wg, bg_ref, sc_ref, halo[k], k, i, tt, first_pos)
            o = o + _dot(yp * (z * _sigmoid(z)), wo[k])
            halo[k] = u[tt - POOL_HALO:, :]
            halo_ref[0, k] = u[tt - POOL_HALO:, :]
        o_ref[...] = o

    sd = jax.ShapeDtypeStruct
    small = pl.BlockSpec((4, 1, gw), lambda i: (0, 0, 0))
    return pl.pallas_call(
        body, name=name, grid=(nt,),
        in_specs=[pl.BlockSpec((tt, d), lambda i: (i, 0)), pl.BlockSpec((1, d), lambda i: (0, 0)), small, small, ANY, ANY, ANY],
        out_specs=[pl.BlockSpec((tt, d), lambda i: (i, 0)), pl.BlockSpec((1, 4, POOL_HALO, gw), lambda i: (i, 0, 0, 0))],
        out_shape=(sd((lp, d), F32), sd((nt, 4, POOL_HALO, gw), F32)),
        scratch_shapes=[pltpu.VMEM(w_in.shape, BF16), pltpu.VMEM(w_grp.shape, BF16), pltpu.VMEM(w_out.shape, BF16),
                        pltpu.VMEM((4, POOL_HALO, gw), F32)],
        compiler_params=_params(),
    )(h, g, b_grp, scale, w_in, w_grp, w_out)


def pool_bwd(h, dh, halos, g, w_in, w_grp, b_grp, scale, w_out, first_pos, name):
    lp, d = h.shape
    tt = TOKEN_TILE
    nt = lp // tt
    gw = w_grp.shape[1]

    def body(h_ref, dh_ref, halo_ref, g_ref, bg_ref, sc_ref, w_hbm, wg_hbm, wo_hbm,
             dho_ref, n_ref, dp_ref, dwo_hbm, dwg_hbm, dbg_hbm, dsc_hbm, dg_hbm,
             w, wg, wo, nxt, dwo, dwg, dbg, dsc, dg):
        i = pl.program_id(0)
        tile = nt - 1 - i

        @pl.when(i == 0)
        def _():
            pltpu.sync_copy(w_hbm, w)
            pltpu.sync_copy(wg_hbm, wg)
            pltpu.sync_copy(wo_hbm, wo)
            for ref in (nxt, dwo, dwg, dbg, dsc, dg):
                ref[...] = jnp.zeros_like(ref)

        gv = g_ref[...]
        nf, hh, rr = _rms_fwd(h_ref[...], gv)
        n = nf.astype(BF16)
        n_ref[...] = nf.T.astype(BF16)
        dhv = dh_ref[...]
        has_prev = (i < nt - 1).astype(F32)
        dn = jnp.zeros((tt, d), F32)
        for k in range(4):
            u, z, mixed, outs, yp = _pool_fwd_group(n, w, wg, bg_ref, sc_ref, halo_ref[0, k] * has_prev, k, tile, tt, first_pos)
            sz, dsz = _silu_and_grad(z)
            dy = _dot_nt(dhv, wo[k])
            dwo[k] += _dot_tn(yp * sz, dhv)
            dyp = dy * sz
            dz = dy * yp * dsz
            dsc[k] += jnp.sum(dyp * outs, axis=0, keepdims=True)
            douts = dyp * sc_ref[k]
            dbg[k] += jnp.sum(douts, axis=0, keepdims=True)
            dwg[k] += _dot_tn(mixed, douts)
            dmixed = _dot_nt(douts, wg[k])
            dm = dmixed * _pool_inv_count(tile, tt, first_pos, POOL_WINDOWS[k], gw)
            ext = jnp.concatenate([dm, nxt[k]], axis=0)
            du = _window_sums_fwd(ext)[k][:tt, :] - dmixed
            nxt[k] = dm[:POOL_HALO, :]
            cols = slice(k * gw, (k + 1) * gw)
            dp_ref[0, :, cols] = du.astype(BF16)
            dp_ref[1, :, cols] = dz.astype(BF16)
            dn = dn + _dot_nt(du, w[k]) + _dot_nt(dz, w[4 + k])
        dg[...] += jnp.sum(dn * hh, axis=0, keepdims=True)
        dho_ref[...] = dhv + _rms_bwd(dn, hh, rr, gv)

        @pl.when(i == nt - 1)
        def _():
            pltpu.sync_copy(dwo, dwo_hbm)
            pltpu.sync_copy(dwg, dwg_hbm)
            pltpu.sync_copy(dbg, dbg_hbm)
            pltpu.sync_copy(dsc, dsc_hbm)
            pltpu.sync_copy(dg, dg_hbm)

    rev = lambda i: (nt - 1 - i, 0)
    sd = jax.ShapeDtypeStruct
    small = pl.BlockSpec((4, 1, gw), lambda i: (0, 0, 0))
    return pl.pallas_call(
        body, name=name, grid=(nt,),
        in_specs=[pl.BlockSpec((tt, d), rev), pl.BlockSpec((tt, d), rev),
                  pl.BlockSpec((1, 4, POOL_HALO, gw), lambda i: (jnp.maximum(nt - 2 - i, 0), 0, 0, 0)),
                  pl.BlockSpec((1, d), lambda i: (0, 0)), small, small, ANY, ANY, ANY],
        out_specs=[pl.BlockSpec((tt, d), rev), pl.BlockSpec((d, tt), lambda i: (0, nt - 1 - i)),
                   pl.BlockSpec((2, tt, 4 * gw), lambda i: (0, nt - 1 - i, 0)), ANY, ANY, ANY, ANY, ANY],
        out_shape=(sd((lp, d), F32), sd((d, lp), BF16), sd((2, lp, 4 * gw), BF16),
                   sd(w_out.shape, F32), sd(w_grp.shape, F32), sd((4, 1, gw), F32), sd((4, 1, gw), F32), sd((1, d), F32)),
        scratch_shapes=[pltpu.VMEM(w_in.shape, BF16), pltpu.VMEM(w_grp.shape, BF16), pltpu.VMEM(w_out.shape, BF16),
                        pltpu.VMEM((4, POOL_HALO, gw), F32), pltpu.VMEM(w_out.shape, F32), pltpu.VMEM(w_grp.shape, F32),
                        pltpu.VMEM((4, 1, gw), F32), pltpu.VMEM((4, 1, gw), F32), pltpu.VMEM((1, d), F32)],
        compiler_params=_params(),
    )(h, dh, halos, g, b_grp, scale, w_in, w_grp, w_out)


def loss_head(h, target, g, pad_tiles, name):
    lp, d = h.shape
    tt = TOKEN_TILE
    nt = lp // tt

    def body(h_ref, t_ref, g_ref, dh_ref, dg_ref, loss_ref, acc):
        i = pl.program_id(0)

        @pl.when(i == 0)
        def _():
            acc[...] = jnp.zeros_like(acc)
            dg_ref[...] = jnp.zeros_like(dg_ref)

        @pl.when(i < pad_tiles)
        def _():
            dh_ref[...] = jnp.zeros_like(dh_ref)

        @pl.when(i >= pad_tiles)
        def _():
            gv = g_ref[...]
            n, hh, rr = _rms_fwd(h_ref[...], gv)
            err = n - t_ref[...]
            acc[...] += 0.5 * jnp.sum(jnp.mean(err * err, axis=-1, keepdims=True), axis=0, keepdims=True)
            dn = err * (1.0 / d)
            dg_ref[...] += jnp.sum(dn * hh, axis=0, keepdims=True)
            dh_ref[...] = _rms_bwd(dn, hh, rr, gv)

        loss_ref[...] = jnp.broadcast_to(acc[...], loss_ref.shape)

    sd = jax.ShapeDtypeStruct
    return pl.pallas_call(
        body, name=name, grid=(nt,),
        in_specs=[pl.BlockSpec((tt, d), lambda i: (i, 0)), pl.BlockSpec((tt, d), lambda i: (jnp.maximum(i - pad_tiles, 0), 0)),
                  pl.BlockSpec((1, d), lambda i: (0, 0))],
        out_specs=[pl.BlockSpec((tt, d), lambda i: (i, 0)), pl.BlockSpec((1, d), lambda i: (0, 0)),
                   pl.BlockSpec((8, 128), lambda i: (0, 0))],
        out_shape=(sd((lp, d), F32), sd((1, d), F32), sd((8, 128), F32)),
        scratch_shapes=[pltpu.VMEM((1, 1), F32)],
        compiler_params=_params(),
    )(h, target, g)


def exchange(arrs, gather, name):
    n = len(arrs)

    def body(*refs):
        ins, outs = refs[:n], refs[n:2 * n]
        send_sems, recv_sems, own_sems = refs[2 * n:]
        x, y, c = lax.axis_index("x"), lax.axis_index("y"), lax.axis_index("c")
        me = 4 * x + 2 * y + c
        own = []
        for a in range(n):
            cp = pltpu.make_async_copy(ins[a] if gather else ins[a].at[me], outs[a].at[me], own_sems.at[a])
            cp.start()
            own.append(cp)
        sent = []
        for k in range(1, N_DEV):
            px = 1 - x if k & 4 else x
            py = 1 - y if k & 2 else y
            pc = 1 - c if k & 1 else c
            peer = 4 * px + 2 * py + pc
            for a in range(n):
                cp = pltpu.make_async_remote_copy(
                    src_ref=ins[a] if gather else ins[a].at[peer], dst_ref=outs[a].at[me],
                    send_sem=send_sems.at[a, k - 1], recv_sem=recv_sems.at[a, k - 1],
                    device_id=(px, py, pc), device_id_type=pl.DeviceIdType.MESH)
                cp.start()
                sent.append((cp, a, k, peer, (px, py, pc)))
        for cp, a, k, peer, pid in sent:
            cp.wait_send()
            pltpu.make_async_remote_copy(
                src_ref=ins[a] if gather else ins[a].at[peer], dst_ref=outs[a].at[peer],
                send_sem=send_sems.at[a, k - 1], recv_sem=recv_sems.at[a, k - 1],
                device_id=pid, device_id_type=pl.DeviceIdType.MESH).wait_recv()
        for cp in own:
            cp.wait()

    hbm = pl.BlockSpec(memory_space=pltpu.HBM)
    out_shape = tuple(jax.ShapeDtypeStruct(((N_DEV,) + a.shape) if gather else a.shape, a.dtype) for a in arrs)
    return pl.pallas_call(
        body, name=name, in_specs=[hbm] * n, out_specs=[hbm] * n, out_shape=out_shape,
        scratch_shapes=[pltpu.SemaphoreType.DMA((n, N_DEV - 1)), pltpu.SemaphoreType.DMA((n, N_DEV - 1)),
                        pltpu.SemaphoreType.DMA((n,))],
    )(*[pltpu.with_memory_space_constraint(a, pltpu.HBM) for a in arrs])


def _peers(x, y, c):
    out = []
    for k in range(1, N_DEV):
        px = 1 - x if k & 4 else x
        py = 1 - y if k & 2 else y
        pc = 1 - c if k & 1 else c
        out.append((k, (px, py, pc), 4 * px + 2 * py + pc))
    return out


def exchange_start(arrs, gather, after, name):
    n = len(arrs)
    me = 4 * lax.axis_index("x") + 2 * lax.axis_index("y") + lax.axis_index("c")
    lands = []
    for a in arrs:
        own = a[None] if gather else lax.dynamic_index_in_dim(a, me, 0, keepdims=True)
        lands.append(lax.dynamic_update_index_in_dim(lax.empty(((N_DEV,) + a.shape) if gather else a.shape, a.dtype), own, me, 0))

    def body(*refs):
        ins, land = refs[:n], refs[n:2 * n]
        send_sems, recv_sems, token = refs[2 * n + 1], refs[2 * n + 2], refs[4 * n + 3]
        x, y, c = lax.axis_index("x"), lax.axis_index("y"), lax.axis_index("c")
        me = 4 * x + 2 * y + c
        for k, pid, peer in _peers(x, y, c):
            for a in range(n):
                pltpu.make_async_remote_copy(
                    src_ref=ins[a] if gather else ins[a].at[peer], dst_ref=land[a].at[me],
                    send_sem=send_sems.at[a * (N_DEV - 1) + k - 1], recv_sem=recv_sems.at[a * (N_DEV - 1) + k - 1],
                    device_id=pid, device_id_type=pl.DeviceIdType.MESH).start()
        token[...] = jnp.zeros_like(token)

    hbm = pl.BlockSpec(memory_space=pltpu.HBM)
    sem = pl.BlockSpec(memory_space=pltpu.SEMAPHORE)
    sems = pltpu.SemaphoreType.DMA((n * (N_DEV - 1),))
    res = pl.pallas_call(
        body, name=name, in_specs=[hbm] * (2 * n) + [ANY],
        out_specs=[sem, sem] + [hbm] * (2 * n) + [pl.BlockSpec(memory_space=pltpu.VMEM)],
        out_shape=[sems, sems] + [pltpu.HBM(a.shape, a.dtype) for a in arrs] + [pltpu.HBM(l.shape, l.dtype) for l in lands]
        + [jax.ShapeDtypeStruct((8, 128), F32)],
        input_output_aliases={a: 2 + a for a in range(2 * n)},
        compiler_params=pltpu.CompilerParams(has_side_effects=pltpu.SideEffectType.DATAFLOW_SIDE_EFFECTING),
    )(*[pltpu.with_memory_space_constraint(a, pltpu.HBM) for a in list(arrs) + lands], after)
    return res[0], res[1], res[2:2 + n], res[2 + n:2 + 2 * n], res[-1]


def exchange_wait(started, gather, after, name):
    send_sems, recv_sems, srcs, lands, _ = started
    n = len(srcs)

    def body(*refs):
        ins, land = refs[:n], refs[n:2 * n]
        send_sems, recv_sems = refs[2 * n], refs[2 * n + 1]
        x, y, c = lax.axis_index("x"), lax.axis_index("y"), lax.axis_index("c")
        for k, pid, peer in _peers(x, y, c):
            for a in range(n):
                cp = pltpu.make_async_remote_copy(
                    src_ref=ins[a] if gather else ins[a].at[peer], dst_ref=land[a].at[peer],
                    send_sem=send_sems.at[a * (N_DEV - 1) + k - 1], recv_sem=recv_sems.at[a * (N_DEV - 1) + k - 1],
                    device_id=pid, device_id_type=pl.DeviceIdType.MESH)
                cp.wait_send()
                cp.wait_recv()

    hbm = pl.BlockSpec(memory_space=pltpu.HBM)
    sem = pl.BlockSpec(memory_space=pltpu.SEMAPHORE)
    res = pl.pallas_call(
        body, name=name, in_specs=[hbm] * (2 * n) + [sem, sem, ANY],
        out_specs=[hbm] * (2 * n),
        out_shape=[pltpu.HBM(a.shape, a.dtype) for a in list(srcs) + list(lands)],
        input_output_aliases={a: a for a in range(2 * n)},
        compiler_params=pltpu.CompilerParams(has_side_effects=pltpu.SideEffectType.DATAFLOW_SIDE_EFFECTING),
    )(*srcs, *lands, send_sems, recv_sems, after)
    return res[n:]


def _adamw(w, g, m, v):
    m = ADAM_B1 * m + (1.0 - ADAM_B1) * g
    v = ADAM_B2 * v + (1.0 - ADAM_B2) * (g * g)
    m_hat = m / (1.0 - ADAM_B1 ** ADAM_STEP)
    v_hat = v / (1.0 - ADAM_B2 ** ADAM_STEP)
    return -ADAM_LR * (m_hat / (jnp.sqrt(v_hat) + ADAM_EPS) + ADAM_WD * w), m, v


def _update_tile_rows(rows, cols):
    if rows * cols <= UPDATE_TILE_ELEMS:
        return rows
    return max(t for t in range(8, UPDATE_TILE_ELEMS // cols + 1, 8) if rows % t == 0)


def _sum_in_order(p_ref):
    g = p_ref[0]
    for j in range(1, p_ref.shape[0]):
        g = g + p_ref[j]
    return g


def sum_parts(parts, name):
    nparts, rows, cols = parts.shape
    tr = _update_tile_rows(rows, cols)

    def body(p_ref, g_ref):
        g_ref[...] = _sum_in_order(p_ref)

    return pl.pallas_call(
        body, name=name, grid=(rows // tr,),
        in_specs=[pl.BlockSpec((nparts, tr, cols), lambda i: (0, i, 0))],
        out_specs=pl.BlockSpec((tr, cols), lambda i: (i, 0)), out_shape=jax.ShapeDtypeStruct((rows, cols), F32),
        compiler_params=_params(),
    )(parts)


def sum_adamw(parts, w, m, v, name):
    rows, cols = w.shape
    nparts = parts.shape[0]
    tr = _update_tile_rows(rows, cols)

    def body(p_ref, w_ref, m_ref, v_ref, g_ref, d_ref, nm_ref, nv_ref):
        g = _sum_in_order(p_ref)
        delta, nm, nv = _adamw(w_ref[...], g, m_ref[...], v_ref[...])
        g_ref[...] = g
        d_ref[...] = delta
        nm_ref[...] = nm
        nv_ref[...] = nv

    blk = pl.BlockSpec((tr, cols), lambda i: (i, 0))
    sd = jax.ShapeDtypeStruct((rows, cols), F32)
    return pl.pallas_call(
        body, name=name, grid=(rows // tr,),
        in_specs=[pl.BlockSpec((nparts, tr, cols), lambda i: (0, i, 0)), blk, blk, blk],
        out_specs=[blk] * 4, out_shape=(sd,) * 4,
        compiler_params=_params(),
    )(parts, w, m, v)


S5_NAMES = ("w_in", "lam_re", "lam_im", "log_dt", "b_re", "b_im", "c_re", "c_im", "d_skip", "w_glu", "b_glu", "w_out")
CONV_NAMES = ("w_in", "conv_w", "conv_b", "w_out")
POOL_NAMES = ("w_in", "w_grp", "b_grp", "scale", "w_out")
LAYER_KINDS = ("s5", "conv", "pool", "s5")
LAYER_NAMES = {"s5": S5_NAMES, "conv": CONV_NAMES, "pool": POOL_NAMES}
SHARDED = {"s5": ("w_in", "w_glu", "w_out"), "conv": ("w_in", "conv_w", "w_out"), "pool": ("w_in", "w_grp", "b_grp", "w_out")}
GATHER_F32 = ("conv_w", "b_grp")


def weight_names():
    names = ["meta_tokens"]
    for i, kind in enumerate(LAYER_KINDS):
        names.append("norm%d_g" % i)
        names += ["l%d_%s" % (i, n) for n in LAYER_NAMES[kind]]
    names.append("final_g")
    return names


def sharded_names():
    return ["meta_tokens"] + ["l%d_%s" % (i, n) for i, kind in enumerate(LAYER_KINDS) for n in SHARDED[kind]]


def _block_diag_in(bb_t, gc):
    i, g, p = bb_t.shape
    t = bb_t.reshape(i, 4, gc, p)
    return jnp.einsum("icjp,jk->cjikp", t, jnp.eye(gc, dtype=F32)).reshape(4, gc * i, gc * p)


def _block_diag_in_grad(dbd, gc):
    i, p = dbd.shape[1] // gc, dbd.shape[2] // gc
    return jnp.einsum("cjijp->icjp", dbd.reshape(4, gc, i, gc, p)).reshape(i, 4 * gc, p)


def _block_diag_out(cc, gc):
    g, i, p = cc.shape
    return jnp.einsum("cjip,jk->cjpki", cc.reshape(4, gc, i, p), jnp.eye(gc, dtype=F32)).reshape(4, gc * p, gc * i)


def _block_diag_out_grad(dcd, gc):
    p, i = dcd.shape[1] // gc, dcd.shape[2] // gc
    return jnp.einsum("cjpji->cjip", dcd.reshape(4, gc, p, gc, i)).reshape(4 * gc, i, p)


def _to_owner_blocks(a, axis):
    shape = a.shape[:axis] + (N_DEV, a.shape[axis] // N_DEV) + a.shape[axis + 1:]
    return jnp.moveaxis(a.reshape(shape), axis, 0)


def _from_owner_blocks(a, axis):
    a = jnp.moveaxis(a, 0, axis)
    return a.reshape(a.shape[:axis] + (a.shape[axis] * a.shape[axis + 1],) + a.shape[axis + 2:])


def _step(x, target, weights, moments_m, moments_v):
    seq, d = x.shape[1], x.shape[2]
    n_meta = weights["meta_tokens"].shape[0]
    tt = TOKEN_TILE
    pad_tiles = -(-n_meta // tt)
    p0 = pad_tiles * tt
    lp = p0 + seq
    first_pos = p0 - n_meta
    gc = d // 4 // S5_GROUP
    cw = d // 4

    big_names = [n for n in sharded_names() if n != "meta_tokens" and n.split("_", 1)[1] not in GATHER_F32]
    small_names = [n for n in sharded_names() if n not in big_names]
    layer_big = [[n for n in big_names if n.startswith("l%d_" % i)] for i in range(len(LAYER_KINDS))]
    layer_big[0] = small_names + layer_big[0]
    gather_started = []
    after = jnp.zeros((8, 128), F32)
    for i, names in enumerate(layer_big):
        gather_started.append(exchange_start([weights[n] if n in small_names else weights[n].astype(BF16) for n in names], True,
                                             after, "gather_start_l%d" % i))
        after = gather_started[-1][4]

    def vec(name):
        return weights[name].reshape(1, -1)

    s5_prep = {}
    for i, kind in enumerate(LAYER_KINDS):
        if kind == "s5":
            p = "l%d_" % i
            lr, li = weights[p + "lam_re"], weights[p + "lam_im"] + after[0, 0]
            ldt = weights[p + "log_dt"].reshape(-1, 1)
            br_t = jnp.transpose(weights[p + "b_re"], (2, 0, 1))
            bi_t = jnp.transpose(weights[p + "b_im"], (2, 0, 1))
            ar, ai, bbr, bbi = s5_disc_fwd(lr, li, ldt, br_t, bi_t, p + "disc_fwd")
            s5_prep[i] = dict(
                disc=(lr, li, ldt, br_t, bi_t), ar=ar.reshape(4, -1, 128), ai=ai.reshape(4, -1, 128),
                bdre=_block_diag_in(bbr, gc).astype(BF16), bdim=_block_diag_in(bbi, gc).astype(BF16),
                cdre=_block_diag_out(weights[p + "c_re"], gc).astype(BF16),
                cdim=_block_diag_out(-weights[p + "c_im"], gc).astype(BF16),
                d_skip=weights[p + "d_skip"].reshape(4, 1, cw), b_glu=vec(p + "b_glu"))

    gathered = dict(zip(layer_big[0], exchange_wait(gather_started[0], True, s5_prep[max(s5_prep)]["cdim"], "gather_wait_l0")))
    meta = _from_owner_blocks(gathered["meta_tokens"], 1)
    h = jnp.concatenate([jnp.zeros((first_pos, d), F32), meta, x[0]], axis=0)

    full = {}

    def layer_weights(i, kind, after):
        p = "l%d_" % i
        if i > 0:
            gathered.update(zip(layer_big[i], exchange_wait(gather_started[i], True, after, "gather_wait_l%d" % i)))
        w_in = gathered[p + "w_in"]
        if kind == "s5":
            full[i] = dict(s5_prep[i], w_in=w_in, w_glu=gathered[p + "w_glu"].reshape(4, cw, d),
                           w_out=gathered[p + "w_out"].reshape(4, cw, d))
        elif kind == "conv":
            ce = w_in.shape[2]
            nch = 2
            conv_w = _from_owner_blocks(gathered[p + "conv_w"], 1)
            full[i] = dict(
                w_in=w_in, conv_w=jnp.transpose(conv_w.reshape(CONV_K, nch, ce), (1, 0, 2)),
                conv_b=weights[p + "conv_b"].reshape(nch, 1, ce), w_out=gathered[p + "w_out"].reshape(nch, ce, d))
        else:
            gw = w_in.shape[2]
            full[i] = dict(
                w_in=w_in, w_grp=_from_owner_blocks(gathered[p + "w_grp"], 1),
                b_grp=_from_owner_blocks(gathered[p + "b_grp"], 1).reshape(4, 1, gw),
                scale=weights[p + "scale"].reshape(4, 1, gw), w_out=gathered[p + "w_out"].reshape(4, gw, d))
        return full[i]

    saved = {}
    for i, kind in enumerate(LAYER_KINDS):
        p, f, g = "l%d_" % i, layer_weights(i, kind, h), vec("norm%d_g" % i)
        if kind == "s5":
            u, z, xs = s5_fwd1(h, g, f["w_in"], f["bdre"], f["bdim"], p + "fwd_in")
            s = s5_scan_fwd(xs, f["ar"], f["ai"], p + "scan_fwd")
            saved[i] = (h, u, z, s)
            h = s5_fwd3(s, u, z, h, f["cdre"], f["cdim"], f["w_glu"], f["w_out"], f["d_skip"], f["b_glu"], p + "fwd_out")
        elif kind == "conv":
            h_new, halos = conv_fwd(h, g, f["w_in"], f["conv_w"], f["conv_b"], f["w_out"], p + "fwd")
            saved[i] = (h, halos)
            h = h_new
        else:
            h_new, halos = pool_fwd(h, g, f["w_in"], f["w_grp"], f["b_grp"], f["scale"], f["w_out"], first_pos, p + "fwd")
            saved[i] = (h, halos)
            h = h_new

    dh, dg_final, loss_tile = loss_head(h, target[0], vec("final_g"), pad_tiles, "loss_head")
    loss = lax.psum(loss_tile[0, 0], ("x", "y", "c"))

    grads = {"final_g": dg_final}
    names = weight_names()
    sh_names = sharded_names()
    rep_names = [n for n in names if n not in sh_names]

    def owner_blocks(a):
        return a.reshape(N_DEV, -1, a.shape[-1])

    def as2d(a):
        return a.reshape(-1, a.shape[-1])

    def pack(tree):
        flat = [jnp.pad(tree[n].reshape(-1), (0, -tree[n].size % 1024)) for n in rep_names]
        flat = jnp.concatenate(flat)
        return jnp.pad(flat, (0, -flat.size % (PACK_ROWS * 128))).reshape(-1, 128)

    layer_sharded, scatter_started = {}, {}
    ordered = jnp.zeros((), F32)
    for i in reversed(range(len(LAYER_KINDS))):
        kind = LAYER_KINDS[i]
        p, f, g = "l%d_" % i, full[i], vec("norm%d_g" % i) + ordered
        if kind == "s5":
            h_in, u, z, s = saved[i]
            dy, dp, dwo, dwg, dbg = s5_bwd3a(dh, s, u, z, f["cdre"], f["cdim"], f["w_glu"], f["w_out"],
                                             f["d_skip"], f["b_glu"] + ordered, p + "bwd_out")
            ds, dus, dcre, dcim, dd = s5_bwd3b(dy, s, u, f["cdre"], f["cdim"], f["d_skip"], p + "bwd_read")
            lam, dar, dai = s5_scan_bwd(ds, s, f["ar"], f["ai"], p + "scan_bwd")
            dp, dh, n, dbre, dbim, dg = s5_bwd1(lam, dus, u, dp, h_in, dh, g, f["w_in"], f["bdre"], f["bdim"], p + "bwd_in")
            dw_in = grad_w_in(n, dp, f["w_in"].shape[2], p + "grad_w_in")
            grads.update({p + "w_in": dw_in, p + "w_glu": dwg.reshape(N_DEV, -1, d), p + "w_out": dwo.reshape(N_DEV, -1, d),
                          p + "d_skip": dd, p + "b_glu": dbg})

            def replicated_grads(p=p, f=f, dar=dar, dai=dai, dbre=dbre, dbim=dbim, dcre=dcre, dcim=dcim, token=None):
                lr, li, ldt, br_t, bi_t = f["disc"]
                dlr, dli, dldt, dbr_t, dbi_t = s5_disc_bwd(
                    lr, li, ldt, br_t, bi_t, dar.reshape(lr.shape) + token, dai.reshape(lr.shape),
                    _block_diag_in_grad(dbre, gc), _block_diag_in_grad(dbim, gc), p + "disc_bwd")
                grads.update({
                    p + "lam_re": dlr, p + "lam_im": dli, p + "log_dt": dldt,
                    p + "b_re": jnp.transpose(dbr_t, (1, 2, 0)), p + "b_im": jnp.transpose(dbi_t, (1, 2, 0)),
                    p + "c_re": _block_diag_out_grad(dcre, gc), p + "c_im": -_block_diag_out_grad(dcim, gc)})
        elif kind == "conv":
            replicated_grads = None
            h_in, halos = saved[i]
            dh, n, dp, dwo, dcw, dcb, dg = conv_bwd(h_in, dh, halos, g, f["w_in"], f["conv_w"], f["conv_b"], f["w_out"], p + "bwd")
            dw_in = grad_w_in(n, dp, f["w_in"].shape[2], p + "grad_w_in")
            dconv_w = jnp.transpose(dcw[:, :CONV_K, :], (1, 0, 2)).reshape(CONV_K, -1)
            grads.update({p + "w_in": dw_in, p + "conv_w": _to_owner_blocks(dconv_w, 1), p + "conv_b": dcb,
                          p + "w_out": dwo.reshape(N_DEV, -1, d)})
        else:
            replicated_grads = None
            h_in, halos = saved[i]
            dh, n, dp, dwo, dwgrp, dbgrp, dsc, dg = pool_bwd(h_in, dh, halos, g, f["w_in"], f["w_grp"], f["b_grp"], f["scale"],
                                                             f["w_out"], first_pos, p + "bwd")
            dw_in = grad_w_in(n, dp, f["w_in"].shape[2], p + "grad_w_in")
            grads.update({p + "w_in": dw_in, p + "w_grp": _to_owner_blocks(dwgrp, 1),
                          p + "b_grp": _to_owner_blocks(dbgrp.reshape(4, -1), 1), p + "scale": dsc,
                          p + "w_out": dwo.reshape(N_DEV, -1, d)})
        grads["norm%d_g" % i] = dg
        layer_sharded[i] = ["l%d_%s" % (i, n) for n in SHARDED[kind]]
        scatter_started[i] = exchange_start([owner_blocks(grads[n]) for n in layer_sharded[i]], False, dh,
                                            "scatter_start_l%d" % i)
        ordered = scatter_started[i][4][0, 0]
        if replicated_grads is not None:
            replicated_grads(token=ordered)
    grad_x = dh[p0:][None]
    grads["meta_tokens"] = _to_owner_blocks(dh[first_pos:p0], 1)
    last = len(LAYER_KINDS)
    layer_sharded[last] = ["meta_tokens", "replicated"]
    scatter_started[last] = exchange_start([owner_blocks(grads["meta_tokens"]), pack(grads).reshape(N_DEV, -1, 128)], False,
                                           scatter_started[0][4], "scatter_start_replicated")

    out = {}
    received = {}
    after = scatter_started[last][4]
    for i in list(reversed(range(last))) + [last]:
        received.update(zip(layer_sharded[i], exchange_wait(scatter_started[i], False, after, "scatter_wait_%d" % i)))
        for n in layer_sharded[i]:
            if n != "replicated":
                res = sum_adamw(received[n], as2d(weights[n]), as2d(moments_m[n]), as2d(moments_v[n]), "update_" + n)
                out[n] = [r.reshape(weights[n].shape) for r in res]
                after = out[n][0]

    g_full = exchange([sum_parts(received["replicated"], "sum_replicated")], True, "gather_small_grads")[0].reshape(1, -1, 128)
    packed = sum_adamw(g_full, pack(weights), pack(moments_m), pack(moments_v), "update_replicated")
    offset = 0
    for n in rep_names:
        size = weights[n].size
        out[n] = [r.reshape(-1)[offset:offset + size].reshape(weights[n].shape) for r in packed]
        offset += size + (-size % 1024)

    return (loss, grad_x) + tuple(out[n][k] for k in range(4) for n in names)


def kernel(x, *rest):
    names = weight_names()
    nw = len(names)
    weights = dict(zip(names, rest[:nw]))
    target = rest[nw]
    moments_m = dict(zip(names, rest[nw + 1:2 * nw + 1]))
    moments_v = dict(zip(names, rest[2 * nw + 1:3 * nw + 1]))
    return _step(x, target, weights, moments_m, moments_v)
```

```python
import functools
import math

import jax
import jax.numpy as jnp
from jax import lax
from jax.experimental import pallas as pl
from jax.experimental.pallas import tpu as pltpu

F32 = jnp.float32
BF16 = jnp.bfloat16
EPS = 1e-6
N_DEV = 8
TOKEN_TILE = 256
SCAN_CHUNKS = 4
S5_GROUP = 16
S5_STATE = 64
POOL_WINDOWS = (2, 4, 8, 16)
POOL_HALO = 16
CONV_K = 3
CONV_HALO = 8
ADAM_LR = 0.001
ADAM_B1 = 0.9
ADAM_B2 = 0.999
ADAM_EPS = 1e-08
ADAM_WD = 0.01
ADAM_STEP = 10
GELU_C = math.sqrt(2.0 / math.pi)
GELU_A = 0.044715
UPDATE_TILE_ELEMS = 1 << 17
PACK_ROWS = 512
VMEM_LIMIT = 56 << 20

ANY = pl.BlockSpec(memory_space=pl.ANY)


def _params(vmem=VMEM_LIMIT, ndim=1):
    return pltpu.CompilerParams(vmem_limit_bytes=vmem, dimension_semantics=("arbitrary",) * ndim)


def _dot(a, b):
    return jnp.dot(a.astype(BF16), b.astype(BF16), preferred_element_type=F32)


def _dot_nt(a, b):
    return lax.dot_general(a.astype(BF16), b.astype(BF16), (((1,), (1,)), ((), ())), preferred_element_type=F32)


def _dot_tn(a, b):
    return lax.dot_general(a.astype(BF16), b.astype(BF16), (((0,), (0,)), ((), ())), preferred_element_type=F32)


def _rms_fwd(h, g):
    r = lax.rsqrt(jnp.mean(h * h, axis=-1, keepdims=True) + EPS)
    hh = h * r
    return hh * g, hh, r


def _rms_bwd(dn, hh, r, g):
    dhh = dn * g
    return r * (dhh - hh * jnp.mean(dhh * hh, axis=-1, keepdims=True))


def _sigmoid(x):
    return 1.0 / (1.0 + jnp.exp(-x))


def _silu_and_grad(z):
    s = _sigmoid(z)
    return z * s, s * (1.0 + z * (1.0 - s))


def _gelu(y):
    t = jnp.tanh(GELU_C * (y + GELU_A * y * y * y))
    return 0.5 * y * (1.0 + t), t


def _gelu_grad(y, t):
    return 0.5 * (1.0 + t) + 0.5 * y * (1.0 - t * t) * GELU_C * (1.0 + 3.0 * GELU_A * y * y)


def _rows(shape):
    return lax.broadcasted_iota(jnp.int32, shape, 0)


def _shift_down(x, k, halo):
    y = pltpu.roll(x, k, 0)
    rows = _rows(x.shape)
    for j in range(k):
        y = jnp.where(rows == j, halo[halo.shape[0] - k + j:halo.shape[0] - k + j + 1, :], y)
    return y


def _shift_up(x, k, halo):
    n = x.shape[0]
    y = pltpu.roll(x, n - k, 0)
    rows = _rows(x.shape)
    for j in range(k):
        y = jnp.where(rows == n - k + j, halo[j:j + 1, :], y)
    return y


def _window_sums_back(ext):
    out = []
    s = ext
    for k in (1, 2, 4, 8):
        s = s + pltpu.roll(s, k, 0)
        out.append(s)
    return out


def _window_sums_fwd(ext):
    n = ext.shape[0]
    out = []
    s = ext
    for k in (1, 2, 4, 8):
        s = s + pltpu.roll(s, n - k, 0)
        out.append(s)
    return out


def _pool_inv_count(tile, tt, first_pos, w, width):
    pos = _rows((tt, width)) + (tile * tt - first_pos + 1)
    return 1.0 / jnp.clip(pos, 1, w).astype(F32)


def _slab_spec(lp, tt, sw, index_map):
    nj = sw // 128
    return pl.BlockSpec((tt * nj, 128), index_map), (lp * 4 * nj, 128)


def _pack_pair(re, im):
    def rounded(v):
        return lax.bitcast_convert_type(v, jnp.int32) + 0x8000
    return lax.bitcast_convert_type((rounded(re) & -65536) | lax.shift_right_logical(rounded(im), 16), F32)


def _unpack_pair(w):
    b = lax.bitcast_convert_type(w, jnp.int32)
    return lax.bitcast_convert_type(b & -65536, F32), lax.bitcast_convert_type(lax.shift_left(b, 16), F32)


def _slab_load(ref):
    nj = ref.shape[0] // TOKEN_TILE
    return _unpack_pair(jnp.concatenate([ref[pl.ds(j, TOKEN_TILE, stride=nj), :] for j in range(nj)], axis=1))


def _slab_store(ref, re, im):
    nj = ref.shape[0] // TOKEN_TILE
    val = _pack_pair(re, im)
    for j in range(nj):
        ref[pl.ds(j, TOKEN_TILE, stride=nj), :] = val[:, j * 128:(j + 1) * 128]


def _s5_disc_math(lr, li, ldt, br, bi):
    dt = jnp.exp(ldt)
    mag = jnp.exp(lr * dt)
    ar = mag * jnp.cos(li * dt)
    ai = mag * jnp.sin(li * dt)
    den = lr * lr + li * li
    kr = ((ar - 1.0) * lr + ai * li) / den
    ki = (ai * lr - (ar - 1.0) * li) / den
    bbr = kr[None] * br - ki[None] * bi
    bbi = kr[None] * bi + ki[None] * br
    return ar, ai, bbr, bbi


def s5_disc_fwd(lr, li, ldt, br_t, bi_t, name):
    def body(lr_ref, li_ref, ldt_ref, br_ref, bi_ref, ar_ref, ai_ref, bbr_ref, bbi_ref):
        ar, ai, bbr, bbi = _s5_disc_math(lr_ref[...], li_ref[...], ldt_ref[...], br_ref[...], bi_ref[...])
        ar_ref[...] = ar
        ai_ref[...] = ai
        bbr_ref[...] = bbr
        bbi_ref[...] = bbi

    sd = jax.ShapeDtypeStruct
    return pl.pallas_call(
        body, name=name,
        out_shape=(sd(lr.shape, F32), sd(lr.shape, F32), sd(br_t.shape, F32), sd(br_t.shape, F32)),
    )(lr, li, ldt, br_t, bi_t)


def s5_disc_bwd(lr, li, ldt, br_t, bi_t, dar, dai, dbbr, dbbi, name):
    def body(lr_ref, li_ref, ldt_ref, br_ref, bi_ref, dar_ref, dai_ref, dbbr_ref, dbbi_ref,
             dlr_ref, dli_ref, dldt_ref, dbr_ref, dbi_ref):
        _, vjp = jax.vjp(_s5_disc_math, lr_ref[...], li_ref[...], ldt_ref[...], br_ref[...], bi_ref[...])
        dlr, dli, dldt, dbr, dbi = vjp((dar_ref[...], dai_ref[...], dbbr_ref[...], dbbi_ref[...]))
        dlr_ref[...] = dlr
        dli_ref[...] = dli
        dldt_ref[...] = dldt
        dbr_ref[...] = dbr
        dbi_ref[...] = dbi

    sd = jax.ShapeDtypeStruct
    return pl.pallas_call(
        body, name=name,
        out_shape=(sd(lr.shape, F32), sd(lr.shape, F32), sd(ldt.shape, F32), sd(br_t.shape, F32), sd(br_t.shape, F32)),
    )(lr, li, ldt, br_t, bi_t, dar, dai, dbbr, dbbi)


def s5_fwd1(h, g, w_in, bdre, bdim, name):
    lp, d = h.shape
    tt = TOKEN_TILE
    cw, sw = bdre.shape[1], bdre.shape[2]

    def body(h_ref, g_ref, w_hbm, bdre_hbm, bdim_hbm, u_ref, z_ref, x_ref, w, bre, bim, n_sc):
        i, c = pl.program_id(0), pl.program_id(1)

        @pl.when((i == 0) & (c == 0))
        def _():
            pltpu.sync_copy(w_hbm, w)
            pltpu.sync_copy(bdre_hbm, bre)
            pltpu.sync_copy(bdim_hbm, bim)

        @pl.when(c == 0)
        def _():
            n_sc[...] = _rms_fwd(h_ref[...], g_ref[...])[0].astype(BF16)

        n = n_sc[...]
        u = jnp.dot(n, w[c], preferred_element_type=F32)
        u_ref[...] = u
        z_ref[...] = jnp.dot(n, w[c + 4], preferred_element_type=F32)
        ub = u.astype(BF16)
        _slab_store(x_ref, jnp.dot(ub, bre[c], preferred_element_type=F32), jnp.dot(ub, bim[c], preferred_element_type=F32))

    sd = jax.ShapeDtypeStruct
    slab, slab_shape = _slab_spec(lp, tt, sw, lambda i, c: (i * 4 + c, 0))
    return pl.pallas_call(
        body, name=name, grid=(lp // tt, 4),
        in_specs=[pl.BlockSpec((tt, d), lambda i, c: (i, 0)), pl.BlockSpec((1, d), lambda i, c: (0, 0)), ANY, ANY, ANY],
        out_specs=[pl.BlockSpec((tt, cw), lambda i, c: (i, c)), pl.BlockSpec((tt, cw), lambda i, c: (i, c)), slab],
        out_shape=(sd((lp, d), F32), sd((lp, d), F32), sd(slab_shape, F32)),
        scratch_shapes=[pltpu.VMEM(w_in.shape, BF16), pltpu.VMEM(bdre.shape, BF16), pltpu.VMEM(bdim.shape, BF16),
                        pltpu.VMEM((tt, d), BF16)],
        compiler_params=_params(ndim=2),
    )(h, g, w_in, bdre, bdim)


def s5_scan_fwd(x, ar, ai, name):
    nj = ar.shape[1]
    tt = TOKEN_TILE
    cpb = SCAN_CHUNKS
    nt = x.shape[0] // (4 * tt * nj)

    def body(x_ref, ar_ref, ai_ref, s_ref, st_r, st_i):
        i, cg = pl.program_id(0), pl.program_id(1)

        @pl.when(i == 0)
        def _():
            for q in range(cpb):
                st_r[cg * cpb + q] = jnp.zeros((nj, 128), F32)
                st_i[cg * cpb + q] = jnp.zeros((nj, 128), F32)

        a_r = [ar_ref[cg * cpb + q] for q in range(cpb)]
        a_i = [ai_ref[cg * cpb + q] for q in range(cpb)]

        def step(t, carry):
            out = []
            for q in range(cpb):
                s_r, s_i = carry[q]
                rows = pl.ds(pl.multiple_of((q * tt + t) * nj, nj), nj)
                x_r, x_i = _unpack_pair(x_ref[rows, :])
                n_r = a_r[q] * s_r - a_i[q] * s_i + x_r
                n_i = a_r[q] * s_i + a_i[q] * s_r + x_i
                s_ref[rows, :] = _pack_pair(n_r, n_i)
                out.append((n_r, n_i))
            return tuple(out)

        init = tuple((st_r[cg * cpb + q], st_i[cg * cpb + q]) for q in range(cpb))
        final = lax.fori_loop(0, tt, step, init, unroll=8)
        for q in range(cpb):
            st_r[cg * cpb + q] = final[q][0]
            st_i[cg * cpb + q] = final[q][1]

    blk = pl.BlockSpec((cpb * tt * nj, 128), lambda i, cg: (i * (4 // cpb) + cg, 0))
    par = pl.BlockSpec((4, nj, 128), lambda i, cg: (0, 0, 0))
    sd = jax.ShapeDtypeStruct
    return pl.pallas_call(
        body, name=name, grid=(nt, 4 // cpb),
        in_specs=[blk, par, par], out_specs=blk,
        out_shape=sd(x.shape, F32),
        scratch_shapes=[pltpu.VMEM((4, nj, 128), F32), pltpu.VMEM((4, nj, 128), F32)],
        compiler_params=_params(ndim=2),
    )(x, ar, ai)


def s5_fwd3(s, u, z, h, cdre, cdim, w_glu, w_out, d_skip, b_glu, name):
    lp, d = h.shape
    tt = TOKEN_TILE
    sw, cw = cdre.shape[1], cdre.shape[2]

    def body(s_ref, u_ref, z_ref, h_ref, d_ref, bg_ref, cre_hbm, cim_hbm, wg_hbm, wo_hbm,
             o_ref, y_ref, q_ref, cre, cim, wg, wo, gy_sc):
        i, c = pl.program_id(0), pl.program_id(1)

        @pl.when((i == 0) & (c == 0))
        def _():
            pltpu.sync_copy(cre_hbm, cre)
            pltpu.sync_copy(cim_hbm, cim)
            pltpu.sync_copy(wg_hbm, wg)
            pltpu.sync_copy(wo_hbm, wo)

        s_r, s_i = _slab_load(s_ref)
        y = _dot(s_r, cre[c]) + _dot(s_i, cim[c]) + d_ref[c] * u_ref[...]
        y_ref[...] = y
        gy = _gelu(y)[0]
        gy_sc[c] = gy
        part = _dot(gy, wg[c])

        @pl.when(c == 0)
        def _():
            q_ref[...] = part

        @pl.when(c > 0)
        def _():
            q_ref[...] += part

        @pl.when(c == 3)
        def _():
            sig = _sigmoid(q_ref[...] + bg_ref[...])
            zz = z_ref[...]
            sz = zz * _sigmoid(zz)
            o = h_ref[...]
            for k in range(4):
                cols = slice(k * cw, (k + 1) * cw)
                o = o + _dot(gy_sc[k] * sig[:, cols] * sz[:, cols], wo[k])
            o_ref[...] = o

    row = lambda i, c: (i, 0)
    chunk = lambda i, c: (i, c)
    slab, _ = _slab_spec(lp, tt, sw, lambda i, c: (i * 4 + c, 0))
    sd = jax.ShapeDtypeStruct((lp, d), F32)
    return pl.pallas_call(
        body, name=name, grid=(lp // tt, 4),
        in_specs=[slab, pl.BlockSpec((tt, cw), chunk),
                  pl.BlockSpec((tt, d), row), pl.BlockSpec((tt, d), row),
                  pl.BlockSpec((4, 1, cw), lambda i, c: (0, 0, 0)), pl.BlockSpec((1, d), lambda i, c: (0, 0)),
                  ANY, ANY, ANY, ANY],
        out_specs=[pl.BlockSpec((tt, d), row), pl.BlockSpec((tt, cw), chunk), pl.BlockSpec((tt, d), row)],
        out_shape=(sd, sd, sd),
        scratch_shapes=[pltpu.VMEM(cdre.shape, BF16), pltpu.VMEM(cdim.shape, BF16), pltpu.VMEM(w_glu.shape, BF16),
                        pltpu.VMEM(w_out.shape, BF16), pltpu.VMEM((4, tt, cw), F32)],
        compiler_params=_params(ndim=2),
    )(s, u, z, h, d_skip, b_glu, cdre, cdim, w_glu, w_out)


def s5_bwd3a(dh, y, q, z, w_glu, w_out, b_glu, name):
    lp, d = dh.shape
    tt = TOKEN_TILE
    nt = lp // tt
    cw = w_glu.shape[1]

    def body(dh_ref, y_ref, q_ref, z_ref, bg_ref, wg_hbm, wo_hbm, dy_ref, dp_ref, dwo_hbm, dwg_hbm, dbg_hbm,
             wg, wo, dwo, dwg, dbg):
        i = pl.program_id(0)

        @pl.when(i == 0)
        def _():
            pltpu.sync_copy(wg_hbm, wg)
            pltpu.sync_copy(wo_hbm, wo)
            dwo[...] = jnp.zeros_like(dwo)
            dwg[...] = jnp.zeros_like(dwg)
            dbg[...] = jnp.zeros_like(dbg)

        sig = _sigmoid(q_ref[...] + bg_ref[...])
        sz, dsz = _silu_and_grad(z_ref[...])
        dhv = dh_ref[...]
        yv = y_ref[...]
        gy, t = _gelu(yv)
        dq_parts, dgy_parts = [], []
        for k in range(4):
            cols = slice(k * cw, (k + 1) * cw)
            gy_k, sig_k, sz_k = gy[:, cols], sig[:, cols], sz[:, cols]
            y2 = gy_k * sig_k
            dy3 = _dot_nt(dhv, wo[k])
            dwo[k] += _dot_tn(y2 * sz_k, dhv)
            dy2 = dy3 * sz_k
            dp_ref[0, :, cols] = (dy3 * y2 * dsz[:, cols]).astype(BF16)
            dq_parts.append(dy2 * gy_k * sig_k * (1.0 - sig_k))
            dgy_parts.append(dy2 * sig_k)
        dq = jnp.concatenate(dq_parts, axis=1)
        dbg[...] += jnp.sum(dq, axis=0, keepdims=True)
        dgelu = _gelu_grad(yv, t)
        for k in range(4):
            cols = slice(k * cw, (k + 1) * cw)
            dwg[k] += _dot_tn(gy[:, cols], dq)
            dy_ref[:, cols] = (dgy_parts[k] + _dot_nt(dq, wg[k])) * dgelu[:, cols]

        @pl.when(i == nt - 1)
        def _():
            pltpu.sync_copy(dwo, dwo_hbm)
            pltpu.sync_copy(dwg, dwg_hbm)
            pltpu.sync_copy(dbg, dbg_hbm)

    row = pl.BlockSpec((tt, d), lambda i: (i, 0))
    sd = jax.ShapeDtypeStruct
    return pl.pallas_call(
        body, name=name, grid=(nt,),
        in_specs=[row, row, row, row, pl.BlockSpec((1, d), lambda i: (0, 0)), ANY, ANY],
        out_specs=[row, pl.BlockSpec((1, tt, d), lambda i: (1, i, 0)), ANY, ANY, ANY],
        out_shape=(sd((lp, d), F32), sd((2, lp, d), BF16), sd(w_out.shape, F32), sd(w_glu.shape, F32), sd((1, d), F32)),
        scratch_shapes=[pltpu.VMEM(w_glu.shape, BF16), pltpu.VMEM(w_out.shape, BF16),
                        pltpu.VMEM(w_out.shape, F32), pltpu.VMEM(w_glu.shape, F32), pltpu.VMEM((1, d), F32)],
        compiler_params=_params(),
    )(dh, y, q, z, b_glu, w_glu, w_out)


def s5_bwd3b(dy, s, u, cdre, cdim, d_skip, name):
    lp, d = dy.shape
    tt = TOKEN_TILE
    nt = lp // tt
    sw, cw = cdre.shape[1], cdre.shape[2]

    def body(dy_ref, s_ref, u_ref, d_ref, cre_hbm, cim_hbm,
             ds_ref, dus_ref, dcre_hbm, dcim_hbm, dd_hbm, cre, cim, dcre, dcim, dd):
        i, c = pl.program_id(0), pl.program_id(1)

        @pl.when((i == 0) & (c == 0))
        def _():
            pltpu.sync_copy(cre_hbm, cre)
            pltpu.sync_copy(cim_hbm, cim)
            dcre[...] = jnp.zeros_like(dcre)
            dcim[...] = jnp.zeros_like(dcim)
            dd[...] = jnp.zeros_like(dd)

        dyv = dy_ref[...]
        dd[c] += jnp.sum(dyv * u_ref[...], axis=0, keepdims=True)
        dus_ref[...] = dyv * d_ref[c]
        _slab_store(ds_ref, _dot_nt(dyv, cre[c]), _dot_nt(dyv, cim[c]))
        s_r, s_i = _slab_load(s_ref)
        dcre[c] += _dot_tn(s_r, dyv)
        dcim[c] += _dot_tn(s_i, dyv)

        @pl.when((i == nt - 1) & (c == 3))
        def _():
            pltpu.sync_copy(dcre, dcre_hbm)
            pltpu.sync_copy(dcim, dcim_hbm)
            pltpu.sync_copy(dd, dd_hbm)

    chunk = lambda i, c: (i, c)
    sd = jax.ShapeDtypeStruct
    slab, slab_shape = _slab_spec(lp, tt, sw, lambda i, c: (i * 4 + c, 0))
    return pl.pallas_call(
        body, name=name, grid=(nt, 4),
        in_specs=[pl.BlockSpec((tt, cw), chunk), slab,
                  pl.BlockSpec((tt, cw), chunk), pl.BlockSpec((4, 1, cw), lambda i, c: (0, 0, 0)), ANY, ANY],
        out_specs=[slab, pl.BlockSpec((tt, cw), chunk), ANY, ANY, ANY],
        out_shape=(sd(slab_shape, F32), sd((lp, d), F32),
                   sd(cdre.shape, F32), sd(cdim.shape, F32), sd((4, 1, cw), F32)),
        scratch_shapes=[pltpu.VMEM(cdre.shape, BF16), pltpu.VMEM(cdim.shape, BF16),
                        pltpu.VMEM(cdre.shape, F32), pltpu.VMEM(cdim.shape, F32), pltpu.VMEM((4, 1, cw), F32)],
        compiler_params=_params(ndim=2),
    )(dy, s, u, d_skip, cdre, cdim)


def s5_scan_bwd(g, s, ar, ai, name):
    nj = ar.shape[1]
    tt = TOKEN_TILE
    cpb = SCAN_CHUNKS
    nt = g.shape[0] // (4 * tt * nj)

    def body(g_ref, s_ref, ar_ref, ai_ref, lam_ref, dar_ref, dai_ref, st_r, st_i, acc_r, acc_i):
        i, cg = pl.program_id(0), pl.program_id(1)

        @pl.when((i == 0) & (cg == 0))
        def _():
            for ref in (st_r, st_i, acc_r, acc_i):
                ref[...] = jnp.zeros_like(ref)

        a_r = [ar_ref[cg * cpb + q] for q in range(cpb)]
        a_i = [ai_ref[cg * cpb + q] for q in range(cpb)]

        def slab(q, t):
            return pl.ds(pl.multiple_of((q * tt + t) * nj, nj), nj)

        def adjoint(q, t, l_r, l_i):
            rows = slab(q, t)
            g_r, g_i = _unpack_pair(g_ref[rows, :])
            n_r = g_r + a_r[q] * l_r + a_i[q] * l_i
            n_i = g_i + a_r[q] * l_i - a_i[q] * l_r
            lam_ref[rows, :] = _pack_pair(n_r, n_i)
            return n_r, n_i

        def pair(q, t, l_r, l_i, d_r, d_i):
            p_r, p_i = _unpack_pair(s_ref[slab(q, t), :])
            return d_r + l_r * p_r + l_i * p_i, d_i + l_i * p_r - l_r * p_i

        def step(k, carry):
            t = tt - 1 - k
            out = []
            for q in range(cpb):
                l_r, l_i, d_r, d_i = carry[q]
                l_r, l_i = adjoint(q, t, l_r, l_i)
                d_r, d_i = pair(q, t - 1, l_r, l_i, d_r, d_i)
                out.append((l_r, l_i, d_r, d_i))
            return tuple(out)

        init = []
        for q in range(cpb):
            ch = cg * cpb + q
            l_r, l_i = st_r[ch], st_i[ch]
            d_r, d_i = pair(q, tt - 1, l_r, l_i, acc_r[ch], acc_i[ch])
            init.append((l_r, l_i, d_r, d_i))
        final = lax.fori_loop(0, tt - 1, step, tuple(init), unroll=8)
        for q in range(cpb):
            ch = cg * cpb + q
            l_r, l_i, d_r, d_i = final[q]
            l_r, l_i = adjoint(q, 0, l_r, l_i)
            st_r[ch] = l_r
            st_i[ch] = l_i
            acc_r[ch] = d_r
            acc_i[ch] = d_i
            dar_ref[ch] = d_r
            dai_ref[ch] = d_i

    blk = pl.BlockSpec((cpb * tt * nj, 128), lambda i, cg: ((nt - 1 - i) * (4 // cpb) + cg, 0))
    par = pl.BlockSpec((4, nj, 128), lambda i, cg: (0, 0, 0))
    sd = jax.ShapeDtypeStruct
    return pl.pallas_call(
        body, name=name, grid=(nt, 4 // cpb),
        in_specs=[blk, blk, par, par], out_specs=[blk, par, par],
        out_shape=(sd(g.shape, F32), sd((4, nj, 128), F32), sd((4, nj, 128), F32)),
        scratch_shapes=[pltpu.VMEM((4, nj, 128), F32)] * 4,
        compiler_params=_params(ndim=2),
    )(g, s, ar, ai)


def s5_bwd1(lam, dus, u, dp, h, dh, g, w_in, bdre, bdim, name):
    lp, d = h.shape
    tt = TOKEN_TILE
    nt = lp // tt
    cw, sw = bdre.shape[1], bdre.shape[2]

    def body(lam_ref, dus_ref, u_ref, dpz_ref, h_ref, dh_ref, g_ref, w_hbm, bre_hbm, bim_hbm,
             dpu_ref, dho_ref, n_ref, dbre_hbm, dbim_hbm, dg_hbm, w, bre, bim, dn_sc, dbre, dbim, dg):
        i, c = pl.program_id(0), pl.program_id(1)

        @pl.when((i == 0) & (c == 0))
        def _():
            pltpu.sync_copy(w_hbm, w)
            pltpu.sync_copy(bre_hbm, bre)
            pltpu.sync_copy(bim_hbm, bim)
            dbre[...] = jnp.zeros_like(dbre)
            dbim[...] = jnp.zeros_like(dbim)
            dg[...] = jnp.zeros_like(dg)

        (l_r, l_i), uv = _slab_load(lam_ref), u_ref[...]
        du = dus_ref[...] + _dot_nt(l_r, bre[c]) + _dot_nt(l_i, bim[c])
        dbre[c] += _dot_tn(uv, l_r)
        dbim[c] += _dot_tn(uv, l_i)
        dpu_ref[0] = du.astype(BF16)
        part = _dot_nt(du, w[c])

        @pl.when(c == 0)
        def _():
            dn_sc[...] = part

        @pl.when(c > 0)
        def _():
            dn_sc[...] += part

        @pl.when(c == 3)
        def _():
            dz = dpz_ref[0]
            dn = dn_sc[...]
            for k in range(4):
                dn = dn + _dot_nt(dz[:, k * cw:(k + 1) * cw], w[4 + k])
            gv = g_ref[...]
            n, hh, rr = _rms_fwd(h_ref[...], gv)
            n_ref[...] = n.T.astype(BF16)
            dg[...] += jnp.sum(dn * hh, axis=0, keepdims=True)
            dho_ref[...] = dh_ref[...] + _rms_bwd(dn, hh, rr, gv)

        @pl.when((i == nt - 1) & (c == 3))
        def _():
            pltpu.sync_copy(dbre, dbre_hbm)
            pltpu.sync_copy(dbim, dbim_hbm)
            pltpu.sync_copy(dg, dg_hbm)

    row = lambda i, c: (i, 0)
    chunk = lambda i, c: (i, c)
    sd = jax.ShapeDtypeStruct
    slab, _ = _slab_spec(lp, tt, sw, lambda i, c: (i * 4 + c, 0))
    return pl.pallas_call(
        body, name=name, grid=(nt, 4),
        in_specs=[slab, pl.BlockSpec((tt, cw), chunk),
                  pl.BlockSpec((tt, cw), chunk), pl.BlockSpec((1, tt, d), lambda i, c: (1, i, 0)),
                  pl.BlockSpec((tt, d), row), pl.BlockSpec((tt, d), row), pl.BlockSpec((1, d), lambda i, c: (0, 0)),
                  ANY, ANY, ANY],
        out_specs=[pl.BlockSpec((1, tt, cw), lambda i, c: (0, i, c)), pl.BlockSpec((tt, d), row),
                   pl.BlockSpec((d, tt), lambda i, c: (0, i)), ANY, ANY, ANY],
        out_shape=(sd(dp.shape, BF16), sd((lp, d), F32), sd((d, lp), BF16),
                   sd(bdre.shape, F32), sd(bdim.shape, F32), sd((1, d), F32)),
        input_output_aliases={3: 0},
        scratch_shapes=[pltpu.VMEM(w_in.shape, BF16), pltpu.VMEM(bdre.shape, BF16), pltpu.VMEM(bdim.shape, BF16),
                        pltpu.VMEM((tt, d), F32), pltpu.VMEM(bdre.shape, F32), pltpu.VMEM(bdim.shape, F32), pltpu.VMEM((1, d), F32)],
        compiler_params=_params(ndim=2),
    )(lam, dus, u, dp, h, dh, g, w_in, bdre, bdim)


def grad_w_in(n_t, dp, blk, name):
    d, lp = n_t.shape
    npart, _, width = dp.shape
    per = width // blk

    def body(n_ref, dp_ref, o_ref):
        o_ref[0] = jnp.dot(n_ref[...], dp_ref[0], preferred_element_type=F32)

    return pl.pallas_call(
        body, name=name, grid=(npart * per,),
        in_specs=[pl.BlockSpec((d, lp), lambda j: (0, 0), pipeline_mode=pl.Buffered(1)),
                  pl.BlockSpec((1, lp, blk), lambda j: (j // per, 0, j % per))],
        out_specs=pl.BlockSpec((1, d, blk), lambda j: (j, 0, 0)),
        out_shape=jax.ShapeDtypeStruct((npart * per, d, blk), F32),
        compiler_params=_params(),
    )(n_t, dp)


def _conv_fwd_chunk(n, w, cw_ref, cb_ref, halo, c, nch):
    bg = jnp.dot(n, w[c], preferred_element_type=F32)
    cg = jnp.dot(n, w[nch + c], preferred_element_type=F32)
    v = jnp.dot(n, w[2 * nch + c], preferred_element_type=F32)
    z = jnp.dot(n, w[3 * nch + c], preferred_element_type=F32)
    hc = cg * v
    taps = cw_ref[c]
    conv = taps[2:3, :] * hc + taps[1:2, :] * _shift_down(hc, 1, halo) + taps[0:1, :] * _shift_down(hc, 2, halo) + cb_ref[c]
    return bg, cg, v, z, hc, conv


def conv_fwd(h, g, w_in, conv_w, conv_b, w_out, name):
    lp, d = h.shape
    tt = TOKEN_TILE
    nt = lp // tt
    nch, ce = w_out.shape[0], w_out.shape[1]

    def body(h_ref, g_ref, cw_ref, cb_ref, w_hbm, wo_hbm, o_ref, halo_ref, w, wo, halo):
        i = pl.program_id(0)

        @pl.when(i == 0)
        def _():
            pltpu.sync_copy(w_hbm, w)
            pltpu.sync_copy(wo_hbm, wo)
            halo[...] = jnp.zeros_like(halo)

        hv = h_ref[...]
        n = _rms_fwd(hv, g_ref[...])[0].astype(BF16)
        o = hv
        for c in range(nch):
            bg, _, _, z, hc, conv = _conv_fwd_chunk(n, w, cw_ref, cb_ref, halo[c], c, nch)
            o = o + _dot(bg * conv * (z * _sigmoid(z)), wo[c])
            halo[c] = hc[tt - CONV_HALO:, :]
            halo_ref[0, c] = hc[tt - CONV_HALO:, :]
        o_ref[...] = o

    sd = jax.ShapeDtypeStruct
    return pl.pallas_call(
        body, name=name, grid=(nt,),
        in_specs=[pl.BlockSpec((tt, d), lambda i: (i, 0)), pl.BlockSpec((1, d), lambda i: (0, 0)),
                  pl.BlockSpec(conv_w.shape, lambda i: (0, 0, 0)), pl.BlockSpec(conv_b.shape, lambda i: (0, 0, 0)), ANY, ANY],
        out_specs=[pl.BlockSpec((tt, d), lambda i: (i, 0)), pl.BlockSpec((1, nch, CONV_HALO, ce), lambda i: (i, 0, 0, 0))],
        out_shape=(sd((lp, d), F32), sd((nt, nch, CONV_HALO, ce), F32)),
        scratch_shapes=[pltpu.VMEM(w_in.shape, BF16), pltpu.VMEM(w_out.shape, BF16), pltpu.VMEM((nch, CONV_HALO, ce), F32)],
        compiler_params=_params(),
    )(h, g, conv_w, conv_b, w_in, w_out)


def conv_bwd(h, dh, halos, g, w_in, conv_w, conv_b, w_out, name):
    lp, d = h.shape
    tt = TOKEN_TILE
    nt = lp // tt
    nch, ce = w_out.shape[0], w_out.shape[1]

    def body(h_ref, dh_ref, halo_ref, g_ref, cw_ref, cb_ref, w_hbm, wo_hbm,
             dho_ref, n_ref, dp_ref, dwo_hbm, dcw_hbm, dcb_hbm, dg_hbm, w, wo, nxt, dwo, dcw, dcb, dg):
        i = pl.program_id(0)

        @pl.when(i == 0)
        def _():
            pltpu.sync_copy(w_hbm, w)
            pltpu.sync_copy(wo_hbm, wo)
            for ref in (nxt, dwo, dcw, dcb, dg):
                ref[...] = jnp.zeros_like(ref)

        gv = g_ref[...]
        nf, hh, rr = _rms_fwd(h_ref[...], gv)
        n = nf.astype(BF16)
        n_ref[...] = nf.T.astype(BF16)
        dhv = dh_ref[...]
        has_prev = (i < nt - 1).astype(F32)
        dn = jnp.zeros((tt, d), F32)
        for c in range(nch):
            halo = halo_ref[0, c] * has_prev
            bg, cg, v, z, hc, conv = _conv_fwd_chunk(n, w, cw_ref, cb_ref, halo, c, nch)
            sz, dsz = _silu_and_grad(z)
            y1 = bg * conv
            dy2 = _dot_nt(dhv, wo[c])
            dwo[c] += _dot_tn(y1 * sz, dhv)
            dy1 = dy2 * sz
            dz = dy2 * y1 * dsz
            dbg = dy1 * conv
            dconv = dy1 * bg
            dcb[c] += jnp.sum(dconv, axis=0, keepdims=True)
            up1 = _shift_up(dconv, 1, nxt[c])
            up2 = _shift_up(dconv, 2, nxt[c])
            nxt[c] = dconv[:CONV_HALO, :]
            taps = cw_ref[c]
            dhc = taps[2:3, :] * dconv + taps[1:2, :] * up1 + taps[0:1, :] * up2
            dcw[c, 0:1, :] += jnp.sum(hc * up2, axis=0, keepdims=True)
            dcw[c, 1:2, :] += jnp.sum(hc * up1, axis=0, keepdims=True)
            dcw[c, 2:3, :] += jnp.sum(hc * dconv, axis=0, keepdims=True)
            dcg = dhc * v
            dv = dhc * cg
            cols = slice(c * ce, (c + 1) * ce)
            for p, val in enumerate((dbg, dcg, dv, dz)):
                dp_ref[p, :, cols] = val.astype(BF16)
                dn = dn + _dot_nt(val, w[p * nch + c])
        dg[...] += jnp.sum(dn * hh, axis=0, keepdims=True)
        dho_ref[...] = dhv + _rms_bwd(dn, hh, rr, gv)

        @pl.when(i == nt - 1)
        def _():
            pltpu.sync_copy(dwo, dwo_hbm)
            pltpu.sync_copy(dcw, dcw_hbm)
            pltpu.sync_copy(dcb, dcb_hbm)
            pltpu.sync_copy(dg, dg_hbm)

    rev = lambda i: (nt - 1 - i, 0)
    sd = jax.ShapeDtypeStruct
    return pl.pallas_call(
        body, name=name, grid=(nt,),
        in_specs=[pl.BlockSpec((tt, d), rev), pl.BlockSpec((tt, d), rev),
                  pl.BlockSpec((1, nch, CONV_HALO, ce), lambda i: (jnp.maximum(nt - 2 - i, 0), 0, 0, 0)),
                  pl.BlockSpec((1, d), lambda i: (0, 0)),
                  pl.BlockSpec(conv_w.shape, lambda i: (0, 0, 0)), pl.BlockSpec(conv_b.shape, lambda i: (0, 0, 0)), ANY, ANY],
        out_specs=[pl.BlockSpec((tt, d), rev), pl.BlockSpec((d, tt), lambda i: (0, nt - 1 - i)),
                   pl.BlockSpec((4, tt, nch * ce), lambda i: (0, nt - 1 - i, 0)), ANY, ANY, ANY, ANY],
        out_shape=(sd((lp, d), F32), sd((d, lp), BF16), sd((4, lp, nch * ce), BF16),
                   sd(w_out.shape, F32), sd((nch, 8, ce), F32), sd((nch, 1, ce), F32), sd((1, d), F32)),
        scratch_shapes=[pltpu.VMEM(w_in.shape, BF16), pltpu.VMEM(w_out.shape, BF16), pltpu.VMEM((nch, CONV_HALO, ce), F32),
                        pltpu.VMEM(w_out.shape, F32), pltpu.VMEM((nch, 8, ce), F32), pltpu.VMEM((nch, 1, ce), F32),
                        pltpu.VMEM((1, d), F32)],
        compiler_params=_params(),
    )(h, dh, halos, g, conv_w, conv_b, w_in, w_out)


def _pool_fwd_group(n, w, wg, bg_ref, sc_ref, halo, k, tile, tt, first_pos):
    u = jnp.dot(n, w[k], preferred_element_type=F32)
    z = jnp.dot(n, w[4 + k], preferred_element_type=F32)
    ext = jnp.concatenate([halo, u], axis=0)
    win = _window_sums_back(ext)[k][POOL_HALO:, :]
    mixed = win * _pool_inv_count(tile, tt, first_pos, POOL_WINDOWS[k], u.shape[1]) - u
    outs = _dot(mixed, wg[k]) + bg_ref[k]
    return u, z, mixed, outs, outs * sc_ref[k]


def pool_fwd(h, g, w_in, w_grp, b_grp, scale, w_out, first_pos, name):
    lp, d = h.shape
    tt = TOKEN_TILE
    nt = lp // tt
    gw = w_grp.shape[1]

    def body(h_ref, g_ref, bg_ref, sc_ref, w_hbm, wg_hbm, wo_hbm, o_ref, halo_ref, w, wg, wo, halo):
        i = pl.program_id(0)

        @pl.when(i == 0)
        def _():
            pltpu.sync_copy(w_hbm, w)
            pltpu.sync_copy(wg_hbm, wg)
            pltpu.sync_copy(wo_hbm, wo)
            halo[...] = jnp.zeros_like(halo)

        hv = h_ref[...]
        n = _rms_fwd(hv, g_ref[...])[0].astype(BF16)
        o = hv
        for k in range(4):
            u, z, _, _, yp = _pool_fwd_group(n, w, wg, bg_ref, sc_ref, halo[k], k, i, tt, first_pos)
            o = o + _dot(yp * (z * _sigmoid(z)), wo[k])
            halo[k] = u[tt - POOL_HALO:, :]
            halo_ref[0, k] = u[tt - POOL_HALO:, :]
        o_ref[...] = o

    sd = jax.ShapeDtypeStruct
    small = pl.BlockSpec((4, 1, gw), lambda i: (0, 0, 0))
    return pl.pallas_call(
        body, name=name, grid=(nt,),
        in_specs=[pl.BlockSpec((tt, d), lambda i: (i, 0)), pl.BlockSpec((1, d), lambda i: (0, 0)), small, small, ANY, ANY, ANY],
        out_specs=[pl.BlockSpec((tt, d), lambda i: (i, 0)), pl.BlockSpec((1, 4, POOL_HALO, gw), lambda i: (i, 0, 0, 0))],
        out_shape=(sd((lp, d), F32), sd((nt, 4, POOL_HALO, gw), F32)),
        scratch_shapes=[pltpu.VMEM(w_in.shape, BF16), pltpu.VMEM(w_grp.shape, BF16), pltpu.VMEM(w_out.shape, BF16),
                        pltpu.VMEM((4, POOL_HALO, gw), F32)],
        compiler_params=_params(),
    )(h, g, b_grp, scale, w_in, w_grp, w_out)


def pool_bwd(h, dh, halos, g, w_in, w_grp, b_grp, scale, w_out, first_pos, name):
    lp, d = h.shape
    tt = TOKEN_TILE
    nt = lp // tt
    gw = w_grp.shape[1]

    def body(h_ref, dh_ref, halo_ref, g_ref, bg_ref, sc_ref, w_hbm, wg_hbm, wo_hbm,
             dho_ref, n_ref, dp_ref, dwo_hbm, dwg_hbm, dbg_hbm, dsc_hbm, dg_hbm,
             w, wg, wo, nxt, dwo, dwg, dbg, dsc, dg):
        i = pl.program_id(0)
        tile = nt - 1 - i

        @pl.when(i == 0)
        def _():
            pltpu.sync_copy(w_hbm, w)
            pltpu.sync_copy(wg_hbm, wg)
            pltpu.sync_copy(wo_hbm, wo)
            for ref in (nxt, dwo, dwg, dbg, dsc, dg):
                ref[...] = jnp.zeros_like(ref)

        gv = g_ref[...]
        nf, hh, rr = _rms_fwd(h_ref[...], gv)
        n = nf.astype(BF16)
        n_ref[...] = nf.T.astype(BF16)
        dhv = dh_ref[...]
        has_prev = (i < nt - 1).astype(F32)
        dn = jnp.zeros((tt, d), F32)
        for k in range(4):
            u, z, mixed, outs, yp = _pool_fwd_group(n, w, wg, bg_ref, sc_ref, halo_ref[0, k] * has_prev, k, tile, tt, first_pos)
            sz, dsz = _silu_and_grad(z)
            dy = _dot_nt(dhv, wo[k])
            dwo[k] += _dot_tn(yp * sz, dhv)
            dyp = dy * sz
            dz = dy * yp * dsz
            dsc[k] += jnp.sum(dyp * outs, axis=0, keepdims=True)
            douts = dyp * sc_ref[k]
            dbg[k] += jnp.sum(douts, axis=0, keepdims=True)
            dwg[k] += _dot_tn(mixed, douts)
            dmixed = _dot_nt(douts, wg[k])
            dm = dmixed * _pool_inv_count(tile, tt, first_pos, POOL_WINDOWS[k], gw)
            ext = jnp.concatenate([dm, nxt[k]], axis=0)
            du = _window_sums_fwd(ext)[k][:tt, :] - dmixed
            nxt[k] = dm[:POOL_HALO, :]
            cols = slice(k * gw, (k + 1) * gw)
            dp_ref[0, :, cols] = du.astype(BF16)
            dp_ref[1, :, cols] = dz.astype(BF16)
            dn = dn + _dot_nt(du, w[k]) + _dot_nt(dz, w[4 + k])
        dg[...] += jnp.sum(dn * hh, axis=0, keepdims=True)
        dho_ref[...] = dhv + _rms_bwd(dn, hh, rr, gv)

        @pl.when(i == nt - 1)
        def _():
            pltpu.sync_copy(dwo, dwo_hbm)
            pltpu.sync_copy(dwg, dwg_hbm)
            pltpu.sync_copy(dbg, dbg_hbm)
            pltpu.sync_copy(dsc, dsc_hbm)
            pltpu.sync_copy(dg, dg_hbm)

    rev = lambda i: (nt - 1 - i, 0)
    sd = jax.ShapeDtypeStruct
    small = pl.BlockSpec((4, 1, gw), lambda i: (0, 0, 0))
    return pl.pallas_call(
        body, name=name, grid=(nt,),
        in_specs=[pl.BlockSpec((tt, d), rev), pl.BlockSpec((tt, d), rev),
                  pl.BlockSpec((1, 4, POOL_HALO, gw), lambda i: (jnp.maximum(nt - 2 - i, 0), 0, 0, 0)),
                  pl.BlockSpec((1, d), lambda i: (0, 0)), small, small, ANY, ANY, ANY],
        out_specs=[pl.BlockSpec((tt, d), rev), pl.BlockSpec((d, tt), lambda i: (0, nt - 1 - i)),
                   pl.BlockSpec((2, tt, 4 * gw), lambda i: (0, nt - 1 - i, 0)), ANY, ANY, ANY, ANY, ANY],
        out_shape=(sd((lp, d), F32), sd((d, lp), BF16), sd((2, lp, 4 * gw), BF16),
                   sd(w_out.shape, F32), sd(w_grp.shape, F32), sd((4, 1, gw), F32), sd((4, 1, gw), F32), sd((1, d), F32)),
        scratch_shapes=[pltpu.VMEM(w_in.shape, BF16), pltpu.VMEM(w_grp.shape, BF16), pltpu.VMEM(w_out.shape, BF16),
                        pltpu.VMEM((4, POOL_HALO, gw), F32), pltpu.VMEM(w_out.shape, F32), pltpu.VMEM(w_grp.shape, F32),
                        pltpu.VMEM((4, 1, gw), F32), pltpu.VMEM((4, 1, gw), F32), pltpu.VMEM((1, d), F32)],
        compiler_params=_params(),
    )(h, dh, halos, g, b_grp, scale, w_in, w_grp, w_out)


def loss_head(h, target, g, pad_tiles, name):
    lp, d = h.shape
    tt = TOKEN_TILE
    nt = lp // tt

    def body(h_ref, t_ref, g_ref, dh_ref, dg_ref, loss_ref, acc):
        i = pl.program_id(0)

        @pl.when(i == 0)
        def _():
            acc[...] = jnp.zeros_like(acc)
            dg_ref[...] = jnp.zeros_like(dg_ref)

        @pl.when(i < pad_tiles)
        def _():
            dh_ref[...] = jnp.zeros_like(dh_ref)

        @pl.when(i >= pad_tiles)
        def _():
            gv = g_ref[...]
            n, hh, rr = _rms_fwd(h_ref[...], gv)
            err = n - t_ref[...]
            acc[...] += 0.5 * jnp.sum(jnp.mean(err * err, axis=-1, keepdims=True), axis=0, keepdims=True)
            dn = err * (1.0 / d)
            dg_ref[...] += jnp.sum(dn * hh, axis=0, keepdims=True)
            dh_ref[...] = _rms_bwd(dn, hh, rr, gv)

        loss_ref[...] = jnp.broadcast_to(acc[...], loss_ref.shape)

    sd = jax.ShapeDtypeStruct
    return pl.pallas_call(
        body, name=name, grid=(nt,),
        in_specs=[pl.BlockSpec((tt, d), lambda i: (i, 0)), pl.BlockSpec((tt, d), lambda i: (jnp.maximum(i - pad_tiles, 0), 0)),
                  pl.BlockSpec((1, d), lambda i: (0, 0))],
        out_specs=[pl.BlockSpec((tt, d), lambda i: (i, 0)), pl.BlockSpec((1, d), lambda i: (0, 0)),
                   pl.BlockSpec((8, 128), lambda i: (0, 0))],
        out_shape=(sd((lp, d), F32), sd((1, d), F32), sd((8, 128), F32)),
        scratch_shapes=[pltpu.VMEM((1, 1), F32)],
        compiler_params=_params(),
    )(h, target, g)


def exchange(arrs, gather, name):
    n = len(arrs)

    def body(*refs):
        ins, outs = refs[:n], refs[n:2 * n]
        send_sems, recv_sems, own_sems = refs[2 * n:]
        x, y, c = lax.axis_index("x"), lax.axis_index("y"), lax.axis_index("c")
        me = 4 * x + 2 * y + c
        own = []
        for a in range(n):
            cp = pltpu.make_async_copy(ins[a] if gather else ins[a].at[me], outs[a].at[me], own_sems.at[a])
            cp.start()
            own.append(cp)
        sent = []
        for k in range(1, N_DEV):
            px = 1 - x if k & 4 else x
            py = 1 - y if k & 2 else y
            pc = 1 - c if k & 1 else c
            peer = 4 * px + 2 * py + pc
            for a in range(n):
                cp = pltpu.make_async_remote_copy(
                    src_ref=ins[a] if gather else ins[a].at[peer], dst_ref=outs[a].at[me],
                    send_sem=send_sems.at[a, k - 1], recv_sem=recv_sems.at[a, k - 1],
                    device_id=(px, py, pc), device_id_type=pl.DeviceIdType.MESH)
                cp.start()
                sent.append((cp, a, k, peer, (px, py, pc)))
        for cp, a, k, peer, pid in sent:
            cp.wait_send()
            pltpu.make_async_remote_copy(
                src_ref=ins[a] if gather else ins[a].at[peer], dst_ref=outs[a].at[peer],
                send_sem=send_sems.at[a, k - 1], recv_sem=recv_sems.at[a, k - 1],
                device_id=pid, device_id_type=pl.DeviceIdType.MESH).wait_recv()
        for cp in own:
            cp.wait()

    hbm = pl.BlockSpec(memory_space=pltpu.HBM)
    out_shape = tuple(jax.ShapeDtypeStruct(((N_DEV,) + a.shape) if gather else a.shape, a.dtype) for a in arrs)
    return pl.pallas_call(
        body, name=name, in_specs=[hbm] * n, out_specs=[hbm] * n, out_shape=out_shape,
        scratch_shapes=[pltpu.SemaphoreType.DMA((n, N_DEV - 1)), pltpu.SemaphoreType.DMA((n, N_DEV - 1)),
                        pltpu.SemaphoreType.DMA((n,))],
    )(*[pltpu.with_memory_space_constraint(a, pltpu.HBM) for a in arrs])


def _peers(x, y, c):
    out = []
    for k in range(1, N_DEV):
        px = 1 - x if k & 4 else x
        py = 1 - y if k & 2 else y
        pc = 1 - c if k & 1 else c
        out.append((k, (px, py, pc), 4 * px + 2 * py + pc))
    return out


def exchange_start(arrs, gather, after, name):
    n = len(arrs)
    me = 4 * lax.axis_index("x") + 2 * lax.axis_index("y") + lax.axis_index("c")
    lands = []
    for a in arrs:
        own = a[None] if gather else lax.dynamic_index_in_dim(a, me, 0, keepdims=True)
        lands.append(lax.dynamic_update_index_in_dim(lax.empty(((N_DEV,) + a.shape) if gather else a.shape, a.dtype), own, me, 0))

    def body(*refs):
        ins, land = refs[:n], refs[n:2 * n]
        send_sems, recv_sems, token = refs[2 * n + 1], refs[2 * n + 2], refs[4 * n + 3]
        x, y, c = lax.axis_index("x"), lax.axis_index("y"), lax.axis_index("c")
        me = 4 * x + 2 * y + c
        for k, pid, peer in _peers(x, y, c):
            for a in range(n):
                pltpu.make_async_remote_copy(
                    src_ref=ins[a] if gather else ins[a].at[peer], dst_ref=land[a].at[me],
                    send_sem=send_sems.at[a * (N_DEV - 1) + k - 1], recv_sem=recv_sems.at[a * (N_DEV - 1) + k - 1],
                    device_id=pid, device_id_type=pl.DeviceIdType.MESH).start()
        token[...] = jnp.zeros_like(token)

    hbm = pl.BlockSpec(memory_space=pltpu.HBM)
    sem = pl.BlockSpec(memory_space=pltpu.SEMAPHORE)
    sems = pltpu.SemaphoreType.DMA((n * (N_DEV - 1),))
    res = pl.pallas_call(
        body, name=name, in_specs=[hbm] * (2 * n) + [ANY],
        out_specs=[sem, sem] + [hbm] * (2 * n) + [pl.BlockSpec(memory_space=pltpu.VMEM)],
        out_shape=[sems, sems] + [pltpu.HBM(a.shape, a.dtype) for a in arrs] + [pltpu.HBM(l.shape, l.dtype) for l in lands]
        + [jax.ShapeDtypeStruct((8, 128), F32)],
        input_output_aliases={a: 2 + a for a in range(2 * n)},
        compiler_params=pltpu.CompilerParams(has_side_effects=pltpu.SideEffectType.DATAFLOW_SIDE_EFFECTING),
    )(*[pltpu.with_memory_space_constraint(a, pltpu.HBM) for a in list(arrs) + lands], after)
    return res[0], res[1], res[2:2 + n], res[2 + n:2 + 2 * n], res[-1]


def exchange_wait(started, gather, after, name):
    send_sems, recv_sems, srcs, lands, _ = started
    n = len(srcs)
    after = list(after) if isinstance(after, (list, tuple)) else [after]

    def body(*refs):
        ins, land = refs[:n], refs[n:2 * n]
        send_sems, recv_sems = refs[2 * n], refs[2 * n + 1]
        x, y, c = lax.axis_index("x"), lax.axis_index("y"), lax.axis_index("c")
        for k, pid, peer in _peers(x, y, c):
            for a in range(n):
                cp = pltpu.make_async_remote_copy(
                    src_ref=ins[a] if gather else ins[a].at[peer], dst_ref=land[a].at[peer],
                    send_sem=send_sems.at[a * (N_DEV - 1) + k - 1], recv_sem=recv_sems.at[a * (N_DEV - 1) + k - 1],
                    device_id=pid, device_id_type=pl.DeviceIdType.MESH)
                cp.wait_send()
                cp.wait_recv()

    hbm = pl.BlockSpec(memory_space=pltpu.HBM)
    sem = pl.BlockSpec(memory_space=pltpu.SEMAPHORE)
    res = pl.pallas_call(
        body, name=name, in_specs=[hbm] * (2 * n) + [sem, sem] + [ANY] * len(after),
        out_specs=[hbm] * (2 * n),
        out_shape=[pltpu.HBM(a.shape, a.dtype) for a in list(srcs) + list(lands)],
        input_output_aliases={a: a for a in range(2 * n)},
        compiler_params=pltpu.CompilerParams(has_side_effects=pltpu.SideEffectType.DATAFLOW_SIDE_EFFECTING),
    )(*srcs, *lands, send_sems, recv_sems, *after)
    return res[n:]


def _adamw(w, g, m, v):
    m = ADAM_B1 * m + (1.0 - ADAM_B1) * g
    v = ADAM_B2 * v + (1.0 - ADAM_B2) * (g * g)
    m_hat = m / (1.0 - ADAM_B1 ** ADAM_STEP)
    v_hat = v / (1.0 - ADAM_B2 ** ADAM_STEP)
    return -ADAM_LR * (m_hat / (jnp.sqrt(v_hat) + ADAM_EPS) + ADAM_WD * w), m, v


def _update_tile_rows(rows, cols):
    if rows * cols <= UPDATE_TILE_ELEMS:
        return rows
    return max(t for t in range(8, UPDATE_TILE_ELEMS // cols + 1, 8) if rows % t == 0)


def _sum_in_order(p_ref):
    g = p_ref[0]
    for j in range(1, p_ref.shape[0]):
        g = g + p_ref[j]
    return g


def sum_parts(parts, name):
    nparts, rows, cols = parts.shape
    tr = _update_tile_rows(rows, cols)

    def body(p_ref, g_ref):
        g_ref[...] = _sum_in_order(p_ref)

    return pl.pallas_call(
        body, name=name, grid=(rows // tr,),
        in_specs=[pl.BlockSpec((nparts, tr, cols), lambda i: (0, i, 0))],
        out_specs=pl.BlockSpec((tr, cols), lambda i: (i, 0)), out_shape=jax.ShapeDtypeStruct((rows, cols), F32),
        compiler_params=_params(),
    )(parts)


def sum_adamw(parts, w, m, v, name):
    rows, cols = w.shape
    nparts = parts.shape[0]
    tr = _update_tile_rows(rows, cols)

    def body(p_ref, w_ref, m_ref, v_ref, g_ref, d_ref, nm_ref, nv_ref):
        g = _sum_in_order(p_ref)
        delta, nm, nv = _adamw(w_ref[...], g, m_ref[...], v_ref[...])
        g_ref[...] = g
        d_ref[...] = delta
        nm_ref[...] = nm
        nv_ref[...] = nv

    blk = pl.BlockSpec((tr, cols), lambda i: (i, 0))
    sd = jax.ShapeDtypeStruct((rows, cols), F32)
    return pl.pallas_call(
        body, name=name, grid=(rows // tr,),
        in_specs=[pl.BlockSpec((nparts, tr, cols), lambda i: (0, i, 0)), blk, blk, blk],
        out_specs=[blk] * 4, out_shape=(sd,) * 4,
        compiler_params=_params(),
    )(parts, w, m, v)


S5_NAMES = ("w_in", "lam_re", "lam_im", "log_dt", "b_re", "b_im", "c_re", "c_im", "d_skip", "w_glu", "b_glu", "w_out")
CONV_NAMES = ("w_in", "conv_w", "conv_b", "w_out")
POOL_NAMES = ("w_in", "w_grp", "b_grp", "scale", "w_out")
LAYER_KINDS = ("s5", "conv", "pool", "s5")
LAYER_NAMES = {"s5": S5_NAMES, "conv": CONV_NAMES, "pool": POOL_NAMES}
SHARDED = {"s5": ("w_in", "w_glu", "w_out"), "conv": ("w_in", "conv_w", "w_out"), "pool": ("w_in", "w_grp", "b_grp", "w_out")}
GATHER_F32 = ("conv_w", "b_grp")


def weight_names():
    names = ["meta_tokens"]
    for i, kind in enumerate(LAYER_KINDS):
        names.append("norm%d_g" % i)
        names += ["l%d_%s" % (i, n) for n in LAYER_NAMES[kind]]
    names.append("final_g")
    return names


def sharded_names():
    return ["meta_tokens"] + ["l%d_%s" % (i, n) for i, kind in enumerate(LAYER_KINDS) for n in SHARDED[kind]]


def _block_diag_in(bb_t, gc):
    i, g, p = bb_t.shape
    t = bb_t.astype(BF16).reshape(i, 4, gc, p)
    return jnp.einsum("icjp,jk->cjikp", t, jnp.eye(gc, dtype=BF16)).reshape(4, gc * i, gc * p)


def _block_diag_in_grad(dbd, gc):
    i, p = dbd.shape[1] // gc, dbd.shape[2] // gc
    return jnp.einsum("cjijp->icjp", dbd.reshape(4, gc, i, gc, p)).reshape(i, 4 * gc, p)


def _block_diag_out(cc, gc):
    g, i, p = cc.shape
    return jnp.einsum("cjip,jk->cjpki", cc.astype(BF16).reshape(4, gc, i, p), jnp.eye(gc, dtype=BF16)).reshape(4, gc * p, gc * i)


def _block_diag_out_grad(dcd, gc):
    p, i = dcd.shape[1] // gc, dcd.shape[2] // gc
    return jnp.einsum("cjpji->cjip", dcd.reshape(4, gc, p, gc, i)).reshape(4 * gc, i, p)


def _to_owner_blocks(a, axis):
    shape = a.shape[:axis] + (N_DEV, a.shape[axis] // N_DEV) + a.shape[axis + 1:]
    return jnp.moveaxis(a.reshape(shape), axis, 0)


def _from_owner_blocks(a, axis):
    a = jnp.moveaxis(a, 0, axis)
    return a.reshape(a.shape[:axis] + (a.shape[axis] * a.shape[axis + 1],) + a.shape[axis + 2:])


def _step(x, target, weights, moments_m, moments_v):
    seq, d = x.shape[1], x.shape[2]
    n_meta = weights["meta_tokens"].shape[0]
    tt = TOKEN_TILE
    pad_tiles = -(-n_meta // tt)
    p0 = pad_tiles * tt
    lp = p0 + seq
    first_pos = p0 - n_meta
    gc = d // 4 // S5_GROUP
    cw = d // 4

    big_names = [n for n in sharded_names() if n != "meta_tokens" and n.split("_", 1)[1] not in GATHER_F32]
    small_names = [n for n in sharded_names() if n not in big_names]
    layer_big = [[n for n in big_names if n.startswith("l%d_" % i)] for i in range(len(LAYER_KINDS))]
    layer_big[0] = small_names + layer_big[0]
    gather_started = []
    after = jnp.zeros((8, 128), F32)
    for i, names in enumerate(layer_big):
        gather_started.append(exchange_start([weights[n] if n in small_names else weights[n].astype(BF16) for n in names], True,
                                             after, "gather_start_l%d" % i))
        after = gather_started[-1][4]

    def vec(name):
        return weights[name].reshape(1, -1)

    s5_prep = {}
    for i, kind in enumerate(LAYER_KINDS):
        if kind == "s5":
            p = "l%d_" % i
            lr, li = weights[p + "lam_re"], weights[p + "lam_im"] + after[0, 0]
            ldt = weights[p + "log_dt"].reshape(-1, 1)
            br_t = jnp.transpose(weights[p + "b_re"], (2, 0, 1))
            bi_t = jnp.transpose(weights[p + "b_im"], (2, 0, 1))
            ar, ai, bbr, bbi = s5_disc_fwd(lr, li, ldt, br_t, bi_t, p + "disc_fwd")
            s5_prep[i] = dict(
                disc=(lr, li, ldt, br_t, bi_t), ar=ar.reshape(4, -1, 128), ai=ai.reshape(4, -1, 128),
                bdre=_block_diag_in(bbr, gc), bdim=_block_diag_in(bbi, gc),
                cdre=_block_diag_out(weights[p + "c_re"], gc), cdim=_block_diag_out(-weights[p + "c_im"], gc),
                d_skip=weights[p + "d_skip"].reshape(4, 1, cw), b_glu=vec(p + "b_glu"))
    h = jnp.concatenate([jnp.zeros((p0, d), F32), x[0] + after[0, 0]], axis=0)

    prepared = [h] + [s5_prep[i][k] for i in s5_prep for k in ("bdre", "bdim", "cdre", "cdim")]
    gathered = dict(zip(layer_big[0], exchange_wait(gather_started[0], True, prepared, "gather_wait_l0")))
    h = lax.dynamic_update_slice(h, _from_owner_blocks(gathered["meta_tokens"], 1), (first_pos, 0))

    full = {}

    def layer_weights(i, kind, after):
        p = "l%d_" % i
        if i > 0:
            gathered.update(zip(layer_big[i], exchange_wait(gather_started[i], True, after, "gather_wait_l%d" % i)))
        w_in = gathered[p + "w_in"]
        if kind == "s5":
            full[i] = dict(s5_prep[i], w_in=w_in, w_glu=gathered[p + "w_glu"].reshape(4, cw, d),
                           w_out=gathered[p + "w_out"].reshape(4, cw, d))
        elif kind == "conv":
            ce = w_in.shape[2]
            nch = 2
            conv_w = _from_owner_blocks(gathered[p + "conv_w"], 1)
            full[i] = dict(
                w_in=w_in, conv_w=jnp.transpose(conv_w.reshape(CONV_K, nch, ce), (1, 0, 2)),
                conv_b=weights[p + "conv_b"].reshape(nch, 1, ce), w_out=gathered[p + "w_out"].reshape(nch, ce, d))
        else:
            gw = w_in.shape[2]
            full[i] = dict(
                w_in=w_in, w_grp=_from_owner_blocks(gathered[p + "w_grp"], 1),
                b_grp=_from_owner_blocks(gathered[p + "b_grp"], 1).reshape(4, 1, gw),
                scale=weights[p + "scale"].reshape(4, 1, gw), w_out=gathered[p + "w_out"].reshape(4, gw, d))
        return full[i]

    saved = {}
    for i, kind in enumerate(LAYER_KINDS):
        p, f, g = "l%d_" % i, layer_weights(i, kind, h), vec("norm%d_g" % i)
        if kind == "s5":
            u, z, xs = s5_fwd1(h, g, f["w_in"], f["bdre"], f["bdim"], p + "fwd_in")
            s = s5_scan_fwd(xs, f["ar"], f["ai"], p + "scan_fwd")
            h_in = h
            h, y, q = s5_fwd3(s, u, z, h, f["cdre"], f["cdim"], f["w_glu"], f["w_out"], f["d_skip"], f["b_glu"], p + "fwd_out")
            saved[i] = (h_in, u, z, s, y, q)
        elif kind == "conv":
            h_new, halos = conv_fwd(h, g, f["w_in"], f["conv_w"], f["conv_b"], f["w_out"], p + "fwd")
            saved[i] = (h, halos)
            h = h_new
        else:
            h_new, halos = pool_fwd(h, g, f["w_in"], f["w_grp"], f["b_grp"], f["scale"], f["w_out"], first_pos, p + "fwd")
            saved[i] = (h, halos)
            h = h_new

    dh, dg_final, loss_tile = loss_head(h, target[0], vec("final_g"), pad_tiles, "loss_head")
    loss = lax.psum(loss_tile[0, 0], ("x", "y", "c"))

    grads = {"final_g": dg_final}
    names = weight_names()
    sh_names = sharded_names()
    rep_names = [n for n in names if n not in sh_names]

    def owner_blocks(a):
        return a.reshape(N_DEV, -1, a.shape[-1])

    def as2d(a):
        return a.reshape(-1, a.shape[-1])

    def pack(tree):
        flat = [jnp.pad(tree[n].reshape(-1), (0, -tree[n].size % 1024)) for n in rep_names]
        flat = jnp.concatenate(flat)
        return jnp.pad(flat, (0, -flat.size % (PACK_ROWS * 128))).reshape(-1, 128)

    layer_sharded, scatter_started = {}, {}
    ordered = jnp.zeros((), F32)
    for i in reversed(range(len(LAYER_KINDS))):
        kind = LAYER_KINDS[i]
        p, f, g = "l%d_" % i, full[i], vec("norm%d_g" % i) + ordered
        if kind == "s5":
            h_in, u, z, s, y, q = saved[i]
            dy, dp, dwo, dwg, dbg = s5_bwd3a(dh, y, q, z, f["w_glu"], f["w_out"], f["b_glu"] + ordered, p + "bwd_out")
            ds, dus, dcre, dcim, dd = s5_bwd3b(dy, s, u, f["cdre"], f["cdim"], f["d_skip"], p + "bwd_read")
            lam, dar, dai = s5_scan_bwd(ds, s, f["ar"], f["ai"], p + "scan_bwd")
            dp, dh, n, dbre, dbim, dg = s5_bwd1(lam, dus, u, dp, h_in, dh, g, f["w_in"], f["bdre"], f["bdim"], p + "bwd_in")
            dw_in = grad_w_in(n, dp, f["w_in"].shape[2], p + "grad_w_in")
            grads.update({p + "w_in": dw_in, p + "w_glu": dwg.reshape(N_DEV, -1, d), p + "w_out": dwo.reshape(N_DEV, -1, d),
                          p + "d_skip": dd, p + "b_glu": dbg})

            def replicated_grads(p=p, f=f, dar=dar, dai=dai, dbre=dbre, dbim=dbim, dcre=dcre, dcim=dcim, token=None):
                lr, li, ldt, br_t, bi_t = f["disc"]
                dlr, dli, dldt, dbr_t, dbi_t = s5_disc_bwd(
                    lr, li, ldt, br_t, bi_t, dar.reshape(lr.shape) + token, dai.reshape(lr.shape),
                    _block_diag_in_grad(dbre, gc), _block_diag_in_grad(dbim, gc), p + "disc_bwd")
                grads.update({
                    p + "lam_re": dlr, p + "lam_im": dli, p + "log_dt": dldt,
                    p + "b_re": jnp.transpose(dbr_t, (1, 2, 0)), p + "b_im": jnp.transpose(dbi_t, (1, 2, 0)),
                    p + "c_re": _block_diag_out_grad(dcre, gc), p + "c_im": -_block_diag_out_grad(dcim, gc)})
        elif kind == "conv":
            replicated_grads = None
            h_in, halos = saved[i]
            dh, n, dp, dwo, dcw, dcb, dg = conv_bwd(h_in, dh, halos, g, f["w_in"], f["conv_w"], f["conv_b"], f["w_out"], p + "bwd")
            dw_in = grad_w_in(n, dp, f["w_in"].shape[2], p + "grad_w_in")
            dconv_w = jnp.transpose(dcw[:, :CONV_K, :], (1, 0, 2)).reshape(CONV_K, -1)
            grads.update({p + "w_in": dw_in, p + "conv_w": _to_owner_blocks(dconv_w, 1), p + "conv_b": dcb,
                          p + "w_out": dwo.reshape(N_DEV, -1, d)})
        else:
            replicated_grads = None
            h_in, halos = saved[i]
            dh, n, dp, dwo, dwgrp, dbgrp, dsc, dg = pool_bwd(h_in, dh, halos, g, f["w_in"], f["w_grp"], f["b_grp"], f["scale"],
                                                             f["w_out"], first_pos, p + "bwd")
            dw_in = grad_w_in(n, dp, f["w_in"].shape[2], p + "grad_w_in")
            grads.update({p + "w_in": dw_in, p + "w_grp": _to_owner_blocks(dwgrp, 1),
                          p + "b_grp": _to_owner_blocks(dbgrp.reshape(4, -1), 1), p + "scale": dsc,
                          p + "w_out": dwo.reshape(N_DEV, -1, d)})
        grads["norm%d_g" % i] = dg
        layer_sharded[i] = ["l%d_%s" % (i, n) for n in SHARDED[kind]]
        scatter_started[i] = exchange_start([owner_blocks(grads[n]) for n in layer_sharded[i]], False, dh,
                                            "scatter_start_l%d" % i)
        ordered = scatter_started[i][4][0, 0]
        if replicated_grads is not None:
            replicated_grads(token=ordered)
    grad_x = dh[p0:][None]
    grads["meta_tokens"] = _to_owner_blocks(dh[first_pos:p0], 1)
    last = len(LAYER_KINDS)
    layer_sharded[last] = ["meta_tokens", "replicated"]
    scatter_started[last] = exchange_start([owner_blocks(grads["meta_tokens"]), pack(grads).reshape(N_DEV, -1, 128)], False,
                                           scatter_started[0][4], "scatter_start_replicated")

    out = {}
    received = {}
    after = scatter_started[last][4]
    for i in list(reversed(range(1, last))) + [last, 0]:
        received.update(zip(layer_sharded[i], exchange_wait(scatter_started[i], False, after, "scatter_wait_%d" % i)))
        for n in layer_sharded[i]:
            if n != "replicated":
                res = sum_adamw(received[n], as2d(weights[n]), as2d(moments_m[n]), as2d(moments_v[n]), "update_" + n)
                out[n] = [r.reshape(weights[n].shape) for r in res]
                after = out[n][0]
        if i == last:
            g_full = exchange([sum_parts(received["replicated"], "sum_replicated")], True, "gather_small_grads")[0]
            packed = sum_adamw(g_full.reshape(1, -1, 128), pack(weights), pack(moments_m), pack(moments_v), "update_replicated")
            offset = 0
            for n in rep_names:
                size = weights[n].size
                out[n] = [r.reshape(-1)[offset:offset + size].reshape(weights[n].shape) for r in packed]
                offset += size + (-size % 1024)
            after = [packed[0]] + [out[n][k] for n in rep_names for k in range(4)]

    return (loss, grad_x) + tuple(out[n][k] for k in range(4) for n in names)


def kernel(x, *rest):
    names = weight_names()
    nw = len(names)
    weights = dict(zip(names, rest[:nw]))
    target = rest[nw]
    moments_m = dict(zip(names, rest[nw + 1:2 * nw + 1]))
    moments_v = dict(zip(names, rest[2 * nw + 1:3 * nw + 1]))
    return _step(x, target, weights, moments_m, moments_v)
```

```python
import functools
import math

import jax
import jax.numpy as jnp
from jax import lax
from jax.experimental import pallas as pl
from jax.experimental.pallas import tpu as pltpu

F32 = jnp.float32
BF16 = jnp.bfloat16
EPS = 1e-6
N_DEV = 8
TOKEN_TILE = 256
SCAN_CHUNKS = 4
S5_GROUP = 16
S5_STATE = 64
POOL_WINDOWS = (2, 4, 8, 16)
POOL_HALO = 16
CONV_K = 3
CONV_HALO = 8
ADAM_LR = 0.001
ADAM_B1 = 0.9
ADAM_B2 = 0.999
ADAM_EPS = 1e-08
ADAM_WD = 0.01
ADAM_STEP = 10
GELU_C = math.sqrt(2.0 / math.pi)
GELU_A = 0.044715
UPDATE_TILE_ELEMS = 1 << 17
PACK_ROWS = 512
VMEM_LIMIT = 56 << 20

ANY = pl.BlockSpec(memory_space=pl.ANY)


def _params(vmem=VMEM_LIMIT, ndim=1):
    return pltpu.CompilerParams(vmem_limit_bytes=vmem, dimension_semantics=("arbitrary",) * ndim)


def _dot(a, b):
    return jnp.dot(a.astype(BF16), b.astype(BF16), preferred_element_type=F32)


def _dot_nt(a, b):
    return lax.dot_general(a.astype(BF16), b.astype(BF16), (((1,), (1,)), ((), ())), preferred_element_type=F32)


def _dot_tn(a, b):
    return lax.dot_general(a.astype(BF16), b.astype(BF16), (((0,), (0,)), ((), ())), preferred_element_type=F32)


def _rms_fwd(h, g):
    r = lax.rsqrt(jnp.mean(h * h, axis=-1, keepdims=True) + EPS)
    hh = h * r
    return hh * g, hh, r


def _rms_bwd(dn, hh, r, g):
    dhh = dn * g
    return r * (dhh - hh * jnp.mean(dhh * hh, axis=-1, keepdims=True))


def _sigmoid(x):
    return 1.0 / (1.0 + jnp.exp(-x))


def _silu_and_grad(z):
    s = _sigmoid(z)
    return z * s, s * (1.0 + z * (1.0 - s))


def _gelu(y):
    t = jnp.tanh(GELU_C * (y + GELU_A * y * y * y))
    return 0.5 * y * (1.0 + t), t


def _gelu_grad(y, t):
    return 0.5 * (1.0 + t) + 0.5 * y * (1.0 - t * t) * GELU_C * (1.0 + 3.0 * GELU_A * y * y)


def _rows(shape):
    return lax.broadcasted_iota(jnp.int32, shape, 0)


def _shift_down(x, k, halo):
    y = pltpu.roll(x, k, 0)
    rows = _rows(x.shape)
    for j in range(k):
        y = jnp.where(rows == j, halo[halo.shape[0] - k + j:halo.shape[0] - k + j + 1, :], y)
    return y


def _shift_up(x, k, halo):
    n = x.shape[0]
    y = pltpu.roll(x, n - k, 0)
    rows = _rows(x.shape)
    for j in range(k):
        y = jnp.where(rows == n - k + j, halo[j:j + 1, :], y)
    return y


def _window_sums_back(ext):
    out = []
    s = ext
    for k in (1, 2, 4, 8):
        s = s + pltpu.roll(s, k, 0)
        out.append(s)
    return out


def _window_sums_fwd(ext):
    n = ext.shape[0]
    out = []
    s = ext
    for k in (1, 2, 4, 8):
        s = s + pltpu.roll(s, n - k, 0)
        out.append(s)
    return out


def _pool_inv_count(tile, tt, first_pos, w, width):
    pos = _rows((tt, width)) + (tile * tt - first_pos + 1)
    return 1.0 / jnp.clip(pos, 1, w).astype(F32)


def _slab_spec(lp, tt, sw, index_map):
    nj = sw // 128
    return pl.BlockSpec((tt * nj, 128), index_map), (lp * 4 * nj, 128)


def _pack_pair(re, im):
    def rounded(v):
        return lax.bitcast_convert_type(v, jnp.int32) + 0x8000
    return lax.bitcast_convert_type((rounded(re) & -65536) | lax.shift_right_logical(rounded(im), 16), F32)


def _unpack_pair(w):
    b = lax.bitcast_convert_type(w, jnp.int32)
    return lax.bitcast_convert_type(b & -65536, F32), lax.bitcast_convert_type(lax.shift_left(b, 16), F32)


def _slab_load(ref):
    nj = ref.shape[0] // TOKEN_TILE
    return _unpack_pair(jnp.concatenate([ref[pl.ds(j, TOKEN_TILE, stride=nj), :] for j in range(nj)], axis=1))


def _slab_store(ref, re, im):
    nj = ref.shape[0] // TOKEN_TILE
    val = _pack_pair(re, im)
    for j in range(nj):
        ref[pl.ds(j, TOKEN_TILE, stride=nj), :] = val[:, j * 128:(j + 1) * 128]


def _s5_disc_math(lr, li, ldt, br, bi):
    dt = jnp.exp(ldt)
    mag = jnp.exp(lr * dt)
    ar = mag * jnp.cos(li * dt)
    ai = mag * jnp.sin(li * dt)
    den = lr * lr + li * li
    kr = ((ar - 1.0) * lr + ai * li) / den
    ki = (ai * lr - (ar - 1.0) * li) / den
    bbr = kr[None] * br - ki[None] * bi
    bbi = kr[None] * bi + ki[None] * br
    return ar, ai, bbr, bbi


def s5_disc_fwd(lr, li, ldt, br_t, bi_t, name):
    def body(lr_ref, li_ref, ldt_ref, br_ref, bi_ref, ar_ref, ai_ref, bbr_ref, bbi_ref):
        ar, ai, bbr, bbi = _s5_disc_math(lr_ref[...], li_ref[...], ldt_ref[...], br_ref[...], bi_ref[...])
        ar_ref[...] = ar
        ai_ref[...] = ai
        bbr_ref[...] = bbr
        bbi_ref[...] = bbi

    sd = jax.ShapeDtypeStruct
    return pl.pallas_call(
        body, name=name,
        out_shape=(sd(lr.shape, F32), sd(lr.shape, F32), sd(br_t.shape, F32), sd(br_t.shape, F32)),
    )(lr, li, ldt, br_t, bi_t)


def s5_disc_bwd(lr, li, ldt, br_t, bi_t, dar, dai, dbbr, dbbi, name):
    def body(lr_ref, li_ref, ldt_ref, br_ref, bi_ref, dar_ref, dai_ref, dbbr_ref, dbbi_ref,
             dlr_ref, dli_ref, dldt_ref, dbr_ref, dbi_ref):
        _, vjp = jax.vjp(_s5_disc_math, lr_ref[...], li_ref[...], ldt_ref[...], br_ref[...], bi_ref[...])
        dlr, dli, dldt, dbr, dbi = vjp((dar_ref[...], dai_ref[...], dbbr_ref[...], dbbi_ref[...]))
        dlr_ref[...] = dlr
        dli_ref[...] = dli
        dldt_ref[...] = dldt
        dbr_ref[...] = dbr
        dbi_ref[...] = dbi

    sd = jax.ShapeDtypeStruct
    return pl.pallas_call(
        body, name=name,
        out_shape=(sd(lr.shape, F32), sd(lr.shape, F32), sd(ldt.shape, F32), sd(br_t.shape, F32), sd(br_t.shape, F32)),
    )(lr, li, ldt, br_t, bi_t, dar, dai, dbbr, dbbi)


def s5_fwd1(h, g, w_in, bdre, bdim, name):
    lp, d = h.shape
    tt = TOKEN_TILE
    cw, sw = bdre.shape[1], bdre.shape[2]

    def body(h_ref, g_ref, w_hbm, bdre_hbm, bdim_hbm, u_ref, z_ref, x_ref, w, bre, bim, n_sc):
        i, c = pl.program_id(0), pl.program_id(1)

        @pl.when((i == 0) & (c == 0))
        def _():
            pltpu.sync_copy(w_hbm, w)
            pltpu.sync_copy(bdre_hbm, bre)
            pltpu.sync_copy(bdim_hbm, bim)

        @pl.when(c == 0)
        def _():
            n_sc[...] = _rms_fwd(h_ref[...], g_ref[...])[0].astype(BF16)

        n = n_sc[...]
        u = jnp.dot(n, w[c], preferred_element_type=F32)
        u_ref[...] = u
        z_ref[...] = jnp.dot(n, w[c + 4], preferred_element_type=F32)
        ub = u.astype(BF16)
        _slab_store(x_ref, jnp.dot(ub, bre[c], preferred_element_type=F32), jnp.dot(ub, bim[c], preferred_element_type=F32))

    sd = jax.ShapeDtypeStruct
    slab, slab_shape = _slab_spec(lp, tt, sw, lambda i, c: (i * 4 + c, 0))
    return pl.pallas_call(
        body, name=name, grid=(lp // tt, 4),
        in_specs=[pl.BlockSpec((tt, d), lambda i, c: (i, 0)), pl.BlockSpec((1, d), lambda i, c: (0, 0)), ANY, ANY, ANY],
        out_specs=[pl.BlockSpec((tt, cw), lambda i, c: (i, c)), pl.BlockSpec((tt, cw), lambda i, c: (i, c)), slab],
        out_shape=(sd((lp, d), F32), sd((lp, d), F32), sd(slab_shape, F32)),
        scratch_shapes=[pltpu.VMEM(w_in.shape, BF16), pltpu.VMEM(bdre.shape, BF16), pltpu.VMEM(bdim.shape, BF16),
                        pltpu.VMEM((tt, d), BF16)],
        compiler_params=_params(ndim=2),
    )(h, g, w_in, bdre, bdim)


def s5_scan_fwd(x, ar, ai, name):
    nj = ar.shape[1]
    tt = TOKEN_TILE
    cpb = SCAN_CHUNKS
    nt = x.shape[0] // (4 * tt * nj)

    def body(x_ref, ar_ref, ai_ref, s_ref, st_r, st_i):
        i, cg = pl.program_id(0), pl.program_id(1)

        @pl.when(i == 0)
        def _():
            for q in range(cpb):
                st_r[cg * cpb + q] = jnp.zeros((nj, 128), F32)
                st_i[cg * cpb + q] = jnp.zeros((nj, 128), F32)

        a_r = [ar_ref[cg * cpb + q] for q in range(cpb)]
        a_i = [ai_ref[cg * cpb + q] for q in range(cpb)]

        def step(t, carry):
            out = []
            for q in range(cpb):
                s_r, s_i = carry[q]
                rows = pl.ds(pl.multiple_of((q * tt + t) * nj, nj), nj)
                x_r, x_i = _unpack_pair(x_ref[rows, :])
                n_r = a_r[q] * s_r - a_i[q] * s_i + x_r
                n_i = a_r[q] * s_i + a_i[q] * s_r + x_i
                s_ref[rows, :] = _pack_pair(n_r, n_i)
                out.append((n_r, n_i))
            return tuple(out)

        init = tuple((st_r[cg * cpb + q], st_i[cg * cpb + q]) for q in range(cpb))
        final = lax.fori_loop(0, tt, step, init, unroll=8)
        for q in range(cpb):
            st_r[cg * cpb + q] = final[q][0]
            st_i[cg * cpb + q] = final[q][1]

    blk = pl.BlockSpec((cpb * tt * nj, 128), lambda i, cg: (i * (4 // cpb) + cg, 0))
    par = pl.BlockSpec((4, nj, 128), lambda i, cg: (0, 0, 0))
    sd = jax.ShapeDtypeStruct
    return pl.pallas_call(
        body, name=name, grid=(nt, 4 // cpb),
        in_specs=[blk, par, par], out_specs=blk,
        out_shape=sd(x.shape, F32),
        scratch_shapes=[pltpu.VMEM((4, nj, 128), F32), pltpu.VMEM((4, nj, 128), F32)],
        compiler_params=_params(ndim=2),
    )(x, ar, ai)


def s5_fwd3(s, u, z, h, cdre, cdim, w_glu, w_out, d_skip, b_glu, name):
    lp, d = h.shape
    tt = TOKEN_TILE
    sw, cw = cdre.shape[1], cdre.shape[2]

    def body(s_ref, u_ref, z_ref, h_ref, d_ref, bg_ref, cre_hbm, cim_hbm, wg_hbm, wo_hbm,
             o_ref, y_ref, q_ref, cre, cim, wg, wo, gy_sc):
        i, c = pl.program_id(0), pl.program_id(1)

        @pl.when((i == 0) & (c == 0))
        def _():
            pltpu.sync_copy(cre_hbm, cre)
            pltpu.sync_copy(cim_hbm, cim)
            pltpu.sync_copy(wg_hbm, wg)
            pltpu.sync_copy(wo_hbm, wo)

        s_r, s_i = _slab_load(s_ref)
        y = _dot(s_r, cre[c]) + _dot(s_i, cim[c]) + d_ref[c] * u_ref[...]
        y_ref[...] = y
        gy = _gelu(y)[0]
        gy_sc[c] = gy
        part = _dot(gy, wg[c])

        @pl.when(c == 0)
        def _():
            q_ref[...] = part

        @pl.when(c > 0)
        def _():
            q_ref[...] += part

        @pl.when(c == 3)
        def _():
            sig = _sigmoid(q_ref[...] + bg_ref[...])
            zz = z_ref[...]
            sz = zz * _sigmoid(zz)
            o = h_ref[...]
            for k in range(4):
                cols = slice(k * cw, (k + 1) * cw)
                o = o + _dot(gy_sc[k] * sig[:, cols] * sz[:, cols], wo[k])
            o_ref[...] = o

    row = lambda i, c: (i, 0)
    chunk = lambda i, c: (i, c)
    slab, _ = _slab_spec(lp, tt, sw, lambda i, c: (i * 4 + c, 0))
    sd = jax.ShapeDtypeStruct((lp, d), F32)
    return pl.pallas_call(
        body, name=name, grid=(lp // tt, 4),
        in_specs=[slab, pl.BlockSpec((tt, cw), chunk),
                  pl.BlockSpec((tt, d), row), pl.BlockSpec((tt, d), row),
                  pl.BlockSpec((4, 1, cw), lambda i, c: (0, 0, 0)), pl.BlockSpec((1, d), lambda i, c: (0, 0)),
                  ANY, ANY, ANY, ANY],
        out_specs=[pl.BlockSpec((tt, d), row), pl.BlockSpec((tt, cw), chunk), pl.BlockSpec((tt, d), row)],
        out_shape=(sd, sd, sd),
        scratch_shapes=[pltpu.VMEM(cdre.shape, BF16), pltpu.VMEM(cdim.shape, BF16), pltpu.VMEM(w_glu.shape, BF16),
                        pltpu.VMEM(w_out.shape, BF16), pltpu.VMEM((4, tt, cw), F32)],
        compiler_params=_params(ndim=2),
    )(s, u, z, h, d_skip, b_glu, cdre, cdim, w_glu, w_out)


def s5_bwd3a(dh, y, q, z, w_glu, w_out, b_glu, name):
    lp, d = dh.shape
    tt = TOKEN_TILE
    nt = lp // tt
    cw = w_glu.shape[1]

    def body(dh_ref, y_ref, q_ref, z_ref, bg_ref, wg_hbm, wo_hbm, dy_ref, dp_ref, dwo_hbm, dwg_hbm, dbg_hbm,
             wg, wo, dwo, dwg, dbg):
        i = pl.program_id(0)

        @pl.when(i == 0)
        def _():
            pltpu.sync_copy(wg_hbm, wg)
            pltpu.sync_copy(wo_hbm, wo)
            dwo[...] = jnp.zeros_like(dwo)
            dwg[...] = jnp.zeros_like(dwg)
            dbg[...] = jnp.zeros_like(dbg)

        sig = _sigmoid(q_ref[...] + bg_ref[...])
        sz, dsz = _silu_and_grad(z_ref[...])
        dhv = dh_ref[...]
        yv = y_ref[...]
        gy, t = _gelu(yv)
        dq_parts, dgy_parts = [], []
        for k in range(4):
            cols = slice(k * cw, (k + 1) * cw)
            gy_k, sig_k, sz_k = gy[:, cols], sig[:, cols], sz[:, cols]
            y2 = gy_k * sig_k
            dy3 = _dot_nt(dhv, wo[k])
            dwo[k] += _dot_tn(y2 * sz_k, dhv)
            dy2 = dy3 * sz_k
            dp_ref[0, :, cols] = (dy3 * y2 * dsz[:, cols]).astype(BF16)
            dq_parts.append(dy2 * gy_k * sig_k * (1.0 - sig_k))
            dgy_parts.append(dy2 * sig_k)
        dq = jnp.concatenate(dq_parts, axis=1)
        dbg[...] += jnp.sum(dq, axis=0, keepdims=True)
        dgelu = _gelu_grad(yv, t)
        for k in range(4):
            cols = slice(k * cw, (k + 1) * cw)
            dwg[k] += _dot_tn(gy[:, cols], dq)
            dy_ref[:, cols] = (dgy_parts[k] + _dot_nt(dq, wg[k])) * dgelu[:, cols]

        @pl.when(i == nt - 1)
        def _():
            pltpu.sync_copy(dwo, dwo_hbm)
            pltpu.sync_copy(dwg, dwg_hbm)
            pltpu.sync_copy(dbg, dbg_hbm)

    row = pl.BlockSpec((tt, d), lambda i: (i, 0))
    sd = jax.ShapeDtypeStruct
    return pl.pallas_call(
        body, name=name, grid=(nt,),
        in_specs=[row, row, row, row, pl.BlockSpec((1, d), lambda i: (0, 0)), ANY, ANY],
        out_specs=[row, pl.BlockSpec((1, tt, d), lambda i: (1, i, 0)), ANY, ANY, ANY],
        out_shape=(sd((lp, d), F32), sd((2, lp, d), BF16), sd(w_out.shape, F32), sd(w_glu.shape, F32), sd((1, d), F32)),
        scratch_shapes=[pltpu.VMEM(w_glu.shape, BF16), pltpu.VMEM(w_out.shape, BF16),
                        pltpu.VMEM(w_out.shape, F32), pltpu.VMEM(w_glu.shape, F32), pltpu.VMEM((1, d), F32)],
        compiler_params=_params(),
    )(dh, y, q, z, b_glu, w_glu, w_out)


def s5_bwd3b(dy, s, u, cdre, cdim, d_skip, name):
    lp, d = dy.shape
    tt = TOKEN_TILE
    nt = lp // tt
    sw, cw = cdre.shape[1], cdre.shape[2]
    gc = cw // S5_GROUP

    def body(dy_ref, s_ref, u_ref, d_ref, cre_hbm, cim_hbm,
             ds_ref, dus_ref, dcre_ref, dcim_ref, dd_hbm, cre, cim, dcre, dcim, dd):
        i, c = pl.program_id(0), pl.program_id(1)

        @pl.when((i == 0) & (c == 0))
        def _():
            pltpu.sync_copy(cre_hbm, cre)
            pltpu.sync_copy(cim_hbm, cim)
            dcre[...] = jnp.zeros_like(dcre)
            dcim[...] = jnp.zeros_like(dcim)
            dd[...] = jnp.zeros_like(dd)

        dyv = dy_ref[...]
        dd[c] += jnp.sum(dyv * u_ref[...], axis=0, keepdims=True)
        dus_ref[...] = dyv * d_ref[c]
        _slab_store(ds_ref, _dot_nt(dyv, cre[c]), _dot_nt(dyv, cim[c]))
        s_r, s_i = _slab_load(s_ref)
        dcre[c] += _dot_tn(s_r, dyv)
        dcim[c] += _dot_tn(s_i, dyv)

        @pl.when((i == nt - 1) & (c == 3))
        def _():
            for k in range(4):
                for j in range(gc):
                    rows, cols = pl.ds(j * S5_STATE, S5_STATE), pl.ds(j * S5_GROUP, S5_GROUP)
                    dcre_ref[k, j] = dcre[k, rows, cols].T
                    dcim_ref[k, j] = dcim[k, rows, cols].T
            pltpu.sync_copy(dd, dd_hbm)

    chunk = lambda i, c: (i, c)
    sd = jax.ShapeDtypeStruct
    slab, slab_shape = _slab_spec(lp, tt, sw, lambda i, c: (i * 4 + c, 0))
    diag = pl.BlockSpec((4, gc, S5_GROUP, S5_STATE), lambda i, c: (0, 0, 0, 0))
    return pl.pallas_call(
        body, name=name, grid=(nt, 4),
        in_specs=[pl.BlockSpec((tt, cw), chunk), slab,
                  pl.BlockSpec((tt, cw), chunk), pl.BlockSpec((4, 1, cw), lambda i, c: (0, 0, 0)), ANY, ANY],
        out_specs=[slab, pl.BlockSpec((tt, cw), chunk), diag, diag, ANY],
        out_shape=(sd(slab_shape, F32), sd((lp, d), F32),
                   sd((4, gc, S5_GROUP, S5_STATE), F32), sd((4, gc, S5_GROUP, S5_STATE), F32), sd((4, 1, cw), F32)),
        scratch_shapes=[pltpu.VMEM(cdre.shape, BF16), pltpu.VMEM(cdim.shape, BF16),
                        pltpu.VMEM(cdre.shape, F32), pltpu.VMEM(cdim.shape, F32), pltpu.VMEM((4, 1, cw), F32)],
        compiler_params=_params(ndim=2),
    )(dy, s, u, d_skip, cdre, cdim)


def s5_scan_bwd(g, s, ar, ai, name):
    nj = ar.shape[1]
    tt = TOKEN_TILE
    cpb = SCAN_CHUNKS
    nt = g.shape[0] // (4 * tt * nj)

    def body(g_ref, s_ref, ar_ref, ai_ref, lam_ref, dar_ref, dai_ref, st_r, st_i, acc_r, acc_i):
        i, cg = pl.program_id(0), pl.program_id(1)

        @pl.when((i == 0) & (cg == 0))
        def _():
            for ref in (st_r, st_i, acc_r, acc_i):
                ref[...] = jnp.zeros_like(ref)

        a_r = [ar_ref[cg * cpb + q] for q in range(cpb)]
        a_i = [ai_ref[cg * cpb + q] for q in range(cpb)]

        def slab(q, t):
            return pl.ds(pl.multiple_of((q * tt + t) * nj, nj), nj)

        def adjoint(q, t, l_r, l_i):
            rows = slab(q, t)
            g_r, g_i = _unpack_pair(g_ref[rows, :])
            n_r = g_r + a_r[q] * l_r + a_i[q] * l_i
            n_i = g_i + a_r[q] * l_i - a_i[q] * l_r
            lam_ref[rows, :] = _pack_pair(n_r, n_i)
            return n_r, n_i

        def pair(q, t, l_r, l_i, d_r, d_i):
            p_r, p_i = _unpack_pair(s_ref[slab(q, t), :])
            return d_r + l_r * p_r + l_i * p_i, d_i + l_i * p_r - l_r * p_i

        def step(k, carry):
            t = tt - 1 - k
            out = []
            for q in range(cpb):
                l_r, l_i, d_r, d_i = carry[q]
                l_r, l_i = adjoint(q, t, l_r, l_i)
                d_r, d_i = pair(q, t - 1, l_r, l_i, d_r, d_i)
                out.append((l_r, l_i, d_r, d_i))
            return tuple(out)

        init = []
        for q in range(cpb):
            ch = cg * cpb + q
            l_r, l_i = st_r[ch], st_i[ch]
            d_r, d_i = pair(q, tt - 1, l_r, l_i, acc_r[ch], acc_i[ch])
            init.append((l_r, l_i, d_r, d_i))
        final = lax.fori_loop(0, tt - 1, step, tuple(init), unroll=8)
        for q in range(cpb):
            ch = cg * cpb + q
            l_r, l_i, d_r, d_i = final[q]
            l_r, l_i = adjoint(q, 0, l_r, l_i)
            st_r[ch] = l_r
            st_i[ch] = l_i
            acc_r[ch] = d_r
            acc_i[ch] = d_i
            dar_ref[ch] = d_r
            dai_ref[ch] = d_i

    blk = pl.BlockSpec((cpb * tt * nj, 128), lambda i, cg: ((nt - 1 - i) * (4 // cpb) + cg, 0))
    par = pl.BlockSpec((4, nj, 128), lambda i, cg: (0, 0, 0))
    sd = jax.ShapeDtypeStruct
    return pl.pallas_call(
        body, name=name, grid=(nt, 4 // cpb),
        in_specs=[blk, blk, par, par], out_specs=[blk, par, par],
        out_shape=(sd(g.shape, F32), sd((4, nj, 128), F32), sd((4, nj, 128), F32)),
        scratch_shapes=[pltpu.VMEM((4, nj, 128), F32)] * 4,
        compiler_params=_params(ndim=2),
    )(g, s, ar, ai)


def s5_bwd1(lam, dus, u, dp, h, dh, g, w_in, bdre, bdim, name):
    lp, d = h.shape
    tt = TOKEN_TILE
    nt = lp // tt
    cw, sw = bdre.shape[1], bdre.shape[2]
    gc = cw // S5_GROUP

    def body(lam_ref, dus_ref, u_ref, dpz_ref, h_ref, dh_ref, g_ref, w_hbm, bre_hbm, bim_hbm,
             dpu_ref, dho_ref, n_ref, dbre_ref, dbim_ref, dg_hbm, w, bre, bim, dn_sc, dbre, dbim, dg):
        i, c = pl.program_id(0), pl.program_id(1)

        @pl.when((i == 0) & (c == 0))
        def _():
            pltpu.sync_copy(w_hbm, w)
            pltpu.sync_copy(bre_hbm, bre)
            pltpu.sync_copy(bim_hbm, bim)
            dbre[...] = jnp.zeros_like(dbre)
            dbim[...] = jnp.zeros_like(dbim)
            dg[...] = jnp.zeros_like(dg)

        (l_r, l_i), uv = _slab_load(lam_ref), u_ref[...]
        du = dus_ref[...] + _dot_nt(l_r, bre[c]) + _dot_nt(l_i, bim[c])
        dbre[c] += _dot_tn(uv, l_r)
        dbim[c] += _dot_tn(uv, l_i)
        dpu_ref[0] = du.astype(BF16)
        part = _dot_nt(du, w[c])

        @pl.when(c == 0)
        def _():
            dn_sc[...] = part

        @pl.when(c > 0)
        def _():
            dn_sc[...] += part

        @pl.when(c == 3)
        def _():
            dz = dpz_ref[0]
            dn = dn_sc[...]
            for k in range(4):
                dn = dn + _dot_nt(dz[:, k * cw:(k + 1) * cw], w[4 + k])
            gv = g_ref[...]
            n, hh, rr = _rms_fwd(h_ref[...], gv)
            n_ref[...] = n.T.astype(BF16)
            dg[...] += jnp.sum(dn * hh, axis=0, keepdims=True)
            dho_ref[...] = dh_ref[...] + _rms_bwd(dn, hh, rr, gv)

        @pl.when((i == nt - 1) & (c == 3))
        def _():
            for k in range(4):
                for j in range(gc):
                    rows, cols = pl.ds(j * S5_GROUP, S5_GROUP), pl.ds(j * S5_STATE, S5_STATE)
                    dbre_ref[k, j] = dbre[k, rows, cols]
                    dbim_ref[k, j] = dbim[k, rows, cols]
            pltpu.sync_copy(dg, dg_hbm)

    row = lambda i, c: (i, 0)
    chunk = lambda i, c: (i, c)
    sd = jax.ShapeDtypeStruct
    slab, _ = _slab_spec(lp, tt, sw, lambda i, c: (i * 4 + c, 0))
    diag = pl.BlockSpec((4, gc, S5_GROUP, S5_STATE), lambda i, c: (0, 0, 0, 0))
    return pl.pallas_call(
        body, name=name, grid=(nt, 4),
        in_specs=[slab, pl.BlockSpec((tt, cw), chunk),
                  pl.BlockSpec((tt, cw), chunk), pl.BlockSpec((1, tt, d), lambda i, c: (1, i, 0)),
                  pl.BlockSpec((tt, d), row), pl.BlockSpec((tt, d), row), pl.BlockSpec((1, d), lambda i, c: (0, 0)),
                  ANY, ANY, ANY],
        out_specs=[pl.BlockSpec((1, tt, cw), lambda i, c: (0, i, c)), pl.BlockSpec((tt, d), row),
                   pl.BlockSpec((d, tt), lambda i, c: (0, i)), diag, diag, ANY],
        out_shape=(sd(dp.shape, BF16), sd((lp, d), F32), sd((d, lp), BF16),
                   sd((4, gc, S5_GROUP, S5_STATE), F32), sd((4, gc, S5_GROUP, S5_STATE), F32), sd((1, d), F32)),
        input_output_aliases={3: 0},
        scratch_shapes=[pltpu.VMEM(w_in.shape, BF16), pltpu.VMEM(bdre.shape, BF16), pltpu.VMEM(bdim.shape, BF16),
                        pltpu.VMEM((tt, d), F32), pltpu.VMEM(bdre.shape, F32), pltpu.VMEM(bdim.shape, F32), pltpu.VMEM((1, d), F32)],
        compiler_params=_params(ndim=2),
    )(lam, dus, u, dp, h, dh, g, w_in, bdre, bdim)


def grad_w_in(n_t, dp, blk, name):
    d, lp = n_t.shape
    npart, _, width = dp.shape
    per = width // blk

    def body(n_ref, dp_ref, o_ref):
        o_ref[0] = jnp.dot(n_ref[...], dp_ref[0], preferred_element_type=F32)

    return pl.pallas_call(
        body, name=name, grid=(npart * per,),
        in_specs=[pl.BlockSpec((d, lp), lambda j: (0, 0), pipeline_mode=pl.Buffered(1)),
                  pl.BlockSpec((1, lp, blk), lambda j: (j // per, 0, j % per))],
        out_specs=pl.BlockSpec((1, d, blk), lambda j: (j, 0, 0)),
        out_shape=jax.ShapeDtypeStruct((npart * per, d, blk), F32),
        compiler_params=_params(),
    )(n_t, dp)


def _conv_fwd_chunk(n, w, cw_ref, cb_ref, halo, c, nch):
    bg = jnp.dot(n, w[c], preferred_element_type=F32)
    cg = jnp.dot(n, w[nch + c], preferred_element_type=F32)
    v = jnp.dot(n, w[2 * nch + c], preferred_element_type=F32)
    z = jnp.dot(n, w[3 * nch + c], preferred_element_type=F32)
    hc = cg * v
    taps = cw_ref[c]
    conv = taps[2:3, :] * hc + taps[1:2, :] * _shift_down(hc, 1, halo) + taps[0:1, :] * _shift_down(hc, 2, halo) + cb_ref[c]
    return bg, cg, v, z, hc, conv


def conv_fwd(h, g, w_in, conv_w, conv_b, w_out, name):
    lp, d = h.shape
    tt = TOKEN_TILE
    nt = lp // tt
    nch, ce = w_out.shape[0], w_out.shape[1]

    def body(h_ref, g_ref, cw_ref, cb_ref, w_hbm, wo_hbm, o_ref, halo_ref, w, wo, halo):
        i = pl.program_id(0)

        @pl.when(i == 0)
        def _():
            pltpu.sync_copy(w_hbm, w)
            pltpu.sync_copy(wo_hbm, wo)
            halo[...] = jnp.zeros_like(halo)

        hv = h_ref[...]
        n = _rms_fwd(hv, g_ref[...])[0].astype(BF16)
        o = hv
        for c in range(nch):
            bg, _, _, z, hc, conv = _conv_fwd_chunk(n, w, cw_ref, cb_ref, halo[c], c, nch)
            o = o + _dot(bg * conv * (z * _sigmoid(z)), wo[c])
            halo[c] = hc[tt - CONV_HALO:, :]
            halo_ref[0, c] = hc[tt - CONV_HALO:, :]
        o_ref[...] = o

    sd = jax.ShapeDtypeStruct
    return pl.pallas_call(
        body, name=name, grid=(nt,),
        in_specs=[pl.BlockSpec((tt, d), lambda i: (i, 0)), pl.BlockSpec((1, d), lambda i: (0, 0)),
                  pl.BlockSpec(conv_w.shape, lambda i: (0, 0, 0)), pl.BlockSpec(conv_b.shape, lambda i: (0, 0, 0)), ANY, ANY],
        out_specs=[pl.BlockSpec((tt, d), lambda i: (i, 0)), pl.BlockSpec((1, nch, CONV_HALO, ce), lambda i: (i, 0, 0, 0))],
        out_shape=(sd((lp, d), F32), sd((nt, nch, CONV_HALO, ce), F32)),
        scratch_shapes=[pltpu.VMEM(w_in.shape, BF16), pltpu.VMEM(w_out.shape, BF16), pltpu.VMEM((nch, CONV_HALO, ce), F32)],
        compiler_params=_params(),
    )(h, g, conv_w, conv_b, w_in, w_out)


def conv_bwd(h, dh, halos, g, w_in, conv_w, conv_b, w_out, name):
    lp, d = h.shape
    tt = TOKEN_TILE
    nt = lp // tt
    nch, ce = w_out.shape[0], w_out.shape[1]

    def body(h_ref, dh_ref, halo_ref, g_ref, cw_ref, cb_ref, w_hbm, wo_hbm,
             dho_ref, n_ref, dp_ref, dwo_hbm, dcw_hbm, dcb_hbm, dg_hbm, w, wo, nxt, dwo, dcw, dcb, dg):
        i = pl.program_id(0)

        @pl.when(i == 0)
        def _():
            pltpu.sync_copy(w_hbm, w)
            pltpu.sync_copy(wo_hbm, wo)
            for ref in (nxt, dwo, dcw, dcb, dg):
                ref[...] = jnp.zeros_like(ref)

        gv = g_ref[...]
        nf, hh, rr = _rms_fwd(h_ref[...], gv)
        n = nf.astype(BF16)
        n_ref[...] = nf.T.astype(BF16)
        dhv = dh_ref[...]
        has_prev = (i < nt - 1).astype(F32)
        dn = jnp.zeros((tt, d), F32)
        for c in range(nch):
            halo = halo_ref[0, c] * has_prev
            bg, cg, v, z, hc, conv = _conv_fwd_chunk(n, w, cw_ref, cb_ref, halo, c, nch)
            sz, dsz = _silu_and_grad(z)
            y1 = bg * conv
            dy2 = _dot_nt(dhv, wo[c])
            dwo[c] += _dot_tn(y1 * sz, dhv)
            dy1 = dy2 * sz
            dz = dy2 * y1 * dsz
            dbg = dy1 * conv
            dconv = dy1 * bg
            dcb[c] += jnp.sum(dconv, axis=0, keepdims=True)
            up1 = _shift_up(dconv, 1, nxt[c])
            up2 = _shift_up(dconv, 2, nxt[c])
            nxt[c] = dconv[:CONV_HALO, :]
            taps = cw_ref[c]
            dhc = taps[2:3, :] * dconv + taps[1:2, :] * up1 + taps[0:1, :] * up2
            dcw[c, 0:1, :] += jnp.sum(hc * up2, axis=0, keepdims=True)
            dcw[c, 1:2, :] += jnp.sum(hc * up1, axis=0, keepdims=True)
            dcw[c, 2:3, :] += jnp.sum(hc * dconv, axis=0, keepdims=True)
            dcg = dhc * v
            dv = dhc * cg
            cols = slice(c * ce, (c + 1) * ce)
            for p, val in enumerate((dbg, dcg, dv, dz)):
                dp_ref[p, :, cols] = val.astype(BF16)
                dn = dn + _dot_nt(val, w[p * nch + c])
        dg[...] += jnp.sum(dn * hh, axis=0, keepdims=True)
        dho_ref[...] = dhv + _rms_bwd(dn, hh, rr, gv)

        @pl.when(i == nt - 1)
        def _():
            pltpu.sync_copy(dwo, dwo_hbm)
            pltpu.sync_copy(dcw, dcw_hbm)
            pltpu.sync_copy(dcb, dcb_hbm)
            pltpu.sync_copy(dg, dg_hbm)

    rev = lambda i: (nt - 1 - i, 0)
    sd = jax.ShapeDtypeStruct
    return pl.pallas_call(
        body, name=name, grid=(nt,),
        in_specs=[pl.BlockSpec((tt, d), rev), pl.BlockSpec((tt, d), rev),
                  pl.BlockSpec((1, nch, CONV_HALO, ce), lambda i: (jnp.maximum(nt - 2 - i, 0), 0, 0, 0)),
                  pl.BlockSpec((1, d), lambda i: (0, 0)),
                  pl.BlockSpec(conv_w.shape, lambda i: (0, 0, 0)), pl.BlockSpec(conv_b.shape, lambda i: (0, 0, 0)), ANY, ANY],
        out_specs=[pl.BlockSpec((tt, d), rev), pl.BlockSpec((d, tt), lambda i: (0, nt - 1 - i)),
                   pl.BlockSpec((4, tt, nch * ce), lambda i: (0, nt - 1 - i, 0)), ANY, ANY, ANY, ANY],
        out_shape=(sd((lp, d), F32), sd((d, lp), BF16), sd((4, lp, nch * ce), BF16),
                   sd(w_out.shape, F32), sd((nch, 8, ce), F32), sd((nch, 1, ce), F32), sd((1, d), F32)),
        scratch_shapes=[pltpu.VMEM(w_in.shape, BF16), pltpu.VMEM(w_out.shape, BF16), pltpu.VMEM((nch, CONV_HALO, ce), F32),
                        pltpu.VMEM(w_out.shape, F32), pltpu.VMEM((nch, 8, ce), F32), pltpu.VMEM((nch, 1, ce), F32),
                        pltpu.VMEM((1, d), F32)],
        compiler_params=_params(),
    )(h, dh, halos, g, conv_w, conv_b, w_in, w_out)


def _pool_fwd_group(n, w, wg, bg_ref, sc_ref, halo, k, tile, tt, first_pos):
    u = jnp.dot(n, w[k], preferred_element_type=F32)
    z = jnp.dot(n, w[4 + k], preferred_element_type=F32)
    ext = jnp.concatenate([halo, u], axis=0)
    win = _window_sums_back(ext)[k][POOL_HALO:, :]
    mixed = win * _pool_inv_count(tile, tt, first_pos, POOL_WINDOWS[k], u.shape[1]) - u
    outs = _dot(mixed, wg[k]) + bg_ref[k]
    return u, z, mixed, outs, outs * sc_ref[k]


def pool_fwd(h, g, w_in, w_grp, b_grp, scale, w_out, first_pos, name):
    lp, d = h.shape
    tt = TOKEN_TILE
    nt = lp // tt
    gw = w_grp.shape[1]

    def body(h_ref, g_ref, bg_ref, sc_ref, w_hbm, wg_hbm, wo_hbm, o_ref, halo_ref, w, wg, wo, halo):
        i = pl.program_id(0)

        @pl.when(i == 0)
        def _():
            pltpu.sync_copy(w_hbm, w)
            pltpu.sync_copy(wg_hbm, wg)
            pltpu.sync_copy(wo_hbm, wo)
            halo[...] = jnp.zeros_like(halo)

        hv = h_ref[...]
        n = _rms_fwd(hv, g_ref[...])[0].astype(BF16)
        o = hv
        for k in range(4):
            u, z, _, _, yp = _pool_fwd_group(n, w, wg, bg_ref, sc_ref, halo[k], k, i, tt, first_pos)
            o = o + _dot(yp * (z * _sigmoid(z)), wo[k])
            halo[k] = u[tt - POOL_HALO:, :]
            halo_ref[0, k] = u[tt - POOL_HALO:, :]
        o_ref[...] = o

    sd = jax.ShapeDtypeStruct
    small = pl.BlockSpec((4, 1, gw), lambda i: (0, 0, 0))
    return pl.pallas_call(
        body, name=name, grid=(nt,),
        in_specs=[pl.BlockSpec((tt, d), lambda i: (i, 0)), pl.BlockSpec((1, d), lambda i: (0, 0)), small, small, ANY, ANY, ANY],
        out_specs=[pl.BlockSpec((tt, d), lambda i: (i, 0)), pl.BlockSpec((1, 4, POOL_HALO, gw), lambda i: (i, 0, 0, 0))],
        out_shape=(sd((lp, d), F32), sd((nt, 4, POOL_HALO, gw), F32)),
        scratch_shapes=[pltpu.VMEM(w_in.shape, BF16), pltpu.VMEM(w_grp.shape, BF16), pltpu.VMEM(w_out.shape, BF16),
                        pltpu.VMEM((4, POOL_HALO, gw), F32)],
        compiler_params=_params(),
    )(h, g, b_grp, scale, w_in, w_grp, w_out)


def pool_bwd(h, dh, halos, g, w_in, w_grp, b_grp, scale, w_out, first_pos, name):
    lp, d = h.shape
    tt = TOKEN_TILE
    nt = lp // tt
    gw = w_grp.shape[1]

    def body(h_ref, dh_ref, halo_ref, g_ref, bg_ref, sc_ref, w_hbm, wg_hbm, wo_hbm,
             dho_ref, n_ref, dp_ref, dwo_hbm, dwg_hbm, dbg_hbm, dsc_hbm, dg_hbm,
             w, wg, wo, nxt, dwo, dwg, dbg, dsc, dg):
        i = pl.program_id(0)
        tile = nt - 1 - i

        @pl.when(i == 0)
        def _():
            pltpu.sync_copy(w_hbm, w)
            pltpu.sync_copy(wg_hbm, wg)
            pltpu.sync_copy(wo_hbm, wo)
            for ref in (nxt, dwo, dwg, dbg, dsc, dg):
                ref[...] = jnp.zeros_like(ref)

        gv = g_ref[...]
        nf, hh, rr = _rms_fwd(h_ref[...], gv)
        n = nf.astype(BF16)
        n_ref[...] = nf.T.astype(BF16)
        dhv = dh_ref[...]
        has_prev = (i < nt - 1).astype(F32)
        dn = jnp.zeros((tt, d), F32)
        for k in range(4):
            u, z, mixed, outs, yp = _pool_fwd_group(n, w, wg, bg_ref, sc_ref, halo_ref[0, k] * has_prev, k, tile, tt, first_pos)
            sz, dsz = _silu_and_grad(z)
            dy = _dot_nt(dhv, wo[k])
            dwo[k] += _dot_tn(yp * sz, dhv)
            dyp = dy * sz
            dz = dy * yp * dsz
            dsc[k] += jnp.sum(dyp * outs, axis=0, keepdims=True)
            douts = dyp * sc_ref[k]
            dbg[k] += jnp.sum(douts, axis=0, keepdims=True)
            dwg[k] += _dot_tn(mixed, douts)
            dmixed = _dot_nt(douts, wg[k])
            dm = dmixed * _pool_inv_count(tile, tt, first_pos, POOL_WINDOWS[k], gw)
            ext = jnp.concatenate([dm, nxt[k]], axis=0)
            du = _window_sums_fwd(ext)[k][:tt, :] - dmixed
            nxt[k] = dm[:POOL_HALO, :]
            cols = slice(k * gw, (k + 1) * gw)
            dp_ref[0, :, cols] = du.astype(BF16)
            dp_ref[1, :, cols] = dz.astype(BF16)
            dn = dn + _dot_nt(du, w[k]) + _dot_nt(dz, w[4 + k])
        dg[...] += jnp.sum(dn * hh, axis=0, keepdims=True)
        dho_ref[...] = dhv + _rms_bwd(dn, hh, rr, gv)

        @pl.when(i == nt - 1)
        def _():
            pltpu.sync_copy(dwo, dwo_hbm)
            pltpu.sync_copy(dwg, dwg_hbm)
            pltpu.sync_copy(dbg, dbg_hbm)
            pltpu.sync_copy(dsc, dsc_hbm)
            pltpu.sync_copy(dg, dg_hbm)

    rev = lambda i: (nt - 1 - i, 0)
    sd = jax.ShapeDtypeStruct
    small = pl.BlockSpec((4, 1, gw), lambda i: (0, 0, 0))
    return pl.pallas_call(
        body, name=name, grid=(nt,),
        in_specs=[pl.BlockSpec((tt, d), rev), pl.BlockSpec((tt, d), rev),
                  pl.BlockSpec((1, 4, POOL_HALO, gw), lambda i: (jnp.maximum(nt - 2 - i, 0), 0, 0, 0)),
                  pl.BlockSpec((1, d), lambda i: (0, 0)), small, small, ANY, ANY, ANY],
        out_specs=[pl.BlockSpec((tt, d), rev), pl.BlockSpec((d, tt), lambda i: (0, nt - 1 - i)),
                   pl.BlockSpec((2, tt, 4 * gw), lambda i: (0, nt - 1 - i, 0)), ANY, ANY, ANY, ANY, ANY],
        out_shape=(sd((lp, d), F32), sd((d, lp), BF16), sd((2, lp, 4 * gw), BF16),
                   sd(w_out.shape, F32), sd(w_grp.shape, F32), sd((4, 1, gw), F32), sd((4, 1, gw), F32), sd((1, d), F32)),
        scratch_shapes=[pltpu.VMEM(w_in.shape, BF16), pltpu.VMEM(w_grp.shape, BF16), pltpu.VMEM(w_out.shape, BF16),
                        pltpu.VMEM((4, POOL_HALO, gw), F32), pltpu.VMEM(w_out.shape, F32), pltpu.VMEM(w_grp.shape, F32),
                        pltpu.VMEM((4, 1, gw), F32), pltpu.VMEM((4, 1, gw), F32), pltpu.VMEM((1, d), F32)],
        compiler_params=_params(),
    )(h, dh, halos, g, b_grp, scale, w_in, w_grp, w_out)


def loss_head(h, target, g, pad_tiles, name):
    lp, d = h.shape
    tt = TOKEN_TILE
    nt = lp // tt

    def body(h_ref, t_ref, g_ref, dh_ref, dg_ref, loss_ref, acc):
        i = pl.program_id(0)

        @pl.when(i == 0)
        def _():
            acc[...] = jnp.zeros_like(acc)
            dg_ref[...] = jnp.zeros_like(dg_ref)

        @pl.when(i < pad_tiles)
        def _():
            dh_ref[...] = jnp.zeros_like(dh_ref)

        @pl.when(i >= pad_tiles)
        def _():
            gv = g_ref[...]
            n, hh, rr = _rms_fwd(h_ref[...], gv)
            err = n - t_ref[...]
            acc[...] += 0.5 * jnp.sum(jnp.mean(err * err, axis=-1, keepdims=True), axis=0, keepdims=True)
            dn = err * (1.0 / d)
            dg_ref[...] += jnp.sum(dn * hh, axis=0, keepdims=True)
            dh_ref[...] = _rms_bwd(dn, hh, rr, gv)

        loss_ref[...] = jnp.broadcast_to(acc[...], loss_ref.shape)

    sd = jax.ShapeDtypeStruct
    return pl.pallas_call(
        body, name=name, grid=(nt,),
        in_specs=[pl.BlockSpec((tt, d), lambda i: (i, 0)), pl.BlockSpec((tt, d), lambda i: (jnp.maximum(i - pad_tiles, 0), 0)),
                  pl.BlockSpec((1, d), lambda i: (0, 0))],
        out_specs=[pl.BlockSpec((tt, d), lambda i: (i, 0)), pl.BlockSpec((1, d), lambda i: (0, 0)),
                   pl.BlockSpec((8, 128), lambda i: (0, 0))],
        out_shape=(sd((lp, d), F32), sd((1, d), F32), sd((8, 128), F32)),
        scratch_shapes=[pltpu.VMEM((1, 1), F32)],
        compiler_params=_params(),
    )(h, target, g)


def exchange(arrs, gather, name):
    n = len(arrs)

    def body(*refs):
        ins, outs = refs[:n], refs[n:2 * n]
        send_sems, recv_sems, own_sems = refs[2 * n:]
        x, y, c = lax.axis_index("x"), lax.axis_index("y"), lax.axis_index("c")
        me = 4 * x + 2 * y + c
        own = []
        for a in range(n):
            cp = pltpu.make_async_copy(ins[a] if gather else ins[a].at[me], outs[a].at[me], own_sems.at[a])
            cp.start()
            own.append(cp)
        sent = []
        for k in range(1, N_DEV):
            px = 1 - x if k & 4 else x
            py = 1 - y if k & 2 else y
            pc = 1 - c if k & 1 else c
            peer = 4 * px + 2 * py + pc
            for a in range(n):
                cp = pltpu.make_async_remote_copy(
                    src_ref=ins[a] if gather else ins[a].at[peer], dst_ref=outs[a].at[me],
                    send_sem=send_sems.at[a, k - 1], recv_sem=recv_sems.at[a, k - 1],
                    device_id=(px, py, pc), device_id_type=pl.DeviceIdType.MESH)
                cp.start()
                sent.append((cp, a, k, peer, (px, py, pc)))
        for cp, a, k, peer, pid in sent:
            cp.wait_send()
            pltpu.make_async_remote_copy(
                src_ref=ins[a] if gather else ins[a].at[peer], dst_ref=outs[a].at[peer],
                send_sem=send_sems.at[a, k - 1], recv_sem=recv_sems.at[a, k - 1],
                device_id=pid, device_id_type=pl.DeviceIdType.MESH).wait_recv()
        for cp in own:
            cp.wait()

    hbm = pl.BlockSpec(memory_space=pltpu.HBM)
    out_shape = tuple(jax.ShapeDtypeStruct(((N_DEV,) + a.shape) if gather else a.shape, a.dtype) for a in arrs)
    return pl.pallas_call(
        body, name=name, in_specs=[hbm] * n, out_specs=[hbm] * n, out_shape=out_shape,
        scratch_shapes=[pltpu.SemaphoreType.DMA((n, N_DEV - 1)), pltpu.SemaphoreType.DMA((n, N_DEV - 1)),
                        pltpu.SemaphoreType.DMA((n,))],
    )(*[pltpu.with_memory_space_constraint(a, pltpu.HBM) for a in arrs])


def _peers(x, y, c):
    out = []
    for k in range(1, N_DEV):
        px = 1 - x if k & 4 else x
        py = 1 - y if k & 2 else y
        pc = 1 - c if k & 1 else c
        out.append((k, (px, py, pc), 4 * px + 2 * py + pc))
    return out


def exchange_start(arrs, gather, after, name):
    n = len(arrs)
    me = 4 * lax.axis_index("x") + 2 * lax.axis_index("y") + lax.axis_index("c")
    lands = []
    for a in arrs:
        own = a[None] if gather else lax.dynamic_index_in_dim(a, me, 0, keepdims=True)
        lands.append(lax.dynamic_update_index_in_dim(lax.empty(((N_DEV,) + a.shape) if gather else a.shape, a.dtype), own, me, 0))

    def body(*refs):
        ins, land = refs[:n], refs[n:2 * n]
        send_sems, recv_sems, token = refs[2 * n + 1], refs[2 * n + 2], refs[4 * n + 3]
        x, y, c = lax.axis_index("x"), lax.axis_index("y"), lax.axis_index("c")
        me = 4 * x + 2 * y + c
        for k, pid, peer in _peers(x, y, c):
            for a in range(n):
                pltpu.make_async_remote_copy(
                    src_ref=ins[a] if gather else ins[a].at[peer], dst_ref=land[a].at[me],
                    send_sem=send_sems.at[a * (N_DEV - 1) + k - 1], recv_sem=recv_sems.at[a * (N_DEV - 1) + k - 1],
                    device_id=pid, device_id_type=pl.DeviceIdType.MESH).start()
        token[...] = jnp.zeros_like(token)

    hbm = pl.BlockSpec(memory_space=pltpu.HBM)
    sem = pl.BlockSpec(memory_space=pltpu.SEMAPHORE)
    sems = pltpu.SemaphoreType.DMA((n * (N_DEV - 1),))
    res = pl.pallas_call(
        body, name=name, in_specs=[hbm] * (2 * n) + [ANY],
        out_specs=[sem, sem] + [hbm] * (2 * n) + [pl.BlockSpec(memory_space=pltpu.VMEM)],
        out_shape=[sems, sems] + [pltpu.HBM(a.shape, a.dtype) for a in arrs] + [pltpu.HBM(l.shape, l.dtype) for l in lands]
        + [jax.ShapeDtypeStruct((8, 128), F32)],
        input_output_aliases={a: 2 + a for a in range(2 * n)},
        compiler_params=pltpu.CompilerParams(has_side_effects=pltpu.SideEffectType.DATAFLOW_SIDE_EFFECTING),
    )(*[pltpu.with_memory_space_constraint(a, pltpu.HBM) for a in list(arrs) + lands], after)
    return res[0], res[1], res[2:2 + n], res[2 + n:2 + 2 * n], res[-1]


def exchange_wait(started, gather, after, name):
    send_sems, recv_sems, srcs, lands, _ = started
    n = len(srcs)
    after = list(after) if isinstance(after, (list, tuple)) else [after]

    def body(*refs):
        ins, land = refs[:n], refs[n:2 * n]
        send_sems, recv_sems = refs[2 * n], refs[2 * n + 1]
        x, y, c = lax.axis_index("x"), lax.axis_index("y"), lax.axis_index("c")
        for k, pid, peer in _peers(x, y, c):
            for a in range(n):
                cp = pltpu.make_async_remote_copy(
                    src_ref=ins[a] if gather else ins[a].at[peer], dst_ref=land[a].at[peer],
                    send_sem=send_sems.at[a * (N_DEV - 1) + k - 1], recv_sem=recv_sems.at[a * (N_DEV - 1) + k - 1],
                    device_id=pid, device_id_type=pl.DeviceIdType.MESH)
                cp.wait_send()
                cp.wait_recv()

    hbm = pl.BlockSpec(memory_space=pltpu.HBM)
    sem = pl.BlockSpec(memory_space=pltpu.SEMAPHORE)
    res = pl.pallas_call(
        body, name=name, in_specs=[hbm] * (2 * n) + [sem, sem] + [ANY] * len(after),
        out_specs=[hbm] * (2 * n),
        out_shape=[pltpu.HBM(a.shape, a.dtype) for a in list(srcs) + list(lands)],
        input_output_aliases={a: a for a in range(2 * n)},
        compiler_params=pltpu.CompilerParams(has_side_effects=pltpu.SideEffectType.DATAFLOW_SIDE_EFFECTING),
    )(*srcs, *lands, send_sems, recv_sems, *after)
    return res[n:]


def _adamw(w, g, m, v):
    m = ADAM_B1 * m + (1.0 - ADAM_B1) * g
    v = ADAM_B2 * v + (1.0 - ADAM_B2) * (g * g)
    m_hat = m / (1.0 - ADAM_B1 ** ADAM_STEP)
    v_hat = v / (1.0 - ADAM_B2 ** ADAM_STEP)
    return -ADAM_LR * (m_hat / (jnp.sqrt(v_hat) + ADAM_EPS) + ADAM_WD * w), m, v


def _update_tile_rows(rows, cols):
    if rows * cols <= UPDATE_TILE_ELEMS:
        return rows
    return max(t for t in range(8, UPDATE_TILE_ELEMS // cols + 1, 8) if rows % t == 0)


def _sum_in_order(p_ref):
    g = p_ref[0]
    for j in range(1, p_ref.shape[0]):
        g = g + p_ref[j]
    return g


def sum_parts(parts, name):
    nparts, rows, cols = parts.shape
    tr = _update_tile_rows(rows, cols)

    def body(p_ref, g_ref):
        g_ref[...] = _sum_in_order(p_ref)

    return pl.pallas_call(
        body, name=name, grid=(rows // tr,),
        in_specs=[pl.BlockSpec((nparts, tr, cols), lambda i: (0, i, 0))],
        out_specs=pl.BlockSpec((tr, cols), lambda i: (i, 0)), out_shape=jax.ShapeDtypeStruct((rows, cols), F32),
        compiler_params=_params(),
    )(parts)


def sum_adamw(parts, w, m, v, name):
    rows, cols = w.shape
    nparts = parts.shape[0]
    tr = _update_tile_rows(rows, cols)

    def body(p_ref, w_ref, m_ref, v_ref, g_ref, d_ref, nm_ref, nv_ref):
        g = _sum_in_order(p_ref)
        delta, nm, nv = _adamw(w_ref[...], g, m_ref[...], v_ref[...])
        g_ref[...] = g
        d_ref[...] = delta
        nm_ref[...] = nm
        nv_ref[...] = nv

    blk = pl.BlockSpec((tr, cols), lambda i: (i, 0))
    sd = jax.ShapeDtypeStruct((rows, cols), F32)
    return pl.pallas_call(
        body, name=name, grid=(rows // tr,),
        in_specs=[pl.BlockSpec((nparts, tr, cols), lambda i: (0, i, 0)), blk, blk, blk],
        out_specs=[blk] * 4, out_shape=(sd,) * 4,
        compiler_params=_params(),
    )(parts, w, m, v)


S5_NAMES = ("w_in", "lam_re", "lam_im", "log_dt", "b_re", "b_im", "c_re", "c_im", "d_skip", "w_glu", "b_glu", "w_out")
CONV_NAMES = ("w_in", "conv_w", "conv_b", "w_out")
POOL_NAMES = ("w_in", "w_grp", "b_grp", "scale", "w_out")
LAYER_KINDS = ("s5", "conv", "pool", "s5")
LAYER_NAMES = {"s5": S5_NAMES, "conv": CONV_NAMES, "pool": POOL_NAMES}
SHARDED = {"s5": ("w_in", "w_glu", "w_out"), "conv": ("w_in", "conv_w", "w_out"), "pool": ("w_in", "w_grp", "b_grp", "w_out")}
GATHER_F32 = ("conv_w", "b_grp")


def weight_names():
    names = ["meta_tokens"]
    for i, kind in enumerate(LAYER_KINDS):
        names.append("norm%d_g" % i)
        names += ["l%d_%s" % (i, n) for n in LAYER_NAMES[kind]]
    names.append("final_g")
    return names


def sharded_names():
    return ["meta_tokens"] + ["l%d_%s" % (i, n) for i, kind in enumerate(LAYER_KINDS) for n in SHARDED[kind]]


def _block_diag_in(bb_t, gc):
    i, g, p = bb_t.shape
    t = bb_t.astype(BF16).reshape(i, 4, gc, p)
    return jnp.einsum("icjp,jk->cjikp", t, jnp.eye(gc, dtype=BF16)).reshape(4, gc * i, gc * p)


def _block_diag_in_grad(blocks):
    _, gc, i, p = blocks.shape
    return jnp.transpose(blocks, (2, 0, 1, 3)).reshape(i, 4 * gc, p)


def _block_diag_out(cc, gc):
    g, i, p = cc.shape
    return jnp.einsum("cjip,jk->cjpki", cc.astype(BF16).reshape(4, gc, i, p), jnp.eye(gc, dtype=BF16)).reshape(4, gc * p, gc * i)


def _block_diag_out_grad(blocks):
    _, gc, i, p = blocks.shape
    return blocks.reshape(4 * gc, i, p)


def _to_owner_blocks(a, axis):
    shape = a.shape[:axis] + (N_DEV, a.shape[axis] // N_DEV) + a.shape[axis + 1:]
    return jnp.moveaxis(a.reshape(shape), axis, 0)


def _from_owner_blocks(a, axis):
    a = jnp.moveaxis(a, 0, axis)
    return a.reshape(a.shape[:axis] + (a.shape[axis] * a.shape[axis + 1],) + a.shape[axis + 2:])


def _step(x, target, weights, moments_m, moments_v):
    seq, d = x.shape[1], x.shape[2]
    n_meta = weights["meta_tokens"].shape[0]
    tt = TOKEN_TILE
    pad_tiles = -(-n_meta // tt)
    p0 = pad_tiles * tt
    lp = p0 + seq
    first_pos = p0 - n_meta
    gc = d // 4 // S5_GROUP
    cw = d // 4

    big_names = [n for n in sharded_names() if n != "meta_tokens" and n.split("_", 1)[1] not in GATHER_F32]
    small_names = [n for n in sharded_names() if n not in big_names]
    layer_big = [[n for n in big_names if n.startswith("l%d_" % i)] for i in range(len(LAYER_KINDS))]
    layer_big[0] = small_names + layer_big[0]
    gather_started = []
    after = jnp.zeros((8, 128), F32)
    for i, names in enumerate(layer_big):
        gather_started.append(exchange_start([weights[n] if n in small_names else weights[n].astype(BF16) for n in names], True,
                                             after, "gather_start_l%d" % i))
        after = gather_started[-1][4]

    def vec(name):
        return weights[name].reshape(1, -1)

    s5_prep = {}
    for i, kind in enumerate(LAYER_KINDS):
        if kind == "s5":
            p = "l%d_" % i
            lr, li = weights[p + "lam_re"], weights[p + "lam_im"] + after[0, 0]
            ldt = weights[p + "log_dt"].reshape(-1, 1)
            br_t = jnp.transpose(weights[p + "b_re"], (2, 0, 1))
            bi_t = jnp.transpose(weights[p + "b_im"], (2, 0, 1))
            ar, ai, bbr, bbi = s5_disc_fwd(lr, li, ldt, br_t, bi_t, p + "disc_fwd")
            s5_prep[i] = dict(
                disc=(lr, li, ldt, br_t, bi_t), ar=ar.reshape(4, -1, 128), ai=ai.reshape(4, -1, 128),
                bdre=_block_diag_in(bbr, gc), bdim=_block_diag_in(bbi, gc),
                cdre=_block_diag_out(weights[p + "c_re"], gc), cdim=_block_diag_out(-weights[p + "c_im"], gc),
                d_skip=weights[p + "d_skip"].reshape(4, 1, cw), b_glu=vec(p + "b_glu"))
    h = jnp.concatenate([jnp.zeros((p0, d), F32), x[0] + after[0, 0]], axis=0)

    prepared = [h] + [s5_prep[i][k] for i in s5_prep for k in ("bdre", "bdim", "cdre", "cdim")]
    gathered = dict(zip(layer_big[0], exchange_wait(gather_started[0], True, prepared, "gather_wait_l0")))
    h = lax.dynamic_update_slice(h, _from_owner_blocks(gathered["meta_tokens"], 1), (first_pos, 0))

    full = {}

    def layer_weights(i, kind, after):
        p = "l%d_" % i
        if i > 0:
            gathered.update(zip(layer_big[i], exchange_wait(gather_started[i], True, after, "gather_wait_l%d" % i)))
        w_in = gathered[p + "w_in"]
        if kind == "s5":
            full[i] = dict(s5_prep[i], w_in=w_in, w_glu=gathered[p + "w_glu"].reshape(4, cw, d),
                           w_out=gathered[p + "w_out"].reshape(4, cw, d))
        elif kind == "conv":
            ce = w_in.shape[2]
            nch = 2
            conv_w = _from_owner_blocks(gathered[p + "conv_w"], 1)
            full[i] = dict(
                w_in=w_in, conv_w=jnp.transpose(conv_w.reshape(CONV_K, nch, ce), (1, 0, 2)),
                conv_b=weights[p + "conv_b"].reshape(nch, 1, ce), w_out=gathered[p + "w_out"].reshape(nch, ce, d))
        else:
            gw = w_in.shape[2]
            full[i] = dict(
                w_in=w_in, w_grp=_from_owner_blocks(gathered[p + "w_grp"], 1),
                b_grp=_from_owner_blocks(gathered[p + "b_grp"], 1).reshape(4, 1, gw),
                scale=weights[p + "scale"].reshape(4, 1, gw), w_out=gathered[p + "w_out"].reshape(4, gw, d))
        return full[i]

    saved = {}
    for i, kind in enumerate(LAYER_KINDS):
        p, f, g = "l%d_" % i, layer_weights(i, kind, h), vec("norm%d_g" % i)
        if kind == "s5":
            u, z, xs = s5_fwd1(h, g, f["w_in"], f["bdre"], f["bdim"], p + "fwd_in")
            s = s5_scan_fwd(xs, f["ar"], f["ai"], p + "scan_fwd")
            h_in = h
            h, y, q = s5_fwd3(s, u, z, h, f["cdre"], f["cdim"], f["w_glu"], f["w_out"], f["d_skip"], f["b_glu"], p + "fwd_out")
            saved[i] = (h_in, u, z, s, y, q)
        elif kind == "conv":
            h_new, halos = conv_fwd(h, g, f["w_in"], f["conv_w"], f["conv_b"], f["w_out"], p + "fwd")
            saved[i] = (h, halos)
            h = h_new
        else:
            h_new, halos = pool_fwd(h, g, f["w_in"], f["w_grp"], f["b_grp"], f["scale"], f["w_out"], first_pos, p + "fwd")
            saved[i] = (h, halos)
            h = h_new

    dh, dg_final, loss_tile = loss_head(h, target[0], vec("final_g"), pad_tiles, "loss_head")
    loss = lax.psum(loss_tile[0, 0], ("x", "y", "c"))

    grads = {"final_g": dg_final}
    names = weight_names()
    sh_names = sharded_names()
    rep_names = [n for n in names if n not in sh_names]

    def owner_blocks(a):
        return a.reshape(N_DEV, -1, a.shape[-1])

    def as2d(a):
        return a.reshape(-1, a.shape[-1])

    def pack(tree):
        flat = [jnp.pad(tree[n].reshape(-1), (0, -tree[n].size % 1024)) for n in rep_names]
        flat = jnp.concatenate(flat)
        return jnp.pad(flat, (0, -flat.size % (PACK_ROWS * 128))).reshape(-1, 128)

    layer_sharded, scatter_started = {}, {}
    ordered = jnp.zeros((), F32)
    for i in reversed(range(len(LAYER_KINDS))):
        kind = LAYER_KINDS[i]
        p, f, g = "l%d_" % i, full[i], vec("norm%d_g" % i) + ordered
        if kind == "s5":
            h_in, u, z, s, y, q = saved[i]
            dy, dp, dwo, dwg, dbg = s5_bwd3a(dh, y, q, z, f["w_glu"], f["w_out"], f["b_glu"] + ordered, p + "bwd_out")
            ds, dus, dcre, dcim, dd = s5_bwd3b(dy, s, u, f["cdre"], f["cdim"], f["d_skip"], p + "bwd_read")
            lam, dar, dai = s5_scan_bwd(ds, s, f["ar"], f["ai"], p + "scan_bwd")
            dp, dh, n, dbre, dbim, dg = s5_bwd1(lam, dus, u, dp, h_in, dh, g, f["w_in"], f["bdre"], f["bdim"], p + "bwd_in")
            dw_in = grad_w_in(n, dp, f["w_in"].shape[2], p + "grad_w_in")
            grads.update({p + "w_in": dw_in, p + "w_glu": dwg.reshape(N_DEV, -1, d), p + "w_out": dwo.reshape(N_DEV, -1, d),
                          p + "d_skip": dd, p + "b_glu": dbg})

            def replicated_grads(p=p, f=f, dar=dar, dai=dai, dbre=dbre, dbim=dbim, dcre=dcre, dcim=dcim, token=None):
                lr, li, ldt, br_t, bi_t = f["disc"]
                dlr, dli, dldt, dbr_t, dbi_t = s5_disc_bwd(
                    lr, li, ldt, br_t, bi_t, dar.reshape(lr.shape) + token, dai.reshape(lr.shape),
                    _block_diag_in_grad(dbre), _block_diag_in_grad(dbim), p + "disc_bwd")
                grads.update({
                    p + "lam_re": dlr, p + "lam_im": dli, p + "log_dt": dldt,
                    p + "b_re": jnp.transpose(dbr_t, (1, 2, 0)), p + "b_im": jnp.transpose(dbi_t, (1, 2, 0)),
                    p + "c_re": _block_diag_out_grad(dcre), p + "c_im": -_block_diag_out_grad(dcim)})
        elif kind == "conv":
            replicated_grads = None
            h_in, halos = saved[i]
            dh, n, dp, dwo, dcw, dcb, dg = conv_bwd(h_in, dh, halos, g, f["w_in"], f["conv_w"], f["conv_b"], f["w_out"], p + "bwd")
            dw_in = grad_w_in(n, dp, f["w_in"].shape[2], p + "grad_w_in")
            dconv_w = jnp.transpose(dcw[:, :CONV_K, :], (1, 0, 2)).reshape(CONV_K, -1)
            grads.update({p + "w_in": dw_in, p + "conv_w": _to_owner_blocks(dconv_w, 1), p + "conv_b": dcb,
                          p + "w_out": dwo.reshape(N_DEV, -1, d)})
        else:
            replicated_grads = None
            h_in, halos = saved[i]
            dh, n, dp, dwo, dwgrp, dbgrp, dsc, dg = pool_bwd(h_in, dh, halos, g, f["w_in"], f["w_grp"], f["b_grp"], f["scale"],
                                                             f["w_out"], first_pos, p + "bwd")
            dw_in = grad_w_in(n, dp, f["w_in"].shape[2], p + "grad_w_in")
            grads.update({p + "w_in": dw_in, p + "w_grp": _to_owner_blocks(dwgrp, 1),
                          p + "b_grp": _to_owner_blocks(dbgrp.reshape(4, -1), 1), p + "scale": dsc,
                          p + "w_out": dwo.reshape(N_DEV, -1, d)})
        grads["norm%d_g" % i] = dg
        layer_sharded[i] = ["l%d_%s" % (i, n) for n in SHARDED[kind]]
        scatter_started[i] = exchange_start([owner_blocks(grads[n]) for n in layer_sharded[i]], False, dh,
                                            "scatter_start_l%d" % i)
        ordered = scatter_started[i][4][0, 0]
        if replicated_grads is not None:
            replicated_grads(token=ordered)
    grad_x = dh[p0:][None]
    grads["meta_tokens"] = _to_owner_blocks(dh[first_pos:p0], 1)
    last = len(LAYER_KINDS)
    layer_sharded[last] = ["meta_tokens", "replicated"]
    scatter_started[last] = exchange_start([owner_blocks(grads["meta_tokens"]), pack(grads).reshape(N_DEV, -1, 128)], False,
                                           scatter_started[0][4], "scatter_start_replicated")

    out = {}
    received = {}
    after = scatter_started[last][4]
    for i in list(reversed(range(1, last))) + [last, 0]:
        received.update(zip(layer_sharded[i], exchange_wait(scatter_started[i], False, after, "scatter_wait_%d" % i)))
        for n in layer_sharded[i]:
            if n != "replicated":
                res = sum_adamw(received[n], as2d(weights[n]), as2d(moments_m[n]), as2d(moments_v[n]), "update_" + n)
                out[n] = [r.reshape(weights[n].shape) for r in res]
                after = out[n][0]
        if i == last:
            g_full = exchange([sum_parts(received["replicated"], "sum_replicated")], True, "gather_small_grads")[0]
            packed = sum_adamw(g_full.reshape(1, -1, 128), pack(weights), pack(moments_m), pack(moments_v), "update_replicated")
            offset = 0
            for n in rep_names:
                size = weights[n].size
                out[n] = [r.reshape(-1)[offset:offset + size].reshape(weights[n].shape) for r in packed]
                offset += size + (-size % 1024)
            after = [packed[0]] + [out[n][k] for n in rep_names for k in range(4)]

    return (loss, grad_x) + tuple(out[n][k] for k in range(4) for n in names)


def kernel(x, *rest):
    names = weight_names()
    nw = len(names)
    weights = dict(zip(names, rest[:nw]))
    target = rest[nw]
    moments_m = dict(zip(names, rest[nw + 1:2 * nw + 1]))
    moments_v = dict(zip(names, rest[2 * nw + 1:3 * nw + 1]))
    return _step(x, target, weights, moments_m, moments_v)
```

```python
import functools
import math

import jax
import jax.numpy as jnp
from jax import lax
from jax.experimental import pallas as pl
from jax.experimental.pallas import tpu as pltpu

F32 = jnp.float32
BF16 = jnp.bfloat16
EPS = 1e-6
N_DEV = 8
TOKEN_TILE = 256
SCAN_CHUNKS = 4
S5_GROUP = 16
S5_STATE = 64
POOL_WINDOWS = (2, 4, 8, 16)
POOL_HALO = 16
CONV_K = 3
CONV_HALO = 8
ADAM_LR = 0.001
ADAM_B1 = 0.9
ADAM_B2 = 0.999
ADAM_EPS = 1e-08
ADAM_WD = 0.01
ADAM_STEP = 10
GELU_C = math.sqrt(2.0 / math.pi)
GELU_A = 0.044715
UPDATE_TILE_ELEMS = 1 << 17
PACK_ROWS = 512
VMEM_LIMIT = 56 << 20

ANY = pl.BlockSpec(memory_space=pl.ANY)


def _params(vmem=VMEM_LIMIT, ndim=1):
    return pltpu.CompilerParams(vmem_limit_bytes=vmem, dimension_semantics=("arbitrary",) * ndim)


def _dot(a, b):
    return jnp.dot(a.astype(BF16), b.astype(BF16), preferred_element_type=F32)


def _dot_nt(a, b):
    return lax.dot_general(a.astype(BF16), b.astype(BF16), (((1,), (1,)), ((), ())), preferred_element_type=F32)


def _dot_tn(a, b):
    return lax.dot_general(a.astype(BF16), b.astype(BF16), (((0,), (0,)), ((), ())), preferred_element_type=F32)


def _rms_fwd(h, g):
    r = lax.rsqrt(jnp.mean(h * h, axis=-1, keepdims=True) + EPS)
    hh = h * r
    return hh * g, hh, r


def _rms_bwd(dn, hh, r, g):
    dhh = dn * g
    return r * (dhh - hh * jnp.mean(dhh * hh, axis=-1, keepdims=True))


def _sigmoid(x):
    return 1.0 / (1.0 + jnp.exp(-x))


def _silu_and_grad(z):
    s = _sigmoid(z)
    return z * s, s * (1.0 + z * (1.0 - s))


def _gelu(y):
    t = jnp.tanh(GELU_C * (y + GELU_A * y * y * y))
    return 0.5 * y * (1.0 + t), t


def _gelu_grad(y, t):
    return 0.5 * (1.0 + t) + 0.5 * y * (1.0 - t * t) * GELU_C * (1.0 + 3.0 * GELU_A * y * y)


def _rows(shape):
    return lax.broadcasted_iota(jnp.int32, shape, 0)


def _shift_down(x, k, halo):
    y = pltpu.roll(x, k, 0)
    rows = _rows(x.shape)
    for j in range(k):
        y = jnp.where(rows == j, halo[halo.shape[0] - k + j:halo.shape[0] - k + j + 1, :], y)
    return y


def _shift_up(x, k, halo):
    n = x.shape[0]
    y = pltpu.roll(x, n - k, 0)
    rows = _rows(x.shape)
    for j in range(k):
        y = jnp.where(rows == n - k + j, halo[j:j + 1, :], y)
    return y


def _window_sums_back(ext):
    out = []
    s = ext
    for k in (1, 2, 4, 8):
        s = s + pltpu.roll(s, k, 0)
        out.append(s)
    return out


def _window_sums_fwd(ext):
    n = ext.shape[0]
    out = []
    s = ext
    for k in (1, 2, 4, 8):
        s = s + pltpu.roll(s, n - k, 0)
        out.append(s)
    return out


def _pool_inv_count(tile, tt, first_pos, w, width):
    pos = _rows((tt, width)) + (tile * tt - first_pos + 1)
    return 1.0 / jnp.clip(pos, 1, w).astype(F32)


def _slab_spec(lp, tt, sw, index_map):
    nj = sw // 128
    return pl.BlockSpec((tt * nj, 128), index_map), (lp * 4 * nj, 128)


def _pack_pair(re, im):
    def rounded(v):
        return lax.bitcast_convert_type(v, jnp.int32) + 0x8000
    return lax.bitcast_convert_type((rounded(re) & -65536) | lax.shift_right_logical(rounded(im), 16), F32)


def _unpack_pair(w):
    b = lax.bitcast_convert_type(w, jnp.int32)
    return lax.bitcast_convert_type(b & -65536, F32), lax.bitcast_convert_type(lax.shift_left(b, 16), F32)


def _slab_load(ref):
    nj = ref.shape[0] // TOKEN_TILE
    return _unpack_pair(jnp.concatenate([ref[pl.ds(j, TOKEN_TILE, stride=nj), :] for j in range(nj)], axis=1))


def _slab_store(ref, re, im):
    nj = ref.shape[0] // TOKEN_TILE
    val = _pack_pair(re, im)
    for j in range(nj):
        ref[pl.ds(j, TOKEN_TILE, stride=nj), :] = val[:, j * 128:(j + 1) * 128]


def _s5_disc_math(lr, li, ldt, br, bi):
    dt = jnp.exp(ldt)
    mag = jnp.exp(lr * dt)
    ar = mag * jnp.cos(li * dt)
    ai = mag * jnp.sin(li * dt)
    den = lr * lr + li * li
    kr = ((ar - 1.0) * lr + ai * li) / den
    ki = (ai * lr - (ar - 1.0) * li) / den
    bbr = kr[None] * br - ki[None] * bi
    bbi = kr[None] * bi + ki[None] * br
    return ar, ai, bbr, bbi


def s5_disc_fwd(lr, li, ldt, br_t, bi_t, name):
    def body(lr_ref, li_ref, ldt_ref, br_ref, bi_ref, ar_ref, ai_ref, bbr_ref, bbi_ref):
        ar, ai, bbr, bbi = _s5_disc_math(lr_ref[...], li_ref[...], ldt_ref[...], br_ref[...], bi_ref[...])
        ar_ref[...] = ar
        ai_ref[...] = ai
        bbr_ref[...] = bbr
        bbi_ref[...] = bbi

    sd = jax.ShapeDtypeStruct
    return pl.pallas_call(
        body, name=name,
        out_shape=(sd(lr.shape, F32), sd(lr.shape, F32), sd(br_t.shape, F32), sd(br_t.shape, F32)),
    )(lr, li, ldt, br_t, bi_t)


def s5_disc_bwd(lr, li, ldt, br_t, bi_t, dar, dai, dbbr, dbbi, name):
    def body(lr_ref, li_ref, ldt_ref, br_ref, bi_ref, dar_ref, dai_ref, dbbr_ref, dbbi_ref,
             dlr_ref, dli_ref, dldt_ref, dbr_ref, dbi_ref):
        _, vjp = jax.vjp(_s5_disc_math, lr_ref[...], li_ref[...], ldt_ref[...], br_ref[...], bi_ref[...])
        dlr, dli, dldt, dbr, dbi = vjp((dar_ref[...], dai_ref[...], dbbr_ref[...], dbbi_ref[...]))
        dlr_ref[...] = dlr
        dli_ref[...] = dli
        dldt_ref[...] = dldt
        dbr_ref[...] = dbr
        dbi_ref[...] = dbi

    sd = jax.ShapeDtypeStruct
    return pl.pallas_call(
        body, name=name,
        out_shape=(sd(lr.shape, F32), sd(lr.shape, F32), sd(ldt.shape, F32), sd(br_t.shape, F32), sd(br_t.shape, F32)),
    )(lr, li, ldt, br_t, bi_t, dar, dai, dbbr, dbbi)


def s5_fwd1(h, g, w_in, bdre, bdim, name):
    lp, d = h.shape
    tt = TOKEN_TILE
    cw, sw = bdre.shape[1], bdre.shape[2]

    def body(h_ref, g_ref, w_hbm, bdre_hbm, bdim_hbm, u_ref, z_ref, x_ref, w, bre, bim, n_sc):
        i, c = pl.program_id(0), pl.program_id(1)

        @pl.when((i == 0) & (c == 0))
        def _():
            pltpu.sync_copy(w_hbm, w)
            pltpu.sync_copy(bdre_hbm, bre)
            pltpu.sync_copy(bdim_hbm, bim)

        @pl.when(c == 0)
        def _():
            n_sc[...] = _rms_fwd(h_ref[...], g_ref[...])[0].astype(BF16)

        n = n_sc[...]
        u = jnp.dot(n, w[c], preferred_element_type=F32)
        u_ref[...] = u
        z_ref[...] = jnp.dot(n, w[c + 4], preferred_element_type=F32)
        ub = u.astype(BF16)
        _slab_store(x_ref, jnp.dot(ub, bre[c], preferred_element_type=F32), jnp.dot(ub, bim[c], preferred_element_type=F32))

    sd = jax.ShapeDtypeStruct
    slab, slab_shape = _slab_spec(lp, tt, sw, lambda i, c: (i * 4 + c, 0))
    return pl.pallas_call(
        body, name=name, grid=(lp // tt, 4),
        in_specs=[pl.BlockSpec((tt, d), lambda i, c: (i, 0)), pl.BlockSpec((1, d), lambda i, c: (0, 0)), ANY, ANY, ANY],
        out_specs=[pl.BlockSpec((tt, cw), lambda i, c: (i, c)), pl.BlockSpec((tt, cw), lambda i, c: (i, c)), slab],
        out_shape=(sd((lp, d), F32), sd((lp, d), F32), sd(slab_shape, F32)),
        scratch_shapes=[pltpu.VMEM(w_in.shape, BF16), pltpu.VMEM(bdre.shape, BF16), pltpu.VMEM(bdim.shape, BF16),
                        pltpu.VMEM((tt, d), BF16)],
        compiler_params=_params(ndim=2),
    )(h, g, w_in, bdre, bdim)


def s5_scan_fwd(x, ar, ai, name):
    nj = ar.shape[1]
    tt = TOKEN_TILE
    cpb = SCAN_CHUNKS
    nt = x.shape[0] // (4 * tt * nj)

    def body(x_ref, ar_ref, ai_ref, s_ref, st_r, st_i):
        i, cg = pl.program_id(0), pl.program_id(1)

        @pl.when(i == 0)
        def _():
            for q in range(cpb):
                st_r[cg * cpb + q] = jnp.zeros((nj, 128), F32)
                st_i[cg * cpb + q] = jnp.zeros((nj, 128), F32)

        a_r = [ar_ref[cg * cpb + q] for q in range(cpb)]
        a_i = [ai_ref[cg * cpb + q] for q in range(cpb)]

        def step(t, carry):
            out = []
            for q in range(cpb):
                s_r, s_i = carry[q]
                rows = pl.ds(pl.multiple_of((q * tt + t) * nj, nj), nj)
                x_r, x_i = _unpack_pair(x_ref[rows, :])
                n_r = a_r[q] * s_r - a_i[q] * s_i + x_r
                n_i = a_r[q] * s_i + a_i[q] * s_r + x_i
                s_ref[rows, :] = _pack_pair(n_r, n_i)
                out.append((n_r, n_i))
            return tuple(out)

        init = tuple((st_r[cg * cpb + q], st_i[cg * cpb + q]) for q in range(cpb))
        final = lax.fori_loop(0, tt, step, init, unroll=8)
        for q in range(cpb):
            st_r[cg * cpb + q] = final[q][0]
            st_i[cg * cpb + q] = final[q][1]

    blk = pl.BlockSpec((cpb * tt * nj, 128), lambda i, cg: (i * (4 // cpb) + cg, 0))
    par = pl.BlockSpec((4, nj, 128), lambda i, cg: (0, 0, 0))
    sd = jax.ShapeDtypeStruct
    return pl.pallas_call(
        body, name=name, grid=(nt, 4 // cpb),
        in_specs=[blk, par, par], out_specs=blk,
        out_shape=sd(x.shape, F32),
        scratch_shapes=[pltpu.VMEM((4, nj, 128), F32), pltpu.VMEM((4, nj, 128), F32)],
        compiler_params=_params(ndim=2),
    )(x, ar, ai)


def s5_fwd3(s, u, z, h, cdre, cdim, w_glu, w_out, d_skip, b_glu, name):
    lp, d = h.shape
    tt = TOKEN_TILE
    sw, cw = cdre.shape[1], cdre.shape[2]

    def body(s_ref, u_ref, z_ref, h_ref, d_ref, bg_ref, cre_hbm, cim_hbm, wg_hbm, wo_hbm,
             o_ref, y_ref, q_ref, cre, cim, wg, wo, gy_sc):
        i, c = pl.program_id(0), pl.program_id(1)

        @pl.when((i == 0) & (c == 0))
        def _():
            pltpu.sync_copy(cre_hbm, cre)
            pltpu.sync_copy(cim_hbm, cim)
            pltpu.sync_copy(wg_hbm, wg)
            pltpu.sync_copy(wo_hbm, wo)

        s_r, s_i = _slab_load(s_ref)
        y = _dot(s_r, cre[c]) + _dot(s_i, cim[c]) + d_ref[c] * u_ref[...]
        y_ref[...] = y
        gy = _gelu(y)[0]
        gy_sc[c] = gy
        part = _dot(gy, wg[c])

        @pl.when(c == 0)
        def _():
            q_ref[...] = part

        @pl.when(c > 0)
        def _():
            q_ref[...] += part

        @pl.when(c == 3)
        def _():
            sig = _sigmoid(q_ref[...] + bg_ref[...])
            zz = z_ref[...]
            sz = zz * _sigmoid(zz)
            o = h_ref[...]
            for k in range(4):
                cols = slice(k * cw, (k + 1) * cw)
                o = o + _dot(gy_sc[k] * sig[:, cols] * sz[:, cols], wo[k])
            o_ref[...] = o

    row = lambda i, c: (i, 0)
    chunk = lambda i, c: (i, c)
    slab, _ = _slab_spec(lp, tt, sw, lambda i, c: (i * 4 + c, 0))
    sd = jax.ShapeDtypeStruct((lp, d), F32)
    return pl.pallas_call(
        body, name=name, grid=(lp // tt, 4),
        in_specs=[slab, pl.BlockSpec((tt, cw), chunk),
                  pl.BlockSpec((tt, d), row), pl.BlockSpec((tt, d), row),
                  pl.BlockSpec((4, 1, cw), lambda i, c: (0, 0, 0)), pl.BlockSpec((1, d), lambda i, c: (0, 0)),
                  ANY, ANY, ANY, ANY],
        out_specs=[pl.BlockSpec((tt, d), row), pl.BlockSpec((tt, cw), chunk), pl.BlockSpec((tt, d), row)],
        out_shape=(sd, sd, sd),
        scratch_shapes=[pltpu.VMEM(cdre.shape, BF16), pltpu.VMEM(cdim.shape, BF16), pltpu.VMEM(w_glu.shape, BF16),
                        pltpu.VMEM(w_out.shape, BF16), pltpu.VMEM((4, tt, cw), F32)],
        compiler_params=_params(ndim=2),
    )(s, u, z, h, d_skip, b_glu, cdre, cdim, w_glu, w_out)


def s5_bwd3a(dh, y, q, z, w_glu, w_out, b_glu, name):
    lp, d = dh.shape
    tt = TOKEN_TILE
    nt = lp // tt
    cw = w_glu.shape[1]

    def body(dh_ref, y_ref, q_ref, z_ref, bg_ref, wg_hbm, wo_hbm, dy_ref, dp_ref, dwo_hbm, dwg_hbm, dbg_hbm,
             wg, wo, dwo, dwg, dbg):
        i = pl.program_id(0)

        @pl.when(i == 0)
        def _():
            pltpu.sync_copy(wg_hbm, wg)
            pltpu.sync_copy(wo_hbm, wo)
            dwo[...] = jnp.zeros_like(dwo)
            dwg[...] = jnp.zeros_like(dwg)
            dbg[...] = jnp.zeros_like(dbg)

        sig = _sigmoid(q_ref[...] + bg_ref[...])
        sz, dsz = _silu_and_grad(z_ref[...])
        dhv = dh_ref[...]
        yv = y_ref[...]
        gy, t = _gelu(yv)
        dq_parts, dgy_parts = [], []
        for k in range(4):
            cols = slice(k * cw, (k + 1) * cw)
            gy_k, sig_k, sz_k = gy[:, cols], sig[:, cols], sz[:, cols]
            y2 = gy_k * sig_k
            dy3 = _dot_nt(dhv, wo[k])
            dwo[k] += _dot_tn(y2 * sz_k, dhv)
            dy2 = dy3 * sz_k
            dp_ref[0, :, cols] = (dy3 * y2 * dsz[:, cols]).astype(BF16)
            dq_parts.append(dy2 * gy_k * sig_k * (1.0 - sig_k))
            dgy_parts.append(dy2 * sig_k)
        dq = jnp.concatenate(dq_parts, axis=1)
        dbg[...] += jnp.sum(dq, axis=0, keepdims=True)
        dgelu = _gelu_grad(yv, t)
        for k in range(4):
            cols = slice(k * cw, (k + 1) * cw)
            dwg[k] += _dot_tn(gy[:, cols], dq)
            dy_ref[:, cols] = (dgy_parts[k] + _dot_nt(dq, wg[k])) * dgelu[:, cols]

        @pl.when(i == nt - 1)
        def _():
            pltpu.sync_copy(dwo, dwo_hbm)
            pltpu.sync_copy(dwg, dwg_hbm)
            pltpu.sync_copy(dbg, dbg_hbm)

    row = pl.BlockSpec((tt, d), lambda i: (i, 0))
    sd = jax.ShapeDtypeStruct
    return pl.pallas_call(
        body, name=name, grid=(nt,),
        in_specs=[row, row, row, row, pl.BlockSpec((1, d), lambda i: (0, 0)), ANY, ANY],
        out_specs=[row, pl.BlockSpec((1, tt, d), lambda i: (1, i, 0)), ANY, ANY, ANY],
        out_shape=(sd((lp, d), F32), sd((2, lp, d), BF16), sd(w_out.shape, F32), sd(w_glu.shape, F32), sd((1, d), F32)),
        scratch_shapes=[pltpu.VMEM(w_glu.shape, BF16), pltpu.VMEM(w_out.shape, BF16),
                        pltpu.VMEM(w_out.shape, F32), pltpu.VMEM(w_glu.shape, F32), pltpu.VMEM((1, d), F32)],
        compiler_params=_params(),
    )(dh, y, q, z, b_glu, w_glu, w_out)


def s5_bwd3b(dy, s, u, cdre, cdim, d_skip, name):
    lp, d = dy.shape
    tt = TOKEN_TILE
    nt = lp // tt
    sw, cw = cdre.shape[1], cdre.shape[2]
    gc = cw // S5_GROUP

    def body(dy_ref, s_ref, u_ref, d_ref, cre_hbm, cim_hbm,
             ds_ref, dus_ref, dcre_ref, dcim_ref, dd_hbm, cre, cim, dcre, dcim, dd):
        i, c = pl.program_id(0), pl.program_id(1)

        @pl.when((i == 0) & (c == 0))
        def _():
            pltpu.sync_copy(cre_hbm, cre)
            pltpu.sync_copy(cim_hbm, cim)
            dcre[...] = jnp.zeros_like(dcre)
            dcim[...] = jnp.zeros_like(dcim)
            dd[...] = jnp.zeros_like(dd)

        dyv = dy_ref[...]
        dd[c] += jnp.sum(dyv * u_ref[...], axis=0, keepdims=True)
        dus_ref[...] = dyv * d_ref[c]
        _slab_store(ds_ref, _dot_nt(dyv, cre[c]), _dot_nt(dyv, cim[c]))
        s_r, s_i = _slab_load(s_ref)
        dcre[c] += _dot_tn(s_r, dyv)
        dcim[c] += _dot_tn(s_i, dyv)

        @pl.when((i == nt - 1) & (c == 3))
        def _():
            for k in range(4):
                for j in range(gc):
                    rows, cols = pl.ds(j * S5_STATE, S5_STATE), pl.ds(j * S5_GROUP, S5_GROUP)
                    dcre_ref[k, j] = dcre[k, rows, cols].T
                    dcim_ref[k, j] = dcim[k, rows, cols].T
            pltpu.sync_copy(dd, dd_hbm)

    chunk = lambda i, c: (i, c)
    sd = jax.ShapeDtypeStruct
    slab, slab_shape = _slab_spec(lp, tt, sw, lambda i, c: (i * 4 + c, 0))
    diag = pl.BlockSpec((4, gc, S5_GROUP, S5_STATE), lambda i, c: (0, 0, 0, 0))
    return pl.pallas_call(
        body, name=name, grid=(nt, 4),
        in_specs=[pl.BlockSpec((tt, cw), chunk), slab,
                  pl.BlockSpec((tt, cw), chunk), pl.BlockSpec((4, 1, cw), lambda i, c: (0, 0, 0)), ANY, ANY],
        out_specs=[slab, pl.BlockSpec((tt, cw), chunk), diag, diag, ANY],
        out_shape=(sd(slab_shape, F32), sd((lp, d), F32),
                   sd((4, gc, S5_GROUP, S5_STATE), F32), sd((4, gc, S5_GROUP, S5_STATE), F32), sd((4, 1, cw), F32)),
        scratch_shapes=[pltpu.VMEM(cdre.shape, BF16), pltpu.VMEM(cdim.shape, BF16),
                        pltpu.VMEM(cdre.shape, F32), pltpu.VMEM(cdim.shape, F32), pltpu.VMEM((4, 1, cw), F32)],
        compiler_params=_params(ndim=2),
    )(dy, s, u, d_skip, cdre, cdim)


def s5_scan_bwd(g, s, ar, ai, name):
    nj = ar.shape[1]
    tt = TOKEN_TILE
    cpb = SCAN_CHUNKS
    nt = g.shape[0] // (4 * tt * nj)

    def body(g_ref, s_ref, ar_ref, ai_ref, lam_ref, dar_ref, dai_ref, st_r, st_i, acc_r, acc_i):
        i, cg = pl.program_id(0), pl.program_id(1)

        @pl.when((i == 0) & (cg == 0))
        def _():
            for ref in (st_r, st_i, acc_r, acc_i):
                ref[...] = jnp.zeros_like(ref)

        a_r = [ar_ref[cg * cpb + q] for q in range(cpb)]
        a_i = [ai_ref[cg * cpb + q] for q in range(cpb)]

        def slab(q, t):
            return pl.ds(pl.multiple_of((q * tt + t) * nj, nj), nj)

        def adjoint(q, t, l_r, l_i):
            rows = slab(q, t)
            g_r, g_i = _unpack_pair(g_ref[rows, :])
            n_r = g_r + a_r[q] * l_r + a_i[q] * l_i
            n_i = g_i + a_r[q] * l_i - a_i[q] * l_r
            lam_ref[rows, :] = _pack_pair(n_r, n_i)
            return n_r, n_i

        def pair(q, t, l_r, l_i, d_r, d_i):
            p_r, p_i = _unpack_pair(s_ref[slab(q, t), :])
            return d_r + l_r * p_r + l_i * p_i, d_i + l_i * p_r - l_r * p_i

        def step(k, carry):
            t = tt - 1 - k
            out = []
            for q in range(cpb):
                l_r, l_i, d_r, d_i = carry[q]
                l_r, l_i = adjoint(q, t, l_r, l_i)
                d_r, d_i = pair(q, t - 1, l_r, l_i, d_r, d_i)
                out.append((l_r, l_i, d_r, d_i))
            return tuple(out)

        init = []
        for q in range(cpb):
            ch = cg * cpb + q
            l_r, l_i = st_r[ch], st_i[ch]
            d_r, d_i = pair(q, tt - 1, l_r, l_i, acc_r[ch], acc_i[ch])
            init.append((l_r, l_i, d_r, d_i))
        final = lax.fori_loop(0, tt - 1, step, tuple(init), unroll=8)
        for q in range(cpb):
            ch = cg * cpb + q
            l_r, l_i, d_r, d_i = final[q]
            l_r, l_i = adjoint(q, 0, l_r, l_i)
            st_r[ch] = l_r
            st_i[ch] = l_i
            acc_r[ch] = d_r
            acc_i[ch] = d_i
            dar_ref[ch] = d_r
            dai_ref[ch] = d_i

    blk = pl.BlockSpec((cpb * tt * nj, 128), lambda i, cg: ((nt - 1 - i) * (4 // cpb) + cg, 0))
    par = pl.BlockSpec((4, nj, 128), lambda i, cg: (0, 0, 0))
    sd = jax.ShapeDtypeStruct
    return pl.pallas_call(
        body, name=name, grid=(nt, 4 // cpb),
        in_specs=[blk, blk, par, par], out_specs=[blk, par, par],
        out_shape=(sd(g.shape, F32), sd((4, nj, 128), F32), sd((4, nj, 128), F32)),
        scratch_shapes=[pltpu.VMEM((4, nj, 128), F32)] * 4,
        compiler_params=_params(ndim=2),
    )(g, s, ar, ai)


def s5_bwd1(lam, dus, u, dp, h, dh, g, w_in, bdre, bdim, name):
    lp, d = h.shape
    tt = TOKEN_TILE
    nt = lp // tt
    cw, sw = bdre.shape[1], bdre.shape[2]
    gc = cw // S5_GROUP

    def body(lam_ref, dus_ref, u_ref, dpz_ref, h_ref, dh_ref, g_ref, w_hbm, bre_hbm, bim_hbm,
             dpu_ref, dho_ref, n_ref, dbre_ref, dbim_ref, dg_hbm, w, bre, bim, dn_sc, dbre, dbim, dg):
        i, c = pl.program_id(0), pl.program_id(1)

        @pl.when((i == 0) & (c == 0))
        def _():
            pltpu.sync_copy(w_hbm, w)
            pltpu.sync_copy(bre_hbm, bre)
            pltpu.sync_copy(bim_hbm, bim)
            dbre[...] = jnp.zeros_like(dbre)
            dbim[...] = jnp.zeros_like(dbim)
            dg[...] = jnp.zeros_like(dg)

        (l_r, l_i), uv = _slab_load(lam_ref), u_ref[...]
        du = dus_ref[...] + _dot_nt(l_r, bre[c]) + _dot_nt(l_i, bim[c])
        dbre[c] += _dot_tn(uv, l_r)
        dbim[c] += _dot_tn(uv, l_i)
        dpu_ref[0] = du.astype(BF16)
        part = _dot_nt(du, w[c])

        @pl.when(c == 0)
        def _():
            dn_sc[...] = part

        @pl.when(c > 0)
        def _():
            dn_sc[...] += part

        @pl.when(c == 3)
        def _():
            dz = dpz_ref[0]
            dn = dn_sc[...]
            for k in range(4):
                dn = dn + _dot_nt(dz[:, k * cw:(k + 1) * cw], w[4 + k])
            gv = g_ref[...]
            n, hh, rr = _rms_fwd(h_ref[...], gv)
            n_ref[...] = n.T.astype(BF16)
            dg[...] += jnp.sum(dn * hh, axis=0, keepdims=True)
            dho_ref[...] = dh_ref[...] + _rms_bwd(dn, hh, rr, gv)

        @pl.when((i == nt - 1) & (c == 3))
        def _():
            for k in range(4):
                for j in range(gc):
                    rows, cols = pl.ds(j * S5_GROUP, S5_GROUP), pl.ds(j * S5_STATE, S5_STATE)
                    dbre_ref[k, j] = dbre[k, rows, cols]
                    dbim_ref[k, j] = dbim[k, rows, cols]
            pltpu.sync_copy(dg, dg_hbm)

    row = lambda i, c: (i, 0)
    chunk = lambda i, c: (i, c)
    sd = jax.ShapeDtypeStruct
    slab, _ = _slab_spec(lp, tt, sw, lambda i, c: (i * 4 + c, 0))
    diag = pl.BlockSpec((4, gc, S5_GROUP, S5_STATE), lambda i, c: (0, 0, 0, 0))
    return pl.pallas_call(
        body, name=name, grid=(nt, 4),
        in_specs=[slab, pl.BlockSpec((tt, cw), chunk),
                  pl.BlockSpec((tt, cw), chunk), pl.BlockSpec((1, tt, d), lambda i, c: (1, i, 0)),
                  pl.BlockSpec((tt, d), row), pl.BlockSpec((tt, d), row), pl.BlockSpec((1, d), lambda i, c: (0, 0)),
                  ANY, ANY, ANY],
        out_specs=[pl.BlockSpec((1, tt, cw), lambda i, c: (0, i, c)), pl.BlockSpec((tt, d), row),
                   pl.BlockSpec((d, tt), lambda i, c: (0, i)), diag, diag, ANY],
        out_shape=(sd(dp.shape, BF16), sd((lp, d), F32), sd((d, lp), BF16),
                   sd((4, gc, S5_GROUP, S5_STATE), F32), sd((4, gc, S5_GROUP, S5_STATE), F32), sd((1, d), F32)),
        input_output_aliases={3: 0},
        scratch_shapes=[pltpu.VMEM(w_in.shape, BF16), pltpu.VMEM(bdre.shape, BF16), pltpu.VMEM(bdim.shape, BF16),
                        pltpu.VMEM((tt, d), F32), pltpu.VMEM(bdre.shape, F32), pltpu.VMEM(bdim.shape, F32), pltpu.VMEM((1, d), F32)],
        compiler_params=_params(ndim=2),
    )(lam, dus, u, dp, h, dh, g, w_in, bdre, bdim)


def grad_w_in(n_t, dp, blk, name):
    d, lp = n_t.shape
    npart, _, width = dp.shape
    per = width // blk

    def body(n_ref, dp_ref, o_ref):
        o_ref[0] = jnp.dot(n_ref[...], dp_ref[0], preferred_element_type=F32)

    return pl.pallas_call(
        body, name=name, grid=(npart * per,),
        in_specs=[pl.BlockSpec((d, lp), lambda j: (0, 0), pipeline_mode=pl.Buffered(1)),
                  pl.BlockSpec((1, lp, blk), lambda j: (j // per, 0, j % per))],
        out_specs=pl.BlockSpec((1, d, blk), lambda j: (j, 0, 0)),
        out_shape=jax.ShapeDtypeStruct((npart * per, d, blk), F32),
        compiler_params=_params(),
    )(n_t, dp)


def _conv_fwd_chunk(n, w, cw_ref, cb_ref, halo, c, nch):
    bg = jnp.dot(n, w[c], preferred_element_type=F32)
    cg = jnp.dot(n, w[nch + c], preferred_element_type=F32)
    v = jnp.dot(n, w[2 * nch + c], preferred_element_type=F32)
    z = jnp.dot(n, w[3 * nch + c], preferred_element_type=F32)
    hc = cg * v
    taps = cw_ref[c]
    conv = taps[2:3, :] * hc + taps[1:2, :] * _shift_down(hc, 1, halo) + taps[0:1, :] * _shift_down(hc, 2, halo) + cb_ref[c]
    return bg, cg, v, z, hc, conv


def conv_fwd(h, g, w_in, conv_w, conv_b, w_out, name):
    lp, d = h.shape
    tt = TOKEN_TILE
    nt = lp // tt
    nch, ce = w_out.shape[0], w_out.shape[1]

    def body(h_ref, g_ref, cw_ref, cb_ref, w_hbm, wo_hbm, o_ref, halo_ref, w, wo, halo):
        i = pl.program_id(0)

        @pl.when(i == 0)
        def _():
            pltpu.sync_copy(w_hbm, w)
            pltpu.sync_copy(wo_hbm, wo)
            halo[...] = jnp.zeros_like(halo)

        hv = h_ref[...]
        n = _rms_fwd(hv, g_ref[...])[0].astype(BF16)
        o = hv
        for c in range(nch):
            bg, _, _, z, hc, conv = _conv_fwd_chunk(n, w, cw_ref, cb_ref, halo[c], c, nch)
            o = o + _dot(bg * conv * (z * _sigmoid(z)), wo[c])
            halo[c] = hc[tt - CONV_HALO:, :]
            halo_ref[0, c] = hc[tt - CONV_HALO:, :]
        o_ref[...] = o

    sd = jax.ShapeDtypeStruct
    return pl.pallas_call(
        body, name=name, grid=(nt,),
        in_specs=[pl.BlockSpec((tt, d), lambda i: (i, 0)), pl.BlockSpec((1, d), lambda i: (0, 0)),
                  pl.BlockSpec(conv_w.shape, lambda i: (0, 0, 0)), pl.BlockSpec(conv_b.shape, lambda i: (0, 0, 0)), ANY, ANY],
        out_specs=[pl.BlockSpec((tt, d), lambda i: (i, 0)), pl.BlockSpec((1, nch, CONV_HALO, ce), lambda i: (i, 0, 0, 0))],
        out_shape=(sd((lp, d), F32), sd((nt, nch, CONV_HALO, ce), F32)),
        scratch_shapes=[pltpu.VMEM(w_in.shape, BF16), pltpu.VMEM(w_out.shape, BF16), pltpu.VMEM((nch, CONV_HALO, ce), F32)],
        compiler_params=_params(),
    )(h, g, conv_w, conv_b, w_in, w_out)


def conv_bwd(h, dh, halos, g, w_in, conv_w, conv_b, w_out, name):
    lp, d = h.shape
    tt = TOKEN_TILE
    nt = lp // tt
    nch, ce = w_out.shape[0], w_out.shape[1]

    def body(h_ref, dh_ref, halo_ref, g_ref, cw_ref, cb_ref, w_hbm, wo_hbm,
             dho_ref, n_ref, dp_ref, dwo_hbm, dcw_hbm, dcb_hbm, dg_hbm, w, wo, nxt, dwo, dcw, dcb, dg):
        i = pl.program_id(0)

        @pl.when(i == 0)
        def _():
            pltpu.sync_copy(w_hbm, w)
            pltpu.sync_copy(wo_hbm, wo)
            for ref in (nxt, dwo, dcw, dcb, dg):
                ref[...] = jnp.zeros_like(ref)

        gv = g_ref[...]
        nf, hh, rr = _rms_fwd(h_ref[...], gv)
        n = nf.astype(BF16)
        n_ref[...] = nf.T.astype(BF16)
        dhv = dh_ref[...]
        has_prev = (i < nt - 1).astype(F32)
        dn = jnp.zeros((tt, d), F32)
        for c in range(nch):
            halo = halo_ref[0, c] * has_prev
            bg, cg, v, z, hc, conv = _conv_fwd_chunk(n, w, cw_ref, cb_ref, halo, c, nch)
            sz, dsz = _silu_and_grad(z)
            y1 = bg * conv
            dy2 = _dot_nt(dhv, wo[c])
            dwo[c] += _dot_tn(y1 * sz, dhv)
            dy1 = dy2 * sz
            dz = dy2 * y1 * dsz
            dbg = dy1 * conv
            dconv = dy1 * bg
            dcb[c] += jnp.sum(dconv, axis=0, keepdims=True)
            up1 = _shift_up(dconv, 1, nxt[c])
            up2 = _shift_up(dconv, 2, nxt[c])
            nxt[c] = dconv[:CONV_HALO, :]
            taps = cw_ref[c]
            dhc = taps[2:3, :] * dconv + taps[1:2, :] * up1 + taps[0:1, :] * up2
            dcw[c, 0:1, :] += jnp.sum(hc * up2, axis=0, keepdims=True)
            dcw[c, 1:2, :] += jnp.sum(hc * up1, axis=0, keepdims=True)
            dcw[c, 2:3, :] += jnp.sum(hc * dconv, axis=0, keepdims=True)
            dcg = dhc * v
            dv = dhc * cg
            cols = slice(c * ce, (c + 1) * ce)
            for p, val in enumerate((dbg, dcg, dv, dz)):
                dp_ref[p, :, cols] = val.astype(BF16)
                dn = dn + _dot_nt(val, w[p * nch + c])
        dg[...] += jnp.sum(dn * hh, axis=0, keepdims=True)
        dho_ref[...] = dhv + _rms_bwd(dn, hh, rr, gv)

        @pl.when(i == nt - 1)
        def _():
            pltpu.sync_copy(dwo, dwo_hbm)
            pltpu.sync_copy(dcw, dcw_hbm)
            pltpu.sync_copy(dcb, dcb_hbm)
            pltpu.sync_copy(dg, dg_hbm)

    rev = lambda i: (nt - 1 - i, 0)
    sd = jax.ShapeDtypeStruct
    return pl.pallas_call(
        body, name=name, grid=(nt,),
        in_specs=[pl.BlockSpec((tt, d), rev), pl.BlockSpec((tt, d), rev),
                  pl.BlockSpec((1, nch, CONV_HALO, ce), lambda i: (jnp.maximum(nt - 2 - i, 0), 0, 0, 0)),
                  pl.BlockSpec((1, d), lambda i: (0, 0)),
                  pl.BlockSpec(conv_w.shape, lambda i: (0, 0, 0)), pl.BlockSpec(conv_b.shape, lambda i: (0, 0, 0)), ANY, ANY],
        out_specs=[pl.BlockSpec((tt, d), rev), pl.BlockSpec((d, tt), lambda i: (0, nt - 1 - i)),
                   pl.BlockSpec((4, tt, nch * ce), lambda i: (0, nt - 1 - i, 0)), ANY, ANY, ANY, ANY],
        out_shape=(sd((lp, d), F32), sd((d, lp), BF16), sd((4, lp, nch * ce), BF16),
                   sd(w_out.shape, F32), sd((nch, 8, ce), F32), sd((nch, 1, ce), F32), sd((1, d), F32)),
        scratch_shapes=[pltpu.VMEM(w_in.shape, BF16), pltpu.VMEM(w_out.shape, BF16), pltpu.VMEM((nch, CONV_HALO, ce), F32),
                        pltpu.VMEM(w_out.shape, F32), pltpu.VMEM((nch, 8, ce), F32), pltpu.VMEM((nch, 1, ce), F32),
                        pltpu.VMEM((1, d), F32)],
        compiler_params=_params(),
    )(h, dh, halos, g, conv_w, conv_b, w_in, w_out)


def _pool_fwd_group(n, w, wg, bg_ref, sc_ref, halo, k, tile, tt, first_pos):
    u = jnp.dot(n, w[k], preferred_element_type=F32)
    z = jnp.dot(n, w[4 + k], preferred_element_type=F32)
    ext = jnp.concatenate([halo, u], axis=0)
    win = _window_sums_back(ext)[k][POOL_HALO:, :]
    mixed = win * _pool_inv_count(tile, tt, first_pos, POOL_WINDOWS[k], u.shape[1]) - u
    outs = _dot(mixed, wg[k]) + bg_ref[k]
    return u, z, mixed, outs, outs * sc_ref[k]


def pool_fwd(h, g, w_in, w_grp, b_grp, scale, w_out, first_pos, name):
    lp, d = h.shape
    tt = TOKEN_TILE
    nt = lp // tt
    gw = w_grp.shape[1]

    def body(h_ref, g_ref, bg_ref, sc_ref, w_hbm, wg_hbm, wo_hbm, o_ref, halo_ref, w, wg, wo, halo):
        i = pl.program_id(0)

        @pl.when(i == 0)
        def _():
            pltpu.sync_copy(w_hbm, w)
            pltpu.sync_copy(wg_hbm, wg)
            pltpu.sync_copy(wo_hbm, wo)
            halo[...] = jnp.zeros_like(halo)

        hv = h_ref[...]
        n = _rms_fwd(hv, g_ref[...])[0].astype(BF16)
        o = hv
        for k in range(4):
            u, z, _, _, yp = _pool_fwd_group(n, w, wg, bg_ref, sc_ref, halo[k], k, i, tt, first_pos)
            o = o + _dot(yp * (z * _sigmoid(z)), wo[k])
            halo[k] = u[tt - POOL_HALO:, :]
            halo_ref[0, k] = u[tt - POOL_HALO:, :]
        o_ref[...] = o

    sd = jax.ShapeDtypeStruct
    small = pl.BlockSpec((4, 1, gw), lambda i: (0, 0, 0))
    return pl.pallas_call(
        body, name=name, grid=(nt,),
        in_specs=[pl.BlockSpec((tt, d), lambda i: (i, 0)), pl.BlockSpec((1, d), lambda i: (0, 0)), small, small, ANY, ANY, ANY],
        out_specs=[pl.BlockSpec((tt, d), lambda i: (i, 0)), pl.BlockSpec((1, 4, POOL_HALO, gw), lambda i: (i, 0, 0, 0))],
        out_shape=(sd((lp, d), F32), sd((nt, 4, POOL_HALO, gw), F32)),
        scratch_shapes=[pltpu.VMEM(w_in.shape, BF16), pltpu.VMEM(w_grp.shape, BF16), pltpu.VMEM(w_out.shape, BF16),
                        pltpu.VMEM((4, POOL_HALO, gw), F32)],
        compiler_params=_params(),
    )(h, g, b_grp, scale, w_in, w_grp, w_out)


def pool_bwd(h, dh, halos, g, w_in, w_grp, b_grp, scale, w_out, first_pos, name):
    lp, d = h.shape
    tt = TOKEN_TILE
    nt = lp // tt
    gw = w_grp.shape[1]

    def body(h_ref, dh_ref, halo_ref, g_ref, bg_ref, sc_ref, w_hbm, wg_hbm, wo_hbm,
             dho_ref, n_ref, dp_ref, dwo_hbm, dwg_hbm, dbg_hbm, dsc_hbm, dg_hbm,
             w, wg, wo, nxt, dwo, dwg, dbg, dsc, dg):
        i = pl.program_id(0)
        tile = nt - 1 - i

        @pl.when(i == 0)
        def _():
            pltpu.sync_copy(w_hbm, w)
            pltpu.sync_copy(wg_hbm, wg)
            pltpu.sync_copy(wo_hbm, wo)
            for ref in (nxt, dwo, dwg, dbg, dsc, dg):
                ref[...] = jnp.zeros_like(ref)

        gv = g_ref[...]
        nf, hh, rr = _rms_fwd(h_ref[...], gv)
        n = nf.astype(BF16)
        n_ref[...] = nf.T.astype(BF16)
        dhv = dh_ref[...]
        has_prev = (i < nt - 1).astype(F32)
        dn = jnp.zeros((tt, d), F32)
        for k in range(4):
            u, z, mixed, outs, yp = _pool_fwd_group(n, w, wg, bg_ref, sc_ref, halo_ref[0, k] * has_prev, k, tile, tt, first_pos)
            sz, dsz = _silu_and_grad(z)
            dy = _dot_nt(dhv, wo[k])
            dwo[k] += _dot_tn(yp * sz, dhv)
            dyp = dy * sz
            dz = dy * yp * dsz
            dsc[k] += jnp.sum(dyp * outs, axis=0, keepdims=True)
            douts = dyp * sc_ref[k]
            dbg[k] += jnp.sum(douts, axis=0, keepdims=True)
            dwg[k] += _dot_tn(mixed, douts)
            dmixed = _dot_nt(douts, wg[k])
            dm = dmixed * _pool_inv_count(tile, tt, first_pos, POOL_WINDOWS[k], gw)
            ext = jnp.concatenate([dm, nxt[k]], axis=0)
            du = _window_sums_fwd(ext)[k][:tt, :] - dmixed
            nxt[k] = dm[:POOL_HALO, :]
            cols = slice(k * gw, (k + 1) * gw)
            dp_ref[0, :, cols] = du.astype(BF16)
            dp_ref[1, :, cols] = dz.astype(BF16)
            dn = dn + _dot_nt(du, w[k]) + _dot_nt(dz, w[4 + k])
        dg[...] += jnp.sum(dn * hh, axis=0, keepdims=True)
        dho_ref[...] = dhv + _rms_bwd(dn, hh, rr, gv)

        @pl.when(i == nt - 1)
        def _():
            pltpu.sync_copy(dwo, dwo_hbm)
            pltpu.sync_copy(dwg, dwg_hbm)
            pltpu.sync_copy(dbg, dbg_hbm)
            pltpu.sync_copy(dsc, dsc_hbm)
            pltpu.sync_copy(dg, dg_hbm)

    rev = lambda i: (nt - 1 - i, 0)
    sd = jax.ShapeDtypeStruct
    small = pl.BlockSpec((4, 1, gw), lambda i: (0, 0, 0))
    return pl.pallas_call(
        body, name=name, grid=(nt,),
        in_specs=[pl.BlockSpec((tt, d), rev), pl.BlockSpec((tt, d), rev),
                  pl.BlockSpec((1, 4, POOL_HALO, gw), lambda i: (jnp.maximum(nt - 2 - i, 0), 0, 0, 0)),
                  pl.BlockSpec((1, d), lambda i: (0, 0)), small, small, ANY, ANY, ANY],
        out_specs=[pl.BlockSpec((tt, d), rev), pl.BlockSpec((d, tt), lambda i: (0, nt - 1 - i)),
                   pl.BlockSpec((2, tt, 4 * gw), lambda i: (0, nt - 1 - i, 0)), ANY, ANY, ANY, ANY, ANY],
        out_shape=(sd((lp, d), F32), sd((d, lp), BF16), sd((2, lp, 4 * gw), BF16),
                   sd(w_out.shape, F32), sd(w_grp.shape, F32), sd((4, 1, gw), F32), sd((4, 1, gw), F32), sd((1, d), F32)),
        scratch_shapes=[pltpu.VMEM(w_in.shape, BF16), pltpu.VMEM(w_grp.shape, BF16), pltpu.VMEM(w_out.shape, BF16),
                        pltpu.VMEM((4, POOL_HALO, gw), F32), pltpu.VMEM(w_out.shape, F32), pltpu.VMEM(w_grp.shape, F32),
                        pltpu.VMEM((4, 1, gw), F32), pltpu.VMEM((4, 1, gw), F32), pltpu.VMEM((1, d), F32)],
        compiler_params=_params(),
    )(h, dh, halos, g, b_grp, scale, w_in, w_grp, w_out)


def loss_head(h, target, g, pad_tiles, name):
    lp, d = h.shape
    tt = TOKEN_TILE
    nt = lp // tt

    def body(h_ref, t_ref, g_ref, dh_ref, dg_ref, loss_ref, acc):
        i = pl.program_id(0)

        @pl.when(i == 0)
        def _():
            acc[...] = jnp.zeros_like(acc)
            dg_ref[...] = jnp.zeros_like(dg_ref)

        @pl.when(i < pad_tiles)
        def _():
            dh_ref[...] = jnp.zeros_like(dh_ref)

        @pl.when(i >= pad_tiles)
        def _():
            gv = g_ref[...]
            n, hh, rr = _rms_fwd(h_ref[...], gv)
            err = n - t_ref[...]
            acc[...] += 0.5 * jnp.sum(jnp.mean(err * err, axis=-1, keepdims=True), axis=0, keepdims=True)
            dn = err * (1.0 / d)
            dg_ref[...] += jnp.sum(dn * hh, axis=0, keepdims=True)
            dh_ref[...] = _rms_bwd(dn, hh, rr, gv)

        loss_ref[...] = jnp.broadcast_to(acc[...], loss_ref.shape)

    sd = jax.ShapeDtypeStruct
    return pl.pallas_call(
        body, name=name, grid=(nt,),
        in_specs=[pl.BlockSpec((tt, d), lambda i: (i, 0)), pl.BlockSpec((tt, d), lambda i: (jnp.maximum(i - pad_tiles, 0), 0)),
                  pl.BlockSpec((1, d), lambda i: (0, 0))],
        out_specs=[pl.BlockSpec((tt, d), lambda i: (i, 0)), pl.BlockSpec((1, d), lambda i: (0, 0)),
                   pl.BlockSpec((8, 128), lambda i: (0, 0))],
        out_shape=(sd((lp, d), F32), sd((1, d), F32), sd((8, 128), F32)),
        scratch_shapes=[pltpu.VMEM((1, 1), F32)],
        compiler_params=_params(),
    )(h, target, g)


def exchange(arrs, gather, name):
    n = len(arrs)

    def body(*refs):
        ins, outs = refs[:n], refs[n:2 * n]
        send_sems, recv_sems, own_sems = refs[2 * n:]
        x, y, c = lax.axis_index("x"), lax.axis_index("y"), lax.axis_index("c")
        me = 4 * x + 2 * y + c
        own = []
        for a in range(n):
            cp = pltpu.make_async_copy(ins[a] if gather else ins[a].at[me], outs[a].at[me], own_sems.at[a])
            cp.start()
            own.append(cp)
        sent = []
        for k in range(1, N_DEV):
            px = 1 - x if k & 4 else x
            py = 1 - y if k & 2 else y
            pc = 1 - c if k & 1 else c
            peer = 4 * px + 2 * py + pc
            for a in range(n):
                cp = pltpu.make_async_remote_copy(
                    src_ref=ins[a] if gather else ins[a].at[peer], dst_ref=outs[a].at[me],
                    send_sem=send_sems.at[a, k - 1], recv_sem=recv_sems.at[a, k - 1],
                    device_id=(px, py, pc), device_id_type=pl.DeviceIdType.MESH)
                cp.start()
                sent.append((cp, a, k, peer, (px, py, pc)))
        for cp, a, k, peer, pid in sent:
            cp.wait_send()
            pltpu.make_async_remote_copy(
                src_ref=ins[a] if gather else ins[a].at[peer], dst_ref=outs[a].at[peer],
                send_sem=send_sems.at[a, k - 1], recv_sem=recv_sems.at[a, k - 1],
                device_id=pid, device_id_type=pl.DeviceIdType.MESH).wait_recv()
        for cp in own:
            cp.wait()

    hbm = pl.BlockSpec(memory_space=pltpu.HBM)
    out_shape = tuple(jax.ShapeDtypeStruct(((N_DEV,) + a.shape) if gather else a.shape, a.dtype) for a in arrs)
    return pl.pallas_call(
        body, name=name, in_specs=[hbm] * n, out_specs=[hbm] * n, out_shape=out_shape,
        scratch_shapes=[pltpu.SemaphoreType.DMA((n, N_DEV - 1)), pltpu.SemaphoreType.DMA((n, N_DEV - 1)),
                        pltpu.SemaphoreType.DMA((n,))],
    )(*[pltpu.with_memory_space_constraint(a, pltpu.HBM) for a in arrs])


def _peers(x, y, c):
    out = []
    for k in range(1, N_DEV):
        px = 1 - x if k & 4 else x
        py = 1 - y if k & 2 else y
        pc = 1 - c if k & 1 else c
        out.append((k, (px, py, pc), 4 * px + 2 * py + pc))
    return out


def exchange_start(arrs, gather, after, name):
    n = len(arrs)
    me = 4 * lax.axis_index("x") + 2 * lax.axis_index("y") + lax.axis_index("c")
    lands = []
    for a in arrs:
        own = a[None] if gather else lax.dynamic_index_in_dim(a, me, 0, keepdims=True)
        lands.append(lax.dynamic_update_index_in_dim(lax.empty(((N_DEV,) + a.shape) if gather else a.shape, a.dtype), own, me, 0))

    def body(*refs):
        ins, land = refs[:n], refs[n:2 * n]
        send_sems, recv_sems, token = refs[2 * n + 1], refs[2 * n + 2], refs[4 * n + 3]
        x, y, c = lax.axis_index("x"), lax.axis_index("y"), lax.axis_index("c")
        me = 4 * x + 2 * y + c
        for k, pid, peer in _peers(x, y, c):
            for a in range(n):
                pltpu.make_async_remote_copy(
                    src_ref=ins[a] if gather else ins[a].at[peer], dst_ref=land[a].at[me],
                    send_sem=send_sems.at[a * (N_DEV - 1) + k - 1], recv_sem=recv_sems.at[a * (N_DEV - 1) + k - 1],
                    device_id=pid, device_id_type=pl.DeviceIdType.MESH).start()
        token[...] = jnp.zeros_like(token)

    hbm = pl.BlockSpec(memory_space=pltpu.HBM)
    sem = pl.BlockSpec(memory_space=pltpu.SEMAPHORE)
    sems = pltpu.SemaphoreType.DMA((n * (N_DEV - 1),))
    res = pl.pallas_call(
        body, name=name, in_specs=[hbm] * (2 * n) + [ANY],
        out_specs=[sem, sem] + [hbm] * (2 * n) + [pl.BlockSpec(memory_space=pltpu.VMEM)],
        out_shape=[sems, sems] + [pltpu.HBM(a.shape, a.dtype) for a in arrs] + [pltpu.HBM(l.shape, l.dtype) for l in lands]
        + [jax.ShapeDtypeStruct((8, 128), F32)],
        input_output_aliases={a: 2 + a for a in range(2 * n)},
        compiler_params=pltpu.CompilerParams(has_side_effects=pltpu.SideEffectType.DATAFLOW_SIDE_EFFECTING),
    )(*[pltpu.with_memory_space_constraint(a, pltpu.HBM) for a in list(arrs) + lands], after)
    return res[0], res[1], res[2:2 + n], res[2 + n:2 + 2 * n], res[-1]


def exchange_wait(started, gather, after, name):
    send_sems, recv_sems, srcs, lands, _ = started
    n = len(srcs)
    after = list(after) if isinstance(after, (list, tuple)) else [after]

    def body(*refs):
        ins, land = refs[:n], refs[n:2 * n]
        send_sems, recv_sems = refs[2 * n], refs[2 * n + 1]
        x, y, c = lax.axis_index("x"), lax.axis_index("y"), lax.axis_index("c")
        for k, pid, peer in _peers(x, y, c):
            for a in range(n):
                cp = pltpu.make_async_remote_copy(
                    src_ref=ins[a] if gather else ins[a].at[peer], dst_ref=land[a].at[peer],
                    send_sem=send_sems.at[a * (N_DEV - 1) + k - 1], recv_sem=recv_sems.at[a * (N_DEV - 1) + k - 1],
                    device_id=pid, device_id_type=pl.DeviceIdType.MESH)
                cp.wait_send()
                cp.wait_recv()

    hbm = pl.BlockSpec(memory_space=pltpu.HBM)
    sem = pl.BlockSpec(memory_space=pltpu.SEMAPHORE)
    res = pl.pallas_call(
        body, name=name, in_specs=[hbm] * (2 * n) + [sem, sem] + [ANY] * len(after),
        out_specs=[hbm] * (2 * n),
        out_shape=[pltpu.HBM(a.shape, a.dtype) for a in list(srcs) + list(lands)],
        input_output_aliases={a: a for a in range(2 * n)},
        compiler_params=pltpu.CompilerParams(has_side_effects=pltpu.SideEffectType.DATAFLOW_SIDE_EFFECTING),
    )(*srcs, *lands, send_sems, recv_sems, *after)
    return res[n:]


def _adamw(w, g, m, v):
    m = ADAM_B1 * m + (1.0 - ADAM_B1) * g
    v = ADAM_B2 * v + (1.0 - ADAM_B2) * (g * g)
    m_hat = m / (1.0 - ADAM_B1 ** ADAM_STEP)
    v_hat = v / (1.0 - ADAM_B2 ** ADAM_STEP)
    return -ADAM_LR * (m_hat / (jnp.sqrt(v_hat) + ADAM_EPS) + ADAM_WD * w), m, v


def _update_tile_rows(rows, cols):
    if rows * cols <= UPDATE_TILE_ELEMS:
        return rows
    return max(t for t in range(8, UPDATE_TILE_ELEMS // cols + 1, 8) if rows % t == 0)


def _sum_in_order(p_ref):
    g = p_ref[0]
    for j in range(1, p_ref.shape[0]):
        g = g + p_ref[j]
    return g


def sum_parts(parts, name):
    nparts, rows, cols = parts.shape
    tr = _update_tile_rows(rows, cols)

    def body(p_ref, g_ref):
        g_ref[...] = _sum_in_order(p_ref)

    return pl.pallas_call(
        body, name=name, grid=(rows // tr,),
        in_specs=[pl.BlockSpec((nparts, tr, cols), lambda i: (0, i, 0))],
        out_specs=pl.BlockSpec((tr, cols), lambda i: (i, 0)), out_shape=jax.ShapeDtypeStruct((rows, cols), F32),
        compiler_params=_params(),
    )(parts)


def sum_adamw(parts, w, m, v, name):
    rows, cols = w.shape
    nparts = parts.shape[0]
    tr = _update_tile_rows(rows, cols)

    def body(p_ref, w_ref, m_ref, v_ref, g_ref, d_ref, nm_ref, nv_ref):
        g = _sum_in_order(p_ref)
        delta, nm, nv = _adamw(w_ref[...], g, m_ref[...], v_ref[...])
        g_ref[...] = g
        d_ref[...] = delta
        nm_ref[...] = nm
        nv_ref[...] = nv

    blk = pl.BlockSpec((tr, cols), lambda i: (i, 0))
    sd = jax.ShapeDtypeStruct((rows, cols), F32)
    return pl.pallas_call(
        body, name=name, grid=(rows // tr,),
        in_specs=[pl.BlockSpec((nparts, tr, cols), lambda i: (0, i, 0)), blk, blk, blk],
        out_specs=[blk] * 4, out_shape=(sd,) * 4,
        compiler_params=_params(),
    )(parts, w, m, v)


S5_NAMES = ("w_in", "lam_re", "lam_im", "log_dt", "b_re", "b_im", "c_re", "c_im", "d_skip", "w_glu", "b_glu", "w_out")
CONV_NAMES = ("w_in", "conv_w", "conv_b", "w_out")
POOL_NAMES = ("w_in", "w_grp", "b_grp", "scale", "w_out")
LAYER_KINDS = ("s5", "conv", "pool", "s5")
LAYER_NAMES = {"s5": S5_NAMES, "conv": CONV_NAMES, "pool": POOL_NAMES}
SHARDED = {"s5": ("w_in", "w_glu", "w_out"), "conv": ("w_in", "conv_w", "w_out"), "pool": ("w_in", "w_grp", "b_grp", "w_out")}
GATHER_F32 = ("conv_w", "b_grp")


def weight_names():
    names = ["meta_tokens"]
    for i, kind in enumerate(LAYER_KINDS):
        names.append("norm%d_g" % i)
        names += ["l%d_%s" % (i, n) for n in LAYER_NAMES[kind]]
    names.append("final_g")
    return names


def sharded_names():
    return ["meta_tokens"] + ["l%d_%s" % (i, n) for i, kind in enumerate(LAYER_KINDS) for n in SHARDED[kind]]


def _block_diag_in(bb_t, gc):
    i, g, p = bb_t.shape
    t = bb_t.astype(BF16).reshape(i, 4, gc, p)
    return jnp.einsum("icjp,jk->cjikp", t, jnp.eye(gc, dtype=BF16)).reshape(4, gc * i, gc * p)


def _block_diag_in_grad(blocks):
    _, gc, i, p = blocks.shape
    return jnp.transpose(blocks, (2, 0, 1, 3)).reshape(i, 4 * gc, p)


def _block_diag_out(cc, gc):
    g, i, p = cc.shape
    return jnp.einsum("cjip,jk->cjpki", cc.astype(BF16).reshape(4, gc, i, p), jnp.eye(gc, dtype=BF16)).reshape(4, gc * p, gc * i)


def _block_diag_out_grad(blocks):
    _, gc, i, p = blocks.shape
    return blocks.reshape(4 * gc, i, p)


def _to_owner_blocks(a, axis):
    shape = a.shape[:axis] + (N_DEV, a.shape[axis] // N_DEV) + a.shape[axis + 1:]
    return jnp.moveaxis(a.reshape(shape), axis, 0)


def _from_owner_blocks(a, axis):
    a = jnp.moveaxis(a, 0, axis)
    return a.reshape(a.shape[:axis] + (a.shape[axis] * a.shape[axis + 1],) + a.shape[axis + 2:])


def _step(x, target, weights, moments_m, moments_v):
    seq, d = x.shape[1], x.shape[2]
    n_meta = weights["meta_tokens"].shape[0]
    tt = TOKEN_TILE
    pad_tiles = -(-n_meta // tt)
    p0 = pad_tiles * tt
    lp = p0 + seq
    first_pos = p0 - n_meta
    gc = d // 4 // S5_GROUP
    cw = d // 4

    big_names = [n for n in sharded_names() if n != "meta_tokens" and n.split("_", 1)[1] not in GATHER_F32]
    small_names = [n for n in sharded_names() if n not in big_names]
    layer_big = [[n for n in big_names if n.startswith("l%d_" % i)] for i in range(len(LAYER_KINDS))]
    layer_big[0] = small_names + layer_big[0]
    gather_started = []
    after = jnp.zeros((8, 128), F32)
    for i, names in enumerate(layer_big):
        gather_started.append(exchange_start([weights[n] if n in small_names else weights[n].astype(BF16) for n in names], True,
                                             after, "gather_start_l%d" % i))
        after = gather_started[-1][4]

    def vec(name):
        return weights[name].reshape(1, -1)

    s5_prep = {}
    for i, kind in enumerate(LAYER_KINDS):
        if kind == "s5":
            p = "l%d_" % i
            lr, li = weights[p + "lam_re"], weights[p + "lam_im"] + after[0, 0]
            ldt = weights[p + "log_dt"].reshape(-1, 1)
            br_t = jnp.transpose(weights[p + "b_re"], (2, 0, 1))
            bi_t = jnp.transpose(weights[p + "b_im"], (2, 0, 1))
            ar, ai, bbr, bbi = s5_disc_fwd(lr, li, ldt, br_t, bi_t, p + "disc_fwd")
            s5_prep[i] = dict(
                disc=(lr, li, ldt, br_t, bi_t), ar=ar.reshape(4, -1, 128), ai=ai.reshape(4, -1, 128),
                bdre=_block_diag_in(bbr, gc), bdim=_block_diag_in(bbi, gc),
                cdre=_block_diag_out(weights[p + "c_re"], gc), cdim=_block_diag_out(-weights[p + "c_im"], gc),
                d_skip=weights[p + "d_skip"].reshape(4, 1, cw), b_glu=vec(p + "b_glu"))
    h = jnp.concatenate([jnp.zeros((p0, d), F32), x[0] + after[0, 0]], axis=0)

    prepared = [h] + [s5_prep[i][k] for i in s5_prep for k in ("bdre", "bdim", "cdre", "cdim")]
    gathered = dict(zip(layer_big[0], exchange_wait(gather_started[0], True, prepared, "gather_wait_l0")))
    h = lax.dynamic_update_slice(h, _from_owner_blocks(gathered["meta_tokens"], 1), (first_pos, 0))

    full = {}

    def layer_weights(i, kind, after):
        p = "l%d_" % i
        if i > 0:
            gathered.update(zip(layer_big[i], exchange_wait(gather_started[i], True, after, "gather_wait_l%d" % i)))
        w_in = gathered[p + "w_in"]
        if kind == "s5":
            full[i] = dict(s5_prep[i], w_in=w_in, w_glu=gathered[p + "w_glu"].reshape(4, cw, d),
                           w_out=gathered[p + "w_out"].reshape(4, cw, d))
        elif kind == "conv":
            ce = w_in.shape[2]
            nch = 2
            conv_w = _from_owner_blocks(gathered[p + "conv_w"], 1)
            full[i] = dict(
                w_in=w_in, conv_w=jnp.transpose(conv_w.reshape(CONV_K, nch, ce), (1, 0, 2)),
                conv_b=weights[p + "conv_b"].reshape(nch, 1, ce), w_out=gathered[p + "w_out"].reshape(nch, ce, d))
        else:
            gw = w_in.shape[2]
            full[i] = dict(
                w_in=w_in, w_grp=_from_owner_blocks(gathered[p + "w_grp"], 1),
                b_grp=_from_owner_blocks(gathered[p + "b_grp"], 1).reshape(4, 1, gw),
                scale=weights[p + "scale"].reshape(4, 1, gw), w_out=gathered[p + "w_out"].reshape(4, gw, d))
        return full[i]

    saved = {}
    for i, kind in enumerate(LAYER_KINDS):
        p, f, g = "l%d_" % i, layer_weights(i, kind, h), vec("norm%d_g" % i)
        if kind == "s5":
            u, z, xs = s5_fwd1(h, g, f["w_in"], f["bdre"], f["bdim"], p + "fwd_in")
            s = s5_scan_fwd(xs, f["ar"], f["ai"], p + "scan_fwd")
            h_in = h
            h, y, q = s5_fwd3(s, u, z, h, f["cdre"], f["cdim"], f["w_glu"], f["w_out"], f["d_skip"], f["b_glu"], p + "fwd_out")
            saved[i] = (h_in, u, z, s, y, q)
        elif kind == "conv":
            h_new, halos = conv_fwd(h, g, f["w_in"], f["conv_w"], f["conv_b"], f["w_out"], p + "fwd")
            saved[i] = (h, halos)
            h = h_new
        else:
            h_new, halos = pool_fwd(h, g, f["w_in"], f["w_grp"], f["b_grp"], f["scale"], f["w_out"], first_pos, p + "fwd")
            saved[i] = (h, halos)
            h = h_new

    dh, dg_final, loss_tile = loss_head(h, target[0], vec("final_g"), pad_tiles, "loss_head")
    loss = lax.psum(loss_tile[0, 0], ("x", "y", "c"))

    grads = {"final_g": dg_final}
    names = weight_names()
    sh_names = sharded_names()
    rep_names = [n for n in names if n not in sh_names]

    def owner_blocks(a):
        return a.reshape(N_DEV, -1, a.shape[-1])

    def as2d(a):
        return a.reshape(-1, a.shape[-1])

    def pack(tree):
        flat = [jnp.pad(tree[n].reshape(-1), (0, -tree[n].size % 1024)) for n in rep_names]
        flat = jnp.concatenate(flat)
        return jnp.pad(flat, (0, -flat.size % (PACK_ROWS * 128))).reshape(-1, 128)

    layer_sharded, scatter_started = {}, {}
    ordered = jnp.zeros((), F32)
    for i in reversed(range(len(LAYER_KINDS))):
        kind = LAYER_KINDS[i]
        p, f, g = "l%d_" % i, full[i], vec("norm%d_g" % i) + ordered
        if kind == "s5":
            h_in, u, z, s, y, q = saved[i]
            dy, dp, dwo, dwg, dbg = s5_bwd3a(dh, y, q, z, f["w_glu"], f["w_out"], f["b_glu"] + ordered, p + "bwd_out")
            d_skip = f["d_skip"]
            if i == 0:
                early_names = [p + "w_glu", p + "w_out"]
                scatter_started["early"] = exchange_start([dwg.reshape(N_DEV, -1, d), dwo.reshape(N_DEV, -1, d)], False, dy,
                                                          "scatter_start_l0_early")
                d_skip = d_skip + scatter_started["early"][4][0, 0]
            ds, dus, dcre, dcim, dd = s5_bwd3b(dy, s, u, f["cdre"], f["cdim"], d_skip, p + "bwd_read")
            lam, dar, dai = s5_scan_bwd(ds, s, f["ar"], f["ai"], p + "scan_bwd")
            dp, dh, n, dbre, dbim, dg = s5_bwd1(lam, dus, u, dp, h_in, dh, g, f["w_in"], f["bdre"], f["bdim"], p + "bwd_in")
            dw_in = grad_w_in(n, dp, f["w_in"].shape[2], p + "grad_w_in")
            grads.update({p + "w_in": dw_in, p + "w_glu": dwg.reshape(N_DEV, -1, d), p + "w_out": dwo.reshape(N_DEV, -1, d),
                          p + "d_skip": dd, p + "b_glu": dbg})

            def replicated_grads(p=p, f=f, dar=dar, dai=dai, dbre=dbre, dbim=dbim, dcre=dcre, dcim=dcim, token=None):
                lr, li, ldt, br_t, bi_t = f["disc"]
                dlr, dli, dldt, dbr_t, dbi_t = s5_disc_bwd(
                    lr, li, ldt, br_t, bi_t, dar.reshape(lr.shape) + token, dai.reshape(lr.shape),
                    _block_diag_in_grad(dbre), _block_diag_in_grad(dbim), p + "disc_bwd")
                grads.update({
                    p + "lam_re": dlr, p + "lam_im": dli, p + "log_dt": dldt,
                    p + "b_re": jnp.transpose(dbr_t, (1, 2, 0)), p + "b_im": jnp.transpose(dbi_t, (1, 2, 0)),
                    p + "c_re": _block_diag_out_grad(dcre), p + "c_im": -_block_diag_out_grad(dcim)})
        elif kind == "conv":
            replicated_grads = None
            h_in, halos = saved[i]
            dh, n, dp, dwo, dcw, dcb, dg = conv_bwd(h_in, dh, halos, g, f["w_in"], f["conv_w"], f["conv_b"], f["w_out"], p + "bwd")
            dw_in = grad_w_in(n, dp, f["w_in"].shape[2], p + "grad_w_in")
            dconv_w = jnp.transpose(dcw[:, :CONV_K, :], (1, 0, 2)).reshape(CONV_K, -1)
            grads.update({p + "w_in": dw_in, p + "conv_w": _to_owner_blocks(dconv_w, 1), p + "conv_b": dcb,
                          p + "w_out": dwo.reshape(N_DEV, -1, d)})
        else:
            replicated_grads = None
            h_in, halos = saved[i]
            dh, n, dp, dwo, dwgrp, dbgrp, dsc, dg = pool_bwd(h_in, dh, halos, g, f["w_in"], f["w_grp"], f["b_grp"], f["scale"],
                                                             f["w_out"], first_pos, p + "bwd")
            dw_in = grad_w_in(n, dp, f["w_in"].shape[2], p + "grad_w_in")
            grads.update({p + "w_in": dw_in, p + "w_grp": _to_owner_blocks(dwgrp, 1),
                          p + "b_grp": _to_owner_blocks(dbgrp.reshape(4, -1), 1), p + "scale": dsc,
                          p + "w_out": dwo.reshape(N_DEV, -1, d)})
        grads["norm%d_g" % i] = dg
        layer_sharded[i] = ["l%d_%s" % (i, n) for n in SHARDED[kind]]
        if i > 0:
            scatter_started[i] = exchange_start([owner_blocks(grads[n]) for n in layer_sharded[i]], False, dh,
                                                "scatter_start_l%d" % i)
            ordered = scatter_started[i][4][0, 0]
        if replicated_grads is not None:
            replicated_grads(token=ordered)
    grad_x = dh[p0:][None]
    grads["meta_tokens"] = _to_owner_blocks(dh[first_pos:p0], 1)
    last = len(LAYER_KINDS)
    layer_sharded[last] = ["meta_tokens", "replicated"]
    scatter_started[last] = exchange_start([owner_blocks(grads["meta_tokens"]), pack(grads).reshape(N_DEV, -1, 128)], False,
                                           dh, "scatter_start_replicated")
    layer_sharded["early"] = early_names
    layer_sharded[0] = [n for n in layer_sharded[0] if n not in early_names]

    out = {}
    received = {}
    after = [scatter_started[last][4]]
    for i in list(reversed(range(1, last))) + [last, "early", 0]:
        received.update(zip(layer_sharded[i], exchange_wait(scatter_started[i], False, after, "scatter_wait_%s" % i)))
        for n in layer_sharded[i]:
            if n != "replicated":
                res = sum_adamw(received[n], as2d(weights[n]), as2d(moments_m[n]), as2d(moments_v[n]), "update_" + n)
                out[n] = [r.reshape(weights[n].shape) for r in res]
                after = [out[n][0]]
        if i == last:
            g_full = exchange([sum_parts(received["replicated"], "sum_replicated")], True, "gather_small_grads")[0]
            scatter_started[0] = exchange_start([owner_blocks(grads[n]) for n in layer_sharded[0]], False, g_full,
                                                "scatter_start_l0")
            g_full = g_full + scatter_started[0][4][0, 0]
            packed = sum_adamw(g_full.reshape(1, -1, 128), pack(weights), pack(moments_m), pack(moments_v), "update_replicated")
            offset = 0
            for n in rep_names:
                size = weights[n].size
                out[n] = [r.reshape(-1)[offset:offset + size].reshape(weights[n].shape) for r in packed]
                offset += size + (-size % 1024)
            after = [packed[0]] + [out[n][k] for n in rep_names for k in range(4)]

    return (loss, grad_x) + tuple(out[n][k] for k in range(4) for n in names)


def kernel(x, *rest):
    names = weight_names()
    nw = len(names)
    weights = dict(zip(names, rest[:nw]))
    target = rest[nw]
    moments_m = dict(zip(names, rest[nw + 1:2 * nw + 1]))
    moments_v = dict(zip(names, rest[2 * nw + 1:3 * nw + 1]))
    return _step(x, target, weights, moments_m, moments_v)
```

```python
import functools
import math

import jax
import jax.numpy as jnp
from jax import lax
from jax.experimental import pallas as pl
from jax.experimental.pallas import tpu as pltpu

F32 = jnp.float32
BF16 = jnp.bfloat16
EPS = 1e-6
N_DEV = 8
TOKEN_TILE = 256
SCAN_CHUNKS = 4
S5_GROUP = 16
S5_STATE = 64
POOL_WINDOWS = (2, 4, 8, 16)
POOL_HALO = 16
CONV_K = 3
CONV_HALO = 8
ADAM_LR = 0.001
ADAM_B1 = 0.9
ADAM_B2 = 0.999
ADAM_EPS = 1e-08
ADAM_WD = 0.01
ADAM_STEP = 10
GELU_C = math.sqrt(2.0 / math.pi)
GELU_A = 0.044715
UPDATE_TILE_ELEMS = 1 << 17
PACK_ROWS = 512
VMEM_LIMIT = 56 << 20

ANY = pl.BlockSpec(memory_space=pl.ANY)


def _params(vmem=VMEM_LIMIT, ndim=1):
    return pltpu.CompilerParams(vmem_limit_bytes=vmem, dimension_semantics=("arbitrary",) * ndim)


def _dot(a, b):
    return jnp.dot(a.astype(BF16), b.astype(BF16), preferred_element_type=F32)


def _dot_nt(a, b):
    return lax.dot_general(a.astype(BF16), b.astype(BF16), (((1,), (1,)), ((), ())), preferred_element_type=F32)


def _dot_tn(a, b):
    return lax.dot_general(a.astype(BF16), b.astype(BF16), (((0,), (0,)), ((), ())), preferred_element_type=F32)


def _rms_fwd(h, g):
    r = lax.rsqrt(jnp.mean(h * h, axis=-1, keepdims=True) + EPS)
    hh = h * r
    return hh * g, hh, r


def _rms_bwd(dn, hh, r, g):
    dhh = dn * g
    return r * (dhh - hh * jnp.mean(dhh * hh, axis=-1, keepdims=True))


def _sigmoid(x):
    return 1.0 / (1.0 + jnp.exp(-x))


def _silu_and_grad(z):
    s = _sigmoid(z)
    return z * s, s * (1.0 + z * (1.0 - s))


def _gelu(y):
    t = jnp.tanh(GELU_C * (y + GELU_A * y * y * y))
    return 0.5 * y * (1.0 + t), t


def _gelu_grad(y, t):
    return 0.5 * (1.0 + t) + 0.5 * y * (1.0 - t * t) * GELU_C * (1.0 + 3.0 * GELU_A * y * y)


def _rows(shape):
    return lax.broadcasted_iota(jnp.int32, shape, 0)


def _shift_down(x, k, halo):
    y = pltpu.roll(x, k, 0)
    rows = _rows(x.shape)
    for j in range(k):
        y = jnp.where(rows == j, halo[halo.shape[0] - k + j:halo.shape[0] - k + j + 1, :], y)
    return y


def _shift_up(x, k, halo):
    n = x.shape[0]
    y = pltpu.roll(x, n - k, 0)
    rows = _rows(x.shape)
    for j in range(k):
        y = jnp.where(rows == n - k + j, halo[j:j + 1, :], y)
    return y


def _window_sums_back(ext):
    out = []
    s = ext
    for k in (1, 2, 4, 8):
        s = s + pltpu.roll(s, k, 0)
        out.append(s)
    return out


def _window_sums_fwd(ext):
    n = ext.shape[0]
    out = []
    s = ext
    for k in (1, 2, 4, 8):
        s = s + pltpu.roll(s, n - k, 0)
        out.append(s)
    return out


def _pool_inv_count(tile, tt, first_pos, w, width):
    pos = _rows((tt, width)) + (tile * tt - first_pos + 1)
    return 1.0 / jnp.clip(pos, 1, w).astype(F32)


def _slab_spec(lp, tt, sw, index_map):
    nj = sw // 128
    return pl.BlockSpec((tt * nj, 128), index_map), (lp * 4 * nj, 128)


def _pack_pair(re, im):
    def rounded(v):
        return lax.bitcast_convert_type(v, jnp.int32) + 0x8000
    return lax.bitcast_convert_type((rounded(re) & -65536) | lax.shift_right_logical(rounded(im), 16), F32)


def _unpack_pair(w):
    b = lax.bitcast_convert_type(w, jnp.int32)
    return lax.bitcast_convert_type(b & -65536, F32), lax.bitcast_convert_type(lax.shift_left(b, 16), F32)


def _slab_load(ref):
    nj = ref.shape[0] // TOKEN_TILE
    return _unpack_pair(jnp.concatenate([ref[pl.ds(j, TOKEN_TILE, stride=nj), :] for j in range(nj)], axis=1))


def _slab_store(ref, re, im):
    nj = ref.shape[0] // TOKEN_TILE
    val = _pack_pair(re, im)
    for j in range(nj):
        ref[pl.ds(j, TOKEN_TILE, stride=nj), :] = val[:, j * 128:(j + 1) * 128]


def _s5_disc_math(lr, li, ldt, br, bi):
    dt = jnp.exp(ldt)
    mag = jnp.exp(lr * dt)
    ar = mag * jnp.cos(li * dt)
    ai = mag * jnp.sin(li * dt)
    den = lr * lr + li * li
    kr = ((ar - 1.0) * lr + ai * li) / den
    ki = (ai * lr - (ar - 1.0) * li) / den
    bbr = kr[None] * br - ki[None] * bi
    bbi = kr[None] * bi + ki[None] * br
    return ar, ai, bbr, bbi


def s5_disc_fwd(lr, li, ldt, br_t, bi_t, name):
    def body(lr_ref, li_ref, ldt_ref, br_ref, bi_ref, ar_ref, ai_ref, bbr_ref, bbi_ref):
        ar, ai, bbr, bbi = _s5_disc_math(lr_ref[...], li_ref[...], ldt_ref[...], br_ref[...], bi_ref[...])
        ar_ref[...] = ar
        ai_ref[...] = ai
        bbr_ref[...] = bbr
        bbi_ref[...] = bbi

    sd = jax.ShapeDtypeStruct
    return pl.pallas_call(
        body, name=name,
        out_shape=(sd(lr.shape, F32), sd(lr.shape, F32), sd(br_t.shape, F32), sd(br_t.shape, F32)),
    )(lr, li, ldt, br_t, bi_t)


def s5_disc_bwd(lr, li, ldt, br_t, bi_t, dar, dai, dbbr, dbbi, name):
    def body(lr_ref, li_ref, ldt_ref, br_ref, bi_ref, dar_ref, dai_ref, dbbr_ref, dbbi_ref,
             dlr_ref, dli_ref, dldt_ref, dbr_ref, dbi_ref):
        _, vjp = jax.vjp(_s5_disc_math, lr_ref[...], li_ref[...], ldt_ref[...], br_ref[...], bi_ref[...])
        dlr, dli, dldt, dbr, dbi = vjp((dar_ref[...], dai_ref[...], dbbr_ref[...], dbbi_ref[...]))
        dlr_ref[...] = dlr
        dli_ref[...] = dli
        dldt_ref[...] = dldt
        dbr_ref[...] = dbr
        dbi_ref[...] = dbi

    sd = jax.ShapeDtypeStruct
    return pl.pallas_call(
        body, name=name,
        out_shape=(sd(lr.shape, F32), sd(lr.shape, F32), sd(ldt.shape, F32), sd(br_t.shape, F32), sd(br_t.shape, F32)),
    )(lr, li, ldt, br_t, bi_t, dar, dai, dbbr, dbbi)


def s5_fwd1(h, g, w_in, bdre, bdim, name):
    lp, d = h.shape
    tt = TOKEN_TILE
    cw, sw = bdre.shape[1], bdre.shape[2]

    def body(h_ref, g_ref, w_hbm, bdre_hbm, bdim_hbm, u_ref, z_ref, x_ref, w, bre, bim, n_sc):
        i, c = pl.program_id(0), pl.program_id(1)

        @pl.when((i == 0) & (c == 0))
        def _():
            pltpu.sync_copy(w_hbm, w)
            pltpu.sync_copy(bdre_hbm, bre)
            pltpu.sync_copy(bdim_hbm, bim)

        @pl.when(c == 0)
        def _():
            n_sc[...] = _rms_fwd(h_ref[...], g_ref[...])[0].astype(BF16)

        n = n_sc[...]
        u = jnp.dot(n, w[c], preferred_element_type=F32)
        u_ref[...] = u
        z_ref[...] = jnp.dot(n, w[c + 4], preferred_element_type=F32)
        ub = u.astype(BF16)
        _slab_store(x_ref, jnp.dot(ub, bre[c], preferred_element_type=F32), jnp.dot(ub, bim[c], preferred_element_type=F32))

    sd = jax.ShapeDtypeStruct
    slab, slab_shape = _slab_spec(lp, tt, sw, lambda i, c: (i * 4 + c, 0))
    return pl.pallas_call(
        body, name=name, grid=(lp // tt, 4),
        in_specs=[pl.BlockSpec((tt, d), lambda i, c: (i, 0)), pl.BlockSpec((1, d), lambda i, c: (0, 0)), ANY, ANY, ANY],
        out_specs=[pl.BlockSpec((tt, cw), lambda i, c: (i, c)), pl.BlockSpec((tt, cw), lambda i, c: (i, c)), slab],
        out_shape=(sd((lp, d), F32), sd((lp, d), F32), sd(slab_shape, F32)),
        scratch_shapes=[pltpu.VMEM(w_in.shape, BF16), pltpu.VMEM(bdre.shape, BF16), pltpu.VMEM(bdim.shape, BF16),
                        pltpu.VMEM((tt, d), BF16)],
        compiler_params=_params(ndim=2),
    )(h, g, w_in, bdre, bdim)


def s5_scan_fwd(x, ar, ai, name):
    nj = ar.shape[1]
    tt = TOKEN_TILE
    cpb = SCAN_CHUNKS
    nt = x.shape[0] // (4 * tt * nj)

    def body(x_ref, ar_ref, ai_ref, s_ref, st_r, st_i):
        i, cg = pl.program_id(0), pl.program_id(1)

        @pl.when(i == 0)
        def _():
            for q in range(cpb):
                st_r[cg * cpb + q] = jnp.zeros((nj, 128), F32)
                st_i[cg * cpb + q] = jnp.zeros((nj, 128), F32)

        a_r = [ar_ref[cg * cpb + q] for q in range(cpb)]
        a_i = [ai_ref[cg * cpb + q] for q in range(cpb)]

        def step(t, carry):
            out = []
            for q in range(cpb):
                s_r, s_i = carry[q]
                rows = pl.ds(pl.multiple_of((q * tt + t) * nj, nj), nj)
                x_r, x_i = _unpack_pair(x_ref[rows, :])
                n_r = a_r[q] * s_r - a_i[q] * s_i + x_r
                n_i = a_r[q] * s_i + a_i[q] * s_r + x_i
                s_ref[rows, :] = _pack_pair(n_r, n_i)
                out.append((n_r, n_i))
            return tuple(out)

        init = tuple((st_r[cg * cpb + q], st_i[cg * cpb + q]) for q in range(cpb))
        final = lax.fori_loop(0, tt, step, init, unroll=8)
        for q in range(cpb):
            st_r[cg * cpb + q] = final[q][0]
            st_i[cg * cpb + q] = final[q][1]

    blk = pl.BlockSpec((cpb * tt * nj, 128), lambda i, cg: (i * (4 // cpb) + cg, 0))
    par = pl.BlockSpec((4, nj, 128), lambda i, cg: (0, 0, 0))
    sd = jax.ShapeDtypeStruct
    return pl.pallas_call(
        body, name=name, grid=(nt, 4 // cpb),
        in_specs=[blk, par, par], out_specs=blk,
        out_shape=sd(x.shape, F32),
        scratch_shapes=[pltpu.VMEM((4, nj, 128), F32), pltpu.VMEM((4, nj, 128), F32)],
        compiler_params=_params(ndim=2),
    )(x, ar, ai)


def s5_fwd3(s, u, z, h, cdre, cdim, w_glu, w_out, d_skip, b_glu, name):
    lp, d = h.shape
    tt = TOKEN_TILE
    sw, cw = cdre.shape[1], cdre.shape[2]

    def body(s_ref, u_ref, z_ref, h_ref, d_ref, bg_ref, cre_hbm, cim_hbm, wg_hbm, wo_hbm,
             o_ref, y_ref, q_ref, cre, cim, wg, wo, gy_sc):
        i, c = pl.program_id(0), pl.program_id(1)

        @pl.when((i == 0) & (c == 0))
        def _():
            pltpu.sync_copy(cre_hbm, cre)
            pltpu.sync_copy(cim_hbm, cim)
            pltpu.sync_copy(wg_hbm, wg)
            pltpu.sync_copy(wo_hbm, wo)

        s_r, s_i = _slab_load(s_ref)
        y = _dot(s_r, cre[c]) + _dot(s_i, cim[c]) + d_ref[c] * u_ref[...]
        y_ref[...] = y
        gy = _gelu(y)[0]
        gy_sc[c] = gy
        part = _dot(gy, wg[c])

        @pl.when(c == 0)
        def _():
            q_ref[...] = part

        @pl.when(c > 0)
        def _():
            q_ref[...] += part

        @pl.when(c == 3)
        def _():
            sig = _sigmoid(q_ref[...] + bg_ref[...])
            zz = z_ref[...]
            sz = zz * _sigmoid(zz)
            o = h_ref[...]
            for k in range(4):
                cols = slice(k * cw, (k + 1) * cw)
                o = o + _dot(gy_sc[k] * sig[:, cols] * sz[:, cols], wo[k])
            o_ref[...] = o

    row = lambda i, c: (i, 0)
    chunk = lambda i, c: (i, c)
    slab, _ = _slab_spec(lp, tt, sw, lambda i, c: (i * 4 + c, 0))
    sd = jax.ShapeDtypeStruct((lp, d), F32)
    return pl.pallas_call(
        body, name=name, grid=(lp // tt, 4),
        in_specs=[slab, pl.BlockSpec((tt, cw), chunk),
                  pl.BlockSpec((tt, d), row), pl.BlockSpec((tt, d), row),
                  pl.BlockSpec((4, 1, cw), lambda i, c: (0, 0, 0)), pl.BlockSpec((1, d), lambda i, c: (0, 0)),
                  ANY, ANY, ANY, ANY],
        out_specs=[pl.BlockSpec((tt, d), row), pl.BlockSpec((tt, cw), chunk), pl.BlockSpec((tt, d), row)],
        out_shape=(sd, sd, sd),
        scratch_shapes=[pltpu.VMEM(cdre.shape, BF16), pltpu.VMEM(cdim.shape, BF16), pltpu.VMEM(w_glu.shape, BF16),
                        pltpu.VMEM(w_out.shape, BF16), pltpu.VMEM((4, tt, cw), F32)],
        compiler_params=_params(ndim=2),
    )(s, u, z, h, d_skip, b_glu, cdre, cdim, w_glu, w_out)


def s5_bwd3a(dh, y, q, z, w_glu, w_out, b_glu, name):
    lp, d = dh.shape
    tt = TOKEN_TILE
    nt = lp // tt
    cw = w_glu.shape[1]

    def body(dh_ref, y_ref, q_ref, z_ref, bg_ref, wg_hbm, wo_hbm, dy_ref, dp_ref, dwo_hbm, dwg_hbm, dbg_hbm,
             wg, wo, dwo, dwg, dbg):
        i = pl.program_id(0)

        @pl.when(i == 0)
        def _():
            pltpu.sync_copy(wg_hbm, wg)
            pltpu.sync_copy(wo_hbm, wo)
            dwo[...] = jnp.zeros_like(dwo)
            dwg[...] = jnp.zeros_like(dwg)
            dbg[...] = jnp.zeros_like(dbg)

        sig = _sigmoid(q_ref[...] + bg_ref[...])
        sz, dsz = _silu_and_grad(z_ref[...])
        dhv = dh_ref[...]
        yv = y_ref[...]
        gy, t = _gelu(yv)
        dq_parts, dgy_parts = [], []
        for k in range(4):
            cols = slice(k * cw, (k + 1) * cw)
            gy_k, sig_k, sz_k = gy[:, cols], sig[:, cols], sz[:, cols]
            y2 = gy_k * sig_k
            dy3 = _dot_nt(dhv, wo[k])
            dwo[k] += _dot_tn(y2 * sz_k, dhv)
            dy2 = dy3 * sz_k
            dp_ref[0, :, cols] = (dy3 * y2 * dsz[:, cols]).astype(BF16)
            dq_parts.append(dy2 * gy_k * sig_k * (1.0 - sig_k))
            dgy_parts.append(dy2 * sig_k)
        dq = jnp.concatenate(dq_parts, axis=1)
        dbg[...] += jnp.sum(dq, axis=0, keepdims=True)
        dgelu = _gelu_grad(yv, t)
        for k in range(4):
            cols = slice(k * cw, (k + 1) * cw)
            dwg[k] += _dot_tn(gy[:, cols], dq)
            dy_ref[:, cols] = (dgy_parts[k] + _dot_nt(dq, wg[k])) * dgelu[:, cols]

        @pl.when(i == nt - 1)
        def _():
            pltpu.sync_copy(dwo, dwo_hbm)
            pltpu.sync_copy(dwg, dwg_hbm)
            pltpu.sync_copy(dbg, dbg_hbm)

    row = pl.BlockSpec((tt, d), lambda i: (i, 0))
    sd = jax.ShapeDtypeStruct
    return pl.pallas_call(
        body, name=name, grid=(nt,),
        in_specs=[row, row, row, row, pl.BlockSpec((1, d), lambda i: (0, 0)), ANY, ANY],
        out_specs=[row, pl.BlockSpec((1, tt, d), lambda i: (1, i, 0)), ANY, ANY, ANY],
        out_shape=(sd((lp, d), F32), sd((2, lp, d), BF16), sd(w_out.shape, F32), sd(w_glu.shape, F32), sd((1, d), F32)),
        scratch_shapes=[pltpu.VMEM(w_glu.shape, BF16), pltpu.VMEM(w_out.shape, BF16),
                        pltpu.VMEM(w_out.shape, F32), pltpu.VMEM(w_glu.shape, F32), pltpu.VMEM((1, d), F32)],
        compiler_params=_params(),
    )(dh, y, q, z, b_glu, w_glu, w_out)


def s5_bwd3b(dy, s, u, cdre, cdim, d_skip, name):
    lp, d = dy.shape
    tt = TOKEN_TILE
    nt = lp // tt
    sw, cw = cdre.shape[1], cdre.shape[2]
    gc = cw // S5_GROUP

    def body(dy_ref, s_ref, u_ref, d_ref, cre_hbm, cim_hbm,
             ds_ref, dus_ref, dcre_ref, dcim_ref, dd_hbm, cre, cim, dcre, dcim, dd):
        i, c = pl.program_id(0), pl.program_id(1)

        @pl.when((i == 0) & (c == 0))
        def _():
            pltpu.sync_copy(cre_hbm, cre)
            pltpu.sync_copy(cim_hbm, cim)
            dcre[...] = jnp.zeros_like(dcre)
            dcim[...] = jnp.zeros_like(dcim)
            dd[...] = jnp.zeros_like(dd)

        dyv = dy_ref[...]
        dd[c] += jnp.sum(dyv * u_ref[...], axis=0, keepdims=True)
        dus_ref[...] = dyv * d_ref[c]
        _slab_store(ds_ref, _dot_nt(dyv, cre[c]), _dot_nt(dyv, cim[c]))
        s_r, s_i = _slab_load(s_ref)
        dcre[c] += _dot_tn(s_r, dyv)
        dcim[c] += _dot_tn(s_i, dyv)

        @pl.when((i == nt - 1) & (c == 3))
        def _():
            for k in range(4):
                for j in range(gc):
                    rows, cols = pl.ds(j * S5_STATE, S5_STATE), pl.ds(j * S5_GROUP, S5_GROUP)
                    dcre_ref[k, j] = dcre[k, rows, cols].T
                    dcim_ref[k, j] = dcim[k, rows, cols].T
            pltpu.sync_copy(dd, dd_hbm)

    chunk = lambda i, c: (i, c)
    sd = jax.ShapeDtypeStruct
    slab, slab_shape = _slab_spec(lp, tt, sw, lambda i, c: (i * 4 + c, 0))
    diag = pl.BlockSpec((4, gc, S5_GROUP, S5_STATE), lambda i, c: (0, 0, 0, 0))
    return pl.pallas_call(
        body, name=name, grid=(nt, 4),
        in_specs=[pl.BlockSpec((tt, cw), chunk), slab,
                  pl.BlockSpec((tt, cw), chunk), pl.BlockSpec((4, 1, cw), lambda i, c: (0, 0, 0)), ANY, ANY],
        out_specs=[slab, pl.BlockSpec((tt, cw), chunk), diag, diag, ANY],
        out_shape=(sd(slab_shape, F32), sd((lp, d), F32),
                   sd((4, gc, S5_GROUP, S5_STATE), F32), sd((4, gc, S5_GROUP, S5_STATE), F32), sd((4, 1, cw), F32)),
        scratch_shapes=[pltpu.VMEM(cdre.shape, BF16), pltpu.VMEM(cdim.shape, BF16),
                        pltpu.VMEM(cdre.shape, F32), pltpu.VMEM(cdim.shape, F32), pltpu.VMEM((4, 1, cw), F32)],
        compiler_params=_params(ndim=2),
    )(dy, s, u, d_skip, cdre, cdim)


def s5_scan_bwd(g, s, ar, ai, name):
    nj = ar.shape[1]
    tt = TOKEN_TILE
    cpb = SCAN_CHUNKS
    nt = g.shape[0] // (4 * tt * nj)

    def body(g_ref, s_ref, ar_ref, ai_ref, lam_ref, dar_ref, dai_ref, st_r, st_i, acc_r, acc_i):
        i, cg = pl.program_id(0), pl.program_id(1)

        @pl.when((i == 0) & (cg == 0))
        def _():
            for ref in (st_r, st_i, acc_r, acc_i):
                ref[...] = jnp.zeros_like(ref)

        a_r = [ar_ref[cg * cpb + q] for q in range(cpb)]
        a_i = [ai_ref[cg * cpb + q] for q in range(cpb)]

        def slab(q, t):
            return pl.ds(pl.multiple_of((q * tt + t) * nj, nj), nj)

        def adjoint(q, t, l_r, l_i):
            rows = slab(q, t)
            g_r, g_i = _unpack_pair(g_ref[rows, :])
            n_r = g_r + a_r[q] * l_r + a_i[q] * l_i
            n_i = g_i + a_r[q] * l_i - a_i[q] * l_r
            lam_ref[rows, :] = _pack_pair(n_r, n_i)
            return n_r, n_i

        def pair(q, t, l_r, l_i, d_r, d_i):
            p_r, p_i = _unpack_pair(s_ref[slab(q, t), :])
            return d_r + l_r * p_r + l_i * p_i, d_i + l_i * p_r - l_r * p_i

        def step(k, carry):
            t = tt - 1 - k
            out = []
            for q in range(cpb):
                l_r, l_i, d_r, d_i = carry[q]
                l_r, l_i = adjoint(q, t, l_r, l_i)
                d_r, d_i = pair(q, t - 1, l_r, l_i, d_r, d_i)
                out.append((l_r, l_i, d_r, d_i))
            return tuple(out)

        init = []
        for q in range(cpb):
            ch = cg * cpb + q
            l_r, l_i = st_r[ch], st_i[ch]
            d_r, d_i = pair(q, tt - 1, l_r, l_i, acc_r[ch], acc_i[ch])
            init.append((l_r, l_i, d_r, d_i))
        final = lax.fori_loop(0, tt - 1, step, tuple(init), unroll=8)
        for q in range(cpb):
            ch = cg * cpb + q
            l_r, l_i, d_r, d_i = final[q]
            l_r, l_i = adjoint(q, 0, l_r, l_i)
            st_r[ch] = l_r
            st_i[ch] = l_i
            acc_r[ch] = d_r
            acc_i[ch] = d_i
            dar_ref[ch] = d_r
            dai_ref[ch] = d_i

    blk = pl.BlockSpec((cpb * tt * nj, 128), lambda i, cg: ((nt - 1 - i) * (4 // cpb) + cg, 0))
    par = pl.BlockSpec((4, nj, 128), lambda i, cg: (0, 0, 0))
    sd = jax.ShapeDtypeStruct
    return pl.pallas_call(
        body, name=name, grid=(nt, 4 // cpb),
        in_specs=[blk, blk, par, par], out_specs=[blk, par, par],
        out_shape=(sd(g.shape, F32), sd((4, nj, 128), F32), sd((4, nj, 128), F32)),
        scratch_shapes=[pltpu.VMEM((4, nj, 128), F32)] * 4,
        compiler_params=_params(ndim=2),
    )(g, s, ar, ai)


def s5_bwd1(lam, dus, u, dp, h, dh, g, w_in, bdre, bdim, name):
    lp, d = h.shape
    tt = TOKEN_TILE
    nt = lp // tt
    cw, sw = bdre.shape[1], bdre.shape[2]
    gc = cw // S5_GROUP

    def body(lam_ref, dus_ref, u_ref, dpz_ref, h_ref, dh_ref, g_ref, w_hbm, bre_hbm, bim_hbm,
             dpu_ref, dho_ref, n_ref, dbre_ref, dbim_ref, dg_hbm, w, bre, bim, dn_sc, dbre, dbim, dg):
        i, c = pl.program_id(0), pl.program_id(1)

        @pl.when((i == 0) & (c == 0))
        def _():
            pltpu.sync_copy(w_hbm, w)
            pltpu.sync_copy(bre_hbm, bre)
            pltpu.sync_copy(bim_hbm, bim)
            dbre[...] = jnp.zeros_like(dbre)
            dbim[...] = jnp.zeros_like(dbim)
            dg[...] = jnp.zeros_like(dg)

        (l_r, l_i), uv = _slab_load(lam_ref), u_ref[...]
        du = dus_ref[...] + _dot_nt(l_r, bre[c]) + _dot_nt(l_i, bim[c])
        dbre[c] += _dot_tn(uv, l_r)
        dbim[c] += _dot_tn(uv, l_i)
        dpu_ref[0] = du.astype(BF16)
        part = _dot_nt(du, w[c])

        @pl.when(c == 0)
        def _():
            dn_sc[...] = part

        @pl.when(c > 0)
        def _():
            dn_sc[...] += part

        @pl.when(c == 3)
        def _():
            dz = dpz_ref[0]
            dn = dn_sc[...]
            for k in range(4):
                dn = dn + _dot_nt(dz[:, k * cw:(k + 1) * cw], w[4 + k])
            gv = g_ref[...]
            n, hh, rr = _rms_fwd(h_ref[...], gv)
            n_ref[...] = n.T.astype(BF16)
            dg[...] += jnp.sum(dn * hh, axis=0, keepdims=True)
            dho_ref[...] = dh_ref[...] + _rms_bwd(dn, hh, rr, gv)

        @pl.when((i == nt - 1) & (c == 3))
        def _():
            for k in range(4):
                for j in range(gc):
                    rows, cols = pl.ds(j * S5_GROUP, S5_GROUP), pl.ds(j * S5_STATE, S5_STATE)
                    dbre_ref[k, j] = dbre[k, rows, cols]
                    dbim_ref[k, j] = dbim[k, rows, cols]
            pltpu.sync_copy(dg, dg_hbm)

    row = lambda i, c: (i, 0)
    chunk = lambda i, c: (i, c)
    sd = jax.ShapeDtypeStruct
    slab, _ = _slab_spec(lp, tt, sw, lambda i, c: (i * 4 + c, 0))
    diag = pl.BlockSpec((4, gc, S5_GROUP, S5_STATE), lambda i, c: (0, 0, 0, 0))
    return pl.pallas_call(
        body, name=name, grid=(nt, 4),
        in_specs=[slab, pl.BlockSpec((tt, cw), chunk),
                  pl.BlockSpec((tt, cw), chunk), pl.BlockSpec((1, tt, d), lambda i, c: (1, i, 0)),
                  pl.BlockSpec((tt, d), row), pl.BlockSpec((tt, d), row), pl.BlockSpec((1, d), lambda i, c: (0, 0)),
                  ANY, ANY, ANY],
        out_specs=[pl.BlockSpec((1, tt, cw), lambda i, c: (0, i, c)), pl.BlockSpec((tt, d), row),
                   pl.BlockSpec((d, tt), lambda i, c: (0, i)), diag, diag, ANY],
        out_shape=(sd(dp.shape, BF16), sd((lp, d), F32), sd((d, lp), BF16),
                   sd((4, gc, S5_GROUP, S5_STATE), F32), sd((4, gc, S5_GROUP, S5_STATE), F32), sd((1, d), F32)),
        input_output_aliases={3: 0},
        scratch_shapes=[pltpu.VMEM(w_in.shape, BF16), pltpu.VMEM(bdre.shape, BF16), pltpu.VMEM(bdim.shape, BF16),
                        pltpu.VMEM((tt, d), F32), pltpu.VMEM(bdre.shape, F32), pltpu.VMEM(bdim.shape, F32), pltpu.VMEM((1, d), F32)],
        compiler_params=_params(ndim=2),
    )(lam, dus, u, dp, h, dh, g, w_in, bdre, bdim)


def grad_w_in(n_t, dp, blk, name):
    d, lp = n_t.shape
    npart, _, width = dp.shape
    per = width // blk

    def body(n_ref, dp_ref, o_ref):
        o_ref[0] = jnp.dot(n_ref[...], dp_ref[0], preferred_element_type=F32)

    return pl.pallas_call(
        body, name=name, grid=(npart * per,),
        in_specs=[pl.BlockSpec((d, lp), lambda j: (0, 0), pipeline_mode=pl.Buffered(1)),
                  pl.BlockSpec((1, lp, blk), lambda j: (j // per, 0, j % per))],
        out_specs=pl.BlockSpec((1, d, blk), lambda j: (j, 0, 0)),
        out_shape=jax.ShapeDtypeStruct((npart * per, d, blk), F32),
        compiler_params=_params(),
    )(n_t, dp)


def _conv_fwd_chunk(n, w, cw_ref, cb_ref, halo, c, nch):
    bg = jnp.dot(n, w[c], preferred_element_type=F32)
    cg = jnp.dot(n, w[nch + c], preferred_element_type=F32)
    v = jnp.dot(n, w[2 * nch + c], preferred_element_type=F32)
    z = jnp.dot(n, w[3 * nch + c], preferred_element_type=F32)
    hc = cg * v
    taps = cw_ref[c]
    conv = taps[2:3, :] * hc + taps[1:2, :] * _shift_down(hc, 1, halo) + taps[0:1, :] * _shift_down(hc, 2, halo) + cb_ref[c]
    return bg, cg, v, z, hc, conv


def conv_fwd(h, g, w_in, conv_w, conv_b, w_out, name):
    lp, d = h.shape
    tt = TOKEN_TILE
    nt = lp // tt
    nch, ce = w_out.shape[0], w_out.shape[1]

    def body(h_ref, g_ref, cw_ref, cb_ref, w_hbm, wo_hbm, o_ref, halo_ref, w, wo, halo):
        i = pl.program_id(0)

        @pl.when(i == 0)
        def _():
            pltpu.sync_copy(w_hbm, w)
            pltpu.sync_copy(wo_hbm, wo)
            halo[...] = jnp.zeros_like(halo)

        hv = h_ref[...]
        n = _rms_fwd(hv, g_ref[...])[0].astype(BF16)
        o = hv
        for c in range(nch):
            bg, _, _, z, hc, conv = _conv_fwd_chunk(n, w, cw_ref, cb_ref, halo[c], c, nch)
            o = o + _dot(bg * conv * (z * _sigmoid(z)), wo[c])
            halo[c] = hc[tt - CONV_HALO:, :]
            halo_ref[0, c] = hc[tt - CONV_HALO:, :]
        o_ref[...] = o

    sd = jax.ShapeDtypeStruct
    return pl.pallas_call(
        body, name=name, grid=(nt,),
        in_specs=[pl.BlockSpec((tt, d), lambda i: (i, 0)), pl.BlockSpec((1, d), lambda i: (0, 0)),
                  pl.BlockSpec(conv_w.shape, lambda i: (0, 0, 0)), pl.BlockSpec(conv_b.shape, lambda i: (0, 0, 0)), ANY, ANY],
        out_specs=[pl.BlockSpec((tt, d), lambda i: (i, 0)), pl.BlockSpec((1, nch, CONV_HALO, ce), lambda i: (i, 0, 0, 0))],
        out_shape=(sd((lp, d), F32), sd((nt, nch, CONV_HALO, ce), F32)),
        scratch_shapes=[pltpu.VMEM(w_in.shape, BF16), pltpu.VMEM(w_out.shape, BF16), pltpu.VMEM((nch, CONV_HALO, ce), F32)],
        compiler_params=_params(),
    )(h, g, conv_w, conv_b, w_in, w_out)


def conv_bwd(h, dh, halos, g, w_in, conv_w, conv_b, w_out, name):
    lp, d = h.shape
    tt = TOKEN_TILE
    nt = lp // tt
    nch, ce = w_out.shape[0], w_out.shape[1]

    def body(h_ref, dh_ref, halo_ref, g_ref, cw_ref, cb_ref, w_hbm, wo_hbm,
             dho_ref, n_ref, dp_ref, dwo_hbm, dcw_hbm, dcb_hbm, dg_hbm, w, wo, nxt, dwo, dcw, dcb, dg):
        i = pl.program_id(0)

        @pl.when(i == 0)
        def _():
            pltpu.sync_copy(w_hbm, w)
            pltpu.sync_copy(wo_hbm, wo)
            for ref in (nxt, dwo, dcw, dcb, dg):
                ref[...] = jnp.zeros_like(ref)

        gv = g_ref[...]
        nf, hh, rr = _rms_fwd(h_ref[...], gv)
        n = nf.astype(BF16)
        n_ref[...] = nf.T.astype(BF16)
        dhv = dh_ref[...]
        has_prev = (i < nt - 1).astype(F32)
        dn = jnp.zeros((tt, d), F32)
        for c in range(nch):
            halo = halo_ref[0, c] * has_prev
            bg, cg, v, z, hc, conv = _conv_fwd_chunk(n, w, cw_ref, cb_ref, halo, c, nch)
            sz, dsz = _silu_and_grad(z)
            y1 = bg * conv
            dy2 = _dot_nt(dhv, wo[c])
            dwo[c] += _dot_tn(y1 * sz, dhv)
            dy1 = dy2 * sz
            dz = dy2 * y1 * dsz
            dbg = dy1 * conv
            dconv = dy1 * bg
            dcb[c] += jnp.sum(dconv, axis=0, keepdims=True)
            up1 = _shift_up(dconv, 1, nxt[c])
            up2 = _shift_up(dconv, 2, nxt[c])
            nxt[c] = dconv[:CONV_HALO, :]
            taps = cw_ref[c]
            dhc = taps[2:3, :] * dconv + taps[1:2, :] * up1 + taps[0:1, :] * up2
            dcw[c, 0:1, :] += jnp.sum(hc * up2, axis=0, keepdims=True)
            dcw[c, 1:2, :] += jnp.sum(hc * up1, axis=0, keepdims=True)
            dcw[c, 2:3, :] += jnp.sum(hc * dconv, axis=0, keepdims=True)
            dcg = dhc * v
            dv = dhc * cg
            cols = slice(c * ce, (c + 1) * ce)
            for p, val in enumerate((dbg, dcg, dv, dz)):
                dp_ref[p, :, cols] = val.astype(BF16)
                dn = dn + _dot_nt(val, w[p * nch + c])
        dg[...] += jnp.sum(dn * hh, axis=0, keepdims=True)
        dho_ref[...] = dhv + _rms_bwd(dn, hh, rr, gv)

        @pl.when(i == nt - 1)
        def _():
            pltpu.sync_copy(dwo, dwo_hbm)
            pltpu.sync_copy(dcw, dcw_hbm)
            pltpu.sync_copy(dcb, dcb_hbm)
            pltpu.sync_copy(dg, dg_hbm)

    rev = lambda i: (nt - 1 - i, 0)
    sd = jax.ShapeDtypeStruct
    return pl.pallas_call(
        body, name=name, grid=(nt,),
        in_specs=[pl.BlockSpec((tt, d), rev), pl.BlockSpec((tt, d), rev),
                  pl.BlockSpec((1, nch, CONV_HALO, ce), lambda i: (jnp.maximum(nt - 2 - i, 0), 0, 0, 0)),
                  pl.BlockSpec((1, d), lambda i: (0, 0)),
                  pl.BlockSpec(conv_w.shape, lambda i: (0, 0, 0)), pl.BlockSpec(conv_b.shape, lambda i: (0, 0, 0)), ANY, ANY],
        out_specs=[pl.BlockSpec((tt, d), rev), pl.BlockSpec((d, tt), lambda i: (0, nt - 1 - i)),
                   pl.BlockSpec((4, tt, nch * ce), lambda i: (0, nt - 1 - i, 0)), ANY, ANY, ANY, ANY],
        out_shape=(sd((lp, d), F32), sd((d, lp), BF16), sd((4, lp, nch * ce), BF16),
                   sd(w_out.shape, F32), sd((nch, 8, ce), F32), sd((nch, 1, ce), F32), sd((1, d), F32)),
        scratch_shapes=[pltpu.VMEM(w_in.shape, BF16), pltpu.VMEM(w_out.shape, BF16), pltpu.VMEM((nch, CONV_HALO, ce), F32),
                        pltpu.VMEM(w_out.shape, F32), pltpu.VMEM((nch, 8, ce), F32), pltpu.VMEM((nch, 1, ce), F32),
                        pltpu.VMEM((1, d), F32)],
        compiler_params=_params(),
    )(h, dh, halos, g, conv_w, conv_b, w_in, w_out)


def _pool_fwd_group(n, w, wg, bg_ref, sc_ref, halo, k, tile, tt, first_pos):
    u = jnp.dot(n, w[k], preferred_element_type=F32)
    z = jnp.dot(n, w[4 + k], preferred_element_type=F32)
    ext = jnp.concatenate([halo, u], axis=0)
    win = _window_sums_back(ext)[k][POOL_HALO:, :]
    mixed = win * _pool_inv_count(tile, tt, first_pos, POOL_WINDOWS[k], u.shape[1]) - u
    outs = _dot(mixed, wg[k]) + bg_ref[k]
    return u, z, mixed, outs, outs * sc_ref[k]


def pool_fwd(h, g, w_in, w_grp, b_grp, scale, w_out, first_pos, name):
    lp, d = h.shape
    tt = TOKEN_TILE
    nt = lp // tt
    gw = w_grp.shape[1]

    def body(h_ref, g_ref, bg_ref, sc_ref, w_hbm, wg_hbm, wo_hbm, o_ref, halo_ref, w, wg, wo, halo):
        i = pl.program_id(0)

        @pl.when(i == 0)
        def _():
            pltpu.sync_copy(w_hbm, w)
            pltpu.sync_copy(wg_hbm, wg)
            pltpu.sync_copy(wo_hbm, wo)
            halo[...] = jnp.zeros_like(halo)

        hv = h_ref[...]
        n = _rms_fwd(hv, g_ref[...])[0].astype(BF16)
        o = hv
        for k in range(4):
            u, z, _, _, yp = _pool_fwd_group(n, w, wg, bg_ref, sc_ref, halo[k], k, i, tt, first_pos)
            o = o + _dot(yp * (z * _sigmoid(z)), wo[k])
            halo[k] = u[tt - POOL_HALO:, :]
            halo_ref[0, k] = u[tt - POOL_HALO:, :]
        o_ref[...] = o

    sd = jax.ShapeDtypeStruct
    small = pl.BlockSpec((4, 1, gw), lambda i: (0, 0, 0))
    return pl.pallas_call(
        body, name=name, grid=(nt,),
        in_specs=[pl.BlockSpec((tt, d), lambda i: (i, 0)), pl.BlockSpec((1, d), lambda i: (0, 0)), small, small, ANY, ANY, ANY],
        out_specs=[pl.BlockSpec((tt, d), lambda i: (i, 0)), pl.BlockSpec((1, 4, POOL_HALO, gw), lambda i: (i, 0, 0, 0))],
        out_shape=(sd((lp, d), F32), sd((nt, 4, POOL_HALO, gw), F32)),
        scratch_shapes=[pltpu.VMEM(w_in.shape, BF16), pltpu.VMEM(w_grp.shape, BF16), pltpu.VMEM(w_out.shape, BF16),
                        pltpu.VMEM((4, POOL_HALO, gw), F32)],
        compiler_params=_params(),
    )(h, g, b_grp, scale, w_in, w_grp, w_out)


def pool_bwd(h, dh, halos, g, w_in, w_grp, b_grp, scale, w_out, first_pos, name):
    lp, d = h.shape
    tt = TOKEN_TILE
    nt = lp // tt
    gw = w_grp.shape[1]

    def body(h_ref, dh_ref, halo_ref, g_ref, bg_ref, sc_ref, w_hbm, wg_hbm, wo_hbm,
             dho_ref, n_ref, dp_ref, dwo_hbm, dwg_hbm, dbg_hbm, dsc_hbm, dg_hbm,
             w, wg, wo, nxt, dwo, dwg, dbg, dsc, dg):
        i = pl.program_id(0)
        tile = nt - 1 - i

        @pl.when(i == 0)
        def _():
            pltpu.sync_copy(w_hbm, w)
            pltpu.sync_copy(wg_hbm, wg)
            pltpu.sync_copy(wo_hbm, wo)
            for ref in (nxt, dwo, dwg, dbg, dsc, dg):
                ref[...] = jnp.zeros_like(ref)

        gv = g_ref[...]
        nf, hh, rr = _rms_fwd(h_ref[...], gv)
        n = nf.astype(BF16)
        n_ref[...] = nf.T.astype(BF16)
        dhv = dh_ref[...]
        has_prev = (i < nt - 1).astype(F32)
        dn = jnp.zeros((tt, d), F32)
        for k in range(4):
            u, z, mixed, outs, yp = _pool_fwd_group(n, w, wg, bg_ref, sc_ref, halo_ref[0, k] * has_prev, k, tile, tt, first_pos)
            sz, dsz = _silu_and_grad(z)
            dy = _dot_nt(dhv, wo[k])
            dwo[k] += _dot_tn(yp * sz, dhv)
            dyp = dy * sz
            dz = dy * yp * dsz
            dsc[k] += jnp.sum(dyp * outs, axis=0, keepdims=True)
            douts = dyp * sc_ref[k]
            dbg[k] += jnp.sum(douts, axis=0, keepdims=True)
            dwg[k] += _dot_tn(mixed, douts)
            dmixed = _dot_nt(douts, wg[k])
            dm = dmixed * _pool_inv_count(tile, tt, first_pos, POOL_WINDOWS[k], gw)
            ext = jnp.concatenate([dm, nxt[k]], axis=0)
            du = _window_sums_fwd(ext)[k][:tt, :] - dmixed
            nxt[k] = dm[:POOL_HALO, :]
            cols = slice(k * gw, (k + 1) * gw)
            dp_ref[0, :, cols] = du.astype(BF16)
            dp_ref[1, :, cols] = dz.astype(BF16)
            dn = dn + _dot_nt(du, w[k]) + _dot_nt(dz, w[4 + k])
        dg[...] += jnp.sum(dn * hh, axis=0, keepdims=True)
        dho_ref[...] = dhv + _rms_bwd(dn, hh, rr, gv)

        @pl.when(i == nt - 1)
        def _():
            pltpu.sync_copy(dwo, dwo_hbm)
            pltpu.sync_copy(dwg, dwg_hbm)
            pltpu.sync_copy(dbg, dbg_hbm)
            pltpu.sync_copy(dsc, dsc_hbm)
            pltpu.sync_copy(dg, dg_hbm)

    rev = lambda i: (nt - 1 - i, 0)
    sd = jax.ShapeDtypeStruct
    small = pl.BlockSpec((4, 1, gw), lambda i: (0, 0, 0))
    return pl.pallas_call(
        body, name=name, grid=(nt,),
        in_specs=[pl.BlockSpec((tt, d), rev), pl.BlockSpec((tt, d), rev),
                  pl.BlockSpec((1, 4, POOL_HALO, gw), lambda i: (jnp.maximum(nt - 2 - i, 0), 0, 0, 0)),
                  pl.BlockSpec((1, d), lambda i: (0, 0)), small, small, ANY, ANY, ANY],
        out_specs=[pl.BlockSpec((tt, d), rev), pl.BlockSpec((d, tt), lambda i: (0, nt - 1 - i)),
                   pl.BlockSpec((2, tt, 4 * gw), lambda i: (0, nt - 1 - i, 0)), ANY, ANY, ANY, ANY, ANY],
        out_shape=(sd((lp, d), F32), sd((d, lp), BF16), sd((2, lp, 4 * gw), BF16),
                   sd(w_out.shape, F32), sd(w_grp.shape, F32), sd((4, 1, gw), F32), sd((4, 1, gw), F32), sd((1, d), F32)),
        scratch_shapes=[pltpu.VMEM(w_in.shape, BF16), pltpu.VMEM(w_grp.shape, BF16), pltpu.VMEM(w_out.shape, BF16),
                        pltpu.VMEM((4, POOL_HALO, gw), F32), pltpu.VMEM(w_out.shape, F32), pltpu.VMEM(w_grp.shape, F32),
                        pltpu.VMEM((4, 1, gw), F32), pltpu.VMEM((4, 1, gw), F32), pltpu.VMEM((1, d), F32)],
        compiler_params=_params(),
    )(h, dh, halos, g, b_grp, scale, w_in, w_grp, w_out)


def loss_head(h, target, g, pad_tiles, name):
    lp, d = h.shape
    tt = TOKEN_TILE
    nt = lp // tt

    def body(h_ref, t_ref, g_ref, dh_ref, dg_ref, loss_ref, acc):
        i = pl.program_id(0)

        @pl.when(i == 0)
        def _():
            acc[...] = jnp.zeros_like(acc)
            dg_ref[...] = jnp.zeros_like(dg_ref)

        @pl.when(i < pad_tiles)
        def _():
            dh_ref[...] = jnp.zeros_like(dh_ref)

        @pl.when(i >= pad_tiles)
        def _():
            gv = g_ref[...]
            n, hh, rr = _rms_fwd(h_ref[...], gv)
            err = n - t_ref[...]
            acc[...] += 0.5 * jnp.sum(jnp.mean(err * err, axis=-1, keepdims=True), axis=0, keepdims=True)
            dn = err * (1.0 / d)
            dg_ref[...] += jnp.sum(dn * hh, axis=0, keepdims=True)
            dh_ref[...] = _rms_bwd(dn, hh, rr, gv)

        loss_ref[...] = jnp.broadcast_to(acc[...], loss_ref.shape)

    sd = jax.ShapeDtypeStruct
    return pl.pallas_call(
        body, name=name, grid=(nt,),
        in_specs=[pl.BlockSpec((tt, d), lambda i: (i, 0)), pl.BlockSpec((tt, d), lambda i: (jnp.maximum(i - pad_tiles, 0), 0)),
                  pl.BlockSpec((1, d), lambda i: (0, 0))],
        out_specs=[pl.BlockSpec((tt, d), lambda i: (i, 0)), pl.BlockSpec((1, d), lambda i: (0, 0)),
                   pl.BlockSpec((8, 128), lambda i: (0, 0))],
        out_shape=(sd((lp, d), F32), sd((1, d), F32), sd((8, 128), F32)),
        scratch_shapes=[pltpu.VMEM((1, 1), F32)],
        compiler_params=_params(),
    )(h, target, g)


def exchange(arrs, gather, name):
    n = len(arrs)

    def body(*refs):
        ins, outs = refs[:n], refs[n:2 * n]
        send_sems, recv_sems, own_sems = refs[2 * n:]
        x, y, c = lax.axis_index("x"), lax.axis_index("y"), lax.axis_index("c")
        me = 4 * x + 2 * y + c
        own = []
        for a in range(n):
            cp = pltpu.make_async_copy(ins[a] if gather else ins[a].at[me], outs[a].at[me], own_sems.at[a])
            cp.start()
            own.append(cp)
        sent = []
        for k in range(1, N_DEV):
            px = 1 - x if k & 4 else x
            py = 1 - y if k & 2 else y
            pc = 1 - c if k & 1 else c
            peer = 4 * px + 2 * py + pc
            for a in range(n):
                cp = pltpu.make_async_remote_copy(
                    src_ref=ins[a] if gather else ins[a].at[peer], dst_ref=outs[a].at[me],
                    send_sem=send_sems.at[a, k - 1], recv_sem=recv_sems.at[a, k - 1],
                    device_id=(px, py, pc), device_id_type=pl.DeviceIdType.MESH)
                cp.start()
                sent.append((cp, a, k, peer, (px, py, pc)))
        for cp, a, k, peer, pid in sent:
            cp.wait_send()
            pltpu.make_async_remote_copy(
                src_ref=ins[a] if gather else ins[a].at[peer], dst_ref=outs[a].at[peer],
                send_sem=send_sems.at[a, k - 1], recv_sem=recv_sems.at[a, k - 1],
                device_id=pid, device_id_type=pl.DeviceIdType.MESH).wait_recv()
        for cp in own:
            cp.wait()

    hbm = pl.BlockSpec(memory_space=pltpu.HBM)
    out_shape = tuple(jax.ShapeDtypeStruct(((N_DEV,) + a.shape) if gather else a.shape, a.dtype) for a in arrs)
    return pl.pallas_call(
        body, name=name, in_specs=[hbm] * n, out_specs=[hbm] * n, out_shape=out_shape,
        scratch_shapes=[pltpu.SemaphoreType.DMA((n, N_DEV - 1)), pltpu.SemaphoreType.DMA((n, N_DEV - 1)),
                        pltpu.SemaphoreType.DMA((n,))],
    )(*[pltpu.with_memory_space_constraint(a, pltpu.HBM) for a in arrs])


def _peers(x, y, c):
    out = []
    for k in range(1, N_DEV):
        px = 1 - x if k & 4 else x
        py = 1 - y if k & 2 else y
        pc = 1 - c if k & 1 else c
        out.append((k, (px, py, pc), 4 * px + 2 * py + pc))
    return out


def exchange_start(arrs, gather, after, name):
    n = len(arrs)
    me = 4 * lax.axis_index("x") + 2 * lax.axis_index("y") + lax.axis_index("c")
    lands = []
    for a in arrs:
        own = a[None] if gather else lax.dynamic_index_in_dim(a, me, 0, keepdims=True)
        lands.append(lax.dynamic_update_index_in_dim(lax.empty(((N_DEV,) + a.shape) if gather else a.shape, a.dtype), own, me, 0))

    def body(*refs):
        ins, land = refs[:n], refs[n:2 * n]
        send_sems, recv_sems, token = refs[2 * n + 1], refs[2 * n + 2], refs[4 * n + 3]
        x, y, c = lax.axis_index("x"), lax.axis_index("y"), lax.axis_index("c")
        me = 4 * x + 2 * y + c
        for k, pid, peer in _peers(x, y, c):
            for a in range(n):
                pltpu.make_async_remote_copy(
                    src_ref=ins[a] if gather else ins[a].at[peer], dst_ref=land[a].at[me],
                    send_sem=send_sems.at[a * (N_DEV - 1) + k - 1], recv_sem=recv_sems.at[a * (N_DEV - 1) + k - 1],
                    device_id=pid, device_id_type=pl.DeviceIdType.MESH).start()
        token[...] = jnp.zeros_like(token)

    hbm = pl.BlockSpec(memory_space=pltpu.HBM)
    sem = pl.BlockSpec(memory_space=pltpu.SEMAPHORE)
    sems = pltpu.SemaphoreType.DMA((n * (N_DEV - 1),))
    res = pl.pallas_call(
        body, name=name, in_specs=[hbm] * (2 * n) + [ANY],
        out_specs=[sem, sem] + [hbm] * (2 * n) + [pl.BlockSpec(memory_space=pltpu.VMEM)],
        out_shape=[sems, sems] + [pltpu.HBM(a.shape, a.dtype) for a in arrs] + [pltpu.HBM(l.shape, l.dtype) for l in lands]
        + [jax.ShapeDtypeStruct((8, 128), F32)],
        input_output_aliases={a: 2 + a for a in range(2 * n)},
        compiler_params=pltpu.CompilerParams(has_side_effects=pltpu.SideEffectType.DATAFLOW_SIDE_EFFECTING),
    )(*[pltpu.with_memory_space_constraint(a, pltpu.HBM) for a in list(arrs) + lands], after)
    return res[0], res[1], res[2:2 + n], res[2 + n:2 + 2 * n], res[-1]


def exchange_wait(started, gather, after, name):
    send_sems, recv_sems, srcs, lands, _ = started
    n = len(srcs)
    after = list(after) if isinstance(after, (list, tuple)) else [after]

    def body(*refs):
        ins, land = refs[:n], refs[n:2 * n]
        send_sems, recv_sems = refs[2 * n], refs[2 * n + 1]
        x, y, c = lax.axis_index("x"), lax.axis_index("y"), lax.axis_index("c")
        for k, pid, peer in _peers(x, y, c):
            for a in range(n):
                cp = pltpu.make_async_remote_copy(
                    src_ref=ins[a] if gather else ins[a].at[peer], dst_ref=land[a].at[peer],
                    send_sem=send_sems.at[a * (N_DEV - 1) + k - 1], recv_sem=recv_sems.at[a * (N_DEV - 1) + k - 1],
                    device_id=pid, device_id_type=pl.DeviceIdType.MESH)
                cp.wait_send()
                cp.wait_recv()

    hbm = pl.BlockSpec(memory_space=pltpu.HBM)
    sem = pl.BlockSpec(memory_space=pltpu.SEMAPHORE)
    res = pl.pallas_call(
        body, name=name, in_specs=[hbm] * (2 * n) + [sem, sem] + [ANY] * len(after),
        out_specs=[hbm] * (2 * n),
        out_shape=[pltpu.HBM(a.shape, a.dtype) for a in list(srcs) + list(lands)],
        input_output_aliases={a: a for a in range(2 * n)},
        compiler_params=pltpu.CompilerParams(has_side_effects=pltpu.SideEffectType.DATAFLOW_SIDE_EFFECTING),
    )(*srcs, *lands, send_sems, recv_sems, *after)
    return res[n:]


def _adamw(w, g, m, v):
    m = ADAM_B1 * m + (1.0 - ADAM_B1) * g
    v = ADAM_B2 * v + (1.0 - ADAM_B2) * (g * g)
    m_hat = m / (1.0 - ADAM_B1 ** ADAM_STEP)
    v_hat = v / (1.0 - ADAM_B2 ** ADAM_STEP)
    return -ADAM_LR * (m_hat / (jnp.sqrt(v_hat) + ADAM_EPS) + ADAM_WD * w), m, v


def _update_tile_rows(rows, cols):
    if rows * cols <= UPDATE_TILE_ELEMS:
        return rows
    return max(t for t in range(8, UPDATE_TILE_ELEMS // cols + 1, 8) if rows % t == 0)


def _sum_in_order(p_ref):
    g = p_ref[0]
    for j in range(1, p_ref.shape[0]):
        g = g + p_ref[j]
    return g


def sum_parts(parts, name):
    nparts, rows, cols = parts.shape
    tr = _update_tile_rows(rows, cols)

    def body(p_ref, g_ref):
        g_ref[...] = _sum_in_order(p_ref)

    return pl.pallas_call(
        body, name=name, grid=(rows // tr,),
        in_specs=[pl.BlockSpec((nparts, tr, cols), lambda i: (0, i, 0))],
        out_specs=pl.BlockSpec((tr, cols), lambda i: (i, 0)), out_shape=jax.ShapeDtypeStruct((rows, cols), F32),
        compiler_params=_params(),
    )(parts)


def sum_adamw(parts, w, m, v, name):
    rows, cols = w.shape
    nparts = parts.shape[0]
    tr = _update_tile_rows(rows, cols)

    def body(p_ref, w_ref, m_ref, v_ref, g_ref, d_ref, nm_ref, nv_ref):
        g = _sum_in_order(p_ref)
        delta, nm, nv = _adamw(w_ref[...], g, m_ref[...], v_ref[...])
        g_ref[...] = g
        d_ref[...] = delta
        nm_ref[...] = nm
        nv_ref[...] = nv

    blk = pl.BlockSpec((tr, cols), lambda i: (i, 0))
    sd = jax.ShapeDtypeStruct((rows, cols), F32)
    return pl.pallas_call(
        body, name=name, grid=(rows // tr,),
        in_specs=[pl.BlockSpec((nparts, tr, cols), lambda i: (0, i, 0)), blk, blk, blk],
        out_specs=[blk] * 4, out_shape=(sd,) * 4,
        compiler_params=_params(),
    )(parts, w, m, v)


def update_packed(g, w, m, v, pieces, name):
    def body(g_ref, w_ref, m_ref, v_ref, *outs):
        gv = g_ref[...]
        res = (gv,) + _adamw(w_ref[...], gv, m_ref[...], v_ref[...])
        for k in range(4):
            outs[k][...] = res[k]
        for p, (row, rows, lanes) in enumerate(pieces):
            for k in range(4):
                outs[4 + 4 * p + k][...] = res[k][row:row + rows, :lanes]

    sd = jax.ShapeDtypeStruct
    shapes = [sd(w.shape, F32)] * 4 + [sd((rows, lanes), F32) for _, rows, lanes in pieces for _ in range(4)]
    return pl.pallas_call(body, name=name, out_shape=shapes,
                          compiler_params=pltpu.CompilerParams(vmem_limit_bytes=VMEM_LIMIT))(g, w, m, v)


S5_NAMES = ("w_in", "lam_re", "lam_im", "log_dt", "b_re", "b_im", "c_re", "c_im", "d_skip", "w_glu", "b_glu", "w_out")
CONV_NAMES = ("w_in", "conv_w", "conv_b", "w_out")
POOL_NAMES = ("w_in", "w_grp", "b_grp", "scale", "w_out")
LAYER_KINDS = ("s5", "conv", "pool", "s5")
LAYER_NAMES = {"s5": S5_NAMES, "conv": CONV_NAMES, "pool": POOL_NAMES}
SHARDED = {"s5": ("w_in", "w_glu", "w_out"), "conv": ("w_in", "conv_w", "w_out"), "pool": ("w_in", "w_grp", "b_grp", "w_out")}
GATHER_F32 = ("conv_w", "b_grp")


def weight_names():
    names = ["meta_tokens"]
    for i, kind in enumerate(LAYER_KINDS):
        names.append("norm%d_g" % i)
        names += ["l%d_%s" % (i, n) for n in LAYER_NAMES[kind]]
    names.append("final_g")
    return names


def sharded_names():
    return ["meta_tokens"] + ["l%d_%s" % (i, n) for i, kind in enumerate(LAYER_KINDS) for n in SHARDED[kind]]


def _block_diag_in(bb_t, gc):
    i, g, p = bb_t.shape
    t = bb_t.astype(BF16).reshape(i, 4, gc, p)
    return jnp.einsum("icjp,jk->cjikp", t, jnp.eye(gc, dtype=BF16)).reshape(4, gc * i, gc * p)


def _block_diag_in_grad(blocks):
    _, gc, i, p = blocks.shape
    return jnp.transpose(blocks, (2, 0, 1, 3)).reshape(i, 4 * gc, p)


def _block_diag_out(cc, gc):
    g, i, p = cc.shape
    return jnp.einsum("cjip,jk->cjpki", cc.astype(BF16).reshape(4, gc, i, p), jnp.eye(gc, dtype=BF16)).reshape(4, gc * p, gc * i)


def _block_diag_out_grad(blocks):
    _, gc, i, p = blocks.shape
    return blocks.reshape(4 * gc, i, p)


def _to_owner_blocks(a, axis):
    shape = a.shape[:axis] + (N_DEV, a.shape[axis] // N_DEV) + a.shape[axis + 1:]
    return jnp.moveaxis(a.reshape(shape), axis, 0)


def _from_owner_blocks(a, axis):
    a = jnp.moveaxis(a, 0, axis)
    return a.reshape(a.shape[:axis] + (a.shape[axis] * a.shape[axis + 1],) + a.shape[axis + 2:])


def _step(x, target, weights, moments_m, moments_v):
    seq, d = x.shape[1], x.shape[2]
    n_meta = weights["meta_tokens"].shape[0]
    tt = TOKEN_TILE
    pad_tiles = -(-n_meta // tt)
    p0 = pad_tiles * tt
    lp = p0 + seq
    first_pos = p0 - n_meta
    gc = d // 4 // S5_GROUP
    cw = d // 4

    big_names = [n for n in sharded_names() if n != "meta_tokens" and n.split("_", 1)[1] not in GATHER_F32]
    small_names = [n for n in sharded_names() if n not in big_names]
    layer_big = [[n for n in big_names if n.startswith("l%d_" % i)] for i in range(len(LAYER_KINDS))]
    layer_big[0] = small_names + layer_big[0]
    gather_started = []
    after = jnp.zeros((8, 128), F32)
    for i, names in enumerate(layer_big):
        gather_started.append(exchange_start([weights[n] if n in small_names else weights[n].astype(BF16) for n in names], True,
                                             after, "gather_start_l%d" % i))
        after = gather_started[-1][4]

    def vec(name):
        return weights[name].reshape(1, -1)

    s5_prep = {}
    for i, kind in enumerate(LAYER_KINDS):
        if kind == "s5":
            p = "l%d_" % i
            lr, li = weights[p + "lam_re"], weights[p + "lam_im"] + after[0, 0]
            ldt = weights[p + "log_dt"].reshape(-1, 1)
            br_t = jnp.transpose(weights[p + "b_re"], (2, 0, 1))
            bi_t = jnp.transpose(weights[p + "b_im"], (2, 0, 1))
            ar, ai, bbr, bbi = s5_disc_fwd(lr, li, ldt, br_t, bi_t, p + "disc_fwd")
            s5_prep[i] = dict(
                disc=(lr, li, ldt, br_t, bi_t), ar=ar.reshape(4, -1, 128), ai=ai.reshape(4, -1, 128),
                bdre=_block_diag_in(bbr, gc), bdim=_block_diag_in(bbi, gc),
                cdre=_block_diag_out(weights[p + "c_re"], gc), cdim=_block_diag_out(-weights[p + "c_im"], gc),
                d_skip=weights[p + "d_skip"].reshape(4, 1, cw), b_glu=vec(p + "b_glu"))
    h = jnp.concatenate([jnp.zeros((p0, d), F32), x[0] + after[0, 0]], axis=0)

    prepared = [h] + [s5_prep[i][k] for i in s5_prep for k in ("bdre", "bdim", "cdre", "cdim")]
    gathered = dict(zip(layer_big[0], exchange_wait(gather_started[0], True, prepared, "gather_wait_l0")))
    h = lax.dynamic_update_slice(h, _from_owner_blocks(gathered["meta_tokens"], 1), (first_pos, 0))

    full = {}

    def layer_weights(i, kind, after):
        p = "l%d_" % i
        if i > 0:
            gathered.update(zip(layer_big[i], exchange_wait(gather_started[i], True, after, "gather_wait_l%d" % i)))
        w_in = gathered[p + "w_in"]
        if kind == "s5":
            full[i] = dict(s5_prep[i], w_in=w_in, w_glu=gathered[p + "w_glu"].reshape(4, cw, d),
                           w_out=gathered[p + "w_out"].reshape(4, cw, d))
        elif kind == "conv":
            ce = w_in.shape[2]
            nch = 2
            conv_w = _from_owner_blocks(gathered[p + "conv_w"], 1)
            full[i] = dict(
                w_in=w_in, conv_w=jnp.transpose(conv_w.reshape(CONV_K, nch, ce), (1, 0, 2)),
                conv_b=weights[p + "conv_b"].reshape(nch, 1, ce), w_out=gathered[p + "w_out"].reshape(nch, ce, d))
        else:
            gw = w_in.shape[2]
            full[i] = dict(
                w_in=w_in, w_grp=_from_owner_blocks(gathered[p + "w_grp"], 1),
                b_grp=_from_owner_blocks(gathered[p + "b_grp"], 1).reshape(4, 1, gw),
                scale=weights[p + "scale"].reshape(4, 1, gw), w_out=gathered[p + "w_out"].reshape(4, gw, d))
        return full[i]

    saved = {}
    for i, kind in enumerate(LAYER_KINDS):
        p, f, g = "l%d_" % i, layer_weights(i, kind, h), vec("norm%d_g" % i)
        if kind == "s5":
            u, z, xs = s5_fwd1(h, g, f["w_in"], f["bdre"], f["bdim"], p + "fwd_in")
            s = s5_scan_fwd(xs, f["ar"], f["ai"], p + "scan_fwd")
            h_in = h
            h, y, q = s5_fwd3(s, u, z, h, f["cdre"], f["cdim"], f["w_glu"], f["w_out"], f["d_skip"], f["b_glu"], p + "fwd_out")
            saved[i] = (h_in, u, z, s, y, q)
        elif kind == "conv":
            h_new, halos = conv_fwd(h, g, f["w_in"], f["conv_w"], f["conv_b"], f["w_out"], p + "fwd")
            saved[i] = (h, halos)
            h = h_new
        else:
            h_new, halos = pool_fwd(h, g, f["w_in"], f["w_grp"], f["b_grp"], f["scale"], f["w_out"], first_pos, p + "fwd")
            saved[i] = (h, halos)
            h = h_new

    dh, dg_final, loss_tile = loss_head(h, target[0], vec("final_g"), pad_tiles, "loss_head")
    loss = lax.psum(loss_tile[0, 0], ("x", "y", "c"))

    grads = {"final_g": dg_final}
    names = weight_names()
    sh_names = sharded_names()
    rep_names = [n for n in names if n not in sh_names]

    def owner_blocks(a):
        return a.reshape(N_DEV, -1, a.shape[-1])

    def as2d(a):
        return a.reshape(-1, a.shape[-1])

    def pack(tree):
        flat = [jnp.pad(tree[n].reshape(-1), (0, -tree[n].size % 1024)) for n in rep_names]
        flat = jnp.concatenate(flat)
        return jnp.pad(flat, (0, -flat.size % (PACK_ROWS * 128))).reshape(-1, 128)

    layer_sharded, scatter_started = {}, {}
    ordered = jnp.zeros((), F32)
    for i in reversed(range(len(LAYER_KINDS))):
        kind = LAYER_KINDS[i]
        p, f, g = "l%d_" % i, full[i], vec("norm%d_g" % i) + ordered
        if kind == "s5":
            h_in, u, z, s, y, q = saved[i]
            dy, dp, dwo, dwg, dbg = s5_bwd3a(dh, y, q, z, f["w_glu"], f["w_out"], f["b_glu"] + ordered, p + "bwd_out")
            d_skip = f["d_skip"]
            if i == 0:
                early_names = [p + "w_glu", p + "w_out"]
                scatter_started["early"] = exchange_start([dwg.reshape(N_DEV, -1, d), dwo.reshape(N_DEV, -1, d)], False, dy,
                                                          "scatter_start_l0_early")
                d_skip = d_skip + scatter_started["early"][4][0, 0]
            ds, dus, dcre, dcim, dd = s5_bwd3b(dy, s, u, f["cdre"], f["cdim"], d_skip, p + "bwd_read")
            lam, dar, dai = s5_scan_bwd(ds, s, f["ar"], f["ai"], p + "scan_bwd")
            dp, dh, n, dbre, dbim, dg = s5_bwd1(lam, dus, u, dp, h_in, dh, g, f["w_in"], f["bdre"], f["bdim"], p + "bwd_in")
            dw_in = grad_w_in(n, dp, f["w_in"].shape[2], p + "grad_w_in")
            grads.update({p + "w_in": dw_in, p + "w_glu": dwg.reshape(N_DEV, -1, d), p + "w_out": dwo.reshape(N_DEV, -1, d),
                          p + "d_skip": dd, p + "b_glu": dbg})

            def replicated_grads(p=p, f=f, dar=dar, dai=dai, dbre=dbre, dbim=dbim, dcre=dcre, dcim=dcim, token=None):
                lr, li, ldt, br_t, bi_t = f["disc"]
                dlr, dli, dldt, dbr_t, dbi_t = s5_disc_bwd(
                    lr, li, ldt, br_t, bi_t, dar.reshape(lr.shape) + token, dai.reshape(lr.shape),
                    _block_diag_in_grad(dbre), _block_diag_in_grad(dbim), p + "disc_bwd")
                grads.update({
                    p + "lam_re": dlr, p + "lam_im": dli, p + "log_dt": dldt,
                    p + "b_re": jnp.transpose(dbr_t, (1, 2, 0)), p + "b_im": jnp.transpose(dbi_t, (1, 2, 0)),
                    p + "c_re": _block_diag_out_grad(dcre), p + "c_im": -_block_diag_out_grad(dcim)})
        elif kind == "conv":
            replicated_grads = None
            h_in, halos = saved[i]
            dh, n, dp, dwo, dcw, dcb, dg = conv_bwd(h_in, dh, halos, g, f["w_in"], f["conv_w"], f["conv_b"], f["w_out"], p + "bwd")
            dw_in = grad_w_in(n, dp, f["w_in"].shape[2], p + "grad_w_in")
            dconv_w = jnp.transpose(dcw[:, :CONV_K, :], (1, 0, 2)).reshape(CONV_K, -1)
            grads.update({p + "w_in": dw_in, p + "conv_w": _to_owner_blocks(dconv_w, 1), p + "conv_b": dcb,
                          p + "w_out": dwo.reshape(N_DEV, -1, d)})
        else:
            replicated_grads = None
            h_in, halos = saved[i]
            dh, n, dp, dwo, dwgrp, dbgrp, dsc, dg = pool_bwd(h_in, dh, halos, g, f["w_in"], f["w_grp"], f["b_grp"], f["scale"],
                                                             f["w_out"], first_pos, p + "bwd")
            dw_in = grad_w_in(n, dp, f["w_in"].shape[2], p + "grad_w_in")
            grads.update({p + "w_in": dw_in, p + "w_grp": _to_owner_blocks(dwgrp, 1),
                          p + "b_grp": _to_owner_blocks(dbgrp.reshape(4, -1), 1), p + "scale": dsc,
                          p + "w_out": dwo.reshape(N_DEV, -1, d)})
        grads["norm%d_g" % i] = dg
        layer_sharded[i] = ["l%d_%s" % (i, n) for n in SHARDED[kind]]
        if i > 0:
            scatter_started[i] = exchange_start([owner_blocks(grads[n]) for n in layer_sharded[i]], False, dh,
                                                "scatter_start_l%d" % i)
            ordered = scatter_started[i][4][0, 0]
        if replicated_grads is not None:
            replicated_grads(token=ordered)
    grad_x = dh[p0:][None]
    grads["meta_tokens"] = _to_owner_blocks(dh[first_pos:p0], 1)
    last = len(LAYER_KINDS)
    layer_sharded[last] = ["meta_tokens", "replicated"]
    scatter_started[last] = exchange_start([owner_blocks(grads["meta_tokens"]), pack(grads).reshape(N_DEV, -1, 128)], False,
                                           dh, "scatter_start_replicated")
    layer_sharded["early"] = early_names
    layer_sharded[0] = [n for n in layer_sharded[0] if n not in early_names]

    out = {}
    received = {}
    after = [scatter_started[last][4]]
    for i in list(reversed(range(1, last))) + [last, "early", 0]:
        received.update(zip(layer_sharded[i], exchange_wait(scatter_started[i], False, after, "scatter_wait_%s" % i)))
        updated = []
        for n in layer_sharded[i]:
            if n != "replicated":
                res = sum_adamw(received[n], as2d(weights[n]), as2d(moments_m[n]), as2d(moments_v[n]), "update_" + n)
                out[n] = [r.reshape(weights[n].shape) for r in res]
                updated.append(out[n][0])
        after = updated or after
        if i == last:
            g_full = exchange([sum_parts(received["replicated"], "sum_replicated")], True, "gather_small_grads")[0]
            scatter_started[0] = exchange_start([owner_blocks(grads[n]) for n in layer_sharded[0]], False, g_full,
                                                "scatter_start_l0")
            g_full = g_full.reshape(-1, 128) + scatter_started[0][4][0, 0]
            offsets, offset = {}, 0
            for n in rep_names:
                offsets[n] = offset
                offset += weights[n].size + (-weights[n].size % 1024)
            vectors = [n for n in rep_names if weights[n].ndim == 1]
            pieces = [(offsets[n] // 128, max(weights[n].size // 128, 1), min(weights[n].size, 128)) for n in vectors]
            res = update_packed(g_full, pack(weights), pack(moments_m), pack(moments_v), pieces, "update_replicated")
            packed = res[:4]
            for j, n in enumerate(vectors):
                out[n] = [r.reshape(weights[n].shape) for r in res[4 + 4 * j:8 + 4 * j]]
            for n in rep_names:
                if n not in vectors:
                    size = weights[n].size
                    out[n] = [r.reshape(-1)[offsets[n]:offsets[n] + size].reshape(weights[n].shape) for r in packed]
            after = [out[n][k] for n in rep_names for k in range(4)]

    return (loss, grad_x) + tuple(out[n][k] for k in range(4) for n in names)


def kernel(x, *rest):
    names = weight_names()
    nw = len(names)
    weights = dict(zip(names, rest[:nw]))
    target = rest[nw]
    moments_m = dict(zip(names, rest[nw + 1:2 * nw + 1]))
    moments_v = dict(zip(names, rest[2 * nw + 1:3 * nw + 1]))
    return _step(x, target, weights, moments_m, moments_v)
```

```python
import functools
import math

import jax
import jax.numpy as jnp
from jax import lax
from jax.experimental import pallas as pl
from jax.experimental.pallas import tpu as pltpu

F32 = jnp.float32
BF16 = jnp.bfloat16
EPS = 1e-6
N_DEV = 8
TOKEN_TILE = 256
SCAN_CHUNKS = 4
S5_GROUP = 16
S5_STATE = 64
POOL_WINDOWS = (2, 4, 8, 16)
POOL_HALO = 16
CONV_K = 3
CONV_HALO = 8
ADAM_LR = 0.001
ADAM_B1 = 0.9
ADAM_B2 = 0.999
ADAM_EPS = 1e-08
ADAM_WD = 0.01
ADAM_STEP = 10
GELU_C = math.sqrt(2.0 / math.pi)
GELU_A = 0.044715
UPDATE_TILE_ELEMS = 1 << 17
PACK_ROWS = 512
VMEM_LIMIT = 56 << 20

ANY = pl.BlockSpec(memory_space=pl.ANY)


def _params(vmem=VMEM_LIMIT, ndim=1):
    return pltpu.CompilerParams(vmem_limit_bytes=vmem, dimension_semantics=("arbitrary",) * ndim)


def _dot(a, b):
    return jnp.dot(a.astype(BF16), b.astype(BF16), preferred_element_type=F32)


def _dot_nt(a, b):
    return lax.dot_general(a.astype(BF16), b.astype(BF16), (((1,), (1,)), ((), ())), preferred_element_type=F32)


def _dot_tn(a, b):
    return lax.dot_general(a.astype(BF16), b.astype(BF16), (((0,), (0,)), ((), ())), preferred_element_type=F32)


def _rms_fwd(h, g):
    r = lax.rsqrt(jnp.mean(h * h, axis=-1, keepdims=True) + EPS)
    hh = h * r
    return hh * g, hh, r


def _rms_bwd(dn, hh, r, g):
    dhh = dn * g
    return r * (dhh - hh * jnp.mean(dhh * hh, axis=-1, keepdims=True))


def _sigmoid(x):
    return 1.0 / (1.0 + jnp.exp(-x))


def _silu_and_grad(z):
    s = _sigmoid(z)
    return z * s, s * (1.0 + z * (1.0 - s))


def _gelu(y):
    t = jnp.tanh(GELU_C * (y + GELU_A * y * y * y))
    return 0.5 * y * (1.0 + t), t


def _gelu_grad(y, t):
    return 0.5 * (1.0 + t) + 0.5 * y * (1.0 - t * t) * GELU_C * (1.0 + 3.0 * GELU_A * y * y)


def _rows(shape):
    return lax.broadcasted_iota(jnp.int32, shape, 0)


def _shift_down(x, k, halo):
    y = pltpu.roll(x, k, 0)
    rows = _rows(x.shape)
    for j in range(k):
        y = jnp.where(rows == j, halo[halo.shape[0] - k + j:halo.shape[0] - k + j + 1, :], y)
    return y


def _shift_up(x, k, halo):
    n = x.shape[0]
    y = pltpu.roll(x, n - k, 0)
    rows = _rows(x.shape)
    for j in range(k):
        y = jnp.where(rows == n - k + j, halo[j:j + 1, :], y)
    return y


def _window_sums_back(ext):
    out = []
    s = ext
    for k in (1, 2, 4, 8):
        s = s + pltpu.roll(s, k, 0)
        out.append(s)
    return out


def _window_sums_fwd(ext):
    n = ext.shape[0]
    out = []
    s = ext
    for k in (1, 2, 4, 8):
        s = s + pltpu.roll(s, n - k, 0)
        out.append(s)
    return out


def _pool_inv_count(tile, tt, first_pos, w, width):
    pos = _rows((tt, width)) + (tile * tt - first_pos + 1)
    return 1.0 / jnp.clip(pos, 1, w).astype(F32)


def _slab_spec(lp, tt, sw):
    nj = sw // 128
    return pl.BlockSpec((4 * tt * nj, 128), lambda i: (i, 0)), (lp * 4 * nj, 128)


def _pack_pair(re, im):
    def rounded(v):
        return lax.bitcast_convert_type(v, jnp.int32) + 0x8000
    return lax.bitcast_convert_type((rounded(re) & -65536) | lax.shift_right_logical(rounded(im), 16), F32)


def _unpack_pair(w):
    b = lax.bitcast_convert_type(w, jnp.int32)
    return lax.bitcast_convert_type(b & -65536, F32), lax.bitcast_convert_type(lax.shift_left(b, 16), F32)


def _slab_load(ref, c):
    nj = ref.shape[0] // (4 * TOKEN_TILE)
    first = c * TOKEN_TILE * nj
    return _unpack_pair(jnp.concatenate([ref[pl.ds(first + j, TOKEN_TILE, stride=nj), :] for j in range(nj)], axis=1))


def _slab_store(ref, c, re, im):
    nj = ref.shape[0] // (4 * TOKEN_TILE)
    first = c * TOKEN_TILE * nj
    val = _pack_pair(re, im)
    for j in range(nj):
        ref[pl.ds(first + j, TOKEN_TILE, stride=nj), :] = val[:, j * 128:(j + 1) * 128]


def _s5_disc_math(lr, li, ldt, br, bi):
    dt = jnp.exp(ldt)
    mag = jnp.exp(lr * dt)
    ar = mag * jnp.cos(li * dt)
    ai = mag * jnp.sin(li * dt)
    den = lr * lr + li * li
    kr = ((ar - 1.0) * lr + ai * li) / den
    ki = (ai * lr - (ar - 1.0) * li) / den
    bbr = kr[None] * br - ki[None] * bi
    bbi = kr[None] * bi + ki[None] * br
    return ar, ai, bbr, bbi


def s5_disc_fwd(lr, li, ldt, br_t, bi_t, name):
    def body(lr_ref, li_ref, ldt_ref, br_ref, bi_ref, ar_ref, ai_ref, bbr_ref, bbi_ref):
        ar, ai, bbr, bbi = _s5_disc_math(lr_ref[...], li_ref[...], ldt_ref[...], br_ref[...], bi_ref[...])
        ar_ref[...] = ar
        ai_ref[...] = ai
        bbr_ref[...] = bbr
        bbi_ref[...] = bbi

    sd = jax.ShapeDtypeStruct
    return pl.pallas_call(
        body, name=name,
        out_shape=(sd(lr.shape, F32), sd(lr.shape, F32), sd(br_t.shape, F32), sd(br_t.shape, F32)),
    )(lr, li, ldt, br_t, bi_t)


def s5_disc_bwd(lr, li, ldt, br_t, bi_t, dar, dai, dbbr, dbbi, name):
    def body(lr_ref, li_ref, ldt_ref, br_ref, bi_ref, dar_ref, dai_ref, dbbr_ref, dbbi_ref,
             dlr_ref, dli_ref, dldt_ref, dbr_ref, dbi_ref):
        _, vjp = jax.vjp(_s5_disc_math, lr_ref[...], li_ref[...], ldt_ref[...], br_ref[...], bi_ref[...])
        dlr, dli, dldt, dbr, dbi = vjp((dar_ref[...], dai_ref[...], dbbr_ref[...], dbbi_ref[...]))
        dlr_ref[...] = dlr
        dli_ref[...] = dli
        dldt_ref[...] = dldt
        dbr_ref[...] = dbr
        dbi_ref[...] = dbi

    sd = jax.ShapeDtypeStruct
    return pl.pallas_call(
        body, name=name,
        out_shape=(sd(lr.shape, F32), sd(lr.shape, F32), sd(ldt.shape, F32), sd(br_t.shape, F32), sd(br_t.shape, F32)),
    )(lr, li, ldt, br_t, bi_t, dar, dai, dbbr, dbbi)


def s5_fwd1(h, g, w_in, bdre, bdim, name):
    lp, d = h.shape
    tt = TOKEN_TILE
    cw, sw = bdre.shape[1], bdre.shape[2]

    def body(h_ref, g_ref, w_hbm, bdre_hbm, bdim_hbm, u_ref, z_ref, x_ref, w, bre, bim):
        @pl.when(pl.program_id(0) == 0)
        def _():
            pltpu.sync_copy(w_hbm, w)
            pltpu.sync_copy(bdre_hbm, bre)
            pltpu.sync_copy(bdim_hbm, bim)

        n = _rms_fwd(h_ref[...], g_ref[...])[0].astype(BF16)
        for c in range(4):
            cols = slice(c * cw, (c + 1) * cw)
            u = jnp.dot(n, w[c], preferred_element_type=F32)
            u_ref[:, cols] = u
            z_ref[:, cols] = jnp.dot(n, w[c + 4], preferred_element_type=F32)
            ub = u.astype(BF16)
            _slab_store(x_ref, c, jnp.dot(ub, bre[c], preferred_element_type=F32), jnp.dot(ub, bim[c], preferred_element_type=F32))

    sd = jax.ShapeDtypeStruct
    slab, slab_shape = _slab_spec(lp, tt, sw)
    row = pl.BlockSpec((tt, d), lambda i: (i, 0))
    return pl.pallas_call(
        body, name=name, grid=(lp // tt,),
        in_specs=[row, pl.BlockSpec((1, d), lambda i: (0, 0)), ANY, ANY, ANY],
        out_specs=[row, row, slab],
        out_shape=(sd((lp, d), F32), sd((lp, d), F32), sd(slab_shape, F32)),
        scratch_shapes=[pltpu.VMEM(w_in.shape, BF16), pltpu.VMEM(bdre.shape, BF16), pltpu.VMEM(bdim.shape, BF16)],
        compiler_params=_params(),
    )(h, g, w_in, bdre, bdim)


def s5_scan_fwd(x, ar, ai, name):
    nj = ar.shape[1]
    tt = TOKEN_TILE
    cpb = SCAN_CHUNKS
    nt = x.shape[0] // (4 * tt * nj)

    def body(x_ref, ar_ref, ai_ref, s_ref, st_r, st_i):
        i, cg = pl.program_id(0), pl.program_id(1)

        @pl.when(i == 0)
        def _():
            for q in range(cpb):
                st_r[cg * cpb + q] = jnp.zeros((nj, 128), F32)
                st_i[cg * cpb + q] = jnp.zeros((nj, 128), F32)

        a_r = [ar_ref[cg * cpb + q] for q in range(cpb)]
        a_i = [ai_ref[cg * cpb + q] for q in range(cpb)]

        def step(t, carry):
            out = []
            for q in range(cpb):
                s_r, s_i = carry[q]
                rows = pl.ds(pl.multiple_of((q * tt + t) * nj, nj), nj)
                x_r, x_i = _unpack_pair(x_ref[rows, :])
                n_r = a_r[q] * s_r - a_i[q] * s_i + x_r
                n_i = a_r[q] * s_i + a_i[q] * s_r + x_i
                s_ref[rows, :] = _pack_pair(n_r, n_i)
                out.append((n_r, n_i))
            return tuple(out)

        init = tuple((st_r[cg * cpb + q], st_i[cg * cpb + q]) for q in range(cpb))
        final = lax.fori_loop(0, tt, step, init, unroll=8)
        for q in range(cpb):
            st_r[cg * cpb + q] = final[q][0]
            st_i[cg * cpb + q] = final[q][1]

    blk = pl.BlockSpec((cpb * tt * nj, 128), lambda i, cg: (i * (4 // cpb) + cg, 0))
    par = pl.BlockSpec((4, nj, 128), lambda i, cg: (0, 0, 0))
    sd = jax.ShapeDtypeStruct
    return pl.pallas_call(
        body, name=name, grid=(nt, 4 // cpb),
        in_specs=[blk, par, par], out_specs=blk,
        out_shape=sd(x.shape, F32),
        scratch_shapes=[pltpu.VMEM((4, nj, 128), F32), pltpu.VMEM((4, nj, 128), F32)],
        compiler_params=_params(ndim=2),
    )(x, ar, ai)


def s5_fwd3(s, u, z, h, cdre, cdim, w_glu, w_out, d_skip, b_glu, name):
    lp, d = h.shape
    tt = TOKEN_TILE
    sw, cw = cdre.shape[1], cdre.shape[2]

    def body(s_ref, u_ref, z_ref, h_ref, d_ref, bg_ref, cre_hbm, cim_hbm, wg_hbm, wo_hbm,
             o_ref, y_ref, q_ref, cre, cim, wg, wo):
        @pl.when(pl.program_id(0) == 0)
        def _():
            pltpu.sync_copy(cre_hbm, cre)
            pltpu.sync_copy(cim_hbm, cim)
            pltpu.sync_copy(wg_hbm, wg)
            pltpu.sync_copy(wo_hbm, wo)

        gys, q = [], None
        for c in range(4):
            cols = slice(c * cw, (c + 1) * cw)
            s_r, s_i = _slab_load(s_ref, c)
            y = _dot(s_r, cre[c]) + _dot(s_i, cim[c]) + d_ref[c] * u_ref[:, cols]
            y_ref[:, cols] = y
            gys.append(_gelu(y)[0])
            part = _dot(gys[c], wg[c])
            q = part if c == 0 else q + part
        q_ref[...] = q
        sig = _sigmoid(q + bg_ref[...])
        zz = z_ref[...]
        sz = zz * _sigmoid(zz)
        o = h_ref[...]
        for k in range(4):
            cols = slice(k * cw, (k + 1) * cw)
            o = o + _dot(gys[k] * sig[:, cols] * sz[:, cols], wo[k])
        o_ref[...] = o

    row = pl.BlockSpec((tt, d), lambda i: (i, 0))
    slab, _ = _slab_spec(lp, tt, sw)
    sd = jax.ShapeDtypeStruct((lp, d), F32)
    return pl.pallas_call(
        body, name=name, grid=(lp // tt,),
        in_specs=[slab, row, row, row, pl.BlockSpec((4, 1, cw), lambda i: (0, 0, 0)), pl.BlockSpec((1, d), lambda i: (0, 0)),
                  ANY, ANY, ANY, ANY],
        out_specs=[row, row, row],
        out_shape=(sd, sd, sd),
        scratch_shapes=[pltpu.VMEM(cdre.shape, BF16), pltpu.VMEM(cdim.shape, BF16), pltpu.VMEM(w_glu.shape, BF16),
                        pltpu.VMEM(w_out.shape, BF16)],
        compiler_params=_params(),
    )(s, u, z, h, d_skip, b_glu, cdre, cdim, w_glu, w_out)


def s5_bwd3a(dh, y, q, z, w_glu, w_out, b_glu, name):
    lp, d = dh.shape
    tt = TOKEN_TILE
    nt = lp // tt
    cw = w_glu.shape[1]

    def body(dh_ref, y_ref, q_ref, z_ref, bg_ref, wg_hbm, wo_hbm, dy_ref, dp_ref, dwo_hbm, dwg_hbm, dbg_hbm,
             wg, wo, dwo, dwg, dbg):
        i = pl.program_id(0)

        @pl.when(i == 0)
        def _():
            pltpu.sync_copy(wg_hbm, wg)
            pltpu.sync_copy(wo_hbm, wo)
            dwo[...] = jnp.zeros_like(dwo)
            dwg[...] = jnp.zeros_like(dwg)
            dbg[...] = jnp.zeros_like(dbg)

        sig = _sigmoid(q_ref[...] + bg_ref[...])
        sz, dsz = _silu_and_grad(z_ref[...])
        dhv = dh_ref[...]
        yv = y_ref[...]
        gy, t = _gelu(yv)
        dq_parts, dgy_parts = [], []
        for k in range(4):
            cols = slice(k * cw, (k + 1) * cw)
            gy_k, sig_k, sz_k = gy[:, cols], sig[:, cols], sz[:, cols]
            y2 = gy_k * sig_k
            dy3 = _dot_nt(dhv, wo[k])
            dwo[k] += _dot_tn(y2 * sz_k, dhv)
            dy2 = dy3 * sz_k
            dp_ref[0, :, cols] = (dy3 * y2 * dsz[:, cols]).astype(BF16)
            dq_parts.append(dy2 * gy_k * sig_k * (1.0 - sig_k))
            dgy_parts.append(dy2 * sig_k)
        dq = jnp.concatenate(dq_parts, axis=1)
        dbg[...] += jnp.sum(dq, axis=0, keepdims=True)
        dgelu = _gelu_grad(yv, t)
        for k in range(4):
            cols = slice(k * cw, (k + 1) * cw)
            dwg[k] += _dot_tn(gy[:, cols], dq)
            dy_ref[:, cols] = (dgy_parts[k] + _dot_nt(dq, wg[k])) * dgelu[:, cols]

        @pl.when(i == nt - 1)
        def _():
            pltpu.sync_copy(dwo, dwo_hbm)
            pltpu.sync_copy(dwg, dwg_hbm)
            pltpu.sync_copy(dbg, dbg_hbm)

    row = pl.BlockSpec((tt, d), lambda i: (i, 0))
    sd = jax.ShapeDtypeStruct
    return pl.pallas_call(
        body, name=name, grid=(nt,),
        in_specs=[row, row, row, row, pl.BlockSpec((1, d), lambda i: (0, 0)), ANY, ANY],
        out_specs=[row, pl.BlockSpec((1, tt, d), lambda i: (1, i, 0)), ANY, ANY, ANY],
        out_shape=(sd((lp, d), F32), sd((2, lp, d), BF16), sd(w_out.shape, F32), sd(w_glu.shape, F32), sd((1, d), F32)),
        scratch_shapes=[pltpu.VMEM(w_glu.shape, BF16), pltpu.VMEM(w_out.shape, BF16),
                        pltpu.VMEM(w_out.shape, F32), pltpu.VMEM(w_glu.shape, F32), pltpu.VMEM((1, d), F32)],
        compiler_params=_params(),
    )(dh, y, q, z, b_glu, w_glu, w_out)


def s5_bwd3b(dy, s, u, cdre, cdim, d_skip, name):
    lp, d = dy.shape
    tt = TOKEN_TILE
    nt = lp // tt
    sw, cw = cdre.shape[1], cdre.shape[2]
    gc = cw // S5_GROUP

    def body(dy_ref, s_ref, u_ref, d_ref, cre_hbm, cim_hbm,
             ds_ref, dus_ref, dcre_ref, dcim_ref, dd_hbm, cre, cim, dcre, dcim, dd):
        i = pl.program_id(0)

        @pl.when(i == 0)
        def _():
            pltpu.sync_copy(cre_hbm, cre)
            pltpu.sync_copy(cim_hbm, cim)
            dcre[...] = jnp.zeros_like(dcre)
            dcim[...] = jnp.zeros_like(dcim)
            dd[...] = jnp.zeros_like(dd)

        for c in range(4):
            chunk = slice(c * cw, (c + 1) * cw)
            dyv = dy_ref[:, chunk]
            dd[c] += jnp.sum(dyv * u_ref[:, chunk], axis=0, keepdims=True)
            dus_ref[:, chunk] = dyv * d_ref[c]
            _slab_store(ds_ref, c, _dot_nt(dyv, cre[c]), _dot_nt(dyv, cim[c]))
            s_r, s_i = _slab_load(s_ref, c)
            dcre[c] += _dot_tn(s_r, dyv)
            dcim[c] += _dot_tn(s_i, dyv)

        @pl.when(i == nt - 1)
        def _():
            for k in range(4):
                for j in range(gc):
                    rows, cols = pl.ds(j * S5_STATE, S5_STATE), pl.ds(j * S5_GROUP, S5_GROUP)
                    dcre_ref[k, j] = dcre[k, rows, cols].T
                    dcim_ref[k, j] = dcim[k, rows, cols].T
            pltpu.sync_copy(dd, dd_hbm)

    sd = jax.ShapeDtypeStruct
    row = pl.BlockSpec((tt, d), lambda i: (i, 0))
    slab, slab_shape = _slab_spec(lp, tt, sw)
    diag = pl.BlockSpec((4, gc, S5_GROUP, S5_STATE), lambda i: (0, 0, 0, 0))
    return pl.pallas_call(
        body, name=name, grid=(nt,),
        in_specs=[row, slab, row, pl.BlockSpec((4, 1, cw), lambda i: (0, 0, 0)), ANY, ANY],
        out_specs=[slab, row, diag, diag, ANY],
        out_shape=(sd(slab_shape, F32), sd((lp, d), F32),
                   sd((4, gc, S5_GROUP, S5_STATE), F32), sd((4, gc, S5_GROUP, S5_STATE), F32), sd((4, 1, cw), F32)),
        scratch_shapes=[pltpu.VMEM(cdre.shape, BF16), pltpu.VMEM(cdim.shape, BF16),
                        pltpu.VMEM(cdre.shape, F32), pltpu.VMEM(cdim.shape, F32), pltpu.VMEM((4, 1, cw), F32)],
        compiler_params=_params(),
    )(dy, s, u, d_skip, cdre, cdim)


def s5_scan_bwd(g, s, ar, ai, name):
    nj = ar.shape[1]
    tt = TOKEN_TILE
    cpb = SCAN_CHUNKS
    nt = g.shape[0] // (4 * tt * nj)

    def body(g_ref, s_ref, ar_ref, ai_ref, lam_ref, dar_ref, dai_ref, st_r, st_i, acc_r, acc_i):
        i, cg = pl.program_id(0), pl.program_id(1)

        @pl.when((i == 0) & (cg == 0))
        def _():
            for ref in (st_r, st_i, acc_r, acc_i):
                ref[...] = jnp.zeros_like(ref)

        a_r = [ar_ref[cg * cpb + q] for q in range(cpb)]
        a_i = [ai_ref[cg * cpb + q] for q in range(cpb)]

        def slab(q, t):
            return pl.ds(pl.multiple_of((q * tt + t) * nj, nj), nj)

        def adjoint(q, t, l_r, l_i):
            rows = slab(q, t)
            g_r, g_i = _unpack_pair(g_ref[rows, :])
            n_r = g_r + a_r[q] * l_r + a_i[q] * l_i
            n_i = g_i + a_r[q] * l_i - a_i[q] * l_r
            lam_ref[rows, :] = _pack_pair(n_r, n_i)
            return n_r, n_i

        def pair(q, t, l_r, l_i, d_r, d_i):
            p_r, p_i = _unpack_pair(s_ref[slab(q, t), :])
            return d_r + l_r * p_r + l_i * p_i, d_i + l_i * p_r - l_r * p_i

        def step(k, carry):
            t = tt - 1 - k
            out = []
            for q in range(cpb):
                l_r, l_i, d_r, d_i = carry[q]
                l_r, l_i = adjoint(q, t, l_r, l_i)
                d_r, d_i = pair(q, t - 1, l_r, l_i, d_r, d_i)
                out.append((l_r, l_i, d_r, d_i))
            return tuple(out)

        init = []
        for q in range(cpb):
            ch = cg * cpb + q
            l_r, l_i = st_r[ch], st_i[ch]
            d_r, d_i = pair(q, tt - 1, l_r, l_i, acc_r[ch], acc_i[ch])
            init.append((l_r, l_i, d_r, d_i))
        final = lax.fori_loop(0, tt - 1, step, tuple(init), unroll=8)
        for q in range(cpb):
            ch = cg * cpb + q
            l_r, l_i, d_r, d_i = final[q]
            l_r, l_i = adjoint(q, 0, l_r, l_i)
            st_r[ch] = l_r
            st_i[ch] = l_i
            acc_r[ch] = d_r
            acc_i[ch] = d_i
            dar_ref[ch] = d_r
            dai_ref[ch] = d_i

    blk = pl.BlockSpec((cpb * tt * nj, 128), lambda i, cg: ((nt - 1 - i) * (4 // cpb) + cg, 0))
    par = pl.BlockSpec((4, nj, 128), lambda i, cg: (0, 0, 0))
    sd = jax.ShapeDtypeStruct
    return pl.pallas_call(
        body, name=name, grid=(nt, 4 // cpb),
        in_specs=[blk, blk, par, par], out_specs=[blk, par, par],
        out_shape=(sd(g.shape, F32), sd((4, nj, 128), F32), sd((4, nj, 128), F32)),
        scratch_shapes=[pltpu.VMEM((4, nj, 128), F32)] * 4,
        compiler_params=_params(ndim=2),
    )(g, s, ar, ai)


def s5_bwd1(lam, dus, u, dp, h, dh, g, w_in, bdre, bdim, name):
    lp, d = h.shape
    tt = TOKEN_TILE
    nt = lp // tt
    cw, sw = bdre.shape[1], bdre.shape[2]
    gc = cw // S5_GROUP

    def body(lam_ref, dus_ref, u_ref, dpz_ref, h_ref, dh_ref, g_ref, w_hbm, bre_hbm, bim_hbm,
             dpu_ref, dho_ref, n_ref, dbre_ref, dbim_ref, dg_hbm, w, bre, bim, dbre, dbim, dg):
        i = pl.program_id(0)

        @pl.when(i == 0)
        def _():
            pltpu.sync_copy(w_hbm, w)
            pltpu.sync_copy(bre_hbm, bre)
            pltpu.sync_copy(bim_hbm, bim)
            dbre[...] = jnp.zeros_like(dbre)
            dbim[...] = jnp.zeros_like(dbim)
            dg[...] = jnp.zeros_like(dg)

        dz = dpz_ref[0]
        dn = None
        for c in range(4):
            chunk = slice(c * cw, (c + 1) * cw)
            (l_r, l_i), uv = _slab_load(lam_ref, c), u_ref[:, chunk]
            du = dus_ref[:, chunk] + _dot_nt(l_r, bre[c]) + _dot_nt(l_i, bim[c])
            dbre[c] += _dot_tn(uv, l_r)
            dbim[c] += _dot_tn(uv, l_i)
            dpu_ref[0, :, chunk] = du.astype(BF16)
            part = _dot_nt(du, w[c]) + _dot_nt(dz[:, chunk], w[4 + c])
            dn = part if c == 0 else dn + part
        gv = g_ref[...]
        n, hh, rr = _rms_fwd(h_ref[...], gv)
        n_ref[...] = n.T.astype(BF16)
        dg[...] += jnp.sum(dn * hh, axis=0, keepdims=True)
        dho_ref[...] = dh_ref[...] + _rms_bwd(dn, hh, rr, gv)

        @pl.when(i == nt - 1)
        def _():
            for k in range(4):
                for j in range(gc):
                    rows, cols = pl.ds(j * S5_GROUP, S5_GROUP), pl.ds(j * S5_STATE, S5_STATE)
                    dbre_ref[k, j] = dbre[k, rows, cols]
                    dbim_ref[k, j] = dbim[k, rows, cols]
            pltpu.sync_copy(dg, dg_hbm)

    sd = jax.ShapeDtypeStruct
    row = pl.BlockSpec((tt, d), lambda i: (i, 0))
    slab, _ = _slab_spec(lp, tt, sw)
    diag = pl.BlockSpec((4, gc, S5_GROUP, S5_STATE), lambda i: (0, 0, 0, 0))
    return pl.pallas_call(
        body, name=name, grid=(nt,),
        in_specs=[slab, row, row, pl.BlockSpec((1, tt, d), lambda i: (1, i, 0)), row, row, pl.BlockSpec((1, d), lambda i: (0, 0)),
                  ANY, ANY, ANY],
        out_specs=[pl.BlockSpec((1, tt, d), lambda i: (0, i, 0)), row, pl.BlockSpec((d, tt), lambda i: (0, i)), diag, diag, ANY],
        out_shape=(sd(dp.shape, BF16), sd((lp, d), F32), sd((d, lp), BF16),
                   sd((4, gc, S5_GROUP, S5_STATE), F32), sd((4, gc, S5_GROUP, S5_STATE), F32), sd((1, d), F32)),
        input_output_aliases={3: 0},
        scratch_shapes=[pltpu.VMEM(w_in.shape, BF16), pltpu.VMEM(bdre.shape, BF16), pltpu.VMEM(bdim.shape, BF16),
                        pltpu.VMEM(bdre.shape, F32), pltpu.VMEM(bdim.shape, F32), pltpu.VMEM((1, d), F32)],
        compiler_params=_params(),
    )(lam, dus, u, dp, h, dh, g, w_in, bdre, bdim)


def grad_w_in(n_t, dp, blk, name):
    d, lp = n_t.shape
    npart, _, width = dp.shape
    per = width // blk

    def body(n_ref, dp_ref, o_ref):
        o_ref[0] = jnp.dot(n_ref[...], dp_ref[0], preferred_element_type=F32)

    return pl.pallas_call(
        body, name=name, grid=(npart * per,),
        in_specs=[pl.BlockSpec((d, lp), lambda j: (0, 0), pipeline_mode=pl.Buffered(1)),
                  pl.BlockSpec((1, lp, blk), lambda j: (j // per, 0, j % per))],
        out_specs=pl.BlockSpec((1, d, blk), lambda j: (j, 0, 0)),
        out_shape=jax.ShapeDtypeStruct((npart * per, d, blk), F32),
        compiler_params=_params(),
    )(n_t, dp)


def _conv_fwd_chunk(n, w, cw_ref, cb_ref, halo, c, nch):
    bg = jnp.dot(n, w[c], preferred_element_type=F32)
    cg = jnp.dot(n, w[nch + c], preferred_element_type=F32)
    v = jnp.dot(n, w[2 * nch + c], preferred_element_type=F32)
    z = jnp.dot(n, w[3 * nch + c], preferred_element_type=F32)
    hc = cg * v
    taps = cw_ref[c]
    conv = taps[2:3, :] * hc + taps[1:2, :] * _shift_down(hc, 1, halo) + taps[0:1, :] * _shift_down(hc, 2, halo) + cb_ref[c]
    return bg, cg, v, z, hc, conv


def conv_fwd(h, g, w_in, conv_w, conv_b, w_out, name):
    lp, d = h.shape
    tt = TOKEN_TILE
    nt = lp // tt
    nch, ce = w_out.shape[0], w_out.shape[1]

    def body(h_ref, g_ref, cw_ref, cb_ref, w_hbm, wo_hbm, o_ref, halo_ref, w, wo, halo):
        i = pl.program_id(0)

        @pl.when(i == 0)
        def _():
            pltpu.sync_copy(w_hbm, w)
            pltpu.sync_copy(wo_hbm, wo)
            halo[...] = jnp.zeros_like(halo)

        hv = h_ref[...]
        n = _rms_fwd(hv, g_ref[...])[0].astype(BF16)
        o = hv
        for c in range(nch):
            bg, _, _, z, hc, conv = _conv_fwd_chunk(n, w, cw_ref, cb_ref, halo[c], c, nch)
            o = o + _dot(bg * conv * (z * _sigmoid(z)), wo[c])
            halo[c] = hc[tt - CONV_HALO:, :]
            halo_ref[0, c] = hc[tt - CONV_HALO:, :]
        o_ref[...] = o

    sd = jax.ShapeDtypeStruct
    return pl.pallas_call(
        body, name=name, grid=(nt,),
        in_specs=[pl.BlockSpec((tt, d), lambda i: (i, 0)), pl.BlockSpec((1, d), lambda i: (0, 0)),
                  pl.BlockSpec(conv_w.shape, lambda i: (0, 0, 0)), pl.BlockSpec(conv_b.shape, lambda i: (0, 0, 0)), ANY, ANY],
        out_specs=[pl.BlockSpec((tt, d), lambda i: (i, 0)), pl.BlockSpec((1, nch, CONV_HALO, ce), lambda i: (i, 0, 0, 0))],
        out_shape=(sd((lp, d), F32), sd((nt, nch, CONV_HALO, ce), F32)),
        scratch_shapes=[pltpu.VMEM(w_in.shape, BF16), pltpu.VMEM(w_out.shape, BF16), pltpu.VMEM((nch, CONV_HALO, ce), F32)],
        compiler_params=_params(),
    )(h, g, conv_w, conv_b, w_in, w_out)


def conv_bwd(h, dh, halos, g, w_in, conv_w, conv_b, w_out, name):
    lp, d = h.shape
    tt = TOKEN_TILE
    nt = lp // tt
    nch, ce = w_out.shape[0], w_out.shape[1]

    def body(h_ref, dh_ref, halo_ref, g_ref, cw_ref, cb_ref, w_hbm, wo_hbm,
             dho_ref, n_ref, dp_ref, dwo_hbm, dcw_hbm, dcb_hbm, dg_hbm, w, wo, nxt, dwo, dcw, dcb, dg):
        i = pl.program_id(0)

        @pl.when(i == 0)
        def _():
            pltpu.sync_copy(w_hbm, w)
            pltpu.sync_copy(wo_hbm, wo)
            for ref in (nxt, dwo, dcw, dcb, dg):
                ref[...] = jnp.zeros_like(ref)

        gv = g_ref[...]
        nf, hh, rr = _rms_fwd(h_ref[...], gv)
        n = nf.astype(BF16)
        n_ref[...] = nf.T.astype(BF16)
        dhv = dh_ref[...]
        has_prev = (i < nt - 1).astype(F32)
        dn = jnp.zeros((tt, d), F32)
        for c in range(nch):
            halo = halo_ref[0, c] * has_prev
            bg, cg, v, z, hc, conv = _conv_fwd_chunk(n, w, cw_ref, cb_ref, halo, c, nch)
            sz, dsz = _silu_and_grad(z)
            y1 = bg * conv
            dy2 = _dot_nt(dhv, wo[c])
            dwo[c] += _dot_tn(y1 * sz, dhv)
            dy1 = dy2 * sz
            dz = dy2 * y1 * dsz
            dbg = dy1 * conv
            dconv = dy1 * bg
            dcb[c] += jnp.sum(dconv, axis=0, keepdims=True)
            up1 = _shift_up(dconv, 1, nxt[c])
            up2 = _shift_up(dconv, 2, nxt[c])
            nxt[c] = dconv[:CONV_HALO, :]
            taps = cw_ref[c]
            dhc = taps[2:3, :] * dconv + taps[1:2, :] * up1 + taps[0:1, :] * up2
            dcw[c, 0:1, :] += jnp.sum(hc * up2, axis=0, keepdims=True)
            dcw[c, 1:2, :] += jnp.sum(hc * up1, axis=0, keepdims=True)
            dcw[c, 2:3, :] += jnp.sum(hc * dconv, axis=0, keepdims=True)
            dcg = dhc * v
            dv = dhc * cg
            cols = slice(c * ce, (c + 1) * ce)
            for p, val in enumerate((dbg, dcg, dv, dz)):
                dp_ref[p, :, cols] = val.astype(BF16)
                dn = dn + _dot_nt(val, w[p * nch + c])
        dg[...] += jnp.sum(dn * hh, axis=0, keepdims=True)
        dho_ref[...] = dhv + _rms_bwd(dn, hh, rr, gv)

        @pl.when(i == nt - 1)
        def _():
            pltpu.sync_copy(dwo, dwo_hbm)
            pltpu.sync_copy(dcw, dcw_hbm)
            pltpu.sync_copy(dcb, dcb_hbm)
            pltpu.sync_copy(dg, dg_hbm)

    rev = lambda i: (nt - 1 - i, 0)
    sd = jax.ShapeDtypeStruct
    return pl.pallas_call(
        body, name=name, grid=(nt,),
        in_specs=[pl.BlockSpec((tt, d), rev), pl.BlockSpec((tt, d), rev),
                  pl.BlockSpec((1, nch, CONV_HALO, ce), lambda i: (jnp.maximum(nt - 2 - i, 0), 0, 0, 0)),
                  pl.BlockSpec((1, d), lambda i: (0, 0)),
                  pl.BlockSpec(conv_w.shape, lambda i: (0, 0, 0)), pl.BlockSpec(conv_b.shape, lambda i: (0, 0, 0)), ANY, ANY],
        out_specs=[pl.BlockSpec((tt, d), rev), pl.BlockSpec((d, tt), lambda i: (0, nt - 1 - i)),
                   pl.BlockSpec((4, tt, nch * ce), lambda i: (0, nt - 1 - i, 0)), ANY, ANY, ANY, ANY],
        out_shape=(sd((lp, d), F32), sd((d, lp), BF16), sd((4, lp, nch * ce), BF16),
                   sd(w_out.shape, F32), sd((nch, 8, ce), F32), sd((nch, 1, ce), F32), sd((1, d), F32)),
        scratch_shapes=[pltpu.VMEM(w_in.shape, BF16), pltpu.VMEM(w_out.shape, BF16), pltpu.VMEM((nch, CONV_HALO, ce), F32),
                        pltpu.VMEM(w_out.shape, F32), pltpu.VMEM((nch, 8, ce), F32), pltpu.VMEM((nch, 1, ce), F32),
                        pltpu.VMEM((1, d), F32)],
        compiler_params=_params(),
    )(h, dh, halos, g, conv_w, conv_b, w_in, w_out)


def _pool_fwd_group(n, w, wg, bg_ref, sc_ref, halo, k, tile, tt, first_pos):
    u = jnp.dot(n, w[k], preferred_element_type=F32)
    z = jnp.dot(n, w[4 + k], preferred_element_type=F32)
    ext = jnp.concatenate([halo, u], axis=0)
    win = _window_sums_back(ext)[k][POOL_HALO:, :]
    mixed = win * _pool_inv_count(tile, tt, first_pos, POOL_WINDOWS[k], u.shape[1]) - u
    outs = _dot(mixed, wg[k]) + bg_ref[k]
    return u, z, mixed, outs, outs * sc_ref[k]


def pool_fwd(h, g, w_in, w_grp, b_grp, scale, w_out, first_pos, name):
    lp, d = h.shape
    tt = TOKEN_TILE
    nt = lp // tt
    gw = w_grp.shape[1]

    def body(h_ref, g_ref, bg_ref, sc_ref, w_hbm, wg_hbm, wo_hbm, o_ref, halo_ref, w, wg, wo, halo):
        i = pl.program_id(0)

        @pl.when(i == 0)
        def _():
            pltpu.sync_copy(w_hbm, w)
            pltpu.sync_copy(wg_hbm, wg)
            pltpu.sync_copy(wo_hbm, wo)
            halo[...] = jnp.zeros_like(halo)

        hv = h_ref[...]
        n = _rms_fwd(hv, g_ref[...])[0].astype(BF16)
        o = hv
        for k in range(4):
            u, z, _, _, yp = _pool_fwd_group(n, w, wg, bg_ref, sc_ref, halo[k], k, i, tt, first_pos)
            o = o + _dot(yp * (z * _sigmoid(z)), wo[k])
            halo[k] = u[tt - POOL_HALO:, :]
            halo_ref[0, k] = u[tt - POOL_HALO:, :]
        o_ref[...] = o

    sd = jax.ShapeDtypeStruct
    small = pl.BlockSpec((4, 1, gw), lambda i: (0, 0, 0))
    return pl.pallas_call(
        body, name=name, grid=(nt,),
        in_specs=[pl.BlockSpec((tt, d), lambda i: (i, 0)), pl.BlockSpec((1, d), lambda i: (0, 0)), small, small, ANY, ANY, ANY],
        out_specs=[pl.BlockSpec((tt, d), lambda i: (i, 0)), pl.BlockSpec((1, 4, POOL_HALO, gw), lambda i: (i, 0, 0, 0))],
        out_shape=(sd((lp, d), F32), sd((nt, 4, POOL_HALO, gw), F32)),
        scratch_shapes=[pltpu.VMEM(w_in.shape, BF16), pltpu.VMEM(w_grp.shape, BF16), pltpu.VMEM(w_out.shape, BF16),
                        pltpu.VMEM((4, POOL_HALO, gw), F32)],
        compiler_params=_params(),
    )(h, g, b_grp, scale, w_in, w_grp, w_out)


def pool_bwd(h, dh, halos, g, w_in, w_grp, b_grp, scale, w_out, first_pos, name):
    lp, d = h.shape
    tt = TOKEN_TILE
    nt = lp // tt
    gw = w_grp.shape[1]

    def body(h_ref, dh_ref, halo_ref, g_ref, bg_ref, sc_ref, w_hbm, wg_hbm, wo_hbm,
             dho_ref, n_ref, dp_ref, dwo_hbm, dwg_hbm, dbg_hbm, dsc_hbm, dg_hbm,
             w, wg, wo, nxt, dwo, dwg, dbg, dsc, dg):
        i = pl.program_id(0)
        tile = nt - 1 - i

        @pl.when(i == 0)
        def _():
            pltpu.sync_copy(w_hbm, w)
            pltpu.sync_copy(wg_hbm, wg)
            pltpu.sync_copy(wo_hbm, wo)
            for ref in (nxt, dwo, dwg, dbg, dsc, dg):
                ref[...] = jnp.zeros_like(ref)

        gv = g_ref[...]
        nf, hh, rr = _rms_fwd(h_ref[...], gv)
        n = nf.astype(BF16)
        n_ref[...] = nf.T.astype(BF16)
        dhv = dh_ref[...]
        has_prev = (i < nt - 1).astype(F32)
        dn = jnp.zeros((tt, d), F32)
        for k in range(4):
            u, z, mixed, outs, yp = _pool_fwd_group(n, w, wg, bg_ref, sc_ref, halo_ref[0, k] * has_prev, k, tile, tt, first_pos)
            sz, dsz = _silu_and_grad(z)
            dy = _dot_nt(dhv, wo[k])
            dwo[k] += _dot_tn(yp * sz, dhv)
            dyp = dy * sz
            dz = dy * yp * dsz
            dsc[k] += jnp.sum(dyp * outs, axis=0, keepdims=True)
            douts = dyp * sc_ref[k]
            dbg[k] += jnp.sum(douts, axis=0, keepdims=True)
            dwg[k] += _dot_tn(mixed, douts)
            dmixed = _dot_nt(douts, wg[k])
            dm = dmixed * _pool_inv_count(tile, tt, first_pos, POOL_WINDOWS[k], gw)
            ext = jnp.concatenate([dm, nxt[k]], axis=0)
            du = _window_sums_fwd(ext)[k][:tt, :] - dmixed
            nxt[k] = dm[:POOL_HALO, :]
            cols = slice(k * gw, (k + 1) * gw)
            dp_ref[0, :, cols] = du.astype(BF16)
            dp_ref[1, :, cols] = dz.astype(BF16)
            dn = dn + _dot_nt(du, w[k]) + _dot_nt(dz, w[4 + k])
        dg[...] += jnp.sum(dn * hh, axis=0, keepdims=True)
        dho_ref[...] = dhv + _rms_bwd(dn, hh, rr, gv)

        @pl.when(i == nt - 1)
        def _():
            pltpu.sync_copy(dwo, dwo_hbm)
            pltpu.sync_copy(dwg, dwg_hbm)
            pltpu.sync_copy(dbg, dbg_hbm)
            pltpu.sync_copy(dsc, dsc_hbm)
            pltpu.sync_copy(dg, dg_hbm)

    rev = lambda i: (nt - 1 - i, 0)
    sd = jax.ShapeDtypeStruct
    small = pl.BlockSpec((4, 1, gw), lambda i: (0, 0, 0))
    return pl.pallas_call(
        body, name=name, grid=(nt,),
        in_specs=[pl.BlockSpec((tt, d), rev), pl.BlockSpec((tt, d), rev),
                  pl.BlockSpec((1, 4, POOL_HALO, gw), lambda i: (jnp.maximum(nt - 2 - i, 0), 0, 0, 0)),
                  pl.BlockSpec((1, d), lambda i: (0, 0)), small, small, ANY, ANY, ANY],
        out_specs=[pl.BlockSpec((tt, d), rev), pl.BlockSpec((d, tt), lambda i: (0, nt - 1 - i)),
                   pl.BlockSpec((2, tt, 4 * gw), lambda i: (0, nt - 1 - i, 0)), ANY, ANY, ANY, ANY, ANY],
        out_shape=(sd((lp, d), F32), sd((d, lp), BF16), sd((2, lp, 4 * gw), BF16),
                   sd(w_out.shape, F32), sd(w_grp.shape, F32), sd((4, 1, gw), F32), sd((4, 1, gw), F32), sd((1, d), F32)),
        scratch_shapes=[pltpu.VMEM(w_in.shape, BF16), pltpu.VMEM(w_grp.shape, BF16), pltpu.VMEM(w_out.shape, BF16),
                        pltpu.VMEM((4, POOL_HALO, gw), F32), pltpu.VMEM(w_out.shape, F32), pltpu.VMEM(w_grp.shape, F32),
                        pltpu.VMEM((4, 1, gw), F32), pltpu.VMEM((4, 1, gw), F32), pltpu.VMEM((1, d), F32)],
        compiler_params=_params(),
    )(h, dh, halos, g, b_grp, scale, w_in, w_grp, w_out)


def loss_head(h, target, g, pad_tiles, name):
    lp, d = h.shape
    tt = TOKEN_TILE
    nt = lp // tt

    def body(h_ref, t_ref, g_ref, dh_ref, dg_ref, loss_ref, acc):
        i = pl.program_id(0)

        @pl.when(i == 0)
        def _():
            acc[...] = jnp.zeros_like(acc)
            dg_ref[...] = jnp.zeros_like(dg_ref)

        @pl.when(i < pad_tiles)
        def _():
            dh_ref[...] = jnp.zeros_like(dh_ref)

        @pl.when(i >= pad_tiles)
        def _():
            gv = g_ref[...]
            n, hh, rr = _rms_fwd(h_ref[...], gv)
            err = n - t_ref[...]
            acc[...] += 0.5 * jnp.sum(jnp.mean(err * err, axis=-1, keepdims=True), axis=0, keepdims=True)
            dn = err * (1.0 / d)
            dg_ref[...] += jnp.sum(dn * hh, axis=0, keepdims=True)
            dh_ref[...] = _rms_bwd(dn, hh, rr, gv)

        loss_ref[...] = jnp.broadcast_to(acc[...], loss_ref.shape)

    sd = jax.ShapeDtypeStruct
    return pl.pallas_call(
        body, name=name, grid=(nt,),
        in_specs=[pl.BlockSpec((tt, d), lambda i: (i, 0)), pl.BlockSpec((tt, d), lambda i: (jnp.maximum(i - pad_tiles, 0), 0)),
                  pl.BlockSpec((1, d), lambda i: (0, 0))],
        out_specs=[pl.BlockSpec((tt, d), lambda i: (i, 0)), pl.BlockSpec((1, d), lambda i: (0, 0)),
                   pl.BlockSpec((8, 128), lambda i: (0, 0))],
        out_shape=(sd((lp, d), F32), sd((1, d), F32), sd((8, 128), F32)),
        scratch_shapes=[pltpu.VMEM((1, 1), F32)],
        compiler_params=_params(),
    )(h, target, g)


def exchange(arrs, gather, name):
    n = len(arrs)

    def body(*refs):
        ins, outs = refs[:n], refs[n:2 * n]
        send_sems, recv_sems, own_sems = refs[2 * n:]
        x, y, c = lax.axis_index("x"), lax.axis_index("y"), lax.axis_index("c")
        me = 4 * x + 2 * y + c
        own = []
        for a in range(n):
            cp = pltpu.make_async_copy(ins[a] if gather else ins[a].at[me], outs[a].at[me], own_sems.at[a])
            cp.start()
            own.append(cp)
        sent = []
        for k in range(1, N_DEV):
            px = 1 - x if k & 4 else x
            py = 1 - y if k & 2 else y
            pc = 1 - c if k & 1 else c
            peer = 4 * px + 2 * py + pc
            for a in range(n):
                cp = pltpu.make_async_remote_copy(
                    src_ref=ins[a] if gather else ins[a].at[peer], dst_ref=outs[a].at[me],
                    send_sem=send_sems.at[a, k - 1], recv_sem=recv_sems.at[a, k - 1],
                    device_id=(px, py, pc), device_id_type=pl.DeviceIdType.MESH)
                cp.start()
                sent.append((cp, a, k, peer, (px, py, pc)))
        for cp, a, k, peer, pid in sent:
            cp.wait_send()
            pltpu.make_async_remote_copy(
                src_ref=ins[a] if gather else ins[a].at[peer], dst_ref=outs[a].at[peer],
                send_sem=send_sems.at[a, k - 1], recv_sem=recv_sems.at[a, k - 1],
                device_id=pid, device_id_type=pl.DeviceIdType.MESH).wait_recv()
        for cp in own:
            cp.wait()

    hbm = pl.BlockSpec(memory_space=pltpu.HBM)
    out_shape = tuple(jax.ShapeDtypeStruct(((N_DEV,) + a.shape) if gather else a.shape, a.dtype) for a in arrs)
    return pl.pallas_call(
        body, name=name, in_specs=[hbm] * n, out_specs=[hbm] * n, out_shape=out_shape,
        scratch_shapes=[pltpu.SemaphoreType.DMA((n, N_DEV - 1)), pltpu.SemaphoreType.DMA((n, N_DEV - 1)),
                        pltpu.SemaphoreType.DMA((n,))],
    )(*[pltpu.with_memory_space_constraint(a, pltpu.HBM) for a in arrs])


def _peers(x, y, c):
    out = []
    for k in range(1, N_DEV):
        px = 1 - x if k & 4 else x
        py = 1 - y if k & 2 else y
        pc = 1 - c if k & 1 else c
        out.append((k, (px, py, pc), 4 * px + 2 * py + pc))
    return out


def exchange_start(arrs, gather, after, name):
    n = len(arrs)
    me = 4 * lax.axis_index("x") + 2 * lax.axis_index("y") + lax.axis_index("c")
    lands = []
    for a in arrs:
        own = a[None] if gather else lax.dynamic_index_in_dim(a, me, 0, keepdims=True)
        lands.append(lax.dynamic_update_index_in_dim(lax.empty(((N_DEV,) + a.shape) if gather else a.shape, a.dtype), own, me, 0))

    def body(*refs):
        ins, land = refs[:n], refs[n:2 * n]
        send_sems, recv_sems, token = refs[2 * n + 1], refs[2 * n + 2], refs[4 * n + 3]
        x, y, c = lax.axis_index("x"), lax.axis_index("y"), lax.axis_index("c")
        me = 4 * x + 2 * y + c
        for k, pid, peer in _peers(x, y, c):
            for a in range(n):
                pltpu.make_async_remote_copy(
                    src_ref=ins[a] if gather else ins[a].at[peer], dst_ref=land[a].at[me],
                    send_sem=send_sems.at[a * (N_DEV - 1) + k - 1], recv_sem=recv_sems.at[a * (N_DEV - 1) + k - 1],
                    device_id=pid, device_id_type=pl.DeviceIdType.MESH).start()
        token[...] = jnp.zeros_like(token)

    hbm = pl.BlockSpec(memory_space=pltpu.HBM)
    sem = pl.BlockSpec(memory_space=pltpu.SEMAPHORE)
    sems = pltpu.SemaphoreType.DMA((n * (N_DEV - 1),))
    res = pl.pallas_call(
        body, name=name, in_specs=[hbm] * (2 * n) + [ANY],
        out_specs=[sem, sem] + [hbm] * (2 * n) + [pl.BlockSpec(memory_space=pltpu.VMEM)],
        out_shape=[sems, sems] + [pltpu.HBM(a.shape, a.dtype) for a in arrs] + [pltpu.HBM(l.shape, l.dtype) for l in lands]
        + [jax.ShapeDtypeStruct((8, 128), F32)],
        input_output_aliases={a: 2 + a for a in range(2 * n)},
        compiler_params=pltpu.CompilerParams(has_side_effects=pltpu.SideEffectType.DATAFLOW_SIDE_EFFECTING),
    )(*[pltpu.with_memory_space_constraint(a, pltpu.HBM) for a in list(arrs) + lands], after)
    return res[0], res[1], res[2:2 + n], res[2 + n:2 + 2 * n], res[-1]


def exchange_wait(started, gather, after, name):
    send_sems, recv_sems, srcs, lands, _ = started
    n = len(srcs)
    after = list(after) if isinstance(after, (list, tuple)) else [after]

    def body(*refs):
        ins, land = refs[:n], refs[n:2 * n]
        send_sems, recv_sems = refs[2 * n], refs[2 * n + 1]
        x, y, c = lax.axis_index("x"), lax.axis_index("y"), lax.axis_index("c")
        for k, pid, peer in _peers(x, y, c):
            for a in range(n):
                cp = pltpu.make_async_remote_copy(
                    src_ref=ins[a] if gather else ins[a].at[peer], dst_ref=land[a].at[peer],
                    send_sem=send_sems.at[a * (N_DEV - 1) + k - 1], recv_sem=recv_sems.at[a * (N_DEV - 1) + k - 1],
                    device_id=pid, device_id_type=pl.DeviceIdType.MESH)
                cp.wait_send()
                cp.wait_recv()

    hbm = pl.BlockSpec(memory_space=pltpu.HBM)
    sem = pl.BlockSpec(memory_space=pltpu.SEMAPHORE)
    res = pl.pallas_call(
        body, name=name, in_specs=[hbm] * (2 * n) + [sem, sem] + [ANY] * len(after),
        out_specs=[hbm] * (2 * n),
        out_shape=[pltpu.HBM(a.shape, a.dtype) for a in list(srcs) + list(lands)],
        input_output_aliases={a: a for a in range(2 * n)},
        compiler_params=pltpu.CompilerParams(has_side_effects=pltpu.SideEffectType.DATAFLOW_SIDE_EFFECTING),
    )(*srcs, *lands, send_sems, recv_sems, *after)
    return res[n:]


def _adamw(w, g, m, v):
    m = ADAM_B1 * m + (1.0 - ADAM_B1) * g
    v = ADAM_B2 * v + (1.0 - ADAM_B2) * (g * g)
    m_hat = m / (1.0 - ADAM_B1 ** ADAM_STEP)
    v_hat = v / (1.0 - ADAM_B2 ** ADAM_STEP)
    return -ADAM_LR * (m_hat / (jnp.sqrt(v_hat) + ADAM_EPS) + ADAM_WD * w), m, v


def _update_tile_rows(rows, cols):
    if rows * cols <= UPDATE_TILE_ELEMS:
        return rows
    return max(t for t in range(8, UPDATE_TILE_ELEMS // cols + 1, 8) if rows % t == 0)


def _sum_in_order(p_ref):
    g = p_ref[0]
    for j in range(1, p_ref.shape[0]):
        g = g + p_ref[j]
    return g


def sum_parts(parts, name):
    nparts, rows, cols = parts.shape
    tr = _update_tile_rows(rows, cols)

    def body(p_ref, g_ref):
        g_ref[...] = _sum_in_order(p_ref)

    return pl.pallas_call(
        body, name=name, grid=(rows // tr,),
        in_specs=[pl.BlockSpec((nparts, tr, cols), lambda i: (0, i, 0))],
        out_specs=pl.BlockSpec((tr, cols), lambda i: (i, 0)), out_shape=jax.ShapeDtypeStruct((rows, cols), F32),
        compiler_params=_params(),
    )(parts)


def sum_adamw(parts, w, m, v, name):
    rows, cols = w.shape
    nparts = parts.shape[0]
    tr = _update_tile_rows(rows, cols)

    def body(p_ref, w_ref, m_ref, v_ref, g_ref, d_ref, nm_ref, nv_ref):
        g = _sum_in_order(p_ref)
        delta, nm, nv = _adamw(w_ref[...], g, m_ref[...], v_ref[...])
        g_ref[...] = g
        d_ref[...] = delta
        nm_ref[...] = nm
        nv_ref[...] = nv

    blk = pl.BlockSpec((tr, cols), lambda i: (i, 0))
    sd = jax.ShapeDtypeStruct((rows, cols), F32)
    return pl.pallas_call(
        body, name=name, grid=(rows // tr,),
        in_specs=[pl.BlockSpec((nparts, tr, cols), lambda i: (0, i, 0)), blk, blk, blk],
        out_specs=[blk] * 4, out_shape=(sd,) * 4,
        compiler_params=_params(),
    )(parts, w, m, v)


def update_packed(g, w, m, v, pieces, name):
    def body(g_ref, w_ref, m_ref, v_ref, *outs):
        gv = g_ref[...]
        res = (gv,) + _adamw(w_ref[...], gv, m_ref[...], v_ref[...])
        for k in range(4):
            outs[k][...] = res[k]
        for p, (row, rows, lanes) in enumerate(pieces):
            for k in range(4):
                outs[4 + 4 * p + k][...] = res[k][row:row + rows, :lanes]

    sd = jax.ShapeDtypeStruct
    shapes = [sd(w.shape, F32)] * 4 + [sd((rows, lanes), F32) for _, rows, lanes in pieces for _ in range(4)]
    return pl.pallas_call(body, name=name, out_shape=shapes,
                          compiler_params=pltpu.CompilerParams(vmem_limit_bytes=VMEM_LIMIT))(g, w, m, v)


S5_NAMES = ("w_in", "lam_re", "lam_im", "log_dt", "b_re", "b_im", "c_re", "c_im", "d_skip", "w_glu", "b_glu", "w_out")
CONV_NAMES = ("w_in", "conv_w", "conv_b", "w_out")
POOL_NAMES = ("w_in", "w_grp", "b_grp", "scale", "w_out")
LAYER_KINDS = ("s5", "conv", "pool", "s5")
LAYER_NAMES = {"s5": S5_NAMES, "conv": CONV_NAMES, "pool": POOL_NAMES}
SHARDED = {"s5": ("w_in", "w_glu", "w_out"), "conv": ("w_in", "conv_w", "w_out"), "pool": ("w_in", "w_grp", "b_grp", "w_out")}
GATHER_F32 = ("conv_w", "b_grp")


def weight_names():
    names = ["meta_tokens"]
    for i, kind in enumerate(LAYER_KINDS):
        names.append("norm%d_g" % i)
        names += ["l%d_%s" % (i, n) for n in LAYER_NAMES[kind]]
    names.append("final_g")
    return names


def sharded_names():
    return ["meta_tokens"] + ["l%d_%s" % (i, n) for i, kind in enumerate(LAYER_KINDS) for n in SHARDED[kind]]


def _block_diag_in(bb_t, gc):
    i, g, p = bb_t.shape
    t = bb_t.astype(BF16).reshape(i, 4, gc, p)
    return jnp.einsum("icjp,jk->cjikp", t, jnp.eye(gc, dtype=BF16)).reshape(4, gc * i, gc * p)


def _block_diag_in_grad(blocks):
    _, gc, i, p = blocks.shape
    return jnp.transpose(blocks, (2, 0, 1, 3)).reshape(i, 4 * gc, p)


def _block_diag_out(cc, gc):
    g, i, p = cc.shape
    return jnp.einsum("cjip,jk->cjpki", cc.astype(BF16).reshape(4, gc, i, p), jnp.eye(gc, dtype=BF16)).reshape(4, gc * p, gc * i)


def _block_diag_out_grad(blocks):
    _, gc, i, p = blocks.shape
    return blocks.reshape(4 * gc, i, p)


def _to_owner_blocks(a, axis):
    shape = a.shape[:axis] + (N_DEV, a.shape[axis] // N_DEV) + a.shape[axis + 1:]
    return jnp.moveaxis(a.reshape(shape), axis, 0)


def _from_owner_blocks(a, axis):
    a = jnp.moveaxis(a, 0, axis)
    return a.reshape(a.shape[:axis] + (a.shape[axis] * a.shape[axis + 1],) + a.shape[axis + 2:])


def _step(x, target, weights, moments_m, moments_v):
    seq, d = x.shape[1], x.shape[2]
    n_meta = weights["meta_tokens"].shape[0]
    tt = TOKEN_TILE
    pad_tiles = -(-n_meta // tt)
    p0 = pad_tiles * tt
    lp = p0 + seq
    first_pos = p0 - n_meta
    gc = d // 4 // S5_GROUP
    cw = d // 4

    big_names = [n for n in sharded_names() if n != "meta_tokens" and n.split("_", 1)[1] not in GATHER_F32]
    small_names = [n for n in sharded_names() if n not in big_names]
    layer_big = [[n for n in big_names if n.startswith("l%d_" % i)] for i in range(len(LAYER_KINDS))]
    layer_big[0] = small_names + layer_big[0]
    gather_started = []
    after = jnp.zeros((8, 128), F32)
    for i, names in enumerate(layer_big):
        gather_started.append(exchange_start([weights[n] if n in small_names else weights[n].astype(BF16) for n in names], True,
                                             after, "gather_start_l%d" % i))
        after = gather_started[-1][4]

    def vec(name):
        return weights[name].reshape(1, -1)

    s5_prep = {}
    for i, kind in enumerate(LAYER_KINDS):
        if kind == "s5":
            p = "l%d_" % i
            lr, li = weights[p + "lam_re"], weights[p + "lam_im"] + after[0, 0]
            ldt = weights[p + "log_dt"].reshape(-1, 1)
            br_t = jnp.transpose(weights[p + "b_re"], (2, 0, 1))
            bi_t = jnp.transpose(weights[p + "b_im"], (2, 0, 1))
            ar, ai, bbr, bbi = s5_disc_fwd(lr, li, ldt, br_t, bi_t, p + "disc_fwd")
            s5_prep[i] = dict(
                disc=(lr, li, ldt, br_t, bi_t), ar=ar.reshape(4, -1, 128), ai=ai.reshape(4, -1, 128),
                bdre=_block_diag_in(bbr, gc), bdim=_block_diag_in(bbi, gc),
                cdre=_block_diag_out(weights[p + "c_re"], gc), cdim=_block_diag_out(-weights[p + "c_im"], gc),
                d_skip=weights[p + "d_skip"].reshape(4, 1, cw), b_glu=vec(p + "b_glu"))
    h = jnp.concatenate([jnp.zeros((p0, d), F32), x[0] + after[0, 0]], axis=0)

    prepared = [h] + [s5_prep[i][k] for i in s5_prep for k in ("bdre", "bdim", "cdre", "cdim")]
    gathered = dict(zip(layer_big[0], exchange_wait(gather_started[0], True, prepared, "gather_wait_l0")))
    h = lax.dynamic_update_slice(h, _from_owner_blocks(gathered["meta_tokens"], 1), (first_pos, 0))

    full = {}

    def layer_weights(i, kind, after):
        p = "l%d_" % i
        if i > 0:
            gathered.update(zip(layer_big[i], exchange_wait(gather_started[i], True, after, "gather_wait_l%d" % i)))
        w_in = gathered[p + "w_in"]
        if kind == "s5":
            full[i] = dict(s5_prep[i], w_in=w_in, w_glu=gathered[p + "w_glu"].reshape(4, cw, d),
                           w_out=gathered[p + "w_out"].reshape(4, cw, d))
        elif kind == "conv":
            ce = w_in.shape[2]
            nch = 2
            conv_w = _from_owner_blocks(gathered[p + "conv_w"], 1)
            full[i] = dict(
                w_in=w_in, conv_w=jnp.transpose(conv_w.reshape(CONV_K, nch, ce), (1, 0, 2)),
                conv_b=weights[p + "conv_b"].reshape(nch, 1, ce), w_out=gathered[p + "w_out"].reshape(nch, ce, d))
        else:
            gw = w_in.shape[2]
            full[i] = dict(
                w_in=w_in, w_grp=_from_owner_blocks(gathered[p + "w_grp"], 1),
                b_grp=_from_owner_blocks(gathered[p + "b_grp"], 1).reshape(4, 1, gw),
                scale=weights[p + "scale"].reshape(4, 1, gw), w_out=gathered[p + "w_out"].reshape(4, gw, d))
        return full[i]

    saved = {}
    for i, kind in enumerate(LAYER_KINDS):
        p, f, g = "l%d_" % i, layer_weights(i, kind, h), vec("norm%d_g" % i)
        if kind == "s5":
            u, z, xs = s5_fwd1(h, g, f["w_in"], f["bdre"], f["bdim"], p + "fwd_in")
            s = s5_scan_fwd(xs, f["ar"], f["ai"], p + "scan_fwd")
            h_in = h
            h, y, q = s5_fwd3(s, u, z, h, f["cdre"], f["cdim"], f["w_glu"], f["w_out"], f["d_skip"], f["b_glu"], p + "fwd_out")
            saved[i] = (h_in, u, z, s, y, q)
        elif kind == "conv":
            h_new, halos = conv_fwd(h, g, f["w_in"], f["conv_w"], f["conv_b"], f["w_out"], p + "fwd")
            saved[i] = (h, halos)
            h = h_new
        else:
            h_new, halos = pool_fwd(h, g, f["w_in"], f["w_grp"], f["b_grp"], f["scale"], f["w_out"], first_pos, p + "fwd")
            saved[i] = (h, halos)
            h = h_new

    dh, dg_final, loss_tile = loss_head(h, target[0], vec("final_g"), pad_tiles, "loss_head")
    loss = lax.psum(loss_tile[0, 0], ("x", "y", "c"))

    grads = {"final_g": dg_final}
    names = weight_names()
    sh_names = sharded_names()
    rep_names = [n for n in names if n not in sh_names]

    def owner_blocks(a):
        return a.reshape(N_DEV, -1, a.shape[-1])

    def as2d(a):
        return a.reshape(-1, a.shape[-1])

    def pack(tree):
        flat = [jnp.pad(tree[n].reshape(-1), (0, -tree[n].size % 1024)) for n in rep_names]
        flat = jnp.concatenate(flat)
        return jnp.pad(flat, (0, -flat.size % (PACK_ROWS * 128))).reshape(-1, 128)

    layer_sharded, scatter_started = {}, {}
    ordered = jnp.zeros((), F32)
    for i in reversed(range(len(LAYER_KINDS))):
        kind = LAYER_KINDS[i]
        p, f, g = "l%d_" % i, full[i], vec("norm%d_g" % i) + ordered
        if kind == "s5":
            h_in, u, z, s, y, q = saved[i]
            dy, dp, dwo, dwg, dbg = s5_bwd3a(dh, y, q, z, f["w_glu"], f["w_out"], f["b_glu"] + ordered, p + "bwd_out")
            d_skip = f["d_skip"]
            if i == 0:
                early_names = [p + "w_glu", p + "w_out"]
                scatter_started["early"] = exchange_start([dwg.reshape(N_DEV, -1, d), dwo.reshape(N_DEV, -1, d)], False, dy,
                                                          "scatter_start_l0_early")
                d_skip = d_skip + scatter_started["early"][4][0, 0]
            ds, dus, dcre, dcim, dd = s5_bwd3b(dy, s, u, f["cdre"], f["cdim"], d_skip, p + "bwd_read")
            lam, dar, dai = s5_scan_bwd(ds, s, f["ar"], f["ai"], p + "scan_bwd")
            dp, dh, n, dbre, dbim, dg = s5_bwd1(lam, dus, u, dp, h_in, dh, g, f["w_in"], f["bdre"], f["bdim"], p + "bwd_in")
            dw_in = grad_w_in(n, dp, f["w_in"].shape[2], p + "grad_w_in")
            grads.update({p + "w_in": dw_in, p + "w_glu": dwg.reshape(N_DEV, -1, d), p + "w_out": dwo.reshape(N_DEV, -1, d),
                          p + "d_skip": dd, p + "b_glu": dbg})

            def replicated_grads(p=p, f=f, dar=dar, dai=dai, dbre=dbre, dbim=dbim, dcre=dcre, dcim=dcim, token=None):
                lr, li, ldt, br_t, bi_t = f["disc"]
                dlr, dli, dldt, dbr_t, dbi_t = s5_disc_bwd(
                    lr, li, ldt, br_t, bi_t, dar.reshape(lr.shape) + token, dai.reshape(lr.shape),
                    _block_diag_in_grad(dbre), _block_diag_in_grad(dbim), p + "disc_bwd")
                grads.update({
                    p + "lam_re": dlr, p + "lam_im": dli, p + "log_dt": dldt,
                    p + "b_re": jnp.transpose(dbr_t, (1, 2, 0)), p + "b_im": jnp.transpose(dbi_t, (1, 2, 0)),
                    p + "c_re": _block_diag_out_grad(dcre), p + "c_im": -_block_diag_out_grad(dcim)})
        elif kind == "conv":
            replicated_grads = None
            h_in, halos = saved[i]
            dh, n, dp, dwo, dcw, dcb, dg = conv_bwd(h_in, dh, halos, g, f["w_in"], f["conv_w"], f["conv_b"], f["w_out"], p + "bwd")
            dw_in = grad_w_in(n, dp, f["w_in"].shape[2], p + "grad_w_in")
            dconv_w = jnp.transpose(dcw[:, :CONV_K, :], (1, 0, 2)).reshape(CONV_K, -1)
            grads.update({p + "w_in": dw_in, p + "conv_w": _to_owner_blocks(dconv_w, 1), p + "conv_b": dcb,
                          p + "w_out": dwo.reshape(N_DEV, -1, d)})
        else:
            replicated_grads = None
            h_in, halos = saved[i]
            dh, n, dp, dwo, dwgrp, dbgrp, dsc, dg = pool_bwd(h_in, dh, halos, g, f["w_in"], f["w_grp"], f["b_grp"], f["scale"],
                                                             f["w_out"], first_pos, p + "bwd")
            dw_in = grad_w_in(n, dp, f["w_in"].shape[2], p + "grad_w_in")
            grads.update({p + "w_in": dw_in, p + "w_grp": _to_owner_blocks(dwgrp, 1),
                          p + "b_grp": _to_owner_blocks(dbgrp.reshape(4, -1), 1), p + "scale": dsc,
                          p + "w_out": dwo.reshape(N_DEV, -1, d)})
        grads["norm%d_g" % i] = dg
        layer_sharded[i] = ["l%d_%s" % (i, n) for n in SHARDED[kind]]
        if i > 0:
            scatter_started[i] = exchange_start([owner_blocks(grads[n]) for n in layer_sharded[i]], False, dh,
                                                "scatter_start_l%d" % i)
            ordered = scatter_started[i][4][0, 0]
        if replicated_grads is not None:
            replicated_grads(token=ordered)
    grad_x = dh[p0:][None]
    grads["meta_tokens"] = _to_owner_blocks(dh[first_pos:p0], 1)
    last = len(LAYER_KINDS)
    layer_sharded[last] = ["meta_tokens", "replicated"]
    scatter_started[last] = exchange_start([owner_blocks(grads["meta_tokens"]), pack(grads).reshape(N_DEV, -1, 128)], False,
                                           dh, "scatter_start_replicated")
    layer_sharded["early"] = early_names
    layer_sharded[0] = [n for n in layer_sharded[0] if n not in early_names]

    out = {}
    received = {}
    after = [scatter_started[last][4]]
    for i in list(reversed(range(1, last))) + [last, "early", 0]:
        received.update(zip(layer_sharded[i], exchange_wait(scatter_started[i], False, after, "scatter_wait_%s" % i)))
        updated = []
        for n in layer_sharded[i]:
            if n != "replicated":
                res = sum_adamw(received[n], as2d(weights[n]), as2d(moments_m[n]), as2d(moments_v[n]), "update_" + n)
                out[n] = [r.reshape(weights[n].shape) for r in res]
                updated.append(out[n][0])
        after = updated or after
        if i == last:
            g_full = exchange([sum_parts(received["replicated"], "sum_replicated")], True, "gather_small_grads")[0]
            scatter_started[0] = exchange_start([owner_blocks(grads[n]) for n in layer_sharded[0]], False, g_full,
                                                "scatter_start_l0")
            g_full = g_full.reshape(-1, 128) + scatter_started[0][4][0, 0]
            offsets, offset = {}, 0
            for n in rep_names:
                offsets[n] = offset
                offset += weights[n].size + (-weights[n].size % 1024)
            vectors = [n for n in rep_names if weights[n].ndim == 1]
            pieces = [(offsets[n] // 128, max(weights[n].size // 128, 1), min(weights[n].size, 128)) for n in vectors]
            res = update_packed(g_full, pack(weights), pack(moments_m), pack(moments_v), pieces, "update_replicated")
            packed = res[:4]
            for j, n in enumerate(vectors):
                out[n] = [r.reshape(weights[n].shape) for r in res[4 + 4 * j:8 + 4 * j]]
            for n in rep_names:
                if n not in vectors:
                    size = weights[n].size
                    out[n] = [r.reshape(-1)[offsets[n]:offsets[n] + size].reshape(weights[n].shape) for r in packed]
            after = [out[n][k] for n in rep_names for k in range(4)]

    return (loss, grad_x) + tuple(out[n][k] for k in range(4) for n in names)


def kernel(x, *rest):
    names = weight_names()
    nw = len(names)
    weights = dict(zip(names, rest[:nw]))
    target = rest[nw]
    moments_m = dict(zip(names, rest[nw + 1:2 * nw + 1]))
    moments_v = dict(zip(names, rest[2 * nw + 1:3 * nw + 1]))
    return _step(x, target, weights, moments_m, moments_v)
```

```python
import functools
import math

import jax
import jax.numpy as jnp
from jax import lax
from jax.experimental import pallas as pl
from jax.experimental.pallas import tpu as pltpu

F32 = jnp.float32
BF16 = jnp.bfloat16
EPS = 1e-6
N_DEV = 8
TOKEN_TILE = 256
SCAN_CHUNKS = 4
S5_GROUP = 16
S5_STATE = 64
POOL_WINDOWS = (2, 4, 8, 16)
POOL_HALO = 16
CONV_K = 3
CONV_HALO = 8
ADAM_LR = 0.001
ADAM_B1 = 0.9
ADAM_B2 = 0.999
ADAM_EPS = 1e-08
ADAM_WD = 0.01
ADAM_STEP = 10
GELU_C = math.sqrt(2.0 / math.pi)
GELU_A = 0.044715
UPDATE_TILE_ELEMS = 1 << 17
PACK_ROWS = 512
VMEM_LIMIT = 56 << 20

ANY = pl.BlockSpec(memory_space=pl.ANY)


def _params(vmem=VMEM_LIMIT, ndim=1):
    return pltpu.CompilerParams(vmem_limit_bytes=vmem, dimension_semantics=("arbitrary",) * ndim)


def _dot(a, b):
    return jnp.dot(a.astype(BF16), b.astype(BF16), preferred_element_type=F32)


def _dot_nt(a, b):
    return lax.dot_general(a.astype(BF16), b.astype(BF16), (((1,), (1,)), ((), ())), preferred_element_type=F32)


def _dot_tn(a, b):
    return lax.dot_general(a.astype(BF16), b.astype(BF16), (((0,), (0,)), ((), ())), preferred_element_type=F32)


def _rms_fwd(h, g):
    r = lax.rsqrt(jnp.mean(h * h, axis=-1, keepdims=True) + EPS)
    hh = h * r
    return hh * g, hh, r


def _rms_bwd(dn, hh, r, g):
    dhh = dn * g
    return r * (dhh - hh * jnp.mean(dhh * hh, axis=-1, keepdims=True))


def _sigmoid(x):
    return 1.0 / (1.0 + jnp.exp(-x))


def _silu_and_grad(z):
    s = _sigmoid(z)
    return z * s, s * (1.0 + z * (1.0 - s))


def _gelu(y):
    t = jnp.tanh(GELU_C * (y + GELU_A * y * y * y))
    return 0.5 * y * (1.0 + t), t


def _gelu_grad(y, t):
    return 0.5 * (1.0 + t) + 0.5 * y * (1.0 - t * t) * GELU_C * (1.0 + 3.0 * GELU_A * y * y)


def _rows(shape):
    return lax.broadcasted_iota(jnp.int32, shape, 0)


def _shift_down(x, k, halo):
    y = pltpu.roll(x, k, 0)
    rows = _rows(x.shape)
    for j in range(k):
        y = jnp.where(rows == j, halo[halo.shape[0] - k + j:halo.shape[0] - k + j + 1, :], y)
    return y


def _shift_up(x, k, halo):
    n = x.shape[0]
    y = pltpu.roll(x, n - k, 0)
    rows = _rows(x.shape)
    for j in range(k):
        y = jnp.where(rows == n - k + j, halo[j:j + 1, :], y)
    return y


def _window_sums_back(ext):
    out = []
    s = ext
    for k in (1, 2, 4, 8):
        s = s + pltpu.roll(s, k, 0)
        out.append(s)
    return out


def _window_sums_fwd(ext):
    n = ext.shape[0]
    out = []
    s = ext
    for k in (1, 2, 4, 8):
        s = s + pltpu.roll(s, n - k, 0)
        out.append(s)
    return out


def _pool_inv_count(tile, tt, first_pos, w, width):
    pos = _rows((tt, width)) + (tile * tt - first_pos + 1)
    return 1.0 / jnp.clip(pos, 1, w).astype(F32)


def _slab_spec(lp, tt, sw):
    nj = sw // 128
    return pl.BlockSpec((4 * tt * nj, 128), lambda i: (i, 0)), (lp * 4 * nj, 128)


def _pack_pair(re, im):
    def rounded(v):
        return lax.bitcast_convert_type(v, jnp.int32) + 0x8000
    return lax.bitcast_convert_type((rounded(re) & -65536) | lax.shift_right_logical(rounded(im), 16), F32)


def _unpack_pair(w):
    b = lax.bitcast_convert_type(w, jnp.int32)
    return lax.bitcast_convert_type(b & -65536, F32), lax.bitcast_convert_type(lax.shift_left(b, 16), F32)


def _slab_load(ref, c):
    nj = ref.shape[0] // (4 * TOKEN_TILE)
    first = c * TOKEN_TILE * nj
    return _unpack_pair(jnp.concatenate([ref[pl.ds(first + j, TOKEN_TILE, stride=nj), :] for j in range(nj)], axis=1))


def _slab_store(ref, c, re, im):
    nj = ref.shape[0] // (4 * TOKEN_TILE)
    first = c * TOKEN_TILE * nj
    val = _pack_pair(re, im)
    for j in range(nj):
        ref[pl.ds(first + j, TOKEN_TILE, stride=nj), :] = val[:, j * 128:(j + 1) * 128]


def _s5_disc_math(lr, li, ldt, br, bi):
    dt = jnp.exp(ldt)
    mag = jnp.exp(lr * dt)
    ar = mag * jnp.cos(li * dt)
    ai = mag * jnp.sin(li * dt)
    den = lr * lr + li * li
    kr = ((ar - 1.0) * lr + ai * li) / den
    ki = (ai * lr - (ar - 1.0) * li) / den
    bbr = kr[None] * br - ki[None] * bi
    bbi = kr[None] * bi + ki[None] * br
    return ar, ai, bbr, bbi


def s5_disc_fwd(lr, li, ldt, br_t, bi_t, name):
    def body(lr_ref, li_ref, ldt_ref, br_ref, bi_ref, ar_ref, ai_ref, bbr_ref, bbi_ref):
        ar, ai, bbr, bbi = _s5_disc_math(lr_ref[...], li_ref[...], ldt_ref[...], br_ref[...], bi_ref[...])
        ar_ref[...] = ar
        ai_ref[...] = ai
        bbr_ref[...] = bbr
        bbi_ref[...] = bbi

    sd = jax.ShapeDtypeStruct
    return pl.pallas_call(
        body, name=name,
        out_shape=(sd(lr.shape, F32), sd(lr.shape, F32), sd(br_t.shape, F32), sd(br_t.shape, F32)),
    )(lr, li, ldt, br_t, bi_t)


def s5_disc_bwd(lr, li, ldt, br_t, bi_t, dar, dai, dbbr, dbbi, name):
    def body(lr_ref, li_ref, ldt_ref, br_ref, bi_ref, dar_ref, dai_ref, dbbr_ref, dbbi_ref,
             dlr_ref, dli_ref, dldt_ref, dbr_ref, dbi_ref):
        _, vjp = jax.vjp(_s5_disc_math, lr_ref[...], li_ref[...], ldt_ref[...], br_ref[...], bi_ref[...])
        dlr, dli, dldt, dbr, dbi = vjp((dar_ref[...], dai_ref[...], dbbr_ref[...], dbbi_ref[...]))
        dlr_ref[...] = dlr
        dli_ref[...] = dli
        dldt_ref[...] = dldt
        dbr_ref[...] = dbr
        dbi_ref[...] = dbi

    sd = jax.ShapeDtypeStruct
    return pl.pallas_call(
        body, name=name,
        out_shape=(sd(lr.shape, F32), sd(lr.shape, F32), sd(ldt.shape, F32), sd(br_t.shape, F32), sd(br_t.shape, F32)),
    )(lr, li, ldt, br_t, bi_t, dar, dai, dbbr, dbbi)


def s5_fwd1(h, g, w_in, bdre, bdim, name):
    lp, d = h.shape
    tt = TOKEN_TILE
    cw, sw = bdre.shape[1], bdre.shape[2]

    def body(h_ref, g_ref, w_hbm, bdre_hbm, bdim_hbm, u_ref, z_ref, x_ref, w, bre, bim):
        @pl.when(pl.program_id(0) == 0)
        def _():
            pltpu.sync_copy(w_hbm, w)
            pltpu.sync_copy(bdre_hbm, bre)
            pltpu.sync_copy(bdim_hbm, bim)

        n = _rms_fwd(h_ref[...], g_ref[...])[0].astype(BF16)
        for c in range(4):
            cols = slice(c * cw, (c + 1) * cw)
            u = jnp.dot(n, w[c], preferred_element_type=F32)
            u_ref[:, cols] = u
            z_ref[:, cols] = jnp.dot(n, w[c + 4], preferred_element_type=F32)
            ub = u.astype(BF16)
            _slab_store(x_ref, c, jnp.dot(ub, bre[c], preferred_element_type=F32), jnp.dot(ub, bim[c], preferred_element_type=F32))

    sd = jax.ShapeDtypeStruct
    slab, slab_shape = _slab_spec(lp, tt, sw)
    row = pl.BlockSpec((tt, d), lambda i: (i, 0))
    return pl.pallas_call(
        body, name=name, grid=(lp // tt,),
        in_specs=[row, pl.BlockSpec((1, d), lambda i: (0, 0)), ANY, ANY, ANY],
        out_specs=[row, row, slab],
        out_shape=(sd((lp, d), F32), sd((lp, d), F32), sd(slab_shape, F32)),
        scratch_shapes=[pltpu.VMEM(w_in.shape, BF16), pltpu.VMEM(bdre.shape, BF16), pltpu.VMEM(bdim.shape, BF16)],
        compiler_params=_params(),
    )(h, g, w_in, bdre, bdim)


def s5_scan_fwd(x, ar, ai, name):
    nj = ar.shape[1]
    tt = TOKEN_TILE
    cpb = SCAN_CHUNKS
    nt = x.shape[0] // (4 * tt * nj)

    def body(x_ref, ar_ref, ai_ref, s_ref, st_r, st_i):
        i, cg = pl.program_id(0), pl.program_id(1)

        @pl.when(i == 0)
        def _():
            for q in range(cpb):
                st_r[cg * cpb + q] = jnp.zeros((nj, 128), F32)
                st_i[cg * cpb + q] = jnp.zeros((nj, 128), F32)

        a_r = [ar_ref[cg * cpb + q] for q in range(cpb)]
        a_i = [ai_ref[cg * cpb + q] for q in range(cpb)]

        def step(t, carry):
            out = []
            for q in range(cpb):
                s_r, s_i = carry[q]
                rows = pl.ds(pl.multiple_of((q * tt + t) * nj, nj), nj)
                x_r, x_i = _unpack_pair(x_ref[rows, :])
                n_r = a_r[q] * s_r - a_i[q] * s_i + x_r
                n_i = a_r[q] * s_i + a_i[q] * s_r + x_i
                s_ref[rows, :] = _pack_pair(n_r, n_i)
                out.append((n_r, n_i))
            return tuple(out)

        init = tuple((st_r[cg * cpb + q], st_i[cg * cpb + q]) for q in range(cpb))
        final = lax.fori_loop(0, tt, step, init, unroll=8)
        for q in range(cpb):
            st_r[cg * cpb + q] = final[q][0]
            st_i[cg * cpb + q] = final[q][1]

    blk = pl.BlockSpec((cpb * tt * nj, 128), lambda i, cg: (i * (4 // cpb) + cg, 0))
    par = pl.BlockSpec((4, nj, 128), lambda i, cg: (0, 0, 0))
    sd = jax.ShapeDtypeStruct
    return pl.pallas_call(
        body, name=name, grid=(nt, 4 // cpb),
        in_specs=[blk, par, par], out_specs=blk,
        out_shape=sd(x.shape, F32),
        scratch_shapes=[pltpu.VMEM((4, nj, 128), F32), pltpu.VMEM((4, nj, 128), F32)],
        compiler_params=_params(ndim=2),
    )(x, ar, ai)


def s5_fwd3(s, u, z, h, cdre, cdim, w_glu, w_out, d_skip, b_glu, name):
    lp, d = h.shape
    tt = TOKEN_TILE
    sw, cw = cdre.shape[1], cdre.shape[2]

    def body(s_ref, u_ref, z_ref, h_ref, d_ref, bg_ref, cre_hbm, cim_hbm, wg_hbm, wo_hbm,
             o_ref, y_ref, q_ref, cre, cim, wg, wo):
        @pl.when(pl.program_id(0) == 0)
        def _():
            pltpu.sync_copy(cre_hbm, cre)
            pltpu.sync_copy(cim_hbm, cim)
            pltpu.sync_copy(wg_hbm, wg)
            pltpu.sync_copy(wo_hbm, wo)

        gys, q = [], None
        for c in range(4):
            cols = slice(c * cw, (c + 1) * cw)
            s_r, s_i = _slab_load(s_ref, c)
            y = _dot(s_r, cre[c]) + _dot(s_i, cim[c]) + d_ref[c] * u_ref[:, cols]
            y_ref[:, cols] = y
            gys.append(_gelu(y)[0])
            part = _dot(gys[c], wg[c])
            q = part if c == 0 else q + part
        q_ref[...] = q
        sig = _sigmoid(q + bg_ref[...])
        zz = z_ref[...]
        sz = zz * _sigmoid(zz)
        o = h_ref[...]
        for k in range(4):
            cols = slice(k * cw, (k + 1) * cw)
            o = o + _dot(gys[k] * sig[:, cols] * sz[:, cols], wo[k])
        o_ref[...] = o

    row = pl.BlockSpec((tt, d), lambda i: (i, 0))
    slab, _ = _slab_spec(lp, tt, sw)
    sd = jax.ShapeDtypeStruct((lp, d), F32)
    return pl.pallas_call(
        body, name=name, grid=(lp // tt,),
        in_specs=[slab, row, row, row, pl.BlockSpec((4, 1, cw), lambda i: (0, 0, 0)), pl.BlockSpec((1, d), lambda i: (0, 0)),
                  ANY, ANY, ANY, ANY],
        out_specs=[row, row, row],
        out_shape=(sd, sd, sd),
        scratch_shapes=[pltpu.VMEM(cdre.shape, BF16), pltpu.VMEM(cdim.shape, BF16), pltpu.VMEM(w_glu.shape, BF16),
                        pltpu.VMEM(w_out.shape, BF16)],
        compiler_params=_params(),
    )(s, u, z, h, d_skip, b_glu, cdre, cdim, w_glu, w_out)


def s5_bwd3a(dh, y, q, z, w_glu, w_out, b_glu, name):
    lp, d = dh.shape
    tt = TOKEN_TILE
    nt = lp // tt
    cw = w_glu.shape[1]

    def body(dh_ref, y_ref, q_ref, z_ref, bg_ref, wg_hbm, wo_hbm, dy_ref, dp_ref, dwo_hbm, dwg_hbm, dbg_hbm,
             wg, wo, dwo, dwg, dbg):
        i = pl.program_id(0)

        @pl.when(i == 0)
        def _():
            pltpu.sync_copy(wg_hbm, wg)
            pltpu.sync_copy(wo_hbm, wo)
            dwo[...] = jnp.zeros_like(dwo)
            dwg[...] = jnp.zeros_like(dwg)
            dbg[...] = jnp.zeros_like(dbg)

        sig = _sigmoid(q_ref[...] + bg_ref[...])
        sz, dsz = _silu_and_grad(z_ref[...])
        dhv = dh_ref[...]
        yv = y_ref[...]
        gy, t = _gelu(yv)
        dq_parts, dgy_parts = [], []
        for k in range(4):
            cols = slice(k * cw, (k + 1) * cw)
            gy_k, sig_k, sz_k = gy[:, cols], sig[:, cols], sz[:, cols]
            y2 = gy_k * sig_k
            dy3 = _dot_nt(dhv, wo[k])
            dwo[k] += _dot_tn(y2 * sz_k, dhv)
            dy2 = dy3 * sz_k
            dp_ref[0, :, cols] = (dy3 * y2 * dsz[:, cols]).astype(BF16)
            dq_parts.append(dy2 * gy_k * sig_k * (1.0 - sig_k))
            dgy_parts.append(dy2 * sig_k)
        dq = jnp.concatenate(dq_parts, axis=1)
        dbg[...] += jnp.sum(dq, axis=0, keepdims=True)
        dgelu = _gelu_grad(yv, t)
        for k in range(4):
            cols = slice(k * cw, (k + 1) * cw)
            dwg[k] += _dot_tn(gy[:, cols], dq)
            dy_ref[:, cols] = (dgy_parts[k] + _dot_nt(dq, wg[k])) * dgelu[:, cols]

        @pl.when(i == nt - 1)
        def _():
            pltpu.sync_copy(dwo, dwo_hbm)
            pltpu.sync_copy(dwg, dwg_hbm)
            pltpu.sync_copy(dbg, dbg_hbm)

    row = pl.BlockSpec((tt, d), lambda i: (i, 0))
    sd = jax.ShapeDtypeStruct
    return pl.pallas_call(
        body, name=name, grid=(nt,),
        in_specs=[row, row, row, row, pl.BlockSpec((1, d), lambda i: (0, 0)), ANY, ANY],
        out_specs=[row, pl.BlockSpec((1, tt, d), lambda i: (1, i, 0)), ANY, ANY, ANY],
        out_shape=(sd((lp, d), F32), sd((2, lp, d), BF16), sd(w_out.shape, F32), sd(w_glu.shape, F32), sd((1, d), F32)),
        scratch_shapes=[pltpu.VMEM(w_glu.shape, BF16), pltpu.VMEM(w_out.shape, BF16),
                        pltpu.VMEM(w_out.shape, F32), pltpu.VMEM(w_glu.shape, F32), pltpu.VMEM((1, d), F32)],
        compiler_params=_params(),
    )(dh, y, q, z, b_glu, w_glu, w_out)


def s5_bwd3b(dy, s, u, cdre, cdim, d_skip, name):
    lp, d = dy.shape
    tt = TOKEN_TILE
    nt = lp // tt
    sw, cw = cdre.shape[1], cdre.shape[2]
    gc = cw // S5_GROUP

    def body(dy_ref, s_ref, u_ref, d_ref, cre_hbm, cim_hbm,
             ds_ref, dus_ref, dcre_ref, dcim_ref, dd_hbm, cre, cim, dcre, dcim, dd):
        i = pl.program_id(0)

        @pl.when(i == 0)
        def _():
            pltpu.sync_copy(cre_hbm, cre)
            pltpu.sync_copy(cim_hbm, cim)
            dcre[...] = jnp.zeros_like(dcre)
            dcim[...] = jnp.zeros_like(dcim)
            dd[...] = jnp.zeros_like(dd)

        for c in range(4):
            chunk = slice(c * cw, (c + 1) * cw)
            dyv = dy_ref[:, chunk]
            dd[c] += jnp.sum(dyv * u_ref[:, chunk], axis=0, keepdims=True)
            dus_ref[:, chunk] = dyv * d_ref[c]
            _slab_store(ds_ref, c, _dot_nt(dyv, cre[c]), _dot_nt(dyv, cim[c]))
            s_r, s_i = _slab_load(s_ref, c)
            dcre[c] += _dot_tn(s_r, dyv)
            dcim[c] += _dot_tn(s_i, dyv)

        @pl.when(i == nt - 1)
        def _():
            for k in range(4):
                for j in range(gc):
                    rows, cols = pl.ds(j * S5_STATE, S5_STATE), pl.ds(j * S5_GROUP, S5_GROUP)
                    dcre_ref[k, j] = dcre[k, rows, cols].T
                    dcim_ref[k, j] = dcim[k, rows, cols].T
            pltpu.sync_copy(dd, dd_hbm)

    sd = jax.ShapeDtypeStruct
    row = pl.BlockSpec((tt, d), lambda i: (i, 0))
    slab, slab_shape = _slab_spec(lp, tt, sw)
    diag = pl.BlockSpec((4, gc, S5_GROUP, S5_STATE), lambda i: (0, 0, 0, 0))
    return pl.pallas_call(
        body, name=name, grid=(nt,),
        in_specs=[row, slab, row, pl.BlockSpec((4, 1, cw), lambda i: (0, 0, 0)), ANY, ANY],
        out_specs=[slab, row, diag, diag, ANY],
        out_shape=(sd(slab_shape, F32), sd((lp, d), F32),
                   sd((4, gc, S5_GROUP, S5_STATE), F32), sd((4, gc, S5_GROUP, S5_STATE), F32), sd((4, 1, cw), F32)),
        scratch_shapes=[pltpu.VMEM(cdre.shape, BF16), pltpu.VMEM(cdim.shape, BF16),
                        pltpu.VMEM(cdre.shape, F32), pltpu.VMEM(cdim.shape, F32), pltpu.VMEM((4, 1, cw), F32)],
        compiler_params=_params(),
    )(dy, s, u, d_skip, cdre, cdim)


def s5_scan_bwd(g, s, ar, ai, name):
    nj = ar.shape[1]
    tt = TOKEN_TILE
    cpb = SCAN_CHUNKS
    nt = g.shape[0] // (4 * tt * nj)

    def body(g_ref, s_ref, ar_ref, ai_ref, lam_ref, dar_ref, dai_ref, st_r, st_i, acc_r, acc_i):
        i, cg = pl.program_id(0), pl.program_id(1)

        @pl.when((i == 0) & (cg == 0))
        def _():
            for ref in (st_r, st_i, acc_r, acc_i):
                ref[...] = jnp.zeros_like(ref)

        a_r = [ar_ref[cg * cpb + q] for q in range(cpb)]
        a_i = [ai_ref[cg * cpb + q] for q in range(cpb)]

        def slab(q, t):
            return pl.ds(pl.multiple_of((q * tt + t) * nj, nj), nj)

        def adjoint(q, t, l_r, l_i):
            rows = slab(q, t)
            g_r, g_i = _unpack_pair(g_ref[rows, :])
            n_r = g_r + a_r[q] * l_r + a_i[q] * l_i
            n_i = g_i + a_r[q] * l_i - a_i[q] * l_r
            lam_ref[rows, :] = _pack_pair(n_r, n_i)
            return n_r, n_i

        def pair(q, t, l_r, l_i, d_r, d_i):
            p_r, p_i = _unpack_pair(s_ref[slab(q, t), :])
            return d_r + l_r * p_r + l_i * p_i, d_i + l_i * p_r - l_r * p_i

        def step(k, carry):
            t = tt - 1 - k
            out = []
            for q in range(cpb):
                l_r, l_i, d_r, d_i = carry[q]
                l_r, l_i = adjoint(q, t, l_r, l_i)
                d_r, d_i = pair(q, t - 1, l_r, l_i, d_r, d_i)
                out.append((l_r, l_i, d_r, d_i))
            return tuple(out)

        init = []
        for q in range(cpb):
            ch = cg * cpb + q
            l_r, l_i = st_r[ch], st_i[ch]
            d_r, d_i = pair(q, tt - 1, l_r, l_i, acc_r[ch], acc_i[ch])
            init.append((l_r, l_i, d_r, d_i))
        final = lax.fori_loop(0, tt - 1, step, tuple(init), unroll=8)
        for q in range(cpb):
            ch = cg * cpb + q
            l_r, l_i, d_r, d_i = final[q]
            l_r, l_i = adjoint(q, 0, l_r, l_i)
            st_r[ch] = l_r
            st_i[ch] = l_i
            acc_r[ch] = d_r
            acc_i[ch] = d_i
            dar_ref[ch] = d_r
            dai_ref[ch] = d_i

    blk = pl.BlockSpec((cpb * tt * nj, 128), lambda i, cg: ((nt - 1 - i) * (4 // cpb) + cg, 0))
    par = pl.BlockSpec((4, nj, 128), lambda i, cg: (0, 0, 0))
    sd = jax.ShapeDtypeStruct
    return pl.pallas_call(
        body, name=name, grid=(nt, 4 // cpb),
        in_specs=[blk, blk, par, par], out_specs=[blk, par, par],
        out_shape=(sd(g.shape, F32), sd((4, nj, 128), F32), sd((4, nj, 128), F32)),
        scratch_shapes=[pltpu.VMEM((4, nj, 128), F32)] * 4,
        compiler_params=_params(ndim=2),
    )(g, s, ar, ai)


def s5_bwd1(lam, dus, u, dp, h, dh, g, w_in, bdre, bdim, name):
    lp, d = h.shape
    tt = TOKEN_TILE
    nt = lp // tt
    cw, sw = bdre.shape[1], bdre.shape[2]
    gc = cw // S5_GROUP

    def body(lam_ref, dus_ref, u_ref, dpz_ref, h_ref, dh_ref, g_ref, w_hbm, bre_hbm, bim_hbm,
             dpu_ref, dho_ref, n_ref, dbre_ref, dbim_ref, dg_hbm, w, bre, bim, dbre, dbim, dg):
        i = pl.program_id(0)

        @pl.when(i == 0)
        def _():
            pltpu.sync_copy(w_hbm, w)
            pltpu.sync_copy(bre_hbm, bre)
            pltpu.sync_copy(bim_hbm, bim)
            dbre[...] = jnp.zeros_like(dbre)
            dbim[...] = jnp.zeros_like(dbim)
            dg[...] = jnp.zeros_like(dg)

        dz = dpz_ref[0]
        dn = None
        for c in range(4):
            chunk = slice(c * cw, (c + 1) * cw)
            (l_r, l_i), uv = _slab_load(lam_ref, c), u_ref[:, chunk]
            du = dus_ref[:, chunk] + _dot_nt(l_r, bre[c]) + _dot_nt(l_i, bim[c])
            dbre[c] += _dot_tn(uv, l_r)
            dbim[c] += _dot_tn(uv, l_i)
            dpu_ref[0, :, chunk] = du.astype(BF16)
            part = _dot_nt(du, w[c]) + _dot_nt(dz[:, chunk], w[4 + c])
            dn = part if c == 0 else dn + part
        gv = g_ref[...]
        n, hh, rr = _rms_fwd(h_ref[...], gv)
        n_ref[...] = n.T.astype(BF16)
        dg[...] += jnp.sum(dn * hh, axis=0, keepdims=True)
        dho_ref[...] = dh_ref[...] + _rms_bwd(dn, hh, rr, gv)

        @pl.when(i == nt - 1)
        def _():
            for k in range(4):
                for j in range(gc):
                    rows, cols = pl.ds(j * S5_GROUP, S5_GROUP), pl.ds(j * S5_STATE, S5_STATE)
                    dbre_ref[k, j] = dbre[k, rows, cols]
                    dbim_ref[k, j] = dbim[k, rows, cols]
            pltpu.sync_copy(dg, dg_hbm)

    sd = jax.ShapeDtypeStruct
    row = pl.BlockSpec((tt, d), lambda i: (i, 0))
    slab, _ = _slab_spec(lp, tt, sw)
    diag = pl.BlockSpec((4, gc, S5_GROUP, S5_STATE), lambda i: (0, 0, 0, 0))
    return pl.pallas_call(
        body, name=name, grid=(nt,),
        in_specs=[slab, row, row, pl.BlockSpec((1, tt, d), lambda i: (1, i, 0)), row, row, pl.BlockSpec((1, d), lambda i: (0, 0)),
                  ANY, ANY, ANY],
        out_specs=[pl.BlockSpec((1, tt, d), lambda i: (0, i, 0)), row, pl.BlockSpec((d, tt), lambda i: (0, i)), diag, diag, ANY],
        out_shape=(sd(dp.shape, BF16), sd((lp, d), F32), sd((d, lp), BF16),
                   sd((4, gc, S5_GROUP, S5_STATE), F32), sd((4, gc, S5_GROUP, S5_STATE), F32), sd((1, d), F32)),
        input_output_aliases={3: 0},
        scratch_shapes=[pltpu.VMEM(w_in.shape, BF16), pltpu.VMEM(bdre.shape, BF16), pltpu.VMEM(bdim.shape, BF16),
                        pltpu.VMEM(bdre.shape, F32), pltpu.VMEM(bdim.shape, F32), pltpu.VMEM((1, d), F32)],
        compiler_params=_params(),
    )(lam, dus, u, dp, h, dh, g, w_in, bdre, bdim)


def grad_w_in(n_t, dp, blk, name):
    d, lp = n_t.shape
    npart, _, width = dp.shape
    per = width // blk

    def body(n_ref, dp_ref, o_ref):
        o_ref[0] = jnp.dot(n_ref[...], dp_ref[0], preferred_element_type=F32).astype(o_ref.dtype)

    return pl.pallas_call(
        body, name=name, grid=(npart * per,),
        in_specs=[pl.BlockSpec((d, lp), lambda j: (0, 0), pipeline_mode=pl.Buffered(1)),
                  pl.BlockSpec((1, lp, blk), lambda j: (j // per, 0, j % per))],
        out_specs=pl.BlockSpec((1, d, blk), lambda j: (j, 0, 0)),
        out_shape=jax.ShapeDtypeStruct((npart * per, d, blk), BF16),
        compiler_params=_params(),
    )(n_t, dp)


def _conv_mix(cg, v, cw_ref, cb_ref, halo, c):
    hc = cg * v
    taps = cw_ref[c]
    conv = taps[2:3, :] * hc + taps[1:2, :] * _shift_down(hc, 1, halo) + taps[0:1, :] * _shift_down(hc, 2, halo) + cb_ref[c]
    return hc, conv


def conv_fwd(h, g, w_in, conv_w, conv_b, w_out, name):
    lp, d = h.shape
    tt = TOKEN_TILE
    nt = lp // tt
    nch, ce = w_out.shape[0], w_out.shape[1]

    def body(h_ref, g_ref, cw_ref, cb_ref, w_hbm, wo_hbm, o_ref, halo_ref, acts_ref, w, wo, halo):
        i = pl.program_id(0)

        @pl.when(i == 0)
        def _():
            pltpu.sync_copy(w_hbm, w)
            pltpu.sync_copy(wo_hbm, wo)
            halo[...] = jnp.zeros_like(halo)

        hv = h_ref[...]
        n = _rms_fwd(hv, g_ref[...])[0].astype(BF16)
        o = hv
        for c in range(nch):
            cols = slice(c * ce, (c + 1) * ce)
            bg, cg, v, z = [jnp.dot(n, w[p * nch + c], preferred_element_type=F32) for p in range(4)]
            for p, val in enumerate((bg, cg, v, z)):
                acts_ref[p, :, cols] = val.astype(BF16)
            hc, conv = _conv_mix(cg, v, cw_ref, cb_ref, halo[c], c)
            o = o + _dot(bg * conv * (z * _sigmoid(z)), wo[c])
            halo[c] = hc[tt - CONV_HALO:, :]
            halo_ref[0, c] = hc[tt - CONV_HALO:, :]
        o_ref[...] = o

    sd = jax.ShapeDtypeStruct
    return pl.pallas_call(
        body, name=name, grid=(nt,),
        in_specs=[pl.BlockSpec((tt, d), lambda i: (i, 0)), pl.BlockSpec((1, d), lambda i: (0, 0)),
                  pl.BlockSpec(conv_w.shape, lambda i: (0, 0, 0)), pl.BlockSpec(conv_b.shape, lambda i: (0, 0, 0)), ANY, ANY],
        out_specs=[pl.BlockSpec((tt, d), lambda i: (i, 0)), pl.BlockSpec((1, nch, CONV_HALO, ce), lambda i: (i, 0, 0, 0)),
                   pl.BlockSpec((4, tt, nch * ce), lambda i: (0, i, 0))],
        out_shape=(sd((lp, d), F32), sd((nt, nch, CONV_HALO, ce), F32), sd((4, lp, nch * ce), BF16)),
        scratch_shapes=[pltpu.VMEM(w_in.shape, BF16), pltpu.VMEM(w_out.shape, BF16), pltpu.VMEM((nch, CONV_HALO, ce), F32)],
        compiler_params=_params(),
    )(h, g, conv_w, conv_b, w_in, w_out)


def conv_bwd(h, dh, halos, acts, g, w_in, conv_w, conv_b, w_out, name):
    lp, d = h.shape
    tt = TOKEN_TILE
    nt = lp // tt
    nch, ce = w_out.shape[0], w_out.shape[1]

    def body(h_ref, dh_ref, halo_ref, acts_ref, g_ref, cw_ref, cb_ref, w_hbm, wo_hbm,
             dho_ref, n_ref, dp_ref, dwo_hbm, dcw_hbm, dcb_hbm, dg_hbm, w, wo, nxt, dwo, dcw, dcb, dg):
        i = pl.program_id(0)

        @pl.when(i == 0)
        def _():
            pltpu.sync_copy(w_hbm, w)
            pltpu.sync_copy(wo_hbm, wo)
            for ref in (nxt, dwo, dcw, dcb, dg):
                ref[...] = jnp.zeros_like(ref)

        gv = g_ref[...]
        nf, hh, rr = _rms_fwd(h_ref[...], gv)
        n_ref[...] = nf.T.astype(BF16)
        dhv = dh_ref[...]
        has_prev = (i < nt - 1).astype(F32)
        dn = jnp.zeros((tt, d), F32)
        for c in range(nch):
            halo = halo_ref[0, c] * has_prev
            cols = slice(c * ce, (c + 1) * ce)
            bg, cg, v, z = [acts_ref[p, :, cols].astype(F32) for p in range(4)]
            hc, conv = _conv_mix(cg, v, cw_ref, cb_ref, halo, c)
            sz, dsz = _silu_and_grad(z)
            y1 = bg * conv
            dy2 = _dot_nt(dhv, wo[c])
            dwo[c] += _dot_tn(y1 * sz, dhv)
            dy1 = dy2 * sz
            dz = dy2 * y1 * dsz
            dbg = dy1 * conv
            dconv = dy1 * bg
            dcb[c] += jnp.sum(dconv, axis=0, keepdims=True)
            up1 = _shift_up(dconv, 1, nxt[c])
            up2 = _shift_up(dconv, 2, nxt[c])
            nxt[c] = dconv[:CONV_HALO, :]
            taps = cw_ref[c]
            dhc = taps[2:3, :] * dconv + taps[1:2, :] * up1 + taps[0:1, :] * up2
            dcw[c, 0:1, :] += jnp.sum(hc * up2, axis=0, keepdims=True)
            dcw[c, 1:2, :] += jnp.sum(hc * up1, axis=0, keepdims=True)
            dcw[c, 2:3, :] += jnp.sum(hc * dconv, axis=0, keepdims=True)
            dcg = dhc * v
            dv = dhc * cg
            for p, val in enumerate((dbg, dcg, dv, dz)):
                dp_ref[p, :, cols] = val.astype(BF16)
                dn = dn + _dot_nt(val, w[p * nch + c])
        dg[...] += jnp.sum(dn * hh, axis=0, keepdims=True)
        dho_ref[...] = dhv + _rms_bwd(dn, hh, rr, gv)

        @pl.when(i == nt - 1)
        def _():
            pltpu.sync_copy(dwo, dwo_hbm)
            pltpu.sync_copy(dcw, dcw_hbm)
            pltpu.sync_copy(dcb, dcb_hbm)
            pltpu.sync_copy(dg, dg_hbm)

    rev = lambda i: (nt - 1 - i, 0)
    sd = jax.ShapeDtypeStruct
    return pl.pallas_call(
        body, name=name, grid=(nt,),
        in_specs=[pl.BlockSpec((tt, d), rev), pl.BlockSpec((tt, d), rev),
                  pl.BlockSpec((1, nch, CONV_HALO, ce), lambda i: (jnp.maximum(nt - 2 - i, 0), 0, 0, 0)),
                  pl.BlockSpec((4, tt, nch * ce), lambda i: (0, nt - 1 - i, 0), pipeline_mode=pl.Buffered(1)),
                  pl.BlockSpec((1, d), lambda i: (0, 0)),
                  pl.BlockSpec(conv_w.shape, lambda i: (0, 0, 0)), pl.BlockSpec(conv_b.shape, lambda i: (0, 0, 0)), ANY, ANY],
        out_specs=[pl.BlockSpec((tt, d), rev), pl.BlockSpec((d, tt), lambda i: (0, nt - 1 - i)),
                   pl.BlockSpec((4, tt, nch * ce), lambda i: (0, nt - 1 - i, 0)), ANY, ANY, ANY, ANY],
        out_shape=(sd((lp, d), F32), sd((d, lp), BF16), sd((4, lp, nch * ce), BF16),
                   sd(w_out.shape, F32), sd((nch, 8, ce), F32), sd((nch, 1, ce), F32), sd((1, d), F32)),
        scratch_shapes=[pltpu.VMEM(w_in.shape, BF16), pltpu.VMEM(w_out.shape, BF16), pltpu.VMEM((nch, CONV_HALO, ce), F32),
                        pltpu.VMEM(w_out.shape, F32), pltpu.VMEM((nch, 8, ce), F32), pltpu.VMEM((nch, 1, ce), F32),
                        pltpu.VMEM((1, d), F32)],
        compiler_params=_params(),
    )(h, dh, halos, acts, g, conv_w, conv_b, w_in, w_out)


def _pool_fwd_group(n, w, wg, bg_ref, sc_ref, halo, k, tile, tt, first_pos):
    u = jnp.dot(n, w[k], preferred_element_type=F32)
    z = jnp.dot(n, w[4 + k], preferred_element_type=F32)
    ext = jnp.concatenate([halo, u], axis=0)
    win = _window_sums_back(ext)[k][POOL_HALO:, :]
    mixed = win * _pool_inv_count(tile, tt, first_pos, POOL_WINDOWS[k], u.shape[1]) - u
    outs = _dot(mixed, wg[k]) + bg_ref[k]
    return u, z, mixed, outs, outs * sc_ref[k]


def pool_fwd(h, g, w_in, w_grp, b_grp, scale, w_out, first_pos, name):
    lp, d = h.shape
    tt = TOKEN_TILE
    nt = lp // tt
    gw = w_grp.shape[1]

    def body(h_ref, g_ref, bg_ref, sc_ref, w_hbm, wg_hbm, wo_hbm, o_ref, halo_ref, w, wg, wo, halo):
        i = pl.program_id(0)

        @pl.when(i == 0)
        def _():
            pltpu.sync_copy(w_hbm, w)
            pltpu.sync_copy(wg_hbm, wg)
            pltpu.sync_copy(wo_hbm, wo)
            halo[...] = jnp.zeros_like(halo)

        hv = h_ref[...]
        n = _rms_fwd(hv, g_ref[...])[0].astype(BF16)
        o = hv
        for k in range(4):
            u, z, _, _, yp = _pool_fwd_group(n, w, wg, bg_ref, sc_ref, halo[k], k, i, tt, first_pos)
            o = o + _dot(yp * (z * _sigmoid(z)), wo[k])
            halo[k] = u[tt - POOL_HALO:, :]
            halo_ref[0, k] = u[tt - POOL_HALO:, :]
        o_ref[...] = o

    sd = jax.ShapeDtypeStruct
    small = pl.BlockSpec((4, 1, gw), lambda i: (0, 0, 0))
    return pl.pallas_call(
        body, name=name, grid=(nt,),
        in_specs=[pl.BlockSpec((tt, d), lambda i: (i, 0)), pl.BlockSpec((1, d), lambda i: (0, 0)), small, small, ANY, ANY, ANY],
        out_specs=[pl.BlockSpec((tt, d), lambda i: (i, 0)), pl.BlockSpec((1, 4, POOL_HALO, gw), lambda i: (i, 0, 0, 0))],
        out_shape=(sd((lp, d), F32), sd((nt, 4, POOL_HALO, gw), F32)),
        scratch_shapes=[pltpu.VMEM(w_in.shape, BF16), pltpu.VMEM(w_grp.shape, BF16), pltpu.VMEM(w_out.shape, BF16),
                        pltpu.VMEM((4, POOL_HALO, gw), F32)],
        compiler_params=_params(),
    )(h, g, b_grp, scale, w_in, w_grp, w_out)


def pool_bwd(h, dh, halos, g, w_in, w_grp, b_grp, scale, w_out, first_pos, name):
    lp, d = h.shape
    tt = TOKEN_TILE
    nt = lp // tt
    gw = w_grp.shape[1]

    def body(h_ref, dh_ref, halo_ref, g_ref, bg_ref, sc_ref, w_hbm, wg_hbm, wo_hbm,
             dho_ref, n_ref, dp_ref, dwo_hbm, dwg_hbm, dbg_hbm, dsc_hbm, dg_hbm,
             w, wg, wo, nxt, dwo, dwg, dbg, dsc, dg):
        i = pl.program_id(0)
        tile = nt - 1 - i

        @pl.when(i == 0)
        def _():
            pltpu.sync_copy(w_hbm, w)
            pltpu.sync_copy(wg_hbm, wg)
            pltpu.sync_copy(wo_hbm, wo)
            for ref in (nxt, dwo, dwg, dbg, dsc, dg):
                ref[...] = jnp.zeros_like(ref)

        gv = g_ref[...]
        nf, hh, rr = _rms_fwd(h_ref[...], gv)
        n = nf.astype(BF16)
        n_ref[...] = nf.T.astype(BF16)
        dhv = dh_ref[...]
        has_prev = (i < nt - 1).astype(F32)
        dn = jnp.zeros((tt, d), F32)
        for k in range(4):
            u, z, mixed, outs, yp = _pool_fwd_group(n, w, wg, bg_ref, sc_ref, halo_ref[0, k] * has_prev, k, tile, tt, first_pos)
            sz, dsz = _silu_and_grad(z)
            dy = _dot_nt(dhv, wo[k])
            dwo[k] += _dot_tn(yp * sz, dhv)
            dyp = dy * sz
            dz = dy * yp * dsz
            dsc[k] += jnp.sum(dyp * outs, axis=0, keepdims=True)
            douts = dyp * sc_ref[k]
            dbg[k] += jnp.sum(douts, axis=0, keepdims=True)
            dwg[k] += _dot_tn(mixed, douts)
            dmixed = _dot_nt(douts, wg[k])
            dm = dmixed * _pool_inv_count(tile, tt, first_pos, POOL_WINDOWS[k], gw)
            ext = jnp.concatenate([dm, nxt[k]], axis=0)
            du = _window_sums_fwd(ext)[k][:tt, :] - dmixed
            nxt[k] = dm[:POOL_HALO, :]
            cols = slice(k * gw, (k + 1) * gw)
            dp_ref[0, :, cols] = du.astype(BF16)
            dp_ref[1, :, cols] = dz.astype(BF16)
            dn = dn + _dot_nt(du, w[k]) + _dot_nt(dz, w[4 + k])
        dg[...] += jnp.sum(dn * hh, axis=0, keepdims=True)
        dho_ref[...] = dhv + _rms_bwd(dn, hh, rr, gv)

        @pl.when(i == nt - 1)
        def _():
            pltpu.sync_copy(dwo, dwo_hbm)
            pltpu.sync_copy(dwg, dwg_hbm)
            pltpu.sync_copy(dbg, dbg_hbm)
            pltpu.sync_copy(dsc, dsc_hbm)
            pltpu.sync_copy(dg, dg_hbm)

    rev = lambda i: (nt - 1 - i, 0)
    sd = jax.ShapeDtypeStruct
    small = pl.BlockSpec((4, 1, gw), lambda i: (0, 0, 0))
    return pl.pallas_call(
        body, name=name, grid=(nt,),
        in_specs=[pl.BlockSpec((tt, d), rev), pl.BlockSpec((tt, d), rev),
                  pl.BlockSpec((1, 4, POOL_HALO, gw), lambda i: (jnp.maximum(nt - 2 - i, 0), 0, 0, 0)),
                  pl.BlockSpec((1, d), lambda i: (0, 0)), small, small, ANY, ANY, ANY],
        out_specs=[pl.BlockSpec((tt, d), rev), pl.BlockSpec((d, tt), lambda i: (0, nt - 1 - i)),
                   pl.BlockSpec((2, tt, 4 * gw), lambda i: (0, nt - 1 - i, 0)), ANY, ANY, ANY, ANY, ANY],
        out_shape=(sd((lp, d), F32), sd((d, lp), BF16), sd((2, lp, 4 * gw), BF16),
                   sd(w_out.shape, F32), sd(w_grp.shape, F32), sd((4, 1, gw), F32), sd((4, 1, gw), F32), sd((1, d), F32)),
        scratch_shapes=[pltpu.VMEM(w_in.shape, BF16), pltpu.VMEM(w_grp.shape, BF16), pltpu.VMEM(w_out.shape, BF16),
                        pltpu.VMEM((4, POOL_HALO, gw), F32), pltpu.VMEM(w_out.shape, F32), pltpu.VMEM(w_grp.shape, F32),
                        pltpu.VMEM((4, 1, gw), F32), pltpu.VMEM((4, 1, gw), F32), pltpu.VMEM((1, d), F32)],
        compiler_params=_params(),
    )(h, dh, halos, g, b_grp, scale, w_in, w_grp, w_out)


def loss_head(h, target, g, pad_tiles, name):
    lp, d = h.shape
    tt = TOKEN_TILE
    nt = lp // tt

    def body(h_ref, t_ref, g_ref, dh_ref, dg_ref, loss_ref, acc):
        i = pl.program_id(0)

        @pl.when(i == 0)
        def _():
            acc[...] = jnp.zeros_like(acc)
            dg_ref[...] = jnp.zeros_like(dg_ref)

        @pl.when(i < pad_tiles)
        def _():
            dh_ref[...] = jnp.zeros_like(dh_ref)

        @pl.when(i >= pad_tiles)
        def _():
            gv = g_ref[...]
            n, hh, rr = _rms_fwd(h_ref[...], gv)
            err = n - t_ref[...]
            acc[...] += 0.5 * jnp.sum(jnp.mean(err * err, axis=-1, keepdims=True), axis=0, keepdims=True)
            dn = err * (1.0 / d)
            dg_ref[...] += jnp.sum(dn * hh, axis=0, keepdims=True)
            dh_ref[...] = _rms_bwd(dn, hh, rr, gv)

        loss_ref[...] = jnp.broadcast_to(acc[...], loss_ref.shape)

    sd = jax.ShapeDtypeStruct
    return pl.pallas_call(
        body, name=name, grid=(nt,),
        in_specs=[pl.BlockSpec((tt, d), lambda i: (i, 0)), pl.BlockSpec((tt, d), lambda i: (jnp.maximum(i - pad_tiles, 0), 0)),
                  pl.BlockSpec((1, d), lambda i: (0, 0))],
        out_specs=[pl.BlockSpec((tt, d), lambda i: (i, 0)), pl.BlockSpec((1, d), lambda i: (0, 0)),
                   pl.BlockSpec((8, 128), lambda i: (0, 0))],
        out_shape=(sd((lp, d), F32), sd((1, d), F32), sd((8, 128), F32)),
        scratch_shapes=[pltpu.VMEM((1, 1), F32)],
        compiler_params=_params(),
    )(h, target, g)


def exchange(arrs, gather, name):
    n = len(arrs)

    def body(*refs):
        ins, outs = refs[:n], refs[n:2 * n]
        send_sems, recv_sems, own_sems = refs[2 * n:]
        x, y, c = lax.axis_index("x"), lax.axis_index("y"), lax.axis_index("c")
        me = 4 * x + 2 * y + c
        own = []
        for a in range(n):
            cp = pltpu.make_async_copy(ins[a] if gather else ins[a].at[me], outs[a].at[me], own_sems.at[a])
            cp.start()
            own.append(cp)
        sent = []
        for k in range(1, N_DEV):
            px = 1 - x if k & 4 else x
            py = 1 - y if k & 2 else y
            pc = 1 - c if k & 1 else c
            peer = 4 * px + 2 * py + pc
            for a in range(n):
                cp = pltpu.make_async_remote_copy(
                    src_ref=ins[a] if gather else ins[a].at[peer], dst_ref=outs[a].at[me],
                    send_sem=send_sems.at[a, k - 1], recv_sem=recv_sems.at[a, k - 1],
                    device_id=(px, py, pc), device_id_type=pl.DeviceIdType.MESH)
                cp.start()
                sent.append((cp, a, k, peer, (px, py, pc)))
        for cp, a, k, peer, pid in sent:
            cp.wait_send()
            pltpu.make_async_remote_copy(
                src_ref=ins[a] if gather else ins[a].at[peer], dst_ref=outs[a].at[peer],
                send_sem=send_sems.at[a, k - 1], recv_sem=recv_sems.at[a, k - 1],
                device_id=pid, device_id_type=pl.DeviceIdType.MESH).wait_recv()
        for cp in own:
            cp.wait()

    hbm = pl.BlockSpec(memory_space=pltpu.HBM)
    out_shape = tuple(jax.ShapeDtypeStruct(((N_DEV,) + a.shape) if gather else a.shape, a.dtype) for a in arrs)
    return pl.pallas_call(
        body, name=name, in_specs=[hbm] * n, out_specs=[hbm] * n, out_shape=out_shape,
        scratch_shapes=[pltpu.SemaphoreType.DMA((n, N_DEV - 1)), pltpu.SemaphoreType.DMA((n, N_DEV - 1)),
                        pltpu.SemaphoreType.DMA((n,))],
    )(*[pltpu.with_memory_space_constraint(a, pltpu.HBM) for a in arrs])


def _peers(x, y, c):
    out = []
    for k in range(1, N_DEV):
        px = 1 - x if k & 4 else x
        py = 1 - y if k & 2 else y
        pc = 1 - c if k & 1 else c
        out.append((k, (px, py, pc), 4 * px + 2 * py + pc))
    return out


def exchange_start(arrs, gather, after, name):
    n = len(arrs)
    me = 4 * lax.axis_index("x") + 2 * lax.axis_index("y") + lax.axis_index("c")
    lands = []
    for a in arrs:
        own = a[None] if gather else lax.dynamic_index_in_dim(a, me, 0, keepdims=True)
        lands.append(lax.dynamic_update_index_in_dim(lax.empty(((N_DEV,) + a.shape) if gather else a.shape, a.dtype), own, me, 0))

    def body(*refs):
        ins, land = refs[:n], refs[n:2 * n]
        send_sems, recv_sems, token = refs[2 * n + 1], refs[2 * n + 2], refs[4 * n + 3]
        x, y, c = lax.axis_index("x"), lax.axis_index("y"), lax.axis_index("c")
        me = 4 * x + 2 * y + c
        for k, pid, peer in _peers(x, y, c):
            for a in range(n):
                pltpu.make_async_remote_copy(
                    src_ref=ins[a] if gather else ins[a].at[peer], dst_ref=land[a].at[me],
                    send_sem=send_sems.at[a * (N_DEV - 1) + k - 1], recv_sem=recv_sems.at[a * (N_DEV - 1) + k - 1],
                    device_id=pid, device_id_type=pl.DeviceIdType.MESH).start()
        token[...] = jnp.zeros_like(token)

    hbm = pl.BlockSpec(memory_space=pltpu.HBM)
    sem = pl.BlockSpec(memory_space=pltpu.SEMAPHORE)
    sems = pltpu.SemaphoreType.DMA((n * (N_DEV - 1),))
    res = pl.pallas_call(
        body, name=name, in_specs=[hbm] * (2 * n) + [ANY],
        out_specs=[sem, sem] + [hbm] * (2 * n) + [pl.BlockSpec(memory_space=pltpu.VMEM)],
        out_shape=[sems, sems] + [pltpu.HBM(a.shape, a.dtype) for a in arrs] + [pltpu.HBM(l.shape, l.dtype) for l in lands]
        + [jax.ShapeDtypeStruct((8, 128), F32)],
        input_output_aliases={a: 2 + a for a in range(2 * n)},
        compiler_params=pltpu.CompilerParams(has_side_effects=pltpu.SideEffectType.DATAFLOW_SIDE_EFFECTING),
    )(*[pltpu.with_memory_space_constraint(a, pltpu.HBM) for a in list(arrs) + lands], after)
    return res[0], res[1], res[2:2 + n], res[2 + n:2 + 2 * n], res[-1]


def exchange_wait(started, gather, after, name):
    send_sems, recv_sems, srcs, lands, _ = started
    n = len(srcs)
    after = list(after) if isinstance(after, (list, tuple)) else [after]

    def body(*refs):
        ins, land = refs[:n], refs[n:2 * n]
        send_sems, recv_sems = refs[2 * n], refs[2 * n + 1]
        x, y, c = lax.axis_index("x"), lax.axis_index("y"), lax.axis_index("c")
        for k, pid, peer in _peers(x, y, c):
            for a in range(n):
                cp = pltpu.make_async_remote_copy(
                    src_ref=ins[a] if gather else ins[a].at[peer], dst_ref=land[a].at[peer],
                    send_sem=send_sems.at[a * (N_DEV - 1) + k - 1], recv_sem=recv_sems.at[a * (N_DEV - 1) + k - 1],
                    device_id=pid, device_id_type=pl.DeviceIdType.MESH)
                cp.wait_send()
                cp.wait_recv()

    hbm = pl.BlockSpec(memory_space=pltpu.HBM)
    sem = pl.BlockSpec(memory_space=pltpu.SEMAPHORE)
    res = pl.pallas_call(
        body, name=name, in_specs=[hbm] * (2 * n) + [sem, sem] + [ANY] * len(after),
        out_specs=[hbm] * (2 * n),
        out_shape=[pltpu.HBM(a.shape, a.dtype) for a in list(srcs) + list(lands)],
        input_output_aliases={a: a for a in range(2 * n)},
        compiler_params=pltpu.CompilerParams(has_side_effects=pltpu.SideEffectType.DATAFLOW_SIDE_EFFECTING),
    )(*srcs, *lands, send_sems, recv_sems, *after)
    return res[n:]


def _adamw(w, g, m, v):
    m = ADAM_B1 * m + (1.0 - ADAM_B1) * g
    v = ADAM_B2 * v + (1.0 - ADAM_B2) * (g * g)
    m_hat = m / (1.0 - ADAM_B1 ** ADAM_STEP)
    v_hat = v / (1.0 - ADAM_B2 ** ADAM_STEP)
    return -ADAM_LR * (m_hat / (jnp.sqrt(v_hat) + ADAM_EPS) + ADAM_WD * w), m, v


def _update_tile_rows(rows, cols):
    if rows * cols <= UPDATE_TILE_ELEMS:
        return rows
    return max(t for t in range(8, UPDATE_TILE_ELEMS // cols + 1, 8) if rows % t == 0)


def _sum_in_order(p_ref):
    g = p_ref[0].astype(F32)
    for j in range(1, p_ref.shape[0]):
        g = g + p_ref[j].astype(F32)
    return g


def sum_parts(parts, name):
    nparts, rows, cols = parts.shape
    tr = _update_tile_rows(rows, cols)

    def body(p_ref, g_ref):
        g_ref[...] = _sum_in_order(p_ref)

    return pl.pallas_call(
        body, name=name, grid=(rows // tr,),
        in_specs=[pl.BlockSpec((nparts, tr, cols), lambda i: (0, i, 0))],
        out_specs=pl.BlockSpec((tr, cols), lambda i: (i, 0)), out_shape=jax.ShapeDtypeStruct((rows, cols), F32),
        compiler_params=_params(),
    )(parts)


def sum_adamw(parts, w, m, v, name):
    rows, cols = w.shape
    nparts = parts.shape[0]
    tr = _update_tile_rows(rows, cols)

    def body(p_ref, w_ref, m_ref, v_ref, g_ref, d_ref, nm_ref, nv_ref):
        g = _sum_in_order(p_ref)
        delta, nm, nv = _adamw(w_ref[...], g, m_ref[...], v_ref[...])
        g_ref[...] = g
        d_ref[...] = delta
        nm_ref[...] = nm
        nv_ref[...] = nv

    blk = pl.BlockSpec((tr, cols), lambda i: (i, 0))
    sd = jax.ShapeDtypeStruct((rows, cols), F32)
    return pl.pallas_call(
        body, name=name, grid=(rows // tr,),
        in_specs=[pl.BlockSpec((nparts, tr, cols), lambda i: (0, i, 0)), blk, blk, blk],
        out_specs=[blk] * 4, out_shape=(sd,) * 4,
        compiler_params=_params(),
    )(parts, w, m, v)


def update_packed(g, w, m, v, pieces, name):
    def body(g_ref, w_ref, m_ref, v_ref, *outs):
        gv = g_ref[...]
        res = (gv,) + _adamw(w_ref[...], gv, m_ref[...], v_ref[...])
        for k in range(4):
            outs[k][...] = res[k]
        for p, (row, rows, lanes) in enumerate(pieces):
            for k in range(4):
                outs[4 + 4 * p + k][...] = res[k][row:row + rows, :lanes]

    sd = jax.ShapeDtypeStruct
    shapes = [sd(w.shape, F32)] * 4 + [sd((rows, lanes), F32) for _, rows, lanes in pieces for _ in range(4)]
    return pl.pallas_call(body, name=name, out_shape=shapes,
                          compiler_params=pltpu.CompilerParams(vmem_limit_bytes=VMEM_LIMIT))(g, w, m, v)


S5_NAMES = ("w_in", "lam_re", "lam_im", "log_dt", "b_re", "b_im", "c_re", "c_im", "d_skip", "w_glu", "b_glu", "w_out")
CONV_NAMES = ("w_in", "conv_w", "conv_b", "w_out")
POOL_NAMES = ("w_in", "w_grp", "b_grp", "scale", "w_out")
LAYER_KINDS = ("s5", "conv", "pool", "s5")
LAYER_NAMES = {"s5": S5_NAMES, "conv": CONV_NAMES, "pool": POOL_NAMES}
SHARDED = {"s5": ("w_in", "w_glu", "w_out"), "conv": ("w_in", "conv_w", "w_out"), "pool": ("w_in", "w_grp", "b_grp", "w_out")}
GATHER_F32 = ("conv_w", "b_grp")


def weight_names():
    names = ["meta_tokens"]
    for i, kind in enumerate(LAYER_KINDS):
        names.append("norm%d_g" % i)
        names += ["l%d_%s" % (i, n) for n in LAYER_NAMES[kind]]
    names.append("final_g")
    return names


def sharded_names():
    return ["meta_tokens"] + ["l%d_%s" % (i, n) for i, kind in enumerate(LAYER_KINDS) for n in SHARDED[kind]]


def _block_diag_in(bb_t, gc):
    i, g, p = bb_t.shape
    t = bb_t.astype(BF16).reshape(i, 4, gc, p)
    return jnp.einsum("icjp,jk->cjikp", t, jnp.eye(gc, dtype=BF16)).reshape(4, gc * i, gc * p)


def _block_diag_in_grad(blocks):
    _, gc, i, p = blocks.shape
    return jnp.transpose(blocks, (2, 0, 1, 3)).reshape(i, 4 * gc, p)


def _block_diag_out(cc, gc):
    g, i, p = cc.shape
    return jnp.einsum("cjip,jk->cjpki", cc.astype(BF16).reshape(4, gc, i, p), jnp.eye(gc, dtype=BF16)).reshape(4, gc * p, gc * i)


def _block_diag_out_grad(blocks):
    _, gc, i, p = blocks.shape
    return blocks.reshape(4 * gc, i, p)


def _to_owner_blocks(a, axis):
    shape = a.shape[:axis] + (N_DEV, a.shape[axis] // N_DEV) + a.shape[axis + 1:]
    return jnp.moveaxis(a.reshape(shape), axis, 0)


def _from_owner_blocks(a, axis):
    a = jnp.moveaxis(a, 0, axis)
    return a.reshape(a.shape[:axis] + (a.shape[axis] * a.shape[axis + 1],) + a.shape[axis + 2:])


def _step(x, target, weights, moments_m, moments_v):
    seq, d = x.shape[1], x.shape[2]
    n_meta = weights["meta_tokens"].shape[0]
    tt = TOKEN_TILE
    pad_tiles = -(-n_meta // tt)
    p0 = pad_tiles * tt
    lp = p0 + seq
    first_pos = p0 - n_meta
    gc = d // 4 // S5_GROUP
    cw = d // 4

    big_names = [n for n in sharded_names() if n != "meta_tokens" and n.split("_", 1)[1] not in GATHER_F32]
    small_names = [n for n in sharded_names() if n not in big_names]
    layer_big = [[n for n in big_names if n.startswith("l%d_" % i)] for i in range(len(LAYER_KINDS))]
    layer_big[0] = small_names + layer_big[0]
    gather_started = []
    after = jnp.zeros((8, 128), F32)
    for i, names in enumerate(layer_big):
        gather_started.append(exchange_start([weights[n] if n in small_names else weights[n].astype(BF16) for n in names], True,
                                             after, "gather_start_l%d" % i))
        after = gather_started[-1][4]

    def vec(name):
        return weights[name].reshape(1, -1)

    s5_prep = {}
    for i, kind in enumerate(LAYER_KINDS):
        if kind == "s5":
            p = "l%d_" % i
            lr, li = weights[p + "lam_re"], weights[p + "lam_im"] + after[0, 0]
            ldt = weights[p + "log_dt"].reshape(-1, 1)
            br_t = jnp.transpose(weights[p + "b_re"], (2, 0, 1))
            bi_t = jnp.transpose(weights[p + "b_im"], (2, 0, 1))
            ar, ai, bbr, bbi = s5_disc_fwd(lr, li, ldt, br_t, bi_t, p + "disc_fwd")
            s5_prep[i] = dict(
                disc=(lr, li, ldt, br_t, bi_t), ar=ar.reshape(4, -1, 128), ai=ai.reshape(4, -1, 128),
                bdre=_block_diag_in(bbr, gc), bdim=_block_diag_in(bbi, gc),
                cdre=_block_diag_out(weights[p + "c_re"], gc), cdim=_block_diag_out(-weights[p + "c_im"], gc),
                d_skip=weights[p + "d_skip"].reshape(4, 1, cw), b_glu=vec(p + "b_glu"))
    h = jnp.concatenate([jnp.zeros((p0, d), F32), x[0] + after[0, 0]], axis=0)

    prepared = [h] + [s5_prep[i][k] for i in s5_prep for k in ("bdre", "bdim", "cdre", "cdim")]
    gathered = dict(zip(layer_big[0], exchange_wait(gather_started[0], True, prepared, "gather_wait_l0")))
    h = lax.dynamic_update_slice(h, _from_owner_blocks(gathered["meta_tokens"], 1), (first_pos, 0))

    full = {}

    def layer_weights(i, kind, after):
        p = "l%d_" % i
        if i > 0:
            gathered.update(zip(layer_big[i], exchange_wait(gather_started[i], True, after, "gather_wait_l%d" % i)))
        w_in = gathered[p + "w_in"]
        if kind == "s5":
            full[i] = dict(s5_prep[i], w_in=w_in, w_glu=gathered[p + "w_glu"].reshape(4, cw, d),
                           w_out=gathered[p + "w_out"].reshape(4, cw, d))
        elif kind == "conv":
            ce = w_in.shape[2]
            nch = 2
            conv_w = _from_owner_blocks(gathered[p + "conv_w"], 1)
            full[i] = dict(
                w_in=w_in, conv_w=jnp.transpose(conv_w.reshape(CONV_K, nch, ce), (1, 0, 2)),
                conv_b=weights[p + "conv_b"].reshape(nch, 1, ce), w_out=gathered[p + "w_out"].reshape(nch, ce, d))
        else:
            gw = w_in.shape[2]
            full[i] = dict(
                w_in=w_in, w_grp=_from_owner_blocks(gathered[p + "w_grp"], 1),
                b_grp=_from_owner_blocks(gathered[p + "b_grp"], 1).reshape(4, 1, gw),
                scale=weights[p + "scale"].reshape(4, 1, gw), w_out=gathered[p + "w_out"].reshape(4, gw, d))
        return full[i]

    saved = {}
    for i, kind in enumerate(LAYER_KINDS):
        p, f, g = "l%d_" % i, layer_weights(i, kind, h), vec("norm%d_g" % i)
        if kind == "s5":
            u, z, xs = s5_fwd1(h, g, f["w_in"], f["bdre"], f["bdim"], p + "fwd_in")
            s = s5_scan_fwd(xs, f["ar"], f["ai"], p + "scan_fwd")
            h_in = h
            h, y, q = s5_fwd3(s, u, z, h, f["cdre"], f["cdim"], f["w_glu"], f["w_out"], f["d_skip"], f["b_glu"], p + "fwd_out")
            saved[i] = (h_in, u, z, s, y, q)
        elif kind == "conv":
            h_new, halos, acts = conv_fwd(h, g, f["w_in"], f["conv_w"], f["conv_b"], f["w_out"], p + "fwd")
            saved[i] = (h, halos, acts)
            h = h_new
        else:
            h_new, halos = pool_fwd(h, g, f["w_in"], f["w_grp"], f["b_grp"], f["scale"], f["w_out"], first_pos, p + "fwd")
            saved[i] = (h, halos)
            h = h_new

    dh, dg_final, loss_tile = loss_head(h, target[0], vec("final_g"), pad_tiles, "loss_head")
    loss = lax.psum(loss_tile[0, 0], ("x", "y", "c"))

    grads = {"final_g": dg_final}
    names = weight_names()
    sh_names = sharded_names()
    rep_names = [n for n in names if n not in sh_names]

    def owner_blocks(a):
        return a.reshape(N_DEV, -1, a.shape[-1]).astype(BF16)

    def as2d(a):
        return a.reshape(-1, a.shape[-1])

    def pack(tree):
        flat = [jnp.pad(tree[n].reshape(-1), (0, -tree[n].size % 1024)) for n in rep_names]
        flat = jnp.concatenate(flat)
        return jnp.pad(flat, (0, -flat.size % (PACK_ROWS * 128))).reshape(-1, 128)

    layer_sharded, scatter_started = {}, {}
    ordered = jnp.zeros((), F32)
    for i in reversed(range(len(LAYER_KINDS))):
        kind = LAYER_KINDS[i]
        p, f, g = "l%d_" % i, full[i], vec("norm%d_g" % i) + ordered
        if kind == "s5":
            h_in, u, z, s, y, q = saved[i]
            dy, dp, dwo, dwg, dbg = s5_bwd3a(dh, y, q, z, f["w_glu"], f["w_out"], f["b_glu"] + ordered, p + "bwd_out")
            d_skip = f["d_skip"]
            if i == 0:
                early_names = [p + "w_glu", p + "w_out"]
                scatter_started["early"] = exchange_start([owner_blocks(dwg), owner_blocks(dwo)], False, dy,
                                                          "scatter_start_l0_early")
                d_skip = d_skip + scatter_started["early"][4][0, 0]
            ds, dus, dcre, dcim, dd = s5_bwd3b(dy, s, u, f["cdre"], f["cdim"], d_skip, p + "bwd_read")
            lam, dar, dai = s5_scan_bwd(ds, s, f["ar"], f["ai"], p + "scan_bwd")
            dp, dh, n, dbre, dbim, dg = s5_bwd1(lam, dus, u, dp, h_in, dh, g, f["w_in"], f["bdre"], f["bdim"], p + "bwd_in")
            dw_in = grad_w_in(n, dp, f["w_in"].shape[2], p + "grad_w_in")
            grads.update({p + "w_in": dw_in, p + "w_glu": dwg.reshape(N_DEV, -1, d), p + "w_out": dwo.reshape(N_DEV, -1, d),
                          p + "d_skip": dd, p + "b_glu": dbg})

            def replicated_grads(p=p, f=f, dar=dar, dai=dai, dbre=dbre, dbim=dbim, dcre=dcre, dcim=dcim, token=None):
                lr, li, ldt, br_t, bi_t = f["disc"]
                dlr, dli, dldt, dbr_t, dbi_t = s5_disc_bwd(
                    lr, li, ldt, br_t, bi_t, dar.reshape(lr.shape) + token, dai.reshape(lr.shape),
                    _block_diag_in_grad(dbre), _block_diag_in_grad(dbim), p + "disc_bwd")
                grads.update({
                    p + "lam_re": dlr, p + "lam_im": dli, p + "log_dt": dldt,
                    p + "b_re": jnp.transpose(dbr_t, (1, 2, 0)), p + "b_im": jnp.transpose(dbi_t, (1, 2, 0)),
                    p + "c_re": _block_diag_out_grad(dcre), p + "c_im": -_block_diag_out_grad(dcim)})
        elif kind == "conv":
            replicated_grads = None
            h_in, halos, acts = saved[i]
            dh, n, dp, dwo, dcw, dcb, dg = conv_bwd(h_in, dh, halos, acts, g, f["w_in"], f["conv_w"], f["conv_b"], f["w_out"], p + "bwd")
            dw_in = grad_w_in(n, dp, f["w_in"].shape[2], p + "grad_w_in")
            dconv_w = jnp.transpose(dcw[:, :CONV_K, :], (1, 0, 2)).reshape(CONV_K, -1)
            grads.update({p + "w_in": dw_in, p + "conv_w": _to_owner_blocks(dconv_w, 1), p + "conv_b": dcb,
                          p + "w_out": dwo.reshape(N_DEV, -1, d)})
        else:
            replicated_grads = None
            h_in, halos = saved[i]
            dh, n, dp, dwo, dwgrp, dbgrp, dsc, dg = pool_bwd(h_in, dh, halos, g, f["w_in"], f["w_grp"], f["b_grp"], f["scale"],
                                                             f["w_out"], first_pos, p + "bwd")
            dw_in = grad_w_in(n, dp, f["w_in"].shape[2], p + "grad_w_in")
            grads.update({p + "w_in": dw_in, p + "w_grp": _to_owner_blocks(dwgrp, 1),
                          p + "b_grp": _to_owner_blocks(dbgrp.reshape(4, -1), 1), p + "scale": dsc,
                          p + "w_out": dwo.reshape(N_DEV, -1, d)})
        grads["norm%d_g" % i] = dg
        layer_sharded[i] = ["l%d_%s" % (i, n) for n in SHARDED[kind]]
        if i > 0:
            scatter_started[i] = exchange_start([owner_blocks(grads[n]) for n in layer_sharded[i]], False, dh,
                                                "scatter_start_l%d" % i)
            ordered = scatter_started[i][4][0, 0]
        if replicated_grads is not None:
            replicated_grads(token=ordered)
    grad_x = dh[p0:][None]
    grads["meta_tokens"] = _to_owner_blocks(dh[first_pos:p0], 1)
    last = len(LAYER_KINDS)
    layer_sharded[last] = ["meta_tokens", "replicated"]
    scatter_started[last] = exchange_start([owner_blocks(grads["meta_tokens"]), pack(grads).reshape(N_DEV, -1, 128)], False,
                                           dh, "scatter_start_replicated")
    layer_sharded["early"] = early_names
    layer_sharded[0] = [n for n in layer_sharded[0] if n not in early_names]

    out = {}
    received = {}
    after = [scatter_started[last][4]]
    for i in list(reversed(range(1, last))) + [last, "early", 0]:
        received.update(zip(layer_sharded[i], exchange_wait(scatter_started[i], False, after, "scatter_wait_%s" % i)))
        updated = []
        for n in layer_sharded[i]:
            if n != "replicated":
                res = sum_adamw(received[n], as2d(weights[n]), as2d(moments_m[n]), as2d(moments_v[n]), "update_" + n)
                out[n] = [r.reshape(weights[n].shape) for r in res]
                updated.append(out[n][0])
        after = updated or after
        if i == last:
            g_full = exchange([sum_parts(received["replicated"], "sum_replicated")], True, "gather_small_grads")[0]
            scatter_started[0] = exchange_start([owner_blocks(grads[n]) for n in layer_sharded[0]], False, g_full,
                                                "scatter_start_l0")
            g_full = g_full.reshape(-1, 128) + scatter_started[0][4][0, 0]
            offsets, offset = {}, 0
            for n in rep_names:
                offsets[n] = offset
                offset += weights[n].size + (-weights[n].size % 1024)
            vectors = [n for n in rep_names if weights[n].ndim == 1]
            pieces = [(offsets[n] // 128, max(weights[n].size // 128, 1), min(weights[n].size, 128)) for n in vectors]
            res = update_packed(g_full, pack(weights), pack(moments_m), pack(moments_v), pieces, "update_replicated")
            packed = res[:4]
            for j, n in enumerate(vectors):
                out[n] = [r.reshape(weights[n].shape) for r in res[4 + 4 * j:8 + 4 * j]]
            for n in rep_names:
                if n not in vectors:
                    size = weights[n].size
                    out[n] = [r.reshape(-1)[offsets[n]:offsets[n] + size].reshape(weights[n].shape) for r in packed]
            after = [out[n][k] for n in rep_names for k in range(4)]

    return (loss, grad_x) + tuple(out[n][k] for k in range(4) for n in names)


def kernel(x, *rest):
    names = weight_names()
    nw = len(names)
    weights = dict(zip(names, rest[:nw]))
    target = rest[nw]
    moments_m = dict(zip(names, rest[nw + 1:2 * nw + 1]))
    moments_v = dict(zip(names, rest[2 * nw + 1:3 * nw + 1]))
    return _step(x, target, weights, moments_m, moments_v)
```

```python
import functools
import math

import jax
import jax.numpy as jnp
from jax import lax
from jax.experimental import pallas as pl
from jax.experimental.pallas import tpu as pltpu

F32 = jnp.float32
BF16 = jnp.bfloat16
EPS = 1e-6
N_DEV = 8
TOKEN_TILE = 256
SCAN_CHUNKS = 4
S5_GROUP = 16
S5_STATE = 64
POOL_WINDOWS = (2, 4, 8, 16)
POOL_HALO = 16
CONV_K = 3
CONV_HALO = 8
ADAM_LR = 0.001
ADAM_B1 = 0.9
ADAM_B2 = 0.999
ADAM_EPS = 1e-08
ADAM_WD = 0.01
ADAM_STEP = 10
GELU_C = math.sqrt(2.0 / math.pi)
GELU_A = 0.044715
UPDATE_TILE_ELEMS = 1 << 17
PACK_ROWS = 512
VMEM_LIMIT = 56 << 20
VMEM_LIMIT_LARGE = 62 << 20

ANY = pl.BlockSpec(memory_space=pl.ANY)


def _params(vmem=VMEM_LIMIT, ndim=1):
    return pltpu.CompilerParams(vmem_limit_bytes=vmem, dimension_semantics=("arbitrary",) * ndim)


def _dot(a, b):
    return jnp.dot(a.astype(BF16), b.astype(BF16), preferred_element_type=F32)


def _dot_nt(a, b):
    return lax.dot_general(a.astype(BF16), b.astype(BF16), (((1,), (1,)), ((), ())), preferred_element_type=F32)


def _dot_tn(a, b):
    return lax.dot_general(a.astype(BF16), b.astype(BF16), (((0,), (0,)), ((), ())), preferred_element_type=F32)


def _rms_fwd(h, g):
    r = lax.rsqrt(jnp.mean(h * h, axis=-1, keepdims=True) + EPS)
    hh = h * r
    return hh * g, hh, r


def _rms_bwd(dn, hh, r, g):
    dhh = dn * g
    return r * (dhh - hh * jnp.mean(dhh * hh, axis=-1, keepdims=True))


def _sigmoid(x):
    return 1.0 / (1.0 + jnp.exp(-x))


def _silu_and_grad(z):
    s = _sigmoid(z)
    return z * s, s * (1.0 + z * (1.0 - s))


def _gelu(y):
    t = jnp.tanh(GELU_C * (y + GELU_A * y * y * y))
    return 0.5 * y * (1.0 + t), t


def _gelu_grad(y, t):
    return 0.5 * (1.0 + t) + 0.5 * y * (1.0 - t * t) * GELU_C * (1.0 + 3.0 * GELU_A * y * y)


def _rows(shape):
    return lax.broadcasted_iota(jnp.int32, shape, 0)


def _shift_down(x, k, halo):
    y = pltpu.roll(x, k, 0)
    rows = _rows(x.shape)
    for j in range(k):
        y = jnp.where(rows == j, halo[halo.shape[0] - k + j:halo.shape[0] - k + j + 1, :], y)
    return y


def _shift_up(x, k, halo):
    n = x.shape[0]
    y = pltpu.roll(x, n - k, 0)
    rows = _rows(x.shape)
    for j in range(k):
        y = jnp.where(rows == n - k + j, halo[j:j + 1, :], y)
    return y


def _window_sums_back(ext):
    out = []
    s = ext
    for k in (1, 2, 4, 8):
        s = s + pltpu.roll(s, k, 0)
        out.append(s)
    return out


def _window_sums_fwd(ext):
    n = ext.shape[0]
    out = []
    s = ext
    for k in (1, 2, 4, 8):
        s = s + pltpu.roll(s, n - k, 0)
        out.append(s)
    return out


def _pool_inv_count(tile, tt, first_pos, w, width):
    pos = _rows((tt, width)) + (tile * tt - first_pos + 1)
    return 1.0 / jnp.clip(pos, 1, w).astype(F32)


def _slab_spec(lp, tt, sw):
    nj = sw // 128
    return pl.BlockSpec((4 * tt * nj, 128), lambda i: (i, 0)), (lp * 4 * nj, 128)


def _pack_pair(re, im):
    def rounded(v):
        return lax.bitcast_convert_type(v, jnp.int32) + 0x8000
    return lax.bitcast_convert_type((rounded(re) & -65536) | lax.shift_right_logical(rounded(im), 16), F32)


def _unpack_pair(w):
    b = lax.bitcast_convert_type(w, jnp.int32)
    return lax.bitcast_convert_type(b & -65536, F32), lax.bitcast_convert_type(lax.shift_left(b, 16), F32)


def _slab_load(ref, c):
    nj = ref.shape[0] // (4 * TOKEN_TILE)
    first = c * TOKEN_TILE * nj
    return _unpack_pair(jnp.concatenate([ref[pl.ds(first + j, TOKEN_TILE, stride=nj), :] for j in range(nj)], axis=1))


def _slab_store(ref, c, re, im):
    nj = ref.shape[0] // (4 * TOKEN_TILE)
    first = c * TOKEN_TILE * nj
    val = _pack_pair(re, im)
    for j in range(nj):
        ref[pl.ds(first + j, TOKEN_TILE, stride=nj), :] = val[:, j * 128:(j + 1) * 128]


def _s5_disc_math(lr, li, ldt, br, bi):
    dt = jnp.exp(ldt)
    mag = jnp.exp(lr * dt)
    ar = mag * jnp.cos(li * dt)
    ai = mag * jnp.sin(li * dt)
    den = lr * lr + li * li
    kr = ((ar - 1.0) * lr + ai * li) / den
    ki = (ai * lr - (ar - 1.0) * li) / den
    bbr = kr[None] * br - ki[None] * bi
    bbi = kr[None] * bi + ki[None] * br
    return ar, ai, bbr, bbi


def s5_disc_fwd(lr, li, ldt, br_t, bi_t, name):
    def body(lr_ref, li_ref, ldt_ref, br_ref, bi_ref, ar_ref, ai_ref, bbr_ref, bbi_ref):
        ar, ai, bbr, bbi = _s5_disc_math(lr_ref[...], li_ref[...], ldt_ref[...], br_ref[...], bi_ref[...])
        ar_ref[...] = ar
        ai_ref[...] = ai
        bbr_ref[...] = bbr
        bbi_ref[...] = bbi

    sd = jax.ShapeDtypeStruct
    return pl.pallas_call(
        body, name=name,
        out_shape=(sd(lr.shape, F32), sd(lr.shape, F32), sd(br_t.shape, F32), sd(br_t.shape, F32)),
    )(lr, li, ldt, br_t, bi_t)


def s5_disc_bwd(lr, li, ldt, br_t, bi_t, dar, dai, dbbr, dbbi, name):
    def body(lr_ref, li_ref, ldt_ref, br_ref, bi_ref, dar_ref, dai_ref, dbbr_ref, dbbi_ref,
             dlr_ref, dli_ref, dldt_ref, dbr_ref, dbi_ref):
        _, vjp = jax.vjp(_s5_disc_math, lr_ref[...], li_ref[...], ldt_ref[...], br_ref[...], bi_ref[...])
        dlr, dli, dldt, dbr, dbi = vjp((dar_ref[...], dai_ref[...], dbbr_ref[...], dbbi_ref[...]))
        dlr_ref[...] = dlr
        dli_ref[...] = dli
        dldt_ref[...] = dldt
        dbr_ref[...] = dbr
        dbi_ref[...] = dbi

    sd = jax.ShapeDtypeStruct
    return pl.pallas_call(
        body, name=name,
        out_shape=(sd(lr.shape, F32), sd(lr.shape, F32), sd(ldt.shape, F32), sd(br_t.shape, F32), sd(br_t.shape, F32)),
    )(lr, li, ldt, br_t, bi_t, dar, dai, dbbr, dbbi)


def s5_fwd1(h, g, w_in, bdre, bdim, name):
    lp, d = h.shape
    tt = TOKEN_TILE
    cw, sw = bdre.shape[1], bdre.shape[2]

    def body(h_ref, g_ref, w_hbm, bdre_hbm, bdim_hbm, u_ref, z_ref, x_ref, w, bre, bim):
        @pl.when(pl.program_id(0) == 0)
        def _():
            pltpu.sync_copy(w_hbm, w)
            pltpu.sync_copy(bdre_hbm, bre)
            pltpu.sync_copy(bdim_hbm, bim)

        n = _rms_fwd(h_ref[...], g_ref[...])[0].astype(BF16)
        for c in range(4):
            cols = slice(c * cw, (c + 1) * cw)
            u = jnp.dot(n, w[c], preferred_element_type=F32)
            u_ref[:, cols] = u
            z_ref[:, cols] = jnp.dot(n, w[c + 4], preferred_element_type=F32)
            ub = u.astype(BF16)
            _slab_store(x_ref, c, jnp.dot(ub, bre[c], preferred_element_type=F32), jnp.dot(ub, bim[c], preferred_element_type=F32))

    sd = jax.ShapeDtypeStruct
    slab, slab_shape = _slab_spec(lp, tt, sw)
    row = pl.BlockSpec((tt, d), lambda i: (i, 0))
    return pl.pallas_call(
        body, name=name, grid=(lp // tt,),
        in_specs=[row, pl.BlockSpec((1, d), lambda i: (0, 0)), ANY, ANY, ANY],
        out_specs=[row, row, slab],
        out_shape=(sd((lp, d), F32), sd((lp, d), F32), sd(slab_shape, F32)),
        scratch_shapes=[pltpu.VMEM(w_in.shape, BF16), pltpu.VMEM(bdre.shape, BF16), pltpu.VMEM(bdim.shape, BF16)],
        compiler_params=_params(),
    )(h, g, w_in, bdre, bdim)


def s5_scan_fwd(x, ar, ai, name):
    nj = ar.shape[1]
    tt = TOKEN_TILE
    cpb = SCAN_CHUNKS
    nt = x.shape[0] // (4 * tt * nj)

    def body(x_ref, ar_ref, ai_ref, s_ref, st_r, st_i):
        i, cg = pl.program_id(0), pl.program_id(1)

        @pl.when(i == 0)
        def _():
            for q in range(cpb):
                st_r[cg * cpb + q] = jnp.zeros((nj, 128), F32)
                st_i[cg * cpb + q] = jnp.zeros((nj, 128), F32)

        a_r = [ar_ref[cg * cpb + q] for q in range(cpb)]
        a_i = [ai_ref[cg * cpb + q] for q in range(cpb)]

        def step(t, carry):
            out = []
            for q in range(cpb):
                s_r, s_i = carry[q]
                rows = pl.ds(pl.multiple_of((q * tt + t) * nj, nj), nj)
                x_r, x_i = _unpack_pair(x_ref[rows, :])
                n_r = a_r[q] * s_r - a_i[q] * s_i + x_r
                n_i = a_r[q] * s_i + a_i[q] * s_r + x_i
                s_ref[rows, :] = _pack_pair(n_r, n_i)
                out.append((n_r, n_i))
            return tuple(out)

        init = tuple((st_r[cg * cpb + q], st_i[cg * cpb + q]) for q in range(cpb))
        final = lax.fori_loop(0, tt, step, init, unroll=8)
        for q in range(cpb):
            st_r[cg * cpb + q] = final[q][0]
            st_i[cg * cpb + q] = final[q][1]

    blk = pl.BlockSpec((cpb * tt * nj, 128), lambda i, cg: (i * (4 // cpb) + cg, 0))
    par = pl.BlockSpec((4, nj, 128), lambda i, cg: (0, 0, 0))
    sd = jax.ShapeDtypeStruct
    return pl.pallas_call(
        body, name=name, grid=(nt, 4 // cpb),
        in_specs=[blk, par, par], out_specs=blk,
        out_shape=sd(x.shape, F32),
        scratch_shapes=[pltpu.VMEM((4, nj, 128), F32), pltpu.VMEM((4, nj, 128), F32)],
        compiler_params=_params(ndim=2),
    )(x, ar, ai)


def s5_fwd3(s, u, z, h, cdre, cdim, w_glu, w_out, d_skip, b_glu, name):
    lp, d = h.shape
    tt = TOKEN_TILE
    sw, cw = cdre.shape[1], cdre.shape[2]

    def body(s_ref, u_ref, z_ref, h_ref, d_ref, bg_ref, cre_hbm, cim_hbm, wg_hbm, wo_hbm,
             o_ref, y_ref, q_ref, cre, cim, wg, wo):
        @pl.when(pl.program_id(0) == 0)
        def _():
            pltpu.sync_copy(cre_hbm, cre)
            pltpu.sync_copy(cim_hbm, cim)
            pltpu.sync_copy(wg_hbm, wg)
            pltpu.sync_copy(wo_hbm, wo)

        gys, q = [], None
        for c in range(4):
            cols = slice(c * cw, (c + 1) * cw)
            s_r, s_i = _slab_load(s_ref, c)
            y = _dot(s_r, cre[c]) + _dot(s_i, cim[c]) + d_ref[c] * u_ref[:, cols]
            y_ref[:, cols] = y
            gys.append(_gelu(y)[0])
            part = _dot(gys[c], wg[c])
            q = part if c == 0 else q + part
        q_ref[...] = q
        sig = _sigmoid(q + bg_ref[...])
        zz = z_ref[...]
        sz = zz * _sigmoid(zz)
        o = h_ref[...]
        for k in range(4):
            cols = slice(k * cw, (k + 1) * cw)
            o = o + _dot(gys[k] * sig[:, cols] * sz[:, cols], wo[k])
        o_ref[...] = o

    row = pl.BlockSpec((tt, d), lambda i: (i, 0))
    slab, _ = _slab_spec(lp, tt, sw)
    sd = jax.ShapeDtypeStruct((lp, d), F32)
    return pl.pallas_call(
        body, name=name, grid=(lp // tt,),
        in_specs=[slab, row, row, row, pl.BlockSpec((4, 1, cw), lambda i: (0, 0, 0)), pl.BlockSpec((1, d), lambda i: (0, 0)),
                  ANY, ANY, ANY, ANY],
        out_specs=[row, row, row],
        out_shape=(sd, sd, sd),
        scratch_shapes=[pltpu.VMEM(cdre.shape, BF16), pltpu.VMEM(cdim.shape, BF16), pltpu.VMEM(w_glu.shape, BF16),
                        pltpu.VMEM(w_out.shape, BF16)],
        compiler_params=_params(),
    )(s, u, z, h, d_skip, b_glu, cdre, cdim, w_glu, w_out)


def s5_bwd3a(dh, y, q, z, w_glu, w_out, b_glu, name):
    lp, d = dh.shape
    tt = TOKEN_TILE
    nt = lp // tt
    cw = w_glu.shape[1]

    def body(dh_ref, y_ref, q_ref, z_ref, bg_ref, wg_hbm, wo_hbm, dy_ref, dp_ref, dwo_hbm, dwg_hbm, dbg_hbm,
             wg, wo, dwo, dwg, dbg):
        i = pl.program_id(0)

        @pl.when(i == 0)
        def _():
            pltpu.sync_copy(wg_hbm, wg)
            pltpu.sync_copy(wo_hbm, wo)
            dwo[...] = jnp.zeros_like(dwo)
            dwg[...] = jnp.zeros_like(dwg)
            dbg[...] = jnp.zeros_like(dbg)

        sig = _sigmoid(q_ref[...] + bg_ref[...])
        sz, dsz = _silu_and_grad(z_ref[...])
        dhv = dh_ref[...]
        yv = y_ref[...]
        gy, t = _gelu(yv)
        dq_parts, dgy_parts = [], []
        for k in range(4):
            cols = slice(k * cw, (k + 1) * cw)
            gy_k, sig_k, sz_k = gy[:, cols], sig[:, cols], sz[:, cols]
            y2 = gy_k * sig_k
            dy3 = _dot_nt(dhv, wo[k])
            dwo[k] += _dot_tn(y2 * sz_k, dhv)
            dy2 = dy3 * sz_k
            dp_ref[0, :, cols] = (dy3 * y2 * dsz[:, cols]).astype(BF16)
            dq_parts.append(dy2 * gy_k * sig_k * (1.0 - sig_k))
            dgy_parts.append(dy2 * sig_k)
        dq = jnp.concatenate(dq_parts, axis=1)
        dbg[...] += jnp.sum(dq, axis=0, keepdims=True)
        dgelu = _gelu_grad(yv, t)
        for k in range(4):
            cols = slice(k * cw, (k + 1) * cw)
            dwg[k] += _dot_tn(gy[:, cols], dq)
            dy_ref[:, cols] = (dgy_parts[k] + _dot_nt(dq, wg[k])) * dgelu[:, cols]

        @pl.when(i == nt - 1)
        def _():
            pltpu.sync_copy(dwo, dwo_hbm)
            pltpu.sync_copy(dwg, dwg_hbm)
            pltpu.sync_copy(dbg, dbg_hbm)

    row = pl.BlockSpec((tt, d), lambda i: (i, 0))
    sd = jax.ShapeDtypeStruct
    return pl.pallas_call(
        body, name=name, grid=(nt,),
        in_specs=[row, row, row, row, pl.BlockSpec((1, d), lambda i: (0, 0)), ANY, ANY],
        out_specs=[row, pl.BlockSpec((1, tt, d), lambda i: (1, i, 0)), ANY, ANY, ANY],
        out_shape=(sd((lp, d), F32), sd((2, lp, d), BF16), sd(w_out.shape, F32), sd(w_glu.shape, F32), sd((1, d), F32)),
        scratch_shapes=[pltpu.VMEM(w_glu.shape, BF16), pltpu.VMEM(w_out.shape, BF16),
                        pltpu.VMEM(w_out.shape, F32), pltpu.VMEM(w_glu.shape, F32), pltpu.VMEM((1, d), F32)],
        compiler_params=_params(),
    )(dh, y, q, z, b_glu, w_glu, w_out)


def s5_bwd3b(dy, s, u, cdre, cdim, d_skip, name):
    lp, d = dy.shape
    tt = TOKEN_TILE
    nt = lp // tt
    sw, cw = cdre.shape[1], cdre.shape[2]
    gc = cw // S5_GROUP

    def body(dy_ref, s_ref, u_ref, d_ref, cre_hbm, cim_hbm,
             ds_ref, dus_ref, dcre_ref, dcim_ref, dd_hbm, cre, cim, dcre, dcim, dd):
        i = pl.program_id(0)

        @pl.when(i == 0)
        def _():
            pltpu.sync_copy(cre_hbm, cre)
            pltpu.sync_copy(cim_hbm, cim)
            dcre[...] = jnp.zeros_like(dcre)
            dcim[...] = jnp.zeros_like(dcim)
            dd[...] = jnp.zeros_like(dd)

        for c in range(4):
            chunk = slice(c * cw, (c + 1) * cw)
            dyv = dy_ref[:, chunk]
            dd[c] += jnp.sum(dyv * u_ref[:, chunk], axis=0, keepdims=True)
            dus_ref[:, chunk] = dyv * d_ref[c]
            _slab_store(ds_ref, c, _dot_nt(dyv, cre[c]), _dot_nt(dyv, cim[c]))
            s_r, s_i = _slab_load(s_ref, c)
            dcre[c] += _dot_tn(s_r, dyv)
            dcim[c] += _dot_tn(s_i, dyv)

        @pl.when(i == nt - 1)
        def _():
            for k in range(4):
                for j in range(gc):
                    rows, cols = pl.ds(j * S5_STATE, S5_STATE), pl.ds(j * S5_GROUP, S5_GROUP)
                    dcre_ref[k, j] = dcre[k, rows, cols].T
                    dcim_ref[k, j] = dcim[k, rows, cols].T
            pltpu.sync_copy(dd, dd_hbm)

    sd = jax.ShapeDtypeStruct
    row = pl.BlockSpec((tt, d), lambda i: (i, 0))
    slab, slab_shape = _slab_spec(lp, tt, sw)
    diag = pl.BlockSpec((4, gc, S5_GROUP, S5_STATE), lambda i: (0, 0, 0, 0))
    return pl.pallas_call(
        body, name=name, grid=(nt,),
        in_specs=[row, slab, row, pl.BlockSpec((4, 1, cw), lambda i: (0, 0, 0)), ANY, ANY],
        out_specs=[slab, row, diag, diag, ANY],
        out_shape=(sd(slab_shape, F32), sd((lp, d), F32),
                   sd((4, gc, S5_GROUP, S5_STATE), F32), sd((4, gc, S5_GROUP, S5_STATE), F32), sd((4, 1, cw), F32)),
        scratch_shapes=[pltpu.VMEM(cdre.shape, BF16), pltpu.VMEM(cdim.shape, BF16),
                        pltpu.VMEM(cdre.shape, F32), pltpu.VMEM(cdim.shape, F32), pltpu.VMEM((4, 1, cw), F32)],
        compiler_params=_params(),
    )(dy, s, u, d_skip, cdre, cdim)


def s5_scan_bwd(g, s, ar, ai, name):
    nj = ar.shape[1]
    tt = TOKEN_TILE
    cpb = SCAN_CHUNKS
    nt = g.shape[0] // (4 * tt * nj)

    def body(g_ref, s_ref, ar_ref, ai_ref, lam_ref, dar_ref, dai_ref, st_r, st_i, acc_r, acc_i):
        i, cg = pl.program_id(0), pl.program_id(1)

        @pl.when((i == 0) & (cg == 0))
        def _():
            for ref in (st_r, st_i, acc_r, acc_i):
                ref[...] = jnp.zeros_like(ref)

        a_r = [ar_ref[cg * cpb + q] for q in range(cpb)]
        a_i = [ai_ref[cg * cpb + q] for q in range(cpb)]

        def slab(q, t):
            return pl.ds(pl.multiple_of((q * tt + t) * nj, nj), nj)

        def adjoint(q, t, l_r, l_i):
            rows = slab(q, t)
            g_r, g_i = _unpack_pair(g_ref[rows, :])
            n_r = g_r + a_r[q] * l_r + a_i[q] * l_i
            n_i = g_i + a_r[q] * l_i - a_i[q] * l_r
            lam_ref[rows, :] = _pack_pair(n_r, n_i)
            return n_r, n_i

        def pair(q, t, l_r, l_i, d_r, d_i):
            p_r, p_i = _unpack_pair(s_ref[slab(q, t), :])
            return d_r + l_r * p_r + l_i * p_i, d_i + l_i * p_r - l_r * p_i

        def step(k, carry):
            t = tt - 1 - k
            out = []
            for q in range(cpb):
                l_r, l_i, d_r, d_i = carry[q]
                l_r, l_i = adjoint(q, t, l_r, l_i)
                d_r, d_i = pair(q, t - 1, l_r, l_i, d_r, d_i)
                out.append((l_r, l_i, d_r, d_i))
            return tuple(out)

        init = []
        for q in range(cpb):
            ch = cg * cpb + q
            l_r, l_i = st_r[ch], st_i[ch]
            d_r, d_i = pair(q, tt - 1, l_r, l_i, acc_r[ch], acc_i[ch])
            init.append((l_r, l_i, d_r, d_i))
        final = lax.fori_loop(0, tt - 1, step, tuple(init), unroll=8)
        for q in range(cpb):
            ch = cg * cpb + q
            l_r, l_i, d_r, d_i = final[q]
            l_r, l_i = adjoint(q, 0, l_r, l_i)
            st_r[ch] = l_r
            st_i[ch] = l_i
            acc_r[ch] = d_r
            acc_i[ch] = d_i
            dar_ref[ch] = d_r
            dai_ref[ch] = d_i

    blk = pl.BlockSpec((cpb * tt * nj, 128), lambda i, cg: ((nt - 1 - i) * (4 // cpb) + cg, 0))
    par = pl.BlockSpec((4, nj, 128), lambda i, cg: (0, 0, 0))
    sd = jax.ShapeDtypeStruct
    return pl.pallas_call(
        body, name=name, grid=(nt, 4 // cpb),
        in_specs=[blk, blk, par, par], out_specs=[blk, par, par],
        out_shape=(sd(g.shape, F32), sd((4, nj, 128), F32), sd((4, nj, 128), F32)),
        scratch_shapes=[pltpu.VMEM((4, nj, 128), F32)] * 4,
        compiler_params=_params(ndim=2),
    )(g, s, ar, ai)


def s5_bwd1(lam, dus, u, dp, h, dh, g, w_in, bdre, bdim, name):
    lp, d = h.shape
    tt = TOKEN_TILE
    nt = lp // tt
    cw, sw = bdre.shape[1], bdre.shape[2]
    gc = cw // S5_GROUP

    def body(lam_ref, dus_ref, u_ref, dpz_ref, h_ref, dh_ref, g_ref, w_hbm, bre_hbm, bim_hbm,
             dpu_ref, dho_ref, n_ref, dbre_ref, dbim_ref, dg_hbm, w, bre, bim, dbre, dbim, dg):
        i = pl.program_id(0)

        @pl.when(i == 0)
        def _():
            pltpu.sync_copy(w_hbm, w)
            pltpu.sync_copy(bre_hbm, bre)
            pltpu.sync_copy(bim_hbm, bim)
            dbre[...] = jnp.zeros_like(dbre)
            dbim[...] = jnp.zeros_like(dbim)
            dg[...] = jnp.zeros_like(dg)

        dz = dpz_ref[0]
        dn = None
        for c in range(4):
            chunk = slice(c * cw, (c + 1) * cw)
            (l_r, l_i), uv = _slab_load(lam_ref, c), u_ref[:, chunk]
            du = dus_ref[:, chunk] + _dot_nt(l_r, bre[c]) + _dot_nt(l_i, bim[c])
            dbre[c] += _dot_tn(uv, l_r)
            dbim[c] += _dot_tn(uv, l_i)
            dpu_ref[0, :, chunk] = du.astype(BF16)
            part = _dot_nt(du, w[c]) + _dot_nt(dz[:, chunk], w[4 + c])
            dn = part if c == 0 else dn + part
        gv = g_ref[...]
        n, hh, rr = _rms_fwd(h_ref[...], gv)
        n_ref[...] = n.T.astype(BF16)
        dg[...] += jnp.sum(dn * hh, axis=0, keepdims=True)
        dho_ref[...] = dh_ref[...] + _rms_bwd(dn, hh, rr, gv)

        @pl.when(i == nt - 1)
        def _():
            for k in range(4):
                for j in range(gc):
                    rows, cols = pl.ds(j * S5_GROUP, S5_GROUP), pl.ds(j * S5_STATE, S5_STATE)
                    dbre_ref[k, j] = dbre[k, rows, cols]
                    dbim_ref[k, j] = dbim[k, rows, cols]
            pltpu.sync_copy(dg, dg_hbm)

    sd = jax.ShapeDtypeStruct
    row = pl.BlockSpec((tt, d), lambda i: (i, 0))
    slab, _ = _slab_spec(lp, tt, sw)
    diag = pl.BlockSpec((4, gc, S5_GROUP, S5_STATE), lambda i: (0, 0, 0, 0))
    return pl.pallas_call(
        body, name=name, grid=(nt,),
        in_specs=[slab, row, row, pl.BlockSpec((1, tt, d), lambda i: (1, i, 0)), row, row, pl.BlockSpec((1, d), lambda i: (0, 0)),
                  ANY, ANY, ANY],
        out_specs=[pl.BlockSpec((1, tt, d), lambda i: (0, i, 0)), row, pl.BlockSpec((d, tt), lambda i: (0, i)), diag, diag, ANY],
        out_shape=(sd(dp.shape, BF16), sd((lp, d), F32), sd((d, lp), BF16),
                   sd((4, gc, S5_GROUP, S5_STATE), F32), sd((4, gc, S5_GROUP, S5_STATE), F32), sd((1, d), F32)),
        input_output_aliases={3: 0},
        scratch_shapes=[pltpu.VMEM(w_in.shape, BF16), pltpu.VMEM(bdre.shape, BF16), pltpu.VMEM(bdim.shape, BF16),
                        pltpu.VMEM(bdre.shape, F32), pltpu.VMEM(bdim.shape, F32), pltpu.VMEM((1, d), F32)],
        compiler_params=_params(),
    )(lam, dus, u, dp, h, dh, g, w_in, bdre, bdim)


def grad_w_in(n_t, dp, blk, name):
    d, lp = n_t.shape
    npart, _, width = dp.shape
    per = width // blk

    def body(n_ref, dp_ref, o_ref):
        o_ref[0] = jnp.dot(n_ref[...], dp_ref[0], preferred_element_type=F32).astype(o_ref.dtype)

    return pl.pallas_call(
        body, name=name, grid=(npart * per,),
        in_specs=[pl.BlockSpec((d, lp), lambda j: (0, 0), pipeline_mode=pl.Buffered(1)),
                  pl.BlockSpec((1, lp, blk), lambda j: (j // per, 0, j % per))],
        out_specs=pl.BlockSpec((1, d, blk), lambda j: (j, 0, 0)),
        out_shape=jax.ShapeDtypeStruct((npart * per, d, blk), BF16),
        compiler_params=_params(),
    )(n_t, dp)


def _conv_mix(cg, v, cw_ref, cb_ref, halo, c):
    hc = cg * v
    taps = cw_ref[c]
    conv = taps[2:3, :] * hc + taps[1:2, :] * _shift_down(hc, 1, halo) + taps[0:1, :] * _shift_down(hc, 2, halo) + cb_ref[c]
    return hc, conv


def conv_fwd(h, g, w_in, conv_w, conv_b, w_out, name):
    lp, d = h.shape
    tt = TOKEN_TILE
    nt = lp // tt
    nch, ce = w_out.shape[0], w_out.shape[1]

    def body(h_ref, g_ref, cw_ref, cb_ref, w_hbm, wo_hbm, o_ref, halo_ref, acts_ref, w, wo, halo):
        i = pl.program_id(0)

        @pl.when(i == 0)
        def _():
            pltpu.sync_copy(w_hbm, w)
            pltpu.sync_copy(wo_hbm, wo)
            halo[...] = jnp.zeros_like(halo)

        hv = h_ref[...]
        n = _rms_fwd(hv, g_ref[...])[0].astype(BF16)
        o = hv
        for c in range(nch):
            cols = slice(c * ce, (c + 1) * ce)
            bg, cg, v, z = [jnp.dot(n, w[p * nch + c], preferred_element_type=F32) for p in range(4)]
            for p, val in enumerate((bg, cg, v, z)):
                acts_ref[p, :, cols] = val.astype(BF16)
            hc, conv = _conv_mix(cg, v, cw_ref, cb_ref, halo[c], c)
            o = o + _dot(bg * conv * (z * _sigmoid(z)), wo[c])
            halo[c] = hc[tt - CONV_HALO:, :]
            halo_ref[0, c] = hc[tt - CONV_HALO:, :]
        o_ref[...] = o

    sd = jax.ShapeDtypeStruct
    return pl.pallas_call(
        body, name=name, grid=(nt,),
        in_specs=[pl.BlockSpec((tt, d), lambda i: (i, 0)), pl.BlockSpec((1, d), lambda i: (0, 0)),
                  pl.BlockSpec(conv_w.shape, lambda i: (0, 0, 0)), pl.BlockSpec(conv_b.shape, lambda i: (0, 0, 0)), ANY, ANY],
        out_specs=[pl.BlockSpec((tt, d), lambda i: (i, 0)), pl.BlockSpec((1, nch, CONV_HALO, ce), lambda i: (i, 0, 0, 0)),
                   pl.BlockSpec((4, tt, nch * ce), lambda i: (0, i, 0))],
        out_shape=(sd((lp, d), F32), sd((nt, nch, CONV_HALO, ce), F32), sd((4, lp, nch * ce), BF16)),
        scratch_shapes=[pltpu.VMEM(w_in.shape, BF16), pltpu.VMEM(w_out.shape, BF16), pltpu.VMEM((nch, CONV_HALO, ce), F32)],
        compiler_params=_params(),
    )(h, g, conv_w, conv_b, w_in, w_out)


def conv_bwd(h, dh, halos, acts, g, w_in, conv_w, conv_b, w_out, name):
    lp, d = h.shape
    tt = TOKEN_TILE
    nt = lp // tt
    nch, ce = w_out.shape[0], w_out.shape[1]

    def body(h_ref, dh_ref, halo_ref, acts_ref, g_ref, cw_ref, cb_ref, w_hbm, wo_hbm,
             dho_ref, n_ref, dp_ref, dwo_hbm, dcw_hbm, dcb_hbm, dg_hbm, w, wo, nxt, dwo, dcw, dcb, dg):
        i = pl.program_id(0)

        @pl.when(i == 0)
        def _():
            pltpu.sync_copy(w_hbm, w)
            pltpu.sync_copy(wo_hbm, wo)
            for ref in (nxt, dwo, dcw, dcb, dg):
                ref[...] = jnp.zeros_like(ref)

        gv = g_ref[...]
        nf, hh, rr = _rms_fwd(h_ref[...], gv)
        n_ref[...] = nf.T.astype(BF16)
        dhv = dh_ref[...]
        has_prev = (i < nt - 1).astype(F32)
        dn = jnp.zeros((tt, d), F32)
        for c in range(nch):
            halo = halo_ref[0, c] * has_prev
            cols = slice(c * ce, (c + 1) * ce)
            bg, cg, v, z = [acts_ref[p, :, cols].astype(F32) for p in range(4)]
            hc, conv = _conv_mix(cg, v, cw_ref, cb_ref, halo, c)
            sz, dsz = _silu_and_grad(z)
            y1 = bg * conv
            dy2 = _dot_nt(dhv, wo[c])
            dwo[c] += _dot_tn(y1 * sz, dhv)
            dy1 = dy2 * sz
            dz = dy2 * y1 * dsz
            dbg = dy1 * conv
            dconv = dy1 * bg
            dcb[c] += jnp.sum(dconv, axis=0, keepdims=True)
            up1 = _shift_up(dconv, 1, nxt[c])
            up2 = _shift_up(dconv, 2, nxt[c])
            nxt[c] = dconv[:CONV_HALO, :]
            taps = cw_ref[c]
            dhc = taps[2:3, :] * dconv + taps[1:2, :] * up1 + taps[0:1, :] * up2
            dcw[c, 0:1, :] += jnp.sum(hc * up2, axis=0, keepdims=True)
            dcw[c, 1:2, :] += jnp.sum(hc * up1, axis=0, keepdims=True)
            dcw[c, 2:3, :] += jnp.sum(hc * dconv, axis=0, keepdims=True)
            dcg = dhc * v
            dv = dhc * cg
            for p, val in enumerate((dbg, dcg, dv, dz)):
                dp_ref[p, :, cols] = val.astype(BF16)
                dn = dn + _dot_nt(val, w[p * nch + c])
        dg[...] += jnp.sum(dn * hh, axis=0, keepdims=True)
        dho_ref[...] = dhv + _rms_bwd(dn, hh, rr, gv)

        @pl.when(i == nt - 1)
        def _():
            pltpu.sync_copy(dwo, dwo_hbm)
            pltpu.sync_copy(dcw, dcw_hbm)
            pltpu.sync_copy(dcb, dcb_hbm)
            pltpu.sync_copy(dg, dg_hbm)

    rev = lambda i: (nt - 1 - i, 0)
    sd = jax.ShapeDtypeStruct
    return pl.pallas_call(
        body, name=name, grid=(nt,),
        in_specs=[pl.BlockSpec((tt, d), rev), pl.BlockSpec((tt, d), rev),
                  pl.BlockSpec((1, nch, CONV_HALO, ce), lambda i: (jnp.maximum(nt - 2 - i, 0), 0, 0, 0)),
                  pl.BlockSpec((4, tt, nch * ce), lambda i: (0, nt - 1 - i, 0)),
                  pl.BlockSpec((1, d), lambda i: (0, 0)),
                  pl.BlockSpec(conv_w.shape, lambda i: (0, 0, 0)), pl.BlockSpec(conv_b.shape, lambda i: (0, 0, 0)), ANY, ANY],
        out_specs=[pl.BlockSpec((tt, d), rev), pl.BlockSpec((d, tt), lambda i: (0, nt - 1 - i)),
                   pl.BlockSpec((4, tt, nch * ce), lambda i: (0, nt - 1 - i, 0)), ANY, ANY, ANY, ANY],
        out_shape=(sd((lp, d), F32), sd((d, lp), BF16), sd((4, lp, nch * ce), BF16),
                   sd(w_out.shape, F32), sd((nch, 8, ce), F32), sd((nch, 1, ce), F32), sd((1, d), F32)),
        scratch_shapes=[pltpu.VMEM(w_in.shape, BF16), pltpu.VMEM(w_out.shape, BF16), pltpu.VMEM((nch, CONV_HALO, ce), F32),
                        pltpu.VMEM(w_out.shape, F32), pltpu.VMEM((nch, 8, ce), F32), pltpu.VMEM((nch, 1, ce), F32),
                        pltpu.VMEM((1, d), F32)],
        compiler_params=_params(vmem=VMEM_LIMIT_LARGE),
    )(h, dh, halos, acts, g, conv_w, conv_b, w_in, w_out)


def _pool_mix(u, wg, bg_ref, sc_ref, halo, k, tile, tt, first_pos):
    ext = jnp.concatenate([halo, u], axis=0)
    win = _window_sums_back(ext)[k][POOL_HALO:, :]
    mixed = win * _pool_inv_count(tile, tt, first_pos, POOL_WINDOWS[k], u.shape[1]) - u
    outs = _dot(mixed, wg[k]) + bg_ref[k]
    return mixed, outs, outs * sc_ref[k]


def pool_fwd(h, g, w_in, w_grp, b_grp, scale, w_out, first_pos, name):
    lp, d = h.shape
    tt = TOKEN_TILE
    nt = lp // tt
    gw = w_grp.shape[1]

    def body(h_ref, g_ref, bg_ref, sc_ref, w_hbm, wg_hbm, wo_hbm, o_ref, halo_ref, acts_ref, w, wg, wo, halo):
        i = pl.program_id(0)

        @pl.when(i == 0)
        def _():
            pltpu.sync_copy(w_hbm, w)
            pltpu.sync_copy(wg_hbm, wg)
            pltpu.sync_copy(wo_hbm, wo)
            halo[...] = jnp.zeros_like(halo)

        hv = h_ref[...]
        n = _rms_fwd(hv, g_ref[...])[0].astype(BF16)
        o = hv
        for k in range(4):
            cols = slice(k * gw, (k + 1) * gw)
            u = jnp.dot(n, w[k], preferred_element_type=F32)
            z = jnp.dot(n, w[4 + k], preferred_element_type=F32)
            acts_ref[0, :, cols] = u.astype(BF16)
            acts_ref[1, :, cols] = z.astype(BF16)
            _, _, yp = _pool_mix(u, wg, bg_ref, sc_ref, halo[k], k, i, tt, first_pos)
            o = o + _dot(yp * (z * _sigmoid(z)), wo[k])
            halo[k] = u[tt - POOL_HALO:, :]
            halo_ref[0, k] = u[tt - POOL_HALO:, :]
        o_ref[...] = o

    sd = jax.ShapeDtypeStruct
    small = pl.BlockSpec((4, 1, gw), lambda i: (0, 0, 0))
    return pl.pallas_call(
        body, name=name, grid=(nt,),
        in_specs=[pl.BlockSpec((tt, d), lambda i: (i, 0)), pl.BlockSpec((1, d), lambda i: (0, 0)), small, small, ANY, ANY, ANY],
        out_specs=[pl.BlockSpec((tt, d), lambda i: (i, 0)), pl.BlockSpec((1, 4, POOL_HALO, gw), lambda i: (i, 0, 0, 0)),
                   pl.BlockSpec((2, tt, 4 * gw), lambda i: (0, i, 0))],
        out_shape=(sd((lp, d), F32), sd((nt, 4, POOL_HALO, gw), F32), sd((2, lp, 4 * gw), BF16)),
        scratch_shapes=[pltpu.VMEM(w_in.shape, BF16), pltpu.VMEM(w_grp.shape, BF16), pltpu.VMEM(w_out.shape, BF16),
                        pltpu.VMEM((4, POOL_HALO, gw), F32)],
        compiler_params=_params(),
    )(h, g, b_grp, scale, w_in, w_grp, w_out)


def pool_bwd(h, dh, halos, acts, g, w_in, w_grp, b_grp, scale, w_out, first_pos, name):
    lp, d = h.shape
    tt = TOKEN_TILE
    nt = lp // tt
    gw = w_grp.shape[1]

    def body(h_ref, dh_ref, halo_ref, acts_ref, g_ref, bg_ref, sc_ref, w_hbm, wg_hbm, wo_hbm,
             dho_ref, n_ref, dp_ref, dwo_hbm, dwg_hbm, dbg_hbm, dsc_hbm, dg_hbm,
             w, wg, wo, nxt, dwo, dwg, dbg, dsc, dg):
        i = pl.program_id(0)
        tile = nt - 1 - i

        @pl.when(i == 0)
        def _():
            pltpu.sync_copy(w_hbm, w)
            pltpu.sync_copy(wg_hbm, wg)
            pltpu.sync_copy(wo_hbm, wo)
            for ref in (nxt, dwo, dwg, dbg, dsc, dg):
                ref[...] = jnp.zeros_like(ref)

        gv = g_ref[...]
        nf, hh, rr = _rms_fwd(h_ref[...], gv)
        n_ref[...] = nf.T.astype(BF16)
        dhv = dh_ref[...]
        has_prev = (i < nt - 1).astype(F32)
        dn = jnp.zeros((tt, d), F32)
        for k in range(4):
            cols = slice(k * gw, (k + 1) * gw)
            u, z = acts_ref[0, :, cols].astype(F32), acts_ref[1, :, cols].astype(F32)
            mixed, outs, yp = _pool_mix(u, wg, bg_ref, sc_ref, halo_ref[0, k] * has_prev, k, tile, tt, first_pos)
            sz, dsz = _silu_and_grad(z)
            dy = _dot_nt(dhv, wo[k])
            dwo[k] += _dot_tn(yp * sz, dhv)
            dyp = dy * sz
            dz = dy * yp * dsz
            dsc[k] += jnp.sum(dyp * outs, axis=0, keepdims=True)
            douts = dyp * sc_ref[k]
            dbg[k] += jnp.sum(douts, axis=0, keepdims=True)
            dwg[k] += _dot_tn(mixed, douts)
            dmixed = _dot_nt(douts, wg[k])
            dm = dmixed * _pool_inv_count(tile, tt, first_pos, POOL_WINDOWS[k], gw)
            ext = jnp.concatenate([dm, nxt[k]], axis=0)
            du = _window_sums_fwd(ext)[k][:tt, :] - dmixed
            nxt[k] = dm[:POOL_HALO, :]
            dp_ref[0, :, cols] = du.astype(BF16)
            dp_ref[1, :, cols] = dz.astype(BF16)
            dn = dn + _dot_nt(du, w[k]) + _dot_nt(dz, w[4 + k])
        dg[...] += jnp.sum(dn * hh, axis=0, keepdims=True)
        dho_ref[...] = dhv + _rms_bwd(dn, hh, rr, gv)

        @pl.when(i == nt - 1)
        def _():
            pltpu.sync_copy(dwo, dwo_hbm)
            pltpu.sync_copy(dwg, dwg_hbm)
            pltpu.sync_copy(dbg, dbg_hbm)
            pltpu.sync_copy(dsc, dsc_hbm)
            pltpu.sync_copy(dg, dg_hbm)

    rev = lambda i: (nt - 1 - i, 0)
    sd = jax.ShapeDtypeStruct
    small = pl.BlockSpec((4, 1, gw), lambda i: (0, 0, 0))
    return pl.pallas_call(
        body, name=name, grid=(nt,),
        in_specs=[pl.BlockSpec((tt, d), rev), pl.BlockSpec((tt, d), rev),
                  pl.BlockSpec((1, 4, POOL_HALO, gw), lambda i: (jnp.maximum(nt - 2 - i, 0), 0, 0, 0)),
                  pl.BlockSpec((2, tt, 4 * gw), lambda i: (0, nt - 1 - i, 0)),
                  pl.BlockSpec((1, d), lambda i: (0, 0)), small, small, ANY, ANY, ANY],
        out_specs=[pl.BlockSpec((tt, d), rev), pl.BlockSpec((d, tt), lambda i: (0, nt - 1 - i)),
                   pl.BlockSpec((2, tt, 4 * gw), lambda i: (0, nt - 1 - i, 0)), ANY, ANY, ANY, ANY, ANY],
        out_shape=(sd((lp, d), F32), sd((d, lp), BF16), sd((2, lp, 4 * gw), BF16),
                   sd(w_out.shape, F32), sd(w_grp.shape, F32), sd((4, 1, gw), F32), sd((4, 1, gw), F32), sd((1, d), F32)),
        scratch_shapes=[pltpu.VMEM(w_in.shape, BF16), pltpu.VMEM(w_grp.shape, BF16), pltpu.VMEM(w_out.shape, BF16),
                        pltpu.VMEM((4, POOL_HALO, gw), F32), pltpu.VMEM(w_out.shape, F32), pltpu.VMEM(w_grp.shape, F32),
                        pltpu.VMEM((4, 1, gw), F32), pltpu.VMEM((4, 1, gw), F32), pltpu.VMEM((1, d), F32)],
        compiler_params=_params(),
    )(h, dh, halos, acts, g, b_grp, scale, w_in, w_grp, w_out)


def loss_head(h, target, g, pad_tiles, name):
    lp, d = h.shape
    tt = TOKEN_TILE
    nt = lp // tt

    def body(h_ref, t_ref, g_ref, dh_ref, dg_ref, loss_ref, acc):
        i = pl.program_id(0)

        @pl.when(i == 0)
        def _():
            acc[...] = jnp.zeros_like(acc)
            dg_ref[...] = jnp.zeros_like(dg_ref)

        @pl.when(i < pad_tiles)
        def _():
            dh_ref[...] = jnp.zeros_like(dh_ref)

        @pl.when(i >= pad_tiles)
        def _():
            gv = g_ref[...]
            n, hh, rr = _rms_fwd(h_ref[...], gv)
            err = n - t_ref[...]
            acc[...] += 0.5 * jnp.sum(jnp.mean(err * err, axis=-1, keepdims=True), axis=0, keepdims=True)
            dn = err * (1.0 / d)
            dg_ref[...] += jnp.sum(dn * hh, axis=0, keepdims=True)
            dh_ref[...] = _rms_bwd(dn, hh, rr, gv)

        loss_ref[...] = jnp.broadcast_to(acc[...], loss_ref.shape)

    sd = jax.ShapeDtypeStruct
    return pl.pallas_call(
        body, name=name, grid=(nt,),
        in_specs=[pl.BlockSpec((tt, d), lambda i: (i, 0)), pl.BlockSpec((tt, d), lambda i: (jnp.maximum(i - pad_tiles, 0), 0)),
                  pl.BlockSpec((1, d), lambda i: (0, 0))],
        out_specs=[pl.BlockSpec((tt, d), lambda i: (i, 0)), pl.BlockSpec((1, d), lambda i: (0, 0)),
                   pl.BlockSpec((8, 128), lambda i: (0, 0))],
        out_shape=(sd((lp, d), F32), sd((1, d), F32), sd((8, 128), F32)),
        scratch_shapes=[pltpu.VMEM((1, 1), F32)],
        compiler_params=_params(),
    )(h, target, g)


def exchange(arrs, gather, name):
    n = len(arrs)

    def body(*refs):
        ins, outs = refs[:n], refs[n:2 * n]
        send_sems, recv_sems, own_sems = refs[2 * n:]
        x, y, c = lax.axis_index("x"), lax.axis_index("y"), lax.axis_index("c")
        me = 4 * x + 2 * y + c
        own = []
        for a in range(n):
            cp = pltpu.make_async_copy(ins[a] if gather else ins[a].at[me], outs[a].at[me], own_sems.at[a])
            cp.start()
            own.append(cp)
        sent = []
        for k in range(1, N_DEV):
            px = 1 - x if k & 4 else x
            py = 1 - y if k & 2 else y
            pc = 1 - c if k & 1 else c
            peer = 4 * px + 2 * py + pc
            for a in range(n):
                cp = pltpu.make_async_remote_copy(
                    src_ref=ins[a] if gather else ins[a].at[peer], dst_ref=outs[a].at[me],
                    send_sem=send_sems.at[a, k - 1], recv_sem=recv_sems.at[a, k - 1],
                    device_id=(px, py, pc), device_id_type=pl.DeviceIdType.MESH)
                cp.start()
                sent.append((cp, a, k, peer, (px, py, pc)))
        for cp, a, k, peer, pid in sent:
            cp.wait_send()
            pltpu.make_async_remote_copy(
                src_ref=ins[a] if gather else ins[a].at[peer], dst_ref=outs[a].at[peer],
                send_sem=send_sems.at[a, k - 1], recv_sem=recv_sems.at[a, k - 1],
                device_id=pid, device_id_type=pl.DeviceIdType.MESH).wait_recv()
        for cp in own:
            cp.wait()

    hbm = pl.BlockSpec(memory_space=pltpu.HBM)
    out_shape = tuple(jax.ShapeDtypeStruct(((N_DEV,) + a.shape) if gather else a.shape, a.dtype) for a in arrs)
    return pl.pallas_call(
        body, name=name, in_specs=[hbm] * n, out_specs=[hbm] * n, out_shape=out_shape,
        scratch_shapes=[pltpu.SemaphoreType.DMA((n, N_DEV - 1)), pltpu.SemaphoreType.DMA((n, N_DEV - 1)),
                        pltpu.SemaphoreType.DMA((n,))],
    )(*[pltpu.with_memory_space_constraint(a, pltpu.HBM) for a in arrs])


def _peers(x, y, c):
    out = []
    for k in range(1, N_DEV):
        px = 1 - x if k & 4 else x
        py = 1 - y if k & 2 else y
        pc = 1 - c if k & 1 else c
        out.append((k, (px, py, pc), 4 * px + 2 * py + pc))
    return out


def exchange_start(arrs, gather, after, name):
    n = len(arrs)
    me = 4 * lax.axis_index("x") + 2 * lax.axis_index("y") + lax.axis_index("c")
    lands = []
    for a in arrs:
        own = a[None] if gather else lax.dynamic_index_in_dim(a, me, 0, keepdims=True)
        lands.append(lax.dynamic_update_index_in_dim(lax.empty(((N_DEV,) + a.shape) if gather else a.shape, a.dtype), own, me, 0))

    def body(*refs):
        ins, land = refs[:n], refs[n:2 * n]
        send_sems, recv_sems, token = refs[2 * n + 1], refs[2 * n + 2], refs[4 * n + 3]
        x, y, c = lax.axis_index("x"), lax.axis_index("y"), lax.axis_index("c")
        me = 4 * x + 2 * y + c
        for k, pid, peer in _peers(x, y, c):
            for a in range(n):
                pltpu.make_async_remote_copy(
                    src_ref=ins[a] if gather else ins[a].at[peer], dst_ref=land[a].at[me],
                    send_sem=send_sems.at[a * (N_DEV - 1) + k - 1], recv_sem=recv_sems.at[a * (N_DEV - 1) + k - 1],
                    device_id=pid, device_id_type=pl.DeviceIdType.MESH).start()
        token[...] = jnp.zeros_like(token)

    hbm = pl.BlockSpec(memory_space=pltpu.HBM)
    sem = pl.BlockSpec(memory_space=pltpu.SEMAPHORE)
    sems = pltpu.SemaphoreType.DMA((n * (N_DEV - 1),))
    res = pl.pallas_call(
        body, name=name, in_specs=[hbm] * (2 * n) + [ANY],
        out_specs=[sem, sem] + [hbm] * (2 * n) + [pl.BlockSpec(memory_space=pltpu.VMEM)],
        out_shape=[sems, sems] + [pltpu.HBM(a.shape, a.dtype) for a in arrs] + [pltpu.HBM(l.shape, l.dtype) for l in lands]
        + [jax.ShapeDtypeStruct((8, 128), F32)],
        input_output_aliases={a: 2 + a for a in range(2 * n)},
        compiler_params=pltpu.CompilerParams(has_side_effects=pltpu.SideEffectType.DATAFLOW_SIDE_EFFECTING),
    )(*[pltpu.with_memory_space_constraint(a, pltpu.HBM) for a in list(arrs) + lands], after)
    return res[0], res[1], res[2:2 + n], res[2 + n:2 + 2 * n], res[-1]


def exchange_wait(started, gather, after, name):
    send_sems, recv_sems, srcs, lands, _ = started
    n = len(srcs)
    after = list(after) if isinstance(after, (list, tuple)) else [after]

    def body(*refs):
        ins, land = refs[:n], refs[n:2 * n]
        send_sems, recv_sems = refs[2 * n], refs[2 * n + 1]
        x, y, c = lax.axis_index("x"), lax.axis_index("y"), lax.axis_index("c")
        for k, pid, peer in _peers(x, y, c):
            for a in range(n):
                cp = pltpu.make_async_remote_copy(
                    src_ref=ins[a] if gather else ins[a].at[peer], dst_ref=land[a].at[peer],
                    send_sem=send_sems.at[a * (N_DEV - 1) + k - 1], recv_sem=recv_sems.at[a * (N_DEV - 1) + k - 1],
                    device_id=pid, device_id_type=pl.DeviceIdType.MESH)
                cp.wait_send()
                cp.wait_recv()

    hbm = pl.BlockSpec(memory_space=pltpu.HBM)
    sem = pl.BlockSpec(memory_space=pltpu.SEMAPHORE)
    res = pl.pallas_call(
        body, name=name, in_specs=[hbm] * (2 * n) + [sem, sem] + [ANY] * len(after),
        out_specs=[hbm] * (2 * n),
        out_shape=[pltpu.HBM(a.shape, a.dtype) for a in list(srcs) + list(lands)],
        input_output_aliases={a: a for a in range(2 * n)},
        compiler_params=pltpu.CompilerParams(has_side_effects=pltpu.SideEffectType.DATAFLOW_SIDE_EFFECTING),
    )(*srcs, *lands, send_sems, recv_sems, *after)
    return res[n:]


def _adamw(w, g, m, v):
    m = ADAM_B1 * m + (1.0 - ADAM_B1) * g
    v = ADAM_B2 * v + (1.0 - ADAM_B2) * (g * g)
    m_hat = m / (1.0 - ADAM_B1 ** ADAM_STEP)
    v_hat = v / (1.0 - ADAM_B2 ** ADAM_STEP)
    return -ADAM_LR * (m_hat / (jnp.sqrt(v_hat) + ADAM_EPS) + ADAM_WD * w), m, v


def _update_tile_rows(rows, cols):
    if rows * cols <= UPDATE_TILE_ELEMS:
        return rows
    return max(t for t in range(8, UPDATE_TILE_ELEMS // cols + 1, 8) if rows % t == 0)


def _sum_in_order(p_ref):
    g = p_ref[0].astype(F32)
    for j in range(1, p_ref.shape[0]):
        g = g + p_ref[j].astype(F32)
    return g


def sum_parts(parts, name):
    nparts, rows, cols = parts.shape
    tr = _update_tile_rows(rows, cols)

    def body(p_ref, g_ref):
        g_ref[...] = _sum_in_order(p_ref)

    return pl.pallas_call(
        body, name=name, grid=(rows // tr,),
        in_specs=[pl.BlockSpec((nparts, tr, cols), lambda i: (0, i, 0))],
        out_specs=pl.BlockSpec((tr, cols), lambda i: (i, 0)), out_shape=jax.ShapeDtypeStruct((rows, cols), F32),
        compiler_params=_params(),
    )(parts)


def sum_adamw(parts, w, m, v, name):
    rows, cols = w.shape
    nparts = parts.shape[0]
    tr = _update_tile_rows(rows, cols)

    def body(p_ref, w_ref, m_ref, v_ref, g_ref, d_ref, nm_ref, nv_ref):
        g = _sum_in_order(p_ref)
        delta, nm, nv = _adamw(w_ref[...], g, m_ref[...], v_ref[...])
        g_ref[...] = g
        d_ref[...] = delta
        nm_ref[...] = nm
        nv_ref[...] = nv

    blk = pl.BlockSpec((tr, cols), lambda i: (i, 0))
    sd = jax.ShapeDtypeStruct((rows, cols), F32)
    return pl.pallas_call(
        body, name=name, grid=(rows // tr,),
        in_specs=[pl.BlockSpec((nparts, tr, cols), lambda i: (0, i, 0)), blk, blk, blk],
        out_specs=[blk] * 4, out_shape=(sd,) * 4,
        compiler_params=_params(),
    )(parts, w, m, v)


def update_packed(g, w, m, v, pieces, name):
    def body(g_ref, w_ref, m_ref, v_ref, *outs):
        gv = g_ref[...]
        res = (gv,) + _adamw(w_ref[...], gv, m_ref[...], v_ref[...])
        for k in range(4):
            outs[k][...] = res[k]
        for p, (row, rows, lanes) in enumerate(pieces):
            for k in range(4):
                outs[4 + 4 * p + k][...] = res[k][row:row + rows, :lanes]

    sd = jax.ShapeDtypeStruct
    shapes = [sd(w.shape, F32)] * 4 + [sd((rows, lanes), F32) for _, rows, lanes in pieces for _ in range(4)]
    return pl.pallas_call(body, name=name, out_shape=shapes,
                          compiler_params=pltpu.CompilerParams(vmem_limit_bytes=VMEM_LIMIT))(g, w, m, v)


S5_NAMES = ("w_in", "lam_re", "lam_im", "log_dt", "b_re", "b_im", "c_re", "c_im", "d_skip", "w_glu", "b_glu", "w_out")
CONV_NAMES = ("w_in", "conv_w", "conv_b", "w_out")
POOL_NAMES = ("w_in", "w_grp", "b_grp", "scale", "w_out")
LAYER_KINDS = ("s5", "conv", "pool", "s5")
LAYER_NAMES = {"s5": S5_NAMES, "conv": CONV_NAMES, "pool": POOL_NAMES}
SHARDED = {"s5": ("w_in", "w_glu", "w_out"), "conv": ("w_in", "conv_w", "w_out"), "pool": ("w_in", "w_grp", "b_grp", "w_out")}
GATHER_F32 = ("conv_w", "b_grp")


def weight_names():
    names = ["meta_tokens"]
    for i, kind in enumerate(LAYER_KINDS):
        names.append("norm%d_g" % i)
        names += ["l%d_%s" % (i, n) for n in LAYER_NAMES[kind]]
    names.append("final_g")
    return names


def sharded_names():
    return ["meta_tokens"] + ["l%d_%s" % (i, n) for i, kind in enumerate(LAYER_KINDS) for n in SHARDED[kind]]


def _block_diag_in(bb_t, gc):
    i, g, p = bb_t.shape
    t = bb_t.astype(BF16).reshape(i, 4, gc, p)
    return jnp.einsum("icjp,jk->cjikp", t, jnp.eye(gc, dtype=BF16)).reshape(4, gc * i, gc * p)


def _block_diag_in_grad(blocks):
    _, gc, i, p = blocks.shape
    return jnp.transpose(blocks, (2, 0, 1, 3)).reshape(i, 4 * gc, p)


def _block_diag_out(cc, gc):
    g, i, p = cc.shape
    return jnp.einsum("cjip,jk->cjpki", cc.astype(BF16).reshape(4, gc, i, p), jnp.eye(gc, dtype=BF16)).reshape(4, gc * p, gc * i)


def _block_diag_out_grad(blocks):
    _, gc, i, p = blocks.shape
    return blocks.reshape(4 * gc, i, p)


def _to_owner_blocks(a, axis):
    shape = a.shape[:axis] + (N_DEV, a.shape[axis] // N_DEV) + a.shape[axis + 1:]
    return jnp.moveaxis(a.reshape(shape), axis, 0)


def _from_owner_blocks(a, axis):
    a = jnp.moveaxis(a, 0, axis)
    return a.reshape(a.shape[:axis] + (a.shape[axis] * a.shape[axis + 1],) + a.shape[axis + 2:])


def _step(x, target, weights, moments_m, moments_v):
    seq, d = x.shape[1], x.shape[2]
    n_meta = weights["meta_tokens"].shape[0]
    tt = TOKEN_TILE
    pad_tiles = -(-n_meta // tt)
    p0 = pad_tiles * tt
    lp = p0 + seq
    first_pos = p0 - n_meta
    gc = d // 4 // S5_GROUP
    cw = d // 4

    big_names = [n for n in sharded_names() if n != "meta_tokens" and n.split("_", 1)[1] not in GATHER_F32]
    small_names = [n for n in sharded_names() if n not in big_names]
    layer_big = [[n for n in big_names if n.startswith("l%d_" % i)] for i in range(len(LAYER_KINDS))]
    layer_big[0] = small_names + layer_big[0]
    gather_started = []
    after = jnp.zeros((8, 128), F32)
    for i, names in enumerate(layer_big):
        gather_started.append(exchange_start([weights[n] if n in small_names else weights[n].astype(BF16) for n in names], True,
                                             after, "gather_start_l%d" % i))
        after = gather_started[-1][4]

    def vec(name):
        return weights[name].reshape(1, -1)

    s5_prep = {}
    for i, kind in enumerate(LAYER_KINDS):
        if kind == "s5":
            p = "l%d_" % i
            lr, li = weights[p + "lam_re"], weights[p + "lam_im"] + after[0, 0]
            ldt = weights[p + "log_dt"].reshape(-1, 1)
            br_t = jnp.transpose(weights[p + "b_re"], (2, 0, 1))
            bi_t = jnp.transpose(weights[p + "b_im"], (2, 0, 1))
            ar, ai, bbr, bbi = s5_disc_fwd(lr, li, ldt, br_t, bi_t, p + "disc_fwd")
            s5_prep[i] = dict(
                disc=(lr, li, ldt, br_t, bi_t), ar=ar.reshape(4, -1, 128), ai=ai.reshape(4, -1, 128),
                bdre=_block_diag_in(bbr, gc), bdim=_block_diag_in(bbi, gc),
                cdre=_block_diag_out(weights[p + "c_re"], gc), cdim=_block_diag_out(-weights[p + "c_im"], gc),
                d_skip=weights[p + "d_skip"].reshape(4, 1, cw), b_glu=vec(p + "b_glu"))
    h = jnp.concatenate([jnp.zeros((p0, d), F32), x[0] + after[0, 0]], axis=0)

    prepared = [h] + [s5_prep[i][k] for i in s5_prep for k in ("bdre", "bdim", "cdre", "cdim")]
    gathered = dict(zip(layer_big[0], exchange_wait(gather_started[0], True, prepared, "gather_wait_l0")))
    h = lax.dynamic_update_slice(h, _from_owner_blocks(gathered["meta_tokens"], 1), (first_pos, 0))

    full = {}

    def layer_weights(i, kind, after):
        p = "l%d_" % i
        if i > 0:
            gathered.update(zip(layer_big[i], exchange_wait(gather_started[i], True, after, "gather_wait_l%d" % i)))
        w_in = gathered[p + "w_in"]
        if kind == "s5":
            full[i] = dict(s5_prep[i], w_in=w_in, w_glu=gathered[p + "w_glu"].reshape(4, cw, d),
                           w_out=gathered[p + "w_out"].reshape(4, cw, d))
        elif kind == "conv":
            ce = w_in.shape[2]
            nch = 2
            conv_w = _from_owner_blocks(gathered[p + "conv_w"], 1)
            full[i] = dict(
                w_in=w_in, conv_w=jnp.transpose(conv_w.reshape(CONV_K, nch, ce), (1, 0, 2)),
                conv_b=weights[p + "conv_b"].reshape(nch, 1, ce), w_out=gathered[p + "w_out"].reshape(nch, ce, d))
        else:
            gw = w_in.shape[2]
            full[i] = dict(
                w_in=w_in, w_grp=_from_owner_blocks(gathered[p + "w_grp"], 1),
                b_grp=_from_owner_blocks(gathered[p + "b_grp"], 1).reshape(4, 1, gw),
                scale=weights[p + "scale"].reshape(4, 1, gw), w_out=gathered[p + "w_out"].reshape(4, gw, d))
        return full[i]

    saved = {}
    for i, kind in enumerate(LAYER_KINDS):
        p, f, g = "l%d_" % i, layer_weights(i, kind, h), vec("norm%d_g" % i)
        if kind == "s5":
            u, z, xs = s5_fwd1(h, g, f["w_in"], f["bdre"], f["bdim"], p + "fwd_in")
            s = s5_scan_fwd(xs, f["ar"], f["ai"], p + "scan_fwd")
            h_in = h
            h, y, q = s5_fwd3(s, u, z, h, f["cdre"], f["cdim"], f["w_glu"], f["w_out"], f["d_skip"], f["b_glu"], p + "fwd_out")
            saved[i] = (h_in, u, z, s, y, q)
        elif kind == "conv":
            h_new, halos, acts = conv_fwd(h, g, f["w_in"], f["conv_w"], f["conv_b"], f["w_out"], p + "fwd")
            saved[i] = (h, halos, acts)
            h = h_new
        else:
            h_new, halos, acts = pool_fwd(h, g, f["w_in"], f["w_grp"], f["b_grp"], f["scale"], f["w_out"], first_pos, p + "fwd")
            saved[i] = (h, halos, acts)
            h = h_new

    dh, dg_final, loss_tile = loss_head(h, target[0], vec("final_g"), pad_tiles, "loss_head")
    loss = lax.psum(loss_tile[0, 0], ("x", "y", "c"))

    grads = {"final_g": dg_final}
    names = weight_names()
    sh_names = sharded_names()
    rep_names = [n for n in names if n not in sh_names]

    def owner_blocks(a):
        return a.reshape(N_DEV, -1, a.shape[-1]).astype(BF16)

    def as2d(a):
        return a.reshape(-1, a.shape[-1])

    def pack(tree):
        flat = [jnp.pad(tree[n].reshape(-1), (0, -tree[n].size % 1024)) for n in rep_names]
        flat = jnp.concatenate(flat)
        return jnp.pad(flat, (0, -flat.size % (PACK_ROWS * 128))).reshape(-1, 128)

    layer_sharded, scatter_started = {}, {}
    ordered = jnp.zeros((), F32)
    for i in reversed(range(len(LAYER_KINDS))):
        kind = LAYER_KINDS[i]
        p, f, g = "l%d_" % i, full[i], vec("norm%d_g" % i) + ordered
        if kind == "s5":
            h_in, u, z, s, y, q = saved[i]
            dy, dp, dwo, dwg, dbg = s5_bwd3a(dh, y, q, z, f["w_glu"], f["w_out"], f["b_glu"] + ordered, p + "bwd_out")
            d_skip = f["d_skip"]
            if i == 0:
                early_names = [p + "w_glu", p + "w_out"]
                scatter_started["early"] = exchange_start([owner_blocks(dwg), owner_blocks(dwo)], False, dy,
                                                          "scatter_start_l0_early")
                d_skip = d_skip + scatter_started["early"][4][0, 0]
            ds, dus, dcre, dcim, dd = s5_bwd3b(dy, s, u, f["cdre"], f["cdim"], d_skip, p + "bwd_read")
            lam, dar, dai = s5_scan_bwd(ds, s, f["ar"], f["ai"], p + "scan_bwd")
            dp, dh, n, dbre, dbim, dg = s5_bwd1(lam, dus, u, dp, h_in, dh, g, f["w_in"], f["bdre"], f["bdim"], p + "bwd_in")
            dw_in = grad_w_in(n, dp, f["w_in"].shape[2], p + "grad_w_in")
            grads.update({p + "w_in": dw_in, p + "w_glu": dwg.reshape(N_DEV, -1, d), p + "w_out": dwo.reshape(N_DEV, -1, d),
                          p + "d_skip": dd, p + "b_glu": dbg})

            def replicated_grads(p=p, f=f, dar=dar, dai=dai, dbre=dbre, dbim=dbim, dcre=dcre, dcim=dcim, token=None):
                lr, li, ldt, br_t, bi_t = f["disc"]
                dlr, dli, dldt, dbr_t, dbi_t = s5_disc_bwd(
                    lr, li, ldt, br_t, bi_t, dar.reshape(lr.shape) + token, dai.reshape(lr.shape),
                    _block_diag_in_grad(dbre), _block_diag_in_grad(dbim), p + "disc_bwd")
                grads.update({
                    p + "lam_re": dlr, p + "lam_im": dli, p + "log_dt": dldt,
                    p + "b_re": jnp.transpose(dbr_t, (1, 2, 0)), p + "b_im": jnp.transpose(dbi_t, (1, 2, 0)),
                    p + "c_re": _block_diag_out_grad(dcre), p + "c_im": -_block_diag_out_grad(dcim)})
        elif kind == "conv":
            replicated_grads = None
            h_in, halos, acts = saved[i]
            dh, n, dp, dwo, dcw, dcb, dg = conv_bwd(h_in, dh, halos, acts, g, f["w_in"], f["conv_w"], f["conv_b"], f["w_out"], p + "bwd")
            dw_in = grad_w_in(n, dp, f["w_in"].shape[2], p + "grad_w_in")
            dconv_w = jnp.transpose(dcw[:, :CONV_K, :], (1, 0, 2)).reshape(CONV_K, -1)
            grads.update({p + "w_in": dw_in, p + "conv_w": _to_owner_blocks(dconv_w, 1), p + "conv_b": dcb,
                          p + "w_out": dwo.reshape(N_DEV, -1, d)})
        else:
            replicated_grads = None
            h_in, halos, acts = saved[i]
            dh, n, dp, dwo, dwgrp, dbgrp, dsc, dg = pool_bwd(h_in, dh, halos, acts, g, f["w_in"], f["w_grp"], f["b_grp"], f["scale"],
                                                             f["w_out"], first_pos, p + "bwd")
            dw_in = grad_w_in(n, dp, f["w_in"].shape[2], p + "grad_w_in")
            grads.update({p + "w_in": dw_in, p + "w_grp": _to_owner_blocks(dwgrp, 1),
                          p + "b_grp": _to_owner_blocks(dbgrp.reshape(4, -1), 1), p + "scale": dsc,
                          p + "w_out": dwo.reshape(N_DEV, -1, d)})
        grads["norm%d_g" % i] = dg
        layer_sharded[i] = ["l%d_%s" % (i, n) for n in SHARDED[kind]]
        if i > 0:
            scatter_started[i] = exchange_start([owner_blocks(grads[n]) for n in layer_sharded[i]], False, dh,
                                                "scatter_start_l%d" % i)
            ordered = scatter_started[i][4][0, 0]
        if replicated_grads is not None:
            replicated_grads(token=ordered)
    grad_x = dh[p0:][None]
    grads["meta_tokens"] = _to_owner_blocks(dh[first_pos:p0], 1)
    last = len(LAYER_KINDS)
    layer_sharded[last] = ["meta_tokens", "replicated"]
    scatter_started[last] = exchange_start([owner_blocks(grads["meta_tokens"]), pack(grads).reshape(N_DEV, -1, 128)], False,
                                           dh, "scatter_start_replicated")
    layer_sharded["early"] = early_names
    layer_sharded[0] = [n for n in layer_sharded[0] if n not in early_names]

    out = {}
    received = {}
    after = [scatter_started[last][4]]
    for i in list(reversed(range(1, last))) + [last, "early", 0]:
        received.update(zip(layer_sharded[i], exchange_wait(scatter_started[i], False, after, "scatter_wait_%s" % i)))
        updated = []
        for n in layer_sharded[i]:
            if n != "replicated":
                res = sum_adamw(received[n], as2d(weights[n]), as2d(moments_m[n]), as2d(moments_v[n]), "update_" + n)
                out[n] = [r.reshape(weights[n].shape) for r in res]
                updated.append(out[n][0])
        after = updated or after
        if i == last:
            g_full = exchange([sum_parts(received["replicated"], "sum_replicated")], True, "gather_small_grads")[0]
            scatter_started[0] = exchange_start([owner_blocks(grads[n]) for n in layer_sharded[0]], False, g_full,
                                                "scatter_start_l0")
            g_full = g_full.reshape(-1, 128) + scatter_started[0][4][0, 0]
            offsets, offset = {}, 0
            for n in rep_names:
                offsets[n] = offset
                offset += weights[n].size + (-weights[n].size % 1024)
            vectors = [n for n in rep_names if weights[n].ndim == 1]
            pieces = [(offsets[n] // 128, max(weights[n].size // 128, 1), min(weights[n].size, 128)) for n in vectors]
            res = update_packed(g_full, pack(weights), pack(moments_m), pack(moments_v), pieces, "update_replicated")
            packed = res[:4]
            for j, n in enumerate(vectors):
                out[n] = [r.reshape(weights[n].shape) for r in res[4 + 4 * j:8 + 4 * j]]
            for n in rep_names:
                if n not in vectors:
                    size = weights[n].size
                    out[n] = [r.reshape(-1)[offsets[n]:offsets[n] + size].reshape(weights[n].shape) for r in packed]
            after = [out[n][k] for n in rep_names for k in range(4)]

    return (loss, grad_x) + tuple(out[n][k] for k in range(4) for n in names)


def kernel(x, *rest):
    names = weight_names()
    nw = len(names)
    weights = dict(zip(names, rest[:nw]))
    target = rest[nw]
    moments_m = dict(zip(names, rest[nw + 1:2 * nw + 1]))
    moments_v = dict(zip(names, rest[2 * nw + 1:3 * nw + 1]))
    return _step(x, target, weights, moments_m, moments_v)
```

```python
import functools
import math

import jax
import jax.numpy as jnp
from jax import lax
from jax.experimental import pallas as pl
from jax.experimental.pallas import tpu as pltpu

F32 = jnp.float32
BF16 = jnp.bfloat16
EPS = 1e-6
N_DEV = 8
TOKEN_TILE = 256
SCAN_CHUNKS = 4
S5_GROUP = 16
S5_STATE = 64
POOL_WINDOWS = (2, 4, 8, 16)
POOL_HALO = 16
CONV_K = 3
CONV_HALO = 8
ADAM_LR = 0.001
ADAM_B1 = 0.9
ADAM_B2 = 0.999
ADAM_EPS = 1e-08
ADAM_WD = 0.01
ADAM_STEP = 10
GELU_C = math.sqrt(2.0 / math.pi)
GELU_A = 0.044715
UPDATE_TILE_ELEMS = 1 << 17
PACK_ROWS = 512
VMEM_LIMIT = 56 << 20
VMEM_LIMIT_LARGE = 62 << 20

ANY = pl.BlockSpec(memory_space=pl.ANY)


def _params(vmem=VMEM_LIMIT, ndim=1):
    return pltpu.CompilerParams(vmem_limit_bytes=vmem, dimension_semantics=("arbitrary",) * ndim)


def _dot(a, b):
    return jnp.dot(a.astype(BF16), b.astype(BF16), preferred_element_type=F32)


def _dot_nt(a, b):
    return lax.dot_general(a.astype(BF16), b.astype(BF16), (((1,), (1,)), ((), ())), preferred_element_type=F32)


def _dot_tn(a, b):
    return lax.dot_general(a.astype(BF16), b.astype(BF16), (((0,), (0,)), ((), ())), preferred_element_type=F32)


def _rms_fwd(h, g):
    r = lax.rsqrt(jnp.mean(h * h, axis=-1, keepdims=True) + EPS)
    hh = h * r
    return hh * g, hh, r


def _rms_bwd(dn, hh, r, g):
    dhh = dn * g
    return r * (dhh - hh * jnp.mean(dhh * hh, axis=-1, keepdims=True))


def _sigmoid(x):
    return 1.0 / (1.0 + jnp.exp(-x))


def _silu_and_grad(z):
    s = _sigmoid(z)
    return z * s, s * (1.0 + z * (1.0 - s))


def _gelu(y):
    t = jnp.tanh(GELU_C * (y + GELU_A * y * y * y))
    return 0.5 * y * (1.0 + t), t


def _gelu_grad(y, t):
    return 0.5 * (1.0 + t) + 0.5 * y * (1.0 - t * t) * GELU_C * (1.0 + 3.0 * GELU_A * y * y)


def _rows(shape):
    return lax.broadcasted_iota(jnp.int32, shape, 0)


def _shift_down(x, k, halo):
    y = pltpu.roll(x, k, 0)
    rows = _rows(x.shape)
    for j in range(k):
        y = jnp.where(rows == j, halo[halo.shape[0] - k + j:halo.shape[0] - k + j + 1, :], y)
    return y


def _shift_up(x, k, halo):
    n = x.shape[0]
    y = pltpu.roll(x, n - k, 0)
    rows = _rows(x.shape)
    for j in range(k):
        y = jnp.where(rows == n - k + j, halo[j:j + 1, :], y)
    return y


def _window_sums_back(ext):
    out = []
    s = ext
    for k in (1, 2, 4, 8):
        s = s + pltpu.roll(s, k, 0)
        out.append(s)
    return out


def _window_sums_fwd(ext):
    n = ext.shape[0]
    out = []
    s = ext
    for k in (1, 2, 4, 8):
        s = s + pltpu.roll(s, n - k, 0)
        out.append(s)
    return out


def _pool_inv_count(tile, tt, first_pos, w, width):
    pos = _rows((tt, width)) + (tile * tt - first_pos + 1)
    return 1.0 / jnp.clip(pos, 1, w).astype(F32)


def _slab_spec(lp, tt, sw):
    nj = sw // 128
    return pl.BlockSpec((4 * tt * nj, 128), lambda i: (i, 0)), (lp * 4 * nj, 128)


def _pack_pair(re, im):
    def rounded(v):
        return lax.bitcast_convert_type(v, jnp.int32) + 0x8000
    return lax.bitcast_convert_type((rounded(re) & -65536) | lax.shift_right_logical(rounded(im), 16), F32)


def _unpack_pair(w):
    b = lax.bitcast_convert_type(w, jnp.int32)
    return lax.bitcast_convert_type(b & -65536, F32), lax.bitcast_convert_type(lax.shift_left(b, 16), F32)


def _slab_load(ref, c):
    nj = ref.shape[0] // (4 * TOKEN_TILE)
    first = c * TOKEN_TILE * nj
    return _unpack_pair(jnp.concatenate([ref[pl.ds(first + j, TOKEN_TILE, stride=nj), :] for j in range(nj)], axis=1))


def _slab_store(ref, c, re, im):
    nj = ref.shape[0] // (4 * TOKEN_TILE)
    first = c * TOKEN_TILE * nj
    val = _pack_pair(re, im)
    for j in range(nj):
        ref[pl.ds(first + j, TOKEN_TILE, stride=nj), :] = val[:, j * 128:(j + 1) * 128]


def _s5_disc_math(lr, li, ldt, br, bi):
    dt = jnp.exp(ldt)
    mag = jnp.exp(lr * dt)
    ar = mag * jnp.cos(li * dt)
    ai = mag * jnp.sin(li * dt)
    den = lr * lr + li * li
    kr = ((ar - 1.0) * lr + ai * li) / den
    ki = (ai * lr - (ar - 1.0) * li) / den
    bbr = kr[None] * br - ki[None] * bi
    bbi = kr[None] * bi + ki[None] * br
    return ar, ai, bbr, bbi


def s5_disc_fwd(lr, li, ldt, br_t, bi_t, name):
    def body(lr_ref, li_ref, ldt_ref, br_ref, bi_ref, ar_ref, ai_ref, bbr_ref, bbi_ref):
        ar, ai, bbr, bbi = _s5_disc_math(lr_ref[...], li_ref[...], ldt_ref[...], br_ref[...], bi_ref[...])
        ar_ref[...] = ar
        ai_ref[...] = ai
        bbr_ref[...] = bbr
        bbi_ref[...] = bbi

    sd = jax.ShapeDtypeStruct
    return pl.pallas_call(
        body, name=name,
        out_shape=(sd(lr.shape, F32), sd(lr.shape, F32), sd(br_t.shape, F32), sd(br_t.shape, F32)),
    )(lr, li, ldt, br_t, bi_t)


def s5_disc_bwd(lr, li, ldt, br_t, bi_t, dar, dai, dbbr, dbbi, name):
    def body(lr_ref, li_ref, ldt_ref, br_ref, bi_ref, dar_ref, dai_ref, dbbr_ref, dbbi_ref,
             dlr_ref, dli_ref, dldt_ref, dbr_ref, dbi_ref):
        _, vjp = jax.vjp(_s5_disc_math, lr_ref[...], li_ref[...], ldt_ref[...], br_ref[...], bi_ref[...])
        dlr, dli, dldt, dbr, dbi = vjp((dar_ref[...], dai_ref[...], dbbr_ref[...], dbbi_ref[...]))
        dlr_ref[...] = dlr
        dli_ref[...] = dli
        dldt_ref[...] = dldt
        dbr_ref[...] = dbr
        dbi_ref[...] = dbi

    sd = jax.ShapeDtypeStruct
    return pl.pallas_call(
        body, name=name,
        out_shape=(sd(lr.shape, F32), sd(lr.shape, F32), sd(ldt.shape, F32), sd(br_t.shape, F32), sd(br_t.shape, F32)),
    )(lr, li, ldt, br_t, bi_t, dar, dai, dbbr, dbbi)


def s5_fwd1(h, g, w_in, bdre, bdim, name):
    lp, d = h.shape
    tt = TOKEN_TILE
    cw, sw = bdre.shape[1], bdre.shape[2]

    def body(h_ref, g_ref, w_hbm, bdre_hbm, bdim_hbm, u_ref, z_ref, x_ref, w, bre, bim):
        @pl.when(pl.program_id(0) == 0)
        def _():
            pltpu.sync_copy(w_hbm, w)
            pltpu.sync_copy(bdre_hbm, bre)
            pltpu.sync_copy(bdim_hbm, bim)

        n = _rms_fwd(h_ref[...], g_ref[...])[0].astype(BF16)
        for c in range(4):
            cols = slice(c * cw, (c + 1) * cw)
            u = jnp.dot(n, w[c], preferred_element_type=F32)
            u_ref[:, cols] = u
            z_ref[:, cols] = jnp.dot(n, w[c + 4], preferred_element_type=F32)
            ub = u.astype(BF16)
            _slab_store(x_ref, c, jnp.dot(ub, bre[c], preferred_element_type=F32), jnp.dot(ub, bim[c], preferred_element_type=F32))

    sd = jax.ShapeDtypeStruct
    slab, slab_shape = _slab_spec(lp, tt, sw)
    row = pl.BlockSpec((tt, d), lambda i: (i, 0))
    return pl.pallas_call(
        body, name=name, grid=(lp // tt,),
        in_specs=[row, pl.BlockSpec((1, d), lambda i: (0, 0)), ANY, ANY, ANY],
        out_specs=[row, row, slab],
        out_shape=(sd((lp, d), F32), sd((lp, d), F32), sd(slab_shape, F32)),
        scratch_shapes=[pltpu.VMEM(w_in.shape, BF16), pltpu.VMEM(bdre.shape, BF16), pltpu.VMEM(bdim.shape, BF16)],
        compiler_params=_params(),
    )(h, g, w_in, bdre, bdim)


def s5_scan_fwd(x, ar, ai, name):
    nj = ar.shape[1]
    tt = TOKEN_TILE
    cpb = SCAN_CHUNKS
    nt = x.shape[0] // (4 * tt * nj)

    def body(x_ref, ar_ref, ai_ref, s_ref, st_r, st_i):
        i, cg = pl.program_id(0), pl.program_id(1)

        @pl.when(i == 0)
        def _():
            for q in range(cpb):
                st_r[cg * cpb + q] = jnp.zeros((nj, 128), F32)
                st_i[cg * cpb + q] = jnp.zeros((nj, 128), F32)

        a_r = [ar_ref[cg * cpb + q] for q in range(cpb)]
        a_i = [ai_ref[cg * cpb + q] for q in range(cpb)]

        def step(t, carry):
            out = []
            for q in range(cpb):
                s_r, s_i = carry[q]
                rows = pl.ds(pl.multiple_of((q * tt + t) * nj, nj), nj)
                x_r, x_i = _unpack_pair(x_ref[rows, :])
                n_r = a_r[q] * s_r - a_i[q] * s_i + x_r
                n_i = a_r[q] * s_i + a_i[q] * s_r + x_i
                s_ref[rows, :] = _pack_pair(n_r, n_i)
                out.append((n_r, n_i))
            return tuple(out)

        init = tuple((st_r[cg * cpb + q], st_i[cg * cpb + q]) for q in range(cpb))
        final = lax.fori_loop(0, tt, step, init, unroll=8)
        for q in range(cpb):
            st_r[cg * cpb + q] = final[q][0]
            st_i[cg * cpb + q] = final[q][1]

    blk = pl.BlockSpec((cpb * tt * nj, 128), lambda i, cg: (i * (4 // cpb) + cg, 0))
    par = pl.BlockSpec((4, nj, 128), lambda i, cg: (0, 0, 0))
    sd = jax.ShapeDtypeStruct
    return pl.pallas_call(
        body, name=name, grid=(nt, 4 // cpb),
        in_specs=[blk, par, par], out_specs=blk,
        out_shape=sd(x.shape, F32),
        scratch_shapes=[pltpu.VMEM((4, nj, 128), F32), pltpu.VMEM((4, nj, 128), F32)],
        compiler_params=_params(ndim=2),
    )(x, ar, ai)


def s5_fwd3(s, u, z, h, cdre, cdim, w_glu, w_out, d_skip, b_glu, name):
    lp, d = h.shape
    tt = TOKEN_TILE
    sw, cw = cdre.shape[1], cdre.shape[2]

    def body(s_ref, u_ref, z_ref, h_ref, d_ref, bg_ref, cre_hbm, cim_hbm, wg_hbm, wo_hbm,
             o_ref, y_ref, q_ref, cre, cim, wg, wo):
        @pl.when(pl.program_id(0) == 0)
        def _():
            pltpu.sync_copy(cre_hbm, cre)
            pltpu.sync_copy(cim_hbm, cim)
            pltpu.sync_copy(wg_hbm, wg)
            pltpu.sync_copy(wo_hbm, wo)

        gys, q = [], None
        for c in range(4):
            cols = slice(c * cw, (c + 1) * cw)
            s_r, s_i = _slab_load(s_ref, c)
            y = _dot(s_r, cre[c]) + _dot(s_i, cim[c]) + d_ref[c] * u_ref[:, cols]
            y_ref[:, cols] = y
            gys.append(_gelu(y)[0])
            part = _dot(gys[c], wg[c])
            q = part if c == 0 else q + part
        q_ref[...] = q
        sig = _sigmoid(q + bg_ref[...])
        zz = z_ref[...]
        sz = zz * _sigmoid(zz)
        o = h_ref[...]
        for k in range(4):
            cols = slice(k * cw, (k + 1) * cw)
            o = o + _dot(gys[k] * sig[:, cols] * sz[:, cols], wo[k])
        o_ref[...] = o

    row = pl.BlockSpec((tt, d), lambda i: (i, 0))
    slab, _ = _slab_spec(lp, tt, sw)
    sd = jax.ShapeDtypeStruct((lp, d), F32)
    return pl.pallas_call(
        body, name=name, grid=(lp // tt,),
        in_specs=[slab, row, row, row, pl.BlockSpec((4, 1, cw), lambda i: (0, 0, 0)), pl.BlockSpec((1, d), lambda i: (0, 0)),
                  ANY, ANY, ANY, ANY],
        out_specs=[row, row, row],
        out_shape=(sd, sd, sd),
        scratch_shapes=[pltpu.VMEM(cdre.shape, BF16), pltpu.VMEM(cdim.shape, BF16), pltpu.VMEM(w_glu.shape, BF16),
                        pltpu.VMEM(w_out.shape, BF16)],
        compiler_params=_params(),
    )(s, u, z, h, d_skip, b_glu, cdre, cdim, w_glu, w_out)


def s5_bwd3a(dh, y, q, z, w_glu, w_out, b_glu, name):
    lp, d = dh.shape
    tt = TOKEN_TILE
    nt = lp // tt
    cw = w_glu.shape[1]

    def body(dh_ref, y_ref, q_ref, z_ref, bg_ref, wg_hbm, wo_hbm, dy_ref, dp_ref, dwo_hbm, dwg_hbm, dbg_hbm,
             wg, wo, dwo, dwg, dbg):
        i = pl.program_id(0)

        @pl.when(i == 0)
        def _():
            pltpu.sync_copy(wg_hbm, wg)
            pltpu.sync_copy(wo_hbm, wo)
            dwo[...] = jnp.zeros_like(dwo)
            dwg[...] = jnp.zeros_like(dwg)
            dbg[...] = jnp.zeros_like(dbg)

        sig = _sigmoid(q_ref[...] + bg_ref[...])
        sz, dsz = _silu_and_grad(z_ref[...])
        dhv = dh_ref[...]
        yv = y_ref[...]
        gy, t = _gelu(yv)
        dq_parts, dgy_parts = [], []
        for k in range(4):
            cols = slice(k * cw, (k + 1) * cw)
            gy_k, sig_k, sz_k = gy[:, cols], sig[:, cols], sz[:, cols]
            y2 = gy_k * sig_k
            dy3 = _dot_nt(dhv, wo[k])
            dwo[k] += _dot_tn(y2 * sz_k, dhv)
            dy2 = dy3 * sz_k
            dp_ref[0, :, cols] = (dy3 * y2 * dsz[:, cols]).astype(BF16)
            dq_parts.append(dy2 * gy_k * sig_k * (1.0 - sig_k))
            dgy_parts.append(dy2 * sig_k)
        dq = jnp.concatenate(dq_parts, axis=1)
        dbg[...] += jnp.sum(dq, axis=0, keepdims=True)
        dgelu = _gelu_grad(yv, t)
        for k in range(4):
            cols = slice(k * cw, (k + 1) * cw)
            dwg[k] += _dot_tn(gy[:, cols], dq)
            dy_ref[:, cols] = (dgy_parts[k] + _dot_nt(dq, wg[k])) * dgelu[:, cols]

        @pl.when(i == nt - 1)
        def _():
            pltpu.sync_copy(dwo, dwo_hbm)
            pltpu.sync_copy(dwg, dwg_hbm)
            pltpu.sync_copy(dbg, dbg_hbm)

    row = pl.BlockSpec((tt, d), lambda i: (i, 0))
    sd = jax.ShapeDtypeStruct
    return pl.pallas_call(
        body, name=name, grid=(nt,),
        in_specs=[row, row, row, row, pl.BlockSpec((1, d), lambda i: (0, 0)), ANY, ANY],
        out_specs=[row, pl.BlockSpec((1, tt, d), lambda i: (1, i, 0)), ANY, ANY, ANY],
        out_shape=(sd((lp, d), F32), sd((2, lp, d), BF16), sd(w_out.shape, F32), sd(w_glu.shape, F32), sd((1, d), F32)),
        scratch_shapes=[pltpu.VMEM(w_glu.shape, BF16), pltpu.VMEM(w_out.shape, BF16),
                        pltpu.VMEM(w_out.shape, F32), pltpu.VMEM(w_glu.shape, F32), pltpu.VMEM((1, d), F32)],
        compiler_params=_params(),
    )(dh, y, q, z, b_glu, w_glu, w_out)


def s5_bwd3b(dy, s, u, cdre, cdim, d_skip, name):
    lp, d = dy.shape
    tt = TOKEN_TILE
    nt = lp // tt
    sw, cw = cdre.shape[1], cdre.shape[2]
    gc = cw // S5_GROUP

    def body(dy_ref, s_ref, u_ref, d_ref, cre_hbm, cim_hbm,
             ds_ref, dus_ref, dcre_ref, dcim_ref, dd_hbm, cre, cim, dcre, dcim, dd):
        i = pl.program_id(0)

        @pl.when(i == 0)
        def _():
            pltpu.sync_copy(cre_hbm, cre)
            pltpu.sync_copy(cim_hbm, cim)
            dcre[...] = jnp.zeros_like(dcre)
            dcim[...] = jnp.zeros_like(dcim)
            dd[...] = jnp.zeros_like(dd)

        for c in range(4):
            chunk = slice(c * cw, (c + 1) * cw)
            dyv = dy_ref[:, chunk]
            dd[c] += jnp.sum(dyv * u_ref[:, chunk], axis=0, keepdims=True)
            dus_ref[:, chunk] = dyv * d_ref[c]
            _slab_store(ds_ref, c, _dot_nt(dyv, cre[c]), _dot_nt(dyv, cim[c]))
            s_r, s_i = _slab_load(s_ref, c)
            dcre[c] += _dot_tn(s_r, dyv)
            dcim[c] += _dot_tn(s_i, dyv)

        @pl.when(i == nt - 1)
        def _():
            for k in range(4):
                for j in range(gc):
                    rows, cols = pl.ds(j * S5_STATE, S5_STATE), pl.ds(j * S5_GROUP, S5_GROUP)
                    dcre_ref[k, j] = dcre[k, rows, cols].T
                    dcim_ref[k, j] = dcim[k, rows, cols].T
            pltpu.sync_copy(dd, dd_hbm)

    sd = jax.ShapeDtypeStruct
    row = pl.BlockSpec((tt, d), lambda i: (i, 0))
    slab, slab_shape = _slab_spec(lp, tt, sw)
    diag = pl.BlockSpec((4, gc, S5_GROUP, S5_STATE), lambda i: (0, 0, 0, 0))
    return pl.pallas_call(
        body, name=name, grid=(nt,),
        in_specs=[row, slab, row, pl.BlockSpec((4, 1, cw), lambda i: (0, 0, 0)), ANY, ANY],
        out_specs=[slab, row, diag, diag, ANY],
        out_shape=(sd(slab_shape, F32), sd((lp, d), F32),
                   sd((4, gc, S5_GROUP, S5_STATE), F32), sd((4, gc, S5_GROUP, S5_STATE), F32), sd((4, 1, cw), F32)),
        scratch_shapes=[pltpu.VMEM(cdre.shape, BF16), pltpu.VMEM(cdim.shape, BF16),
                        pltpu.VMEM(cdre.shape, F32), pltpu.VMEM(cdim.shape, F32), pltpu.VMEM((4, 1, cw), F32)],
        compiler_params=_params(),
    )(dy, s, u, d_skip, cdre, cdim)


def s5_scan_bwd(g, s, ar, ai, name):
    nj = ar.shape[1]
    tt = TOKEN_TILE
    cpb = SCAN_CHUNKS
    nt = g.shape[0] // (4 * tt * nj)

    def body(g_ref, s_ref, ar_ref, ai_ref, lam_ref, dar_ref, dai_ref, st_r, st_i, acc_r, acc_i):
        i, cg = pl.program_id(0), pl.program_id(1)

        @pl.when((i == 0) & (cg == 0))
        def _():
            for ref in (st_r, st_i, acc_r, acc_i):
                ref[...] = jnp.zeros_like(ref)

        a_r = [ar_ref[cg * cpb + q] for q in range(cpb)]
        a_i = [ai_ref[cg * cpb + q] for q in range(cpb)]

        def slab(q, t):
            return pl.ds(pl.multiple_of((q * tt + t) * nj, nj), nj)

        def adjoint(q, t, l_r, l_i):
            rows = slab(q, t)
            g_r, g_i = _unpack_pair(g_ref[rows, :])
            n_r = g_r + a_r[q] * l_r + a_i[q] * l_i
            n_i = g_i + a_r[q] * l_i - a_i[q] * l_r
            lam_ref[rows, :] = _pack_pair(n_r, n_i)
            return n_r, n_i

        def pair(q, t, l_r, l_i, d_r, d_i):
            p_r, p_i = _unpack_pair(s_ref[slab(q, t), :])
            return d_r + l_r * p_r + l_i * p_i, d_i + l_i * p_r - l_r * p_i

        def step(k, carry):
            t = tt - 1 - k
            out = []
            for q in range(cpb):
                l_r, l_i, d_r, d_i = carry[q]
                l_r, l_i = adjoint(q, t, l_r, l_i)
                d_r, d_i = pair(q, t - 1, l_r, l_i, d_r, d_i)
                out.append((l_r, l_i, d_r, d_i))
            return tuple(out)

        init = []
        for q in range(cpb):
            ch = cg * cpb + q
            l_r, l_i = st_r[ch], st_i[ch]
            d_r, d_i = pair(q, tt - 1, l_r, l_i, acc_r[ch], acc_i[ch])
            init.append((l_r, l_i, d_r, d_i))
        final = lax.fori_loop(0, tt - 1, step, tuple(init), unroll=8)
        for q in range(cpb):
            ch = cg * cpb + q
            l_r, l_i, d_r, d_i = final[q]
            l_r, l_i = adjoint(q, 0, l_r, l_i)
            st_r[ch] = l_r
            st_i[ch] = l_i
            acc_r[ch] = d_r
            acc_i[ch] = d_i
            dar_ref[ch] = d_r
            dai_ref[ch] = d_i

    blk = pl.BlockSpec((cpb * tt * nj, 128), lambda i, cg: ((nt - 1 - i) * (4 // cpb) + cg, 0))
    par = pl.BlockSpec((4, nj, 128), lambda i, cg: (0, 0, 0))
    sd = jax.ShapeDtypeStruct
    return pl.pallas_call(
        body, name=name, grid=(nt, 4 // cpb),
        in_specs=[blk, blk, par, par], out_specs=[blk, par, par],
        out_shape=(sd(g.shape, F32), sd((4, nj, 128), F32), sd((4, nj, 128), F32)),
        scratch_shapes=[pltpu.VMEM((4, nj, 128), F32)] * 4,
        compiler_params=_params(ndim=2),
    )(g, s, ar, ai)


def s5_bwd1(lam, dus, u, dp, h, dh, g, w_in, bdre, bdim, name):
    lp, d = h.shape
    tt = TOKEN_TILE
    nt = lp // tt
    cw, sw = bdre.shape[1], bdre.shape[2]
    gc = cw // S5_GROUP

    def body(lam_ref, dus_ref, u_ref, dpz_ref, h_ref, dh_ref, g_ref, w_hbm, bre_hbm, bim_hbm,
             dpu_ref, dho_ref, n_ref, dbre_ref, dbim_ref, dg_hbm, w, bre, bim, dbre, dbim, dg):
        i = pl.program_id(0)

        @pl.when(i == 0)
        def _():
            pltpu.sync_copy(w_hbm, w)
            pltpu.sync_copy(bre_hbm, bre)
            pltpu.sync_copy(bim_hbm, bim)
            dbre[...] = jnp.zeros_like(dbre)
            dbim[...] = jnp.zeros_like(dbim)
            dg[...] = jnp.zeros_like(dg)

        dz = dpz_ref[0]
        dn = None
        for c in range(4):
            chunk = slice(c * cw, (c + 1) * cw)
            (l_r, l_i), uv = _slab_load(lam_ref, c), u_ref[:, chunk]
            du = dus_ref[:, chunk] + _dot_nt(l_r, bre[c]) + _dot_nt(l_i, bim[c])
            dbre[c] += _dot_tn(uv, l_r)
            dbim[c] += _dot_tn(uv, l_i)
            dpu_ref[0, :, chunk] = du.astype(BF16)
            part = _dot_nt(du, w[c]) + _dot_nt(dz[:, chunk], w[4 + c])
            dn = part if c == 0 else dn + part
        gv = g_ref[...]
        n, hh, rr = _rms_fwd(h_ref[...], gv)
        n_ref[...] = n.T.astype(BF16)
        dg[...] += jnp.sum(dn * hh, axis=0, keepdims=True)
        dho_ref[...] = dh_ref[...] + _rms_bwd(dn, hh, rr, gv)

        @pl.when(i == nt - 1)
        def _():
            for k in range(4):
                for j in range(gc):
                    rows, cols = pl.ds(j * S5_GROUP, S5_GROUP), pl.ds(j * S5_STATE, S5_STATE)
                    dbre_ref[k, j] = dbre[k, rows, cols]
                    dbim_ref[k, j] = dbim[k, rows, cols]
            pltpu.sync_copy(dg, dg_hbm)

    sd = jax.ShapeDtypeStruct
    row = pl.BlockSpec((tt, d), lambda i: (i, 0))
    slab, _ = _slab_spec(lp, tt, sw)
    diag = pl.BlockSpec((4, gc, S5_GROUP, S5_STATE), lambda i: (0, 0, 0, 0))
    return pl.pallas_call(
        body, name=name, grid=(nt,),
        in_specs=[slab, row, row, pl.BlockSpec((1, tt, d), lambda i: (1, i, 0)), row, row, pl.BlockSpec((1, d), lambda i: (0, 0)),
                  ANY, ANY, ANY],
        out_specs=[pl.BlockSpec((1, tt, d), lambda i: (0, i, 0)), row, pl.BlockSpec((d, tt), lambda i: (0, i)), diag, diag, ANY],
        out_shape=(sd(dp.shape, BF16), sd((lp, d), F32), sd((d, lp), BF16),
                   sd((4, gc, S5_GROUP, S5_STATE), F32), sd((4, gc, S5_GROUP, S5_STATE), F32), sd((1, d), F32)),
        input_output_aliases={3: 0},
        scratch_shapes=[pltpu.VMEM(w_in.shape, BF16), pltpu.VMEM(bdre.shape, BF16), pltpu.VMEM(bdim.shape, BF16),
                        pltpu.VMEM(bdre.shape, F32), pltpu.VMEM(bdim.shape, F32), pltpu.VMEM((1, d), F32)],
        compiler_params=_params(),
    )(lam, dus, u, dp, h, dh, g, w_in, bdre, bdim)


def grad_w_in(n_t, dp, blk, name):
    d, lp = n_t.shape
    npart, _, width = dp.shape
    per = width // blk

    def body(n_ref, dp_ref, o_ref):
        o_ref[0] = jnp.dot(n_ref[...], dp_ref[0], preferred_element_type=F32).astype(o_ref.dtype)

    return pl.pallas_call(
        body, name=name, grid=(npart * per,),
        in_specs=[pl.BlockSpec((d, lp), lambda j: (0, 0), pipeline_mode=pl.Buffered(1)),
                  pl.BlockSpec((1, lp, blk), lambda j: (j // per, 0, j % per))],
        out_specs=pl.BlockSpec((1, d, blk), lambda j: (j, 0, 0)),
        out_shape=jax.ShapeDtypeStruct((npart * per, d, blk), BF16),
        compiler_params=_params(),
    )(n_t, dp)


def _conv_mix(cg, v, cw_ref, cb_ref, halo, c):
    hc = cg * v
    taps = cw_ref[c]
    conv = taps[2:3, :] * hc + taps[1:2, :] * _shift_down(hc, 1, halo) + taps[0:1, :] * _shift_down(hc, 2, halo) + cb_ref[c]
    return hc, conv


def conv_fwd(h, g, w_in, conv_w, conv_b, w_out, name):
    lp, d = h.shape
    tt = TOKEN_TILE
    nt = lp // tt
    nch, ce = w_out.shape[0], w_out.shape[1]

    def body(h_ref, g_ref, cw_ref, cb_ref, w_hbm, wo_hbm, o_ref, halo_ref, acts_ref, w, wo, halo):
        i = pl.program_id(0)

        @pl.when(i == 0)
        def _():
            pltpu.sync_copy(w_hbm, w)
            pltpu.sync_copy(wo_hbm, wo)
            halo[...] = jnp.zeros_like(halo)

        hv = h_ref[...]
        n = _rms_fwd(hv, g_ref[...])[0].astype(BF16)
        o = hv
        for c in range(nch):
            cols = slice(c * ce, (c + 1) * ce)
            bg, cg, v, z = [jnp.dot(n, w[p * nch + c], preferred_element_type=F32) for p in range(4)]
            for p, val in enumerate((bg, cg, v, z)):
                acts_ref[p, :, cols] = val.astype(BF16)
            hc, conv = _conv_mix(cg, v, cw_ref, cb_ref, halo[c], c)
            o = o + _dot(bg * conv * (z * _sigmoid(z)), wo[c])
            halo[c] = hc[tt - CONV_HALO:, :]
            halo_ref[0, c] = hc[tt - CONV_HALO:, :]
        o_ref[...] = o

    sd = jax.ShapeDtypeStruct
    return pl.pallas_call(
        body, name=name, grid=(nt,),
        in_specs=[pl.BlockSpec((tt, d), lambda i: (i, 0)), pl.BlockSpec((1, d), lambda i: (0, 0)),
                  pl.BlockSpec(conv_w.shape, lambda i: (0, 0, 0)), pl.BlockSpec(conv_b.shape, lambda i: (0, 0, 0)), ANY, ANY],
        out_specs=[pl.BlockSpec((tt, d), lambda i: (i, 0)), pl.BlockSpec((1, nch, CONV_HALO, ce), lambda i: (i, 0, 0, 0)),
                   pl.BlockSpec((4, tt, nch * ce), lambda i: (0, i, 0))],
        out_shape=(sd((lp, d), F32), sd((nt, nch, CONV_HALO, ce), F32), sd((4, lp, nch * ce), BF16)),
        scratch_shapes=[pltpu.VMEM(w_in.shape, BF16), pltpu.VMEM(w_out.shape, BF16), pltpu.VMEM((nch, CONV_HALO, ce), F32)],
        compiler_params=_params(),
    )(h, g, conv_w, conv_b, w_in, w_out)


def conv_bwd(h, dh, halos, acts, g, w_in, conv_w, conv_b, w_out, name):
    lp, d = h.shape
    tt = TOKEN_TILE
    nt = lp // tt
    nch, ce = w_out.shape[0], w_out.shape[1]

    def body(h_ref, dh_ref, halo_ref, acts_ref, g_ref, cw_ref, cb_ref, w_hbm, wo_hbm,
             dho_ref, n_ref, dp_ref, dwo_hbm, dcw_hbm, dcb_hbm, dg_hbm, w, wo, nxt, dwo, dcw, dcb, dg):
        i = pl.program_id(0)

        @pl.when(i == 0)
        def _():
            pltpu.sync_copy(w_hbm, w)
            pltpu.sync_copy(wo_hbm, wo)
            for ref in (nxt, dwo, dcw, dcb, dg):
                ref[...] = jnp.zeros_like(ref)

        gv = g_ref[...]
        nf, hh, rr = _rms_fwd(h_ref[...], gv)
        n_ref[...] = nf.T.astype(BF16)
        dhv = dh_ref[...]
        has_prev = (i < nt - 1).astype(F32)
        dn = jnp.zeros((tt, d), F32)
        for c in range(nch):
            halo = halo_ref[0, c] * has_prev
            cols = slice(c * ce, (c + 1) * ce)
            bg, cg, v, z = [acts_ref[p, :, cols].astype(F32) for p in range(4)]
            hc, conv = _conv_mix(cg, v, cw_ref, cb_ref, halo, c)
            sz, dsz = _silu_and_grad(z)
            y1 = bg * conv
            dy2 = _dot_nt(dhv, wo[c])
            dwo[c] += _dot_tn(y1 * sz, dhv)
            dy1 = dy2 * sz
            dz = dy2 * y1 * dsz
            dbg = dy1 * conv
            dconv = dy1 * bg
            dcb[c] += jnp.sum(dconv, axis=0, keepdims=True)
            up1 = _shift_up(dconv, 1, nxt[c])
            up2 = _shift_up(dconv, 2, nxt[c])
            nxt[c] = dconv[:CONV_HALO, :]
            taps = cw_ref[c]
            dhc = taps[2:3, :] * dconv + taps[1:2, :] * up1 + taps[0:1, :] * up2
            dcw[c, 0:1, :] += jnp.sum(hc * up2, axis=0, keepdims=True)
            dcw[c, 1:2, :] += jnp.sum(hc * up1, axis=0, keepdims=True)
            dcw[c, 2:3, :] += jnp.sum(hc * dconv, axis=0, keepdims=True)
            dcg = dhc * v
            dv = dhc * cg
            for p, val in enumerate((dbg, dcg, dv, dz)):
                dp_ref[p, :, cols] = val.astype(BF16)
                dn = dn + _dot_nt(val, w[p * nch + c])
        dg[...] += jnp.sum(dn * hh, axis=0, keepdims=True)
        dho_ref[...] = dhv + _rms_bwd(dn, hh, rr, gv)

        @pl.when(i == nt - 1)
        def _():
            pltpu.sync_copy(dwo, dwo_hbm)
            pltpu.sync_copy(dcw, dcw_hbm)
            pltpu.sync_copy(dcb, dcb_hbm)
            pltpu.sync_copy(dg, dg_hbm)

    rev = lambda i: (nt - 1 - i, 0)
    sd = jax.ShapeDtypeStruct
    return pl.pallas_call(
        body, name=name, grid=(nt,),
        in_specs=[pl.BlockSpec((tt, d), rev), pl.BlockSpec((tt, d), rev),
                  pl.BlockSpec((1, nch, CONV_HALO, ce), lambda i: (jnp.maximum(nt - 2 - i, 0), 0, 0, 0)),
                  pl.BlockSpec((4, tt, nch * ce), lambda i: (0, nt - 1 - i, 0)),
                  pl.BlockSpec((1, d), lambda i: (0, 0)),
                  pl.BlockSpec(conv_w.shape, lambda i: (0, 0, 0)), pl.BlockSpec(conv_b.shape, lambda i: (0, 0, 0)), ANY, ANY],
        out_specs=[pl.BlockSpec((tt, d), rev), pl.BlockSpec((d, tt), lambda i: (0, nt - 1 - i)),
                   pl.BlockSpec((4, tt, nch * ce), lambda i: (0, nt - 1 - i, 0)), ANY, ANY, ANY, ANY],
        out_shape=(sd((lp, d), F32), sd((d, lp), BF16), sd((4, lp, nch * ce), BF16),
                   sd(w_out.shape, F32), sd((nch, 8, ce), F32), sd((nch, 1, ce), F32), sd((1, d), F32)),
        scratch_shapes=[pltpu.VMEM(w_in.shape, BF16), pltpu.VMEM(w_out.shape, BF16), pltpu.VMEM((nch, CONV_HALO, ce), F32),
                        pltpu.VMEM(w_out.shape, F32), pltpu.VMEM((nch, 8, ce), F32), pltpu.VMEM((nch, 1, ce), F32),
                        pltpu.VMEM((1, d), F32)],
        compiler_params=_params(vmem=VMEM_LIMIT_LARGE),
    )(h, dh, halos, acts, g, conv_w, conv_b, w_in, w_out)


def _pool_mix(u, wg, bg_ref, sc_ref, halo, k, tile, tt, first_pos):
    ext = jnp.concatenate([halo, u], axis=0)
    win = _window_sums_back(ext)[k][POOL_HALO:, :]
    mixed = win * _pool_inv_count(tile, tt, first_pos, POOL_WINDOWS[k], u.shape[1]) - u
    outs = _dot(mixed, wg[k]) + bg_ref[k]
    return mixed, outs, outs * sc_ref[k]


def pool_fwd(h, g, w_in, w_grp, b_grp, scale, w_out, first_pos, name):
    lp, d = h.shape
    tt = TOKEN_TILE
    nt = lp // tt
    gw = w_grp.shape[1]

    def body(h_ref, g_ref, bg_ref, sc_ref, w_hbm, wg_hbm, wo_hbm, o_ref, halo_ref, acts_ref, w, wg, wo, halo):
        i = pl.program_id(0)

        @pl.when(i == 0)
        def _():
            pltpu.sync_copy(w_hbm, w)
            pltpu.sync_copy(wg_hbm, wg)
            pltpu.sync_copy(wo_hbm, wo)
            halo[...] = jnp.zeros_like(halo)

        hv = h_ref[...]
        n = _rms_fwd(hv, g_ref[...])[0].astype(BF16)
        o = hv
        for k in range(4):
            cols = slice(k * gw, (k + 1) * gw)
            u = jnp.dot(n, w[k], preferred_element_type=F32)
            z = jnp.dot(n, w[4 + k], preferred_element_type=F32)
            acts_ref[0, :, cols] = u.astype(BF16)
            acts_ref[1, :, cols] = z.astype(BF16)
            _, _, yp = _pool_mix(u, wg, bg_ref, sc_ref, halo[k], k, i, tt, first_pos)
            o = o + _dot(yp * (z * _sigmoid(z)), wo[k])
            halo[k] = u[tt - POOL_HALO:, :]
            halo_ref[0, k] = u[tt - POOL_HALO:, :]
        o_ref[...] = o

    sd = jax.ShapeDtypeStruct
    small = pl.BlockSpec((4, 1, gw), lambda i: (0, 0, 0))
    return pl.pallas_call(
        body, name=name, grid=(nt,),
        in_specs=[pl.BlockSpec((tt, d), lambda i: (i, 0)), pl.BlockSpec((1, d), lambda i: (0, 0)), small, small, ANY, ANY, ANY],
        out_specs=[pl.BlockSpec((tt, d), lambda i: (i, 0)), pl.BlockSpec((1, 4, POOL_HALO, gw), lambda i: (i, 0, 0, 0)),
                   pl.BlockSpec((2, tt, 4 * gw), lambda i: (0, i, 0))],
        out_shape=(sd((lp, d), F32), sd((nt, 4, POOL_HALO, gw), F32), sd((2, lp, 4 * gw), BF16)),
        scratch_shapes=[pltpu.VMEM(w_in.shape, BF16), pltpu.VMEM(w_grp.shape, BF16), pltpu.VMEM(w_out.shape, BF16),
                        pltpu.VMEM((4, POOL_HALO, gw), F32)],
        compiler_params=_params(),
    )(h, g, b_grp, scale, w_in, w_grp, w_out)


def pool_bwd(h, dh, halos, acts, g, w_in, w_grp, b_grp, scale, w_out, first_pos, name):
    lp, d = h.shape
    tt = TOKEN_TILE
    nt = lp // tt
    gw = w_grp.shape[1]

    def body(h_ref, dh_ref, halo_ref, acts_ref, g_ref, bg_ref, sc_ref, w_hbm, wg_hbm, wo_hbm,
             dho_ref, n_ref, dp_ref, dwo_hbm, dwg_hbm, dbg_hbm, dsc_hbm, dg_hbm,
             w, wg, wo, nxt, dwo, dwg, dbg, dsc, dg):
        i = pl.program_id(0)
        tile = nt - 1 - i

        @pl.when(i == 0)
        def _():
            pltpu.sync_copy(w_hbm, w)
            pltpu.sync_copy(wg_hbm, wg)
            pltpu.sync_copy(wo_hbm, wo)
            for ref in (nxt, dwo, dwg, dbg, dsc, dg):
                ref[...] = jnp.zeros_like(ref)

        gv = g_ref[...]
        nf, hh, rr = _rms_fwd(h_ref[...], gv)
        n_ref[...] = nf.T.astype(BF16)
        dhv = dh_ref[...]
        has_prev = (i < nt - 1).astype(F32)
        dn = jnp.zeros((tt, d), F32)
        for k in range(4):
            cols = slice(k * gw, (k + 1) * gw)
            u, z = acts_ref[0, :, cols].astype(F32), acts_ref[1, :, cols].astype(F32)
            mixed, outs, yp = _pool_mix(u, wg, bg_ref, sc_ref, halo_ref[0, k] * has_prev, k, tile, tt, first_pos)
            sz, dsz = _silu_and_grad(z)
            dy = _dot_nt(dhv, wo[k])
            dwo[k] += _dot_tn(yp * sz, dhv)
            dyp = dy * sz
            dz = dy * yp * dsz
            dsc[k] += jnp.sum(dyp * outs, axis=0, keepdims=True)
            douts = dyp * sc_ref[k]
            dbg[k] += jnp.sum(douts, axis=0, keepdims=True)
            dwg[k] += _dot_tn(mixed, douts)
            dmixed = _dot_nt(douts, wg[k])
            dm = dmixed * _pool_inv_count(tile, tt, first_pos, POOL_WINDOWS[k], gw)
            ext = jnp.concatenate([dm, nxt[k]], axis=0)
            du = _window_sums_fwd(ext)[k][:tt, :] - dmixed
            nxt[k] = dm[:POOL_HALO, :]
            dp_ref[0, :, cols] = du.astype(BF16)
            dp_ref[1, :, cols] = dz.astype(BF16)
            dn = dn + _dot_nt(du, w[k]) + _dot_nt(dz, w[4 + k])
        dg[...] += jnp.sum(dn * hh, axis=0, keepdims=True)
        dho_ref[...] = dhv + _rms_bwd(dn, hh, rr, gv)

        @pl.when(i == nt - 1)
        def _():
            pltpu.sync_copy(dwo, dwo_hbm)
            pltpu.sync_copy(dwg, dwg_hbm)
            pltpu.sync_copy(dbg, dbg_hbm)
            pltpu.sync_copy(dsc, dsc_hbm)
            pltpu.sync_copy(dg, dg_hbm)

    rev = lambda i: (nt - 1 - i, 0)
    sd = jax.ShapeDtypeStruct
    small = pl.BlockSpec((4, 1, gw), lambda i: (0, 0, 0))
    return pl.pallas_call(
        body, name=name, grid=(nt,),
        in_specs=[pl.BlockSpec((tt, d), rev), pl.BlockSpec((tt, d), rev),
                  pl.BlockSpec((1, 4, POOL_HALO, gw), lambda i: (jnp.maximum(nt - 2 - i, 0), 0, 0, 0)),
                  pl.BlockSpec((2, tt, 4 * gw), lambda i: (0, nt - 1 - i, 0)),
                  pl.BlockSpec((1, d), lambda i: (0, 0)), small, small, ANY, ANY, ANY],
        out_specs=[pl.BlockSpec((tt, d), rev), pl.BlockSpec((d, tt), lambda i: (0, nt - 1 - i)),
                   pl.BlockSpec((2, tt, 4 * gw), lambda i: (0, nt - 1 - i, 0)), ANY, ANY, ANY, ANY, ANY],
        out_shape=(sd((lp, d), F32), sd((d, lp), BF16), sd((2, lp, 4 * gw), BF16),
                   sd(w_out.shape, F32), sd(w_grp.shape, F32), sd((4, 1, gw), F32), sd((4, 1, gw), F32), sd((1, d), F32)),
        scratch_shapes=[pltpu.VMEM(w_in.shape, BF16), pltpu.VMEM(w_grp.shape, BF16), pltpu.VMEM(w_out.shape, BF16),
                        pltpu.VMEM((4, POOL_HALO, gw), F32), pltpu.VMEM(w_out.shape, F32), pltpu.VMEM(w_grp.shape, F32),
                        pltpu.VMEM((4, 1, gw), F32), pltpu.VMEM((4, 1, gw), F32), pltpu.VMEM((1, d), F32)],
        compiler_params=_params(),
    )(h, dh, halos, acts, g, b_grp, scale, w_in, w_grp, w_out)


def loss_head(h, target, g, pad_tiles, name):
    lp, d = h.shape
    tt = TOKEN_TILE
    nt = lp // tt

    def body(h_ref, t_ref, g_ref, dh_ref, dg_ref, loss_ref, acc):
        i = pl.program_id(0)

        @pl.when(i == 0)
        def _():
            acc[...] = jnp.zeros_like(acc)
            dg_ref[...] = jnp.zeros_like(dg_ref)

        @pl.when(i < pad_tiles)
        def _():
            dh_ref[...] = jnp.zeros_like(dh_ref)

        @pl.when(i >= pad_tiles)
        def _():
            gv = g_ref[...]
            n, hh, rr = _rms_fwd(h_ref[...], gv)
            err = n - t_ref[...]
            acc[...] += 0.5 * jnp.sum(jnp.mean(err * err, axis=-1, keepdims=True), axis=0, keepdims=True)
            dn = err * (1.0 / d)
            dg_ref[...] += jnp.sum(dn * hh, axis=0, keepdims=True)
            dh_ref[...] = _rms_bwd(dn, hh, rr, gv)

        loss_ref[...] = jnp.broadcast_to(acc[...], loss_ref.shape)

    sd = jax.ShapeDtypeStruct
    return pl.pallas_call(
        body, name=name, grid=(nt,),
        in_specs=[pl.BlockSpec((tt, d), lambda i: (i, 0)), pl.BlockSpec((tt, d), lambda i: (jnp.maximum(i - pad_tiles, 0), 0)),
                  pl.BlockSpec((1, d), lambda i: (0, 0))],
        out_specs=[pl.BlockSpec((tt, d), lambda i: (i, 0)), pl.BlockSpec((1, d), lambda i: (0, 0)),
                   pl.BlockSpec((8, 128), lambda i: (0, 0))],
        out_shape=(sd((lp, d), F32), sd((1, d), F32), sd((8, 128), F32)),
        scratch_shapes=[pltpu.VMEM((1, 1), F32)],
        compiler_params=_params(),
    )(h, target, g)


def exchange(arrs, gather, name):
    n = len(arrs)

    def body(*refs):
        ins, outs = refs[:n], refs[n:2 * n]
        send_sems, recv_sems, own_sems = refs[2 * n:]
        x, y, c = lax.axis_index("x"), lax.axis_index("y"), lax.axis_index("c")
        me = 4 * x + 2 * y + c
        own = []
        for a in range(n):
            cp = pltpu.make_async_copy(ins[a] if gather else ins[a].at[me], outs[a].at[me], own_sems.at[a])
            cp.start()
            own.append(cp)
        sent = []
        for k in range(1, N_DEV):
            px = 1 - x if k & 4 else x
            py = 1 - y if k & 2 else y
            pc = 1 - c if k & 1 else c
            peer = 4 * px + 2 * py + pc
            for a in range(n):
                cp = pltpu.make_async_remote_copy(
                    src_ref=ins[a] if gather else ins[a].at[peer], dst_ref=outs[a].at[me],
                    send_sem=send_sems.at[a, k - 1], recv_sem=recv_sems.at[a, k - 1],
                    device_id=(px, py, pc), device_id_type=pl.DeviceIdType.MESH)
                cp.start()
                sent.append((cp, a, k, peer, (px, py, pc)))
        for cp, a, k, peer, pid in sent:
            cp.wait_send()
            pltpu.make_async_remote_copy(
                src_ref=ins[a] if gather else ins[a].at[peer], dst_ref=outs[a].at[peer],
                send_sem=send_sems.at[a, k - 1], recv_sem=recv_sems.at[a, k - 1],
                device_id=pid, device_id_type=pl.DeviceIdType.MESH).wait_recv()
        for cp in own:
            cp.wait()

    hbm = pl.BlockSpec(memory_space=pltpu.HBM)
    out_shape = tuple(jax.ShapeDtypeStruct(((N_DEV,) + a.shape) if gather else a.shape, a.dtype) for a in arrs)
    return pl.pallas_call(
        body, name=name, in_specs=[hbm] * n, out_specs=[hbm] * n, out_shape=out_shape,
        scratch_shapes=[pltpu.SemaphoreType.DMA((n, N_DEV - 1)), pltpu.SemaphoreType.DMA((n, N_DEV - 1)),
                        pltpu.SemaphoreType.DMA((n,))],
    )(*[pltpu.with_memory_space_constraint(a, pltpu.HBM) for a in arrs])


def _peers(x, y, c):
    out = []
    for k in range(1, N_DEV):
        px = 1 - x if k & 4 else x
        py = 1 - y if k & 2 else y
        pc = 1 - c if k & 1 else c
        out.append((k, (px, py, pc), 4 * px + 2 * py + pc))
    return out


def exchange_start(arrs, gather, after, name):
    n = len(arrs)
    me = 4 * lax.axis_index("x") + 2 * lax.axis_index("y") + lax.axis_index("c")
    lands = []
    for a in arrs:
        own = a[None] if gather else lax.dynamic_index_in_dim(a, me, 0, keepdims=True)
        lands.append(lax.dynamic_update_index_in_dim(lax.empty(((N_DEV,) + a.shape) if gather else a.shape, a.dtype), own, me, 0))

    def body(*refs):
        ins, land = refs[:n], refs[n:2 * n]
        send_sems, recv_sems, token = refs[2 * n + 1], refs[2 * n + 2], refs[4 * n + 3]
        x, y, c = lax.axis_index("x"), lax.axis_index("y"), lax.axis_index("c")
        me = 4 * x + 2 * y + c
        for k, pid, peer in _peers(x, y, c):
            for a in range(n):
                pltpu.make_async_remote_copy(
                    src_ref=ins[a] if gather else ins[a].at[peer], dst_ref=land[a].at[me],
                    send_sem=send_sems.at[a * (N_DEV - 1) + k - 1], recv_sem=recv_sems.at[a * (N_DEV - 1) + k - 1],
                    device_id=pid, device_id_type=pl.DeviceIdType.MESH).start()
        token[...] = jnp.zeros_like(token)

    hbm = pl.BlockSpec(memory_space=pltpu.HBM)
    sem = pl.BlockSpec(memory_space=pltpu.SEMAPHORE)
    sems = pltpu.SemaphoreType.DMA((n * (N_DEV - 1),))
    res = pl.pallas_call(
        body, name=name, in_specs=[hbm] * (2 * n) + [ANY],
        out_specs=[sem, sem] + [hbm] * (2 * n) + [pl.BlockSpec(memory_space=pltpu.VMEM)],
        out_shape=[sems, sems] + [pltpu.HBM(a.shape, a.dtype) for a in arrs] + [pltpu.HBM(l.shape, l.dtype) for l in lands]
        + [jax.ShapeDtypeStruct((8, 128), F32)],
        input_output_aliases={a: 2 + a for a in range(2 * n)},
        compiler_params=pltpu.CompilerParams(has_side_effects=pltpu.SideEffectType.DATAFLOW_SIDE_EFFECTING),
    )(*[pltpu.with_memory_space_constraint(a, pltpu.HBM) for a in list(arrs) + lands], after)
    return res[0], res[1], res[2:2 + n], res[2 + n:2 + 2 * n], res[-1]


def exchange_wait(started, gather, after, name):
    send_sems, recv_sems, srcs, lands, _ = started
    n = len(srcs)
    after = list(after) if isinstance(after, (list, tuple)) else [after]

    def body(*refs):
        ins, land = refs[:n], refs[n:2 * n]
        send_sems, recv_sems = refs[2 * n], refs[2 * n + 1]
        x, y, c = lax.axis_index("x"), lax.axis_index("y"), lax.axis_index("c")
        for k, pid, peer in _peers(x, y, c):
            for a in range(n):
                cp = pltpu.make_async_remote_copy(
                    src_ref=ins[a] if gather else ins[a].at[peer], dst_ref=land[a].at[peer],
                    send_sem=send_sems.at[a * (N_DEV - 1) + k - 1], recv_sem=recv_sems.at[a * (N_DEV - 1) + k - 1],
                    device_id=pid, device_id_type=pl.DeviceIdType.MESH)
                cp.wait_send()
                cp.wait_recv()

    hbm = pl.BlockSpec(memory_space=pltpu.HBM)
    sem = pl.BlockSpec(memory_space=pltpu.SEMAPHORE)
    res = pl.pallas_call(
        body, name=name, in_specs=[hbm] * (2 * n) + [sem, sem] + [ANY] * len(after),
        out_specs=[hbm] * (2 * n),
        out_shape=[pltpu.HBM(a.shape, a.dtype) for a in list(srcs) + list(lands)],
        input_output_aliases={a: a for a in range(2 * n)},
        compiler_params=pltpu.CompilerParams(has_side_effects=pltpu.SideEffectType.DATAFLOW_SIDE_EFFECTING),
    )(*srcs, *lands, send_sems, recv_sems, *after)
    return res[n:]


def _adamw(w, g, m, v):
    m = ADAM_B1 * m + (1.0 - ADAM_B1) * g
    v = ADAM_B2 * v + (1.0 - ADAM_B2) * (g * g)
    m_hat = m / (1.0 - ADAM_B1 ** ADAM_STEP)
    v_hat = v / (1.0 - ADAM_B2 ** ADAM_STEP)
    return -ADAM_LR * (m_hat / (jnp.sqrt(v_hat) + ADAM_EPS) + ADAM_WD * w), m, v


def _update_tile_rows(rows, cols):
    if rows * cols <= UPDATE_TILE_ELEMS:
        return rows
    return max(t for t in range(8, UPDATE_TILE_ELEMS // cols + 1, 8) if rows % t == 0)


def _sum_in_order(p_ref):
    g = p_ref[0].astype(F32)
    for j in range(1, p_ref.shape[0]):
        g = g + p_ref[j].astype(F32)
    return g


def sum_parts(parts, name):
    nparts, rows, cols = parts.shape
    tr = _update_tile_rows(rows, cols)

    def body(p_ref, g_ref):
        g_ref[...] = _sum_in_order(p_ref)

    return pl.pallas_call(
        body, name=name, grid=(rows // tr,),
        in_specs=[pl.BlockSpec((nparts, tr, cols), lambda i: (0, i, 0))],
        out_specs=pl.BlockSpec((tr, cols), lambda i: (i, 0)), out_shape=jax.ShapeDtypeStruct((rows, cols), F32),
        compiler_params=_params(),
    )(parts)


def sum_adamw(parts, w, m, v, name):
    rows, cols = w.shape
    nparts = parts.shape[0]
    tr = _update_tile_rows(rows, cols)

    def body(p_ref, w_ref, m_ref, v_ref, g_ref, d_ref, nm_ref, nv_ref):
        g = _sum_in_order(p_ref)
        delta, nm, nv = _adamw(w_ref[...], g, m_ref[...], v_ref[...])
        g_ref[...] = g
        d_ref[...] = delta
        nm_ref[...] = nm
        nv_ref[...] = nv

    blk = pl.BlockSpec((tr, cols), lambda i: (i, 0))
    sd = jax.ShapeDtypeStruct((rows, cols), F32)
    return pl.pallas_call(
        body, name=name, grid=(rows // tr,),
        in_specs=[pl.BlockSpec((nparts, tr, cols), lambda i: (0, i, 0)), blk, blk, blk],
        out_specs=[blk] * 4, out_shape=(sd,) * 4,
        compiler_params=_params(),
    )(parts, w, m, v)


def update_packed(g, w, m, v, pieces, name):
    rows_all = w.shape[0]

    def body(g_ref, w_ref, m_ref, v_ref, *outs):
        gv = g_ref[:rows_all, :]
        res = (gv,) + _adamw(w_ref[...], gv, m_ref[...], v_ref[...])
        for p, (row, rows, lanes) in enumerate(pieces):
            for k in range(4):
                outs[4 * p + k][...] = res[k][row:row + rows, :lanes]

    shapes = [jax.ShapeDtypeStruct((rows, lanes), F32) for _, rows, lanes in pieces for _ in range(4)]
    return pl.pallas_call(body, name=name, out_shape=shapes,
                          compiler_params=pltpu.CompilerParams(vmem_limit_bytes=VMEM_LIMIT))(g, w, m, v)


def update_natural(groups, name):
    n = len(groups)
    steps = 8

    def body(*refs):
        ins, outs = refs[:4 * n], refs[4 * n:]
        for j in range(n):
            g_ref, w_ref, m_ref, v_ref = ins[4 * j:4 * j + 4]
            gv = g_ref[...]
            res = (gv,) + _adamw(w_ref[...], gv, m_ref[...], v_ref[...])
            for k in range(4):
                outs[4 * j + k][...] = res[k]

    specs, shapes = [], []
    for g, w, m, v in groups:
        rows, cols = w.shape
        specs += [pl.BlockSpec((rows // steps, cols), lambda i: (i, 0))] * 4
        shapes += [jax.ShapeDtypeStruct((rows, cols), F32)] * 4
    return pl.pallas_call(body, name=name, grid=(steps,), in_specs=specs, out_specs=specs, out_shape=shapes,
                          compiler_params=_params())(*[a for grp in groups for a in grp])


S5_NAMES = ("w_in", "lam_re", "lam_im", "log_dt", "b_re", "b_im", "c_re", "c_im", "d_skip", "w_glu", "b_glu", "w_out")
CONV_NAMES = ("w_in", "conv_w", "conv_b", "w_out")
POOL_NAMES = ("w_in", "w_grp", "b_grp", "scale", "w_out")
LAYER_KINDS = ("s5", "conv", "pool", "s5")
LAYER_NAMES = {"s5": S5_NAMES, "conv": CONV_NAMES, "pool": POOL_NAMES}
SHARDED = {"s5": ("w_in", "w_glu", "w_out"), "conv": ("w_in", "conv_w", "w_out"), "pool": ("w_in", "w_grp", "b_grp", "w_out")}
GATHER_F32 = ("conv_w", "b_grp")


def weight_names():
    names = ["meta_tokens"]
    for i, kind in enumerate(LAYER_KINDS):
        names.append("norm%d_g" % i)
        names += ["l%d_%s" % (i, n) for n in LAYER_NAMES[kind]]
    names.append("final_g")
    return names


def sharded_names():
    return ["meta_tokens"] + ["l%d_%s" % (i, n) for i, kind in enumerate(LAYER_KINDS) for n in SHARDED[kind]]


def _block_diag_in(bb_t, gc):
    i, g, p = bb_t.shape
    t = bb_t.astype(BF16).reshape(i, 4, gc, p)
    return jnp.einsum("icjp,jk->cjikp", t, jnp.eye(gc, dtype=BF16)).reshape(4, gc * i, gc * p)


def _block_diag_in_grad(blocks):
    _, gc, i, p = blocks.shape
    return jnp.transpose(blocks, (2, 0, 1, 3)).reshape(i, 4 * gc, p)


def _block_diag_out(cc, gc):
    g, i, p = cc.shape
    return jnp.einsum("cjip,jk->cjpki", cc.astype(BF16).reshape(4, gc, i, p), jnp.eye(gc, dtype=BF16)).reshape(4, gc * p, gc * i)


def _block_diag_out_grad(blocks):
    _, gc, i, p = blocks.shape
    return blocks.reshape(4 * gc, i, p)


def _to_owner_blocks(a, axis):
    shape = a.shape[:axis] + (N_DEV, a.shape[axis] // N_DEV) + a.shape[axis + 1:]
    return jnp.moveaxis(a.reshape(shape), axis, 0)


def _from_owner_blocks(a, axis):
    a = jnp.moveaxis(a, 0, axis)
    return a.reshape(a.shape[:axis] + (a.shape[axis] * a.shape[axis + 1],) + a.shape[axis + 2:])


def _step(x, target, weights, moments_m, moments_v):
    seq, d = x.shape[1], x.shape[2]
    n_meta = weights["meta_tokens"].shape[0]
    tt = TOKEN_TILE
    pad_tiles = -(-n_meta // tt)
    p0 = pad_tiles * tt
    lp = p0 + seq
    first_pos = p0 - n_meta
    gc = d // 4 // S5_GROUP
    cw = d // 4

    big_names = [n for n in sharded_names() if n != "meta_tokens" and n.split("_", 1)[1] not in GATHER_F32]
    small_names = [n for n in sharded_names() if n not in big_names]
    layer_big = [[n for n in big_names if n.startswith("l%d_" % i)] for i in range(len(LAYER_KINDS))]
    layer_big[0] = small_names + layer_big[0]
    gather_started = []
    after = jnp.zeros((8, 128), F32)
    for i, names in enumerate(layer_big):
        gather_started.append(exchange_start([weights[n] if n in small_names else weights[n].astype(BF16) for n in names], True,
                                             after, "gather_start_l%d" % i))
        after = gather_started[-1][4]

    def vec(name):
        return weights[name].reshape(1, -1)

    s5_prep = {}
    for i, kind in enumerate(LAYER_KINDS):
        if kind == "s5":
            p = "l%d_" % i
            lr, li = weights[p + "lam_re"], weights[p + "lam_im"] + after[0, 0]
            ldt = weights[p + "log_dt"].reshape(-1, 1)
            br_t = jnp.transpose(weights[p + "b_re"], (2, 0, 1))
            bi_t = jnp.transpose(weights[p + "b_im"], (2, 0, 1))
            ar, ai, bbr, bbi = s5_disc_fwd(lr, li, ldt, br_t, bi_t, p + "disc_fwd")
            s5_prep[i] = dict(
                disc=(lr, li, ldt, br_t, bi_t), ar=ar.reshape(4, -1, 128), ai=ai.reshape(4, -1, 128),
                bdre=_block_diag_in(bbr, gc), bdim=_block_diag_in(bbi, gc),
                cdre=_block_diag_out(weights[p + "c_re"], gc), cdim=_block_diag_out(-weights[p + "c_im"], gc),
                d_skip=weights[p + "d_skip"].reshape(4, 1, cw), b_glu=vec(p + "b_glu"))
    h = jnp.concatenate([jnp.zeros((p0, d), F32), x[0] + after[0, 0]], axis=0)

    prepared = [h] + [s5_prep[i][k] for i in s5_prep for k in ("bdre", "bdim", "cdre", "cdim")]
    gathered = dict(zip(layer_big[0], exchange_wait(gather_started[0], True, prepared, "gather_wait_l0")))
    h = lax.dynamic_update_slice(h, _from_owner_blocks(gathered["meta_tokens"], 1), (first_pos, 0))

    full = {}

    def layer_weights(i, kind, after):
        p = "l%d_" % i
        if i > 0:
            gathered.update(zip(layer_big[i], exchange_wait(gather_started[i], True, after, "gather_wait_l%d" % i)))
        w_in = gathered[p + "w_in"]
        if kind == "s5":
            full[i] = dict(s5_prep[i], w_in=w_in, w_glu=gathered[p + "w_glu"].reshape(4, cw, d),
                           w_out=gathered[p + "w_out"].reshape(4, cw, d))
        elif kind == "conv":
            ce = w_in.shape[2]
            nch = 2
            conv_w = _from_owner_blocks(gathered[p + "conv_w"], 1)
            full[i] = dict(
                w_in=w_in, conv_w=jnp.transpose(conv_w.reshape(CONV_K, nch, ce), (1, 0, 2)),
                conv_b=weights[p + "conv_b"].reshape(nch, 1, ce), w_out=gathered[p + "w_out"].reshape(nch, ce, d))
        else:
            gw = w_in.shape[2]
            full[i] = dict(
                w_in=w_in, w_grp=_from_owner_blocks(gathered[p + "w_grp"], 1),
                b_grp=_from_owner_blocks(gathered[p + "b_grp"], 1).reshape(4, 1, gw),
                scale=weights[p + "scale"].reshape(4, 1, gw), w_out=gathered[p + "w_out"].reshape(4, gw, d))
        return full[i]

    saved = {}
    for i, kind in enumerate(LAYER_KINDS):
        p, f, g = "l%d_" % i, layer_weights(i, kind, h), vec("norm%d_g" % i)
        if kind == "s5":
            u, z, xs = s5_fwd1(h, g, f["w_in"], f["bdre"], f["bdim"], p + "fwd_in")
            s = s5_scan_fwd(xs, f["ar"], f["ai"], p + "scan_fwd")
            h_in = h
            h, y, q = s5_fwd3(s, u, z, h, f["cdre"], f["cdim"], f["w_glu"], f["w_out"], f["d_skip"], f["b_glu"], p + "fwd_out")
            saved[i] = (h_in, u, z, s, y, q)
        elif kind == "conv":
            h_new, halos, acts = conv_fwd(h, g, f["w_in"], f["conv_w"], f["conv_b"], f["w_out"], p + "fwd")
            saved[i] = (h, halos, acts)
            h = h_new
        else:
            h_new, halos, acts = pool_fwd(h, g, f["w_in"], f["w_grp"], f["b_grp"], f["scale"], f["w_out"], first_pos, p + "fwd")
            saved[i] = (h, halos, acts)
            h = h_new

    dh, dg_final, loss_tile = loss_head(h, target[0], vec("final_g"), pad_tiles, "loss_head")
    loss = lax.psum(loss_tile[0, 0], ("x", "y", "c"))

    grads = {"final_g": dg_final}
    names = weight_names()
    sh_names = sharded_names()
    replicated = [n for n in names if n not in sh_names]
    vectors = [n for n in replicated if weights[n].ndim == 1]
    matrices = [n for n in replicated if weights[n].ndim > 1]
    rep_names = vectors + matrices

    def owner_blocks(a):
        return a.reshape(N_DEV, -1, a.shape[-1]).astype(BF16)

    def as2d(a):
        return a.reshape(-1, a.shape[-1])

    def pack(tree, which=None):
        flat = [jnp.pad(tree[n].reshape(-1), (0, -tree[n].size % 1024)) for n in (which or rep_names)]
        flat = jnp.concatenate(flat)
        if which is None:
            flat = jnp.pad(flat, (0, -flat.size % (PACK_ROWS * 128)))
        return flat.reshape(-1, 128)

    layer_sharded, scatter_started = {}, {}
    ordered = jnp.zeros((), F32)
    for i in reversed(range(len(LAYER_KINDS))):
        kind = LAYER_KINDS[i]
        p, f, g = "l%d_" % i, full[i], vec("norm%d_g" % i) + ordered
        if kind == "s5":
            h_in, u, z, s, y, q = saved[i]
            dy, dp, dwo, dwg, dbg = s5_bwd3a(dh, y, q, z, f["w_glu"], f["w_out"], f["b_glu"] + ordered, p + "bwd_out")
            d_skip = f["d_skip"]
            if i == 0:
                early_names = [p + "w_glu", p + "w_out"]
                scatter_started["early"] = exchange_start([owner_blocks(dwg), owner_blocks(dwo)], False, dy,
                                                          "scatter_start_l0_early")
                d_skip = d_skip + scatter_started["early"][4][0, 0]
            ds, dus, dcre, dcim, dd = s5_bwd3b(dy, s, u, f["cdre"], f["cdim"], d_skip, p + "bwd_read")
            lam, dar, dai = s5_scan_bwd(ds, s, f["ar"], f["ai"], p + "scan_bwd")
            dp, dh, n, dbre, dbim, dg = s5_bwd1(lam, dus, u, dp, h_in, dh, g, f["w_in"], f["bdre"], f["bdim"], p + "bwd_in")
            dw_in = grad_w_in(n, dp, f["w_in"].shape[2], p + "grad_w_in")
            grads.update({p + "w_in": dw_in, p + "w_glu": dwg.reshape(N_DEV, -1, d), p + "w_out": dwo.reshape(N_DEV, -1, d),
                          p + "d_skip": dd, p + "b_glu": dbg})

            def replicated_grads(p=p, f=f, dar=dar, dai=dai, dbre=dbre, dbim=dbim, dcre=dcre, dcim=dcim, token=None):
                lr, li, ldt, br_t, bi_t = f["disc"]
                dlr, dli, dldt, dbr_t, dbi_t = s5_disc_bwd(
                    lr, li, ldt, br_t, bi_t, dar.reshape(lr.shape) + token, dai.reshape(lr.shape),
                    _block_diag_in_grad(dbre), _block_diag_in_grad(dbim), p + "disc_bwd")
                grads.update({
                    p + "lam_re": dlr, p + "lam_im": dli, p + "log_dt": dldt,
                    p + "b_re": jnp.transpose(dbr_t, (1, 2, 0)), p + "b_im": jnp.transpose(dbi_t, (1, 2, 0)),
                    p + "c_re": _block_diag_out_grad(dcre), p + "c_im": -_block_diag_out_grad(dcim)})
        elif kind == "conv":
            replicated_grads = None
            h_in, halos, acts = saved[i]
            dh, n, dp, dwo, dcw, dcb, dg = conv_bwd(h_in, dh, halos, acts, g, f["w_in"], f["conv_w"], f["conv_b"], f["w_out"], p + "bwd")
            dw_in = grad_w_in(n, dp, f["w_in"].shape[2], p + "grad_w_in")
            dconv_w = jnp.transpose(dcw[:, :CONV_K, :], (1, 0, 2)).reshape(CONV_K, -1)
            grads.update({p + "w_in": dw_in, p + "conv_w": _to_owner_blocks(dconv_w, 1), p + "conv_b": dcb,
                          p + "w_out": dwo.reshape(N_DEV, -1, d)})
        else:
            replicated_grads = None
            h_in, halos, acts = saved[i]
            dh, n, dp, dwo, dwgrp, dbgrp, dsc, dg = pool_bwd(h_in, dh, halos, acts, g, f["w_in"], f["w_grp"], f["b_grp"], f["scale"],
                                                             f["w_out"], first_pos, p + "bwd")
            dw_in = grad_w_in(n, dp, f["w_in"].shape[2], p + "grad_w_in")
            grads.update({p + "w_in": dw_in, p + "w_grp": _to_owner_blocks(dwgrp, 1),
                          p + "b_grp": _to_owner_blocks(dbgrp.reshape(4, -1), 1), p + "scale": dsc,
                          p + "w_out": dwo.reshape(N_DEV, -1, d)})
        grads["norm%d_g" % i] = dg
        layer_sharded[i] = ["l%d_%s" % (i, n) for n in SHARDED[kind]]
        if i > 0:
            scatter_started[i] = exchange_start([owner_blocks(grads[n]) for n in layer_sharded[i]], False, dh,
                                                "scatter_start_l%d" % i)
            ordered = scatter_started[i][4][0, 0]
        if replicated_grads is not None:
            replicated_grads(token=ordered)
    grad_x = dh[p0:][None]
    grads["meta_tokens"] = _to_owner_blocks(dh[first_pos:p0], 1)
    last = len(LAYER_KINDS)
    layer_sharded[last] = ["meta_tokens", "replicated"]
    scatter_started[last] = exchange_start([owner_blocks(grads["meta_tokens"]), pack(grads).reshape(N_DEV, -1, 128)], False,
                                           dh, "scatter_start_replicated")
    layer_sharded["early"] = early_names
    layer_sharded[0] = [n for n in layer_sharded[0] if n not in early_names]

    out = {}
    received = {}
    after = [scatter_started[last][4]]
    for i in list(reversed(range(1, last))) + [last, "early", 0]:
        received.update(zip(layer_sharded[i], exchange_wait(scatter_started[i], False, after, "scatter_wait_%s" % i)))
        updated = []
        for n in layer_sharded[i]:
            if n != "replicated":
                res = sum_adamw(received[n], as2d(weights[n]), as2d(moments_m[n]), as2d(moments_v[n]), "update_" + n)
                out[n] = [r.reshape(weights[n].shape) for r in res]
                updated.append(out[n][0])
        after = updated or after
        if i == last:
            g_full = exchange([sum_parts(received["replicated"], "sum_replicated")], True, "gather_small_grads")[0]
            scatter_started[0] = exchange_start([owner_blocks(grads[n]) for n in layer_sharded[0]], False, g_full,
                                                "scatter_start_l0")
            g_full = g_full.reshape(-1, 128) + scatter_started[0][4][0, 0]
            offsets, offset = {}, 0
            for n in rep_names:
                offsets[n] = offset
                offset += weights[n].size + (-weights[n].size % 1024)
            pieces = [(offsets[n] // 128, max(weights[n].size // 128, 1), min(weights[n].size, 128)) for n in vectors]
            res = update_packed(g_full, pack(weights, vectors), pack(moments_m, vectors), pack(moments_v, vectors), pieces,
                                "update_replicated_vectors")
            for j, n in enumerate(vectors):
                out[n] = [r.reshape(weights[n].shape) for r in res[4 * j:4 * j + 4]]
            flat = g_full.reshape(-1)
            groups = [(flat[offsets[n]:offsets[n] + weights[n].size].reshape(as2d(weights[n]).shape), as2d(weights[n]),
                       as2d(moments_m[n]), as2d(moments_v[n])) for n in matrices]
            res = update_natural(groups, "update_replicated_matrices")
            for j, n in enumerate(matrices):
                out[n] = [r.reshape(weights[n].shape) for r in res[4 * j:4 * j + 4]]
            after = [out[n][k] for n in rep_names for k in range(4)]

    return (loss, grad_x) + tuple(out[n][k] for k in range(4) for n in names)


def kernel(x, *rest):
    names = weight_names()
    nw = len(names)
    weights = dict(zip(names, rest[:nw]))
    target = rest[nw]
    moments_m = dict(zip(names, rest[nw + 1:2 * nw + 1]))
    moments_v = dict(zip(names, rest[2 * nw + 1:3 * nw + 1]))
    return _step(x, target, weights, moments_m, moments_v)
```

```python
import functools
import math

import jax
import jax.numpy as jnp
from jax import lax
from jax.experimental import pallas as pl
from jax.experimental.pallas import tpu as pltpu

F32 = jnp.float32
BF16 = jnp.bfloat16
EPS = 1e-6
N_DEV = 8
TOKEN_TILE = 256
SCAN_CHUNKS = 4
S5_GROUP = 16
S5_STATE = 64
POOL_WINDOWS = (2, 4, 8, 16)
POOL_HALO = 16
CONV_K = 3
CONV_HALO = 8
ADAM_LR = 0.001
ADAM_B1 = 0.9
ADAM_B2 = 0.999
ADAM_EPS = 1e-08
ADAM_WD = 0.01
ADAM_STEP = 10
GELU_C = math.sqrt(2.0 / math.pi)
GELU_A = 0.044715
UPDATE_TILE_ELEMS = 1 << 17
PACK_ROWS = 512
VMEM_LIMIT = 56 << 20
VMEM_LIMIT_LARGE = 62 << 20

ANY = pl.BlockSpec(memory_space=pl.ANY)


def _params(vmem=VMEM_LIMIT, ndim=1):
    return pltpu.CompilerParams(vmem_limit_bytes=vmem, dimension_semantics=("arbitrary",) * ndim)


def _dot(a, b):
    return jnp.dot(a.astype(BF16), b.astype(BF16), preferred_element_type=F32)


def _dot_nt(a, b):
    return lax.dot_general(a.astype(BF16), b.astype(BF16), (((1,), (1,)), ((), ())), preferred_element_type=F32)


def _dot_tn(a, b):
    return lax.dot_general(a.astype(BF16), b.astype(BF16), (((0,), (0,)), ((), ())), preferred_element_type=F32)


def _rms_fwd(h, g):
    r = lax.rsqrt(jnp.mean(h * h, axis=-1, keepdims=True) + EPS)
    hh = h * r
    return hh * g, hh, r


def _rms_bwd(dn, hh, r, g):
    dhh = dn * g
    return r * (dhh - hh * jnp.mean(dhh * hh, axis=-1, keepdims=True))


def _sigmoid(x):
    return 1.0 / (1.0 + jnp.exp(-x))


def _silu_and_grad(z):
    s = _sigmoid(z)
    return z * s, s * (1.0 + z * (1.0 - s))


def _gelu(y):
    t = jnp.tanh(GELU_C * (y + GELU_A * y * y * y))
    return 0.5 * y * (1.0 + t), t


def _gelu_grad(y, t):
    return 0.5 * (1.0 + t) + 0.5 * y * (1.0 - t * t) * GELU_C * (1.0 + 3.0 * GELU_A * y * y)


def _rows(shape):
    return lax.broadcasted_iota(jnp.int32, shape, 0)


def _shift_down(x, k, halo):
    y = pltpu.roll(x, k, 0)
    rows = _rows(x.shape)
    for j in range(k):
        y = jnp.where(rows == j, halo[halo.shape[0] - k + j:halo.shape[0] - k + j + 1, :], y)
    return y


def _shift_up(x, k, halo):
    n = x.shape[0]
    y = pltpu.roll(x, n - k, 0)
    rows = _rows(x.shape)
    for j in range(k):
        y = jnp.where(rows == n - k + j, halo[j:j + 1, :], y)
    return y


def _window_sums_back(ext):
    out = []
    s = ext
    for k in (1, 2, 4, 8):
        s = s + pltpu.roll(s, k, 0)
        out.append(s)
    return out


def _window_sums_fwd(ext):
    n = ext.shape[0]
    out = []
    s = ext
    for k in (1, 2, 4, 8):
        s = s + pltpu.roll(s, n - k, 0)
        out.append(s)
    return out


def _pool_inv_count(tile, tt, first_pos, w, width):
    pos = _rows((tt, width)) + (tile * tt - first_pos + 1)
    return 1.0 / jnp.clip(pos, 1, w).astype(F32)


def _slab_spec(lp, tt, sw):
    nj = sw // 128
    return pl.BlockSpec((4 * tt * nj, 128), lambda i: (i, 0)), (lp * 4 * nj, 128)


def _pack_pair(re, im):
    def rounded(v):
        return lax.bitcast_convert_type(v, jnp.int32) + 0x8000
    return lax.bitcast_convert_type((rounded(re) & -65536) | lax.shift_right_logical(rounded(im), 16), F32)


def _unpack_pair(w):
    b = lax.bitcast_convert_type(w, jnp.int32)
    return lax.bitcast_convert_type(b & -65536, F32), lax.bitcast_convert_type(lax.shift_left(b, 16), F32)


def _slab_load(ref, c):
    nj = ref.shape[0] // (4 * TOKEN_TILE)
    first = c * TOKEN_TILE * nj
    return _unpack_pair(jnp.concatenate([ref[pl.ds(first + j, TOKEN_TILE, stride=nj), :] for j in range(nj)], axis=1))


def _slab_store(ref, c, re, im):
    nj = ref.shape[0] // (4 * TOKEN_TILE)
    first = c * TOKEN_TILE * nj
    val = _pack_pair(re, im)
    for j in range(nj):
        ref[pl.ds(first + j, TOKEN_TILE, stride=nj), :] = val[:, j * 128:(j + 1) * 128]


def _s5_disc_math(lr, li, ldt, br, bi):
    dt = jnp.exp(ldt)
    mag = jnp.exp(lr * dt)
    ar = mag * jnp.cos(li * dt)
    ai = mag * jnp.sin(li * dt)
    den = lr * lr + li * li
    kr = ((ar - 1.0) * lr + ai * li) / den
    ki = (ai * lr - (ar - 1.0) * li) / den
    bbr = kr[None] * br - ki[None] * bi
    bbi = kr[None] * bi + ki[None] * br
    return ar, ai, bbr, bbi


def s5_disc_fwd(lr, li, ldt, br_t, bi_t, name):
    def body(lr_ref, li_ref, ldt_ref, br_ref, bi_ref, ar_ref, ai_ref, bbr_ref, bbi_ref):
        ar, ai, bbr, bbi = _s5_disc_math(lr_ref[...], li_ref[...], ldt_ref[...], br_ref[...], bi_ref[...])
        ar_ref[...] = ar
        ai_ref[...] = ai
        bbr_ref[...] = bbr
        bbi_ref[...] = bbi

    sd = jax.ShapeDtypeStruct
    return pl.pallas_call(
        body, name=name,
        out_shape=(sd(lr.shape, F32), sd(lr.shape, F32), sd(br_t.shape, F32), sd(br_t.shape, F32)),
    )(lr, li, ldt, br_t, bi_t)


def s5_disc_bwd(lr, li, ldt, br_t, bi_t, dar, dai, dbbr, dbbi, name):
    def body(lr_ref, li_ref, ldt_ref, br_ref, bi_ref, dar_ref, dai_ref, dbbr_ref, dbbi_ref,
             dlr_ref, dli_ref, dldt_ref, dbr_ref, dbi_ref):
        _, vjp = jax.vjp(_s5_disc_math, lr_ref[...], li_ref[...], ldt_ref[...], br_ref[...], bi_ref[...])
        dlr, dli, dldt, dbr, dbi = vjp((dar_ref[...], dai_ref[...], dbbr_ref[...], dbbi_ref[...]))
        dlr_ref[...] = dlr
        dli_ref[...] = dli
        dldt_ref[...] = dldt
        dbr_ref[...] = dbr
        dbi_ref[...] = dbi

    sd = jax.ShapeDtypeStruct
    return pl.pallas_call(
        body, name=name,
        out_shape=(sd(lr.shape, F32), sd(lr.shape, F32), sd(ldt.shape, F32), sd(br_t.shape, F32), sd(br_t.shape, F32)),
    )(lr, li, ldt, br_t, bi_t, dar, dai, dbbr, dbbi)


def s5_fwd1(h, g, w_in, bdre, bdim, name):
    lp, d = h.shape
    tt = TOKEN_TILE
    cw, sw = bdre.shape[1], bdre.shape[2]

    def body(h_ref, g_ref, w_hbm, bdre_hbm, bdim_hbm, u_ref, z_ref, x_ref, w, bre, bim):
        @pl.when(pl.program_id(0) == 0)
        def _():
            pltpu.sync_copy(w_hbm, w)
            pltpu.sync_copy(bdre_hbm, bre)
            pltpu.sync_copy(bdim_hbm, bim)

        n = _rms_fwd(h_ref[...], g_ref[...])[0].astype(BF16)
        for c in range(4):
            cols = slice(c * cw, (c + 1) * cw)
            u = jnp.dot(n, w[c], preferred_element_type=F32)
            u_ref[:, cols] = u
            z_ref[:, cols] = jnp.dot(n, w[c + 4], preferred_element_type=F32)
            ub = u.astype(BF16)
            _slab_store(x_ref, c, jnp.dot(ub, bre[c], preferred_element_type=F32), jnp.dot(ub, bim[c], preferred_element_type=F32))

    sd = jax.ShapeDtypeStruct
    slab, slab_shape = _slab_spec(lp, tt, sw)
    row = pl.BlockSpec((tt, d), lambda i: (i, 0))
    return pl.pallas_call(
        body, name=name, grid=(lp // tt,),
        in_specs=[row, pl.BlockSpec((1, d), lambda i: (0, 0)), ANY, ANY, ANY],
        out_specs=[row, row, slab],
        out_shape=(sd((lp, d), F32), sd((lp, d), F32), sd(slab_shape, F32)),
        scratch_shapes=[pltpu.VMEM(w_in.shape, BF16), pltpu.VMEM(bdre.shape, BF16), pltpu.VMEM(bdim.shape, BF16)],
        compiler_params=_params(),
    )(h, g, w_in, bdre, bdim)


def s5_scan_fwd(x, ar, ai, name):
    nj = ar.shape[1]
    tt = TOKEN_TILE
    cpb = SCAN_CHUNKS
    nt = x.shape[0] // (4 * tt * nj)

    def body(x_ref, ar_ref, ai_ref, s_ref, st_r, st_i):
        i, cg = pl.program_id(0), pl.program_id(1)

        @pl.when(i == 0)
        def _():
            for q in range(cpb):
                st_r[cg * cpb + q] = jnp.zeros((nj, 128), F32)
                st_i[cg * cpb + q] = jnp.zeros((nj, 128), F32)

        a_r = [ar_ref[cg * cpb + q] for q in range(cpb)]
        a_i = [ai_ref[cg * cpb + q] for q in range(cpb)]

        def step(t, carry):
            out = []
            for q in range(cpb):
                s_r, s_i = carry[q]
                rows = pl.ds(pl.multiple_of((q * tt + t) * nj, nj), nj)
                x_r, x_i = _unpack_pair(x_ref[rows, :])
                n_r = a_r[q] * s_r - a_i[q] * s_i + x_r
                n_i = a_r[q] * s_i + a_i[q] * s_r + x_i
                s_ref[rows, :] = _pack_pair(n_r, n_i)
                out.append((n_r, n_i))
            return tuple(out)

        init = tuple((st_r[cg * cpb + q], st_i[cg * cpb + q]) for q in range(cpb))
        final = lax.fori_loop(0, tt, step, init, unroll=8)
        for q in range(cpb):
            st_r[cg * cpb + q] = final[q][0]
            st_i[cg * cpb + q] = final[q][1]

    blk = pl.BlockSpec((cpb * tt * nj, 128), lambda i, cg: (i * (4 // cpb) + cg, 0))
    par = pl.BlockSpec((4, nj, 128), lambda i, cg: (0, 0, 0))
    sd = jax.ShapeDtypeStruct
    return pl.pallas_call(
        body, name=name, grid=(nt, 4 // cpb),
        in_specs=[blk, par, par], out_specs=blk,
        out_shape=sd(x.shape, F32),
        scratch_shapes=[pltpu.VMEM((4, nj, 128), F32), pltpu.VMEM((4, nj, 128), F32)],
        compiler_params=_params(ndim=2),
    )(x, ar, ai)


def s5_fwd3(s, u, z, h, cdre, cdim, w_glu, w_out, d_skip, b_glu, name):
    lp, d = h.shape
    tt = TOKEN_TILE
    sw, cw = cdre.shape[1], cdre.shape[2]

    def body(s_ref, u_ref, z_ref, h_ref, d_ref, bg_ref, cre_hbm, cim_hbm, wg_hbm, wo_hbm,
             o_ref, y_ref, q_ref, cre, cim, wg, wo):
        @pl.when(pl.program_id(0) == 0)
        def _():
            pltpu.sync_copy(cre_hbm, cre)
            pltpu.sync_copy(cim_hbm, cim)
            pltpu.sync_copy(wg_hbm, wg)
            pltpu.sync_copy(wo_hbm, wo)

        gys, q = [], None
        for c in range(4):
            cols = slice(c * cw, (c + 1) * cw)
            s_r, s_i = _slab_load(s_ref, c)
            y = _dot(s_r, cre[c]) + _dot(s_i, cim[c]) + d_ref[c] * u_ref[:, cols]
            y_ref[:, cols] = y
            gys.append(_gelu(y)[0])
            part = _dot(gys[c], wg[c])
            q = part if c == 0 else q + part
        q_ref[...] = q
        sig = _sigmoid(q + bg_ref[...])
        zz = z_ref[...]
        sz = zz * _sigmoid(zz)
        o = h_ref[...]
        for k in range(4):
            cols = slice(k * cw, (k + 1) * cw)
            o = o + _dot(gys[k] * sig[:, cols] * sz[:, cols], wo[k])
        o_ref[...] = o

    row = pl.BlockSpec((tt, d), lambda i: (i, 0))
    slab, _ = _slab_spec(lp, tt, sw)
    sd = jax.ShapeDtypeStruct((lp, d), F32)
    return pl.pallas_call(
        body, name=name, grid=(lp // tt,),
        in_specs=[slab, row, row, row, pl.BlockSpec((4, 1, cw), lambda i: (0, 0, 0)), pl.BlockSpec((1, d), lambda i: (0, 0)),
                  ANY, ANY, ANY, ANY],
        out_specs=[row, row, row],
        out_shape=(sd, sd, sd),
        scratch_shapes=[pltpu.VMEM(cdre.shape, BF16), pltpu.VMEM(cdim.shape, BF16), pltpu.VMEM(w_glu.shape, BF16),
                        pltpu.VMEM(w_out.shape, BF16)],
        compiler_params=_params(),
    )(s, u, z, h, d_skip, b_glu, cdre, cdim, w_glu, w_out)


def s5_bwd3a(dh, y, q, z, w_glu, w_out, b_glu, name):
    lp, d = dh.shape
    tt = TOKEN_TILE
    nt = lp // tt
    cw = w_glu.shape[1]

    def body(dh_ref, y_ref, q_ref, z_ref, bg_ref, wg_hbm, wo_hbm, dy_ref, dp_ref, dwo_hbm, dwg_hbm, dbg_hbm,
             wg, wo, dwo, dwg, dbg):
        i = pl.program_id(0)

        @pl.when(i == 0)
        def _():
            pltpu.sync_copy(wg_hbm, wg)
            pltpu.sync_copy(wo_hbm, wo)
            dwo[...] = jnp.zeros_like(dwo)
            dwg[...] = jnp.zeros_like(dwg)
            dbg[...] = jnp.zeros_like(dbg)

        sig = _sigmoid(q_ref[...] + bg_ref[...])
        sz, dsz = _silu_and_grad(z_ref[...])
        dhv = dh_ref[...]
        yv = y_ref[...]
        gy, t = _gelu(yv)
        dq_parts, dgy_parts = [], []
        for k in range(4):
            cols = slice(k * cw, (k + 1) * cw)
            gy_k, sig_k, sz_k = gy[:, cols], sig[:, cols], sz[:, cols]
            y2 = gy_k * sig_k
            dy3 = _dot_nt(dhv, wo[k])
            dwo[k] += _dot_tn(y2 * sz_k, dhv)
            dy2 = dy3 * sz_k
            dp_ref[0, :, cols] = (dy3 * y2 * dsz[:, cols]).astype(BF16)
            dq_parts.append(dy2 * gy_k * sig_k * (1.0 - sig_k))
            dgy_parts.append(dy2 * sig_k)
        dq = jnp.concatenate(dq_parts, axis=1)
        dbg[...] += jnp.sum(dq, axis=0, keepdims=True)
        dgelu = _gelu_grad(yv, t)
        for k in range(4):
            cols = slice(k * cw, (k + 1) * cw)
            dwg[k] += _dot_tn(gy[:, cols], dq)
            dy_ref[:, cols] = (dgy_parts[k] + _dot_nt(dq, wg[k])) * dgelu[:, cols]

        @pl.when(i == nt - 1)
        def _():
            pltpu.sync_copy(dwo, dwo_hbm)
            pltpu.sync_copy(dwg, dwg_hbm)
            pltpu.sync_copy(dbg, dbg_hbm)

    row = pl.BlockSpec((tt, d), lambda i: (i, 0))
    sd = jax.ShapeDtypeStruct
    return pl.pallas_call(
        body, name=name, grid=(nt,),
        in_specs=[row, row, row, row, pl.BlockSpec((1, d), lambda i: (0, 0)), ANY, ANY],
        out_specs=[row, pl.BlockSpec((1, tt, d), lambda i: (1, i, 0)), ANY, ANY, ANY],
        out_shape=(sd((lp, d), F32), sd((2, lp, d), BF16), sd(w_out.shape, F32), sd(w_glu.shape, F32), sd((1, d), F32)),
        scratch_shapes=[pltpu.VMEM(w_glu.shape, BF16), pltpu.VMEM(w_out.shape, BF16),
                        pltpu.VMEM(w_out.shape, F32), pltpu.VMEM(w_glu.shape, F32), pltpu.VMEM((1, d), F32)],
        compiler_params=_params(),
    )(dh, y, q, z, b_glu, w_glu, w_out)


def s5_bwd3b(dy, s, u, cdre, cdim, d_skip, name):
    lp, d = dy.shape
    tt = TOKEN_TILE
    nt = lp // tt
    sw, cw = cdre.shape[1], cdre.shape[2]
    gc = cw // S5_GROUP

    def body(dy_ref, s_ref, u_ref, d_ref, cre_hbm, cim_hbm,
             ds_ref, dus_ref, dcre_ref, dcim_ref, dd_hbm, cre, cim, dcre, dcim, dd):
        i = pl.program_id(0)

        @pl.when(i == 0)
        def _():
            pltpu.sync_copy(cre_hbm, cre)
            pltpu.sync_copy(cim_hbm, cim)
            dcre[...] = jnp.zeros_like(dcre)
            dcim[...] = jnp.zeros_like(dcim)
            dd[...] = jnp.zeros_like(dd)

        for c in range(4):
            chunk = slice(c * cw, (c + 1) * cw)
            dyv = dy_ref[:, chunk]
            dd[c] += jnp.sum(dyv * u_ref[:, chunk], axis=0, keepdims=True)
            dus_ref[:, chunk] = dyv * d_ref[c]
            _slab_store(ds_ref, c, _dot_nt(dyv, cre[c]), _dot_nt(dyv, cim[c]))
            s_r, s_i = _slab_load(s_ref, c)
            dcre[c] += _dot_tn(s_r, dyv)
            dcim[c] += _dot_tn(s_i, dyv)

        @pl.when(i == nt - 1)
        def _():
            for k in range(4):
                for j in range(gc):
                    rows, cols = pl.ds(j * S5_STATE, S5_STATE), pl.ds(j * S5_GROUP, S5_GROUP)
                    dcre_ref[k, j] = dcre[k, rows, cols].T
                    dcim_ref[k, j] = dcim[k, rows, cols].T
            pltpu.sync_copy(dd, dd_hbm)

    sd = jax.ShapeDtypeStruct
    row = pl.BlockSpec((tt, d), lambda i: (i, 0))
    slab, slab_shape = _slab_spec(lp, tt, sw)
    diag = pl.BlockSpec((4, gc, S5_GROUP, S5_STATE), lambda i: (0, 0, 0, 0))
    return pl.pallas_call(
        body, name=name, grid=(nt,),
        in_specs=[row, slab, row, pl.BlockSpec((4, 1, cw), lambda i: (0, 0, 0)), ANY, ANY],
        out_specs=[slab, row, diag, diag, ANY],
        out_shape=(sd(slab_shape, F32), sd((lp, d), F32),
                   sd((4, gc, S5_GROUP, S5_STATE), F32), sd((4, gc, S5_GROUP, S5_STATE), F32), sd((4, 1, cw), F32)),
        scratch_shapes=[pltpu.VMEM(cdre.shape, BF16), pltpu.VMEM(cdim.shape, BF16),
                        pltpu.VMEM(cdre.shape, F32), pltpu.VMEM(cdim.shape, F32), pltpu.VMEM((4, 1, cw), F32)],
        compiler_params=_params(),
    )(dy, s, u, d_skip, cdre, cdim)


def s5_scan_bwd(g, s, ar, ai, name):
    nj = ar.shape[1]
    tt = TOKEN_TILE
    cpb = SCAN_CHUNKS
    nt = g.shape[0] // (4 * tt * nj)

    def body(g_ref, s_ref, ar_ref, ai_ref, lam_ref, dar_ref, dai_ref, st_r, st_i, acc_r, acc_i):
        i, cg = pl.program_id(0), pl.program_id(1)

        @pl.when((i == 0) & (cg == 0))
        def _():
            for ref in (st_r, st_i, acc_r, acc_i):
                ref[...] = jnp.zeros_like(ref)

        a_r = [ar_ref[cg * cpb + q] for q in range(cpb)]
        a_i = [ai_ref[cg * cpb + q] for q in range(cpb)]

        def slab(q, t):
            return pl.ds(pl.multiple_of((q * tt + t) * nj, nj), nj)

        def adjoint(q, t, l_r, l_i):
            rows = slab(q, t)
            g_r, g_i = _unpack_pair(g_ref[rows, :])
            n_r = g_r + a_r[q] * l_r + a_i[q] * l_i
            n_i = g_i + a_r[q] * l_i - a_i[q] * l_r
            lam_ref[rows, :] = _pack_pair(n_r, n_i)
            return n_r, n_i

        def pair(q, t, l_r, l_i, d_r, d_i):
            p_r, p_i = _unpack_pair(s_ref[slab(q, t), :])
            return d_r + l_r * p_r + l_i * p_i, d_i + l_i * p_r - l_r * p_i

        def step(k, carry):
            t = tt - 1 - k
            out = []
            for q in range(cpb):
                l_r, l_i, d_r, d_i = carry[q]
                l_r, l_i = adjoint(q, t, l_r, l_i)
                d_r, d_i = pair(q, t - 1, l_r, l_i, d_r, d_i)
                out.append((l_r, l_i, d_r, d_i))
            return tuple(out)

        init = []
        for q in range(cpb):
            ch = cg * cpb + q
            l_r, l_i = st_r[ch], st_i[ch]
            d_r, d_i = pair(q, tt - 1, l_r, l_i, acc_r[ch], acc_i[ch])
            init.append((l_r, l_i, d_r, d_i))
        final = lax.fori_loop(0, tt - 1, step, tuple(init), unroll=8)
        for q in range(cpb):
            ch = cg * cpb + q
            l_r, l_i, d_r, d_i = final[q]
            l_r, l_i = adjoint(q, 0, l_r, l_i)
            st_r[ch] = l_r
            st_i[ch] = l_i
            acc_r[ch] = d_r
            acc_i[ch] = d_i
            dar_ref[ch] = d_r
            dai_ref[ch] = d_i

    blk = pl.BlockSpec((cpb * tt * nj, 128), lambda i, cg: ((nt - 1 - i) * (4 // cpb) + cg, 0))
    par = pl.BlockSpec((4, nj, 128), lambda i, cg: (0, 0, 0))
    sd = jax.ShapeDtypeStruct
    return pl.pallas_call(
        body, name=name, grid=(nt, 4 // cpb),
        in_specs=[blk, blk, par, par], out_specs=[blk, par, par],
        out_shape=(sd(g.shape, F32), sd((4, nj, 128), F32), sd((4, nj, 128), F32)),
        scratch_shapes=[pltpu.VMEM((4, nj, 128), F32)] * 4,
        compiler_params=_params(ndim=2),
    )(g, s, ar, ai)


def s5_bwd1(lam, dus, u, dp, h, dh, g, w_in, bdre, bdim, name):
    lp, d = h.shape
    tt = TOKEN_TILE
    nt = lp // tt
    cw, sw = bdre.shape[1], bdre.shape[2]
    gc = cw // S5_GROUP

    def body(lam_ref, dus_ref, u_ref, dpz_ref, h_ref, dh_ref, g_ref, w_hbm, bre_hbm, bim_hbm,
             dpu_ref, dho_ref, n_ref, dbre_ref, dbim_ref, dg_hbm, w, bre, bim, dbre, dbim, dg):
        i = pl.program_id(0)

        @pl.when(i == 0)
        def _():
            pltpu.sync_copy(w_hbm, w)
            pltpu.sync_copy(bre_hbm, bre)
            pltpu.sync_copy(bim_hbm, bim)
            dbre[...] = jnp.zeros_like(dbre)
            dbim[...] = jnp.zeros_like(dbim)
            dg[...] = jnp.zeros_like(dg)

        dz = dpz_ref[0]
        dn = None
        for c in range(4):
            chunk = slice(c * cw, (c + 1) * cw)
            (l_r, l_i), uv = _slab_load(lam_ref, c), u_ref[:, chunk]
            du = dus_ref[:, chunk] + _dot_nt(l_r, bre[c]) + _dot_nt(l_i, bim[c])
            dbre[c] += _dot_tn(uv, l_r)
            dbim[c] += _dot_tn(uv, l_i)
            dpu_ref[0, :, chunk] = du.astype(BF16)
            part = _dot_nt(du, w[c]) + _dot_nt(dz[:, chunk], w[4 + c])
            dn = part if c == 0 else dn + part
        gv = g_ref[...]
        n, hh, rr = _rms_fwd(h_ref[...], gv)
        n_ref[...] = n.T.astype(BF16)
        dg[...] += jnp.sum(dn * hh, axis=0, keepdims=True)
        dho_ref[...] = dh_ref[...] + _rms_bwd(dn, hh, rr, gv)

        @pl.when(i == nt - 1)
        def _():
            for k in range(4):
                for j in range(gc):
                    rows, cols = pl.ds(j * S5_GROUP, S5_GROUP), pl.ds(j * S5_STATE, S5_STATE)
                    dbre_ref[k, j] = dbre[k, rows, cols]
                    dbim_ref[k, j] = dbim[k, rows, cols]
            pltpu.sync_copy(dg, dg_hbm)

    sd = jax.ShapeDtypeStruct
    row = pl.BlockSpec((tt, d), lambda i: (i, 0))
    slab, _ = _slab_spec(lp, tt, sw)
    diag = pl.BlockSpec((4, gc, S5_GROUP, S5_STATE), lambda i: (0, 0, 0, 0))
    return pl.pallas_call(
        body, name=name, grid=(nt,),
        in_specs=[slab, row, row, pl.BlockSpec((1, tt, d), lambda i: (1, i, 0)), row, row, pl.BlockSpec((1, d), lambda i: (0, 0)),
                  ANY, ANY, ANY],
        out_specs=[pl.BlockSpec((1, tt, d), lambda i: (0, i, 0)), row, pl.BlockSpec((d, tt), lambda i: (0, i)), diag, diag, ANY],
        out_shape=(sd(dp.shape, BF16), sd((lp, d), F32), sd((d, lp), BF16),
                   sd((4, gc, S5_GROUP, S5_STATE), F32), sd((4, gc, S5_GROUP, S5_STATE), F32), sd((1, d), F32)),
        input_output_aliases={3: 0},
        scratch_shapes=[pltpu.VMEM(w_in.shape, BF16), pltpu.VMEM(bdre.shape, BF16), pltpu.VMEM(bdim.shape, BF16),
                        pltpu.VMEM(bdre.shape, F32), pltpu.VMEM(bdim.shape, F32), pltpu.VMEM((1, d), F32)],
        compiler_params=_params(),
    )(lam, dus, u, dp, h, dh, g, w_in, bdre, bdim)


def grad_w_in(n_t, dp, blk, name):
    d, lp = n_t.shape
    npart, _, width = dp.shape
    per = width // blk

    def body(n_ref, dp_ref, o_ref):
        o_ref[0] = jnp.dot(n_ref[...], dp_ref[0], preferred_element_type=F32).astype(o_ref.dtype)

    return pl.pallas_call(
        body, name=name, grid=(npart * per,),
        in_specs=[pl.BlockSpec((d, lp), lambda j: (0, 0), pipeline_mode=pl.Buffered(1)),
                  pl.BlockSpec((1, lp, blk), lambda j: (j // per, 0, j % per))],
        out_specs=pl.BlockSpec((1, d, blk), lambda j: (j, 0, 0)),
        out_shape=jax.ShapeDtypeStruct((npart * per, d, blk), BF16),
        compiler_params=_params(),
    )(n_t, dp)


def _conv_mix(cg, v, cw_ref, cb_ref, halo, c):
    hc = cg * v
    taps = cw_ref[c]
    conv = taps[2:3, :] * hc + taps[1:2, :] * _shift_down(hc, 1, halo) + taps[0:1, :] * _shift_down(hc, 2, halo) + cb_ref[c]
    return hc, conv


def conv_fwd(h, g, w_in, conv_w, conv_b, w_out, name):
    lp, d = h.shape
    tt = TOKEN_TILE
    nt = lp // tt
    nch, ce = w_out.shape[0], w_out.shape[1]

    def body(h_ref, g_ref, cw_ref, cb_ref, w_hbm, wo_hbm, o_ref, halo_ref, acts_ref, w, wo, halo):
        i = pl.program_id(0)

        @pl.when(i == 0)
        def _():
            pltpu.sync_copy(w_hbm, w)
            pltpu.sync_copy(wo_hbm, wo)
            halo[...] = jnp.zeros_like(halo)

        hv = h_ref[...]
        n = _rms_fwd(hv, g_ref[...])[0].astype(BF16)
        o = hv
        for c in range(nch):
            cols = slice(c * ce, (c + 1) * ce)
            bg, cg, v, z = [jnp.dot(n, w[p * nch + c], preferred_element_type=F32) for p in range(4)]
            for p, val in enumerate((bg, cg, v, z)):
                acts_ref[p, :, cols] = val.astype(BF16)
            hc, conv = _conv_mix(cg, v, cw_ref, cb_ref, halo[c], c)
            o = o + _dot(bg * conv * (z * _sigmoid(z)), wo[c])
            halo[c] = hc[tt - CONV_HALO:, :]
            halo_ref[0, c] = hc[tt - CONV_HALO:, :]
        o_ref[...] = o

    sd = jax.ShapeDtypeStruct
    return pl.pallas_call(
        body, name=name, grid=(nt,),
        in_specs=[pl.BlockSpec((tt, d), lambda i: (i, 0)), pl.BlockSpec((1, d), lambda i: (0, 0)),
                  pl.BlockSpec(conv_w.shape, lambda i: (0, 0, 0)), pl.BlockSpec(conv_b.shape, lambda i: (0, 0, 0)), ANY, ANY],
        out_specs=[pl.BlockSpec((tt, d), lambda i: (i, 0)), pl.BlockSpec((1, nch, CONV_HALO, ce), lambda i: (i, 0, 0, 0)),
                   pl.BlockSpec((4, tt, nch * ce), lambda i: (0, i, 0))],
        out_shape=(sd((lp, d), F32), sd((nt, nch, CONV_HALO, ce), F32), sd((4, lp, nch * ce), BF16)),
        scratch_shapes=[pltpu.VMEM(w_in.shape, BF16), pltpu.VMEM(w_out.shape, BF16), pltpu.VMEM((nch, CONV_HALO, ce), F32)],
        compiler_params=_params(),
    )(h, g, conv_w, conv_b, w_in, w_out)


def conv_bwd(h, dh, halos, acts, g, w_in, conv_w, conv_b, w_out, name):
    lp, d = h.shape
    tt = TOKEN_TILE
    nt = lp // tt
    nch, ce = w_out.shape[0], w_out.shape[1]

    def body(h_ref, dh_ref, halo_ref, acts_ref, g_ref, cw_ref, cb_ref, w_hbm, wo_hbm,
             dho_ref, n_ref, dp_ref, dwo_hbm, dcw_hbm, dcb_hbm, dg_hbm, w, wo, nxt, dwo, dcw, dcb, dg):
        i = pl.program_id(0)

        @pl.when(i == 0)
        def _():
            pltpu.sync_copy(w_hbm, w)
            pltpu.sync_copy(wo_hbm, wo)
            for ref in (nxt, dwo, dcw, dcb, dg):
                ref[...] = jnp.zeros_like(ref)

        gv = g_ref[...]
        nf, hh, rr = _rms_fwd(h_ref[...], gv)
        n_ref[...] = nf.T.astype(BF16)
        dhv = dh_ref[...]
        has_prev = (i < nt - 1).astype(F32)
        dn = jnp.zeros((tt, d), F32)
        for c in range(nch):
            halo = halo_ref[0, c] * has_prev
            cols = slice(c * ce, (c + 1) * ce)
            bg, cg, v, z = [acts_ref[p, :, cols].astype(F32) for p in range(4)]
            hc, conv = _conv_mix(cg, v, cw_ref, cb_ref, halo, c)
            sz, dsz = _silu_and_grad(z)
            y1 = bg * conv
            dy2 = _dot_nt(dhv, wo[c])
            dwo[c] += _dot_tn(y1 * sz, dhv)
            dy1 = dy2 * sz
            dz = dy2 * y1 * dsz
            dbg = dy1 * conv
            dconv = dy1 * bg
            dcb[c] += jnp.sum(dconv, axis=0, keepdims=True)
            up1 = _shift_up(dconv, 1, nxt[c])
            up2 = _shift_up(dconv, 2, nxt[c])
            nxt[c] = dconv[:CONV_HALO, :]
            taps = cw_ref[c]
            dhc = taps[2:3, :] * dconv + taps[1:2, :] * up1 + taps[0:1, :] * up2
            dcw[c, 0:1, :] += jnp.sum(hc * up2, axis=0, keepdims=True)
            dcw[c, 1:2, :] += jnp.sum(hc * up1, axis=0, keepdims=True)
            dcw[c, 2:3, :] += jnp.sum(hc * dconv, axis=0, keepdims=True)
            dcg = dhc * v
            dv = dhc * cg
            for p, val in enumerate((dbg, dcg, dv, dz)):
                dp_ref[p, :, cols] = val.astype(BF16)
                dn = dn + _dot_nt(val, w[p * nch + c])
        dg[...] += jnp.sum(dn * hh, axis=0, keepdims=True)
        dho_ref[...] = dhv + _rms_bwd(dn, hh, rr, gv)

        @pl.when(i == nt - 1)
        def _():
            pltpu.sync_copy(dwo, dwo_hbm)
            pltpu.sync_copy(dcw, dcw_hbm)
            pltpu.sync_copy(dcb, dcb_hbm)
            pltpu.sync_copy(dg, dg_hbm)

    rev = lambda i: (nt - 1 - i, 0)
    sd = jax.ShapeDtypeStruct
    return pl.pallas_call(
        body, name=name, grid=(nt,),
        in_specs=[pl.BlockSpec((tt, d), rev), pl.BlockSpec((tt, d), rev),
                  pl.BlockSpec((1, nch, CONV_HALO, ce), lambda i: (jnp.maximum(nt - 2 - i, 0), 0, 0, 0)),
                  pl.BlockSpec((4, tt, nch * ce), lambda i: (0, nt - 1 - i, 0)),
                  pl.BlockSpec((1, d), lambda i: (0, 0)),
                  pl.BlockSpec(conv_w.shape, lambda i: (0, 0, 0)), pl.BlockSpec(conv_b.shape, lambda i: (0, 0, 0)), ANY, ANY],
        out_specs=[pl.BlockSpec((tt, d), rev), pl.BlockSpec((d, tt), lambda i: (0, nt - 1 - i)),
                   pl.BlockSpec((4, tt, nch * ce), lambda i: (0, nt - 1 - i, 0)), ANY, ANY, ANY, ANY],
        out_shape=(sd((lp, d), F32), sd((d, lp), BF16), sd((4, lp, nch * ce), BF16),
                   sd(w_out.shape, F32), sd((nch, 8, ce), F32), sd((nch, 1, ce), F32), sd((1, d), F32)),
        scratch_shapes=[pltpu.VMEM(w_in.shape, BF16), pltpu.VMEM(w_out.shape, BF16), pltpu.VMEM((nch, CONV_HALO, ce), F32),
                        pltpu.VMEM(w_out.shape, F32), pltpu.VMEM((nch, 8, ce), F32), pltpu.VMEM((nch, 1, ce), F32),
                        pltpu.VMEM((1, d), F32)],
        compiler_params=_params(vmem=VMEM_LIMIT_LARGE),
    )(h, dh, halos, acts, g, conv_w, conv_b, w_in, w_out)


def _pool_mix(u, wg, bg_ref, sc_ref, halo, k, tile, tt, first_pos):
    ext = jnp.concatenate([halo, u], axis=0)
    win = _window_sums_back(ext)[k][POOL_HALO:, :]
    mixed = win * _pool_inv_count(tile, tt, first_pos, POOL_WINDOWS[k], u.shape[1]) - u
    outs = _dot(mixed, wg[k]) + bg_ref[k]
    return mixed, outs, outs * sc_ref[k]


def pool_fwd(h, g, w_in, w_grp, b_grp, scale, w_out, first_pos, name):
    lp, d = h.shape
    tt = TOKEN_TILE
    nt = lp // tt
    gw = w_grp.shape[1]

    def body(h_ref, g_ref, bg_ref, sc_ref, w_hbm, wg_hbm, wo_hbm, o_ref, halo_ref, acts_ref, w, wg, wo, halo):
        i = pl.program_id(0)

        @pl.when(i == 0)
        def _():
            pltpu.sync_copy(w_hbm, w)
            pltpu.sync_copy(wg_hbm, wg)
            pltpu.sync_copy(wo_hbm, wo)
            halo[...] = jnp.zeros_like(halo)

        hv = h_ref[...]
        n = _rms_fwd(hv, g_ref[...])[0].astype(BF16)
        o = hv
        for k in range(4):
            cols = slice(k * gw, (k + 1) * gw)
            u = jnp.dot(n, w[k], preferred_element_type=F32)
            z = jnp.dot(n, w[4 + k], preferred_element_type=F32)
            acts_ref[0, :, cols] = u.astype(BF16)
            acts_ref[1, :, cols] = z.astype(BF16)
            _, _, yp = _pool_mix(u, wg, bg_ref, sc_ref, halo[k], k, i, tt, first_pos)
            o = o + _dot(yp * (z * _sigmoid(z)), wo[k])
            halo[k] = u[tt - POOL_HALO:, :]
            halo_ref[0, k] = u[tt - POOL_HALO:, :]
        o_ref[...] = o

    sd = jax.ShapeDtypeStruct
    small = pl.BlockSpec((4, 1, gw), lambda i: (0, 0, 0))
    return pl.pallas_call(
        body, name=name, grid=(nt,),
        in_specs=[pl.BlockSpec((tt, d), lambda i: (i, 0)), pl.BlockSpec((1, d), lambda i: (0, 0)), small, small, ANY, ANY, ANY],
        out_specs=[pl.BlockSpec((tt, d), lambda i: (i, 0)), pl.BlockSpec((1, 4, POOL_HALO, gw), lambda i: (i, 0, 0, 0)),
                   pl.BlockSpec((2, tt, 4 * gw), lambda i: (0, i, 0))],
        out_shape=(sd((lp, d), F32), sd((nt, 4, POOL_HALO, gw), F32), sd((2, lp, 4 * gw), BF16)),
        scratch_shapes=[pltpu.VMEM(w_in.shape, BF16), pltpu.VMEM(w_grp.shape, BF16), pltpu.VMEM(w_out.shape, BF16),
                        pltpu.VMEM((4, POOL_HALO, gw), F32)],
        compiler_params=_params(),
    )(h, g, b_grp, scale, w_in, w_grp, w_out)


def pool_bwd(h, dh, halos, acts, g, w_in, w_grp, b_grp, scale, w_out, first_pos, name):
    lp, d = h.shape
    tt = TOKEN_TILE
    nt = lp // tt
    gw = w_grp.shape[1]

    def body(h_ref, dh_ref, halo_ref, acts_ref, g_ref, bg_ref, sc_ref, w_hbm, wg_hbm, wo_hbm,
             dho_ref, n_ref, dp_ref, dwo_hbm, dwg_hbm, dbg_hbm, dsc_hbm, dg_hbm,
             w, wg, wo, nxt, dwo, dwg, dbg, dsc, dg):
        i = pl.program_id(0)
        tile = nt - 1 - i

        @pl.when(i == 0)
        def _():
            pltpu.sync_copy(w_hbm, w)
            pltpu.sync_copy(wg_hbm, wg)
            pltpu.sync_copy(wo_hbm, wo)
            for ref in (nxt, dwo, dwg, dbg, dsc, dg):
                ref[...] = jnp.zeros_like(ref)

        gv = g_ref[...]
        nf, hh, rr = _rms_fwd(h_ref[...], gv)
        n_ref[...] = nf.T.astype(BF16)
        dhv = dh_ref[...]
        has_prev = (i < nt - 1).astype(F32)
        dn = jnp.zeros((tt, d), F32)
        for k in range(4):
            cols = slice(k * gw, (k + 1) * gw)
            u, z = acts_ref[0, :, cols].astype(F32), acts_ref[1, :, cols].astype(F32)
            mixed, outs, yp = _pool_mix(u, wg, bg_ref, sc_ref, halo_ref[0, k] * has_prev, k, tile, tt, first_pos)
            sz, dsz = _silu_and_grad(z)
            dy = _dot_nt(dhv, wo[k])
            dwo[k] += _dot_tn(yp * sz, dhv)
            dyp = dy * sz
            dz = dy * yp * dsz
            dsc[k] += jnp.sum(dyp * outs, axis=0, keepdims=True)
            douts = dyp * sc_ref[k]
            dbg[k] += jnp.sum(douts, axis=0, keepdims=True)
            dwg[k] += _dot_tn(mixed, douts)
            dmixed = _dot_nt(douts, wg[k])
            dm = dmixed * _pool_inv_count(tile, tt, first_pos, POOL_WINDOWS[k], gw)
            ext = jnp.concatenate([dm, nxt[k]], axis=0)
            du = _window_sums_fwd(ext)[k][:tt, :] - dmixed
            nxt[k] = dm[:POOL_HALO, :]
            dp_ref[0, :, cols] = du.astype(BF16)
            dp_ref[1, :, cols] = dz.astype(BF16)
            dn = dn + _dot_nt(du, w[k]) + _dot_nt(dz, w[4 + k])
        dg[...] += jnp.sum(dn * hh, axis=0, keepdims=True)
        dho_ref[...] = dhv + _rms_bwd(dn, hh, rr, gv)

        @pl.when(i == nt - 1)
        def _():
            pltpu.sync_copy(dwo, dwo_hbm)
            pltpu.sync_copy(dwg, dwg_hbm)
            pltpu.sync_copy(dbg, dbg_hbm)
            pltpu.sync_copy(dsc, dsc_hbm)
            pltpu.sync_copy(dg, dg_hbm)

    rev = lambda i: (nt - 1 - i, 0)
    sd = jax.ShapeDtypeStruct
    small = pl.BlockSpec((4, 1, gw), lambda i: (0, 0, 0))
    return pl.pallas_call(
        body, name=name, grid=(nt,),
        in_specs=[pl.BlockSpec((tt, d), rev), pl.BlockSpec((tt, d), rev),
                  pl.BlockSpec((1, 4, POOL_HALO, gw), lambda i: (jnp.maximum(nt - 2 - i, 0), 0, 0, 0)),
                  pl.BlockSpec((2, tt, 4 * gw), lambda i: (0, nt - 1 - i, 0)),
                  pl.BlockSpec((1, d), lambda i: (0, 0)), small, small, ANY, ANY, ANY],
        out_specs=[pl.BlockSpec((tt, d), rev), pl.BlockSpec((d, tt), lambda i: (0, nt - 1 - i)),
                   pl.BlockSpec((2, tt, 4 * gw), lambda i: (0, nt - 1 - i, 0)), ANY, ANY, ANY, ANY, ANY],
        out_shape=(sd((lp, d), F32), sd((d, lp), BF16), sd((2, lp, 4 * gw), BF16),
                   sd(w_out.shape, F32), sd(w_grp.shape, F32), sd((4, 1, gw), F32), sd((4, 1, gw), F32), sd((1, d), F32)),
        scratch_shapes=[pltpu.VMEM(w_in.shape, BF16), pltpu.VMEM(w_grp.shape, BF16), pltpu.VMEM(w_out.shape, BF16),
                        pltpu.VMEM((4, POOL_HALO, gw), F32), pltpu.VMEM(w_out.shape, F32), pltpu.VMEM(w_grp.shape, F32),
                        pltpu.VMEM((4, 1, gw), F32), pltpu.VMEM((4, 1, gw), F32), pltpu.VMEM((1, d), F32)],
        compiler_params=_params(),
    )(h, dh, halos, acts, g, b_grp, scale, w_in, w_grp, w_out)


def loss_head(h, target, g, pad_tiles, name):
    lp, d = h.shape
    tt = TOKEN_TILE
    nt = lp // tt

    def body(h_ref, t_ref, g_ref, dh_ref, dg_ref, loss_ref, acc):
        i = pl.program_id(0)

        @pl.when(i == 0)
        def _():
            acc[...] = jnp.zeros_like(acc)
            dg_ref[...] = jnp.zeros_like(dg_ref)

        @pl.when(i < pad_tiles)
        def _():
            dh_ref[...] = jnp.zeros_like(dh_ref)

        @pl.when(i >= pad_tiles)
        def _():
            gv = g_ref[...]
            n, hh, rr = _rms_fwd(h_ref[...], gv)
            err = n - t_ref[...]
            acc[...] += 0.5 * jnp.sum(jnp.mean(err * err, axis=-1, keepdims=True), axis=0, keepdims=True)
            dn = err * (1.0 / d)
            dg_ref[...] += jnp.sum(dn * hh, axis=0, keepdims=True)
            dh_ref[...] = _rms_bwd(dn, hh, rr, gv)

        loss_ref[...] = jnp.broadcast_to(acc[...], loss_ref.shape)

    sd = jax.ShapeDtypeStruct
    return pl.pallas_call(
        body, name=name, grid=(nt,),
        in_specs=[pl.BlockSpec((tt, d), lambda i: (i, 0)), pl.BlockSpec((tt, d), lambda i: (jnp.maximum(i - pad_tiles, 0), 0)),
                  pl.BlockSpec((1, d), lambda i: (0, 0))],
        out_specs=[pl.BlockSpec((tt, d), lambda i: (i, 0)), pl.BlockSpec((1, d), lambda i: (0, 0)),
                   pl.BlockSpec((8, 128), lambda i: (0, 0))],
        out_shape=(sd((lp, d), F32), sd((1, d), F32), sd((8, 128), F32)),
        scratch_shapes=[pltpu.VMEM((1, 1), F32)],
        compiler_params=_params(),
    )(h, target, g)


def exchange(arrs, gather, name):
    n = len(arrs)

    def body(*refs):
        ins, outs = refs[:n], refs[n:2 * n]
        send_sems, recv_sems, own_sems = refs[2 * n:]
        x, y, c = lax.axis_index("x"), lax.axis_index("y"), lax.axis_index("c")
        me = 4 * x + 2 * y + c
        own = []
        for a in range(n):
            cp = pltpu.make_async_copy(ins[a] if gather else ins[a].at[me], outs[a].at[me], own_sems.at[a])
            cp.start()
            own.append(cp)
        sent = []
        for k in range(1, N_DEV):
            px = 1 - x if k & 4 else x
            py = 1 - y if k & 2 else y
            pc = 1 - c if k & 1 else c
            peer = 4 * px + 2 * py + pc
            for a in range(n):
                cp = pltpu.make_async_remote_copy(
                    src_ref=ins[a] if gather else ins[a].at[peer], dst_ref=outs[a].at[me],
                    send_sem=send_sems.at[a, k - 1], recv_sem=recv_sems.at[a, k - 1],
                    device_id=(px, py, pc), device_id_type=pl.DeviceIdType.MESH)
                cp.start()
                sent.append((cp, a, k, peer, (px, py, pc)))
        for cp, a, k, peer, pid in sent:
            cp.wait_send()
            pltpu.make_async_remote_copy(
                src_ref=ins[a] if gather else ins[a].at[peer], dst_ref=outs[a].at[peer],
                send_sem=send_sems.at[a, k - 1], recv_sem=recv_sems.at[a, k - 1],
                device_id=pid, device_id_type=pl.DeviceIdType.MESH).wait_recv()
        for cp in own:
            cp.wait()

    hbm = pl.BlockSpec(memory_space=pltpu.HBM)
    out_shape = tuple(jax.ShapeDtypeStruct(((N_DEV,) + a.shape) if gather else a.shape, a.dtype) for a in arrs)
    return pl.pallas_call(
        body, name=name, in_specs=[hbm] * n, out_specs=[hbm] * n, out_shape=out_shape,
        scratch_shapes=[pltpu.SemaphoreType.DMA((n, N_DEV - 1)), pltpu.SemaphoreType.DMA((n, N_DEV - 1)),
                        pltpu.SemaphoreType.DMA((n,))],
    )(*[pltpu.with_memory_space_constraint(a, pltpu.HBM) for a in arrs])


def _peers(x, y, c):
    out = []
    for k in range(1, N_DEV):
        px = 1 - x if k & 4 else x
        py = 1 - y if k & 2 else y
        pc = 1 - c if k & 1 else c
        out.append((k, (px, py, pc), 4 * px + 2 * py + pc))
    return out


def exchange_start(arrs, gather, after, name):
    n = len(arrs)
    me = 4 * lax.axis_index("x") + 2 * lax.axis_index("y") + lax.axis_index("c")
    lands = []
    for a in arrs:
        own = a[None] if gather else lax.dynamic_index_in_dim(a, me, 0, keepdims=True)
        lands.append(lax.dynamic_update_index_in_dim(lax.empty(((N_DEV,) + a.shape) if gather else a.shape, a.dtype), own, me, 0))

    def body(*refs):
        ins, land = refs[:n], refs[n:2 * n]
        send_sems, recv_sems, token = refs[2 * n + 1], refs[2 * n + 2], refs[4 * n + 3]
        x, y, c = lax.axis_index("x"), lax.axis_index("y"), lax.axis_index("c")
        me = 4 * x + 2 * y + c
        for k, pid, peer in _peers(x, y, c):
            for a in range(n):
                pltpu.make_async_remote_copy(
                    src_ref=ins[a] if gather else ins[a].at[peer], dst_ref=land[a].at[me],
                    send_sem=send_sems.at[a * (N_DEV - 1) + k - 1], recv_sem=recv_sems.at[a * (N_DEV - 1) + k - 1],
                    device_id=pid, device_id_type=pl.DeviceIdType.MESH).start()
        token[...] = jnp.zeros_like(token)

    hbm = pl.BlockSpec(memory_space=pltpu.HBM)
    sem = pl.BlockSpec(memory_space=pltpu.SEMAPHORE)
    sems = pltpu.SemaphoreType.DMA((n * (N_DEV - 1),))
    res = pl.pallas_call(
        body, name=name, in_specs=[hbm] * (2 * n) + [ANY],
        out_specs=[sem, sem] + [hbm] * (2 * n) + [pl.BlockSpec(memory_space=pltpu.VMEM)],
        out_shape=[sems, sems] + [pltpu.HBM(a.shape, a.dtype) for a in arrs] + [pltpu.HBM(l.shape, l.dtype) for l in lands]
        + [jax.ShapeDtypeStruct((8, 128), F32)],
        input_output_aliases={a: 2 + a for a in range(2 * n)},
        compiler_params=pltpu.CompilerParams(has_side_effects=pltpu.SideEffectType.DATAFLOW_SIDE_EFFECTING),
    )(*[pltpu.with_memory_space_constraint(a, pltpu.HBM) for a in list(arrs) + lands], after)
    return res[0], res[1], res[2:2 + n], res[2 + n:2 + 2 * n], res[-1]


def exchange_wait(started, gather, after, name):
    send_sems, recv_sems, srcs, lands, _ = started
    n = len(srcs)
    after = list(after) if isinstance(after, (list, tuple)) else [after]

    def body(*refs):
        ins, land = refs[:n], refs[n:2 * n]
        send_sems, recv_sems = refs[2 * n], refs[2 * n + 1]
        x, y, c = lax.axis_index("x"), lax.axis_index("y"), lax.axis_index("c")
        for k, pid, peer in _peers(x, y, c):
            for a in range(n):
                cp = pltpu.make_async_remote_copy(
                    src_ref=ins[a] if gather else ins[a].at[peer], dst_ref=land[a].at[peer],
                    send_sem=send_sems.at[a * (N_DEV - 1) + k - 1], recv_sem=recv_sems.at[a * (N_DEV - 1) + k - 1],
                    device_id=pid, device_id_type=pl.DeviceIdType.MESH)
                cp.wait_send()
                cp.wait_recv()

    hbm = pl.BlockSpec(memory_space=pltpu.HBM)
    sem = pl.BlockSpec(memory_space=pltpu.SEMAPHORE)
    res = pl.pallas_call(
        body, name=name, in_specs=[hbm] * (2 * n) + [sem, sem] + [ANY] * len(after),
        out_specs=[hbm] * (2 * n),
        out_shape=[pltpu.HBM(a.shape, a.dtype) for a in list(srcs) + list(lands)],
        input_output_aliases={a: a for a in range(2 * n)},
        compiler_params=pltpu.CompilerParams(has_side_effects=pltpu.SideEffectType.DATAFLOW_SIDE_EFFECTING),
    )(*srcs, *lands, send_sems, recv_sems, *after)
    return res[n:]


def _adamw(w, g, m, v):
    m = ADAM_B1 * m + (1.0 - ADAM_B1) * g
    v = ADAM_B2 * v + (1.0 - ADAM_B2) * (g * g)
    m_hat = m / (1.0 - ADAM_B1 ** ADAM_STEP)
    v_hat = v / (1.0 - ADAM_B2 ** ADAM_STEP)
    return -ADAM_LR * (m_hat / (jnp.sqrt(v_hat) + ADAM_EPS) + ADAM_WD * w), m, v


def _update_tile_rows(rows, cols):
    if rows * cols <= UPDATE_TILE_ELEMS:
        return rows
    return max(t for t in range(8, UPDATE_TILE_ELEMS // cols + 1, 8) if rows % t == 0)


def _sum_in_order(p_ref):
    g = p_ref[0].astype(F32)
    for j in range(1, p_ref.shape[0]):
        g = g + p_ref[j].astype(F32)
    return g


def sum_parts(parts, name):
    nparts, rows, cols = parts.shape
    tr = _update_tile_rows(rows, cols)

    def body(p_ref, g_ref):
        g_ref[...] = _sum_in_order(p_ref)

    return pl.pallas_call(
        body, name=name, grid=(rows // tr,),
        in_specs=[pl.BlockSpec((nparts, tr, cols), lambda i: (0, i, 0))],
        out_specs=pl.BlockSpec((tr, cols), lambda i: (i, 0)), out_shape=jax.ShapeDtypeStruct((rows, cols), F32),
        compiler_params=_params(),
    )(parts)


def sum_adamw(parts, w, m, v, name):
    rows, cols = w.shape
    nparts = parts.shape[0]
    tr = _update_tile_rows(rows, cols)

    def body(p_ref, w_ref, m_ref, v_ref, g_ref, d_ref, nm_ref, nv_ref):
        g = _sum_in_order(p_ref)
        delta, nm, nv = _adamw(w_ref[...], g, m_ref[...], v_ref[...])
        g_ref[...] = g
        d_ref[...] = delta
        nm_ref[...] = nm
        nv_ref[...] = nv

    blk = pl.BlockSpec((tr, cols), lambda i: (i, 0))
    sd = jax.ShapeDtypeStruct((rows, cols), F32)
    return pl.pallas_call(
        body, name=name, grid=(rows // tr,),
        in_specs=[pl.BlockSpec((nparts, tr, cols), lambda i: (0, i, 0)), blk, blk, blk],
        out_specs=[blk] * 4, out_shape=(sd,) * 4,
        compiler_params=_params(),
    )(parts, w, m, v)


def update_packed(g, w, m, v, pieces, name):
    rows_all = w.shape[0]

    def body(g_ref, w_ref, m_ref, v_ref, *outs):
        gv = g_ref[:rows_all, :]
        res = (gv,) + _adamw(w_ref[...], gv, m_ref[...], v_ref[...])
        for p, (row, rows, lanes) in enumerate(pieces):
            for k in range(4):
                outs[4 * p + k][...] = res[k][row:row + rows, :lanes]

    shapes = [jax.ShapeDtypeStruct((rows, lanes), F32) for _, rows, lanes in pieces for _ in range(4)]
    return pl.pallas_call(body, name=name, out_shape=shapes,
                          compiler_params=pltpu.CompilerParams(vmem_limit_bytes=VMEM_LIMIT))(g, w, m, v)


def update_natural(groups, name):
    n = len(groups)
    steps = 8

    def body(*refs):
        ins, outs = refs[:4 * n], refs[4 * n:]
        for j in range(n):
            g_ref, w_ref, m_ref, v_ref = ins[4 * j:4 * j + 4]
            gv = g_ref[...]
            res = (gv,) + _adamw(w_ref[...], gv, m_ref[...], v_ref[...])
            for k in range(4):
                outs[4 * j + k][...] = res[k]

    specs, shapes = [], []
    for g, w, m, v in groups:
        rows, cols = w.shape
        specs += [pl.BlockSpec((rows // steps, cols), lambda i: (i, 0))] * 4
        shapes += [jax.ShapeDtypeStruct((rows, cols), F32)] * 4
    return pl.pallas_call(body, name=name, grid=(steps,), in_specs=specs, out_specs=specs, out_shape=shapes,
                          compiler_params=_params())(*[a for grp in groups for a in grp])


S5_NAMES = ("w_in", "lam_re", "lam_im", "log_dt", "b_re", "b_im", "c_re", "c_im", "d_skip", "w_glu", "b_glu", "w_out")
CONV_NAMES = ("w_in", "conv_w", "conv_b", "w_out")
POOL_NAMES = ("w_in", "w_grp", "b_grp", "scale", "w_out")
LAYER_KINDS = ("s5", "conv", "pool", "s5")
LAYER_NAMES = {"s5": S5_NAMES, "conv": CONV_NAMES, "pool": POOL_NAMES}
SHARDED = {"s5": ("w_in", "w_glu", "w_out"), "conv": ("w_in", "conv_w", "w_out"), "pool": ("w_in", "w_grp", "b_grp", "w_out")}
GATHER_F32 = ("conv_w", "b_grp")


def weight_names():
    names = ["meta_tokens"]
    for i, kind in enumerate(LAYER_KINDS):
        names.append("norm%d_g" % i)
        names += ["l%d_%s" % (i, n) for n in LAYER_NAMES[kind]]
    names.append("final_g")
    return names


def sharded_names():
    return ["meta_tokens"] + ["l%d_%s" % (i, n) for i, kind in enumerate(LAYER_KINDS) for n in SHARDED[kind]]


def _block_diag_in(bb_t, gc):
    i, g, p = bb_t.shape
    t = bb_t.astype(BF16).reshape(i, 4, gc, p)
    return jnp.einsum("icjp,jk->cjikp", t, jnp.eye(gc, dtype=BF16)).reshape(4, gc * i, gc * p)


def _block_diag_in_grad(blocks):
    _, gc, i, p = blocks.shape
    return jnp.transpose(blocks, (2, 0, 1, 3)).reshape(i, 4 * gc, p)


def _block_diag_out(cc, gc):
    g, i, p = cc.shape
    return jnp.einsum("cjip,jk->cjpki", cc.astype(BF16).reshape(4, gc, i, p), jnp.eye(gc, dtype=BF16)).reshape(4, gc * p, gc * i)


def _block_diag_out_grad(blocks):
    _, gc, i, p = blocks.shape
    return blocks.reshape(4 * gc, i, p)


def _to_owner_blocks(a, axis):
    shape = a.shape[:axis] + (N_DEV, a.shape[axis] // N_DEV) + a.shape[axis + 1:]
    return jnp.moveaxis(a.reshape(shape), axis, 0)


def _from_owner_blocks(a, axis):
    a = jnp.moveaxis(a, 0, axis)
    return a.reshape(a.shape[:axis] + (a.shape[axis] * a.shape[axis + 1],) + a.shape[axis + 2:])


def _step(x, target, weights, moments_m, moments_v):
    seq, d = x.shape[1], x.shape[2]
    n_meta = weights["meta_tokens"].shape[0]
    tt = TOKEN_TILE
    pad_tiles = -(-n_meta // tt)
    p0 = pad_tiles * tt
    lp = p0 + seq
    first_pos = p0 - n_meta
    gc = d // 4 // S5_GROUP
    cw = d // 4

    big_names = [n for n in sharded_names() if n != "meta_tokens" and n.split("_", 1)[1] not in GATHER_F32]
    small_names = [n for n in sharded_names() if n not in big_names]
    layer_big = [[n for n in big_names if n.startswith("l%d_" % i)] for i in range(len(LAYER_KINDS))]
    layer_big[0] = small_names + layer_big[0]
    gather_started = []
    after = jnp.zeros((8, 128), F32)
    for i, names in enumerate(layer_big):
        gather_started.append(exchange_start([weights[n] if n in small_names else weights[n].astype(BF16) for n in names], True,
                                             after, "gather_start_l%d" % i))
        after = gather_started[-1][4]

    def vec(name):
        return weights[name].reshape(1, -1)

    s5_prep = {}
    for i, kind in enumerate(LAYER_KINDS):
        if kind == "s5":
            p = "l%d_" % i
            lr, li = weights[p + "lam_re"], weights[p + "lam_im"] + after[0, 0]
            ldt = weights[p + "log_dt"].reshape(-1, 1)
            br_t = jnp.transpose(weights[p + "b_re"], (2, 0, 1))
            bi_t = jnp.transpose(weights[p + "b_im"], (2, 0, 1))
            ar, ai, bbr, bbi = s5_disc_fwd(lr, li, ldt, br_t, bi_t, p + "disc_fwd")
            s5_prep[i] = dict(
                disc=(lr, li, ldt, br_t, bi_t), ar=ar.reshape(4, -1, 128), ai=ai.reshape(4, -1, 128),
                bdre=_block_diag_in(bbr, gc), bdim=_block_diag_in(bbi, gc),
                cdre=_block_diag_out(weights[p + "c_re"], gc), cdim=_block_diag_out(-weights[p + "c_im"], gc),
                d_skip=weights[p + "d_skip"].reshape(4, 1, cw), b_glu=vec(p + "b_glu"))
    h = jnp.concatenate([jnp.zeros((p0, d), F32), x[0] + after[0, 0]], axis=0)

    prepared = [h] + [s5_prep[i][k] for i in s5_prep for k in ("bdre", "bdim", "cdre", "cdim")]
    gathered = dict(zip(layer_big[0], exchange_wait(gather_started[0], True, prepared, "gather_wait_l0")))
    h = lax.dynamic_update_slice(h, _from_owner_blocks(gathered["meta_tokens"], 1), (first_pos, 0))

    full = {}

    def layer_weights(i, kind, after):
        p = "l%d_" % i
        if i > 0:
            gathered.update(zip(layer_big[i], exchange_wait(gather_started[i], True, after, "gather_wait_l%d" % i)))
        w_in = gathered[p + "w_in"]
        if kind == "s5":
            full[i] = dict(s5_prep[i], w_in=w_in, w_glu=gathered[p + "w_glu"].reshape(4, cw, d),
                           w_out=gathered[p + "w_out"].reshape(4, cw, d))
        elif kind == "conv":
            ce = w_in.shape[2]
            nch = 2
            conv_w = _from_owner_blocks(gathered[p + "conv_w"], 1)
            full[i] = dict(
                w_in=w_in, conv_w=jnp.transpose(conv_w.reshape(CONV_K, nch, ce), (1, 0, 2)),
                conv_b=weights[p + "conv_b"].reshape(nch, 1, ce), w_out=gathered[p + "w_out"].reshape(nch, ce, d))
        else:
            gw = w_in.shape[2]
            full[i] = dict(
                w_in=w_in, w_grp=_from_owner_blocks(gathered[p + "w_grp"], 1),
                b_grp=_from_owner_blocks(gathered[p + "b_grp"], 1).reshape(4, 1, gw),
                scale=weights[p + "scale"].reshape(4, 1, gw), w_out=gathered[p + "w_out"].reshape(4, gw, d))
        return full[i]

    saved = {}
    for i, kind in enumerate(LAYER_KINDS):
        p, f, g = "l%d_" % i, layer_weights(i, kind, h), vec("norm%d_g" % i)
        if kind == "s5":
            u, z, xs = s5_fwd1(h, g, f["w_in"], f["bdre"], f["bdim"], p + "fwd_in")
            s = s5_scan_fwd(xs, f["ar"], f["ai"], p + "scan_fwd")
            h_in = h
            h, y, q = s5_fwd3(s, u, z, h, f["cdre"], f["cdim"], f["w_glu"], f["w_out"], f["d_skip"], f["b_glu"], p + "fwd_out")
            saved[i] = (h_in, u, z, s, y, q)
        elif kind == "conv":
            h_new, halos, acts = conv_fwd(h, g, f["w_in"], f["conv_w"], f["conv_b"], f["w_out"], p + "fwd")
            saved[i] = (h, halos, acts)
            h = h_new
        else:
            h_new, halos, acts = pool_fwd(h, g, f["w_in"], f["w_grp"], f["b_grp"], f["scale"], f["w_out"], first_pos, p + "fwd")
            saved[i] = (h, halos, acts)
            h = h_new

    dh, dg_final, loss_tile = loss_head(h, target[0], vec("final_g"), pad_tiles, "loss_head")
    loss = lax.psum(loss_tile[0, 0], ("x", "y", "c"))

    grads = {"final_g": dg_final}
    names = weight_names()
    sh_names = sharded_names()
    replicated = [n for n in names if n not in sh_names]
    vectors = [n for n in replicated if weights[n].ndim == 1]
    matrices = [n for n in replicated if weights[n].ndim > 1]
    rep_names = vectors + matrices

    def owner_blocks(a):
        return a.reshape(N_DEV, -1, a.shape[-1]).astype(BF16)

    def as2d(a):
        return a.reshape(-1, a.shape[-1])

    def pack(tree, which=None):
        flat = [jnp.pad(tree[n].reshape(-1), (0, -tree[n].size % 1024)) for n in (which or rep_names)]
        flat = jnp.concatenate(flat)
        if which is None:
            flat = jnp.pad(flat, (0, -flat.size % (PACK_ROWS * 128)))
        return flat.reshape(-1, 128)

    layer_sharded, scatter_started = {}, {}
    ordered = jnp.zeros((), F32)
    for i in reversed(range(len(LAYER_KINDS))):
        kind = LAYER_KINDS[i]
        p, f, g = "l%d_" % i, full[i], vec("norm%d_g" % i) + ordered
        if kind == "s5":
            h_in, u, z, s, y, q = saved[i]
            dy, dp, dwo, dwg, dbg = s5_bwd3a(dh, y, q, z, f["w_glu"], f["w_out"], f["b_glu"] + ordered, p + "bwd_out")
            d_skip = f["d_skip"]
            if i == 0:
                early_names = [p + "w_glu", p + "w_out"]
                scatter_started["early"] = exchange_start([owner_blocks(dwg), owner_blocks(dwo)], False, dy,
                                                          "scatter_start_l0_early")
                d_skip = d_skip + scatter_started["early"][4][0, 0]
            ds, dus, dcre, dcim, dd = s5_bwd3b(dy, s, u, f["cdre"], f["cdim"], d_skip, p + "bwd_read")
            lam, dar, dai = s5_scan_bwd(ds, s, f["ar"], f["ai"], p + "scan_bwd")
            dp, dh, n, dbre, dbim, dg = s5_bwd1(lam, dus, u, dp, h_in, dh, g, f["w_in"], f["bdre"], f["bdim"], p + "bwd_in")
            dw_in = grad_w_in(n, dp, f["w_in"].shape[2], p + "grad_w_in")
            grads.update({p + "w_in": dw_in, p + "w_glu": dwg.reshape(N_DEV, -1, d), p + "w_out": dwo.reshape(N_DEV, -1, d),
                          p + "d_skip": dd, p + "b_glu": dbg})

            def replicated_grads(p=p, f=f, dar=dar, dai=dai, dbre=dbre, dbim=dbim, dcre=dcre, dcim=dcim, token=None):
                lr, li, ldt, br_t, bi_t = f["disc"]
                dlr, dli, dldt, dbr_t, dbi_t = s5_disc_bwd(
                    lr, li, ldt, br_t, bi_t, dar.reshape(lr.shape) + token, dai.reshape(lr.shape),
                    _block_diag_in_grad(dbre), _block_diag_in_grad(dbim), p + "disc_bwd")
                grads.update({
                    p + "lam_re": dlr, p + "lam_im": dli, p + "log_dt": dldt,
                    p + "b_re": jnp.transpose(dbr_t, (1, 2, 0)), p + "b_im": jnp.transpose(dbi_t, (1, 2, 0)),
                    p + "c_re": _block_diag_out_grad(dcre), p + "c_im": -_block_diag_out_grad(dcim)})
        elif kind == "conv":
            replicated_grads = None
            h_in, halos, acts = saved[i]
            dh, n, dp, dwo, dcw, dcb, dg = conv_bwd(h_in, dh, halos, acts, g, f["w_in"], f["conv_w"], f["conv_b"], f["w_out"], p + "bwd")
            dw_in = grad_w_in(n, dp, f["w_in"].shape[2], p + "grad_w_in")
            dconv_w = jnp.transpose(dcw[:, :CONV_K, :], (1, 0, 2)).reshape(CONV_K, -1)
            grads.update({p + "w_in": dw_in, p + "conv_w": _to_owner_blocks(dconv_w, 1), p + "conv_b": dcb,
                          p + "w_out": dwo.reshape(N_DEV, -1, d)})
        else:
            replicated_grads = None
            h_in, halos, acts = saved[i]
            dh, n, dp, dwo, dwgrp, dbgrp, dsc, dg = pool_bwd(h_in, dh, halos, acts, g, f["w_in"], f["w_grp"], f["b_grp"], f["scale"],
                                                             f["w_out"], first_pos, p + "bwd")
            dw_in = grad_w_in(n, dp, f["w_in"].shape[2], p + "grad_w_in")
            grads.update({p + "w_in": dw_in, p + "w_grp": _to_owner_blocks(dwgrp, 1),
                          p + "b_grp": _to_owner_blocks(dbgrp.reshape(4, -1), 1), p + "scale": dsc,
                          p + "w_out": dwo.reshape(N_DEV, -1, d)})
        grads["norm%d_g" % i] = dg
        layer_sharded[i] = ["l%d_%s" % (i, n) for n in SHARDED[kind]]
        if i > 0:
            scatter_started[i] = exchange_start([owner_blocks(grads[n]) for n in layer_sharded[i]], False, dh,
                                                "scatter_start_l%d" % i)
            ordered = scatter_started[i][4][0, 0]
        if replicated_grads is not None:
            replicated_grads(token=ordered)
    grad_x = dh[p0:][None]
    grads["meta_tokens"] = _to_owner_blocks(dh[first_pos:p0], 1)
    last = len(LAYER_KINDS)
    layer_sharded[last] = ["meta_tokens", "replicated"]
    scatter_started[last] = exchange_start([owner_blocks(grads["meta_tokens"]), pack(grads).reshape(N_DEV, -1, 128)], False,
                                           dh, "scatter_start_replicated")
    layer_sharded["early"] = early_names
    layer_sharded[0] = [n for n in layer_sharded[0] if n not in early_names]

    out = {}
    received = {}
    after = [scatter_started[last][4]]
    for i in list(reversed(range(1, last))) + [last, "early", 0]:
        received.update(zip(layer_sharded[i], exchange_wait(scatter_started[i], False, after, "scatter_wait_%s" % i)))
        updated = []
        for n in layer_sharded[i]:
            if n != "replicated":
                res = sum_adamw(received[n], as2d(weights[n]), as2d(moments_m[n]), as2d(moments_v[n]), "update_" + n)
                out[n] = [r.reshape(weights[n].shape) for r in res]
                updated.append(out[n][0])
        after = updated or after
        if i == last:
            g_sum = sum_parts(received["replicated"], "sum_replicated")
            small_gather = exchange_start([g_sum], True, g_sum, "gather_small_grads_start")
            scatter_started[0] = exchange_start([owner_blocks(grads[n]) for n in layer_sharded[0]], False, small_gather[4],
                                                "scatter_start_l0")
            g_full = exchange_wait(small_gather, True, scatter_started[0][4], "gather_small_grads_wait")[0].reshape(-1, 128)
            offsets, offset = {}, 0
            for n in rep_names:
                offsets[n] = offset
                offset += weights[n].size + (-weights[n].size % 1024)
            pieces = [(offsets[n] // 128, max(weights[n].size // 128, 1), min(weights[n].size, 128)) for n in vectors]
            res = update_packed(g_full, pack(weights, vectors), pack(moments_m, vectors), pack(moments_v, vectors), pieces,
                                "update_replicated_vectors")
            for j, n in enumerate(vectors):
                out[n] = [r.reshape(weights[n].shape) for r in res[4 * j:4 * j + 4]]
            flat = g_full.reshape(-1)
            groups = [(flat[offsets[n]:offsets[n] + weights[n].size].reshape(as2d(weights[n]).shape), as2d(weights[n]),
                       as2d(moments_m[n]), as2d(moments_v[n])) for n in matrices]
            res = update_natural(groups, "update_replicated_matrices")
            for j, n in enumerate(matrices):
                out[n] = [r.reshape(weights[n].shape) for r in res[4 * j:4 * j + 4]]
            after = [out[n][k] for n in rep_names for k in range(4)]

    return (loss, grad_x) + tuple(out[n][k] for k in range(4) for n in names)


def kernel(x, *rest):
    names = weight_names()
    nw = len(names)
    weights = dict(zip(names, rest[:nw]))
    target = rest[nw]
    moments_m = dict(zip(names, rest[nw + 1:2 * nw + 1]))
    moments_v = dict(zip(names, rest[2 * nw + 1:3 * nw + 1]))
    return _step(x, target, weights, moments_m, moments_v)
```

```python
import math

import jax
import jax.numpy as jnp
from jax import lax
from jax.experimental import pallas as pl
from jax.experimental.pallas import tpu as pltpu

F32 = jnp.float32
BF16 = jnp.bfloat16
EPS = 1e-6
N_DEV = 8
TOKEN_TILE = 256
SCAN_CHUNKS = 4
S5_GROUP = 16
S5_STATE = 64
POOL_WINDOWS = (2, 4, 8, 16)
POOL_HALO = 16
CONV_K = 3
CONV_HALO = 8
ADAM_LR = 0.001
ADAM_B1 = 0.9
ADAM_B2 = 0.999
ADAM_EPS = 1e-08
ADAM_WD = 0.01
ADAM_STEP = 10
GELU_C = math.sqrt(2.0 / math.pi)
GELU_A = 0.044715
UPDATE_TILE_ELEMS = 1 << 17
PACK_ROWS = 512
PACK_ALIGN = 8 * 128
HIGH_HALF = -65536
HALF_OF_LOW_HALF = 0x8000
VMEM_LIMIT = 56 << 20
VMEM_LIMIT_LARGE = 62 << 20

ANY = pl.BlockSpec(memory_space=pl.ANY)


def _params(vmem=VMEM_LIMIT, ndim=1):
    return pltpu.CompilerParams(vmem_limit_bytes=vmem, dimension_semantics=("arbitrary",) * ndim)


def _dot(a, b):
    return jnp.dot(a.astype(BF16), b.astype(BF16), preferred_element_type=F32)


def _dot_nt(a, b):
    return lax.dot_general(a.astype(BF16), b.astype(BF16), (((1,), (1,)), ((), ())), preferred_element_type=F32)


def _dot_tn(a, b):
    return lax.dot_general(a.astype(BF16), b.astype(BF16), (((0,), (0,)), ((), ())), preferred_element_type=F32)


def _rms_fwd(h, g):
    r = lax.rsqrt(jnp.mean(h * h, axis=-1, keepdims=True) + EPS)
    hh = h * r
    return hh * g, hh, r


def _rms_bwd(dn, hh, r, g):
    dhh = dn * g
    return r * (dhh - hh * jnp.mean(dhh * hh, axis=-1, keepdims=True))


def _sigmoid(x):
    return 1.0 / (1.0 + jnp.exp(-x))


def _silu_and_grad(z):
    s = _sigmoid(z)
    return z * s, s * (1.0 + z * (1.0 - s))


def _gelu(y):
    t = jnp.tanh(GELU_C * (y + GELU_A * y * y * y))
    return 0.5 * y * (1.0 + t), t


def _gelu_grad(y, t):
    return 0.5 * (1.0 + t) + 0.5 * y * (1.0 - t * t) * GELU_C * (1.0 + 3.0 * GELU_A * y * y)


def _rows(shape):
    return lax.broadcasted_iota(jnp.int32, shape, 0)


def _shift_down(x, k, halo):
    y = pltpu.roll(x, k, 0)
    rows = _rows(x.shape)
    for j in range(k):
        y = jnp.where(rows == j, halo[halo.shape[0] - k + j:halo.shape[0] - k + j + 1, :], y)
    return y


def _shift_up(x, k, halo):
    n = x.shape[0]
    y = pltpu.roll(x, n - k, 0)
    rows = _rows(x.shape)
    for j in range(k):
        y = jnp.where(rows == n - k + j, halo[j:j + 1, :], y)
    return y


def _window_sums_back(ext):
    out = []
    s = ext
    for k in (1, 2, 4, 8):
        s = s + pltpu.roll(s, k, 0)
        out.append(s)
    return out


def _window_sums_fwd(ext):
    n = ext.shape[0]
    out = []
    s = ext
    for k in (1, 2, 4, 8):
        s = s + pltpu.roll(s, n - k, 0)
        out.append(s)
    return out


def _pool_inv_count(tile, tt, first_pos, w, width):
    pos = _rows((tt, width)) + (tile * tt - first_pos + 1)
    return 1.0 / jnp.clip(pos, 1, w).astype(F32)


def _slab_spec(lp, tt, sw):
    nj = sw // 128
    return pl.BlockSpec((4 * tt * nj, 128), lambda i: (i, 0)), (lp * 4 * nj, 128)


def _pack_pair(re, im):
    def rounded(v):
        return lax.bitcast_convert_type(v, jnp.int32) + HALF_OF_LOW_HALF
    return lax.bitcast_convert_type((rounded(re) & HIGH_HALF) | lax.shift_right_logical(rounded(im), 16), F32)


def _unpack_pair(w):
    b = lax.bitcast_convert_type(w, jnp.int32)
    return lax.bitcast_convert_type(b & HIGH_HALF, F32), lax.bitcast_convert_type(lax.shift_left(b, 16), F32)


def _slab_load(ref, c):
    nj = ref.shape[0] // (4 * TOKEN_TILE)
    first = c * TOKEN_TILE * nj
    return _unpack_pair(jnp.concatenate([ref[pl.ds(first + j, TOKEN_TILE, stride=nj), :] for j in range(nj)], axis=1))


def _slab_store(ref, c, re, im):
    nj = ref.shape[0] // (4 * TOKEN_TILE)
    first = c * TOKEN_TILE * nj
    val = _pack_pair(re, im)
    for j in range(nj):
        ref[pl.ds(first + j, TOKEN_TILE, stride=nj), :] = val[:, j * 128:(j + 1) * 128]


def _s5_disc_math(lr, li, ldt, br, bi):
    dt = jnp.exp(ldt)
    mag = jnp.exp(lr * dt)
    ar = mag * jnp.cos(li * dt)
    ai = mag * jnp.sin(li * dt)
    den = lr * lr + li * li
    kr = ((ar - 1.0) * lr + ai * li) / den
    ki = (ai * lr - (ar - 1.0) * li) / den
    bbr = kr[None] * br - ki[None] * bi
    bbi = kr[None] * bi + ki[None] * br
    return ar, ai, bbr, bbi


def s5_disc_fwd(lr, li, ldt, br_t, bi_t, c_re, c_im, name):
    ni, ng, npp = br_t.shape
    gc = ng // 4

    def body(lr_ref, li_ref, ldt_ref, br_ref, bi_ref, cre_ref, cim_ref,
             ar_ref, ai_ref, bdre_ref, bdim_ref, cdre_ref, cdim_ref, bbr_sc, bbi_sc):
        ar, ai, bbr, bbi = _s5_disc_math(lr_ref[...], li_ref[...], ldt_ref[...], br_ref[...], bi_ref[...])
        ar_ref[...] = ar
        ai_ref[...] = ai
        bbr_sc[...] = bbr
        bbi_sc[...] = bbi
        for ref in (bdre_ref, bdim_ref, cdre_ref, cdim_ref):
            ref[...] = jnp.zeros_like(ref)
        for k in range(4):
            for j in range(gc):
                g = k * gc + j
                ins, states = pl.ds(j * ni, ni), pl.ds(j * npp, npp)
                bdre_ref[k, ins, states] = bbr_sc[:, g, :].astype(BF16)
                bdim_ref[k, ins, states] = bbi_sc[:, g, :].astype(BF16)
                cdre_ref[k, states, ins] = cre_ref[g].T.astype(BF16)
                cdim_ref[k, states, ins] = (-cim_ref[g]).T.astype(BF16)

    sd = jax.ShapeDtypeStruct
    return pl.pallas_call(
        body, name=name,
        out_shape=(sd(lr.shape, F32), sd(lr.shape, F32), sd((4, gc * ni, gc * npp), BF16), sd((4, gc * ni, gc * npp), BF16),
                   sd((4, gc * npp, gc * ni), BF16), sd((4, gc * npp, gc * ni), BF16)),
        scratch_shapes=[pltpu.VMEM(br_t.shape, F32), pltpu.VMEM(br_t.shape, F32)],
        compiler_params=pltpu.CompilerParams(vmem_limit_bytes=VMEM_LIMIT),
    )(lr, li, ldt, br_t, bi_t, c_re, c_im)


def s5_disc_bwd(lr, li, ldt, br_t, bi_t, dar, dai, dbbr, dbbi, name):
    def body(lr_ref, li_ref, ldt_ref, br_ref, bi_ref, dar_ref, dai_ref, dbbr_ref, dbbi_ref,
             dlr_ref, dli_ref, dldt_ref, dbr_ref, dbi_ref):
        _, vjp = jax.vjp(_s5_disc_math, lr_ref[...], li_ref[...], ldt_ref[...], br_ref[...], bi_ref[...])
        dlr, dli, dldt, dbr, dbi = vjp((dar_ref[...], dai_ref[...], dbbr_ref[...], dbbi_ref[...]))
        dlr_ref[...] = dlr
        dli_ref[...] = dli
        dldt_ref[...] = dldt
        dbr_ref[...] = dbr
        dbi_ref[...] = dbi

    sd = jax.ShapeDtypeStruct
    return pl.pallas_call(
        body, name=name,
        out_shape=(sd(lr.shape, F32), sd(lr.shape, F32), sd(ldt.shape, F32), sd(br_t.shape, F32), sd(br_t.shape, F32)),
    )(lr, li, ldt, br_t, bi_t, dar, dai, dbbr, dbbi)


def s5_fwd1(h, g, w_in, bdre, bdim, name):
    lp, d = h.shape
    tt = TOKEN_TILE
    cw, sw = bdre.shape[1], bdre.shape[2]

    def body(h_ref, g_ref, w_hbm, bdre_hbm, bdim_hbm, u_ref, z_ref, x_ref, w, bre, bim):
        @pl.when(pl.program_id(0) == 0)
        def _():
            pltpu.sync_copy(w_hbm, w)
            pltpu.sync_copy(bdre_hbm, bre)
            pltpu.sync_copy(bdim_hbm, bim)

        n = _rms_fwd(h_ref[...], g_ref[...])[0].astype(BF16)
        for c in range(4):
            cols = slice(c * cw, (c + 1) * cw)
            u = jnp.dot(n, w[c], preferred_element_type=F32)
            u_ref[:, cols] = u
            z_ref[:, cols] = jnp.dot(n, w[c + 4], preferred_element_type=F32)
            ub = u.astype(BF16)
            _slab_store(x_ref, c, jnp.dot(ub, bre[c], preferred_element_type=F32), jnp.dot(ub, bim[c], preferred_element_type=F32))

    sd = jax.ShapeDtypeStruct
    slab, slab_shape = _slab_spec(lp, tt, sw)
    row = pl.BlockSpec((tt, d), lambda i: (i, 0))
    return pl.pallas_call(
        body, name=name, grid=(lp // tt,),
        in_specs=[row, pl.BlockSpec((1, d), lambda i: (0, 0)), ANY, ANY, ANY],
        out_specs=[row, row, slab],
        out_shape=(sd((lp, d), F32), sd((lp, d), F32), sd(slab_shape, F32)),
        scratch_shapes=[pltpu.VMEM(w_in.shape, BF16), pltpu.VMEM(bdre.shape, BF16), pltpu.VMEM(bdim.shape, BF16)],
        compiler_params=_params(),
    )(h, g, w_in, bdre, bdim)


def s5_scan_fwd(x, ar, ai, name):
    nj = ar.shape[1]
    tt = TOKEN_TILE
    cpb = SCAN_CHUNKS
    nt = x.shape[0] // (4 * tt * nj)

    def body(x_ref, ar_ref, ai_ref, s_ref, st_r, st_i):
        i, cg = pl.program_id(0), pl.program_id(1)

        @pl.when(i == 0)
        def _():
            for q in range(cpb):
                st_r[cg * cpb + q] = jnp.zeros((nj, 128), F32)
                st_i[cg * cpb + q] = jnp.zeros((nj, 128), F32)

        a_r = [ar_ref[cg * cpb + q] for q in range(cpb)]
        a_i = [ai_ref[cg * cpb + q] for q in range(cpb)]

        def step(t, carry):
            out = []
            for q in range(cpb):
                s_r, s_i = carry[q]
                rows = pl.ds(pl.multiple_of((q * tt + t) * nj, nj), nj)
                x_r, x_i = _unpack_pair(x_ref[rows, :])
                n_r = a_r[q] * s_r - a_i[q] * s_i + x_r
                n_i = a_r[q] * s_i + a_i[q] * s_r + x_i
                s_ref[rows, :] = _pack_pair(n_r, n_i)
                out.append((n_r, n_i))
            return tuple(out)

        init = tuple((st_r[cg * cpb + q], st_i[cg * cpb + q]) for q in range(cpb))
        final = lax.fori_loop(0, tt, step, init, unroll=8)
        for q in range(cpb):
            st_r[cg * cpb + q] = final[q][0]
            st_i[cg * cpb + q] = final[q][1]

    blk = pl.BlockSpec((cpb * tt * nj, 128), lambda i, cg: (i * (4 // cpb) + cg, 0))
    par = pl.BlockSpec((4, nj, 128), lambda i, cg: (0, 0, 0))
    sd = jax.ShapeDtypeStruct
    return pl.pallas_call(
        body, name=name, grid=(nt, 4 // cpb),
        in_specs=[blk, par, par], out_specs=blk,
        out_shape=sd(x.shape, F32),
        scratch_shapes=[pltpu.VMEM((4, nj, 128), F32), pltpu.VMEM((4, nj, 128), F32)],
        compiler_params=_params(ndim=2),
    )(x, ar, ai)


def s5_fwd3(s, u, z, h, cdre, cdim, w_glu, w_out, d_skip, b_glu, name):
    lp, d = h.shape
    tt = TOKEN_TILE
    sw, cw = cdre.shape[1], cdre.shape[2]

    def body(s_ref, u_ref, z_ref, h_ref, d_ref, bg_ref, cre_hbm, cim_hbm, wg_hbm, wo_hbm,
             o_ref, y_ref, q_ref, cre, cim, wg, wo):
        @pl.when(pl.program_id(0) == 0)
        def _():
            pltpu.sync_copy(cre_hbm, cre)
            pltpu.sync_copy(cim_hbm, cim)
            pltpu.sync_copy(wg_hbm, wg)
            pltpu.sync_copy(wo_hbm, wo)

        gys, q = [], None
        for c in range(4):
            cols = slice(c * cw, (c + 1) * cw)
            s_r, s_i = _slab_load(s_ref, c)
            y = _dot(s_r, cre[c]) + _dot(s_i, cim[c]) + d_ref[c] * u_ref[:, cols]
            y_ref[:, cols] = y
            gys.append(_gelu(y)[0])
            part = _dot(gys[c], wg[c])
            q = part if c == 0 else q + part
        q_ref[...] = q
        sig = _sigmoid(q + bg_ref[...])
        zz = z_ref[...]
        sz = zz * _sigmoid(zz)
        o = h_ref[...]
        for k in range(4):
            cols = slice(k * cw, (k + 1) * cw)
            o = o + _dot(gys[k] * sig[:, cols] * sz[:, cols], wo[k])
        o_ref[...] = o

    row = pl.BlockSpec((tt, d), lambda i: (i, 0))
    slab, _ = _slab_spec(lp, tt, sw)
    sd = jax.ShapeDtypeStruct((lp, d), F32)
    return pl.pallas_call(
        body, name=name, grid=(lp // tt,),
        in_specs=[slab, row, row, row, pl.BlockSpec((4, 1, cw), lambda i: (0, 0, 0)), pl.BlockSpec((1, d), lambda i: (0, 0)),
                  ANY, ANY, ANY, ANY],
        out_specs=[row, row, row],
        out_shape=(sd, sd, sd),
        scratch_shapes=[pltpu.VMEM(cdre.shape, BF16), pltpu.VMEM(cdim.shape, BF16), pltpu.VMEM(w_glu.shape, BF16),
                        pltpu.VMEM(w_out.shape, BF16)],
        compiler_params=_params(),
    )(s, u, z, h, d_skip, b_glu, cdre, cdim, w_glu, w_out)


def s5_bwd3a(dh, y, q, z, w_glu, w_out, b_glu, name):
    lp, d = dh.shape
    tt = TOKEN_TILE
    nt = lp // tt
    cw = w_glu.shape[1]

    def body(dh_ref, y_ref, q_ref, z_ref, bg_ref, wg_hbm, wo_hbm, dy_ref, dp_ref, dwo_hbm, dwg_hbm, dbg_hbm,
             wg, wo, dwo, dwg, dbg):
        i = pl.program_id(0)

        @pl.when(i == 0)
        def _():
            pltpu.sync_copy(wg_hbm, wg)
            pltpu.sync_copy(wo_hbm, wo)
            dwo[...] = jnp.zeros_like(dwo)
            dwg[...] = jnp.zeros_like(dwg)
            dbg[...] = jnp.zeros_like(dbg)

        sig = _sigmoid(q_ref[...] + bg_ref[...])
        sz, dsz = _silu_and_grad(z_ref[...])
        dhv = dh_ref[...]
        yv = y_ref[...]
        gy, t = _gelu(yv)
        dq_parts, dgy_parts = [], []
        for k in range(4):
            cols = slice(k * cw, (k + 1) * cw)
            gy_k, sig_k, sz_k = gy[:, cols], sig[:, cols], sz[:, cols]
            y2 = gy_k * sig_k
            dy3 = _dot_nt(dhv, wo[k])
            dwo[k] += _dot_tn(y2 * sz_k, dhv)
            dy2 = dy3 * sz_k
            dp_ref[0, :, cols] = (dy3 * y2 * dsz[:, cols]).astype(BF16)
            dq_parts.append(dy2 * gy_k * sig_k * (1.0 - sig_k))
            dgy_parts.append(dy2 * sig_k)
        dq = jnp.concatenate(dq_parts, axis=1)
        dbg[...] += jnp.sum(dq, axis=0, keepdims=True)
        dgelu = _gelu_grad(yv, t)
        for k in range(4):
            cols = slice(k * cw, (k + 1) * cw)
            dwg[k] += _dot_tn(gy[:, cols], dq)
            dy_ref[:, cols] = (dgy_parts[k] + _dot_nt(dq, wg[k])) * dgelu[:, cols]

        @pl.when(i == nt - 1)
        def _():
            pltpu.sync_copy(dwo, dwo_hbm)
            pltpu.sync_copy(dwg, dwg_hbm)
            pltpu.sync_copy(dbg, dbg_hbm)

    row = pl.BlockSpec((tt, d), lambda i: (i, 0))
    sd = jax.ShapeDtypeStruct
    return pl.pallas_call(
        body, name=name, grid=(nt,),
        in_specs=[row, row, row, row, pl.BlockSpec((1, d), lambda i: (0, 0)), ANY, ANY],
        out_specs=[row, pl.BlockSpec((1, tt, d), lambda i: (1, i, 0)), ANY, ANY, ANY],
        out_shape=(sd((lp, d), F32), sd((2, lp, d), BF16), sd(w_out.shape, F32), sd(w_glu.shape, F32), sd((1, d), F32)),
        scratch_shapes=[pltpu.VMEM(w_glu.shape, BF16), pltpu.VMEM(w_out.shape, BF16),
                        pltpu.VMEM(w_out.shape, F32), pltpu.VMEM(w_glu.shape, F32), pltpu.VMEM((1, d), F32)],
        compiler_params=_params(),
    )(dh, y, q, z, b_glu, w_glu, w_out)


def s5_bwd3b(dy, s, u, cdre, cdim, d_skip, name):
    lp, d = dy.shape
    tt = TOKEN_TILE
    nt = lp // tt
    sw, cw = cdre.shape[1], cdre.shape[2]
    gc = cw // S5_GROUP

    def body(dy_ref, s_ref, u_ref, d_ref, cre_hbm, cim_hbm,
             ds_ref, dus_ref, dcre_ref, dcim_ref, dd_hbm, cre, cim, dcre, dcim, dd):
        i = pl.program_id(0)

        @pl.when(i == 0)
        def _():
            pltpu.sync_copy(cre_hbm, cre)
            pltpu.sync_copy(cim_hbm, cim)
            dcre[...] = jnp.zeros_like(dcre)
            dcim[...] = jnp.zeros_like(dcim)
            dd[...] = jnp.zeros_like(dd)

        for c in range(4):
            chunk = slice(c * cw, (c + 1) * cw)
            dyv = dy_ref[:, chunk]
            dd[c] += jnp.sum(dyv * u_ref[:, chunk], axis=0, keepdims=True)
            dus_ref[:, chunk] = dyv * d_ref[c]
            _slab_store(ds_ref, c, _dot_nt(dyv, cre[c]), _dot_nt(dyv, cim[c]))
            s_r, s_i = _slab_load(s_ref, c)
            dcre[c] += _dot_tn(s_r, dyv)
            dcim[c] += _dot_tn(s_i, dyv)

        @pl.when(i == nt - 1)
        def _():
            for k in range(4):
                for j in range(gc):
                    rows, cols = pl.ds(j * S5_STATE, S5_STATE), pl.ds(j * S5_GROUP, S5_GROUP)
                    dcre_ref[k, j] = dcre[k, rows, cols].T
                    dcim_ref[k, j] = dcim[k, rows, cols].T
            pltpu.sync_copy(dd, dd_hbm)

    sd = jax.ShapeDtypeStruct
    row = pl.BlockSpec((tt, d), lambda i: (i, 0))
    slab, slab_shape = _slab_spec(lp, tt, sw)
    diag = pl.BlockSpec((4, gc, S5_GROUP, S5_STATE), lambda i: (0, 0, 0, 0))
    return pl.pallas_call(
        body, name=name, grid=(nt,),
        in_specs=[row, slab, row, pl.BlockSpec((4, 1, cw), lambda i: (0, 0, 0)), ANY, ANY],
        out_specs=[slab, row, diag, diag, ANY],
        out_shape=(sd(slab_shape, F32), sd((lp, d), F32),
                   sd((4, gc, S5_GROUP, S5_STATE), F32), sd((4, gc, S5_GROUP, S5_STATE), F32), sd((4, 1, cw), F32)),
        scratch_shapes=[pltpu.VMEM(cdre.shape, BF16), pltpu.VMEM(cdim.shape, BF16),
                        pltpu.VMEM(cdre.shape, F32), pltpu.VMEM(cdim.shape, F32), pltpu.VMEM((4, 1, cw), F32)],
        compiler_params=_params(),
    )(dy, s, u, d_skip, cdre, cdim)


def s5_scan_bwd(g, s, ar, ai, name):
    nj = ar.shape[1]
    tt = TOKEN_TILE
    cpb = SCAN_CHUNKS
    nt = g.shape[0] // (4 * tt * nj)

    def body(g_ref, s_ref, ar_ref, ai_ref, lam_ref, dar_ref, dai_ref, st_r, st_i, acc_r, acc_i):
        i, cg = pl.program_id(0), pl.program_id(1)

        @pl.when((i == 0) & (cg == 0))
        def _():
            for ref in (st_r, st_i, acc_r, acc_i):
                ref[...] = jnp.zeros_like(ref)

        a_r = [ar_ref[cg * cpb + q] for q in range(cpb)]
        a_i = [ai_ref[cg * cpb + q] for q in range(cpb)]

        def slab(q, t):
            return pl.ds(pl.multiple_of((q * tt + t) * nj, nj), nj)

        def adjoint(q, t, l_r, l_i):
            rows = slab(q, t)
            g_r, g_i = _unpack_pair(g_ref[rows, :])
            n_r = g_r + a_r[q] * l_r + a_i[q] * l_i
            n_i = g_i + a_r[q] * l_i - a_i[q] * l_r
            lam_ref[rows, :] = _pack_pair(n_r, n_i)
            return n_r, n_i

        def pair(q, t, l_r, l_i, d_r, d_i):
            p_r, p_i = _unpack_pair(s_ref[slab(q, t), :])
            return d_r + l_r * p_r + l_i * p_i, d_i + l_i * p_r - l_r * p_i

        def step(k, carry):
            t = tt - 1 - k
            out = []
            for q in range(cpb):
                l_r, l_i, d_r, d_i = carry[q]
                l_r, l_i = adjoint(q, t, l_r, l_i)
                d_r, d_i = pair(q, t - 1, l_r, l_i, d_r, d_i)
                out.append((l_r, l_i, d_r, d_i))
            return tuple(out)

        init = []
        for q in range(cpb):
            ch = cg * cpb + q
            l_r, l_i = st_r[ch], st_i[ch]
            d_r, d_i = pair(q, tt - 1, l_r, l_i, acc_r[ch], acc_i[ch])
            init.append((l_r, l_i, d_r, d_i))
        final = lax.fori_loop(0, tt - 1, step, tuple(init), unroll=8)
        for q in range(cpb):
            ch = cg * cpb + q
            l_r, l_i, d_r, d_i = final[q]
            l_r, l_i = adjoint(q, 0, l_r, l_i)
            st_r[ch] = l_r
            st_i[ch] = l_i
            acc_r[ch] = d_r
            acc_i[ch] = d_i
            dar_ref[ch] = d_r
            dai_ref[ch] = d_i

    blk = pl.BlockSpec((cpb * tt * nj, 128), lambda i, cg: ((nt - 1 - i) * (4 // cpb) + cg, 0))
    par = pl.BlockSpec((4, nj, 128), lambda i, cg: (0, 0, 0))
    sd = jax.ShapeDtypeStruct
    return pl.pallas_call(
        body, name=name, grid=(nt, 4 // cpb),
        in_specs=[blk, blk, par, par], out_specs=[blk, par, par],
        out_shape=(sd(g.shape, F32), sd((4, nj, 128), F32), sd((4, nj, 128), F32)),
        scratch_shapes=[pltpu.VMEM((4, nj, 128), F32)] * 4,
        compiler_params=_params(ndim=2),
    )(g, s, ar, ai)


def s5_bwd1(lam, dus, u, dp, h, dh, g, w_in, bdre, bdim, name):
    lp, d = h.shape
    tt = TOKEN_TILE
    nt = lp // tt
    cw, sw = bdre.shape[1], bdre.shape[2]
    gc = cw // S5_GROUP

    def body(lam_ref, dus_ref, u_ref, dpz_ref, h_ref, dh_ref, g_ref, w_hbm, bre_hbm, bim_hbm,
             dpu_ref, dho_ref, n_ref, dbre_ref, dbim_ref, dg_hbm, w, bre, bim, dbre, dbim, dg):
        i = pl.program_id(0)

        @pl.when(i == 0)
        def _():
            pltpu.sync_copy(w_hbm, w)
            pltpu.sync_copy(bre_hbm, bre)
            pltpu.sync_copy(bim_hbm, bim)
            dbre[...] = jnp.zeros_like(dbre)
            dbim[...] = jnp.zeros_like(dbim)
            dg[...] = jnp.zeros_like(dg)

        dz = dpz_ref[0]
        dn = None
        for c in range(4):
            chunk = slice(c * cw, (c + 1) * cw)
            (l_r, l_i), uv = _slab_load(lam_ref, c), u_ref[:, chunk]
            du = dus_ref[:, chunk] + _dot_nt(l_r, bre[c]) + _dot_nt(l_i, bim[c])
            dbre[c] += _dot_tn(uv, l_r)
            dbim[c] += _dot_tn(uv, l_i)
            dpu_ref[0, :, chunk] = du.astype(BF16)
            part = _dot_nt(du, w[c]) + _dot_nt(dz[:, chunk], w[4 + c])
            dn = part if c == 0 else dn + part
        gv = g_ref[...]
        n, hh, rr = _rms_fwd(h_ref[...], gv)
        n_ref[...] = n.T.astype(BF16)
        dg[...] += jnp.sum(dn * hh, axis=0, keepdims=True)
        dho_ref[...] = dh_ref[...] + _rms_bwd(dn, hh, rr, gv)

        @pl.when(i == nt - 1)
        def _():
            for k in range(4):
                for j in range(gc):
                    rows, cols = pl.ds(j * S5_GROUP, S5_GROUP), pl.ds(j * S5_STATE, S5_STATE)
                    dbre_ref[k, j] = dbre[k, rows, cols]
                    dbim_ref[k, j] = dbim[k, rows, cols]
            pltpu.sync_copy(dg, dg_hbm)

    sd = jax.ShapeDtypeStruct
    row = pl.BlockSpec((tt, d), lambda i: (i, 0))
    slab, _ = _slab_spec(lp, tt, sw)
    diag = pl.BlockSpec((4, gc, S5_GROUP, S5_STATE), lambda i: (0, 0, 0, 0))
    return pl.pallas_call(
        body, name=name, grid=(nt,),
        in_specs=[slab, row, row, pl.BlockSpec((1, tt, d), lambda i: (1, i, 0)), row, row, pl.BlockSpec((1, d), lambda i: (0, 0)),
                  ANY, ANY, ANY],
        out_specs=[pl.BlockSpec((1, tt, d), lambda i: (0, i, 0)), row, pl.BlockSpec((d, tt), lambda i: (0, i)), diag, diag, ANY],
        out_shape=(sd(dp.shape, BF16), sd((lp, d), F32), sd((d, lp), BF16),
                   sd((4, gc, S5_GROUP, S5_STATE), F32), sd((4, gc, S5_GROUP, S5_STATE), F32), sd((1, d), F32)),
        input_output_aliases={3: 0},
        scratch_shapes=[pltpu.VMEM(w_in.shape, BF16), pltpu.VMEM(bdre.shape, BF16), pltpu.VMEM(bdim.shape, BF16),
                        pltpu.VMEM(bdre.shape, F32), pltpu.VMEM(bdim.shape, F32), pltpu.VMEM((1, d), F32)],
        compiler_params=_params(),
    )(lam, dus, u, dp, h, dh, g, w_in, bdre, bdim)


def grad_w_in(n_t, dp, blk, name):
    d, lp = n_t.shape
    npart, _, width = dp.shape
    per = width // blk

    def body(n_ref, dp_ref, o_ref):
        o_ref[0] = jnp.dot(n_ref[...], dp_ref[0], preferred_element_type=F32).astype(o_ref.dtype)

    return pl.pallas_call(
        body, name=name, grid=(npart * per,),
        in_specs=[pl.BlockSpec((d, lp), lambda j: (0, 0), pipeline_mode=pl.Buffered(1)),
                  pl.BlockSpec((1, lp, blk), lambda j: (j // per, 0, j % per))],
        out_specs=pl.BlockSpec((1, d, blk), lambda j: (j, 0, 0)),
        out_shape=jax.ShapeDtypeStruct((npart * per, d, blk), BF16),
        compiler_params=_params(),
    )(n_t, dp)


def _conv_mix(cg, v, cw_ref, cb_ref, halo, c):
    hc = cg * v
    taps = cw_ref[c]
    conv = taps[2:3, :] * hc + taps[1:2, :] * _shift_down(hc, 1, halo) + taps[0:1, :] * _shift_down(hc, 2, halo) + cb_ref[c]
    return hc, conv


def conv_fwd(h, g, w_in, conv_w, conv_b, w_out, name):
    lp, d = h.shape
    tt = TOKEN_TILE
    nt = lp // tt
    nch, ce = w_out.shape[0], w_out.shape[1]

    def body(h_ref, g_ref, cw_ref, cb_ref, w_hbm, wo_hbm, o_ref, halo_ref, acts_ref, w, wo, halo):
        i = pl.program_id(0)

        @pl.when(i == 0)
        def _():
            pltpu.sync_copy(w_hbm, w)
            pltpu.sync_copy(wo_hbm, wo)
            halo[...] = jnp.zeros_like(halo)

        hv = h_ref[...]
        n = _rms_fwd(hv, g_ref[...])[0].astype(BF16)
        o = hv
        for c in range(nch):
            cols = slice(c * ce, (c + 1) * ce)
            bg, cg, v, z = [jnp.dot(n, w[p * nch + c], preferred_element_type=F32) for p in range(4)]
            for p, val in enumerate((bg, cg, v, z)):
                acts_ref[p, :, cols] = val.astype(BF16)
            hc, conv = _conv_mix(cg, v, cw_ref, cb_ref, halo[c], c)
            o = o + _dot(bg * conv * (z * _sigmoid(z)), wo[c])
            halo[c] = hc[tt - CONV_HALO:, :]
            halo_ref[0, c] = hc[tt - CONV_HALO:, :]
        o_ref[...] = o

    sd = jax.ShapeDtypeStruct
    return pl.pallas_call(
        body, name=name, grid=(nt,),
        in_specs=[pl.BlockSpec((tt, d), lambda i: (i, 0)), pl.BlockSpec((1, d), lambda i: (0, 0)),
                  pl.BlockSpec(conv_w.shape, lambda i: (0, 0, 0)), pl.BlockSpec(conv_b.shape, lambda i: (0, 0, 0)), ANY, ANY],
        out_specs=[pl.BlockSpec((tt, d), lambda i: (i, 0)), pl.BlockSpec((1, nch, CONV_HALO, ce), lambda i: (i, 0, 0, 0)),
                   pl.BlockSpec((4, tt, nch * ce), lambda i: (0, i, 0))],
        out_shape=(sd((lp, d), F32), sd((nt, nch, CONV_HALO, ce), F32), sd((4, lp, nch * ce), BF16)),
        scratch_shapes=[pltpu.VMEM(w_in.shape, BF16), pltpu.VMEM(w_out.shape, BF16), pltpu.VMEM((nch, CONV_HALO, ce), F32)],
        compiler_params=_params(),
    )(h, g, conv_w, conv_b, w_in, w_out)


def conv_bwd(h, dh, halos, acts, g, w_in, conv_w, conv_b, w_out, name):
    lp, d = h.shape
    tt = TOKEN_TILE
    nt = lp // tt
    nch, ce = w_out.shape[0], w_out.shape[1]

    def body(h_ref, dh_ref, halo_ref, acts_ref, g_ref, cw_ref, cb_ref, w_hbm, wo_hbm,
             dho_ref, n_ref, dp_ref, dwo_hbm, dcw_hbm, dcb_hbm, dg_hbm, w, wo, nxt, dwo, dcw, dcb, dg):
        i = pl.program_id(0)

        @pl.when(i == 0)
        def _():
            pltpu.sync_copy(w_hbm, w)
            pltpu.sync_copy(wo_hbm, wo)
            for ref in (nxt, dwo, dcw, dcb, dg):
                ref[...] = jnp.zeros_like(ref)

        gv = g_ref[...]
        nf, hh, rr = _rms_fwd(h_ref[...], gv)
        n_ref[...] = nf.T.astype(BF16)
        dhv = dh_ref[...]
        has_prev = (i < nt - 1).astype(F32)
        dn = jnp.zeros((tt, d), F32)
        for c in range(nch):
            halo = halo_ref[0, c] * has_prev
            cols = slice(c * ce, (c + 1) * ce)
            bg, cg, v, z = [acts_ref[p, :, cols].astype(F32) for p in range(4)]
            hc, conv = _conv_mix(cg, v, cw_ref, cb_ref, halo, c)
            sz, dsz = _silu_and_grad(z)
            y1 = bg * conv
            dy2 = _dot_nt(dhv, wo[c])
            dwo[c] += _dot_tn(y1 * sz, dhv)
            dy1 = dy2 * sz
            dz = dy2 * y1 * dsz
            dbg = dy1 * conv
            dconv = dy1 * bg
            dcb[c] += jnp.sum(dconv, axis=0, keepdims=True)
            up1 = _shift_up(dconv, 1, nxt[c])
            up2 = _shift_up(dconv, 2, nxt[c])
            nxt[c] = dconv[:CONV_HALO, :]
            taps = cw_ref[c]
            dhc = taps[2:3, :] * dconv + taps[1:2, :] * up1 + taps[0:1, :] * up2
            dcw[c, 0:1, :] += jnp.sum(hc * up2, axis=0, keepdims=True)
            dcw[c, 1:2, :] += jnp.sum(hc * up1, axis=0, keepdims=True)
            dcw[c, 2:3, :] += jnp.sum(hc * dconv, axis=0, keepdims=True)
            dcg = dhc * v
            dv = dhc * cg
            for p, val in enumerate((dbg, dcg, dv, dz)):
                dp_ref[p, :, cols] = val.astype(BF16)
                dn = dn + _dot_nt(val, w[p * nch + c])
        dg[...] += jnp.sum(dn * hh, axis=0, keepdims=True)
        dho_ref[...] = dhv + _rms_bwd(dn, hh, rr, gv)

        @pl.when(i == nt - 1)
        def _():
            pltpu.sync_copy(dwo, dwo_hbm)
            pltpu.sync_copy(dcw, dcw_hbm)
            pltpu.sync_copy(dcb, dcb_hbm)
            pltpu.sync_copy(dg, dg_hbm)

    rev = lambda i: (nt - 1 - i, 0)
    sd = jax.ShapeDtypeStruct
    return pl.pallas_call(
        body, name=name, grid=(nt,),
        in_specs=[pl.BlockSpec((tt, d), rev), pl.BlockSpec((tt, d), rev),
                  pl.BlockSpec((1, nch, CONV_HALO, ce), lambda i: (jnp.maximum(nt - 2 - i, 0), 0, 0, 0)),
                  pl.BlockSpec((4, tt, nch * ce), lambda i: (0, nt - 1 - i, 0)),
                  pl.BlockSpec((1, d), lambda i: (0, 0)),
                  pl.BlockSpec(conv_w.shape, lambda i: (0, 0, 0)), pl.BlockSpec(conv_b.shape, lambda i: (0, 0, 0)), ANY, ANY],
        out_specs=[pl.BlockSpec((tt, d), rev), pl.BlockSpec((d, tt), lambda i: (0, nt - 1 - i)),
                   pl.BlockSpec((4, tt, nch * ce), lambda i: (0, nt - 1 - i, 0)), ANY, ANY, ANY, ANY],
        out_shape=(sd((lp, d), F32), sd((d, lp), BF16), sd((4, lp, nch * ce), BF16),
                   sd(w_out.shape, F32), sd((nch, 8, ce), F32), sd((nch, 1, ce), F32), sd((1, d), F32)),
        scratch_shapes=[pltpu.VMEM(w_in.shape, BF16), pltpu.VMEM(w_out.shape, BF16), pltpu.VMEM((nch, CONV_HALO, ce), F32),
                        pltpu.VMEM(w_out.shape, F32), pltpu.VMEM((nch, 8, ce), F32), pltpu.VMEM((nch, 1, ce), F32),
                        pltpu.VMEM((1, d), F32)],
        compiler_params=_params(vmem=VMEM_LIMIT_LARGE),
    )(h, dh, halos, acts, g, conv_w, conv_b, w_in, w_out)


def _pool_mix(u, wg, bg_ref, sc_ref, halo, k, tile, tt, first_pos):
    ext = jnp.concatenate([halo, u], axis=0)
    win = _window_sums_back(ext)[k][POOL_HALO:, :]
    mixed = win * _pool_inv_count(tile, tt, first_pos, POOL_WINDOWS[k], u.shape[1]) - u
    outs = _dot(mixed, wg[k]) + bg_ref[k]
    return mixed, outs, outs * sc_ref[k]


def pool_fwd(h, g, w_in, w_grp, b_grp, scale, w_out, first_pos, name):
    lp, d = h.shape
    tt = TOKEN_TILE
    nt = lp // tt
    gw = w_grp.shape[1]

    def body(h_ref, g_ref, bg_ref, sc_ref, w_hbm, wg_hbm, wo_hbm, o_ref, halo_ref, acts_ref, w, wg, wo, halo):
        i = pl.program_id(0)

        @pl.when(i == 0)
        def _():
            pltpu.sync_copy(w_hbm, w)
            pltpu.sync_copy(wg_hbm, wg)
            pltpu.sync_copy(wo_hbm, wo)
            halo[...] = jnp.zeros_like(halo)

        hv = h_ref[...]
        n = _rms_fwd(hv, g_ref[...])[0].astype(BF16)
        o = hv
        for k in range(4):
            cols = slice(k * gw, (k + 1) * gw)
            u = jnp.dot(n, w[k], preferred_element_type=F32)
            z = jnp.dot(n, w[4 + k], preferred_element_type=F32)
            acts_ref[0, :, cols] = u.astype(BF16)
            acts_ref[1, :, cols] = z.astype(BF16)
            _, _, yp = _pool_mix(u, wg, bg_ref, sc_ref, halo[k], k, i, tt, first_pos)
            o = o + _dot(yp * (z * _sigmoid(z)), wo[k])
            halo[k] = u[tt - POOL_HALO:, :]
            halo_ref[0, k] = u[tt - POOL_HALO:, :]
        o_ref[...] = o

    sd = jax.ShapeDtypeStruct
    small = pl.BlockSpec((4, 1, gw), lambda i: (0, 0, 0))
    return pl.pallas_call(
        body, name=name, grid=(nt,),
        in_specs=[pl.BlockSpec((tt, d), lambda i: (i, 0)), pl.BlockSpec((1, d), lambda i: (0, 0)), small, small, ANY, ANY, ANY],
        out_specs=[pl.BlockSpec((tt, d), lambda i: (i, 0)), pl.BlockSpec((1, 4, POOL_HALO, gw), lambda i: (i, 0, 0, 0)),
                   pl.BlockSpec((2, tt, 4 * gw), lambda i: (0, i, 0))],
        out_shape=(sd((lp, d), F32), sd((nt, 4, POOL_HALO, gw), F32), sd((2, lp, 4 * gw), BF16)),
        scratch_shapes=[pltpu.VMEM(w_in.shape, BF16), pltpu.VMEM(w_grp.shape, BF16), pltpu.VMEM(w_out.shape, BF16),
                        pltpu.VMEM((4, POOL_HALO, gw), F32)],
        compiler_params=_params(),
    )(h, g, b_grp, scale, w_in, w_grp, w_out)


def pool_bwd(h, dh, halos, acts, g, w_in, w_grp, b_grp, scale, w_out, first_pos, name):
    lp, d = h.shape
    tt = TOKEN_TILE
    nt = lp // tt
    gw = w_grp.shape[1]

    def body(h_ref, dh_ref, halo_ref, acts_ref, g_ref, bg_ref, sc_ref, w_hbm, wg_hbm, wo_hbm,
             dho_ref, n_ref, dp_ref, dwo_hbm, dwg_hbm, dbg_hbm, dsc_hbm, dg_hbm,
             w, wg, wo, nxt, dwo, dwg, dbg, dsc, dg):
        i = pl.program_id(0)
        tile = nt - 1 - i

        @pl.when(i == 0)
        def _():
            pltpu.sync_copy(w_hbm, w)
            pltpu.sync_copy(wg_hbm, wg)
            pltpu.sync_copy(wo_hbm, wo)
            for ref in (nxt, dwo, dwg, dbg, dsc, dg):
                ref[...] = jnp.zeros_like(ref)

        gv = g_ref[...]
        nf, hh, rr = _rms_fwd(h_ref[...], gv)
        n_ref[...] = nf.T.astype(BF16)
        dhv = dh_ref[...]
        has_prev = (i < nt - 1).astype(F32)
        dn = jnp.zeros((tt, d), F32)
        for k in range(4):
            cols = slice(k * gw, (k + 1) * gw)
            u, z = acts_ref[0, :, cols].astype(F32), acts_ref[1, :, cols].astype(F32)
            mixed, outs, yp = _pool_mix(u, wg, bg_ref, sc_ref, halo_ref[0, k] * has_prev, k, tile, tt, first_pos)
            sz, dsz = _silu_and_grad(z)
            dy = _dot_nt(dhv, wo[k])
            dwo[k] += _dot_tn(yp * sz, dhv)
            dyp = dy * sz
            dz = dy * yp * dsz
            dsc[k] += jnp.sum(dyp * outs, axis=0, keepdims=True)
            douts = dyp * sc_ref[k]
            dbg[k] += jnp.sum(douts, axis=0, keepdims=True)
            dwg[k] += _dot_tn(mixed, douts)
            dmixed = _dot_nt(douts, wg[k])
            dm = dmixed * _pool_inv_count(tile, tt, first_pos, POOL_WINDOWS[k], gw)
            ext = jnp.concatenate([dm, nxt[k]], axis=0)
            du = _window_sums_fwd(ext)[k][:tt, :] - dmixed
            nxt[k] = dm[:POOL_HALO, :]
            dp_ref[0, :, cols] = du.astype(BF16)
            dp_ref[1, :, cols] = dz.astype(BF16)
            dn = dn + _dot_nt(du, w[k]) + _dot_nt(dz, w[4 + k])
        dg[...] += jnp.sum(dn * hh, axis=0, keepdims=True)
        dho_ref[...] = dhv + _rms_bwd(dn, hh, rr, gv)

        @pl.when(i == nt - 1)
        def _():
            pltpu.sync_copy(dwo, dwo_hbm)
            pltpu.sync_copy(dwg, dwg_hbm)
            pltpu.sync_copy(dbg, dbg_hbm)
            pltpu.sync_copy(dsc, dsc_hbm)
            pltpu.sync_copy(dg, dg_hbm)

    rev = lambda i: (nt - 1 - i, 0)
    sd = jax.ShapeDtypeStruct
    small = pl.BlockSpec((4, 1, gw), lambda i: (0, 0, 0))
    return pl.pallas_call(
        body, name=name, grid=(nt,),
        in_specs=[pl.BlockSpec((tt, d), rev), pl.BlockSpec((tt, d), rev),
                  pl.BlockSpec((1, 4, POOL_HALO, gw), lambda i: (jnp.maximum(nt - 2 - i, 0), 0, 0, 0)),
                  pl.BlockSpec((2, tt, 4 * gw), lambda i: (0, nt - 1 - i, 0)),
                  pl.BlockSpec((1, d), lambda i: (0, 0)), small, small, ANY, ANY, ANY],
        out_specs=[pl.BlockSpec((tt, d), rev), pl.BlockSpec((d, tt), lambda i: (0, nt - 1 - i)),
                   pl.BlockSpec((2, tt, 4 * gw), lambda i: (0, nt - 1 - i, 0)), ANY, ANY, ANY, ANY, ANY],
        out_shape=(sd((lp, d), F32), sd((d, lp), BF16), sd((2, lp, 4 * gw), BF16),
                   sd(w_out.shape, F32), sd(w_grp.shape, F32), sd((4, 1, gw), F32), sd((4, 1, gw), F32), sd((1, d), F32)),
        scratch_shapes=[pltpu.VMEM(w_in.shape, BF16), pltpu.VMEM(w_grp.shape, BF16), pltpu.VMEM(w_out.shape, BF16),
                        pltpu.VMEM((4, POOL_HALO, gw), F32), pltpu.VMEM(w_out.shape, F32), pltpu.VMEM(w_grp.shape, F32),
                        pltpu.VMEM((4, 1, gw), F32), pltpu.VMEM((4, 1, gw), F32), pltpu.VMEM((1, d), F32)],
        compiler_params=_params(),
    )(h, dh, halos, acts, g, b_grp, scale, w_in, w_grp, w_out)


def loss_head(h, target, g, pad_tiles, name):
    lp, d = h.shape
    tt = TOKEN_TILE
    nt = lp // tt

    def body(h_ref, t_ref, g_ref, dh_ref, dg_ref, loss_ref, acc):
        i = pl.program_id(0)

        @pl.when(i == 0)
        def _():
            acc[...] = jnp.zeros_like(acc)
            dg_ref[...] = jnp.zeros_like(dg_ref)

        @pl.when(i < pad_tiles)
        def _():
            dh_ref[...] = jnp.zeros_like(dh_ref)

        @pl.when(i >= pad_tiles)
        def _():
            gv = g_ref[...]
            n, hh, rr = _rms_fwd(h_ref[...], gv)
            err = n - t_ref[...]
            acc[...] += 0.5 * jnp.sum(jnp.mean(err * err, axis=-1, keepdims=True), axis=0, keepdims=True)
            dn = err * (1.0 / d)
            dg_ref[...] += jnp.sum(dn * hh, axis=0, keepdims=True)
            dh_ref[...] = _rms_bwd(dn, hh, rr, gv)

        loss_ref[...] = jnp.broadcast_to(acc[...], loss_ref.shape)

    sd = jax.ShapeDtypeStruct
    return pl.pallas_call(
        body, name=name, grid=(nt,),
        in_specs=[pl.BlockSpec((tt, d), lambda i: (i, 0)), pl.BlockSpec((tt, d), lambda i: (jnp.maximum(i - pad_tiles, 0), 0)),
                  pl.BlockSpec((1, d), lambda i: (0, 0))],
        out_specs=[pl.BlockSpec((tt, d), lambda i: (i, 0)), pl.BlockSpec((1, d), lambda i: (0, 0)),
                   pl.BlockSpec((8, 128), lambda i: (0, 0))],
        out_shape=(sd((lp, d), F32), sd((1, d), F32), sd((8, 128), F32)),
        scratch_shapes=[pltpu.VMEM((1, 1), F32)],
        compiler_params=_params(),
    )(h, target, g)


def _peers(x, y, c):
    out = []
    for k in range(1, N_DEV):
        px = 1 - x if k & 4 else x
        py = 1 - y if k & 2 else y
        pc = 1 - c if k & 1 else c
        out.append((k, (px, py, pc), 4 * px + 2 * py + pc))
    return out


def exchange_start(arrs, gather, after, name):
    n = len(arrs)
    me = 4 * lax.axis_index("x") + 2 * lax.axis_index("y") + lax.axis_index("c")
    lands = []
    for a in arrs:
        own = a[None] if gather else lax.dynamic_index_in_dim(a, me, 0, keepdims=True)
        lands.append(lax.dynamic_update_index_in_dim(lax.empty(((N_DEV,) + a.shape) if gather else a.shape, a.dtype), own, me, 0))

    def body(*refs):
        ins, land = refs[:n], refs[n:2 * n]
        send_sems, recv_sems, token = refs[2 * n + 1], refs[2 * n + 2], refs[4 * n + 3]
        x, y, c = lax.axis_index("x"), lax.axis_index("y"), lax.axis_index("c")
        me = 4 * x + 2 * y + c
        for k, pid, peer in _peers(x, y, c):
            for a in range(n):
                pltpu.make_async_remote_copy(
                    src_ref=ins[a] if gather else ins[a].at[peer], dst_ref=land[a].at[me],
                    send_sem=send_sems.at[a * (N_DEV - 1) + k - 1], recv_sem=recv_sems.at[a * (N_DEV - 1) + k - 1],
                    device_id=pid, device_id_type=pl.DeviceIdType.MESH).start()
        token[...] = jnp.zeros_like(token)

    hbm = pl.BlockSpec(memory_space=pltpu.HBM)
    sem = pl.BlockSpec(memory_space=pltpu.SEMAPHORE)
    sems = pltpu.SemaphoreType.DMA((n * (N_DEV - 1),))
    res = pl.pallas_call(
        body, name=name, in_specs=[hbm] * (2 * n) + [ANY],
        out_specs=[sem, sem] + [hbm] * (2 * n) + [pl.BlockSpec(memory_space=pltpu.VMEM)],
        out_shape=[sems, sems] + [pltpu.HBM(a.shape, a.dtype) for a in arrs] + [pltpu.HBM(l.shape, l.dtype) for l in lands]
        + [jax.ShapeDtypeStruct((8, 128), F32)],
        input_output_aliases={a: 2 + a for a in range(2 * n)},
        compiler_params=pltpu.CompilerParams(has_side_effects=pltpu.SideEffectType.DATAFLOW_SIDE_EFFECTING),
    )(*[pltpu.with_memory_space_constraint(a, pltpu.HBM) for a in list(arrs) + lands], after)
    return res[0], res[1], res[2:2 + n], res[2 + n:2 + 2 * n], res[-1]


def exchange_wait(started, gather, after, name):
    send_sems, recv_sems, srcs, lands, _ = started
    n = len(srcs)
    after = list(after) if isinstance(after, (list, tuple)) else [after]

    def body(*refs):
        ins, land = refs[:n], refs[n:2 * n]
        send_sems, recv_sems = refs[2 * n], refs[2 * n + 1]
        x, y, c = lax.axis_index("x"), lax.axis_index("y"), lax.axis_index("c")
        for k, pid, peer in _peers(x, y, c):
            for a in range(n):
                cp = pltpu.make_async_remote_copy(
                    src_ref=ins[a] if gather else ins[a].at[peer], dst_ref=land[a].at[peer],
                    send_sem=send_sems.at[a * (N_DEV - 1) + k - 1], recv_sem=recv_sems.at[a * (N_DEV - 1) + k - 1],
                    device_id=pid, device_id_type=pl.DeviceIdType.MESH)
                cp.wait_send()
                cp.wait_recv()

    hbm = pl.BlockSpec(memory_space=pltpu.HBM)
    sem = pl.BlockSpec(memory_space=pltpu.SEMAPHORE)
    res = pl.pallas_call(
        body, name=name, in_specs=[hbm] * (2 * n) + [sem, sem] + [ANY] * len(after),
        out_specs=[hbm] * (2 * n),
        out_shape=[pltpu.HBM(a.shape, a.dtype) for a in list(srcs) + list(lands)],
        input_output_aliases={a: a for a in range(2 * n)},
        compiler_params=pltpu.CompilerParams(has_side_effects=pltpu.SideEffectType.DATAFLOW_SIDE_EFFECTING),
    )(*srcs, *lands, send_sems, recv_sems, *after)
    return res[n:]


def _adamw(w, g, m, v):
    m = ADAM_B1 * m + (1.0 - ADAM_B1) * g
    v = ADAM_B2 * v + (1.0 - ADAM_B2) * (g * g)
    m_hat = m / (1.0 - ADAM_B1 ** ADAM_STEP)
    v_hat = v / (1.0 - ADAM_B2 ** ADAM_STEP)
    return -ADAM_LR * (m_hat / (jnp.sqrt(v_hat) + ADAM_EPS) + ADAM_WD * w), m, v


def _update_tile_rows(rows, cols):
    if rows * cols <= UPDATE_TILE_ELEMS:
        return rows
    return max(t for t in range(8, UPDATE_TILE_ELEMS // cols + 1, 8) if rows % t == 0)


def _sum_in_order(p_ref):
    g = p_ref[0].astype(F32)
    for j in range(1, p_ref.shape[0]):
        g = g + p_ref[j].astype(F32)
    return g


def sum_parts(parts, name):
    nparts, rows, cols = parts.shape
    tr = _update_tile_rows(rows, cols)

    def body(p_ref, g_ref):
        g_ref[...] = _sum_in_order(p_ref)

    return pl.pallas_call(
        body, name=name, grid=(rows // tr,),
        in_specs=[pl.BlockSpec((nparts, tr, cols), lambda i: (0, i, 0))],
        out_specs=pl.BlockSpec((tr, cols), lambda i: (i, 0)), out_shape=jax.ShapeDtypeStruct((rows, cols), F32),
        compiler_params=_params(),
    )(parts)


def sum_adamw(parts, w, m, v, name):
    rows, cols = w.shape
    nparts = parts.shape[0]
    tr = _update_tile_rows(rows, cols)

    def body(p_ref, w_ref, m_ref, v_ref, g_ref, d_ref, nm_ref, nv_ref):
        g = _sum_in_order(p_ref)
        delta, nm, nv = _adamw(w_ref[...], g, m_ref[...], v_ref[...])
        g_ref[...] = g
        d_ref[...] = delta
        nm_ref[...] = nm
        nv_ref[...] = nv

    blk = pl.BlockSpec((tr, cols), lambda i: (i, 0))
    sd = jax.ShapeDtypeStruct((rows, cols), F32)
    return pl.pallas_call(
        body, name=name, grid=(rows // tr,),
        in_specs=[pl.BlockSpec((nparts, tr, cols), lambda i: (0, i, 0)), blk, blk, blk],
        out_specs=[blk] * 4, out_shape=(sd,) * 4,
        compiler_params=_params(),
    )(parts, w, m, v)


def update_packed(g, w, m, v, pieces, name):
    rows_all = w.shape[0]

    def body(g_ref, w_ref, m_ref, v_ref, *outs):
        gv = g_ref[:rows_all, :]
        res = (gv,) + _adamw(w_ref[...], gv, m_ref[...], v_ref[...])
        for p, (row, rows, lanes) in enumerate(pieces):
            for k in range(4):
                outs[4 * p + k][...] = res[k][row:row + rows, :lanes]

    shapes = [jax.ShapeDtypeStruct((rows, lanes), F32) for _, rows, lanes in pieces for _ in range(4)]
    return pl.pallas_call(body, name=name, out_shape=shapes,
                          compiler_params=pltpu.CompilerParams(vmem_limit_bytes=VMEM_LIMIT))(g, w, m, v)


def update_natural(groups, name):
    n = len(groups)
    steps = 8

    def body(*refs):
        ins, outs = refs[:4 * n], refs[4 * n:]
        for j in range(n):
            g_ref, w_ref, m_ref, v_ref = ins[4 * j:4 * j + 4]
            gv = g_ref[...]
            res = (gv,) + _adamw(w_ref[...], gv, m_ref[...], v_ref[...])
            for k in range(4):
                outs[4 * j + k][...] = res[k]

    specs, shapes = [], []
    for g, w, m, v in groups:
        rows, cols = w.shape
        specs += [pl.BlockSpec((rows // steps, cols), lambda i: (i, 0))] * 4
        shapes += [jax.ShapeDtypeStruct((rows, cols), F32)] * 4
    return pl.pallas_call(body, name=name, grid=(steps,), in_specs=specs, out_specs=specs, out_shape=shapes,
                          compiler_params=_params())(*[a for grp in groups for a in grp])


S5_NAMES = ("w_in", "lam_re", "lam_im", "log_dt", "b_re", "b_im", "c_re", "c_im", "d_skip", "w_glu", "b_glu", "w_out")
CONV_NAMES = ("w_in", "conv_w", "conv_b", "w_out")
POOL_NAMES = ("w_in", "w_grp", "b_grp", "scale", "w_out")
LAYER_KINDS = ("s5", "conv", "pool", "s5")
LAYER_NAMES = {"s5": S5_NAMES, "conv": CONV_NAMES, "pool": POOL_NAMES}
SHARDED = {"s5": ("w_in", "w_glu", "w_out"), "conv": ("w_in", "conv_w", "w_out"), "pool": ("w_in", "w_grp", "b_grp", "w_out")}
GATHER_F32 = ("conv_w", "b_grp")


def weight_names():
    names = ["meta_tokens"]
    for i, kind in enumerate(LAYER_KINDS):
        names.append("norm%d_g" % i)
        names += ["l%d_%s" % (i, n) for n in LAYER_NAMES[kind]]
    names.append("final_g")
    return names


def sharded_names():
    return ["meta_tokens"] + ["l%d_%s" % (i, n) for i, kind in enumerate(LAYER_KINDS) for n in SHARDED[kind]]


def _block_diag_in_grad(blocks):
    _, gc, i, p = blocks.shape
    return jnp.transpose(blocks, (2, 0, 1, 3)).reshape(i, 4 * gc, p)


def _block_diag_out_grad(blocks):
    _, gc, i, p = blocks.shape
    return blocks.reshape(4 * gc, i, p)


def _to_owner_blocks(a, axis):
    shape = a.shape[:axis] + (N_DEV, a.shape[axis] // N_DEV) + a.shape[axis + 1:]
    return jnp.moveaxis(a.reshape(shape), axis, 0)


def _from_owner_blocks(a, axis):
    a = jnp.moveaxis(a, 0, axis)
    return a.reshape(a.shape[:axis] + (a.shape[axis] * a.shape[axis + 1],) + a.shape[axis + 2:])


def _step(x, target, weights, moments_m, moments_v):
    seq, d = x.shape[1], x.shape[2]
    n_meta = weights["meta_tokens"].shape[0]
    tt = TOKEN_TILE
    pad_tiles = -(-n_meta // tt)
    p0 = pad_tiles * tt
    lp = p0 + seq
    first_pos = p0 - n_meta
    gc = d // 4 // S5_GROUP
    cw = d // 4

    big_names = [n for n in sharded_names() if n != "meta_tokens" and n.split("_", 1)[1] not in GATHER_F32]
    small_names = [n for n in sharded_names() if n not in big_names]
    layer_big = [[n for n in big_names if n.startswith("l%d_" % i)] for i in range(len(LAYER_KINDS))]
    layer_big[0] = small_names + layer_big[0]
    gather_started = []
    after = jnp.zeros((8, 128), F32)
    for i, names in enumerate(layer_big):
        gather_started.append(exchange_start([weights[n] if n in small_names else weights[n].astype(BF16) for n in names], True,
                                             after, "gather_start_l%d" % i))
        after = gather_started[-1][4]

    def vec(name):
        return weights[name].reshape(1, -1)

    s5_prep = {}
    for i, kind in enumerate(LAYER_KINDS):
        if kind == "s5":
            p = "l%d_" % i
            lr, li = weights[p + "lam_re"], weights[p + "lam_im"] + after[0, 0]
            ldt = weights[p + "log_dt"].reshape(-1, 1)
            br_t = jnp.transpose(weights[p + "b_re"], (2, 0, 1))
            bi_t = jnp.transpose(weights[p + "b_im"], (2, 0, 1))
            ar, ai, bdre, bdim, cdre, cdim = s5_disc_fwd(lr, li, ldt, br_t, bi_t, weights[p + "c_re"], weights[p + "c_im"],
                                                         p + "disc_fwd")
            s5_prep[i] = dict(
                disc=(lr, li, ldt, br_t, bi_t), ar=ar.reshape(4, -1, 128), ai=ai.reshape(4, -1, 128),
                bdre=bdre, bdim=bdim, cdre=cdre, cdim=cdim,
                d_skip=weights[p + "d_skip"].reshape(4, 1, cw), b_glu=vec(p + "b_glu"))
    h = jnp.concatenate([jnp.zeros((p0, d), F32), x[0] + after[0, 0]], axis=0)

    prepared = [h] + [s5_prep[i][k] for i in s5_prep for k in ("bdre", "bdim", "cdre", "cdim")]
    gathered = dict(zip(layer_big[0], exchange_wait(gather_started[0], True, prepared, "gather_wait_l0")))
    h = lax.dynamic_update_slice(h, _from_owner_blocks(gathered["meta_tokens"], 1), (first_pos, 0))

    full = {}

    def layer_weights(i, kind, after):
        p = "l%d_" % i
        if i > 0:
            gathered.update(zip(layer_big[i], exchange_wait(gather_started[i], True, after, "gather_wait_l%d" % i)))
        w_in = gathered[p + "w_in"]
        if kind == "s5":
            full[i] = dict(s5_prep[i], w_in=w_in, w_glu=gathered[p + "w_glu"].reshape(4, cw, d),
                           w_out=gathered[p + "w_out"].reshape(4, cw, d))
        elif kind == "conv":
            ce = w_in.shape[2]
            nch = 2
            conv_w = _from_owner_blocks(gathered[p + "conv_w"], 1)
            full[i] = dict(
                w_in=w_in, conv_w=jnp.transpose(conv_w.reshape(CONV_K, nch, ce), (1, 0, 2)),
                conv_b=weights[p + "conv_b"].reshape(nch, 1, ce), w_out=gathered[p + "w_out"].reshape(nch, ce, d))
        else:
            gw = w_in.shape[2]
            full[i] = dict(
                w_in=w_in, w_grp=_from_owner_blocks(gathered[p + "w_grp"], 1),
                b_grp=_from_owner_blocks(gathered[p + "b_grp"], 1).reshape(4, 1, gw),
                scale=weights[p + "scale"].reshape(4, 1, gw), w_out=gathered[p + "w_out"].reshape(4, gw, d))
        return full[i]

    saved = {}
    for i, kind in enumerate(LAYER_KINDS):
        p, f, g = "l%d_" % i, layer_weights(i, kind, h), vec("norm%d_g" % i)
        if kind == "s5":
            u, z, xs = s5_fwd1(h, g, f["w_in"], f["bdre"], f["bdim"], p + "fwd_in")
            s = s5_scan_fwd(xs, f["ar"], f["ai"], p + "scan_fwd")
            h_in = h
            h, y, q = s5_fwd3(s, u, z, h, f["cdre"], f["cdim"], f["w_glu"], f["w_out"], f["d_skip"], f["b_glu"], p + "fwd_out")
            saved[i] = (h_in, u, z, s, y, q)
        elif kind == "conv":
            h_new, halos, acts = conv_fwd(h, g, f["w_in"], f["conv_w"], f["conv_b"], f["w_out"], p + "fwd")
            saved[i] = (h, halos, acts)
            h = h_new
        else:
            h_new, halos, acts = pool_fwd(h, g, f["w_in"], f["w_grp"], f["b_grp"], f["scale"], f["w_out"], first_pos, p + "fwd")
            saved[i] = (h, halos, acts)
            h = h_new

    dh, dg_final, loss_tile = loss_head(h, target[0], vec("final_g"), pad_tiles, "loss_head")
    loss = lax.psum(loss_tile[0, 0], ("x", "y", "c"))

    grads = {"final_g": dg_final}
    names = weight_names()
    sh_names = sharded_names()
    replicated = [n for n in names if n not in sh_names]
    vectors = [n for n in replicated if weights[n].ndim == 1]
    matrices = [n for n in replicated if weights[n].ndim > 1]
    rep_names = vectors + matrices

    def owner_blocks(a):
        return a.reshape(N_DEV, -1, a.shape[-1]).astype(BF16)

    def as2d(a):
        return a.reshape(-1, a.shape[-1])

    def pack(tree, which=None):
        flat = [jnp.pad(tree[n].reshape(-1), (0, -tree[n].size % PACK_ALIGN)) for n in (which or rep_names)]
        flat = jnp.concatenate(flat)
        if which is None:
            flat = jnp.pad(flat, (0, -flat.size % (PACK_ROWS * 128)))
        return flat.reshape(-1, 128)

    layer_sharded, scatter_started = {}, {}
    ordered = jnp.zeros((), F32)
    for i in reversed(range(len(LAYER_KINDS))):
        kind = LAYER_KINDS[i]
        p, f, g = "l%d_" % i, full[i], vec("norm%d_g" % i) + ordered
        if kind == "s5":
            h_in, u, z, s, y, q = saved[i]
            dy, dp, dwo, dwg, dbg = s5_bwd3a(dh, y, q, z, f["w_glu"], f["w_out"], f["b_glu"] + ordered, p + "bwd_out")
            d_skip = f["d_skip"]
            if i == 0:
                early_names = [p + "w_glu", p + "w_out"]
                scatter_started["early"] = exchange_start([owner_blocks(dwg), owner_blocks(dwo)], False, dy,
                                                          "scatter_start_l0_early")
                d_skip = d_skip + scatter_started["early"][4][0, 0]
            ds, dus, dcre, dcim, dd = s5_bwd3b(dy, s, u, f["cdre"], f["cdim"], d_skip, p + "bwd_read")
            lam, dar, dai = s5_scan_bwd(ds, s, f["ar"], f["ai"], p + "scan_bwd")
            dp, dh, n, dbre, dbim, dg = s5_bwd1(lam, dus, u, dp, h_in, dh, g, f["w_in"], f["bdre"], f["bdim"], p + "bwd_in")
            dw_in = grad_w_in(n, dp, f["w_in"].shape[2], p + "grad_w_in")
            grads.update({p + "w_in": dw_in, p + "w_glu": dwg.reshape(N_DEV, -1, d), p + "w_out": dwo.reshape(N_DEV, -1, d),
                          p + "d_skip": dd, p + "b_glu": dbg})

            def replicated_grads(p=p, f=f, dar=dar, dai=dai, dbre=dbre, dbim=dbim, dcre=dcre, dcim=dcim, token=None):
                lr, li, ldt, br_t, bi_t = f["disc"]
                dlr, dli, dldt, dbr_t, dbi_t = s5_disc_bwd(
                    lr, li, ldt, br_t, bi_t, dar.reshape(lr.shape) + token, dai.reshape(lr.shape),
                    _block_diag_in_grad(dbre), _block_diag_in_grad(dbim), p + "disc_bwd")
                grads.update({
                    p + "lam_re": dlr, p + "lam_im": dli, p + "log_dt": dldt,
                    p + "b_re": jnp.transpose(dbr_t, (1, 2, 0)), p + "b_im": jnp.transpose(dbi_t, (1, 2, 0)),
                    p + "c_re": _block_diag_out_grad(dcre), p + "c_im": -_block_diag_out_grad(dcim)})
        elif kind == "conv":
            replicated_grads = None
            h_in, halos, acts = saved[i]
            dh, n, dp, dwo, dcw, dcb, dg = conv_bwd(h_in, dh, halos, acts, g, f["w_in"], f["conv_w"], f["conv_b"], f["w_out"], p + "bwd")
            dw_in = grad_w_in(n, dp, f["w_in"].shape[2], p + "grad_w_in")
            dconv_w = jnp.transpose(dcw[:, :CONV_K, :], (1, 0, 2)).reshape(CONV_K, -1)
            grads.update({p + "w_in": dw_in, p + "conv_w": _to_owner_blocks(dconv_w, 1), p + "conv_b": dcb,
                          p + "w_out": dwo.reshape(N_DEV, -1, d)})
        else:
            replicated_grads = None
            h_in, halos, acts = saved[i]
            dh, n, dp, dwo, dwgrp, dbgrp, dsc, dg = pool_bwd(h_in, dh, halos, acts, g, f["w_in"], f["w_grp"], f["b_grp"], f["scale"],
                                                             f["w_out"], first_pos, p + "bwd")
            dw_in = grad_w_in(n, dp, f["w_in"].shape[2], p + "grad_w_in")
            grads.update({p + "w_in": dw_in, p + "w_grp": _to_owner_blocks(dwgrp, 1),
                          p + "b_grp": _to_owner_blocks(dbgrp.reshape(4, -1), 1), p + "scale": dsc,
                          p + "w_out": dwo.reshape(N_DEV, -1, d)})
        grads["norm%d_g" % i] = dg
        layer_sharded[i] = ["l%d_%s" % (i, n) for n in SHARDED[kind]]
        if i > 0:
            scatter_started[i] = exchange_start([owner_blocks(grads[n]) for n in layer_sharded[i]], False, dh,
                                                "scatter_start_l%d" % i)
            ordered = scatter_started[i][4][0, 0]
        if replicated_grads is not None:
            replicated_grads(token=ordered)
    grad_x = dh[p0:][None]
    grads["meta_tokens"] = _to_owner_blocks(dh[first_pos:p0], 1)
    last = len(LAYER_KINDS)
    layer_sharded[last] = ["meta_tokens", "replicated"]
    scatter_started[last] = exchange_start([owner_blocks(grads["meta_tokens"]), pack(grads).reshape(N_DEV, -1, 128)], False,
                                           dh, "scatter_start_replicated")
    layer_sharded["early"] = early_names
    layer_sharded[0] = [n for n in layer_sharded[0] if n not in early_names]

    out = {}
    received = {}
    after = [scatter_started[last][4]]
    for i in list(reversed(range(1, last))) + [last, "early", 0]:
        received.update(zip(layer_sharded[i], exchange_wait(scatter_started[i], False, after, "scatter_wait_%s" % i)))
        updated = []
        for n in layer_sharded[i]:
            if n != "replicated":
                res = sum_adamw(received[n], as2d(weights[n]), as2d(moments_m[n]), as2d(moments_v[n]), "update_" + n)
                out[n] = [r.reshape(weights[n].shape) for r in res]
                updated.append(out[n][0])
        after = updated or after
        if i == last:
            g_sum = sum_parts(received["replicated"], "sum_replicated")
            small_gather = exchange_start([g_sum], True, g_sum, "gather_small_grads_start")
            scatter_started[0] = exchange_start([owner_blocks(grads[n]) for n in layer_sharded[0]], False, small_gather[4],
                                                "scatter_start_l0")
            g_full = exchange_wait(small_gather, True, scatter_started[0][4], "gather_small_grads_wait")[0].reshape(-1, 128)
            offsets, offset = {}, 0
            for n in rep_names:
                offsets[n] = offset
                offset += weights[n].size + (-weights[n].size % PACK_ALIGN)
            pieces = [(offsets[n] // 128, max(weights[n].size // 128, 1), min(weights[n].size, 128)) for n in vectors]
            res = update_packed(g_full, pack(weights, vectors), pack(moments_m, vectors), pack(moments_v, vectors), pieces,
                                "update_replicated_vectors")
            for j, n in enumerate(vectors):
                out[n] = [r.reshape(weights[n].shape) for r in res[4 * j:4 * j + 4]]
            flat = g_full.reshape(-1)
            groups = [(flat[offsets[n]:offsets[n] + weights[n].size].reshape(as2d(weights[n]).shape), as2d(weights[n]),
                       as2d(moments_m[n]), as2d(moments_v[n])) for n in matrices]
            res = update_natural(groups, "update_replicated_matrices")
            for j, n in enumerate(matrices):
                out[n] = [r.reshape(weights[n].shape) for r in res[4 * j:4 * j + 4]]
            after = [out[n][k] for n in rep_names for k in range(4)]

    return (loss, grad_x) + tuple(out[n][k] for k in range(4) for n in names)


def kernel(x, *rest):
    names = weight_names()
    nw = len(names)
    weights = dict(zip(names, rest[:nw]))
    target = rest[nw]
    moments_m = dict(zip(names, rest[nw + 1:2 * nw + 1]))
    moments_v = dict(zip(names, rest[2 * nw + 1:3 * nw + 1]))
    return _step(x, target, weights, moments_m, moments_v)
```

```python
import math

import jax
import jax.numpy as jnp
from jax import lax
from jax.experimental import pallas as pl
from jax.experimental.pallas import tpu as pltpu

F32 = jnp.float32
BF16 = jnp.bfloat16
EPS = 1e-6
N_DEV = 8
TOKEN_TILE = 256
SCAN_CHUNKS = 4
S5_GROUP = 16
S5_STATE = 64
POOL_WINDOWS = (2, 4, 8, 16)
POOL_HALO = 16
CONV_K = 3
CONV_HALO = 8
ADAM_LR = 0.001
ADAM_B1 = 0.9
ADAM_B2 = 0.999
ADAM_EPS = 1e-08
ADAM_WD = 0.01
ADAM_STEP = 10
GELU_C = math.sqrt(2.0 / math.pi)
GELU_A = 0.044715
UPDATE_TILE_ELEMS = 1 << 17
PACK_ROWS = 512
PACK_ALIGN = 8 * 128
HIGH_HALF = -65536
HALF_OF_LOW_HALF = 0x8000
VMEM_LIMIT = 56 << 20
VMEM_LIMIT_LARGE = 62 << 20

ANY = pl.BlockSpec(memory_space=pl.ANY)


def _params(vmem=VMEM_LIMIT, ndim=1):
    return pltpu.CompilerParams(vmem_limit_bytes=vmem, dimension_semantics=("arbitrary",) * ndim)


def _dot(a, b):
    return jnp.dot(a.astype(BF16), b.astype(BF16), preferred_element_type=F32)


def _dot_nt(a, b):
    return lax.dot_general(a.astype(BF16), b.astype(BF16), (((1,), (1,)), ((), ())), preferred_element_type=F32)


def _dot_tn(a, b):
    return lax.dot_general(a.astype(BF16), b.astype(BF16), (((0,), (0,)), ((), ())), preferred_element_type=F32)


def _rms_fwd(h, g):
    r = lax.rsqrt(jnp.mean(h * h, axis=-1, keepdims=True) + EPS)
    hh = h * r
    return hh * g, hh, r


def _rms_bwd(dn, hh, r, g):
    dhh = dn * g
    return r * (dhh - hh * jnp.mean(dhh * hh, axis=-1, keepdims=True))


def _sigmoid(x):
    return 1.0 / (1.0 + jnp.exp(-x))


def _silu_and_grad(z):
    s = _sigmoid(z)
    return z * s, s * (1.0 + z * (1.0 - s))


def _gelu(y):
    t = jnp.tanh(GELU_C * (y + GELU_A * y * y * y))
    return 0.5 * y * (1.0 + t), t


def _gelu_grad(y, t):
    return 0.5 * (1.0 + t) + 0.5 * y * (1.0 - t * t) * GELU_C * (1.0 + 3.0 * GELU_A * y * y)


def _rows(shape):
    return lax.broadcasted_iota(jnp.int32, shape, 0)


def _shift_down(x, k, halo):
    y = pltpu.roll(x, k, 0)
    rows = _rows(x.shape)
    for j in range(k):
        y = jnp.where(rows == j, halo[halo.shape[0] - k + j:halo.shape[0] - k + j + 1, :], y)
    return y


def _shift_up(x, k, halo):
    n = x.shape[0]
    y = pltpu.roll(x, n - k, 0)
    rows = _rows(x.shape)
    for j in range(k):
        y = jnp.where(rows == n - k + j, halo[j:j + 1, :], y)
    return y


def _window_sums_back(ext):
    out = []
    s = ext
    for k in (1, 2, 4, 8):
        s = s + pltpu.roll(s, k, 0)
        out.append(s)
    return out


def _window_sums_fwd(ext):
    n = ext.shape[0]
    out = []
    s = ext
    for k in (1, 2, 4, 8):
        s = s + pltpu.roll(s, n - k, 0)
        out.append(s)
    return out


def _pool_inv_count(tile, tt, first_pos, w, width):
    pos = _rows((tt, width)) + (tile * tt - first_pos + 1)
    return 1.0 / jnp.clip(pos, 1, w).astype(F32)


def _slab_spec(lp, tt, sw):
    nj = sw // 128
    return pl.BlockSpec((4 * tt * nj, 128), lambda i: (i, 0)), (lp * 4 * nj, 128)


def _pack_pair(re, im):
    def rounded(v):
        return lax.bitcast_convert_type(v, jnp.int32) + HALF_OF_LOW_HALF
    return lax.bitcast_convert_type((rounded(re) & HIGH_HALF) | lax.shift_right_logical(rounded(im), 16), F32)


def _unpack_pair(w):
    b = lax.bitcast_convert_type(w, jnp.int32)
    return lax.bitcast_convert_type(b & HIGH_HALF, F32), lax.bitcast_convert_type(lax.shift_left(b, 16), F32)


def _slab_load(ref, c):
    nj = ref.shape[0] // (4 * TOKEN_TILE)
    first = c * TOKEN_TILE * nj
    return _unpack_pair(jnp.concatenate([ref[pl.ds(first + j, TOKEN_TILE, stride=nj), :] for j in range(nj)], axis=1))


def _slab_store(ref, c, re, im):
    nj = ref.shape[0] // (4 * TOKEN_TILE)
    first = c * TOKEN_TILE * nj
    val = _pack_pair(re, im)
    for j in range(nj):
        ref[pl.ds(first + j, TOKEN_TILE, stride=nj), :] = val[:, j * 128:(j + 1) * 128]


def _s5_disc_math(lr, li, ldt, br, bi):
    dt = jnp.exp(ldt)
    mag = jnp.exp(lr * dt)
    ar = mag * jnp.cos(li * dt)
    ai = mag * jnp.sin(li * dt)
    den = lr * lr + li * li
    kr = ((ar - 1.0) * lr + ai * li) / den
    ki = (ai * lr - (ar - 1.0) * li) / den
    bbr = kr[None] * br - ki[None] * bi
    bbi = kr[None] * bi + ki[None] * br
    return ar, ai, bbr, bbi


def s5_disc_fwd(lr, li, ldt, br_t, bi_t, c_re, c_im, name):
    ni, ng, npp = br_t.shape
    gc = ng // 4

    def body(lr_ref, li_ref, ldt_ref, br_ref, bi_ref, cre_ref, cim_ref,
             ar_ref, ai_ref, bdre_ref, bdim_ref, cdre_ref, cdim_ref, bbr_sc, bbi_sc):
        ar, ai, bbr, bbi = _s5_disc_math(lr_ref[...], li_ref[...], ldt_ref[...], br_ref[...], bi_ref[...])
        ar_ref[...] = ar
        ai_ref[...] = ai
        bbr_sc[...] = bbr
        bbi_sc[...] = bbi
        for ref in (bdre_ref, bdim_ref, cdre_ref, cdim_ref):
            ref[...] = jnp.zeros_like(ref)
        for k in range(4):
            for j in range(gc):
                g = k * gc + j
                ins, states = pl.ds(j * ni, ni), pl.ds(j * npp, npp)
                bdre_ref[k, ins, states] = bbr_sc[:, g, :].astype(BF16)
                bdim_ref[k, ins, states] = bbi_sc[:, g, :].astype(BF16)
                cdre_ref[k, states, ins] = cre_ref[g].T.astype(BF16)
                cdim_ref[k, states, ins] = (-cim_ref[g]).T.astype(BF16)

    sd = jax.ShapeDtypeStruct
    return pl.pallas_call(
        body, name=name,
        out_shape=(sd(lr.shape, F32), sd(lr.shape, F32), sd((4, gc * ni, gc * npp), BF16), sd((4, gc * ni, gc * npp), BF16),
                   sd((4, gc * npp, gc * ni), BF16), sd((4, gc * npp, gc * ni), BF16)),
        scratch_shapes=[pltpu.VMEM(br_t.shape, F32), pltpu.VMEM(br_t.shape, F32)],
        compiler_params=pltpu.CompilerParams(vmem_limit_bytes=VMEM_LIMIT),
    )(lr, li, ldt, br_t, bi_t, c_re, c_im)


def s5_disc_bwd(lr, li, ldt, br_t, bi_t, dar, dai, dbbr, dbbi, name):
    def body(lr_ref, li_ref, ldt_ref, br_ref, bi_ref, dar_ref, dai_ref, dbbr_ref, dbbi_ref,
             dlr_ref, dli_ref, dldt_ref, dbr_ref, dbi_ref):
        _, vjp = jax.vjp(_s5_disc_math, lr_ref[...], li_ref[...], ldt_ref[...], br_ref[...], bi_ref[...])
        dlr, dli, dldt, dbr, dbi = vjp((dar_ref[...], dai_ref[...], dbbr_ref[...], dbbi_ref[...]))
        dlr_ref[...] = dlr
        dli_ref[...] = dli
        dldt_ref[...] = dldt
        dbr_ref[...] = dbr
        dbi_ref[...] = dbi

    sd = jax.ShapeDtypeStruct
    return pl.pallas_call(
        body, name=name,
        out_shape=(sd(lr.shape, F32), sd(lr.shape, F32), sd(ldt.shape, F32), sd(br_t.shape, F32), sd(br_t.shape, F32)),
    )(lr, li, ldt, br_t, bi_t, dar, dai, dbbr, dbbi)


def s5_fwd1(h, g, w_in, bdre, bdim, name):
    lp, d = h.shape
    tt = TOKEN_TILE
    cw, sw = bdre.shape[1], bdre.shape[2]

    def body(h_ref, g_ref, w_hbm, bdre_hbm, bdim_hbm, u_ref, z_ref, x_ref, w, bre, bim):
        @pl.when(pl.program_id(0) == 0)
        def _():
            pltpu.sync_copy(w_hbm, w)
            pltpu.sync_copy(bdre_hbm, bre)
            pltpu.sync_copy(bdim_hbm, bim)

        n = _rms_fwd(h_ref[...], g_ref[...])[0].astype(BF16)
        for c in range(4):
            cols = slice(c * cw, (c + 1) * cw)
            u = jnp.dot(n, w[c], preferred_element_type=F32)
            u_ref[:, cols] = u
            z_ref[:, cols] = jnp.dot(n, w[c + 4], preferred_element_type=F32)
            ub = u.astype(BF16)
            _slab_store(x_ref, c, jnp.dot(ub, bre[c], preferred_element_type=F32), jnp.dot(ub, bim[c], preferred_element_type=F32))

    sd = jax.ShapeDtypeStruct
    slab, slab_shape = _slab_spec(lp, tt, sw)
    row = pl.BlockSpec((tt, d), lambda i: (i, 0))
    return pl.pallas_call(
        body, name=name, grid=(lp // tt,),
        in_specs=[row, pl.BlockSpec((1, d), lambda i: (0, 0)), ANY, ANY, ANY],
        out_specs=[row, row, slab],
        out_shape=(sd((lp, d), F32), sd((lp, d), F32), sd(slab_shape, F32)),
        scratch_shapes=[pltpu.VMEM(w_in.shape, BF16), pltpu.VMEM(bdre.shape, BF16), pltpu.VMEM(bdim.shape, BF16)],
        compiler_params=_params(),
    )(h, g, w_in, bdre, bdim)


def s5_scan_fwd(x, ar, ai, name):
    nj = ar.shape[1]
    tt = TOKEN_TILE
    cpb = SCAN_CHUNKS
    nt = x.shape[0] // (4 * tt * nj)

    def body(x_ref, ar_ref, ai_ref, s_ref, st_r, st_i):
        i, cg = pl.program_id(0), pl.program_id(1)

        @pl.when(i == 0)
        def _():
            for q in range(cpb):
                st_r[cg * cpb + q] = jnp.zeros((nj, 128), F32)
                st_i[cg * cpb + q] = jnp.zeros((nj, 128), F32)

        a_r = [ar_ref[cg * cpb + q] for q in range(cpb)]
        a_i = [ai_ref[cg * cpb + q] for q in range(cpb)]

        def step(t, carry):
            out = []
            for q in range(cpb):
                s_r, s_i = carry[q]
                rows = pl.ds(pl.multiple_of((q * tt + t) * nj, nj), nj)
                x_r, x_i = _unpack_pair(x_ref[rows, :])
                n_r = a_r[q] * s_r - a_i[q] * s_i + x_r
                n_i = a_r[q] * s_i + a_i[q] * s_r + x_i
                s_ref[rows, :] = _pack_pair(n_r, n_i)
                out.append((n_r, n_i))
            return tuple(out)

        init = tuple((st_r[cg * cpb + q], st_i[cg * cpb + q]) for q in range(cpb))
        final = lax.fori_loop(0, tt, step, init, unroll=8)
        for q in range(cpb):
            st_r[cg * cpb + q] = final[q][0]
            st_i[cg * cpb + q] = final[q][1]

    blk = pl.BlockSpec((cpb * tt * nj, 128), lambda i, cg: (i * (4 // cpb) + cg, 0))
    par = pl.BlockSpec((4, nj, 128), lambda i, cg: (0, 0, 0))
    sd = jax.ShapeDtypeStruct
    return pl.pallas_call(
        body, name=name, grid=(nt, 4 // cpb),
        in_specs=[blk, par, par], out_specs=blk,
        out_shape=sd(x.shape, F32),
        scratch_shapes=[pltpu.VMEM((4, nj, 128), F32), pltpu.VMEM((4, nj, 128), F32)],
        compiler_params=_params(ndim=2),
    )(x, ar, ai)


def s5_fwd3(s, u, z, h, cdre, cdim, w_glu, w_out, d_skip, b_glu, name):
    lp, d = h.shape
    tt = TOKEN_TILE
    sw, cw = cdre.shape[1], cdre.shape[2]

    def body(s_ref, u_ref, z_ref, h_ref, d_ref, bg_ref, cre_hbm, cim_hbm, wg_hbm, wo_hbm,
             o_ref, y_ref, q_ref, cre, cim, wg, wo):
        @pl.when(pl.program_id(0) == 0)
        def _():
            pltpu.sync_copy(cre_hbm, cre)
            pltpu.sync_copy(cim_hbm, cim)
            pltpu.sync_copy(wg_hbm, wg)
            pltpu.sync_copy(wo_hbm, wo)

        gys, q = [], None
        for c in range(4):
            cols = slice(c * cw, (c + 1) * cw)
            s_r, s_i = _slab_load(s_ref, c)
            y = _dot(s_r, cre[c]) + _dot(s_i, cim[c]) + d_ref[c] * u_ref[:, cols]
            y_ref[:, cols] = y
            gys.append(_gelu(y)[0])
            part = _dot(gys[c], wg[c])
            q = part if c == 0 else q + part
        q_ref[...] = q
        sig = _sigmoid(q + bg_ref[...])
        zz = z_ref[...]
        sz = zz * _sigmoid(zz)
        o = h_ref[...]
        for k in range(4):
            cols = slice(k * cw, (k + 1) * cw)
            o = o + _dot(gys[k] * sig[:, cols] * sz[:, cols], wo[k])
        o_ref[...] = o

    row = pl.BlockSpec((tt, d), lambda i: (i, 0))
    slab, _ = _slab_spec(lp, tt, sw)
    sd = jax.ShapeDtypeStruct((lp, d), F32)
    return pl.pallas_call(
        body, name=name, grid=(lp // tt,),
        in_specs=[slab, row, row, row, pl.BlockSpec((4, 1, cw), lambda i: (0, 0, 0)), pl.BlockSpec((1, d), lambda i: (0, 0)),
                  ANY, ANY, ANY, ANY],
        out_specs=[row, row, row],
        out_shape=(sd, sd, sd),
        scratch_shapes=[pltpu.VMEM(cdre.shape, BF16), pltpu.VMEM(cdim.shape, BF16), pltpu.VMEM(w_glu.shape, BF16),
                        pltpu.VMEM(w_out.shape, BF16)],
        compiler_params=_params(),
    )(s, u, z, h, d_skip, b_glu, cdre, cdim, w_glu, w_out)


def s5_bwd3a(dh, y, q, z, w_glu, w_out, b_glu, name):
    lp, d = dh.shape
    tt = TOKEN_TILE
    nt = lp // tt
    cw = w_glu.shape[1]

    def body(dh_ref, y_ref, q_ref, z_ref, bg_ref, wg_hbm, wo_hbm, dy_ref, dp_ref, dwo_hbm, dwg_hbm, dbg_hbm,
             wg, wo, dwo, dwg, dbg):
        i = pl.program_id(0)

        @pl.when(i == 0)
        def _():
            pltpu.sync_copy(wg_hbm, wg)
            pltpu.sync_copy(wo_hbm, wo)
            dwo[...] = jnp.zeros_like(dwo)
            dwg[...] = jnp.zeros_like(dwg)
            dbg[...] = jnp.zeros_like(dbg)

        sig = _sigmoid(q_ref[...] + bg_ref[...])
        sz, dsz = _silu_and_grad(z_ref[...])
        dhv = dh_ref[...]
        yv = y_ref[...]
        gy, t = _gelu(yv)
        dq_parts, dgy_parts = [], []
        for k in range(4):
            cols = slice(k * cw, (k + 1) * cw)
            gy_k, sig_k, sz_k = gy[:, cols], sig[:, cols], sz[:, cols]
            y2 = gy_k * sig_k
            dy3 = _dot_nt(dhv, wo[k])
            dwo[k] += _dot_tn(y2 * sz_k, dhv)
            dy2 = dy3 * sz_k
            dp_ref[0, :, cols] = (dy3 * y2 * dsz[:, cols]).astype(BF16)
            dq_parts.append(dy2 * gy_k * sig_k * (1.0 - sig_k))
            dgy_parts.append(dy2 * sig_k)
        dq = jnp.concatenate(dq_parts, axis=1)
        dbg[...] += jnp.sum(dq, axis=0, keepdims=True)
        dgelu = _gelu_grad(yv, t)
        for k in range(4):
            cols = slice(k * cw, (k + 1) * cw)
            dwg[k] += _dot_tn(gy[:, cols], dq)
            dy_ref[:, cols] = (dgy_parts[k] + _dot_nt(dq, wg[k])) * dgelu[:, cols]

        @pl.when(i == nt - 1)
        def _():
            pltpu.sync_copy(dwo, dwo_hbm)
            pltpu.sync_copy(dwg, dwg_hbm)
            pltpu.sync_copy(dbg, dbg_hbm)

    row = pl.BlockSpec((tt, d), lambda i: (i, 0))
    sd = jax.ShapeDtypeStruct
    return pl.pallas_call(
        body, name=name, grid=(nt,),
        in_specs=[row, row, row, row, pl.BlockSpec((1, d), lambda i: (0, 0)), ANY, ANY],
        out_specs=[row, pl.BlockSpec((1, tt, d), lambda i: (1, i, 0)), ANY, ANY, ANY],
        out_shape=(sd((lp, d), F32), sd((2, lp, d), BF16), sd(w_out.shape, F32), sd(w_glu.shape, F32), sd((1, d), F32)),
        scratch_shapes=[pltpu.VMEM(w_glu.shape, BF16), pltpu.VMEM(w_out.shape, BF16),
                        pltpu.VMEM(w_out.shape, F32), pltpu.VMEM(w_glu.shape, F32), pltpu.VMEM((1, d), F32)],
        compiler_params=_params(),
    )(dh, y, q, z, b_glu, w_glu, w_out)


def s5_bwd3b(dy, s, u, cdre, cdim, d_skip, name):
    lp, d = dy.shape
    tt = TOKEN_TILE
    nt = lp // tt
    sw, cw = cdre.shape[1], cdre.shape[2]
    gc = cw // S5_GROUP

    def body(dy_ref, s_ref, u_ref, d_ref, cre_hbm, cim_hbm,
             ds_ref, dus_ref, dcre_ref, dcim_ref, dd_hbm, cre, cim, dcre, dcim, dd):
        i = pl.program_id(0)

        @pl.when(i == 0)
        def _():
            pltpu.sync_copy(cre_hbm, cre)
            pltpu.sync_copy(cim_hbm, cim)
            dcre[...] = jnp.zeros_like(dcre)
            dcim[...] = jnp.zeros_like(dcim)
            dd[...] = jnp.zeros_like(dd)

        for c in range(4):
            chunk = slice(c * cw, (c + 1) * cw)
            dyv = dy_ref[:, chunk]
            dd[c] += jnp.sum(dyv * u_ref[:, chunk], axis=0, keepdims=True)
            dus_ref[:, chunk] = dyv * d_ref[c]
            _slab_store(ds_ref, c, _dot_nt(dyv, cre[c]), _dot_nt(dyv, cim[c]))
            s_r, s_i = _slab_load(s_ref, c)
            dcre[c] += _dot_tn(s_r, dyv)
            dcim[c] += _dot_tn(s_i, dyv)

        @pl.when(i == nt - 1)
        def _():
            for k in range(4):
                for j in range(gc):
                    rows, cols = pl.ds(j * S5_STATE, S5_STATE), pl.ds(j * S5_GROUP, S5_GROUP)
                    dcre_ref[k, j] = dcre[k, rows, cols].T
                    dcim_ref[k, j] = dcim[k, rows, cols].T
            pltpu.sync_copy(dd, dd_hbm)

    sd = jax.ShapeDtypeStruct
    row = pl.BlockSpec((tt, d), lambda i: (i, 0))
    slab, slab_shape = _slab_spec(lp, tt, sw)
    diag = pl.BlockSpec((4, gc, S5_GROUP, S5_STATE), lambda i: (0, 0, 0, 0))
    return pl.pallas_call(
        body, name=name, grid=(nt,),
        in_specs=[row, slab, row, pl.BlockSpec((4, 1, cw), lambda i: (0, 0, 0)), ANY, ANY],
        out_specs=[slab, row, diag, diag, ANY],
        out_shape=(sd(slab_shape, F32), sd((lp, d), F32),
                   sd((4, gc, S5_GROUP, S5_STATE), F32), sd((4, gc, S5_GROUP, S5_STATE), F32), sd((4, 1, cw), F32)),
        scratch_shapes=[pltpu.VMEM(cdre.shape, BF16), pltpu.VMEM(cdim.shape, BF16),
                        pltpu.VMEM(cdre.shape, F32), pltpu.VMEM(cdim.shape, F32), pltpu.VMEM((4, 1, cw), F32)],
        compiler_params=_params(),
    )(dy, s, u, d_skip, cdre, cdim)


def s5_scan_bwd(g, s, ar, ai, name):
    nj = ar.shape[1]
    tt = TOKEN_TILE
    cpb = SCAN_CHUNKS
    nt = g.shape[0] // (4 * tt * nj)

    def body(g_ref, s_ref, ar_ref, ai_ref, lam_ref, dar_ref, dai_ref, st_r, st_i, acc_r, acc_i):
        i, cg = pl.program_id(0), pl.program_id(1)

        @pl.when((i == 0) & (cg == 0))
        def _():
            for ref in (st_r, st_i, acc_r, acc_i):
                ref[...] = jnp.zeros_like(ref)

        a_r = [ar_ref[cg * cpb + q] for q in range(cpb)]
        a_i = [ai_ref[cg * cpb + q] for q in range(cpb)]

        def slab(q, t):
            return pl.ds(pl.multiple_of((q * tt + t) * nj, nj), nj)

        def adjoint(q, t, l_r, l_i):
            rows = slab(q, t)
            g_r, g_i = _unpack_pair(g_ref[rows, :])
            n_r = g_r + a_r[q] * l_r + a_i[q] * l_i
            n_i = g_i + a_r[q] * l_i - a_i[q] * l_r
            lam_ref[rows, :] = _pack_pair(n_r, n_i)
            return n_r, n_i

        def pair(q, t, l_r, l_i, d_r, d_i):
            p_r, p_i = _unpack_pair(s_ref[slab(q, t), :])
            return d_r + l_r * p_r + l_i * p_i, d_i + l_i * p_r - l_r * p_i

        def step(k, carry):
            t = tt - 1 - k
            out = []
            for q in range(cpb):
                l_r, l_i, d_r, d_i = carry[q]
                l_r, l_i = adjoint(q, t, l_r, l_i)
                d_r, d_i = pair(q, t - 1, l_r, l_i, d_r, d_i)
                out.append((l_r, l_i, d_r, d_i))
            return tuple(out)

        init = []
        for q in range(cpb):
            ch = cg * cpb + q
            l_r, l_i = st_r[ch], st_i[ch]
            d_r, d_i = pair(q, tt - 1, l_r, l_i, acc_r[ch], acc_i[ch])
            init.append((l_r, l_i, d_r, d_i))
        final = lax.fori_loop(0, tt - 1, step, tuple(init), unroll=8)
        for q in range(cpb):
            ch = cg * cpb + q
            l_r, l_i, d_r, d_i = final[q]
            l_r, l_i = adjoint(q, 0, l_r, l_i)
            st_r[ch] = l_r
            st_i[ch] = l_i
            acc_r[ch] = d_r
            acc_i[ch] = d_i
            dar_ref[ch] = d_r
            dai_ref[ch] = d_i

    blk = pl.BlockSpec((cpb * tt * nj, 128), lambda i, cg: ((nt - 1 - i) * (4 // cpb) + cg, 0))
    par = pl.BlockSpec((4, nj, 128), lambda i, cg: (0, 0, 0))
    sd = jax.ShapeDtypeStruct
    return pl.pallas_call(
        body, name=name, grid=(nt, 4 // cpb),
        in_specs=[blk, blk, par, par], out_specs=[blk, par, par],
        out_shape=(sd(g.shape, F32), sd((4, nj, 128), F32), sd((4, nj, 128), F32)),
        scratch_shapes=[pltpu.VMEM((4, nj, 128), F32)] * 4,
        compiler_params=_params(ndim=2),
    )(g, s, ar, ai)


def s5_bwd1(lam, dus, u, dp, h, dh, g, w_in, bdre, bdim, name, pad_tiles=None):
    lp, d = h.shape
    tt = TOKEN_TILE
    nt = lp // tt
    cw, sw = bdre.shape[1], bdre.shape[2]
    gc = cw // S5_GROUP

    def body(lam_ref, dus_ref, u_ref, dpz_ref, h_ref, dh_ref, g_ref, w_hbm, bre_hbm, bim_hbm,
             dpu_ref, dho_ref, n_ref, dbre_ref, dbim_ref, dg_hbm, *rest):
        (gx_ref,), (w, bre, bim, dbre, dbim, dg) = (rest[:1], rest[1:]) if pad_tiles is not None else ((None,), rest)
        i = pl.program_id(0)

        @pl.when(i == 0)
        def _():
            pltpu.sync_copy(w_hbm, w)
            pltpu.sync_copy(bre_hbm, bre)
            pltpu.sync_copy(bim_hbm, bim)
            dbre[...] = jnp.zeros_like(dbre)
            dbim[...] = jnp.zeros_like(dbim)
            dg[...] = jnp.zeros_like(dg)

        dz = dpz_ref[0]
        dn = None
        for c in range(4):
            chunk = slice(c * cw, (c + 1) * cw)
            (l_r, l_i), uv = _slab_load(lam_ref, c), u_ref[:, chunk]
            du = dus_ref[:, chunk] + _dot_nt(l_r, bre[c]) + _dot_nt(l_i, bim[c])
            dbre[c] += _dot_tn(uv, l_r)
            dbim[c] += _dot_tn(uv, l_i)
            dpu_ref[0, :, chunk] = du.astype(BF16)
            part = _dot_nt(du, w[c]) + _dot_nt(dz[:, chunk], w[4 + c])
            dn = part if c == 0 else dn + part
        gv = g_ref[...]
        n, hh, rr = _rms_fwd(h_ref[...], gv)
        n_ref[...] = n.T.astype(BF16)
        dg[...] += jnp.sum(dn * hh, axis=0, keepdims=True)
        dh_in = dh_ref[...] + _rms_bwd(dn, hh, rr, gv)
        dho_ref[...] = dh_in
        if pad_tiles is not None:
            @pl.when(i >= pad_tiles)
            def _():
                gx_ref[...] = dh_in

        @pl.when(i == nt - 1)
        def _():
            for k in range(4):
                for j in range(gc):
                    rows, cols = pl.ds(j * S5_GROUP, S5_GROUP), pl.ds(j * S5_STATE, S5_STATE)
                    dbre_ref[k, j] = dbre[k, rows, cols]
                    dbim_ref[k, j] = dbim[k, rows, cols]
            pltpu.sync_copy(dg, dg_hbm)

    sd = jax.ShapeDtypeStruct
    row = pl.BlockSpec((tt, d), lambda i: (i, 0))
    slab, _ = _slab_spec(lp, tt, sw)
    diag = pl.BlockSpec((4, gc, S5_GROUP, S5_STATE), lambda i: (0, 0, 0, 0))
    extra_specs, extra_shapes = [], ()
    if pad_tiles is not None:
        extra_specs = [pl.BlockSpec((tt, d), lambda i: (jnp.maximum(i - pad_tiles, 0), 0))]
        extra_shapes = (sd((lp - pad_tiles * tt, d), F32),)
    return pl.pallas_call(
        body, name=name, grid=(nt,),
        in_specs=[slab, row, row, pl.BlockSpec((1, tt, d), lambda i: (1, i, 0)), row, row, pl.BlockSpec((1, d), lambda i: (0, 0)),
                  ANY, ANY, ANY],
        out_specs=[pl.BlockSpec((1, tt, d), lambda i: (0, i, 0)), row, pl.BlockSpec((d, tt), lambda i: (0, i)), diag, diag, ANY]
        + extra_specs,
        out_shape=(sd(dp.shape, BF16), sd((lp, d), F32), sd((d, lp), BF16),
                   sd((4, gc, S5_GROUP, S5_STATE), F32), sd((4, gc, S5_GROUP, S5_STATE), F32), sd((1, d), F32)) + extra_shapes,
        input_output_aliases={3: 0},
        scratch_shapes=[pltpu.VMEM(w_in.shape, BF16), pltpu.VMEM(bdre.shape, BF16), pltpu.VMEM(bdim.shape, BF16),
                        pltpu.VMEM(bdre.shape, F32), pltpu.VMEM(bdim.shape, F32), pltpu.VMEM((1, d), F32)],
        compiler_params=_params(),
    )(lam, dus, u, dp, h, dh, g, w_in, bdre, bdim)


def grad_w_in(n_t, dp, blk, name):
    d, lp = n_t.shape
    npart, _, width = dp.shape
    per = width // blk

    def body(n_ref, dp_ref, o_ref):
        o_ref[0] = jnp.dot(n_ref[...], dp_ref[0], preferred_element_type=F32).astype(o_ref.dtype)

    return pl.pallas_call(
        body, name=name, grid=(npart * per,),
        in_specs=[pl.BlockSpec((d, lp), lambda j: (0, 0), pipeline_mode=pl.Buffered(1)),
                  pl.BlockSpec((1, lp, blk), lambda j: (j // per, 0, j % per))],
        out_specs=pl.BlockSpec((1, d, blk), lambda j: (j, 0, 0)),
        out_shape=jax.ShapeDtypeStruct((npart * per, d, blk), BF16),
        compiler_params=_params(),
    )(n_t, dp)


def _conv_mix(cg, v, cw_ref, cb_ref, halo, c):
    hc = cg * v
    taps = cw_ref[c]
    conv = taps[2:3, :] * hc + taps[1:2, :] * _shift_down(hc, 1, halo) + taps[0:1, :] * _shift_down(hc, 2, halo) + cb_ref[c]
    return hc, conv


def conv_fwd(h, g, w_in, conv_w, conv_b, w_out, name):
    lp, d = h.shape
    tt = TOKEN_TILE
    nt = lp // tt
    nch, ce = w_out.shape[0], w_out.shape[1]

    def body(h_ref, g_ref, cw_ref, cb_ref, w_hbm, wo_hbm, o_ref, halo_ref, acts_ref, w, wo, halo):
        i = pl.program_id(0)

        @pl.when(i == 0)
        def _():
            pltpu.sync_copy(w_hbm, w)
            pltpu.sync_copy(wo_hbm, wo)
            halo[...] = jnp.zeros_like(halo)

        hv = h_ref[...]
        n = _rms_fwd(hv, g_ref[...])[0].astype(BF16)
        o = hv
        for c in range(nch):
            cols = slice(c * ce, (c + 1) * ce)
            bg, cg, v, z = [jnp.dot(n, w[p * nch + c], preferred_element_type=F32) for p in range(4)]
            for p, val in enumerate((bg, cg, v, z)):
                acts_ref[p, :, cols] = val.astype(BF16)
            hc, conv = _conv_mix(cg, v, cw_ref, cb_ref, halo[c], c)
            o = o + _dot(bg * conv * (z * _sigmoid(z)), wo[c])
            halo[c] = hc[tt - CONV_HALO:, :]
            halo_ref[0, c] = hc[tt - CONV_HALO:, :]
        o_ref[...] = o

    sd = jax.ShapeDtypeStruct
    return pl.pallas_call(
        body, name=name, grid=(nt,),
        in_specs=[pl.BlockSpec((tt, d), lambda i: (i, 0)), pl.BlockSpec((1, d), lambda i: (0, 0)),
                  pl.BlockSpec(conv_w.shape, lambda i: (0, 0, 0)), pl.BlockSpec(conv_b.shape, lambda i: (0, 0, 0)), ANY, ANY],
        out_specs=[pl.BlockSpec((tt, d), lambda i: (i, 0)), pl.BlockSpec((1, nch, CONV_HALO, ce), lambda i: (i, 0, 0, 0)),
                   pl.BlockSpec((4, tt, nch * ce), lambda i: (0, i, 0))],
        out_shape=(sd((lp, d), F32), sd((nt, nch, CONV_HALO, ce), F32), sd((4, lp, nch * ce), BF16)),
        scratch_shapes=[pltpu.VMEM(w_in.shape, BF16), pltpu.VMEM(w_out.shape, BF16), pltpu.VMEM((nch, CONV_HALO, ce), F32)],
        compiler_params=_params(),
    )(h, g, conv_w, conv_b, w_in, w_out)


def conv_bwd(h, dh, halos, acts, g, w_in, conv_w, conv_b, w_out, name):
    lp, d = h.shape
    tt = TOKEN_TILE
    nt = lp // tt
    nch, ce = w_out.shape[0], w_out.shape[1]

    def body(h_ref, dh_ref, halo_ref, acts_ref, g_ref, cw_ref, cb_ref, w_hbm, wo_hbm,
             dho_ref, n_ref, dp_ref, dwo_hbm, dcw_hbm, dcb_hbm, dg_hbm, w, wo, nxt, dwo, dcw, dcb, dg):
        i = pl.program_id(0)

        @pl.when(i == 0)
        def _():
            pltpu.sync_copy(w_hbm, w)
            pltpu.sync_copy(wo_hbm, wo)
            for ref in (nxt, dwo, dcw, dcb, dg):
                ref[...] = jnp.zeros_like(ref)

        gv = g_ref[...]
        nf, hh, rr = _rms_fwd(h_ref[...], gv)
        n_ref[...] = nf.T.astype(BF16)
        dhv = dh_ref[...]
        has_prev = (i < nt - 1).astype(F32)
        dn = jnp.zeros((tt, d), F32)
        for c in range(nch):
            halo = halo_ref[0, c] * has_prev
            cols = slice(c * ce, (c + 1) * ce)
            bg, cg, v, z = [acts_ref[p, :, cols].astype(F32) for p in range(4)]
            hc, conv = _conv_mix(cg, v, cw_ref, cb_ref, halo, c)
            sz, dsz = _silu_and_grad(z)
            y1 = bg * conv
            dy2 = _dot_nt(dhv, wo[c])
            dwo[c] += _dot_tn(y1 * sz, dhv)
            dy1 = dy2 * sz
            dz = dy2 * y1 * dsz
            dbg = dy1 * conv
            dconv = dy1 * bg
            dcb[c] += jnp.sum(dconv, axis=0, keepdims=True)
            up1 = _shift_up(dconv, 1, nxt[c])
            up2 = _shift_up(dconv, 2, nxt[c])
            nxt[c] = dconv[:CONV_HALO, :]
            taps = cw_ref[c]
            dhc = taps[2:3, :] * dconv + taps[1:2, :] * up1 + taps[0:1, :] * up2
            dcw[c, 0:1, :] += jnp.sum(hc * up2, axis=0, keepdims=True)
            dcw[c, 1:2, :] += jnp.sum(hc * up1, axis=0, keepdims=True)
            dcw[c, 2:3, :] += jnp.sum(hc * dconv, axis=0, keepdims=True)
            dcg = dhc * v
            dv = dhc * cg
            for p, val in enumerate((dbg, dcg, dv, dz)):
                dp_ref[p, :, cols] = val.astype(BF16)
                dn = dn + _dot_nt(val, w[p * nch + c])
        dg[...] += jnp.sum(dn * hh, axis=0, keepdims=True)
        dho_ref[...] = dhv + _rms_bwd(dn, hh, rr, gv)

        @pl.when(i == nt - 1)
        def _():
            pltpu.sync_copy(dwo, dwo_hbm)
            pltpu.sync_copy(dcw, dcw_hbm)
            pltpu.sync_copy(dcb, dcb_hbm)
            pltpu.sync_copy(dg, dg_hbm)

    rev = lambda i: (nt - 1 - i, 0)
    sd = jax.ShapeDtypeStruct
    return pl.pallas_call(
        body, name=name, grid=(nt,),
        in_specs=[pl.BlockSpec((tt, d), rev), pl.BlockSpec((tt, d), rev),
                  pl.BlockSpec((1, nch, CONV_HALO, ce), lambda i: (jnp.maximum(nt - 2 - i, 0), 0, 0, 0)),
                  pl.BlockSpec((4, tt, nch * ce), lambda i: (0, nt - 1 - i, 0)),
                  pl.BlockSpec((1, d), lambda i: (0, 0)),
                  pl.BlockSpec(conv_w.shape, lambda i: (0, 0, 0)), pl.BlockSpec(conv_b.shape, lambda i: (0, 0, 0)), ANY, ANY],
        out_specs=[pl.BlockSpec((tt, d), rev), pl.BlockSpec((d, tt), lambda i: (0, nt - 1 - i)),
                   pl.BlockSpec((4, tt, nch * ce), lambda i: (0, nt - 1 - i, 0)), ANY, ANY, ANY, ANY],
        out_shape=(sd((lp, d), F32), sd((d, lp), BF16), sd((4, lp, nch * ce), BF16),
                   sd(w_out.shape, F32), sd((nch, 8, ce), F32), sd((nch, 1, ce), F32), sd((1, d), F32)),
        scratch_shapes=[pltpu.VMEM(w_in.shape, BF16), pltpu.VMEM(w_out.shape, BF16), pltpu.VMEM((nch, CONV_HALO, ce), F32),
                        pltpu.VMEM(w_out.shape, F32), pltpu.VMEM((nch, 8, ce), F32), pltpu.VMEM((nch, 1, ce), F32),
                        pltpu.VMEM((1, d), F32)],
        compiler_params=_params(vmem=VMEM_LIMIT_LARGE),
    )(h, dh, halos, acts, g, conv_w, conv_b, w_in, w_out)


def _pool_mix(u, wg, bg_ref, sc_ref, halo, k, tile, tt, first_pos):
    ext = jnp.concatenate([halo, u], axis=0)
    win = _window_sums_back(ext)[k][POOL_HALO:, :]
    mixed = win * _pool_inv_count(tile, tt, first_pos, POOL_WINDOWS[k], u.shape[1]) - u
    outs = _dot(mixed, wg[k]) + bg_ref[k]
    return mixed, outs, outs * sc_ref[k]


def pool_fwd(h, g, w_in, w_grp, b_grp, scale, w_out, first_pos, name):
    lp, d = h.shape
    tt = TOKEN_TILE
    nt = lp // tt
    gw = w_grp.shape[1]

    def body(h_ref, g_ref, bg_ref, sc_ref, w_hbm, wg_hbm, wo_hbm, o_ref, halo_ref, acts_ref, w, wg, wo, halo):
        i = pl.program_id(0)

        @pl.when(i == 0)
        def _():
            pltpu.sync_copy(w_hbm, w)
            pltpu.sync_copy(wg_hbm, wg)
            pltpu.sync_copy(wo_hbm, wo)
            halo[...] = jnp.zeros_like(halo)

        hv = h_ref[...]
        n = _rms_fwd(hv, g_ref[...])[0].astype(BF16)
        o = hv
        for k in range(4):
            cols = slice(k * gw, (k + 1) * gw)
            u = jnp.dot(n, w[k], preferred_element_type=F32)
            z = jnp.dot(n, w[4 + k], preferred_element_type=F32)
            acts_ref[0, :, cols] = u.astype(BF16)
            acts_ref[1, :, cols] = z.astype(BF16)
            _, _, yp = _pool_mix(u, wg, bg_ref, sc_ref, halo[k], k, i, tt, first_pos)
            o = o + _dot(yp * (z * _sigmoid(z)), wo[k])
            halo[k] = u[tt - POOL_HALO:, :]
            halo_ref[0, k] = u[tt - POOL_HALO:, :]
        o_ref[...] = o

    sd = jax.ShapeDtypeStruct
    small = pl.BlockSpec((4, 1, gw), lambda i: (0, 0, 0))
    return pl.pallas_call(
        body, name=name, grid=(nt,),
        in_specs=[pl.BlockSpec((tt, d), lambda i: (i, 0)), pl.BlockSpec((1, d), lambda i: (0, 0)), small, small, ANY, ANY, ANY],
        out_specs=[pl.BlockSpec((tt, d), lambda i: (i, 0)), pl.BlockSpec((1, 4, POOL_HALO, gw), lambda i: (i, 0, 0, 0)),
                   pl.BlockSpec((2, tt, 4 * gw), lambda i: (0, i, 0))],
        out_shape=(sd((lp, d), F32), sd((nt, 4, POOL_HALO, gw), F32), sd((2, lp, 4 * gw), BF16)),
        scratch_shapes=[pltpu.VMEM(w_in.shape, BF16), pltpu.VMEM(w_grp.shape, BF16), pltpu.VMEM(w_out.shape, BF16),
                        pltpu.VMEM((4, POOL_HALO, gw), F32)],
        compiler_params=_params(),
    )(h, g, b_grp, scale, w_in, w_grp, w_out)


def pool_bwd(h, dh, halos, acts, g, w_in, w_grp, b_grp, scale, w_out, first_pos, name):
    lp, d = h.shape
    tt = TOKEN_TILE
    nt = lp // tt
    gw = w_grp.shape[1]

    def body(h_ref, dh_ref, halo_ref, acts_ref, g_ref, bg_ref, sc_ref, w_hbm, wg_hbm, wo_hbm,
             dho_ref, n_ref, dp_ref, dwo_hbm, dwg_hbm, dbg_hbm, dsc_hbm, dg_hbm,
             w, wg, wo, nxt, dwo, dwg, dbg, dsc, dg):
        i = pl.program_id(0)
        tile = nt - 1 - i

        @pl.when(i == 0)
        def _():
            pltpu.sync_copy(w_hbm, w)
            pltpu.sync_copy(wg_hbm, wg)
            pltpu.sync_copy(wo_hbm, wo)
            for ref in (nxt, dwo, dwg, dbg, dsc, dg):
                ref[...] = jnp.zeros_like(ref)

        gv = g_ref[...]
        nf, hh, rr = _rms_fwd(h_ref[...], gv)
        n_ref[...] = nf.T.astype(BF16)
        dhv = dh_ref[...]
        has_prev = (i < nt - 1).astype(F32)
        dn = jnp.zeros((tt, d), F32)
        for k in range(4):
            cols = slice(k * gw, (k + 1) * gw)
            u, z = acts_ref[0, :, cols].astype(F32), acts_ref[1, :, cols].astype(F32)
            mixed, outs, yp = _pool_mix(u, wg, bg_ref, sc_ref, halo_ref[0, k] * has_prev, k, tile, tt, first_pos)
            sz, dsz = _silu_and_grad(z)
            dy = _dot_nt(dhv, wo[k])
            dwo[k] += _dot_tn(yp * sz, dhv)
            dyp = dy * sz
            dz = dy * yp * dsz
            dsc[k] += jnp.sum(dyp * outs, axis=0, keepdims=True)
            douts = dyp * sc_ref[k]
            dbg[k] += jnp.sum(douts, axis=0, keepdims=True)
            dwg[k] += _dot_tn(mixed, douts)
            dmixed = _dot_nt(douts, wg[k])
            dm = dmixed * _pool_inv_count(tile, tt, first_pos, POOL_WINDOWS[k], gw)
            ext = jnp.concatenate([dm, nxt[k]], axis=0)
            du = _window_sums_fwd(ext)[k][:tt, :] - dmixed
            nxt[k] = dm[:POOL_HALO, :]
            dp_ref[0, :, cols] = du.astype(BF16)
            dp_ref[1, :, cols] = dz.astype(BF16)
            dn = dn + _dot_nt(du, w[k]) + _dot_nt(dz, w[4 + k])
        dg[...] += jnp.sum(dn * hh, axis=0, keepdims=True)
        dho_ref[...] = dhv + _rms_bwd(dn, hh, rr, gv)

        @pl.when(i == nt - 1)
        def _():
            pltpu.sync_copy(dwo, dwo_hbm)
            pltpu.sync_copy(dwg, dwg_hbm)
            pltpu.sync_copy(dbg, dbg_hbm)
            pltpu.sync_copy(dsc, dsc_hbm)
            pltpu.sync_copy(dg, dg_hbm)

    rev = lambda i: (nt - 1 - i, 0)
    sd = jax.ShapeDtypeStruct
    small = pl.BlockSpec((4, 1, gw), lambda i: (0, 0, 0))
    return pl.pallas_call(
        body, name=name, grid=(nt,),
        in_specs=[pl.BlockSpec((tt, d), rev), pl.BlockSpec((tt, d), rev),
                  pl.BlockSpec((1, 4, POOL_HALO, gw), lambda i: (jnp.maximum(nt - 2 - i, 0), 0, 0, 0)),
                  pl.BlockSpec((2, tt, 4 * gw), lambda i: (0, nt - 1 - i, 0)),
                  pl.BlockSpec((1, d), lambda i: (0, 0)), small, small, ANY, ANY, ANY],
        out_specs=[pl.BlockSpec((tt, d), rev), pl.BlockSpec((d, tt), lambda i: (0, nt - 1 - i)),
                   pl.BlockSpec((2, tt, 4 * gw), lambda i: (0, nt - 1 - i, 0)), ANY, ANY, ANY, ANY, ANY],
        out_shape=(sd((lp, d), F32), sd((d, lp), BF16), sd((2, lp, 4 * gw), BF16),
                   sd(w_out.shape, F32), sd(w_grp.shape, F32), sd((4, 1, gw), F32), sd((4, 1, gw), F32), sd((1, d), F32)),
        scratch_shapes=[pltpu.VMEM(w_in.shape, BF16), pltpu.VMEM(w_grp.shape, BF16), pltpu.VMEM(w_out.shape, BF16),
                        pltpu.VMEM((4, POOL_HALO, gw), F32), pltpu.VMEM(w_out.shape, F32), pltpu.VMEM(w_grp.shape, F32),
                        pltpu.VMEM((4, 1, gw), F32), pltpu.VMEM((4, 1, gw), F32), pltpu.VMEM((1, d), F32)],
        compiler_params=_params(),
    )(h, dh, halos, acts, g, b_grp, scale, w_in, w_grp, w_out)


def loss_head(h, target, g, pad_tiles, name):
    lp, d = h.shape
    tt = TOKEN_TILE
    nt = lp // tt

    def body(h_ref, t_ref, g_ref, dh_ref, dg_ref, loss_ref, acc):
        i = pl.program_id(0)

        @pl.when(i == 0)
        def _():
            acc[...] = jnp.zeros_like(acc)
            dg_ref[...] = jnp.zeros_like(dg_ref)

        @pl.when(i < pad_tiles)
        def _():
            dh_ref[...] = jnp.zeros_like(dh_ref)

        @pl.when(i >= pad_tiles)
        def _():
            gv = g_ref[...]
            n, hh, rr = _rms_fwd(h_ref[...], gv)
            err = n - t_ref[...]
            acc[...] += 0.5 * jnp.sum(jnp.mean(err * err, axis=-1, keepdims=True), axis=0, keepdims=True)
            dn = err * (1.0 / d)
            dg_ref[...] += jnp.sum(dn * hh, axis=0, keepdims=True)
            dh_ref[...] = _rms_bwd(dn, hh, rr, gv)

        loss_ref[...] = jnp.broadcast_to(acc[...], loss_ref.shape)

    sd = jax.ShapeDtypeStruct
    return pl.pallas_call(
        body, name=name, grid=(nt,),
        in_specs=[pl.BlockSpec((tt, d), lambda i: (i, 0)), pl.BlockSpec((tt, d), lambda i: (jnp.maximum(i - pad_tiles, 0), 0)),
                  pl.BlockSpec((1, d), lambda i: (0, 0))],
        out_specs=[pl.BlockSpec((tt, d), lambda i: (i, 0)), pl.BlockSpec((1, d), lambda i: (0, 0)),
                   pl.BlockSpec((8, 128), lambda i: (0, 0))],
        out_shape=(sd((lp, d), F32), sd((1, d), F32), sd((8, 128), F32)),
        scratch_shapes=[pltpu.VMEM((1, 1), F32)],
        compiler_params=_params(),
    )(h, target, g)


def _peers(x, y, c):
    out = []
    for k in range(1, N_DEV):
        px = 1 - x if k & 4 else x
        py = 1 - y if k & 2 else y
        pc = 1 - c if k & 1 else c
        out.append((k, (px, py, pc), 4 * px + 2 * py + pc))
    return out


def exchange_start(arrs, gather, after, name):
    n = len(arrs)
    me = 4 * lax.axis_index("x") + 2 * lax.axis_index("y") + lax.axis_index("c")
    lands = []
    for a in arrs:
        own = a[None] if gather else lax.dynamic_index_in_dim(a, me, 0, keepdims=True)
        lands.append(lax.dynamic_update_index_in_dim(lax.empty(((N_DEV,) + a.shape) if gather else a.shape, a.dtype), own, me, 0))

    def body(*refs):
        ins, land = refs[:n], refs[n:2 * n]
        send_sems, recv_sems, token = refs[2 * n + 1], refs[2 * n + 2], refs[4 * n + 3]
        x, y, c = lax.axis_index("x"), lax.axis_index("y"), lax.axis_index("c")
        me = 4 * x + 2 * y + c
        for k, pid, peer in _peers(x, y, c):
            for a in range(n):
                pltpu.make_async_remote_copy(
                    src_ref=ins[a] if gather else ins[a].at[peer], dst_ref=land[a].at[me],
                    send_sem=send_sems.at[a * (N_DEV - 1) + k - 1], recv_sem=recv_sems.at[a * (N_DEV - 1) + k - 1],
                    device_id=pid, device_id_type=pl.DeviceIdType.MESH).start()
        token[...] = jnp.zeros_like(token)

    hbm = pl.BlockSpec(memory_space=pltpu.HBM)
    sem = pl.BlockSpec(memory_space=pltpu.SEMAPHORE)
    sems = pltpu.SemaphoreType.DMA((n * (N_DEV - 1),))
    res = pl.pallas_call(
        body, name=name, in_specs=[hbm] * (2 * n) + [ANY],
        out_specs=[sem, sem] + [hbm] * (2 * n) + [pl.BlockSpec(memory_space=pltpu.VMEM)],
        out_shape=[sems, sems] + [pltpu.HBM(a.shape, a.dtype) for a in arrs] + [pltpu.HBM(l.shape, l.dtype) for l in lands]
        + [jax.ShapeDtypeStruct((8, 128), F32)],
        input_output_aliases={a: 2 + a for a in range(2 * n)},
        compiler_params=pltpu.CompilerParams(has_side_effects=pltpu.SideEffectType.DATAFLOW_SIDE_EFFECTING),
    )(*[pltpu.with_memory_space_constraint(a, pltpu.HBM) for a in list(arrs) + lands], after)
    return res[0], res[1], res[2:2 + n], res[2 + n:2 + 2 * n], res[-1]


def exchange_wait(started, gather, after, name):
    send_sems, recv_sems, srcs, lands, _ = started
    n = len(srcs)
    after = list(after) if isinstance(after, (list, tuple)) else [after]

    def body(*refs):
        ins, land = refs[:n], refs[n:2 * n]
        send_sems, recv_sems = refs[2 * n], refs[2 * n + 1]
        x, y, c = lax.axis_index("x"), lax.axis_index("y"), lax.axis_index("c")
        for k, pid, peer in _peers(x, y, c):
            for a in range(n):
                cp = pltpu.make_async_remote_copy(
                    src_ref=ins[a] if gather else ins[a].at[peer], dst_ref=land[a].at[peer],
                    send_sem=send_sems.at[a * (N_DEV - 1) + k - 1], recv_sem=recv_sems.at[a * (N_DEV - 1) + k - 1],
                    device_id=pid, device_id_type=pl.DeviceIdType.MESH)
                cp.wait_send()
                cp.wait_recv()

    hbm = pl.BlockSpec(memory_space=pltpu.HBM)
    sem = pl.BlockSpec(memory_space=pltpu.SEMAPHORE)
    res = pl.pallas_call(
        body, name=name, in_specs=[hbm] * (2 * n) + [sem, sem] + [ANY] * len(after),
        out_specs=[hbm] * (2 * n),
        out_shape=[pltpu.HBM(a.shape, a.dtype) for a in list(srcs) + list(lands)],
        input_output_aliases={a: a for a in range(2 * n)},
        compiler_params=pltpu.CompilerParams(has_side_effects=pltpu.SideEffectType.DATAFLOW_SIDE_EFFECTING),
    )(*srcs, *lands, send_sems, recv_sems, *after)
    return res[n:]


def _adamw(w, g, m, v):
    m = ADAM_B1 * m + (1.0 - ADAM_B1) * g
    v = ADAM_B2 * v + (1.0 - ADAM_B2) * (g * g)
    m_hat = m / (1.0 - ADAM_B1 ** ADAM_STEP)
    v_hat = v / (1.0 - ADAM_B2 ** ADAM_STEP)
    return -ADAM_LR * (m_hat / (jnp.sqrt(v_hat) + ADAM_EPS) + ADAM_WD * w), m, v


def _update_tile_rows(rows, cols):
    if rows * cols <= UPDATE_TILE_ELEMS:
        return rows
    return max(t for t in range(8, UPDATE_TILE_ELEMS // cols + 1, 8) if rows % t == 0)


def _sum_in_order(p_ref):
    g = p_ref[0].astype(F32)
    for j in range(1, p_ref.shape[0]):
        g = g + p_ref[j].astype(F32)
    return g


def sum_parts(parts, name):
    nparts, rows, cols = parts.shape
    tr = _update_tile_rows(rows, cols)

    def body(p_ref, g_ref):
        g_ref[...] = _sum_in_order(p_ref)

    return pl.pallas_call(
        body, name=name, grid=(rows // tr,),
        in_specs=[pl.BlockSpec((nparts, tr, cols), lambda i: (0, i, 0))],
        out_specs=pl.BlockSpec((tr, cols), lambda i: (i, 0)), out_shape=jax.ShapeDtypeStruct((rows, cols), F32),
        compiler_params=_params(),
    )(parts)


def sum_adamw(parts, w, m, v, name):
    rows, cols = w.shape
    nparts = parts.shape[0]
    tr = _update_tile_rows(rows, cols)

    def body(p_ref, w_ref, m_ref, v_ref, g_ref, d_ref, nm_ref, nv_ref):
        g = _sum_in_order(p_ref)
        delta, nm, nv = _adamw(w_ref[...], g, m_ref[...], v_ref[...])
        g_ref[...] = g
        d_ref[...] = delta
        nm_ref[...] = nm
        nv_ref[...] = nv

    blk = pl.BlockSpec((tr, cols), lambda i: (i, 0))
    sd = jax.ShapeDtypeStruct((rows, cols), F32)
    return pl.pallas_call(
        body, name=name, grid=(rows // tr,),
        in_specs=[pl.BlockSpec((nparts, tr, cols), lambda i: (0, i, 0)), blk, blk, blk],
        out_specs=[blk] * 4, out_shape=(sd,) * 4,
        compiler_params=_params(),
    )(parts, w, m, v)


def update_packed(g, w, m, v, pieces, name):
    rows_all = w.shape[0]

    def body(g_ref, w_ref, m_ref, v_ref, *outs):
        gv = g_ref[:rows_all, :]
        res = (gv,) + _adamw(w_ref[...], gv, m_ref[...], v_ref[...])
        for p, (row, rows, lanes) in enumerate(pieces):
            for k in range(4):
                outs[4 * p + k][...] = res[k][row:row + rows, :lanes]

    shapes = [jax.ShapeDtypeStruct((rows, lanes), F32) for _, rows, lanes in pieces for _ in range(4)]
    return pl.pallas_call(body, name=name, out_shape=shapes,
                          compiler_params=pltpu.CompilerParams(vmem_limit_bytes=VMEM_LIMIT))(g, w, m, v)


def update_natural(groups, name):
    n = len(groups)
    steps = 8

    def body(*refs):
        ins, outs = refs[:4 * n], refs[4 * n:]
        for j in range(n):
            g_ref, w_ref, m_ref, v_ref = ins[4 * j:4 * j + 4]
            gv = g_ref[...]
            res = (gv,) + _adamw(w_ref[...], gv, m_ref[...], v_ref[...])
            for k in range(4):
                outs[4 * j + k][...] = res[k]

    specs, shapes = [], []
    for g, w, m, v in groups:
        rows, cols = w.shape
        specs += [pl.BlockSpec((rows // steps, cols), lambda i: (i, 0))] * 4
        shapes += [jax.ShapeDtypeStruct((rows, cols), F32)] * 4
    return pl.pallas_call(body, name=name, grid=(steps,), in_specs=specs, out_specs=specs, out_shape=shapes,
                          compiler_params=_params())(*[a for grp in groups for a in grp])


S5_NAMES = ("w_in", "lam_re", "lam_im", "log_dt", "b_re", "b_im", "c_re", "c_im", "d_skip", "w_glu", "b_glu", "w_out")
CONV_NAMES = ("w_in", "conv_w", "conv_b", "w_out")
POOL_NAMES = ("w_in", "w_grp", "b_grp", "scale", "w_out")
LAYER_KINDS = ("s5", "conv", "pool", "s5")
LAYER_NAMES = {"s5": S5_NAMES, "conv": CONV_NAMES, "pool": POOL_NAMES}
SHARDED = {"s5": ("w_in", "w_glu", "w_out"), "conv": ("w_in", "conv_w", "w_out"), "pool": ("w_in", "w_grp", "b_grp", "w_out")}
GATHER_F32 = ("conv_w", "b_grp")


def weight_names():
    names = ["meta_tokens"]
    for i, kind in enumerate(LAYER_KINDS):
        names.append("norm%d_g" % i)
        names += ["l%d_%s" % (i, n) for n in LAYER_NAMES[kind]]
    names.append("final_g")
    return names


def sharded_names():
    return ["meta_tokens"] + ["l%d_%s" % (i, n) for i, kind in enumerate(LAYER_KINDS) for n in SHARDED[kind]]


def _block_diag_in_grad(blocks):
    _, gc, i, p = blocks.shape
    return jnp.transpose(blocks, (2, 0, 1, 3)).reshape(i, 4 * gc, p)


def _block_diag_out_grad(blocks):
    _, gc, i, p = blocks.shape
    return blocks.reshape(4 * gc, i, p)


def _to_owner_blocks(a, axis):
    shape = a.shape[:axis] + (N_DEV, a.shape[axis] // N_DEV) + a.shape[axis + 1:]
    return jnp.moveaxis(a.reshape(shape), axis, 0)


def _from_owner_blocks(a, axis):
    a = jnp.moveaxis(a, 0, axis)
    return a.reshape(a.shape[:axis] + (a.shape[axis] * a.shape[axis + 1],) + a.shape[axis + 2:])


def _step(x, target, weights, moments_m, moments_v):
    seq, d = x.shape[1], x.shape[2]
    n_meta = weights["meta_tokens"].shape[0]
    tt = TOKEN_TILE
    pad_tiles = -(-n_meta // tt)
    p0 = pad_tiles * tt
    lp = p0 + seq
    first_pos = p0 - n_meta
    gc = d // 4 // S5_GROUP
    cw = d // 4

    big_names = [n for n in sharded_names() if n != "meta_tokens" and n.split("_", 1)[1] not in GATHER_F32]
    small_names = [n for n in sharded_names() if n not in big_names]
    layer_big = [[n for n in big_names if n.startswith("l%d_" % i)] for i in range(len(LAYER_KINDS))]
    layer_big[0] = small_names + layer_big[0]
    gather_started = []
    after = jnp.zeros((8, 128), F32)
    for i, names in enumerate(layer_big):
        gather_started.append(exchange_start([weights[n] if n in small_names else weights[n].astype(BF16) for n in names], True,
                                             after, "gather_start_l%d" % i))
        after = gather_started[-1][4]

    def vec(name):
        return weights[name].reshape(1, -1)

    s5_prep = {}
    for i, kind in enumerate(LAYER_KINDS):
        if kind == "s5":
            p = "l%d_" % i
            lr, li = weights[p + "lam_re"], weights[p + "lam_im"] + after[0, 0]
            ldt = weights[p + "log_dt"].reshape(-1, 1)
            br_t = jnp.transpose(weights[p + "b_re"], (2, 0, 1))
            bi_t = jnp.transpose(weights[p + "b_im"], (2, 0, 1))
            ar, ai, bdre, bdim, cdre, cdim = s5_disc_fwd(lr, li, ldt, br_t, bi_t, weights[p + "c_re"], weights[p + "c_im"],
                                                         p + "disc_fwd")
            s5_prep[i] = dict(
                disc=(lr, li, ldt, br_t, bi_t), ar=ar.reshape(4, -1, 128), ai=ai.reshape(4, -1, 128),
                bdre=bdre, bdim=bdim, cdre=cdre, cdim=cdim,
                d_skip=weights[p + "d_skip"].reshape(4, 1, cw), b_glu=vec(p + "b_glu"))
    h = jnp.concatenate([jnp.zeros((p0, d), F32), x[0] + after[0, 0]], axis=0)

    prepared = [h] + [s5_prep[i][k] for i in s5_prep for k in ("bdre", "bdim", "cdre", "cdim")]
    gathered = dict(zip(layer_big[0], exchange_wait(gather_started[0], True, prepared, "gather_wait_l0")))
    h = lax.dynamic_update_slice(h, _from_owner_blocks(gathered["meta_tokens"], 1), (first_pos, 0))

    full = {}

    def layer_weights(i, kind, after):
        p = "l%d_" % i
        if i > 0:
            gathered.update(zip(layer_big[i], exchange_wait(gather_started[i], True, after, "gather_wait_l%d" % i)))
        w_in = gathered[p + "w_in"]
        if kind == "s5":
            full[i] = dict(s5_prep[i], w_in=w_in, w_glu=gathered[p + "w_glu"].reshape(4, cw, d),
                           w_out=gathered[p + "w_out"].reshape(4, cw, d))
        elif kind == "conv":
            ce = w_in.shape[2]
            nch = 2
            conv_w = _from_owner_blocks(gathered[p + "conv_w"], 1)
            full[i] = dict(
                w_in=w_in, conv_w=jnp.transpose(conv_w.reshape(CONV_K, nch, ce), (1, 0, 2)),
                conv_b=weights[p + "conv_b"].reshape(nch, 1, ce), w_out=gathered[p + "w_out"].reshape(nch, ce, d))
        else:
            gw = w_in.shape[2]
            full[i] = dict(
                w_in=w_in, w_grp=_from_owner_blocks(gathered[p + "w_grp"], 1),
                b_grp=_from_owner_blocks(gathered[p + "b_grp"], 1).reshape(4, 1, gw),
                scale=weights[p + "scale"].reshape(4, 1, gw), w_out=gathered[p + "w_out"].reshape(4, gw, d))
        return full[i]

    saved = {}
    for i, kind in enumerate(LAYER_KINDS):
        p, f, g = "l%d_" % i, layer_weights(i, kind, h), vec("norm%d_g" % i)
        if kind == "s5":
            u, z, xs = s5_fwd1(h, g, f["w_in"], f["bdre"], f["bdim"], p + "fwd_in")
            s = s5_scan_fwd(xs, f["ar"], f["ai"], p + "scan_fwd")
            h_in = h
            h, y, q = s5_fwd3(s, u, z, h, f["cdre"], f["cdim"], f["w_glu"], f["w_out"], f["d_skip"], f["b_glu"], p + "fwd_out")
            saved[i] = (h_in, u, z, s, y, q)
        elif kind == "conv":
            h_new, halos, acts = conv_fwd(h, g, f["w_in"], f["conv_w"], f["conv_b"], f["w_out"], p + "fwd")
            saved[i] = (h, halos, acts)
            h = h_new
        else:
            h_new, halos, acts = pool_fwd(h, g, f["w_in"], f["w_grp"], f["b_grp"], f["scale"], f["w_out"], first_pos, p + "fwd")
            saved[i] = (h, halos, acts)
            h = h_new

    dh, dg_final, loss_tile = loss_head(h, target[0], vec("final_g"), pad_tiles, "loss_head")
    loss = lax.psum(loss_tile[0, 0], ("x", "y", "c"))

    grads = {"final_g": dg_final}
    names = weight_names()
    sh_names = sharded_names()
    replicated = [n for n in names if n not in sh_names]
    vectors = [n for n in replicated if weights[n].ndim == 1]
    matrices = [n for n in replicated if weights[n].ndim > 1]
    rep_names = vectors + matrices

    def owner_blocks(a):
        return a.reshape(N_DEV, -1, a.shape[-1]).astype(BF16)

    def as2d(a):
        return a.reshape(-1, a.shape[-1])

    def pack(tree, which=None):
        flat = [jnp.pad(tree[n].reshape(-1), (0, -tree[n].size % PACK_ALIGN)) for n in (which or rep_names)]
        flat = jnp.concatenate(flat)
        if which is None:
            flat = jnp.pad(flat, (0, -flat.size % (PACK_ROWS * 128)))
        return flat.reshape(-1, 128)

    layer_sharded, scatter_started = {}, {}
    ordered = jnp.zeros((), F32)
    for i in reversed(range(len(LAYER_KINDS))):
        kind = LAYER_KINDS[i]
        p, f, g = "l%d_" % i, full[i], vec("norm%d_g" % i) + ordered
        if kind == "s5":
            h_in, u, z, s, y, q = saved[i]
            dy, dp, dwo, dwg, dbg = s5_bwd3a(dh, y, q, z, f["w_glu"], f["w_out"], f["b_glu"] + ordered, p + "bwd_out")
            d_skip = f["d_skip"]
            if i == 0:
                early_names = [p + "w_glu", p + "w_out"]
                scatter_started["early"] = exchange_start([owner_blocks(dwg), owner_blocks(dwo)], False, dy,
                                                          "scatter_start_l0_early")
                d_skip = d_skip + scatter_started["early"][4][0, 0]
            ds, dus, dcre, dcim, dd = s5_bwd3b(dy, s, u, f["cdre"], f["cdim"], d_skip, p + "bwd_read")
            lam, dar, dai = s5_scan_bwd(ds, s, f["ar"], f["ai"], p + "scan_bwd")
            res = s5_bwd1(lam, dus, u, dp, h_in, dh, g, f["w_in"], f["bdre"], f["bdim"], p + "bwd_in",
                          pad_tiles=pad_tiles if i == 0 else None)
            dp, dh, n, dbre, dbim, dg = res[:6]
            if i == 0:
                grad_x = res[6][None]
            dw_in = grad_w_in(n, dp, f["w_in"].shape[2], p + "grad_w_in")
            grads.update({p + "w_in": dw_in, p + "w_glu": dwg.reshape(N_DEV, -1, d), p + "w_out": dwo.reshape(N_DEV, -1, d),
                          p + "d_skip": dd, p + "b_glu": dbg})

            def replicated_grads(p=p, f=f, dar=dar, dai=dai, dbre=dbre, dbim=dbim, dcre=dcre, dcim=dcim, token=None):
                lr, li, ldt, br_t, bi_t = f["disc"]
                dlr, dli, dldt, dbr_t, dbi_t = s5_disc_bwd(
                    lr, li, ldt, br_t, bi_t, dar.reshape(lr.shape) + token, dai.reshape(lr.shape),
                    _block_diag_in_grad(dbre), _block_diag_in_grad(dbim), p + "disc_bwd")
                grads.update({
                    p + "lam_re": dlr, p + "lam_im": dli, p + "log_dt": dldt,
                    p + "b_re": jnp.transpose(dbr_t, (1, 2, 0)), p + "b_im": jnp.transpose(dbi_t, (1, 2, 0)),
                    p + "c_re": _block_diag_out_grad(dcre), p + "c_im": -_block_diag_out_grad(dcim)})
        elif kind == "conv":
            replicated_grads = None
            h_in, halos, acts = saved[i]
            dh, n, dp, dwo, dcw, dcb, dg = conv_bwd(h_in, dh, halos, acts, g, f["w_in"], f["conv_w"], f["conv_b"], f["w_out"], p + "bwd")
            dw_in = grad_w_in(n, dp, f["w_in"].shape[2], p + "grad_w_in")
            dconv_w = jnp.transpose(dcw[:, :CONV_K, :], (1, 0, 2)).reshape(CONV_K, -1)
            grads.update({p + "w_in": dw_in, p + "conv_w": _to_owner_blocks(dconv_w, 1), p + "conv_b": dcb,
                          p + "w_out": dwo.reshape(N_DEV, -1, d)})
        else:
            replicated_grads = None
            h_in, halos, acts = saved[i]
            dh, n, dp, dwo, dwgrp, dbgrp, dsc, dg = pool_bwd(h_in, dh, halos, acts, g, f["w_in"], f["w_grp"], f["b_grp"], f["scale"],
                                                             f["w_out"], first_pos, p + "bwd")
            dw_in = grad_w_in(n, dp, f["w_in"].shape[2], p + "grad_w_in")
            grads.update({p + "w_in": dw_in, p + "w_grp": _to_owner_blocks(dwgrp, 1),
                          p + "b_grp": _to_owner_blocks(dbgrp.reshape(4, -1), 1), p + "scale": dsc,
                          p + "w_out": dwo.reshape(N_DEV, -1, d)})
        grads["norm%d_g" % i] = dg
        layer_sharded[i] = ["l%d_%s" % (i, n) for n in SHARDED[kind]]
        if i > 0:
            scatter_started[i] = exchange_start([owner_blocks(grads[n]) for n in layer_sharded[i]], False, dh,
                                                "scatter_start_l%d" % i)
            ordered = scatter_started[i][4][0, 0]
        if replicated_grads is not None:
            replicated_grads(token=ordered)
    grads["meta_tokens"] = _to_owner_blocks(dh[first_pos:p0], 1)
    last = len(LAYER_KINDS)
    layer_sharded[last] = ["meta_tokens", "replicated"]
    scatter_started[last] = exchange_start([owner_blocks(grads["meta_tokens"]), pack(grads).reshape(N_DEV, -1, 128)], False,
                                           dh, "scatter_start_replicated")
    layer_sharded["early"] = early_names
    layer_sharded[0] = [n for n in layer_sharded[0] if n not in early_names]

    out = {}
    received = {}
    after = [scatter_started[last][4]]
    for i in list(reversed(range(1, last))) + [last, "early", 0]:
        received.update(zip(layer_sharded[i], exchange_wait(scatter_started[i], False, after, "scatter_wait_%s" % i)))
        updated = []
        for n in layer_sharded[i]:
            if n != "replicated":
                res = sum_adamw(received[n], as2d(weights[n]), as2d(moments_m[n]), as2d(moments_v[n]), "update_" + n)
                out[n] = [r.reshape(weights[n].shape) for r in res]
                updated.append(out[n][0])
        after = updated or after
        if i == last:
            g_sum = sum_parts(received["replicated"], "sum_replicated")
            small_gather = exchange_start([g_sum], True, g_sum, "gather_small_grads_start")
            scatter_started[0] = exchange_start([owner_blocks(grads[n]) for n in layer_sharded[0]], False, small_gather[4],
                                                "scatter_start_l0")
            g_full = exchange_wait(small_gather, True, scatter_started[0][4], "gather_small_grads_wait")[0].reshape(-1, 128)
            offsets, offset = {}, 0
            for n in rep_names:
                offsets[n] = offset
                offset += weights[n].size + (-weights[n].size % PACK_ALIGN)
            pieces = [(offsets[n] // 128, max(weights[n].size // 128, 1), min(weights[n].size, 128)) for n in vectors]
            res = update_packed(g_full, pack(weights, vectors), pack(moments_m, vectors), pack(moments_v, vectors), pieces,
                                "update_replicated_vectors")
            for j, n in enumerate(vectors):
                out[n] = [r.reshape(weights[n].shape) for r in res[4 * j:4 * j + 4]]
            flat = g_full.reshape(-1)
            groups = [(flat[offsets[n]:offsets[n] + weights[n].size].reshape(as2d(weights[n]).shape), as2d(weights[n]),
                       as2d(moments_m[n]), as2d(moments_v[n])) for n in matrices]
            res = update_natural(groups, "update_replicated_matrices")
            for j, n in enumerate(matrices):
                out[n] = [r.reshape(weights[n].shape) for r in res[4 * j:4 * j + 4]]
            after = [out[n][k] for n in rep_names for k in range(4)]

    return (loss, grad_x) + tuple(out[n][k] for k in range(4) for n in names)


def kernel(x, *rest):
    names = weight_names()
    nw = len(names)
    weights = dict(zip(names, rest[:nw]))
    target = rest[nw]
    moments_m = dict(zip(names, rest[nw + 1:2 * nw + 1]))
    moments_v = dict(zip(names, rest[2 * nw + 1:3 * nw + 1]))
    return _step(x, target, weights, moments_m, moments_v)
```

```python
import math

import jax
import jax.numpy as jnp
from jax import lax
from jax.experimental import pallas as pl
from jax.experimental.pallas import tpu as pltpu

F32 = jnp.float32
BF16 = jnp.bfloat16
EPS = 1e-6
N_DEV = 8
TOKEN_TILE = 256
SCAN_CHUNKS = 4
SCAN_UNROLL = 4
S5_GROUP = 16
S5_STATE = 64
POOL_WINDOWS = (2, 4, 8, 16)
POOL_HALO = 16
CONV_K = 3
CONV_HALO = 8
ADAM_LR = 0.001
ADAM_B1 = 0.9
ADAM_B2 = 0.999
ADAM_EPS = 1e-08
ADAM_WD = 0.01
ADAM_STEP = 10
GELU_C = math.sqrt(2.0 / math.pi)
GELU_A = 0.044715
UPDATE_TILE_ELEMS = 1 << 17
PACK_ROWS = 512
PACK_ALIGN = 8 * 128
HIGH_HALF = -65536
HALF_OF_LOW_HALF = 0x8000
VMEM_LIMIT = 56 << 20
VMEM_LIMIT_LARGE = 62 << 20

ANY = pl.BlockSpec(memory_space=pl.ANY)


def _params(vmem=VMEM_LIMIT, ndim=1):
    return pltpu.CompilerParams(vmem_limit_bytes=vmem, dimension_semantics=("arbitrary",) * ndim)


def _dot(a, b):
    return jnp.dot(a.astype(BF16), b.astype(BF16), preferred_element_type=F32)


def _dot_nt(a, b):
    return lax.dot_general(a.astype(BF16), b.astype(BF16), (((1,), (1,)), ((), ())), preferred_element_type=F32)


def _dot_tn(a, b):
    return lax.dot_general(a.astype(BF16), b.astype(BF16), (((0,), (0,)), ((), ())), preferred_element_type=F32)


def _rms_fwd(h, g):
    r = lax.rsqrt(jnp.mean(h * h, axis=-1, keepdims=True) + EPS)
    hh = h * r
    return hh * g, hh, r


def _rms_bwd(dn, hh, r, g):
    dhh = dn * g
    return r * (dhh - hh * jnp.mean(dhh * hh, axis=-1, keepdims=True))


def _sigmoid(x):
    return 1.0 / (1.0 + jnp.exp(-x))


def _silu_and_grad(z):
    s = _sigmoid(z)
    return z * s, s * (1.0 + z * (1.0 - s))


def _gelu(y):
    t = jnp.tanh(GELU_C * (y + GELU_A * y * y * y))
    return 0.5 * y * (1.0 + t), t


def _gelu_grad(y, t):
    return 0.5 * (1.0 + t) + 0.5 * y * (1.0 - t * t) * GELU_C * (1.0 + 3.0 * GELU_A * y * y)


def _rows(shape):
    return lax.broadcasted_iota(jnp.int32, shape, 0)


def _shift_down(x, k, halo):
    y = pltpu.roll(x, k, 0)
    rows = _rows(x.shape)
    for j in range(k):
        y = jnp.where(rows == j, halo[halo.shape[0] - k + j:halo.shape[0] - k + j + 1, :], y)
    return y


def _shift_up(x, k, halo):
    n = x.shape[0]
    y = pltpu.roll(x, n - k, 0)
    rows = _rows(x.shape)
    for j in range(k):
        y = jnp.where(rows == n - k + j, halo[j:j + 1, :], y)
    return y


def _window_sums_back(ext):
    out = []
    s = ext
    for k in (1, 2, 4, 8):
        s = s + pltpu.roll(s, k, 0)
        out.append(s)
    return out


def _window_sums_fwd(ext):
    n = ext.shape[0]
    out = []
    s = ext
    for k in (1, 2, 4, 8):
        s = s + pltpu.roll(s, n - k, 0)
        out.append(s)
    return out


def _pool_inv_count(tile, tt, first_pos, w, width):
    pos = _rows((tt, width)) + (tile * tt - first_pos + 1)
    return 1.0 / jnp.clip(pos, 1, w).astype(F32)


def _slab_spec(lp, tt, sw):
    nj = sw // 128
    return pl.BlockSpec((4 * tt * nj, 128), lambda i: (i, 0)), (lp * 4 * nj, 128)


def _pack_pair(re, im):
    def rounded(v):
        return lax.bitcast_convert_type(v, jnp.int32) + HALF_OF_LOW_HALF
    return lax.bitcast_convert_type((rounded(re) & HIGH_HALF) | lax.shift_right_logical(rounded(im), 16), F32)


def _unpack_pair(w):
    b = lax.bitcast_convert_type(w, jnp.int32)
    return lax.bitcast_convert_type(b & HIGH_HALF, F32), lax.bitcast_convert_type(lax.shift_left(b, 16), F32)


def _slab_load(ref, c):
    nj = ref.shape[0] // (4 * TOKEN_TILE)
    first = c * TOKEN_TILE * nj
    return _unpack_pair(jnp.concatenate([ref[pl.ds(first + j, TOKEN_TILE, stride=nj), :] for j in range(nj)], axis=1))


def _slab_store(ref, c, re, im):
    nj = ref.shape[0] // (4 * TOKEN_TILE)
    first = c * TOKEN_TILE * nj
    val = _pack_pair(re, im)
    for j in range(nj):
        ref[pl.ds(first + j, TOKEN_TILE, stride=nj), :] = val[:, j * 128:(j + 1) * 128]


def _s5_disc_math(lr, li, ldt, br, bi):
    dt = jnp.exp(ldt)
    mag = jnp.exp(lr * dt)
    ar = mag * jnp.cos(li * dt)
    ai = mag * jnp.sin(li * dt)
    den = lr * lr + li * li
    kr = ((ar - 1.0) * lr + ai * li) / den
    ki = (ai * lr - (ar - 1.0) * li) / den
    bbr = kr[None] * br - ki[None] * bi
    bbi = kr[None] * bi + ki[None] * br
    return ar, ai, bbr, bbi


def s5_disc_fwd(lr, li, ldt, br_t, bi_t, c_re, c_im, name):
    ni, ng, npp = br_t.shape
    gc = ng // 4

    def body(lr_ref, li_ref, ldt_ref, br_ref, bi_ref, cre_ref, cim_ref,
             ar_ref, ai_ref, bdre_ref, bdim_ref, cdre_ref, cdim_ref, bbr_sc, bbi_sc):
        ar, ai, bbr, bbi = _s5_disc_math(lr_ref[...], li_ref[...], ldt_ref[...], br_ref[...], bi_ref[...])
        ar_ref[...] = ar
        ai_ref[...] = ai
        bbr_sc[...] = bbr
        bbi_sc[...] = bbi
        for ref in (bdre_ref, bdim_ref, cdre_ref, cdim_ref):
            ref[...] = jnp.zeros_like(ref)
        for k in range(4):
            for j in range(gc):
                g = k * gc + j
                ins, states = pl.ds(j * ni, ni), pl.ds(j * npp, npp)
                bdre_ref[k, ins, states] = bbr_sc[:, g, :].astype(BF16)
                bdim_ref[k, ins, states] = bbi_sc[:, g, :].astype(BF16)
                cdre_ref[k, states, ins] = cre_ref[g].T.astype(BF16)
                cdim_ref[k, states, ins] = (-cim_ref[g]).T.astype(BF16)

    sd = jax.ShapeDtypeStruct
    return pl.pallas_call(
        body, name=name,
        out_shape=(sd(lr.shape, F32), sd(lr.shape, F32), sd((4, gc * ni, gc * npp), BF16), sd((4, gc * ni, gc * npp), BF16),
                   sd((4, gc * npp, gc * ni), BF16), sd((4, gc * npp, gc * ni), BF16)),
        scratch_shapes=[pltpu.VMEM(br_t.shape, F32), pltpu.VMEM(br_t.shape, F32)],
        compiler_params=pltpu.CompilerParams(vmem_limit_bytes=VMEM_LIMIT),
    )(lr, li, ldt, br_t, bi_t, c_re, c_im)


def s5_disc_bwd(lr, li, ldt, br_t, bi_t, dar, dai, dbbr, dbbi, name):
    def body(lr_ref, li_ref, ldt_ref, br_ref, bi_ref, dar_ref, dai_ref, dbbr_ref, dbbi_ref,
             dlr_ref, dli_ref, dldt_ref, dbr_ref, dbi_ref):
        _, vjp = jax.vjp(_s5_disc_math, lr_ref[...], li_ref[...], ldt_ref[...], br_ref[...], bi_ref[...])
        dlr, dli, dldt, dbr, dbi = vjp((dar_ref[...], dai_ref[...], dbbr_ref[...], dbbi_ref[...]))
        dlr_ref[...] = dlr
        dli_ref[...] = dli
        dldt_ref[...] = dldt
        dbr_ref[...] = dbr
        dbi_ref[...] = dbi

    sd = jax.ShapeDtypeStruct
    return pl.pallas_call(
        body, name=name,
        out_shape=(sd(lr.shape, F32), sd(lr.shape, F32), sd(ldt.shape, F32), sd(br_t.shape, F32), sd(br_t.shape, F32)),
    )(lr, li, ldt, br_t, bi_t, dar, dai, dbbr, dbbi)


def s5_fwd1(h, g, w_in, bdre, bdim, name):
    lp, d = h.shape
    tt = TOKEN_TILE
    cw, sw = bdre.shape[1], bdre.shape[2]

    def body(h_ref, g_ref, w_hbm, bdre_hbm, bdim_hbm, u_ref, z_ref, x_ref, w, bre, bim):
        @pl.when(pl.program_id(0) == 0)
        def _():
            pltpu.sync_copy(w_hbm, w)
            pltpu.sync_copy(bdre_hbm, bre)
            pltpu.sync_copy(bdim_hbm, bim)

        n = _rms_fwd(h_ref[...], g_ref[...])[0].astype(BF16)
        for c in range(4):
            cols = slice(c * cw, (c + 1) * cw)
            u = jnp.dot(n, w[c], preferred_element_type=F32)
            u_ref[:, cols] = u
            z_ref[:, cols] = jnp.dot(n, w[c + 4], preferred_element_type=F32)
            ub = u.astype(BF16)
            _slab_store(x_ref, c, jnp.dot(ub, bre[c], preferred_element_type=F32), jnp.dot(ub, bim[c], preferred_element_type=F32))

    sd = jax.ShapeDtypeStruct
    slab, slab_shape = _slab_spec(lp, tt, sw)
    row = pl.BlockSpec((tt, d), lambda i: (i, 0))
    return pl.pallas_call(
        body, name=name, grid=(lp // tt,),
        in_specs=[row, pl.BlockSpec((1, d), lambda i: (0, 0)), ANY, ANY, ANY],
        out_specs=[row, row, slab],
        out_shape=(sd((lp, d), F32), sd((lp, d), F32), sd(slab_shape, F32)),
        scratch_shapes=[pltpu.VMEM(w_in.shape, BF16), pltpu.VMEM(bdre.shape, BF16), pltpu.VMEM(bdim.shape, BF16)],
        compiler_params=_params(),
    )(h, g, w_in, bdre, bdim)


def s5_scan_fwd(x, ar, ai, name):
    nj = ar.shape[1]
    tt = TOKEN_TILE
    cpb = SCAN_CHUNKS
    nt = x.shape[0] // (4 * tt * nj)

    def body(x_ref, ar_ref, ai_ref, s_ref, st_r, st_i):
        i, cg = pl.program_id(0), pl.program_id(1)

        @pl.when(i == 0)
        def _():
            for q in range(cpb):
                st_r[cg * cpb + q] = jnp.zeros((nj, 128), F32)
                st_i[cg * cpb + q] = jnp.zeros((nj, 128), F32)

        a_r = [ar_ref[cg * cpb + q] for q in range(cpb)]
        a_i = [ai_ref[cg * cpb + q] for q in range(cpb)]

        def step(k, carry):
            carry = list(carry)
            for uu in range(SCAN_UNROLL):
                t = k * SCAN_UNROLL + uu
                for q in range(cpb):
                    s_r, s_i = carry[q]
                    rows = pl.ds(pl.multiple_of((q * tt + t) * nj, nj), nj)
                    x_r, x_i = _unpack_pair(x_ref[rows, :])
                    n_r = a_r[q] * s_r - a_i[q] * s_i + x_r
                    n_i = a_r[q] * s_i + a_i[q] * s_r + x_i
                    s_ref[rows, :] = _pack_pair(n_r, n_i)
                    carry[q] = (n_r, n_i)
            return tuple(carry)

        init = tuple((st_r[cg * cpb + q], st_i[cg * cpb + q]) for q in range(cpb))
        final = lax.fori_loop(0, tt // SCAN_UNROLL, step, init)
        for q in range(cpb):
            st_r[cg * cpb + q] = final[q][0]
            st_i[cg * cpb + q] = final[q][1]

    blk = pl.BlockSpec((cpb * tt * nj, 128), lambda i, cg: (i * (4 // cpb) + cg, 0))
    par = pl.BlockSpec((4, nj, 128), lambda i, cg: (0, 0, 0))
    sd = jax.ShapeDtypeStruct
    return pl.pallas_call(
        body, name=name, grid=(nt, 4 // cpb),
        in_specs=[blk, par, par], out_specs=blk,
        out_shape=sd(x.shape, F32),
        scratch_shapes=[pltpu.VMEM((4, nj, 128), F32), pltpu.VMEM((4, nj, 128), F32)],
        compiler_params=_params(ndim=2),
    )(x, ar, ai)


def s5_fwd3(s, u, z, h, cdre, cdim, w_glu, w_out, d_skip, b_glu, name):
    lp, d = h.shape
    tt = TOKEN_TILE
    sw, cw = cdre.shape[1], cdre.shape[2]

    def body(s_ref, u_ref, z_ref, h_ref, d_ref, bg_ref, cre_hbm, cim_hbm, wg_hbm, wo_hbm,
             o_ref, y_ref, q_ref, cre, cim, wg, wo):
        @pl.when(pl.program_id(0) == 0)
        def _():
            pltpu.sync_copy(cre_hbm, cre)
            pltpu.sync_copy(cim_hbm, cim)
            pltpu.sync_copy(wg_hbm, wg)
            pltpu.sync_copy(wo_hbm, wo)

        gys, q = [], None
        for c in range(4):
            cols = slice(c * cw, (c + 1) * cw)
            s_r, s_i = _slab_load(s_ref, c)
            y = _dot(s_r, cre[c]) + _dot(s_i, cim[c]) + d_ref[c] * u_ref[:, cols]
            y_ref[:, cols] = y
            gys.append(_gelu(y)[0])
            part = _dot(gys[c], wg[c])
            q = part if c == 0 else q + part
        q_ref[...] = q
        sig = _sigmoid(q + bg_ref[...])
        zz = z_ref[...]
        sz = zz * _sigmoid(zz)
        o = h_ref[...]
        for k in range(4):
            cols = slice(k * cw, (k + 1) * cw)
            o = o + _dot(gys[k] * sig[:, cols] * sz[:, cols], wo[k])
        o_ref[...] = o

    row = pl.BlockSpec((tt, d), lambda i: (i, 0))
    slab, _ = _slab_spec(lp, tt, sw)
    sd = jax.ShapeDtypeStruct((lp, d), F32)
    return pl.pallas_call(
        body, name=name, grid=(lp // tt,),
        in_specs=[slab, row, row, row, pl.BlockSpec((4, 1, cw), lambda i: (0, 0, 0)), pl.BlockSpec((1, d), lambda i: (0, 0)),
                  ANY, ANY, ANY, ANY],
        out_specs=[row, row, row],
        out_shape=(sd, sd, sd),
        scratch_shapes=[pltpu.VMEM(cdre.shape, BF16), pltpu.VMEM(cdim.shape, BF16), pltpu.VMEM(w_glu.shape, BF16),
                        pltpu.VMEM(w_out.shape, BF16)],
        compiler_params=_params(),
    )(s, u, z, h, d_skip, b_glu, cdre, cdim, w_glu, w_out)


def s5_bwd3a(dh, y, q, z, w_glu, w_out, b_glu, name):
    lp, d = dh.shape
    tt = TOKEN_TILE
    nt = lp // tt
    cw = w_glu.shape[1]

    def body(dh_ref, y_ref, q_ref, z_ref, bg_ref, wg_hbm, wo_hbm, dy_ref, dp_ref, dwo_hbm, dwg_hbm, dbg_hbm,
             wg, wo, dwo, dwg, dbg):
        i = pl.program_id(0)

        @pl.when(i == 0)
        def _():
            pltpu.sync_copy(wg_hbm, wg)
            pltpu.sync_copy(wo_hbm, wo)
            dwo[...] = jnp.zeros_like(dwo)
            dwg[...] = jnp.zeros_like(dwg)
            dbg[...] = jnp.zeros_like(dbg)

        sig = _sigmoid(q_ref[...] + bg_ref[...])
        sz, dsz = _silu_and_grad(z_ref[...])
        dhv = dh_ref[...]
        yv = y_ref[...]
        gy, t = _gelu(yv)
        dq_parts, dgy_parts = [], []
        for k in range(4):
            cols = slice(k * cw, (k + 1) * cw)
            gy_k, sig_k, sz_k = gy[:, cols], sig[:, cols], sz[:, cols]
            y2 = gy_k * sig_k
            dy3 = _dot_nt(dhv, wo[k])
            dwo[k] += _dot_tn(y2 * sz_k, dhv)
            dy2 = dy3 * sz_k
            dp_ref[0, :, cols] = (dy3 * y2 * dsz[:, cols]).astype(BF16)
            dq_parts.append(dy2 * gy_k * sig_k * (1.0 - sig_k))
            dgy_parts.append(dy2 * sig_k)
        dq = jnp.concatenate(dq_parts, axis=1)
        dbg[...] += jnp.sum(dq, axis=0, keepdims=True)
        dgelu = _gelu_grad(yv, t)
        for k in range(4):
            cols = slice(k * cw, (k + 1) * cw)
            dwg[k] += _dot_tn(gy[:, cols], dq)
            dy_ref[:, cols] = (dgy_parts[k] + _dot_nt(dq, wg[k])) * dgelu[:, cols]

        @pl.when(i == nt - 1)
        def _():
            pltpu.sync_copy(dwo, dwo_hbm)
            pltpu.sync_copy(dwg, dwg_hbm)
            pltpu.sync_copy(dbg, dbg_hbm)

    row = pl.BlockSpec((tt, d), lambda i: (i, 0))
    sd = jax.ShapeDtypeStruct
    return pl.pallas_call(
        body, name=name, grid=(nt,),
        in_specs=[row, row, row, row, pl.BlockSpec((1, d), lambda i: (0, 0)), ANY, ANY],
        out_specs=[row, pl.BlockSpec((1, tt, d), lambda i: (1, i, 0)), ANY, ANY, ANY],
        out_shape=(sd((lp, d), F32), sd((2, lp, d), BF16), sd(w_out.shape, F32), sd(w_glu.shape, F32), sd((1, d), F32)),
        scratch_shapes=[pltpu.VMEM(w_glu.shape, BF16), pltpu.VMEM(w_out.shape, BF16),
                        pltpu.VMEM(w_out.shape, F32), pltpu.VMEM(w_glu.shape, F32), pltpu.VMEM((1, d), F32)],
        compiler_params=_params(),
    )(dh, y, q, z, b_glu, w_glu, w_out)


def s5_bwd3b(dy, s, u, cdre, cdim, d_skip, name):
    lp, d = dy.shape
    tt = TOKEN_TILE
    nt = lp // tt
    sw, cw = cdre.shape[1], cdre.shape[2]
    gc = cw // S5_GROUP

    def body(dy_ref, s_ref, u_ref, d_ref, cre_hbm, cim_hbm,
             ds_ref, dus_ref, dcre_ref, dcim_ref, dd_hbm, cre, cim, dcre, dcim, dd):
        i = pl.program_id(0)

        @pl.when(i == 0)
        def _():
            pltpu.sync_copy(cre_hbm, cre)
            pltpu.sync_copy(cim_hbm, cim)
            dcre[...] = jnp.zeros_like(dcre)
            dcim[...] = jnp.zeros_like(dcim)
            dd[...] = jnp.zeros_like(dd)

        for c in range(4):
            chunk = slice(c * cw, (c + 1) * cw)
            dyv = dy_ref[:, chunk]
            dd[c] += jnp.sum(dyv * u_ref[:, chunk], axis=0, keepdims=True)
            dus_ref[:, chunk] = dyv * d_ref[c]
            _slab_store(ds_ref, c, _dot_nt(dyv, cre[c]), _dot_nt(dyv, cim[c]))
            s_r, s_i = _slab_load(s_ref, c)
            dcre[c] += _dot_tn(s_r, dyv)
            dcim[c] += _dot_tn(s_i, dyv)

        @pl.when(i == nt - 1)
        def _():
            for k in range(4):
                for j in range(gc):
                    rows, cols = pl.ds(j * S5_STATE, S5_STATE), pl.ds(j * S5_GROUP, S5_GROUP)
                    dcre_ref[k, j] = dcre[k, rows, cols].T
                    dcim_ref[k, j] = dcim[k, rows, cols].T
            pltpu.sync_copy(dd, dd_hbm)

    sd = jax.ShapeDtypeStruct
    row = pl.BlockSpec((tt, d), lambda i: (i, 0))
    slab, slab_shape = _slab_spec(lp, tt, sw)
    diag = pl.BlockSpec((4, gc, S5_GROUP, S5_STATE), lambda i: (0, 0, 0, 0))
    return pl.pallas_call(
        body, name=name, grid=(nt,),
        in_specs=[row, slab, row, pl.BlockSpec((4, 1, cw), lambda i: (0, 0, 0)), ANY, ANY],
        out_specs=[slab, row, diag, diag, ANY],
        out_shape=(sd(slab_shape, F32), sd((lp, d), F32),
                   sd((4, gc, S5_GROUP, S5_STATE), F32), sd((4, gc, S5_GROUP, S5_STATE), F32), sd((4, 1, cw), F32)),
        scratch_shapes=[pltpu.VMEM(cdre.shape, BF16), pltpu.VMEM(cdim.shape, BF16),
                        pltpu.VMEM(cdre.shape, F32), pltpu.VMEM(cdim.shape, F32), pltpu.VMEM((4, 1, cw), F32)],
        compiler_params=_params(),
    )(dy, s, u, d_skip, cdre, cdim)


def s5_scan_bwd(g, s, ar, ai, name):
    nj = ar.shape[1]
    tt = TOKEN_TILE
    cpb = SCAN_CHUNKS
    nt = g.shape[0] // (4 * tt * nj)

    def body(g_ref, s_ref, ar_ref, ai_ref, lam_ref, dar_ref, dai_ref, st_r, st_i, acc_r, acc_i):
        i, cg = pl.program_id(0), pl.program_id(1)

        @pl.when((i == 0) & (cg == 0))
        def _():
            for ref in (st_r, st_i, acc_r, acc_i):
                ref[...] = jnp.zeros_like(ref)

        a_r = [ar_ref[cg * cpb + q] for q in range(cpb)]
        a_i = [ai_ref[cg * cpb + q] for q in range(cpb)]

        def slab(q, t):
            return pl.ds(pl.multiple_of((q * tt + t) * nj, nj), nj)

        def adjoint(q, t, l_r, l_i):
            rows = slab(q, t)
            g_r, g_i = _unpack_pair(g_ref[rows, :])
            n_r = g_r + a_r[q] * l_r + a_i[q] * l_i
            n_i = g_i + a_r[q] * l_i - a_i[q] * l_r
            lam_ref[rows, :] = _pack_pair(n_r, n_i)
            return n_r, n_i

        def pair(q, t, l_r, l_i, d_r, d_i):
            p_r, p_i = _unpack_pair(s_ref[slab(q, t), :])
            return d_r + l_r * p_r + l_i * p_i, d_i + l_i * p_r - l_r * p_i

        def step(k, carry):
            carry = list(carry)
            for uu in range(SCAN_UNROLL):
                t = tt - 1 - (k * SCAN_UNROLL + uu)
                for q in range(cpb):
                    l_r, l_i, d_r, d_i = carry[q]
                    d_r, d_i = pair(q, t, l_r, l_i, d_r, d_i)
                    l_r, l_i = adjoint(q, t, l_r, l_i)
                    carry[q] = (l_r, l_i, d_r, d_i)
            return tuple(carry)

        init = tuple((st_r[cg * cpb + q], st_i[cg * cpb + q], acc_r[cg * cpb + q], acc_i[cg * cpb + q]) for q in range(cpb))
        final = lax.fori_loop(0, tt // SCAN_UNROLL, step, init)
        for q in range(cpb):
            ch = cg * cpb + q
            l_r, l_i, d_r, d_i = final[q]
            st_r[ch] = l_r
            st_i[ch] = l_i
            acc_r[ch] = d_r
            acc_i[ch] = d_i
            dar_ref[ch] = d_r
            dai_ref[ch] = d_i

    blk = pl.BlockSpec((cpb * tt * nj, 128), lambda i, cg: ((nt - 1 - i) * (4 // cpb) + cg, 0))
    par = pl.BlockSpec((4, nj, 128), lambda i, cg: (0, 0, 0))
    sd = jax.ShapeDtypeStruct
    return pl.pallas_call(
        body, name=name, grid=(nt, 4 // cpb),
        in_specs=[blk, blk, par, par], out_specs=[blk, par, par],
        out_shape=(sd(g.shape, F32), sd((4, nj, 128), F32), sd((4, nj, 128), F32)),
        scratch_shapes=[pltpu.VMEM((4, nj, 128), F32)] * 4,
        compiler_params=_params(ndim=2),
    )(g, s, ar, ai)


def s5_bwd1(lam, dus, u, dp, h, dh, g, w_in, bdre, bdim, name, pad_tiles=None):
    lp, d = h.shape
    tt = TOKEN_TILE
    nt = lp // tt
    cw, sw = bdre.shape[1], bdre.shape[2]
    gc = cw // S5_GROUP

    def body(lam_ref, dus_ref, u_ref, dpz_ref, h_ref, dh_ref, g_ref, w_hbm, bre_hbm, bim_hbm,
             dpu_ref, dho_ref, n_ref, dbre_ref, dbim_ref, dg_hbm, *rest):
        (gx_ref,), (w, bre, bim, dbre, dbim, dg) = (rest[:1], rest[1:]) if pad_tiles is not None else ((None,), rest)
        i = pl.program_id(0)

        @pl.when(i == 0)
        def _():
            pltpu.sync_copy(w_hbm, w)
            pltpu.sync_copy(bre_hbm, bre)
            pltpu.sync_copy(bim_hbm, bim)
            dbre[...] = jnp.zeros_like(dbre)
            dbim[...] = jnp.zeros_like(dbim)
            dg[...] = jnp.zeros_like(dg)

        dz = dpz_ref[0]
        dn = None
        for c in range(4):
            chunk = slice(c * cw, (c + 1) * cw)
            (l_r, l_i), uv = _slab_load(lam_ref, c), u_ref[:, chunk]
            du = dus_ref[:, chunk] + _dot_nt(l_r, bre[c]) + _dot_nt(l_i, bim[c])
            dbre[c] += _dot_tn(uv, l_r)
            dbim[c] += _dot_tn(uv, l_i)
            dpu_ref[0, :, chunk] = du.astype(BF16)
            part = _dot_nt(du, w[c]) + _dot_nt(dz[:, chunk], w[4 + c])
            dn = part if c == 0 else dn + part
        gv = g_ref[...]
        n, hh, rr = _rms_fwd(h_ref[...], gv)
        n_ref[...] = n.T.astype(BF16)
        dg[...] += jnp.sum(dn * hh, axis=0, keepdims=True)
        dh_in = dh_ref[...] + _rms_bwd(dn, hh, rr, gv)
        dho_ref[...] = dh_in
        if pad_tiles is not None:
            @pl.when(i >= pad_tiles)
            def _():
                gx_ref[...] = dh_in

        @pl.when(i == nt - 1)
        def _():
            for k in range(4):
                for j in range(gc):
                    rows, cols = pl.ds(j * S5_GROUP, S5_GROUP), pl.ds(j * S5_STATE, S5_STATE)
                    dbre_ref[k, j] = dbre[k, rows, cols]
                    dbim_ref[k, j] = dbim[k, rows, cols]
            pltpu.sync_copy(dg, dg_hbm)

    sd = jax.ShapeDtypeStruct
    row = pl.BlockSpec((tt, d), lambda i: (i, 0))
    slab, _ = _slab_spec(lp, tt, sw)
    diag = pl.BlockSpec((4, gc, S5_GROUP, S5_STATE), lambda i: (0, 0, 0, 0))
    extra_specs, extra_shapes = [], ()
    if pad_tiles is not None:
        extra_specs = [pl.BlockSpec((tt, d), lambda i: (jnp.maximum(i - pad_tiles, 0), 0))]
        extra_shapes = (sd((lp - pad_tiles * tt, d), F32),)
    return pl.pallas_call(
        body, name=name, grid=(nt,),
        in_specs=[slab, row, row, pl.BlockSpec((1, tt, d), lambda i: (1, i, 0)), row, row, pl.BlockSpec((1, d), lambda i: (0, 0)),
                  ANY, ANY, ANY],
        out_specs=[pl.BlockSpec((1, tt, d), lambda i: (0, i, 0)), row, pl.BlockSpec((d, tt), lambda i: (0, i)), diag, diag, ANY]
        + extra_specs,
        out_shape=(sd(dp.shape, BF16), sd((lp, d), F32), sd((d, lp), BF16),
                   sd((4, gc, S5_GROUP, S5_STATE), F32), sd((4, gc, S5_GROUP, S5_STATE), F32), sd((1, d), F32)) + extra_shapes,
        input_output_aliases={3: 0},
        scratch_shapes=[pltpu.VMEM(w_in.shape, BF16), pltpu.VMEM(bdre.shape, BF16), pltpu.VMEM(bdim.shape, BF16),
                        pltpu.VMEM(bdre.shape, F32), pltpu.VMEM(bdim.shape, F32), pltpu.VMEM((1, d), F32)],
        compiler_params=_params(),
    )(lam, dus, u, dp, h, dh, g, w_in, bdre, bdim)


def grad_w_in(n_t, dp, blk, name):
    d, lp = n_t.shape
    npart, _, width = dp.shape
    per = width // blk

    def body(n_ref, dp_ref, o_ref):
        o_ref[0] = jnp.dot(n_ref[...], dp_ref[0], preferred_element_type=F32).astype(o_ref.dtype)

    return pl.pallas_call(
        body, name=name, grid=(npart * per,),
        in_specs=[pl.BlockSpec((d, lp), lambda j: (0, 0), pipeline_mode=pl.Buffered(1)),
                  pl.BlockSpec((1, lp, blk), lambda j: (j // per, 0, j % per))],
        out_specs=pl.BlockSpec((1, d, blk), lambda j: (j, 0, 0)),
        out_shape=jax.ShapeDtypeStruct((npart * per, d, blk), BF16),
        compiler_params=_params(),
    )(n_t, dp)


def _conv_mix(cg, v, cw_ref, cb_ref, halo, c):
    hc = cg * v
    taps = cw_ref[c]
    conv = taps[2:3, :] * hc + taps[1:2, :] * _shift_down(hc, 1, halo) + taps[0:1, :] * _shift_down(hc, 2, halo) + cb_ref[c]
    return hc, conv


def conv_fwd(h, g, w_in, conv_w, conv_b, w_out, name):
    lp, d = h.shape
    tt = TOKEN_TILE
    nt = lp // tt
    nch, ce = w_out.shape[0], w_out.shape[1]

    def body(h_ref, g_ref, cw_ref, cb_ref, w_hbm, wo_hbm, o_ref, halo_ref, acts_ref, w, wo, halo):
        i = pl.program_id(0)

        @pl.when(i == 0)
        def _():
            pltpu.sync_copy(w_hbm, w)
            pltpu.sync_copy(wo_hbm, wo)
            halo[...] = jnp.zeros_like(halo)

        hv = h_ref[...]
        n = _rms_fwd(hv, g_ref[...])[0].astype(BF16)
        o = hv
        for c in range(nch):
            cols = slice(c * ce, (c + 1) * ce)
            bg, cg, v, z = [jnp.dot(n, w[p * nch + c], preferred_element_type=F32) for p in range(4)]
            for p, val in enumerate((bg, cg, v, z)):
                acts_ref[p, :, cols] = val.astype(BF16)
            hc, conv = _conv_mix(cg, v, cw_ref, cb_ref, halo[c], c)
            o = o + _dot(bg * conv * (z * _sigmoid(z)), wo[c])
            halo[c] = hc[tt - CONV_HALO:, :]
            halo_ref[0, c] = hc[tt - CONV_HALO:, :]
        o_ref[...] = o

    sd = jax.ShapeDtypeStruct
    return pl.pallas_call(
        body, name=name, grid=(nt,),
        in_specs=[pl.BlockSpec((tt, d), lambda i: (i, 0)), pl.BlockSpec((1, d), lambda i: (0, 0)),
                  pl.BlockSpec(conv_w.shape, lambda i: (0, 0, 0)), pl.BlockSpec(conv_b.shape, lambda i: (0, 0, 0)), ANY, ANY],
        out_specs=[pl.BlockSpec((tt, d), lambda i: (i, 0)), pl.BlockSpec((1, nch, CONV_HALO, ce), lambda i: (i, 0, 0, 0)),
                   pl.BlockSpec((4, tt, nch * ce), lambda i: (0, i, 0))],
        out_shape=(sd((lp, d), F32), sd((nt, nch, CONV_HALO, ce), F32), sd((4, lp, nch * ce), BF16)),
        scratch_shapes=[pltpu.VMEM(w_in.shape, BF16), pltpu.VMEM(w_out.shape, BF16), pltpu.VMEM((nch, CONV_HALO, ce), F32)],
        compiler_params=_params(),
    )(h, g, conv_w, conv_b, w_in, w_out)


def conv_bwd(h, dh, halos, acts, g, w_in, conv_w, conv_b, w_out, name):
    lp, d = h.shape
    tt = TOKEN_TILE
    nt = lp // tt
    nch, ce = w_out.shape[0], w_out.shape[1]

    def body(h_ref, dh_ref, halo_ref, acts_ref, g_ref, cw_ref, cb_ref, w_hbm, wo_hbm,
             dho_ref, n_ref, dp_ref, dwo_hbm, dcw_hbm, dcb_hbm, dg_hbm, w, wo, nxt, dwo, dcw, dcb, dg):
        i = pl.program_id(0)

        @pl.when(i == 0)
        def _():
            pltpu.sync_copy(w_hbm, w)
            pltpu.sync_copy(wo_hbm, wo)
            for ref in (nxt, dwo, dcw, dcb, dg):
                ref[...] = jnp.zeros_like(ref)

        gv = g_ref[...]
        nf, hh, rr = _rms_fwd(h_ref[...], gv)
        n_ref[...] = nf.T.astype(BF16)
        dhv = dh_ref[...]
        has_prev = (i < nt - 1).astype(F32)
        dn = jnp.zeros((tt, d), F32)
        for c in range(nch):
            halo = halo_ref[0, c] * has_prev
            cols = slice(c * ce, (c + 1) * ce)
            bg, cg, v, z = [acts_ref[p, :, cols].astype(F32) for p in range(4)]
            hc, conv = _conv_mix(cg, v, cw_ref, cb_ref, halo, c)
            sz, dsz = _silu_and_grad(z)
            y1 = bg * conv
            dy2 = _dot_nt(dhv, wo[c])
            dwo[c] += _dot_tn(y1 * sz, dhv)
            dy1 = dy2 * sz
            dz = dy2 * y1 * dsz
            dbg = dy1 * conv
            dconv = dy1 * bg
            dcb[c] += jnp.sum(dconv, axis=0, keepdims=True)
            up1 = _shift_up(dconv, 1, nxt[c])
            up2 = _shift_up(dconv, 2, nxt[c])
            nxt[c] = dconv[:CONV_HALO, :]
            taps = cw_ref[c]
            dhc = taps[2:3, :] * dconv + taps[1:2, :] * up1 + taps[0:1, :] * up2
            dcw[c, 0:1, :] += jnp.sum(hc * up2, axis=0, keepdims=True)
            dcw[c, 1:2, :] += jnp.sum(hc * up1, axis=0, keepdims=True)
            dcw[c, 2:3, :] += jnp.sum(hc * dconv, axis=0, keepdims=True)
            dcg = dhc * v
            dv = dhc * cg
            for p, val in enumerate((dbg, dcg, dv, dz)):
                dp_ref[p, :, cols] = val.astype(BF16)
                dn = dn + _dot_nt(val, w[p * nch + c])
        dg[...] += jnp.sum(dn * hh, axis=0, keepdims=True)
        dho_ref[...] = dhv + _rms_bwd(dn, hh, rr, gv)

        @pl.when(i == nt - 1)
        def _():
            pltpu.sync_copy(dwo, dwo_hbm)
            pltpu.sync_copy(dcw, dcw_hbm)
            pltpu.sync_copy(dcb, dcb_hbm)
            pltpu.sync_copy(dg, dg_hbm)

    rev = lambda i: (nt - 1 - i, 0)
    sd = jax.ShapeDtypeStruct
    return pl.pallas_call(
        body, name=name, grid=(nt,),
        in_specs=[pl.BlockSpec((tt, d), rev), pl.BlockSpec((tt, d), rev),
                  pl.BlockSpec((1, nch, CONV_HALO, ce), lambda i: (jnp.maximum(nt - 2 - i, 0), 0, 0, 0)),
                  pl.BlockSpec((4, tt, nch * ce), lambda i: (0, nt - 1 - i, 0)),
                  pl.BlockSpec((1, d), lambda i: (0, 0)),
                  pl.BlockSpec(conv_w.shape, lambda i: (0, 0, 0)), pl.BlockSpec(conv_b.shape, lambda i: (0, 0, 0)), ANY, ANY],
        out_specs=[pl.BlockSpec((tt, d), rev), pl.BlockSpec((d, tt), lambda i: (0, nt - 1 - i)),
                   pl.BlockSpec((4, tt, nch * ce), lambda i: (0, nt - 1 - i, 0)), ANY, ANY, ANY, ANY],
        out_shape=(sd((lp, d), F32), sd((d, lp), BF16), sd((4, lp, nch * ce), BF16),
                   sd(w_out.shape, F32), sd((nch, 8, ce), F32), sd((nch, 1, ce), F32), sd((1, d), F32)),
        scratch_shapes=[pltpu.VMEM(w_in.shape, BF16), pltpu.VMEM(w_out.shape, BF16), pltpu.VMEM((nch, CONV_HALO, ce), F32),
                        pltpu.VMEM(w_out.shape, F32), pltpu.VMEM((nch, 8, ce), F32), pltpu.VMEM((nch, 1, ce), F32),
                        pltpu.VMEM((1, d), F32)],
        compiler_params=_params(vmem=VMEM_LIMIT_LARGE),
    )(h, dh, halos, acts, g, conv_w, conv_b, w_in, w_out)


def _pool_mix(u, wg, bg_ref, sc_ref, halo, k, tile, tt, first_pos):
    ext = jnp.concatenate([halo, u], axis=0)
    win = _window_sums_back(ext)[k][POOL_HALO:, :]
    mixed = win * _pool_inv_count(tile, tt, first_pos, POOL_WINDOWS[k], u.shape[1]) - u
    outs = _dot(mixed, wg[k]) + bg_ref[k]
    return mixed, outs, outs * sc_ref[k]


def pool_fwd(h, g, w_in, w_grp, b_grp, scale, w_out, first_pos, name):
    lp, d = h.shape
    tt = TOKEN_TILE
    nt = lp // tt
    gw = w_grp.shape[1]

    def body(h_ref, g_ref, bg_ref, sc_ref, w_hbm, wg_hbm, wo_hbm, o_ref, halo_ref, acts_ref, w, wg, wo, halo):
        i = pl.program_id(0)

        @pl.when(i == 0)
        def _():
            pltpu.sync_copy(w_hbm, w)
            pltpu.sync_copy(wg_hbm, wg)
            pltpu.sync_copy(wo_hbm, wo)
            halo[...] = jnp.zeros_like(halo)

        hv = h_ref[...]
        n = _rms_fwd(hv, g_ref[...])[0].astype(BF16)
        o = hv
        for k in range(4):
            cols = slice(k * gw, (k + 1) * gw)
            u = jnp.dot(n, w[k], preferred_element_type=F32)
            z = jnp.dot(n, w[4 + k], preferred_element_type=F32)
            acts_ref[0, :, cols] = u.astype(BF16)
            acts_ref[1, :, cols] = z.astype(BF16)
            _, _, yp = _pool_mix(u, wg, bg_ref, sc_ref, halo[k], k, i, tt, first_pos)
            o = o + _dot(yp * (z * _sigmoid(z)), wo[k])
            halo[k] = u[tt - POOL_HALO:, :]
            halo_ref[0, k] = u[tt - POOL_HALO:, :]
        o_ref[...] = o

    sd = jax.ShapeDtypeStruct
    small = pl.BlockSpec((4, 1, gw), lambda i: (0, 0, 0))
    return pl.pallas_call(
        body, name=name, grid=(nt,),
        in_specs=[pl.BlockSpec((tt, d), lambda i: (i, 0)), pl.BlockSpec((1, d), lambda i: (0, 0)), small, small, ANY, ANY, ANY],
        out_specs=[pl.BlockSpec((tt, d), lambda i: (i, 0)), pl.BlockSpec((1, 4, POOL_HALO, gw), lambda i: (i, 0, 0, 0)),
                   pl.BlockSpec((2, tt, 4 * gw), lambda i: (0, i, 0))],
        out_shape=(sd((lp, d), F32), sd((nt, 4, POOL_HALO, gw), F32), sd((2, lp, 4 * gw), BF16)),
        scratch_shapes=[pltpu.VMEM(w_in.shape, BF16), pltpu.VMEM(w_grp.shape, BF16), pltpu.VMEM(w_out.shape, BF16),
                        pltpu.VMEM((4, POOL_HALO, gw), F32)],
        compiler_params=_params(),
    )(h, g, b_grp, scale, w_in, w_grp, w_out)


def pool_bwd(h, dh, halos, acts, g, w_in, w_grp, b_grp, scale, w_out, first_pos, name):
    lp, d = h.shape
    tt = TOKEN_TILE
    nt = lp // tt
    gw = w_grp.shape[1]

    def body(h_ref, dh_ref, halo_ref, acts_ref, g_ref, bg_ref, sc_ref, w_hbm, wg_hbm, wo_hbm,
             dho_ref, n_ref, dp_ref, dwo_hbm, dwg_hbm, dbg_hbm, dsc_hbm, dg_hbm,
             w, wg, wo, nxt, dwo, dwg, dbg, dsc, dg):
        i = pl.program_id(0)
        tile = nt - 1 - i

        @pl.when(i == 0)
        def _():
            pltpu.sync_copy(w_hbm, w)
            pltpu.sync_copy(wg_hbm, wg)
            pltpu.sync_copy(wo_hbm, wo)
            for ref in (nxt, dwo, dwg, dbg, dsc, dg):
                ref[...] = jnp.zeros_like(ref)

        gv = g_ref[...]
        nf, hh, rr = _rms_fwd(h_ref[...], gv)
        n_ref[...] = nf.T.astype(BF16)
        dhv = dh_ref[...]
        has_prev = (i < nt - 1).astype(F32)
        dn = jnp.zeros((tt, d), F32)
        for k in range(4):
            cols = slice(k * gw, (k + 1) * gw)
            u, z = acts_ref[0, :, cols].astype(F32), acts_ref[1, :, cols].astype(F32)
            mixed, outs, yp = _pool_mix(u, wg, bg_ref, sc_ref, halo_ref[0, k] * has_prev, k, tile, tt, first_pos)
            sz, dsz = _silu_and_grad(z)
            dy = _dot_nt(dhv, wo[k])
            dwo[k] += _dot_tn(yp * sz, dhv)
            dyp = dy * sz
            dz = dy * yp * dsz
            dsc[k] += jnp.sum(dyp * outs, axis=0, keepdims=True)
            douts = dyp * sc_ref[k]
            dbg[k] += jnp.sum(douts, axis=0, keepdims=True)
            dwg[k] += _dot_tn(mixed, douts)
            dmixed = _dot_nt(douts, wg[k])
            dm = dmixed * _pool_inv_count(tile, tt, first_pos, POOL_WINDOWS[k], gw)
            ext = jnp.concatenate([dm, nxt[k]], axis=0)
            du = _window_sums_fwd(ext)[k][:tt, :] - dmixed
            nxt[k] = dm[:POOL_HALO, :]
            dp_ref[0, :, cols] = du.astype(BF16)
            dp_ref[1, :, cols] = dz.astype(BF16)
            dn = dn + _dot_nt(du, w[k]) + _dot_nt(dz, w[4 + k])
        dg[...] += jnp.sum(dn * hh, axis=0, keepdims=True)
        dho_ref[...] = dhv + _rms_bwd(dn, hh, rr, gv)

        @pl.when(i == nt - 1)
        def _():
            pltpu.sync_copy(dwo, dwo_hbm)
            pltpu.sync_copy(dwg, dwg_hbm)
            pltpu.sync_copy(dbg, dbg_hbm)
            pltpu.sync_copy(dsc, dsc_hbm)
            pltpu.sync_copy(dg, dg_hbm)

    rev = lambda i: (nt - 1 - i, 0)
    sd = jax.ShapeDtypeStruct
    small = pl.BlockSpec((4, 1, gw), lambda i: (0, 0, 0))
    return pl.pallas_call(
        body, name=name, grid=(nt,),
        in_specs=[pl.BlockSpec((tt, d), rev), pl.BlockSpec((tt, d), rev),
                  pl.BlockSpec((1, 4, POOL_HALO, gw), lambda i: (jnp.maximum(nt - 2 - i, 0), 0, 0, 0)),
                  pl.BlockSpec((2, tt, 4 * gw), lambda i: (0, nt - 1 - i, 0)),
                  pl.BlockSpec((1, d), lambda i: (0, 0)), small, small, ANY, ANY, ANY],
        out_specs=[pl.BlockSpec((tt, d), rev), pl.BlockSpec((d, tt), lambda i: (0, nt - 1 - i)),
                   pl.BlockSpec((2, tt, 4 * gw), lambda i: (0, nt - 1 - i, 0)), ANY, ANY, ANY, ANY, ANY],
        out_shape=(sd((lp, d), F32), sd((d, lp), BF16), sd((2, lp, 4 * gw), BF16),
                   sd(w_out.shape, F32), sd(w_grp.shape, F32), sd((4, 1, gw), F32), sd((4, 1, gw), F32), sd((1, d), F32)),
        scratch_shapes=[pltpu.VMEM(w_in.shape, BF16), pltpu.VMEM(w_grp.shape, BF16), pltpu.VMEM(w_out.shape, BF16),
                        pltpu.VMEM((4, POOL_HALO, gw), F32), pltpu.VMEM(w_out.shape, F32), pltpu.VMEM(w_grp.shape, F32),
                        pltpu.VMEM((4, 1, gw), F32), pltpu.VMEM((4, 1, gw), F32), pltpu.VMEM((1, d), F32)],
        compiler_params=_params(),
    )(h, dh, halos, acts, g, b_grp, scale, w_in, w_grp, w_out)


def loss_head(h, target, g, pad_tiles, name):
    lp, d = h.shape
    tt = TOKEN_TILE
    nt = lp // tt

    def body(h_ref, t_ref, g_ref, dh_ref, dg_ref, loss_ref, acc):
        i = pl.program_id(0)

        @pl.when(i == 0)
        def _():
            acc[...] = jnp.zeros_like(acc)
            dg_ref[...] = jnp.zeros_like(dg_ref)

        @pl.when(i < pad_tiles)
        def _():
            dh_ref[...] = jnp.zeros_like(dh_ref)

        @pl.when(i >= pad_tiles)
        def _():
            gv = g_ref[...]
            n, hh, rr = _rms_fwd(h_ref[...], gv)
            err = n - t_ref[...]
            acc[...] += 0.5 * jnp.sum(jnp.mean(err * err, axis=-1, keepdims=True), axis=0, keepdims=True)
            dn = err * (1.0 / d)
            dg_ref[...] += jnp.sum(dn * hh, axis=0, keepdims=True)
            dh_ref[...] = _rms_bwd(dn, hh, rr, gv)

        loss_ref[...] = jnp.broadcast_to(acc[...], loss_ref.shape)

    sd = jax.ShapeDtypeStruct
    return pl.pallas_call(
        body, name=name, grid=(nt,),
        in_specs=[pl.BlockSpec((tt, d), lambda i: (i, 0)), pl.BlockSpec((tt, d), lambda i: (jnp.maximum(i - pad_tiles, 0), 0)),
                  pl.BlockSpec((1, d), lambda i: (0, 0))],
        out_specs=[pl.BlockSpec((tt, d), lambda i: (i, 0)), pl.BlockSpec((1, d), lambda i: (0, 0)),
                   pl.BlockSpec((8, 128), lambda i: (0, 0))],
        out_shape=(sd((lp, d), F32), sd((1, d), F32), sd((8, 128), F32)),
        scratch_shapes=[pltpu.VMEM((1, 1), F32)],
        compiler_params=_params(),
    )(h, target, g)


def _peers(x, y, c):
    out = []
    for k in range(1, N_DEV):
        px = 1 - x if k & 4 else x
        py = 1 - y if k & 2 else y
        pc = 1 - c if k & 1 else c
        out.append((k, (px, py, pc), 4 * px + 2 * py + pc))
    return out


def exchange_start(arrs, gather, after, name):
    n = len(arrs)
    me = 4 * lax.axis_index("x") + 2 * lax.axis_index("y") + lax.axis_index("c")
    lands = []
    for a in arrs:
        own = a[None] if gather else lax.dynamic_index_in_dim(a, me, 0, keepdims=True)
        lands.append(lax.dynamic_update_index_in_dim(lax.empty(((N_DEV,) + a.shape) if gather else a.shape, a.dtype), own, me, 0))

    def body(*refs):
        ins, land = refs[:n], refs[n:2 * n]
        send_sems, recv_sems, token = refs[2 * n + 1], refs[2 * n + 2], refs[4 * n + 3]
        x, y, c = lax.axis_index("x"), lax.axis_index("y"), lax.axis_index("c")
        me = 4 * x + 2 * y + c
        for k, pid, peer in _peers(x, y, c):
            for a in range(n):
                pltpu.make_async_remote_copy(
                    src_ref=ins[a] if gather else ins[a].at[peer], dst_ref=land[a].at[me],
                    send_sem=send_sems.at[a * (N_DEV - 1) + k - 1], recv_sem=recv_sems.at[a * (N_DEV - 1) + k - 1],
                    device_id=pid, device_id_type=pl.DeviceIdType.MESH).start()
        token[...] = jnp.zeros_like(token)

    hbm = pl.BlockSpec(memory_space=pltpu.HBM)
    sem = pl.BlockSpec(memory_space=pltpu.SEMAPHORE)
    sems = pltpu.SemaphoreType.DMA((n * (N_DEV - 1),))
    res = pl.pallas_call(
        body, name=name, in_specs=[hbm] * (2 * n) + [ANY],
        out_specs=[sem, sem] + [hbm] * (2 * n) + [pl.BlockSpec(memory_space=pltpu.VMEM)],
        out_shape=[sems, sems] + [pltpu.HBM(a.shape, a.dtype) for a in arrs] + [pltpu.HBM(l.shape, l.dtype) for l in lands]
        + [jax.ShapeDtypeStruct((8, 128), F32)],
        input_output_aliases={a: 2 + a for a in range(2 * n)},
        compiler_params=pltpu.CompilerParams(has_side_effects=pltpu.SideEffectType.DATAFLOW_SIDE_EFFECTING),
    )(*[pltpu.with_memory_space_constraint(a, pltpu.HBM) for a in list(arrs) + lands], after)
    return res[0], res[1], res[2:2 + n], res[2 + n:2 + 2 * n], res[-1]


def exchange_wait(started, gather, after, name):
    send_sems, recv_sems, srcs, lands, _ = started
    n = len(srcs)
    after = list(after) if isinstance(after, (list, tuple)) else [after]

    def body(*refs):
        ins, land = refs[:n], refs[n:2 * n]
        send_sems, recv_sems = refs[2 * n], refs[2 * n + 1]
        x, y, c = lax.axis_index("x"), lax.axis_index("y"), lax.axis_index("c")
        for k, pid, peer in _peers(x, y, c):
            for a in range(n):
                cp = pltpu.make_async_remote_copy(
                    src_ref=ins[a] if gather else ins[a].at[peer], dst_ref=land[a].at[peer],
                    send_sem=send_sems.at[a * (N_DEV - 1) + k - 1], recv_sem=recv_sems.at[a * (N_DEV - 1) + k - 1],
                    device_id=pid, device_id_type=pl.DeviceIdType.MESH)
                cp.wait_send()
                cp.wait_recv()

    hbm = pl.BlockSpec(memory_space=pltpu.HBM)
    sem = pl.BlockSpec(memory_space=pltpu.SEMAPHORE)
    res = pl.pallas_call(
        body, name=name, in_specs=[hbm] * (2 * n) + [sem, sem] + [ANY] * len(after),
        out_specs=[hbm] * (2 * n),
        out_shape=[pltpu.HBM(a.shape, a.dtype) for a in list(srcs) + list(lands)],
        input_output_aliases={a: a for a in range(2 * n)},
        compiler_params=pltpu.CompilerParams(has_side_effects=pltpu.SideEffectType.DATAFLOW_SIDE_EFFECTING),
    )(*srcs, *lands, send_sems, recv_sems, *after)
    return res[n:]


def _adamw(w, g, m, v):
    m = ADAM_B1 * m + (1.0 - ADAM_B1) * g
    v = ADAM_B2 * v + (1.0 - ADAM_B2) * (g * g)
    m_hat = m / (1.0 - ADAM_B1 ** ADAM_STEP)
    v_hat = v / (1.0 - ADAM_B2 ** ADAM_STEP)
    return -ADAM_LR * (m_hat / (jnp.sqrt(v_hat) + ADAM_EPS) + ADAM_WD * w), m, v


def _update_tile_rows(rows, cols):
    if rows * cols <= UPDATE_TILE_ELEMS:
        return rows
    return max(t for t in range(8, UPDATE_TILE_ELEMS // cols + 1, 8) if rows % t == 0)


def _sum_in_order(p_ref):
    g = p_ref[0].astype(F32)
    for j in range(1, p_ref.shape[0]):
        g = g + p_ref[j].astype(F32)
    return g


def sum_parts(parts, name):
    nparts, rows, cols = parts.shape
    tr = _update_tile_rows(rows, cols)

    def body(p_ref, g_ref):
        g_ref[...] = _sum_in_order(p_ref)

    return pl.pallas_call(
        body, name=name, grid=(rows // tr,),
        in_specs=[pl.BlockSpec((nparts, tr, cols), lambda i: (0, i, 0))],
        out_specs=pl.BlockSpec((tr, cols), lambda i: (i, 0)), out_shape=jax.ShapeDtypeStruct((rows, cols), F32),
        compiler_params=_params(),
    )(parts)


def sum_adamw(parts, w, m, v, name):
    rows, cols = w.shape
    nparts = parts.shape[0]
    tr = _update_tile_rows(rows, cols)

    def body(p_ref, w_ref, m_ref, v_ref, g_ref, d_ref, nm_ref, nv_ref):
        g = _sum_in_order(p_ref)
        delta, nm, nv = _adamw(w_ref[...], g, m_ref[...], v_ref[...])
        g_ref[...] = g
        d_ref[...] = delta
        nm_ref[...] = nm
        nv_ref[...] = nv

    blk = pl.BlockSpec((tr, cols), lambda i: (i, 0))
    sd = jax.ShapeDtypeStruct((rows, cols), F32)
    return pl.pallas_call(
        body, name=name, grid=(rows // tr,),
        in_specs=[pl.BlockSpec((nparts, tr, cols), lambda i: (0, i, 0)), blk, blk, blk],
        out_specs=[blk] * 4, out_shape=(sd,) * 4,
        compiler_params=_params(),
    )(parts, w, m, v)


def update_packed(g, w, m, v, pieces, name):
    rows_all = w.shape[0]

    def body(g_ref, w_ref, m_ref, v_ref, *outs):
        gv = g_ref[:rows_all, :]
        res = (gv,) + _adamw(w_ref[...], gv, m_ref[...], v_ref[...])
        for p, (row, rows, lanes) in enumerate(pieces):
            for k in range(4):
                outs[4 * p + k][...] = res[k][row:row + rows, :lanes]

    shapes = [jax.ShapeDtypeStruct((rows, lanes), F32) for _, rows, lanes in pieces for _ in range(4)]
    return pl.pallas_call(body, name=name, out_shape=shapes,
                          compiler_params=pltpu.CompilerParams(vmem_limit_bytes=VMEM_LIMIT))(g, w, m, v)


def update_natural(groups, name):
    n = len(groups)
    steps = 8

    def body(*refs):
        ins, outs = refs[:4 * n], refs[4 * n:]
        for j in range(n):
            g_ref, w_ref, m_ref, v_ref = ins[4 * j:4 * j + 4]
            gv = g_ref[...]
            res = (gv,) + _adamw(w_ref[...], gv, m_ref[...], v_ref[...])
            for k in range(4):
                outs[4 * j + k][...] = res[k]

    specs, shapes = [], []
    for g, w, m, v in groups:
        rows, cols = w.shape
        specs += [pl.BlockSpec((rows // steps, cols), lambda i: (i, 0))] * 4
        shapes += [jax.ShapeDtypeStruct((rows, cols), F32)] * 4
    return pl.pallas_call(body, name=name, grid=(steps,), in_specs=specs, out_specs=specs, out_shape=shapes,
                          compiler_params=_params())(*[a for grp in groups for a in grp])


S5_NAMES = ("w_in", "lam_re", "lam_im", "log_dt", "b_re", "b_im", "c_re", "c_im", "d_skip", "w_glu", "b_glu", "w_out")
CONV_NAMES = ("w_in", "conv_w", "conv_b", "w_out")
POOL_NAMES = ("w_in", "w_grp", "b_grp", "scale", "w_out")
LAYER_KINDS = ("s5", "conv", "pool", "s5")
LAYER_NAMES = {"s5": S5_NAMES, "conv": CONV_NAMES, "pool": POOL_NAMES}
SHARDED = {"s5": ("w_in", "w_glu", "w_out"), "conv": ("w_in", "conv_w", "w_out"), "pool": ("w_in", "w_grp", "b_grp", "w_out")}
GATHER_F32 = ("conv_w", "b_grp")


def weight_names():
    names = ["meta_tokens"]
    for i, kind in enumerate(LAYER_KINDS):
        names.append("norm%d_g" % i)
        names += ["l%d_%s" % (i, n) for n in LAYER_NAMES[kind]]
    names.append("final_g")
    return names


def sharded_names():
    return ["meta_tokens"] + ["l%d_%s" % (i, n) for i, kind in enumerate(LAYER_KINDS) for n in SHARDED[kind]]


def _block_diag_in_grad(blocks):
    _, gc, i, p = blocks.shape
    return jnp.transpose(blocks, (2, 0, 1, 3)).reshape(i, 4 * gc, p)


def _block_diag_out_grad(blocks):
    _, gc, i, p = blocks.shape
    return blocks.reshape(4 * gc, i, p)


def _to_owner_blocks(a, axis):
    shape = a.shape[:axis] + (N_DEV, a.shape[axis] // N_DEV) + a.shape[axis + 1:]
    return jnp.moveaxis(a.reshape(shape), axis, 0)


def _from_owner_blocks(a, axis):
    a = jnp.moveaxis(a, 0, axis)
    return a.reshape(a.shape[:axis] + (a.shape[axis] * a.shape[axis + 1],) + a.shape[axis + 2:])


def _step(x, target, weights, moments_m, moments_v):
    seq, d = x.shape[1], x.shape[2]
    n_meta = weights["meta_tokens"].shape[0]
    tt = TOKEN_TILE
    pad_tiles = -(-n_meta // tt)
    p0 = pad_tiles * tt
    lp = p0 + seq
    first_pos = p0 - n_meta
    gc = d // 4 // S5_GROUP
    cw = d // 4

    big_names = [n for n in sharded_names() if n != "meta_tokens" and n.split("_", 1)[1] not in GATHER_F32]
    small_names = [n for n in sharded_names() if n not in big_names]
    layer_big = [[n for n in big_names if n.startswith("l%d_" % i)] for i in range(len(LAYER_KINDS))]
    layer_big[0] = small_names + layer_big[0]
    gather_started = []
    after = jnp.zeros((8, 128), F32)
    for i, names in enumerate(layer_big):
        gather_started.append(exchange_start([weights[n] if n in small_names else weights[n].astype(BF16) for n in names], True,
                                             after, "gather_start_l%d" % i))
        after = gather_started[-1][4]

    def vec(name):
        return weights[name].reshape(1, -1)

    s5_prep = {}
    for i, kind in enumerate(LAYER_KINDS):
        if kind == "s5":
            p = "l%d_" % i
            lr, li = weights[p + "lam_re"], weights[p + "lam_im"] + after[0, 0]
            ldt = weights[p + "log_dt"].reshape(-1, 1)
            br_t = jnp.transpose(weights[p + "b_re"], (2, 0, 1))
            bi_t = jnp.transpose(weights[p + "b_im"], (2, 0, 1))
            ar, ai, bdre, bdim, cdre, cdim = s5_disc_fwd(lr, li, ldt, br_t, bi_t, weights[p + "c_re"], weights[p + "c_im"],
                                                         p + "disc_fwd")
            s5_prep[i] = dict(
                disc=(lr, li, ldt, br_t, bi_t), ar=ar.reshape(4, -1, 128), ai=ai.reshape(4, -1, 128),
                bdre=bdre, bdim=bdim, cdre=cdre, cdim=cdim,
                d_skip=weights[p + "d_skip"].reshape(4, 1, cw), b_glu=vec(p + "b_glu"))
    h = jnp.concatenate([jnp.zeros((p0, d), F32), x[0] + after[0, 0]], axis=0)

    prepared = [h] + [s5_prep[i][k] for i in s5_prep for k in ("bdre", "bdim", "cdre", "cdim")]
    gathered = dict(zip(layer_big[0], exchange_wait(gather_started[0], True, prepared, "gather_wait_l0")))
    h = lax.dynamic_update_slice(h, _from_owner_blocks(gathered["meta_tokens"], 1), (first_pos, 0))

    full = {}

    def layer_weights(i, kind, after):
        p = "l%d_" % i
        if i > 0:
            gathered.update(zip(layer_big[i], exchange_wait(gather_started[i], True, after, "gather_wait_l%d" % i)))
        w_in = gathered[p + "w_in"]
        if kind == "s5":
            full[i] = dict(s5_prep[i], w_in=w_in, w_glu=gathered[p + "w_glu"].reshape(4, cw, d),
                           w_out=gathered[p + "w_out"].reshape(4, cw, d))
        elif kind == "conv":
            ce = w_in.shape[2]
            nch = 2
            conv_w = _from_owner_blocks(gathered[p + "conv_w"], 1)
            full[i] = dict(
                w_in=w_in, conv_w=jnp.transpose(conv_w.reshape(CONV_K, nch, ce), (1, 0, 2)),
                conv_b=weights[p + "conv_b"].reshape(nch, 1, ce), w_out=gathered[p + "w_out"].reshape(nch, ce, d))
        else:
            gw = w_in.shape[2]
            full[i] = dict(
                w_in=w_in, w_grp=_from_owner_blocks(gathered[p + "w_grp"], 1),
                b_grp=_from_owner_blocks(gathered[p + "b_grp"], 1).reshape(4, 1, gw),
                scale=weights[p + "scale"].reshape(4, 1, gw), w_out=gathered[p + "w_out"].reshape(4, gw, d))
        return full[i]

    saved = {}
    for i, kind in enumerate(LAYER_KINDS):
        p, f, g = "l%d_" % i, layer_weights(i, kind, h), vec("norm%d_g" % i)
        if kind == "s5":
            u, z, xs = s5_fwd1(h, g, f["w_in"], f["bdre"], f["bdim"], p + "fwd_in")
            s = s5_scan_fwd(xs, f["ar"], f["ai"], p + "scan_fwd")
            h_in = h
            h, y, q = s5_fwd3(s, u, z, h, f["cdre"], f["cdim"], f["w_glu"], f["w_out"], f["d_skip"], f["b_glu"], p + "fwd_out")
            saved[i] = (h_in, u, z, s, y, q)
        elif kind == "conv":
            h_new, halos, acts = conv_fwd(h, g, f["w_in"], f["conv_w"], f["conv_b"], f["w_out"], p + "fwd")
            saved[i] = (h, halos, acts)
            h = h_new
        else:
            h_new, halos, acts = pool_fwd(h, g, f["w_in"], f["w_grp"], f["b_grp"], f["scale"], f["w_out"], first_pos, p + "fwd")
            saved[i] = (h, halos, acts)
            h = h_new

    dh, dg_final, loss_tile = loss_head(h, target[0], vec("final_g"), pad_tiles, "loss_head")
    loss = lax.psum(loss_tile[0, 0], ("x", "y", "c"))

    grads = {"final_g": dg_final}
    names = weight_names()
    sh_names = sharded_names()
    replicated = [n for n in names if n not in sh_names]
    vectors = [n for n in replicated if weights[n].ndim == 1]
    matrices = [n for n in replicated if weights[n].ndim > 1]
    rep_names = vectors + matrices

    def owner_blocks(a):
        return a.reshape(N_DEV, -1, a.shape[-1]).astype(BF16)

    def as2d(a):
        return a.reshape(-1, a.shape[-1])

    def pack(tree, which=None):
        flat = [jnp.pad(tree[n].reshape(-1), (0, -tree[n].size % PACK_ALIGN)) for n in (which or rep_names)]
        flat = jnp.concatenate(flat)
        if which is None:
            flat = jnp.pad(flat, (0, -flat.size % (PACK_ROWS * 128)))
        return flat.reshape(-1, 128)

    layer_sharded, scatter_started = {}, {}
    ordered = jnp.zeros((), F32)
    for i in reversed(range(len(LAYER_KINDS))):
        kind = LAYER_KINDS[i]
        p, f, g = "l%d_" % i, full[i], vec("norm%d_g" % i) + ordered
        if kind == "s5":
            h_in, u, z, s, y, q = saved[i]
            dy, dp, dwo, dwg, dbg = s5_bwd3a(dh, y, q, z, f["w_glu"], f["w_out"], f["b_glu"] + ordered, p + "bwd_out")
            d_skip = f["d_skip"]
            if i == 0:
                early_names = [p + "w_glu", p + "w_out"]
                scatter_started["early"] = exchange_start([owner_blocks(dwg), owner_blocks(dwo)], False, dy,
                                                          "scatter_start_l0_early")
                d_skip = d_skip + scatter_started["early"][4][0, 0]
            ds, dus, dcre, dcim, dd = s5_bwd3b(dy, s, u, f["cdre"], f["cdim"], d_skip, p + "bwd_read")
            lam, dar, dai = s5_scan_bwd(ds, s, f["ar"], f["ai"], p + "scan_bwd")
            res = s5_bwd1(lam, dus, u, dp, h_in, dh, g, f["w_in"], f["bdre"], f["bdim"], p + "bwd_in",
                          pad_tiles=pad_tiles if i == 0 else None)
            dp, dh, n, dbre, dbim, dg = res[:6]
            if i == 0:
                grad_x = res[6][None]
            dw_in = grad_w_in(n, dp, f["w_in"].shape[2], p + "grad_w_in")
            grads.update({p + "w_in": dw_in, p + "w_glu": dwg.reshape(N_DEV, -1, d), p + "w_out": dwo.reshape(N_DEV, -1, d),
                          p + "d_skip": dd, p + "b_glu": dbg})

            def replicated_grads(p=p, f=f, dar=dar, dai=dai, dbre=dbre, dbim=dbim, dcre=dcre, dcim=dcim, token=None):
                lr, li, ldt, br_t, bi_t = f["disc"]
                dlr, dli, dldt, dbr_t, dbi_t = s5_disc_bwd(
                    lr, li, ldt, br_t, bi_t, dar.reshape(lr.shape) + token, dai.reshape(lr.shape),
                    _block_diag_in_grad(dbre), _block_diag_in_grad(dbim), p + "disc_bwd")
                grads.update({
                    p + "lam_re": dlr, p + "lam_im": dli, p + "log_dt": dldt,
                    p + "b_re": jnp.transpose(dbr_t, (1, 2, 0)), p + "b_im": jnp.transpose(dbi_t, (1, 2, 0)),
                    p + "c_re": _block_diag_out_grad(dcre), p + "c_im": -_block_diag_out_grad(dcim)})
        elif kind == "conv":
            replicated_grads = None
            h_in, halos, acts = saved[i]
            dh, n, dp, dwo, dcw, dcb, dg = conv_bwd(h_in, dh, halos, acts, g, f["w_in"], f["conv_w"], f["conv_b"], f["w_out"], p + "bwd")
            dw_in = grad_w_in(n, dp, f["w_in"].shape[2], p + "grad_w_in")
            dconv_w = jnp.transpose(dcw[:, :CONV_K, :], (1, 0, 2)).reshape(CONV_K, -1)
            grads.update({p + "w_in": dw_in, p + "conv_w": _to_owner_blocks(dconv_w, 1), p + "conv_b": dcb,
                          p + "w_out": dwo.reshape(N_DEV, -1, d)})
        else:
            replicated_grads = None
            h_in, halos, acts = saved[i]
            dh, n, dp, dwo, dwgrp, dbgrp, dsc, dg = pool_bwd(h_in, dh, halos, acts, g, f["w_in"], f["w_grp"], f["b_grp"], f["scale"],
                                                             f["w_out"], first_pos, p + "bwd")
            dw_in = grad_w_in(n, dp, f["w_in"].shape[2], p + "grad_w_in")
            grads.update({p + "w_in": dw_in, p + "w_grp": _to_owner_blocks(dwgrp, 1),
                          p + "b_grp": _to_owner_blocks(dbgrp.reshape(4, -1), 1), p + "scale": dsc,
                          p + "w_out": dwo.reshape(N_DEV, -1, d)})
        grads["norm%d_g" % i] = dg
        layer_sharded[i] = ["l%d_%s" % (i, n) for n in SHARDED[kind]]
        if i > 0:
            scatter_started[i] = exchange_start([owner_blocks(grads[n]) for n in layer_sharded[i]], False, dh,
                                                "scatter_start_l%d" % i)
            ordered = scatter_started[i][4][0, 0]
        if replicated_grads is not None:
            replicated_grads(token=ordered)
    grads["meta_tokens"] = _to_owner_blocks(dh[first_pos:p0], 1)
    last = len(LAYER_KINDS)
    layer_sharded[last] = ["meta_tokens", "replicated"]
    scatter_started[last] = exchange_start([owner_blocks(grads["meta_tokens"]), pack(grads).reshape(N_DEV, -1, 128)], False,
                                           dh, "scatter_start_replicated")
    layer_sharded["early"] = early_names
    layer_sharded[0] = [n for n in layer_sharded[0] if n not in early_names]

    out = {}
    received = {}
    after = [scatter_started[last][4]]
    for i in list(reversed(range(1, last))) + [last, "early", 0]:
        received.update(zip(layer_sharded[i], exchange_wait(scatter_started[i], False, after, "scatter_wait_%s" % i)))
        updated = []
        for n in layer_sharded[i]:
            if n != "replicated":
                res = sum_adamw(received[n], as2d(weights[n]), as2d(moments_m[n]), as2d(moments_v[n]), "update_" + n)
                out[n] = [r.reshape(weights[n].shape) for r in res]
                updated.append(out[n][0])
        after = updated or after
        if i == last:
            g_sum = sum_parts(received["replicated"], "sum_replicated")
            small_gather = exchange_start([g_sum], True, g_sum, "gather_small_grads_start")
            scatter_started[0] = exchange_start([owner_blocks(grads[n]) for n in layer_sharded[0]], False, small_gather[4],
                                                "scatter_start_l0")
            g_full = exchange_wait(small_gather, True, scatter_started[0][4], "gather_small_grads_wait")[0].reshape(-1, 128)
            offsets, offset = {}, 0
            for n in rep_names:
                offsets[n] = offset
                offset += weights[n].size + (-weights[n].size % PACK_ALIGN)
            pieces = [(offsets[n] // 128, max(weights[n].size // 128, 1), min(weights[n].size, 128)) for n in vectors]
            res = update_packed(g_full, pack(weights, vectors), pack(moments_m, vectors), pack(moments_v, vectors), pieces,
                                "update_replicated_vectors")
            for j, n in enumerate(vectors):
                out[n] = [r.reshape(weights[n].shape) for r in res[4 * j:4 * j + 4]]
            flat = g_full.reshape(-1)
            groups = [(flat[offsets[n]:offsets[n] + weights[n].size].reshape(as2d(weights[n]).shape), as2d(weights[n]),
                       as2d(moments_m[n]), as2d(moments_v[n])) for n in matrices]
            res = update_natural(groups, "update_replicated_matrices")
            for j, n in enumerate(matrices):
                out[n] = [r.reshape(weights[n].shape) for r in res[4 * j:4 * j + 4]]
            after = [out[n][k] for n in rep_names for k in range(4)]

    return (loss, grad_x) + tuple(out[n][k] for k in range(4) for n in names)


def kernel(x, *rest):
    names = weight_names()
    nw = len(names)
    weights = dict(zip(names, rest[:nw]))
    target = rest[nw]
    moments_m = dict(zip(names, rest[nw + 1:2 * nw + 1]))
    moments_v = dict(zip(names, rest[2 * nw + 1:3 * nw + 1]))
    return _step(x, target, weights, moments_m, moments_v)
```

```python
import math

import jax
import jax.numpy as jnp
from jax import lax
from jax.experimental import pallas as pl
from jax.experimental.pallas import tpu as pltpu

F32 = jnp.float32
BF16 = jnp.bfloat16
EPS = 1e-6
N_DEV = 8
TOKEN_TILE = 256
SCAN_CHUNKS = 4
SCAN_UNROLL = 8
S5_GROUP = 16
S5_STATE = 64
POOL_WINDOWS = (2, 4, 8, 16)
POOL_HALO = 16
CONV_K = 3
CONV_HALO = 8
ADAM_LR = 0.001
ADAM_B1 = 0.9
ADAM_B2 = 0.999
ADAM_EPS = 1e-08
ADAM_WD = 0.01
ADAM_STEP = 10
GELU_C = math.sqrt(2.0 / math.pi)
GELU_A = 0.044715
UPDATE_TILE_ELEMS = 1 << 17
PACK_ROWS = 512
PACK_ALIGN = 8 * 128
HIGH_HALF = -65536
HALF_OF_LOW_HALF = 0x8000
VMEM_LIMIT = 56 << 20
VMEM_LIMIT_LARGE = 62 << 20

ANY = pl.BlockSpec(memory_space=pl.ANY)


def _params(vmem=VMEM_LIMIT, ndim=1):
    return pltpu.CompilerParams(vmem_limit_bytes=vmem, dimension_semantics=("arbitrary",) * ndim)


def _dot(a, b):
    return jnp.dot(a.astype(BF16), b.astype(BF16), preferred_element_type=F32)


def _dot_nt(a, b):
    return lax.dot_general(a.astype(BF16), b.astype(BF16), (((1,), (1,)), ((), ())), preferred_element_type=F32)


def _dot_tn(a, b):
    return lax.dot_general(a.astype(BF16), b.astype(BF16), (((0,), (0,)), ((), ())), preferred_element_type=F32)


def _rms_fwd(h, g):
    r = lax.rsqrt(jnp.mean(h * h, axis=-1, keepdims=True) + EPS)
    hh = h * r
    return hh * g, hh, r


def _rms_bwd(dn, hh, r, g):
    dhh = dn * g
    return r * (dhh - hh * jnp.mean(dhh * hh, axis=-1, keepdims=True))


def _sigmoid(x):
    return 1.0 / (1.0 + jnp.exp(-x))


def _silu_and_grad(z):
    s = _sigmoid(z)
    return z * s, s * (1.0 + z * (1.0 - s))


def _gelu(y):
    t = jnp.tanh(GELU_C * (y + GELU_A * y * y * y))
    return 0.5 * y * (1.0 + t), t


def _gelu_grad(y, t):
    return 0.5 * (1.0 + t) + 0.5 * y * (1.0 - t * t) * GELU_C * (1.0 + 3.0 * GELU_A * y * y)


def _rows(shape):
    return lax.broadcasted_iota(jnp.int32, shape, 0)


def _shift_down(x, k, halo):
    y = pltpu.roll(x, k, 0)
    rows = _rows(x.shape)
    for j in range(k):
        y = jnp.where(rows == j, halo[halo.shape[0] - k + j:halo.shape[0] - k + j + 1, :], y)
    return y


def _shift_up(x, k, halo):
    n = x.shape[0]
    y = pltpu.roll(x, n - k, 0)
    rows = _rows(x.shape)
    for j in range(k):
        y = jnp.where(rows == n - k + j, halo[j:j + 1, :], y)
    return y


def _window_sums_back(ext):
    out = []
    s = ext
    for k in (1, 2, 4, 8):
        s = s + pltpu.roll(s, k, 0)
        out.append(s)
    return out


def _window_sums_fwd(ext):
    n = ext.shape[0]
    out = []
    s = ext
    for k in (1, 2, 4, 8):
        s = s + pltpu.roll(s, n - k, 0)
        out.append(s)
    return out


def _pool_inv_count(tile, tt, first_pos, w, width):
    pos = _rows((tt, width)) + (tile * tt - first_pos + 1)
    return 1.0 / jnp.clip(pos, 1, w).astype(F32)


def _slab_spec(lp, tt, sw):
    nj = sw // 128
    return pl.BlockSpec((4 * tt * nj, 128), lambda i: (i, 0)), (lp * 4 * nj, 128)


def _pack_pair(re, im):
    def rounded(v):
        return lax.bitcast_convert_type(v, jnp.int32) + HALF_OF_LOW_HALF
    return lax.bitcast_convert_type((rounded(re) & HIGH_HALF) | lax.shift_right_logical(rounded(im), 16), F32)


def _unpack_pair(w):
    b = lax.bitcast_convert_type(w, jnp.int32)
    return lax.bitcast_convert_type(b & HIGH_HALF, F32), lax.bitcast_convert_type(lax.shift_left(b, 16), F32)


def _slab_load(ref, c):
    nj = ref.shape[0] // (4 * TOKEN_TILE)
    first = c * TOKEN_TILE * nj
    return _unpack_pair(jnp.concatenate([ref[pl.ds(first + j, TOKEN_TILE, stride=nj), :] for j in range(nj)], axis=1))


def _slab_store(ref, c, re, im):
    nj = ref.shape[0] // (4 * TOKEN_TILE)
    first = c * TOKEN_TILE * nj
    val = _pack_pair(re, im)
    for j in range(nj):
        ref[pl.ds(first + j, TOKEN_TILE, stride=nj), :] = val[:, j * 128:(j + 1) * 128]


def _s5_disc_math(lr, li, ldt, br, bi):
    dt = jnp.exp(ldt)
    mag = jnp.exp(lr * dt)
    ar = mag * jnp.cos(li * dt)
    ai = mag * jnp.sin(li * dt)
    den = lr * lr + li * li
    kr = ((ar - 1.0) * lr + ai * li) / den
    ki = (ai * lr - (ar - 1.0) * li) / den
    bbr = kr[None] * br - ki[None] * bi
    bbi = kr[None] * bi + ki[None] * br
    return ar, ai, bbr, bbi


def s5_disc_fwd(lr, li, ldt, br_t, bi_t, c_re, c_im, name):
    ni, ng, npp = br_t.shape
    gc = ng // 4

    def body(lr_ref, li_ref, ldt_ref, br_ref, bi_ref, cre_ref, cim_ref,
             ar_ref, ai_ref, bdre_ref, bdim_ref, cdre_ref, cdim_ref, bbr_sc, bbi_sc):
        ar, ai, bbr, bbi = _s5_disc_math(lr_ref[...], li_ref[...], ldt_ref[...], br_ref[...], bi_ref[...])
        ar_ref[...] = ar
        ai_ref[...] = ai
        bbr_sc[...] = bbr
        bbi_sc[...] = bbi
        for ref in (bdre_ref, bdim_ref, cdre_ref, cdim_ref):
            ref[...] = jnp.zeros_like(ref)
        for k in range(4):
            for j in range(gc):
                g = k * gc + j
                ins, states = pl.ds(j * ni, ni), pl.ds(j * npp, npp)
                bdre_ref[k, ins, states] = bbr_sc[:, g, :].astype(BF16)
                bdim_ref[k, ins, states] = bbi_sc[:, g, :].astype(BF16)
                cdre_ref[k, states, ins] = cre_ref[g].T.astype(BF16)
                cdim_ref[k, states, ins] = (-cim_ref[g]).T.astype(BF16)

    sd = jax.ShapeDtypeStruct
    return pl.pallas_call(
        body, name=name,
        out_shape=(sd(lr.shape, F32), sd(lr.shape, F32), sd((4, gc * ni, gc * npp), BF16), sd((4, gc * ni, gc * npp), BF16),
                   sd((4, gc * npp, gc * ni), BF16), sd((4, gc * npp, gc * ni), BF16)),
        scratch_shapes=[pltpu.VMEM(br_t.shape, F32), pltpu.VMEM(br_t.shape, F32)],
        compiler_params=pltpu.CompilerParams(vmem_limit_bytes=VMEM_LIMIT),
    )(lr, li, ldt, br_t, bi_t, c_re, c_im)


def s5_disc_bwd(lr, li, ldt, br_t, bi_t, dar, dai, dbbr, dbbi, name):
    def body(lr_ref, li_ref, ldt_ref, br_ref, bi_ref, dar_ref, dai_ref, dbbr_ref, dbbi_ref,
             dlr_ref, dli_ref, dldt_ref, dbr_ref, dbi_ref):
        _, vjp = jax.vjp(_s5_disc_math, lr_ref[...], li_ref[...], ldt_ref[...], br_ref[...], bi_ref[...])
        dlr, dli, dldt, dbr, dbi = vjp((dar_ref[...], dai_ref[...], dbbr_ref[...], dbbi_ref[...]))
        dlr_ref[...] = dlr
        dli_ref[...] = dli
        dldt_ref[...] = dldt
        dbr_ref[...] = dbr
        dbi_ref[...] = dbi

    sd = jax.ShapeDtypeStruct
    return pl.pallas_call(
        body, name=name,
        out_shape=(sd(lr.shape, F32), sd(lr.shape, F32), sd(ldt.shape, F32), sd(br_t.shape, F32), sd(br_t.shape, F32)),
    )(lr, li, ldt, br_t, bi_t, dar, dai, dbbr, dbbi)


def s5_fwd1(h, g, w_in, bdre, bdim, name):
    lp, d = h.shape
    tt = TOKEN_TILE
    cw, sw = bdre.shape[1], bdre.shape[2]

    def body(h_ref, g_ref, w_hbm, bdre_hbm, bdim_hbm, u_ref, z_ref, x_ref, w, bre, bim):
        @pl.when(pl.program_id(0) == 0)
        def _():
            pltpu.sync_copy(w_hbm, w)
            pltpu.sync_copy(bdre_hbm, bre)
            pltpu.sync_copy(bdim_hbm, bim)

        n = _rms_fwd(h_ref[...], g_ref[...])[0].astype(BF16)
        for c in range(4):
            cols = slice(c * cw, (c + 1) * cw)
            u = jnp.dot(n, w[c], preferred_element_type=F32)
            u_ref[:, cols] = u
            z_ref[:, cols] = jnp.dot(n, w[c + 4], preferred_element_type=F32)
            ub = u.astype(BF16)
            _slab_store(x_ref, c, jnp.dot(ub, bre[c], preferred_element_type=F32), jnp.dot(ub, bim[c], preferred_element_type=F32))

    sd = jax.ShapeDtypeStruct
    slab, slab_shape = _slab_spec(lp, tt, sw)
    row = pl.BlockSpec((tt, d), lambda i: (i, 0))
    return pl.pallas_call(
        body, name=name, grid=(lp // tt,),
        in_specs=[row, pl.BlockSpec((1, d), lambda i: (0, 0)), ANY, ANY, ANY],
        out_specs=[row, row, slab],
        out_shape=(sd((lp, d), F32), sd((lp, d), F32), sd(slab_shape, F32)),
        scratch_shapes=[pltpu.VMEM(w_in.shape, BF16), pltpu.VMEM(bdre.shape, BF16), pltpu.VMEM(bdim.shape, BF16)],
        compiler_params=_params(),
    )(h, g, w_in, bdre, bdim)


def s5_scan_fwd(x, ar, ai, name):
    nj = ar.shape[1]
    tt = TOKEN_TILE
    cpb = SCAN_CHUNKS
    nt = x.shape[0] // (4 * tt * nj)

    def body(x_ref, ar_ref, ai_ref, s_ref, st_r, st_i):
        i, cg = pl.program_id(0), pl.program_id(1)

        @pl.when(i == 0)
        def _():
            for q in range(cpb):
                st_r[cg * cpb + q] = jnp.zeros((nj, 128), F32)
                st_i[cg * cpb + q] = jnp.zeros((nj, 128), F32)

        a_r = [ar_ref[cg * cpb + q] for q in range(cpb)]
        a_i = [ai_ref[cg * cpb + q] for q in range(cpb)]

        def step(k, carry):
            carry = list(carry)
            for uu in range(SCAN_UNROLL):
                t = k * SCAN_UNROLL + uu
                for q in range(cpb):
                    s_r, s_i = carry[q]
                    rows = pl.ds(pl.multiple_of((q * tt + t) * nj, nj), nj)
                    x_r, x_i = _unpack_pair(x_ref[rows, :])
                    n_r = a_r[q] * s_r - a_i[q] * s_i + x_r
                    n_i = a_r[q] * s_i + a_i[q] * s_r + x_i
                    s_ref[rows, :] = _pack_pair(n_r, n_i)
                    carry[q] = (n_r, n_i)
            return tuple(carry)

        init = tuple((st_r[cg * cpb + q], st_i[cg * cpb + q]) for q in range(cpb))
        final = lax.fori_loop(0, tt // SCAN_UNROLL, step, init)
        for q in range(cpb):
            st_r[cg * cpb + q] = final[q][0]
            st_i[cg * cpb + q] = final[q][1]

    blk = pl.BlockSpec((cpb * tt * nj, 128), lambda i, cg: (i * (4 // cpb) + cg, 0))
    par = pl.BlockSpec((4, nj, 128), lambda i, cg: (0, 0, 0))
    sd = jax.ShapeDtypeStruct
    return pl.pallas_call(
        body, name=name, grid=(nt, 4 // cpb),
        in_specs=[blk, par, par], out_specs=blk,
        out_shape=sd(x.shape, F32),
        scratch_shapes=[pltpu.VMEM((4, nj, 128), F32), pltpu.VMEM((4, nj, 128), F32)],
        compiler_params=_params(ndim=2),
    )(x, ar, ai)


def s5_fwd3(s, u, z, h, cdre, cdim, w_glu, w_out, d_skip, b_glu, name):
    lp, d = h.shape
    tt = TOKEN_TILE
    sw, cw = cdre.shape[1], cdre.shape[2]

    def body(s_ref, u_ref, z_ref, h_ref, d_ref, bg_ref, cre_hbm, cim_hbm, wg_hbm, wo_hbm,
             o_ref, y_ref, q_ref, cre, cim, wg, wo):
        @pl.when(pl.program_id(0) == 0)
        def _():
            pltpu.sync_copy(cre_hbm, cre)
            pltpu.sync_copy(cim_hbm, cim)
            pltpu.sync_copy(wg_hbm, wg)
            pltpu.sync_copy(wo_hbm, wo)

        gys, q = [], None
        for c in range(4):
            cols = slice(c * cw, (c + 1) * cw)
            s_r, s_i = _slab_load(s_ref, c)
            y = _dot(s_r, cre[c]) + _dot(s_i, cim[c]) + d_ref[c] * u_ref[:, cols]
            y_ref[:, cols] = y
            gys.append(_gelu(y)[0])
            part = _dot(gys[c], wg[c])
            q = part if c == 0 else q + part
        q_ref[...] = q
        sig = _sigmoid(q + bg_ref[...])
        zz = z_ref[...]
        sz = zz * _sigmoid(zz)
        o = h_ref[...]
        for k in range(4):
            cols = slice(k * cw, (k + 1) * cw)
            o = o + _dot(gys[k] * sig[:, cols] * sz[:, cols], wo[k])
        o_ref[...] = o

    row = pl.BlockSpec((tt, d), lambda i: (i, 0))
    slab, _ = _slab_spec(lp, tt, sw)
    sd = jax.ShapeDtypeStruct((lp, d), F32)
    return pl.pallas_call(
        body, name=name, grid=(lp // tt,),
        in_specs=[slab, row, row, row, pl.BlockSpec((4, 1, cw), lambda i: (0, 0, 0)), pl.BlockSpec((1, d), lambda i: (0, 0)),
                  ANY, ANY, ANY, ANY],
        out_specs=[row, row, row],
        out_shape=(sd, sd, sd),
        scratch_shapes=[pltpu.VMEM(cdre.shape, BF16), pltpu.VMEM(cdim.shape, BF16), pltpu.VMEM(w_glu.shape, BF16),
                        pltpu.VMEM(w_out.shape, BF16)],
        compiler_params=_params(),
    )(s, u, z, h, d_skip, b_glu, cdre, cdim, w_glu, w_out)


def s5_bwd3a(dh, y, q, z, w_glu, w_out, b_glu, name):
    lp, d = dh.shape
    tt = TOKEN_TILE
    nt = lp // tt
    cw = w_glu.shape[1]

    def body(dh_ref, y_ref, q_ref, z_ref, bg_ref, wg_hbm, wo_hbm, dy_ref, dp_ref, dwo_hbm, dwg_hbm, dbg_hbm,
             wg, wo, dwo, dwg, dbg):
        i = pl.program_id(0)

        @pl.when(i == 0)
        def _():
            pltpu.sync_copy(wg_hbm, wg)
            pltpu.sync_copy(wo_hbm, wo)
            dwo[...] = jnp.zeros_like(dwo)
            dwg[...] = jnp.zeros_like(dwg)
            dbg[...] = jnp.zeros_like(dbg)

        sig = _sigmoid(q_ref[...] + bg_ref[...])
        sz, dsz = _silu_and_grad(z_ref[...])
        dhv = dh_ref[...]
        yv = y_ref[...]
        gy, t = _gelu(yv)
        dq_parts, dgy_parts = [], []
        for k in range(4):
            cols = slice(k * cw, (k + 1) * cw)
            gy_k, sig_k, sz_k = gy[:, cols], sig[:, cols], sz[:, cols]
            y2 = gy_k * sig_k
            dy3 = _dot_nt(dhv, wo[k])
            dwo[k] += _dot_tn(y2 * sz_k, dhv)
            dy2 = dy3 * sz_k
            dp_ref[0, :, cols] = (dy3 * y2 * dsz[:, cols]).astype(BF16)
            dq_parts.append(dy2 * gy_k * sig_k * (1.0 - sig_k))
            dgy_parts.append(dy2 * sig_k)
        dq = jnp.concatenate(dq_parts, axis=1)
        dbg[...] += jnp.sum(dq, axis=0, keepdims=True)
        dgelu = _gelu_grad(yv, t)
        for k in range(4):
            cols = slice(k * cw, (k + 1) * cw)
            dwg[k] += _dot_tn(gy[:, cols], dq)
            dy_ref[:, cols] = (dgy_parts[k] + _dot_nt(dq, wg[k])) * dgelu[:, cols]

        @pl.when(i == nt - 1)
        def _():
            pltpu.sync_copy(dwo, dwo_hbm)
            pltpu.sync_copy(dwg, dwg_hbm)
            pltpu.sync_copy(dbg, dbg_hbm)

    row = pl.BlockSpec((tt, d), lambda i: (i, 0))
    sd = jax.ShapeDtypeStruct
    return pl.pallas_call(
        body, name=name, grid=(nt,),
        in_specs=[row, row, row, row, pl.BlockSpec((1, d), lambda i: (0, 0)), ANY, ANY],
        out_specs=[row, pl.BlockSpec((1, tt, d), lambda i: (1, i, 0)), ANY, ANY, ANY],
        out_shape=(sd((lp, d), F32), sd((2, lp, d), BF16), sd(w_out.shape, F32), sd(w_glu.shape, F32), sd((1, d), F32)),
        scratch_shapes=[pltpu.VMEM(w_glu.shape, BF16), pltpu.VMEM(w_out.shape, BF16),
                        pltpu.VMEM(w_out.shape, F32), pltpu.VMEM(w_glu.shape, F32), pltpu.VMEM((1, d), F32)],
        compiler_params=_params(),
    )(dh, y, q, z, b_glu, w_glu, w_out)


def s5_bwd3b(dy, s, u, cdre, cdim, d_skip, name):
    lp, d = dy.shape
    tt = TOKEN_TILE
    nt = lp // tt
    sw, cw = cdre.shape[1], cdre.shape[2]
    gc = cw // S5_GROUP

    def body(dy_ref, s_ref, u_ref, d_ref, cre_hbm, cim_hbm,
             ds_ref, dus_ref, dcre_ref, dcim_ref, dd_hbm, cre, cim, dcre, dcim, dd):
        i = pl.program_id(0)

        @pl.when(i == 0)
        def _():
            pltpu.sync_copy(cre_hbm, cre)
            pltpu.sync_copy(cim_hbm, cim)
            dcre[...] = jnp.zeros_like(dcre)
            dcim[...] = jnp.zeros_like(dcim)
            dd[...] = jnp.zeros_like(dd)

        for c in range(4):
            chunk = slice(c * cw, (c + 1) * cw)
            dyv = dy_ref[:, chunk]
            dd[c] += jnp.sum(dyv * u_ref[:, chunk], axis=0, keepdims=True)
            dus_ref[:, chunk] = dyv * d_ref[c]
            _slab_store(ds_ref, c, _dot_nt(dyv, cre[c]), _dot_nt(dyv, cim[c]))
            s_r, s_i = _slab_load(s_ref, c)
            dcre[c] += _dot_tn(s_r, dyv)
            dcim[c] += _dot_tn(s_i, dyv)

        @pl.when(i == nt - 1)
        def _():
            for k in range(4):
                for j in range(gc):
                    rows, cols = pl.ds(j * S5_STATE, S5_STATE), pl.ds(j * S5_GROUP, S5_GROUP)
                    dcre_ref[k, j] = dcre[k, rows, cols].T
                    dcim_ref[k, j] = dcim[k, rows, cols].T
            pltpu.sync_copy(dd, dd_hbm)

    sd = jax.ShapeDtypeStruct
    row = pl.BlockSpec((tt, d), lambda i: (i, 0))
    slab, slab_shape = _slab_spec(lp, tt, sw)
    diag = pl.BlockSpec((4, gc, S5_GROUP, S5_STATE), lambda i: (0, 0, 0, 0))
    return pl.pallas_call(
        body, name=name, grid=(nt,),
        in_specs=[row, slab, row, pl.BlockSpec((4, 1, cw), lambda i: (0, 0, 0)), ANY, ANY],
        out_specs=[slab, row, diag, diag, ANY],
        out_shape=(sd(slab_shape, F32), sd((lp, d), F32),
                   sd((4, gc, S5_GROUP, S5_STATE), F32), sd((4, gc, S5_GROUP, S5_STATE), F32), sd((4, 1, cw), F32)),
        scratch_shapes=[pltpu.VMEM(cdre.shape, BF16), pltpu.VMEM(cdim.shape, BF16),
                        pltpu.VMEM(cdre.shape, F32), pltpu.VMEM(cdim.shape, F32), pltpu.VMEM((4, 1, cw), F32)],
        compiler_params=_params(),
    )(dy, s, u, d_skip, cdre, cdim)


def s5_scan_bwd(g, s, ar, ai, name):
    nj = ar.shape[1]
    tt = TOKEN_TILE
    cpb = SCAN_CHUNKS
    nt = g.shape[0] // (4 * tt * nj)

    def body(g_ref, s_ref, ar_ref, ai_ref, lam_ref, dar_ref, dai_ref, st_r, st_i, acc_r, acc_i):
        i, cg = pl.program_id(0), pl.program_id(1)

        @pl.when((i == 0) & (cg == 0))
        def _():
            for ref in (st_r, st_i, acc_r, acc_i):
                ref[...] = jnp.zeros_like(ref)

        a_r = [ar_ref[cg * cpb + q] for q in range(cpb)]
        a_i = [ai_ref[cg * cpb + q] for q in range(cpb)]

        def slab(q, t):
            return pl.ds(pl.multiple_of((q * tt + t) * nj, nj), nj)

        def adjoint(q, t, l_r, l_i):
            rows = slab(q, t)
            g_r, g_i = _unpack_pair(g_ref[rows, :])
            n_r = g_r + a_r[q] * l_r + a_i[q] * l_i
            n_i = g_i + a_r[q] * l_i - a_i[q] * l_r
            lam_ref[rows, :] = _pack_pair(n_r, n_i)
            return n_r, n_i

        def pair(q, t, l_r, l_i, d_r, d_i):
            p_r, p_i = _unpack_pair(s_ref[slab(q, t), :])
            return d_r + l_r * p_r + l_i * p_i, d_i + l_i * p_r - l_r * p_i

        def step(k, carry):
            carry = list(carry)
            for uu in range(SCAN_UNROLL):
                t = tt - 1 - (k * SCAN_UNROLL + uu)
                for q in range(cpb):
                    l_r, l_i, d_r, d_i = carry[q]
                    d_r, d_i = pair(q, t, l_r, l_i, d_r, d_i)
                    l_r, l_i = adjoint(q, t, l_r, l_i)
                    carry[q] = (l_r, l_i, d_r, d_i)
            return tuple(carry)

        init = tuple((st_r[cg * cpb + q], st_i[cg * cpb + q], acc_r[cg * cpb + q], acc_i[cg * cpb + q]) for q in range(cpb))
        final = lax.fori_loop(0, tt // SCAN_UNROLL, step, init)
        for q in range(cpb):
            ch = cg * cpb + q
            l_r, l_i, d_r, d_i = final[q]
            st_r[ch] = l_r
            st_i[ch] = l_i
            acc_r[ch] = d_r
            acc_i[ch] = d_i
            dar_ref[ch] = d_r
            dai_ref[ch] = d_i

    blk = pl.BlockSpec((cpb * tt * nj, 128), lambda i, cg: ((nt - 1 - i) * (4 // cpb) + cg, 0))
    par = pl.BlockSpec((4, nj, 128), lambda i, cg: (0, 0, 0))
    sd = jax.ShapeDtypeStruct
    return pl.pallas_call(
        body, name=name, grid=(nt, 4 // cpb),
        in_specs=[blk, blk, par, par], out_specs=[blk, par, par],
        out_shape=(sd(g.shape, F32), sd((4, nj, 128), F32), sd((4, nj, 128), F32)),
        scratch_shapes=[pltpu.VMEM((4, nj, 128), F32)] * 4,
        compiler_params=_params(ndim=2),
    )(g, s, ar, ai)


def s5_bwd1(lam, dus, u, dp, h, dh, g, w_in, bdre, bdim, name, pad_tiles=None):
    lp, d = h.shape
    tt = TOKEN_TILE
    nt = lp // tt
    cw, sw = bdre.shape[1], bdre.shape[2]
    gc = cw // S5_GROUP

    def body(lam_ref, dus_ref, u_ref, dpz_ref, h_ref, dh_ref, g_ref, w_hbm, bre_hbm, bim_hbm,
             dpu_ref, dho_ref, n_ref, dbre_ref, dbim_ref, dg_hbm, *rest):
        (gx_ref,), (w, bre, bim, dbre, dbim, dg) = (rest[:1], rest[1:]) if pad_tiles is not None else ((None,), rest)
        i = pl.program_id(0)

        @pl.when(i == 0)
        def _():
            pltpu.sync_copy(w_hbm, w)
            pltpu.sync_copy(bre_hbm, bre)
            pltpu.sync_copy(bim_hbm, bim)
            dbre[...] = jnp.zeros_like(dbre)
            dbim[...] = jnp.zeros_like(dbim)
            dg[...] = jnp.zeros_like(dg)

        dz = dpz_ref[0]
        dn = None
        for c in range(4):
            chunk = slice(c * cw, (c + 1) * cw)
            (l_r, l_i), uv = _slab_load(lam_ref, c), u_ref[:, chunk]
            du = dus_ref[:, chunk] + _dot_nt(l_r, bre[c]) + _dot_nt(l_i, bim[c])
            dbre[c] += _dot_tn(uv, l_r)
            dbim[c] += _dot_tn(uv, l_i)
            dpu_ref[0, :, chunk] = du.astype(BF16)
            part = _dot_nt(du, w[c]) + _dot_nt(dz[:, chunk], w[4 + c])
            dn = part if c == 0 else dn + part
        gv = g_ref[...]
        n, hh, rr = _rms_fwd(h_ref[...], gv)
        n_ref[...] = n.T.astype(BF16)
        dg[...] += jnp.sum(dn * hh, axis=0, keepdims=True)
        dh_in = dh_ref[...] + _rms_bwd(dn, hh, rr, gv)
        dho_ref[...] = dh_in
        if pad_tiles is not None:
            @pl.when(i >= pad_tiles)
            def _():
                gx_ref[...] = dh_in

        @pl.when(i == nt - 1)
        def _():
            for k in range(4):
                for j in range(gc):
                    rows, cols = pl.ds(j * S5_GROUP, S5_GROUP), pl.ds(j * S5_STATE, S5_STATE)
                    dbre_ref[k, j] = dbre[k, rows, cols]
                    dbim_ref[k, j] = dbim[k, rows, cols]
            pltpu.sync_copy(dg, dg_hbm)

    sd = jax.ShapeDtypeStruct
    row = pl.BlockSpec((tt, d), lambda i: (i, 0))
    slab, _ = _slab_spec(lp, tt, sw)
    diag = pl.BlockSpec((4, gc, S5_GROUP, S5_STATE), lambda i: (0, 0, 0, 0))
    extra_specs, extra_shapes = [], ()
    if pad_tiles is not None:
        extra_specs = [pl.BlockSpec((tt, d), lambda i: (jnp.maximum(i - pad_tiles, 0), 0))]
        extra_shapes = (sd((lp - pad_tiles * tt, d), F32),)
    return pl.pallas_call(
        body, name=name, grid=(nt,),
        in_specs=[slab, row, row, pl.BlockSpec((1, tt, d), lambda i: (1, i, 0)), row, row, pl.BlockSpec((1, d), lambda i: (0, 0)),
                  ANY, ANY, ANY],
        out_specs=[pl.BlockSpec((1, tt, d), lambda i: (0, i, 0)), row, pl.BlockSpec((d, tt), lambda i: (0, i)), diag, diag, ANY]
        + extra_specs,
        out_shape=(sd(dp.shape, BF16), sd((lp, d), F32), sd((d, lp), BF16),
                   sd((4, gc, S5_GROUP, S5_STATE), F32), sd((4, gc, S5_GROUP, S5_STATE), F32), sd((1, d), F32)) + extra_shapes,
        input_output_aliases={3: 0},
        scratch_shapes=[pltpu.VMEM(w_in.shape, BF16), pltpu.VMEM(bdre.shape, BF16), pltpu.VMEM(bdim.shape, BF16),
                        pltpu.VMEM(bdre.shape, F32), pltpu.VMEM(bdim.shape, F32), pltpu.VMEM((1, d), F32)],
        compiler_params=_params(),
    )(lam, dus, u, dp, h, dh, g, w_in, bdre, bdim)


def grad_w_in(n_t, dp, blk, name):
    d, lp = n_t.shape
    npart, _, width = dp.shape
    per = width // blk

    def body(n_ref, dp_ref, o_ref):
        o_ref[0] = jnp.dot(n_ref[...], dp_ref[0], preferred_element_type=F32).astype(o_ref.dtype)

    return pl.pallas_call(
        body, name=name, grid=(npart * per,),
        in_specs=[pl.BlockSpec((d, lp), lambda j: (0, 0), pipeline_mode=pl.Buffered(1)),
                  pl.BlockSpec((1, lp, blk), lambda j: (j // per, 0, j % per))],
        out_specs=pl.BlockSpec((1, d, blk), lambda j: (j, 0, 0)),
        out_shape=jax.ShapeDtypeStruct((npart * per, d, blk), BF16),
        compiler_params=_params(),
    )(n_t, dp)


def _conv_mix(cg, v, cw_ref, cb_ref, halo, c):
    hc = cg * v
    taps = cw_ref[c]
    conv = taps[2:3, :] * hc + taps[1:2, :] * _shift_down(hc, 1, halo) + taps[0:1, :] * _shift_down(hc, 2, halo) + cb_ref[c]
    return hc, conv


def conv_fwd(h, g, w_in, conv_w, conv_b, w_out, name):
    lp, d = h.shape
    tt = TOKEN_TILE
    nt = lp // tt
    nch, ce = w_out.shape[0], w_out.shape[1]

    def body(h_ref, g_ref, cw_ref, cb_ref, w_hbm, wo_hbm, o_ref, halo_ref, acts_ref, w, wo, halo):
        i = pl.program_id(0)

        @pl.when(i == 0)
        def _():
            pltpu.sync_copy(w_hbm, w)
            pltpu.sync_copy(wo_hbm, wo)
            halo[...] = jnp.zeros_like(halo)

        hv = h_ref[...]
        n = _rms_fwd(hv, g_ref[...])[0].astype(BF16)
        o = hv
        for c in range(nch):
            cols = slice(c * ce, (c + 1) * ce)
            bg, cg, v, z = [jnp.dot(n, w[p * nch + c], preferred_element_type=F32) for p in range(4)]
            for p, val in enumerate((bg, cg, v, z)):
                acts_ref[p, :, cols] = val.astype(BF16)
            hc, conv = _conv_mix(cg, v, cw_ref, cb_ref, halo[c], c)
            o = o + _dot(bg * conv * (z * _sigmoid(z)), wo[c])
            halo[c] = hc[tt - CONV_HALO:, :]
            halo_ref[0, c] = hc[tt - CONV_HALO:, :]
        o_ref[...] = o

    sd = jax.ShapeDtypeStruct
    return pl.pallas_call(
        body, name=name, grid=(nt,),
        in_specs=[pl.BlockSpec((tt, d), lambda i: (i, 0)), pl.BlockSpec((1, d), lambda i: (0, 0)),
                  pl.BlockSpec(conv_w.shape, lambda i: (0, 0, 0)), pl.BlockSpec(conv_b.shape, lambda i: (0, 0, 0)), ANY, ANY],
        out_specs=[pl.BlockSpec((tt, d), lambda i: (i, 0)), pl.BlockSpec((1, nch, CONV_HALO, ce), lambda i: (i, 0, 0, 0)),
                   pl.BlockSpec((4, tt, nch * ce), lambda i: (0, i, 0))],
        out_shape=(sd((lp, d), F32), sd((nt, nch, CONV_HALO, ce), F32), sd((4, lp, nch * ce), BF16)),
        scratch_shapes=[pltpu.VMEM(w_in.shape, BF16), pltpu.VMEM(w_out.shape, BF16), pltpu.VMEM((nch, CONV_HALO, ce), F32)],
        compiler_params=_params(),
    )(h, g, conv_w, conv_b, w_in, w_out)


def conv_bwd(h, dh, halos, acts, g, w_in, conv_w, conv_b, w_out, name):
    lp, d = h.shape
    tt = TOKEN_TILE
    nt = lp // tt
    nch, ce = w_out.shape[0], w_out.shape[1]

    def body(h_ref, dh_ref, halo_ref, acts_ref, g_ref, cw_ref, cb_ref, w_hbm, wo_hbm,
             dho_ref, n_ref, dp_ref, dwo_hbm, dcw_hbm, dcb_hbm, dg_hbm, w, wo, nxt, dwo, dcw, dcb, dg):
        i = pl.program_id(0)

        @pl.when(i == 0)
        def _():
            pltpu.sync_copy(w_hbm, w)
            pltpu.sync_copy(wo_hbm, wo)
            for ref in (nxt, dwo, dcw, dcb, dg):
                ref[...] = jnp.zeros_like(ref)

        gv = g_ref[...]
        nf, hh, rr = _rms_fwd(h_ref[...], gv)
        n_ref[...] = nf.T.astype(BF16)
        dhv = dh_ref[...]
        has_prev = (i < nt - 1).astype(F32)
        dn = jnp.zeros((tt, d), F32)
        for c in range(nch):
            halo = halo_ref[0, c] * has_prev
            cols = slice(c * ce, (c + 1) * ce)
            bg, cg, v, z = [acts_ref[p, :, cols].astype(F32) for p in range(4)]
            hc, conv = _conv_mix(cg, v, cw_ref, cb_ref, halo, c)
            sz, dsz = _silu_and_grad(z)
            y1 = bg * conv
            dy2 = _dot_nt(dhv, wo[c])
            dwo[c] += _dot_tn(y1 * sz, dhv)
            dy1 = dy2 * sz
            dz = dy2 * y1 * dsz
            dbg = dy1 * conv
            dconv = dy1 * bg
            dcb[c] += jnp.sum(dconv, axis=0, keepdims=True)
            up1 = _shift_up(dconv, 1, nxt[c])
            up2 = _shift_up(dconv, 2, nxt[c])
            nxt[c] = dconv[:CONV_HALO, :]
            taps = cw_ref[c]
            dhc = taps[2:3, :] * dconv + taps[1:2, :] * up1 + taps[0:1, :] * up2
            dcw[c, 0:1, :] += jnp.sum(hc * up2, axis=0, keepdims=True)
            dcw[c, 1:2, :] += jnp.sum(hc * up1, axis=0, keepdims=True)
            dcw[c, 2:3, :] += jnp.sum(hc * dconv, axis=0, keepdims=True)
            dcg = dhc * v
            dv = dhc * cg
            for p, val in enumerate((dbg, dcg, dv, dz)):
                dp_ref[p, :, cols] = val.astype(BF16)
                dn = dn + _dot_nt(val, w[p * nch + c])
        dg[...] += jnp.sum(dn * hh, axis=0, keepdims=True)
        dho_ref[...] = dhv + _rms_bwd(dn, hh, rr, gv)

        @pl.when(i == nt - 1)
        def _():
            pltpu.sync_copy(dwo, dwo_hbm)
            pltpu.sync_copy(dcw, dcw_hbm)
            pltpu.sync_copy(dcb, dcb_hbm)
            pltpu.sync_copy(dg, dg_hbm)

    rev = lambda i: (nt - 1 - i, 0)
    sd = jax.ShapeDtypeStruct
    return pl.pallas_call(
        body, name=name, grid=(nt,),
        in_specs=[pl.BlockSpec((tt, d), rev), pl.BlockSpec((tt, d), rev),
                  pl.BlockSpec((1, nch, CONV_HALO, ce), lambda i: (jnp.maximum(nt - 2 - i, 0), 0, 0, 0)),
                  pl.BlockSpec((4, tt, nch * ce), lambda i: (0, nt - 1 - i, 0)),
                  pl.BlockSpec((1, d), lambda i: (0, 0)),
                  pl.BlockSpec(conv_w.shape, lambda i: (0, 0, 0)), pl.BlockSpec(conv_b.shape, lambda i: (0, 0, 0)), ANY, ANY],
        out_specs=[pl.BlockSpec((tt, d), rev), pl.BlockSpec((d, tt), lambda i: (0, nt - 1 - i)),
                   pl.BlockSpec((4, tt, nch * ce), lambda i: (0, nt - 1 - i, 0)), ANY, ANY, ANY, ANY],
        out_shape=(sd((lp, d), F32), sd((d, lp), BF16), sd((4, lp, nch * ce), BF16),
                   sd(w_out.shape, F32), sd((nch, 8, ce), F32), sd((nch, 1, ce), F32), sd((1, d), F32)),
        scratch_shapes=[pltpu.VMEM(w_in.shape, BF16), pltpu.VMEM(w_out.shape, BF16), pltpu.VMEM((nch, CONV_HALO, ce), F32),
                        pltpu.VMEM(w_out.shape, F32), pltpu.VMEM((nch, 8, ce), F32), pltpu.VMEM((nch, 1, ce), F32),
                        pltpu.VMEM((1, d), F32)],
        compiler_params=_params(vmem=VMEM_LIMIT_LARGE),
    )(h, dh, halos, acts, g, conv_w, conv_b, w_in, w_out)


def _pool_mix(u, wg, bg_ref, sc_ref, halo, k, tile, tt, first_pos):
    ext = jnp.concatenate([halo, u], axis=0)
    win = _window_sums_back(ext)[k][POOL_HALO:, :]
    mixed = win * _pool_inv_count(tile, tt, first_pos, POOL_WINDOWS[k], u.shape[1]) - u
    outs = _dot(mixed, wg[k]) + bg_ref[k]
    return mixed, outs, outs * sc_ref[k]


def pool_fwd(h, g, w_in, w_grp, b_grp, scale, w_out, first_pos, name):
    lp, d = h.shape
    tt = TOKEN_TILE
    nt = lp // tt
    gw = w_grp.shape[1]

    def body(h_ref, g_ref, bg_ref, sc_ref, w_hbm, wg_hbm, wo_hbm, o_ref, halo_ref, acts_ref, w, wg, wo, halo):
        i = pl.program_id(0)

        @pl.when(i == 0)
        def _():
            pltpu.sync_copy(w_hbm, w)
            pltpu.sync_copy(wg_hbm, wg)
            pltpu.sync_copy(wo_hbm, wo)
            halo[...] = jnp.zeros_like(halo)

        hv = h_ref[...]
        n = _rms_fwd(hv, g_ref[...])[0].astype(BF16)
        o = hv
        for k in range(4):
            cols = slice(k * gw, (k + 1) * gw)
            u = jnp.dot(n, w[k], preferred_element_type=F32)
            z = jnp.dot(n, w[4 + k], preferred_element_type=F32)
            acts_ref[0, :, cols] = u.astype(BF16)
            acts_ref[1, :, cols] = z.astype(BF16)
            _, _, yp = _pool_mix(u, wg, bg_ref, sc_ref, halo[k], k, i, tt, first_pos)
            o = o + _dot(yp * (z * _sigmoid(z)), wo[k])
            halo[k] = u[tt - POOL_HALO:, :]
            halo_ref[0, k] = u[tt - POOL_HALO:, :]
        o_ref[...] = o

    sd = jax.ShapeDtypeStruct
    small = pl.BlockSpec((4, 1, gw), lambda i: (0, 0, 0))
    return pl.pallas_call(
        body, name=name, grid=(nt,),
        in_specs=[pl.BlockSpec((tt, d), lambda i: (i, 0)), pl.BlockSpec((1, d), lambda i: (0, 0)), small, small, ANY, ANY, ANY],
        out_specs=[pl.BlockSpec((tt, d), lambda i: (i, 0)), pl.BlockSpec((1, 4, POOL_HALO, gw), lambda i: (i, 0, 0, 0)),
                   pl.BlockSpec((2, tt, 4 * gw), lambda i: (0, i, 0))],
        out_shape=(sd((lp, d), F32), sd((nt, 4, POOL_HALO, gw), F32), sd((2, lp, 4 * gw), BF16)),
        scratch_shapes=[pltpu.VMEM(w_in.shape, BF16), pltpu.VMEM(w_grp.shape, BF16), pltpu.VMEM(w_out.shape, BF16),
                        pltpu.VMEM((4, POOL_HALO, gw), F32)],
        compiler_params=_params(),
    )(h, g, b_grp, scale, w_in, w_grp, w_out)


def pool_bwd(h, dh, halos, acts, g, w_in, w_grp, b_grp, scale, w_out, first_pos, name):
    lp, d = h.shape
    tt = TOKEN_TILE
    nt = lp // tt
    gw = w_grp.shape[1]

    def body(h_ref, dh_ref, halo_ref, acts_ref, g_ref, bg_ref, sc_ref, w_hbm, wg_hbm, wo_hbm,
             dho_ref, n_ref, dp_ref, dwo_hbm, dwg_hbm, dbg_hbm, dsc_hbm, dg_hbm,
             w, wg, wo, nxt, dwo, dwg, dbg, dsc, dg):
        i = pl.program_id(0)
        tile = nt - 1 - i

        @pl.when(i == 0)
        def _():
            pltpu.sync_copy(w_hbm, w)
            pltpu.sync_copy(wg_hbm, wg)
            pltpu.sync_copy(wo_hbm, wo)
            for ref in (nxt, dwo, dwg, dbg, dsc, dg):
                ref[...] = jnp.zeros_like(ref)

        gv = g_ref[...]
        nf, hh, rr = _rms_fwd(h_ref[...], gv)
        n_ref[...] = nf.T.astype(BF16)
        dhv = dh_ref[...]
        has_prev = (i < nt - 1).astype(F32)
        dn = jnp.zeros((tt, d), F32)
        for k in range(4):
            cols = slice(k * gw, (k + 1) * gw)
            u, z = acts_ref[0, :, cols].astype(F32), acts_ref[1, :, cols].astype(F32)
            mixed, outs, yp = _pool_mix(u, wg, bg_ref, sc_ref, halo_ref[0, k] * has_prev, k, tile, tt, first_pos)
            sz, dsz = _silu_and_grad(z)
            dy = _dot_nt(dhv, wo[k])
            dwo[k] += _dot_tn(yp * sz, dhv)
            dyp = dy * sz
            dz = dy * yp * dsz
            dsc[k] += jnp.sum(dyp * outs, axis=0, keepdims=True)
            douts = dyp * sc_ref[k]
            dbg[k] += jnp.sum(douts, axis=0, keepdims=True)
            dwg[k] += _dot_tn(mixed, douts)
            dmixed = _dot_nt(douts, wg[k])
            dm = dmixed * _pool_inv_count(tile, tt, first_pos, POOL_WINDOWS[k], gw)
            ext = jnp.concatenate([dm, nxt[k]], axis=0)
            du = _window_sums_fwd(ext)[k][:tt, :] - dmixed
            nxt[k] = dm[:POOL_HALO, :]
            dp_ref[0, :, cols] = du.astype(BF16)
            dp_ref[1, :, cols] = dz.astype(BF16)
            dn = dn + _dot_nt(du, w[k]) + _dot_nt(dz, w[4 + k])
        dg[...] += jnp.sum(dn * hh, axis=0, keepdims=True)
        dho_ref[...] = dhv + _rms_bwd(dn, hh, rr, gv)

        @pl.when(i == nt - 1)
        def _():
            pltpu.sync_copy(dwo, dwo_hbm)
            pltpu.sync_copy(dwg, dwg_hbm)
            pltpu.sync_copy(dbg, dbg_hbm)
            pltpu.sync_copy(dsc, dsc_hbm)
            pltpu.sync_copy(dg, dg_hbm)

    rev = lambda i: (nt - 1 - i, 0)
    sd = jax.ShapeDtypeStruct
    small = pl.BlockSpec((4, 1, gw), lambda i: (0, 0, 0))
    return pl.pallas_call(
        body, name=name, grid=(nt,),
        in_specs=[pl.BlockSpec((tt, d), rev), pl.BlockSpec((tt, d), rev),
                  pl.BlockSpec((1, 4, POOL_HALO, gw), lambda i: (jnp.maximum(nt - 2 - i, 0), 0, 0, 0)),
                  pl.BlockSpec((2, tt, 4 * gw), lambda i: (0, nt - 1 - i, 0)),
                  pl.BlockSpec((1, d), lambda i: (0, 0)), small, small, ANY, ANY, ANY],
        out_specs=[pl.BlockSpec((tt, d), rev), pl.BlockSpec((d, tt), lambda i: (0, nt - 1 - i)),
                   pl.BlockSpec((2, tt, 4 * gw), lambda i: (0, nt - 1 - i, 0)), ANY, ANY, ANY, ANY, ANY],
        out_shape=(sd((lp, d), F32), sd((d, lp), BF16), sd((2, lp, 4 * gw), BF16),
                   sd(w_out.shape, F32), sd(w_grp.shape, F32), sd((4, 1, gw), F32), sd((4, 1, gw), F32), sd((1, d), F32)),
        scratch_shapes=[pltpu.VMEM(w_in.shape, BF16), pltpu.VMEM(w_grp.shape, BF16), pltpu.VMEM(w_out.shape, BF16),
                        pltpu.VMEM((4, POOL_HALO, gw), F32), pltpu.VMEM(w_out.shape, F32), pltpu.VMEM(w_grp.shape, F32),
                        pltpu.VMEM((4, 1, gw), F32), pltpu.VMEM((4, 1, gw), F32), pltpu.VMEM((1, d), F32)],
        compiler_params=_params(),
    )(h, dh, halos, acts, g, b_grp, scale, w_in, w_grp, w_out)


def loss_head(h, target, g, pad_tiles, name):
    lp, d = h.shape
    tt = TOKEN_TILE
    nt = lp // tt

    def body(h_ref, t_ref, g_ref, dh_ref, dg_ref, loss_ref, acc):
        i = pl.program_id(0)

        @pl.when(i == 0)
        def _():
            acc[...] = jnp.zeros_like(acc)
            dg_ref[...] = jnp.zeros_like(dg_ref)

        @pl.when(i < pad_tiles)
        def _():
            dh_ref[...] = jnp.zeros_like(dh_ref)

        @pl.when(i >= pad_tiles)
        def _():
            gv = g_ref[...]
            n, hh, rr = _rms_fwd(h_ref[...], gv)
            err = n - t_ref[...]
            acc[...] += 0.5 * jnp.sum(jnp.mean(err * err, axis=-1, keepdims=True), axis=0, keepdims=True)
            dn = err * (1.0 / d)
            dg_ref[...] += jnp.sum(dn * hh, axis=0, keepdims=True)
            dh_ref[...] = _rms_bwd(dn, hh, rr, gv)

        loss_ref[...] = jnp.broadcast_to(acc[...], loss_ref.shape)

    sd = jax.ShapeDtypeStruct
    return pl.pallas_call(
        body, name=name, grid=(nt,),
        in_specs=[pl.BlockSpec((tt, d), lambda i: (i, 0)), pl.BlockSpec((tt, d), lambda i: (jnp.maximum(i - pad_tiles, 0), 0)),
                  pl.BlockSpec((1, d), lambda i: (0, 0))],
        out_specs=[pl.BlockSpec((tt, d), lambda i: (i, 0)), pl.BlockSpec((1, d), lambda i: (0, 0)),
                   pl.BlockSpec((8, 128), lambda i: (0, 0))],
        out_shape=(sd((lp, d), F32), sd((1, d), F32), sd((8, 128), F32)),
        scratch_shapes=[pltpu.VMEM((1, 1), F32)],
        compiler_params=_params(),
    )(h, target, g)


def _peers(x, y, c):
    out = []
    for k in range(1, N_DEV):
        px = 1 - x if k & 4 else x
        py = 1 - y if k & 2 else y
        pc = 1 - c if k & 1 else c
        out.append((k, (px, py, pc), 4 * px + 2 * py + pc))
    return out


def exchange_start(arrs, gather, after, name):
    n = len(arrs)
    me = 4 * lax.axis_index("x") + 2 * lax.axis_index("y") + lax.axis_index("c")
    lands = []
    for a in arrs:
        own = a[None] if gather else lax.dynamic_index_in_dim(a, me, 0, keepdims=True)
        lands.append(lax.dynamic_update_index_in_dim(lax.empty(((N_DEV,) + a.shape) if gather else a.shape, a.dtype), own, me, 0))

    def body(*refs):
        ins, land = refs[:n], refs[n:2 * n]
        send_sems, recv_sems, token = refs[2 * n + 1], refs[2 * n + 2], refs[4 * n + 3]
        x, y, c = lax.axis_index("x"), lax.axis_index("y"), lax.axis_index("c")
        me = 4 * x + 2 * y + c
        for k, pid, peer in _peers(x, y, c):
            for a in range(n):
                pltpu.make_async_remote_copy(
                    src_ref=ins[a] if gather else ins[a].at[peer], dst_ref=land[a].at[me],
                    send_sem=send_sems.at[a * (N_DEV - 1) + k - 1], recv_sem=recv_sems.at[a * (N_DEV - 1) + k - 1],
                    device_id=pid, device_id_type=pl.DeviceIdType.MESH).start()
        token[...] = jnp.zeros_like(token)

    hbm = pl.BlockSpec(memory_space=pltpu.HBM)
    sem = pl.BlockSpec(memory_space=pltpu.SEMAPHORE)
    sems = pltpu.SemaphoreType.DMA((n * (N_DEV - 1),))
    res = pl.pallas_call(
        body, name=name, in_specs=[hbm] * (2 * n) + [ANY],
        out_specs=[sem, sem] + [hbm] * (2 * n) + [pl.BlockSpec(memory_space=pltpu.VMEM)],
        out_shape=[sems, sems] + [pltpu.HBM(a.shape, a.dtype) for a in arrs] + [pltpu.HBM(l.shape, l.dtype) for l in lands]
        + [jax.ShapeDtypeStruct((8, 128), F32)],
        input_output_aliases={a: 2 + a for a in range(2 * n)},
        compiler_params=pltpu.CompilerParams(has_side_effects=pltpu.SideEffectType.DATAFLOW_SIDE_EFFECTING),
    )(*[pltpu.with_memory_space_constraint(a, pltpu.HBM) for a in list(arrs) + lands], after)
    return res[0], res[1], res[2:2 + n], res[2 + n:2 + 2 * n], res[-1]


def exchange_wait(started, gather, after, name):
    send_sems, recv_sems, srcs, lands, _ = started
    n = len(srcs)
    after = list(after) if isinstance(after, (list, tuple)) else [after]

    def body(*refs):
        ins, land = refs[:n], refs[n:2 * n]
        send_sems, recv_sems = refs[2 * n], refs[2 * n + 1]
        x, y, c = lax.axis_index("x"), lax.axis_index("y"), lax.axis_index("c")
        for k, pid, peer in _peers(x, y, c):
            for a in range(n):
                cp = pltpu.make_async_remote_copy(
                    src_ref=ins[a] if gather else ins[a].at[peer], dst_ref=land[a].at[peer],
                    send_sem=send_sems.at[a * (N_DEV - 1) + k - 1], recv_sem=recv_sems.at[a * (N_DEV - 1) + k - 1],
                    device_id=pid, device_id_type=pl.DeviceIdType.MESH)
                cp.wait_send()
                cp.wait_recv()

    hbm = pl.BlockSpec(memory_space=pltpu.HBM)
    sem = pl.BlockSpec(memory_space=pltpu.SEMAPHORE)
    res = pl.pallas_call(
        body, name=name, in_specs=[hbm] * (2 * n) + [sem, sem] + [ANY] * len(after),
        out_specs=[hbm] * (2 * n),
        out_shape=[pltpu.HBM(a.shape, a.dtype) for a in list(srcs) + list(lands)],
        input_output_aliases={a: a for a in range(2 * n)},
        compiler_params=pltpu.CompilerParams(has_side_effects=pltpu.SideEffectType.DATAFLOW_SIDE_EFFECTING),
    )(*srcs, *lands, send_sems, recv_sems, *after)
    return res[n:]


def _adamw(w, g, m, v):
    m = ADAM_B1 * m + (1.0 - ADAM_B1) * g
    v = ADAM_B2 * v + (1.0 - ADAM_B2) * (g * g)
    m_hat = m / (1.0 - ADAM_B1 ** ADAM_STEP)
    v_hat = v / (1.0 - ADAM_B2 ** ADAM_STEP)
    return -ADAM_LR * (m_hat / (jnp.sqrt(v_hat) + ADAM_EPS) + ADAM_WD * w), m, v


def _update_tile_rows(rows, cols):
    if rows * cols <= UPDATE_TILE_ELEMS:
        return rows
    return max(t for t in range(8, UPDATE_TILE_ELEMS // cols + 1, 8) if rows % t == 0)


def _sum_in_order(p_ref):
    g = p_ref[0].astype(F32)
    for j in range(1, p_ref.shape[0]):
        g = g + p_ref[j].astype(F32)
    return g


def sum_parts(parts, name):
    nparts, rows, cols = parts.shape
    tr = _update_tile_rows(rows, cols)

    def body(p_ref, g_ref):
        g_ref[...] = _sum_in_order(p_ref)

    return pl.pallas_call(
        body, name=name, grid=(rows // tr,),
        in_specs=[pl.BlockSpec((nparts, tr, cols), lambda i: (0, i, 0))],
        out_specs=pl.BlockSpec((tr, cols), lambda i: (i, 0)), out_shape=jax.ShapeDtypeStruct((rows, cols), F32),
        compiler_params=_params(),
    )(parts)


def sum_adamw(parts, w, m, v, name):
    rows, cols = w.shape
    nparts = parts.shape[0]
    tr = _update_tile_rows(rows, cols)

    def body(p_ref, w_ref, m_ref, v_ref, g_ref, d_ref, nm_ref, nv_ref):
        g = _sum_in_order(p_ref)
        delta, nm, nv = _adamw(w_ref[...], g, m_ref[...], v_ref[...])
        g_ref[...] = g
        d_ref[...] = delta
        nm_ref[...] = nm
        nv_ref[...] = nv

    blk = pl.BlockSpec((tr, cols), lambda i: (i, 0))
    sd = jax.ShapeDtypeStruct((rows, cols), F32)
    return pl.pallas_call(
        body, name=name, grid=(rows // tr,),
        in_specs=[pl.BlockSpec((nparts, tr, cols), lambda i: (0, i, 0)), blk, blk, blk],
        out_specs=[blk] * 4, out_shape=(sd,) * 4,
        compiler_params=_params(),
    )(parts, w, m, v)


def update_packed(g, w, m, v, pieces, name):
    rows_all = w.shape[0]

    def body(g_ref, w_ref, m_ref, v_ref, *outs):
        gv = g_ref[:rows_all, :]
        res = (gv,) + _adamw(w_ref[...], gv, m_ref[...], v_ref[...])
        for p, (row, rows, lanes) in enumerate(pieces):
            for k in range(4):
                outs[4 * p + k][...] = res[k][row:row + rows, :lanes]

    shapes = [jax.ShapeDtypeStruct((rows, lanes), F32) for _, rows, lanes in pieces for _ in range(4)]
    return pl.pallas_call(body, name=name, out_shape=shapes,
                          compiler_params=pltpu.CompilerParams(vmem_limit_bytes=VMEM_LIMIT))(g, w, m, v)


def update_natural(groups, name):
    n = len(groups)
    steps = 8

    def body(*refs):
        ins, outs = refs[:4 * n], refs[4 * n:]
        for j in range(n):
            g_ref, w_ref, m_ref, v_ref = ins[4 * j:4 * j + 4]
            gv = g_ref[...]
            res = (gv,) + _adamw(w_ref[...], gv, m_ref[...], v_ref[...])
            for k in range(4):
                outs[4 * j + k][...] = res[k]

    specs, shapes = [], []
    for g, w, m, v in groups:
        rows, cols = w.shape
        specs += [pl.BlockSpec((rows // steps, cols), lambda i: (i, 0))] * 4
        shapes += [jax.ShapeDtypeStruct((rows, cols), F32)] * 4
    return pl.pallas_call(body, name=name, grid=(steps,), in_specs=specs, out_specs=specs, out_shape=shapes,
                          compiler_params=_params())(*[a for grp in groups for a in grp])


S5_NAMES = ("w_in", "lam_re", "lam_im", "log_dt", "b_re", "b_im", "c_re", "c_im", "d_skip", "w_glu", "b_glu", "w_out")
CONV_NAMES = ("w_in", "conv_w", "conv_b", "w_out")
POOL_NAMES = ("w_in", "w_grp", "b_grp", "scale", "w_out")
LAYER_KINDS = ("s5", "conv", "pool", "s5")
LAYER_NAMES = {"s5": S5_NAMES, "conv": CONV_NAMES, "pool": POOL_NAMES}
SHARDED = {"s5": ("w_in", "w_glu", "w_out"), "conv": ("w_in", "conv_w", "w_out"), "pool": ("w_in", "w_grp", "b_grp", "w_out")}
GATHER_F32 = ("conv_w", "b_grp")


def weight_names():
    names = ["meta_tokens"]
    for i, kind in enumerate(LAYER_KINDS):
        names.append("norm%d_g" % i)
        names += ["l%d_%s" % (i, n) for n in LAYER_NAMES[kind]]
    names.append("final_g")
    return names


def sharded_names():
    return ["meta_tokens"] + ["l%d_%s" % (i, n) for i, kind in enumerate(LAYER_KINDS) for n in SHARDED[kind]]


def _block_diag_in_grad(blocks):
    _, gc, i, p = blocks.shape
    return jnp.transpose(blocks, (2, 0, 1, 3)).reshape(i, 4 * gc, p)


def _block_diag_out_grad(blocks):
    _, gc, i, p = blocks.shape
    return blocks.reshape(4 * gc, i, p)


def _to_owner_blocks(a, axis):
    shape = a.shape[:axis] + (N_DEV, a.shape[axis] // N_DEV) + a.shape[axis + 1:]
    return jnp.moveaxis(a.reshape(shape), axis, 0)


def _from_owner_blocks(a, axis):
    a = jnp.moveaxis(a, 0, axis)
    return a.reshape(a.shape[:axis] + (a.shape[axis] * a.shape[axis + 1],) + a.shape[axis + 2:])


def _step(x, target, weights, moments_m, moments_v):
    seq, d = x.shape[1], x.shape[2]
    n_meta = weights["meta_tokens"].shape[0]
    tt = TOKEN_TILE
    pad_tiles = -(-n_meta // tt)
    p0 = pad_tiles * tt
    lp = p0 + seq
    first_pos = p0 - n_meta
    gc = d // 4 // S5_GROUP
    cw = d // 4

    big_names = [n for n in sharded_names() if n != "meta_tokens" and n.split("_", 1)[1] not in GATHER_F32]
    small_names = [n for n in sharded_names() if n not in big_names]
    layer_big = [[n for n in big_names if n.startswith("l%d_" % i)] for i in range(len(LAYER_KINDS))]
    layer_big[0] = small_names + layer_big[0]
    gather_started = []
    after = jnp.zeros((8, 128), F32)
    for i, names in enumerate(layer_big):
        gather_started.append(exchange_start([weights[n] if n in small_names else weights[n].astype(BF16) for n in names], True,
                                             after, "gather_start_l%d" % i))
        after = gather_started[-1][4]

    def vec(name):
        return weights[name].reshape(1, -1)

    s5_prep = {}
    for i, kind in enumerate(LAYER_KINDS):
        if kind == "s5":
            p = "l%d_" % i
            lr, li = weights[p + "lam_re"], weights[p + "lam_im"] + after[0, 0]
            ldt = weights[p + "log_dt"].reshape(-1, 1)
            br_t = jnp.transpose(weights[p + "b_re"], (2, 0, 1))
            bi_t = jnp.transpose(weights[p + "b_im"], (2, 0, 1))
            ar, ai, bdre, bdim, cdre, cdim = s5_disc_fwd(lr, li, ldt, br_t, bi_t, weights[p + "c_re"], weights[p + "c_im"],
                                                         p + "disc_fwd")
            s5_prep[i] = dict(
                disc=(lr, li, ldt, br_t, bi_t), ar=ar.reshape(4, -1, 128), ai=ai.reshape(4, -1, 128),
                bdre=bdre, bdim=bdim, cdre=cdre, cdim=cdim,
                d_skip=weights[p + "d_skip"].reshape(4, 1, cw), b_glu=vec(p + "b_glu"))
    h = jnp.concatenate([jnp.zeros((p0, d), F32), x[0] + after[0, 0]], axis=0)

    prepared = [h] + [s5_prep[i][k] for i in s5_prep for k in ("bdre", "bdim", "cdre", "cdim")]
    gathered = dict(zip(layer_big[0], exchange_wait(gather_started[0], True, prepared, "gather_wait_l0")))
    h = lax.dynamic_update_slice(h, _from_owner_blocks(gathered["meta_tokens"], 1), (first_pos, 0))

    full = {}

    def layer_weights(i, kind, after):
        p = "l%d_" % i
        if i > 0:
            gathered.update(zip(layer_big[i], exchange_wait(gather_started[i], True, after, "gather_wait_l%d" % i)))
        w_in = gathered[p + "w_in"]
        if kind == "s5":
            full[i] = dict(s5_prep[i], w_in=w_in, w_glu=gathered[p + "w_glu"].reshape(4, cw, d),
                           w_out=gathered[p + "w_out"].reshape(4, cw, d))
        elif kind == "conv":
            ce = w_in.shape[2]
            nch = 2
            conv_w = _from_owner_blocks(gathered[p + "conv_w"], 1)
            full[i] = dict(
                w_in=w_in, conv_w=jnp.transpose(conv_w.reshape(CONV_K, nch, ce), (1, 0, 2)),
                conv_b=weights[p + "conv_b"].reshape(nch, 1, ce), w_out=gathered[p + "w_out"].reshape(nch, ce, d))
        else:
            gw = w_in.shape[2]
            full[i] = dict(
                w_in=w_in, w_grp=_from_owner_blocks(gathered[p + "w_grp"], 1),
                b_grp=_from_owner_blocks(gathered[p + "b_grp"], 1).reshape(4, 1, gw),
                scale=weights[p + "scale"].reshape(4, 1, gw), w_out=gathered[p + "w_out"].reshape(4, gw, d))
        return full[i]

    saved = {}
    for i, kind in enumerate(LAYER_KINDS):
        p, f, g = "l%d_" % i, layer_weights(i, kind, h), vec("norm%d_g" % i)
        if kind == "s5":
            u, z, xs = s5_fwd1(h, g, f["w_in"], f["bdre"], f["bdim"], p + "fwd_in")
            s = s5_scan_fwd(xs, f["ar"], f["ai"], p + "scan_fwd")
            h_in = h
            h, y, q = s5_fwd3(s, u, z, h, f["cdre"], f["cdim"], f["w_glu"], f["w_out"], f["d_skip"], f["b_glu"], p + "fwd_out")
            saved[i] = (h_in, u, z, s, y, q)
        elif kind == "conv":
            h_new, halos, acts = conv_fwd(h, g, f["w_in"], f["conv_w"], f["conv_b"], f["w_out"], p + "fwd")
            saved[i] = (h, halos, acts)
            h = h_new
        else:
            h_new, halos, acts = pool_fwd(h, g, f["w_in"], f["w_grp"], f["b_grp"], f["scale"], f["w_out"], first_pos, p + "fwd")
            saved[i] = (h, halos, acts)
            h = h_new

    dh, dg_final, loss_tile = loss_head(h, target[0], vec("final_g"), pad_tiles, "loss_head")
    loss = lax.psum(loss_tile[0, 0], ("x", "y", "c"))

    grads = {"final_g": dg_final}
    names = weight_names()
    sh_names = sharded_names()
    replicated = [n for n in names if n not in sh_names]
    vectors = [n for n in replicated if weights[n].ndim == 1]
    matrices = [n for n in replicated if weights[n].ndim > 1]
    rep_names = vectors + matrices

    def owner_blocks(a):
        return a.reshape(N_DEV, -1, a.shape[-1]).astype(BF16)

    def as2d(a):
        return a.reshape(-1, a.shape[-1])

    def pack(tree, which=None):
        flat = [jnp.pad(tree[n].reshape(-1), (0, -tree[n].size % PACK_ALIGN)) for n in (which or rep_names)]
        flat = jnp.concatenate(flat)
        if which is None:
            flat = jnp.pad(flat, (0, -flat.size % (PACK_ROWS * 128)))
        return flat.reshape(-1, 128)

    layer_sharded, scatter_started = {}, {}
    ordered = jnp.zeros((), F32)
    for i in reversed(range(len(LAYER_KINDS))):
        kind = LAYER_KINDS[i]
        p, f, g = "l%d_" % i, full[i], vec("norm%d_g" % i) + ordered
        if kind == "s5":
            h_in, u, z, s, y, q = saved[i]
            dy, dp, dwo, dwg, dbg = s5_bwd3a(dh, y, q, z, f["w_glu"], f["w_out"], f["b_glu"] + ordered, p + "bwd_out")
            d_skip = f["d_skip"]
            if i == 0:
                early_names = [p + "w_glu", p + "w_out"]
                scatter_started["early"] = exchange_start([owner_blocks(dwg), owner_blocks(dwo)], False, dy,
                                                          "scatter_start_l0_early")
                d_skip = d_skip + scatter_started["early"][4][0, 0]
            ds, dus, dcre, dcim, dd = s5_bwd3b(dy, s, u, f["cdre"], f["cdim"], d_skip, p + "bwd_read")
            lam, dar, dai = s5_scan_bwd(ds, s, f["ar"], f["ai"], p + "scan_bwd")
            res = s5_bwd1(lam, dus, u, dp, h_in, dh, g, f["w_in"], f["bdre"], f["bdim"], p + "bwd_in",
                          pad_tiles=pad_tiles if i == 0 else None)
            dp, dh, n, dbre, dbim, dg = res[:6]
            if i == 0:
                grad_x = res[6][None]
            dw_in = grad_w_in(n, dp, f["w_in"].shape[2], p + "grad_w_in")
            grads.update({p + "w_in": dw_in, p + "w_glu": dwg.reshape(N_DEV, -1, d), p + "w_out": dwo.reshape(N_DEV, -1, d),
                          p + "d_skip": dd, p + "b_glu": dbg})

            def replicated_grads(p=p, f=f, dar=dar, dai=dai, dbre=dbre, dbim=dbim, dcre=dcre, dcim=dcim, token=None):
                lr, li, ldt, br_t, bi_t = f["disc"]
                dlr, dli, dldt, dbr_t, dbi_t = s5_disc_bwd(
                    lr, li, ldt, br_t, bi_t, dar.reshape(lr.shape) + token, dai.reshape(lr.shape),
                    _block_diag_in_grad(dbre), _block_diag_in_grad(dbim), p + "disc_bwd")
                grads.update({
                    p + "lam_re": dlr, p + "lam_im": dli, p + "log_dt": dldt,
                    p + "b_re": jnp.transpose(dbr_t, (1, 2, 0)), p + "b_im": jnp.transpose(dbi_t, (1, 2, 0)),
                    p + "c_re": _block_diag_out_grad(dcre), p + "c_im": -_block_diag_out_grad(dcim)})
        elif kind == "conv":
            replicated_grads = None
            h_in, halos, acts = saved[i]
            dh, n, dp, dwo, dcw, dcb, dg = conv_bwd(h_in, dh, halos, acts, g, f["w_in"], f["conv_w"], f["conv_b"], f["w_out"], p + "bwd")
            dw_in = grad_w_in(n, dp, f["w_in"].shape[2], p + "grad_w_in")
            dconv_w = jnp.transpose(dcw[:, :CONV_K, :], (1, 0, 2)).reshape(CONV_K, -1)
            grads.update({p + "w_in": dw_in, p + "conv_w": _to_owner_blocks(dconv_w, 1), p + "conv_b": dcb,
                          p + "w_out": dwo.reshape(N_DEV, -1, d)})
        else:
            replicated_grads = None
            h_in, halos, acts = saved[i]
            dh, n, dp, dwo, dwgrp, dbgrp, dsc, dg = pool_bwd(h_in, dh, halos, acts, g, f["w_in"], f["w_grp"], f["b_grp"], f["scale"],
                                                             f["w_out"], first_pos, p + "bwd")
            dw_in = grad_w_in(n, dp, f["w_in"].shape[2], p + "grad_w_in")
            grads.update({p + "w_in": dw_in, p + "w_grp": _to_owner_blocks(dwgrp, 1),
                          p + "b_grp": _to_owner_blocks(dbgrp.reshape(4, -1), 1), p + "scale": dsc,
                          p + "w_out": dwo.reshape(N_DEV, -1, d)})
        grads["norm%d_g" % i] = dg
        layer_sharded[i] = ["l%d_%s" % (i, n) for n in SHARDED[kind]]
        if i > 0:
            scatter_started[i] = exchange_start([owner_blocks(grads[n]) for n in layer_sharded[i]], False, dh,
                                                "scatter_start_l%d" % i)
            ordered = scatter_started[i][4][0, 0]
        if replicated_grads is not None:
            replicated_grads(token=ordered)
    grads["meta_tokens"] = _to_owner_blocks(dh[first_pos:p0], 1)
    last = len(LAYER_KINDS)
    layer_sharded[last] = ["meta_tokens", "replicated"]
    scatter_started[last] = exchange_start([owner_blocks(grads["meta_tokens"]), pack(grads).reshape(N_DEV, -1, 128)], False,
                                           dh, "scatter_start_replicated")
    layer_sharded["early"] = early_names
    layer_sharded[0] = [n for n in layer_sharded[0] if n not in early_names]

    out = {}
    received = {}
    after = [scatter_started[last][4]]
    for i in list(reversed(range(1, last))) + [last, "early", 0]:
        received.update(zip(layer_sharded[i], exchange_wait(scatter_started[i], False, after, "scatter_wait_%s" % i)))
        updated = []
        for n in layer_sharded[i]:
            if n != "replicated":
                res = sum_adamw(received[n], as2d(weights[n]), as2d(moments_m[n]), as2d(moments_v[n]), "update_" + n)
                out[n] = [r.reshape(weights[n].shape) for r in res]
                updated.append(out[n][0])
        after = updated or after
        if i == last:
            g_sum = sum_parts(received["replicated"], "sum_replicated")
            small_gather = exchange_start([g_sum], True, g_sum, "gather_small_grads_start")
            scatter_started[0] = exchange_start([owner_blocks(grads[n]) for n in layer_sharded[0]], False, small_gather[4],
                                                "scatter_start_l0")
            g_full = exchange_wait(small_gather, True, scatter_started[0][4], "gather_small_grads_wait")[0].reshape(-1, 128)
            offsets, offset = {}, 0
            for n in rep_names:
                offsets[n] = offset
                offset += weights[n].size + (-weights[n].size % PACK_ALIGN)
            pieces = [(offsets[n] // 128, max(weights[n].size // 128, 1), min(weights[n].size, 128)) for n in vectors]
            res = update_packed(g_full, pack(weights, vectors), pack(moments_m, vectors), pack(moments_v, vectors), pieces,
                                "update_replicated_vectors")
            for j, n in enumerate(vectors):
                out[n] = [r.reshape(weights[n].shape) for r in res[4 * j:4 * j + 4]]
            flat = g_full.reshape(-1)
            groups = [(flat[offsets[n]:offsets[n] + weights[n].size].reshape(as2d(weights[n]).shape), as2d(weights[n]),
                       as2d(moments_m[n]), as2d(moments_v[n])) for n in matrices]
            res = update_natural(groups, "update_replicated_matrices")
            for j, n in enumerate(matrices):
                out[n] = [r.reshape(weights[n].shape) for r in res[4 * j:4 * j + 4]]
            after = [out[n][k] for n in rep_names for k in range(4)]

    return (loss, grad_x) + tuple(out[n][k] for k in range(4) for n in names)


def kernel(x, *rest):
    names = weight_names()
    nw = len(names)
    weights = dict(zip(names, rest[:nw]))
    target = rest[nw]
    moments_m = dict(zip(names, rest[nw + 1:2 * nw + 1]))
    moments_v = dict(zip(names, rest[2 * nw + 1:3 * nw + 1]))
    return _step(x, target, weights, moments_m, moments_v)
```

```python
import math

import jax
import jax.numpy as jnp
from jax import lax
from jax.experimental import pallas as pl
from jax.experimental.pallas import tpu as pltpu

F32 = jnp.float32
BF16 = jnp.bfloat16
EPS = 1e-6
N_DEV = 8
TOKEN_TILE = 256
SCAN_CHUNKS = 4
SCAN_UNROLL = 4
S5_GROUP = 16
S5_STATE = 64
POOL_WINDOWS = (2, 4, 8, 16)
POOL_HALO = 16
CONV_K = 3
CONV_HALO = 8
CONV_SPLIT = 2
ADAM_LR = 0.001
ADAM_B1 = 0.9
ADAM_B2 = 0.999
ADAM_EPS = 1e-08
ADAM_WD = 0.01
ADAM_STEP = 10
GELU_C = math.sqrt(2.0 / math.pi)
GELU_A = 0.044715
UPDATE_TILE_ELEMS = 1 << 17
PACK_ROWS = 512
PACK_ALIGN = 8 * 128
HIGH_HALF = -65536
HALF_OF_LOW_HALF = 0x8000
VMEM_LIMIT = 56 << 20
VMEM_LIMIT_LARGE = 62 << 20

ANY = pl.BlockSpec(memory_space=pl.ANY)


def _params(vmem=VMEM_LIMIT, ndim=1):
    return pltpu.CompilerParams(vmem_limit_bytes=vmem, dimension_semantics=("arbitrary",) * ndim)


def _dot(a, b):
    return jnp.dot(a.astype(BF16), b.astype(BF16), preferred_element_type=F32)


def _dot_nt(a, b):
    return lax.dot_general(a.astype(BF16), b.astype(BF16), (((1,), (1,)), ((), ())), preferred_element_type=F32)


def _dot_tn(a, b):
    return lax.dot_general(a.astype(BF16), b.astype(BF16), (((0,), (0,)), ((), ())), preferred_element_type=F32)


def _rms_fwd(h, g):
    r = lax.rsqrt(jnp.mean(h * h, axis=-1, keepdims=True) + EPS)
    hh = h * r
    return hh * g, hh, r


def _rms_bwd(dn, hh, r, g):
    dhh = dn * g
    return r * (dhh - hh * jnp.mean(dhh * hh, axis=-1, keepdims=True))


def _sigmoid(x):
    return 1.0 / (1.0 + jnp.exp(-x))


def _silu_and_grad(z):
    s = _sigmoid(z)
    return z * s, s * (1.0 + z * (1.0 - s))


def _gelu(y):
    t = jnp.tanh(GELU_C * (y + GELU_A * y * y * y))
    return 0.5 * y * (1.0 + t), t


def _gelu_grad(y, t):
    return 0.5 * (1.0 + t) + 0.5 * y * (1.0 - t * t) * GELU_C * (1.0 + 3.0 * GELU_A * y * y)


def _rows(shape):
    return lax.broadcasted_iota(jnp.int32, shape, 0)


def _shift_down(x, k, halo):
    y = pltpu.roll(x, k, 0)
    rows = _rows(x.shape)
    for j in range(k):
        y = jnp.where(rows == j, halo[halo.shape[0] - k + j:halo.shape[0] - k + j + 1, :], y)
    return y


def _shift_up(x, k, halo):
    n = x.shape[0]
    y = pltpu.roll(x, n - k, 0)
    rows = _rows(x.shape)
    for j in range(k):
        y = jnp.where(rows == n - k + j, halo[j:j + 1, :], y)
    return y


def _window_sums_back(ext):
    out = []
    s = ext
    for k in (1, 2, 4, 8):
        s = s + pltpu.roll(s, k, 0)
        out.append(s)
    return out


def _window_sums_fwd(ext):
    n = ext.shape[0]
    out = []
    s = ext
    for k in (1, 2, 4, 8):
        s = s + pltpu.roll(s, n - k, 0)
        out.append(s)
    return out


def _pool_inv_count(tile, tt, first_pos, w, width):
    pos = _rows((tt, width)) + (tile * tt - first_pos + 1)
    return 1.0 / jnp.clip(pos, 1, w).astype(F32)


def _slab_spec(lp, tt, sw):
    nj = sw // 128
    return pl.BlockSpec((4 * tt * nj, 128), lambda i: (i, 0)), (lp * 4 * nj, 128)


def _pack_pair(re, im):
    def rounded(v):
        return lax.bitcast_convert_type(v, jnp.int32) + HALF_OF_LOW_HALF
    return lax.bitcast_convert_type((rounded(re) & HIGH_HALF) | lax.shift_right_logical(rounded(im), 16), F32)


def _unpack_pair(w):
    b = lax.bitcast_convert_type(w, jnp.int32)
    return lax.bitcast_convert_type(b & HIGH_HALF, F32), lax.bitcast_convert_type(lax.shift_left(b, 16), F32)


def _slab_load(ref, c):
    nj = ref.shape[0] // (4 * TOKEN_TILE)
    first = c * TOKEN_TILE * nj
    return _unpack_pair(jnp.concatenate([ref[pl.ds(first + j, TOKEN_TILE, stride=nj), :] for j in range(nj)], axis=1))


def _slab_store(ref, c, re, im):
    nj = ref.shape[0] // (4 * TOKEN_TILE)
    first = c * TOKEN_TILE * nj
    val = _pack_pair(re, im)
    for j in range(nj):
        ref[pl.ds(first + j, TOKEN_TILE, stride=nj), :] = val[:, j * 128:(j + 1) * 128]


def _s5_disc_math(lr, li, ldt, br, bi):
    dt = jnp.exp(ldt)
    mag = jnp.exp(lr * dt)
    ar = mag * jnp.cos(li * dt)
    ai = mag * jnp.sin(li * dt)
    den = lr * lr + li * li
    kr = ((ar - 1.0) * lr + ai * li) / den
    ki = (ai * lr - (ar - 1.0) * li) / den
    bbr = kr[None] * br - ki[None] * bi
    bbi = kr[None] * bi + ki[None] * br
    return ar, ai, bbr, bbi


def s5_disc_fwd(lr, li, ldt, br_t, bi_t, c_re, c_im, name):
    ni, ng, npp = br_t.shape
    gc = ng // 4

    def body(lr_ref, li_ref, ldt_ref, br_ref, bi_ref, cre_ref, cim_ref,
             ar_ref, ai_ref, bdre_ref, bdim_ref, cdre_ref, cdim_ref, bbr_sc, bbi_sc):
        ar, ai, bbr, bbi = _s5_disc_math(lr_ref[...], li_ref[...], ldt_ref[...], br_ref[...], bi_ref[...])
        ar_ref[...] = ar
        ai_ref[...] = ai
        bbr_sc[...] = bbr
        bbi_sc[...] = bbi
        for ref in (bdre_ref, bdim_ref, cdre_ref, cdim_ref):
            ref[...] = jnp.zeros_like(ref)
        for k in range(4):
            for j in range(gc):
                g = k * gc + j
                ins, states = pl.ds(j * ni, ni), pl.ds(j * npp, npp)
                bdre_ref[k, ins, states] = bbr_sc[:, g, :].astype(BF16)
                bdim_ref[k, ins, states] = bbi_sc[:, g, :].astype(BF16)
                cdre_ref[k, states, ins] = cre_ref[g].T.astype(BF16)
                cdim_ref[k, states, ins] = (-cim_ref[g]).T.astype(BF16)

    sd = jax.ShapeDtypeStruct
    return pl.pallas_call(
        body, name=name,
        out_shape=(sd(lr.shape, F32), sd(lr.shape, F32), sd((4, gc * ni, gc * npp), BF16), sd((4, gc * ni, gc * npp), BF16),
                   sd((4, gc * npp, gc * ni), BF16), sd((4, gc * npp, gc * ni), BF16)),
        scratch_shapes=[pltpu.VMEM(br_t.shape, F32), pltpu.VMEM(br_t.shape, F32)],
        compiler_params=pltpu.CompilerParams(vmem_limit_bytes=VMEM_LIMIT),
    )(lr, li, ldt, br_t, bi_t, c_re, c_im)


def s5_disc_bwd(lr, li, ldt, br_t, bi_t, dar, dai, dbbr, dbbi, name):
    def body(lr_ref, li_ref, ldt_ref, br_ref, bi_ref, dar_ref, dai_ref, dbbr_ref, dbbi_ref,
             dlr_ref, dli_ref, dldt_ref, dbr_ref, dbi_ref):
        _, vjp = jax.vjp(_s5_disc_math, lr_ref[...], li_ref[...], ldt_ref[...], br_ref[...], bi_ref[...])
        dlr, dli, dldt, dbr, dbi = vjp((dar_ref[...], dai_ref[...], dbbr_ref[...], dbbi_ref[...]))
        dlr_ref[...] = dlr
        dli_ref[...] = dli
        dldt_ref[...] = dldt
        dbr_ref[...] = dbr
        dbi_ref[...] = dbi

    sd = jax.ShapeDtypeStruct
    return pl.pallas_call(
        body, name=name,
        out_shape=(sd(lr.shape, F32), sd(lr.shape, F32), sd(ldt.shape, F32), sd(br_t.shape, F32), sd(br_t.shape, F32)),
    )(lr, li, ldt, br_t, bi_t, dar, dai, dbbr, dbbi)


def s5_fwd1(h, g, w_in, bdre, bdim, name):
    lp, d = h.shape
    tt = TOKEN_TILE
    cw, sw = bdre.shape[1], bdre.shape[2]

    def body(h_ref, g_ref, w_hbm, bdre_hbm, bdim_hbm, u_ref, z_ref, x_ref, w, bre, bim):
        @pl.when(pl.program_id(0) == 0)
        def _():
            pltpu.sync_copy(w_hbm, w)
            pltpu.sync_copy(bdre_hbm, bre)
            pltpu.sync_copy(bdim_hbm, bim)

        n = _rms_fwd(h_ref[...], g_ref[...])[0].astype(BF16)
        for c in range(4):
            cols = slice(c * cw, (c + 1) * cw)
            u = jnp.dot(n, w[c], preferred_element_type=F32)
            u_ref[:, cols] = u
            z_ref[:, cols] = jnp.dot(n, w[c + 4], preferred_element_type=F32)
            ub = u.astype(BF16)
            _slab_store(x_ref, c, jnp.dot(ub, bre[c], preferred_element_type=F32), jnp.dot(ub, bim[c], preferred_element_type=F32))

    sd = jax.ShapeDtypeStruct
    slab, slab_shape = _slab_spec(lp, tt, sw)
    row = pl.BlockSpec((tt, d), lambda i: (i, 0))
    return pl.pallas_call(
        body, name=name, grid=(lp // tt,),
        in_specs=[row, pl.BlockSpec((1, d), lambda i: (0, 0)), ANY, ANY, ANY],
        out_specs=[row, row, slab],
        out_shape=(sd((lp, d), F32), sd((lp, d), F32), sd(slab_shape, F32)),
        scratch_shapes=[pltpu.VMEM(w_in.shape, BF16), pltpu.VMEM(bdre.shape, BF16), pltpu.VMEM(bdim.shape, BF16)],
        compiler_params=_params(),
    )(h, g, w_in, bdre, bdim)


def s5_scan_fwd(x, ar, ai, name):
    nj = ar.shape[1]
    tt = TOKEN_TILE
    cpb = SCAN_CHUNKS
    nt = x.shape[0] // (4 * tt * nj)

    def body(x_ref, ar_ref, ai_ref, s_ref, st_r, st_i):
        i, cg = pl.program_id(0), pl.program_id(1)

        @pl.when(i == 0)
        def _():
            for q in range(cpb):
                st_r[cg * cpb + q] = jnp.zeros((nj, 128), F32)
                st_i[cg * cpb + q] = jnp.zeros((nj, 128), F32)

        a_r = [ar_ref[cg * cpb + q] for q in range(cpb)]
        a_i = [ai_ref[cg * cpb + q] for q in range(cpb)]

        def step(k, carry):
            carry = list(carry)
            for uu in range(SCAN_UNROLL):
                t = k * SCAN_UNROLL + uu
                for q in range(cpb):
                    s_r, s_i = carry[q]
                    rows = pl.ds(pl.multiple_of((q * tt + t) * nj, nj), nj)
                    x_r, x_i = _unpack_pair(x_ref[rows, :])
                    n_r = a_r[q] * s_r - a_i[q] * s_i + x_r
                    n_i = a_r[q] * s_i + a_i[q] * s_r + x_i
                    s_ref[rows, :] = _pack_pair(n_r, n_i)
                    carry[q] = (n_r, n_i)
            return tuple(carry)

        init = tuple((st_r[cg * cpb + q], st_i[cg * cpb + q]) for q in range(cpb))
        final = lax.fori_loop(0, tt // SCAN_UNROLL, step, init)
        for q in range(cpb):
            st_r[cg * cpb + q] = final[q][0]
            st_i[cg * cpb + q] = final[q][1]

    blk = pl.BlockSpec((cpb * tt * nj, 128), lambda i, cg: (i * (4 // cpb) + cg, 0))
    par = pl.BlockSpec((4, nj, 128), lambda i, cg: (0, 0, 0))
    sd = jax.ShapeDtypeStruct
    return pl.pallas_call(
        body, name=name, grid=(nt, 4 // cpb),
        in_specs=[blk, par, par], out_specs=blk,
        out_shape=sd(x.shape, F32),
        scratch_shapes=[pltpu.VMEM((4, nj, 128), F32), pltpu.VMEM((4, nj, 128), F32)],
        compiler_params=_params(ndim=2),
    )(x, ar, ai)


def s5_fwd3(s, u, z, h, cdre, cdim, w_glu, w_out, d_skip, b_glu, name):
    lp, d = h.shape
    tt = TOKEN_TILE
    sw, cw = cdre.shape[1], cdre.shape[2]

    def body(s_ref, u_ref, z_ref, h_ref, d_ref, bg_ref, cre_hbm, cim_hbm, wg_hbm, wo_hbm,
             o_ref, y_ref, q_ref, cre, cim, wg, wo):
        @pl.when(pl.program_id(0) == 0)
        def _():
            pltpu.sync_copy(cre_hbm, cre)
            pltpu.sync_copy(cim_hbm, cim)
            pltpu.sync_copy(wg_hbm, wg)
            pltpu.sync_copy(wo_hbm, wo)

        gys, q = [], None
        for c in range(4):
            cols = slice(c * cw, (c + 1) * cw)
            s_r, s_i = _slab_load(s_ref, c)
            y = _dot(s_r, cre[c]) + _dot(s_i, cim[c]) + d_ref[c] * u_ref[:, cols]
            y_ref[:, cols] = y
            gys.append(_gelu(y)[0])
            part = _dot(gys[c], wg[c])
            q = part if c == 0 else q + part
        q_ref[...] = q
        sig = _sigmoid(q + bg_ref[...])
        zz = z_ref[...]
        sz = zz * _sigmoid(zz)
        o = h_ref[...]
        for k in range(4):
            cols = slice(k * cw, (k + 1) * cw)
            o = o + _dot(gys[k] * sig[:, cols] * sz[:, cols], wo[k])
        o_ref[...] = o

    row = pl.BlockSpec((tt, d), lambda i: (i, 0))
    slab, _ = _slab_spec(lp, tt, sw)
    sd = jax.ShapeDtypeStruct((lp, d), F32)
    return pl.pallas_call(
        body, name=name, grid=(lp // tt,),
        in_specs=[slab, row, row, row, pl.BlockSpec((4, 1, cw), lambda i: (0, 0, 0)), pl.BlockSpec((1, d), lambda i: (0, 0)),
                  ANY, ANY, ANY, ANY],
        out_specs=[row, row, row],
        out_shape=(sd, sd, sd),
        scratch_shapes=[pltpu.VMEM(cdre.shape, BF16), pltpu.VMEM(cdim.shape, BF16), pltpu.VMEM(w_glu.shape, BF16),
                        pltpu.VMEM(w_out.shape, BF16)],
        compiler_params=_params(),
    )(s, u, z, h, d_skip, b_glu, cdre, cdim, w_glu, w_out)


def s5_bwd3a(dh, y, q, z, w_glu, w_out, b_glu, name):
    lp, d = dh.shape
    tt = TOKEN_TILE
    nt = lp // tt
    cw = w_glu.shape[1]

    def body(dh_ref, y_ref, q_ref, z_ref, bg_ref, wg_hbm, wo_hbm, dy_ref, dp_ref, dwo_hbm, dwg_hbm, dbg_hbm,
             wg, wo, dwo, dwg, dbg):
        i = pl.program_id(0)

        @pl.when(i == 0)
        def _():
            pltpu.sync_copy(wg_hbm, wg)
            pltpu.sync_copy(wo_hbm, wo)
            dwo[...] = jnp.zeros_like(dwo)
            dwg[...] = jnp.zeros_like(dwg)
            dbg[...] = jnp.zeros_like(dbg)

        sig = _sigmoid(q_ref[...] + bg_ref[...])
        sz, dsz = _silu_and_grad(z_ref[...])
        dhv = dh_ref[...]
        yv = y_ref[...]
        gy, t = _gelu(yv)
        dq_parts, dgy_parts = [], []
        for k in range(4):
            cols = slice(k * cw, (k + 1) * cw)
            gy_k, sig_k, sz_k = gy[:, cols], sig[:, cols], sz[:, cols]
            y2 = gy_k * sig_k
            dy3 = _dot_nt(dhv, wo[k])
            dwo[k] += _dot_tn(y2 * sz_k, dhv)
            dy2 = dy3 * sz_k
            dp_ref[0, :, cols] = (dy3 * y2 * dsz[:, cols]).astype(BF16)
            dq_parts.append(dy2 * gy_k * sig_k * (1.0 - sig_k))
            dgy_parts.append(dy2 * sig_k)
        dq = jnp.concatenate(dq_parts, axis=1)
        dbg[...] += jnp.sum(dq, axis=0, keepdims=True)
        dgelu = _gelu_grad(yv, t)
        for k in range(4):
            cols = slice(k * cw, (k + 1) * cw)
            dwg[k] += _dot_tn(gy[:, cols], dq)
            dy_ref[:, cols] = (dgy_parts[k] + _dot_nt(dq, wg[k])) * dgelu[:, cols]

        @pl.when(i == nt - 1)
        def _():
            pltpu.sync_copy(dwo, dwo_hbm)
            pltpu.sync_copy(dwg, dwg_hbm)
            pltpu.sync_copy(dbg, dbg_hbm)

    row = pl.BlockSpec((tt, d), lambda i: (i, 0))
    sd = jax.ShapeDtypeStruct
    return pl.pallas_call(
        body, name=name, grid=(nt,),
        in_specs=[row, row, row, row, pl.BlockSpec((1, d), lambda i: (0, 0)), ANY, ANY],
        out_specs=[row, pl.BlockSpec((1, tt, d), lambda i: (1, i, 0)), ANY, ANY, ANY],
        out_shape=(sd((lp, d), F32), sd((2, lp, d), BF16), sd(w_out.shape, F32), sd(w_glu.shape, F32), sd((1, d), F32)),
        scratch_shapes=[pltpu.VMEM(w_glu.shape, BF16), pltpu.VMEM(w_out.shape, BF16),
                        pltpu.VMEM(w_out.shape, F32), pltpu.VMEM(w_glu.shape, F32), pltpu.VMEM((1, d), F32)],
        compiler_params=_params(),
    )(dh, y, q, z, b_glu, w_glu, w_out)


def s5_bwd3b(dy, s, u, cdre, cdim, d_skip, name):
    lp, d = dy.shape
    tt = TOKEN_TILE
    nt = lp // tt
    sw, cw = cdre.shape[1], cdre.shape[2]
    gc = cw // S5_GROUP

    def body(dy_ref, s_ref, u_ref, d_ref, cre_hbm, cim_hbm,
             ds_ref, dus_ref, dcre_ref, dcim_ref, dd_hbm, cre, cim, dcre, dcim, dd):
        i = pl.program_id(0)

        @pl.when(i == 0)
        def _():
            pltpu.sync_copy(cre_hbm, cre)
            pltpu.sync_copy(cim_hbm, cim)
            dcre[...] = jnp.zeros_like(dcre)
            dcim[...] = jnp.zeros_like(dcim)
            dd[...] = jnp.zeros_like(dd)

        for c in range(4):
            chunk = slice(c * cw, (c + 1) * cw)
            dyv = dy_ref[:, chunk]
            dd[c] += jnp.sum(dyv * u_ref[:, chunk], axis=0, keepdims=True)
            dus_ref[:, chunk] = dyv * d_ref[c]
            _slab_store(ds_ref, c, _dot_nt(dyv, cre[c]), _dot_nt(dyv, cim[c]))
            s_r, s_i = _slab_load(s_ref, c)
            dcre[c] += _dot_tn(s_r, dyv)
            dcim[c] += _dot_tn(s_i, dyv)

        @pl.when(i == nt - 1)
        def _():
            for k in range(4):
                for j in range(gc):
                    rows, cols = pl.ds(j * S5_STATE, S5_STATE), pl.ds(j * S5_GROUP, S5_GROUP)
                    dcre_ref[k, j] = dcre[k, rows, cols].T
                    dcim_ref[k, j] = dcim[k, rows, cols].T
            pltpu.sync_copy(dd, dd_hbm)

    sd = jax.ShapeDtypeStruct
    row = pl.BlockSpec((tt, d), lambda i: (i, 0))
    slab, slab_shape = _slab_spec(lp, tt, sw)
    diag = pl.BlockSpec((4, gc, S5_GROUP, S5_STATE), lambda i: (0, 0, 0, 0))
    return pl.pallas_call(
        body, name=name, grid=(nt,),
        in_specs=[row, slab, row, pl.BlockSpec((4, 1, cw), lambda i: (0, 0, 0)), ANY, ANY],
        out_specs=[slab, row, diag, diag, ANY],
        out_shape=(sd(slab_shape, F32), sd((lp, d), F32),
                   sd((4, gc, S5_GROUP, S5_STATE), F32), sd((4, gc, S5_GROUP, S5_STATE), F32), sd((4, 1, cw), F32)),
        scratch_shapes=[pltpu.VMEM(cdre.shape, BF16), pltpu.VMEM(cdim.shape, BF16),
                        pltpu.VMEM(cdre.shape, F32), pltpu.VMEM(cdim.shape, F32), pltpu.VMEM((4, 1, cw), F32)],
        compiler_params=_params(),
    )(dy, s, u, d_skip, cdre, cdim)


def s5_scan_bwd(g, s, ar, ai, name):
    nj = ar.shape[1]
    tt = TOKEN_TILE
    cpb = SCAN_CHUNKS
    nt = g.shape[0] // (4 * tt * nj)

    def body(g_ref, s_ref, ar_ref, ai_ref, lam_ref, dar_ref, dai_ref, st_r, st_i, acc_r, acc_i):
        i, cg = pl.program_id(0), pl.program_id(1)

        @pl.when((i == 0) & (cg == 0))
        def _():
            for ref in (st_r, st_i, acc_r, acc_i):
                ref[...] = jnp.zeros_like(ref)

        a_r = [ar_ref[cg * cpb + q] for q in range(cpb)]
        a_i = [ai_ref[cg * cpb + q] for q in range(cpb)]

        def slab(q, t):
            return pl.ds(pl.multiple_of((q * tt + t) * nj, nj), nj)

        def adjoint(q, t, l_r, l_i):
            rows = slab(q, t)
            g_r, g_i = _unpack_pair(g_ref[rows, :])
            n_r = g_r + a_r[q] * l_r + a_i[q] * l_i
            n_i = g_i + a_r[q] * l_i - a_i[q] * l_r
            lam_ref[rows, :] = _pack_pair(n_r, n_i)
            return n_r, n_i

        def pair(q, t, l_r, l_i, d_r, d_i):
            p_r, p_i = _unpack_pair(s_ref[slab(q, t), :])
            return d_r + l_r * p_r + l_i * p_i, d_i + l_i * p_r - l_r * p_i

        def step(k, carry):
            carry = list(carry)
            for uu in range(SCAN_UNROLL):
                t = tt - 1 - (k * SCAN_UNROLL + uu)
                for q in range(cpb):
                    l_r, l_i, d_r, d_i = carry[q]
                    d_r, d_i = pair(q, t, l_r, l_i, d_r, d_i)
                    l_r, l_i = adjoint(q, t, l_r, l_i)
                    carry[q] = (l_r, l_i, d_r, d_i)
            return tuple(carry)

        init = tuple((st_r[cg * cpb + q], st_i[cg * cpb + q], acc_r[cg * cpb + q], acc_i[cg * cpb + q]) for q in range(cpb))
        final = lax.fori_loop(0, tt // SCAN_UNROLL, step, init)
        for q in range(cpb):
            ch = cg * cpb + q
            l_r, l_i, d_r, d_i = final[q]
            st_r[ch] = l_r
            st_i[ch] = l_i
            acc_r[ch] = d_r
            acc_i[ch] = d_i
            dar_ref[ch] = d_r
            dai_ref[ch] = d_i

    blk = pl.BlockSpec((cpb * tt * nj, 128), lambda i, cg: ((nt - 1 - i) * (4 // cpb) + cg, 0))
    par = pl.BlockSpec((4, nj, 128), lambda i, cg: (0, 0, 0))
    sd = jax.ShapeDtypeStruct
    return pl.pallas_call(
        body, name=name, grid=(nt, 4 // cpb),
        in_specs=[blk, blk, par, par], out_specs=[blk, par, par],
        out_shape=(sd(g.shape, F32), sd((4, nj, 128), F32), sd((4, nj, 128), F32)),
        scratch_shapes=[pltpu.VMEM((4, nj, 128), F32)] * 4,
        compiler_params=_params(ndim=2),
    )(g, s, ar, ai)


def s5_bwd1(lam, dus, u, dp, h, dh, g, w_in, bdre, bdim, name, pad_tiles=None):
    lp, d = h.shape
    tt = TOKEN_TILE
    nt = lp // tt
    cw, sw = bdre.shape[1], bdre.shape[2]
    gc = cw // S5_GROUP

    def body(lam_ref, dus_ref, u_ref, dpz_ref, h_ref, dh_ref, g_ref, w_hbm, bre_hbm, bim_hbm,
             dpu_ref, dho_ref, n_ref, dbre_ref, dbim_ref, dg_hbm, *rest):
        (gx_ref,), (w, bre, bim, dbre, dbim, dg) = (rest[:1], rest[1:]) if pad_tiles is not None else ((None,), rest)
        i = pl.program_id(0)

        @pl.when(i == 0)
        def _():
            pltpu.sync_copy(w_hbm, w)
            pltpu.sync_copy(bre_hbm, bre)
            pltpu.sync_copy(bim_hbm, bim)
            dbre[...] = jnp.zeros_like(dbre)
            dbim[...] = jnp.zeros_like(dbim)
            dg[...] = jnp.zeros_like(dg)

        dz = dpz_ref[0]
        dn = None
        for c in range(4):
            chunk = slice(c * cw, (c + 1) * cw)
            (l_r, l_i), uv = _slab_load(lam_ref, c), u_ref[:, chunk]
            du = dus_ref[:, chunk] + _dot_nt(l_r, bre[c]) + _dot_nt(l_i, bim[c])
            dbre[c] += _dot_tn(uv, l_r)
            dbim[c] += _dot_tn(uv, l_i)
            dpu_ref[0, :, chunk] = du.astype(BF16)
            part = _dot_nt(du, w[c]) + _dot_nt(dz[:, chunk], w[4 + c])
            dn = part if c == 0 else dn + part
        gv = g_ref[...]
        n, hh, rr = _rms_fwd(h_ref[...], gv)
        n_ref[...] = n.T.astype(BF16)
        dg[...] += jnp.sum(dn * hh, axis=0, keepdims=True)
        dh_in = dh_ref[...] + _rms_bwd(dn, hh, rr, gv)
        dho_ref[...] = dh_in
        if pad_tiles is not None:
            @pl.when(i >= pad_tiles)
            def _():
                gx_ref[...] = dh_in

        @pl.when(i == nt - 1)
        def _():
            for k in range(4):
                for j in range(gc):
                    rows, cols = pl.ds(j * S5_GROUP, S5_GROUP), pl.ds(j * S5_STATE, S5_STATE)
                    dbre_ref[k, j] = dbre[k, rows, cols]
                    dbim_ref[k, j] = dbim[k, rows, cols]
            pltpu.sync_copy(dg, dg_hbm)

    sd = jax.ShapeDtypeStruct
    row = pl.BlockSpec((tt, d), lambda i: (i, 0))
    slab, _ = _slab_spec(lp, tt, sw)
    diag = pl.BlockSpec((4, gc, S5_GROUP, S5_STATE), lambda i: (0, 0, 0, 0))
    extra_specs, extra_shapes = [], ()
    if pad_tiles is not None:
        extra_specs = [pl.BlockSpec((tt, d), lambda i: (jnp.maximum(i - pad_tiles, 0), 0))]
        extra_shapes = (sd((lp - pad_tiles * tt, d), F32),)
    return pl.pallas_call(
        body, name=name, grid=(nt,),
        in_specs=[slab, row, row, pl.BlockSpec((1, tt, d), lambda i: (1, i, 0)), row, row, pl.BlockSpec((1, d), lambda i: (0, 0)),
                  ANY, ANY, ANY],
        out_specs=[pl.BlockSpec((1, tt, d), lambda i: (0, i, 0)), row, pl.BlockSpec((d, tt), lambda i: (0, i)), diag, diag, ANY]
        + extra_specs,
        out_shape=(sd(dp.shape, BF16), sd((lp, d), F32), sd((d, lp), BF16),
                   sd((4, gc, S5_GROUP, S5_STATE), F32), sd((4, gc, S5_GROUP, S5_STATE), F32), sd((1, d), F32)) + extra_shapes,
        input_output_aliases={3: 0},
        scratch_shapes=[pltpu.VMEM(w_in.shape, BF16), pltpu.VMEM(bdre.shape, BF16), pltpu.VMEM(bdim.shape, BF16),
                        pltpu.VMEM(bdre.shape, F32), pltpu.VMEM(bdim.shape, F32), pltpu.VMEM((1, d), F32)],
        compiler_params=_params(),
    )(lam, dus, u, dp, h, dh, g, w_in, bdre, bdim)


def grad_w_in(n_t, dp, blk, name):
    d, lp = n_t.shape
    npart, _, width = dp.shape
    per = width // blk

    def body(n_ref, dp_ref, o_ref):
        o_ref[0] = jnp.dot(n_ref[...], dp_ref[0], preferred_element_type=F32).astype(o_ref.dtype)

    return pl.pallas_call(
        body, name=name, grid=(npart * per,),
        in_specs=[pl.BlockSpec((d, lp), lambda j: (0, 0), pipeline_mode=pl.Buffered(1)),
                  pl.BlockSpec((1, lp, blk), lambda j: (j // per, 0, j % per))],
        out_specs=pl.BlockSpec((1, d, blk), lambda j: (j, 0, 0)),
        out_shape=jax.ShapeDtypeStruct((npart * per, d, blk), BF16),
        compiler_params=_params(),
    )(n_t, dp)


def _conv_mix(cg, v, taps, bias, halo):
    hc = cg * v
    conv = taps[2:3, :] * hc + taps[1:2, :] * _shift_down(hc, 1, halo) + taps[0:1, :] * _shift_down(hc, 2, halo) + bias
    return hc, conv


def conv_fwd(h, g, w_in, conv_w, conv_b, w_out, name):
    lp, d = h.shape
    tt = TOKEN_TILE
    nt = lp // tt
    nch, ce = w_out.shape[0], w_out.shape[1]

    def body(h_ref, g_ref, cw_ref, cb_ref, w_hbm, wo_hbm, o_ref, halo_ref, acts_ref, w, wo, halo):
        i = pl.program_id(0)

        @pl.when(i == 0)
        def _():
            pltpu.sync_copy(w_hbm, w)
            pltpu.sync_copy(wo_hbm, wo)
            halo[...] = jnp.zeros_like(halo)

        hv = h_ref[...]
        n = _rms_fwd(hv, g_ref[...])[0].astype(BF16)
        o = hv
        for c in range(nch):
            cols = slice(c * ce, (c + 1) * ce)
            bg, cg, v, z = [jnp.dot(n, w[p * nch + c], preferred_element_type=F32) for p in range(4)]
            for p, val in enumerate((bg, cg, v, z)):
                acts_ref[p, :, cols] = val.astype(BF16)
            hc, conv = _conv_mix(cg, v, cw_ref[c], cb_ref[c], halo[c])
            o = o + _dot(bg * conv * (z * _sigmoid(z)), wo[c])
            halo[c] = hc[tt - CONV_HALO:, :]
            halo_ref[0, c] = hc[tt - CONV_HALO:, :]
        o_ref[...] = o

    sd = jax.ShapeDtypeStruct
    return pl.pallas_call(
        body, name=name, grid=(nt,),
        in_specs=[pl.BlockSpec((tt, d), lambda i: (i, 0)), pl.BlockSpec((1, d), lambda i: (0, 0)),
                  pl.BlockSpec(conv_w.shape, lambda i: (0, 0, 0)), pl.BlockSpec(conv_b.shape, lambda i: (0, 0, 0)), ANY, ANY],
        out_specs=[pl.BlockSpec((tt, d), lambda i: (i, 0)), pl.BlockSpec((1, nch, CONV_HALO, ce), lambda i: (i, 0, 0, 0)),
                   pl.BlockSpec((4, tt, nch * ce), lambda i: (0, i, 0))],
        out_shape=(sd((lp, d), F32), sd((nt, nch, CONV_HALO, ce), F32), sd((4, lp, nch * ce), BF16)),
        scratch_shapes=[pltpu.VMEM(w_in.shape, BF16), pltpu.VMEM(w_out.shape, BF16), pltpu.VMEM((nch, CONV_HALO, ce), F32)],
        compiler_params=_params(),
    )(h, g, conv_w, conv_b, w_in, w_out)


def conv_bwd(h, dh, halos, acts, g, w_in, conv_w, conv_b, w_out, name):
    lp, d = h.shape
    tt = TOKEN_TILE
    nt = lp // tt
    nch, ce = w_out.shape[0], w_out.shape[1]

    def body(h_ref, dh_ref, halo_ref, acts_ref, g_ref, cw_ref, cb_ref, w_hbm, wo_hbm,
             dho_ref, n_ref, dp_ref, dwo_hbm, dcw_hbm, dcb_hbm, dg_hbm, w, wo, nxt, dwo, dcw, dcb, dg):
        i = pl.program_id(0)

        @pl.when(i == 0)
        def _():
            pltpu.sync_copy(w_hbm, w)
            pltpu.sync_copy(wo_hbm, wo)
            for ref in (nxt, dwo, dcw, dcb, dg):
                ref[...] = jnp.zeros_like(ref)

        gv = g_ref[...]
        nf, hh, rr = _rms_fwd(h_ref[...], gv)
        n_ref[...] = nf.T.astype(BF16)
        dhv = dh_ref[...]
        has_prev = (i < nt - 1).astype(F32)
        dn = jnp.zeros((tt, d), F32)
        hw = ce // CONV_SPLIT
        for c, part in [(c, part) for c in range(nch) for part in range(CONV_SPLIT)]:
            sub = slice(part * hw, (part + 1) * hw)
            cols = slice(c * ce + part * hw, c * ce + (part + 1) * hw)
            halo = halo_ref[0, c, :, sub] * has_prev
            bg, cg, v, z = [acts_ref[p, :, cols].astype(F32) for p in range(4)]
            taps = cw_ref[c, :, sub]
            hc, conv = _conv_mix(cg, v, taps, cb_ref[c, :, sub], halo)
            sz, dsz = _silu_and_grad(z)
            y1 = bg * conv
            dy2 = _dot_nt(dhv, wo[c, sub, :])
            dwo[c, sub, :] += _dot_tn(y1 * sz, dhv)
            dy1 = dy2 * sz
            dz = dy2 * y1 * dsz
            dbg = dy1 * conv
            dconv = dy1 * bg
            dcb[c, :, sub] += jnp.sum(dconv, axis=0, keepdims=True)
            up1 = _shift_up(dconv, 1, nxt[c, :, sub])
            up2 = _shift_up(dconv, 2, nxt[c, :, sub])
            nxt[c, :, sub] = dconv[:CONV_HALO, :]
            dhc = taps[2:3, :] * dconv + taps[1:2, :] * up1 + taps[0:1, :] * up2
            dcw[c, 0:1, sub] += jnp.sum(hc * up2, axis=0, keepdims=True)
            dcw[c, 1:2, sub] += jnp.sum(hc * up1, axis=0, keepdims=True)
            dcw[c, 2:3, sub] += jnp.sum(hc * dconv, axis=0, keepdims=True)
            dcg = dhc * v
            dv = dhc * cg
            for p, val in enumerate((dbg, dcg, dv, dz)):
                dp_ref[p, :, cols] = val.astype(BF16)
                dn = dn + _dot_nt(val, w[p * nch + c, :, sub])
        dg[...] += jnp.sum(dn * hh, axis=0, keepdims=True)
        dho_ref[...] = dhv + _rms_bwd(dn, hh, rr, gv)

        @pl.when(i == nt - 1)
        def _():
            pltpu.sync_copy(dwo, dwo_hbm)
            pltpu.sync_copy(dcw, dcw_hbm)
            pltpu.sync_copy(dcb, dcb_hbm)
            pltpu.sync_copy(dg, dg_hbm)

    rev = lambda i: (nt - 1 - i, 0)
    sd = jax.ShapeDtypeStruct
    return pl.pallas_call(
        body, name=name, grid=(nt,),
        in_specs=[pl.BlockSpec((tt, d), rev), pl.BlockSpec((tt, d), rev),
                  pl.BlockSpec((1, nch, CONV_HALO, ce), lambda i: (jnp.maximum(nt - 2 - i, 0), 0, 0, 0)),
                  pl.BlockSpec((4, tt, nch * ce), lambda i: (0, nt - 1 - i, 0)),
                  pl.BlockSpec((1, d), lambda i: (0, 0)),
                  pl.BlockSpec(conv_w.shape, lambda i: (0, 0, 0)), pl.BlockSpec(conv_b.shape, lambda i: (0, 0, 0)), ANY, ANY],
        out_specs=[pl.BlockSpec((tt, d), rev), pl.BlockSpec((d, tt), lambda i: (0, nt - 1 - i)),
                   pl.BlockSpec((4, tt, nch * ce), lambda i: (0, nt - 1 - i, 0)), ANY, ANY, ANY, ANY],
        out_shape=(sd((lp, d), F32), sd((d, lp), BF16), sd((4, lp, nch * ce), BF16),
                   sd(w_out.shape, F32), sd((nch, 8, ce), F32), sd((nch, 1, ce), F32), sd((1, d), F32)),
        scratch_shapes=[pltpu.VMEM(w_in.shape, BF16), pltpu.VMEM(w_out.shape, BF16), pltpu.VMEM((nch, CONV_HALO, ce), F32),
                        pltpu.VMEM(w_out.shape, F32), pltpu.VMEM((nch, 8, ce), F32), pltpu.VMEM((nch, 1, ce), F32),
                        pltpu.VMEM((1, d), F32)],
        compiler_params=_params(vmem=VMEM_LIMIT_LARGE),
    )(h, dh, halos, acts, g, conv_w, conv_b, w_in, w_out)


def _pool_mix(u, wg, bg_ref, sc_ref, halo, k, tile, tt, first_pos):
    ext = jnp.concatenate([halo, u], axis=0)
    win = _window_sums_back(ext)[k][POOL_HALO:, :]
    mixed = win * _pool_inv_count(tile, tt, first_pos, POOL_WINDOWS[k], u.shape[1]) - u
    outs = _dot(mixed, wg[k]) + bg_ref[k]
    return mixed, outs, outs * sc_ref[k]


def pool_fwd(h, g, w_in, w_grp, b_grp, scale, w_out, first_pos, name):
    lp, d = h.shape
    tt = TOKEN_TILE
    nt = lp // tt
    gw = w_grp.shape[1]

    def body(h_ref, g_ref, bg_ref, sc_ref, w_hbm, wg_hbm, wo_hbm, o_ref, halo_ref, acts_ref, w, wg, wo, halo):
        i = pl.program_id(0)

        @pl.when(i == 0)
        def _():
            pltpu.sync_copy(w_hbm, w)
            pltpu.sync_copy(wg_hbm, wg)
            pltpu.sync_copy(wo_hbm, wo)
            halo[...] = jnp.zeros_like(halo)

        hv = h_ref[...]
        n = _rms_fwd(hv, g_ref[...])[0].astype(BF16)
        o = hv
        for k in range(4):
            cols = slice(k * gw, (k + 1) * gw)
            u = jnp.dot(n, w[k], preferred_element_type=F32)
            z = jnp.dot(n, w[4 + k], preferred_element_type=F32)
            acts_ref[0, :, cols] = u.astype(BF16)
            acts_ref[1, :, cols] = z.astype(BF16)
            _, _, yp = _pool_mix(u, wg, bg_ref, sc_ref, halo[k], k, i, tt, first_pos)
            o = o + _dot(yp * (z * _sigmoid(z)), wo[k])
            halo[k] = u[tt - POOL_HALO:, :]
            halo_ref[0, k] = u[tt - POOL_HALO:, :]
        o_ref[...] = o

    sd = jax.ShapeDtypeStruct
    small = pl.BlockSpec((4, 1, gw), lambda i: (0, 0, 0))
    return pl.pallas_call(
        body, name=name, grid=(nt,),
        in_specs=[pl.BlockSpec((tt, d), lambda i: (i, 0)), pl.BlockSpec((1, d), lambda i: (0, 0)), small, small, ANY, ANY, ANY],
        out_specs=[pl.BlockSpec((tt, d), lambda i: (i, 0)), pl.BlockSpec((1, 4, POOL_HALO, gw), lambda i: (i, 0, 0, 0)),
                   pl.BlockSpec((2, tt, 4 * gw), lambda i: (0, i, 0))],
        out_shape=(sd((lp, d), F32), sd((nt, 4, POOL_HALO, gw), F32), sd((2, lp, 4 * gw), BF16)),
        scratch_shapes=[pltpu.VMEM(w_in.shape, BF16), pltpu.VMEM(w_grp.shape, BF16), pltpu.VMEM(w_out.shape, BF16),
                        pltpu.VMEM((4, POOL_HALO, gw), F32)],
        compiler_params=_params(),
    )(h, g, b_grp, scale, w_in, w_grp, w_out)


def pool_bwd(h, dh, halos, acts, g, w_in, w_grp, b_grp, scale, w_out, first_pos, name):
    lp, d = h.shape
    tt = TOKEN_TILE
    nt = lp // tt
    gw = w_grp.shape[1]

    def body(h_ref, dh_ref, halo_ref, acts_ref, g_ref, bg_ref, sc_ref, w_hbm, wg_hbm, wo_hbm,
             dho_ref, n_ref, dp_ref, dwo_hbm, dwg_hbm, dbg_hbm, dsc_hbm, dg_hbm,
             w, wg, wo, nxt, dwo, dwg, dbg, dsc, dg):
        i = pl.program_id(0)
        tile = nt - 1 - i

        @pl.when(i == 0)
        def _():
            pltpu.sync_copy(w_hbm, w)
            pltpu.sync_copy(wg_hbm, wg)
            pltpu.sync_copy(wo_hbm, wo)
            for ref in (nxt, dwo, dwg, dbg, dsc, dg):
                ref[...] = jnp.zeros_like(ref)

        gv = g_ref[...]
        nf, hh, rr = _rms_fwd(h_ref[...], gv)
        n_ref[...] = nf.T.astype(BF16)
        dhv = dh_ref[...]
        has_prev = (i < nt - 1).astype(F32)
        dn = jnp.zeros((tt, d), F32)
        for k in range(4):
            cols = slice(k * gw, (k + 1) * gw)
            u, z = acts_ref[0, :, cols].astype(F32), acts_ref[1, :, cols].astype(F32)
            mixed, outs, yp = _pool_mix(u, wg, bg_ref, sc_ref, halo_ref[0, k] * has_prev, k, tile, tt, first_pos)
            sz, dsz = _silu_and_grad(z)
            dy = _dot_nt(dhv, wo[k])
            dwo[k] += _dot_tn(yp * sz, dhv)
            dyp = dy * sz
            dz = dy * yp * dsz
            dsc[k] += jnp.sum(dyp * outs, axis=0, keepdims=True)
            douts = dyp * sc_ref[k]
            dbg[k] += jnp.sum(douts, axis=0, keepdims=True)
            dwg[k] += _dot_tn(mixed, douts)
            dmixed = _dot_nt(douts, wg[k])
            dm = dmixed * _pool_inv_count(tile, tt, first_pos, POOL_WINDOWS[k], gw)
            ext = jnp.concatenate([dm, nxt[k]], axis=0)
            du = _window_sums_fwd(ext)[k][:tt, :] - dmixed
            nxt[k] = dm[:POOL_HALO, :]
            dp_ref[0, :, cols] = du.astype(BF16)
            dp_ref[1, :, cols] = dz.astype(BF16)
            dn = dn + _dot_nt(du, w[k]) + _dot_nt(dz, w[4 + k])
        dg[...] += jnp.sum(dn * hh, axis=0, keepdims=True)
        dho_ref[...] = dhv + _rms_bwd(dn, hh, rr, gv)

        @pl.when(i == nt - 1)
        def _():
            pltpu.sync_copy(dwo, dwo_hbm)
            pltpu.sync_copy(dwg, dwg_hbm)
            pltpu.sync_copy(dbg, dbg_hbm)
            pltpu.sync_copy(dsc, dsc_hbm)
            pltpu.sync_copy(dg, dg_hbm)

    rev = lambda i: (nt - 1 - i, 0)
    sd = jax.ShapeDtypeStruct
    small = pl.BlockSpec((4, 1, gw), lambda i: (0, 0, 0))
    return pl.pallas_call(
        body, name=name, grid=(nt,),
        in_specs=[pl.BlockSpec((tt, d), rev), pl.BlockSpec((tt, d), rev),
                  pl.BlockSpec((1, 4, POOL_HALO, gw), lambda i: (jnp.maximum(nt - 2 - i, 0), 0, 0, 0)),
                  pl.BlockSpec((2, tt, 4 * gw), lambda i: (0, nt - 1 - i, 0)),
                  pl.BlockSpec((1, d), lambda i: (0, 0)), small, small, ANY, ANY, ANY],
        out_specs=[pl.BlockSpec((tt, d), rev), pl.BlockSpec((d, tt), lambda i: (0, nt - 1 - i)),
                   pl.BlockSpec((2, tt, 4 * gw), lambda i: (0, nt - 1 - i, 0)), ANY, ANY, ANY, ANY, ANY],
        out_shape=(sd((lp, d), F32), sd((d, lp), BF16), sd((2, lp, 4 * gw), BF16),
                   sd(w_out.shape, F32), sd(w_grp.shape, F32), sd((4, 1, gw), F32), sd((4, 1, gw), F32), sd((1, d), F32)),
        scratch_shapes=[pltpu.VMEM(w_in.shape, BF16), pltpu.VMEM(w_grp.shape, BF16), pltpu.VMEM(w_out.shape, BF16),
                        pltpu.VMEM((4, POOL_HALO, gw), F32), pltpu.VMEM(w_out.shape, F32), pltpu.VMEM(w_grp.shape, F32),
                        pltpu.VMEM((4, 1, gw), F32), pltpu.VMEM((4, 1, gw), F32), pltpu.VMEM((1, d), F32)],
        compiler_params=_params(),
    )(h, dh, halos, acts, g, b_grp, scale, w_in, w_grp, w_out)


def loss_head(h, target, g, pad_tiles, name):
    lp, d = h.shape
    tt = TOKEN_TILE
    nt = lp // tt

    def body(h_ref, t_ref, g_ref, dh_ref, dg_ref, loss_ref, acc):
        i = pl.program_id(0)

        @pl.when(i == 0)
        def _():
            acc[...] = jnp.zeros_like(acc)
            dg_ref[...] = jnp.zeros_like(dg_ref)

        @pl.when(i < pad_tiles)
        def _():
            dh_ref[...] = jnp.zeros_like(dh_ref)

        @pl.when(i >= pad_tiles)
        def _():
            gv = g_ref[...]
            n, hh, rr = _rms_fwd(h_ref[...], gv)
            err = n - t_ref[...]
            acc[...] += 0.5 * jnp.sum(jnp.mean(err * err, axis=-1, keepdims=True), axis=0, keepdims=True)
            dn = err * (1.0 / d)
            dg_ref[...] += jnp.sum(dn * hh, axis=0, keepdims=True)
            dh_ref[...] = _rms_bwd(dn, hh, rr, gv)

        loss_ref[...] = jnp.broadcast_to(acc[...], loss_ref.shape)

    sd = jax.ShapeDtypeStruct
    return pl.pallas_call(
        body, name=name, grid=(nt,),
        in_specs=[pl.BlockSpec((tt, d), lambda i: (i, 0)), pl.BlockSpec((tt, d), lambda i: (jnp.maximum(i - pad_tiles, 0), 0)),
                  pl.BlockSpec((1, d), lambda i: (0, 0))],
        out_specs=[pl.BlockSpec((tt, d), lambda i: (i, 0)), pl.BlockSpec((1, d), lambda i: (0, 0)),
                   pl.BlockSpec((8, 128), lambda i: (0, 0))],
        out_shape=(sd((lp, d), F32), sd((1, d), F32), sd((8, 128), F32)),
        scratch_shapes=[pltpu.VMEM((1, 1), F32)],
        compiler_params=_params(),
    )(h, target, g)


def _peers(x, y, c):
    out = []
    for k in range(1, N_DEV):
        px = 1 - x if k & 4 else x
        py = 1 - y if k & 2 else y
        pc = 1 - c if k & 1 else c
        out.append((k, (px, py, pc), 4 * px + 2 * py + pc))
    return out


def exchange_start(arrs, gather, after, name):
    n = len(arrs)
    me = 4 * lax.axis_index("x") + 2 * lax.axis_index("y") + lax.axis_index("c")
    lands = []
    for a in arrs:
        own = a[None] if gather else lax.dynamic_index_in_dim(a, me, 0, keepdims=True)
        lands.append(lax.dynamic_update_index_in_dim(lax.empty(((N_DEV,) + a.shape) if gather else a.shape, a.dtype), own, me, 0))

    def body(*refs):
        ins, land = refs[:n], refs[n:2 * n]
        send_sems, recv_sems, token = refs[2 * n + 1], refs[2 * n + 2], refs[4 * n + 3]
        x, y, c = lax.axis_index("x"), lax.axis_index("y"), lax.axis_index("c")
        me = 4 * x + 2 * y + c
        for k, pid, peer in _peers(x, y, c):
            for a in range(n):
                pltpu.make_async_remote_copy(
                    src_ref=ins[a] if gather else ins[a].at[peer], dst_ref=land[a].at[me],
                    send_sem=send_sems.at[a * (N_DEV - 1) + k - 1], recv_sem=recv_sems.at[a * (N_DEV - 1) + k - 1],
                    device_id=pid, device_id_type=pl.DeviceIdType.MESH).start()
        token[...] = jnp.zeros_like(token)

    hbm = pl.BlockSpec(memory_space=pltpu.HBM)
    sem = pl.BlockSpec(memory_space=pltpu.SEMAPHORE)
    sems = pltpu.SemaphoreType.DMA((n * (N_DEV - 1),))
    res = pl.pallas_call(
        body, name=name, in_specs=[hbm] * (2 * n) + [ANY],
        out_specs=[sem, sem] + [hbm] * (2 * n) + [pl.BlockSpec(memory_space=pltpu.VMEM)],
        out_shape=[sems, sems] + [pltpu.HBM(a.shape, a.dtype) for a in arrs] + [pltpu.HBM(l.shape, l.dtype) for l in lands]
        + [jax.ShapeDtypeStruct((8, 128), F32)],
        input_output_aliases={a: 2 + a for a in range(2 * n)},
        compiler_params=pltpu.CompilerParams(has_side_effects=pltpu.SideEffectType.DATAFLOW_SIDE_EFFECTING),
    )(*[pltpu.with_memory_space_constraint(a, pltpu.HBM) for a in list(arrs) + lands], after)
    return res[0], res[1], res[2:2 + n], res[2 + n:2 + 2 * n], res[-1]


def exchange_wait(started, gather, after, name):
    send_sems, recv_sems, srcs, lands, _ = started
    n = len(srcs)
    after = list(after) if isinstance(after, (list, tuple)) else [after]

    def body(*refs):
        ins, land = refs[:n], refs[n:2 * n]
        send_sems, recv_sems = refs[2 * n], refs[2 * n + 1]
        x, y, c = lax.axis_index("x"), lax.axis_index("y"), lax.axis_index("c")
        for k, pid, peer in _peers(x, y, c):
            for a in range(n):
                cp = pltpu.make_async_remote_copy(
                    src_ref=ins[a] if gather else ins[a].at[peer], dst_ref=land[a].at[peer],
                    send_sem=send_sems.at[a * (N_DEV - 1) + k - 1], recv_sem=recv_sems.at[a * (N_DEV - 1) + k - 1],
                    device_id=pid, device_id_type=pl.DeviceIdType.MESH)
                cp.wait_send()
                cp.wait_recv()

    hbm = pl.BlockSpec(memory_space=pltpu.HBM)
    sem = pl.BlockSpec(memory_space=pltpu.SEMAPHORE)
    res = pl.pallas_call(
        body, name=name, in_specs=[hbm] * (2 * n) + [sem, sem] + [ANY] * len(after),
        out_specs=[hbm] * (2 * n),
        out_shape=[pltpu.HBM(a.shape, a.dtype) for a in list(srcs) + list(lands)],
        input_output_aliases={a: a for a in range(2 * n)},
        compiler_params=pltpu.CompilerParams(has_side_effects=pltpu.SideEffectType.DATAFLOW_SIDE_EFFECTING),
    )(*srcs, *lands, send_sems, recv_sems, *after)
    return res[n:]


def _adamw(w, g, m, v):
    m = ADAM_B1 * m + (1.0 - ADAM_B1) * g
    v = ADAM_B2 * v + (1.0 - ADAM_B2) * (g * g)
    m_hat = m / (1.0 - ADAM_B1 ** ADAM_STEP)
    v_hat = v / (1.0 - ADAM_B2 ** ADAM_STEP)
    return -ADAM_LR * (m_hat / (jnp.sqrt(v_hat) + ADAM_EPS) + ADAM_WD * w), m, v


def _update_tile_rows(rows, cols):
    if rows * cols <= UPDATE_TILE_ELEMS:
        return rows
    return max(t for t in range(8, UPDATE_TILE_ELEMS // cols + 1, 8) if rows % t == 0)


def _sum_in_order(p_ref):
    g = p_ref[0].astype(F32)
    for j in range(1, p_ref.shape[0]):
        g = g + p_ref[j].astype(F32)
    return g


def sum_parts(parts, name):
    nparts, rows, cols = parts.shape
    tr = _update_tile_rows(rows, cols)

    def body(p_ref, g_ref):
        g_ref[...] = _sum_in_order(p_ref)

    return pl.pallas_call(
        body, name=name, grid=(rows // tr,),
        in_specs=[pl.BlockSpec((nparts, tr, cols), lambda i: (0, i, 0))],
        out_specs=pl.BlockSpec((tr, cols), lambda i: (i, 0)), out_shape=jax.ShapeDtypeStruct((rows, cols), F32),
        compiler_params=_params(),
    )(parts)


def sum_adamw(parts, w, m, v, name):
    rows, cols = w.shape
    nparts = parts.shape[0]
    tr = _update_tile_rows(rows, cols)

    def body(p_ref, w_ref, m_ref, v_ref, g_ref, d_ref, nm_ref, nv_ref):
        g = _sum_in_order(p_ref)
        delta, nm, nv = _adamw(w_ref[...], g, m_ref[...], v_ref[...])
        g_ref[...] = g
        d_ref[...] = delta
        nm_ref[...] = nm
        nv_ref[...] = nv

    blk = pl.BlockSpec((tr, cols), lambda i: (i, 0))
    sd = jax.ShapeDtypeStruct((rows, cols), F32)
    return pl.pallas_call(
        body, name=name, grid=(rows // tr,),
        in_specs=[pl.BlockSpec((nparts, tr, cols), lambda i: (0, i, 0)), blk, blk, blk],
        out_specs=[blk] * 4, out_shape=(sd,) * 4,
        compiler_params=_params(),
    )(parts, w, m, v)


def update_packed(g, w, m, v, pieces, name):
    rows_all = w.shape[0]

    def body(g_ref, w_ref, m_ref, v_ref, *outs):
        gv = g_ref[:rows_all, :]
        res = (gv,) + _adamw(w_ref[...], gv, m_ref[...], v_ref[...])
        for p, (row, rows, lanes) in enumerate(pieces):
            for k in range(4):
                outs[4 * p + k][...] = res[k][row:row + rows, :lanes]

    shapes = [jax.ShapeDtypeStruct((rows, lanes), F32) for _, rows, lanes in pieces for _ in range(4)]
    return pl.pallas_call(body, name=name, out_shape=shapes,
                          compiler_params=pltpu.CompilerParams(vmem_limit_bytes=VMEM_LIMIT))(g, w, m, v)


def update_natural(groups, name):
    n = len(groups)
    steps = 8

    def body(*refs):
        ins, outs = refs[:4 * n], refs[4 * n:]
        for j in range(n):
            g_ref, w_ref, m_ref, v_ref = ins[4 * j:4 * j + 4]
            gv = g_ref[...]
            res = (gv,) + _adamw(w_ref[...], gv, m_ref[...], v_ref[...])
            for k in range(4):
                outs[4 * j + k][...] = res[k]

    specs, shapes = [], []
    for g, w, m, v in groups:
        rows, cols = w.shape
        specs += [pl.BlockSpec((rows // steps, cols), lambda i: (i, 0))] * 4
        shapes += [jax.ShapeDtypeStruct((rows, cols), F32)] * 4
    return pl.pallas_call(body, name=name, grid=(steps,), in_specs=specs, out_specs=specs, out_shape=shapes,
                          compiler_params=_params())(*[a for grp in groups for a in grp])


S5_NAMES = ("w_in", "lam_re", "lam_im", "log_dt", "b_re", "b_im", "c_re", "c_im", "d_skip", "w_glu", "b_glu", "w_out")
CONV_NAMES = ("w_in", "conv_w", "conv_b", "w_out")
POOL_NAMES = ("w_in", "w_grp", "b_grp", "scale", "w_out")
LAYER_KINDS = ("s5", "conv", "pool", "s5")
LAYER_NAMES = {"s5": S5_NAMES, "conv": CONV_NAMES, "pool": POOL_NAMES}
SHARDED = {"s5": ("w_in", "w_glu", "w_out"), "conv": ("w_in", "conv_w", "w_out"), "pool": ("w_in", "w_grp", "b_grp", "w_out")}
GATHER_F32 = ("conv_w", "b_grp")


def weight_names():
    names = ["meta_tokens"]
    for i, kind in enumerate(LAYER_KINDS):
        names.append("norm%d_g" % i)
        names += ["l%d_%s" % (i, n) for n in LAYER_NAMES[kind]]
    names.append("final_g")
    return names


def sharded_names():
    return ["meta_tokens"] + ["l%d_%s" % (i, n) for i, kind in enumerate(LAYER_KINDS) for n in SHARDED[kind]]


def _block_diag_in_grad(blocks):
    _, gc, i, p = blocks.shape
    return jnp.transpose(blocks, (2, 0, 1, 3)).reshape(i, 4 * gc, p)


def _block_diag_out_grad(blocks):
    _, gc, i, p = blocks.shape
    return blocks.reshape(4 * gc, i, p)


def _to_owner_blocks(a, axis):
    shape = a.shape[:axis] + (N_DEV, a.shape[axis] // N_DEV) + a.shape[axis + 1:]
    return jnp.moveaxis(a.reshape(shape), axis, 0)


def _from_owner_blocks(a, axis):
    a = jnp.moveaxis(a, 0, axis)
    return a.reshape(a.shape[:axis] + (a.shape[axis] * a.shape[axis + 1],) + a.shape[axis + 2:])


def _step(x, target, weights, moments_m, moments_v):
    seq, d = x.shape[1], x.shape[2]
    n_meta = weights["meta_tokens"].shape[0]
    tt = TOKEN_TILE
    pad_tiles = -(-n_meta // tt)
    p0 = pad_tiles * tt
    lp = p0 + seq
    first_pos = p0 - n_meta
    gc = d // 4 // S5_GROUP
    cw = d // 4

    big_names = [n for n in sharded_names() if n != "meta_tokens" and n.split("_", 1)[1] not in GATHER_F32]
    small_names = [n for n in sharded_names() if n not in big_names]
    layer_big = [[n for n in big_names if n.startswith("l%d_" % i)] for i in range(len(LAYER_KINDS))]
    layer_big[0] = small_names + layer_big[0]
    gather_started = []
    after = jnp.zeros((8, 128), F32)
    for i, names in enumerate(layer_big):
        gather_started.append(exchange_start([weights[n] if n in small_names else weights[n].astype(BF16) for n in names], True,
                                             after, "gather_start_l%d" % i))
        after = gather_started[-1][4]

    def vec(name):
        return weights[name].reshape(1, -1)

    s5_prep = {}
    for i, kind in enumerate(LAYER_KINDS):
        if kind == "s5":
            p = "l%d_" % i
            lr, li = weights[p + "lam_re"], weights[p + "lam_im"] + after[0, 0]
            ldt = weights[p + "log_dt"].reshape(-1, 1)
            br_t = jnp.transpose(weights[p + "b_re"], (2, 0, 1))
            bi_t = jnp.transpose(weights[p + "b_im"], (2, 0, 1))
            ar, ai, bdre, bdim, cdre, cdim = s5_disc_fwd(lr, li, ldt, br_t, bi_t, weights[p + "c_re"], weights[p + "c_im"],
                                                         p + "disc_fwd")
            s5_prep[i] = dict(
                disc=(lr, li, ldt, br_t, bi_t), ar=ar.reshape(4, -1, 128), ai=ai.reshape(4, -1, 128),
                bdre=bdre, bdim=bdim, cdre=cdre, cdim=cdim,
                d_skip=weights[p + "d_skip"].reshape(4, 1, cw), b_glu=vec(p + "b_glu"))
    h = jnp.concatenate([jnp.zeros((p0, d), F32), x[0] + after[0, 0]], axis=0)

    prepared = [h] + [s5_prep[i][k] for i in s5_prep for k in ("bdre", "bdim", "cdre", "cdim")]
    gathered = dict(zip(layer_big[0], exchange_wait(gather_started[0], True, prepared, "gather_wait_l0")))
    h = lax.dynamic_update_slice(h, _from_owner_blocks(gathered["meta_tokens"], 1), (first_pos, 0))

    full = {}

    def layer_weights(i, kind, after):
        p = "l%d_" % i
        if i > 0:
            gathered.update(zip(layer_big[i], exchange_wait(gather_started[i], True, after, "gather_wait_l%d" % i)))
        w_in = gathered[p + "w_in"]
        if kind == "s5":
            full[i] = dict(s5_prep[i], w_in=w_in, w_glu=gathered[p + "w_glu"].reshape(4, cw, d),
                           w_out=gathered[p + "w_out"].reshape(4, cw, d))
        elif kind == "conv":
            ce = w_in.shape[2]
            nch = 2
            conv_w = _from_owner_blocks(gathered[p + "conv_w"], 1)
            full[i] = dict(
                w_in=w_in, conv_w=jnp.transpose(conv_w.reshape(CONV_K, nch, ce), (1, 0, 2)),
                conv_b=weights[p + "conv_b"].reshape(nch, 1, ce), w_out=gathered[p + "w_out"].reshape(nch, ce, d))
        else:
            gw = w_in.shape[2]
            full[i] = dict(
                w_in=w_in, w_grp=_from_owner_blocks(gathered[p + "w_grp"], 1),
                b_grp=_from_owner_blocks(gathered[p + "b_grp"], 1).reshape(4, 1, gw),
                scale=weights[p + "scale"].reshape(4, 1, gw), w_out=gathered[p + "w_out"].reshape(4, gw, d))
        return full[i]

    saved = {}
    for i, kind in enumerate(LAYER_KINDS):
        p, f, g = "l%d_" % i, layer_weights(i, kind, h), vec("norm%d_g" % i)
        if kind == "s5":
            u, z, xs = s5_fwd1(h, g, f["w_in"], f["bdre"], f["bdim"], p + "fwd_in")
            s = s5_scan_fwd(xs, f["ar"], f["ai"], p + "scan_fwd")
            h_in = h
            h, y, q = s5_fwd3(s, u, z, h, f["cdre"], f["cdim"], f["w_glu"], f["w_out"], f["d_skip"], f["b_glu"], p + "fwd_out")
            saved[i] = (h_in, u, z, s, y, q)
        elif kind == "conv":
            h_new, halos, acts = conv_fwd(h, g, f["w_in"], f["conv_w"], f["conv_b"], f["w_out"], p + "fwd")
            saved[i] = (h, halos, acts)
            h = h_new
        else:
            h_new, halos, acts = pool_fwd(h, g, f["w_in"], f["w_grp"], f["b_grp"], f["scale"], f["w_out"], first_pos, p + "fwd")
            saved[i] = (h, halos, acts)
            h = h_new

    dh, dg_final, loss_tile = loss_head(h, target[0], vec("final_g"), pad_tiles, "loss_head")
    loss = lax.psum(loss_tile[0, 0], ("x", "y", "c"))

    grads = {"final_g": dg_final}
    names = weight_names()
    sh_names = sharded_names()
    replicated = [n for n in names if n not in sh_names]
    vectors = [n for n in replicated if weights[n].ndim == 1]
    matrices = [n for n in replicated if weights[n].ndim > 1]
    rep_names = vectors + matrices

    def owner_blocks(a):
        return a.reshape(N_DEV, -1, a.shape[-1]).astype(BF16)

    def as2d(a):
        return a.reshape(-1, a.shape[-1])

    def pack(tree, which=None):
        flat = [jnp.pad(tree[n].reshape(-1), (0, -tree[n].size % PACK_ALIGN)) for n in (which or rep_names)]
        flat = jnp.concatenate(flat)
        if which is None:
            flat = jnp.pad(flat, (0, -flat.size % (PACK_ROWS * 128)))
        return flat.reshape(-1, 128)

    layer_sharded, scatter_started = {}, {}
    ordered = jnp.zeros((), F32)
    for i in reversed(range(len(LAYER_KINDS))):
        kind = LAYER_KINDS[i]
        p, f, g = "l%d_" % i, full[i], vec("norm%d_g" % i) + ordered
        if kind == "s5":
            h_in, u, z, s, y, q = saved[i]
            dy, dp, dwo, dwg, dbg = s5_bwd3a(dh, y, q, z, f["w_glu"], f["w_out"], f["b_glu"] + ordered, p + "bwd_out")
            d_skip = f["d_skip"]
            if i == 0:
                early_names = [p + "w_glu", p + "w_out"]
                scatter_started["early"] = exchange_start([owner_blocks(dwg), owner_blocks(dwo)], False, dy,
                                                          "scatter_start_l0_early")
                d_skip = d_skip + scatter_started["early"][4][0, 0]
            ds, dus, dcre, dcim, dd = s5_bwd3b(dy, s, u, f["cdre"], f["cdim"], d_skip, p + "bwd_read")
            lam, dar, dai = s5_scan_bwd(ds, s, f["ar"], f["ai"], p + "scan_bwd")
            res = s5_bwd1(lam, dus, u, dp, h_in, dh, g, f["w_in"], f["bdre"], f["bdim"], p + "bwd_in",
                          pad_tiles=pad_tiles if i == 0 else None)
            dp, dh, n, dbre, dbim, dg = res[:6]
            if i == 0:
                grad_x = res[6][None]
            dw_in = grad_w_in(n, dp, f["w_in"].shape[2], p + "grad_w_in")
            grads.update({p + "w_in": dw_in, p + "w_glu": dwg.reshape(N_DEV, -1, d), p + "w_out": dwo.reshape(N_DEV, -1, d),
                          p + "d_skip": dd, p + "b_glu": dbg})

            def replicated_grads(p=p, f=f, dar=dar, dai=dai, dbre=dbre, dbim=dbim, dcre=dcre, dcim=dcim, token=None):
                lr, li, ldt, br_t, bi_t = f["disc"]
                dlr, dli, dldt, dbr_t, dbi_t = s5_disc_bwd(
                    lr, li, ldt, br_t, bi_t, dar.reshape(lr.shape) + token, dai.reshape(lr.shape),
                    _block_diag_in_grad(dbre), _block_diag_in_grad(dbim), p + "disc_bwd")
                grads.update({
                    p + "lam_re": dlr, p + "lam_im": dli, p + "log_dt": dldt,
                    p + "b_re": jnp.transpose(dbr_t, (1, 2, 0)), p + "b_im": jnp.transpose(dbi_t, (1, 2, 0)),
                    p + "c_re": _block_diag_out_grad(dcre), p + "c_im": -_block_diag_out_grad(dcim)})
        elif kind == "conv":
            replicated_grads = None
            h_in, halos, acts = saved[i]
            dh, n, dp, dwo, dcw, dcb, dg = conv_bwd(h_in, dh, halos, acts, g, f["w_in"], f["conv_w"], f["conv_b"], f["w_out"], p + "bwd")
            dw_in = grad_w_in(n, dp, f["w_in"].shape[2], p + "grad_w_in")
            dconv_w = jnp.transpose(dcw[:, :CONV_K, :], (1, 0, 2)).reshape(CONV_K, -1)
            grads.update({p + "w_in": dw_in, p + "conv_w": _to_owner_blocks(dconv_w, 1), p + "conv_b": dcb,
                          p + "w_out": dwo.reshape(N_DEV, -1, d)})
        else:
            replicated_grads = None
            h_in, halos, acts = saved[i]
            dh, n, dp, dwo, dwgrp, dbgrp, dsc, dg = pool_bwd(h_in, dh, halos, acts, g, f["w_in"], f["w_grp"], f["b_grp"], f["scale"],
                                                             f["w_out"], first_pos, p + "bwd")
            dw_in = grad_w_in(n, dp, f["w_in"].shape[2], p + "grad_w_in")
            grads.update({p + "w_in": dw_in, p + "w_grp": _to_owner_blocks(dwgrp, 1),
                          p + "b_grp": _to_owner_blocks(dbgrp.reshape(4, -1), 1), p + "scale": dsc,
                          p + "w_out": dwo.reshape(N_DEV, -1, d)})
        grads["norm%d_g" % i] = dg
        layer_sharded[i] = ["l%d_%s" % (i, n) for n in SHARDED[kind]]
        if i > 0:
            scatter_started[i] = exchange_start([owner_blocks(grads[n]) for n in layer_sharded[i]], False, dh,
                                                "scatter_start_l%d" % i)
            ordered = scatter_started[i][4][0, 0]
        if replicated_grads is not None:
            replicated_grads(token=ordered)
    grads["meta_tokens"] = _to_owner_blocks(dh[first_pos:p0], 1)
    last = len(LAYER_KINDS)
    layer_sharded[last] = ["meta_tokens", "replicated"]
    scatter_started[last] = exchange_start([owner_blocks(grads["meta_tokens"]), pack(grads).reshape(N_DEV, -1, 128)], False,
                                           dh, "scatter_start_replicated")
    layer_sharded["early"] = early_names
    layer_sharded[0] = [n for n in layer_sharded[0] if n not in early_names]

    out = {}
    received = {}
    after = [scatter_started[last][4]]
    for i in list(reversed(range(1, last))) + [last, "early", 0]:
        received.update(zip(layer_sharded[i], exchange_wait(scatter_started[i], False, after, "scatter_wait_%s" % i)))
        updated = []
        for n in layer_sharded[i]:
            if n != "replicated":
                res = sum_adamw(received[n], as2d(weights[n]), as2d(moments_m[n]), as2d(moments_v[n]), "update_" + n)
                out[n] = [r.reshape(weights[n].shape) for r in res]
                updated.append(out[n][0])
        after = updated or after
        if i == last:
            g_sum = sum_parts(received["replicated"], "sum_replicated")
            small_gather = exchange_start([g_sum], True, g_sum, "gather_small_grads_start")
            scatter_started[0] = exchange_start([owner_blocks(grads[n]) for n in layer_sharded[0]], False, small_gather[4],
                                                "scatter_start_l0")
            g_full = exchange_wait(small_gather, True, scatter_started[0][4], "gather_small_grads_wait")[0].reshape(-1, 128)
            offsets, offset = {}, 0
            for n in rep_names:
                offsets[n] = offset
                offset += weights[n].size + (-weights[n].size % PACK_ALIGN)
            pieces = [(offsets[n] // 128, max(weights[n].size // 128, 1), min(weights[n].size, 128)) for n in vectors]
            res = update_packed(g_full, pack(weights, vectors), pack(moments_m, vectors), pack(moments_v, vectors), pieces,
                                "update_replicated_vectors")
            for j, n in enumerate(vectors):
                out[n] = [r.reshape(weights[n].shape) for r in res[4 * j:4 * j + 4]]
            flat = g_full.reshape(-1)
            groups = [(flat[offsets[n]:offsets[n] + weights[n].size].reshape(as2d(weights[n]).shape), as2d(weights[n]),
                       as2d(moments_m[n]), as2d(moments_v[n])) for n in matrices]
            res = update_natural(groups, "update_replicated_matrices")
            for j, n in enumerate(matrices):
                out[n] = [r.reshape(weights[n].shape) for r in res[4 * j:4 * j + 4]]
            after = [out[n][k] for n in rep_names for k in range(4)]

    return (loss, grad_x) + tuple(out[n][k] for k in range(4) for n in names)


def kernel(x, *rest):
    names = weight_names()
    nw = len(names)
    weights = dict(zip(names, rest[:nw]))
    target = rest[nw]
    moments_m = dict(zip(names, rest[nw + 1:2 * nw + 1]))
    moments_v = dict(zip(names, rest[2 * nw + 1:3 * nw + 1]))
    return _step(x, target, weights, moments_m, moments_v)
```

```python
import math

import jax
import jax.numpy as jnp
from jax import lax
from jax.experimental import pallas as pl
from jax.experimental.pallas import tpu as pltpu

F32 = jnp.float32
BF16 = jnp.bfloat16
EPS = 1e-6
N_DEV = 8
TOKEN_TILE = 256
SCAN_CHUNKS = 4
SCAN_UNROLL = 4
S5_GROUP = 16
S5_STATE = 64
POOL_WINDOWS = (2, 4, 8, 16)
POOL_HALO = 16
CONV_K = 3
CONV_HALO = 8
CONV_SPLIT = 4
ADAM_LR = 0.001
ADAM_B1 = 0.9
ADAM_B2 = 0.999
ADAM_EPS = 1e-08
ADAM_WD = 0.01
ADAM_STEP = 10
GELU_C = math.sqrt(2.0 / math.pi)
GELU_A = 0.044715
UPDATE_TILE_ELEMS = 1 << 17
PACK_ROWS = 512
PACK_ALIGN = 8 * 128
HIGH_HALF = -65536
HALF_OF_LOW_HALF = 0x8000
VMEM_LIMIT = 56 << 20
VMEM_LIMIT_LARGE = 62 << 20

ANY = pl.BlockSpec(memory_space=pl.ANY)


def _params(vmem=VMEM_LIMIT, ndim=1):
    return pltpu.CompilerParams(vmem_limit_bytes=vmem, dimension_semantics=("arbitrary",) * ndim)


def _dot(a, b):
    return jnp.dot(a.astype(BF16), b.astype(BF16), preferred_element_type=F32)


def _dot_nt(a, b):
    return lax.dot_general(a.astype(BF16), b.astype(BF16), (((1,), (1,)), ((), ())), preferred_element_type=F32)


def _dot_tn(a, b):
    return lax.dot_general(a.astype(BF16), b.astype(BF16), (((0,), (0,)), ((), ())), preferred_element_type=F32)


def _rms_fwd(h, g):
    r = lax.rsqrt(jnp.mean(h * h, axis=-1, keepdims=True) + EPS)
    hh = h * r
    return hh * g, hh, r


def _rms_bwd(dn, hh, r, g):
    dhh = dn * g
    return r * (dhh - hh * jnp.mean(dhh * hh, axis=-1, keepdims=True))


def _sigmoid(x):
    return 1.0 / (1.0 + jnp.exp(-x))


def _silu_and_grad(z):
    s = _sigmoid(z)
    return z * s, s * (1.0 + z * (1.0 - s))


def _gelu(y):
    t = jnp.tanh(GELU_C * (y + GELU_A * y * y * y))
    return 0.5 * y * (1.0 + t), t


def _gelu_grad(y, t):
    return 0.5 * (1.0 + t) + 0.5 * y * (1.0 - t * t) * GELU_C * (1.0 + 3.0 * GELU_A * y * y)


def _rows(shape):
    return lax.broadcasted_iota(jnp.int32, shape, 0)


def _shift_down(x, k, halo):
    y = pltpu.roll(x, k, 0)
    rows = _rows(x.shape)
    for j in range(k):
        y = jnp.where(rows == j, halo[halo.shape[0] - k + j:halo.shape[0] - k + j + 1, :], y)
    return y


def _shift_up(x, k, halo):
    n = x.shape[0]
    y = pltpu.roll(x, n - k, 0)
    rows = _rows(x.shape)
    for j in range(k):
        y = jnp.where(rows == n - k + j, halo[j:j + 1, :], y)
    return y


def _window_sums_back(ext):
    out = []
    s = ext
    for k in (1, 2, 4, 8):
        s = s + pltpu.roll(s, k, 0)
        out.append(s)
    return out


def _window_sums_fwd(ext):
    n = ext.shape[0]
    out = []
    s = ext
    for k in (1, 2, 4, 8):
        s = s + pltpu.roll(s, n - k, 0)
        out.append(s)
    return out


def _pool_inv_count(tile, tt, first_pos, w, width):
    pos = _rows((tt, width)) + (tile * tt - first_pos + 1)
    return 1.0 / jnp.clip(pos, 1, w).astype(F32)


def _slab_spec(lp, tt, sw):
    nj = sw // 128
    return pl.BlockSpec((4 * tt * nj, 128), lambda i: (i, 0)), (lp * 4 * nj, 128)


def _pack_pair(re, im):
    def rounded(v):
        return lax.bitcast_convert_type(v, jnp.int32) + HALF_OF_LOW_HALF
    return lax.bitcast_convert_type((rounded(re) & HIGH_HALF) | lax.shift_right_logical(rounded(im), 16), F32)


def _unpack_pair(w):
    b = lax.bitcast_convert_type(w, jnp.int32)
    return lax.bitcast_convert_type(b & HIGH_HALF, F32), lax.bitcast_convert_type(lax.shift_left(b, 16), F32)


def _slab_load(ref, c):
    nj = ref.shape[0] // (4 * TOKEN_TILE)
    first = c * TOKEN_TILE * nj
    return _unpack_pair(jnp.concatenate([ref[pl.ds(first + j, TOKEN_TILE, stride=nj), :] for j in range(nj)], axis=1))


def _slab_store(ref, c, re, im):
    nj = ref.shape[0] // (4 * TOKEN_TILE)
    first = c * TOKEN_TILE * nj
    val = _pack_pair(re, im)
    for j in range(nj):
        ref[pl.ds(first + j, TOKEN_TILE, stride=nj), :] = val[:, j * 128:(j + 1) * 128]


def _s5_disc_math(lr, li, ldt, br, bi):
    dt = jnp.exp(ldt)
    mag = jnp.exp(lr * dt)
    ar = mag * jnp.cos(li * dt)
    ai = mag * jnp.sin(li * dt)
    den = lr * lr + li * li
    kr = ((ar - 1.0) * lr + ai * li) / den
    ki = (ai * lr - (ar - 1.0) * li) / den
    bbr = kr[None] * br - ki[None] * bi
    bbi = kr[None] * bi + ki[None] * br
    return ar, ai, bbr, bbi


def s5_disc_fwd(lr, li, ldt, br_t, bi_t, c_re, c_im, name):
    ni, ng, npp = br_t.shape
    gc = ng // 4

    def body(lr_ref, li_ref, ldt_ref, br_ref, bi_ref, cre_ref, cim_ref,
             ar_ref, ai_ref, bdre_ref, bdim_ref, cdre_ref, cdim_ref, bbr_sc, bbi_sc):
        ar, ai, bbr, bbi = _s5_disc_math(lr_ref[...], li_ref[...], ldt_ref[...], br_ref[...], bi_ref[...])
        ar_ref[...] = ar
        ai_ref[...] = ai
        bbr_sc[...] = bbr
        bbi_sc[...] = bbi
        for ref in (bdre_ref, bdim_ref, cdre_ref, cdim_ref):
            ref[...] = jnp.zeros_like(ref)
        for k in range(4):
            for j in range(gc):
                g = k * gc + j
                ins, states = pl.ds(j * ni, ni), pl.ds(j * npp, npp)
                bdre_ref[k, ins, states] = bbr_sc[:, g, :].astype(BF16)
                bdim_ref[k, ins, states] = bbi_sc[:, g, :].astype(BF16)
                cdre_ref[k, states, ins] = cre_ref[g].T.astype(BF16)
                cdim_ref[k, states, ins] = (-cim_ref[g]).T.astype(BF16)

    sd = jax.ShapeDtypeStruct
    return pl.pallas_call(
        body, name=name,
        out_shape=(sd(lr.shape, F32), sd(lr.shape, F32), sd((4, gc * ni, gc * npp), BF16), sd((4, gc * ni, gc * npp), BF16),
                   sd((4, gc * npp, gc * ni), BF16), sd((4, gc * npp, gc * ni), BF16)),
        scratch_shapes=[pltpu.VMEM(br_t.shape, F32), pltpu.VMEM(br_t.shape, F32)],
        compiler_params=pltpu.CompilerParams(vmem_limit_bytes=VMEM_LIMIT),
    )(lr, li, ldt, br_t, bi_t, c_re, c_im)


def s5_disc_bwd(lr, li, ldt, br_t, bi_t, dar, dai, dbbr, dbbi, name):
    def body(lr_ref, li_ref, ldt_ref, br_ref, bi_ref, dar_ref, dai_ref, dbbr_ref, dbbi_ref,
             dlr_ref, dli_ref, dldt_ref, dbr_ref, dbi_ref):
        _, vjp = jax.vjp(_s5_disc_math, lr_ref[...], li_ref[...], ldt_ref[...], br_ref[...], bi_ref[...])
        dlr, dli, dldt, dbr, dbi = vjp((dar_ref[...], dai_ref[...], dbbr_ref[...], dbbi_ref[...]))
        dlr_ref[...] = dlr
        dli_ref[...] = dli
        dldt_ref[...] = dldt
        dbr_ref[...] = dbr
        dbi_ref[...] = dbi

    sd = jax.ShapeDtypeStruct
    return pl.pallas_call(
        body, name=name,
        out_shape=(sd(lr.shape, F32), sd(lr.shape, F32), sd(ldt.shape, F32), sd(br_t.shape, F32), sd(br_t.shape, F32)),
    )(lr, li, ldt, br_t, bi_t, dar, dai, dbbr, dbbi)


def s5_fwd1(h, g, w_in, bdre, bdim, name):
    lp, d = h.shape
    tt = TOKEN_TILE
    cw, sw = bdre.shape[1], bdre.shape[2]

    def body(h_ref, g_ref, w_hbm, bdre_hbm, bdim_hbm, u_ref, z_ref, x_ref, w, bre, bim):
        @pl.when(pl.program_id(0) == 0)
        def _():
            pltpu.sync_copy(w_hbm, w)
            pltpu.sync_copy(bdre_hbm, bre)
            pltpu.sync_copy(bdim_hbm, bim)

        n = _rms_fwd(h_ref[...], g_ref[...])[0].astype(BF16)
        for c in range(4):
            cols = slice(c * cw, (c + 1) * cw)
            u = jnp.dot(n, w[c], preferred_element_type=F32)
            u_ref[:, cols] = u
            z_ref[:, cols] = jnp.dot(n, w[c + 4], preferred_element_type=F32)
            ub = u.astype(BF16)
            _slab_store(x_ref, c, jnp.dot(ub, bre[c], preferred_element_type=F32), jnp.dot(ub, bim[c], preferred_element_type=F32))

    sd = jax.ShapeDtypeStruct
    slab, slab_shape = _slab_spec(lp, tt, sw)
    row = pl.BlockSpec((tt, d), lambda i: (i, 0))
    return pl.pallas_call(
        body, name=name, grid=(lp // tt,),
        in_specs=[row, pl.BlockSpec((1, d), lambda i: (0, 0)), ANY, ANY, ANY],
        out_specs=[row, row, slab],
        out_shape=(sd((lp, d), F32), sd((lp, d), F32), sd(slab_shape, F32)),
        scratch_shapes=[pltpu.VMEM(w_in.shape, BF16), pltpu.VMEM(bdre.shape, BF16), pltpu.VMEM(bdim.shape, BF16)],
        compiler_params=_params(),
    )(h, g, w_in, bdre, bdim)


def s5_scan_fwd(x, ar, ai, name):
    nj = ar.shape[1]
    tt = TOKEN_TILE
    cpb = SCAN_CHUNKS
    nt = x.shape[0] // (4 * tt * nj)

    def body(x_ref, ar_ref, ai_ref, s_ref, st_r, st_i):
        i, cg = pl.program_id(0), pl.program_id(1)

        @pl.when(i == 0)
        def _():
            for q in range(cpb):
                st_r[cg * cpb + q] = jnp.zeros((nj, 128), F32)
                st_i[cg * cpb + q] = jnp.zeros((nj, 128), F32)

        a_r = [ar_ref[cg * cpb + q] for q in range(cpb)]
        a_i = [ai_ref[cg * cpb + q] for q in range(cpb)]

        def step(k, carry):
            carry = list(carry)
            for uu in range(SCAN_UNROLL):
                t = k * SCAN_UNROLL + uu
                for q in range(cpb):
                    s_r, s_i = carry[q]
                    rows = pl.ds(pl.multiple_of((q * tt + t) * nj, nj), nj)
                    x_r, x_i = _unpack_pair(x_ref[rows, :])
                    n_r = a_r[q] * s_r - a_i[q] * s_i + x_r
                    n_i = a_r[q] * s_i + a_i[q] * s_r + x_i
                    s_ref[rows, :] = _pack_pair(n_r, n_i)
                    carry[q] = (n_r, n_i)
            return tuple(carry)

        init = tuple((st_r[cg * cpb + q], st_i[cg * cpb + q]) for q in range(cpb))
        final = lax.fori_loop(0, tt // SCAN_UNROLL, step, init)
        for q in range(cpb):
            st_r[cg * cpb + q] = final[q][0]
            st_i[cg * cpb + q] = final[q][1]

    blk = pl.BlockSpec((cpb * tt * nj, 128), lambda i, cg: (i * (4 // cpb) + cg, 0))
    par = pl.BlockSpec((4, nj, 128), lambda i, cg: (0, 0, 0))
    sd = jax.ShapeDtypeStruct
    return pl.pallas_call(
        body, name=name, grid=(nt, 4 // cpb),
        in_specs=[blk, par, par], out_specs=blk,
        out_shape=sd(x.shape, F32),
        scratch_shapes=[pltpu.VMEM((4, nj, 128), F32), pltpu.VMEM((4, nj, 128), F32)],
        compiler_params=_params(ndim=2),
    )(x, ar, ai)


def s5_fwd3(s, u, z, h, cdre, cdim, w_glu, w_out, d_skip, b_glu, name):
    lp, d = h.shape
    tt = TOKEN_TILE
    sw, cw = cdre.shape[1], cdre.shape[2]

    def body(s_ref, u_ref, z_ref, h_ref, d_ref, bg_ref, cre_hbm, cim_hbm, wg_hbm, wo_hbm,
             o_ref, y_ref, q_ref, cre, cim, wg, wo):
        @pl.when(pl.program_id(0) == 0)
        def _():
            pltpu.sync_copy(cre_hbm, cre)
            pltpu.sync_copy(cim_hbm, cim)
            pltpu.sync_copy(wg_hbm, wg)
            pltpu.sync_copy(wo_hbm, wo)

        gys, q = [], None
        for c in range(4):
            cols = slice(c * cw, (c + 1) * cw)
            s_r, s_i = _slab_load(s_ref, c)
            y = _dot(s_r, cre[c]) + _dot(s_i, cim[c]) + d_ref[c] * u_ref[:, cols]
            y_ref[:, cols] = y
            gys.append(_gelu(y)[0])
            part = _dot(gys[c], wg[c])
            q = part if c == 0 else q + part
        q_ref[...] = q
        sig = _sigmoid(q + bg_ref[...])
        zz = z_ref[...]
        sz = zz * _sigmoid(zz)
        o = h_ref[...]
        for k in range(4):
            cols = slice(k * cw, (k + 1) * cw)
            o = o + _dot(gys[k] * sig[:, cols] * sz[:, cols], wo[k])
        o_ref[...] = o

    row = pl.BlockSpec((tt, d), lambda i: (i, 0))
    slab, _ = _slab_spec(lp, tt, sw)
    sd = jax.ShapeDtypeStruct((lp, d), F32)
    return pl.pallas_call(
        body, name=name, grid=(lp // tt,),
        in_specs=[slab, row, row, row, pl.BlockSpec((4, 1, cw), lambda i: (0, 0, 0)), pl.BlockSpec((1, d), lambda i: (0, 0)),
                  ANY, ANY, ANY, ANY],
        out_specs=[row, row, row],
        out_shape=(sd, sd, sd),
        scratch_shapes=[pltpu.VMEM(cdre.shape, BF16), pltpu.VMEM(cdim.shape, BF16), pltpu.VMEM(w_glu.shape, BF16),
                        pltpu.VMEM(w_out.shape, BF16)],
        compiler_params=_params(),
    )(s, u, z, h, d_skip, b_glu, cdre, cdim, w_glu, w_out)


def s5_bwd3a(dh, y, q, z, w_glu, w_out, b_glu, name):
    lp, d = dh.shape
    tt = TOKEN_TILE
    nt = lp // tt
    cw = w_glu.shape[1]

    def body(dh_ref, y_ref, q_ref, z_ref, bg_ref, wg_hbm, wo_hbm, dy_ref, dp_ref, dwo_hbm, dwg_hbm, dbg_hbm,
             wg, wo, dwo, dwg, dbg):
        i = pl.program_id(0)

        @pl.when(i == 0)
        def _():
            pltpu.sync_copy(wg_hbm, wg)
            pltpu.sync_copy(wo_hbm, wo)
            dwo[...] = jnp.zeros_like(dwo)
            dwg[...] = jnp.zeros_like(dwg)
            dbg[...] = jnp.zeros_like(dbg)

        sig = _sigmoid(q_ref[...] + bg_ref[...])
        sz, dsz = _silu_and_grad(z_ref[...])
        dhv = dh_ref[...]
        yv = y_ref[...]
        gy, t = _gelu(yv)
        dq_parts, dgy_parts = [], []
        for k in range(4):
            cols = slice(k * cw, (k + 1) * cw)
            gy_k, sig_k, sz_k = gy[:, cols], sig[:, cols], sz[:, cols]
            y2 = gy_k * sig_k
            dy3 = _dot_nt(dhv, wo[k])
            dwo[k] += _dot_tn(y2 * sz_k, dhv)
            dy2 = dy3 * sz_k
            dp_ref[0, :, cols] = (dy3 * y2 * dsz[:, cols]).astype(BF16)
            dq_parts.append(dy2 * gy_k * sig_k * (1.0 - sig_k))
            dgy_parts.append(dy2 * sig_k)
        dq = jnp.concatenate(dq_parts, axis=1)
        dbg[...] += jnp.sum(dq, axis=0, keepdims=True)
        dgelu = _gelu_grad(yv, t)
        for k in range(4):
            cols = slice(k * cw, (k + 1) * cw)
            dwg[k] += _dot_tn(gy[:, cols], dq)
            dy_ref[:, cols] = (dgy_parts[k] + _dot_nt(dq, wg[k])) * dgelu[:, cols]

        @pl.when(i == nt - 1)
        def _():
            pltpu.sync_copy(dwo, dwo_hbm)
            pltpu.sync_copy(dwg, dwg_hbm)
            pltpu.sync_copy(dbg, dbg_hbm)

    row = pl.BlockSpec((tt, d), lambda i: (i, 0))
    sd = jax.ShapeDtypeStruct
    return pl.pallas_call(
        body, name=name, grid=(nt,),
        in_specs=[row, row, row, row, pl.BlockSpec((1, d), lambda i: (0, 0)), ANY, ANY],
        out_specs=[row, pl.BlockSpec((1, tt, d), lambda i: (1, i, 0)), ANY, ANY, ANY],
        out_shape=(sd((lp, d), F32), sd((2, lp, d), BF16), sd(w_out.shape, F32), sd(w_glu.shape, F32), sd((1, d), F32)),
        scratch_shapes=[pltpu.VMEM(w_glu.shape, BF16), pltpu.VMEM(w_out.shape, BF16),
                        pltpu.VMEM(w_out.shape, F32), pltpu.VMEM(w_glu.shape, F32), pltpu.VMEM((1, d), F32)],
        compiler_params=_params(),
    )(dh, y, q, z, b_glu, w_glu, w_out)


def s5_bwd3b(dy, s, u, cdre, cdim, d_skip, name):
    lp, d = dy.shape
    tt = TOKEN_TILE
    nt = lp // tt
    sw, cw = cdre.shape[1], cdre.shape[2]
    gc = cw // S5_GROUP

    def body(dy_ref, s_ref, u_ref, d_ref, cre_hbm, cim_hbm,
             ds_ref, dus_ref, dcre_ref, dcim_ref, dd_hbm, cre, cim, dcre, dcim, dd):
        i = pl.program_id(0)

        @pl.when(i == 0)
        def _():
            pltpu.sync_copy(cre_hbm, cre)
            pltpu.sync_copy(cim_hbm, cim)
            dcre[...] = jnp.zeros_like(dcre)
            dcim[...] = jnp.zeros_like(dcim)
            dd[...] = jnp.zeros_like(dd)

        for c in range(4):
            chunk = slice(c * cw, (c + 1) * cw)
            dyv = dy_ref[:, chunk]
            dd[c] += jnp.sum(dyv * u_ref[:, chunk], axis=0, keepdims=True)
            dus_ref[:, chunk] = dyv * d_ref[c]
            _slab_store(ds_ref, c, _dot_nt(dyv, cre[c]), _dot_nt(dyv, cim[c]))
            s_r, s_i = _slab_load(s_ref, c)
            dcre[c] += _dot_tn(s_r, dyv)
            dcim[c] += _dot_tn(s_i, dyv)

        @pl.when(i == nt - 1)
        def _():
            for k in range(4):
                for j in range(gc):
                    rows, cols = pl.ds(j * S5_STATE, S5_STATE), pl.ds(j * S5_GROUP, S5_GROUP)
                    dcre_ref[k, j] = dcre[k, rows, cols].T
                    dcim_ref[k, j] = dcim[k, rows, cols].T
            pltpu.sync_copy(dd, dd_hbm)

    sd = jax.ShapeDtypeStruct
    row = pl.BlockSpec((tt, d), lambda i: (i, 0))
    slab, slab_shape = _slab_spec(lp, tt, sw)
    diag = pl.BlockSpec((4, gc, S5_GROUP, S5_STATE), lambda i: (0, 0, 0, 0))
    return pl.pallas_call(
        body, name=name, grid=(nt,),
        in_specs=[row, slab, row, pl.BlockSpec((4, 1, cw), lambda i: (0, 0, 0)), ANY, ANY],
        out_specs=[slab, row, diag, diag, ANY],
        out_shape=(sd(slab_shape, F32), sd((lp, d), F32),
                   sd((4, gc, S5_GROUP, S5_STATE), F32), sd((4, gc, S5_GROUP, S5_STATE), F32), sd((4, 1, cw), F32)),
        scratch_shapes=[pltpu.VMEM(cdre.shape, BF16), pltpu.VMEM(cdim.shape, BF16),
                        pltpu.VMEM(cdre.shape, F32), pltpu.VMEM(cdim.shape, F32), pltpu.VMEM((4, 1, cw), F32)],
        compiler_params=_params(),
    )(dy, s, u, d_skip, cdre, cdim)


def s5_scan_bwd(g, s, ar, ai, name):
    nj = ar.shape[1]
    tt = TOKEN_TILE
    cpb = SCAN_CHUNKS
    nt = g.shape[0] // (4 * tt * nj)

    def body(g_ref, s_ref, ar_ref, ai_ref, lam_ref, dar_ref, dai_ref, st_r, st_i, acc_r, acc_i):
        i, cg = pl.program_id(0), pl.program_id(1)

        @pl.when((i == 0) & (cg == 0))
        def _():
            for ref in (st_r, st_i, acc_r, acc_i):
                ref[...] = jnp.zeros_like(ref)

        a_r = [ar_ref[cg * cpb + q] for q in range(cpb)]
        a_i = [ai_ref[cg * cpb + q] for q in range(cpb)]

        def slab(q, t):
            return pl.ds(pl.multiple_of((q * tt + t) * nj, nj), nj)

        def adjoint(q, t, l_r, l_i):
            rows = slab(q, t)
            g_r, g_i = _unpack_pair(g_ref[rows, :])
            n_r = g_r + a_r[q] * l_r + a_i[q] * l_i
            n_i = g_i + a_r[q] * l_i - a_i[q] * l_r
            lam_ref[rows, :] = _pack_pair(n_r, n_i)
            return n_r, n_i

        def pair(q, t, l_r, l_i, d_r, d_i):
            p_r, p_i = _unpack_pair(s_ref[slab(q, t), :])
            return d_r + l_r * p_r + l_i * p_i, d_i + l_i * p_r - l_r * p_i

        def step(k, carry):
            carry = list(carry)
            for uu in range(SCAN_UNROLL):
                t = tt - 1 - (k * SCAN_UNROLL + uu)
                for q in range(cpb):
                    l_r, l_i, d_r, d_i = carry[q]
                    d_r, d_i = pair(q, t, l_r, l_i, d_r, d_i)
                    l_r, l_i = adjoint(q, t, l_r, l_i)
                    carry[q] = (l_r, l_i, d_r, d_i)
            return tuple(carry)

        init = tuple((st_r[cg * cpb + q], st_i[cg * cpb + q], acc_r[cg * cpb + q], acc_i[cg * cpb + q]) for q in range(cpb))
        final = lax.fori_loop(0, tt // SCAN_UNROLL, step, init)
        for q in range(cpb):
            ch = cg * cpb + q
            l_r, l_i, d_r, d_i = final[q]
            st_r[ch] = l_r
            st_i[ch] = l_i
            acc_r[ch] = d_r
            acc_i[ch] = d_i
            dar_ref[ch] = d_r
            dai_ref[ch] = d_i

    blk = pl.BlockSpec((cpb * tt * nj, 128), lambda i, cg: ((nt - 1 - i) * (4 // cpb) + cg, 0))
    par = pl.BlockSpec((4, nj, 128), lambda i, cg: (0, 0, 0))
    sd = jax.ShapeDtypeStruct
    return pl.pallas_call(
        body, name=name, grid=(nt, 4 // cpb),
        in_specs=[blk, blk, par, par], out_specs=[blk, par, par],
        out_shape=(sd(g.shape, F32), sd((4, nj, 128), F32), sd((4, nj, 128), F32)),
        scratch_shapes=[pltpu.VMEM((4, nj, 128), F32)] * 4,
        compiler_params=_params(ndim=2),
    )(g, s, ar, ai)


def s5_bwd1(lam, dus, u, dp, h, dh, g, w_in, bdre, bdim, name, pad_tiles=None):
    lp, d = h.shape
    tt = TOKEN_TILE
    nt = lp // tt
    cw, sw = bdre.shape[1], bdre.shape[2]
    gc = cw // S5_GROUP

    def body(lam_ref, dus_ref, u_ref, dpz_ref, h_ref, dh_ref, g_ref, w_hbm, bre_hbm, bim_hbm,
             dpu_ref, dho_ref, n_ref, dbre_ref, dbim_ref, dg_hbm, *rest):
        (gx_ref,), (w, bre, bim, dbre, dbim, dg) = (rest[:1], rest[1:]) if pad_tiles is not None else ((None,), rest)
        i = pl.program_id(0)

        @pl.when(i == 0)
        def _():
            pltpu.sync_copy(w_hbm, w)
            pltpu.sync_copy(bre_hbm, bre)
            pltpu.sync_copy(bim_hbm, bim)
            dbre[...] = jnp.zeros_like(dbre)
            dbim[...] = jnp.zeros_like(dbim)
            dg[...] = jnp.zeros_like(dg)

        dz = dpz_ref[0]
        dn = None
        for c in range(4):
            chunk = slice(c * cw, (c + 1) * cw)
            (l_r, l_i), uv = _slab_load(lam_ref, c), u_ref[:, chunk]
            du = dus_ref[:, chunk] + _dot_nt(l_r, bre[c]) + _dot_nt(l_i, bim[c])
            dbre[c] += _dot_tn(uv, l_r)
            dbim[c] += _dot_tn(uv, l_i)
            dpu_ref[0, :, chunk] = du.astype(BF16)
            part = _dot_nt(du, w[c]) + _dot_nt(dz[:, chunk], w[4 + c])
            dn = part if c == 0 else dn + part
        gv = g_ref[...]
        n, hh, rr = _rms_fwd(h_ref[...], gv)
        n_ref[...] = n.T.astype(BF16)
        dg[...] += jnp.sum(dn * hh, axis=0, keepdims=True)
        dh_in = dh_ref[...] + _rms_bwd(dn, hh, rr, gv)
        dho_ref[...] = dh_in
        if pad_tiles is not None:
            @pl.when(i >= pad_tiles)
            def _():
                gx_ref[...] = dh_in

        @pl.when(i == nt - 1)
        def _():
            for k in range(4):
                for j in range(gc):
                    rows, cols = pl.ds(j * S5_GROUP, S5_GROUP), pl.ds(j * S5_STATE, S5_STATE)
                    dbre_ref[k, j] = dbre[k, rows, cols]
                    dbim_ref[k, j] = dbim[k, rows, cols]
            pltpu.sync_copy(dg, dg_hbm)

    sd = jax.ShapeDtypeStruct
    row = pl.BlockSpec((tt, d), lambda i: (i, 0))
    slab, _ = _slab_spec(lp, tt, sw)
    diag = pl.BlockSpec((4, gc, S5_GROUP, S5_STATE), lambda i: (0, 0, 0, 0))
    extra_specs, extra_shapes = [], ()
    if pad_tiles is not None:
        extra_specs = [pl.BlockSpec((tt, d), lambda i: (jnp.maximum(i - pad_tiles, 0), 0))]
        extra_shapes = (sd((lp - pad_tiles * tt, d), F32),)
    return pl.pallas_call(
        body, name=name, grid=(nt,),
        in_specs=[slab, row, row, pl.BlockSpec((1, tt, d), lambda i: (1, i, 0)), row, row, pl.BlockSpec((1, d), lambda i: (0, 0)),
                  ANY, ANY, ANY],
        out_specs=[pl.BlockSpec((1, tt, d), lambda i: (0, i, 0)), row, pl.BlockSpec((d, tt), lambda i: (0, i)), diag, diag, ANY]
        + extra_specs,
        out_shape=(sd(dp.shape, BF16), sd((lp, d), F32), sd((d, lp), BF16),
                   sd((4, gc, S5_GROUP, S5_STATE), F32), sd((4, gc, S5_GROUP, S5_STATE), F32), sd((1, d), F32)) + extra_shapes,
        input_output_aliases={3: 0},
        scratch_shapes=[pltpu.VMEM(w_in.shape, BF16), pltpu.VMEM(bdre.shape, BF16), pltpu.VMEM(bdim.shape, BF16),
                        pltpu.VMEM(bdre.shape, F32), pltpu.VMEM(bdim.shape, F32), pltpu.VMEM((1, d), F32)],
        compiler_params=_params(),
    )(lam, dus, u, dp, h, dh, g, w_in, bdre, bdim)


def grad_w_in(n_t, dp, blk, name):
    d, lp = n_t.shape
    npart, _, width = dp.shape
    per = width // blk

    def body(n_ref, dp_ref, o_ref):
        o_ref[0] = jnp.dot(n_ref[...], dp_ref[0], preferred_element_type=F32).astype(o_ref.dtype)

    return pl.pallas_call(
        body, name=name, grid=(npart * per,),
        in_specs=[pl.BlockSpec((d, lp), lambda j: (0, 0), pipeline_mode=pl.Buffered(1)),
                  pl.BlockSpec((1, lp, blk), lambda j: (j // per, 0, j % per))],
        out_specs=pl.BlockSpec((1, d, blk), lambda j: (j, 0, 0)),
        out_shape=jax.ShapeDtypeStruct((npart * per, d, blk), BF16),
        compiler_params=_params(),
    )(n_t, dp)


def _conv_mix(cg, v, taps, bias, halo):
    hc = cg * v
    conv = taps[2:3, :] * hc + taps[1:2, :] * _shift_down(hc, 1, halo) + taps[0:1, :] * _shift_down(hc, 2, halo) + bias
    return hc, conv


def conv_fwd(h, g, w_in, conv_w, conv_b, w_out, name):
    lp, d = h.shape
    tt = TOKEN_TILE
    nt = lp // tt
    nch, ce = w_out.shape[0], w_out.shape[1]

    def body(h_ref, g_ref, cw_ref, cb_ref, w_hbm, wo_hbm, o_ref, halo_ref, acts_ref, w, wo, halo):
        i = pl.program_id(0)

        @pl.when(i == 0)
        def _():
            pltpu.sync_copy(w_hbm, w)
            pltpu.sync_copy(wo_hbm, wo)
            halo[...] = jnp.zeros_like(halo)

        hv = h_ref[...]
        n = _rms_fwd(hv, g_ref[...])[0].astype(BF16)
        o = hv
        for c in range(nch):
            cols = slice(c * ce, (c + 1) * ce)
            bg, cg, v, z = [jnp.dot(n, w[p * nch + c], preferred_element_type=F32) for p in range(4)]
            for p, val in enumerate((bg, cg, v, z)):
                acts_ref[p, :, cols] = val.astype(BF16)
            hc, conv = _conv_mix(cg, v, cw_ref[c], cb_ref[c], halo[c])
            o = o + _dot(bg * conv * (z * _sigmoid(z)), wo[c])
            halo[c] = hc[tt - CONV_HALO:, :]
            halo_ref[0, c] = hc[tt - CONV_HALO:, :]
        o_ref[...] = o

    sd = jax.ShapeDtypeStruct
    return pl.pallas_call(
        body, name=name, grid=(nt,),
        in_specs=[pl.BlockSpec((tt, d), lambda i: (i, 0)), pl.BlockSpec((1, d), lambda i: (0, 0)),
                  pl.BlockSpec(conv_w.shape, lambda i: (0, 0, 0)), pl.BlockSpec(conv_b.shape, lambda i: (0, 0, 0)), ANY, ANY],
        out_specs=[pl.BlockSpec((tt, d), lambda i: (i, 0)), pl.BlockSpec((1, nch, CONV_HALO, ce), lambda i: (i, 0, 0, 0)),
                   pl.BlockSpec((4, tt, nch * ce), lambda i: (0, i, 0))],
        out_shape=(sd((lp, d), F32), sd((nt, nch, CONV_HALO, ce), F32), sd((4, lp, nch * ce), BF16)),
        scratch_shapes=[pltpu.VMEM(w_in.shape, BF16), pltpu.VMEM(w_out.shape, BF16), pltpu.VMEM((nch, CONV_HALO, ce), F32)],
        compiler_params=_params(),
    )(h, g, conv_w, conv_b, w_in, w_out)


def conv_bwd(h, dh, halos, acts, g, w_in, conv_w, conv_b, w_out, name):
    lp, d = h.shape
    tt = TOKEN_TILE
    nt = lp // tt
    nch, ce = w_out.shape[0], w_out.shape[1]

    def body(h_ref, dh_ref, halo_ref, acts_ref, g_ref, cw_ref, cb_ref, w_hbm, wo_hbm,
             dho_ref, n_ref, dp_ref, dwo_hbm, dcw_hbm, dcb_hbm, dg_hbm, w, wo, nxt, dwo, dcw, dcb, dg):
        i = pl.program_id(0)

        @pl.when(i == 0)
        def _():
            pltpu.sync_copy(w_hbm, w)
            pltpu.sync_copy(wo_hbm, wo)
            for ref in (nxt, dwo, dcw, dcb, dg):
                ref[...] = jnp.zeros_like(ref)

        gv = g_ref[...]
        nf, hh, rr = _rms_fwd(h_ref[...], gv)
        n_ref[...] = nf.T.astype(BF16)
        dhv = dh_ref[...]
        has_prev = (i < nt - 1).astype(F32)
        dn = jnp.zeros((tt, d), F32)
        hw = ce // CONV_SPLIT
        for c, part in [(c, part) for c in range(nch) for part in range(CONV_SPLIT)]:
            sub = slice(part * hw, (part + 1) * hw)
            cols = slice(c * ce + part * hw, c * ce + (part + 1) * hw)
            halo = halo_ref[0, c, :, sub] * has_prev
            bg, cg, v, z = [acts_ref[p, :, cols].astype(F32) for p in range(4)]
            taps = cw_ref[c, :, sub]
            hc, conv = _conv_mix(cg, v, taps, cb_ref[c, :, sub], halo)
            sz, dsz = _silu_and_grad(z)
            y1 = bg * conv
            dy2 = _dot_nt(dhv, wo[c, sub, :])
            dwo[c, sub, :] += _dot_tn(y1 * sz, dhv)
            dy1 = dy2 * sz
            dz = dy2 * y1 * dsz
            dbg = dy1 * conv
            dconv = dy1 * bg
            dcb[c, :, sub] += jnp.sum(dconv, axis=0, keepdims=True)
            up1 = _shift_up(dconv, 1, nxt[c, :, sub])
            up2 = _shift_up(dconv, 2, nxt[c, :, sub])
            nxt[c, :, sub] = dconv[:CONV_HALO, :]
            dhc = taps[2:3, :] * dconv + taps[1:2, :] * up1 + taps[0:1, :] * up2
            dcw[c, 0:1, sub] += jnp.sum(hc * up2, axis=0, keepdims=True)
            dcw[c, 1:2, sub] += jnp.sum(hc * up1, axis=0, keepdims=True)
            dcw[c, 2:3, sub] += jnp.sum(hc * dconv, axis=0, keepdims=True)
            dcg = dhc * v
            dv = dhc * cg
            for p, val in enumerate((dbg, dcg, dv, dz)):
                dp_ref[p, :, cols] = val.astype(BF16)
                dn = dn + _dot_nt(val, w[p * nch + c, :, sub])
        dg[...] += jnp.sum(dn * hh, axis=0, keepdims=True)
        dho_ref[...] = dhv + _rms_bwd(dn, hh, rr, gv)

        @pl.when(i == nt - 1)
        def _():
            pltpu.sync_copy(dwo, dwo_hbm)
            pltpu.sync_copy(dcw, dcw_hbm)
            pltpu.sync_copy(dcb, dcb_hbm)
            pltpu.sync_copy(dg, dg_hbm)

    rev = lambda i: (nt - 1 - i, 0)
    sd = jax.ShapeDtypeStruct
    return pl.pallas_call(
        body, name=name, grid=(nt,),
        in_specs=[pl.BlockSpec((tt, d), rev), pl.BlockSpec((tt, d), rev),
                  pl.BlockSpec((1, nch, CONV_HALO, ce), lambda i: (jnp.maximum(nt - 2 - i, 0), 0, 0, 0)),
                  pl.BlockSpec((4, tt, nch * ce), lambda i: (0, nt - 1 - i, 0)),
                  pl.BlockSpec((1, d), lambda i: (0, 0)),
                  pl.BlockSpec(conv_w.shape, lambda i: (0, 0, 0)), pl.BlockSpec(conv_b.shape, lambda i: (0, 0, 0)), ANY, ANY],
        out_specs=[pl.BlockSpec((tt, d), rev), pl.BlockSpec((d, tt), lambda i: (0, nt - 1 - i)),
                   pl.BlockSpec((4, tt, nch * ce), lambda i: (0, nt - 1 - i, 0)), ANY, ANY, ANY, ANY],
        out_shape=(sd((lp, d), F32), sd((d, lp), BF16), sd((4, lp, nch * ce), BF16),
                   sd(w_out.shape, F32), sd((nch, 8, ce), F32), sd((nch, 1, ce), F32), sd((1, d), F32)),
        scratch_shapes=[pltpu.VMEM(w_in.shape, BF16), pltpu.VMEM(w_out.shape, BF16), pltpu.VMEM((nch, CONV_HALO, ce), F32),
                        pltpu.VMEM(w_out.shape, F32), pltpu.VMEM((nch, 8, ce), F32), pltpu.VMEM((nch, 1, ce), F32),
                        pltpu.VMEM((1, d), F32)],
        compiler_params=_params(vmem=VMEM_LIMIT_LARGE),
    )(h, dh, halos, acts, g, conv_w, conv_b, w_in, w_out)


def _pool_mix(u, wg, bg_ref, sc_ref, halo, k, tile, tt, first_pos):
    ext = jnp.concatenate([halo, u], axis=0)
    win = _window_sums_back(ext)[k][POOL_HALO:, :]
    mixed = win * _pool_inv_count(tile, tt, first_pos, POOL_WINDOWS[k], u.shape[1]) - u
    outs = _dot(mixed, wg[k]) + bg_ref[k]
    return mixed, outs, outs * sc_ref[k]


def pool_fwd(h, g, w_in, w_grp, b_grp, scale, w_out, first_pos, name):
    lp, d = h.shape
    tt = TOKEN_TILE
    nt = lp // tt
    gw = w_grp.shape[1]

    def body(h_ref, g_ref, bg_ref, sc_ref, w_hbm, wg_hbm, wo_hbm, o_ref, halo_ref, acts_ref, w, wg, wo, halo):
        i = pl.program_id(0)

        @pl.when(i == 0)
        def _():
            pltpu.sync_copy(w_hbm, w)
            pltpu.sync_copy(wg_hbm, wg)
            pltpu.sync_copy(wo_hbm, wo)
            halo[...] = jnp.zeros_like(halo)

        hv = h_ref[...]
        n = _rms_fwd(hv, g_ref[...])[0].astype(BF16)
        o = hv
        for k in range(4):
            cols = slice(k * gw, (k + 1) * gw)
            u = jnp.dot(n, w[k], preferred_element_type=F32)
            z = jnp.dot(n, w[4 + k], preferred_element_type=F32)
            acts_ref[0, :, cols] = u.astype(BF16)
            acts_ref[1, :, cols] = z.astype(BF16)
            _, _, yp = _pool_mix(u, wg, bg_ref, sc_ref, halo[k], k, i, tt, first_pos)
            o = o + _dot(yp * (z * _sigmoid(z)), wo[k])
            halo[k] = u[tt - POOL_HALO:, :]
            halo_ref[0, k] = u[tt - POOL_HALO:, :]
        o_ref[...] = o

    sd = jax.ShapeDtypeStruct
    small = pl.BlockSpec((4, 1, gw), lambda i: (0, 0, 0))
    return pl.pallas_call(
        body, name=name, grid=(nt,),
        in_specs=[pl.BlockSpec((tt, d), lambda i: (i, 0)), pl.BlockSpec((1, d), lambda i: (0, 0)), small, small, ANY, ANY, ANY],
        out_specs=[pl.BlockSpec((tt, d), lambda i: (i, 0)), pl.BlockSpec((1, 4, POOL_HALO, gw), lambda i: (i, 0, 0, 0)),
                   pl.BlockSpec((2, tt, 4 * gw), lambda i: (0, i, 0))],
        out_shape=(sd((lp, d), F32), sd((nt, 4, POOL_HALO, gw), F32), sd((2, lp, 4 * gw), BF16)),
        scratch_shapes=[pltpu.VMEM(w_in.shape, BF16), pltpu.VMEM(w_grp.shape, BF16), pltpu.VMEM(w_out.shape, BF16),
                        pltpu.VMEM((4, POOL_HALO, gw), F32)],
        compiler_params=_params(),
    )(h, g, b_grp, scale, w_in, w_grp, w_out)


def pool_bwd(h, dh, halos, acts, g, w_in, w_grp, b_grp, scale, w_out, first_pos, name):
    lp, d = h.shape
    tt = TOKEN_TILE
    nt = lp // tt
    gw = w_grp.shape[1]

    def body(h_ref, dh_ref, halo_ref, acts_ref, g_ref, bg_ref, sc_ref, w_hbm, wg_hbm, wo_hbm,
             dho_ref, n_ref, dp_ref, dwo_hbm, dwg_hbm, dbg_hbm, dsc_hbm, dg_hbm,
             w, wg, wo, nxt, dwo, dwg, dbg, dsc, dg):
        i = pl.program_id(0)
        tile = nt - 1 - i

        @pl.when(i == 0)
        def _():
            pltpu.sync_copy(w_hbm, w)
            pltpu.sync_copy(wg_hbm, wg)
            pltpu.sync_copy(wo_hbm, wo)
            for ref in (nxt, dwo, dwg, dbg, dsc, dg):
                ref[...] = jnp.zeros_like(ref)

        gv = g_ref[...]
        nf, hh, rr = _rms_fwd(h_ref[...], gv)
        n_ref[...] = nf.T.astype(BF16)
        dhv = dh_ref[...]
        has_prev = (i < nt - 1).astype(F32)
        dn = jnp.zeros((tt, d), F32)
        for k in range(4):
            cols = slice(k * gw, (k + 1) * gw)
            u, z = acts_ref[0, :, cols].astype(F32), acts_ref[1, :, cols].astype(F32)
            mixed, outs, yp = _pool_mix(u, wg, bg_ref, sc_ref, halo_ref[0, k] * has_prev, k, tile, tt, first_pos)
            sz, dsz = _silu_and_grad(z)
            dy = _dot_nt(dhv, wo[k])
            dwo[k] += _dot_tn(yp * sz, dhv)
            dyp = dy * sz
            dz = dy * yp * dsz
            dsc[k] += jnp.sum(dyp * outs, axis=0, keepdims=True)
            douts = dyp * sc_ref[k]
            dbg[k] += jnp.sum(douts, axis=0, keepdims=True)
            dwg[k] += _dot_tn(mixed, douts)
            dmixed = _dot_nt(douts, wg[k])
            dm = dmixed * _pool_inv_count(tile, tt, first_pos, POOL_WINDOWS[k], gw)
            ext = jnp.concatenate([dm, nxt[k]], axis=0)
            du = _window_sums_fwd(ext)[k][:tt, :] - dmixed
            nxt[k] = dm[:POOL_HALO, :]
            dp_ref[0, :, cols] = du.astype(BF16)
            dp_ref[1, :, cols] = dz.astype(BF16)
            dn = dn + _dot_nt(du, w[k]) + _dot_nt(dz, w[4 + k])
        dg[...] += jnp.sum(dn * hh, axis=0, keepdims=True)
        dho_ref[...] = dhv + _rms_bwd(dn, hh, rr, gv)

        @pl.when(i == nt - 1)
        def _():
            pltpu.sync_copy(dwo, dwo_hbm)
            pltpu.sync_copy(dwg, dwg_hbm)
            pltpu.sync_copy(dbg, dbg_hbm)
            pltpu.sync_copy(dsc, dsc_hbm)
            pltpu.sync_copy(dg, dg_hbm)

    rev = lambda i: (nt - 1 - i, 0)
    sd = jax.ShapeDtypeStruct
    small = pl.BlockSpec((4, 1, gw), lambda i: (0, 0, 0))
    return pl.pallas_call(
        body, name=name, grid=(nt,),
        in_specs=[pl.BlockSpec((tt, d), rev), pl.BlockSpec((tt, d), rev),
                  pl.BlockSpec((1, 4, POOL_HALO, gw), lambda i: (jnp.maximum(nt - 2 - i, 0), 0, 0, 0)),
                  pl.BlockSpec((2, tt, 4 * gw), lambda i: (0, nt - 1 - i, 0)),
                  pl.BlockSpec((1, d), lambda i: (0, 0)), small, small, ANY, ANY, ANY],
        out_specs=[pl.BlockSpec((tt, d), rev), pl.BlockSpec((d, tt), lambda i: (0, nt - 1 - i)),
                   pl.BlockSpec((2, tt, 4 * gw), lambda i: (0, nt - 1 - i, 0)), ANY, ANY, ANY, ANY, ANY],
        out_shape=(sd((lp, d), F32), sd((d, lp), BF16), sd((2, lp, 4 * gw), BF16),
                   sd(w_out.shape, F32), sd(w_grp.shape, F32), sd((4, 1, gw), F32), sd((4, 1, gw), F32), sd((1, d), F32)),
        scratch_shapes=[pltpu.VMEM(w_in.shape, BF16), pltpu.VMEM(w_grp.shape, BF16), pltpu.VMEM(w_out.shape, BF16),
                        pltpu.VMEM((4, POOL_HALO, gw), F32), pltpu.VMEM(w_out.shape, F32), pltpu.VMEM(w_grp.shape, F32),
                        pltpu.VMEM((4, 1, gw), F32), pltpu.VMEM((4, 1, gw), F32), pltpu.VMEM((1, d), F32)],
        compiler_params=_params(),
    )(h, dh, halos, acts, g, b_grp, scale, w_in, w_grp, w_out)


def loss_head(h, target, g, pad_tiles, name):
    lp, d = h.shape
    tt = TOKEN_TILE
    nt = lp // tt

    def body(h_ref, t_ref, g_ref, dh_ref, dg_ref, loss_ref, acc):
        i = pl.program_id(0)

        @pl.when(i == 0)
        def _():
            acc[...] = jnp.zeros_like(acc)
            dg_ref[...] = jnp.zeros_like(dg_ref)

        @pl.when(i < pad_tiles)
        def _():
            dh_ref[...] = jnp.zeros_like(dh_ref)

        @pl.when(i >= pad_tiles)
        def _():
            gv = g_ref[...]
            n, hh, rr = _rms_fwd(h_ref[...], gv)
            err = n - t_ref[...]
            acc[...] += 0.5 * jnp.sum(jnp.mean(err * err, axis=-1, keepdims=True), axis=0, keepdims=True)
            dn = err * (1.0 / d)
            dg_ref[...] += jnp.sum(dn * hh, axis=0, keepdims=True)
            dh_ref[...] = _rms_bwd(dn, hh, rr, gv)

        loss_ref[...] = jnp.broadcast_to(acc[...], loss_ref.shape)

    sd = jax.ShapeDtypeStruct
    return pl.pallas_call(
        body, name=name, grid=(nt,),
        in_specs=[pl.BlockSpec((tt, d), lambda i: (i, 0)), pl.BlockSpec((tt, d), lambda i: (jnp.maximum(i - pad_tiles, 0), 0)),
                  pl.BlockSpec((1, d), lambda i: (0, 0))],
        out_specs=[pl.BlockSpec((tt, d), lambda i: (i, 0)), pl.BlockSpec((1, d), lambda i: (0, 0)),
                   pl.BlockSpec((8, 128), lambda i: (0, 0))],
        out_shape=(sd((lp, d), F32), sd((1, d), F32), sd((8, 128), F32)),
        scratch_shapes=[pltpu.VMEM((1, 1), F32)],
        compiler_params=_params(),
    )(h, target, g)


def _peers(x, y, c):
    out = []
    for k in range(1, N_DEV):
        px = 1 - x if k & 4 else x
        py = 1 - y if k & 2 else y
        pc = 1 - c if k & 1 else c
        out.append((k, (px, py, pc), 4 * px + 2 * py + pc))
    return out


def exchange_start(arrs, gather, after, name):
    n = len(arrs)
    me = 4 * lax.axis_index("x") + 2 * lax.axis_index("y") + lax.axis_index("c")
    lands = []
    for a in arrs:
        own = a[None] if gather else lax.dynamic_index_in_dim(a, me, 0, keepdims=True)
        lands.append(lax.dynamic_update_index_in_dim(lax.empty(((N_DEV,) + a.shape) if gather else a.shape, a.dtype), own, me, 0))

    def body(*refs):
        ins, land = refs[:n], refs[n:2 * n]
        send_sems, recv_sems, token = refs[2 * n + 1], refs[2 * n + 2], refs[4 * n + 3]
        x, y, c = lax.axis_index("x"), lax.axis_index("y"), lax.axis_index("c")
        me = 4 * x + 2 * y + c
        for k, pid, peer in _peers(x, y, c):
            for a in range(n):
                pltpu.make_async_remote_copy(
                    src_ref=ins[a] if gather else ins[a].at[peer], dst_ref=land[a].at[me],
                    send_sem=send_sems.at[a * (N_DEV - 1) + k - 1], recv_sem=recv_sems.at[a * (N_DEV - 1) + k - 1],
                    device_id=pid, device_id_type=pl.DeviceIdType.MESH).start()
        token[...] = jnp.zeros_like(token)

    hbm = pl.BlockSpec(memory_space=pltpu.HBM)
    sem = pl.BlockSpec(memory_space=pltpu.SEMAPHORE)
    sems = pltpu.SemaphoreType.DMA((n * (N_DEV - 1),))
    res = pl.pallas_call(
        body, name=name, in_specs=[hbm] * (2 * n) + [ANY],
        out_specs=[sem, sem] + [hbm] * (2 * n) + [pl.BlockSpec(memory_space=pltpu.VMEM)],
        out_shape=[sems, sems] + [pltpu.HBM(a.shape, a.dtype) for a in arrs] + [pltpu.HBM(l.shape, l.dtype) for l in lands]
        + [jax.ShapeDtypeStruct((8, 128), F32)],
        input_output_aliases={a: 2 + a for a in range(2 * n)},
        compiler_params=pltpu.CompilerParams(has_side_effects=pltpu.SideEffectType.DATAFLOW_SIDE_EFFECTING),
    )(*[pltpu.with_memory_space_constraint(a, pltpu.HBM) for a in list(arrs) + lands], after)
    return res[0], res[1], res[2:2 + n], res[2 + n:2 + 2 * n], res[-1]


def exchange_wait(started, gather, after, name):
    send_sems, recv_sems, srcs, lands, _ = started
    n = len(srcs)
    after = list(after) if isinstance(after, (list, tuple)) else [after]

    def body(*refs):
        ins, land = refs[:n], refs[n:2 * n]
        send_sems, recv_sems = refs[2 * n], refs[2 * n + 1]
        x, y, c = lax.axis_index("x"), lax.axis_index("y"), lax.axis_index("c")
        for k, pid, peer in _peers(x, y, c):
            for a in range(n):
                cp = pltpu.make_async_remote_copy(
                    src_ref=ins[a] if gather else ins[a].at[peer], dst_ref=land[a].at[peer],
                    send_sem=send_sems.at[a * (N_DEV - 1) + k - 1], recv_sem=recv_sems.at[a * (N_DEV - 1) + k - 1],
                    device_id=pid, device_id_type=pl.DeviceIdType.MESH)
                cp.wait_send()
                cp.wait_recv()

    hbm = pl.BlockSpec(memory_space=pltpu.HBM)
    sem = pl.BlockSpec(memory_space=pltpu.SEMAPHORE)
    res = pl.pallas_call(
        body, name=name, in_specs=[hbm] * (2 * n) + [sem, sem] + [ANY] * len(after),
        out_specs=[hbm] * (2 * n),
        out_shape=[pltpu.HBM(a.shape, a.dtype) for a in list(srcs) + list(lands)],
        input_output_aliases={a: a for a in range(2 * n)},
        compiler_params=pltpu.CompilerParams(has_side_effects=pltpu.SideEffectType.DATAFLOW_SIDE_EFFECTING),
    )(*srcs, *lands, send_sems, recv_sems, *after)
    return res[n:]


def _adamw(w, g, m, v):
    m = ADAM_B1 * m + (1.0 - ADAM_B1) * g
    v = ADAM_B2 * v + (1.0 - ADAM_B2) * (g * g)
    m_hat = m / (1.0 - ADAM_B1 ** ADAM_STEP)
    v_hat = v / (1.0 - ADAM_B2 ** ADAM_STEP)
    return -ADAM_LR * (m_hat / (jnp.sqrt(v_hat) + ADAM_EPS) + ADAM_WD * w), m, v


def _update_tile_rows(rows, cols):
    if rows * cols <= UPDATE_TILE_ELEMS:
        return rows
    return max(t for t in range(8, UPDATE_TILE_ELEMS // cols + 1, 8) if rows % t == 0)


def _sum_in_order(p_ref):
    g = p_ref[0].astype(F32)
    for j in range(1, p_ref.shape[0]):
        g = g + p_ref[j].astype(F32)
    return g


def sum_parts(parts, name):
    nparts, rows, cols = parts.shape
    tr = _update_tile_rows(rows, cols)

    def body(p_ref, g_ref):
        g_ref[...] = _sum_in_order(p_ref)

    return pl.pallas_call(
        body, name=name, grid=(rows // tr,),
        in_specs=[pl.BlockSpec((nparts, tr, cols), lambda i: (0, i, 0))],
        out_specs=pl.BlockSpec((tr, cols), lambda i: (i, 0)), out_shape=jax.ShapeDtypeStruct((rows, cols), F32),
        compiler_params=_params(),
    )(parts)


def sum_adamw(parts, w, m, v, name):
    rows, cols = w.shape
    nparts = parts.shape[0]
    tr = _update_tile_rows(rows, cols)

    def body(p_ref, w_ref, m_ref, v_ref, g_ref, d_ref, nm_ref, nv_ref):
        g = _sum_in_order(p_ref)
        delta, nm, nv = _adamw(w_ref[...], g, m_ref[...], v_ref[...])
        g_ref[...] = g
        d_ref[...] = delta
        nm_ref[...] = nm
        nv_ref[...] = nv

    blk = pl.BlockSpec((tr, cols), lambda i: (i, 0))
    sd = jax.ShapeDtypeStruct((rows, cols), F32)
    return pl.pallas_call(
        body, name=name, grid=(rows // tr,),
        in_specs=[pl.BlockSpec((nparts, tr, cols), lambda i: (0, i, 0)), blk, blk, blk],
        out_specs=[blk] * 4, out_shape=(sd,) * 4,
        compiler_params=_params(),
    )(parts, w, m, v)


def update_packed(g, w, m, v, pieces, name):
    rows_all = w.shape[0]

    def body(g_ref, w_ref, m_ref, v_ref, *outs):
        gv = g_ref[:rows_all, :]
        res = (gv,) + _adamw(w_ref[...], gv, m_ref[...], v_ref[...])
        for p, (row, rows, lanes) in enumerate(pieces):
            for k in range(4):
                outs[4 * p + k][...] = res[k][row:row + rows, :lanes]

    shapes = [jax.ShapeDtypeStruct((rows, lanes), F32) for _, rows, lanes in pieces for _ in range(4)]
    return pl.pallas_call(body, name=name, out_shape=shapes,
                          compiler_params=pltpu.CompilerParams(vmem_limit_bytes=VMEM_LIMIT))(g, w, m, v)


def update_natural(groups, name):
    n = len(groups)
    steps = 8

    def body(*refs):
        ins, outs = refs[:4 * n], refs[4 * n:]
        for j in range(n):
            g_ref, w_ref, m_ref, v_ref = ins[4 * j:4 * j + 4]
            gv = g_ref[...]
            res = (gv,) + _adamw(w_ref[...], gv, m_ref[...], v_ref[...])
            for k in range(4):
                outs[4 * j + k][...] = res[k]

    specs, shapes = [], []
    for g, w, m, v in groups:
        rows, cols = w.shape
        specs += [pl.BlockSpec((rows // steps, cols), lambda i: (i, 0))] * 4
        shapes += [jax.ShapeDtypeStruct((rows, cols), F32)] * 4
    return pl.pallas_call(body, name=name, grid=(steps,), in_specs=specs, out_specs=specs, out_shape=shapes,
                          compiler_params=_params())(*[a for grp in groups for a in grp])


S5_NAMES = ("w_in", "lam_re", "lam_im", "log_dt", "b_re", "b_im", "c_re", "c_im", "d_skip", "w_glu", "b_glu", "w_out")
CONV_NAMES = ("w_in", "conv_w", "conv_b", "w_out")
POOL_NAMES = ("w_in", "w_grp", "b_grp", "scale", "w_out")
LAYER_KINDS = ("s5", "conv", "pool", "s5")
LAYER_NAMES = {"s5": S5_NAMES, "conv": CONV_NAMES, "pool": POOL_NAMES}
SHARDED = {"s5": ("w_in", "w_glu", "w_out"), "conv": ("w_in", "conv_w", "w_out"), "pool": ("w_in", "w_grp", "b_grp", "w_out")}
GATHER_F32 = ("conv_w", "b_grp")


def weight_names():
    names = ["meta_tokens"]
    for i, kind in enumerate(LAYER_KINDS):
        names.append("norm%d_g" % i)
        names += ["l%d_%s" % (i, n) for n in LAYER_NAMES[kind]]
    names.append("final_g")
    return names


def sharded_names():
    return ["meta_tokens"] + ["l%d_%s" % (i, n) for i, kind in enumerate(LAYER_KINDS) for n in SHARDED[kind]]


def _block_diag_in_grad(blocks):
    _, gc, i, p = blocks.shape
    return jnp.transpose(blocks, (2, 0, 1, 3)).reshape(i, 4 * gc, p)


def _block_diag_out_grad(blocks):
    _, gc, i, p = blocks.shape
    return blocks.reshape(4 * gc, i, p)


def _to_owner_blocks(a, axis):
    shape = a.shape[:axis] + (N_DEV, a.shape[axis] // N_DEV) + a.shape[axis + 1:]
    return jnp.moveaxis(a.reshape(shape), axis, 0)


def _from_owner_blocks(a, axis):
    a = jnp.moveaxis(a, 0, axis)
    return a.reshape(a.shape[:axis] + (a.shape[axis] * a.shape[axis + 1],) + a.shape[axis + 2:])


def _step(x, target, weights, moments_m, moments_v):
    seq, d = x.shape[1], x.shape[2]
    n_meta = weights["meta_tokens"].shape[0]
    tt = TOKEN_TILE
    pad_tiles = -(-n_meta // tt)
    p0 = pad_tiles * tt
    lp = p0 + seq
    first_pos = p0 - n_meta
    gc = d // 4 // S5_GROUP
    cw = d // 4

    big_names = [n for n in sharded_names() if n != "meta_tokens" and n.split("_", 1)[1] not in GATHER_F32]
    small_names = [n for n in sharded_names() if n not in big_names]
    layer_big = [[n for n in big_names if n.startswith("l%d_" % i)] for i in range(len(LAYER_KINDS))]
    layer_big[0] = small_names + layer_big[0]
    gather_started = []
    after = jnp.zeros((8, 128), F32)
    for i, names in enumerate(layer_big):
        gather_started.append(exchange_start([weights[n] if n in small_names else weights[n].astype(BF16) for n in names], True,
                                             after, "gather_start_l%d" % i))
        after = gather_started[-1][4]

    def vec(name):
        return weights[name].reshape(1, -1)

    s5_prep = {}
    for i, kind in enumerate(LAYER_KINDS):
        if kind == "s5":
            p = "l%d_" % i
            lr, li = weights[p + "lam_re"], weights[p + "lam_im"] + after[0, 0]
            ldt = weights[p + "log_dt"].reshape(-1, 1)
            br_t = jnp.transpose(weights[p + "b_re"], (2, 0, 1))
            bi_t = jnp.transpose(weights[p + "b_im"], (2, 0, 1))
            ar, ai, bdre, bdim, cdre, cdim = s5_disc_fwd(lr, li, ldt, br_t, bi_t, weights[p + "c_re"], weights[p + "c_im"],
                                                         p + "disc_fwd")
            s5_prep[i] = dict(
                disc=(lr, li, ldt, br_t, bi_t), ar=ar.reshape(4, -1, 128), ai=ai.reshape(4, -1, 128),
                bdre=bdre, bdim=bdim, cdre=cdre, cdim=cdim,
                d_skip=weights[p + "d_skip"].reshape(4, 1, cw), b_glu=vec(p + "b_glu"))
    h = jnp.concatenate([jnp.zeros((p0, d), F32), x[0] + after[0, 0]], axis=0)

    prepared = [h] + [s5_prep[i][k] for i in s5_prep for k in ("bdre", "bdim", "cdre", "cdim")]
    gathered = dict(zip(layer_big[0], exchange_wait(gather_started[0], True, prepared, "gather_wait_l0")))
    h = lax.dynamic_update_slice(h, _from_owner_blocks(gathered["meta_tokens"], 1), (first_pos, 0))

    full = {}

    def layer_weights(i, kind, after):
        p = "l%d_" % i
        if i > 0:
            gathered.update(zip(layer_big[i], exchange_wait(gather_started[i], True, after, "gather_wait_l%d" % i)))
        w_in = gathered[p + "w_in"]
        if kind == "s5":
            full[i] = dict(s5_prep[i], w_in=w_in, w_glu=gathered[p + "w_glu"].reshape(4, cw, d),
                           w_out=gathered[p + "w_out"].reshape(4, cw, d))
        elif kind == "conv":
            ce = w_in.shape[2]
            nch = 2
            conv_w = _from_owner_blocks(gathered[p + "conv_w"], 1)
            full[i] = dict(
                w_in=w_in, conv_w=jnp.transpose(conv_w.reshape(CONV_K, nch, ce), (1, 0, 2)),
                conv_b=weights[p + "conv_b"].reshape(nch, 1, ce), w_out=gathered[p + "w_out"].reshape(nch, ce, d))
        else:
            gw = w_in.shape[2]
            full[i] = dict(
                w_in=w_in, w_grp=_from_owner_blocks(gathered[p + "w_grp"], 1),
                b_grp=_from_owner_blocks(gathered[p + "b_grp"], 1).reshape(4, 1, gw),
                scale=weights[p + "scale"].reshape(4, 1, gw), w_out=gathered[p + "w_out"].reshape(4, gw, d))
        return full[i]

    saved = {}
    for i, kind in enumerate(LAYER_KINDS):
        p, f, g = "l%d_" % i, layer_weights(i, kind, h), vec("norm%d_g" % i)
        if kind == "s5":
            u, z, xs = s5_fwd1(h, g, f["w_in"], f["bdre"], f["bdim"], p + "fwd_in")
            s = s5_scan_fwd(xs, f["ar"], f["ai"], p + "scan_fwd")
            h_in = h
            h, y, q = s5_fwd3(s, u, z, h, f["cdre"], f["cdim"], f["w_glu"], f["w_out"], f["d_skip"], f["b_glu"], p + "fwd_out")
            saved[i] = (h_in, u, z, s, y, q)
        elif kind == "conv":
            h_new, halos, acts = conv_fwd(h, g, f["w_in"], f["conv_w"], f["conv_b"], f["w_out"], p + "fwd")
            saved[i] = (h, halos, acts)
            h = h_new
        else:
            h_new, halos, acts = pool_fwd(h, g, f["w_in"], f["w_grp"], f["b_grp"], f["scale"], f["w_out"], first_pos, p + "fwd")
            saved[i] = (h, halos, acts)
            h = h_new

    dh, dg_final, loss_tile = loss_head(h, target[0], vec("final_g"), pad_tiles, "loss_head")
    loss = lax.psum(loss_tile[0, 0], ("x", "y", "c"))

    grads = {"final_g": dg_final}
    names = weight_names()
    sh_names = sharded_names()
    replicated = [n for n in names if n not in sh_names]
    vectors = [n for n in replicated if weights[n].ndim == 1]
    matrices = [n for n in replicated if weights[n].ndim > 1]
    rep_names = vectors + matrices

    def owner_blocks(a):
        return a.reshape(N_DEV, -1, a.shape[-1]).astype(BF16)

    def as2d(a):
        return a.reshape(-1, a.shape[-1])

    def pack(tree, which=None):
        flat = [jnp.pad(tree[n].reshape(-1), (0, -tree[n].size % PACK_ALIGN)) for n in (which or rep_names)]
        flat = jnp.concatenate(flat)
        if which is None:
            flat = jnp.pad(flat, (0, -flat.size % (PACK_ROWS * 128)))
        return flat.reshape(-1, 128)

    layer_sharded, scatter_started = {}, {}
    ordered = jnp.zeros((), F32)
    for i in reversed(range(len(LAYER_KINDS))):
        kind = LAYER_KINDS[i]
        p, f, g = "l%d_" % i, full[i], vec("norm%d_g" % i) + ordered
        if kind == "s5":
            h_in, u, z, s, y, q = saved[i]
            dy, dp, dwo, dwg, dbg = s5_bwd3a(dh, y, q, z, f["w_glu"], f["w_out"], f["b_glu"] + ordered, p + "bwd_out")
            d_skip = f["d_skip"]
            if i == 0:
                early_names = [p + "w_glu", p + "w_out"]
                scatter_started["early"] = exchange_start([owner_blocks(dwg), owner_blocks(dwo)], False, dy,
                                                          "scatter_start_l0_early")
                d_skip = d_skip + scatter_started["early"][4][0, 0]
            ds, dus, dcre, dcim, dd = s5_bwd3b(dy, s, u, f["cdre"], f["cdim"], d_skip, p + "bwd_read")
            lam, dar, dai = s5_scan_bwd(ds, s, f["ar"], f["ai"], p + "scan_bwd")
            res = s5_bwd1(lam, dus, u, dp, h_in, dh, g, f["w_in"], f["bdre"], f["bdim"], p + "bwd_in",
                          pad_tiles=pad_tiles if i == 0 else None)
            dp, dh, n, dbre, dbim, dg = res[:6]
            if i == 0:
                grad_x = res[6][None]
            dw_in = grad_w_in(n, dp, f["w_in"].shape[2], p + "grad_w_in")
            grads.update({p + "w_in": dw_in, p + "w_glu": dwg.reshape(N_DEV, -1, d), p + "w_out": dwo.reshape(N_DEV, -1, d),
                          p + "d_skip": dd, p + "b_glu": dbg})

            def replicated_grads(p=p, f=f, dar=dar, dai=dai, dbre=dbre, dbim=dbim, dcre=dcre, dcim=dcim, token=None):
                lr, li, ldt, br_t, bi_t = f["disc"]
                dlr, dli, dldt, dbr_t, dbi_t = s5_disc_bwd(
                    lr, li, ldt, br_t, bi_t, dar.reshape(lr.shape) + token, dai.reshape(lr.shape),
                    _block_diag_in_grad(dbre), _block_diag_in_grad(dbim), p + "disc_bwd")
                grads.update({
                    p + "lam_re": dlr, p + "lam_im": dli, p + "log_dt": dldt,
                    p + "b_re": jnp.transpose(dbr_t, (1, 2, 0)), p + "b_im": jnp.transpose(dbi_t, (1, 2, 0)),
                    p + "c_re": _block_diag_out_grad(dcre), p + "c_im": -_block_diag_out_grad(dcim)})
        elif kind == "conv":
            replicated_grads = None
            h_in, halos, acts = saved[i]
            dh, n, dp, dwo, dcw, dcb, dg = conv_bwd(h_in, dh, halos, acts, g, f["w_in"], f["conv_w"], f["conv_b"], f["w_out"], p + "bwd")
            dw_in = grad_w_in(n, dp, f["w_in"].shape[2], p + "grad_w_in")
            dconv_w = jnp.transpose(dcw[:, :CONV_K, :], (1, 0, 2)).reshape(CONV_K, -1)
            grads.update({p + "w_in": dw_in, p + "conv_w": _to_owner_blocks(dconv_w, 1), p + "conv_b": dcb,
                          p + "w_out": dwo.reshape(N_DEV, -1, d)})
        else:
            replicated_grads = None
            h_in, halos, acts = saved[i]
            dh, n, dp, dwo, dwgrp, dbgrp, dsc, dg = pool_bwd(h_in, dh, halos, acts, g, f["w_in"], f["w_grp"], f["b_grp"], f["scale"],
                                                             f["w_out"], first_pos, p + "bwd")
            dw_in = grad_w_in(n, dp, f["w_in"].shape[2], p + "grad_w_in")
            grads.update({p + "w_in": dw_in, p + "w_grp": _to_owner_blocks(dwgrp, 1),
                          p + "b_grp": _to_owner_blocks(dbgrp.reshape(4, -1), 1), p + "scale": dsc,
                          p + "w_out": dwo.reshape(N_DEV, -1, d)})
        grads["norm%d_g" % i] = dg
        layer_sharded[i] = ["l%d_%s" % (i, n) for n in SHARDED[kind]]
        if i > 0:
            scatter_started[i] = exchange_start([owner_blocks(grads[n]) for n in layer_sharded[i]], False, dh,
                                                "scatter_start_l%d" % i)
            ordered = scatter_started[i][4][0, 0]
        if replicated_grads is not None:
            replicated_grads(token=ordered)
    grads["meta_tokens"] = _to_owner_blocks(dh[first_pos:p0], 1)
    last = len(LAYER_KINDS)
    layer_sharded[last] = ["meta_tokens", "replicated"]
    scatter_started[last] = exchange_start([owner_blocks(grads["meta_tokens"]), pack(grads).reshape(N_DEV, -1, 128)], False,
                                           dh, "scatter_start_replicated")
    layer_sharded["early"] = early_names
    layer_sharded[0] = [n for n in layer_sharded[0] if n not in early_names]

    out = {}
    received = {}
    after = [scatter_started[last][4]]
    for i in list(reversed(range(1, last))) + [last, "early", 0]:
        received.update(zip(layer_sharded[i], exchange_wait(scatter_started[i], False, after, "scatter_wait_%s" % i)))
        updated = []
        for n in layer_sharded[i]:
            if n != "replicated":
                res = sum_adamw(received[n], as2d(weights[n]), as2d(moments_m[n]), as2d(moments_v[n]), "update_" + n)
                out[n] = [r.reshape(weights[n].shape) for r in res]
                updated.append(out[n][0])
        after = updated or after
        if i == last:
            g_sum = sum_parts(received["replicated"], "sum_replicated")
            small_gather = exchange_start([g_sum], True, g_sum, "gather_small_grads_start")
            scatter_started[0] = exchange_start([owner_blocks(grads[n]) for n in layer_sharded[0]], False, small_gather[4],
                                                "scatter_start_l0")
            g_full = exchange_wait(small_gather, True, scatter_started[0][4], "gather_small_grads_wait")[0].reshape(-1, 128)
            offsets, offset = {}, 0
            for n in rep_names:
                offsets[n] = offset
                offset += weights[n].size + (-weights[n].size % PACK_ALIGN)
            pieces = [(offsets[n] // 128, max(weights[n].size // 128, 1), min(weights[n].size, 128)) for n in vectors]
            res = update_packed(g_full, pack(weights, vectors), pack(moments_m, vectors), pack(moments_v, vectors), pieces,
                                "update_replicated_vectors")
            for j, n in enumerate(vectors):
                out[n] = [r.reshape(weights[n].shape) for r in res[4 * j:4 * j + 4]]
            flat = g_full.reshape(-1)
            groups = [(flat[offsets[n]:offsets[n] + weights[n].size].reshape(as2d(weights[n]).shape), as2d(weights[n]),
                       as2d(moments_m[n]), as2d(moments_v[n])) for n in matrices]
            res = update_natural(groups, "update_replicated_matrices")
            for j, n in enumerate(matrices):
                out[n] = [r.reshape(weights[n].shape) for r in res[4 * j:4 * j + 4]]
            after = [out[n][k] for n in rep_names for k in range(4)]

    return (loss, grad_x) + tuple(out[n][k] for k in range(4) for n in names)


def kernel(x, *rest):
    names = weight_names()
    nw = len(names)
    weights = dict(zip(names, rest[:nw]))
    target = rest[nw]
    moments_m = dict(zip(names, rest[nw + 1:2 * nw + 1]))
    moments_v = dict(zip(names, rest[2 * nw + 1:3 * nw + 1]))
    return _step(x, target, weights, moments_m, moments_v)
```
